```python
import jax, jax.numpy as jnp
from jax import lax
import numpy as np

D_MODEL = 1024
BATCH = 8
SEQ = 2048
DEPTH = 2

PLE_DIM = 256
BRANCH_WIDTH = D_MODEL // 2
N_BRANCH = 3
FOX_HEAD_DIM = 64
FOX_HEADS = BRANCH_WIDTH // FOX_HEAD_DIM
FOX_WIDTH = FOX_HEADS * FOX_HEAD_DIM
FOX_BLOCK = 128
SC_WIDTH = BRANCH_WIDTH
SC_KERNEL = 3
DN_HEAD_DIM = 128
DN_HEADS = BRANCH_WIDTH // DN_HEAD_DIM
DN_WIDTH = DN_HEADS * DN_HEAD_DIM
DN_CONV = 4
DN_CHUNK = 64
D_FF = 128 * ((8 * D_MODEL // 3 + 127) // 128)
FFN_CONV = 3
EPS = 1e-6

IN_SIZES = (3 * FOX_WIDTH, FOX_HEADS, 3 * SC_WIDTH, 3 * DN_WIDTH, DN_HEADS, DN_HEADS, DN_WIDTH, N_BRANCH * D_MODEL)
IN_WIDTH = sum(IN_SIZES)

kernel_name = 'hybrid_fox_shortconv_gdn_parallel_block'


def split_cols(t, sizes):
    offs = []
    acc = 0
    for s in sizes[:-1]:
        acc += s
        offs.append(acc)
    return jnp.split(t, offs, axis=-1)


def rmsnorm(x, gain):
    xf = x.astype(jnp.float32)
    y = xf * lax.rsqrt(jnp.mean(xf * xf, axis=-1, keepdims=True) + EPS)
    return (y * gain.astype(jnp.float32)).astype(x.dtype)


def l2norm(x):
    xf = x.astype(jnp.float32)
    return xf * lax.rsqrt(jnp.sum(xf * xf, axis=-1, keepdims=True) + EPS)


def causal_dwconv(x, w):
    k_width, chans = w.shape
    return lax.conv_general_dilated(x, w[:, None, :].astype(x.dtype), window_strides=(1,),
                                    padding=[(k_width - 1, 0)],
                                    dimension_numbers=('NWC', 'WIO', 'NWC'),
                                    feature_group_count=chans)


def forgetting_attention(q, k, v, f_logit, b_f, q_gain, k_gain):
    seq = q.shape[1]
    dh = q.shape[-1]
    q = rmsnorm(q, q_gain).transpose(0, 2, 1, 3)
    k = rmsnorm(k, k_gain).transpose(0, 2, 1, 3)
    v = v.transpose(0, 2, 1, 3)
    log_f = jax.nn.log_sigmoid(f_logit.astype(jnp.float32) + b_f.astype(jnp.float32))
    cum_f = jnp.cumsum(log_f, axis=1).transpose(0, 2, 1)
    scale = dh ** -0.5
    outs = []
    for start in range(0, seq, FOX_BLOCK):
        end = start + FOX_BLOCK
        s = jnp.einsum('bhqd,bhkd->bhqk', q[:, :, start:end], k[:, :, :end],
                       preferred_element_type=jnp.float32) * scale
        s = s + cum_f[:, :, start:end, None] - cum_f[:, :, None, :end]
        causal = jnp.arange(start, end)[:, None] >= jnp.arange(end)[None, :]
        s = jnp.where(causal, s, -jnp.inf)
        pr = jax.nn.softmax(s, axis=-1).astype(v.dtype)
        outs.append(jnp.einsum('bhqk,bhkd->bqhd', pr, v[:, :, :end]))
    return jnp.concatenate(outs, axis=1)


def gated_delta_rule(q, k, v, g, beta):
    bsz, seq, heads, dk = q.shape
    dv = v.shape[-1]
    c = DN_CHUNK
    n_chunks = seq // c

    def to_chunks(t):
        return t.astype(jnp.float32).reshape(bsz, n_chunks, c, heads, -1).transpose(1, 0, 3, 2, 4)

    qc = to_chunks(q) * dk ** -0.5
    kc = to_chunks(k)
    vc = to_chunks(v)
    gc = to_chunks(g[..., None])[..., 0]
    bc = to_chunks(beta[..., None])[..., 0]
    gcum = jnp.cumsum(gc, axis=-1)
    incl = jnp.tril(jnp.ones((c, c), dtype=bool))
    strict = jnp.tril(jnp.ones((c, c), dtype=bool), k=-1)
    decay = jnp.exp(jnp.where(incl, gcum[..., :, None] - gcum[..., None, :], -jnp.inf))
    kb = kc * bc[..., None]
    a_mat = jnp.where(strict, jnp.einsum('nbhid,nbhjd->nbhij', kb, kc) * decay, 0.0) \
        + jnp.eye(c, dtype=jnp.float32)
    rhs = jnp.concatenate([vc * bc[..., None], kb * jnp.exp(gcum)[..., None]], axis=-1)
    sol = lax.linalg.triangular_solve(a_mat, rhs, left_side=True, lower=True, unit_diagonal=True)
    u_val, k_cum = sol[..., :dv], sol[..., dv:]
    qk = jnp.where(incl, jnp.einsum('nbhid,nbhjd->nbhij', qc, kc) * decay, 0.0)
    q_dec = qc * jnp.exp(gcum)[..., None]
    k_dec = kc * jnp.exp(gcum[..., -1:] - gcum)[..., None]
    g_tot = jnp.exp(gcum[..., -1])

    def step(state, xs):
        u_i, kcum_i, qk_i, qdec_i, kdec_i, gtot_i = xs
        v_new = u_i - jnp.einsum('bhck,bhkv->bhcv', kcum_i, state)
        out = jnp.einsum('bhck,bhkv->bhcv', qdec_i, state) + jnp.einsum('bhij,bhjv->bhiv', qk_i, v_new)
        state = state * gtot_i[..., None, None] + jnp.einsum('bhck,bhcv->bhkv', kdec_i, v_new)
        return state, out

    state0 = jnp.zeros((bsz, heads, dk, dv), jnp.float32)
    _, out = lax.scan(step, state0, (u_val, k_cum, qk, q_dec, k_dec, g_tot))
    return out.transpose(1, 0, 3, 2, 4).reshape(bsz, seq, heads, dv)


def hybrid_layer(x, p_i, g_mix, w_in, b_fox_f, fox_q_gain, fox_k_gain, sc_conv_w, dn_conv_w,
                 dn_a_log, dn_dt_bias, dn_norm_gain, w_branch, w_o, g_ffn, w_up, ffn_conv_w,
                 w_down, g_ple, w_ple_gate, w_ple):
    bsz, seq, _ = x.shape
    h = rmsnorm(x, g_mix)
    proj = h @ w_in
    fox_qkv, fox_f, sc_bcv, dn_qkv, dn_b, dn_a, dn_z, br_gate = split_cols(proj, IN_SIZES)

    fq, fk, fv = [t.reshape(bsz, seq, FOX_HEADS, FOX_HEAD_DIM) for t in jnp.split(fox_qkv, 3, axis=-1)]
    y_fox = forgetting_attention(fq, fk, fv, fox_f, b_fox_f, fox_q_gain, fox_k_gain)
    y_fox = y_fox.reshape(bsz, seq, FOX_WIDTH)

    sb, sc, sv = jnp.split(sc_bcv, 3, axis=-1)
    y_sc = sb * causal_dwconv(sc * sv, sc_conv_w)

    dn_qkv = jax.nn.silu(causal_dwconv(dn_qkv, dn_conv_w))
    dq, dk_, dv_ = [t.reshape(bsz, seq, DN_HEADS, DN_HEAD_DIM) for t in jnp.split(dn_qkv, 3, axis=-1)]
    beta = jax.nn.sigmoid(dn_b.astype(jnp.float32))
    g = -jnp.exp(dn_a_log.astype(jnp.float32)) * jax.nn.softplus(dn_a.astype(jnp.float32) + dn_dt_bias.astype(jnp.float32))
    o_dn = gated_delta_rule(l2norm(dq), l2norm(dk_), dv_, g, beta).astype(x.dtype)
    z = dn_z.reshape(bsz, seq, DN_HEADS, DN_HEAD_DIM)
    y_dn = (rmsnorm(o_dn, dn_norm_gain) * jax.nn.silu(z)).reshape(bsz, seq, DN_WIDTH)

    ys = jnp.stack([y_fox, y_sc, y_dn], axis=2)
    gates = jax.nn.sigmoid(br_gate).reshape(bsz, seq, N_BRANCH, D_MODEL)
    merged = jnp.sum(jnp.einsum('bsnc,ncd->bsnd', ys, w_branch) * gates, axis=2)
    x = x + merged @ w_o

    u = causal_dwconv(rmsnorm(x, g_ffn) @ w_up, ffn_conv_w)
    u_gate, u_val = jnp.split(u, 2, axis=-1)
    x = x + (jax.nn.silu(u_gate) * u_val) @ w_down

    x = x + jax.nn.sigmoid(rmsnorm(x, g_ple) @ w_ple_gate) * (p_i.astype(x.dtype) @ w_ple)
    return x


def _fwd_setup_inputs(seed: int = 0) -> dict:
    key = jax.random.key(seed)
    ks = jax.random.split(key, 24)
    f32 = jnp.float32

    def nrm(k, shape, scale):
        return jax.random.normal(k, shape, f32) * scale

    def gain(k, shape):
        return 1.0 + 0.02 * jax.random.normal(k, shape, f32)

    x = nrm(ks[0], (BATCH, SEQ, D_MODEL), 1.0)
    p = nrm(ks[1], (DEPTH, BATCH, SEQ, PLE_DIM), 1.0)
    g_mix = gain(ks[2], (DEPTH, D_MODEL))
    w_in = nrm(ks[3], (DEPTH, D_MODEL, IN_WIDTH), D_MODEL ** -0.5)
    b_fox_f = jnp.linspace(1.0, 5.0, FOX_HEADS, dtype=f32)[None, :] + nrm(ks[4], (DEPTH, FOX_HEADS), 0.1)
    fox_q_gain = gain(ks[5], (DEPTH, FOX_HEAD_DIM))
    fox_k_gain = gain(ks[6], (DEPTH, FOX_HEAD_DIM))
    sc_conv_w = nrm(ks[7], (DEPTH, SC_KERNEL, SC_WIDTH), SC_KERNEL ** -0.5)
    dn_conv_w = nrm(ks[8], (DEPTH, DN_CONV, 3 * DN_WIDTH), DN_CONV ** -0.5)
    dn_a_log = jnp.log(jax.random.uniform(ks[9], (DEPTH, DN_HEADS), f32, 1.0, 16.0))
    dt = jnp.exp(jax.random.uniform(ks[10], (DEPTH, DN_HEADS), f32, float(np.log(1e-3)), float(np.log(1e-1))))
    dn_dt_bias = dt + jnp.log(-jnp.expm1(-dt))
    dn_norm_gain = gain(ks[11], (DEPTH, DN_HEAD_DIM))
    w_branch = nrm(ks[12], (DEPTH, N_BRANCH, BRANCH_WIDTH, D_MODEL), BRANCH_WIDTH ** -0.5)
    w_o = nrm(ks[13], (DEPTH, D_MODEL, D_MODEL), D_MODEL ** -0.5)
    g_ffn = gain(ks[14], (DEPTH, D_MODEL))
    w_up = nrm(ks[15], (DEPTH, D_MODEL, 2 * D_FF), D_MODEL ** -0.5)
    ffn_conv_w = nrm(ks[16], (DEPTH, FFN_CONV, 2 * D_FF), FFN_CONV ** -0.5)
    w_down = nrm(ks[17], (DEPTH, D_FF, D_MODEL), D_FF ** -0.5)
    g_ple = gain(ks[18], (DEPTH, D_MODEL))
    w_ple_gate = nrm(ks[19], (DEPTH, D_MODEL, D_MODEL), D_MODEL ** -0.5)
    w_ple = nrm(ks[20], (DEPTH, PLE_DIM, D_MODEL), PLE_DIM ** -0.5)
    return {'x': x, 'p': p, 'g_mix': g_mix, 'w_in': w_in, 'b_fox_f': b_fox_f,
            'fox_q_gain': fox_q_gain, 'fox_k_gain': fox_k_gain, 'sc_conv_w': sc_conv_w,
            'dn_conv_w': dn_conv_w, 'dn_a_log': dn_a_log, 'dn_dt_bias': dn_dt_bias,
            'dn_norm_gain': dn_norm_gain, 'w_branch': w_branch, 'w_o': w_o, 'g_ffn': g_ffn,
            'w_up': w_up, 'ffn_conv_w': ffn_conv_w, 'w_down': w_down, 'g_ple': g_ple,
            'w_ple_gate': w_ple_gate, 'w_ple': w_ple}


def _fwd_reference(x, p, g_mix, w_in, b_fox_f, fox_q_gain, fox_k_gain, sc_conv_w, dn_conv_w,
              dn_a_log, dn_dt_bias, dn_norm_gain, w_branch, w_o, g_ffn, w_up, ffn_conv_w,
              w_down, g_ple, w_ple_gate, w_ple):
    for i in range(DEPTH):
        x = hybrid_layer(x, p[i], g_mix[i], w_in[i], b_fox_f[i], fox_q_gain[i], fox_k_gain[i],
                         sc_conv_w[i], dn_conv_w[i], dn_a_log[i], dn_dt_bias[i], dn_norm_gain[i],
                         w_branch[i], w_o[i], g_ffn[i], w_up[i], ffn_conv_w[i], w_down[i],
                         g_ple[i], w_ple_gate[i], w_ple[i])
    return x


import jax as _jax
import jax.numpy as _jnp

TWIN_FORMAT = 'train_step'
FWD_PARAMS = ['x', 'p', 'g_mix', 'w_in', 'b_fox_f', 'fox_q_gain', 'fox_k_gain', 'sc_conv_w', 'dn_conv_w', 'dn_a_log', 'dn_dt_bias', 'dn_norm_gain', 'w_branch', 'w_o', 'g_ffn', 'w_up', 'ffn_conv_w', 'w_down', 'g_ple', 'w_ple_gate', 'w_ple']
TWIN_WEIGHTS = ['g_mix', 'w_in', 'b_fox_f', 'fox_q_gain', 'fox_k_gain', 'sc_conv_w', 'dn_conv_w', 'dn_a_log', 'dn_dt_bias', 'dn_norm_gain', 'w_branch', 'w_o', 'g_ffn', 'w_up', 'ffn_conv_w', 'w_down', 'g_ple', 'w_ple_gate', 'w_ple']
TWIN_DIFF_INPUT = 'x'
TWIN_INPUTS = ['x', 'p', 'g_mix', 'w_in', 'b_fox_f', 'fox_q_gain', 'fox_k_gain', 'sc_conv_w', 'dn_conv_w', 'dn_a_log', 'dn_dt_bias', 'dn_norm_gain', 'w_branch', 'w_o', 'g_ffn', 'w_up', 'ffn_conv_w', 'w_down', 'g_ple', 'w_ple_gate', 'w_ple', 'loss_target', 'm_g_mix', 'm_w_in', 'm_b_fox_f', 'm_fox_q_gain', 'm_fox_k_gain', 'm_sc_conv_w', 'm_dn_conv_w', 'm_dn_a_log', 'm_dn_dt_bias', 'm_dn_norm_gain', 'm_w_branch', 'm_w_o', 'm_g_ffn', 'm_w_up', 'm_ffn_conv_w', 'm_w_down', 'm_g_ple', 'm_w_ple_gate', 'm_w_ple', 'v_g_mix', 'v_w_in', 'v_b_fox_f', 'v_fox_q_gain', 'v_fox_k_gain', 'v_sc_conv_w', 'v_dn_conv_w', 'v_dn_a_log', 'v_dn_dt_bias', 'v_dn_norm_gain', 'v_w_branch', 'v_w_o', 'v_g_ffn', 'v_w_up', 'v_ffn_conv_w', 'v_w_down', 'v_g_ple', 'v_w_ple_gate', 'v_w_ple']
TWIN_OUTPUTS = ['loss', 'grad_x', 'grad_g_mix', 'grad_w_in', 'grad_b_fox_f', 'grad_fox_q_gain', 'grad_fox_k_gain', 'grad_sc_conv_w', 'grad_dn_conv_w', 'grad_dn_a_log', 'grad_dn_dt_bias', 'grad_dn_norm_gain', 'grad_w_branch', 'grad_w_o', 'grad_g_ffn', 'grad_w_up', 'grad_ffn_conv_w', 'grad_w_down', 'grad_g_ple', 'grad_w_ple_gate', 'grad_w_ple', 'delta_g_mix', 'delta_w_in', 'delta_b_fox_f', 'delta_fox_q_gain', 'delta_fox_k_gain', 'delta_sc_conv_w', 'delta_dn_conv_w', 'delta_dn_a_log', 'delta_dn_dt_bias', 'delta_dn_norm_gain', 'delta_w_branch', 'delta_w_o', 'delta_g_ffn', 'delta_w_up', 'delta_ffn_conv_w', 'delta_w_down', 'delta_g_ple', 'delta_w_ple_gate', 'delta_w_ple', 'new_m_g_mix', 'new_m_w_in', 'new_m_b_fox_f', 'new_m_fox_q_gain', 'new_m_fox_k_gain', 'new_m_sc_conv_w', 'new_m_dn_conv_w', 'new_m_dn_a_log', 'new_m_dn_dt_bias', 'new_m_dn_norm_gain', 'new_m_w_branch', 'new_m_w_o', 'new_m_g_ffn', 'new_m_w_up', 'new_m_ffn_conv_w', 'new_m_w_down', 'new_m_g_ple', 'new_m_w_ple_gate', 'new_m_w_ple', 'new_v_g_mix', 'new_v_w_in', 'new_v_b_fox_f', 'new_v_fox_q_gain', 'new_v_fox_k_gain', 'new_v_sc_conv_w', 'new_v_dn_conv_w', 'new_v_dn_a_log', 'new_v_dn_dt_bias', 'new_v_dn_norm_gain', 'new_v_w_branch', 'new_v_w_o', 'new_v_g_ffn', 'new_v_w_up', 'new_v_ffn_conv_w', 'new_v_w_down', 'new_v_g_ple', 'new_v_w_ple_gate', 'new_v_w_ple']
TWIN_LEAF_KINDS = {'loss': 'loss', 'grad_x': 'grad_x', 'grad_g_mix': 'grad_w', 'grad_w_in': 'grad_w', 'grad_b_fox_f': 'grad_w', 'grad_fox_q_gain': 'grad_w', 'grad_fox_k_gain': 'grad_w', 'grad_sc_conv_w': 'grad_w', 'grad_dn_conv_w': 'grad_w', 'grad_dn_a_log': 'grad_w', 'grad_dn_dt_bias': 'grad_w', 'grad_dn_norm_gain': 'grad_w', 'grad_w_branch': 'grad_w', 'grad_w_o': 'grad_w', 'grad_g_ffn': 'grad_w', 'grad_w_up': 'grad_w', 'grad_ffn_conv_w': 'grad_w', 'grad_w_down': 'grad_w', 'grad_g_ple': 'grad_w', 'grad_w_ple_gate': 'grad_w', 'grad_w_ple': 'grad_w', 'delta_g_mix': 'delta_w', 'delta_w_in': 'delta_w', 'delta_b_fox_f': 'delta_w', 'delta_fox_q_gain': 'delta_w', 'delta_fox_k_gain': 'delta_w', 'delta_sc_conv_w': 'delta_w', 'delta_dn_conv_w': 'delta_w', 'delta_dn_a_log': 'delta_w', 'delta_dn_dt_bias': 'delta_w', 'delta_dn_norm_gain': 'delta_w', 'delta_w_branch': 'delta_w', 'delta_w_o': 'delta_w', 'delta_g_ffn': 'delta_w', 'delta_w_up': 'delta_w', 'delta_ffn_conv_w': 'delta_w', 'delta_w_down': 'delta_w', 'delta_g_ple': 'delta_w', 'delta_w_ple_gate': 'delta_w', 'delta_w_ple': 'delta_w', 'new_m_g_mix': 'new_m', 'new_m_w_in': 'new_m', 'new_m_b_fox_f': 'new_m', 'new_m_fox_q_gain': 'new_m', 'new_m_fox_k_gain': 'new_m', 'new_m_sc_conv_w': 'new_m', 'new_m_dn_conv_w': 'new_m', 'new_m_dn_a_log': 'new_m', 'new_m_dn_dt_bias': 'new_m', 'new_m_dn_norm_gain': 'new_m', 'new_m_w_branch': 'new_m', 'new_m_w_o': 'new_m', 'new_m_g_ffn': 'new_m', 'new_m_w_up': 'new_m', 'new_m_ffn_conv_w': 'new_m', 'new_m_w_down': 'new_m', 'new_m_g_ple': 'new_m', 'new_m_w_ple_gate': 'new_m', 'new_m_w_ple': 'new_m', 'new_v_g_mix': 'new_v', 'new_v_w_in': 'new_v', 'new_v_b_fox_f': 'new_v', 'new_v_fox_q_gain': 'new_v', 'new_v_fox_k_gain': 'new_v', 'new_v_sc_conv_w': 'new_v', 'new_v_dn_conv_w': 'new_v', 'new_v_dn_a_log': 'new_v', 'new_v_dn_dt_bias': 'new_v', 'new_v_dn_norm_gain': 'new_v', 'new_v_w_branch': 'new_v', 'new_v_w_o': 'new_v', 'new_v_g_ffn': 'new_v', 'new_v_w_up': 'new_v', 'new_v_ffn_conv_w': 'new_v', 'new_v_w_down': 'new_v', 'new_v_g_ple': 'new_v', 'new_v_w_ple_gate': 'new_v', 'new_v_w_ple': 'new_v'}


def _forward(args):
    return _fwd_reference(*[args[k] for k in FWD_PARAMS])


def _output_shape():
    out = _jax.eval_shape(lambda: _forward(_fwd_setup_inputs(0)))
    return out.shape, out.dtype

N_MICROBATCH = 1
ADAM_LR = 0.001
ADAM_B1 = 0.9
ADAM_B2 = 0.999
ADAM_EPS = 1e-08
ADAM_WD = 0.01
ADAM_STEP = 10
PER_EXAMPLE_BATCH_AXIS = {'x': 0, 'p': 1, 'loss_target': 0}
SHARED_INPUTS = []
_WEIGHT_DTYPES = {'g_mix': _jnp.float32, 'w_in': _jnp.float32, 'b_fox_f': _jnp.float32, 'fox_q_gain': _jnp.float32, 'fox_k_gain': _jnp.float32, 'sc_conv_w': _jnp.float32, 'dn_conv_w': _jnp.float32, 'dn_a_log': _jnp.float32, 'dn_dt_bias': _jnp.float32, 'dn_norm_gain': _jnp.float32, 'w_branch': _jnp.float32, 'w_o': _jnp.float32, 'g_ffn': _jnp.float32, 'w_up': _jnp.float32, 'ffn_conv_w': _jnp.float32, 'w_down': _jnp.float32, 'g_ple': _jnp.float32, 'w_ple_gate': _jnp.float32, 'w_ple': _jnp.float32}
MOMENT_SCALE = {'g_mix': 1.734248e+01, 'w_in': 2.693949e-01, 'b_fox_f': 2.435434e+01, 'fox_q_gain': 3.077211e+00, 'fox_k_gain': 3.070592e+00, 'sc_conv_w': 5.308162e+00, 'dn_conv_w': 2.796232e-01, 'dn_a_log': 8.981867e+00, 'dn_dt_bias': 8.575606e+00, 'dn_norm_gain': 1.399306e+01, 'w_branch': 3.025828e-01, 'w_o': 5.008729e-01, 'g_ffn': 1.276586e+01, 'w_up': 2.182628e-01, 'ffn_conv_w': 1.761125e+00, 'w_down': 2.901097e-01, 'g_ple': 4.662265e-01, 'w_ple_gate': 1.050458e-01, 'w_ple': 3.037274e-01}


def _to_microbatches(a, axis):
    t = _jnp.moveaxis(a, axis, 0)
    t = t.reshape((N_MICROBATCH, t.shape[0] // N_MICROBATCH) + t.shape[1:])
    return _jnp.moveaxis(t, 1, axis + 1)


def setup_inputs(seed: int = 0) -> dict:
    inp = _fwd_setup_inputs(seed)
    key = _jax.random.fold_in(_jax.random.key(seed), 7919)
    shape, _ = _output_shape()
    out = dict(inp)
    out["loss_target"] = _jax.random.normal(_jax.random.fold_in(key, 0), shape, _jnp.float32)
    for i, name in enumerate(TWIN_WEIGHTS):
        w = inp[name].astype(_jnp.float32)
        if MOMENT_SCALE is None:
            s = _jnp.sqrt(_jnp.mean(_jnp.square(w)) + 1e-30)
        else:
            s = MOMENT_SCALE[name]
        km, kv = _jax.random.split(_jax.random.fold_in(key, i + 1))
        out[name] = w
        out["m_" + name] = s * _jax.random.normal(km, w.shape, _jnp.float32)
        out["v_" + name] = (s * s) * _jax.random.uniform(kv, w.shape, _jnp.float32, 0.5, 1.5)
    if N_MICROBATCH > 1:
        for name, axis in PER_EXAMPLE_BATCH_AXIS.items():
            out[name] = _to_microbatches(out[name], axis)
    return {'x': out['x'], 'p': out['p'], 'g_mix': out['g_mix'], 'w_in': out['w_in'], 'b_fox_f': out['b_fox_f'], 'fox_q_gain': out['fox_q_gain'], 'fox_k_gain': out['fox_k_gain'], 'sc_conv_w': out['sc_conv_w'], 'dn_conv_w': out['dn_conv_w'], 'dn_a_log': out['dn_a_log'], 'dn_dt_bias': out['dn_dt_bias'], 'dn_norm_gain': out['dn_norm_gain'], 'w_branch': out['w_branch'], 'w_o': out['w_o'], 'g_ffn': out['g_ffn'], 'w_up': out['w_up'], 'ffn_conv_w': out['ffn_conv_w'], 'w_down': out['w_down'], 'g_ple': out['g_ple'], 'w_ple_gate': out['w_ple_gate'], 'w_ple': out['w_ple'], 'loss_target': out['loss_target'], 'm_g_mix': out['m_g_mix'], 'm_w_in': out['m_w_in'], 'm_b_fox_f': out['m_b_fox_f'], 'm_fox_q_gain': out['m_fox_q_gain'], 'm_fox_k_gain': out['m_fox_k_gain'], 'm_sc_conv_w': out['m_sc_conv_w'], 'm_dn_conv_w': out['m_dn_conv_w'], 'm_dn_a_log': out['m_dn_a_log'], 'm_dn_dt_bias': out['m_dn_dt_bias'], 'm_dn_norm_gain': out['m_dn_norm_gain'], 'm_w_branch': out['m_w_branch'], 'm_w_o': out['m_w_o'], 'm_g_ffn': out['m_g_ffn'], 'm_w_up': out['m_w_up'], 'm_ffn_conv_w': out['m_ffn_conv_w'], 'm_w_down': out['m_w_down'], 'm_g_ple': out['m_g_ple'], 'm_w_ple_gate': out['m_w_ple_gate'], 'm_w_ple': out['m_w_ple'], 'v_g_mix': out['v_g_mix'], 'v_w_in': out['v_w_in'], 'v_b_fox_f': out['v_b_fox_f'], 'v_fox_q_gain': out['v_fox_q_gain'], 'v_fox_k_gain': out['v_fox_k_gain'], 'v_sc_conv_w': out['v_sc_conv_w'], 'v_dn_conv_w': out['v_dn_conv_w'], 'v_dn_a_log': out['v_dn_a_log'], 'v_dn_dt_bias': out['v_dn_dt_bias'], 'v_dn_norm_gain': out['v_dn_norm_gain'], 'v_w_branch': out['v_w_branch'], 'v_w_o': out['v_w_o'], 'v_g_ffn': out['v_g_ffn'], 'v_w_up': out['v_w_up'], 'v_ffn_conv_w': out['v_ffn_conv_w'], 'v_w_down': out['v_w_down'], 'v_g_ple': out['v_g_ple'], 'v_w_ple_gate': out['v_w_ple_gate'], 'v_w_ple': out['v_w_ple']}


def _loss(weights, diff, rest, loss_target):
    with _jax.named_scope("forward"):
        args = {**rest, TWIN_DIFF_INPUT: diff, **{k: w.astype(_WEIGHT_DTYPES[k]) for k, w in weights.items()}}
        y = _forward(args)
    with _jax.named_scope("loss_head"):
        err = _jnp.square(y.astype(_jnp.float32) - loss_target)
        return 0.5 * _jnp.sum(_jnp.mean(err, axis=-1)) if err.ndim else 0.5 * err


def _adamw(w, g, m, v):
    m = ADAM_B1 * m + (1.0 - ADAM_B1) * g
    v = ADAM_B2 * v + (1.0 - ADAM_B2) * _jnp.square(g)
    m_hat = m / (1.0 - ADAM_B1 ** ADAM_STEP)
    v_hat = v / (1.0 - ADAM_B2 ** ADAM_STEP)
    delta = -ADAM_LR * (m_hat / (_jnp.sqrt(v_hat) + ADAM_EPS) + ADAM_WD * w)
    return delta, m, v


def reference(x, p, g_mix, w_in, b_fox_f, fox_q_gain, fox_k_gain, sc_conv_w, dn_conv_w, dn_a_log, dn_dt_bias, dn_norm_gain, w_branch, w_o, g_ffn, w_up, ffn_conv_w, w_down, g_ple, w_ple_gate, w_ple, loss_target, m_g_mix, m_w_in, m_b_fox_f, m_fox_q_gain, m_fox_k_gain, m_sc_conv_w, m_dn_conv_w, m_dn_a_log, m_dn_dt_bias, m_dn_norm_gain, m_w_branch, m_w_o, m_g_ffn, m_w_up, m_ffn_conv_w, m_w_down, m_g_ple, m_w_ple_gate, m_w_ple, v_g_mix, v_w_in, v_b_fox_f, v_fox_q_gain, v_fox_k_gain, v_sc_conv_w, v_dn_conv_w, v_dn_a_log, v_dn_dt_bias, v_dn_norm_gain, v_w_branch, v_w_o, v_g_ffn, v_w_up, v_ffn_conv_w, v_w_down, v_g_ple, v_w_ple_gate, v_w_ple):
    given = dict(x=x, p=p, g_mix=g_mix, w_in=w_in, b_fox_f=b_fox_f, fox_q_gain=fox_q_gain, fox_k_gain=fox_k_gain, sc_conv_w=sc_conv_w, dn_conv_w=dn_conv_w, dn_a_log=dn_a_log, dn_dt_bias=dn_dt_bias, dn_norm_gain=dn_norm_gain, w_branch=w_branch, w_o=w_o, g_ffn=g_ffn, w_up=w_up, ffn_conv_w=ffn_conv_w, w_down=w_down, g_ple=g_ple, w_ple_gate=w_ple_gate, w_ple=w_ple, loss_target=loss_target, m_g_mix=m_g_mix, m_w_in=m_w_in, m_b_fox_f=m_b_fox_f, m_fox_q_gain=m_fox_q_gain, m_fox_k_gain=m_fox_k_gain, m_sc_conv_w=m_sc_conv_w, m_dn_conv_w=m_dn_conv_w, m_dn_a_log=m_dn_a_log, m_dn_dt_bias=m_dn_dt_bias, m_dn_norm_gain=m_dn_norm_gain, m_w_branch=m_w_branch, m_w_o=m_w_o, m_g_ffn=m_g_ffn, m_w_up=m_w_up, m_ffn_conv_w=m_ffn_conv_w, m_w_down=m_w_down, m_g_ple=m_g_ple, m_w_ple_gate=m_w_ple_gate, m_w_ple=m_w_ple, v_g_mix=v_g_mix, v_w_in=v_w_in, v_b_fox_f=v_b_fox_f, v_fox_q_gain=v_fox_q_gain, v_fox_k_gain=v_fox_k_gain, v_sc_conv_w=v_sc_conv_w, v_dn_conv_w=v_dn_conv_w, v_dn_a_log=v_dn_a_log, v_dn_dt_bias=v_dn_dt_bias, v_dn_norm_gain=v_dn_norm_gain, v_w_branch=v_w_branch, v_w_o=v_w_o, v_g_ffn=v_g_ffn, v_w_up=v_w_up, v_ffn_conv_w=v_ffn_conv_w, v_w_down=v_w_down, v_g_ple=v_g_ple, v_w_ple_gate=v_w_ple_gate, v_w_ple=v_w_ple)
    weights = {n: given[n] for n in TWIN_WEIGHTS}
    shared = {n: given[n] for n in SHARED_INPUTS}
    per_example = {n: given[n] for n in ['x', 'p']}
    grad_fn = _jax.value_and_grad(_loss, argnums=(0, 1))

    def one_microbatch(ex, loss_target):
        ex = dict(ex)
        diff = ex.pop(TWIN_DIFF_INPUT)
        return grad_fn(weights, diff, {**shared, **ex}, loss_target)

    if N_MICROBATCH == 1:
        loss, (grad_w, grad_x) = one_microbatch(per_example, given["loss_target"])
    else:
        def body(carry, xs):
            loss_sum, grad_sum = carry
            l_k, (gw_k, gx_k) = one_microbatch(xs[0], xs[1])
            with _jax.named_scope("update"):
                return (loss_sum + l_k, _jax.tree.map(_jnp.add, grad_sum, gw_k)), gx_k

        init = (_jnp.zeros((), _jnp.float32), _jax.tree.map(_jnp.zeros_like, weights))
        (loss, grad_w), grad_x = _jax.lax.scan(body, init, (per_example, given["loss_target"]))
    with _jax.named_scope("update"):
        delta_w, new_m, new_v = {}, {}, {}
        for n in TWIN_WEIGHTS:
            delta_w[n], new_m[n], new_v[n] = _adamw(weights[n], grad_w[n], given["m_" + n], given["v_" + n])
    return (loss, grad_x, *[grad_w[n] for n in TWIN_WEIGHTS], *[delta_w[n] for n in TWIN_WEIGHTS],
            *[new_m[n] for n in TWIN_WEIGHTS], *[new_v[n] for n in TWIN_WEIGHTS])
```

```python
import functools

import jax
import jax.numpy as jnp
from jax import lax
from jax.experimental import pallas as pl
from jax.experimental.pallas import tpu as pltpu

F32 = jnp.float32
BF16 = jnp.bfloat16
HI = lax.Precision.HIGHEST
MESH = pl.DeviceIdType.MESH

D_MODEL = 1024
BRANCH = 512
FOX_DH = 64
DN_DH = 128
DN_HEADS = 4
DN_CHUNK = 64
FOX_BLOCK = 128
D_FF = 2816
EPS = 1e-6
N_CHIPS = 4
LANES = 128

ADAM_LR, ADAM_B1, ADAM_B2, ADAM_EPS, ADAM_WD, ADAM_STEP = 0.001, 0.9, 0.999, 1e-08, 0.01, 10

VMEM_LIMIT = 56 * 1024 * 1024

C_FQ, C_FK, C_FV, C_SB, C_SC, C_SV, C_DN, C_DZ, C_GATE = 0, 512, 1024, 1536, 2048, 2560, 3072, 4608, 5120
IN_MAIN = 8192
IN_SIZES = (1536, 8, 1536, 1536, 4, 4, 512, 3072)

BIG = ("w_in", "w_branch", "w_o", "w_up", "w_down", "w_ple_gate", "w_ple")
BIG_AXIS = {"w_in": 2, "w_branch": 3, "w_o": 1, "w_up": 2, "w_down": 1, "w_ple_gate": 1, "w_ple": 2}
CONVS = ("sc_conv_w", "dn_conv_w", "ffn_conv_w")
SMALL = ("g_mix", "b_fox_f", "fox_q_gain", "fox_k_gain", "dn_a_log", "dn_dt_bias", "dn_norm_gain", "g_ffn", "g_ple")
WEIGHTS = ("g_mix", "w_in", "b_fox_f", "fox_q_gain", "fox_k_gain", "sc_conv_w", "dn_conv_w", "dn_a_log", "dn_dt_bias",
           "dn_norm_gain", "w_branch", "w_o", "g_ffn", "w_up", "ffn_conv_w", "w_down", "g_ple", "w_ple_gate", "w_ple")


def _iota(shape, dim):
    return lax.broadcasted_iota(jnp.int32, shape, dim)


def _dg(a, b, mode, prec=None):
    dims = {"nn": ((1,), (0,)), "nt": ((1,), (1,)), "tn": ((0,), (0,))}[mode]
    return lax.dot_general(a, b, (dims, ((), ())), precision=prec, preferred_element_type=F32)


def _bdot_impl(a, b, mode):
    return _dg(a.astype(BF16), b.astype(BF16), mode)


@functools.partial(jax.custom_vjp, nondiff_argnums=(2,))
def _bdot_diff(a, b, mode):
    return _bdot_impl(a, b, mode)


def _bdot_fwd(a, b, mode):
    return _bdot_impl(a, b, mode), (a, b)


def _bdot_bwd(mode, res, g):
    a, b = res
    if mode == "nn":
        da, db = _bdot_impl(g, b, "nt"), _bdot_impl(a, g, "tn")
    elif mode == "nt":
        da, db = _bdot_impl(g, b, "nn"), _bdot_impl(g, a, "tn")
    else:
        da, db = _bdot_impl(b, g, "nt"), _bdot_impl(a, g, "nn")
    return da.astype(a.dtype), db.astype(b.dtype)


_bdot_diff.defvjp(_bdot_fwd, _bdot_bwd)


def _bdot(d):
    return _bdot_diff if d else _bdot_impl


def _shift_impl(x, k):
    return jnp.where(_iota(x.shape, 0) >= k, pltpu.roll(x, k, 0), 0.0)


def _unshift_impl(g, k):
    n = g.shape[0]
    return jnp.where(_iota(g.shape, 0) < n - k, pltpu.roll(g, n - k, 0), 0.0)


@functools.partial(jax.custom_vjp, nondiff_argnums=(1,))
def _shift_diff(x, k):
    return _shift_impl(x, k)


_shift_diff.defvjp(lambda x, k: (_shift_impl(x, k), None), lambda k, _, g: (_unshift_impl(g, k),))


def _row(w, j):
    return jnp.sum(jnp.where(_iota(w.shape, 0) == j, w, 0.0), axis=0, keepdims=True)


def _col(w, j):
    return jnp.sum(jnp.where(_iota(w.shape, 1) == j, w, 0.0), axis=1, keepdims=True)


def _conv(d, x, w):
    shift = _shift_diff if d else _shift_impl
    taps = w.shape[0]
    y = x * _row(w, taps - 1)
    for j in range(taps - 1):
        y = y + shift(x, taps - 1 - j) * _row(w, j)
    return y


def _softplus(x):
    return jnp.maximum(x, 0.0) + jnp.log(1.0 + jnp.exp(-jnp.abs(x)))


def _silu(x):
    return x * jax.nn.sigmoid(x)


def _rms(x, gain):
    return x * lax.rsqrt(jnp.mean(x * x, axis=-1, keepdims=True) + EPS) * gain


def _rms_fn(d, pids, x, gain):
    return (_rms(x, gain),)


def _loss_fn(d, pids, y, t):
    e = y - t
    part = 0.5 / D_MODEL * jnp.sum(e * e, keepdims=True)
    return e * (1.0 / D_MODEL), jnp.broadcast_to(part, (8, LANES))


def _fox_prep_fn(d, pids, q, k, gq, gk):
    first = _iota(q.shape, 1) < FOX_DH

    def norm(x, gain):
        sq = x * x
        ss_a = jnp.sum(jnp.where(first, sq, 0.0), axis=1, keepdims=True)
        ss_b = jnp.sum(jnp.where(first, 0.0, sq), axis=1, keepdims=True)
        rs = jnp.where(first, lax.rsqrt(ss_a / FOX_DH + EPS), lax.rsqrt(ss_b / FOX_DH + EPS))
        return x * rs * gain

    return norm(q, gq) * FOX_DH ** -0.5, norm(k, gk)


def _fox_gate_fn(d, pids, f, bias):
    logf = -_softplus(-(f + bias))
    n_r, n_c = logf.shape
    tri = (_iota((n_c, n_c), 0) <= _iota((n_c, n_c), 1)).astype(F32)
    within = _dg(logf, tri, "nn", HI)
    tot = jnp.broadcast_to(jnp.sum(logf, axis=1, keepdims=True), logf.shape)
    below = (_iota((n_r, n_r), 1) < _iota((n_r, n_r), 0)).astype(F32)
    return (within + _dg(below, tot, "nn", HI),)


def _fox_attn_fn(d, pids, q, k, v, cq_a, cq_b, ck_a, ck_b):
    dot = _bdot(d)
    first = _iota(q.shape, 1) < FOX_DH
    n_q, n_k = q.shape[0], k.shape[0]
    causal = (pids[1] * n_q + _iota((n_q, n_k), 0)) >= _iota((n_q, n_k), 1)

    def head(qh, cq, ck):
        s = dot(qh, k, "nt") + cq - ck
        s = jnp.where(causal, s, -1e30)
        m = lax.stop_gradient(jnp.max(s, axis=1, keepdims=True))
        e = jnp.exp(s - m)
        pr = e / jnp.sum(e, axis=1, keepdims=True)
        return dot(pr, v, "nn")

    o_a = head(jnp.where(first, q, 0.0), cq_a, ck_a)
    o_b = head(jnp.where(first, 0.0, q), cq_b, ck_b)
    return (jnp.where(first, o_a, o_b),)


def _sconv_fn(d, pids, sb, sc, sv, w):
    return (sb * _conv(d, sc * sv, w),)


def _dnconv_fn(d, pids, x, w):
    return (_silu(_conv(d, x, w)),)


def _merge_fn(d, pids, y0, y1, y2, g0, g1, g2):
    return (jax.nn.sigmoid(g0) * y0 + jax.nn.sigmoid(g1) * y1 + jax.nn.sigmoid(g2) * y2,)


def _ffn_act_fn(d, pids, ug, uv, wg, wv):
    return (_silu(_conv(d, ug, wg)) * _conv(d, uv, wv),)


def _ple_fn(d, pids, gpre, pe, x):
    return (x + jax.nn.sigmoid(gpre) * pe,)


def _adam_fn(d, pids, w, g, m, v):
    m2 = ADAM_B1 * m + (1.0 - ADAM_B1) * g
    v2 = ADAM_B2 * v + (1.0 - ADAM_B2) * (g * g)
    m_hat = m2 / (1.0 - ADAM_B1 ** ADAM_STEP)
    v_hat = v2 / (1.0 - ADAM_B2 ** ADAM_STEP)
    delta = -ADAM_LR * (m_hat / (jnp.sqrt(v_hat) + ADAM_EPS) + ADAM_WD * w)
    return delta, m2, v2


def _tri_inv_impl(a):
    n = a.shape[0]
    r, c = _iota((n, n), 0), _iota((n, n), 1)
    diag_blk = (r >> 4) == (c >> 4)
    eye = (r == c).astype(F32)
    mm = lambda u, w: _dg(u, w, "nn", HI)
    x = jnp.where(diag_blk, -a, 0.0)
    p = eye + x
    x2 = mm(x, x)
    p = p + mm(p, x2)
    x4 = mm(x2, x2)
    p = p + mm(p, x4)
    x8 = mm(x4, x4)
    p = p + mm(p, x8)
    y = -mm(p, jnp.where(diag_blk, 0.0, a))
    q = eye + y
    q = q + mm(q, mm(y, y))
    return mm(q, p)


@jax.custom_vjp
def _tri_inv_diff(a):
    return _tri_inv_impl(a)


def _tri_inv_fwd(a):
    t = _tri_inv_impl(a)
    return t, t


def _tri_inv_bwd(t, g):
    return (-_dg(_dg(t, g, "tn", HI), t, "nt", HI),)


_tri_inv_diff.defvjp(_tri_inv_fwd, _tri_inv_bwd)


def _dn_chunk(d, s_prev, q, k, v, z, a_c, a_r, b_c, a_log, dt_b, gain):
    dot = _bdot(d)
    inv = _tri_inv_diff if d else _tri_inv_impl
    n = q.shape[0]
    r, c = _iota((n, n), 0), _iota((n, n), 1)
    incl, strict = r >= c, r > c
    q = q * lax.rsqrt(jnp.sum(q * q, axis=1, keepdims=True) + EPS) * DN_DH ** -0.5
    k = k * lax.rsqrt(jnp.sum(k * k, axis=1, keepdims=True) + EPS)
    beta = jax.nn.sigmoid(b_c)
    rate = -jnp.exp(a_log)
    g_c = rate * _softplus(a_c + dt_b)
    g_r = rate * _softplus(a_r + dt_b)
    gcum_c = jnp.sum(jnp.where(incl, g_r, 0.0), axis=1, keepdims=True)
    gcum_r = jnp.sum(jnp.where(r <= c, g_c, 0.0), axis=0, keepdims=True)
    decay = jnp.exp(jnp.where(incl, gcum_c - gcum_r, -1e30))
    kb = k * beta
    t = inv(jnp.where(strict, dot(kb, k, "nt") * decay, 0.0))
    e_g = jnp.exp(gcum_c)
    u = _dg(t, v * beta, "nn", HI)
    k_cum = _dg(t, kb * e_g, "nn", HI)
    qk = jnp.where(incl, dot(q, k, "nt") * decay, 0.0)
    g_last = jnp.sum(g_c, axis=0, keepdims=True)
    k_dec = k * jnp.exp(g_last - gcum_c)
    v_new = u - dot(k_cum, s_prev, "nn")
    out = dot(q * e_g, s_prev, "nn") + dot(qk, v_new, "nn")
    s_next = s_prev * jnp.exp(g_last) + dot(k_dec, v_new, "tn")
    return _rms(out, gain) * _silu(z), s_next


def _split_heads(t):
    return [t[:, h * DN_DH:(h + 1) * DN_DH] for h in range(t.shape[1] // DN_DH)]


def _dn_heads(d, s_heads, q_heads, k_heads, v_heads, z_heads, ps, a_rows, ad, gain):
    ys, states = [], []
    for h in range(DN_HEADS):
        y, s = _dn_chunk(d, s_heads[h], q_heads[h], k_heads[h], v_heads[h], z_heads[h], _col(ps, 12 + h), _row(a_rows, h),
                         _col(ps, 8 + h), _col(_row(ad, 0), h), _col(_row(ad, 1), h), gain)
        ys.append(y)
        states.append(s)
    return ys, states


def _cparams(n_axes):
    return pltpu.CompilerParams(dimension_semantics=("arbitrary",) * n_axes, vmem_limit_bytes=VMEM_LIMIT)


def _first_visit(acc_axes):
    cond = None
    for a in acc_axes:
        here = pl.program_id(a) == 0
        cond = here if cond is None else jnp.logical_and(cond, here)
    return cond


def _tile(ref):
    val = ref[...]
    shape = val.shape
    while len(shape) > 2 and shape[0] == 1:
        shape = shape[1:]
    return val.reshape(shape)


def _store(ref, val, first):
    val = val.astype(ref.dtype).reshape(ref.shape)
    if first is None:
        ref[...] = val
        return

    @pl.when(first)
    def _():
        ref[...] = val

    @pl.when(jnp.logical_not(first))
    def _():
        ref[...] += val


def _specs(ops):
    return [pl.BlockSpec(block, imap) for _, block, imap in ops]


def tile_fwd(name, fn, grid, ins, outs):
    n_in = len(ins)

    def body(*refs):
        pids = tuple(pl.program_id(a) for a in range(len(grid)))
        firsts = [_first_visit(o[4]) if o[4] else None for o in outs]
        res = fn(False, pids, *[_tile(r) for r in refs[:n_in]])
        for ref, val, first in zip(refs[n_in:], res, firsts):
            _store(ref, val, first)

    out = pl.pallas_call(
        body, grid=grid, in_specs=_specs(ins),
        out_specs=[pl.BlockSpec(o[2], o[3]) for o in outs],
        out_shape=[jax.ShapeDtypeStruct(o[0], o[1]) for o in outs],
        name=name, compiler_params=_cparams(len(grid)),
    )(*[a for a, _, _ in ins])
    return out


def tile_bwd(name, fn, grid, ins, cots, diff, adds=None):
    adds = adds or {}
    n_in, n_cot = len(ins), len(cots)
    add_pos = sorted(adds)
    diff_idx = [d[0] for d in diff]
    out_desc = [d[2] if len(d) > 2 else (ins[d[0]][0].shape, ins[d[0]][1], ins[d[0]][2]) for d in diff]

    def body(*refs):
        pids = tuple(pl.program_id(a) for a in range(len(grid)))
        firsts = [_first_visit(d[1]) if d[1] else None for d in diff]
        vals = [_tile(r) for r in refs[:n_in]]
        cot_vals = [_tile(r) for r in refs[n_in:n_in + n_cot]]
        add_vals = [_tile(r) for r in refs[n_in + n_cot:n_in + n_cot + len(add_pos)]]
        out_refs = refs[n_in + n_cot + len(add_pos):]

        def f(*dv):
            full = list(vals)
            for i, val in zip(diff_idx, dv):
                full[i] = val
            return fn(True, pids, *full)

        prim, vjp = jax.vjp(f, *[vals[i].astype(F32) for i in diff_idx])
        grads = list(vjp(tuple(c.astype(o.dtype) for c, o in zip(cot_vals, prim))))
        for pos, val in zip(add_pos, add_vals):
            grads[pos] = grads[pos] + val.astype(F32)
        for ref, val, first in zip(out_refs, grads, firsts):
            _store(ref, val, first)

    all_ins = list(ins) + list(cots) + [adds[p] for p in add_pos]
    out = pl.pallas_call(
        body, grid=grid, in_specs=_specs(all_ins),
        out_specs=[pl.BlockSpec(o[1], o[2]) for o in out_desc],
        out_shape=[jax.ShapeDtypeStruct(o[0], F32) for o in out_desc],
        name=name, compiler_params=_cparams(len(grid)),
    )(*[a for a, _, _ in all_ins])
    return out


def _pick(dim, cands):
    for c in cands:
        if dim % c == 0:
            return c
    return dim


def mm(name, a, b, mode, add=None, out_dtype=F32):
    if mode == "nn":
        (m, kk), n = a.shape, b.shape[1]
    elif mode == "nt":
        (m, kk), n = a.shape, b.shape[0]
    else:
        (kk, m), n = a.shape, b.shape[1]
    tm, tn, tk = _pick(m, (512, 256, 128)), _pick(n, (512, 256, 128)), _pick(kk, (1024, 512, 256, 128))
    nk = kk // tk
    a_spec = pl.BlockSpec((tk, tm), lambda i, j, k: (k, i)) if mode == "tn" else pl.BlockSpec((tm, tk), lambda i, j, k: (i, k))
    b_spec = pl.BlockSpec((tn, tk), lambda i, j, k: (j, k)) if mode == "nt" else pl.BlockSpec((tk, tn), lambda i, j, k: (k, j))
    o_spec = pl.BlockSpec((tm, tn), lambda i, j, k: (i, j))

    def body(*refs):
        a_ref, b_ref = refs[0], refs[1]
        add_ref = refs[2] if add is not None else None
        o_ref, acc = refs[-2], refs[-1]
        k = pl.program_id(2)
        part = _bdot_impl(a_ref[...], b_ref[...], mode)

        @pl.when(k == 0)
        def _():
            acc[...] = part

        @pl.when(k > 0)
        def _():
            acc[...] += part

        @pl.when(k == nk - 1)
        def _():
            res = acc[...]
            if add_ref is not None:
                res = res + add_ref[...]
            o_ref[...] = res.astype(o_ref.dtype)

    operands = [a, b] + ([add] if add is not None else [])
    in_specs = [a_spec, b_spec] + ([o_spec] if add is not None else [])
    return pl.pallas_call(
        body, grid=(m // tm, n // tn, nk), in_specs=in_specs, out_specs=o_spec,
        out_shape=jax.ShapeDtypeStruct((m, n), out_dtype),
        scratch_shapes=[pltpu.VMEM((tm, tn), F32)],
        name=name, compiler_params=_cparams(3),
    )(*operands)


def _rows(x, width=None, off=0, tm=256):
    width = x.shape[1] if width is None else width
    return (x, (tm, width), lambda i, off=off: (i, off))


def _whole(x):
    nd = x.ndim
    return (x, x.shape, lambda *pids, nd=nd: (0,) * nd)


def _rms_ops(x, gain):
    return [_rows(x), _whole(gain)]


def rms_fwd(name, x, gain):
    s, dm = x.shape
    return tile_fwd(name, _rms_fn, (s // 256,), _rms_ops(x, gain), [((s, dm), BF16, (256, dm), lambda i: (i, 0), ())])[0]


def rms_bwd(name, x, gain, dh, dres):
    s = x.shape[0]
    return tile_bwd(name, _rms_fn, (s // 256,), _rms_ops(x, gain), [_rows(dh)], [(0, ()), (1, (0,))], adds={0: _rows(dres)})


def loss_call(y, t):
    s, dm = y.shape
    dy, part = tile_fwd("loss", _loss_fn, (s // 256,), [_rows(y), _rows(t)],
                        [((s, dm), F32, (256, dm), lambda i: (i, 0), ()), ((8, LANES), F32, (8, LANES), lambda i: (0, 0), (0,))])
    return dy, part[0, 0]


def _fox_prep_ops(pm, gq, gk):
    tm = 512
    return [(pm, (tm, LANES), lambda i, j: (i, C_FQ // LANES + j)), (pm, (tm, LANES), lambda i, j: (i, C_FK // LANES + j)),
            _whole(gq), _whole(gk)]


def fox_prep_fwd(name, pm, gq, gk):
    s = pm.shape[0]
    out = ((s, BRANCH), BF16, (512, LANES), lambda i, j: (i, j), ())
    return tile_fwd(name, _fox_prep_fn, (s // 512, 4), _fox_prep_ops(pm, gq, gk), [out, out])


def fox_prep_bwd(name, pm, gq, gk, dqn, dkn):
    s = pm.shape[0]
    cot = lambda g: (g, (512, LANES), lambda i, j: (i, j))
    own = ((s, BRANCH), (512, LANES), lambda i, j: (i, j))
    return tile_bwd(name, _fox_prep_fn, (s // 512, 4), _fox_prep_ops(pm, gq, gk), [cot(dqn), cot(dkn)],
                    [(0, (), own), (1, (), own), (2, (0, 1)), (3, (0, 1))])


def _fox_gate_ops(f_t, bias):
    return [(f_t, (1,) + f_t.shape[1:], lambda h: (h, 0, 0)), (bias, (1, 1, 1), lambda h: (h, 0, 0))]


def fox_gate_fwd(name, f_t, bias):
    n_h = f_t.shape[0]
    return tile_fwd(name, _fox_gate_fn, (n_h,), _fox_gate_ops(f_t, bias),
                    [(f_t.shape, F32, (1,) + f_t.shape[1:], lambda h: (h, 0, 0), ())])[0]


def fox_gate_bwd(name, f_t, bias, dcum):
    n_h = f_t.shape[0]
    return tile_bwd(name, _fox_gate_fn, (n_h,), _fox_gate_ops(f_t, bias),
                    [(dcum, (1,) + f_t.shape[1:], lambda h: (h, 0, 0))], [(0, ()), (1, ())])


def _fox_attn_ops(qn, kn, pm, cum_c, cum_r):
    s = qn.shape[0]
    nb = FOX_BLOCK
    return [(qn, (nb, LANES), lambda p, i: (i, p)), (kn, (s, LANES), lambda p, i: (0, p)),
            (pm, (s, LANES), lambda p, i: (0, C_FV // LANES + p)),
            (cum_c, (1, nb, 1), lambda p, i: (2 * p, i, 0)), (cum_c, (1, nb, 1), lambda p, i: (2 * p + 1, i, 0)),
            (cum_r, (1, 1, s), lambda p, i: (2 * p, 0, 0)), (cum_r, (1, 1, s), lambda p, i: (2 * p + 1, 0, 0))]


def fox_attn_fwd(name, qn, kn, pm, cum_c, cum_r):
    s = qn.shape[0]
    return tile_fwd(name, _fox_attn_fn, (4, s // FOX_BLOCK), _fox_attn_ops(qn, kn, pm, cum_c, cum_r),
                    [((s, BRANCH), F32, (FOX_BLOCK, LANES), lambda p, i: (i, p), ())])[0]


def fox_attn_bwd(name, qn, kn, pm, cum_c, cum_r, dy):
    s = qn.shape[0]
    pair_c = ((4, s, 1), (1, FOX_BLOCK, 1), lambda p, i: (p, i, 0))
    pair_r = ((4, 1, s), (1, 1, s), lambda p, i: (p, 0, 0))
    d_qn, d_kn, d_v, d_cqa, d_cqb, d_cka, d_ckb = tile_bwd(
        name, _fox_attn_fn, (4, s // FOX_BLOCK), _fox_attn_ops(qn, kn, pm, cum_c, cum_r),
        [(dy, (FOX_BLOCK, LANES), lambda p, i: (i, p))],
        [(0, ()), (1, (1,)), (2, (1,), ((s, BRANCH), (s, LANES), lambda p, i: (0, p))),
         (3, (), pair_c), (4, (), pair_c), (5, (1,), pair_r), (6, (1,), pair_r)])
    d_cum = jnp.stack([d_cqa[:, :, 0] + d_cka[:, 0, :], d_cqb[:, :, 0] + d_ckb[:, 0, :]], axis=1).reshape(8, s)
    return d_qn, d_kn, d_v, d_cum


def sconv_ops(pm, w):
    s = pm.shape[0]
    blk = lambda c0: (pm, (s, LANES), lambda j, c0=c0: (0, c0 // LANES + j))
    return [blk(C_SB), blk(C_SC), blk(C_SV), (w, (w.shape[0], LANES), lambda j: (0, j))]


def dnconv_ops(pm, w):
    s = pm.shape[0]
    return [(pm, (s, LANES), lambda j: (0, C_DN // LANES + j)), (w, (w.shape[0], LANES), lambda j: (0, j))]


def ffn_ops(ug, uv, w):
    s = ug.shape[0]
    n_t = D_FF // LANES
    return [(ug, (s, LANES), lambda j: (0, j)), (uv, (s, LANES), lambda j: (0, j)),
            (w, (w.shape[0], LANES), lambda j: (0, j)), (w, (w.shape[0], LANES), lambda j: (0, n_t + j))]


def _col_out(s, width, dtype=F32):
    return ((s, width), dtype, (s, LANES), lambda j: (0, j), ())


def _col_cot(g):
    return (g, (g.shape[0], LANES), lambda j: (0, j))


def merge_ops(yp, pm):
    gate = lambda b: (pm, (256, D_MODEL), lambda i, b=b: (i, C_GATE // D_MODEL + b))
    return [_rows(yp[0]), _rows(yp[1]), _rows(yp[2]), gate(0), gate(1), gate(2)]


def ple_ops(gpre, pe, x):
    return [_rows(gpre), _rows(pe), _rows(x)]


def adam_call(name, w, g, m, v):
    shape = w.shape
    last = shape[-1]
    rows = w.size // last
    flat = lambda t: t.reshape(rows, last)
    tm = rows
    for cand in (512, 256, 128, 64, 32, 16, 8):
        if rows % cand == 0 and cand * last * 4 <= 2 * 1024 * 1024:
            tm = cand
            break
    spec = lambda t: (flat(t), (tm, last), lambda i: (i, 0))
    out = ((rows, last), F32, (tm, last), lambda i: (i, 0), ())
    res = tile_fwd(name, _adam_fn, (rows // tm,), [spec(w), spec(g), spec(m), spec(v)], [out, out, out])
    return [r.reshape(shape) for r in res]


def dn_fwd(name, dn_act, pm, ps, a_rows, ad, gain):
    s = dn_act.shape[0]
    n_c = s // DN_CHUNK

    def body(qkv_ref, z_ref, ps_ref, ar_ref, ad_ref, g_ref, y_ref, hist_ref, state):
        @pl.when(pl.program_id(0) == 0)
        def _():
            state[...] = jnp.zeros_like(state)

        hist_ref[0] = state[...]
        heads = _split_heads(qkv_ref[...])
        ys, s_next = _dn_heads(False, [state[h] for h in range(DN_HEADS)], heads[0:4], heads[4:8], heads[8:12],
                               _split_heads(z_ref[...]), ps_ref[...], ar_ref[0], ad_ref[...], g_ref[...])
        y_ref[...] = jnp.concatenate(ys, axis=1)
        for h in range(DN_HEADS):
            state[h] = s_next[h]

    return pl.pallas_call(
        body, grid=(n_c,),
        in_specs=[pl.BlockSpec((DN_CHUNK, 3 * BRANCH), lambda j: (j, 0)),
                  pl.BlockSpec((DN_CHUNK, BRANCH), lambda j: (j, C_DZ // BRANCH)),
                  pl.BlockSpec((DN_CHUNK, LANES), lambda j: (j, 0)),
                  pl.BlockSpec((1, DN_HEADS, DN_CHUNK), lambda j: (j, 0, 0)),
                  pl.BlockSpec((2, DN_HEADS), lambda j: (0, 0)),
                  pl.BlockSpec((1, DN_DH), lambda j: (0, 0))],
        out_specs=[pl.BlockSpec((DN_CHUNK, BRANCH), lambda j: (j, 0)),
                   pl.BlockSpec((1, DN_HEADS, DN_DH, DN_DH), lambda j: (j, 0, 0, 0))],
        out_shape=[jax.ShapeDtypeStruct((s, BRANCH), F32), jax.ShapeDtypeStruct((n_c, DN_HEADS, DN_DH, DN_DH), F32)],
        scratch_shapes=[pltpu.VMEM((DN_HEADS, DN_DH, DN_DH), F32)],
        name=name, compiler_params=_cparams(1),
    )(dn_act, pm, ps, a_rows, ad, gain)


def dn_bwd(name, dn_act, pm, ps, a_rows, ad, gain, hist, dy):
    s = dn_act.shape[0]
    n_c = s // DN_CHUNK

    def body(qkv_ref, z_ref, ps_ref, ar_ref, ad_ref, g_ref, hist_ref, dy_ref,
             dqkv_ref, dz_ref, dps_ref, dar_ref, dad_ref, dg_ref, d_state):
        first = pl.program_id(0) == 0

        @pl.when(first)
        def _():
            d_state[...] = jnp.zeros_like(d_state)

        def f(*args):
            return _dn_heads(True, *args)

        heads = _split_heads(qkv_ref[...])
        _, vjp = jax.vjp(f, [hist_ref[0, h] for h in range(DN_HEADS)], heads[0:4], heads[4:8], heads[8:12],
                         _split_heads(z_ref[...]), ps_ref[...], ar_ref[0], ad_ref[...], g_ref[...])
        d_s, d_q, d_k, d_v, d_z, d_ps, d_ar, d_ad, d_gain = vjp((_split_heads(dy_ref[...]), [d_state[h] for h in range(DN_HEADS)]))
        for h in range(DN_HEADS):
            d_state[h] = d_s[h]
        dqkv_ref[...] = jnp.concatenate(list(d_q) + list(d_k) + list(d_v), axis=1)
        dz_ref[...] = jnp.concatenate(list(d_z), axis=1)
        dps_ref[...] = d_ps
        dar_ref[0] = d_ar
        _store(dad_ref, d_ad, first)
        _store(dg_ref, d_gain, first)

    rev = lambda j: n_c - 1 - j
    return pl.pallas_call(
        body, grid=(n_c,),
        in_specs=[pl.BlockSpec((DN_CHUNK, 3 * BRANCH), lambda j: (rev(j), 0)),
                  pl.BlockSpec((DN_CHUNK, BRANCH), lambda j: (rev(j), C_DZ // BRANCH)),
                  pl.BlockSpec((DN_CHUNK, LANES), lambda j: (rev(j), 0)),
                  pl.BlockSpec((1, DN_HEADS, DN_CHUNK), lambda j: (rev(j), 0, 0)),
                  pl.BlockSpec((2, DN_HEADS), lambda j: (0, 0)),
                  pl.BlockSpec((1, DN_DH), lambda j: (0, 0)),
                  pl.BlockSpec((1, DN_HEADS, DN_DH, DN_DH), lambda j: (rev(j), 0, 0, 0)),
                  pl.BlockSpec((DN_CHUNK, BRANCH), lambda j: (rev(j), 0))],
        out_specs=[pl.BlockSpec((DN_CHUNK, 3 * BRANCH), lambda j: (rev(j), 0)),
                   pl.BlockSpec((DN_CHUNK, BRANCH), lambda j: (rev(j), 0)),
                   pl.BlockSpec((DN_CHUNK, LANES), lambda j: (rev(j), 0)),
                   pl.BlockSpec((1, DN_HEADS, DN_CHUNK), lambda j: (rev(j), 0, 0)),
                   pl.BlockSpec((2, DN_HEADS), lambda j: (0, 0)),
                   pl.BlockSpec((1, DN_DH), lambda j: (0, 0))],
        out_shape=[jax.ShapeDtypeStruct((s, 3 * BRANCH), F32), jax.ShapeDtypeStruct((s, BRANCH), F32),
                   jax.ShapeDtypeStruct((s, LANES), F32), jax.ShapeDtypeStruct((n_c, DN_HEADS, DN_CHUNK), F32),
                   jax.ShapeDtypeStruct((2, DN_HEADS), F32), jax.ShapeDtypeStruct((1, DN_DH), F32)],
        scratch_shapes=[pltpu.VMEM((DN_HEADS, DN_DH, DN_DH), F32)],
        name=name, compiler_params=_cparams(1),
    )(dn_act, pm, ps, a_rows, ad, gain, hist, dy)


def _seq_layouts(cols, s):
    return cols.T.reshape(cols.shape[1], s // LANES, LANES)


def layer_fwd(li, x, p, w):
    s = x.shape[0]
    n = lambda t: f"{t}_l{li}"
    h = rms_fwd(n("rms_mix"), x, w["g_mix"])
    pm = mm(n("in_main"), h, w["in_main"], "nn")
    ps = mm(n("in_small"), h, w["in_small"], "nn")
    qn, kn = fox_prep_fwd(n("fox_prep"), pm, w["gq"], w["gk"])
    f_t = _seq_layouts(ps[:, 0:8], s)
    cum = fox_gate_fwd(n("fox_gate"), f_t, w["b_f"])
    cum_c, cum_r = cum.reshape(8, s, 1), cum.reshape(8, 1, s)
    y_fox = fox_attn_fwd(n("fox_attn"), qn, kn, pm, cum_c, cum_r)
    y_sc = tile_fwd(n("sconv"), _sconv_fn, (BRANCH // LANES,), sconv_ops(pm, w["sc_conv_w"]), [_col_out(s, BRANCH)])[0]
    dn_act = tile_fwd(n("dnconv"), _dnconv_fn, (3 * BRANCH // LANES,), dnconv_ops(pm, w["dn_conv_w"]), [_col_out(s, 3 * BRANCH)])[0]
    a_rows = ps[:, 12:16].reshape(s // DN_CHUNK, DN_CHUNK, DN_HEADS).transpose(0, 2, 1)
    y_dn, hist = dn_fwd(n("dn_fwd"), dn_act, pm, ps, a_rows, w["ad"], w["dn_gain"])
    ys = (y_fox, y_sc, y_dn)
    yp = [mm(n(f"branch{b}"), ys[b], w["branch"][b], "nn") for b in range(3)]
    merged = tile_fwd(n("merge"), _merge_fn, (s // 256,), merge_ops(yp, pm), [((s, D_MODEL), BF16, (256, D_MODEL), lambda i: (i, 0), ())])[0]
    x1 = mm(n("w_o"), merged, w["o"], "nn", add=x)
    h2 = rms_fwd(n("rms_ffn"), x1, w["g_ffn"])
    ug = mm(n("up_g"), h2, w["up_g"], "nn")
    uv = mm(n("up_v"), h2, w["up_v"], "nn")
    act = tile_fwd(n("ffn_act"), _ffn_act_fn, (D_FF // LANES,), ffn_ops(ug, uv, w["ffn_conv_w"]), [_col_out(s, D_FF, BF16)])[0]
    x2 = mm(n("down"), act, w["down"], "nn", add=x1)
    h3 = rms_fwd(n("rms_ple"), x2, w["g_ple"])
    gpre = mm(n("ple_gate"), h3, w["pg"], "nn")
    pe = mm(n("ple_emb"), p, w["ple"], "nn")
    x3 = tile_fwd(n("ple"), _ple_fn, (s // 256,), ple_ops(gpre, pe, x2), [((s, D_MODEL), F32, (256, D_MODEL), lambda i: (i, 0), ())])[0]
    saved = dict(x=x, h=h, pm=pm, ps=ps, qn=qn, kn=kn, f_t=f_t, cum_c=cum_c, cum_r=cum_r, ys=ys, dn_act=dn_act,
                 a_rows=a_rows, hist=hist, yp=yp, merged=merged, x1=x1, h2=h2, ug=ug, uv=uv, act=act, x2=x2, h3=h3,
                 gpre=gpre, pe=pe, p=p)
    return x3, saved


def layer_bwd(li, dx3, sv, w):
    s = dx3.shape[0]
    n = lambda t: f"{t}_l{li}"
    g = {}
    col_own = lambda width: ((s, width), (s, LANES), lambda j: (0, j))
    d_gpre, d_pe = tile_bwd(n("ple_bwd"), _ple_fn, (s // 256,), ple_ops(sv["gpre"], sv["pe"], sv["x2"]), [_rows(dx3)], [(0, ()), (1, ())])
    g["w_ple"] = mm(n("d_w_ple"), sv["p"], d_pe, "tn")
    g["w_ple_gate"] = mm(n("d_w_pg"), sv["h3"], d_gpre, "tn")
    dh3 = mm(n("d_h3"), d_gpre, w["pg"], "nt")
    dx2, d_g_ple = rms_bwd(n("rms_ple_bwd"), sv["x2"], w["g_ple"], dh3, dx3)
    dact = mm(n("d_act"), dx2, w["down"], "nt")
    g["w_down"] = mm(n("d_w_down"), sv["act"], dx2, "tn")
    taps_own = ((w["ffn_conv_w"].shape[0], D_FF), (w["ffn_conv_w"].shape[0], LANES), lambda j: (0, j))
    d_ug, d_uv, d_fw_g, d_fw_v = tile_bwd(n("ffn_act_bwd"), _ffn_act_fn, (D_FF // LANES,), ffn_ops(sv["ug"], sv["uv"], w["ffn_conv_w"]),
                                          [_col_cot(dact)], [(0, ()), (1, ()), (2, (), taps_own), (3, (), taps_own)])
    g["ffn_conv_w"] = jnp.concatenate([d_fw_g, d_fw_v], axis=1)
    g["w_up"] = jnp.concatenate([mm(n("d_w_up_g"), sv["h2"], d_ug, "tn"), mm(n("d_w_up_v"), sv["h2"], d_uv, "tn")], axis=1)
    dh2 = mm(n("d_h2_v"), d_uv, w["up_v"], "nt", add=mm(n("d_h2_g"), d_ug, w["up_g"], "nt"))
    dx1, d_g_ffn = rms_bwd(n("rms_ffn_bwd"), sv["x1"], w["g_ffn"], dh2, dx2)
    dmerged = mm(n("d_merged"), dx1, w["o"], "nt")
    g["w_o"] = mm(n("d_w_o"), sv["merged"], dx1, "tn")
    gate_own = ((s, D_MODEL), (256, D_MODEL), lambda i: (i, 0))
    d_yp0, d_yp1, d_yp2, d_g0, d_g1, d_g2 = tile_bwd(n("merge_bwd"), _merge_fn, (s // 256,), merge_ops(sv["yp"], sv["pm"]), [_rows(dmerged)],
                                                     [(0, ()), (1, ()), (2, ()), (3, (), gate_own), (4, (), gate_own), (5, (), gate_own)])
    d_yp = (d_yp0, d_yp1, d_yp2)
    g["w_branch"] = jnp.stack([mm(n(f"d_w_branch{b}"), sv["ys"][b], d_yp[b], "tn") for b in range(3)])
    d_ys = [mm(n(f"d_y{b}"), d_yp[b], w["branch"][b], "nt") for b in range(3)]
    d_dnact, d_z, d_ps_dn, d_arows, d_ad, d_dngain = dn_bwd(n("dn_bwd"), sv["dn_act"], sv["pm"], sv["ps"], sv["a_rows"], w["ad"], w["dn_gain"],
                                                            sv["hist"], d_ys[2])
    g["ad"], g["dn_norm_gain"] = d_ad, d_dngain[0]
    d_dnqkv, g["dn_conv_w"] = tile_bwd(n("dnconv_bwd"), _dnconv_fn, (3 * BRANCH // LANES,), dnconv_ops(sv["pm"], w["dn_conv_w"]),
                                       [_col_cot(d_dnact)], [(0, (), col_own(3 * BRANCH)), (1, ())])
    d_sb, d_sc, d_sv, g["sc_conv_w"] = tile_bwd(n("sconv_bwd"), _sconv_fn, (BRANCH // LANES,), sconv_ops(sv["pm"], w["sc_conv_w"]), [_col_cot(d_ys[1])],
                                                [(0, (), col_own(BRANCH)), (1, (), col_own(BRANCH)), (2, (), col_own(BRANCH)), (3, ())])
    d_qn, d_kn, d_fv, d_cum = fox_attn_bwd(n("fox_attn_bwd"), sv["qn"], sv["kn"], sv["pm"], sv["cum_c"], sv["cum_r"], d_ys[0])
    d_ft, d_bf = fox_gate_bwd(n("fox_gate_bwd"), sv["f_t"], w["b_f"], d_cum.reshape(8, s // LANES, LANES))
    g["b_fox_f"] = d_bf.reshape(8)
    d_fq, d_fk, d_gq, d_gk = fox_prep_bwd(n("fox_prep_bwd"), sv["pm"], w["gq"], w["gk"], d_qn, d_kn)
    g["fox_q_gain"] = d_gq[0, :FOX_DH] + d_gq[0, FOX_DH:]
    g["fox_k_gain"] = d_gk[0, :FOX_DH] + d_gk[0, FOX_DH:]
    d_pm = jnp.concatenate([d_fq, d_fk, d_fv, d_sb, d_sc, d_sv, d_dnqkv, d_z, d_g0, d_g1, d_g2], axis=1)
    d_a_cols = d_arows.transpose(0, 2, 1).reshape(s, DN_HEADS)
    d_f_cols = d_ft.reshape(8, s).T
    d_ps = d_ps_dn + jnp.concatenate([d_f_cols, jnp.zeros((s, 4), F32), d_a_cols, jnp.zeros((s, LANES - 16), F32)], axis=1)
    g["in_main"] = mm(n("d_w_in_main"), sv["h"], d_pm, "tn")
    g["in_small"] = mm(n("d_w_in_small"), sv["h"], d_ps, "tn")
    dh = mm(n("d_h_small"), d_ps, w["in_small"], "nt", add=mm(n("d_h_main"), d_pm, w["in_main"], "nt"))
    dx, d_g_mix = rms_bwd(n("rms_mix_bwd"), sv["x"], w["g_mix"], dh, dx1)
    g["g_mix"], g["g_ffn"], g["g_ple"] = d_g_mix[0], d_g_ffn[0], d_g_ple[0]
    return dx, g


def split_w_in(w_in):
    offs = [0]
    for sz in IN_SIZES:
        offs.append(offs[-1] + sz)
    seg = lambda i: w_in[:, offs[i]:offs[i + 1]]
    main = jnp.concatenate([seg(0), seg(2), seg(3), seg(6), seg(7)], axis=1)
    small = jnp.concatenate([seg(1), seg(4), seg(5), jnp.zeros((w_in.shape[0], LANES - 16), w_in.dtype)], axis=1)
    return main, small


def join_w_in(main, small):
    m = lambda lo, hi: main[:, lo:hi]
    return jnp.concatenate([m(0, 1536), small[:, 0:8], m(1536, 3072), m(3072, 4608), small[:, 8:12], small[:, 12:16],
                            m(4608, 5120), m(5120, 8192)], axis=1)


def layer_weights(li, full, conv, a):
    main, small = split_w_in(full["w_in"][li])
    w_up = full["w_up"][li]
    tile2 = lambda v: jnp.concatenate([v, v])[None, :]
    return dict(
        in_main=main, in_small=small, branch=full["w_branch"][li], o=full["w_o"][li], up_g=w_up[:, :D_FF], up_v=w_up[:, D_FF:],
        down=full["w_down"][li], pg=full["w_ple_gate"][li], ple=full["w_ple"][li],
        g_mix=a["g_mix"][li][None, :], g_ffn=a["g_ffn"][li][None, :], g_ple=a["g_ple"][li][None, :],
        gq=tile2(a["fox_q_gain"][li]), gk=tile2(a["fox_k_gain"][li]), b_f=a["b_fox_f"][li].reshape(8, 1, 1),
        ad=jnp.stack([a["dn_a_log"][li], a["dn_dt_bias"][li]]), dn_gain=a["dn_norm_gain"][li][None, :],
        sc_conv_w=conv["sc_conv_w"][li], dn_conv_w=conv["dn_conv_w"][li], ffn_conv_w=conv["ffn_conv_w"][li])


def pack_rows(arrs, dtype):
    flat = jnp.concatenate([t.reshape(-1).astype(dtype) for t in arrs])
    pad = (-flat.shape[0]) % (8 * LANES)
    if pad:
        flat = jnp.concatenate([flat, jnp.zeros((pad,), dtype)])
    return flat.reshape(-1, LANES)


def unpack_rows(buf, shapes):
    flat = buf.reshape(-1)
    out, off = [], 0
    for shp in shapes:
        size = 1
        for dim in shp:
            size *= dim
        out.append(flat[off:off + size].reshape(shp))
        off += size
    return out


def chip_shard(t, axis, k):
    width = t.shape[axis] // N_CHIPS
    return lax.slice_in_dim(t, k * width, (k + 1) * width, axis=axis)


ANY = pl.BlockSpec(memory_space=pl.ANY)


def _position():
    x, y, c = lax.axis_index("x"), lax.axis_index("y"), lax.axis_index("c")
    return x, y, c, [(1 - x, y), (x, 1 - y), (1 - x, 1 - y)]


def gather_small(name, block):
    m_per, n = block.shape

    def body(x_ref, out_ref, send_sems, recv_sems, local_sem):
        x, y, c, chips = _position()
        me, sibling = (x, y, c), (x, y, 1 - c)

        def rows(px, py, pc):
            return out_ref.at[pl.ds((4 * px + 2 * py + pc) * m_per, m_per), :]

        def copy(k, blk, to, src=None):
            return pltpu.make_async_remote_copy(src_ref=rows(*blk) if src is None else src, dst_ref=rows(*blk),
                                                send_sem=send_sems.at[k], recv_sem=recv_sems.at[k], device_id=to, device_id_type=MESH)

        mine = pltpu.make_async_copy(x_ref, rows(*me), local_sem)
        mine.start()
        first = [copy(0, me, sibling, src=x_ref)] + [copy(1 + j, me, (*chip, c), src=x_ref) for j, chip in enumerate(chips)]
        for cp in first:
            cp.start()
        passed = [copy(4 + j, (*chip, c), sibling) for j, chip in enumerate(chips)]
        for j, chip in enumerate(chips):
            copy(1 + j, (*chip, c), me).wait_recv()
            passed[j].start()
        copy(0, sibling, me).wait_recv()
        for j, chip in enumerate(chips):
            copy(4 + j, (*chip, 1 - c), me).wait_recv()
        for cp in first + passed:
            cp.wait_send()
        mine.wait()

    return pl.pallas_call(
        body, out_shape=jax.ShapeDtypeStruct((8 * m_per, n), block.dtype),
        in_specs=[pl.BlockSpec(memory_space=pltpu.VMEM)], out_specs=pl.BlockSpec(memory_space=pltpu.VMEM),
        scratch_shapes=[pltpu.SemaphoreType.DMA((7,)), pltpu.SemaphoreType.DMA((7,)), pltpu.SemaphoreType.DMA],
        name=name, compiler_params=pltpu.CompilerParams(vmem_limit_bytes=VMEM_LIMIT),
    )(block)


def gather_weights(shard):
    r_rows, n = shard.shape
    half = r_rows // 2

    def body(x_ref, out_ref, send_sems, recv_sems, local_sem):
        x, y, c, chips = _position()
        sibling = (x, y, 1 - c)

        def part(px, py, pc):
            return out_ref.at[2 * px + py, pl.ds(pc * half, half), :]

        def copy(k, blk, to, src=None):
            return pltpu.make_async_remote_copy(src_ref=part(*blk) if src is None else src, dst_ref=part(*blk),
                                                send_sem=send_sems.at[k], recv_sem=recv_sems.at[k], device_id=to, device_id_type=MESH)

        mine = pltpu.make_async_copy(x_ref, out_ref.at[2 * x + y], local_sem)
        mine.start()
        first = [copy(j, (x, y, c), (*chip, c), src=x_ref.at[pl.ds(c * half, half), :]) for j, chip in enumerate(chips)]
        for cp in first:
            cp.start()
        passed = [copy(3 + j, (*chip, c), sibling) for j, chip in enumerate(chips)]
        for j, chip in enumerate(chips):
            copy(j, (*chip, c), (x, y, c)).wait_recv()
            passed[j].start()
        for j, chip in enumerate(chips):
            copy(3 + j, (*chip, 1 - c), (x, y, c)).wait_recv()
        for cp in first + passed:
            cp.wait_send()
        mine.wait()

    return pl.pallas_call(
        body, out_shape=jax.ShapeDtypeStruct((N_CHIPS, r_rows, n), shard.dtype), in_specs=[ANY], out_specs=ANY,
        scratch_shapes=[pltpu.SemaphoreType.DMA((6,)), pltpu.SemaphoreType.DMA((6,)), pltpu.SemaphoreType.DMA],
        name="gather_weights",
    )(shard)


def swap_sibling(name, buf):
    def body(x_ref, out_ref, send_sem, recv_sem):
        x, y, c, _ = _position()
        cp = pltpu.make_async_remote_copy(src_ref=x_ref, dst_ref=out_ref, send_sem=send_sem, recv_sem=recv_sem,
                                          device_id=(x, y, 1 - c), device_id_type=MESH)
        cp.start()
        cp.wait()

    return pl.pallas_call(
        body, out_shape=jax.ShapeDtypeStruct(buf.shape, buf.dtype), in_specs=[ANY], out_specs=ANY,
        scratch_shapes=[pltpu.SemaphoreType.DMA, pltpu.SemaphoreType.DMA], name=name,
    )(buf)


def scatter_chips(partial):
    _, h_rows, n = partial.shape

    def body(x_ref, out_ref, send_sems, recv_sems):
        x, y, c, chips = _position()
        cps = [pltpu.make_async_remote_copy(src_ref=x_ref.at[2 * cx + cy], dst_ref=out_ref.at[j], send_sem=send_sems.at[j],
                                            recv_sem=recv_sems.at[j], device_id=(cx, cy, c), device_id_type=MESH)
               for j, (cx, cy) in enumerate(chips)]
        for cp in cps:
            cp.start()
        for cp in cps:
            cp.wait()

    return pl.pallas_call(
        body, out_shape=jax.ShapeDtypeStruct((3, h_rows, n), partial.dtype), in_specs=[ANY], out_specs=ANY,
        scratch_shapes=[pltpu.SemaphoreType.DMA((3,)), pltpu.SemaphoreType.DMA((3,))], name="scatter_chips",
    )(partial)


def share_halves(fin):
    h_rows, n = fin.shape

    def body(x_ref, out_ref, send_sem, recv_sem, local_sem):
        x, y, c, _ = _position()
        mine = pltpu.make_async_copy(x_ref, out_ref.at[pl.ds(c * h_rows, h_rows), :], local_sem)
        mine.start()
        cp = pltpu.make_async_remote_copy(src_ref=x_ref, dst_ref=out_ref.at[pl.ds(c * h_rows, h_rows), :], send_sem=send_sem,
                                          recv_sem=recv_sem, device_id=(x, y, 1 - c), device_id_type=MESH)
        cp.start()
        recv = pltpu.make_async_remote_copy(src_ref=x_ref, dst_ref=out_ref.at[pl.ds((1 - c) * h_rows, h_rows), :], send_sem=send_sem,
                                            recv_sem=recv_sem, device_id=(x, y, 1 - c), device_id_type=MESH)
        recv.wait_recv()
        cp.wait_send()
        mine.wait()

    return pl.pallas_call(
        body, out_shape=jax.ShapeDtypeStruct((2 * h_rows, n), fin.dtype), in_specs=[ANY], out_specs=ANY,
        scratch_shapes=[pltpu.SemaphoreType.DMA, pltpu.SemaphoreType.DMA, pltpu.SemaphoreType.DMA], name="share_halves",
    )(fin)


ROW_TILE = 1952


def _half_spec(n_tiles, other):
    if other:
        return pl.BlockSpec((1, ROW_TILE, LANES), lambda k, i, pos: (k, (1 - pos[0]) * n_tiles + i, 0))
    return pl.BlockSpec((1, ROW_TILE, LANES), lambda k, i, pos: (k, pos[0] * n_tiles + i, 0))


def cast_other_half(pos, gpack):
    _, r_rows, n = gpack.shape
    n_tiles = r_rows // 2 // ROW_TILE

    def body(pos_ref, g_ref, o_ref):
        o_ref[...] = g_ref[...].astype(BF16)

    return pl.pallas_call(
        body, grid_spec=pltpu.PrefetchScalarGridSpec(
            num_scalar_prefetch=1, grid=(N_CHIPS, n_tiles), in_specs=[_half_spec(n_tiles, True)],
            out_specs=pl.BlockSpec((1, ROW_TILE, LANES), lambda k, i, pos: (k, i, 0))),
        out_shape=jax.ShapeDtypeStruct((N_CHIPS, r_rows // 2, n), BF16), name="cast_other_half", compiler_params=_cparams(2),
    )(pos, gpack)


def pair_sum(pos, gpack, from_sibling):
    _, r_rows, n = gpack.shape
    n_tiles = r_rows // 2 // ROW_TILE

    def body(pos_ref, g_ref, s_ref, f_ref, b_ref):
        tot = g_ref[...] + s_ref[...].astype(F32)
        f_ref[...] = tot
        b_ref[...] = tot.astype(BF16)

    half_blk = pl.BlockSpec((1, ROW_TILE, LANES), lambda k, i, pos: (k, i, 0))
    return pl.pallas_call(
        body, grid_spec=pltpu.PrefetchScalarGridSpec(
            num_scalar_prefetch=1, grid=(N_CHIPS, n_tiles), in_specs=[_half_spec(n_tiles, False), half_blk],
            out_specs=[half_blk, half_blk]),
        out_shape=[jax.ShapeDtypeStruct((N_CHIPS, r_rows // 2, n), F32), jax.ShapeDtypeStruct((N_CHIPS, r_rows // 2, n), BF16)],
        name="pair_sum", compiler_params=_cparams(2),
    )(pos, gpack, from_sibling)


def chip_sum(pos, pair_f32, landed):
    _, h_rows, n = pair_f32.shape
    n_tiles = h_rows // ROW_TILE

    def body(pos_ref, p_ref, l_ref, o_ref):
        o_ref[...] = ((p_ref[0] + l_ref[0].astype(F32)) + l_ref[1].astype(F32)) + l_ref[2].astype(F32)

    return pl.pallas_call(
        body, grid_spec=pltpu.PrefetchScalarGridSpec(
            num_scalar_prefetch=1, grid=(n_tiles,),
            in_specs=[pl.BlockSpec((1, ROW_TILE, LANES), lambda i, pos: (pos[1], i, 0)),
                      pl.BlockSpec((3, ROW_TILE, LANES), lambda i, pos: (0, i, 0))],
            out_specs=pl.BlockSpec((ROW_TILE, LANES), lambda i, pos: (i, 0))),
        out_shape=jax.ShapeDtypeStruct((h_rows, n), F32), name="chip_sum", compiler_params=_cparams(1),
    )(pos, pair_f32, landed)


def sum_devices(gathered):
    m_per = gathered.shape[0] // 8

    def body(g_ref, o_ref):
        tot = g_ref[pl.ds(0, m_per), :]
        for dev in range(1, 8):
            tot = tot + g_ref[pl.ds(dev * m_per, m_per), :]
        o_ref[...] = tot

    return pl.pallas_call(
        body, out_shape=jax.ShapeDtypeStruct((m_per, gathered.shape[1]), F32),
        in_specs=[pl.BlockSpec(memory_space=pltpu.VMEM)], out_specs=pl.BlockSpec(memory_space=pltpu.VMEM), name="sum_devices",
    )(gathered)


def kernel(x, p, g_mix, w_in, b_fox_f, fox_q_gain, fox_k_gain, sc_conv_w, dn_conv_w, dn_a_log, dn_dt_bias, dn_norm_gain, w_branch, w_o, g_ffn, w_up, ffn_conv_w, w_down, g_ple, w_ple_gate, w_ple, loss_target, m_g_mix, m_w_in, m_b_fox_f, m_fox_q_gain, m_fox_k_gain, m_sc_conv_w, m_dn_conv_w, m_dn_a_log, m_dn_dt_bias, m_dn_norm_gain, m_w_branch, m_w_o, m_g_ffn, m_w_up, m_ffn_conv_w, m_w_down, m_g_ple, m_w_ple_gate, m_w_ple, v_g_mix, v_w_in, v_b_fox_f, v_fox_q_gain, v_fox_k_gain, v_sc_conv_w, v_dn_conv_w, v_dn_a_log, v_dn_dt_bias, v_dn_norm_gain, v_w_branch, v_w_o, v_g_ffn, v_w_up, v_ffn_conv_w, v_w_down, v_g_ple, v_w_ple_gate, v_w_ple):
    a = dict(g_mix=g_mix, w_in=w_in, b_fox_f=b_fox_f, fox_q_gain=fox_q_gain, fox_k_gain=fox_k_gain, sc_conv_w=sc_conv_w,
             dn_conv_w=dn_conv_w, dn_a_log=dn_a_log, dn_dt_bias=dn_dt_bias, dn_norm_gain=dn_norm_gain, w_branch=w_branch, w_o=w_o,
             g_ffn=g_ffn, w_up=w_up, ffn_conv_w=ffn_conv_w, w_down=w_down, g_ple=g_ple, w_ple_gate=w_ple_gate, w_ple=w_ple)
    mom = dict(g_mix=m_g_mix, w_in=m_w_in, b_fox_f=m_b_fox_f, fox_q_gain=m_fox_q_gain, fox_k_gain=m_fox_k_gain, sc_conv_w=m_sc_conv_w,
               dn_conv_w=m_dn_conv_w, dn_a_log=m_dn_a_log, dn_dt_bias=m_dn_dt_bias, dn_norm_gain=m_dn_norm_gain, w_branch=m_w_branch,
               w_o=m_w_o, g_ffn=m_g_ffn, w_up=m_w_up, ffn_conv_w=m_ffn_conv_w, w_down=m_w_down, g_ple=m_g_ple, w_ple_gate=m_w_ple_gate,
               w_ple=m_w_ple)
    var = dict(g_mix=v_g_mix, w_in=v_w_in, b_fox_f=v_b_fox_f, fox_q_gain=v_fox_q_gain, fox_k_gain=v_fox_k_gain, sc_conv_w=v_sc_conv_w,
               dn_conv_w=v_dn_conv_w, dn_a_log=v_dn_a_log, dn_dt_bias=v_dn_dt_bias, dn_norm_gain=v_dn_norm_gain, w_branch=v_w_branch,
               w_o=v_w_o, g_ffn=v_g_ffn, w_up=v_w_up, ffn_conv_w=v_ffn_conv_w, w_down=v_w_down, g_ple=v_g_ple, w_ple_gate=v_w_ple_gate,
               w_ple=v_w_ple)
    cx, cy, cc = lax.axis_index("x"), lax.axis_index("y"), lax.axis_index("c")
    chip = 2 * cx + cy
    pos = jnp.stack([cc, chip]).astype(jnp.int32)

    shard_shapes = [a[nm].shape for nm in BIG]
    gathered = gather_weights(pack_rows([a[nm] for nm in BIG], BF16))
    per_chip = [unpack_rows(gathered[k], shard_shapes) for k in range(N_CHIPS)]
    full = {nm: jnp.concatenate([per_chip[k][i] for k in range(N_CHIPS)], axis=BIG_AXIS[nm]) for i, nm in enumerate(BIG)}
    conv_shapes = [a[nm].shape for nm in CONVS]
    conv_all = gather_small("gather_conv_w", pack_rows([a[nm] for nm in CONVS], F32))
    conv_rows = conv_all.shape[0] // 8
    conv_chip = [unpack_rows(conv_all[2 * k * conv_rows:(2 * k + 1) * conv_rows], conv_shapes) for k in range(N_CHIPS)]
    conv = {nm: jnp.concatenate([conv_chip[k][i] for k in range(N_CHIPS)], axis=2) for i, nm in enumerate(CONVS)}

    act = x[0]
    weights, saved = [], []
    for li in range(2):
        weights.append(layer_weights(li, full, conv, a))
        act, sv = layer_fwd(li, act, p[li, 0], weights[li])
        saved.append(sv)
    d_act, loss_part = loss_call(act, loss_target[0])
    loss = lax.psum(loss_part, ("x", "y", "c"))
    layer_grads = [None, None]
    for li in (1, 0):
        d_act, layer_grads[li] = layer_bwd(li, d_act, saved[li], weights[li])
    grad_x = d_act[None]

    def both(nm):
        return jnp.stack([layer_grads[0][nm], layer_grads[1][nm]])

    local = {nm: both(nm) for nm in ("g_mix", "b_fox_f", "fox_q_gain", "fox_k_gain", "dn_norm_gain", "g_ffn", "g_ple", "sc_conv_w",
                                      "dn_conv_w", "ffn_conv_w", "w_branch", "w_o", "w_up", "w_down", "w_ple_gate", "w_ple")}
    local["dn_a_log"] = jnp.stack([layer_grads[li]["ad"][0] for li in range(2)])
    local["dn_dt_bias"] = jnp.stack([layer_grads[li]["ad"][1] for li in range(2)])
    local["w_in"] = jnp.stack([join_w_in(layer_grads[li]["in_main"], layer_grads[li]["in_small"][:, :16]) for li in range(2)])

    gpack = jnp.stack([pack_rows([chip_shard(local[nm], BIG_AXIS[nm], k) for nm in BIG], F32) for k in range(N_CHIPS)])
    from_sibling = swap_sibling("swap_pair", cast_other_half(pos, gpack))
    pair_f32, pair_bf16 = pair_sum(pos, gpack, from_sibling)
    reduced_half = chip_sum(pos, pair_f32, scatter_chips(pair_bf16))
    big_grads = dict(zip(BIG, unpack_rows(share_halves(reduced_half), shard_shapes)))

    small_names = SMALL + CONVS
    small_shapes = [local[nm].shape for nm in small_names]
    small_sum = sum_devices(gather_small("gather_small_grads", pack_rows([local[nm] for nm in small_names], F32)))
    small_grads = dict(zip(small_names, unpack_rows(small_sum, small_shapes)))
    for nm in CONVS:
        width = a[nm].shape[2]
        small_grads[nm] = lax.dynamic_slice_in_dim(small_grads[nm], chip * width, width, axis=2)

    grads = {**big_grads, **small_grads}
    deltas, new_m, new_v = {}, {}, {}
    for nm in WEIGHTS:
        deltas[nm], new_m[nm], new_v[nm] = adam_call(f"adam_{nm}", a[nm], grads[nm], mom[nm], var[nm])
    return (loss, grad_x, *[grads[nm] for nm in WEIGHTS], *[deltas[nm] for nm in WEIGHTS], *[new_m[nm] for nm in WEIGHTS],
            *[new_v[nm] for nm in WEIGHTS])
```

```python
import functools

import jax
import jax.numpy as jnp
from jax import lax
from jax.experimental import pallas as pl
from jax.experimental.pallas import tpu as pltpu

F32 = jnp.float32
BF16 = jnp.bfloat16
HI = lax.Precision.HIGHEST
MESH = pl.DeviceIdType.MESH

D_MODEL = 1024
BRANCH = 512
FOX_DH = 64
DN_DH = 128
DN_HEADS = 4
DN_CHUNK = 64
FOX_BLOCK = 128
D_FF = 2816
EPS = 1e-6
N_CHIPS = 4
LANES = 128

ADAM_LR, ADAM_B1, ADAM_B2, ADAM_EPS, ADAM_WD, ADAM_STEP = 0.001, 0.9, 0.999, 1e-08, 0.01, 10

VMEM_LIMIT = 56 * 1024 * 1024

C_FQ, C_FK, C_FV, C_SB, C_SC, C_SV, C_DN, C_DZ, C_GATE = 0, 512, 1024, 1536, 2048, 2560, 3072, 4608, 5120
IN_MAIN = 8192
IN_SIZES = (1536, 8, 1536, 1536, 4, 4, 512, 3072)

BIG = ("w_in", "w_branch", "w_o", "w_up", "w_down", "w_ple_gate", "w_ple")
BIG_AXIS = {"w_in": 2, "w_branch": 3, "w_o": 1, "w_up": 2, "w_down": 1, "w_ple_gate": 1, "w_ple": 2}
CONVS = ("sc_conv_w", "dn_conv_w", "ffn_conv_w")
SMALL = ("g_mix", "b_fox_f", "fox_q_gain", "fox_k_gain", "dn_a_log", "dn_dt_bias", "dn_norm_gain", "g_ffn", "g_ple")
WEIGHTS = ("g_mix", "w_in", "b_fox_f", "fox_q_gain", "fox_k_gain", "sc_conv_w", "dn_conv_w", "dn_a_log", "dn_dt_bias",
           "dn_norm_gain", "w_branch", "w_o", "g_ffn", "w_up", "ffn_conv_w", "w_down", "g_ple", "w_ple_gate", "w_ple")


def _iota(shape, dim):
    return lax.broadcasted_iota(jnp.int32, shape, dim)


def _dg(a, b, mode, prec=None):
    dims = {"nn": ((1,), (0,)), "nt": ((1,), (1,)), "tn": ((0,), (0,))}[mode]
    return lax.dot_general(a, b, (dims, ((), ())), precision=prec, preferred_element_type=F32)


def _bdot_impl(a, b, mode):
    return _dg(a.astype(BF16), b.astype(BF16), mode)


@functools.partial(jax.custom_vjp, nondiff_argnums=(2,))
def _bdot_diff(a, b, mode):
    return _bdot_impl(a, b, mode)


def _bdot_fwd(a, b, mode):
    return _bdot_impl(a, b, mode), (a, b)


def _bdot_bwd(mode, res, g):
    a, b = res
    if mode == "nn":
        da, db = _bdot_impl(g, b, "nt"), _bdot_impl(a, g, "tn")
    elif mode == "nt":
        da, db = _bdot_impl(g, b, "nn"), _bdot_impl(g, a, "tn")
    else:
        da, db = _bdot_impl(b, g, "nt"), _bdot_impl(a, g, "nn")
    return da.astype(a.dtype), db.astype(b.dtype)


_bdot_diff.defvjp(_bdot_fwd, _bdot_bwd)


def _bdot(d):
    return _bdot_diff if d else _bdot_impl


def _shift_impl(x, k):
    return jnp.where(_iota(x.shape, 0) >= k, pltpu.roll(x, k, 0), 0.0)


def _unshift_impl(g, k):
    n = g.shape[0]
    return jnp.where(_iota(g.shape, 0) < n - k, pltpu.roll(g, n - k, 0), 0.0)


@functools.partial(jax.custom_vjp, nondiff_argnums=(1,))
def _shift_diff(x, k):
    return _shift_impl(x, k)


_shift_diff.defvjp(lambda x, k: (_shift_impl(x, k), None), lambda k, _, g: (_unshift_impl(g, k),))


def _row(w, j):
    return jnp.sum(jnp.where(_iota(w.shape, 0) == j, w, 0.0), axis=0, keepdims=True)


def _col(w, j):
    return jnp.sum(jnp.where(_iota(w.shape, 1) == j, w, 0.0), axis=1, keepdims=True)


def _conv(d, x, w):
    shift = _shift_diff if d else _shift_impl
    taps = w.shape[0]
    y = x * _row(w, taps - 1)
    for j in range(taps - 1):
        y = y + shift(x, taps - 1 - j) * _row(w, j)
    return y


def _softplus(x):
    return jnp.maximum(x, 0.0) + jnp.log(1.0 + jnp.exp(-jnp.abs(x)))


def _silu(x):
    return x * jax.nn.sigmoid(x)


def _rms(x, gain):
    return x * lax.rsqrt(jnp.mean(x * x, axis=-1, keepdims=True) + EPS) * gain


def _rms_fn(d, pids, x, gain):
    return (_rms(x, gain),)


def _loss_fn(d, pids, y, t):
    e = y - t
    part = 0.5 / D_MODEL * jnp.sum(e * e, keepdims=True)
    return e * (1.0 / D_MODEL), jnp.broadcast_to(part, (8, LANES))


def _fox_prep_fn(d, pids, q, k, gq, gk):
    first = _iota(q.shape, 1) < FOX_DH

    def norm(x, gain):
        sq = x * x
        ss_a = jnp.sum(jnp.where(first, sq, 0.0), axis=1, keepdims=True)
        ss_b = jnp.sum(jnp.where(first, 0.0, sq), axis=1, keepdims=True)
        rs = jnp.where(first, lax.rsqrt(ss_a / FOX_DH + EPS), lax.rsqrt(ss_b / FOX_DH + EPS))
        return x * rs * gain

    return norm(q, gq) * FOX_DH ** -0.5, norm(k, gk)


def _fox_gate_fn(d, pids, f, bias):
    logf = -_softplus(-(f + bias))
    n_r, n_c = logf.shape
    tri = (_iota((n_c, n_c), 0) <= _iota((n_c, n_c), 1)).astype(F32)
    within = _dg(logf, tri, "nn", HI)
    tot = jnp.broadcast_to(jnp.sum(logf, axis=1, keepdims=True), logf.shape)
    below = (_iota((n_r, n_r), 1) < _iota((n_r, n_r), 0)).astype(F32)
    return (within + _dg(below, tot, "nn", HI),)


def _fox_attn_fn(d, pids, q, k, v, cq_a, cq_b, ck_a, ck_b):
    dot = _bdot(d)
    first = _iota(q.shape, 1) < FOX_DH
    n_q, n_k = q.shape[0], k.shape[0]
    causal = (pids[1] * n_q + _iota((n_q, n_k), 0)) >= _iota((n_q, n_k), 1)

    def head(qh, cq, ck):
        s = dot(qh, k, "nt") + cq - ck
        s = jnp.where(causal, s, -1e30)
        m = lax.stop_gradient(jnp.max(s, axis=1, keepdims=True))
        e = jnp.exp(s - m)
        pr = e / jnp.sum(e, axis=1, keepdims=True)
        return dot(pr, v, "nn")

    o_a = head(jnp.where(first, q, 0.0), cq_a, ck_a)
    o_b = head(jnp.where(first, 0.0, q), cq_b, ck_b)
    return (jnp.where(first, o_a, o_b),)


def _sconv_fn(d, pids, sb, sc, sv, w):
    return (sb * _conv(d, sc * sv, w),)


def _dnconv_fn(d, pids, x, w):
    return (_silu(_conv(d, x, w)),)


def _merge_fn(d, pids, y0, y1, y2, g0, g1, g2):
    return (jax.nn.sigmoid(g0) * y0 + jax.nn.sigmoid(g1) * y1 + jax.nn.sigmoid(g2) * y2,)


def _ffn_act_fn(d, pids, ug, uv, wg, wv):
    return (_silu(_conv(d, ug, wg)) * _conv(d, uv, wv),)


def _ple_fn(d, pids, gpre, pe, x):
    return (x + jax.nn.sigmoid(gpre) * pe,)


def _adam_fn(d, pids, w, g, m, v):
    m2 = ADAM_B1 * m + (1.0 - ADAM_B1) * g
    v2 = ADAM_B2 * v + (1.0 - ADAM_B2) * (g * g)
    m_hat = m2 / (1.0 - ADAM_B1 ** ADAM_STEP)
    v_hat = v2 / (1.0 - ADAM_B2 ** ADAM_STEP)
    delta = -ADAM_LR * (m_hat / (jnp.sqrt(v_hat) + ADAM_EPS) + ADAM_WD * w)
    return delta, m2, v2


def _tri_inv_impl(a):
    n = a.shape[0]
    r, c = _iota((n, n), 0), _iota((n, n), 1)
    diag_blk = (r >> 4) == (c >> 4)
    eye = (r == c).astype(F32)
    mm = lambda u, w: _dg(u, w, "nn", HI)
    x = jnp.where(diag_blk, -a, 0.0)
    p = eye + x
    x2 = mm(x, x)
    p = p + mm(p, x2)
    x4 = mm(x2, x2)
    p = p + mm(p, x4)
    x8 = mm(x4, x4)
    p = p + mm(p, x8)
    y = -mm(p, jnp.where(diag_blk, 0.0, a))
    q = eye + y
    q = q + mm(q, mm(y, y))
    return mm(q, p)


@jax.custom_vjp
def _tri_inv_diff(a):
    return _tri_inv_impl(a)


def _tri_inv_fwd(a):
    t = _tri_inv_impl(a)
    return t, t


def _tri_inv_bwd(t, g):
    return (-_dg(_dg(t, g, "tn", HI), t, "nt", HI),)


_tri_inv_diff.defvjp(_tri_inv_fwd, _tri_inv_bwd)


def _dn_chunk(d, s_prev, q, k, v, z, a_c, a_r, b_c, a_log, dt_b, gain):
    dot = _bdot(d)
    inv = _tri_inv_diff if d else _tri_inv_impl
    n = q.shape[0]
    r, c = _iota((n, n), 0), _iota((n, n), 1)
    incl, strict = r >= c, r > c
    q = q * lax.rsqrt(jnp.sum(q * q, axis=1, keepdims=True) + EPS) * DN_DH ** -0.5
    k = k * lax.rsqrt(jnp.sum(k * k, axis=1, keepdims=True) + EPS)
    beta = jax.nn.sigmoid(b_c)
    rate = -jnp.exp(a_log)
    g_c = rate * _softplus(a_c + dt_b)
    g_r = rate * _softplus(a_r + dt_b)
    gcum_c = jnp.sum(jnp.where(incl, g_r, 0.0), axis=1, keepdims=True)
    gcum_r = jnp.sum(jnp.where(r <= c, g_c, 0.0), axis=0, keepdims=True)
    decay = jnp.exp(jnp.where(incl, gcum_c - gcum_r, -1e30))
    kb = k * beta
    t = inv(jnp.where(strict, dot(kb, k, "nt") * decay, 0.0))
    e_g = jnp.exp(gcum_c)
    u = _dg(t, v * beta, "nn", HI)
    k_cum = _dg(t, kb * e_g, "nn", HI)
    qk = jnp.where(incl, dot(q, k, "nt") * decay, 0.0)
    g_last = jnp.sum(g_c, axis=0, keepdims=True)
    k_dec = k * jnp.exp(g_last - gcum_c)
    v_new = u - dot(k_cum, s_prev, "nn")
    out = dot(q * e_g, s_prev, "nn") + dot(qk, v_new, "nn")
    s_next = s_prev * jnp.exp(g_last) + dot(k_dec, v_new, "tn")
    return _rms(out, gain) * _silu(z), s_next


def _split_heads(t):
    return [t[:, h * DN_DH:(h + 1) * DN_DH] for h in range(t.shape[1] // DN_DH)]


def _dn_heads(d, s_heads, q_heads, k_heads, v_heads, z_heads, ps, a_rows, ad, gain):
    ys, states = [], []
    for h in range(DN_HEADS):
        y, s = _dn_chunk(d, s_heads[h], q_heads[h], k_heads[h], v_heads[h], z_heads[h], _col(ps, 12 + h), _row(a_rows, h),
                         _col(ps, 8 + h), _col(_row(ad, 0), h), _col(_row(ad, 1), h), gain)
        ys.append(y)
        states.append(s)
    return ys, states


def _cparams(n_axes):
    return pltpu.CompilerParams(dimension_semantics=("arbitrary",) * n_axes, vmem_limit_bytes=VMEM_LIMIT)


def _first_visit(acc_axes):
    cond = None
    for a in acc_axes:
        here = pl.program_id(a) == 0
        cond = here if cond is None else jnp.logical_and(cond, here)
    return cond


def _tile(ref):
    val = ref[...]
    shape = val.shape
    while len(shape) > 2 and shape[0] == 1:
        shape = shape[1:]
    return val.reshape(shape)


def _store(ref, val, first):
    val = val.astype(ref.dtype).reshape(ref.shape)
    if first is None:
        ref[...] = val
        return

    @pl.when(first)
    def _():
        ref[...] = val

    @pl.when(jnp.logical_not(first))
    def _():
        ref[...] += val


def _specs(ops):
    return [pl.BlockSpec(block, imap) for _, block, imap in ops]


def tile_fwd(name, fn, grid, ins, outs):
    n_in = len(ins)

    def body(*refs):
        pids = tuple(pl.program_id(a) for a in range(len(grid)))
        firsts = [_first_visit(o[4]) if o[4] else None for o in outs]
        res = fn(False, pids, *[_tile(r) for r in refs[:n_in]])
        for ref, val, first in zip(refs[n_in:], res, firsts):
            _store(ref, val, first)

    out = pl.pallas_call(
        body, grid=grid, in_specs=_specs(ins),
        out_specs=[pl.BlockSpec(o[2], o[3]) for o in outs],
        out_shape=[jax.ShapeDtypeStruct(o[0], o[1]) for o in outs],
        name=name, compiler_params=_cparams(len(grid)),
    )(*[a for a, _, _ in ins])
    return out


def tile_bwd(name, fn, grid, ins, cots, diff, adds=None):
    adds = adds or {}
    n_in, n_cot = len(ins), len(cots)
    add_pos = sorted(adds)
    diff_idx = [d[0] for d in diff]
    out_desc = [d[2] if len(d) > 2 and d[2] is not None else (ins[d[0]][0].shape, ins[d[0]][1], ins[d[0]][2]) for d in diff]
    out_dtypes = [d[3] if len(d) > 3 else F32 for d in diff]

    def body(*refs):
        pids = tuple(pl.program_id(a) for a in range(len(grid)))
        firsts = [_first_visit(d[1]) if d[1] else None for d in diff]
        vals = [_tile(r) for r in refs[:n_in]]
        cot_vals = [_tile(r) for r in refs[n_in:n_in + n_cot]]
        add_vals = [_tile(r) for r in refs[n_in + n_cot:n_in + n_cot + len(add_pos)]]
        out_refs = refs[n_in + n_cot + len(add_pos):]

        def f(*dv):
            full = list(vals)
            for i, val in zip(diff_idx, dv):
                full[i] = val
            return fn(True, pids, *full)

        prim, vjp = jax.vjp(f, *[vals[i].astype(F32) for i in diff_idx])
        grads = list(vjp(tuple(c.astype(o.dtype) for c, o in zip(cot_vals, prim))))
        for pos, val in zip(add_pos, add_vals):
            grads[pos] = grads[pos] + val.astype(F32)
        for ref, val, first in zip(out_refs, grads, firsts):
            _store(ref, val, first)

    all_ins = list(ins) + list(cots) + [adds[p] for p in add_pos]
    out = pl.pallas_call(
        body, grid=grid, in_specs=_specs(all_ins),
        out_specs=[pl.BlockSpec(o[1], o[2]) for o in out_desc],
        out_shape=[jax.ShapeDtypeStruct(o[0], dt) for o, dt in zip(out_desc, out_dtypes)],
        name=name, compiler_params=_cparams(len(grid)),
    )(*[a for a, _, _ in all_ins])
    return out


def _pick(dim, cands):
    for c in cands:
        if dim % c == 0:
            return c
    return dim


MM_TILES = (1024, 512, 256, 128)


def mm(name, a, b, mode, add=None, out_dtype=F32, blocks=None):
    wide = None
    if mode == "nn":
        (m, kk), n = a.shape, b.shape[-1]
    elif mode == "nt":
        (m, kk), n = a.shape, b.shape[-2]
    else:
        (kk, m), n = a.shape, b.shape[1]
    if blocks is not None:
        lo, n_blk = blocks
        wide = b.shape[-1] if mode != "tn" else n // n_blk
        if mode == "nn":
            n = wide * n_blk
    tm = _pick(m, MM_TILES)
    if mode == "nt" and blocks is not None:
        tn, tk = _pick(n, MM_TILES), _pick(wide, MM_TILES[:-1])
    elif blocks is not None:
        tn, tk = _pick(wide, MM_TILES[:-1]), _pick(kk, MM_TILES)
    else:
        tn, tk = _pick(n, MM_TILES), _pick(kk, MM_TILES)
    nk = kk // tk
    a_spec = pl.BlockSpec((tk, tm), lambda i, j, k: (k, i)) if mode == "tn" else pl.BlockSpec((tm, tk), lambda i, j, k: (i, k))
    o_spec = pl.BlockSpec((tm, tn), lambda i, j, k: (i, j))
    out_shape = (m, n)
    if blocks is None:
        b_spec = pl.BlockSpec((tn, tk), lambda i, j, k: (j, k)) if mode == "nt" else pl.BlockSpec((tk, tn), lambda i, j, k: (k, j))
    elif mode == "nn":
        per = wide // tn
        b_spec = pl.BlockSpec((1, tk, tn), lambda i, j, k: (lo + j // per, k, j % per))
    elif mode == "nt":
        per = wide // tk
        b_spec = pl.BlockSpec((1, tn, tk), lambda i, j, k: (lo + k // per, j, k % per))
    else:
        per = wide // tn
        b_spec = pl.BlockSpec((tk, tn), lambda i, j, k: (k, j))
        o_spec = pl.BlockSpec((1, tm, tn), lambda i, j, k: (j // per, i, j % per))
        out_shape = (n_blk, m, wide)

    def body(*refs):
        a_ref, b_ref = refs[0], refs[1]
        add_ref = refs[2] if add is not None else None
        o_ref, acc = refs[-2], refs[-1]
        k = pl.program_id(2)
        part = _bdot_impl(_tile(a_ref), _tile(b_ref), mode)

        @pl.when(k == 0)
        def _():
            acc[...] = part

        @pl.when(k > 0)
        def _():
            acc[...] += part

        @pl.when(k == nk - 1)
        def _():
            res = acc[...]
            if add_ref is not None:
                res = res + add_ref[...]
            o_ref[...] = res.astype(o_ref.dtype).reshape(o_ref.shape)

    operands = [a, b] + ([add] if add is not None else [])
    in_specs = [a_spec, b_spec] + ([o_spec] if add is not None else [])
    return pl.pallas_call(
        body, grid=(m // tm, n // tn, nk), in_specs=in_specs, out_specs=o_spec,
        out_shape=jax.ShapeDtypeStruct(out_shape, out_dtype),
        scratch_shapes=[pltpu.VMEM((tm, tn), F32)],
        name=name, compiler_params=_cparams(3),
    )(*operands)


def _rows(x, width=None, off=0, tm=256):
    width = x.shape[1] if width is None else width
    return (x, (tm, width), lambda i, off=off: (i, off))


def _whole(x):
    nd = x.ndim
    return (x, x.shape, lambda *pids, nd=nd: (0,) * nd)


def _rms_ops(x, gain):
    return [_rows(x), _whole(gain)]


def rms_fwd(name, x, gain):
    s, dm = x.shape
    return tile_fwd(name, _rms_fn, (s // 256,), _rms_ops(x, gain), [((s, dm), BF16, (256, dm), lambda i: (i, 0), ())])[0]


def rms_bwd(name, x, gain, dh, dres):
    s = x.shape[0]
    return tile_bwd(name, _rms_fn, (s // 256,), _rms_ops(x, gain), [_rows(dh)], [(0, ()), (1, (0,))], adds={0: _rows(dres)})


def loss_call(y, t):
    s, dm = y.shape
    dy, part = tile_fwd("loss", _loss_fn, (s // 256,), [_rows(y), _rows(t)],
                        [((s, dm), F32, (256, dm), lambda i: (i, 0), ()), ((8, LANES), F32, (8, LANES), lambda i: (0, 0), (0,))])
    return dy, part[0, 0]


def _fox_prep_ops(pm, gq, gk):
    tm = 512
    return [(pm, (tm, LANES), lambda i, j: (i, C_FQ // LANES + j)), (pm, (tm, LANES), lambda i, j: (i, C_FK // LANES + j)),
            _whole(gq), _whole(gk)]


def fox_prep_fwd(name, pm, gq, gk):
    s = pm.shape[0]
    out = ((s, BRANCH), BF16, (512, LANES), lambda i, j: (i, j), ())
    return tile_fwd(name, _fox_prep_fn, (s // 512, 4), _fox_prep_ops(pm, gq, gk), [out, out])


def fox_prep_bwd(name, pm, gq, gk, dqn, dkn):
    s = pm.shape[0]
    cot = lambda g: (g, (512, LANES), lambda i, j: (i, j))
    own = ((s, BRANCH), (512, LANES), lambda i, j: (i, j))
    return tile_bwd(name, _fox_prep_fn, (s // 512, 4), _fox_prep_ops(pm, gq, gk), [cot(dqn), cot(dkn)],
                    [(0, (), own, BF16), (1, (), own, BF16), (2, (0, 1)), (3, (0, 1))])


def _fox_gate_ops(f_t, bias):
    return [(f_t, (1,) + f_t.shape[1:], lambda h: (h, 0, 0)), (bias, (1, 1, 1), lambda h: (h, 0, 0))]


def fox_gate_fwd(name, f_t, bias):
    n_h = f_t.shape[0]
    return tile_fwd(name, _fox_gate_fn, (n_h,), _fox_gate_ops(f_t, bias),
                    [(f_t.shape, F32, (1,) + f_t.shape[1:], lambda h: (h, 0, 0), ())])[0]


def fox_gate_bwd(name, f_t, bias, dcum):
    n_h = f_t.shape[0]
    return tile_bwd(name, _fox_gate_fn, (n_h,), _fox_gate_ops(f_t, bias),
                    [(dcum, (1,) + f_t.shape[1:], lambda h: (h, 0, 0))], [(0, ()), (1, ())])


def _fox_attn_ops(qn, kn, pm, cum_c, cum_r):
    s = qn.shape[0]
    nb = FOX_BLOCK
    return [(qn, (nb, LANES), lambda p, i: (i, p)), (kn, (s, LANES), lambda p, i: (0, p)),
            (pm, (s, LANES), lambda p, i: (0, C_FV // LANES + p)),
            (cum_c, (1, nb, 1), lambda p, i: (2 * p, i, 0)), (cum_c, (1, nb, 1), lambda p, i: (2 * p + 1, i, 0)),
            (cum_r, (1, 1, s), lambda p, i: (2 * p, 0, 0)), (cum_r, (1, 1, s), lambda p, i: (2 * p + 1, 0, 0))]


def fox_attn_fwd(name, qn, kn, pm, cum_c, cum_r):
    s = qn.shape[0]
    return tile_fwd(name, _fox_attn_fn, (4, s // FOX_BLOCK), _fox_attn_ops(qn, kn, pm, cum_c, cum_r),
                    [((s, BRANCH), BF16, (FOX_BLOCK, LANES), lambda p, i: (i, p), ())])[0]


def fox_attn_bwd(name, qn, kn, pm, cum_c, cum_r, dy):
    s = qn.shape[0]
    pair_c = ((4, s, 1), (1, FOX_BLOCK, 1), lambda p, i: (p, i, 0))
    pair_r = ((4, 1, s), (1, 1, s), lambda p, i: (p, 0, 0))
    d_qn, d_kn, d_v, d_cqa, d_cqb, d_cka, d_ckb = tile_bwd(
        name, _fox_attn_fn, (4, s // FOX_BLOCK), _fox_attn_ops(qn, kn, pm, cum_c, cum_r),
        [(dy, (FOX_BLOCK, LANES), lambda p, i: (i, p))],
        [(0, ()), (1, (1,)), (2, (1,), ((s, BRANCH), (s, LANES), lambda p, i: (0, p))),
         (3, (), pair_c), (4, (), pair_c), (5, (1,), pair_r), (6, (1,), pair_r)])
    d_cum = jnp.stack([d_cqa[:, :, 0] + d_cka[:, 0, :], d_cqb[:, :, 0] + d_ckb[:, 0, :]], axis=1).reshape(8, s)
    return d_qn, d_kn, d_v, d_cum


def sconv_ops(pm, w):
    s = pm.shape[0]
    blk = lambda c0: (pm, (s, LANES), lambda j, c0=c0: (0, c0 // LANES + j))
    return [blk(C_SB), blk(C_SC), blk(C_SV), (w, (w.shape[0], LANES), lambda j: (0, j))]


def dnconv_ops(pm, w):
    s = pm.shape[0]
    return [(pm, (s, LANES), lambda j: (0, C_DN // LANES + j)), (w, (w.shape[0], LANES), lambda j: (0, j))]


def ffn_ops(ug, uv, w):
    s = ug.shape[0]
    n_t = D_FF // LANES
    return [(ug, (s, LANES), lambda j: (0, j)), (uv, (s, LANES), lambda j: (0, j)),
            (w, (w.shape[0], LANES), lambda j: (0, j)), (w, (w.shape[0], LANES), lambda j: (0, n_t + j))]


def _col_out(s, width, dtype=F32):
    return ((s, width), dtype, (s, LANES), lambda j: (0, j), ())


def _col_cot(g):
    return (g, (g.shape[0], LANES), lambda j: (0, j))


def merge_ops(yp, pm):
    gate = lambda b: (pm, (256, D_MODEL), lambda i, b=b: (i, C_GATE // D_MODEL + b))
    return [_rows(yp[0]), _rows(yp[1]), _rows(yp[2]), gate(0), gate(1), gate(2)]


def ple_ops(gpre, pe, x):
    return [_rows(gpre), _rows(pe), _rows(x)]


def adam_call(name, w, g, m, v):
    shape = w.shape
    last = shape[-1]
    rows = w.size // last
    flat = lambda t: t.reshape(rows, last)
    tm = rows
    for cand in (512, 256, 128, 64, 32, 16, 8):
        if rows % cand == 0 and cand * last * 4 <= 2 * 1024 * 1024:
            tm = cand
            break
    spec = lambda t: (flat(t), (tm, last), lambda i: (i, 0))
    out = ((rows, last), F32, (tm, last), lambda i: (i, 0), ())
    res = tile_fwd(name, _adam_fn, (rows // tm,), [spec(w), spec(g), spec(m), spec(v)], [out, out, out])
    return [r.reshape(shape) for r in res]


def _adam_layers_fn(d, pids, w, m, v, g0, g1):
    g = jnp.where(pids[0] == 0, g0, g1)
    return (g,) + _adam_fn(d, pids, w, g, m, v)


def adam_layers(name, w, m, v, g0, g1):
    _, rows, cols = w.shape
    tm = _row_tile(rows, cols)
    n_t = rows // tm
    lay = lambda t: (t, (1, tm, cols), lambda l, i: (l, i, 0))
    ins = [lay(w), lay(m), lay(v), (g0, (tm, cols), lambda l, i: (i * (1 - l) + (n_t - 1) * l, 0)), (g1, (tm, cols), lambda l, i: (i * l, 0))]
    out = (w.shape, F32, (1, tm, cols), lambda l, i: (l, i, 0), ())
    return tile_fwd(name, _adam_layers_fn, (2, n_t), ins, [out, out, out, out])


def dn_fwd(name, dn_act, pm, ps, a_rows, ad, gain):
    s = dn_act.shape[0]
    n_c = s // DN_CHUNK

    def body(qkv_ref, z_ref, ps_ref, ar_ref, ad_ref, g_ref, y_ref, hist_ref, state):
        @pl.when(pl.program_id(0) == 0)
        def _():
            state[...] = jnp.zeros_like(state)

        hist_ref[0] = state[...]
        heads = _split_heads(qkv_ref[...])
        ys, s_next = _dn_heads(False, [state[h] for h in range(DN_HEADS)], heads[0:4], heads[4:8], heads[8:12],
                               _split_heads(z_ref[...]), ps_ref[...], ar_ref[0], ad_ref[...], g_ref[...])
        y_ref[...] = jnp.concatenate(ys, axis=1).astype(y_ref.dtype)
        for h in range(DN_HEADS):
            state[h] = s_next[h]

    return pl.pallas_call(
        body, grid=(n_c,),
        in_specs=[pl.BlockSpec((DN_CHUNK, 3 * BRANCH), lambda j: (j, 0)),
                  pl.BlockSpec((DN_CHUNK, BRANCH), lambda j: (j, C_DZ // BRANCH)),
                  pl.BlockSpec((DN_CHUNK, LANES), lambda j: (j, 0)),
                  pl.BlockSpec((1, DN_HEADS, DN_CHUNK), lambda j: (j, 0, 0)),
                  pl.BlockSpec((2, DN_HEADS), lambda j: (0, 0)),
                  pl.BlockSpec((1, DN_DH), lambda j: (0, 0))],
        out_specs=[pl.BlockSpec((DN_CHUNK, BRANCH), lambda j: (j, 0)),
                   pl.BlockSpec((1, DN_HEADS, DN_DH, DN_DH), lambda j: (j, 0, 0, 0))],
        out_shape=[jax.ShapeDtypeStruct((s, BRANCH), BF16), jax.ShapeDtypeStruct((n_c, DN_HEADS, DN_DH, DN_DH), F32)],
        scratch_shapes=[pltpu.VMEM((DN_HEADS, DN_DH, DN_DH), F32)],
        name=name, compiler_params=_cparams(1),
    )(dn_act, pm, ps, a_rows, ad, gain)


def dn_bwd(name, dn_act, pm, ps, a_rows, ad, gain, hist, dy):
    s = dn_act.shape[0]
    n_c = s // DN_CHUNK

    def body(qkv_ref, z_ref, ps_ref, ar_ref, ad_ref, g_ref, hist_ref, dy_ref,
             dqkv_ref, dz_ref, dps_ref, dar_ref, dad_ref, dg_ref, d_state):
        first = pl.program_id(0) == 0

        @pl.when(first)
        def _():
            d_state[...] = jnp.zeros_like(d_state)

        def f(*args):
            return _dn_heads(True, *args)

        heads = _split_heads(qkv_ref[...])
        _, vjp = jax.vjp(f, [hist_ref[0, h] for h in range(DN_HEADS)], heads[0:4], heads[4:8], heads[8:12],
                         _split_heads(z_ref[...]), ps_ref[...], ar_ref[0], ad_ref[...], g_ref[...])
        d_s, d_q, d_k, d_v, d_z, d_ps, d_ar, d_ad, d_gain = vjp((_split_heads(dy_ref[...]), [d_state[h] for h in range(DN_HEADS)]))
        for h in range(DN_HEADS):
            d_state[h] = d_s[h]
        dqkv_ref[...] = jnp.concatenate(list(d_q) + list(d_k) + list(d_v), axis=1)
        dz_ref[...] = jnp.concatenate(list(d_z), axis=1).astype(dz_ref.dtype)
        dps_ref[...] = d_ps
        dar_ref[0] = d_ar
        _store(dad_ref, d_ad, first)
        _store(dg_ref, d_gain, first)

    rev = lambda j: n_c - 1 - j
    return pl.pallas_call(
        body, grid=(n_c,),
        in_specs=[pl.BlockSpec((DN_CHUNK, 3 * BRANCH), lambda j: (rev(j), 0)),
                  pl.BlockSpec((DN_CHUNK, BRANCH), lambda j: (rev(j), C_DZ // BRANCH)),
                  pl.BlockSpec((DN_CHUNK, LANES), lambda j: (rev(j), 0)),
                  pl.BlockSpec((1, DN_HEADS, DN_CHUNK), lambda j: (rev(j), 0, 0)),
                  pl.BlockSpec((2, DN_HEADS), lambda j: (0, 0)),
                  pl.BlockSpec((1, DN_DH), lambda j: (0, 0)),
                  pl.BlockSpec((1, DN_HEADS, DN_DH, DN_DH), lambda j: (rev(j), 0, 0, 0)),
                  pl.BlockSpec((DN_CHUNK, BRANCH), lambda j: (rev(j), 0))],
        out_specs=[pl.BlockSpec((DN_CHUNK, 3 * BRANCH), lambda j: (rev(j), 0)),
                   pl.BlockSpec((DN_CHUNK, BRANCH), lambda j: (rev(j), 0)),
                   pl.BlockSpec((DN_CHUNK, LANES), lambda j: (rev(j), 0)),
                   pl.BlockSpec((1, DN_HEADS, DN_CHUNK), lambda j: (rev(j), 0, 0)),
                   pl.BlockSpec((2, DN_HEADS), lambda j: (0, 0)),
                   pl.BlockSpec((1, DN_DH), lambda j: (0, 0))],
        out_shape=[jax.ShapeDtypeStruct((s, 3 * BRANCH), F32), jax.ShapeDtypeStruct((s, BRANCH), BF16),
                   jax.ShapeDtypeStruct((s, LANES), F32), jax.ShapeDtypeStruct((n_c, DN_HEADS, DN_CHUNK), F32),
                   jax.ShapeDtypeStruct((2, DN_HEADS), F32), jax.ShapeDtypeStruct((1, DN_DH), F32)],
        scratch_shapes=[pltpu.VMEM((DN_HEADS, DN_DH, DN_DH), F32)],
        name=name, compiler_params=_cparams(1),
    )(dn_act, pm, ps, a_rows, ad, gain, hist, dy)


def _seq_layouts(cols, s):
    return cols.T.reshape(cols.shape[1], s // LANES, LANES)


def layer_fwd(li, x, p, w):
    s = x.shape[0]
    n = lambda t: f"{t}_l{li}"
    h = rms_fwd(n("rms_mix"), x, w["g_mix"])
    pm = mm(n("in_main"), h, w["in_main"], "nn")
    ps = mm(n("in_small"), h, w["in_small"], "nn")
    qn, kn = fox_prep_fwd(n("fox_prep"), pm, w["gq"], w["gk"])
    f_t = _seq_layouts(ps[:, 0:8], s)
    cum = fox_gate_fwd(n("fox_gate"), f_t, w["b_f"])
    cum_c, cum_r = cum.reshape(8, s, 1), cum.reshape(8, 1, s)
    y_fox = fox_attn_fwd(n("fox_attn"), qn, kn, pm, cum_c, cum_r)
    y_sc = tile_fwd(n("sconv"), _sconv_fn, (BRANCH // LANES,), sconv_ops(pm, w["sc_conv_w"]), [_col_out(s, BRANCH, BF16)])[0]
    dn_act = tile_fwd(n("dnconv"), _dnconv_fn, (3 * BRANCH // LANES,), dnconv_ops(pm, w["dn_conv_w"]), [_col_out(s, 3 * BRANCH)])[0]
    a_rows = ps[:, 12:16].reshape(s // DN_CHUNK, DN_CHUNK, DN_HEADS).transpose(0, 2, 1)
    y_dn, hist = dn_fwd(n("dn_fwd"), dn_act, pm, ps, a_rows, w["ad"], w["dn_gain"])
    ys = (y_fox, y_sc, y_dn)
    yp = [mm(n(f"branch{b}"), ys[b], w["branch"][b], "nn", blocks=(0, N_CHIPS)) for b in range(3)]
    merged = tile_fwd(n("merge"), _merge_fn, (s // 256,), merge_ops(yp, pm), [((s, D_MODEL), BF16, (256, D_MODEL), lambda i: (i, 0), ())])[0]
    x1 = mm(n("w_o"), merged, w["o"], "nn", add=x)
    h2 = rms_fwd(n("rms_ffn"), x1, w["g_ffn"])
    ug = mm(n("up_g"), h2, w["up"], "nn", blocks=(0, 2))
    uv = mm(n("up_v"), h2, w["up"], "nn", blocks=(2, 2))
    act = tile_fwd(n("ffn_act"), _ffn_act_fn, (D_FF // LANES,), ffn_ops(ug, uv, w["ffn_conv_w"]), [_col_out(s, D_FF, BF16)])[0]
    x2 = mm(n("down"), act, w["down"], "nn", add=x1)
    h3 = rms_fwd(n("rms_ple"), x2, w["g_ple"])
    gpre = mm(n("ple_gate"), h3, w["pg"], "nn")
    pe = mm(n("ple_emb"), p, w["ple"], "nn", blocks=(0, N_CHIPS))
    x3 = tile_fwd(n("ple"), _ple_fn, (s // 256,), ple_ops(gpre, pe, x2), [((s, D_MODEL), F32, (256, D_MODEL), lambda i: (i, 0), ())])[0]
    saved = dict(x=x, h=h, pm=pm, ps=ps, qn=qn, kn=kn, f_t=f_t, cum_c=cum_c, cum_r=cum_r, ys=ys, dn_act=dn_act,
                 a_rows=a_rows, hist=hist, yp=yp, merged=merged, x1=x1, h2=h2, ug=ug, uv=uv, act=act, x2=x2, h3=h3,
                 gpre=gpre, pe=pe, p=p)
    return x3, saved


def layer_bwd(li, dx3, sv, w):
    s = dx3.shape[0]
    n = lambda t: f"{t}_l{li}"
    g = {}
    col_own = lambda width: ((s, width), (s, LANES), lambda j: (0, j))
    d_gpre, d_pe = tile_bwd(n("ple_bwd"), _ple_fn, (s // 256,), ple_ops(sv["gpre"], sv["pe"], sv["x2"]), [_rows(dx3)],
                            [(0, (), None, BF16), (1, (), None, BF16)])
    g["w_ple"] = mm(n("d_w_ple"), sv["p"], d_pe, "tn", blocks=(0, N_CHIPS))
    g["w_ple_gate"] = mm(n("d_w_pg"), sv["h3"], d_gpre, "tn").reshape(N_CHIPS, -1, D_MODEL)
    dh3 = mm(n("d_h3"), d_gpre, w["pg"], "nt")
    dx2, d_g_ple = rms_bwd(n("rms_ple_bwd"), sv["x2"], w["g_ple"], dh3, dx3)
    dact = mm(n("d_act"), dx2, w["down"], "nt")
    g["w_down"] = mm(n("d_w_down"), sv["act"], dx2, "tn").reshape(N_CHIPS, -1, D_MODEL)
    taps_own = ((w["ffn_conv_w"].shape[0], D_FF), (w["ffn_conv_w"].shape[0], LANES), lambda j: (0, j))
    d_ug, d_uv, d_fw_g, d_fw_v = tile_bwd(n("ffn_act_bwd"), _ffn_act_fn, (D_FF // LANES,), ffn_ops(sv["ug"], sv["uv"], w["ffn_conv_w"]),
                                          [_col_cot(dact)], [(0, (), None, BF16), (1, (), None, BF16), (2, (), taps_own), (3, (), taps_own)])
    g["ffn_conv_w"] = jnp.concatenate([d_fw_g, d_fw_v], axis=1)
    g["w_up"] = jnp.concatenate([mm(n("d_w_up_g"), sv["h2"], d_ug, "tn", blocks=(0, 2)), mm(n("d_w_up_v"), sv["h2"], d_uv, "tn", blocks=(0, 2))])
    dh2 = mm(n("d_h2_v"), d_uv, w["up"], "nt", blocks=(2, 2), add=mm(n("d_h2_g"), d_ug, w["up"], "nt", blocks=(0, 2)))
    dx1, d_g_ffn = rms_bwd(n("rms_ffn_bwd"), sv["x1"], w["g_ffn"], dh2, dx2)
    dmerged = mm(n("d_merged"), dx1, w["o"], "nt")
    g["w_o"] = mm(n("d_w_o"), sv["merged"], dx1, "tn").reshape(N_CHIPS, -1, D_MODEL)
    gate_own = ((s, D_MODEL), (256, D_MODEL), lambda i: (i, 0))
    d_yp0, d_yp1, d_yp2, d_g0, d_g1, d_g2 = tile_bwd(
        n("merge_bwd"), _merge_fn, (s // 256,), merge_ops(sv["yp"], sv["pm"]), [_rows(dmerged)],
        [(0, (), None, BF16), (1, (), None, BF16), (2, (), None, BF16), (3, (), gate_own, BF16), (4, (), gate_own, BF16), (5, (), gate_own, BF16)])
    d_yp = (d_yp0, d_yp1, d_yp2)
    g["w_branch"] = jnp.concatenate([mm(n(f"d_w_branch{b}"), sv["ys"][b], d_yp[b], "tn", blocks=(0, N_CHIPS)) for b in range(3)], axis=1)
    d_ys = [mm(n(f"d_y{b}"), d_yp[b], w["branch"][b], "nt", blocks=(0, N_CHIPS)) for b in range(3)]
    d_dnact, d_z, d_ps_dn, d_arows, d_ad, d_dngain = dn_bwd(n("dn_bwd"), sv["dn_act"], sv["pm"], sv["ps"], sv["a_rows"], w["ad"], w["dn_gain"],
                                                            sv["hist"], d_ys[2])
    g["ad"], g["dn_norm_gain"] = d_ad, d_dngain[0]
    d_dnqkv, g["dn_conv_w"] = tile_bwd(n("dnconv_bwd"), _dnconv_fn, (3 * BRANCH // LANES,), dnconv_ops(sv["pm"], w["dn_conv_w"]),
                                       [_col_cot(d_dnact)], [(0, (), col_own(3 * BRANCH), BF16), (1, ())])
    d_sb, d_sc, d_sv, g["sc_conv_w"] = tile_bwd(n("sconv_bwd"), _sconv_fn, (BRANCH // LANES,), sconv_ops(sv["pm"], w["sc_conv_w"]), [_col_cot(d_ys[1])],
                                                [(0, (), col_own(BRANCH), BF16), (1, (), col_own(BRANCH), BF16), (2, (), col_own(BRANCH), BF16), (3, ())])
    d_qn, d_kn, d_fv, d_cum = fox_attn_bwd(n("fox_attn_bwd"), sv["qn"], sv["kn"], sv["pm"], sv["cum_c"], sv["cum_r"], d_ys[0])
    d_ft, d_bf = fox_gate_bwd(n("fox_gate_bwd"), sv["f_t"], w["b_f"], d_cum.reshape(8, s // LANES, LANES))
    g["b_fox_f"] = d_bf.reshape(8)
    d_fq, d_fk, d_gq, d_gk = fox_prep_bwd(n("fox_prep_bwd"), sv["pm"], w["gq"], w["gk"], d_qn, d_kn)
    g["fox_q_gain"] = d_gq[0, :FOX_DH] + d_gq[0, FOX_DH:]
    g["fox_k_gain"] = d_gk[0, :FOX_DH] + d_gk[0, FOX_DH:]
    d_pm = jnp.concatenate([d_fq, d_fk, d_fv.astype(BF16), d_sb, d_sc, d_sv, d_dnqkv, d_z, d_g0, d_g1, d_g2], axis=1)
    d_a_cols = d_arows.transpose(0, 2, 1).reshape(s, DN_HEADS)
    d_f_cols = d_ft.reshape(8, s).T
    d_ps = d_ps_dn + jnp.concatenate([d_f_cols, jnp.zeros((s, 4), F32), d_a_cols, jnp.zeros((s, LANES - 16), F32)], axis=1)
    g["w_in"] = chip_blocks_w_in(mm(n("d_w_in_main"), sv["h"], d_pm, "tn"), mm(n("d_w_in_small"), sv["h"], d_ps, "tn"))
    dh = mm(n("d_h_small"), d_ps, w["in_small"], "nt", add=mm(n("d_h_main"), d_pm, w["in_main"], "nt"))
    dx, d_g_mix = rms_bwd(n("rms_mix_bwd"), sv["x"], w["g_mix"], dh, dx1)
    g["g_mix"], g["g_ffn"], g["g_ple"] = d_g_mix[0], d_g_ffn[0], d_g_ple[0]
    return dx, g


IN_SHARD = 2052
MAIN_RANGES = ((0, 1536), (1544, 3080), (3080, 4616), (4624, 5136), (5136, 8208))
SMALL_RANGES = ((1536, 1544), (4616, 4620), (4620, 4624))


def _from_chip_blocks(blocks, ranges):
    parts = []
    for lo, hi in ranges:
        for k in range(N_CHIPS):
            a0, a1 = max(lo, k * IN_SHARD), min(hi, (k + 1) * IN_SHARD)
            if a0 < a1:
                parts.append(blocks[k][:, a0 - k * IN_SHARD:a1 - k * IN_SHARD])
    return parts


def split_w_in(blocks):
    main = jnp.concatenate(_from_chip_blocks(blocks, MAIN_RANGES), axis=1)
    pad = jnp.zeros((blocks.shape[1], LANES - 16), blocks.dtype)
    return main, jnp.concatenate(_from_chip_blocks(blocks, SMALL_RANGES) + [pad], axis=1)


def chip_blocks_w_in(main, small):
    pieces, m_off, s_off = [], 0, 0
    ranges = sorted([(lo, hi, "m") for lo, hi in MAIN_RANGES] + [(lo, hi, "s") for lo, hi in SMALL_RANGES])
    offs = {}
    for lo, hi in MAIN_RANGES:
        offs[lo] = m_off
        m_off += hi - lo
    for lo, hi in SMALL_RANGES:
        offs[lo] = s_off
        s_off += hi - lo
    blocks = []
    for k in range(N_CHIPS):
        parts = []
        for lo, hi, src in ranges:
            a0, a1 = max(lo, k * IN_SHARD), min(hi, (k + 1) * IN_SHARD)
            if a0 < a1:
                arr = main if src == "m" else small
                parts.append(arr[:, offs[lo] + a0 - lo:offs[lo] + a1 - lo])
        blocks.append(jnp.concatenate(parts, axis=1))
    return jnp.stack(blocks)


def layer_weights(li, got, conv, a):
    g_in, g_branch, g_o, g_up, g_down, g_pg, g_ple = got
    main, small = split_w_in(g_in)
    tile2 = lambda v: jnp.concatenate([v, v])[None, :]
    branch = g_branch.reshape(N_CHIPS, 3, BRANCH, -1)
    return dict(
        in_main=main, in_small=small, branch=[branch[:, b] for b in range(3)], o=g_o.reshape(D_MODEL, D_MODEL), up=g_up,
        down=g_down.reshape(D_FF, D_MODEL), pg=g_pg.reshape(D_MODEL, D_MODEL), ple=g_ple,
        g_mix=a["g_mix"][li][None, :], g_ffn=a["g_ffn"][li][None, :], g_ple=a["g_ple"][li][None, :],
        gq=tile2(a["fox_q_gain"][li]), gk=tile2(a["fox_k_gain"][li]), b_f=a["b_fox_f"][li].reshape(8, 1, 1),
        ad=jnp.stack([a["dn_a_log"][li], a["dn_dt_bias"][li]]), dn_gain=a["dn_norm_gain"][li][None, :],
        sc_conv_w=conv["sc_conv_w"][li], dn_conv_w=conv["dn_conv_w"][li], ffn_conv_w=conv["ffn_conv_w"][li])


def pack_rows(arrs, dtype):
    flat = jnp.concatenate([t.reshape(-1).astype(dtype) for t in arrs])
    pad = (-flat.shape[0]) % (8 * LANES)
    if pad:
        flat = jnp.concatenate([flat, jnp.zeros((pad,), dtype)])
    return flat.reshape(-1, LANES)


def unpack_rows(buf, shapes):
    flat = buf.reshape(-1)
    out, off = [], 0
    for shp in shapes:
        size = 1
        for dim in shp:
            size *= dim
        out.append(flat[off:off + size].reshape(shp))
        off += size
    return out


def chip_shard(t, axis, k):
    width = t.shape[axis] // N_CHIPS
    return lax.slice_in_dim(t, k * width, (k + 1) * width, axis=axis)


ANY = pl.BlockSpec(memory_space=pl.ANY)


def _position():
    x, y, c = lax.axis_index("x"), lax.axis_index("y"), lax.axis_index("c")
    return x, y, c, [(1 - x, y), (x, 1 - y), (1 - x, 1 - y)]


def gather_small(name, block):
    m_per, n = block.shape

    def body(x_ref, out_ref, send_sems, recv_sems, local_sem):
        x, y, c, chips = _position()
        me, sibling = (x, y, c), (x, y, 1 - c)

        def rows(px, py, pc):
            return out_ref.at[pl.ds((4 * px + 2 * py + pc) * m_per, m_per), :]

        def copy(k, blk, to, src=None):
            return pltpu.make_async_remote_copy(src_ref=rows(*blk) if src is None else src, dst_ref=rows(*blk),
                                                send_sem=send_sems.at[k], recv_sem=recv_sems.at[k], device_id=to, device_id_type=MESH)

        mine = pltpu.make_async_copy(x_ref, rows(*me), local_sem)
        mine.start()
        first = [copy(0, me, sibling, src=x_ref)] + [copy(1 + j, me, (*chip, c), src=x_ref) for j, chip in enumerate(chips)]
        for cp in first:
            cp.start()
        passed = [copy(4 + j, (*chip, c), sibling) for j, chip in enumerate(chips)]
        for j, chip in enumerate(chips):
            copy(1 + j, (*chip, c), me).wait_recv()
            passed[j].start()
        copy(0, sibling, me).wait_recv()
        for j, chip in enumerate(chips):
            copy(4 + j, (*chip, 1 - c), me).wait_recv()
        for cp in first + passed:
            cp.wait_send()
        mine.wait()

    return pl.pallas_call(
        body, out_shape=jax.ShapeDtypeStruct((8 * m_per, n), block.dtype),
        in_specs=[pl.BlockSpec(memory_space=pltpu.VMEM)], out_specs=pl.BlockSpec(memory_space=pltpu.VMEM),
        scratch_shapes=[pltpu.SemaphoreType.DMA((7,)), pltpu.SemaphoreType.DMA((7,)), pltpu.SemaphoreType.DMA],
        name=name, compiler_params=pltpu.CompilerParams(vmem_limit_bytes=VMEM_LIMIT),
    )(block)


def _sems(n):
    return [pltpu.SemaphoreType.DMA((n,)), pltpu.SemaphoreType.DMA((n,))]


def gather_layer(name, shards):
    n_w = len(shards)
    halves = [s.shape[0] // 2 for s in shards]

    def body(*refs):
        ins, outs = refs[:n_w], refs[n_w:2 * n_w]
        send_sems, recv_sems = refs[2 * n_w:]
        x, y, c, chips = _position()
        sibling = (x, y, 1 - c)

        def part(w, px, py, pc):
            return outs[w].at[2 * px + py, pl.ds(pc * halves[w], halves[w]), :]

        def copy(k, w, blk, to, src=None):
            return pltpu.make_async_remote_copy(src_ref=part(w, *blk) if src is None else src, dst_ref=part(w, *blk),
                                                send_sem=send_sems.at[k], recv_sem=recv_sems.at[k], device_id=to, device_id_type=MESH)

        pairs = [(w, j, chip) for w in range(n_w) for j, chip in enumerate(chips)]
        first = [copy(3 * w + j, w, (x, y, c), (*chip, c), src=ins[w].at[pl.ds(c * halves[w], halves[w]), :]) for w, j, chip in pairs]
        for cp in first:
            cp.start()
        passed = [copy(3 * n_w + 3 * w + j, w, (*chip, c), sibling) for w, j, chip in pairs]
        for (w, j, chip), fwd in zip(pairs, passed):
            copy(3 * w + j, w, (*chip, c), (x, y, c)).wait_recv()
            fwd.start()
        for w, j, chip in pairs:
            copy(3 * n_w + 3 * w + j, w, (*chip, 1 - c), (x, y, c)).wait_recv()
        for cp in first + passed:
            cp.wait_send()

    return pl.pallas_call(
        body, out_shape=[jax.ShapeDtypeStruct((N_CHIPS,) + s.shape, s.dtype) for s in shards],
        in_specs=[ANY] * n_w, out_specs=[ANY] * n_w, scratch_shapes=_sems(6 * n_w), name=name,
    )(*shards)


def swap_halves(name, grads):
    n_w = len(grads)
    halves = [g.shape[1] // 2 for g in grads]

    def body(*refs):
        ins, outs = refs[:n_w], refs[n_w:2 * n_w]
        send_sems, recv_sems = refs[2 * n_w:]
        x, y, c, _ = _position()
        cps = [pltpu.make_async_remote_copy(src_ref=ins[w].at[:, pl.ds((1 - c) * halves[w], halves[w]), :], dst_ref=outs[w],
                                            send_sem=send_sems.at[w], recv_sem=recv_sems.at[w], device_id=(x, y, 1 - c),
                                            device_id_type=MESH) for w in range(n_w)]
        for cp in cps:
            cp.start()
        for cp in cps:
            cp.wait()

    return pl.pallas_call(
        body, out_shape=[jax.ShapeDtypeStruct((N_CHIPS, h, g.shape[2]), g.dtype) for g, h in zip(grads, halves)],
        in_specs=[ANY] * n_w, out_specs=[ANY] * n_w, scratch_shapes=_sems(n_w), name=name,
    )(*grads)


def scatter_chips(name, partials):
    n_w = len(partials)

    def body(*refs):
        ins, outs = refs[:n_w], refs[n_w:2 * n_w]
        send_sems, recv_sems = refs[2 * n_w:]
        x, y, c, chips = _position()
        cps = [pltpu.make_async_remote_copy(src_ref=ins[w].at[2 * cx + cy], dst_ref=outs[w].at[j], send_sem=send_sems.at[3 * w + j],
                                            recv_sem=recv_sems.at[3 * w + j], device_id=(cx, cy, c), device_id_type=MESH)
               for w in range(n_w) for j, (cx, cy) in enumerate(chips)]
        for cp in cps:
            cp.start()
        for cp in cps:
            cp.wait()

    return pl.pallas_call(
        body, out_shape=[jax.ShapeDtypeStruct((3,) + p.shape[1:], p.dtype) for p in partials],
        in_specs=[ANY] * n_w, out_specs=[ANY] * n_w, scratch_shapes=_sems(3 * n_w), name=name,
    )(*partials)


def share_halves(name, bufs):
    n_w = len(bufs)
    halves = [b.shape[0] // 2 for b in bufs]

    def body(*refs):
        outs = refs[n_w:2 * n_w]
        send_sems, recv_sems = refs[2 * n_w:]
        x, y, c, _ = _position()

        def copy(w, pc):
            half = outs[w].at[pl.ds(pc * halves[w], halves[w]), :]
            return pltpu.make_async_remote_copy(src_ref=half, dst_ref=half, send_sem=send_sems.at[w], recv_sem=recv_sems.at[w],
                                                device_id=(x, y, 1 - c), device_id_type=MESH)

        for w in range(n_w):
            copy(w, c).start()
        for w in range(n_w):
            copy(w, 1 - c).wait_recv()
            copy(w, c).wait_send()

    return pl.pallas_call(
        body, out_shape=[jax.ShapeDtypeStruct(b.shape, b.dtype) for b in bufs], in_specs=[ANY] * n_w, out_specs=[ANY] * n_w,
        input_output_aliases={w: w for w in range(n_w)}, scratch_shapes=_sems(n_w), name=name,
    )(*bufs)


def _row_tile(rows, cols):
    best = 16
    for t in range(16, rows + 1, 16):
        if rows % t == 0 and t * cols * 4 <= 1024 * 1024:
            best = t
    return best


def pair_sum(name, pos, grad, from_sibling):
    _, rows, cols = grad.shape
    half = rows // 2
    tr = _row_tile(half, cols)
    n_t = half // tr

    def body(pos_ref, g_ref, s_ref, b_ref, f_ref):
        tot = g_ref[...] + s_ref[...]
        b_ref[...] = tot.astype(BF16)

        @pl.when(pl.program_id(1) == pos_ref[1])
        def _():
            f_ref[...] = tot[0]

    blk = pl.BlockSpec((1, tr, cols), lambda i, k, pos: (k, i, 0))
    return pl.pallas_call(
        body, grid_spec=pltpu.PrefetchScalarGridSpec(
            num_scalar_prefetch=1, grid=(n_t, N_CHIPS),
            in_specs=[pl.BlockSpec((1, tr, cols), lambda i, k, pos: (k, pos[0] * n_t + i, 0)), blk],
            out_specs=[blk, pl.BlockSpec((tr, cols), lambda i, k, pos: (i, 0))]),
        out_shape=[jax.ShapeDtypeStruct((N_CHIPS, half, cols), BF16), jax.ShapeDtypeStruct((half, cols), F32)],
        name=name, compiler_params=_cparams(2),
    )(pos, grad, from_sibling)


def chip_sum(name, pos, own, landed):
    half, cols = own.shape
    tr = _row_tile(half, cols)
    n_t = half // tr

    def body(pos_ref, p_ref, l_ref, o_ref):
        o_ref[...] = ((p_ref[...] + l_ref[0].astype(F32)) + l_ref[1].astype(F32)) + l_ref[2].astype(F32)

    return pl.pallas_call(
        body, grid_spec=pltpu.PrefetchScalarGridSpec(
            num_scalar_prefetch=1, grid=(n_t,),
            in_specs=[pl.BlockSpec((tr, cols), lambda i, pos: (i, 0)), pl.BlockSpec((3, tr, cols), lambda i, pos: (0, i, 0))],
            out_specs=pl.BlockSpec((tr, cols), lambda i, pos: (pos[0] * n_t + i, 0))),
        out_shape=jax.ShapeDtypeStruct((2 * half, cols), F32), name=name, compiler_params=_cparams(1),
    )(pos, own, landed)


def reduce_scatter_layer(li, pos, grads):
    n = lambda t: f"{t}_l{li}"
    from_sibling = swap_halves(n("swap_halves"), grads)
    sums = [pair_sum(n(f"pair_sum{w}"), pos, g, s) for w, (g, s) in enumerate(zip(grads, from_sibling))]
    landed = scatter_chips(n("scatter_chips"), [b for b, _ in sums])
    halves = [chip_sum(n(f"chip_sum{w}"), pos, own, l) for w, ((_, own), l) in enumerate(zip(sums, landed))]
    return share_halves(n("share_halves"), halves)


def sum_devices(gathered):
    m_per = gathered.shape[0] // 8

    def body(g_ref, o_ref):
        tot = g_ref[pl.ds(0, m_per), :]
        for dev in range(1, 8):
            tot = tot + g_ref[pl.ds(dev * m_per, m_per), :]
        o_ref[...] = tot

    return pl.pallas_call(
        body, out_shape=jax.ShapeDtypeStruct((m_per, gathered.shape[1]), F32),
        in_specs=[pl.BlockSpec(memory_space=pltpu.VMEM)], out_specs=pl.BlockSpec(memory_space=pltpu.VMEM), name="sum_devices",
    )(gathered)


def kernel(x, p, g_mix, w_in, b_fox_f, fox_q_gain, fox_k_gain, sc_conv_w, dn_conv_w, dn_a_log, dn_dt_bias, dn_norm_gain, w_branch, w_o, g_ffn, w_up, ffn_conv_w, w_down, g_ple, w_ple_gate, w_ple, loss_target, m_g_mix, m_w_in, m_b_fox_f, m_fox_q_gain, m_fox_k_gain, m_sc_conv_w, m_dn_conv_w, m_dn_a_log, m_dn_dt_bias, m_dn_norm_gain, m_w_branch, m_w_o, m_g_ffn, m_w_up, m_ffn_conv_w, m_w_down, m_g_ple, m_w_ple_gate, m_w_ple, v_g_mix, v_w_in, v_b_fox_f, v_fox_q_gain, v_fox_k_gain, v_sc_conv_w, v_dn_conv_w, v_dn_a_log, v_dn_dt_bias, v_dn_norm_gain, v_w_branch, v_w_o, v_g_ffn, v_w_up, v_ffn_conv_w, v_w_down, v_g_ple, v_w_ple_gate, v_w_ple):
    a = dict(g_mix=g_mix, w_in=w_in, b_fox_f=b_fox_f, fox_q_gain=fox_q_gain, fox_k_gain=fox_k_gain, sc_conv_w=sc_conv_w,
             dn_conv_w=dn_conv_w, dn_a_log=dn_a_log, dn_dt_bias=dn_dt_bias, dn_norm_gain=dn_norm_gain, w_branch=w_branch, w_o=w_o,
             g_ffn=g_ffn, w_up=w_up, ffn_conv_w=ffn_conv_w, w_down=w_down, g_ple=g_ple, w_ple_gate=w_ple_gate, w_ple=w_ple)
    mom = dict(g_mix=m_g_mix, w_in=m_w_in, b_fox_f=m_b_fox_f, fox_q_gain=m_fox_q_gain, fox_k_gain=m_fox_k_gain, sc_conv_w=m_sc_conv_w,
               dn_conv_w=m_dn_conv_w, dn_a_log=m_dn_a_log, dn_dt_bias=m_dn_dt_bias, dn_norm_gain=m_dn_norm_gain, w_branch=m_w_branch,
               w_o=m_w_o, g_ffn=m_g_ffn, w_up=m_w_up, ffn_conv_w=m_ffn_conv_w, w_down=m_w_down, g_ple=m_g_ple, w_ple_gate=m_w_ple_gate,
               w_ple=m_w_ple)
    var = dict(g_mix=v_g_mix, w_in=v_w_in, b_fox_f=v_b_fox_f, fox_q_gain=v_fox_q_gain, fox_k_gain=v_fox_k_gain, sc_conv_w=v_sc_conv_w,
               dn_conv_w=v_dn_conv_w, dn_a_log=v_dn_a_log, dn_dt_bias=v_dn_dt_bias, dn_norm_gain=v_dn_norm_gain, w_branch=v_w_branch,
               w_o=v_w_o, g_ffn=v_g_ffn, w_up=v_w_up, ffn_conv_w=v_ffn_conv_w, w_down=v_w_down, g_ple=v_g_ple, w_ple_gate=v_w_ple_gate,
               w_ple=v_w_ple)
    cx, cy, cc = lax.axis_index("x"), lax.axis_index("y"), lax.axis_index("c")
    chip = 2 * cx + cy
    pos = jnp.stack([cc, chip]).astype(jnp.int32)

    def as_blocks(t):
        return t.reshape(2, -1, t.shape[-1])

    gathered = []
    for li in range(2):
        shards = [as_blocks(a[nm])[li].astype(BF16) for nm in BIG]
        got = gather_layer(f"gather_weights_l{li}", shards)
        gathered.append([lax.dynamic_update_slice(g, s[None], (chip, 0, 0)) for g, s in zip(got, shards)])
    conv_shapes = [a[nm].shape for nm in CONVS]
    conv_all = gather_small("gather_conv_w", pack_rows([a[nm] for nm in CONVS], F32))
    conv_rows = conv_all.shape[0] // 8
    conv_chip = [unpack_rows(conv_all[2 * k * conv_rows:(2 * k + 1) * conv_rows], conv_shapes) for k in range(N_CHIPS)]
    conv = {nm: jnp.concatenate([conv_chip[k][i] for k in range(N_CHIPS)], axis=2) for i, nm in enumerate(CONVS)}

    act = x[0]
    weights, saved = [], []
    for li in range(2):
        weights.append(layer_weights(li, gathered[li], conv, a))
        act, sv = layer_fwd(li, act, p[li, 0], weights[li])
        saved.append(sv)
    d_act, loss_part = loss_call(act, loss_target[0])
    loss = lax.psum(loss_part, ("x", "y", "c"))
    layer_grads, reduced = [None, None], [None, None]
    for li in (1, 0):
        d_act, layer_grads[li] = layer_bwd(li, d_act, saved[li], weights[li])
        reduced[li] = reduce_scatter_layer(li, pos, [layer_grads[li][nm] for nm in BIG])
    grad_x = d_act[None]

    def both(nm):
        return jnp.stack([layer_grads[0][nm], layer_grads[1][nm]])

    local = {nm: both(nm) for nm in ("g_mix", "b_fox_f", "fox_q_gain", "fox_k_gain", "dn_norm_gain", "g_ffn", "g_ple", "sc_conv_w",
                                      "dn_conv_w", "ffn_conv_w")}
    local["dn_a_log"] = jnp.stack([layer_grads[li]["ad"][0] for li in range(2)])
    local["dn_dt_bias"] = jnp.stack([layer_grads[li]["ad"][1] for li in range(2)])

    small_names = SMALL + CONVS
    small_shapes = [local[nm].shape for nm in small_names]
    small_sum = sum_devices(gather_small("gather_small_grads", pack_rows([local[nm] for nm in small_names], F32)))
    small_grads = dict(zip(small_names, unpack_rows(small_sum, small_shapes)))
    for nm in CONVS:
        width = a[nm].shape[2]
        small_grads[nm] = lax.dynamic_slice_in_dim(small_grads[nm], chip * width, width, axis=2)

    grads, deltas, new_m, new_v = dict(small_grads), {}, {}, {}
    for nm in small_names:
        deltas[nm], new_m[nm], new_v[nm] = adam_call(f"adam_{nm}", a[nm], grads[nm], mom[nm], var[nm])
    for i, nm in enumerate(BIG):
        res = adam_layers(f"adam_{nm}", as_blocks(a[nm]), as_blocks(mom[nm]), as_blocks(var[nm]), reduced[0][i], reduced[1][i])
        grads[nm], deltas[nm], new_m[nm], new_v[nm] = [r.reshape(a[nm].shape) for r in res]
    return (loss, grad_x, *[grads[nm] for nm in WEIGHTS], *[deltas[nm] for nm in WEIGHTS], *[new_m[nm] for nm in WEIGHTS],
            *[new_v[nm] for nm in WEIGHTS])
```

```python
import functools

import jax
import jax.numpy as jnp
from jax import lax
from jax.experimental import pallas as pl
from jax.experimental.pallas import tpu as pltpu

F32 = jnp.float32
BF16 = jnp.bfloat16
HI = lax.Precision.HIGHEST
MESH = pl.DeviceIdType.MESH

D_MODEL = 1024
BRANCH = 512
FOX_DH = 64
DN_DH = 128
DN_HEADS = 4
DN_CHUNK = 64
FOX_BLOCK = 128
D_FF = 2816
EPS = 1e-6
N_CHIPS = 4
LANES = 128

ADAM_LR, ADAM_B1, ADAM_B2, ADAM_EPS, ADAM_WD, ADAM_STEP = 0.001, 0.9, 0.999, 1e-08, 0.01, 10

VMEM_LIMIT = 56 * 1024 * 1024

C_FQ, C_FK, C_FV, C_SB, C_SC, C_SV, C_DN, C_DZ, C_GATE = 0, 512, 1024, 1536, 2048, 2560, 3072, 4608, 5120
IN_MAIN = 8192
IN_SIZES = (1536, 8, 1536, 1536, 4, 4, 512, 3072)

BIG = ("w_in", "w_branch", "w_o", "w_up", "w_down", "w_ple_gate", "w_ple")
BIG_AXIS = {"w_in": 2, "w_branch": 3, "w_o": 1, "w_up": 2, "w_down": 1, "w_ple_gate": 1, "w_ple": 2}
CONVS = ("sc_conv_w", "dn_conv_w", "ffn_conv_w")
SMALL = ("g_mix", "b_fox_f", "fox_q_gain", "fox_k_gain", "dn_a_log", "dn_dt_bias", "dn_norm_gain", "g_ffn", "g_ple")
WEIGHTS = ("g_mix", "w_in", "b_fox_f", "fox_q_gain", "fox_k_gain", "sc_conv_w", "dn_conv_w", "dn_a_log", "dn_dt_bias",
           "dn_norm_gain", "w_branch", "w_o", "g_ffn", "w_up", "ffn_conv_w", "w_down", "g_ple", "w_ple_gate", "w_ple")


def _iota(shape, dim):
    return lax.broadcasted_iota(jnp.int32, shape, dim)


def _dg(a, b, mode, prec=None):
    dims = {"nn": ((1,), (0,)), "nt": ((1,), (1,)), "tn": ((0,), (0,))}[mode]
    return lax.dot_general(a, b, (dims, ((), ())), precision=prec, preferred_element_type=F32)


def _bdot_impl(a, b, mode):
    return _dg(a.astype(BF16), b.astype(BF16), mode)


@functools.partial(jax.custom_vjp, nondiff_argnums=(2,))
def _bdot_diff(a, b, mode):
    return _bdot_impl(a, b, mode)


def _bdot_fwd(a, b, mode):
    return _bdot_impl(a, b, mode), (a, b)


def _bdot_bwd(mode, res, g):
    a, b = res
    if mode == "nn":
        da, db = _bdot_impl(g, b, "nt"), _bdot_impl(a, g, "tn")
    elif mode == "nt":
        da, db = _bdot_impl(g, b, "nn"), _bdot_impl(g, a, "tn")
    else:
        da, db = _bdot_impl(b, g, "nt"), _bdot_impl(a, g, "nn")
    return da.astype(a.dtype), db.astype(b.dtype)


_bdot_diff.defvjp(_bdot_fwd, _bdot_bwd)


def _bdot(d):
    return _bdot_diff if d else _bdot_impl


def _shift_impl(x, k):
    return jnp.where(_iota(x.shape, 0) >= k, pltpu.roll(x, k, 0), 0.0)


def _unshift_impl(g, k):
    n = g.shape[0]
    return jnp.where(_iota(g.shape, 0) < n - k, pltpu.roll(g, n - k, 0), 0.0)


@functools.partial(jax.custom_vjp, nondiff_argnums=(1,))
def _shift_diff(x, k):
    return _shift_impl(x, k)


_shift_diff.defvjp(lambda x, k: (_shift_impl(x, k), None), lambda k, _, g: (_unshift_impl(g, k),))


def _row(w, j):
    return jnp.sum(jnp.where(_iota(w.shape, 0) == j, w, 0.0), axis=0, keepdims=True)


def _col(w, j):
    return jnp.sum(jnp.where(_iota(w.shape, 1) == j, w, 0.0), axis=1, keepdims=True)


def _conv(d, x, w):
    shift = _shift_diff if d else _shift_impl
    taps = w.shape[0]
    y = x * _row(w, taps - 1)
    for j in range(taps - 1):
        y = y + shift(x, taps - 1 - j) * _row(w, j)
    return y


def _softplus(x):
    return jnp.maximum(x, 0.0) + jnp.log(1.0 + jnp.exp(-jnp.abs(x)))


def _silu(x):
    return x * jax.nn.sigmoid(x)


def _rms(x, gain):
    return x * lax.rsqrt(jnp.mean(x * x, axis=-1, keepdims=True) + EPS) * gain


def _rms_fn(d, pids, x, gain):
    return (_rms(x, gain),)


def _loss_fn(d, pids, y, t):
    e = y - t
    part = 0.5 / D_MODEL * jnp.sum(e * e, keepdims=True)
    return e * (1.0 / D_MODEL), jnp.broadcast_to(part, (8, LANES))


def _fox_prep_fn(d, pids, q, k, gq, gk):
    first = _iota(q.shape, 1) < FOX_DH

    def norm(x, gain):
        sq = x * x
        ss_a = jnp.sum(jnp.where(first, sq, 0.0), axis=1, keepdims=True)
        ss_b = jnp.sum(jnp.where(first, 0.0, sq), axis=1, keepdims=True)
        rs = jnp.where(first, lax.rsqrt(ss_a / FOX_DH + EPS), lax.rsqrt(ss_b / FOX_DH + EPS))
        return x * rs * gain

    return norm(q, gq) * FOX_DH ** -0.5, norm(k, gk)


def _fox_gate_fn(d, pids, f, bias):
    logf = -_softplus(-(f + bias))
    n_r, n_c = logf.shape
    tri = (_iota((n_c, n_c), 0) <= _iota((n_c, n_c), 1)).astype(F32)
    within = _dg(logf, tri, "nn", HI)
    tot = jnp.broadcast_to(jnp.sum(logf, axis=1, keepdims=True), logf.shape)
    below = (_iota((n_r, n_r), 1) < _iota((n_r, n_r), 0)).astype(F32)
    return (within + _dg(below, tot, "nn", HI),)


def _fox_attn_fn(d, pids, q, k, v, cq_a, cq_b, ck_a, ck_b):
    dot = _bdot(d)
    first = _iota(q.shape, 1) < FOX_DH
    n_q, n_k = q.shape[0], k.shape[0]
    causal = (pids[1] * n_q + _iota((n_q, n_k), 0)) >= _iota((n_q, n_k), 1)

    qs = [jnp.where(first, q, 0.0), jnp.where(first, 0.0, q)]
    s = _each(lambda qh, cq, ck: jnp.where(causal, dot(qh, k, "nt") + cq - ck, -1e30), qs, [cq_a, cq_b], [ck_a, ck_b])
    e = [jnp.exp(si - lax.stop_gradient(jnp.max(si, axis=1, keepdims=True))) for si in s]
    o_a, o_b = [dot(ei / jnp.sum(ei, axis=1, keepdims=True), v, "nn") for ei in e]
    return (jnp.where(first, o_a, o_b),)


def _sconv_fn(d, pids, sb, sc, sv, w):
    return (sb * _conv(d, sc * sv, w),)


def _dnconv_fn(d, pids, x, w):
    return (_silu(_conv(d, x, w)),)


def _merge_fn(d, pids, y0, y1, y2, g0, g1, g2):
    return (jax.nn.sigmoid(g0) * y0 + jax.nn.sigmoid(g1) * y1 + jax.nn.sigmoid(g2) * y2,)


def _ffn_act_fn(d, pids, ug, uv, wg, wv):
    return (_silu(_conv(d, ug, wg)) * _conv(d, uv, wv),)


def _ple_fn(d, pids, gpre, pe, x):
    return (x + jax.nn.sigmoid(gpre) * pe,)


def _adam_fn(d, pids, w, g, m, v):
    m2 = ADAM_B1 * m + (1.0 - ADAM_B1) * g
    v2 = ADAM_B2 * v + (1.0 - ADAM_B2) * (g * g)
    m_hat = m2 / (1.0 - ADAM_B1 ** ADAM_STEP)
    v_hat = v2 / (1.0 - ADAM_B2 ** ADAM_STEP)
    delta = -ADAM_LR * (m_hat / (jnp.sqrt(v_hat) + ADAM_EPS) + ADAM_WD * w)
    return delta, m2, v2


def _each(fn, *lists):
    return [fn(*args) for args in zip(*lists)]


def _tri_inv_impl(mats):
    n = mats[0].shape[0]
    r, c = _iota((n, n), 0), _iota((n, n), 1)
    diag_blk = (r >> 4) == (c >> 4)
    eye = (r == c).astype(F32)
    mm = lambda us, ws: _each(lambda u, w: _dg(u, w, "nn", HI), us, ws)
    grow = lambda ps, xs: _each(lambda p, px: p + px, ps, mm(ps, xs))
    x = [jnp.where(diag_blk, -a, 0.0) for a in mats]
    p = [eye + xi for xi in x]
    x2 = mm(x, x)
    p = grow(p, x2)
    x4 = mm(x2, x2)
    p = grow(p, x4)
    p = grow(p, mm(x4, x4))
    y = [-yi for yi in mm(p, [jnp.where(diag_blk, 0.0, a) for a in mats])]
    q = grow([eye + yi for yi in y], mm(y, y))
    return mm(q, p)


@jax.custom_vjp
def _tri_inv_diff(mats):
    return _tri_inv_impl(mats)


def _tri_inv_fwd(mats):
    ts = _tri_inv_impl(mats)
    return ts, ts


def _tri_inv_bwd(ts, gs):
    left = _each(lambda t, g: _dg(t, g, "tn", HI), ts, gs)
    return ([-m for m in _each(lambda l, t: _dg(l, t, "nt", HI), left, ts)],)


_tri_inv_diff.defvjp(_tri_inv_fwd, _tri_inv_bwd)


def _dn_local(d, qs, ks, vs, a_cs, a_rs, b_cs, a_logs, dt_bs):
    dot = _bdot(d)
    inv = _tri_inv_diff if d else _tri_inv_impl
    n = qs[0].shape[0]
    r, c = _iota((n, n), 0), _iota((n, n), 1)
    incl, strict, upper = r >= c, r > c, r <= c
    qs = [q * lax.rsqrt(jnp.sum(q * q, axis=1, keepdims=True) + EPS) * DN_DH ** -0.5 for q in qs]
    ks = [k * lax.rsqrt(jnp.sum(k * k, axis=1, keepdims=True) + EPS) for k in ks]
    betas = [jax.nn.sigmoid(b) for b in b_cs]
    rates = [-jnp.exp(a) for a in a_logs]
    g_cs = _each(lambda rate, a, dt: rate * _softplus(a + dt), rates, a_cs, dt_bs)
    g_rs = _each(lambda rate, a, dt: rate * _softplus(a + dt), rates, a_rs, dt_bs)
    gcum_cs = [jnp.sum(jnp.where(incl, g, 0.0), axis=1, keepdims=True) for g in g_rs]
    gcum_rs = [jnp.sum(jnp.where(upper, g, 0.0), axis=0, keepdims=True) for g in g_cs]
    decays = _each(lambda gc, gr: jnp.exp(jnp.where(incl, gc - gr, -1e30)), gcum_cs, gcum_rs)
    kbs = _each(lambda k, b: k * b, ks, betas)
    kk = _each(lambda kb, k: dot(kb, k, "nt"), kbs, ks)
    ts = inv(_each(lambda m, dec: jnp.where(strict, m * dec, 0.0), kk, decays))
    e_gs = [jnp.exp(g) for g in gcum_cs]
    us = _each(lambda t, v, b: _dg(t, v * b, "nn", HI), ts, vs, betas)
    k_cums = _each(lambda t, kb, e: _dg(t, kb * e, "nn", HI), ts, kbs, e_gs)
    qk = _each(lambda q, k: dot(q, k, "nt"), qs, ks)
    qk = _each(lambda m, dec: jnp.where(incl, m * dec, 0.0), qk, decays)
    g_lasts = [jnp.sum(g, axis=0, keepdims=True) for g in g_cs]
    q_decs = _each(lambda q, e: q * e, qs, e_gs)
    k_decs = _each(lambda k, gl, gc: k * jnp.exp(gl - gc), ks, g_lasts, gcum_cs)
    return list(zip(us, k_cums, q_decs, k_decs, qk, g_lasts))


def _dn_step(d, s_prevs, items, zs, gain):
    dot = _bdot(d)
    us, k_cums, q_decs, k_decs, qks, g_lasts = [list(t) for t in zip(*items)]
    v_news = _each(lambda u, kc, s: u - dot(kc, s, "nn"), us, k_cums, s_prevs)
    inter = _each(lambda qd, s: dot(qd, s, "nn"), q_decs, s_prevs)
    outs = _each(lambda o, qk, vn: o + dot(qk, vn, "nn"), inter, qks, v_news)
    s_nexts = _each(lambda s, gl, kd, vn: s * jnp.exp(gl) + dot(kd, vn, "tn"), s_prevs, g_lasts, k_decs, v_news)
    return _each(lambda o, z: _rms(o, gain) * _silu(z), outs, zs), s_nexts


def _split_heads(t):
    return [t[:, h * DN_DH:(h + 1) * DN_DH] for h in range(t.shape[1] // DN_DH)]


def _dn_gates(ps, a_rows, ad):
    hs = range(DN_HEADS)
    return ([_col(ps, 12 + h) for h in hs], [_row(a_rows, h) for h in hs], [_col(ps, 8 + h) for h in hs],
            [_col(_row(ad, 0), h) for h in hs], [_col(_row(ad, 1), h) for h in hs])


def _head_rows(vals):
    row = _iota((8, LANES), 0)
    tile = jnp.zeros((8, LANES), F32)
    for h, val in enumerate(vals):
        tile = tile + jnp.where(row == h, val, 0.0)
    return tile


def _cparams(n_axes):
    return pltpu.CompilerParams(dimension_semantics=("arbitrary",) * n_axes, vmem_limit_bytes=VMEM_LIMIT)


def _first_visit(acc_axes):
    cond = None
    for a in acc_axes:
        here = pl.program_id(a) == 0
        cond = here if cond is None else jnp.logical_and(cond, here)
    return cond


def _tile(ref):
    val = ref[...]
    shape = val.shape
    while len(shape) > 2 and shape[0] == 1:
        shape = shape[1:]
    return val.reshape(shape)


def _store(ref, val, first):
    val = val.astype(ref.dtype).reshape(ref.shape)
    if first is None:
        ref[...] = val
        return

    @pl.when(first)
    def _():
        ref[...] = val

    @pl.when(jnp.logical_not(first))
    def _():
        ref[...] += val


def _specs(ops):
    return [pl.BlockSpec(block, imap) for _, block, imap in ops]


def tile_fwd(name, fn, grid, ins, outs):
    n_in = len(ins)

    def body(*refs):
        pids = tuple(pl.program_id(a) for a in range(len(grid)))
        firsts = [_first_visit(o[4]) if o[4] else None for o in outs]
        res = fn(False, pids, *[_tile(r) for r in refs[:n_in]])
        for ref, val, first in zip(refs[n_in:], res, firsts):
            _store(ref, val, first)

    out = pl.pallas_call(
        body, grid=grid, in_specs=_specs(ins),
        out_specs=[pl.BlockSpec(o[2], o[3]) for o in outs],
        out_shape=[jax.ShapeDtypeStruct(o[0], o[1]) for o in outs],
        name=name, compiler_params=_cparams(len(grid)),
    )(*[a for a, _, _ in ins])
    return out


def tile_bwd(name, fn, grid, ins, cots, diff, adds=None):
    adds = adds or {}
    n_in, n_cot = len(ins), len(cots)
    add_pos = sorted(adds)
    diff_idx = [d[0] for d in diff]
    out_desc = [d[2] if len(d) > 2 and d[2] is not None else (ins[d[0]][0].shape, ins[d[0]][1], ins[d[0]][2]) for d in diff]
    out_dtypes = [d[3] if len(d) > 3 else F32 for d in diff]

    def body(*refs):
        pids = tuple(pl.program_id(a) for a in range(len(grid)))
        firsts = [_first_visit(d[1]) if d[1] else None for d in diff]
        vals = [_tile(r) for r in refs[:n_in]]
        cot_vals = [_tile(r) for r in refs[n_in:n_in + n_cot]]
        add_vals = [_tile(r) for r in refs[n_in + n_cot:n_in + n_cot + len(add_pos)]]
        out_refs = refs[n_in + n_cot + len(add_pos):]

        def f(*dv):
            full = list(vals)
            for i, val in zip(diff_idx, dv):
                full[i] = val
            return fn(True, pids, *full)

        prim, vjp = jax.vjp(f, *[vals[i].astype(F32) for i in diff_idx])
        grads = list(vjp(tuple(c.astype(o.dtype) for c, o in zip(cot_vals, prim))))
        for pos, val in zip(add_pos, add_vals):
            grads[pos] = grads[pos] + val.astype(F32)
        for ref, val, first in zip(out_refs, grads, firsts):
            _store(ref, val, first)

    all_ins = list(ins) + list(cots) + [adds[p] for p in add_pos]
    out = pl.pallas_call(
        body, grid=grid, in_specs=_specs(all_ins),
        out_specs=[pl.BlockSpec(o[1], o[2]) for o in out_desc],
        out_shape=[jax.ShapeDtypeStruct(o[0], dt) for o, dt in zip(out_desc, out_dtypes)],
        name=name, compiler_params=_cparams(len(grid)),
    )(*[a for a, _, _ in all_ins])
    return out


def _pick(dim, cands):
    for c in cands:
        if dim % c == 0:
            return c
    return dim


MM_TILES = (1024, 512, 256, 128)


def mm(name, a, b, mode, add=None, out_dtype=F32, blocks=None):
    wide = None
    if mode == "nn":
        (m, kk), n = a.shape, b.shape[-1]
    elif mode == "nt":
        (m, kk), n = a.shape, b.shape[-2]
    else:
        (kk, m), n = a.shape, b.shape[1]
    if blocks is not None:
        lo, n_blk = blocks
        wide = b.shape[-1] if mode != "tn" else n // n_blk
        if mode == "nn":
            n = wide * n_blk
    tm = _pick(m, MM_TILES)
    if mode == "nt" and blocks is not None:
        tn, tk = _pick(n, MM_TILES), _pick(wide, MM_TILES[:-1])
    elif blocks is not None:
        tn, tk = _pick(wide, MM_TILES[:-1]), _pick(kk, MM_TILES)
    else:
        tn, tk = _pick(n, MM_TILES), _pick(kk, MM_TILES)
    nk = kk // tk
    a_spec = pl.BlockSpec((tk, tm), lambda i, j, k: (k, i)) if mode == "tn" else pl.BlockSpec((tm, tk), lambda i, j, k: (i, k))
    o_spec = pl.BlockSpec((tm, tn), lambda i, j, k: (i, j))
    out_shape = (m, n)
    if blocks is None:
        b_spec = pl.BlockSpec((tn, tk), lambda i, j, k: (j, k)) if mode == "nt" else pl.BlockSpec((tk, tn), lambda i, j, k: (k, j))
    elif mode == "nn":
        per = wide // tn
        b_spec = pl.BlockSpec((1, tk, tn), lambda i, j, k: (lo + j // per, k, j % per))
    elif mode == "nt":
        per = wide // tk
        b_spec = pl.BlockSpec((1, tn, tk), lambda i, j, k: (lo + k // per, j, k % per))
    else:
        per = wide // tn
        b_spec = pl.BlockSpec((tk, tn), lambda i, j, k: (k, j))
        o_spec = pl.BlockSpec((1, tm, tn), lambda i, j, k: (j // per, i, j % per))
        out_shape = (n_blk, m, wide)

    def body(*refs):
        a_ref, b_ref = refs[0], refs[1]
        add_ref = refs[2] if add is not None else None
        o_ref, acc = refs[-2], refs[-1]
        k = pl.program_id(2)
        part = _bdot_impl(_tile(a_ref), _tile(b_ref), mode)

        @pl.when(k == 0)
        def _():
            acc[...] = part

        @pl.when(k > 0)
        def _():
            acc[...] += part

        @pl.when(k == nk - 1)
        def _():
            res = acc[...]
            if add_ref is not None:
                res = res + add_ref[...]
            o_ref[...] = res.astype(o_ref.dtype).reshape(o_ref.shape)

    operands = [a, b] + ([add] if add is not None else [])
    in_specs = [a_spec, b_spec] + ([o_spec] if add is not None else [])
    return pl.pallas_call(
        body, grid=(m // tm, n // tn, nk), in_specs=in_specs, out_specs=o_spec,
        out_shape=jax.ShapeDtypeStruct(out_shape, out_dtype),
        scratch_shapes=[pltpu.VMEM((tm, tn), F32)],
        name=name, compiler_params=_cparams(3),
    )(*operands)


def _rows(x, width=None, off=0, tm=256):
    width = x.shape[1] if width is None else width
    return (x, (tm, width), lambda i, off=off: (i, off))


def _whole(x):
    nd = x.ndim
    return (x, x.shape, lambda *pids, nd=nd: (0,) * nd)


def _rms_ops(x, gain):
    return [_rows(x), _whole(gain)]


def rms_fwd(name, x, gain):
    s, dm = x.shape
    return tile_fwd(name, _rms_fn, (s // 256,), _rms_ops(x, gain), [((s, dm), BF16, (256, dm), lambda i: (i, 0), ())])[0]


def rms_bwd(name, x, gain, dh, dres):
    s = x.shape[0]
    return tile_bwd(name, _rms_fn, (s // 256,), _rms_ops(x, gain), [_rows(dh)], [(0, ()), (1, (0,))], adds={0: _rows(dres)})


def loss_call(y, t):
    s, dm = y.shape
    dy, part = tile_fwd("loss", _loss_fn, (s // 256,), [_rows(y), _rows(t)],
                        [((s, dm), F32, (256, dm), lambda i: (i, 0), ()), ((8, LANES), F32, (8, LANES), lambda i: (0, 0), (0,))])
    return dy, part[0, 0]


def _fox_prep_ops(pm, gq, gk):
    tm = 512
    return [(pm, (tm, LANES), lambda i, j: (i, C_FQ // LANES + j)), (pm, (tm, LANES), lambda i, j: (i, C_FK // LANES + j)),
            _whole(gq), _whole(gk)]


def fox_prep_fwd(name, pm, gq, gk):
    s = pm.shape[0]
    out = ((s, BRANCH), BF16, (512, LANES), lambda i, j: (i, j), ())
    return tile_fwd(name, _fox_prep_fn, (s // 512, 4), _fox_prep_ops(pm, gq, gk), [out, out])


def fox_prep_bwd(name, pm, gq, gk, dqn, dkn):
    s = pm.shape[0]
    cot = lambda g: (g, (512, LANES), lambda i, j: (i, j))
    own = ((s, BRANCH), (512, LANES), lambda i, j: (i, j))
    return tile_bwd(name, _fox_prep_fn, (s // 512, 4), _fox_prep_ops(pm, gq, gk), [cot(dqn), cot(dkn)],
                    [(0, (), own, BF16), (1, (), own, BF16), (2, (0, 1)), (3, (0, 1))])


def _fox_gate_ops(f_t, bias):
    return [(f_t, (1,) + f_t.shape[1:], lambda h: (h, 0, 0)), (bias, (1, 1, 1), lambda h: (h, 0, 0))]


def fox_gate_fwd(name, f_t, bias):
    n_h = f_t.shape[0]
    return tile_fwd(name, _fox_gate_fn, (n_h,), _fox_gate_ops(f_t, bias),
                    [(f_t.shape, F32, (1,) + f_t.shape[1:], lambda h: (h, 0, 0), ())])[0]


def fox_gate_bwd(name, f_t, bias, dcum):
    n_h = f_t.shape[0]
    return tile_bwd(name, _fox_gate_fn, (n_h,), _fox_gate_ops(f_t, bias),
                    [(dcum, (1,) + f_t.shape[1:], lambda h: (h, 0, 0))], [(0, ()), (1, ())])


def _fox_attn_ops(qn, kn, pm, cum_c, cum_r):
    s = qn.shape[0]
    nb = FOX_BLOCK
    return [(qn, (nb, LANES), lambda p, i: (i, p)), (kn, (s, LANES), lambda p, i: (0, p)),
            (pm, (s, LANES), lambda p, i: (0, C_FV // LANES + p)),
            (cum_c, (1, nb, 1), lambda p, i: (2 * p, i, 0)), (cum_c, (1, nb, 1), lambda p, i: (2 * p + 1, i, 0)),
            (cum_r, (1, 1, s), lambda p, i: (2 * p, 0, 0)), (cum_r, (1, 1, s), lambda p, i: (2 * p + 1, 0, 0))]


def fox_attn_fwd(name, qn, kn, pm, cum_c, cum_r):
    s = qn.shape[0]
    return tile_fwd(name, _fox_attn_fn, (4, s // FOX_BLOCK), _fox_attn_ops(qn, kn, pm, cum_c, cum_r),
                    [((s, BRANCH), BF16, (FOX_BLOCK, LANES), lambda p, i: (i, p), ())])[0]


def fox_attn_bwd(name, qn, kn, pm, cum_c, cum_r, dy):
    s = qn.shape[0]
    pair_c = ((4, s, 1), (1, FOX_BLOCK, 1), lambda p, i: (p, i, 0))
    pair_r = ((4, 1, s), (1, 1, s), lambda p, i: (p, 0, 0))
    d_qn, d_kn, d_v, d_cqa, d_cqb, d_cka, d_ckb = tile_bwd(
        name, _fox_attn_fn, (4, s // FOX_BLOCK), _fox_attn_ops(qn, kn, pm, cum_c, cum_r),
        [(dy, (FOX_BLOCK, LANES), lambda p, i: (i, p))],
        [(0, ()), (1, (1,)), (2, (1,), ((s, BRANCH), (s, LANES), lambda p, i: (0, p))),
         (3, (), pair_c), (4, (), pair_c), (5, (1,), pair_r), (6, (1,), pair_r)])
    d_cum = jnp.stack([d_cqa[:, :, 0] + d_cka[:, 0, :], d_cqb[:, :, 0] + d_ckb[:, 0, :]], axis=1).reshape(8, s)
    return d_qn, d_kn, d_v, d_cum


def sconv_ops(pm, w):
    s = pm.shape[0]
    blk = lambda c0: (pm, (s, LANES), lambda j, c0=c0: (0, c0 // LANES + j))
    return [blk(C_SB), blk(C_SC), blk(C_SV), (w, (w.shape[0], LANES), lambda j: (0, j))]


def dnconv_ops(pm, w):
    s = pm.shape[0]
    return [(pm, (s, LANES), lambda j: (0, C_DN // LANES + j)), (w, (w.shape[0], LANES), lambda j: (0, j))]


def ffn_ops(ug, uv, w):
    s = ug.shape[0]
    n_t = D_FF // LANES
    return [(ug, (s, LANES), lambda j: (0, j)), (uv, (s, LANES), lambda j: (0, j)),
            (w, (w.shape[0], LANES), lambda j: (0, j)), (w, (w.shape[0], LANES), lambda j: (0, n_t + j))]


def _col_out(s, width, dtype=F32):
    return ((s, width), dtype, (s, LANES), lambda j: (0, j), ())


def _col_cot(g):
    return (g, (g.shape[0], LANES), lambda j: (0, j))


def merge_ops(yp, pm):
    gate = lambda b: (pm, (256, D_MODEL), lambda i, b=b: (i, C_GATE // D_MODEL + b))
    return [_rows(yp[0]), _rows(yp[1]), _rows(yp[2]), gate(0), gate(1), gate(2)]


def ple_ops(gpre, pe, x):
    return [_rows(gpre), _rows(pe), _rows(x)]


def adam_call(name, w, g, m, v):
    shape = w.shape
    last = shape[-1]
    rows = w.size // last
    flat = lambda t: t.reshape(rows, last)
    tm = rows
    for cand in (512, 256, 128, 64, 32, 16, 8):
        if rows % cand == 0 and cand * last * 4 <= 2 * 1024 * 1024:
            tm = cand
            break
    spec = lambda t: (flat(t), (tm, last), lambda i: (i, 0))
    out = ((rows, last), F32, (tm, last), lambda i: (i, 0), ())
    res = tile_fwd(name, _adam_fn, (rows // tm,), [spec(w), spec(g), spec(m), spec(v)], [out, out, out])
    return [r.reshape(shape) for r in res]


def _adam_layers_fn(d, pids, w, m, v, g0, g1):
    g = jnp.where(pids[0] == 0, g0, g1)
    return (g,) + _adam_fn(d, pids, w, g, m, v)


def adam_layers(name, w, m, v, g0, g1):
    _, rows, cols = w.shape
    tm = _row_tile(rows, cols)
    n_t = rows // tm
    lay = lambda t: (t, (1, tm, cols), lambda l, i: (l, i, 0))
    ins = [lay(w), lay(m), lay(v), (g0, (tm, cols), lambda l, i: (i * (1 - l) + (n_t - 1) * l, 0)), (g1, (tm, cols), lambda l, i: (i * l, 0))]
    out = (w.shape, F32, (1, tm, cols), lambda l, i: (l, i, 0), ())
    return tile_fwd(name, _adam_layers_fn, (2, n_t), ins, [out, out, out, out])


DN_GROUP = 4


def _dn_local_specs(rev_n=None):
    rows = DN_GROUP * DN_CHUNK
    idx = (lambda j: j) if rev_n is None else (lambda j: rev_n - 1 - j)
    return [pl.BlockSpec((rows, 3 * BRANCH), lambda j: (idx(j), 0)), pl.BlockSpec((rows, LANES), lambda j: (idx(j), 0)),
            pl.BlockSpec((DN_GROUP, DN_HEADS, DN_CHUNK), lambda j: (idx(j), 0, 0)), pl.BlockSpec((2, DN_HEADS), lambda j: (0, 0))]


def _dn_group_inputs(qkv, ps, a_rows, c):
    lo = c * DN_CHUNK
    heads = _split_heads(qkv[lo:lo + DN_CHUNK])
    return heads[0:4], heads[4:8], heads[8:12], ps[lo:lo + DN_CHUNK], a_rows[c]


def dn_local_fwd(name, dn_act, ps, a_rows, ad):
    s = dn_act.shape[0]
    n_c, n_g = s // DN_CHUNK, s // (DN_GROUP * DN_CHUNK)
    rows = DN_GROUP * DN_CHUNK

    def body(qkv_ref, ps_ref, ar_ref, ad_ref, u_ref, kc_ref, qd_ref, kd_ref, qk_ref, gl_ref):
        qkv, ps_v, a_rows_v, ad_v = qkv_ref[...], ps_ref[...], ar_ref[...], ad_ref[...]
        args = [[] for _ in range(8)]
        for c in range(DN_GROUP):
            q4, k4, v4, ps_c, ar_c = _dn_group_inputs(qkv, ps_v, a_rows_v, c)
            for lst, vals in zip(args, (q4, k4, v4) + _dn_gates(ps_c, ar_c, ad_v)):
                lst.extend(vals)
        everything = _dn_local(False, *args)
        for c in range(DN_GROUP):
            res = everything[c * DN_HEADS:(c + 1) * DN_HEADS]
            at = pl.ds(c * DN_CHUNK, DN_CHUNK)
            for ref, i in ((u_ref, 0), (kc_ref, 1), (qd_ref, 2), (kd_ref, 3)):
                ref[at, :] = jnp.concatenate([r[i] for r in res], axis=1)
            for h in range(DN_HEADS):
                qk_ref[c, h] = res[h][4]
            gl_ref[c] = _head_rows([r[5] for r in res])

    wide = pl.BlockSpec((rows, BRANCH), lambda j: (j, 0))
    return pl.pallas_call(
        body, grid=(n_g,), in_specs=_dn_local_specs(),
        out_specs=[wide, wide, wide, wide, pl.BlockSpec((DN_GROUP, DN_HEADS, DN_CHUNK, DN_CHUNK), lambda j: (j, 0, 0, 0)),
                   pl.BlockSpec((DN_GROUP, 8, LANES), lambda j: (j, 0, 0))],
        out_shape=[jax.ShapeDtypeStruct((s, BRANCH), F32)] * 4 + [jax.ShapeDtypeStruct((n_c, DN_HEADS, DN_CHUNK, DN_CHUNK), F32),
                                                                 jax.ShapeDtypeStruct((n_c, 8, LANES), F32)],
        name=name, compiler_params=_cparams(1),
    )(dn_act, ps, a_rows, ad)


def dn_local_bwd(name, dn_act, ps, a_rows, ad, cots):
    s = dn_act.shape[0]
    n_c, n_g = s // DN_CHUNK, s // (DN_GROUP * DN_CHUNK)
    rows = DN_GROUP * DN_CHUNK

    def body(qkv_ref, ps_ref, ar_ref, ad_ref, du_ref, dkc_ref, dqd_ref, dkd_ref, dqk_ref, dgl_ref, dqkv_ref, dps_ref, dar_ref, dad_ref):
        first = pl.program_id(0) == 0
        qkv, ps_v, a_rows_v, ad_v = qkv_ref[...], ps_ref[...], ar_ref[...], ad_ref[...]
        d_wide = [r[...] for r in (du_ref, dkc_ref, dqd_ref, dkd_ref)]
        qs, ks, vs, ps_cs, ar_cs, cot = [], [], [], [], [], []
        for c in range(DN_GROUP):
            q4, k4, v4, ps_c, ar_c = _dn_group_inputs(qkv, ps_v, a_rows_v, c)
            qs, ks, vs, ps_cs, ar_cs = qs + q4, ks + k4, vs + v4, ps_cs + [ps_c], ar_cs + [ar_c]
            lo = c * DN_CHUNK
            d_tiles = [_split_heads(t[lo:lo + DN_CHUNK]) for t in d_wide]
            d_gl = dgl_ref[c]
            cot += [(d_tiles[0][h], d_tiles[1][h], d_tiles[2][h], d_tiles[3][h], dqk_ref[c, h], _col(_row(d_gl, h), 0))
                    for h in range(DN_HEADS)]

        def f(qs, ks, vs, ps_cs, ar_cs, ad_v):
            gates = [[] for _ in range(5)]
            for ps_c, ar_c in zip(ps_cs, ar_cs):
                for lst, vals in zip(gates, _dn_gates(ps_c, ar_c, ad_v)):
                    lst.extend(vals)
            return _dn_local(True, qs, ks, vs, *gates)

        _, vjp = jax.vjp(f, qs, ks, vs, ps_cs, ar_cs, ad_v)
        d_q, d_k, d_v, d_ps, d_ar, d_ad = vjp(cot)
        for c in range(DN_GROUP):
            at, hs = pl.ds(c * DN_CHUNK, DN_CHUNK), slice(c * DN_HEADS, (c + 1) * DN_HEADS)
            dqkv_ref[at, :] = jnp.concatenate(d_q[hs] + d_k[hs] + d_v[hs], axis=1).astype(dqkv_ref.dtype)
            dps_ref[at, :] = d_ps[c]
            dar_ref[c] = d_ar[c]
        _store(dad_ref, d_ad, first)

    wide = pl.BlockSpec((rows, BRANCH), lambda j: (j, 0))
    specs = _dn_local_specs()
    return pl.pallas_call(
        body, grid=(n_g,),
        in_specs=specs + [wide, wide, wide, wide, pl.BlockSpec((DN_GROUP, DN_HEADS, DN_CHUNK, DN_CHUNK), lambda j: (j, 0, 0, 0)),
                          pl.BlockSpec((DN_GROUP, 8, LANES), lambda j: (j, 0, 0))],
        out_specs=specs,
        out_shape=[jax.ShapeDtypeStruct((s, 3 * BRANCH), F32), jax.ShapeDtypeStruct((s, LANES), F32),
                   jax.ShapeDtypeStruct((n_c, DN_HEADS, DN_CHUNK), F32), jax.ShapeDtypeStruct((2, DN_HEADS), F32)],
        name=name, compiler_params=_cparams(1),
    )(dn_act, ps, a_rows, ad, *cots)


def _dn_scan_specs(n_c, rev):
    idx = (lambda j: n_c - 1 - j) if rev else (lambda j: j)
    wide = pl.BlockSpec((DN_CHUNK, BRANCH), lambda j: (idx(j), 0))
    return [wide, wide, wide, wide, pl.BlockSpec((1, DN_HEADS, DN_CHUNK, DN_CHUNK), lambda j: (idx(j), 0, 0, 0)),
            pl.BlockSpec((1, 8, LANES), lambda j: (idx(j), 0, 0)), pl.BlockSpec((DN_CHUNK, BRANCH), lambda j: (idx(j), C_DZ // BRANCH)),
            pl.BlockSpec((1, DN_DH), lambda j: (0, 0))]


def _dn_scan_tiles(refs):
    u_ref, kc_ref, qd_ref, kd_ref, qk_ref, gl_ref, z_ref, g_ref = refs
    wide = [_split_heads(r[...]) for r in (u_ref, kc_ref, qd_ref, kd_ref)]
    gl = gl_ref[0]
    return [(wide[0][h], wide[1][h], wide[2][h], wide[3][h], qk_ref[0, h], _col(_row(gl, h), 0)) for h in range(DN_HEADS)], \
        _split_heads(z_ref[...]), g_ref[...]


def dn_scan_fwd(name, local, pm, gain):
    s = pm.shape[0]
    n_c = s // DN_CHUNK

    def body(*refs):
        y_ref, hist_ref, state = refs[8:]

        @pl.when(pl.program_id(0) == 0)
        def _():
            state[...] = jnp.zeros_like(state)

        hist_ref[0] = state[...]
        per_head, z4, gain_v = _dn_scan_tiles(refs[:8])
        ys, s_nexts = _dn_step(False, [state[h] for h in range(DN_HEADS)], per_head, z4, gain_v)
        for h in range(DN_HEADS):
            state[h] = s_nexts[h]
        y_ref[...] = jnp.concatenate(ys, axis=1).astype(y_ref.dtype)

    return pl.pallas_call(
        body, grid=(n_c,), in_specs=_dn_scan_specs(n_c, False),
        out_specs=[pl.BlockSpec((DN_CHUNK, BRANCH), lambda j: (j, 0)),
                   pl.BlockSpec((1, DN_HEADS, DN_DH, DN_DH), lambda j: (j, 0, 0, 0))],
        out_shape=[jax.ShapeDtypeStruct((s, BRANCH), BF16), jax.ShapeDtypeStruct((n_c, DN_HEADS, DN_DH, DN_DH), F32)],
        scratch_shapes=[pltpu.VMEM((DN_HEADS, DN_DH, DN_DH), F32)],
        name=name, compiler_params=_cparams(1),
    )(*local, pm, gain)


def dn_scan_bwd(name, local, pm, gain, hist, dy):
    s = pm.shape[0]
    n_c = s // DN_CHUNK

    def body(*refs):
        hist_ref, dy_ref = refs[8:10]
        du_ref, dkc_ref, dqd_ref, dkd_ref, dqk_ref, dgl_ref, dz_ref, dg_ref, d_state = refs[10:]
        first = pl.program_id(0) == 0

        @pl.when(first)
        def _():
            d_state[...] = jnp.zeros_like(d_state)

        per_head, z4, gain_v = _dn_scan_tiles(refs[:8])
        _, vjp = jax.vjp(functools.partial(_dn_step, True), [hist_ref[0, h] for h in range(DN_HEADS)], per_head, z4, gain_v)
        d_s, grads, d_z, d_gain = vjp((_split_heads(dy_ref[...]), [d_state[h] for h in range(DN_HEADS)]))
        for h in range(DN_HEADS):
            d_state[h] = d_s[h]
        for ref, i in ((du_ref, 0), (dkc_ref, 1), (dqd_ref, 2), (dkd_ref, 3)):
            ref[...] = jnp.concatenate([g[i] for g in grads], axis=1)
        dz_ref[...] = jnp.concatenate(d_z, axis=1).astype(dz_ref.dtype)
        for h in range(DN_HEADS):
            dqk_ref[0, h] = grads[h][4]
        dgl_ref[0] = _head_rows([g[5] for g in grads])
        _store(dg_ref, d_gain, first)

    rev = lambda j: n_c - 1 - j
    specs = _dn_scan_specs(n_c, True)
    return pl.pallas_call(
        body, grid=(n_c,),
        in_specs=specs + [pl.BlockSpec((1, DN_HEADS, DN_DH, DN_DH), lambda j: (rev(j), 0, 0, 0)),
                          pl.BlockSpec((DN_CHUNK, BRANCH), lambda j: (rev(j), 0))],
        out_specs=specs[:6] + [pl.BlockSpec((DN_CHUNK, BRANCH), lambda j: (rev(j), 0)), specs[7]],
        out_shape=[jax.ShapeDtypeStruct((s, BRANCH), F32)] * 4 + [
            jax.ShapeDtypeStruct((n_c, DN_HEADS, DN_CHUNK, DN_CHUNK), F32), jax.ShapeDtypeStruct((n_c, 8, LANES), F32),
            jax.ShapeDtypeStruct((s, BRANCH), BF16), jax.ShapeDtypeStruct((1, DN_DH), F32)],
        scratch_shapes=[pltpu.VMEM((DN_HEADS, DN_DH, DN_DH), F32)],
        name=name, compiler_params=_cparams(1),
    )(*local, pm, gain, hist, dy)


def _seq_layouts(cols, s):
    return cols.T.reshape(cols.shape[1], s // LANES, LANES)


def layer_fwd(li, x, p, w):
    s = x.shape[0]
    n = lambda t: f"{t}_l{li}"
    h = rms_fwd(n("rms_mix"), x, w["g_mix"])
    pm = mm(n("in_main"), h, w["in_main"], "nn")
    ps = mm(n("in_small"), h, w["in_small"], "nn")
    qn, kn = fox_prep_fwd(n("fox_prep"), pm, w["gq"], w["gk"])
    f_t = _seq_layouts(ps[:, 0:8], s)
    cum = fox_gate_fwd(n("fox_gate"), f_t, w["b_f"])
    cum_c, cum_r = cum.reshape(8, s, 1), cum.reshape(8, 1, s)
    y_fox = fox_attn_fwd(n("fox_attn"), qn, kn, pm, cum_c, cum_r)
    y_sc = tile_fwd(n("sconv"), _sconv_fn, (BRANCH // LANES,), sconv_ops(pm, w["sc_conv_w"]), [_col_out(s, BRANCH, BF16)])[0]
    dn_act = tile_fwd(n("dnconv"), _dnconv_fn, (3 * BRANCH // LANES,), dnconv_ops(pm, w["dn_conv_w"]), [_col_out(s, 3 * BRANCH)])[0]
    a_rows = ps[:, 12:16].reshape(s // DN_CHUNK, DN_CHUNK, DN_HEADS).transpose(0, 2, 1)
    dn_local = dn_local_fwd(n("dn_local"), dn_act, ps, a_rows, w["ad"])
    y_dn, hist = dn_scan_fwd(n("dn_scan"), dn_local, pm, w["dn_gain"])
    ys = (y_fox, y_sc, y_dn)
    yp = [mm(n(f"branch{b}"), ys[b], w["branch"][b], "nn", blocks=(0, N_CHIPS)) for b in range(3)]
    merged = tile_fwd(n("merge"), _merge_fn, (s // 256,), merge_ops(yp, pm), [((s, D_MODEL), BF16, (256, D_MODEL), lambda i: (i, 0), ())])[0]
    x1 = mm(n("w_o"), merged, w["o"], "nn", add=x)
    h2 = rms_fwd(n("rms_ffn"), x1, w["g_ffn"])
    ug = mm(n("up_g"), h2, w["up"], "nn", blocks=(0, 2))
    uv = mm(n("up_v"), h2, w["up"], "nn", blocks=(2, 2))
    act = tile_fwd(n("ffn_act"), _ffn_act_fn, (D_FF // LANES,), ffn_ops(ug, uv, w["ffn_conv_w"]), [_col_out(s, D_FF, BF16)])[0]
    x2 = mm(n("down"), act, w["down"], "nn", add=x1)
    h3 = rms_fwd(n("rms_ple"), x2, w["g_ple"])
    gpre = mm(n("ple_gate"), h3, w["pg"], "nn")
    pe = mm(n("ple_emb"), p, w["ple"], "nn", blocks=(0, N_CHIPS))
    x3 = tile_fwd(n("ple"), _ple_fn, (s // 256,), ple_ops(gpre, pe, x2), [((s, D_MODEL), F32, (256, D_MODEL), lambda i: (i, 0), ())])[0]
    saved = dict(x=x, h=h, pm=pm, ps=ps, qn=qn, kn=kn, f_t=f_t, cum_c=cum_c, cum_r=cum_r, ys=ys, dn_act=dn_act, dn_local=dn_local,
                 a_rows=a_rows, hist=hist, yp=yp, merged=merged, x1=x1, h2=h2, ug=ug, uv=uv, act=act, x2=x2, h3=h3,
                 gpre=gpre, pe=pe, p=p)
    return x3, saved


def layer_bwd(li, dx3, sv, w):
    s = dx3.shape[0]
    n = lambda t: f"{t}_l{li}"
    g = {}
    col_own = lambda width: ((s, width), (s, LANES), lambda j: (0, j))
    d_gpre, d_pe = tile_bwd(n("ple_bwd"), _ple_fn, (s // 256,), ple_ops(sv["gpre"], sv["pe"], sv["x2"]), [_rows(dx3)],
                            [(0, (), None, BF16), (1, (), None, BF16)])
    g["w_ple"] = mm(n("d_w_ple"), sv["p"], d_pe, "tn", blocks=(0, N_CHIPS))
    g["w_ple_gate"] = mm(n("d_w_pg"), sv["h3"], d_gpre, "tn").reshape(N_CHIPS, -1, D_MODEL)
    dh3 = mm(n("d_h3"), d_gpre, w["pg"], "nt")
    dx2, d_g_ple = rms_bwd(n("rms_ple_bwd"), sv["x2"], w["g_ple"], dh3, dx3)
    dact = mm(n("d_act"), dx2, w["down"], "nt")
    g["w_down"] = mm(n("d_w_down"), sv["act"], dx2, "tn").reshape(N_CHIPS, -1, D_MODEL)
    taps_own = ((w["ffn_conv_w"].shape[0], D_FF), (w["ffn_conv_w"].shape[0], LANES), lambda j: (0, j))
    d_ug, d_uv, d_fw_g, d_fw_v = tile_bwd(n("ffn_act_bwd"), _ffn_act_fn, (D_FF // LANES,), ffn_ops(sv["ug"], sv["uv"], w["ffn_conv_w"]),
                                          [_col_cot(dact)], [(0, (), None, BF16), (1, (), None, BF16), (2, (), taps_own), (3, (), taps_own)])
    g["ffn_conv_w"] = jnp.concatenate([d_fw_g, d_fw_v], axis=1)
    g["w_up"] = jnp.concatenate([mm(n("d_w_up_g"), sv["h2"], d_ug, "tn", blocks=(0, 2)), mm(n("d_w_up_v"), sv["h2"], d_uv, "tn", blocks=(0, 2))])
    dh2 = mm(n("d_h2_v"), d_uv, w["up"], "nt", blocks=(2, 2), add=mm(n("d_h2_g"), d_ug, w["up"], "nt", blocks=(0, 2)))
    dx1, d_g_ffn = rms_bwd(n("rms_ffn_bwd"), sv["x1"], w["g_ffn"], dh2, dx2)
    dmerged = mm(n("d_merged"), dx1, w["o"], "nt")
    g["w_o"] = mm(n("d_w_o"), sv["merged"], dx1, "tn").reshape(N_CHIPS, -1, D_MODEL)
    gate_own = ((s, D_MODEL), (256, D_MODEL), lambda i: (i, 0))
    d_yp0, d_yp1, d_yp2, d_g0, d_g1, d_g2 = tile_bwd(
        n("merge_bwd"), _merge_fn, (s // 256,), merge_ops(sv["yp"], sv["pm"]), [_rows(dmerged)],
        [(0, (), None, BF16), (1, (), None, BF16), (2, (), None, BF16), (3, (), gate_own, BF16), (4, (), gate_own, BF16), (5, (), gate_own, BF16)])
    d_yp = (d_yp0, d_yp1, d_yp2)
    g["w_branch"] = jnp.concatenate([mm(n(f"d_w_branch{b}"), sv["ys"][b], d_yp[b], "tn", blocks=(0, N_CHIPS)) for b in range(3)], axis=1)
    d_ys = [mm(n(f"d_y{b}"), d_yp[b], w["branch"][b], "nt", blocks=(0, N_CHIPS)) for b in range(3)]
    *d_local, d_z, d_dngain = dn_scan_bwd(n("dn_scan_bwd"), sv["dn_local"], sv["pm"], w["dn_gain"], sv["hist"], d_ys[2])
    d_dnact, d_ps_dn, d_arows, d_ad = dn_local_bwd(n("dn_local_bwd"), sv["dn_act"], sv["ps"], sv["a_rows"], w["ad"], d_local)
    g["ad"], g["dn_norm_gain"] = d_ad, d_dngain[0]
    d_dnqkv, g["dn_conv_w"] = tile_bwd(n("dnconv_bwd"), _dnconv_fn, (3 * BRANCH // LANES,), dnconv_ops(sv["pm"], w["dn_conv_w"]),
                                       [_col_cot(d_dnact)], [(0, (), col_own(3 * BRANCH), BF16), (1, ())])
    d_sb, d_sc, d_sv, g["sc_conv_w"] = tile_bwd(n("sconv_bwd"), _sconv_fn, (BRANCH // LANES,), sconv_ops(sv["pm"], w["sc_conv_w"]), [_col_cot(d_ys[1])],
                                                [(0, (), col_own(BRANCH), BF16), (1, (), col_own(BRANCH), BF16), (2, (), col_own(BRANCH), BF16), (3, ())])
    d_qn, d_kn, d_fv, d_cum = fox_attn_bwd(n("fox_attn_bwd"), sv["qn"], sv["kn"], sv["pm"], sv["cum_c"], sv["cum_r"], d_ys[0])
    d_ft, d_bf = fox_gate_bwd(n("fox_gate_bwd"), sv["f_t"], w["b_f"], d_cum.reshape(8, s // LANES, LANES))
    g["b_fox_f"] = d_bf.reshape(8)
    d_fq, d_fk, d_gq, d_gk = fox_prep_bwd(n("fox_prep_bwd"), sv["pm"], w["gq"], w["gk"], d_qn, d_kn)
    g["fox_q_gain"] = d_gq[0, :FOX_DH] + d_gq[0, FOX_DH:]
    g["fox_k_gain"] = d_gk[0, :FOX_DH] + d_gk[0, FOX_DH:]
    d_pm = jnp.concatenate([d_fq, d_fk, d_fv.astype(BF16), d_sb, d_sc, d_sv, d_dnqkv, d_z, d_g0, d_g1, d_g2], axis=1)
    d_a_cols = d_arows.transpose(0, 2, 1).reshape(s, DN_HEADS)
    d_f_cols = d_ft.reshape(8, s).T
    d_ps = d_ps_dn + jnp.concatenate([d_f_cols, jnp.zeros((s, 4), F32), d_a_cols, jnp.zeros((s, LANES - 16), F32)], axis=1)
    g["w_in"] = chip_blocks_w_in(mm(n("d_w_in_main"), sv["h"], d_pm, "tn"), mm(n("d_w_in_small"), sv["h"], d_ps, "tn"))
    dh = mm(n("d_h_small"), d_ps, w["in_small"], "nt", add=mm(n("d_h_main"), d_pm, w["in_main"], "nt"))
    dx, d_g_mix = rms_bwd(n("rms_mix_bwd"), sv["x"], w["g_mix"], dh, dx1)
    g["g_mix"], g["g_ffn"], g["g_ple"] = d_g_mix[0], d_g_ffn[0], d_g_ple[0]
    return dx, g


IN_SHARD = 2052
MAIN_RANGES = ((0, 1536), (1544, 3080), (3080, 4616), (4624, 5136), (5136, 8208))
SMALL_RANGES = ((1536, 1544), (4616, 4620), (4620, 4624))


def _from_chip_blocks(blocks, ranges):
    parts = []
    for lo, hi in ranges:
        for k in range(N_CHIPS):
            a0, a1 = max(lo, k * IN_SHARD), min(hi, (k + 1) * IN_SHARD)
            if a0 < a1:
                parts.append(blocks[k][:, a0 - k * IN_SHARD:a1 - k * IN_SHARD])
    return parts


def split_w_in(blocks):
    main = jnp.concatenate(_from_chip_blocks(blocks, MAIN_RANGES), axis=1)
    pad = jnp.zeros((blocks.shape[1], LANES - 16), blocks.dtype)
    return main, jnp.concatenate(_from_chip_blocks(blocks, SMALL_RANGES) + [pad], axis=1)


def chip_blocks_w_in(main, small):
    pieces, m_off, s_off = [], 0, 0
    ranges = sorted([(lo, hi, "m") for lo, hi in MAIN_RANGES] + [(lo, hi, "s") for lo, hi in SMALL_RANGES])
    offs = {}
    for lo, hi in MAIN_RANGES:
        offs[lo] = m_off
        m_off += hi - lo
    for lo, hi in SMALL_RANGES:
        offs[lo] = s_off
        s_off += hi - lo
    blocks = []
    for k in range(N_CHIPS):
        parts = []
        for lo, hi, src in ranges:
            a0, a1 = max(lo, k * IN_SHARD), min(hi, (k + 1) * IN_SHARD)
            if a0 < a1:
                arr = main if src == "m" else small
                parts.append(arr[:, offs[lo] + a0 - lo:offs[lo] + a1 - lo])
        blocks.append(jnp.concatenate(parts, axis=1))
    return jnp.stack(blocks)


def layer_weights(li, got, conv, a):
    g_in, g_branch, g_o, g_up, g_down, g_pg, g_ple = got
    main, small = split_w_in(g_in)
    tile2 = lambda v: jnp.concatenate([v, v])[None, :]
    branch = g_branch.reshape(N_CHIPS, 3, BRANCH, -1)
    return dict(
        in_main=main, in_small=small, branch=[branch[:, b] for b in range(3)], o=g_o.reshape(D_MODEL, D_MODEL), up=g_up,
        down=g_down.reshape(D_FF, D_MODEL), pg=g_pg.reshape(D_MODEL, D_MODEL), ple=g_ple,
        g_mix=a["g_mix"][li][None, :], g_ffn=a["g_ffn"][li][None, :], g_ple=a["g_ple"][li][None, :],
        gq=tile2(a["fox_q_gain"][li]), gk=tile2(a["fox_k_gain"][li]), b_f=a["b_fox_f"][li].reshape(8, 1, 1),
        ad=jnp.stack([a["dn_a_log"][li], a["dn_dt_bias"][li]]), dn_gain=a["dn_norm_gain"][li][None, :],
        sc_conv_w=conv["sc_conv_w"][li], dn_conv_w=conv["dn_conv_w"][li], ffn_conv_w=conv["ffn_conv_w"][li])


def pack_rows(arrs, dtype):
    flat = jnp.concatenate([t.reshape(-1).astype(dtype) for t in arrs])
    pad = (-flat.shape[0]) % (8 * LANES)
    if pad:
        flat = jnp.concatenate([flat, jnp.zeros((pad,), dtype)])
    return flat.reshape(-1, LANES)


def unpack_rows(buf, shapes):
    flat = buf.reshape(-1)
    out, off = [], 0
    for shp in shapes:
        size = 1
        for dim in shp:
            size *= dim
        out.append(flat[off:off + size].reshape(shp))
        off += size
    return out


def chip_shard(t, axis, k):
    width = t.shape[axis] // N_CHIPS
    return lax.slice_in_dim(t, k * width, (k + 1) * width, axis=axis)


ANY = pl.BlockSpec(memory_space=pl.ANY)


def _position():
    x, y, c = lax.axis_index("x"), lax.axis_index("y"), lax.axis_index("c")
    return x, y, c, [(1 - x, y), (x, 1 - y), (1 - x, 1 - y)]


def gather_small(name, block):
    m_per, n = block.shape

    def body(x_ref, out_ref, send_sems, recv_sems, local_sem):
        x, y, c, chips = _position()
        me, sibling = (x, y, c), (x, y, 1 - c)

        def rows(px, py, pc):
            return out_ref.at[pl.ds((4 * px + 2 * py + pc) * m_per, m_per), :]

        def copy(k, blk, to, src=None):
            return pltpu.make_async_remote_copy(src_ref=rows(*blk) if src is None else src, dst_ref=rows(*blk),
                                                send_sem=send_sems.at[k], recv_sem=recv_sems.at[k], device_id=to, device_id_type=MESH)

        mine = pltpu.make_async_copy(x_ref, rows(*me), local_sem)
        mine.start()
        first = [copy(0, me, sibling, src=x_ref)] + [copy(1 + j, me, (*chip, c), src=x_ref) for j, chip in enumerate(chips)]
        for cp in first:
            cp.start()
        passed = [copy(4 + j, (*chip, c), sibling) for j, chip in enumerate(chips)]
        for j, chip in enumerate(chips):
            copy(1 + j, (*chip, c), me).wait_recv()
            passed[j].start()
        copy(0, sibling, me).wait_recv()
        for j, chip in enumerate(chips):
            copy(4 + j, (*chip, 1 - c), me).wait_recv()
        for cp in first + passed:
            cp.wait_send()
        mine.wait()

    return pl.pallas_call(
        body, out_shape=jax.ShapeDtypeStruct((8 * m_per, n), block.dtype),
        in_specs=[pl.BlockSpec(memory_space=pltpu.VMEM)], out_specs=pl.BlockSpec(memory_space=pltpu.VMEM),
        scratch_shapes=[pltpu.SemaphoreType.DMA((7,)), pltpu.SemaphoreType.DMA((7,)), pltpu.SemaphoreType.DMA],
        name=name, compiler_params=pltpu.CompilerParams(vmem_limit_bytes=VMEM_LIMIT),
    )(block)


def _sems(n):
    return [pltpu.SemaphoreType.DMA((n,)), pltpu.SemaphoreType.DMA((n,))]


def gather_layer(name, shards):
    n_w = len(shards)
    halves = [s.shape[0] // 2 for s in shards]

    def body(*refs):
        ins, outs = refs[:n_w], refs[n_w:2 * n_w]
        send_sems, recv_sems = refs[2 * n_w:]
        x, y, c, chips = _position()
        sibling = (x, y, 1 - c)

        def part(w, px, py, pc):
            return outs[w].at[2 * px + py, pl.ds(pc * halves[w], halves[w]), :]

        def copy(k, w, blk, to, src=None):
            return pltpu.make_async_remote_copy(src_ref=part(w, *blk) if src is None else src, dst_ref=part(w, *blk),
                                                send_sem=send_sems.at[k], recv_sem=recv_sems.at[k], device_id=to, device_id_type=MESH)

        pairs = [(w, j, chip) for w in range(n_w) for j, chip in enumerate(chips)]
        first = [copy(3 * w + j, w, (x, y, c), (*chip, c), src=ins[w].at[pl.ds(c * halves[w], halves[w]), :]) for w, j, chip in pairs]
        for cp in first:
            cp.start()
        passed = [copy(3 * n_w + 3 * w + j, w, (*chip, c), sibling) for w, j, chip in pairs]
        for (w, j, chip), fwd in zip(pairs, passed):
            copy(3 * w + j, w, (*chip, c), (x, y, c)).wait_recv()
            fwd.start()
        for w, j, chip in pairs:
            copy(3 * n_w + 3 * w + j, w, (*chip, 1 - c), (x, y, c)).wait_recv()
        for cp in first + passed:
            cp.wait_send()

    return pl.pallas_call(
        body, out_shape=[jax.ShapeDtypeStruct((N_CHIPS,) + s.shape, s.dtype) for s in shards],
        in_specs=[ANY] * n_w, out_specs=[ANY] * n_w, scratch_shapes=_sems(6 * n_w), name=name,
    )(*shards)


def swap_halves(name, grads):
    n_w = len(grads)
    halves = [g.shape[1] // 2 for g in grads]

    def body(*refs):
        ins, outs = refs[:n_w], refs[n_w:2 * n_w]
        send_sems, recv_sems = refs[2 * n_w:]
        x, y, c, _ = _position()
        cps = [pltpu.make_async_remote_copy(src_ref=ins[w].at[:, pl.ds((1 - c) * halves[w], halves[w]), :], dst_ref=outs[w],
                                            send_sem=send_sems.at[w], recv_sem=recv_sems.at[w], device_id=(x, y, 1 - c),
                                            device_id_type=MESH) for w in range(n_w)]
        for cp in cps:
            cp.start()
        for cp in cps:
            cp.wait()

    return pl.pallas_call(
        body, out_shape=[jax.ShapeDtypeStruct((N_CHIPS, h, g.shape[2]), g.dtype) for g, h in zip(grads, halves)],
        in_specs=[ANY] * n_w, out_specs=[ANY] * n_w, scratch_shapes=_sems(n_w), name=name,
    )(*grads)


def scatter_chips(name, partials):
    n_w = len(partials)

    def body(*refs):
        ins, outs = refs[:n_w], refs[n_w:2 * n_w]
        send_sems, recv_sems = refs[2 * n_w:]
        x, y, c, chips = _position()
        cps = [pltpu.make_async_remote_copy(src_ref=ins[w].at[2 * cx + cy], dst_ref=outs[w].at[j], send_sem=send_sems.at[3 * w + j],
                                            recv_sem=recv_sems.at[3 * w + j], device_id=(cx, cy, c), device_id_type=MESH)
               for w in range(n_w) for j, (cx, cy) in enumerate(chips)]
        for cp in cps:
            cp.start()
        for cp in cps:
            cp.wait()

    return pl.pallas_call(
        body, out_shape=[jax.ShapeDtypeStruct((3,) + p.shape[1:], p.dtype) for p in partials],
        in_specs=[ANY] * n_w, out_specs=[ANY] * n_w, scratch_shapes=_sems(3 * n_w), name=name,
    )(*partials)


def share_halves(name, bufs):
    n_w = len(bufs)
    halves = [b.shape[0] // 2 for b in bufs]

    def body(*refs):
        outs = refs[n_w:2 * n_w]
        send_sems, recv_sems = refs[2 * n_w:]
        x, y, c, _ = _position()

        def copy(w, pc):
            half = outs[w].at[pl.ds(pc * halves[w], halves[w]), :]
            return pltpu.make_async_remote_copy(src_ref=half, dst_ref=half, send_sem=send_sems.at[w], recv_sem=recv_sems.at[w],
                                                device_id=(x, y, 1 - c), device_id_type=MESH)

        for w in range(n_w):
            copy(w, c).start()
        for w in range(n_w):
            copy(w, 1 - c).wait_recv()
            copy(w, c).wait_send()

    return pl.pallas_call(
        body, out_shape=[jax.ShapeDtypeStruct(b.shape, b.dtype) for b in bufs], in_specs=[ANY] * n_w, out_specs=[ANY] * n_w,
        input_output_aliases={w: w for w in range(n_w)}, scratch_shapes=_sems(n_w), name=name,
    )(*bufs)


def _row_tile(rows, cols):
    best = 16
    for t in range(16, rows + 1, 16):
        if rows % t == 0 and t * cols * 4 <= 1024 * 1024:
            best = t
    return best


def pair_sum(name, pos, grad, from_sibling):
    _, rows, cols = grad.shape
    half = rows // 2
    tr = _row_tile(half, cols)
    n_t = half // tr

    def body(pos_ref, g_ref, s_ref, b_ref, f_ref):
        tot = g_ref[...] + s_ref[...]
        b_ref[...] = tot.astype(BF16)

        @pl.when(pl.program_id(1) == pos_ref[1])
        def _():
            f_ref[...] = tot[0]

    blk = pl.BlockSpec((1, tr, cols), lambda i, k, pos: (k, i, 0))
    return pl.pallas_call(
        body, grid_spec=pltpu.PrefetchScalarGridSpec(
            num_scalar_prefetch=1, grid=(n_t, N_CHIPS),
            in_specs=[pl.BlockSpec((1, tr, cols), lambda i, k, pos: (k, pos[0] * n_t + i, 0)), blk],
            out_specs=[blk, pl.BlockSpec((tr, cols), lambda i, k, pos: (i, 0))]),
        out_shape=[jax.ShapeDtypeStruct((N_CHIPS, half, cols), BF16), jax.ShapeDtypeStruct((half, cols), F32)],
        name=name, compiler_params=_cparams(2),
    )(pos, grad, from_sibling)


def chip_sum(name, pos, own, landed):
    half, cols = own.shape
    tr = _row_tile(half, cols)
    n_t = half // tr

    def body(pos_ref, p_ref, l_ref, o_ref):
        o_ref[...] = ((p_ref[...] + l_ref[0].astype(F32)) + l_ref[1].astype(F32)) + l_ref[2].astype(F32)

    return pl.pallas_call(
        body, grid_spec=pltpu.PrefetchScalarGridSpec(
            num_scalar_prefetch=1, grid=(n_t,),
            in_specs=[pl.BlockSpec((tr, cols), lambda i, pos: (i, 0)), pl.BlockSpec((3, tr, cols), lambda i, pos: (0, i, 0))],
            out_specs=pl.BlockSpec((tr, cols), lambda i, pos: (pos[0] * n_t + i, 0))),
        out_shape=jax.ShapeDtypeStruct((2 * half, cols), F32), name=name, compiler_params=_cparams(1),
    )(pos, own, landed)


def reduce_scatter_layer(li, pos, grads):
    n = lambda t: f"{t}_l{li}"
    from_sibling = swap_halves(n("swap_halves"), grads)
    sums = [pair_sum(n(f"pair_sum{w}"), pos, g, s) for w, (g, s) in enumerate(zip(grads, from_sibling))]
    landed = scatter_chips(n("scatter_chips"), [b for b, _ in sums])
    halves = [chip_sum(n(f"chip_sum{w}"), pos, own, l) for w, ((_, own), l) in enumerate(zip(sums, landed))]
    return share_halves(n("share_halves"), halves)


def sum_devices(gathered):
    m_per = gathered.shape[0] // 8

    def body(g_ref, o_ref):
        tot = g_ref[pl.ds(0, m_per), :]
        for dev in range(1, 8):
            tot = tot + g_ref[pl.ds(dev * m_per, m_per), :]
        o_ref[...] = tot

    return pl.pallas_call(
        body, out_shape=jax.ShapeDtypeStruct((m_per, gathered.shape[1]), F32),
        in_specs=[pl.BlockSpec(memory_space=pltpu.VMEM)], out_specs=pl.BlockSpec(memory_space=pltpu.VMEM), name="sum_devices",
    )(gathered)


def kernel(x, p, g_mix, w_in, b_fox_f, fox_q_gain, fox_k_gain, sc_conv_w, dn_conv_w, dn_a_log, dn_dt_bias, dn_norm_gain, w_branch, w_o, g_ffn, w_up, ffn_conv_w, w_down, g_ple, w_ple_gate, w_ple, loss_target, m_g_mix, m_w_in, m_b_fox_f, m_fox_q_gain, m_fox_k_gain, m_sc_conv_w, m_dn_conv_w, m_dn_a_log, m_dn_dt_bias, m_dn_norm_gain, m_w_branch, m_w_o, m_g_ffn, m_w_up, m_ffn_conv_w, m_w_down, m_g_ple, m_w_ple_gate, m_w_ple, v_g_mix, v_w_in, v_b_fox_f, v_fox_q_gain, v_fox_k_gain, v_sc_conv_w, v_dn_conv_w, v_dn_a_log, v_dn_dt_bias, v_dn_norm_gain, v_w_branch, v_w_o, v_g_ffn, v_w_up, v_ffn_conv_w, v_w_down, v_g_ple, v_w_ple_gate, v_w_ple):
    a = dict(g_mix=g_mix, w_in=w_in, b_fox_f=b_fox_f, fox_q_gain=fox_q_gain, fox_k_gain=fox_k_gain, sc_conv_w=sc_conv_w,
             dn_conv_w=dn_conv_w, dn_a_log=dn_a_log, dn_dt_bias=dn_dt_bias, dn_norm_gain=dn_norm_gain, w_branch=w_branch, w_o=w_o,
             g_ffn=g_ffn, w_up=w_up, ffn_conv_w=ffn_conv_w, w_down=w_down, g_ple=g_ple, w_ple_gate=w_ple_gate, w_ple=w_ple)
    mom = dict(g_mix=m_g_mix, w_in=m_w_in, b_fox_f=m_b_fox_f, fox_q_gain=m_fox_q_gain, fox_k_gain=m_fox_k_gain, sc_conv_w=m_sc_conv_w,
               dn_conv_w=m_dn_conv_w, dn_a_log=m_dn_a_log, dn_dt_bias=m_dn_dt_bias, dn_norm_gain=m_dn_norm_gain, w_branch=m_w_branch,
               w_o=m_w_o, g_ffn=m_g_ffn, w_up=m_w_up, ffn_conv_w=m_ffn_conv_w, w_down=m_w_down, g_ple=m_g_ple, w_ple_gate=m_w_ple_gate,
               w_ple=m_w_ple)
    var = dict(g_mix=v_g_mix, w_in=v_w_in, b_fox_f=v_b_fox_f, fox_q_gain=v_fox_q_gain, fox_k_gain=v_fox_k_gain, sc_conv_w=v_sc_conv_w,
               dn_conv_w=v_dn_conv_w, dn_a_log=v_dn_a_log, dn_dt_bias=v_dn_dt_bias, dn_norm_gain=v_dn_norm_gain, w_branch=v_w_branch,
               w_o=v_w_o, g_ffn=v_g_ffn, w_up=v_w_up, ffn_conv_w=v_ffn_conv_w, w_down=v_w_down, g_ple=v_g_ple, w_ple_gate=v_w_ple_gate,
               w_ple=v_w_ple)
    cx, cy, cc = lax.axis_index("x"), lax.axis_index("y"), lax.axis_index("c")
    chip = 2 * cx + cy
    pos = jnp.stack([cc, chip]).astype(jnp.int32)

    def as_blocks(t):
        return t.reshape(2, -1, t.shape[-1])

    gathered = []
    for li in range(2):
        shards = [as_blocks(a[nm])[li].astype(BF16) for nm in BIG]
        got = gather_layer(f"gather_weights_l{li}", shards)
        gathered.append([lax.dynamic_update_slice(g, s[None], (chip, 0, 0)) for g, s in zip(got, shards)])
    conv_shapes = [a[nm].shape for nm in CONVS]
    conv_all = gather_small("gather_conv_w", pack_rows([a[nm] for nm in CONVS], F32))
    conv_rows = conv_all.shape[0] // 8
    conv_chip = [unpack_rows(conv_all[2 * k * conv_rows:(2 * k + 1) * conv_rows], conv_shapes) for k in range(N_CHIPS)]
    conv = {nm: jnp.concatenate([conv_chip[k][i] for k in range(N_CHIPS)], axis=2) for i, nm in enumerate(CONVS)}

    act = x[0]
    weights, saved = [], []
    for li in range(2):
        weights.append(layer_weights(li, gathered[li], conv, a))
        act, sv = layer_fwd(li, act, p[li, 0], weights[li])
        saved.append(sv)
    d_act, loss_part = loss_call(act, loss_target[0])
    loss = lax.psum(loss_part, ("x", "y", "c"))
    layer_grads, reduced = [None, None], [None, None]
    for li in (1, 0):
        d_act, layer_grads[li] = layer_bwd(li, d_act, saved[li], weights[li])
        reduced[li] = reduce_scatter_layer(li, pos, [layer_grads[li][nm] for nm in BIG])
    grad_x = d_act[None]

    def both(nm):
        return jnp.stack([layer_grads[0][nm], layer_grads[1][nm]])

    local = {nm: both(nm) for nm in ("g_mix", "b_fox_f", "fox_q_gain", "fox_k_gain", "dn_norm_gain", "g_ffn", "g_ple", "sc_conv_w",
                                      "dn_conv_w", "ffn_conv_w")}
    local["dn_a_log"] = jnp.stack([layer_grads[li]["ad"][0] for li in range(2)])
    local["dn_dt_bias"] = jnp.stack([layer_grads[li]["ad"][1] for li in range(2)])

    small_names = SMALL + CONVS
    small_shapes = [local[nm].shape for nm in small_names]
    small_sum = sum_devices(gather_small("gather_small_grads", pack_rows([local[nm] for nm in small_names], F32)))
    small_grads = dict(zip(small_names, unpack_rows(small_sum, small_shapes)))
    for nm in CONVS:
        width = a[nm].shape[2]
        small_grads[nm] = lax.dynamic_slice_in_dim(small_grads[nm], chip * width, width, axis=2)

    grads, deltas, new_m, new_v = dict(small_grads), {}, {}, {}
    for nm in small_names:
        deltas[nm], new_m[nm], new_v[nm] = adam_call(f"adam_{nm}", a[nm], grads[nm], mom[nm], var[nm])
    for i, nm in enumerate(BIG):
        res = adam_layers(f"adam_{nm}", as_blocks(a[nm]), as_blocks(mom[nm]), as_blocks(var[nm]), reduced[0][i], reduced[1][i])
        grads[nm], deltas[nm], new_m[nm], new_v[nm] = [r.reshape(a[nm].shape) for r in res]
    return (loss, grad_x, *[grads[nm] for nm in WEIGHTS], *[deltas[nm] for nm in WEIGHTS], *[new_m[nm] for nm in WEIGHTS],
            *[new_v[nm] for nm in WEIGHTS])
```

```python
import functools

import jax
import jax.numpy as jnp
from jax import lax
from jax.experimental import pallas as pl
from jax.experimental.pallas import tpu as pltpu

F32 = jnp.float32
BF16 = jnp.bfloat16
HI = lax.Precision.HIGHEST
MESH = pl.DeviceIdType.MESH

D_MODEL = 1024
BRANCH = 512
FOX_DH = 64
DN_DH = 128
DN_HEADS = 4
DN_CHUNK = 64
FOX_BLOCK = 128
D_FF = 2816
EPS = 1e-6
N_CHIPS = 4
LANES = 128

ADAM_LR, ADAM_B1, ADAM_B2, ADAM_EPS, ADAM_WD, ADAM_STEP = 0.001, 0.9, 0.999, 1e-08, 0.01, 10

VMEM_LIMIT = 56 * 1024 * 1024

C_FQ, C_FK, C_FV, C_SB, C_SC, C_SV, C_DN, C_DZ, C_GATE = 0, 512, 1024, 1536, 2048, 2560, 3072, 4608, 5120
IN_MAIN = 8192
IN_SIZES = (1536, 8, 1536, 1536, 4, 4, 512, 3072)

BIG = ("w_in", "w_branch", "w_o", "w_up", "w_down", "w_ple_gate", "w_ple")
BIG_AXIS = {"w_in": 2, "w_branch": 3, "w_o": 1, "w_up": 2, "w_down": 1, "w_ple_gate": 1, "w_ple": 2}
CONVS = ("sc_conv_w", "dn_conv_w", "ffn_conv_w")
SMALL = ("g_mix", "b_fox_f", "fox_q_gain", "fox_k_gain", "dn_a_log", "dn_dt_bias", "dn_norm_gain", "g_ffn", "g_ple")
WEIGHTS = ("g_mix", "w_in", "b_fox_f", "fox_q_gain", "fox_k_gain", "sc_conv_w", "dn_conv_w", "dn_a_log", "dn_dt_bias",
           "dn_norm_gain", "w_branch", "w_o", "g_ffn", "w_up", "ffn_conv_w", "w_down", "g_ple", "w_ple_gate", "w_ple")


def _iota(shape, dim):
    return lax.broadcasted_iota(jnp.int32, shape, dim)


def _dg(a, b, mode, prec=None):
    dims = {"nn": ((1,), (0,)), "nt": ((1,), (1,)), "tn": ((0,), (0,))}[mode]
    return lax.dot_general(a, b, (dims, ((), ())), precision=prec, preferred_element_type=F32)


def _bdot_impl(a, b, mode):
    return _dg(a.astype(BF16), b.astype(BF16), mode)


@functools.partial(jax.custom_vjp, nondiff_argnums=(2,))
def _bdot_diff(a, b, mode):
    return _bdot_impl(a, b, mode)


def _bdot_fwd(a, b, mode):
    return _bdot_impl(a, b, mode), (a, b)


def _bdot_bwd(mode, res, g):
    a, b = res
    if mode == "nn":
        da, db = _bdot_impl(g, b, "nt"), _bdot_impl(a, g, "tn")
    elif mode == "nt":
        da, db = _bdot_impl(g, b, "nn"), _bdot_impl(g, a, "tn")
    else:
        da, db = _bdot_impl(b, g, "nt"), _bdot_impl(a, g, "nn")
    return da.astype(a.dtype), db.astype(b.dtype)


_bdot_diff.defvjp(_bdot_fwd, _bdot_bwd)


def _bdot(d):
    return _bdot_diff if d else _bdot_impl


def _shift_impl(x, k):
    return jnp.where(_iota(x.shape, 0) >= k, pltpu.roll(x, k, 0), 0.0)


def _unshift_impl(g, k):
    n = g.shape[0]
    return jnp.where(_iota(g.shape, 0) < n - k, pltpu.roll(g, n - k, 0), 0.0)


@functools.partial(jax.custom_vjp, nondiff_argnums=(1,))
def _shift_diff(x, k):
    return _shift_impl(x, k)


_shift_diff.defvjp(lambda x, k: (_shift_impl(x, k), None), lambda k, _, g: (_unshift_impl(g, k),))


def _row(w, j):
    return jnp.sum(jnp.where(_iota(w.shape, 0) == j, w, 0.0), axis=0, keepdims=True)


def _col(w, j):
    return jnp.sum(jnp.where(_iota(w.shape, 1) == j, w, 0.0), axis=1, keepdims=True)


def _conv(d, x, w):
    shift = _shift_diff if d else _shift_impl
    taps = w.shape[0]
    y = x * _row(w, taps - 1)
    for j in range(taps - 1):
        y = y + shift(x, taps - 1 - j) * _row(w, j)
    return y


def _softplus(x):
    return jnp.maximum(x, 0.0) + jnp.log(1.0 + jnp.exp(-jnp.abs(x)))


def _silu(x):
    return x * jax.nn.sigmoid(x)


def _rms(x, gain):
    return x * lax.rsqrt(jnp.mean(x * x, axis=-1, keepdims=True) + EPS) * gain


def _rms_fn(d, pids, x, gain):
    return (_rms(x, gain),)


def _loss_fn(d, pids, y, t):
    e = y - t
    part = 0.5 / D_MODEL * jnp.sum(e * e, keepdims=True)
    return e * (1.0 / D_MODEL), jnp.broadcast_to(part, (8, LANES))


def _fox_prep_fn(d, pids, q, k, gq, gk):
    first = _iota(q.shape, 1) < FOX_DH

    def norm(x, gain):
        sq = x * x
        ss_a = jnp.sum(jnp.where(first, sq, 0.0), axis=1, keepdims=True)
        ss_b = jnp.sum(jnp.where(first, 0.0, sq), axis=1, keepdims=True)
        rs = jnp.where(first, lax.rsqrt(ss_a / FOX_DH + EPS), lax.rsqrt(ss_b / FOX_DH + EPS))
        return x * rs * gain

    return norm(q, gq) * FOX_DH ** -0.5, norm(k, gk)


def _fox_gate_fn(d, pids, f, bias):
    logf = -_softplus(-(f + bias))
    n_r, n_c = logf.shape
    tri = (_iota((n_c, n_c), 0) <= _iota((n_c, n_c), 1)).astype(F32)
    within = _dg(logf, tri, "nn", HI)
    tot = jnp.broadcast_to(jnp.sum(logf, axis=1, keepdims=True), logf.shape)
    below = (_iota((n_r, n_r), 1) < _iota((n_r, n_r), 0)).astype(F32)
    return (within + _dg(below, tot, "nn", HI),)


def _fox_attn_fn(d, pids, q, k, v, cq_a, cq_b, ck_a, ck_b):
    dot = _bdot(d)
    first = _iota(q.shape, 1) < FOX_DH
    n_q, n_k = q.shape[0], k.shape[0]
    causal = (pids[1] * n_q + _iota((n_q, n_k), 0)) >= _iota((n_q, n_k), 1)

    qs = [jnp.where(first, q, 0.0), jnp.where(first, 0.0, q)]
    s = _each(lambda qh, cq, ck: jnp.where(causal, dot(qh, k, "nt") + cq - ck, -1e30), qs, [cq_a, cq_b], [ck_a, ck_b])
    e = [jnp.exp(si - lax.stop_gradient(jnp.max(si, axis=1, keepdims=True))) for si in s]
    o_a, o_b = [dot(ei / jnp.sum(ei, axis=1, keepdims=True), v, "nn") for ei in e]
    return (jnp.where(first, o_a, o_b),)


def _sconv_fn(d, pids, sb, sc, sv, w):
    return (sb * _conv(d, sc * sv, w),)


def _dnconv_fn(d, pids, x, w):
    return (_silu(_conv(d, x, w)),)


def _merge_fn(d, pids, y0, y1, y2, g0, g1, g2):
    return (jax.nn.sigmoid(g0) * y0 + jax.nn.sigmoid(g1) * y1 + jax.nn.sigmoid(g2) * y2,)


def _ffn_act_fn(d, pids, ug, uv, wg, wv):
    return (_silu(_conv(d, ug, wg)) * _conv(d, uv, wv),)


def _ple_fn(d, pids, gpre, pe, x):
    return (x + jax.nn.sigmoid(gpre) * pe,)


def _adam_fn(d, pids, w, g, m, v):
    m2 = ADAM_B1 * m + (1.0 - ADAM_B1) * g
    v2 = ADAM_B2 * v + (1.0 - ADAM_B2) * (g * g)
    m_hat = m2 / (1.0 - ADAM_B1 ** ADAM_STEP)
    v_hat = v2 / (1.0 - ADAM_B2 ** ADAM_STEP)
    delta = -ADAM_LR * (m_hat / (jnp.sqrt(v_hat) + ADAM_EPS) + ADAM_WD * w)
    return delta, m2, v2


def _each(fn, *lists):
    return [fn(*args) for args in zip(*lists)]


def _tri_inv_impl(mats):
    n = mats[0].shape[0]
    r, c = _iota((n, n), 0), _iota((n, n), 1)
    diag_blk = (r >> 4) == (c >> 4)
    eye = (r == c).astype(F32)
    mm = lambda us, ws: _each(lambda u, w: _dg(u, w, "nn", HI), us, ws)
    grow = lambda ps, xs: _each(lambda p, px: p + px, ps, mm(ps, xs))
    x = [jnp.where(diag_blk, -a, 0.0) for a in mats]
    p = [eye + xi for xi in x]
    x2 = mm(x, x)
    p = grow(p, x2)
    x4 = mm(x2, x2)
    p = grow(p, x4)
    p = grow(p, mm(x4, x4))
    y = [-yi for yi in mm(p, [jnp.where(diag_blk, 0.0, a) for a in mats])]
    q = grow([eye + yi for yi in y], mm(y, y))
    return mm(q, p)


@jax.custom_vjp
def _tri_inv_diff(mats):
    return _tri_inv_impl(mats)


def _tri_inv_fwd(mats):
    ts = _tri_inv_impl(mats)
    return ts, ts


def _tri_inv_bwd(ts, gs):
    left = _each(lambda t, g: _dg(t, g, "tn", HI), ts, gs)
    return ([-m for m in _each(lambda l, t: _dg(l, t, "nt", HI), left, ts)],)


_tri_inv_diff.defvjp(_tri_inv_fwd, _tri_inv_bwd)


def _dn_local(d, qs, ks, vs, a_cs, a_rs, b_cs, a_logs, dt_bs):
    dot = _bdot(d)
    inv = _tri_inv_diff if d else _tri_inv_impl
    n = qs[0].shape[0]
    r, c = _iota((n, n), 0), _iota((n, n), 1)
    incl, strict, upper = r >= c, r > c, r <= c
    qs = [q * lax.rsqrt(jnp.sum(q * q, axis=1, keepdims=True) + EPS) * DN_DH ** -0.5 for q in qs]
    ks = [k * lax.rsqrt(jnp.sum(k * k, axis=1, keepdims=True) + EPS) for k in ks]
    betas = [jax.nn.sigmoid(b) for b in b_cs]
    rates = [-jnp.exp(a) for a in a_logs]
    g_cs = _each(lambda rate, a, dt: rate * _softplus(a + dt), rates, a_cs, dt_bs)
    g_rs = _each(lambda rate, a, dt: rate * _softplus(a + dt), rates, a_rs, dt_bs)
    gcum_cs = [jnp.sum(jnp.where(incl, g, 0.0), axis=1, keepdims=True) for g in g_rs]
    gcum_rs = [jnp.sum(jnp.where(upper, g, 0.0), axis=0, keepdims=True) for g in g_cs]
    decays = _each(lambda gc, gr: jnp.exp(jnp.where(incl, gc - gr, -1e30)), gcum_cs, gcum_rs)
    kbs = _each(lambda k, b: k * b, ks, betas)
    kk = _each(lambda kb, k: dot(kb, k, "nt"), kbs, ks)
    ts = inv(_each(lambda m, dec: jnp.where(strict, m * dec, 0.0), kk, decays))
    e_gs = [jnp.exp(g) for g in gcum_cs]
    us = _each(lambda t, v, b: _dg(t, v * b, "nn", HI), ts, vs, betas)
    k_cums = _each(lambda t, kb, e: _dg(t, kb * e, "nn", HI), ts, kbs, e_gs)
    qk = _each(lambda q, k: dot(q, k, "nt"), qs, ks)
    qk = _each(lambda m, dec: jnp.where(incl, m * dec, 0.0), qk, decays)
    g_lasts = [jnp.sum(g, axis=0, keepdims=True) for g in g_cs]
    q_decs = _each(lambda q, e: q * e, qs, e_gs)
    k_decs = _each(lambda k, gl, gc: k * jnp.exp(gl - gc), ks, g_lasts, gcum_cs)
    return list(zip(us, k_cums, q_decs, k_decs, qk, g_lasts))


def _dn_step(d, s_prevs, items, zs, gain):
    dot = _bdot(d)
    us, k_cums, q_decs, k_decs, qks, g_lasts = [list(t) for t in zip(*items)]
    v_news = _each(lambda u, kc, s: u - dot(kc, s, "nn"), us, k_cums, s_prevs)
    inter = _each(lambda qd, s: dot(qd, s, "nn"), q_decs, s_prevs)
    outs = _each(lambda o, qk, vn: o + dot(qk, vn, "nn"), inter, qks, v_news)
    s_nexts = _each(lambda s, gl, kd, vn: s * jnp.exp(gl) + dot(kd, vn, "tn"), s_prevs, g_lasts, k_decs, v_news)
    return _each(lambda o, z: _rms(o, gain) * _silu(z), outs, zs), s_nexts


def _split_heads(t):
    return [t[:, h * DN_DH:(h + 1) * DN_DH] for h in range(t.shape[1] // DN_DH)]


def _dn_gates(ps, a_rows, ad):
    hs = range(DN_HEADS)
    return ([_col(ps, 12 + h) for h in hs], [_row(a_rows, h) for h in hs], [_col(ps, 8 + h) for h in hs],
            [_col(_row(ad, 0), h) for h in hs], [_col(_row(ad, 1), h) for h in hs])


def _head_rows(vals):
    row = _iota((8, LANES), 0)
    tile = jnp.zeros((8, LANES), F32)
    for h, val in enumerate(vals):
        tile = tile + jnp.where(row == h, val, 0.0)
    return tile


def _cparams(n_axes):
    return pltpu.CompilerParams(dimension_semantics=("arbitrary",) * n_axes, vmem_limit_bytes=VMEM_LIMIT)


def _first_visit(acc_axes):
    cond = None
    for a in acc_axes:
        here = pl.program_id(a) == 0
        cond = here if cond is None else jnp.logical_and(cond, here)
    return cond


def _tile(ref):
    val = ref[...]
    shape = val.shape
    while len(shape) > 2 and shape[0] == 1:
        shape = shape[1:]
    return val.reshape(shape)


def _store(ref, val, first):
    val = val.astype(ref.dtype).reshape(ref.shape)
    if first is None:
        ref[...] = val
        return

    @pl.when(first)
    def _():
        ref[...] = val

    @pl.when(jnp.logical_not(first))
    def _():
        ref[...] += val


def _specs(ops):
    return [pl.BlockSpec(block, imap) for _, block, imap in ops]


def tile_fwd(name, fn, grid, ins, outs):
    n_in = len(ins)

    def body(*refs):
        pids = tuple(pl.program_id(a) for a in range(len(grid)))
        firsts = [_first_visit(o[4]) if o[4] else None for o in outs]
        res = fn(False, pids, *[_tile(r) for r in refs[:n_in]])
        for ref, val, first in zip(refs[n_in:], res, firsts):
            _store(ref, val, first)

    out = pl.pallas_call(
        body, grid=grid, in_specs=_specs(ins),
        out_specs=[pl.BlockSpec(o[2], o[3]) for o in outs],
        out_shape=[jax.ShapeDtypeStruct(o[0], o[1]) for o in outs],
        name=name, compiler_params=_cparams(len(grid)),
    )(*[a for a, _, _ in ins])
    return out


def tile_bwd(name, fn, grid, ins, cots, diff, adds=None):
    adds = adds or {}
    n_in, n_cot = len(ins), len(cots)
    add_pos = sorted(adds)
    diff_idx = [d[0] for d in diff]
    out_desc = [d[2] if len(d) > 2 and d[2] is not None else (ins[d[0]][0].shape, ins[d[0]][1], ins[d[0]][2]) for d in diff]
    out_dtypes = [d[3] if len(d) > 3 else F32 for d in diff]

    def body(*refs):
        pids = tuple(pl.program_id(a) for a in range(len(grid)))
        firsts = [_first_visit(d[1]) if d[1] else None for d in diff]
        vals = [_tile(r) for r in refs[:n_in]]
        cot_vals = [_tile(r) for r in refs[n_in:n_in + n_cot]]
        add_vals = [_tile(r) for r in refs[n_in + n_cot:n_in + n_cot + len(add_pos)]]
        out_refs = refs[n_in + n_cot + len(add_pos):]

        def f(*dv):
            full = list(vals)
            for i, val in zip(diff_idx, dv):
                full[i] = val
            return fn(True, pids, *full)

        prim, vjp = jax.vjp(f, *[vals[i].astype(F32) for i in diff_idx])
        grads = list(vjp(tuple(c.astype(o.dtype) for c, o in zip(cot_vals, prim))))
        for pos, val in zip(add_pos, add_vals):
            grads[pos] = grads[pos] + val.astype(F32)
        for ref, val, first in zip(out_refs, grads, firsts):
            _store(ref, val, first)

    all_ins = list(ins) + list(cots) + [adds[p] for p in add_pos]
    out = pl.pallas_call(
        body, grid=grid, in_specs=_specs(all_ins),
        out_specs=[pl.BlockSpec(o[1], o[2]) for o in out_desc],
        out_shape=[jax.ShapeDtypeStruct(o[0], dt) for o, dt in zip(out_desc, out_dtypes)],
        name=name, compiler_params=_cparams(len(grid)),
    )(*[a for a, _, _ in all_ins])
    return out


def _pick(dim, cands):
    for c in cands:
        if dim % c == 0:
            return c
    return dim


MM_TILES = (1024, 512, 256, 128)


def mm(name, a, b, mode, add=None, out_dtype=F32, blocks=None):
    wide = None
    if mode == "nn":
        (m, kk), n = a.shape, b.shape[-1]
    elif mode == "nt":
        (m, kk), n = a.shape, b.shape[-2]
    else:
        (kk, m), n = a.shape, b.shape[1]
    if blocks is not None:
        lo, n_blk = blocks
        wide = b.shape[-1] if mode != "tn" else n // n_blk
        if mode == "nn":
            n = wide * n_blk
    tm = _pick(m, MM_TILES)
    if mode == "nt" and blocks is not None:
        tn, tk = _pick(n, MM_TILES), _pick(wide, MM_TILES[:-1])
    elif blocks is not None:
        tn, tk = _pick(wide, MM_TILES[:-1]), _pick(kk, MM_TILES)
    else:
        tn, tk = _pick(n, MM_TILES), _pick(kk, MM_TILES)
    nk = kk // tk
    a_spec = pl.BlockSpec((tk, tm), lambda i, j, k: (k, i)) if mode == "tn" else pl.BlockSpec((tm, tk), lambda i, j, k: (i, k))
    o_spec = pl.BlockSpec((tm, tn), lambda i, j, k: (i, j))
    out_shape = (m, n)
    if blocks is None:
        b_spec = pl.BlockSpec((tn, tk), lambda i, j, k: (j, k)) if mode == "nt" else pl.BlockSpec((tk, tn), lambda i, j, k: (k, j))
    elif mode == "nn":
        per = wide // tn
        b_spec = pl.BlockSpec((1, tk, tn), lambda i, j, k: (lo + j // per, k, j % per))
    elif mode == "nt":
        per = wide // tk
        b_spec = pl.BlockSpec((1, tn, tk), lambda i, j, k: (lo + k // per, j, k % per))
    else:
        per = wide // tn
        b_spec = pl.BlockSpec((tk, tn), lambda i, j, k: (k, j))
        o_spec = pl.BlockSpec((1, tm, tn), lambda i, j, k: (j // per, i, j % per))
        out_shape = (n_blk, m, wide)

    def body(*refs):
        a_ref, b_ref = refs[0], refs[1]
        add_ref = refs[2] if add is not None else None
        o_ref, acc = refs[-2], refs[-1]
        k = pl.program_id(2)
        part = _bdot_impl(_tile(a_ref), _tile(b_ref), mode)

        @pl.when(k == 0)
        def _():
            acc[...] = part

        @pl.when(k > 0)
        def _():
            acc[...] += part

        @pl.when(k == nk - 1)
        def _():
            res = acc[...]
            if add_ref is not None:
                res = res + add_ref[...]
            o_ref[...] = res.astype(o_ref.dtype).reshape(o_ref.shape)

    operands = [a, b] + ([add] if add is not None else [])
    in_specs = [a_spec, b_spec] + ([o_spec] if add is not None else [])
    return pl.pallas_call(
        body, grid=(m // tm, n // tn, nk), in_specs=in_specs, out_specs=o_spec,
        out_shape=jax.ShapeDtypeStruct(out_shape, out_dtype),
        scratch_shapes=[pltpu.VMEM((tm, tn), F32)],
        name=name, compiler_params=_cparams(3),
    )(*operands)


def _rows(x, width=None, off=0, tm=256):
    width = x.shape[1] if width is None else width
    return (x, (tm, width), lambda i, off=off: (i, off))


def _whole(x):
    nd = x.ndim
    return (x, x.shape, lambda *pids, nd=nd: (0,) * nd)


def _rms_ops(x, gain):
    return [_rows(x), _whole(gain)]


def rms_fwd(name, x, gain):
    s, dm = x.shape
    return tile_fwd(name, _rms_fn, (s // 256,), _rms_ops(x, gain), [((s, dm), BF16, (256, dm), lambda i: (i, 0), ())])[0]


def rms_bwd(name, x, gain, dh, dres):
    s = x.shape[0]
    return tile_bwd(name, _rms_fn, (s // 256,), _rms_ops(x, gain), [_rows(dh)], [(0, ()), (1, (0,))], adds={0: _rows(dres)})


def loss_call(y, t):
    s, dm = y.shape
    dy, part = tile_fwd("loss", _loss_fn, (s // 256,), [_rows(y), _rows(t)],
                        [((s, dm), F32, (256, dm), lambda i: (i, 0), ()), ((8, LANES), F32, (8, LANES), lambda i: (0, 0), (0,))])
    return dy, part[0, 0]


def _fox_prep_ops(pm, gq, gk):
    tm = 512
    return [(pm, (tm, LANES), lambda i, j: (i, C_FQ // LANES + j)), (pm, (tm, LANES), lambda i, j: (i, C_FK // LANES + j)),
            _whole(gq), _whole(gk)]


def fox_prep_fwd(name, pm, gq, gk):
    s = pm.shape[0]
    out = ((s, BRANCH), BF16, (512, LANES), lambda i, j: (i, j), ())
    return tile_fwd(name, _fox_prep_fn, (s // 512, 4), _fox_prep_ops(pm, gq, gk), [out, out])


def fox_prep_bwd(name, pm, gq, gk, dqn, dkn):
    s = pm.shape[0]
    cot = lambda g: (g, (512, LANES), lambda i, j: (i, j))
    own = ((s, BRANCH), (512, LANES), lambda i, j: (i, j))
    return tile_bwd(name, _fox_prep_fn, (s // 512, 4), _fox_prep_ops(pm, gq, gk), [cot(dqn), cot(dkn)],
                    [(0, (), own, BF16), (1, (), own, BF16), (2, (0, 1)), (3, (0, 1))])


def _fox_gate_ops(f_t, bias):
    return [(f_t, (1,) + f_t.shape[1:], lambda h: (h, 0, 0)), (bias, (1, 1, 1), lambda h: (h, 0, 0))]


def fox_gate_fwd(name, f_t, bias):
    n_h = f_t.shape[0]
    return tile_fwd(name, _fox_gate_fn, (n_h,), _fox_gate_ops(f_t, bias),
                    [(f_t.shape, F32, (1,) + f_t.shape[1:], lambda h: (h, 0, 0), ())])[0]


def fox_gate_bwd(name, f_t, bias, dcum):
    n_h = f_t.shape[0]
    return tile_bwd(name, _fox_gate_fn, (n_h,), _fox_gate_ops(f_t, bias),
                    [(dcum, (1,) + f_t.shape[1:], lambda h: (h, 0, 0))], [(0, ()), (1, ())])


def _fox_attn_ops(qn, kn, pm, cum_c, cum_r):
    s = qn.shape[0]
    nb = FOX_BLOCK
    return [(qn, (nb, LANES), lambda p, i: (i, p)), (kn, (s, LANES), lambda p, i: (0, p)),
            (pm, (s, LANES), lambda p, i: (0, C_FV // LANES + p)),
            (cum_c, (1, nb, 1), lambda p, i: (2 * p, i, 0)), (cum_c, (1, nb, 1), lambda p, i: (2 * p + 1, i, 0)),
            (cum_r, (1, 1, s), lambda p, i: (2 * p, 0, 0)), (cum_r, (1, 1, s), lambda p, i: (2 * p + 1, 0, 0))]


def fox_attn_fwd(name, qn, kn, pm, cum_c, cum_r):
    s = qn.shape[0]
    return tile_fwd(name, _fox_attn_fn, (4, s // FOX_BLOCK), _fox_attn_ops(qn, kn, pm, cum_c, cum_r),
                    [((s, BRANCH), BF16, (FOX_BLOCK, LANES), lambda p, i: (i, p), ())])[0]


def fox_attn_bwd(name, qn, kn, pm, cum_c, cum_r, dy):
    s = qn.shape[0]
    pair_c = ((4, s, 1), (1, FOX_BLOCK, 1), lambda p, i: (p, i, 0))
    pair_r = ((4, 1, s), (1, 1, s), lambda p, i: (p, 0, 0))
    d_qn, d_kn, d_v, d_cqa, d_cqb, d_cka, d_ckb = tile_bwd(
        name, _fox_attn_fn, (4, s // FOX_BLOCK), _fox_attn_ops(qn, kn, pm, cum_c, cum_r),
        [(dy, (FOX_BLOCK, LANES), lambda p, i: (i, p))],
        [(0, ()), (1, (1,)), (2, (1,), ((s, BRANCH), (s, LANES), lambda p, i: (0, p))),
         (3, (), pair_c), (4, (), pair_c), (5, (1,), pair_r), (6, (1,), pair_r)])
    d_cum = jnp.stack([d_cqa[:, :, 0] + d_cka[:, 0, :], d_cqb[:, :, 0] + d_ckb[:, 0, :]], axis=1).reshape(8, s)
    return d_qn, d_kn, d_v, d_cum


def sconv_ops(pm, w):
    s = pm.shape[0]
    blk = lambda c0: (pm, (s, LANES), lambda j, c0=c0: (0, c0 // LANES + j))
    return [blk(C_SB), blk(C_SC), blk(C_SV), (w, (w.shape[0], LANES), lambda j: (0, j))]


def dnconv_ops(pm, w):
    s = pm.shape[0]
    return [(pm, (s, LANES), lambda j: (0, C_DN // LANES + j)), (w, (w.shape[0], LANES), lambda j: (0, j))]


def ffn_ops(ug, uv, w):
    s = ug.shape[0]
    n_t = D_FF // LANES
    return [(ug, (s, LANES), lambda j: (0, j)), (uv, (s, LANES), lambda j: (0, j)),
            (w, (w.shape[0], LANES), lambda j: (0, j)), (w, (w.shape[0], LANES), lambda j: (0, n_t + j))]


def _col_out(s, width, dtype=F32):
    return ((s, width), dtype, (s, LANES), lambda j: (0, j), ())


def _col_cot(g):
    return (g, (g.shape[0], LANES), lambda j: (0, j))


def merge_ops(yp, pm):
    gate = lambda b: (pm, (256, D_MODEL), lambda i, b=b: (i, C_GATE // D_MODEL + b))
    return [_rows(yp[0]), _rows(yp[1]), _rows(yp[2]), gate(0), gate(1), gate(2)]


def ple_ops(gpre, pe, x):
    return [_rows(gpre), _rows(pe), _rows(x)]


def adam_call(name, w, g, m, v):
    shape = w.shape
    last = shape[-1]
    rows = w.size // last
    flat = lambda t: t.reshape(rows, last)
    tm = rows
    for cand in (512, 256, 128, 64, 32, 16, 8):
        if rows % cand == 0 and cand * last * 4 <= 2 * 1024 * 1024:
            tm = cand
            break
    spec = lambda t: (flat(t), (tm, last), lambda i: (i, 0))
    out = ((rows, last), F32, (tm, last), lambda i: (i, 0), ())
    res = tile_fwd(name, _adam_fn, (rows // tm,), [spec(w), spec(g), spec(m), spec(v)], [out, out, out])
    return [r.reshape(shape) for r in res]


def _adam_layers_fn(d, pids, w, m, v, g0, g1):
    g = jnp.where(pids[0] == 0, g0, g1)
    return (g,) + _adam_fn(d, pids, w, g, m, v)


def adam_layers(name, w, m, v, g0, g1):
    _, rows, cols = w.shape
    tm = _row_tile(rows, cols)
    n_t = rows // tm
    lay = lambda t: (t, (1, tm, cols), lambda l, i: (l, i, 0))
    ins = [lay(w), lay(m), lay(v), (g0, (tm, cols), lambda l, i: (i * (1 - l) + (n_t - 1) * l, 0)), (g1, (tm, cols), lambda l, i: (i * l, 0))]
    out = (w.shape, F32, (1, tm, cols), lambda l, i: (l, i, 0), ())
    return tile_fwd(name, _adam_layers_fn, (2, n_t), ins, [out, out, out, out])


DN_GROUP = 4


def _dn_local_specs(rev_n=None):
    rows = DN_GROUP * DN_CHUNK
    idx = (lambda j: j) if rev_n is None else (lambda j: rev_n - 1 - j)
    return [pl.BlockSpec((rows, 3 * BRANCH), lambda j: (idx(j), 0)), pl.BlockSpec((rows, LANES), lambda j: (idx(j), 0)),
            pl.BlockSpec((DN_GROUP, DN_HEADS, DN_CHUNK), lambda j: (idx(j), 0, 0)), pl.BlockSpec((2, DN_HEADS), lambda j: (0, 0))]


def _dn_group_inputs(qkv, ps, a_rows, c):
    lo = c * DN_CHUNK
    heads = _split_heads(qkv[lo:lo + DN_CHUNK])
    return heads[0:4], heads[4:8], heads[8:12], ps[lo:lo + DN_CHUNK], a_rows[c]


def dn_local_fwd(name, dn_act, ps, a_rows, ad):
    s = dn_act.shape[0]
    n_c, n_g = s // DN_CHUNK, s // (DN_GROUP * DN_CHUNK)
    rows = DN_GROUP * DN_CHUNK

    def body(qkv_ref, ps_ref, ar_ref, ad_ref, u_ref, kc_ref, qd_ref, kd_ref, qk_ref, gl_ref):
        qkv, ps_v, a_rows_v, ad_v = qkv_ref[...], ps_ref[...], ar_ref[...], ad_ref[...]
        args = [[] for _ in range(8)]
        for c in range(DN_GROUP):
            q4, k4, v4, ps_c, ar_c = _dn_group_inputs(qkv, ps_v, a_rows_v, c)
            for lst, vals in zip(args, (q4, k4, v4) + _dn_gates(ps_c, ar_c, ad_v)):
                lst.extend(vals)
        everything = _dn_local(False, *args)
        for c in range(DN_GROUP):
            res = everything[c * DN_HEADS:(c + 1) * DN_HEADS]
            at = pl.ds(c * DN_CHUNK, DN_CHUNK)
            for ref, i in ((u_ref, 0), (kc_ref, 1), (qd_ref, 2), (kd_ref, 3)):
                ref[at, :] = jnp.concatenate([r[i] for r in res], axis=1)
            for h in range(DN_HEADS):
                qk_ref[c, h] = res[h][4]
            gl_ref[c] = _head_rows([r[5] for r in res])

    wide = pl.BlockSpec((rows, BRANCH), lambda j: (j, 0))
    return pl.pallas_call(
        body, grid=(n_g,), in_specs=_dn_local_specs(),
        out_specs=[wide, wide, wide, wide, pl.BlockSpec((DN_GROUP, DN_HEADS, DN_CHUNK, DN_CHUNK), lambda j: (j, 0, 0, 0)),
                   pl.BlockSpec((DN_GROUP, 8, LANES), lambda j: (j, 0, 0))],
        out_shape=[jax.ShapeDtypeStruct((s, BRANCH), F32)] * 4 + [jax.ShapeDtypeStruct((n_c, DN_HEADS, DN_CHUNK, DN_CHUNK), F32),
                                                                 jax.ShapeDtypeStruct((n_c, 8, LANES), F32)],
        name=name, compiler_params=_cparams(1),
    )(dn_act, ps, a_rows, ad)


def dn_local_bwd(name, dn_act, ps, a_rows, ad, cots):
    s = dn_act.shape[0]
    n_c, n_g = s // DN_CHUNK, s // (DN_GROUP * DN_CHUNK)
    rows = DN_GROUP * DN_CHUNK

    def body(qkv_ref, ps_ref, ar_ref, ad_ref, du_ref, dkc_ref, dqd_ref, dkd_ref, dqk_ref, dgl_ref, dqkv_ref, dps_ref, dar_ref, dad_ref):
        first = pl.program_id(0) == 0
        qkv, ps_v, a_rows_v, ad_v = qkv_ref[...], ps_ref[...], ar_ref[...], ad_ref[...]
        d_wide = [r[...] for r in (du_ref, dkc_ref, dqd_ref, dkd_ref)]
        qs, ks, vs, ps_cs, ar_cs, cot = [], [], [], [], [], []
        for c in range(DN_GROUP):
            q4, k4, v4, ps_c, ar_c = _dn_group_inputs(qkv, ps_v, a_rows_v, c)
            qs, ks, vs, ps_cs, ar_cs = qs + q4, ks + k4, vs + v4, ps_cs + [ps_c], ar_cs + [ar_c]
            lo = c * DN_CHUNK
            d_tiles = [_split_heads(t[lo:lo + DN_CHUNK]) for t in d_wide]
            d_gl = dgl_ref[c]
            cot += [(d_tiles[0][h], d_tiles[1][h], d_tiles[2][h], d_tiles[3][h], dqk_ref[c, h], _col(_row(d_gl, h), 0))
                    for h in range(DN_HEADS)]

        def f(qs, ks, vs, ps_cs, ar_cs, ad_v):
            gates = [[] for _ in range(5)]
            for ps_c, ar_c in zip(ps_cs, ar_cs):
                for lst, vals in zip(gates, _dn_gates(ps_c, ar_c, ad_v)):
                    lst.extend(vals)
            return _dn_local(True, qs, ks, vs, *gates)

        _, vjp = jax.vjp(f, qs, ks, vs, ps_cs, ar_cs, ad_v)
        d_q, d_k, d_v, d_ps, d_ar, d_ad = vjp(cot)
        for c in range(DN_GROUP):
            at, hs = pl.ds(c * DN_CHUNK, DN_CHUNK), slice(c * DN_HEADS, (c + 1) * DN_HEADS)
            dqkv_ref[at, :] = jnp.concatenate(d_q[hs] + d_k[hs] + d_v[hs], axis=1).astype(dqkv_ref.dtype)
            dps_ref[at, :] = d_ps[c]
            dar_ref[c] = d_ar[c]
        _store(dad_ref, d_ad, first)

    wide = pl.BlockSpec((rows, BRANCH), lambda j: (j, 0))
    specs = _dn_local_specs()
    return pl.pallas_call(
        body, grid=(n_g,),
        in_specs=specs + [wide, wide, wide, wide, pl.BlockSpec((DN_GROUP, DN_HEADS, DN_CHUNK, DN_CHUNK), lambda j: (j, 0, 0, 0)),
                          pl.BlockSpec((DN_GROUP, 8, LANES), lambda j: (j, 0, 0))],
        out_specs=specs,
        out_shape=[jax.ShapeDtypeStruct((s, 3 * BRANCH), F32), jax.ShapeDtypeStruct((s, LANES), F32),
                   jax.ShapeDtypeStruct((n_c, DN_HEADS, DN_CHUNK), F32), jax.ShapeDtypeStruct((2, DN_HEADS), F32)],
        name=name, compiler_params=_cparams(1),
    )(dn_act, ps, a_rows, ad, *cots)


def _dn_scan_specs(n_c, rev):
    idx = (lambda j: n_c - 1 - j) if rev else (lambda j: j)
    wide = pl.BlockSpec((DN_CHUNK, BRANCH), lambda j: (idx(j), 0))
    return [wide, wide, wide, wide, pl.BlockSpec((1, DN_HEADS, DN_CHUNK, DN_CHUNK), lambda j: (idx(j), 0, 0, 0)),
            pl.BlockSpec((1, 8, LANES), lambda j: (idx(j), 0, 0)), pl.BlockSpec((DN_CHUNK, BRANCH), lambda j: (idx(j), C_DZ // BRANCH)),
            pl.BlockSpec((1, DN_DH), lambda j: (0, 0))]


def _dn_scan_tiles(refs):
    u_ref, kc_ref, qd_ref, kd_ref, qk_ref, gl_ref, z_ref, g_ref = refs
    wide = [_split_heads(r[...]) for r in (u_ref, kc_ref, qd_ref, kd_ref)]
    gl = gl_ref[0]
    return [(wide[0][h], wide[1][h], wide[2][h], wide[3][h], qk_ref[0, h], _col(_row(gl, h), 0)) for h in range(DN_HEADS)], \
        _split_heads(z_ref[...]), g_ref[...]


def dn_scan_fwd(name, local, pm, gain):
    s = pm.shape[0]
    n_c = s // DN_CHUNK

    def body(*refs):
        y_ref, hist_ref, state = refs[8:]

        @pl.when(pl.program_id(0) == 0)
        def _():
            state[...] = jnp.zeros_like(state)

        hist_ref[0] = state[...]
        per_head, z4, gain_v = _dn_scan_tiles(refs[:8])
        ys, s_nexts = _dn_step(False, [state[h] for h in range(DN_HEADS)], per_head, z4, gain_v)
        for h in range(DN_HEADS):
            state[h] = s_nexts[h]
        y_ref[...] = jnp.concatenate(ys, axis=1).astype(y_ref.dtype)

    return pl.pallas_call(
        body, grid=(n_c,), in_specs=_dn_scan_specs(n_c, False),
        out_specs=[pl.BlockSpec((DN_CHUNK, BRANCH), lambda j: (j, 0)),
                   pl.BlockSpec((1, DN_HEADS, DN_DH, DN_DH), lambda j: (j, 0, 0, 0))],
        out_shape=[jax.ShapeDtypeStruct((s, BRANCH), BF16), jax.ShapeDtypeStruct((n_c, DN_HEADS, DN_DH, DN_DH), F32)],
        scratch_shapes=[pltpu.VMEM((DN_HEADS, DN_DH, DN_DH), F32)],
        name=name, compiler_params=_cparams(1),
    )(*local, pm, gain)


def dn_scan_bwd(name, local, pm, gain, hist, dy):
    s = pm.shape[0]
    n_c = s // DN_CHUNK

    def body(*refs):
        hist_ref, dy_ref = refs[8:10]
        du_ref, dkc_ref, dqd_ref, dkd_ref, dqk_ref, dgl_ref, dz_ref, dg_ref, d_state = refs[10:]
        first = pl.program_id(0) == 0

        @pl.when(first)
        def _():
            d_state[...] = jnp.zeros_like(d_state)

        per_head, z4, gain_v = _dn_scan_tiles(refs[:8])
        _, vjp = jax.vjp(functools.partial(_dn_step, True), [hist_ref[0, h] for h in range(DN_HEADS)], per_head, z4, gain_v)
        d_s, grads, d_z, d_gain = vjp((_split_heads(dy_ref[...]), [d_state[h] for h in range(DN_HEADS)]))
        for h in range(DN_HEADS):
            d_state[h] = d_s[h]
        for ref, i in ((du_ref, 0), (dkc_ref, 1), (dqd_ref, 2), (dkd_ref, 3)):
            ref[...] = jnp.concatenate([g[i] for g in grads], axis=1)
        dz_ref[...] = jnp.concatenate(d_z, axis=1).astype(dz_ref.dtype)
        for h in range(DN_HEADS):
            dqk_ref[0, h] = grads[h][4]
        dgl_ref[0] = _head_rows([g[5] for g in grads])
        _store(dg_ref, d_gain, first)

    rev = lambda j: n_c - 1 - j
    specs = _dn_scan_specs(n_c, True)
    return pl.pallas_call(
        body, grid=(n_c,),
        in_specs=specs + [pl.BlockSpec((1, DN_HEADS, DN_DH, DN_DH), lambda j: (rev(j), 0, 0, 0)),
                          pl.BlockSpec((DN_CHUNK, BRANCH), lambda j: (rev(j), 0))],
        out_specs=specs[:6] + [pl.BlockSpec((DN_CHUNK, BRANCH), lambda j: (rev(j), 0)), specs[7]],
        out_shape=[jax.ShapeDtypeStruct((s, BRANCH), F32)] * 4 + [
            jax.ShapeDtypeStruct((n_c, DN_HEADS, DN_CHUNK, DN_CHUNK), F32), jax.ShapeDtypeStruct((n_c, 8, LANES), F32),
            jax.ShapeDtypeStruct((s, BRANCH), BF16), jax.ShapeDtypeStruct((1, DN_DH), F32)],
        scratch_shapes=[pltpu.VMEM((DN_HEADS, DN_DH, DN_DH), F32)],
        name=name, compiler_params=_cparams(1),
    )(*local, pm, gain, hist, dy)


def _seq_layouts(cols, s):
    return cols.T.reshape(cols.shape[1], s // LANES, LANES)


def layer_fwd(li, x, p, w):
    s = x.shape[0]
    n = lambda t: f"{t}_l{li}"
    h = rms_fwd(n("rms_mix"), x, w["g_mix"])
    pm = mm(n("in_main"), h, w["in_main"], "nn")
    ps = mm(n("in_small"), h, w["in_small"], "nn")
    qn, kn = fox_prep_fwd(n("fox_prep"), pm, w["gq"], w["gk"])
    f_t = _seq_layouts(ps[:, 0:8], s)
    cum = fox_gate_fwd(n("fox_gate"), f_t, w["b_f"])
    cum_c, cum_r = cum.reshape(8, s, 1), cum.reshape(8, 1, s)
    y_fox = fox_attn_fwd(n("fox_attn"), qn, kn, pm, cum_c, cum_r)
    y_sc = tile_fwd(n("sconv"), _sconv_fn, (BRANCH // LANES,), sconv_ops(pm, w["sc_conv_w"]), [_col_out(s, BRANCH, BF16)])[0]
    dn_act = tile_fwd(n("dnconv"), _dnconv_fn, (3 * BRANCH // LANES,), dnconv_ops(pm, w["dn_conv_w"]), [_col_out(s, 3 * BRANCH)])[0]
    a_rows = ps[:, 12:16].reshape(s // DN_CHUNK, DN_CHUNK, DN_HEADS).transpose(0, 2, 1)
    dn_local = dn_local_fwd(n("dn_local"), dn_act, ps, a_rows, w["ad"])
    y_dn, hist = dn_scan_fwd(n("dn_scan"), dn_local, pm, w["dn_gain"])
    ys = (y_fox, y_sc, y_dn)
    yp = [mm(n(f"branch{b}"), ys[b], w["branch"][b], "nn", blocks=(0, N_CHIPS)) for b in range(3)]
    merged = tile_fwd(n("merge"), _merge_fn, (s // 256,), merge_ops(yp, pm), [((s, D_MODEL), BF16, (256, D_MODEL), lambda i: (i, 0), ())])[0]
    x1 = mm(n("w_o"), merged, w["o"], "nn", add=x)
    h2 = rms_fwd(n("rms_ffn"), x1, w["g_ffn"])
    ug = mm(n("up_g"), h2, w["up"], "nn", blocks=(0, 2))
    uv = mm(n("up_v"), h2, w["up"], "nn", blocks=(2, 2))
    act = tile_fwd(n("ffn_act"), _ffn_act_fn, (D_FF // LANES,), ffn_ops(ug, uv, w["ffn_conv_w"]), [_col_out(s, D_FF, BF16)])[0]
    x2 = mm(n("down"), act, w["down"], "nn", add=x1)
    h3 = rms_fwd(n("rms_ple"), x2, w["g_ple"])
    gpre = mm(n("ple_gate"), h3, w["pg"], "nn")
    pe = mm(n("ple_emb"), p, w["ple"], "nn", blocks=(0, N_CHIPS))
    x3 = tile_fwd(n("ple"), _ple_fn, (s // 256,), ple_ops(gpre, pe, x2), [((s, D_MODEL), F32, (256, D_MODEL), lambda i: (i, 0), ())])[0]
    saved = dict(x=x, h=h, pm=pm, ps=ps, qn=qn, kn=kn, f_t=f_t, cum_c=cum_c, cum_r=cum_r, ys=ys, dn_act=dn_act, dn_local=dn_local,
                 a_rows=a_rows, hist=hist, yp=yp, merged=merged, x1=x1, h2=h2, ug=ug, uv=uv, act=act, x2=x2, h3=h3,
                 gpre=gpre, pe=pe, p=p)
    return x3, saved


def hang_on(w, token):
    zero = token[0, 0]
    small = ("g_mix", "g_ffn", "g_ple", "gq", "gk", "b_f", "ad", "dn_gain", "sc_conv_w", "dn_conv_w", "ffn_conv_w")
    return {**w, **{k: w[k] + zero for k in small}}


def layer_bwd(li, dx3, sv, w, mid_hook=None):
    s = dx3.shape[0]
    n = lambda t: f"{t}_l{li}"
    g = {}
    col_own = lambda width: ((s, width), (s, LANES), lambda j: (0, j))
    d_gpre, d_pe = tile_bwd(n("ple_bwd"), _ple_fn, (s // 256,), ple_ops(sv["gpre"], sv["pe"], sv["x2"]), [_rows(dx3)],
                            [(0, (), None, BF16), (1, (), None, BF16)])
    g["w_ple"] = mm(n("d_w_ple"), sv["p"], d_pe, "tn", blocks=(0, N_CHIPS))
    g["w_ple_gate"] = mm(n("d_w_pg"), sv["h3"], d_gpre, "tn").reshape(N_CHIPS, -1, D_MODEL)
    dh3 = mm(n("d_h3"), d_gpre, w["pg"], "nt")
    dx2, d_g_ple = rms_bwd(n("rms_ple_bwd"), sv["x2"], w["g_ple"], dh3, dx3)
    dact = mm(n("d_act"), dx2, w["down"], "nt")
    g["w_down"] = mm(n("d_w_down"), sv["act"], dx2, "tn").reshape(N_CHIPS, -1, D_MODEL)
    taps_own = ((w["ffn_conv_w"].shape[0], D_FF), (w["ffn_conv_w"].shape[0], LANES), lambda j: (0, j))
    d_ug, d_uv, d_fw_g, d_fw_v = tile_bwd(n("ffn_act_bwd"), _ffn_act_fn, (D_FF // LANES,), ffn_ops(sv["ug"], sv["uv"], w["ffn_conv_w"]),
                                          [_col_cot(dact)], [(0, (), None, BF16), (1, (), None, BF16), (2, (), taps_own), (3, (), taps_own)])
    g["ffn_conv_w"] = jnp.concatenate([d_fw_g, d_fw_v], axis=1)
    g["w_up"] = jnp.concatenate([mm(n("d_w_up_g"), sv["h2"], d_ug, "tn", blocks=(0, 2)), mm(n("d_w_up_v"), sv["h2"], d_uv, "tn", blocks=(0, 2))])
    dh2 = mm(n("d_h2_v"), d_uv, w["up"], "nt", blocks=(2, 2), add=mm(n("d_h2_g"), d_ug, w["up"], "nt", blocks=(0, 2)))
    dx1, d_g_ffn = rms_bwd(n("rms_ffn_bwd"), sv["x1"], w["g_ffn"], dh2, dx2)
    if mid_hook is not None:
        w = hang_on(w, mid_hook(dx1))
    dmerged = mm(n("d_merged"), dx1, w["o"], "nt")
    g["w_o"] = mm(n("d_w_o"), sv["merged"], dx1, "tn").reshape(N_CHIPS, -1, D_MODEL)
    gate_own = ((s, D_MODEL), (256, D_MODEL), lambda i: (i, 0))
    d_yp0, d_yp1, d_yp2, d_g0, d_g1, d_g2 = tile_bwd(
        n("merge_bwd"), _merge_fn, (s // 256,), merge_ops(sv["yp"], sv["pm"]), [_rows(dmerged)],
        [(0, (), None, BF16), (1, (), None, BF16), (2, (), None, BF16), (3, (), gate_own, BF16), (4, (), gate_own, BF16), (5, (), gate_own, BF16)])
    d_yp = (d_yp0, d_yp1, d_yp2)
    g["w_branch"] = jnp.concatenate([mm(n(f"d_w_branch{b}"), sv["ys"][b], d_yp[b], "tn", blocks=(0, N_CHIPS)) for b in range(3)], axis=1)
    d_ys = [mm(n(f"d_y{b}"), d_yp[b], w["branch"][b], "nt", blocks=(0, N_CHIPS)) for b in range(3)]
    *d_local, d_z, d_dngain = dn_scan_bwd(n("dn_scan_bwd"), sv["dn_local"], sv["pm"], w["dn_gain"], sv["hist"], d_ys[2])
    d_dnact, d_ps_dn, d_arows, d_ad = dn_local_bwd(n("dn_local_bwd"), sv["dn_act"], sv["ps"], sv["a_rows"], w["ad"], d_local)
    g["ad"], g["dn_norm_gain"] = d_ad, d_dngain[0]
    d_dnqkv, g["dn_conv_w"] = tile_bwd(n("dnconv_bwd"), _dnconv_fn, (3 * BRANCH // LANES,), dnconv_ops(sv["pm"], w["dn_conv_w"]),
                                       [_col_cot(d_dnact)], [(0, (), col_own(3 * BRANCH), BF16), (1, ())])
    d_sb, d_sc, d_sv, g["sc_conv_w"] = tile_bwd(n("sconv_bwd"), _sconv_fn, (BRANCH // LANES,), sconv_ops(sv["pm"], w["sc_conv_w"]), [_col_cot(d_ys[1])],
                                                [(0, (), col_own(BRANCH), BF16), (1, (), col_own(BRANCH), BF16), (2, (), col_own(BRANCH), BF16), (3, ())])
    d_qn, d_kn, d_fv, d_cum = fox_attn_bwd(n("fox_attn_bwd"), sv["qn"], sv["kn"], sv["pm"], sv["cum_c"], sv["cum_r"], d_ys[0])
    d_ft, d_bf = fox_gate_bwd(n("fox_gate_bwd"), sv["f_t"], w["b_f"], d_cum.reshape(8, s // LANES, LANES))
    g["b_fox_f"] = d_bf.reshape(8)
    d_fq, d_fk, d_gq, d_gk = fox_prep_bwd(n("fox_prep_bwd"), sv["pm"], w["gq"], w["gk"], d_qn, d_kn)
    g["fox_q_gain"] = d_gq[0, :FOX_DH] + d_gq[0, FOX_DH:]
    g["fox_k_gain"] = d_gk[0, :FOX_DH] + d_gk[0, FOX_DH:]
    d_pm = jnp.concatenate([d_fq, d_fk, d_fv.astype(BF16), d_sb, d_sc, d_sv, d_dnqkv, d_z, d_g0, d_g1, d_g2], axis=1)
    d_a_cols = d_arows.transpose(0, 2, 1).reshape(s, DN_HEADS)
    d_f_cols = d_ft.reshape(8, s).T
    d_ps = d_ps_dn + jnp.concatenate([d_f_cols, jnp.zeros((s, 4), F32), d_a_cols, jnp.zeros((s, LANES - 16), F32)], axis=1)
    g["w_in"] = chip_blocks_w_in(mm(n("d_w_in_main"), sv["h"], d_pm, "tn"), mm(n("d_w_in_small"), sv["h"], d_ps, "tn"))
    dh = mm(n("d_h_small"), d_ps, w["in_small"], "nt", add=mm(n("d_h_main"), d_pm, w["in_main"], "nt"))
    dx, d_g_mix = rms_bwd(n("rms_mix_bwd"), sv["x"], w["g_mix"], dh, dx1)
    g["g_mix"], g["g_ffn"], g["g_ple"] = d_g_mix[0], d_g_ffn[0], d_g_ple[0]
    return dx, g


IN_SHARD = 2052
MAIN_RANGES = ((0, 1536), (1544, 3080), (3080, 4616), (4624, 5136), (5136, 8208))
SMALL_RANGES = ((1536, 1544), (4616, 4620), (4620, 4624))


def _from_chip_blocks(blocks, ranges):
    parts = []
    for lo, hi in ranges:
        for k in range(N_CHIPS):
            a0, a1 = max(lo, k * IN_SHARD), min(hi, (k + 1) * IN_SHARD)
            if a0 < a1:
                parts.append(blocks[k][:, a0 - k * IN_SHARD:a1 - k * IN_SHARD])
    return parts


def split_w_in(blocks):
    main = jnp.concatenate(_from_chip_blocks(blocks, MAIN_RANGES), axis=1)
    pad = jnp.zeros((blocks.shape[1], LANES - 16), blocks.dtype)
    return main, jnp.concatenate(_from_chip_blocks(blocks, SMALL_RANGES) + [pad], axis=1)


def chip_blocks_w_in(main, small):
    pieces, m_off, s_off = [], 0, 0
    ranges = sorted([(lo, hi, "m") for lo, hi in MAIN_RANGES] + [(lo, hi, "s") for lo, hi in SMALL_RANGES])
    offs = {}
    for lo, hi in MAIN_RANGES:
        offs[lo] = m_off
        m_off += hi - lo
    for lo, hi in SMALL_RANGES:
        offs[lo] = s_off
        s_off += hi - lo
    blocks = []
    for k in range(N_CHIPS):
        parts = []
        for lo, hi, src in ranges:
            a0, a1 = max(lo, k * IN_SHARD), min(hi, (k + 1) * IN_SHARD)
            if a0 < a1:
                arr = main if src == "m" else small
                parts.append(arr[:, offs[lo] + a0 - lo:offs[lo] + a1 - lo])
        blocks.append(jnp.concatenate(parts, axis=1))
    return jnp.stack(blocks)


def layer_weights(li, got, conv, a):
    g_in, g_branch, g_o, g_up, g_down, g_pg, g_ple = got
    main, small = split_w_in(g_in)
    tile2 = lambda v: jnp.concatenate([v, v])[None, :]
    branch = g_branch.reshape(N_CHIPS, 3, BRANCH, -1)
    return dict(
        in_main=main, in_small=small, branch=[branch[:, b] for b in range(3)], o=g_o.reshape(D_MODEL, D_MODEL), up=g_up,
        down=g_down.reshape(D_FF, D_MODEL), pg=g_pg.reshape(D_MODEL, D_MODEL), ple=g_ple,
        g_mix=a["g_mix"][li][None, :], g_ffn=a["g_ffn"][li][None, :], g_ple=a["g_ple"][li][None, :],
        gq=tile2(a["fox_q_gain"][li]), gk=tile2(a["fox_k_gain"][li]), b_f=a["b_fox_f"][li].reshape(8, 1, 1),
        ad=jnp.stack([a["dn_a_log"][li], a["dn_dt_bias"][li]]), dn_gain=a["dn_norm_gain"][li][None, :],
        sc_conv_w=conv["sc_conv_w"][li], dn_conv_w=conv["dn_conv_w"][li], ffn_conv_w=conv["ffn_conv_w"][li])


def pack_rows(arrs, dtype):
    flat = jnp.concatenate([t.reshape(-1).astype(dtype) for t in arrs])
    pad = (-flat.shape[0]) % (8 * LANES)
    if pad:
        flat = jnp.concatenate([flat, jnp.zeros((pad,), dtype)])
    return flat.reshape(-1, LANES)


def unpack_rows(buf, shapes):
    flat = buf.reshape(-1)
    out, off = [], 0
    for shp in shapes:
        size = 1
        for dim in shp:
            size *= dim
        out.append(flat[off:off + size].reshape(shp))
        off += size
    return out


def chip_shard(t, axis, k):
    width = t.shape[axis] // N_CHIPS
    return lax.slice_in_dim(t, k * width, (k + 1) * width, axis=axis)


ANY = pl.BlockSpec(memory_space=pl.ANY)


def _position():
    x, y, c = lax.axis_index("x"), lax.axis_index("y"), lax.axis_index("c")
    return x, y, c, [(1 - x, y), (x, 1 - y), (1 - x, 1 - y)]


def gather_small(name, block):
    m_per, n = block.shape

    def body(x_ref, out_ref, send_sems, recv_sems, local_sem):
        x, y, c, chips = _position()
        me, sibling = (x, y, c), (x, y, 1 - c)

        def rows(px, py, pc):
            return out_ref.at[pl.ds((4 * px + 2 * py + pc) * m_per, m_per), :]

        def copy(k, blk, to, src=None):
            return pltpu.make_async_remote_copy(src_ref=rows(*blk) if src is None else src, dst_ref=rows(*blk),
                                                send_sem=send_sems.at[k], recv_sem=recv_sems.at[k], device_id=to, device_id_type=MESH)

        mine = pltpu.make_async_copy(x_ref, rows(*me), local_sem)
        mine.start()
        first = [copy(0, me, sibling, src=x_ref)] + [copy(1 + j, me, (*chip, c), src=x_ref) for j, chip in enumerate(chips)]
        for cp in first:
            cp.start()
        passed = [copy(4 + j, (*chip, c), sibling) for j, chip in enumerate(chips)]
        for j, chip in enumerate(chips):
            copy(1 + j, (*chip, c), me).wait_recv()
            passed[j].start()
        copy(0, sibling, me).wait_recv()
        for j, chip in enumerate(chips):
            copy(4 + j, (*chip, 1 - c), me).wait_recv()
        for cp in first + passed:
            cp.wait_send()
        mine.wait()

    return pl.pallas_call(
        body, out_shape=jax.ShapeDtypeStruct((8 * m_per, n), block.dtype),
        in_specs=[pl.BlockSpec(memory_space=pltpu.VMEM)], out_specs=pl.BlockSpec(memory_space=pltpu.VMEM),
        scratch_shapes=[pltpu.SemaphoreType.DMA((7,)), pltpu.SemaphoreType.DMA((7,)), pltpu.SemaphoreType.DMA],
        name=name, compiler_params=pltpu.CompilerParams(vmem_limit_bytes=VMEM_LIMIT),
    )(block)


def _sems(n):
    return [pltpu.SemaphoreType.DMA((n,)), pltpu.SemaphoreType.DMA((n,))]


def gather_layer(name, shards):
    n_w = len(shards)
    halves = [s.shape[0] // 2 for s in shards]

    def body(*refs):
        ins, outs = refs[:n_w], refs[n_w:2 * n_w]
        token, send_sems, recv_sems = refs[2 * n_w:]
        token[...] = jnp.zeros_like(token)
        x, y, c, chips = _position()
        sibling = (x, y, 1 - c)

        def part(w, px, py, pc):
            return outs[w].at[2 * px + py, pl.ds(pc * halves[w], halves[w]), :]

        def copy(k, w, blk, to, src=None):
            return pltpu.make_async_remote_copy(src_ref=part(w, *blk) if src is None else src, dst_ref=part(w, *blk),
                                                send_sem=send_sems.at[k], recv_sem=recv_sems.at[k], device_id=to, device_id_type=MESH)

        pairs = [(w, j, chip) for w in range(n_w) for j, chip in enumerate(chips)]
        first = [copy(3 * w + j, w, (x, y, c), (*chip, c), src=ins[w].at[pl.ds(c * halves[w], halves[w]), :]) for w, j, chip in pairs]
        for cp in first:
            cp.start()
        passed = [copy(3 * n_w + 3 * w + j, w, (*chip, c), sibling) for w, j, chip in pairs]
        for (w, j, chip), fwd in zip(pairs, passed):
            copy(3 * w + j, w, (*chip, c), (x, y, c)).wait_recv()
            fwd.start()
        for w, j, chip in pairs:
            copy(3 * n_w + 3 * w + j, w, (*chip, 1 - c), (x, y, c)).wait_recv()
        for cp in first + passed:
            cp.wait_send()

    out = pl.pallas_call(
        body, out_shape=[jax.ShapeDtypeStruct((N_CHIPS,) + s.shape, s.dtype) for s in shards] + [jax.ShapeDtypeStruct((8, LANES), F32)],
        in_specs=[ANY] * n_w, out_specs=[ANY] * n_w + [pl.BlockSpec(memory_space=pltpu.VMEM)], scratch_shapes=_sems(6 * n_w), name=name,
    )(*shards)
    return out[:n_w], out[n_w]


def swap_halves(name, grads):
    n_w = len(grads)
    halves = [g.shape[1] // 2 for g in grads]

    def body(*refs):
        ins, outs = refs[:n_w], refs[n_w:2 * n_w]
        send_sems, recv_sems = refs[2 * n_w:]
        x, y, c, _ = _position()
        cps = [pltpu.make_async_remote_copy(src_ref=ins[w].at[:, pl.ds((1 - c) * halves[w], halves[w]), :], dst_ref=outs[w],
                                            send_sem=send_sems.at[w], recv_sem=recv_sems.at[w], device_id=(x, y, 1 - c),
                                            device_id_type=MESH) for w in range(n_w)]
        for cp in cps:
            cp.start()
        for cp in cps:
            cp.wait()

    return pl.pallas_call(
        body, out_shape=[jax.ShapeDtypeStruct((N_CHIPS, h, g.shape[2]), g.dtype) for g, h in zip(grads, halves)],
        in_specs=[ANY] * n_w, out_specs=[ANY] * n_w, scratch_shapes=_sems(n_w), name=name,
    )(*grads)


def scatter_chips(name, partials):
    n_w = len(partials)

    def body(*refs):
        ins, outs = refs[:n_w], refs[n_w:2 * n_w]
        send_sems, recv_sems = refs[2 * n_w:]
        x, y, c, chips = _position()
        cps = [pltpu.make_async_remote_copy(src_ref=ins[w].at[2 * cx + cy], dst_ref=outs[w].at[j], send_sem=send_sems.at[3 * w + j],
                                            recv_sem=recv_sems.at[3 * w + j], device_id=(cx, cy, c), device_id_type=MESH)
               for w in range(n_w) for j, (cx, cy) in enumerate(chips)]
        for cp in cps:
            cp.start()
        for cp in cps:
            cp.wait()

    return pl.pallas_call(
        body, out_shape=[jax.ShapeDtypeStruct((3,) + p.shape[1:], p.dtype) for p in partials],
        in_specs=[ANY] * n_w, out_specs=[ANY] * n_w, scratch_shapes=_sems(3 * n_w), name=name,
    )(*partials)


def share_halves(name, bufs):
    n_w = len(bufs)
    halves = [b.shape[0] // 2 for b in bufs]

    def body(*refs):
        outs = refs[n_w:2 * n_w]
        send_sems, recv_sems = refs[2 * n_w:]
        x, y, c, _ = _position()

        def copy(w, pc):
            half = outs[w].at[pl.ds(pc * halves[w], halves[w]), :]
            return pltpu.make_async_remote_copy(src_ref=half, dst_ref=half, send_sem=send_sems.at[w], recv_sem=recv_sems.at[w],
                                                device_id=(x, y, 1 - c), device_id_type=MESH)

        for w in range(n_w):
            copy(w, c).start()
        for w in range(n_w):
            copy(w, 1 - c).wait_recv()
            copy(w, c).wait_send()

    return pl.pallas_call(
        body, out_shape=[jax.ShapeDtypeStruct(b.shape, b.dtype) for b in bufs], in_specs=[ANY] * n_w, out_specs=[ANY] * n_w,
        input_output_aliases={w: w for w in range(n_w)}, scratch_shapes=_sems(n_w), name=name,
    )(*bufs)


HBM = pl.BlockSpec(memory_space=pltpu.HBM)
SEM = pl.BlockSpec(memory_space=pltpu.SEMAPHORE)
EFFECT = pltpu.SideEffectType.DATAFLOW_SIDE_EFFECTING


def _exchange_copies(kind, srcs, lands):
    x, y, c, chips = _position()
    out = []
    for src, land in zip(srcs, lands):
        if kind == "swap":
            half = src.shape[1] // 2
            out.append((src.at[:, pl.ds((1 - c) * half, half), :], land, (x, y, 1 - c)))
            continue
        for j, (cx, cy) in enumerate(chips):
            if kind == "gather":
                out.append((src, land.at[2 * x + y], (cx, cy, c)))
            else:
                out.append((src.at[2 * cx + cy], land.at[j], (cx, cy, c)))
    return out


def _land_shapes(kind, srcs):
    if kind == "gather":
        return [(N_CHIPS,) + s.shape for s in srcs]
    if kind == "swap":
        return [(N_CHIPS, s.shape[1] // 2, s.shape[2]) for s in srcs]
    return [(3,) + s.shape[1:] for s in srcs]


def exchange_start(name, kind, srcs):
    n_w = len(srcs)
    shapes = _land_shapes(kind, srcs)
    n_sem = n_w if kind == "swap" else 3 * n_w

    def body(*refs):
        ins, lands = refs[:n_w], refs[n_w:2 * n_w]
        send_sems, recv_sems = refs[2 * n_w:2 * n_w + 2]
        token = refs[-1]
        for i, (src, dst, dev) in enumerate(_exchange_copies(kind, ins, lands)):
            pltpu.make_async_remote_copy(src_ref=src, dst_ref=dst, send_sem=send_sems.at[i], recv_sem=recv_sems.at[i],
                                         device_id=dev, device_id_type=MESH).start()
        token[...] = jnp.zeros_like(token)

    out = pl.pallas_call(
        body, name=name,
        out_shape=(pltpu.SemaphoreType.DMA((n_sem,)), pltpu.SemaphoreType.DMA((n_sem,)),
                   *[pltpu.HBM(s.shape, s.dtype) for s in srcs], *[pltpu.HBM(shp, s.dtype) for shp, s in zip(shapes, srcs)],
                   jax.ShapeDtypeStruct((8, LANES), F32)),
        in_specs=(HBM,) * (2 * n_w), out_specs=(SEM, SEM) + (HBM,) * (2 * n_w) + (pl.BlockSpec(memory_space=pltpu.VMEM),),
        input_output_aliases={i: 2 + i for i in range(2 * n_w)},
        compiler_params=pltpu.CompilerParams(has_side_effects=EFFECT),
    )(*[pltpu.with_memory_space_constraint(s, pltpu.HBM) for s in srcs],
      *[pltpu.with_memory_space_constraint(lax.empty(shp, s.dtype), pltpu.HBM) for shp, s in zip(shapes, srcs)])
    return (kind, n_w, out[:-1]), out[-1]


def exchange_wait(name, handle, after):
    kind, n_w, (send_sems, recv_sems, *thru) = handle
    n_sem = n_w if kind == "swap" else 3 * n_w

    def body(*refs):
        ins, lands = refs[:n_w], refs[n_w:2 * n_w]
        send_sems, recv_sems = refs[2 * n_w:2 * n_w + 2]
        for i, (src, dst, dev) in enumerate(_exchange_copies(kind, ins, lands)):
            cp = pltpu.make_async_remote_copy(src_ref=src, dst_ref=dst, send_sem=send_sems.at[i], recv_sem=recv_sems.at[i],
                                              device_id=dev, device_id_type=MESH)
            cp.wait_send()
            cp.wait_recv()

    out = pl.pallas_call(
        body, name=name, out_shape=tuple(pltpu.HBM(t.shape, t.dtype) for t in thru),
        in_specs=(HBM,) * (2 * n_w) + (SEM, SEM, pl.BlockSpec(memory_space=pl.ANY)), out_specs=(HBM,) * (2 * n_w),
        input_output_aliases={i: i for i in range(2 * n_w)},
        compiler_params=pltpu.CompilerParams(has_side_effects=EFFECT),
    )(*thru, send_sems, recv_sems, after)
    return list(out[n_w:])


def _row_tile(rows, cols):
    best = 16
    for t in range(16, rows + 1, 16):
        if rows % t == 0 and t * cols * 4 <= 1024 * 1024:
            best = t
    return best


def pair_sum(name, pos, grad, from_sibling):
    _, rows, cols = grad.shape
    half = rows // 2
    tr = _row_tile(half, cols)
    n_t = half // tr

    def body(pos_ref, g_ref, s_ref, b_ref, f_ref):
        tot = g_ref[...] + s_ref[...]
        b_ref[...] = tot.astype(BF16)

        @pl.when(pl.program_id(1) == pos_ref[1])
        def _():
            f_ref[...] = tot[0]

    blk = pl.BlockSpec((1, tr, cols), lambda i, k, pos: (k, i, 0))
    return pl.pallas_call(
        body, grid_spec=pltpu.PrefetchScalarGridSpec(
            num_scalar_prefetch=1, grid=(n_t, N_CHIPS),
            in_specs=[pl.BlockSpec((1, tr, cols), lambda i, k, pos: (k, pos[0] * n_t + i, 0)), blk],
            out_specs=[blk, pl.BlockSpec((tr, cols), lambda i, k, pos: (i, 0))]),
        out_shape=[jax.ShapeDtypeStruct((N_CHIPS, half, cols), BF16), jax.ShapeDtypeStruct((half, cols), F32)],
        name=name, compiler_params=_cparams(2),
    )(pos, grad, from_sibling)


def chip_sum(name, pos, own, landed):
    half, cols = own.shape
    tr = _row_tile(half, cols)
    n_t = half // tr

    def body(pos_ref, p_ref, l_ref, o_ref):
        o_ref[...] = ((p_ref[...] + l_ref[0].astype(F32)) + l_ref[1].astype(F32)) + l_ref[2].astype(F32)

    return pl.pallas_call(
        body, grid_spec=pltpu.PrefetchScalarGridSpec(
            num_scalar_prefetch=1, grid=(n_t,),
            in_specs=[pl.BlockSpec((tr, cols), lambda i, pos: (i, 0)), pl.BlockSpec((3, tr, cols), lambda i, pos: (0, i, 0))],
            out_specs=pl.BlockSpec((tr, cols), lambda i, pos: (pos[0] * n_t + i, 0))),
        out_shape=jax.ShapeDtypeStruct((2 * half, cols), F32), name=name, compiler_params=_cparams(1),
    )(pos, own, landed)


def reduce_scatter_layer(li, pos, grads):
    n = lambda t: f"{t}_l{li}"
    from_sibling = swap_halves(n("swap_halves"), grads)
    sums = [pair_sum(n(f"pair_sum{w}"), pos, g, s) for w, (g, s) in enumerate(zip(grads, from_sibling))]
    landed = scatter_chips(n("scatter_chips"), [b for b, _ in sums])
    halves = [chip_sum(n(f"chip_sum{w}"), pos, own, l) for w, ((_, own), l) in enumerate(zip(sums, landed))]
    return share_halves(n("share_halves"), halves)


class OverlappedReduceScatter:
    def __init__(self, li, pos, grads):
        self.n = lambda t: f"{t}_l{li}"
        self.pos, self.grads = pos, grads
        self.swap, self.token = exchange_start(self.n("swap_start"), "swap", grads)

    def middle(self, after):
        from_sibling = exchange_wait(self.n("swap_wait"), self.swap, after)
        self.sums = [pair_sum(self.n(f"pair_sum{w}"), self.pos, g, s) for w, (g, s) in enumerate(zip(self.grads, from_sibling))]
        self.scatter, self.token = exchange_start(self.n("scatter_start"), "scatter", [b for b, _ in self.sums])

    def finish(self, after):
        landed = exchange_wait(self.n("scatter_wait"), self.scatter, after)
        halves = [chip_sum(self.n(f"chip_sum{w}"), self.pos, own, l) for w, ((_, own), l) in enumerate(zip(self.sums, landed))]
        return share_halves(self.n("share_halves"), halves)


def sum_devices(gathered):
    m_per = gathered.shape[0] // 8

    def body(g_ref, o_ref):
        tot = g_ref[pl.ds(0, m_per), :]
        for dev in range(1, 8):
            tot = tot + g_ref[pl.ds(dev * m_per, m_per), :]
        o_ref[...] = tot

    return pl.pallas_call(
        body, out_shape=jax.ShapeDtypeStruct((m_per, gathered.shape[1]), F32),
        in_specs=[pl.BlockSpec(memory_space=pltpu.VMEM)], out_specs=pl.BlockSpec(memory_space=pltpu.VMEM), name="sum_devices",
    )(gathered)


def kernel(x, p, g_mix, w_in, b_fox_f, fox_q_gain, fox_k_gain, sc_conv_w, dn_conv_w, dn_a_log, dn_dt_bias, dn_norm_gain, w_branch, w_o, g_ffn, w_up, ffn_conv_w, w_down, g_ple, w_ple_gate, w_ple, loss_target, m_g_mix, m_w_in, m_b_fox_f, m_fox_q_gain, m_fox_k_gain, m_sc_conv_w, m_dn_conv_w, m_dn_a_log, m_dn_dt_bias, m_dn_norm_gain, m_w_branch, m_w_o, m_g_ffn, m_w_up, m_ffn_conv_w, m_w_down, m_g_ple, m_w_ple_gate, m_w_ple, v_g_mix, v_w_in, v_b_fox_f, v_fox_q_gain, v_fox_k_gain, v_sc_conv_w, v_dn_conv_w, v_dn_a_log, v_dn_dt_bias, v_dn_norm_gain, v_w_branch, v_w_o, v_g_ffn, v_w_up, v_ffn_conv_w, v_w_down, v_g_ple, v_w_ple_gate, v_w_ple):
    a = dict(g_mix=g_mix, w_in=w_in, b_fox_f=b_fox_f, fox_q_gain=fox_q_gain, fox_k_gain=fox_k_gain, sc_conv_w=sc_conv_w,
             dn_conv_w=dn_conv_w, dn_a_log=dn_a_log, dn_dt_bias=dn_dt_bias, dn_norm_gain=dn_norm_gain, w_branch=w_branch, w_o=w_o,
             g_ffn=g_ffn, w_up=w_up, ffn_conv_w=ffn_conv_w, w_down=w_down, g_ple=g_ple, w_ple_gate=w_ple_gate, w_ple=w_ple)
    mom = dict(g_mix=m_g_mix, w_in=m_w_in, b_fox_f=m_b_fox_f, fox_q_gain=m_fox_q_gain, fox_k_gain=m_fox_k_gain, sc_conv_w=m_sc_conv_w,
               dn_conv_w=m_dn_conv_w, dn_a_log=m_dn_a_log, dn_dt_bias=m_dn_dt_bias, dn_norm_gain=m_dn_norm_gain, w_branch=m_w_branch,
               w_o=m_w_o, g_ffn=m_g_ffn, w_up=m_w_up, ffn_conv_w=m_ffn_conv_w, w_down=m_w_down, g_ple=m_g_ple, w_ple_gate=m_w_ple_gate,
               w_ple=m_w_ple)
    var = dict(g_mix=v_g_mix, w_in=v_w_in, b_fox_f=v_b_fox_f, fox_q_gain=v_fox_q_gain, fox_k_gain=v_fox_k_gain, sc_conv_w=v_sc_conv_w,
               dn_conv_w=v_dn_conv_w, dn_a_log=v_dn_a_log, dn_dt_bias=v_dn_dt_bias, dn_norm_gain=v_dn_norm_gain, w_branch=v_w_branch,
               w_o=v_w_o, g_ffn=v_g_ffn, w_up=v_w_up, ffn_conv_w=v_ffn_conv_w, w_down=v_w_down, g_ple=v_g_ple, w_ple_gate=v_w_ple_gate,
               w_ple=v_w_ple)
    cx, cy, cc = lax.axis_index("x"), lax.axis_index("y"), lax.axis_index("c")
    chip = 2 * cx + cy
    pos = jnp.stack([cc, chip]).astype(jnp.int32)

    def as_blocks(t):
        return t.reshape(2, -1, t.shape[-1])

    def own_block_in(got, shards):
        return [lax.dynamic_update_slice(g, s[None], (chip, 0, 0)) for g, s in zip(got, shards)]

    shards0 = [as_blocks(a[nm])[0].astype(BF16) for nm in BIG]
    got0, gathered_token = gather_layer("gather_weights_l0", shards0)
    shards1 = [(as_blocks(a[nm])[1] + gathered_token[0, 0]).astype(BF16) for nm in BIG]
    gather1, gather1_token = exchange_start("gather_start_l1", "gather", shards1)
    conv_shapes = [a[nm].shape for nm in CONVS]
    conv_all = gather_small("gather_conv_w", pack_rows([a[nm] for nm in CONVS], F32))
    conv_rows = conv_all.shape[0] // 8
    conv_chip = [unpack_rows(conv_all[2 * k * conv_rows:(2 * k + 1) * conv_rows], conv_shapes) for k in range(N_CHIPS)]
    conv = {nm: jnp.concatenate([conv_chip[k][i] for k in range(N_CHIPS)], axis=2) for i, nm in enumerate(CONVS)}

    weights, saved = [None, None], [None, None]
    weights[0] = hang_on(layer_weights(0, own_block_in(got0, shards0), conv, a), gather1_token)
    act, saved[0] = layer_fwd(0, x[0], p[0, 0], weights[0])
    got1 = exchange_wait("gather_wait_l1", gather1, act)
    weights[1] = layer_weights(1, own_block_in(got1, shards1), conv, a)
    act, saved[1] = layer_fwd(1, act, p[1, 0], weights[1])
    d_act, loss_part = loss_call(act, loss_target[0])
    loss = lax.psum(loss_part, ("x", "y", "c"))
    layer_grads, reduced = [None, None], [None, None]
    d_act, layer_grads[1] = layer_bwd(1, d_act, saved[1], weights[1])
    rs1 = OverlappedReduceScatter(1, pos, [layer_grads[1][nm] for nm in BIG])

    def stage_middle(after):
        rs1.middle(after)
        return rs1.token

    d_act, layer_grads[0] = layer_bwd(0, d_act, saved[0], hang_on(weights[0], rs1.token), mid_hook=stage_middle)
    reduced[1] = rs1.finish(d_act)
    reduced[0] = reduce_scatter_layer(0, pos, [layer_grads[0][nm] for nm in BIG])
    grad_x = d_act[None]

    def both(nm):
        return jnp.stack([layer_grads[0][nm], layer_grads[1][nm]])

    local = {nm: both(nm) for nm in ("g_mix", "b_fox_f", "fox_q_gain", "fox_k_gain", "dn_norm_gain", "g_ffn", "g_ple", "sc_conv_w",
                                      "dn_conv_w", "ffn_conv_w")}
    local["dn_a_log"] = jnp.stack([layer_grads[li]["ad"][0] for li in range(2)])
    local["dn_dt_bias"] = jnp.stack([layer_grads[li]["ad"][1] for li in range(2)])

    small_names = SMALL + CONVS
    small_shapes = [local[nm].shape for nm in small_names]
    small_sum = sum_devices(gather_small("gather_small_grads", pack_rows([local[nm] for nm in small_names], F32)))
    small_grads = dict(zip(small_names, unpack_rows(small_sum, small_shapes)))
    for nm in CONVS:
        width = a[nm].shape[2]
        small_grads[nm] = lax.dynamic_slice_in_dim(small_grads[nm], chip * width, width, axis=2)

    grads, deltas, new_m, new_v = dict(small_grads), {}, {}, {}
    for nm in small_names:
        deltas[nm], new_m[nm], new_v[nm] = adam_call(f"adam_{nm}", a[nm], grads[nm], mom[nm], var[nm])
    for i, nm in enumerate(BIG):
        res = adam_layers(f"adam_{nm}", as_blocks(a[nm]), as_blocks(mom[nm]), as_blocks(var[nm]), reduced[0][i], reduced[1][i])
        grads[nm], deltas[nm], new_m[nm], new_v[nm] = [r.reshape(a[nm].shape) for r in res]
    return (loss, grad_x, *[grads[nm] for nm in WEIGHTS], *[deltas[nm] for nm in WEIGHTS], *[new_m[nm] for nm in WEIGHTS],
            *[new_v[nm] for nm in WEIGHTS])
```

```python
import functools

import jax
import jax.numpy as jnp
from jax import lax
from jax.experimental import pallas as pl
from jax.experimental.pallas import tpu as pltpu

F32 = jnp.float32
BF16 = jnp.bfloat16
HI = lax.Precision.HIGHEST
MESH = pl.DeviceIdType.MESH

D_MODEL = 1024
BRANCH = 512
FOX_DH = 64
DN_DH = 128
DN_HEADS = 4
DN_CHUNK = 64
FOX_BLOCK = 128
D_FF = 2816
EPS = 1e-6
N_CHIPS = 4
LANES = 128

ADAM_LR, ADAM_B1, ADAM_B2, ADAM_EPS, ADAM_WD, ADAM_STEP = 0.001, 0.9, 0.999, 1e-08, 0.01, 10

VMEM_LIMIT = 56 * 1024 * 1024

C_FQ, C_FK, C_FV, C_SB, C_SC, C_SV, C_DN, C_DZ, C_GATE = 0, 512, 1024, 1536, 2048, 2560, 3072, 4608, 5120
IN_MAIN = 8192
IN_SIZES = (1536, 8, 1536, 1536, 4, 4, 512, 3072)

BIG = ("w_in", "w_branch", "w_o", "w_up", "w_down", "w_ple_gate", "w_ple")
BIG_AXIS = {"w_in": 2, "w_branch": 3, "w_o": 1, "w_up": 2, "w_down": 1, "w_ple_gate": 1, "w_ple": 2}
CONVS = ("sc_conv_w", "dn_conv_w", "ffn_conv_w")
SMALL = ("g_mix", "b_fox_f", "fox_q_gain", "fox_k_gain", "dn_a_log", "dn_dt_bias", "dn_norm_gain", "g_ffn", "g_ple")
WEIGHTS = ("g_mix", "w_in", "b_fox_f", "fox_q_gain", "fox_k_gain", "sc_conv_w", "dn_conv_w", "dn_a_log", "dn_dt_bias",
           "dn_norm_gain", "w_branch", "w_o", "g_ffn", "w_up", "ffn_conv_w", "w_down", "g_ple", "w_ple_gate", "w_ple")


def _iota(shape, dim):
    return lax.broadcasted_iota(jnp.int32, shape, dim)


def _dg(a, b, mode, prec=None):
    dims = {"nn": ((1,), (0,)), "nt": ((1,), (1,)), "tn": ((0,), (0,))}[mode]
    return lax.dot_general(a, b, (dims, ((), ())), precision=prec, preferred_element_type=F32)


def _bdot_impl(a, b, mode):
    return _dg(a.astype(BF16), b.astype(BF16), mode)


@functools.partial(jax.custom_vjp, nondiff_argnums=(2,))
def _bdot_diff(a, b, mode):
    return _bdot_impl(a, b, mode)


def _bdot_fwd(a, b, mode):
    return _bdot_impl(a, b, mode), (a, b)


def _bdot_bwd(mode, res, g):
    a, b = res
    if mode == "nn":
        da, db = _bdot_impl(g, b, "nt"), _bdot_impl(a, g, "tn")
    elif mode == "nt":
        da, db = _bdot_impl(g, b, "nn"), _bdot_impl(g, a, "tn")
    else:
        da, db = _bdot_impl(b, g, "nt"), _bdot_impl(a, g, "nn")
    return da.astype(a.dtype), db.astype(b.dtype)


_bdot_diff.defvjp(_bdot_fwd, _bdot_bwd)


def _bdot(d):
    return _bdot_diff if d else _bdot_impl


def _shift_impl(x, k):
    return jnp.where(_iota(x.shape, 0) >= k, pltpu.roll(x, k, 0), 0.0)


def _unshift_impl(g, k):
    n = g.shape[0]
    return jnp.where(_iota(g.shape, 0) < n - k, pltpu.roll(g, n - k, 0), 0.0)


@functools.partial(jax.custom_vjp, nondiff_argnums=(1,))
def _shift_diff(x, k):
    return _shift_impl(x, k)


_shift_diff.defvjp(lambda x, k: (_shift_impl(x, k), None), lambda k, _, g: (_unshift_impl(g, k),))


def _row(w, j):
    return jnp.sum(jnp.where(_iota(w.shape, 0) == j, w, 0.0), axis=0, keepdims=True)


def _col(w, j):
    return jnp.sum(jnp.where(_iota(w.shape, 1) == j, w, 0.0), axis=1, keepdims=True)


def _conv(d, x, w):
    shift = _shift_diff if d else _shift_impl
    taps = w.shape[0]
    y = x * _row(w, taps - 1)
    for j in range(taps - 1):
        y = y + shift(x, taps - 1 - j) * _row(w, j)
    return y


def _softplus(x):
    return jnp.maximum(x, 0.0) + jnp.log(1.0 + jnp.exp(-jnp.abs(x)))


def _silu(x):
    return x * jax.nn.sigmoid(x)


def _rms(x, gain):
    return x * lax.rsqrt(jnp.mean(x * x, axis=-1, keepdims=True) + EPS) * gain


def _rms_fn(d, pids, x, gain):
    return (_rms(x, gain),)


def _loss_fn(d, pids, y, t):
    e = y - t
    part = 0.5 / D_MODEL * jnp.sum(e * e, keepdims=True)
    return e * (1.0 / D_MODEL), jnp.broadcast_to(part, (8, LANES))


def _fox_prep_fn(d, pids, q, k, gq, gk):
    first = _iota(q.shape, 1) < FOX_DH

    def norm(x, gain):
        sq = x * x
        ss_a = jnp.sum(jnp.where(first, sq, 0.0), axis=1, keepdims=True)
        ss_b = jnp.sum(jnp.where(first, 0.0, sq), axis=1, keepdims=True)
        rs = jnp.where(first, lax.rsqrt(ss_a / FOX_DH + EPS), lax.rsqrt(ss_b / FOX_DH + EPS))
        return x * rs * gain

    return norm(q, gq) * FOX_DH ** -0.5, norm(k, gk)


def _fox_gate_fn(d, pids, f, bias):
    logf = -_softplus(-(f + bias))
    n_r, n_c = logf.shape
    tri = (_iota((n_c, n_c), 0) <= _iota((n_c, n_c), 1)).astype(F32)
    within = _dg(logf, tri, "nn", HI)
    tot = jnp.broadcast_to(jnp.sum(logf, axis=1, keepdims=True), logf.shape)
    below = (_iota((n_r, n_r), 1) < _iota((n_r, n_r), 0)).astype(F32)
    return (within + _dg(below, tot, "nn", HI),)


def _fox_attn_fn(q_block0, d, pids, q, k, v, cq_a, cq_b, ck_a, ck_b):
    dot = _bdot(d)
    first = _iota(q.shape, 1) < FOX_DH
    n_q, n_k = q.shape[0], k.shape[0]
    causal = ((q_block0 + pids[1]) * n_q + _iota((n_q, n_k), 0)) >= _iota((n_q, n_k), 1)

    qs = [jnp.where(first, q, 0.0), jnp.where(first, 0.0, q)]
    s = _each(lambda qh, cq, ck: jnp.where(causal, dot(qh, k, "nt") + cq - ck, -1e30), qs, [cq_a, cq_b], [ck_a, ck_b])
    e = [jnp.exp(si - lax.stop_gradient(jnp.max(si, axis=1, keepdims=True))) for si in s]
    o_a, o_b = [dot(ei / jnp.sum(ei, axis=1, keepdims=True), v, "nn") for ei in e]
    return (jnp.where(first, o_a, o_b),)


def _sconv_fn(d, pids, sb, sc, sv, w):
    return (sb * _conv(d, sc * sv, w),)


def _dnconv_fn(d, pids, x, w):
    return (_silu(_conv(d, x, w)),)


def _merge_fn(d, pids, y0, y1, y2, g0, g1, g2):
    return (jax.nn.sigmoid(g0) * y0 + jax.nn.sigmoid(g1) * y1 + jax.nn.sigmoid(g2) * y2,)


def _ffn_act_fn(d, pids, ug, uv, wg, wv):
    return (_silu(_conv(d, ug, wg)) * _conv(d, uv, wv),)


def _ple_fn(d, pids, gpre, pe, x):
    return (x + jax.nn.sigmoid(gpre) * pe,)


def _adam_fn(d, pids, w, g, m, v):
    m2 = ADAM_B1 * m + (1.0 - ADAM_B1) * g
    v2 = ADAM_B2 * v + (1.0 - ADAM_B2) * (g * g)
    m_hat = m2 / (1.0 - ADAM_B1 ** ADAM_STEP)
    v_hat = v2 / (1.0 - ADAM_B2 ** ADAM_STEP)
    delta = -ADAM_LR * (m_hat / (jnp.sqrt(v_hat) + ADAM_EPS) + ADAM_WD * w)
    return delta, m2, v2


def _each(fn, *lists):
    return [fn(*args) for args in zip(*lists)]


def _tri_inv_impl(mats):
    n = mats[0].shape[0]
    r, c = _iota((n, n), 0), _iota((n, n), 1)
    diag_blk = (r >> 4) == (c >> 4)
    eye = (r == c).astype(F32)
    mm = lambda us, ws: _each(lambda u, w: _dg(u, w, "nn", HI), us, ws)
    grow = lambda ps, xs: _each(lambda p, px: p + px, ps, mm(ps, xs))
    x = [jnp.where(diag_blk, -a, 0.0) for a in mats]
    p = [eye + xi for xi in x]
    x2 = mm(x, x)
    p = grow(p, x2)
    x4 = mm(x2, x2)
    p = grow(p, x4)
    p = grow(p, mm(x4, x4))
    y = [-yi for yi in mm(p, [jnp.where(diag_blk, 0.0, a) for a in mats])]
    q = grow([eye + yi for yi in y], mm(y, y))
    return mm(q, p)


@jax.custom_vjp
def _tri_inv_diff(mats):
    return _tri_inv_impl(mats)


def _tri_inv_fwd(mats):
    ts = _tri_inv_impl(mats)
    return ts, ts


def _tri_inv_bwd(ts, gs):
    left = _each(lambda t, g: _dg(t, g, "tn", HI), ts, gs)
    return ([-m for m in _each(lambda l, t: _dg(l, t, "nt", HI), left, ts)],)


_tri_inv_diff.defvjp(_tri_inv_fwd, _tri_inv_bwd)


def _dn_local(d, qs, ks, vs, a_cs, a_rs, b_cs, a_logs, dt_bs):
    dot = _bdot(d)
    inv = _tri_inv_diff if d else _tri_inv_impl
    n = qs[0].shape[0]
    r, c = _iota((n, n), 0), _iota((n, n), 1)
    incl, strict, upper = r >= c, r > c, r <= c
    qs = [q * lax.rsqrt(jnp.sum(q * q, axis=1, keepdims=True) + EPS) * DN_DH ** -0.5 for q in qs]
    ks = [k * lax.rsqrt(jnp.sum(k * k, axis=1, keepdims=True) + EPS) for k in ks]
    betas = [jax.nn.sigmoid(b) for b in b_cs]
    rates = [-jnp.exp(a) for a in a_logs]
    g_cs = _each(lambda rate, a, dt: rate * _softplus(a + dt), rates, a_cs, dt_bs)
    g_rs = _each(lambda rate, a, dt: rate * _softplus(a + dt), rates, a_rs, dt_bs)
    gcum_cs = [jnp.sum(jnp.where(incl, g, 0.0), axis=1, keepdims=True) for g in g_rs]
    gcum_rs = [jnp.sum(jnp.where(upper, g, 0.0), axis=0, keepdims=True) for g in g_cs]
    decays = _each(lambda gc, gr: jnp.exp(jnp.where(incl, gc - gr, -1e30)), gcum_cs, gcum_rs)
    kbs = _each(lambda k, b: k * b, ks, betas)
    kk = _each(lambda kb, k: dot(kb, k, "nt"), kbs, ks)
    ts = inv(_each(lambda m, dec: jnp.where(strict, m * dec, 0.0), kk, decays))
    e_gs = [jnp.exp(g) for g in gcum_cs]
    us = _each(lambda t, v, b: _dg(t, v * b, "nn", HI), ts, vs, betas)
    k_cums = _each(lambda t, kb, e: _dg(t, kb * e, "nn", HI), ts, kbs, e_gs)
    qk = _each(lambda q, k: dot(q, k, "nt"), qs, ks)
    qk = _each(lambda m, dec: jnp.where(incl, m * dec, 0.0), qk, decays)
    g_lasts = [jnp.sum(g, axis=0, keepdims=True) for g in g_cs]
    q_decs = _each(lambda q, e: q * e, qs, e_gs)
    k_decs = _each(lambda k, gl, gc: k * jnp.exp(gl - gc), ks, g_lasts, gcum_cs)
    return list(zip(us, k_cums, q_decs, k_decs, qk, g_lasts))


def _dn_step(d, s_prevs, items, zs, gain):
    dot = _bdot(d)
    us, k_cums, q_decs, k_decs, qks, g_lasts = [list(t) for t in zip(*items)]
    v_news = _each(lambda u, kc, s: u - dot(kc, s, "nn"), us, k_cums, s_prevs)
    inter = _each(lambda qd, s: dot(qd, s, "nn"), q_decs, s_prevs)
    outs = _each(lambda o, qk, vn: o + dot(qk, vn, "nn"), inter, qks, v_news)
    s_nexts = _each(lambda s, gl, kd, vn: s * jnp.exp(gl) + dot(kd, vn, "tn"), s_prevs, g_lasts, k_decs, v_news)
    return _each(lambda o, z: _rms(o, gain) * _silu(z), outs, zs), s_nexts


def _split_heads(t):
    return [t[:, h * DN_DH:(h + 1) * DN_DH] for h in range(t.shape[1] // DN_DH)]


def _dn_gates(ps, a_rows, ad):
    hs = range(DN_HEADS)
    return ([_col(ps, 12 + h) for h in hs], [_row(a_rows, h) for h in hs], [_col(ps, 8 + h) for h in hs],
            [_col(_row(ad, 0), h) for h in hs], [_col(_row(ad, 1), h) for h in hs])


def _head_rows(vals):
    row = _iota((8, LANES), 0)
    tile = jnp.zeros((8, LANES), F32)
    for h, val in enumerate(vals):
        tile = tile + jnp.where(row == h, val, 0.0)
    return tile


def _cparams(n_axes):
    return pltpu.CompilerParams(dimension_semantics=("arbitrary",) * n_axes, vmem_limit_bytes=VMEM_LIMIT)


def _first_visit(acc_axes):
    cond = None
    for a in acc_axes:
        here = pl.program_id(a) == 0
        cond = here if cond is None else jnp.logical_and(cond, here)
    return cond


def _tile(ref):
    val = ref[...]
    shape = val.shape
    while len(shape) > 2 and shape[0] == 1:
        shape = shape[1:]
    return val.reshape(shape)


def _store(ref, val, first):
    val = val.astype(ref.dtype).reshape(ref.shape)
    if first is None:
        ref[...] = val
        return

    @pl.when(first)
    def _():
        ref[...] = val

    @pl.when(jnp.logical_not(first))
    def _():
        ref[...] += val


def _specs(ops):
    return [pl.BlockSpec(block, imap) for _, block, imap in ops]


def tile_fwd(name, fn, grid, ins, outs):
    n_in = len(ins)

    def body(*refs):
        pids = tuple(pl.program_id(a) for a in range(len(grid)))
        firsts = [_first_visit(o[4]) if o[4] else None for o in outs]
        res = fn(False, pids, *[_tile(r) for r in refs[:n_in]])
        for ref, val, first in zip(refs[n_in:], res, firsts):
            _store(ref, val, first)

    out = pl.pallas_call(
        body, grid=grid, in_specs=_specs(ins),
        out_specs=[pl.BlockSpec(o[2], o[3]) for o in outs],
        out_shape=[jax.ShapeDtypeStruct(o[0], o[1]) for o in outs],
        name=name, compiler_params=_cparams(len(grid)),
    )(*[a for a, _, _ in ins])
    return out


def tile_bwd(name, fn, grid, ins, cots, diff, adds=None):
    adds = adds or {}
    n_in, n_cot = len(ins), len(cots)
    add_pos = sorted(adds)
    diff_idx = [d[0] for d in diff]
    out_desc = [d[2] if len(d) > 2 and d[2] is not None else (ins[d[0]][0].shape, ins[d[0]][1], ins[d[0]][2]) for d in diff]
    out_dtypes = [d[3] if len(d) > 3 else F32 for d in diff]

    def body(*refs):
        pids = tuple(pl.program_id(a) for a in range(len(grid)))
        firsts = [_first_visit(d[1]) if d[1] else None for d in diff]
        vals = [_tile(r) for r in refs[:n_in]]
        cot_vals = [_tile(r) for r in refs[n_in:n_in + n_cot]]
        add_vals = [_tile(r) for r in refs[n_in + n_cot:n_in + n_cot + len(add_pos)]]
        out_refs = refs[n_in + n_cot + len(add_pos):]

        def f(*dv):
            full = list(vals)
            for i, val in zip(diff_idx, dv):
                full[i] = val
            return fn(True, pids, *full)

        prim, vjp = jax.vjp(f, *[vals[i].astype(F32) for i in diff_idx])
        grads = list(vjp(tuple(c.astype(o.dtype) for c, o in zip(cot_vals, prim))))
        for pos, val in zip(add_pos, add_vals):
            grads[pos] = grads[pos] + val.astype(F32)
        for ref, val, first in zip(out_refs, grads, firsts):
            _store(ref, val, first)

    all_ins = list(ins) + list(cots) + [adds[p] for p in add_pos]
    out = pl.pallas_call(
        body, grid=grid, in_specs=_specs(all_ins),
        out_specs=[pl.BlockSpec(o[1], o[2]) for o in out_desc],
        out_shape=[jax.ShapeDtypeStruct(o[0], dt) for o, dt in zip(out_desc, out_dtypes)],
        name=name, compiler_params=_cparams(len(grid)),
    )(*[a for a, _, _ in all_ins])
    return out


def _pick(dim, cands):
    for c in cands:
        if dim % c == 0:
            return c
    return dim


MM_TILES = (1024, 512, 1408, 256, 128)


def mm(name, a, b, mode, add=None, out_dtype=F32, blocks=None):
    wide = None
    if mode == "nn":
        (m, kk), n = a.shape, b.shape[-1]
    elif mode == "nt":
        (m, kk), n = a.shape, b.shape[-2]
    else:
        (kk, m), n = a.shape, b.shape[1]
    if blocks is not None:
        lo, n_blk = blocks
        wide = b.shape[-1] if mode != "tn" else n // n_blk
        if mode == "nn":
            n = wide * n_blk
    tm = _pick(m, MM_TILES)
    if mode == "nt" and blocks is not None:
        tn, tk = _pick(n, MM_TILES), _pick(wide, MM_TILES[:-1])
    elif blocks is not None:
        tn, tk = _pick(wide, MM_TILES[:-1]), _pick(kk, MM_TILES)
    else:
        tn, tk = _pick(n, MM_TILES), _pick(kk, MM_TILES)
    nk = kk // tk
    a_spec = pl.BlockSpec((tk, tm), lambda i, j, k: (k, i)) if mode == "tn" else pl.BlockSpec((tm, tk), lambda i, j, k: (i, k))
    o_spec = pl.BlockSpec((tm, tn), lambda i, j, k: (i, j))
    out_shape = (m, n)
    if blocks is None:
        b_spec = pl.BlockSpec((tn, tk), lambda i, j, k: (j, k)) if mode == "nt" else pl.BlockSpec((tk, tn), lambda i, j, k: (k, j))
    elif mode == "nn":
        per = wide // tn
        b_spec = pl.BlockSpec((1, tk, tn), lambda i, j, k: (lo + j // per, k, j % per))
    elif mode == "nt":
        per = wide // tk
        b_spec = pl.BlockSpec((1, tn, tk), lambda i, j, k: (lo + k // per, j, k % per))
    else:
        per = wide // tn
        b_spec = pl.BlockSpec((tk, tn), lambda i, j, k: (k, j))
        o_spec = pl.BlockSpec((1, tm, tn), lambda i, j, k: (j // per, i, j % per))
        out_shape = (n_blk, m, wide)

    def body(*refs):
        a_ref, b_ref = refs[0], refs[1]
        add_ref = refs[2] if add is not None else None
        o_ref, acc = refs[-2], refs[-1]
        k = pl.program_id(2)
        part = _bdot_impl(_tile(a_ref), _tile(b_ref), mode)

        @pl.when(k == 0)
        def _():
            acc[...] = part

        @pl.when(k > 0)
        def _():
            acc[...] += part

        @pl.when(k == nk - 1)
        def _():
            res = acc[...]
            if add_ref is not None:
                res = res + add_ref[...]
            o_ref[...] = res.astype(o_ref.dtype).reshape(o_ref.shape)

    operands = [a, b] + ([add] if add is not None else [])
    in_specs = [a_spec, b_spec] + ([o_spec] if add is not None else [])
    return pl.pallas_call(
        body, grid=(m // tm, n // tn, nk), in_specs=in_specs, out_specs=o_spec,
        out_shape=jax.ShapeDtypeStruct(out_shape, out_dtype),
        scratch_shapes=[pltpu.VMEM((tm, tn), F32)],
        name=name, compiler_params=_cparams(3),
    )(*operands)


def _rows(x, width=None, off=0, tm=256):
    width = x.shape[1] if width is None else width
    return (x, (tm, width), lambda i, off=off: (i, off))


def _whole(x):
    nd = x.ndim
    return (x, x.shape, lambda *pids, nd=nd: (0,) * nd)


def _rms_ops(x, gain):
    return [_rows(x), _whole(gain)]


def rms_fwd(name, x, gain):
    s, dm = x.shape
    return tile_fwd(name, _rms_fn, (s // 256,), _rms_ops(x, gain), [((s, dm), BF16, (256, dm), lambda i: (i, 0), ())])[0]


def rms_bwd(name, x, gain, dh, dres):
    s = x.shape[0]
    return tile_bwd(name, _rms_fn, (s // 256,), _rms_ops(x, gain), [_rows(dh)], [(0, ()), (1, (0,))], adds={0: _rows(dres)})


def loss_call(y, t):
    s, dm = y.shape
    dy, part = tile_fwd("loss", _loss_fn, (s // 256,), [_rows(y), _rows(t)],
                        [((s, dm), F32, (256, dm), lambda i: (i, 0), ()), ((8, LANES), F32, (8, LANES), lambda i: (0, 0), (0,))])
    return dy, part[0, 0]


def _fox_prep_ops(pm, gq, gk):
    tm = 512
    return [(pm, (tm, LANES), lambda i, j: (i, C_FQ // LANES + j)), (pm, (tm, LANES), lambda i, j: (i, C_FK // LANES + j)),
            _whole(gq), _whole(gk)]


def fox_prep_fwd(name, pm, gq, gk):
    s = pm.shape[0]
    out = ((s, BRANCH), BF16, (512, LANES), lambda i, j: (i, j), ())
    return tile_fwd(name, _fox_prep_fn, (s // 512, 4), _fox_prep_ops(pm, gq, gk), [out, out])


def fox_prep_bwd(name, pm, gq, gk, dqn, dkn):
    s = pm.shape[0]
    cot = lambda g: (g, (512, LANES), lambda i, j: (i, j))
    own = ((s, BRANCH), (512, LANES), lambda i, j: (i, j))
    return tile_bwd(name, _fox_prep_fn, (s // 512, 4), _fox_prep_ops(pm, gq, gk), [cot(dqn), cot(dkn)],
                    [(0, (), own, BF16), (1, (), own, BF16), (2, (0, 1)), (3, (0, 1))])


def _fox_gate_ops(f_t, bias):
    return [(f_t, (1,) + f_t.shape[1:], lambda h: (h, 0, 0)), (bias, (1, 1, 1), lambda h: (h, 0, 0))]


def fox_gate_fwd(name, f_t, bias):
    n_h = f_t.shape[0]
    return tile_fwd(name, _fox_gate_fn, (n_h,), _fox_gate_ops(f_t, bias),
                    [(f_t.shape, F32, (1,) + f_t.shape[1:], lambda h: (h, 0, 0), ())])[0]


def fox_gate_bwd(name, f_t, bias, dcum):
    n_h = f_t.shape[0]
    return tile_bwd(name, _fox_gate_fn, (n_h,), _fox_gate_ops(f_t, bias),
                    [(dcum, (1,) + f_t.shape[1:], lambda h: (h, 0, 0))], [(0, ()), (1, ())])


FOX_GROUPS = 4


def _fox_groups(s):
    per = s // FOX_BLOCK // FOX_GROUPS
    return [(g * per, per, (g + 1) * per * FOX_BLOCK) for g in range(FOX_GROUPS)]


def _fox_attn_ops(qn, kn, pm, cum_c, cum_r, q0, keys):
    nb = FOX_BLOCK
    return [(qn, (nb, LANES), lambda p, i: (q0 + i, p)), (kn, (keys, LANES), lambda p, i: (0, p)),
            (pm, (keys, LANES), lambda p, i: (0, C_FV // LANES + p)),
            (cum_c, (1, nb, 1), lambda p, i: (2 * p, q0 + i, 0)), (cum_c, (1, nb, 1), lambda p, i: (2 * p + 1, q0 + i, 0)),
            (cum_r, (1, 1, keys), lambda p, i: (2 * p, 0, 0)), (cum_r, (1, 1, keys), lambda p, i: (2 * p + 1, 0, 0))]


def fox_attn_fwd(name, qn, kn, pm, cum_c, cum_r):
    s = qn.shape[0]
    parts = []
    for g, (q0, n_q, keys) in enumerate(_fox_groups(s)):
        parts.append(tile_fwd(f"{name}_g{g}", functools.partial(_fox_attn_fn, q0), (4, n_q), _fox_attn_ops(qn, kn, pm, cum_c, cum_r, q0, keys),
                              [((n_q * FOX_BLOCK, BRANCH), BF16, (FOX_BLOCK, LANES), lambda p, i: (i, p), ())])[0])
    return jnp.concatenate(parts, axis=0)


def fox_attn_bwd(name, qn, kn, pm, cum_c, cum_r, dy):
    s = qn.shape[0]
    d_qn, d_kn, d_v, d_cum = [], 0.0, 0.0, 0.0
    for g, (q0, n_q, keys) in enumerate(_fox_groups(s)):
        rows = n_q * FOX_BLOCK
        own_q = ((rows, BRANCH), (FOX_BLOCK, LANES), lambda p, i: (i, p))
        own_k = ((keys, BRANCH), (keys, LANES), lambda p, i: (0, p))
        pair_c = ((4, rows, 1), (1, FOX_BLOCK, 1), lambda p, i: (p, i, 0))
        pair_r = ((4, 1, keys), (1, 1, keys), lambda p, i: (p, 0, 0))
        g_qn, g_kn, g_v, g_cqa, g_cqb, g_cka, g_ckb = tile_bwd(
            f"{name}_g{g}", functools.partial(_fox_attn_fn, q0), (4, n_q), _fox_attn_ops(qn, kn, pm, cum_c, cum_r, q0, keys),
            [(dy, (FOX_BLOCK, LANES), lambda p, i, q0=q0: (q0 + i, p))],
            [(0, (), own_q), (1, (1,), own_k), (2, (1,), own_k), (3, (), pair_c), (4, (), pair_c), (5, (1,), pair_r), (6, (1,), pair_r)])
        d_qn.append(g_qn)
        tail = lambda t, axis: jnp.pad(t, [(0, s - keys) if ax == axis else (0, 0) for ax in range(t.ndim)])
        d_kn, d_v = d_kn + tail(g_kn, 0), d_v + tail(g_v, 0)
        by_q = jnp.stack([g_cqa[:, :, 0], g_cqb[:, :, 0]], axis=1).reshape(8, rows)
        by_k = jnp.stack([g_cka[:, 0, :], g_ckb[:, 0, :]], axis=1).reshape(8, keys)
        d_cum = d_cum + jnp.pad(by_q, [(0, 0), (q0 * FOX_BLOCK, s - q0 * FOX_BLOCK - rows)]) + tail(by_k, 1)
    return jnp.concatenate(d_qn, axis=0), d_kn, d_v, d_cum


def sconv_ops(pm, w):
    s = pm.shape[0]
    blk = lambda c0: (pm, (s, LANES), lambda j, c0=c0: (0, c0 // LANES + j))
    return [blk(C_SB), blk(C_SC), blk(C_SV), (w, (w.shape[0], LANES), lambda j: (0, j))]


def dnconv_ops(pm, w):
    s = pm.shape[0]
    return [(pm, (s, LANES), lambda j: (0, C_DN // LANES + j)), (w, (w.shape[0], LANES), lambda j: (0, j))]


def ffn_ops(ug, uv, w):
    s = ug.shape[0]
    n_t = D_FF // LANES
    return [(ug, (s, LANES), lambda j: (0, j)), (uv, (s, LANES), lambda j: (0, j)),
            (w, (w.shape[0], LANES), lambda j: (0, j)), (w, (w.shape[0], LANES), lambda j: (0, n_t + j))]


def _col_out(s, width, dtype=F32):
    return ((s, width), dtype, (s, LANES), lambda j: (0, j), ())


def _col_cot(g):
    return (g, (g.shape[0], LANES), lambda j: (0, j))


def merge_ops(yp, pm):
    gate = lambda b: (pm, (256, D_MODEL), lambda i, b=b: (i, C_GATE // D_MODEL + b))
    return [_rows(yp[0]), _rows(yp[1]), _rows(yp[2]), gate(0), gate(1), gate(2)]


def ple_ops(gpre, pe, x):
    return [_rows(gpre), _rows(pe), _rows(x)]


def adam_call(name, w, g, m, v):
    shape = w.shape
    last = shape[-1]
    rows = w.size // last
    flat = lambda t: t.reshape(rows, last)
    tm = rows
    for cand in (512, 256, 128, 64, 32, 16, 8):
        if rows % cand == 0 and cand * last * 4 <= 2 * 1024 * 1024:
            tm = cand
            break
    spec = lambda t: (flat(t), (tm, last), lambda i: (i, 0))
    out = ((rows, last), F32, (tm, last), lambda i: (i, 0), ())
    res = tile_fwd(name, _adam_fn, (rows // tm,), [spec(w), spec(g), spec(m), spec(v)], [out, out, out])
    return [r.reshape(shape) for r in res]


def _adam_layers_fn(d, pids, w, m, v, g0, g1):
    g = jnp.where(pids[0] == 0, g0, g1)
    return (g,) + _adam_fn(d, pids, w, g, m, v)


def adam_layers(name, w, m, v, g0, g1):
    _, rows, cols = w.shape
    tm = _row_tile(rows, cols)
    n_t = rows // tm
    lay = lambda t: (t, (1, tm, cols), lambda l, i: (l, i, 0))
    ins = [lay(w), lay(m), lay(v), (g0, (tm, cols), lambda l, i: (i * (1 - l) + (n_t - 1) * l, 0)), (g1, (tm, cols), lambda l, i: (i * l, 0))]
    out = (w.shape, F32, (1, tm, cols), lambda l, i: (l, i, 0), ())
    return tile_fwd(name, _adam_layers_fn, (2, n_t), ins, [out, out, out, out])


DN_GROUP = 4


def _dn_local_specs(rev_n=None):
    rows = DN_GROUP * DN_CHUNK
    idx = (lambda j: j) if rev_n is None else (lambda j: rev_n - 1 - j)
    return [pl.BlockSpec((rows, 3 * BRANCH), lambda j: (idx(j), 0)), pl.BlockSpec((rows, LANES), lambda j: (idx(j), 0)),
            pl.BlockSpec((DN_GROUP, DN_HEADS, DN_CHUNK), lambda j: (idx(j), 0, 0)), pl.BlockSpec((2, DN_HEADS), lambda j: (0, 0))]


def _dn_group_inputs(qkv, ps, a_rows, c):
    lo = c * DN_CHUNK
    heads = _split_heads(qkv[lo:lo + DN_CHUNK])
    return heads[0:4], heads[4:8], heads[8:12], ps[lo:lo + DN_CHUNK], a_rows[c]


def dn_local_fwd(name, dn_act, ps, a_rows, ad):
    s = dn_act.shape[0]
    n_c, n_g = s // DN_CHUNK, s // (DN_GROUP * DN_CHUNK)
    rows = DN_GROUP * DN_CHUNK

    def body(qkv_ref, ps_ref, ar_ref, ad_ref, u_ref, kc_ref, qd_ref, kd_ref, qk_ref, gl_ref):
        qkv, ps_v, a_rows_v, ad_v = qkv_ref[...], ps_ref[...], ar_ref[...], ad_ref[...]
        args = [[] for _ in range(8)]
        for c in range(DN_GROUP):
            q4, k4, v4, ps_c, ar_c = _dn_group_inputs(qkv, ps_v, a_rows_v, c)
            for lst, vals in zip(args, (q4, k4, v4) + _dn_gates(ps_c, ar_c, ad_v)):
                lst.extend(vals)
        everything = _dn_local(False, *args)
        for c in range(DN_GROUP):
            res = everything[c * DN_HEADS:(c + 1) * DN_HEADS]
            at = pl.ds(c * DN_CHUNK, DN_CHUNK)
            for ref, i in ((u_ref, 0), (kc_ref, 1), (qd_ref, 2), (kd_ref, 3)):
                ref[at, :] = jnp.concatenate([r[i] for r in res], axis=1)
            for h in range(DN_HEADS):
                qk_ref[c, h] = res[h][4]
            gl_ref[c] = _head_rows([r[5] for r in res])

    wide = pl.BlockSpec((rows, BRANCH), lambda j: (j, 0))
    return pl.pallas_call(
        body, grid=(n_g,), in_specs=_dn_local_specs(),
        out_specs=[wide, wide, wide, wide, pl.BlockSpec((DN_GROUP, DN_HEADS, DN_CHUNK, DN_CHUNK), lambda j: (j, 0, 0, 0)),
                   pl.BlockSpec((DN_GROUP, 8, LANES), lambda j: (j, 0, 0))],
        out_shape=[jax.ShapeDtypeStruct((s, BRANCH), F32)] * 4 + [jax.ShapeDtypeStruct((n_c, DN_HEADS, DN_CHUNK, DN_CHUNK), F32),
                                                                 jax.ShapeDtypeStruct((n_c, 8, LANES), F32)],
        name=name, compiler_params=_cparams(1),
    )(dn_act, ps, a_rows, ad)


def dn_local_bwd(name, dn_act, ps, a_rows, ad, cots):
    s = dn_act.shape[0]
    n_c, n_g = s // DN_CHUNK, s // (DN_GROUP * DN_CHUNK)
    rows = DN_GROUP * DN_CHUNK

    def body(qkv_ref, ps_ref, ar_ref, ad_ref, du_ref, dkc_ref, dqd_ref, dkd_ref, dqk_ref, dgl_ref, dqkv_ref, dps_ref, dar_ref, dad_ref):
        first = pl.program_id(0) == 0
        qkv, ps_v, a_rows_v, ad_v = qkv_ref[...], ps_ref[...], ar_ref[...], ad_ref[...]
        d_wide = [r[...] for r in (du_ref, dkc_ref, dqd_ref, dkd_ref)]
        qs, ks, vs, ps_cs, ar_cs, cot = [], [], [], [], [], []
        for c in range(DN_GROUP):
            q4, k4, v4, ps_c, ar_c = _dn_group_inputs(qkv, ps_v, a_rows_v, c)
            qs, ks, vs, ps_cs, ar_cs = qs + q4, ks + k4, vs + v4, ps_cs + [ps_c], ar_cs + [ar_c]
            lo = c * DN_CHUNK
            d_tiles = [_split_heads(t[lo:lo + DN_CHUNK]) for t in d_wide]
            d_gl = dgl_ref[c]
            cot += [(d_tiles[0][h], d_tiles[1][h], d_tiles[2][h], d_tiles[3][h], dqk_ref[c, h], _col(_row(d_gl, h), 0))
                    for h in range(DN_HEADS)]

        def f(qs, ks, vs, ps_cs, ar_cs, ad_v):
            gates = [[] for _ in range(5)]
            for ps_c, ar_c in zip(ps_cs, ar_cs):
                for lst, vals in zip(gates, _dn_gates(ps_c, ar_c, ad_v)):
                    lst.extend(vals)
            return _dn_local(True, qs, ks, vs, *gates)

        _, vjp = jax.vjp(f, qs, ks, vs, ps_cs, ar_cs, ad_v)
        d_q, d_k, d_v, d_ps, d_ar, d_ad = vjp(cot)
        for c in range(DN_GROUP):
            at, hs = pl.ds(c * DN_CHUNK, DN_CHUNK), slice(c * DN_HEADS, (c + 1) * DN_HEADS)
            dqkv_ref[at, :] = jnp.concatenate(d_q[hs] + d_k[hs] + d_v[hs], axis=1).astype(dqkv_ref.dtype)
            dps_ref[at, :] = d_ps[c]
            dar_ref[c] = d_ar[c]
        _store(dad_ref, d_ad, first)

    wide = pl.BlockSpec((rows, BRANCH), lambda j: (j, 0))
    specs = _dn_local_specs()
    return pl.pallas_call(
        body, grid=(n_g,),
        in_specs=specs + [wide, wide, wide, wide, pl.BlockSpec((DN_GROUP, DN_HEADS, DN_CHUNK, DN_CHUNK), lambda j: (j, 0, 0, 0)),
                          pl.BlockSpec((DN_GROUP, 8, LANES), lambda j: (j, 0, 0))],
        out_specs=specs,
        out_shape=[jax.ShapeDtypeStruct((s, 3 * BRANCH), F32), jax.ShapeDtypeStruct((s, LANES), F32),
                   jax.ShapeDtypeStruct((n_c, DN_HEADS, DN_CHUNK), F32), jax.ShapeDtypeStruct((2, DN_HEADS), F32)],
        name=name, compiler_params=_cparams(1),
    )(dn_act, ps, a_rows, ad, *cots)


def _dn_scan_specs(n_c, rev):
    idx = (lambda j: n_c - 1 - j) if rev else (lambda j: j)
    wide = pl.BlockSpec((DN_CHUNK, BRANCH), lambda j: (idx(j), 0))
    return [wide, wide, wide, wide, pl.BlockSpec((1, DN_HEADS, DN_CHUNK, DN_CHUNK), lambda j: (idx(j), 0, 0, 0)),
            pl.BlockSpec((1, 8, LANES), lambda j: (idx(j), 0, 0)), pl.BlockSpec((DN_CHUNK, BRANCH), lambda j: (idx(j), C_DZ // BRANCH)),
            pl.BlockSpec((1, DN_DH), lambda j: (0, 0))]


def _dn_scan_tiles(refs):
    u_ref, kc_ref, qd_ref, kd_ref, qk_ref, gl_ref, z_ref, g_ref = refs
    wide = [_split_heads(r[...]) for r in (u_ref, kc_ref, qd_ref, kd_ref)]
    gl = gl_ref[0]
    return [(wide[0][h], wide[1][h], wide[2][h], wide[3][h], qk_ref[0, h], _col(_row(gl, h), 0)) for h in range(DN_HEADS)], \
        _split_heads(z_ref[...]), g_ref[...]


def dn_scan_fwd(name, local, pm, gain):
    s = pm.shape[0]
    n_c = s // DN_CHUNK

    def body(*refs):
        y_ref, hist_ref, state = refs[8:]

        @pl.when(pl.program_id(0) == 0)
        def _():
            state[...] = jnp.zeros_like(state)

        hist_ref[0] = state[...]
        per_head, z4, gain_v = _dn_scan_tiles(refs[:8])
        ys, s_nexts = _dn_step(False, [state[h] for h in range(DN_HEADS)], per_head, z4, gain_v)
        for h in range(DN_HEADS):
            state[h] = s_nexts[h]
        y_ref[...] = jnp.concatenate(ys, axis=1).astype(y_ref.dtype)

    return pl.pallas_call(
        body, grid=(n_c,), in_specs=_dn_scan_specs(n_c, False),
        out_specs=[pl.BlockSpec((DN_CHUNK, BRANCH), lambda j: (j, 0)),
                   pl.BlockSpec((1, DN_HEADS, DN_DH, DN_DH), lambda j: (j, 0, 0, 0))],
        out_shape=[jax.ShapeDtypeStruct((s, BRANCH), BF16), jax.ShapeDtypeStruct((n_c, DN_HEADS, DN_DH, DN_DH), F32)],
        scratch_shapes=[pltpu.VMEM((DN_HEADS, DN_DH, DN_DH), F32)],
        name=name, compiler_params=_cparams(1),
    )(*local, pm, gain)


def dn_scan_bwd(name, local, pm, gain, hist, dy):
    s = pm.shape[0]
    n_c = s // DN_CHUNK

    def body(*refs):
        hist_ref, dy_ref = refs[8:10]
        du_ref, dkc_ref, dqd_ref, dkd_ref, dqk_ref, dgl_ref, dz_ref, dg_ref, d_state = refs[10:]
        first = pl.program_id(0) == 0

        @pl.when(first)
        def _():
            d_state[...] = jnp.zeros_like(d_state)

        per_head, z4, gain_v = _dn_scan_tiles(refs[:8])
        _, vjp = jax.vjp(functools.partial(_dn_step, True), [hist_ref[0, h] for h in range(DN_HEADS)], per_head, z4, gain_v)
        d_s, grads, d_z, d_gain = vjp((_split_heads(dy_ref[...]), [d_state[h] for h in range(DN_HEADS)]))
        for h in range(DN_HEADS):
            d_state[h] = d_s[h]
        for ref, i in ((du_ref, 0), (dkc_ref, 1), (dqd_ref, 2), (dkd_ref, 3)):
            ref[...] = jnp.concatenate([g[i] for g in grads], axis=1)
        dz_ref[...] = jnp.concatenate(d_z, axis=1).astype(dz_ref.dtype)
        for h in range(DN_HEADS):
            dqk_ref[0, h] = grads[h][4]
        dgl_ref[0] = _head_rows([g[5] for g in grads])
        _store(dg_ref, d_gain, first)

    rev = lambda j: n_c - 1 - j
    specs = _dn_scan_specs(n_c, True)
    return pl.pallas_call(
        body, grid=(n_c,),
        in_specs=specs + [pl.BlockSpec((1, DN_HEADS, DN_DH, DN_DH), lambda j: (rev(j), 0, 0, 0)),
                          pl.BlockSpec((DN_CHUNK, BRANCH), lambda j: (rev(j), 0))],
        out_specs=specs[:6] + [pl.BlockSpec((DN_CHUNK, BRANCH), lambda j: (rev(j), 0)), specs[7]],
        out_shape=[jax.ShapeDtypeStruct((s, BRANCH), F32)] * 4 + [
            jax.ShapeDtypeStruct((n_c, DN_HEADS, DN_CHUNK, DN_CHUNK), F32), jax.ShapeDtypeStruct((n_c, 8, LANES), F32),
            jax.ShapeDtypeStruct((s, BRANCH), BF16), jax.ShapeDtypeStruct((1, DN_DH), F32)],
        scratch_shapes=[pltpu.VMEM((DN_HEADS, DN_DH, DN_DH), F32)],
        name=name, compiler_params=_cparams(1),
    )(*local, pm, gain, hist, dy)


def _seq_layouts(cols, s):
    return cols.T.reshape(cols.shape[1], s // LANES, LANES)


def layer_fwd(li, x, p, w):
    s = x.shape[0]
    n = lambda t: f"{t}_l{li}"
    h = rms_fwd(n("rms_mix"), x, w["g_mix"])
    pm = mm(n("in_main"), h, w["in_main"], "nn")
    ps = mm(n("in_small"), h, w["in_small"], "nn")
    qn, kn = fox_prep_fwd(n("fox_prep"), pm, w["gq"], w["gk"])
    f_t = _seq_layouts(ps[:, 0:8], s)
    cum = fox_gate_fwd(n("fox_gate"), f_t, w["b_f"])
    cum_c, cum_r = cum.reshape(8, s, 1), cum.reshape(8, 1, s)
    y_fox = fox_attn_fwd(n("fox_attn"), qn, kn, pm, cum_c, cum_r)
    y_sc = tile_fwd(n("sconv"), _sconv_fn, (BRANCH // LANES,), sconv_ops(pm, w["sc_conv_w"]), [_col_out(s, BRANCH, BF16)])[0]
    dn_act = tile_fwd(n("dnconv"), _dnconv_fn, (3 * BRANCH // LANES,), dnconv_ops(pm, w["dn_conv_w"]), [_col_out(s, 3 * BRANCH)])[0]
    a_rows = ps[:, 12:16].reshape(s // DN_CHUNK, DN_CHUNK, DN_HEADS).transpose(0, 2, 1)
    dn_local = dn_local_fwd(n("dn_local"), dn_act, ps, a_rows, w["ad"])
    y_dn, hist = dn_scan_fwd(n("dn_scan"), dn_local, pm, w["dn_gain"])
    ys = (y_fox, y_sc, y_dn)
    yp = [mm(n(f"branch{b}"), ys[b], w["branch"][b], "nn", blocks=(0, N_CHIPS)) for b in range(3)]
    merged = tile_fwd(n("merge"), _merge_fn, (s // 256,), merge_ops(yp, pm), [((s, D_MODEL), BF16, (256, D_MODEL), lambda i: (i, 0), ())])[0]
    x1 = mm(n("w_o"), merged, w["o"], "nn", add=x)
    h2 = rms_fwd(n("rms_ffn"), x1, w["g_ffn"])
    ug = mm(n("up_g"), h2, w["up"], "nn", blocks=(0, 2))
    uv = mm(n("up_v"), h2, w["up"], "nn", blocks=(2, 2))
    act = tile_fwd(n("ffn_act"), _ffn_act_fn, (D_FF // LANES,), ffn_ops(ug, uv, w["ffn_conv_w"]), [_col_out(s, D_FF, BF16)])[0]
    x2 = mm(n("down"), act, w["down"], "nn", add=x1)
    h3 = rms_fwd(n("rms_ple"), x2, w["g_ple"])
    gpre = mm(n("ple_gate"), h3, w["pg"], "nn")
    pe = mm(n("ple_emb"), p, w["ple"], "nn", blocks=(0, N_CHIPS))
    x3 = tile_fwd(n("ple"), _ple_fn, (s // 256,), ple_ops(gpre, pe, x2), [((s, D_MODEL), F32, (256, D_MODEL), lambda i: (i, 0), ())])[0]
    saved = dict(x=x, h=h, pm=pm, ps=ps, qn=qn, kn=kn, f_t=f_t, cum_c=cum_c, cum_r=cum_r, ys=ys, dn_act=dn_act, dn_local=dn_local,
                 a_rows=a_rows, hist=hist, yp=yp, merged=merged, x1=x1, h2=h2, ug=ug, uv=uv, act=act, x2=x2, h3=h3,
                 gpre=gpre, pe=pe, p=p)
    return x3, saved


def hang_on(w, token):
    zero = token[0, 0]
    small = ("g_mix", "g_ffn", "g_ple", "gq", "gk", "b_f", "ad", "dn_gain", "sc_conv_w", "dn_conv_w", "ffn_conv_w")
    return {**w, **{k: w[k] + zero for k in small}}


def layer_bwd(li, dx3, sv, w, mid_hook=None):
    s = dx3.shape[0]
    n = lambda t: f"{t}_l{li}"
    g = {}
    col_own = lambda width: ((s, width), (s, LANES), lambda j: (0, j))
    d_gpre, d_pe = tile_bwd(n("ple_bwd"), _ple_fn, (s // 256,), ple_ops(sv["gpre"], sv["pe"], sv["x2"]), [_rows(dx3)],
                            [(0, (), None, BF16), (1, (), None, BF16)])
    g["w_ple"] = mm(n("d_w_ple"), sv["p"], d_pe, "tn", blocks=(0, N_CHIPS))
    g["w_ple_gate"] = mm(n("d_w_pg"), sv["h3"], d_gpre, "tn").reshape(N_CHIPS, -1, D_MODEL)
    dh3 = mm(n("d_h3"), d_gpre, w["pg"], "nt")
    dx2, d_g_ple = rms_bwd(n("rms_ple_bwd"), sv["x2"], w["g_ple"], dh3, dx3)
    dact = mm(n("d_act"), dx2, w["down"], "nt")
    g["w_down"] = mm(n("d_w_down"), sv["act"], dx2, "tn").reshape(N_CHIPS, -1, D_MODEL)
    taps_own = ((w["ffn_conv_w"].shape[0], D_FF), (w["ffn_conv_w"].shape[0], LANES), lambda j: (0, j))
    d_ug, d_uv, d_fw_g, d_fw_v = tile_bwd(n("ffn_act_bwd"), _ffn_act_fn, (D_FF // LANES,), ffn_ops(sv["ug"], sv["uv"], w["ffn_conv_w"]),
                                          [_col_cot(dact)], [(0, (), None, BF16), (1, (), None, BF16), (2, (), taps_own), (3, (), taps_own)])
    g["ffn_conv_w"] = jnp.concatenate([d_fw_g, d_fw_v], axis=1)
    g["w_up"] = jnp.concatenate([mm(n("d_w_up_g"), sv["h2"], d_ug, "tn", blocks=(0, 2)), mm(n("d_w_up_v"), sv["h2"], d_uv, "tn", blocks=(0, 2))])
    dh2 = mm(n("d_h2_v"), d_uv, w["up"], "nt", blocks=(2, 2), add=mm(n("d_h2_g"), d_ug, w["up"], "nt", blocks=(0, 2)))
    dx1, d_g_ffn = rms_bwd(n("rms_ffn_bwd"), sv["x1"], w["g_ffn"], dh2, dx2)
    if mid_hook is not None:
        w = hang_on(w, mid_hook(dx1))
    dmerged = mm(n("d_merged"), dx1, w["o"], "nt")
    g["w_o"] = mm(n("d_w_o"), sv["merged"], dx1, "tn").reshape(N_CHIPS, -1, D_MODEL)
    gate_own = ((s, D_MODEL), (256, D_MODEL), lambda i: (i, 0))
    d_yp0, d_yp1, d_yp2, d_g0, d_g1, d_g2 = tile_bwd(
        n("merge_bwd"), _merge_fn, (s // 256,), merge_ops(sv["yp"], sv["pm"]), [_rows(dmerged)],
        [(0, (), None, BF16), (1, (), None, BF16), (2, (), None, BF16), (3, (), gate_own, BF16), (4, (), gate_own, BF16), (5, (), gate_own, BF16)])
    d_yp = (d_yp0, d_yp1, d_yp2)
    g["w_branch"] = jnp.concatenate([mm(n(f"d_w_branch{b}"), sv["ys"][b], d_yp[b], "tn", blocks=(0, N_CHIPS)) for b in range(3)], axis=1)
    d_ys = [mm(n(f"d_y{b}"), d_yp[b], w["branch"][b], "nt", blocks=(0, N_CHIPS)) for b in range(3)]
    *d_local, d_z, d_dngain = dn_scan_bwd(n("dn_scan_bwd"), sv["dn_local"], sv["pm"], w["dn_gain"], sv["hist"], d_ys[2])
    d_dnact, d_ps_dn, d_arows, d_ad = dn_local_bwd(n("dn_local_bwd"), sv["dn_act"], sv["ps"], sv["a_rows"], w["ad"], d_local)
    g["ad"], g["dn_norm_gain"] = d_ad, d_dngain[0]
    d_dnqkv, g["dn_conv_w"] = tile_bwd(n("dnconv_bwd"), _dnconv_fn, (3 * BRANCH // LANES,), dnconv_ops(sv["pm"], w["dn_conv_w"]),
                                       [_col_cot(d_dnact)], [(0, (), col_own(3 * BRANCH), BF16), (1, ())])
    d_sb, d_sc, d_sv, g["sc_conv_w"] = tile_bwd(n("sconv_bwd"), _sconv_fn, (BRANCH // LANES,), sconv_ops(sv["pm"], w["sc_conv_w"]), [_col_cot(d_ys[1])],
                                                [(0, (), col_own(BRANCH), BF16), (1, (), col_own(BRANCH), BF16), (2, (), col_own(BRANCH), BF16), (3, ())])
    d_qn, d_kn, d_fv, d_cum = fox_attn_bwd(n("fox_attn_bwd"), sv["qn"], sv["kn"], sv["pm"], sv["cum_c"], sv["cum_r"], d_ys[0])
    d_ft, d_bf = fox_gate_bwd(n("fox_gate_bwd"), sv["f_t"], w["b_f"], d_cum.reshape(8, s // LANES, LANES))
    g["b_fox_f"] = d_bf.reshape(8)
    d_fq, d_fk, d_gq, d_gk = fox_prep_bwd(n("fox_prep_bwd"), sv["pm"], w["gq"], w["gk"], d_qn, d_kn)
    g["fox_q_gain"] = d_gq[0, :FOX_DH] + d_gq[0, FOX_DH:]
    g["fox_k_gain"] = d_gk[0, :FOX_DH] + d_gk[0, FOX_DH:]
    d_pm = jnp.concatenate([d_fq, d_fk, d_fv.astype(BF16), d_sb, d_sc, d_sv, d_dnqkv, d_z, d_g0, d_g1, d_g2], axis=1)
    d_a_cols = d_arows.transpose(0, 2, 1).reshape(s, DN_HEADS)
    d_f_cols = d_ft.reshape(8, s).T
    d_ps = d_ps_dn + jnp.concatenate([d_f_cols, jnp.zeros((s, 4), F32), d_a_cols, jnp.zeros((s, LANES - 16), F32)], axis=1)
    g["w_in"] = chip_blocks_w_in(mm(n("d_w_in_main"), sv["h"], d_pm, "tn"), mm(n("d_w_in_small"), sv["h"], d_ps, "tn"))
    dh = mm(n("d_h_small"), d_ps, w["in_small"], "nt", add=mm(n("d_h_main"), d_pm, w["in_main"], "nt"))
    dx, d_g_mix = rms_bwd(n("rms_mix_bwd"), sv["x"], w["g_mix"], dh, dx1)
    g["g_mix"], g["g_ffn"], g["g_ple"] = d_g_mix[0], d_g_ffn[0], d_g_ple[0]
    return dx, g


IN_SHARD = 2052
MAIN_RANGES = ((0, 1536), (1544, 3080), (3080, 4616), (4624, 5136), (5136, 8208))
SMALL_RANGES = ((1536, 1544), (4616, 4620), (4620, 4624))


def _from_chip_blocks(blocks, ranges):
    parts = []
    for lo, hi in ranges:
        for k in range(N_CHIPS):
            a0, a1 = max(lo, k * IN_SHARD), min(hi, (k + 1) * IN_SHARD)
            if a0 < a1:
                parts.append(blocks[k][:, a0 - k * IN_SHARD:a1 - k * IN_SHARD])
    return parts


def split_w_in(blocks):
    main = jnp.concatenate(_from_chip_blocks(blocks, MAIN_RANGES), axis=1)
    pad = jnp.zeros((blocks.shape[1], LANES - 16), blocks.dtype)
    return main, jnp.concatenate(_from_chip_blocks(blocks, SMALL_RANGES) + [pad], axis=1)


def chip_blocks_w_in(main, small):
    pieces, m_off, s_off = [], 0, 0
    ranges = sorted([(lo, hi, "m") for lo, hi in MAIN_RANGES] + [(lo, hi, "s") for lo, hi in SMALL_RANGES])
    offs = {}
    for lo, hi in MAIN_RANGES:
        offs[lo] = m_off
        m_off += hi - lo
    for lo, hi in SMALL_RANGES:
        offs[lo] = s_off
        s_off += hi - lo
    blocks = []
    for k in range(N_CHIPS):
        parts = []
        for lo, hi, src in ranges:
            a0, a1 = max(lo, k * IN_SHARD), min(hi, (k + 1) * IN_SHARD)
            if a0 < a1:
                arr = main if src == "m" else small
                parts.append(arr[:, offs[lo] + a0 - lo:offs[lo] + a1 - lo])
        blocks.append(jnp.concatenate(parts, axis=1))
    return jnp.stack(blocks)


def layer_weights(li, got, conv, a):
    g_in, g_branch, g_o, g_up, g_down, g_pg, g_ple = got
    main, small = split_w_in(g_in)
    tile2 = lambda v: jnp.concatenate([v, v])[None, :]
    branch = g_branch.reshape(N_CHIPS, 3, BRANCH, -1)
    return dict(
        in_main=main, in_small=small, branch=[branch[:, b] for b in range(3)], o=g_o.reshape(D_MODEL, D_MODEL), up=g_up,
        down=g_down.reshape(D_FF, D_MODEL), pg=g_pg.reshape(D_MODEL, D_MODEL), ple=g_ple,
        g_mix=a["g_mix"][li][None, :], g_ffn=a["g_ffn"][li][None, :], g_ple=a["g_ple"][li][None, :],
        gq=tile2(a["fox_q_gain"][li]), gk=tile2(a["fox_k_gain"][li]), b_f=a["b_fox_f"][li].reshape(8, 1, 1),
        ad=jnp.stack([a["dn_a_log"][li], a["dn_dt_bias"][li]]), dn_gain=a["dn_norm_gain"][li][None, :],
        sc_conv_w=conv["sc_conv_w"][li], dn_conv_w=conv["dn_conv_w"][li], ffn_conv_w=conv["ffn_conv_w"][li])


def pack_rows(arrs, dtype):
    flat = jnp.concatenate([t.reshape(-1).astype(dtype) for t in arrs])
    pad = (-flat.shape[0]) % (8 * LANES)
    if pad:
        flat = jnp.concatenate([flat, jnp.zeros((pad,), dtype)])
    return flat.reshape(-1, LANES)


def unpack_rows(buf, shapes):
    flat = buf.reshape(-1)
    out, off = [], 0
    for shp in shapes:
        size = 1
        for dim in shp:
            size *= dim
        out.append(flat[off:off + size].reshape(shp))
        off += size
    return out


def chip_shard(t, axis, k):
    width = t.shape[axis] // N_CHIPS
    return lax.slice_in_dim(t, k * width, (k + 1) * width, axis=axis)


ANY = pl.BlockSpec(memory_space=pl.ANY)


def _position():
    x, y, c = lax.axis_index("x"), lax.axis_index("y"), lax.axis_index("c")
    return x, y, c, [(1 - x, y), (x, 1 - y), (1 - x, 1 - y)]


def gather_small(name, block):
    m_per, n = block.shape

    def body(x_ref, out_ref, token, send_sems, recv_sems, local_sem):
        token[...] = jnp.zeros_like(token)
        x, y, c, chips = _position()
        me, sibling = (x, y, c), (x, y, 1 - c)

        def rows(px, py, pc):
            return out_ref.at[pl.ds((4 * px + 2 * py + pc) * m_per, m_per), :]

        def copy(k, blk, to, src=None):
            return pltpu.make_async_remote_copy(src_ref=rows(*blk) if src is None else src, dst_ref=rows(*blk),
                                                send_sem=send_sems.at[k], recv_sem=recv_sems.at[k], device_id=to, device_id_type=MESH)

        mine = pltpu.make_async_copy(x_ref, rows(*me), local_sem)
        mine.start()
        first = [copy(0, me, sibling, src=x_ref)] + [copy(1 + j, me, (*chip, c), src=x_ref) for j, chip in enumerate(chips)]
        for cp in first:
            cp.start()
        passed = [copy(4 + j, (*chip, c), sibling) for j, chip in enumerate(chips)]
        for j, chip in enumerate(chips):
            copy(1 + j, (*chip, c), me).wait_recv()
            passed[j].start()
        copy(0, sibling, me).wait_recv()
        for j, chip in enumerate(chips):
            copy(4 + j, (*chip, 1 - c), me).wait_recv()
        for cp in first + passed:
            cp.wait_send()
        mine.wait()

    in_vmem = pl.BlockSpec(memory_space=pltpu.VMEM)
    return pl.pallas_call(
        body, out_shape=[jax.ShapeDtypeStruct((8 * m_per, n), block.dtype), jax.ShapeDtypeStruct((8, LANES), F32)],
        in_specs=[in_vmem], out_specs=[in_vmem, in_vmem],
        scratch_shapes=[pltpu.SemaphoreType.DMA((7,)), pltpu.SemaphoreType.DMA((7,)), pltpu.SemaphoreType.DMA],
        name=name, compiler_params=pltpu.CompilerParams(vmem_limit_bytes=VMEM_LIMIT),
    )(block)


def _sems(n):
    return [pltpu.SemaphoreType.DMA((n,)), pltpu.SemaphoreType.DMA((n,))]


def gather_layer(name, shards):
    n_w = len(shards)
    halves = [s.shape[0] // 2 for s in shards]

    def body(*refs):
        ins, outs = refs[:n_w], refs[n_w:2 * n_w]
        token, send_sems, recv_sems = refs[2 * n_w:]
        token[...] = jnp.zeros_like(token)
        x, y, c, chips = _position()
        sibling = (x, y, 1 - c)

        def part(w, px, py, pc):
            return outs[w].at[2 * px + py, pl.ds(pc * halves[w], halves[w]), :]

        def copy(k, w, blk, to, src=None):
            return pltpu.make_async_remote_copy(src_ref=part(w, *blk) if src is None else src, dst_ref=part(w, *blk),
                                                send_sem=send_sems.at[k], recv_sem=recv_sems.at[k], device_id=to, device_id_type=MESH)

        pairs = [(w, j, chip) for w in range(n_w) for j, chip in enumerate(chips)]
        first = [copy(3 * w + j, w, (x, y, c), (*chip, c), src=ins[w].at[pl.ds(c * halves[w], halves[w]), :]) for w, j, chip in pairs]
        for cp in first:
            cp.start()
        passed = [copy(3 * n_w + 3 * w + j, w, (*chip, c), sibling) for w, j, chip in pairs]
        for (w, j, chip), fwd in zip(pairs, passed):
            copy(3 * w + j, w, (*chip, c), (x, y, c)).wait_recv()
            fwd.start()
        for w, j, chip in pairs:
            copy(3 * n_w + 3 * w + j, w, (*chip, 1 - c), (x, y, c)).wait_recv()
        for cp in first + passed:
            cp.wait_send()

    out = pl.pallas_call(
        body, out_shape=[jax.ShapeDtypeStruct((N_CHIPS,) + s.shape, s.dtype) for s in shards] + [jax.ShapeDtypeStruct((8, LANES), F32)],
        in_specs=[ANY] * n_w, out_specs=[ANY] * n_w + [pl.BlockSpec(memory_space=pltpu.VMEM)], scratch_shapes=_sems(6 * n_w), name=name,
    )(*shards)
    return out[:n_w], out[n_w]


def swap_halves(name, grads):
    n_w = len(grads)
    halves = [g.shape[1] // 2 for g in grads]

    def body(*refs):
        ins, outs = refs[:n_w], refs[n_w:2 * n_w]
        send_sems, recv_sems = refs[2 * n_w:]
        x, y, c, _ = _position()
        cps = [pltpu.make_async_remote_copy(src_ref=ins[w].at[:, pl.ds((1 - c) * halves[w], halves[w]), :], dst_ref=outs[w],
                                            send_sem=send_sems.at[w], recv_sem=recv_sems.at[w], device_id=(x, y, 1 - c),
                                            device_id_type=MESH) for w in range(n_w)]
        for cp in cps:
            cp.start()
        for cp in cps:
            cp.wait()

    return pl.pallas_call(
        body, out_shape=[jax.ShapeDtypeStruct((N_CHIPS, h, g.shape[2]), g.dtype) for g, h in zip(grads, halves)],
        in_specs=[ANY] * n_w, out_specs=[ANY] * n_w, scratch_shapes=_sems(n_w), name=name,
    )(*grads)


def scatter_chips(name, partials):
    n_w = len(partials)

    def body(*refs):
        ins, outs = refs[:n_w], refs[n_w:2 * n_w]
        send_sems, recv_sems = refs[2 * n_w:]
        x, y, c, chips = _position()
        cps = [pltpu.make_async_remote_copy(src_ref=ins[w].at[2 * cx + cy], dst_ref=outs[w].at[j], send_sem=send_sems.at[3 * w + j],
                                            recv_sem=recv_sems.at[3 * w + j], device_id=(cx, cy, c), device_id_type=MESH)
               for w in range(n_w) for j, (cx, cy) in enumerate(chips)]
        for cp in cps:
            cp.start()
        for cp in cps:
            cp.wait()

    return pl.pallas_call(
        body, out_shape=[jax.ShapeDtypeStruct((3,) + p.shape[1:], p.dtype) for p in partials],
        in_specs=[ANY] * n_w, out_specs=[ANY] * n_w, scratch_shapes=_sems(3 * n_w), name=name,
    )(*partials)


def share_halves(name, bufs):
    n_w = len(bufs)
    halves = [b.shape[0] // 2 for b in bufs]

    def body(*refs):
        outs = refs[n_w:2 * n_w]
        send_sems, recv_sems = refs[2 * n_w:]
        x, y, c, _ = _position()

        def copy(w, pc):
            half = outs[w].at[pl.ds(pc * halves[w], halves[w]), :]
            return pltpu.make_async_remote_copy(src_ref=half, dst_ref=half, send_sem=send_sems.at[w], recv_sem=recv_sems.at[w],
                                                device_id=(x, y, 1 - c), device_id_type=MESH)

        for w in range(n_w):
            copy(w, c).start()
        for w in range(n_w):
            copy(w, 1 - c).wait_recv()
            copy(w, c).wait_send()

    return pl.pallas_call(
        body, out_shape=[jax.ShapeDtypeStruct(b.shape, b.dtype) for b in bufs], in_specs=[ANY] * n_w, out_specs=[ANY] * n_w,
        input_output_aliases={w: w for w in range(n_w)}, scratch_shapes=_sems(n_w), name=name,
    )(*bufs)


HBM = pl.BlockSpec(memory_space=pltpu.HBM)
SEM = pl.BlockSpec(memory_space=pltpu.SEMAPHORE)
EFFECT = pltpu.SideEffectType.DATAFLOW_SIDE_EFFECTING


def _exchange_copies(kind, srcs, lands):
    x, y, c, chips = _position()
    out = []
    for src, land in zip(srcs, lands):
        if kind == "swap":
            half = src.shape[1] // 2
            out.append((src.at[:, pl.ds((1 - c) * half, half), :], land, (x, y, 1 - c)))
            continue
        for j, (cx, cy) in enumerate(chips):
            if kind == "gather":
                out.append((src, land.at[2 * x + y], (cx, cy, c)))
            else:
                out.append((src.at[2 * cx + cy], land.at[j], (cx, cy, c)))
    return out


def _land_shapes(kind, srcs):
    if kind == "gather":
        return [(N_CHIPS,) + s.shape for s in srcs]
    if kind == "swap":
        return [(N_CHIPS, s.shape[1] // 2, s.shape[2]) for s in srcs]
    return [(3,) + s.shape[1:] for s in srcs]


def exchange_start(name, kind, srcs):
    n_w = len(srcs)
    shapes = _land_shapes(kind, srcs)
    n_sem = n_w if kind == "swap" else 3 * n_w

    def body(*refs):
        ins, lands = refs[:n_w], refs[n_w:2 * n_w]
        send_sems, recv_sems = refs[2 * n_w:2 * n_w + 2]
        token = refs[-1]
        for i, (src, dst, dev) in enumerate(_exchange_copies(kind, ins, lands)):
            pltpu.make_async_remote_copy(src_ref=src, dst_ref=dst, send_sem=send_sems.at[i], recv_sem=recv_sems.at[i],
                                         device_id=dev, device_id_type=MESH).start()
        token[...] = jnp.zeros_like(token)

    out = pl.pallas_call(
        body, name=name,
        out_shape=(pltpu.SemaphoreType.DMA((n_sem,)), pltpu.SemaphoreType.DMA((n_sem,)),
                   *[pltpu.HBM(s.shape, s.dtype) for s in srcs], *[pltpu.HBM(shp, s.dtype) for shp, s in zip(shapes, srcs)],
                   jax.ShapeDtypeStruct((8, LANES), F32)),
        in_specs=(HBM,) * (2 * n_w), out_specs=(SEM, SEM) + (HBM,) * (2 * n_w) + (pl.BlockSpec(memory_space=pltpu.VMEM),),
        input_output_aliases={i: 2 + i for i in range(2 * n_w)},
        compiler_params=pltpu.CompilerParams(has_side_effects=EFFECT),
    )(*[pltpu.with_memory_space_constraint(s, pltpu.HBM) for s in srcs],
      *[pltpu.with_memory_space_constraint(lax.empty(shp, s.dtype), pltpu.HBM) for shp, s in zip(shapes, srcs)])
    return (kind, n_w, out[:-1]), out[-1]


def exchange_wait(name, handle, after):
    kind, n_w, (send_sems, recv_sems, *thru) = handle
    n_sem = n_w if kind == "swap" else 3 * n_w

    def body(*refs):
        ins, lands = refs[:n_w], refs[n_w:2 * n_w]
        send_sems, recv_sems = refs[2 * n_w:2 * n_w + 2]
        for i, (src, dst, dev) in enumerate(_exchange_copies(kind, ins, lands)):
            cp = pltpu.make_async_remote_copy(src_ref=src, dst_ref=dst, send_sem=send_sems.at[i], recv_sem=recv_sems.at[i],
                                              device_id=dev, device_id_type=MESH)
            cp.wait_send()
            cp.wait_recv()

    out = pl.pallas_call(
        body, name=name, out_shape=tuple(pltpu.HBM(t.shape, t.dtype) for t in thru),
        in_specs=(HBM,) * (2 * n_w) + (SEM, SEM, pl.BlockSpec(memory_space=pl.ANY)), out_specs=(HBM,) * (2 * n_w),
        input_output_aliases={i: i for i in range(2 * n_w)},
        compiler_params=pltpu.CompilerParams(has_side_effects=EFFECT),
    )(*thru, send_sems, recv_sems, after)
    return list(out[n_w:])


def _row_tile(rows, cols):
    best = 16
    for t in range(16, rows + 1, 16):
        if rows % t == 0 and t * cols * 4 <= 1024 * 1024:
            best = t
    return best


def pair_sum(name, pos, grad, from_sibling):
    _, rows, cols = grad.shape
    half = rows // 2
    tr = _row_tile(half, cols)
    n_t = half // tr

    def body(pos_ref, g_ref, s_ref, b_ref, f_ref):
        tot = g_ref[...] + s_ref[...]
        b_ref[...] = tot.astype(BF16)

        @pl.when(pl.program_id(1) == pos_ref[1])
        def _():
            f_ref[...] = tot[0]

    blk = pl.BlockSpec((1, tr, cols), lambda i, k, pos: (k, i, 0))
    return pl.pallas_call(
        body, grid_spec=pltpu.PrefetchScalarGridSpec(
            num_scalar_prefetch=1, grid=(n_t, N_CHIPS),
            in_specs=[pl.BlockSpec((1, tr, cols), lambda i, k, pos: (k, pos[0] * n_t + i, 0)), blk],
            out_specs=[blk, pl.BlockSpec((tr, cols), lambda i, k, pos: (i, 0))]),
        out_shape=[jax.ShapeDtypeStruct((N_CHIPS, half, cols), BF16), jax.ShapeDtypeStruct((half, cols), F32)],
        name=name, compiler_params=_cparams(2),
    )(pos, grad, from_sibling)


def chip_sum(name, pos, own, landed):
    half, cols = own.shape
    tr = _row_tile(half, cols)
    n_t = half // tr

    def body(pos_ref, p_ref, l_ref, o_ref):
        o_ref[...] = ((p_ref[...] + l_ref[0].astype(F32)) + l_ref[1].astype(F32)) + l_ref[2].astype(F32)

    return pl.pallas_call(
        body, grid_spec=pltpu.PrefetchScalarGridSpec(
            num_scalar_prefetch=1, grid=(n_t,),
            in_specs=[pl.BlockSpec((tr, cols), lambda i, pos: (i, 0)), pl.BlockSpec((3, tr, cols), lambda i, pos: (0, i, 0))],
            out_specs=pl.BlockSpec((tr, cols), lambda i, pos: (pos[0] * n_t + i, 0))),
        out_shape=jax.ShapeDtypeStruct((2 * half, cols), F32), name=name, compiler_params=_cparams(1),
    )(pos, own, landed)


def reduce_scatter_layer(li, pos, grads):
    n = lambda t: f"{t}_l{li}"
    from_sibling = swap_halves(n("swap_halves"), grads)
    sums = [pair_sum(n(f"pair_sum{w}"), pos, g, s) for w, (g, s) in enumerate(zip(grads, from_sibling))]
    landed = scatter_chips(n("scatter_chips"), [b for b, _ in sums])
    halves = [chip_sum(n(f"chip_sum{w}"), pos, own, l) for w, ((_, own), l) in enumerate(zip(sums, landed))]
    return share_halves(n("share_halves"), halves)


class OverlappedReduceScatter:
    def __init__(self, li, pos, grads):
        self.n = lambda t: f"{t}_l{li}"
        self.pos, self.grads = pos, grads
        self.swap, self.token = exchange_start(self.n("swap_start"), "swap", grads)

    def middle(self, after):
        from_sibling = exchange_wait(self.n("swap_wait"), self.swap, after)
        self.sums = [pair_sum(self.n(f"pair_sum{w}"), self.pos, g, s) for w, (g, s) in enumerate(zip(self.grads, from_sibling))]
        self.scatter, self.token = exchange_start(self.n("scatter_start"), "scatter", [b for b, _ in self.sums])

    def finish(self, after):
        landed = exchange_wait(self.n("scatter_wait"), self.scatter, after)
        halves = [chip_sum(self.n(f"chip_sum{w}"), self.pos, own, l) for w, ((_, own), l) in enumerate(zip(self.sums, landed))]
        return share_halves(self.n("share_halves"), halves)


def sum_devices(gathered):
    m_per = gathered.shape[0] // 8

    def body(g_ref, o_ref):
        tot = g_ref[pl.ds(0, m_per), :]
        for dev in range(1, 8):
            tot = tot + g_ref[pl.ds(dev * m_per, m_per), :]
        o_ref[...] = tot

    return pl.pallas_call(
        body, out_shape=jax.ShapeDtypeStruct((m_per, gathered.shape[1]), F32),
        in_specs=[pl.BlockSpec(memory_space=pltpu.VMEM)], out_specs=pl.BlockSpec(memory_space=pltpu.VMEM), name="sum_devices",
    )(gathered)


def kernel(x, p, g_mix, w_in, b_fox_f, fox_q_gain, fox_k_gain, sc_conv_w, dn_conv_w, dn_a_log, dn_dt_bias, dn_norm_gain, w_branch, w_o, g_ffn, w_up, ffn_conv_w, w_down, g_ple, w_ple_gate, w_ple, loss_target, m_g_mix, m_w_in, m_b_fox_f, m_fox_q_gain, m_fox_k_gain, m_sc_conv_w, m_dn_conv_w, m_dn_a_log, m_dn_dt_bias, m_dn_norm_gain, m_w_branch, m_w_o, m_g_ffn, m_w_up, m_ffn_conv_w, m_w_down, m_g_ple, m_w_ple_gate, m_w_ple, v_g_mix, v_w_in, v_b_fox_f, v_fox_q_gain, v_fox_k_gain, v_sc_conv_w, v_dn_conv_w, v_dn_a_log, v_dn_dt_bias, v_dn_norm_gain, v_w_branch, v_w_o, v_g_ffn, v_w_up, v_ffn_conv_w, v_w_down, v_g_ple, v_w_ple_gate, v_w_ple):
    a = dict(g_mix=g_mix, w_in=w_in, b_fox_f=b_fox_f, fox_q_gain=fox_q_gain, fox_k_gain=fox_k_gain, sc_conv_w=sc_conv_w,
             dn_conv_w=dn_conv_w, dn_a_log=dn_a_log, dn_dt_bias=dn_dt_bias, dn_norm_gain=dn_norm_gain, w_branch=w_branch, w_o=w_o,
             g_ffn=g_ffn, w_up=w_up, ffn_conv_w=ffn_conv_w, w_down=w_down, g_ple=g_ple, w_ple_gate=w_ple_gate, w_ple=w_ple)
    mom = dict(g_mix=m_g_mix, w_in=m_w_in, b_fox_f=m_b_fox_f, fox_q_gain=m_fox_q_gain, fox_k_gain=m_fox_k_gain, sc_conv_w=m_sc_conv_w,
               dn_conv_w=m_dn_conv_w, dn_a_log=m_dn_a_log, dn_dt_bias=m_dn_dt_bias, dn_norm_gain=m_dn_norm_gain, w_branch=m_w_branch,
               w_o=m_w_o, g_ffn=m_g_ffn, w_up=m_w_up, ffn_conv_w=m_ffn_conv_w, w_down=m_w_down, g_ple=m_g_ple, w_ple_gate=m_w_ple_gate,
               w_ple=m_w_ple)
    var = dict(g_mix=v_g_mix, w_in=v_w_in, b_fox_f=v_b_fox_f, fox_q_gain=v_fox_q_gain, fox_k_gain=v_fox_k_gain, sc_conv_w=v_sc_conv_w,
               dn_conv_w=v_dn_conv_w, dn_a_log=v_dn_a_log, dn_dt_bias=v_dn_dt_bias, dn_norm_gain=v_dn_norm_gain, w_branch=v_w_branch,
               w_o=v_w_o, g_ffn=v_g_ffn, w_up=v_w_up, ffn_conv_w=v_ffn_conv_w, w_down=v_w_down, g_ple=v_g_ple, w_ple_gate=v_w_ple_gate,
               w_ple=v_w_ple)
    cx, cy, cc = lax.axis_index("x"), lax.axis_index("y"), lax.axis_index("c")
    chip = 2 * cx + cy
    pos = jnp.stack([cc, chip]).astype(jnp.int32)

    def as_blocks(t):
        return t.reshape(2, -1, t.shape[-1])

    def own_block_in(got, shards):
        return [lax.dynamic_update_slice(g, s[None], (chip, 0, 0)) for g, s in zip(got, shards)]

    conv_shapes = [a[nm].shape for nm in CONVS]
    conv_all, conv_token = gather_small("gather_conv_w", pack_rows([a[nm] for nm in CONVS], F32))
    shards0 = [(as_blocks(a[nm])[0] + conv_token[0, 0]).astype(BF16) for nm in BIG]
    got0, gathered_token = gather_layer("gather_weights_l0", shards0)
    shards1 = [(as_blocks(a[nm])[1] + gathered_token[0, 0]).astype(BF16) for nm in BIG]
    gather1, gather1_token = exchange_start("gather_start_l1", "gather", shards1)
    conv_rows = conv_all.shape[0] // 8
    conv_chip = [unpack_rows(conv_all[2 * k * conv_rows:(2 * k + 1) * conv_rows], conv_shapes) for k in range(N_CHIPS)]
    conv = {nm: jnp.concatenate([conv_chip[k][i] for k in range(N_CHIPS)], axis=2) for i, nm in enumerate(CONVS)}

    weights, saved = [None, None], [None, None]
    weights[0] = hang_on(layer_weights(0, own_block_in(got0, shards0), conv, a), gather1_token)
    act, saved[0] = layer_fwd(0, x[0], p[0, 0], weights[0])
    got1 = exchange_wait("gather_wait_l1", gather1, act)
    weights[1] = layer_weights(1, own_block_in(got1, shards1), conv, a)
    act, saved[1] = layer_fwd(1, act, p[1, 0], weights[1])
    d_act, loss_part = loss_call(act, loss_target[0])
    loss = lax.psum(loss_part, ("x", "y", "c"))
    layer_grads, reduced = [None, None], [None, None]
    d_act, layer_grads[1] = layer_bwd(1, d_act, saved[1], weights[1])
    rs1 = OverlappedReduceScatter(1, pos, [layer_grads[1][nm] for nm in BIG])

    def stage_middle(after):
        rs1.middle(after)
        return rs1.token

    d_act, layer_grads[0] = layer_bwd(0, d_act, saved[0], hang_on(weights[0], rs1.token), mid_hook=stage_middle)
    reduced[1] = rs1.finish(d_act)
    reduced[0] = reduce_scatter_layer(0, pos, [layer_grads[0][nm] for nm in BIG])
    grad_x = d_act[None]

    def both(nm):
        return jnp.stack([layer_grads[0][nm], layer_grads[1][nm]])

    local = {nm: both(nm) for nm in ("g_mix", "b_fox_f", "fox_q_gain", "fox_k_gain", "dn_norm_gain", "g_ffn", "g_ple", "sc_conv_w",
                                      "dn_conv_w", "ffn_conv_w")}
    local["dn_a_log"] = jnp.stack([layer_grads[li]["ad"][0] for li in range(2)])
    local["dn_dt_bias"] = jnp.stack([layer_grads[li]["ad"][1] for li in range(2)])

    small_names = SMALL + CONVS
    small_shapes = [local[nm].shape for nm in small_names]
    small_sum = sum_devices(gather_small("gather_small_grads", pack_rows([local[nm] for nm in small_names], F32))[0])
    small_grads = dict(zip(small_names, unpack_rows(small_sum, small_shapes)))
    for nm in CONVS:
        width = a[nm].shape[2]
        small_grads[nm] = lax.dynamic_slice_in_dim(small_grads[nm], chip * width, width, axis=2)

    grads, deltas, new_m, new_v = dict(small_grads), {}, {}, {}
    for nm in small_names:
        deltas[nm], new_m[nm], new_v[nm] = adam_call(f"adam_{nm}", a[nm], grads[nm], mom[nm], var[nm])
    for i, nm in enumerate(BIG):
        res = adam_layers(f"adam_{nm}", as_blocks(a[nm]), as_blocks(mom[nm]), as_blocks(var[nm]), reduced[0][i], reduced[1][i])
        grads[nm], deltas[nm], new_m[nm], new_v[nm] = [r.reshape(a[nm].shape) for r in res]
    return (loss, grad_x, *[grads[nm] for nm in WEIGHTS], *[deltas[nm] for nm in WEIGHTS], *[new_m[nm] for nm in WEIGHTS],
            *[new_v[nm] for nm in WEIGHTS])
```

```python
import functools

import jax
import jax.numpy as jnp
from jax import lax
from jax.experimental import pallas as pl
from jax.experimental.pallas import tpu as pltpu

F32 = jnp.float32
BF16 = jnp.bfloat16
HI = lax.Precision.HIGHEST
MESH = pl.DeviceIdType.MESH

D_MODEL = 1024
BRANCH = 512
FOX_DH = 64
DN_DH = 128
DN_HEADS = 4
DN_CHUNK = 64
FOX_BLOCK = 128
D_FF = 2816
EPS = 1e-6
N_CHIPS = 4
LANES = 128

ADAM_LR, ADAM_B1, ADAM_B2, ADAM_EPS, ADAM_WD, ADAM_STEP = 0.001, 0.9, 0.999, 1e-08, 0.01, 10

VMEM_LIMIT = 56 * 1024 * 1024

C_FQ, C_FK, C_FV, C_SB, C_SC, C_SV, C_DN, C_DZ, C_GATE = 0, 512, 1024, 1536, 2048, 2560, 3072, 4608, 5120
IN_MAIN = 8192
IN_SIZES = (1536, 8, 1536, 1536, 4, 4, 512, 3072)

BIG = ("w_in", "w_branch", "w_o", "w_up", "w_down", "w_ple_gate", "w_ple")
BIG_AXIS = {"w_in": 2, "w_branch": 3, "w_o": 1, "w_up": 2, "w_down": 1, "w_ple_gate": 1, "w_ple": 2}
CONVS = ("sc_conv_w", "dn_conv_w", "ffn_conv_w")
SMALL = ("g_mix", "b_fox_f", "fox_q_gain", "fox_k_gain", "dn_a_log", "dn_dt_bias", "dn_norm_gain", "g_ffn", "g_ple")
WEIGHTS = ("g_mix", "w_in", "b_fox_f", "fox_q_gain", "fox_k_gain", "sc_conv_w", "dn_conv_w", "dn_a_log", "dn_dt_bias",
           "dn_norm_gain", "w_branch", "w_o", "g_ffn", "w_up", "ffn_conv_w", "w_down", "g_ple", "w_ple_gate", "w_ple")


def _iota(shape, dim):
    return lax.broadcasted_iota(jnp.int32, shape, dim)


def _dg(a, b, mode, prec=None):
    dims = {"nn": ((1,), (0,)), "nt": ((1,), (1,)), "tn": ((0,), (0,))}[mode]
    return lax.dot_general(a, b, (dims, ((), ())), precision=prec, preferred_element_type=F32)


def _bdot_impl(a, b, mode):
    return _dg(a.astype(BF16), b.astype(BF16), mode)


@functools.partial(jax.custom_vjp, nondiff_argnums=(2,))
def _bdot_diff(a, b, mode):
    return _bdot_impl(a, b, mode)


def _bdot_fwd(a, b, mode):
    return _bdot_impl(a, b, mode), (a, b)


def _bdot_bwd(mode, res, g):
    a, b = res
    if mode == "nn":
        da, db = _bdot_impl(g, b, "nt"), _bdot_impl(a, g, "tn")
    elif mode == "nt":
        da, db = _bdot_impl(g, b, "nn"), _bdot_impl(g, a, "tn")
    else:
        da, db = _bdot_impl(b, g, "nt"), _bdot_impl(a, g, "nn")
    return da.astype(a.dtype), db.astype(b.dtype)


_bdot_diff.defvjp(_bdot_fwd, _bdot_bwd)


def _bdot(d):
    return _bdot_diff if d else _bdot_impl


def _shift_impl(x, k):
    return jnp.where(_iota(x.shape, 0) >= k, pltpu.roll(x, k, 0), 0.0)


def _unshift_impl(g, k):
    n = g.shape[0]
    return jnp.where(_iota(g.shape, 0) < n - k, pltpu.roll(g, n - k, 0), 0.0)


@functools.partial(jax.custom_vjp, nondiff_argnums=(1,))
def _shift_diff(x, k):
    return _shift_impl(x, k)


_shift_diff.defvjp(lambda x, k: (_shift_impl(x, k), None), lambda k, _, g: (_unshift_impl(g, k),))


def _row(w, j):
    return jnp.sum(jnp.where(_iota(w.shape, 0) == j, w, 0.0), axis=0, keepdims=True)


def _col(w, j):
    return jnp.sum(jnp.where(_iota(w.shape, 1) == j, w, 0.0), axis=1, keepdims=True)


def _conv(d, x, w):
    shift = _shift_diff if d else _shift_impl
    taps = w.shape[0]
    y = x * _row(w, taps - 1)
    for j in range(taps - 1):
        y = y + shift(x, taps - 1 - j) * _row(w, j)
    return y


def _softplus(x):
    return jnp.maximum(x, 0.0) + jnp.log(1.0 + jnp.exp(-jnp.abs(x)))


def _silu(x):
    return x * jax.nn.sigmoid(x)


def _rms(x, gain):
    return x * lax.rsqrt(jnp.mean(x * x, axis=-1, keepdims=True) + EPS) * gain


def _rms_fn(d, pids, x, gain):
    return (_rms(x, gain),)


def _loss_fn(d, pids, y, t):
    e = y - t
    part = 0.5 / D_MODEL * jnp.sum(e * e, keepdims=True)
    return e * (1.0 / D_MODEL), jnp.broadcast_to(part, (8, LANES))


def _fox_prep_fn(d, pids, q, k, gq, gk):
    first = _iota(q.shape, 1) < FOX_DH

    def norm(x, gain):
        sq = x * x
        ss_a = jnp.sum(jnp.where(first, sq, 0.0), axis=1, keepdims=True)
        ss_b = jnp.sum(jnp.where(first, 0.0, sq), axis=1, keepdims=True)
        rs = jnp.where(first, lax.rsqrt(ss_a / FOX_DH + EPS), lax.rsqrt(ss_b / FOX_DH + EPS))
        return x * rs * gain

    return norm(q, gq) * FOX_DH ** -0.5, norm(k, gk)


def _fox_gate_fn(d, pids, f, bias):
    logf = -_softplus(-(f + bias))
    n_r, n_c = logf.shape
    tri = (_iota((n_c, n_c), 0) <= _iota((n_c, n_c), 1)).astype(F32)
    within = _dg(logf, tri, "nn", HI)
    tot = jnp.broadcast_to(jnp.sum(logf, axis=1, keepdims=True), logf.shape)
    below = (_iota((n_r, n_r), 1) < _iota((n_r, n_r), 0)).astype(F32)
    return (within + _dg(below, tot, "nn", HI),)


def _fox_attn_fn(q_block0, d, pids, q, k, v, cq_a, cq_b, ck_a, ck_b):
    dot = _bdot(d)
    first = _iota(q.shape, 1) < FOX_DH
    n_q, n_k = q.shape[0], k.shape[0]
    causal = ((q_block0 + pids[1]) * n_q + _iota((n_q, n_k), 0)) >= _iota((n_q, n_k), 1)

    qs = [jnp.where(first, q, 0.0), jnp.where(first, 0.0, q)]
    s = _each(lambda qh, cq, ck: jnp.where(causal, dot(qh, k, "nt") + cq - ck, -1e30), qs, [cq_a, cq_b], [ck_a, ck_b])
    e = [jnp.exp(si - lax.stop_gradient(jnp.max(si, axis=1, keepdims=True))) for si in s]
    o_a, o_b = [dot(ei / jnp.sum(ei, axis=1, keepdims=True), v, "nn") for ei in e]
    return (jnp.where(first, o_a, o_b),)


def _sconv_fn(d, pids, sb, sc, sv, w):
    return (sb * _conv(d, sc * sv, w),)


def _dnconv_fn(d, pids, x, w):
    return (_silu(_conv(d, x, w)),)


def _merge_fn(d, pids, y0, y1, y2, g0, g1, g2):
    return (jax.nn.sigmoid(g0) * y0 + jax.nn.sigmoid(g1) * y1 + jax.nn.sigmoid(g2) * y2,)


def _ffn_act_fn(d, pids, ug, uv, wg, wv):
    return (_silu(_conv(d, ug, wg)) * _conv(d, uv, wv),)


def _ple_fn(d, pids, gpre, pe, x):
    return (x + jax.nn.sigmoid(gpre) * pe,)


def _adam_fn(d, pids, w, g, m, v):
    m2 = ADAM_B1 * m + (1.0 - ADAM_B1) * g
    v2 = ADAM_B2 * v + (1.0 - ADAM_B2) * (g * g)
    m_hat = m2 / (1.0 - ADAM_B1 ** ADAM_STEP)
    v_hat = v2 / (1.0 - ADAM_B2 ** ADAM_STEP)
    delta = -ADAM_LR * (m_hat / (jnp.sqrt(v_hat) + ADAM_EPS) + ADAM_WD * w)
    return delta, m2, v2


def _each(fn, *lists):
    return [fn(*args) for args in zip(*lists)]


def _tri_inv_impl(mats):
    n = mats[0].shape[0]
    r, c = _iota((n, n), 0), _iota((n, n), 1)
    diag_blk = (r >> 4) == (c >> 4)
    eye = (r == c).astype(F32)
    mm = lambda us, ws: _each(lambda u, w: _dg(u, w, "nn", HI), us, ws)
    grow = lambda ps, xs: _each(lambda p, px: p + px, ps, mm(ps, xs))
    x = [jnp.where(diag_blk, -a, 0.0) for a in mats]
    p = [eye + xi for xi in x]
    x2 = mm(x, x)
    p = grow(p, x2)
    x4 = mm(x2, x2)
    p = grow(p, x4)
    p = grow(p, mm(x4, x4))
    y = [-yi for yi in mm(p, [jnp.where(diag_blk, 0.0, a) for a in mats])]
    q = grow([eye + yi for yi in y], mm(y, y))
    return mm(q, p)


@jax.custom_vjp
def _tri_inv_diff(mats):
    return _tri_inv_impl(mats)


def _tri_inv_fwd(mats):
    ts = _tri_inv_impl(mats)
    return ts, ts


def _tri_inv_bwd(ts, gs):
    left = _each(lambda t, g: _dg(t, g, "tn", HI), ts, gs)
    return ([-m for m in _each(lambda l, t: _dg(l, t, "nt", HI), left, ts)],)


_tri_inv_diff.defvjp(_tri_inv_fwd, _tri_inv_bwd)


def _dn_local(d, qs, ks, vs, a_cs, a_rs, b_cs, a_logs, dt_bs):
    dot = _bdot(d)
    inv = _tri_inv_diff if d else _tri_inv_impl
    n = qs[0].shape[0]
    r, c = _iota((n, n), 0), _iota((n, n), 1)
    incl, strict, upper = r >= c, r > c, r <= c
    qs = [q * lax.rsqrt(jnp.sum(q * q, axis=1, keepdims=True) + EPS) * DN_DH ** -0.5 for q in qs]
    ks = [k * lax.rsqrt(jnp.sum(k * k, axis=1, keepdims=True) + EPS) for k in ks]
    betas = [jax.nn.sigmoid(b) for b in b_cs]
    rates = [-jnp.exp(a) for a in a_logs]
    g_cs = _each(lambda rate, a, dt: rate * _softplus(a + dt), rates, a_cs, dt_bs)
    g_rs = _each(lambda rate, a, dt: rate * _softplus(a + dt), rates, a_rs, dt_bs)
    gcum_cs = [jnp.sum(jnp.where(incl, g, 0.0), axis=1, keepdims=True) for g in g_rs]
    gcum_rs = [jnp.sum(jnp.where(upper, g, 0.0), axis=0, keepdims=True) for g in g_cs]
    decays = _each(lambda gc, gr: jnp.exp(jnp.where(incl, gc - gr, -1e30)), gcum_cs, gcum_rs)
    kbs = _each(lambda k, b: k * b, ks, betas)
    kk = _each(lambda kb, k: dot(kb, k, "nt"), kbs, ks)
    ts = inv(_each(lambda m, dec: jnp.where(strict, m * dec, 0.0), kk, decays))
    e_gs = [jnp.exp(g) for g in gcum_cs]
    us = _each(lambda t, v, b: _dg(t, v * b, "nn", HI), ts, vs, betas)
    k_cums = _each(lambda t, kb, e: _dg(t, kb * e, "nn", HI), ts, kbs, e_gs)
    qk = _each(lambda q, k: dot(q, k, "nt"), qs, ks)
    qk = _each(lambda m, dec: jnp.where(incl, m * dec, 0.0), qk, decays)
    g_lasts = [jnp.sum(g, axis=0, keepdims=True) for g in g_cs]
    q_decs = _each(lambda q, e: q * e, qs, e_gs)
    k_decs = _each(lambda k, gl, gc: k * jnp.exp(gl - gc), ks, g_lasts, gcum_cs)
    return list(zip(us, k_cums, q_decs, k_decs, qk, g_lasts))


def _dn_step(d, s_prevs, items, zs, gain):
    dot = _bdot(d)
    us, k_cums, q_decs, k_decs, qks, g_lasts = [list(t) for t in zip(*items)]
    v_news = _each(lambda u, kc, s: u - dot(kc, s, "nn"), us, k_cums, s_prevs)
    inter = _each(lambda qd, s: dot(qd, s, "nn"), q_decs, s_prevs)
    outs = _each(lambda o, qk, vn: o + dot(qk, vn, "nn"), inter, qks, v_news)
    s_nexts = _each(lambda s, gl, kd, vn: s * jnp.exp(gl) + dot(kd, vn, "tn"), s_prevs, g_lasts, k_decs, v_news)
    return _each(lambda o, z: _rms(o, gain) * _silu(z), outs, zs), s_nexts


def _split_heads(t):
    return [t[:, h * DN_DH:(h + 1) * DN_DH] for h in range(t.shape[1] // DN_DH)]


def _dn_gates(ps, a_rows, ad):
    hs = range(DN_HEADS)
    return ([_col(ps, 12 + h) for h in hs], [_row(a_rows, h) for h in hs], [_col(ps, 8 + h) for h in hs],
            [_col(_row(ad, 0), h) for h in hs], [_col(_row(ad, 1), h) for h in hs])


def _head_rows(vals):
    row = _iota((8, LANES), 0)
    tile = jnp.zeros((8, LANES), F32)
    for h, val in enumerate(vals):
        tile = tile + jnp.where(row == h, val, 0.0)
    return tile


def _cparams(n_axes):
    return pltpu.CompilerParams(dimension_semantics=("arbitrary",) * n_axes, vmem_limit_bytes=VMEM_LIMIT)


def _first_visit(acc_axes):
    cond = None
    for a in acc_axes:
        here = pl.program_id(a) == 0
        cond = here if cond is None else jnp.logical_and(cond, here)
    return cond


def _tile(ref):
    val = ref[...]
    shape = val.shape
    while len(shape) > 2 and shape[0] == 1:
        shape = shape[1:]
    return val.reshape(shape)


def _store(ref, val, first):
    val = val.astype(ref.dtype).reshape(ref.shape)
    if first is None:
        ref[...] = val
        return

    @pl.when(first)
    def _():
        ref[...] = val

    @pl.when(jnp.logical_not(first))
    def _():
        ref[...] += val


def _specs(ops):
    return [pl.BlockSpec(block, imap) for _, block, imap in ops]


def tile_fwd(name, fn, grid, ins, outs):
    n_in = len(ins)

    def body(*refs):
        pids = tuple(pl.program_id(a) for a in range(len(grid)))
        firsts = [_first_visit(o[4]) if o[4] else None for o in outs]
        res = fn(False, pids, *[_tile(r) for r in refs[:n_in]])
        for ref, val, first in zip(refs[n_in:], res, firsts):
            _store(ref, val, first)

    out = pl.pallas_call(
        body, grid=grid, in_specs=_specs(ins),
        out_specs=[pl.BlockSpec(o[2], o[3]) for o in outs],
        out_shape=[jax.ShapeDtypeStruct(o[0], o[1]) for o in outs],
        name=name, compiler_params=_cparams(len(grid)),
    )(*[a for a, _, _ in ins])
    return out


def tile_bwd(name, fn, grid, ins, cots, diff, adds=None):
    adds = adds or {}
    n_in, n_cot = len(ins), len(cots)
    add_pos = sorted(adds)
    diff_idx = [d[0] for d in diff]
    out_desc = [d[2] if len(d) > 2 and d[2] is not None else (ins[d[0]][0].shape, ins[d[0]][1], ins[d[0]][2]) for d in diff]
    out_dtypes = [d[3] if len(d) > 3 else F32 for d in diff]

    def body(*refs):
        pids = tuple(pl.program_id(a) for a in range(len(grid)))
        firsts = [_first_visit(d[1]) if d[1] else None for d in diff]
        vals = [_tile(r) for r in refs[:n_in]]
        cot_vals = [_tile(r) for r in refs[n_in:n_in + n_cot]]
        add_vals = [_tile(r) for r in refs[n_in + n_cot:n_in + n_cot + len(add_pos)]]
        out_refs = refs[n_in + n_cot + len(add_pos):]

        def f(*dv):
            full = list(vals)
            for i, val in zip(diff_idx, dv):
                full[i] = val
            return fn(True, pids, *full)

        prim, vjp = jax.vjp(f, *[vals[i].astype(F32) for i in diff_idx])
        grads = list(vjp(tuple(c.astype(o.dtype) for c, o in zip(cot_vals, prim))))
        for pos, val in zip(add_pos, add_vals):
            grads[pos] = grads[pos] + val.astype(F32)
        for ref, val, first in zip(out_refs, grads, firsts):
            _store(ref, val, first)

    all_ins = list(ins) + list(cots) + [adds[p] for p in add_pos]
    out = pl.pallas_call(
        body, grid=grid, in_specs=_specs(all_ins),
        out_specs=[pl.BlockSpec(o[1], o[2]) for o in out_desc],
        out_shape=[jax.ShapeDtypeStruct(o[0], dt) for o, dt in zip(out_desc, out_dtypes)],
        name=name, compiler_params=_cparams(len(grid)),
    )(*[a for a, _, _ in all_ins])
    return out


def _pick(dim, cands):
    for c in cands:
        if dim % c == 0:
            return c
    return dim


MM_TILES = (1024, 512, 1408, 256, 128)


def mm(name, a, b, mode, add=None, out_dtype=F32, blocks=None):
    wide = None
    if mode == "nn":
        (m, kk), n = a.shape, b.shape[-1]
    elif mode == "nt":
        (m, kk), n = a.shape, b.shape[-2]
    else:
        (kk, m), n = a.shape, b.shape[1]
    if blocks is not None:
        lo, n_blk = blocks
        wide = b.shape[-1] if mode != "tn" else n // n_blk
        if mode == "nn":
            n = wide * n_blk
    tm = _pick(m, MM_TILES)
    if mode == "nt" and blocks is not None:
        tn, tk = _pick(n, MM_TILES), _pick(wide, MM_TILES[:-1])
    elif blocks is not None:
        tn, tk = _pick(wide, MM_TILES[:-1]), _pick(kk, MM_TILES)
    else:
        tn, tk = _pick(n, MM_TILES), _pick(kk, MM_TILES)
    nk = kk // tk
    a_spec = pl.BlockSpec((tk, tm), lambda i, j, k: (k, i)) if mode == "tn" else pl.BlockSpec((tm, tk), lambda i, j, k: (i, k))
    o_spec = pl.BlockSpec((tm, tn), lambda i, j, k: (i, j))
    out_shape = (m, n)
    if blocks is None:
        b_spec = pl.BlockSpec((tn, tk), lambda i, j, k: (j, k)) if mode == "nt" else pl.BlockSpec((tk, tn), lambda i, j, k: (k, j))
    elif mode == "nn":
        per = wide // tn
        b_spec = pl.BlockSpec((1, tk, tn), lambda i, j, k: (lo + j // per, k, j % per))
    elif mode == "nt":
        per = wide // tk
        b_spec = pl.BlockSpec((1, tn, tk), lambda i, j, k: (lo + k // per, j, k % per))
    else:
        per = wide // tn
        b_spec = pl.BlockSpec((tk, tn), lambda i, j, k: (k, j))
        o_spec = pl.BlockSpec((1, tm, tn), lambda i, j, k: (j // per, i, j % per))
        out_shape = (n_blk, m, wide)

    def body(*refs):
        a_ref, b_ref = refs[0], refs[1]
        add_ref = refs[2] if add is not None else None
        o_ref, acc = refs[-2], refs[-1]
        k = pl.program_id(2)
        part = _bdot_impl(_tile(a_ref), _tile(b_ref), mode)

        @pl.when(k == 0)
        def _():
            acc[...] = part

        @pl.when(k > 0)
        def _():
            acc[...] += part

        @pl.when(k == nk - 1)
        def _():
            res = acc[...]
            if add_ref is not None:
                res = res + add_ref[...]
            o_ref[...] = res.astype(o_ref.dtype).reshape(o_ref.shape)

    operands = [a, b] + ([add] if add is not None else [])
    in_specs = [a_spec, b_spec] + ([o_spec] if add is not None else [])
    return pl.pallas_call(
        body, grid=(m // tm, n // tn, nk), in_specs=in_specs, out_specs=o_spec,
        out_shape=jax.ShapeDtypeStruct(out_shape, out_dtype),
        scratch_shapes=[pltpu.VMEM((tm, tn), F32)],
        name=name, compiler_params=_cparams(3),
    )(*operands)


def _rows(x, width=None, off=0, tm=256):
    width = x.shape[1] if width is None else width
    return (x, (tm, width), lambda i, off=off: (i, off))


def _whole(x):
    nd = x.ndim
    return (x, x.shape, lambda *pids, nd=nd: (0,) * nd)


def _rms_ops(x, gain):
    return [_rows(x), _whole(gain)]


def rms_fwd(name, x, gain):
    s, dm = x.shape
    return tile_fwd(name, _rms_fn, (s // 256,), _rms_ops(x, gain), [((s, dm), BF16, (256, dm), lambda i: (i, 0), ())])[0]


def rms_bwd(name, x, gain, dh, dres):
    s = x.shape[0]
    return tile_bwd(name, _rms_fn, (s // 256,), _rms_ops(x, gain), [_rows(dh)], [(0, ()), (1, (0,))], adds={0: _rows(dres)})


def loss_call(y, t):
    s, dm = y.shape
    dy, part = tile_fwd("loss", _loss_fn, (s // 256,), [_rows(y), _rows(t)],
                        [((s, dm), F32, (256, dm), lambda i: (i, 0), ()), ((8, LANES), F32, (8, LANES), lambda i: (0, 0), (0,))])
    return dy, part[0, 0]


def _fox_prep_ops(pm, gq, gk):
    tm = 512
    return [(pm, (tm, LANES), lambda i, j: (i, C_FQ // LANES + j)), (pm, (tm, LANES), lambda i, j: (i, C_FK // LANES + j)),
            _whole(gq), _whole(gk)]


def fox_prep_fwd(name, pm, gq, gk):
    s = pm.shape[0]
    out = ((s, BRANCH), BF16, (512, LANES), lambda i, j: (i, j), ())
    return tile_fwd(name, _fox_prep_fn, (s // 512, 4), _fox_prep_ops(pm, gq, gk), [out, out])


def fox_prep_bwd(name, pm, gq, gk, dqn, dkn):
    s = pm.shape[0]
    cot = lambda g: (g, (512, LANES), lambda i, j: (i, j))
    own = ((s, BRANCH), (512, LANES), lambda i, j: (i, j))
    return tile_bwd(name, _fox_prep_fn, (s // 512, 4), _fox_prep_ops(pm, gq, gk), [cot(dqn), cot(dkn)],
                    [(0, (), own, BF16), (1, (), own, BF16), (2, (0, 1)), (3, (0, 1))])


def _fox_gate_ops(f_t, bias):
    return [(f_t, (1,) + f_t.shape[1:], lambda h: (h, 0, 0)), (bias, (1, 1, 1), lambda h: (h, 0, 0))]


def fox_gate_fwd(name, f_t, bias):
    n_h = f_t.shape[0]
    return tile_fwd(name, _fox_gate_fn, (n_h,), _fox_gate_ops(f_t, bias),
                    [(f_t.shape, F32, (1,) + f_t.shape[1:], lambda h: (h, 0, 0), ())])[0]


def fox_gate_bwd(name, f_t, bias, dcum):
    n_h = f_t.shape[0]
    return tile_bwd(name, _fox_gate_fn, (n_h,), _fox_gate_ops(f_t, bias),
                    [(dcum, (1,) + f_t.shape[1:], lambda h: (h, 0, 0))], [(0, ()), (1, ())])


FOX_GROUPS = 4


def _fox_groups(s):
    per = s // FOX_BLOCK // FOX_GROUPS
    return [(g * per, per, (g + 1) * per * FOX_BLOCK) for g in range(FOX_GROUPS)]


def _fox_attn_ops(qn, kn, pm, cum_c, cum_r, q0, keys):
    nb = FOX_BLOCK
    return [(qn, (nb, LANES), lambda p, i: (q0 + i, p)), (kn, (keys, LANES), lambda p, i: (0, p)),
            (pm, (keys, LANES), lambda p, i: (0, C_FV // LANES + p)),
            (cum_c, (1, nb, 1), lambda p, i: (2 * p, q0 + i, 0)), (cum_c, (1, nb, 1), lambda p, i: (2 * p + 1, q0 + i, 0)),
            (cum_r, (1, 1, keys), lambda p, i: (2 * p, 0, 0)), (cum_r, (1, 1, keys), lambda p, i: (2 * p + 1, 0, 0))]


def fox_attn_fwd(name, qn, kn, pm, cum_c, cum_r):
    s = qn.shape[0]
    parts = []
    for g, (q0, n_q, keys) in enumerate(_fox_groups(s)):
        parts.append(tile_fwd(f"{name}_g{g}", functools.partial(_fox_attn_fn, q0), (4, n_q), _fox_attn_ops(qn, kn, pm, cum_c, cum_r, q0, keys),
                              [((n_q * FOX_BLOCK, BRANCH), BF16, (FOX_BLOCK, LANES), lambda p, i: (i, p), ())])[0])
    return jnp.concatenate(parts, axis=0)


def fox_attn_bwd(name, qn, kn, pm, cum_c, cum_r, dy):
    s = qn.shape[0]
    d_qn, d_kn, d_v, d_cum = [], 0.0, 0.0, 0.0
    for g, (q0, n_q, keys) in enumerate(_fox_groups(s)):
        rows = n_q * FOX_BLOCK
        own_q = ((rows, BRANCH), (FOX_BLOCK, LANES), lambda p, i: (i, p))
        own_k = ((keys, BRANCH), (keys, LANES), lambda p, i: (0, p))
        pair_c = ((4, rows, 1), (1, FOX_BLOCK, 1), lambda p, i: (p, i, 0))
        pair_r = ((4, 1, keys), (1, 1, keys), lambda p, i: (p, 0, 0))
        g_qn, g_kn, g_v, g_cqa, g_cqb, g_cka, g_ckb = tile_bwd(
            f"{name}_g{g}", functools.partial(_fox_attn_fn, q0), (4, n_q), _fox_attn_ops(qn, kn, pm, cum_c, cum_r, q0, keys),
            [(dy, (FOX_BLOCK, LANES), lambda p, i, q0=q0: (q0 + i, p))],
            [(0, (), own_q), (1, (1,), own_k), (2, (1,), own_k), (3, (), pair_c), (4, (), pair_c), (5, (1,), pair_r), (6, (1,), pair_r)])
        d_qn.append(g_qn)
        tail = lambda t, axis: jnp.pad(t, [(0, s - keys) if ax == axis else (0, 0) for ax in range(t.ndim)])
        d_kn, d_v = d_kn + tail(g_kn, 0), d_v + tail(g_v, 0)
        by_q = jnp.stack([g_cqa[:, :, 0], g_cqb[:, :, 0]], axis=1).reshape(8, rows)
        by_k = jnp.stack([g_cka[:, 0, :], g_ckb[:, 0, :]], axis=1).reshape(8, keys)
        d_cum = d_cum + jnp.pad(by_q, [(0, 0), (q0 * FOX_BLOCK, s - q0 * FOX_BLOCK - rows)]) + tail(by_k, 1)
    return jnp.concatenate(d_qn, axis=0), d_kn, d_v, d_cum


def sconv_ops(pm, w):
    s = pm.shape[0]
    blk = lambda c0: (pm, (s, LANES), lambda j, c0=c0: (0, c0 // LANES + j))
    return [blk(C_SB), blk(C_SC), blk(C_SV), (w, (w.shape[0], LANES), lambda j: (0, j))]


def dnconv_ops(pm, w):
    s = pm.shape[0]
    return [(pm, (s, LANES), lambda j: (0, C_DN // LANES + j)), (w, (w.shape[0], LANES), lambda j: (0, j))]


def ffn_ops(ug, uv, w):
    s = ug.shape[0]
    n_t = D_FF // LANES
    return [(ug, (s, LANES), lambda j: (0, j)), (uv, (s, LANES), lambda j: (0, j)),
            (w, (w.shape[0], LANES), lambda j: (0, j)), (w, (w.shape[0], LANES), lambda j: (0, n_t + j))]


def _col_out(s, width, dtype=F32):
    return ((s, width), dtype, (s, LANES), lambda j: (0, j), ())


def _col_cot(g):
    return (g, (g.shape[0], LANES), lambda j: (0, j))


def merge_ops(yp, pm):
    gate = lambda b: (pm, (256, D_MODEL), lambda i, b=b: (i, C_GATE // D_MODEL + b))
    return [_rows(yp[0]), _rows(yp[1]), _rows(yp[2]), gate(0), gate(1), gate(2)]


def ple_ops(gpre, pe, x):
    return [_rows(gpre), _rows(pe), _rows(x)]


def adam_call(name, w, g, m, v):
    shape = w.shape
    last = shape[-1]
    rows = w.size // last
    flat = lambda t: t.reshape(rows, last)
    tm = rows
    for cand in (512, 256, 128, 64, 32, 16, 8):
        if rows % cand == 0 and cand * last * 4 <= 2 * 1024 * 1024:
            tm = cand
            break
    spec = lambda t: (flat(t), (tm, last), lambda i: (i, 0))
    out = ((rows, last), F32, (tm, last), lambda i: (i, 0), ())
    res = tile_fwd(name, _adam_fn, (rows // tm,), [spec(w), spec(g), spec(m), spec(v)], [out, out, out])
    return [r.reshape(shape) for r in res]


def _adam_layers_fn(d, pids, w, m, v, g0, g1):
    g = jnp.where(pids[0] == 0, g0, g1)
    return (g,) + _adam_fn(d, pids, w, g, m, v)


def adam_layers(name, w, m, v, g0, g1):
    _, rows, cols = w.shape
    tm = _row_tile(rows, cols)
    n_t = rows // tm
    lay = lambda t: (t, (1, tm, cols), lambda l, i: (l, i, 0))
    ins = [lay(w), lay(m), lay(v), (g0, (tm, cols), lambda l, i: (i * (1 - l) + (n_t - 1) * l, 0)), (g1, (tm, cols), lambda l, i: (i * l, 0))]
    out = (w.shape, F32, (1, tm, cols), lambda l, i: (l, i, 0), ())
    return tile_fwd(name, _adam_layers_fn, (2, n_t), ins, [out, out, out, out])


DN_GROUP = 4


def _dn_local_specs(rev_n=None):
    rows = DN_GROUP * DN_CHUNK
    idx = (lambda j: j) if rev_n is None else (lambda j: rev_n - 1 - j)
    return [pl.BlockSpec((rows, 3 * BRANCH), lambda j: (idx(j), 0)), pl.BlockSpec((rows, LANES), lambda j: (idx(j), 0)),
            pl.BlockSpec((DN_GROUP, DN_HEADS, DN_CHUNK), lambda j: (idx(j), 0, 0)), pl.BlockSpec((2, DN_HEADS), lambda j: (0, 0))]


def _dn_group_inputs(qkv, ps, a_rows, c):
    lo = c * DN_CHUNK
    heads = _split_heads(qkv[lo:lo + DN_CHUNK])
    return heads[0:4], heads[4:8], heads[8:12], ps[lo:lo + DN_CHUNK], a_rows[c]


def dn_local_fwd(name, dn_act, ps, a_rows, ad):
    s = dn_act.shape[0]
    n_c, n_g = s // DN_CHUNK, s // (DN_GROUP * DN_CHUNK)
    rows = DN_GROUP * DN_CHUNK

    def body(qkv_ref, ps_ref, ar_ref, ad_ref, u_ref, kc_ref, qd_ref, kd_ref, qk_ref, gl_ref):
        qkv, ps_v, a_rows_v, ad_v = qkv_ref[...], ps_ref[...], ar_ref[...], ad_ref[...]
        args = [[] for _ in range(8)]
        for c in range(DN_GROUP):
            q4, k4, v4, ps_c, ar_c = _dn_group_inputs(qkv, ps_v, a_rows_v, c)
            for lst, vals in zip(args, (q4, k4, v4) + _dn_gates(ps_c, ar_c, ad_v)):
                lst.extend(vals)
        everything = _dn_local(False, *args)
        for c in range(DN_GROUP):
            res = everything[c * DN_HEADS:(c + 1) * DN_HEADS]
            at = pl.ds(c * DN_CHUNK, DN_CHUNK)
            for ref, i in ((u_ref, 0), (kc_ref, 1), (qd_ref, 2), (kd_ref, 3)):
                ref[at, :] = jnp.concatenate([r[i] for r in res], axis=1)
            for h in range(DN_HEADS):
                qk_ref[c, h] = res[h][4]
            gl_ref[c] = _head_rows([r[5] for r in res])

    wide = pl.BlockSpec((rows, BRANCH), lambda j: (j, 0))
    return pl.pallas_call(
        body, grid=(n_g,), in_specs=_dn_local_specs(),
        out_specs=[wide, wide, wide, wide, pl.BlockSpec((DN_GROUP, DN_HEADS, DN_CHUNK, DN_CHUNK), lambda j: (j, 0, 0, 0)),
                   pl.BlockSpec((DN_GROUP, 8, LANES), lambda j: (j, 0, 0))],
        out_shape=[jax.ShapeDtypeStruct((s, BRANCH), F32)] * 4 + [jax.ShapeDtypeStruct((n_c, DN_HEADS, DN_CHUNK, DN_CHUNK), F32),
                                                                 jax.ShapeDtypeStruct((n_c, 8, LANES), F32)],
        name=name, compiler_params=_cparams(1),
    )(dn_act, ps, a_rows, ad)


def dn_local_bwd(name, dn_act, ps, a_rows, ad, cots):
    s = dn_act.shape[0]
    n_c, n_g = s // DN_CHUNK, s // (DN_GROUP * DN_CHUNK)
    rows = DN_GROUP * DN_CHUNK

    def body(qkv_ref, ps_ref, ar_ref, ad_ref, du_ref, dkc_ref, dqd_ref, dkd_ref, dqk_ref, dgl_ref, dqkv_ref, dps_ref, dar_ref, dad_ref):
        first = pl.program_id(0) == 0
        qkv, ps_v, a_rows_v, ad_v = qkv_ref[...], ps_ref[...], ar_ref[...], ad_ref[...]
        d_wide = [r[...] for r in (du_ref, dkc_ref, dqd_ref, dkd_ref)]
        qs, ks, vs, ps_cs, ar_cs, cot = [], [], [], [], [], []
        for c in range(DN_GROUP):
            q4, k4, v4, ps_c, ar_c = _dn_group_inputs(qkv, ps_v, a_rows_v, c)
            qs, ks, vs, ps_cs, ar_cs = qs + q4, ks + k4, vs + v4, ps_cs + [ps_c], ar_cs + [ar_c]
            lo = c * DN_CHUNK
            d_tiles = [_split_heads(t[lo:lo + DN_CHUNK]) for t in d_wide]
            d_gl = dgl_ref[c]
            cot += [(d_tiles[0][h], d_tiles[1][h], d_tiles[2][h], d_tiles[3][h], dqk_ref[c, h], _col(_row(d_gl, h), 0))
                    for h in range(DN_HEADS)]

        def f(qs, ks, vs, ps_cs, ar_cs, ad_v):
            gates = [[] for _ in range(5)]
            for ps_c, ar_c in zip(ps_cs, ar_cs):
                for lst, vals in zip(gates, _dn_gates(ps_c, ar_c, ad_v)):
                    lst.extend(vals)
            return _dn_local(True, qs, ks, vs, *gates)

        _, vjp = jax.vjp(f, qs, ks, vs, ps_cs, ar_cs, ad_v)
        d_q, d_k, d_v, d_ps, d_ar, d_ad = vjp(cot)
        for c in range(DN_GROUP):
            at, hs = pl.ds(c * DN_CHUNK, DN_CHUNK), slice(c * DN_HEADS, (c + 1) * DN_HEADS)
            dqkv_ref[at, :] = jnp.concatenate(d_q[hs] + d_k[hs] + d_v[hs], axis=1).astype(dqkv_ref.dtype)
            dps_ref[at, :] = d_ps[c]
            dar_ref[c] = d_ar[c]
        _store(dad_ref, d_ad, first)

    wide = pl.BlockSpec((rows, BRANCH), lambda j: (j, 0))
    specs = _dn_local_specs()
    return pl.pallas_call(
        body, grid=(n_g,),
        in_specs=specs + [wide, wide, wide, wide, pl.BlockSpec((DN_GROUP, DN_HEADS, DN_CHUNK, DN_CHUNK), lambda j: (j, 0, 0, 0)),
                          pl.BlockSpec((DN_GROUP, 8, LANES), lambda j: (j, 0, 0))],
        out_specs=specs,
        out_shape=[jax.ShapeDtypeStruct((s, 3 * BRANCH), F32), jax.ShapeDtypeStruct((s, LANES), F32),
                   jax.ShapeDtypeStruct((n_c, DN_HEADS, DN_CHUNK), F32), jax.ShapeDtypeStruct((2, DN_HEADS), F32)],
        name=name, compiler_params=_cparams(1),
    )(dn_act, ps, a_rows, ad, *cots)


def _dn_scan_specs(n_c, rev):
    idx = (lambda j: n_c - 1 - j) if rev else (lambda j: j)
    wide = pl.BlockSpec((DN_CHUNK, BRANCH), lambda j: (idx(j), 0))
    return [wide, wide, wide, wide, pl.BlockSpec((1, DN_HEADS, DN_CHUNK, DN_CHUNK), lambda j: (idx(j), 0, 0, 0)),
            pl.BlockSpec((1, 8, LANES), lambda j: (idx(j), 0, 0)), pl.BlockSpec((DN_CHUNK, BRANCH), lambda j: (idx(j), C_DZ // BRANCH)),
            pl.BlockSpec((1, DN_DH), lambda j: (0, 0))]


def _dn_scan_tiles(refs):
    u_ref, kc_ref, qd_ref, kd_ref, qk_ref, gl_ref, z_ref, g_ref = refs
    wide = [_split_heads(r[...]) for r in (u_ref, kc_ref, qd_ref, kd_ref)]
    gl = gl_ref[0]
    return [(wide[0][h], wide[1][h], wide[2][h], wide[3][h], qk_ref[0, h], _col(_row(gl, h), 0)) for h in range(DN_HEADS)], \
        _split_heads(z_ref[...]), g_ref[...]


def dn_scan_fwd(name, local, pm, gain):
    s = pm.shape[0]
    n_c = s // DN_CHUNK

    def body(*refs):
        y_ref, hist_ref, state = refs[8:]

        @pl.when(pl.program_id(0) == 0)
        def _():
            state[...] = jnp.zeros_like(state)

        hist_ref[0] = state[...]
        per_head, z4, gain_v = _dn_scan_tiles(refs[:8])
        ys, s_nexts = _dn_step(False, [state[h] for h in range(DN_HEADS)], per_head, z4, gain_v)
        for h in range(DN_HEADS):
            state[h] = s_nexts[h]
        y_ref[...] = jnp.concatenate(ys, axis=1).astype(y_ref.dtype)

    return pl.pallas_call(
        body, grid=(n_c,), in_specs=_dn_scan_specs(n_c, False),
        out_specs=[pl.BlockSpec((DN_CHUNK, BRANCH), lambda j: (j, 0)),
                   pl.BlockSpec((1, DN_HEADS, DN_DH, DN_DH), lambda j: (j, 0, 0, 0))],
        out_shape=[jax.ShapeDtypeStruct((s, BRANCH), BF16), jax.ShapeDtypeStruct((n_c, DN_HEADS, DN_DH, DN_DH), F32)],
        scratch_shapes=[pltpu.VMEM((DN_HEADS, DN_DH, DN_DH), F32)],
        name=name, compiler_params=_cparams(1),
    )(*local, pm, gain)


def dn_scan_bwd(name, local, pm, gain, hist, dy):
    s = pm.shape[0]
    n_c = s // DN_CHUNK

    def body(*refs):
        hist_ref, dy_ref = refs[8:10]
        du_ref, dkc_ref, dqd_ref, dkd_ref, dqk_ref, dgl_ref, dz_ref, dg_ref, d_state = refs[10:]
        first = pl.program_id(0) == 0

        @pl.when(first)
        def _():
            d_state[...] = jnp.zeros_like(d_state)

        per_head, z4, gain_v = _dn_scan_tiles(refs[:8])
        _, vjp = jax.vjp(functools.partial(_dn_step, True), [hist_ref[0, h] for h in range(DN_HEADS)], per_head, z4, gain_v)
        d_s, grads, d_z, d_gain = vjp((_split_heads(dy_ref[...]), [d_state[h] for h in range(DN_HEADS)]))
        for h in range(DN_HEADS):
            d_state[h] = d_s[h]
        for ref, i in ((du_ref, 0), (dkc_ref, 1), (dqd_ref, 2), (dkd_ref, 3)):
            ref[...] = jnp.concatenate([g[i] for g in grads], axis=1)
        dz_ref[...] = jnp.concatenate(d_z, axis=1).astype(dz_ref.dtype)
        for h in range(DN_HEADS):
            dqk_ref[0, h] = grads[h][4]
        dgl_ref[0] = _head_rows([g[5] for g in grads])
        _store(dg_ref, d_gain, first)

    rev = lambda j: n_c - 1 - j
    specs = _dn_scan_specs(n_c, True)
    return pl.pallas_call(
        body, grid=(n_c,),
        in_specs=specs + [pl.BlockSpec((1, DN_HEADS, DN_DH, DN_DH), lambda j: (rev(j), 0, 0, 0)),
                          pl.BlockSpec((DN_CHUNK, BRANCH), lambda j: (rev(j), 0))],
        out_specs=specs[:6] + [pl.BlockSpec((DN_CHUNK, BRANCH), lambda j: (rev(j), 0)), specs[7]],
        out_shape=[jax.ShapeDtypeStruct((s, BRANCH), F32)] * 4 + [
            jax.ShapeDtypeStruct((n_c, DN_HEADS, DN_CHUNK, DN_CHUNK), F32), jax.ShapeDtypeStruct((n_c, 8, LANES), F32),
            jax.ShapeDtypeStruct((s, BRANCH), BF16), jax.ShapeDtypeStruct((1, DN_DH), F32)],
        scratch_shapes=[pltpu.VMEM((DN_HEADS, DN_DH, DN_DH), F32)],
        name=name, compiler_params=_cparams(1),
    )(*local, pm, gain, hist, dy)


def _seq_layouts(cols, s):
    return cols.T.reshape(cols.shape[1], s // LANES, LANES)


def layer_fwd(li, x, p, w, more_weights=None):
    s = x.shape[0]
    n = lambda t: f"{t}_l{li}"
    h = rms_fwd(n("rms_mix"), x, w["g_mix"])
    pm = mm(n("in_main"), h, w["in_main"], "nn")
    ps = mm(n("in_small"), h, w["in_small"], "nn")
    qn, kn = fox_prep_fwd(n("fox_prep"), pm, w["gq"], w["gk"])
    f_t = _seq_layouts(ps[:, 0:8], s)
    cum = fox_gate_fwd(n("fox_gate"), f_t, w["b_f"])
    cum_c, cum_r = cum.reshape(8, s, 1), cum.reshape(8, 1, s)
    y_fox = fox_attn_fwd(n("fox_attn"), qn, kn, pm, cum_c, cum_r)
    y_sc = tile_fwd(n("sconv"), _sconv_fn, (BRANCH // LANES,), sconv_ops(pm, w["sc_conv_w"]), [_col_out(s, BRANCH, BF16)])[0]
    dn_act = tile_fwd(n("dnconv"), _dnconv_fn, (3 * BRANCH // LANES,), dnconv_ops(pm, w["dn_conv_w"]), [_col_out(s, 3 * BRANCH)])[0]
    a_rows = ps[:, 12:16].reshape(s // DN_CHUNK, DN_CHUNK, DN_HEADS).transpose(0, 2, 1)
    dn_local = dn_local_fwd(n("dn_local"), dn_act, ps, a_rows, w["ad"])
    y_dn, hist = dn_scan_fwd(n("dn_scan"), dn_local, pm, w["dn_gain"])
    ys = (y_fox, y_sc, y_dn)
    if more_weights is not None:
        w = {**w, **more_weights(y_dn)}
    yp = [mm(n(f"branch{b}"), ys[b], w["branch"][b], "nn", blocks=(0, N_CHIPS)) for b in range(3)]
    merged = tile_fwd(n("merge"), _merge_fn, (s // 256,), merge_ops(yp, pm), [((s, D_MODEL), BF16, (256, D_MODEL), lambda i: (i, 0), ())])[0]
    x1 = mm(n("w_o"), merged, w["o"], "nn", add=x)
    h2 = rms_fwd(n("rms_ffn"), x1, w["g_ffn"])
    ug = mm(n("up_g"), h2, w["up"], "nn", blocks=(0, 2))
    uv = mm(n("up_v"), h2, w["up"], "nn", blocks=(2, 2))
    act = tile_fwd(n("ffn_act"), _ffn_act_fn, (D_FF // LANES,), ffn_ops(ug, uv, w["ffn_conv_w"]), [_col_out(s, D_FF, BF16)])[0]
    x2 = mm(n("down"), act, w["down"], "nn", add=x1)
    h3 = rms_fwd(n("rms_ple"), x2, w["g_ple"])
    gpre = mm(n("ple_gate"), h3, w["pg"], "nn")
    pe = mm(n("ple_emb"), p, w["ple"], "nn", blocks=(0, N_CHIPS))
    x3 = tile_fwd(n("ple"), _ple_fn, (s // 256,), ple_ops(gpre, pe, x2), [((s, D_MODEL), F32, (256, D_MODEL), lambda i: (i, 0), ())])[0]
    saved = dict(x=x, h=h, pm=pm, ps=ps, qn=qn, kn=kn, f_t=f_t, cum_c=cum_c, cum_r=cum_r, ys=ys, dn_act=dn_act, dn_local=dn_local,
                 a_rows=a_rows, hist=hist, yp=yp, merged=merged, x1=x1, h2=h2, ug=ug, uv=uv, act=act, x2=x2, h3=h3,
                 gpre=gpre, pe=pe, p=p)
    return x3, saved, w


def hang_on(w, token):
    zero = token[0, 0]
    small = ("g_mix", "g_ffn", "g_ple", "gq", "gk", "b_f", "ad", "dn_gain", "sc_conv_w", "dn_conv_w", "ffn_conv_w")
    return {**w, **{k: w[k] + zero for k in small}}


def layer_bwd(li, dx3, sv, w, hooks=None):
    hooks = hooks or {}

    def stage(key, after, w):
        return hang_on(w, hooks[key](after, g)) if key in hooks else w

    s = dx3.shape[0]
    n = lambda t: f"{t}_l{li}"
    g = {}
    col_own = lambda width: ((s, width), (s, LANES), lambda j: (0, j))
    d_gpre, d_pe = tile_bwd(n("ple_bwd"), _ple_fn, (s // 256,), ple_ops(sv["gpre"], sv["pe"], sv["x2"]), [_rows(dx3)],
                            [(0, (), None, BF16), (1, (), None, BF16)])
    g["w_ple"] = mm(n("d_w_ple"), sv["p"], d_pe, "tn", blocks=(0, N_CHIPS))
    g["w_ple_gate"] = mm(n("d_w_pg"), sv["h3"], d_gpre, "tn").reshape(N_CHIPS, -1, D_MODEL)
    dh3 = mm(n("d_h3"), d_gpre, w["pg"], "nt")
    dx2, d_g_ple = rms_bwd(n("rms_ple_bwd"), sv["x2"], w["g_ple"], dh3, dx3)
    dact = mm(n("d_act"), dx2, w["down"], "nt")
    g["w_down"] = mm(n("d_w_down"), sv["act"], dx2, "tn").reshape(N_CHIPS, -1, D_MODEL)
    taps_own = ((w["ffn_conv_w"].shape[0], D_FF), (w["ffn_conv_w"].shape[0], LANES), lambda j: (0, j))
    d_ug, d_uv, d_fw_g, d_fw_v = tile_bwd(n("ffn_act_bwd"), _ffn_act_fn, (D_FF // LANES,), ffn_ops(sv["ug"], sv["uv"], w["ffn_conv_w"]),
                                          [_col_cot(dact)], [(0, (), None, BF16), (1, (), None, BF16), (2, (), taps_own), (3, (), taps_own)])
    g["ffn_conv_w"] = jnp.concatenate([d_fw_g, d_fw_v], axis=1)
    g["w_up"] = jnp.concatenate([mm(n("d_w_up_g"), sv["h2"], d_ug, "tn", blocks=(0, 2)), mm(n("d_w_up_v"), sv["h2"], d_uv, "tn", blocks=(0, 2))])
    dh2 = mm(n("d_h2_v"), d_uv, w["up"], "nt", blocks=(2, 2), add=mm(n("d_h2_g"), d_ug, w["up"], "nt", blocks=(0, 2)))
    dx1, d_g_ffn = rms_bwd(n("rms_ffn_bwd"), sv["x1"], w["g_ffn"], dh2, dx2)
    w = stage("mid", dx1, w)
    dmerged = mm(n("d_merged"), dx1, w["o"], "nt")
    g["w_o"] = mm(n("d_w_o"), sv["merged"], dx1, "tn").reshape(N_CHIPS, -1, D_MODEL)
    gate_own = ((s, D_MODEL), (256, D_MODEL), lambda i: (i, 0))
    d_yp0, d_yp1, d_yp2, d_g0, d_g1, d_g2 = tile_bwd(
        n("merge_bwd"), _merge_fn, (s // 256,), merge_ops(sv["yp"], sv["pm"]), [_rows(dmerged)],
        [(0, (), None, BF16), (1, (), None, BF16), (2, (), None, BF16), (3, (), gate_own, BF16), (4, (), gate_own, BF16), (5, (), gate_own, BF16)])
    d_yp = (d_yp0, d_yp1, d_yp2)
    g["w_branch"] = jnp.concatenate([mm(n(f"d_w_branch{b}"), sv["ys"][b], d_yp[b], "tn", blocks=(0, N_CHIPS)) for b in range(3)], axis=1)
    d_ys = [mm(n(f"d_y{b}"), d_yp[b], w["branch"][b], "nt", blocks=(0, N_CHIPS)) for b in range(3)]
    w = stage("late", d_ys[2], w)
    *d_local, d_z, d_dngain = dn_scan_bwd(n("dn_scan_bwd"), sv["dn_local"], sv["pm"], w["dn_gain"], sv["hist"], d_ys[2])
    d_dnact, d_ps_dn, d_arows, d_ad = dn_local_bwd(n("dn_local_bwd"), sv["dn_act"], sv["ps"], sv["a_rows"], w["ad"], d_local)
    g["ad"], g["dn_norm_gain"] = d_ad, d_dngain[0]
    d_dnqkv, g["dn_conv_w"] = tile_bwd(n("dnconv_bwd"), _dnconv_fn, (3 * BRANCH // LANES,), dnconv_ops(sv["pm"], w["dn_conv_w"]),
                                       [_col_cot(d_dnact)], [(0, (), col_own(3 * BRANCH), BF16), (1, ())])
    d_sb, d_sc, d_sv, g["sc_conv_w"] = tile_bwd(n("sconv_bwd"), _sconv_fn, (BRANCH // LANES,), sconv_ops(sv["pm"], w["sc_conv_w"]), [_col_cot(d_ys[1])],
                                                [(0, (), col_own(BRANCH), BF16), (1, (), col_own(BRANCH), BF16), (2, (), col_own(BRANCH), BF16), (3, ())])
    w = stage("last", d_dnqkv, w)
    d_qn, d_kn, d_fv, d_cum = fox_attn_bwd(n("fox_attn_bwd"), sv["qn"], sv["kn"], sv["pm"], sv["cum_c"], sv["cum_r"], d_ys[0])
    d_ft, d_bf = fox_gate_bwd(n("fox_gate_bwd"), sv["f_t"], w["b_f"], d_cum.reshape(8, s // LANES, LANES))
    g["b_fox_f"] = d_bf.reshape(8)
    d_fq, d_fk, d_gq, d_gk = fox_prep_bwd(n("fox_prep_bwd"), sv["pm"], w["gq"], w["gk"], d_qn, d_kn)
    g["fox_q_gain"] = d_gq[0, :FOX_DH] + d_gq[0, FOX_DH:]
    g["fox_k_gain"] = d_gk[0, :FOX_DH] + d_gk[0, FOX_DH:]
    d_pm = jnp.concatenate([d_fq, d_fk, d_fv.astype(BF16), d_sb, d_sc, d_sv, d_dnqkv, d_z, d_g0, d_g1, d_g2], axis=1)
    d_a_cols = d_arows.transpose(0, 2, 1).reshape(s, DN_HEADS)
    d_f_cols = d_ft.reshape(8, s).T
    d_ps = d_ps_dn + jnp.concatenate([d_f_cols, jnp.zeros((s, 4), F32), d_a_cols, jnp.zeros((s, LANES - 16), F32)], axis=1)
    g["w_in"] = chip_blocks_w_in(mm(n("d_w_in_main"), sv["h"], d_pm, "tn"), mm(n("d_w_in_small"), sv["h"], d_ps, "tn"))
    dh = mm(n("d_h_small"), d_ps, w["in_small"], "nt", add=mm(n("d_h_main"), d_pm, w["in_main"], "nt"))
    dx, d_g_mix = rms_bwd(n("rms_mix_bwd"), sv["x"], w["g_mix"], dh, dx1)
    g["g_mix"], g["g_ffn"], g["g_ple"] = d_g_mix[0], d_g_ffn[0], d_g_ple[0]
    return dx, g


IN_SHARD = 2052
MAIN_RANGES = ((0, 1536), (1544, 3080), (3080, 4616), (4624, 5136), (5136, 8208))
SMALL_RANGES = ((1536, 1544), (4616, 4620), (4620, 4624))


def _from_chip_blocks(blocks, ranges):
    parts = []
    for lo, hi in ranges:
        for k in range(N_CHIPS):
            a0, a1 = max(lo, k * IN_SHARD), min(hi, (k + 1) * IN_SHARD)
            if a0 < a1:
                parts.append(blocks[k][:, a0 - k * IN_SHARD:a1 - k * IN_SHARD])
    return parts


def split_w_in(blocks):
    main = jnp.concatenate(_from_chip_blocks(blocks, MAIN_RANGES), axis=1)
    pad = jnp.zeros((blocks.shape[1], LANES - 16), blocks.dtype)
    return main, jnp.concatenate(_from_chip_blocks(blocks, SMALL_RANGES) + [pad], axis=1)


def chip_blocks_w_in(main, small):
    pieces, m_off, s_off = [], 0, 0
    ranges = sorted([(lo, hi, "m") for lo, hi in MAIN_RANGES] + [(lo, hi, "s") for lo, hi in SMALL_RANGES])
    offs = {}
    for lo, hi in MAIN_RANGES:
        offs[lo] = m_off
        m_off += hi - lo
    for lo, hi in SMALL_RANGES:
        offs[lo] = s_off
        s_off += hi - lo
    blocks = []
    for k in range(N_CHIPS):
        parts = []
        for lo, hi, src in ranges:
            a0, a1 = max(lo, k * IN_SHARD), min(hi, (k + 1) * IN_SHARD)
            if a0 < a1:
                arr = main if src == "m" else small
                parts.append(arr[:, offs[lo] + a0 - lo:offs[lo] + a1 - lo])
        blocks.append(jnp.concatenate(parts, axis=1))
    return jnp.stack(blocks)


def later_weights(got):
    g_branch, g_o, g_up, g_down, g_pg, g_ple = got
    branch = g_branch.reshape(N_CHIPS, 3, BRANCH, -1)
    return dict(branch=[branch[:, b] for b in range(3)], o=g_o.reshape(D_MODEL, D_MODEL), up=g_up,
                down=g_down.reshape(D_FF, D_MODEL), pg=g_pg.reshape(D_MODEL, D_MODEL), ple=g_ple)


def layer_weights(li, got, conv, a):
    main, small = split_w_in(got[0])
    tile2 = lambda v: jnp.concatenate([v, v])[None, :]
    rest = later_weights(got[1:]) if len(got) > 1 else {}
    return dict(
        in_main=main, in_small=small, **rest,
        g_mix=a["g_mix"][li][None, :], g_ffn=a["g_ffn"][li][None, :], g_ple=a["g_ple"][li][None, :],
        gq=tile2(a["fox_q_gain"][li]), gk=tile2(a["fox_k_gain"][li]), b_f=a["b_fox_f"][li].reshape(8, 1, 1),
        ad=jnp.stack([a["dn_a_log"][li], a["dn_dt_bias"][li]]), dn_gain=a["dn_norm_gain"][li][None, :],
        sc_conv_w=conv["sc_conv_w"][li], dn_conv_w=conv["dn_conv_w"][li], ffn_conv_w=conv["ffn_conv_w"][li])


def pack_rows(arrs, dtype):
    flat = jnp.concatenate([t.reshape(-1).astype(dtype) for t in arrs])
    pad = (-flat.shape[0]) % (8 * LANES)
    if pad:
        flat = jnp.concatenate([flat, jnp.zeros((pad,), dtype)])
    return flat.reshape(-1, LANES)


def unpack_rows(buf, shapes):
    flat = buf.reshape(-1)
    out, off = [], 0
    for shp in shapes:
        size = 1
        for dim in shp:
            size *= dim
        out.append(flat[off:off + size].reshape(shp))
        off += size
    return out


def chip_shard(t, axis, k):
    width = t.shape[axis] // N_CHIPS
    return lax.slice_in_dim(t, k * width, (k + 1) * width, axis=axis)


ANY = pl.BlockSpec(memory_space=pl.ANY)


def _position():
    x, y, c = lax.axis_index("x"), lax.axis_index("y"), lax.axis_index("c")
    return x, y, c, [(1 - x, y), (x, 1 - y), (1 - x, 1 - y)]


def gather_small(name, block):
    m_per, n = block.shape

    def body(x_ref, out_ref, token, send_sems, recv_sems, local_sem):
        token[...] = jnp.zeros_like(token)
        x, y, c, chips = _position()
        me, sibling = (x, y, c), (x, y, 1 - c)

        def rows(px, py, pc):
            return out_ref.at[pl.ds((4 * px + 2 * py + pc) * m_per, m_per), :]

        def copy(k, blk, to, src=None):
            return pltpu.make_async_remote_copy(src_ref=rows(*blk) if src is None else src, dst_ref=rows(*blk),
                                                send_sem=send_sems.at[k], recv_sem=recv_sems.at[k], device_id=to, device_id_type=MESH)

        mine = pltpu.make_async_copy(x_ref, rows(*me), local_sem)
        mine.start()
        first = [copy(0, me, sibling, src=x_ref)] + [copy(1 + j, me, (*chip, c), src=x_ref) for j, chip in enumerate(chips)]
        for cp in first:
            cp.start()
        passed = [copy(4 + j, (*chip, c), sibling) for j, chip in enumerate(chips)]
        for j, chip in enumerate(chips):
            copy(1 + j, (*chip, c), me).wait_recv()
            passed[j].start()
        copy(0, sibling, me).wait_recv()
        for j, chip in enumerate(chips):
            copy(4 + j, (*chip, 1 - c), me).wait_recv()
        for cp in first + passed:
            cp.wait_send()
        mine.wait()

    in_vmem = pl.BlockSpec(memory_space=pltpu.VMEM)
    return pl.pallas_call(
        body, out_shape=[jax.ShapeDtypeStruct((8 * m_per, n), block.dtype), jax.ShapeDtypeStruct((8, LANES), F32)],
        in_specs=[in_vmem], out_specs=[in_vmem, in_vmem],
        scratch_shapes=[pltpu.SemaphoreType.DMA((7,)), pltpu.SemaphoreType.DMA((7,)), pltpu.SemaphoreType.DMA],
        name=name, compiler_params=pltpu.CompilerParams(vmem_limit_bytes=VMEM_LIMIT),
    )(block)


def _sems(n):
    return [pltpu.SemaphoreType.DMA((n,)), pltpu.SemaphoreType.DMA((n,))]


def gather_layer(name, shards):
    n_w = len(shards)
    halves = [s.shape[0] // 2 for s in shards]

    def body(*refs):
        ins, outs = refs[:n_w], refs[n_w:2 * n_w]
        token, send_sems, recv_sems = refs[2 * n_w:]
        token[...] = jnp.zeros_like(token)
        x, y, c, chips = _position()
        sibling = (x, y, 1 - c)

        def part(w, px, py, pc):
            return outs[w].at[2 * px + py, pl.ds(pc * halves[w], halves[w]), :]

        def copy(k, w, blk, to, src=None):
            return pltpu.make_async_remote_copy(src_ref=part(w, *blk) if src is None else src, dst_ref=part(w, *blk),
                                                send_sem=send_sems.at[k], recv_sem=recv_sems.at[k], device_id=to, device_id_type=MESH)

        pairs = [(w, j, chip) for w in range(n_w) for j, chip in enumerate(chips)]
        first = [copy(3 * w + j, w, (x, y, c), (*chip, c), src=ins[w].at[pl.ds(c * halves[w], halves[w]), :]) for w, j, chip in pairs]
        for cp in first:
            cp.start()
        passed = [copy(3 * n_w + 3 * w + j, w, (*chip, c), sibling) for w, j, chip in pairs]
        for (w, j, chip), fwd in zip(pairs, passed):
            copy(3 * w + j, w, (*chip, c), (x, y, c)).wait_recv()
            fwd.start()
        for w, j, chip in pairs:
            copy(3 * n_w + 3 * w + j, w, (*chip, 1 - c), (x, y, c)).wait_recv()
        for cp in first + passed:
            cp.wait_send()

    out = pl.pallas_call(
        body, out_shape=[jax.ShapeDtypeStruct((N_CHIPS,) + s.shape, s.dtype) for s in shards] + [jax.ShapeDtypeStruct((8, LANES), F32)],
        in_specs=[ANY] * n_w, out_specs=[ANY] * n_w + [pl.BlockSpec(memory_space=pltpu.VMEM)], scratch_shapes=_sems(6 * n_w), name=name,
    )(*shards)
    return out[:n_w], out[n_w]


def swap_halves(name, grads):
    n_w = len(grads)
    halves = [g.shape[1] // 2 for g in grads]

    def body(*refs):
        ins, outs = refs[:n_w], refs[n_w:2 * n_w]
        send_sems, recv_sems = refs[2 * n_w:]
        x, y, c, _ = _position()
        cps = [pltpu.make_async_remote_copy(src_ref=ins[w].at[:, pl.ds((1 - c) * halves[w], halves[w]), :], dst_ref=outs[w],
                                            send_sem=send_sems.at[w], recv_sem=recv_sems.at[w], device_id=(x, y, 1 - c),
                                            device_id_type=MESH) for w in range(n_w)]
        for cp in cps:
            cp.start()
        for cp in cps:
            cp.wait()

    return pl.pallas_call(
        body, out_shape=[jax.ShapeDtypeStruct((N_CHIPS, h, g.shape[2]), g.dtype) for g, h in zip(grads, halves)],
        in_specs=[ANY] * n_w, out_specs=[ANY] * n_w, scratch_shapes=_sems(n_w), name=name,
    )(*grads)


def scatter_chips(name, partials):
    n_w = len(partials)

    def body(*refs):
        ins, outs = refs[:n_w], refs[n_w:2 * n_w]
        send_sems, recv_sems = refs[2 * n_w:]
        x, y, c, chips = _position()
        cps = [pltpu.make_async_remote_copy(src_ref=ins[w].at[2 * cx + cy], dst_ref=outs[w].at[j], send_sem=send_sems.at[3 * w + j],
                                            recv_sem=recv_sems.at[3 * w + j], device_id=(cx, cy, c), device_id_type=MESH)
               for w in range(n_w) for j, (cx, cy) in enumerate(chips)]
        for cp in cps:
            cp.start()
        for cp in cps:
            cp.wait()

    return pl.pallas_call(
        body, out_shape=[jax.ShapeDtypeStruct((3,) + p.shape[1:], p.dtype) for p in partials],
        in_specs=[ANY] * n_w, out_specs=[ANY] * n_w, scratch_shapes=_sems(3 * n_w), name=name,
    )(*partials)


def share_halves(name, bufs):
    n_w = len(bufs)
    halves = [b.shape[0] // 2 for b in bufs]

    def body(*refs):
        outs = refs[n_w:2 * n_w]
        send_sems, recv_sems = refs[2 * n_w:]
        x, y, c, _ = _position()

        def copy(w, pc):
            half = outs[w].at[pl.ds(pc * halves[w], halves[w]), :]
            return pltpu.make_async_remote_copy(src_ref=half, dst_ref=half, send_sem=send_sems.at[w], recv_sem=recv_sems.at[w],
                                                device_id=(x, y, 1 - c), device_id_type=MESH)

        for w in range(n_w):
            copy(w, c).start()
        for w in range(n_w):
            copy(w, 1 - c).wait_recv()
            copy(w, c).wait_send()

    return pl.pallas_call(
        body, out_shape=[jax.ShapeDtypeStruct(b.shape, b.dtype) for b in bufs], in_specs=[ANY] * n_w, out_specs=[ANY] * n_w,
        input_output_aliases={w: w for w in range(n_w)}, scratch_shapes=_sems(n_w), name=name,
    )(*bufs)


HBM = pl.BlockSpec(memory_space=pltpu.HBM)
SEM = pl.BlockSpec(memory_space=pltpu.SEMAPHORE)
EFFECT = pltpu.SideEffectType.DATAFLOW_SIDE_EFFECTING


def _exchange_copies(kind, srcs, lands):
    x, y, c, chips = _position()
    out = []
    for src, land in zip(srcs, lands):
        if kind == "swap":
            half = src.shape[1] // 2
            out.append((src.at[:, pl.ds((1 - c) * half, half), :], land, (x, y, 1 - c)))
            continue
        for j, (cx, cy) in enumerate(chips):
            if kind == "gather":
                out.append((src, land.at[2 * x + y], (cx, cy, c)))
            else:
                out.append((src.at[2 * cx + cy], land.at[j], (cx, cy, c)))
    return out


def _land_shapes(kind, srcs):
    if kind == "gather":
        return [(N_CHIPS,) + s.shape for s in srcs]
    if kind == "swap":
        return [(N_CHIPS, s.shape[1] // 2, s.shape[2]) for s in srcs]
    return [(3,) + s.shape[1:] for s in srcs]


def exchange_start(name, kind, srcs):
    n_w = len(srcs)
    shapes = _land_shapes(kind, srcs)
    n_sem = n_w if kind == "swap" else 3 * n_w

    def body(*refs):
        ins, lands = refs[:n_w], refs[n_w:2 * n_w]
        send_sems, recv_sems = refs[2 * n_w:2 * n_w + 2]
        token = refs[-1]
        for i, (src, dst, dev) in enumerate(_exchange_copies(kind, ins, lands)):
            pltpu.make_async_remote_copy(src_ref=src, dst_ref=dst, send_sem=send_sems.at[i], recv_sem=recv_sems.at[i],
                                         device_id=dev, device_id_type=MESH).start()
        token[...] = jnp.zeros_like(token)

    out = pl.pallas_call(
        body, name=name,
        out_shape=(pltpu.SemaphoreType.DMA((n_sem,)), pltpu.SemaphoreType.DMA((n_sem,)),
                   *[pltpu.HBM(s.shape, s.dtype) for s in srcs], *[pltpu.HBM(shp, s.dtype) for shp, s in zip(shapes, srcs)],
                   jax.ShapeDtypeStruct((8, LANES), F32)),
        in_specs=(HBM,) * (2 * n_w), out_specs=(SEM, SEM) + (HBM,) * (2 * n_w) + (pl.BlockSpec(memory_space=pltpu.VMEM),),
        input_output_aliases={i: 2 + i for i in range(2 * n_w)},
        compiler_params=pltpu.CompilerParams(has_side_effects=EFFECT),
    )(*[pltpu.with_memory_space_constraint(s, pltpu.HBM) for s in srcs],
      *[pltpu.with_memory_space_constraint(lax.empty(shp, s.dtype), pltpu.HBM) for shp, s in zip(shapes, srcs)])
    return (kind, n_w, out[:-1]), out[-1]


def exchange_wait(name, handle, after):
    kind, n_w, (send_sems, recv_sems, *thru) = handle
    n_sem = n_w if kind == "swap" else 3 * n_w

    def body(*refs):
        ins, lands = refs[:n_w], refs[n_w:2 * n_w]
        send_sems, recv_sems = refs[2 * n_w:2 * n_w + 2]
        for i, (src, dst, dev) in enumerate(_exchange_copies(kind, ins, lands)):
            cp = pltpu.make_async_remote_copy(src_ref=src, dst_ref=dst, send_sem=send_sems.at[i], recv_sem=recv_sems.at[i],
                                              device_id=dev, device_id_type=MESH)
            cp.wait_send()
            cp.wait_recv()

    out = pl.pallas_call(
        body, name=name, out_shape=tuple(pltpu.HBM(t.shape, t.dtype) for t in thru),
        in_specs=(HBM,) * (2 * n_w) + (SEM, SEM, pl.BlockSpec(memory_space=pl.ANY)), out_specs=(HBM,) * (2 * n_w),
        input_output_aliases={i: i for i in range(2 * n_w)},
        compiler_params=pltpu.CompilerParams(has_side_effects=EFFECT),
    )(*thru, send_sems, recv_sems, after)
    return list(out[n_w:])


def _row_tile(rows, cols):
    best = 16
    for t in range(16, rows + 1, 16):
        if rows % t == 0 and t * cols * 4 <= 1024 * 1024:
            best = t
    return best


def pair_sum(name, pos, grad, from_sibling):
    _, rows, cols = grad.shape
    half = rows // 2
    tr = _row_tile(half, cols)
    n_t = half // tr

    def body(pos_ref, g_ref, s_ref, b_ref, f_ref):
        tot = g_ref[...] + s_ref[...]
        b_ref[...] = tot.astype(BF16)

        @pl.when(pl.program_id(1) == pos_ref[1])
        def _():
            f_ref[...] = tot[0]

    blk = pl.BlockSpec((1, tr, cols), lambda i, k, pos: (k, i, 0))
    return pl.pallas_call(
        body, grid_spec=pltpu.PrefetchScalarGridSpec(
            num_scalar_prefetch=1, grid=(n_t, N_CHIPS),
            in_specs=[pl.BlockSpec((1, tr, cols), lambda i, k, pos: (k, pos[0] * n_t + i, 0)), blk],
            out_specs=[blk, pl.BlockSpec((tr, cols), lambda i, k, pos: (i, 0))]),
        out_shape=[jax.ShapeDtypeStruct((N_CHIPS, half, cols), BF16), jax.ShapeDtypeStruct((half, cols), F32)],
        name=name, compiler_params=_cparams(2),
    )(pos, grad, from_sibling)


def chip_sum(name, pos, own, landed):
    half, cols = own.shape
    tr = _row_tile(half, cols)
    n_t = half // tr

    def body(pos_ref, p_ref, l_ref, o_ref):
        o_ref[...] = ((p_ref[...] + l_ref[0].astype(F32)) + l_ref[1].astype(F32)) + l_ref[2].astype(F32)

    return pl.pallas_call(
        body, grid_spec=pltpu.PrefetchScalarGridSpec(
            num_scalar_prefetch=1, grid=(n_t,),
            in_specs=[pl.BlockSpec((tr, cols), lambda i, pos: (i, 0)), pl.BlockSpec((3, tr, cols), lambda i, pos: (0, i, 0))],
            out_specs=pl.BlockSpec((tr, cols), lambda i, pos: (pos[0] * n_t + i, 0))),
        out_shape=jax.ShapeDtypeStruct((2 * half, cols), F32), name=name, compiler_params=_cparams(1),
    )(pos, own, landed)


def reduce_scatter_layer(tag, pos, grads):
    n = lambda t: f"{t}_{tag}"
    from_sibling = swap_halves(n("swap_halves"), grads)
    sums = [pair_sum(n(f"pair_sum{w}"), pos, g, s) for w, (g, s) in enumerate(zip(grads, from_sibling))]
    landed = scatter_chips(n("scatter_chips"), [b for b, _ in sums])
    halves = [chip_sum(n(f"chip_sum{w}"), pos, own, l) for w, ((_, own), l) in enumerate(zip(sums, landed))]
    return share_halves(n("share_halves"), halves)


class OverlappedReduceScatter:
    def __init__(self, tag, pos, grads):
        self.n = lambda t: f"{t}_{tag}"
        self.pos, self.grads = pos, grads
        self.swap, self.token = exchange_start(self.n("swap_start"), "swap", grads)

    def middle(self, after):
        from_sibling = exchange_wait(self.n("swap_wait"), self.swap, after)
        self.sums = [pair_sum(self.n(f"pair_sum{w}"), self.pos, g, s) for w, (g, s) in enumerate(zip(self.grads, from_sibling))]
        self.scatter, self.token = exchange_start(self.n("scatter_start"), "scatter", [b for b, _ in self.sums])

    def finish(self, after):
        landed = exchange_wait(self.n("scatter_wait"), self.scatter, after)
        halves = [chip_sum(self.n(f"chip_sum{w}"), self.pos, own, l) for w, ((_, own), l) in enumerate(zip(self.sums, landed))]
        return share_halves(self.n("share_halves"), halves)


def sum_devices(gathered):
    m_per = gathered.shape[0] // 8

    def body(g_ref, o_ref):
        tot = g_ref[pl.ds(0, m_per), :]
        for dev in range(1, 8):
            tot = tot + g_ref[pl.ds(dev * m_per, m_per), :]
        o_ref[...] = tot

    return pl.pallas_call(
        body, out_shape=jax.ShapeDtypeStruct((m_per, gathered.shape[1]), F32),
        in_specs=[pl.BlockSpec(memory_space=pltpu.VMEM)], out_specs=pl.BlockSpec(memory_space=pltpu.VMEM), name="sum_devices",
    )(gathered)


def kernel(x, p, g_mix, w_in, b_fox_f, fox_q_gain, fox_k_gain, sc_conv_w, dn_conv_w, dn_a_log, dn_dt_bias, dn_norm_gain, w_branch, w_o, g_ffn, w_up, ffn_conv_w, w_down, g_ple, w_ple_gate, w_ple, loss_target, m_g_mix, m_w_in, m_b_fox_f, m_fox_q_gain, m_fox_k_gain, m_sc_conv_w, m_dn_conv_w, m_dn_a_log, m_dn_dt_bias, m_dn_norm_gain, m_w_branch, m_w_o, m_g_ffn, m_w_up, m_ffn_conv_w, m_w_down, m_g_ple, m_w_ple_gate, m_w_ple, v_g_mix, v_w_in, v_b_fox_f, v_fox_q_gain, v_fox_k_gain, v_sc_conv_w, v_dn_conv_w, v_dn_a_log, v_dn_dt_bias, v_dn_norm_gain, v_w_branch, v_w_o, v_g_ffn, v_w_up, v_ffn_conv_w, v_w_down, v_g_ple, v_w_ple_gate, v_w_ple):
    a = dict(g_mix=g_mix, w_in=w_in, b_fox_f=b_fox_f, fox_q_gain=fox_q_gain, fox_k_gain=fox_k_gain, sc_conv_w=sc_conv_w,
             dn_conv_w=dn_conv_w, dn_a_log=dn_a_log, dn_dt_bias=dn_dt_bias, dn_norm_gain=dn_norm_gain, w_branch=w_branch, w_o=w_o,
             g_ffn=g_ffn, w_up=w_up, ffn_conv_w=ffn_conv_w, w_down=w_down, g_ple=g_ple, w_ple_gate=w_ple_gate, w_ple=w_ple)
    mom = dict(g_mix=m_g_mix, w_in=m_w_in, b_fox_f=m_b_fox_f, fox_q_gain=m_fox_q_gain, fox_k_gain=m_fox_k_gain, sc_conv_w=m_sc_conv_w,
               dn_conv_w=m_dn_conv_w, dn_a_log=m_dn_a_log, dn_dt_bias=m_dn_dt_bias, dn_norm_gain=m_dn_norm_gain, w_branch=m_w_branch,
               w_o=m_w_o, g_ffn=m_g_ffn, w_up=m_w_up, ffn_conv_w=m_ffn_conv_w, w_down=m_w_down, g_ple=m_g_ple, w_ple_gate=m_w_ple_gate,
               w_ple=m_w_ple)
    var = dict(g_mix=v_g_mix, w_in=v_w_in, b_fox_f=v_b_fox_f, fox_q_gain=v_fox_q_gain, fox_k_gain=v_fox_k_gain, sc_conv_w=v_sc_conv_w,
               dn_conv_w=v_dn_conv_w, dn_a_log=v_dn_a_log, dn_dt_bias=v_dn_dt_bias, dn_norm_gain=v_dn_norm_gain, w_branch=v_w_branch,
               w_o=v_w_o, g_ffn=v_g_ffn, w_up=v_w_up, ffn_conv_w=v_ffn_conv_w, w_down=v_w_down, g_ple=v_g_ple, w_ple_gate=v_w_ple_gate,
               w_ple=v_w_ple)
    cx, cy, cc = lax.axis_index("x"), lax.axis_index("y"), lax.axis_index("c")
    chip = 2 * cx + cy
    pos = jnp.stack([cc, chip]).astype(jnp.int32)

    def as_blocks(t):
        return t.reshape(2, -1, t.shape[-1])

    def own_block_in(got, shards):
        return [lax.dynamic_update_slice(g, s[None], (chip, 0, 0)) for g, s in zip(got, shards)]

    conv_shapes = [a[nm].shape for nm in CONVS]
    conv_all, conv_token = gather_small("gather_conv_w", pack_rows([a[nm] for nm in CONVS], F32))
    shards0 = [(as_blocks(a[nm])[0] + conv_token[0, 0]).astype(BF16) for nm in BIG]
    got0, gathered_token = gather_layer("gather_w_in_l0", shards0[:1])
    shards0[1:] = [s + gathered_token[0, 0].astype(BF16) for s in shards0[1:]]
    gather0, gather0_token = exchange_start("gather_start_l0", "gather", shards0[1:])
    shards1 = [(as_blocks(a[nm])[1] + gather0_token[0, 0]).astype(BF16) for nm in BIG]
    gather1, gather1_token = exchange_start("gather_start_l1", "gather", shards1)
    conv_rows = conv_all.shape[0] // 8
    conv_chip = [unpack_rows(conv_all[2 * k * conv_rows:(2 * k + 1) * conv_rows], conv_shapes) for k in range(N_CHIPS)]
    conv = {nm: jnp.concatenate([conv_chip[k][i] for k in range(N_CHIPS)], axis=2) for i, nm in enumerate(CONVS)}

    weights, saved = [None, None], [None, None]
    first_weights = hang_on(layer_weights(0, own_block_in(got0, shards0[:1]), conv, a), gather1_token)

    def rest_of_layer0(after):
        return later_weights(own_block_in(exchange_wait("gather_wait_l0", gather0, after), shards0[1:]))

    act, saved[0], weights[0] = layer_fwd(0, x[0], p[0, 0], first_weights, more_weights=rest_of_layer0)
    got1 = exchange_wait("gather_wait_l1", gather1, act)
    act, saved[1], weights[1] = layer_fwd(1, act, p[1, 0], layer_weights(1, own_block_in(got1, shards1), conv, a))
    d_act, loss_part = loss_call(act, loss_target[0])
    loss = lax.psum(loss_part, ("x", "y", "c"))
    layer_grads = [None, None]
    d_act, layer_grads[1] = layer_bwd(1, d_act, saved[1], weights[1])
    rs1 = OverlappedReduceScatter("l1", pos, [layer_grads[1][nm] for nm in BIG])
    rs0 = []

    def stage_mid(after, g):
        rs1.middle(after)
        return rs1.token

    def stage_late(after, g):
        rs0.append(OverlappedReduceScatter("l0", pos, [g[nm] for nm in BIG[1:]]))
        return rs0[0].token

    def stage_last(after, g):
        rs0[0].middle(after)
        return rs0[0].token

    d_act, layer_grads[0] = layer_bwd(0, d_act, saved[0], hang_on(weights[0], rs1.token),
                                      hooks=dict(mid=stage_mid, late=stage_late, last=stage_last))
    reduced = [None, rs1.finish(d_act)]
    reduced[0] = reduce_scatter_layer("w_in_l0", pos, [layer_grads[0]["w_in"]]) + rs0[0].finish(d_act)
    grad_x = d_act[None]

    def both(nm):
        return jnp.stack([layer_grads[0][nm], layer_grads[1][nm]])

    local = {nm: both(nm) for nm in ("g_mix", "b_fox_f", "fox_q_gain", "fox_k_gain", "dn_norm_gain", "g_ffn", "g_ple", "sc_conv_w",
                                      "dn_conv_w", "ffn_conv_w")}
    local["dn_a_log"] = jnp.stack([layer_grads[li]["ad"][0] for li in range(2)])
    local["dn_dt_bias"] = jnp.stack([layer_grads[li]["ad"][1] for li in range(2)])

    small_names = SMALL + CONVS
    small_shapes = [local[nm].shape for nm in small_names]
    small_sum = sum_devices(gather_small("gather_small_grads", pack_rows([local[nm] for nm in small_names], F32))[0])
    small_grads = dict(zip(small_names, unpack_rows(small_sum, small_shapes)))
    for nm in CONVS:
        width = a[nm].shape[2]
        small_grads[nm] = lax.dynamic_slice_in_dim(small_grads[nm], chip * width, width, axis=2)

    grads, deltas, new_m, new_v = dict(small_grads), {}, {}, {}
    for nm in small_names:
        deltas[nm], new_m[nm], new_v[nm] = adam_call(f"adam_{nm}", a[nm], grads[nm], mom[nm], var[nm])
    for i, nm in enumerate(BIG):
        res = adam_layers(f"adam_{nm}", as_blocks(a[nm]), as_blocks(mom[nm]), as_blocks(var[nm]), reduced[0][i], reduced[1][i])
        grads[nm], deltas[nm], new_m[nm], new_v[nm] = [r.reshape(a[nm].shape) for r in res]
    return (loss, grad_x, *[grads[nm] for nm in WEIGHTS], *[deltas[nm] for nm in WEIGHTS], *[new_m[nm] for nm in WEIGHTS],
            *[new_v[nm] for nm in WEIGHTS])
```

```python
import functools

import jax
import jax.numpy as jnp
from jax import lax
from jax.experimental import pallas as pl
from jax.experimental.pallas import tpu as pltpu

F32 = jnp.float32
BF16 = jnp.bfloat16
HI = lax.Precision.HIGHEST
MESH = pl.DeviceIdType.MESH

D_MODEL = 1024
BRANCH = 512
FOX_DH = 64
DN_DH = 128
DN_HEADS = 4
DN_CHUNK = 64
FOX_BLOCK = 128
D_FF = 2816
EPS = 1e-6
N_CHIPS = 4
LANES = 128

ADAM_LR, ADAM_B1, ADAM_B2, ADAM_EPS, ADAM_WD, ADAM_STEP = 0.001, 0.9, 0.999, 1e-08, 0.01, 10

VMEM_LIMIT = 56 * 1024 * 1024

C_FQ, C_FK, C_FV, C_SB, C_SC, C_SV, C_DN, C_DZ, C_GATE = 0, 512, 1024, 1536, 2048, 2560, 3072, 4608, 5120
IN_MAIN = 8192
IN_SIZES = (1536, 8, 1536, 1536, 4, 4, 512, 3072)

BIG = ("w_in", "w_branch", "w_o", "w_up", "w_down", "w_ple_gate", "w_ple")
BIG_AXIS = {"w_in": 2, "w_branch": 3, "w_o": 1, "w_up": 2, "w_down": 1, "w_ple_gate": 1, "w_ple": 2}
CONVS = ("sc_conv_w", "dn_conv_w", "ffn_conv_w")
SMALL = ("g_mix", "b_fox_f", "fox_q_gain", "fox_k_gain", "dn_a_log", "dn_dt_bias", "dn_norm_gain", "g_ffn", "g_ple")
WEIGHTS = ("g_mix", "w_in", "b_fox_f", "fox_q_gain", "fox_k_gain", "sc_conv_w", "dn_conv_w", "dn_a_log", "dn_dt_bias",
           "dn_norm_gain", "w_branch", "w_o", "g_ffn", "w_up", "ffn_conv_w", "w_down", "g_ple", "w_ple_gate", "w_ple")


def _iota(shape, dim):
    return lax.broadcasted_iota(jnp.int32, shape, dim)


def _dg(a, b, mode, prec=None):
    dims = {"nn": ((1,), (0,)), "nt": ((1,), (1,)), "tn": ((0,), (0,))}[mode]
    return lax.dot_general(a, b, (dims, ((), ())), precision=prec, preferred_element_type=F32)


def _bdot_impl(a, b, mode):
    return _dg(a.astype(BF16), b.astype(BF16), mode)


@functools.partial(jax.custom_vjp, nondiff_argnums=(2,))
def _bdot_diff(a, b, mode):
    return _bdot_impl(a, b, mode)


def _bdot_fwd(a, b, mode):
    return _bdot_impl(a, b, mode), (a, b)


def _bdot_bwd(mode, res, g):
    a, b = res
    if mode == "nn":
        da, db = _bdot_impl(g, b, "nt"), _bdot_impl(a, g, "tn")
    elif mode == "nt":
        da, db = _bdot_impl(g, b, "nn"), _bdot_impl(g, a, "tn")
    else:
        da, db = _bdot_impl(b, g, "nt"), _bdot_impl(a, g, "nn")
    return da.astype(a.dtype), db.astype(b.dtype)


_bdot_diff.defvjp(_bdot_fwd, _bdot_bwd)


def _bdot(d):
    return _bdot_diff if d else _bdot_impl


def _shift_impl(x, k):
    return jnp.where(_iota(x.shape, 0) >= k, pltpu.roll(x, k, 0), 0.0)


def _unshift_impl(g, k):
    n = g.shape[0]
    return jnp.where(_iota(g.shape, 0) < n - k, pltpu.roll(g, n - k, 0), 0.0)


@functools.partial(jax.custom_vjp, nondiff_argnums=(1,))
def _shift_diff(x, k):
    return _shift_impl(x, k)


_shift_diff.defvjp(lambda x, k: (_shift_impl(x, k), None), lambda k, _, g: (_unshift_impl(g, k),))


def _row(w, j):
    return jnp.sum(jnp.where(_iota(w.shape, 0) == j, w, 0.0), axis=0, keepdims=True)


def _col(w, j):
    return jnp.sum(jnp.where(_iota(w.shape, 1) == j, w, 0.0), axis=1, keepdims=True)


def _conv(d, x, w):
    shift = _shift_diff if d else _shift_impl
    taps = w.shape[0]
    y = x * _row(w, taps - 1)
    for j in range(taps - 1):
        y = y + shift(x, taps - 1 - j) * _row(w, j)
    return y


def _softplus(x):
    return jnp.maximum(x, 0.0) + jnp.log(1.0 + jnp.exp(-jnp.abs(x)))


def _silu(x):
    return x * jax.nn.sigmoid(x)


def _rms(x, gain):
    return x * lax.rsqrt(jnp.mean(x * x, axis=-1, keepdims=True) + EPS) * gain


def _rms_fn(d, pids, x, gain):
    return (_rms(x, gain),)


def _loss_fn(d, pids, y, t):
    e = y - t
    part = 0.5 / D_MODEL * jnp.sum(e * e, keepdims=True)
    return e * (1.0 / D_MODEL), jnp.broadcast_to(part, (8, LANES))


def _fox_prep_fn(d, pids, q, k, gq, gk):
    first = _iota(q.shape, 1) < FOX_DH

    def norm(x, gain):
        sq = x * x
        ss_a = jnp.sum(jnp.where(first, sq, 0.0), axis=1, keepdims=True)
        ss_b = jnp.sum(jnp.where(first, 0.0, sq), axis=1, keepdims=True)
        rs = jnp.where(first, lax.rsqrt(ss_a / FOX_DH + EPS), lax.rsqrt(ss_b / FOX_DH + EPS))
        return x * rs * gain

    return norm(q, gq) * FOX_DH ** -0.5, norm(k, gk)


def _fox_gate_fn(d, pids, f, bias):
    logf = -_softplus(-(f + bias))
    n_r, n_c = logf.shape
    tri = (_iota((n_c, n_c), 0) <= _iota((n_c, n_c), 1)).astype(F32)
    within = _dg(logf, tri, "nn", HI)
    tot = jnp.broadcast_to(jnp.sum(logf, axis=1, keepdims=True), logf.shape)
    below = (_iota((n_r, n_r), 1) < _iota((n_r, n_r), 0)).astype(F32)
    return (within + _dg(below, tot, "nn", HI),)


def _fox_attn_fn(q_block0, d, pids, q, k, v, cq_a, cq_b, ck_a, ck_b):
    dot = _bdot(d)
    first = _iota(q.shape, 1) < FOX_DH
    n_q, n_k = q.shape[0], k.shape[0]
    causal = ((q_block0 + pids[1]) * n_q + _iota((n_q, n_k), 0)) >= _iota((n_q, n_k), 1)

    qs = [jnp.where(first, q, 0.0), jnp.where(first, 0.0, q)]
    s = _each(lambda qh, cq, ck: jnp.where(causal, dot(qh, k, "nt") + cq - ck, -1e30), qs, [cq_a, cq_b], [ck_a, ck_b])
    e = [jnp.exp(si - lax.stop_gradient(jnp.max(si, axis=1, keepdims=True))) for si in s]
    o_a, o_b = [dot(ei / jnp.sum(ei, axis=1, keepdims=True), v, "nn") for ei in e]
    return (jnp.where(first, o_a, o_b),)


def _sconv_fn(d, pids, sb, sc, sv, w):
    return (sb * _conv(d, sc * sv, w),)


def _dnconv_fn(d, pids, x, w):
    return (_silu(_conv(d, x, w)),)


def _merge_fn(d, pids, y0, y1, y2, g0, g1, g2):
    return (jax.nn.sigmoid(g0) * y0 + jax.nn.sigmoid(g1) * y1 + jax.nn.sigmoid(g2) * y2,)


def _ffn_act_fn(d, pids, ug, uv, wg, wv):
    return (_silu(_conv(d, ug, wg)) * _conv(d, uv, wv),)


def _ple_fn(d, pids, gpre, pe, x):
    return (x + jax.nn.sigmoid(gpre) * pe,)


def _adam_fn(d, pids, w, g, m, v):
    m2 = ADAM_B1 * m + (1.0 - ADAM_B1) * g
    v2 = ADAM_B2 * v + (1.0 - ADAM_B2) * (g * g)
    m_hat = m2 / (1.0 - ADAM_B1 ** ADAM_STEP)
    v_hat = v2 / (1.0 - ADAM_B2 ** ADAM_STEP)
    delta = -ADAM_LR * (m_hat / (jnp.sqrt(v_hat) + ADAM_EPS) + ADAM_WD * w)
    return delta, m2, v2


def _each(fn, *lists):
    return [fn(*args) for args in zip(*lists)]


def _tri_inv_impl(mats):
    n = mats[0].shape[0]
    r, c = _iota((n, n), 0), _iota((n, n), 1)
    diag_blk = (r >> 4) == (c >> 4)
    eye = (r == c).astype(F32)
    mm = lambda us, ws: _each(lambda u, w: _dg(u, w, "nn", HI), us, ws)
    grow = lambda ps, xs: _each(lambda p, px: p + px, ps, mm(ps, xs))
    x = [jnp.where(diag_blk, -a, 0.0) for a in mats]
    p = [eye + xi for xi in x]
    x2 = mm(x, x)
    p = grow(p, x2)
    x4 = mm(x2, x2)
    p = grow(p, x4)
    p = grow(p, mm(x4, x4))
    y = [-yi for yi in mm(p, [jnp.where(diag_blk, 0.0, a) for a in mats])]
    q = grow([eye + yi for yi in y], mm(y, y))
    return mm(q, p)


@jax.custom_vjp
def _tri_inv_diff(mats):
    return _tri_inv_impl(mats)


def _tri_inv_fwd(mats):
    ts = _tri_inv_impl(mats)
    return ts, ts


def _tri_inv_bwd(ts, gs):
    left = _each(lambda t, g: _dg(t, g, "tn", HI), ts, gs)
    return ([-m for m in _each(lambda l, t: _dg(l, t, "nt", HI), left, ts)],)


_tri_inv_diff.defvjp(_tri_inv_fwd, _tri_inv_bwd)


def _dn_local(d, qs, ks, vs, a_cs, a_rs, b_cs, a_logs, dt_bs):
    dot = _bdot(d)
    inv = _tri_inv_diff if d else _tri_inv_impl
    n = qs[0].shape[0]
    r, c = _iota((n, n), 0), _iota((n, n), 1)
    incl, strict, upper = r >= c, r > c, r <= c
    qs = [q * lax.rsqrt(jnp.sum(q * q, axis=1, keepdims=True) + EPS) * DN_DH ** -0.5 for q in qs]
    ks = [k * lax.rsqrt(jnp.sum(k * k, axis=1, keepdims=True) + EPS) for k in ks]
    betas = [jax.nn.sigmoid(b) for b in b_cs]
    rates = [-jnp.exp(a) for a in a_logs]
    g_cs = _each(lambda rate, a, dt: rate * _softplus(a + dt), rates, a_cs, dt_bs)
    g_rs = _each(lambda rate, a, dt: rate * _softplus(a + dt), rates, a_rs, dt_bs)
    gcum_cs = [jnp.sum(jnp.where(incl, g, 0.0), axis=1, keepdims=True) for g in g_rs]
    gcum_rs = [jnp.sum(jnp.where(upper, g, 0.0), axis=0, keepdims=True) for g in g_cs]
    decays = _each(lambda gc, gr: jnp.exp(jnp.where(incl, gc - gr, -1e30)), gcum_cs, gcum_rs)
    kbs = _each(lambda k, b: k * b, ks, betas)
    kk = _each(lambda kb, k: dot(kb, k, "nt"), kbs, ks)
    ts = inv(_each(lambda m, dec: jnp.where(strict, m * dec, 0.0), kk, decays))
    e_gs = [jnp.exp(g) for g in gcum_cs]
    us = _each(lambda t, v, b: _dg(t, v * b, "nn", HI), ts, vs, betas)
    k_cums = _each(lambda t, kb, e: _dg(t, kb * e, "nn", HI), ts, kbs, e_gs)
    qk = _each(lambda q, k: dot(q, k, "nt"), qs, ks)
    qk = _each(lambda m, dec: jnp.where(incl, m * dec, 0.0), qk, decays)
    g_lasts = [jnp.sum(g, axis=0, keepdims=True) for g in g_cs]
    q_decs = _each(lambda q, e: q * e, qs, e_gs)
    k_decs = _each(lambda k, gl, gc: k * jnp.exp(gl - gc), ks, g_lasts, gcum_cs)
    return list(zip(us, k_cums, q_decs, k_decs, qk, g_lasts))


def _dn_step(d, s_prevs, items, zs, gain):
    dot = _bdot(d)
    us, k_cums, q_decs, k_decs, qks, g_lasts = [list(t) for t in zip(*items)]
    v_news = _each(lambda u, kc, s: u - dot(kc, s, "nn"), us, k_cums, s_prevs)
    inter = _each(lambda qd, s: dot(qd, s, "nn"), q_decs, s_prevs)
    outs = _each(lambda o, qk, vn: o + dot(qk, vn, "nn"), inter, qks, v_news)
    s_nexts = _each(lambda s, gl, kd, vn: s * jnp.exp(gl) + dot(kd, vn, "tn"), s_prevs, g_lasts, k_decs, v_news)
    return _each(lambda o, z: _rms(o, gain) * _silu(z), outs, zs), s_nexts


def _split_heads(t):
    return [t[:, h * DN_DH:(h + 1) * DN_DH] for h in range(t.shape[1] // DN_DH)]


def _dn_gates(ps, a_rows, ad):
    hs = range(DN_HEADS)
    return ([_col(ps, 12 + h) for h in hs], [_row(a_rows, h) for h in hs], [_col(ps, 8 + h) for h in hs],
            [_col(_row(ad, 0), h) for h in hs], [_col(_row(ad, 1), h) for h in hs])


def _head_rows(vals):
    row = _iota((8, LANES), 0)
    tile = jnp.zeros((8, LANES), F32)
    for h, val in enumerate(vals):
        tile = tile + jnp.where(row == h, val, 0.0)
    return tile


def _cparams(n_axes):
    return pltpu.CompilerParams(dimension_semantics=("arbitrary",) * n_axes, vmem_limit_bytes=VMEM_LIMIT)


def _first_visit(acc_axes):
    cond = None
    for a in acc_axes:
        here = pl.program_id(a) == 0
        cond = here if cond is None else jnp.logical_and(cond, here)
    return cond


def _tile(ref, widen=False):
    val = ref[...]
    shape = val.shape
    while len(shape) > 2 and shape[0] == 1:
        shape = shape[1:]
    val = val.reshape(shape)
    return val.astype(F32) if widen and val.dtype == BF16 else val


def _store(ref, val, first):
    val = val.astype(ref.dtype).reshape(ref.shape)
    if first is None:
        ref[...] = val
        return

    @pl.when(first)
    def _():
        ref[...] = val

    @pl.when(jnp.logical_not(first))
    def _():
        ref[...] += val


def _specs(ops):
    return [pl.BlockSpec(block, imap) for _, block, imap in ops]


def tile_fwd(name, fn, grid, ins, outs, raw=()):
    n_in = len(ins)

    def body(*refs):
        pids = tuple(pl.program_id(a) for a in range(len(grid)))
        firsts = [_first_visit(o[4]) if o[4] else None for o in outs]
        res = fn(False, pids, *[_tile(r, i not in raw) for i, r in enumerate(refs[:n_in])])
        for ref, val, first in zip(refs[n_in:], res, firsts):
            _store(ref, val, first)

    out = pl.pallas_call(
        body, grid=grid, in_specs=_specs(ins),
        out_specs=[pl.BlockSpec(o[2], o[3]) for o in outs],
        out_shape=[jax.ShapeDtypeStruct(o[0], o[1]) for o in outs],
        name=name, compiler_params=_cparams(len(grid)),
    )(*[a for a, _, _ in ins])
    return out


def tile_bwd(name, fn, grid, ins, cots, diff, adds=None, raw=()):
    adds = adds or {}
    n_in, n_cot = len(ins), len(cots)
    add_pos = sorted(adds)
    diff_idx = [d[0] for d in diff]
    out_desc = [d[2] if len(d) > 2 and d[2] is not None else (ins[d[0]][0].shape, ins[d[0]][1], ins[d[0]][2]) for d in diff]
    out_dtypes = [d[3] if len(d) > 3 else F32 for d in diff]

    def body(*refs):
        pids = tuple(pl.program_id(a) for a in range(len(grid)))
        firsts = [_first_visit(d[1]) if d[1] else None for d in diff]
        vals = [_tile(r, i not in raw) for i, r in enumerate(refs[:n_in])]
        cot_vals = [_tile(r, True) for r in refs[n_in:n_in + n_cot]]
        add_vals = [_tile(r) for r in refs[n_in + n_cot:n_in + n_cot + len(add_pos)]]
        out_refs = refs[n_in + n_cot + len(add_pos):]

        def f(*dv):
            full = list(vals)
            for i, val in zip(diff_idx, dv):
                full[i] = val
            return fn(True, pids, *full)

        prim, vjp = jax.vjp(f, *[vals[i].astype(F32) for i in diff_idx])
        grads = list(vjp(tuple(c.astype(o.dtype) for c, o in zip(cot_vals, prim))))
        for pos, val in zip(add_pos, add_vals):
            grads[pos] = grads[pos] + val.astype(F32)
        for ref, val, first in zip(out_refs, grads, firsts):
            _store(ref, val, first)

    all_ins = list(ins) + list(cots) + [adds[p] for p in add_pos]
    out = pl.pallas_call(
        body, grid=grid, in_specs=_specs(all_ins),
        out_specs=[pl.BlockSpec(o[1], o[2]) for o in out_desc],
        out_shape=[jax.ShapeDtypeStruct(o[0], dt) for o, dt in zip(out_desc, out_dtypes)],
        name=name, compiler_params=_cparams(len(grid)),
    )(*[a for a, _, _ in all_ins])
    return out


def _pick(dim, cands):
    for c in cands:
        if dim % c == 0:
            return c
    return dim


MM_TILES = (1024, 512, 1408, 256, 128)


def mm(name, a, b, mode, add=None, out_dtype=F32, blocks=None):
    wide = None
    if mode == "nn":
        (m, kk), n = a.shape, b.shape[-1]
    elif mode == "nt":
        (m, kk), n = a.shape, b.shape[-2]
    else:
        (kk, m), n = a.shape, b.shape[1]
    if blocks is not None:
        lo, n_blk = blocks
        wide = b.shape[-1] if mode != "tn" else n // n_blk
        if mode == "nn":
            n = wide * n_blk
    tm = _pick(m, MM_TILES)
    if mode == "nt" and blocks is not None:
        tn, tk = _pick(n, MM_TILES), _pick(wide, MM_TILES[:-1])
    elif blocks is not None:
        tn, tk = _pick(wide, MM_TILES[:-1]), _pick(kk, MM_TILES)
    else:
        tn, tk = _pick(n, MM_TILES), _pick(kk, MM_TILES)
    nk = kk // tk
    a_spec = pl.BlockSpec((tk, tm), lambda i, j, k: (k, i)) if mode == "tn" else pl.BlockSpec((tm, tk), lambda i, j, k: (i, k))
    o_spec = pl.BlockSpec((tm, tn), lambda i, j, k: (i, j))
    out_shape = (m, n)
    if blocks is None:
        b_spec = pl.BlockSpec((tn, tk), lambda i, j, k: (j, k)) if mode == "nt" else pl.BlockSpec((tk, tn), lambda i, j, k: (k, j))
    elif mode == "nn":
        per = wide // tn
        b_spec = pl.BlockSpec((1, tk, tn), lambda i, j, k: (lo + j // per, k, j % per))
    elif mode == "nt":
        per = wide // tk
        b_spec = pl.BlockSpec((1, tn, tk), lambda i, j, k: (lo + k // per, j, k % per))
    else:
        per = wide // tn
        b_spec = pl.BlockSpec((tk, tn), lambda i, j, k: (k, j))
        o_spec = pl.BlockSpec((1, tm, tn), lambda i, j, k: (j // per, i, j % per))
        out_shape = (n_blk, m, wide)

    def body(*refs):
        a_ref, b_ref = refs[0], refs[1]
        add_ref = refs[2] if add is not None else None
        o_ref, acc = refs[-2], refs[-1]
        k = pl.program_id(2)
        part = _bdot_impl(_tile(a_ref), _tile(b_ref), mode)

        @pl.when(k == 0)
        def _():
            acc[...] = part

        @pl.when(k > 0)
        def _():
            acc[...] += part

        @pl.when(k == nk - 1)
        def _():
            res = acc[...]
            if add_ref is not None:
                res = res + add_ref[...]
            o_ref[...] = res.astype(o_ref.dtype).reshape(o_ref.shape)

    operands = [a, b] + ([add] if add is not None else [])
    in_specs = [a_spec, b_spec] + ([o_spec] if add is not None else [])
    return pl.pallas_call(
        body, grid=(m // tm, n // tn, nk), in_specs=in_specs, out_specs=o_spec,
        out_shape=jax.ShapeDtypeStruct(out_shape, out_dtype),
        scratch_shapes=[pltpu.VMEM((tm, tn), F32)],
        name=name, compiler_params=_cparams(3),
    )(*operands)


def _rows(x, width=None, off=0, tm=256):
    width = x.shape[1] if width is None else width
    return (x, (tm, width), lambda i, off=off: (i, off))


def _whole(x):
    nd = x.ndim
    return (x, x.shape, lambda *pids, nd=nd: (0,) * nd)


def _rms_ops(x, gain):
    return [_rows(x), _whole(gain)]


def rms_fwd(name, x, gain):
    s, dm = x.shape
    return tile_fwd(name, _rms_fn, (s // 256,), _rms_ops(x, gain), [((s, dm), BF16, (256, dm), lambda i: (i, 0), ())])[0]


def rms_bwd(name, x, gain, dh, dres):
    s = x.shape[0]
    return tile_bwd(name, _rms_fn, (s // 256,), _rms_ops(x, gain), [_rows(dh)], [(0, ()), (1, (0,))], adds={0: _rows(dres)})


def loss_call(y, t):
    s, dm = y.shape
    dy, part = tile_fwd("loss", _loss_fn, (s // 256,), [_rows(y), _rows(t)],
                        [((s, dm), F32, (256, dm), lambda i: (i, 0), ()), ((8, LANES), F32, (8, LANES), lambda i: (0, 0), (0,))])
    return dy, part[0, 0]


def _fox_prep_ops(pm, gq, gk):
    tm = 512
    return [(pm, (tm, LANES), lambda i, j: (i, C_FQ // LANES + j)), (pm, (tm, LANES), lambda i, j: (i, C_FK // LANES + j)),
            _whole(gq), _whole(gk)]


def fox_prep_fwd(name, pm, gq, gk):
    s = pm.shape[0]
    out = ((s, BRANCH), BF16, (512, LANES), lambda i, j: (i, j), ())
    return tile_fwd(name, _fox_prep_fn, (s // 512, 4), _fox_prep_ops(pm, gq, gk), [out, out])


def fox_prep_bwd(name, pm, gq, gk, dqn, dkn):
    s = pm.shape[0]
    cot = lambda g: (g, (512, LANES), lambda i, j: (i, j))
    own = ((s, BRANCH), (512, LANES), lambda i, j: (i, j))
    return tile_bwd(name, _fox_prep_fn, (s // 512, 4), _fox_prep_ops(pm, gq, gk), [cot(dqn), cot(dkn)],
                    [(0, (), own, BF16), (1, (), own, BF16), (2, (0, 1)), (3, (0, 1))])


def _fox_gate_ops(f_t, bias):
    return [(f_t, (1,) + f_t.shape[1:], lambda h: (h, 0, 0)), (bias, (1, 1, 1), lambda h: (h, 0, 0))]


def fox_gate_fwd(name, f_t, bias):
    n_h = f_t.shape[0]
    return tile_fwd(name, _fox_gate_fn, (n_h,), _fox_gate_ops(f_t, bias),
                    [(f_t.shape, F32, (1,) + f_t.shape[1:], lambda h: (h, 0, 0), ())])[0]


def fox_gate_bwd(name, f_t, bias, dcum):
    n_h = f_t.shape[0]
    return tile_bwd(name, _fox_gate_fn, (n_h,), _fox_gate_ops(f_t, bias),
                    [(dcum, (1,) + f_t.shape[1:], lambda h: (h, 0, 0))], [(0, ()), (1, ())])


FOX_GROUPS = 4


def _fox_groups(s):
    per = s // FOX_BLOCK // FOX_GROUPS
    return [(g * per, per, (g + 1) * per * FOX_BLOCK) for g in range(FOX_GROUPS)]


def _fox_attn_ops(qn, kn, pm, cum_c, cum_r, q0, keys):
    nb = FOX_BLOCK
    return [(qn, (nb, LANES), lambda p, i: (q0 + i, p)), (kn, (keys, LANES), lambda p, i: (0, p)),
            (pm, (keys, LANES), lambda p, i: (0, C_FV // LANES + p)),
            (cum_c, (1, nb, 1), lambda p, i: (2 * p, q0 + i, 0)), (cum_c, (1, nb, 1), lambda p, i: (2 * p + 1, q0 + i, 0)),
            (cum_r, (1, 1, keys), lambda p, i: (2 * p, 0, 0)), (cum_r, (1, 1, keys), lambda p, i: (2 * p + 1, 0, 0))]


def fox_attn_fwd(name, qn, kn, pm, cum_c, cum_r):
    s = qn.shape[0]
    parts = []
    for g, (q0, n_q, keys) in enumerate(_fox_groups(s)):
        parts.append(tile_fwd(f"{name}_g{g}", functools.partial(_fox_attn_fn, q0), (4, n_q), _fox_attn_ops(qn, kn, pm, cum_c, cum_r, q0, keys),
                              [((n_q * FOX_BLOCK, BRANCH), BF16, (FOX_BLOCK, LANES), lambda p, i: (i, p), ())], raw=(0, 1, 2))[0])
    return jnp.concatenate(parts, axis=0)


def fox_attn_bwd(name, qn, kn, pm, cum_c, cum_r, dy):
    s = qn.shape[0]
    d_qn, d_kn, d_v, d_cum = [], 0.0, 0.0, 0.0
    for g, (q0, n_q, keys) in enumerate(_fox_groups(s)):
        rows = n_q * FOX_BLOCK
        own_q = ((rows, BRANCH), (FOX_BLOCK, LANES), lambda p, i: (i, p))
        own_k = ((keys, BRANCH), (keys, LANES), lambda p, i: (0, p))
        pair_c = ((4, rows, 1), (1, FOX_BLOCK, 1), lambda p, i: (p, i, 0))
        pair_r = ((4, 1, keys), (1, 1, keys), lambda p, i: (p, 0, 0))
        g_qn, g_kn, g_v, g_cqa, g_cqb, g_cka, g_ckb = tile_bwd(
            f"{name}_g{g}", functools.partial(_fox_attn_fn, q0), (4, n_q), _fox_attn_ops(qn, kn, pm, cum_c, cum_r, q0, keys),
            [(dy, (FOX_BLOCK, LANES), lambda p, i, q0=q0: (q0 + i, p))],
            [(0, (), own_q), (1, (1,), own_k), (2, (1,), own_k), (3, (), pair_c), (4, (), pair_c), (5, (1,), pair_r), (6, (1,), pair_r)])
        d_qn.append(g_qn)
        tail = lambda t, axis: jnp.pad(t, [(0, s - keys) if ax == axis else (0, 0) for ax in range(t.ndim)])
        d_kn, d_v = d_kn + tail(g_kn, 0), d_v + tail(g_v, 0)
        by_q = jnp.stack([g_cqa[:, :, 0], g_cqb[:, :, 0]], axis=1).reshape(8, rows)
        by_k = jnp.stack([g_cka[:, 0, :], g_ckb[:, 0, :]], axis=1).reshape(8, keys)
        d_cum = d_cum + jnp.pad(by_q, [(0, 0), (q0 * FOX_BLOCK, s - q0 * FOX_BLOCK - rows)]) + tail(by_k, 1)
    return jnp.concatenate(d_qn, axis=0), d_kn, d_v, d_cum


def sconv_ops(pm, w):
    s = pm.shape[0]
    blk = lambda c0: (pm, (s, LANES), lambda j, c0=c0: (0, c0 // LANES + j))
    return [blk(C_SB), blk(C_SC), blk(C_SV), (w, (w.shape[0], LANES), lambda j: (0, j))]


def dnconv_ops(pm, w):
    s = pm.shape[0]
    return [(pm, (s, LANES), lambda j: (0, C_DN // LANES + j)), (w, (w.shape[0], LANES), lambda j: (0, j))]


def ffn_ops(ug, uv, w):
    s = ug.shape[0]
    n_t = D_FF // LANES
    return [(ug, (s, LANES), lambda j: (0, j)), (uv, (s, LANES), lambda j: (0, j)),
            (w, (w.shape[0], LANES), lambda j: (0, j)), (w, (w.shape[0], LANES), lambda j: (0, n_t + j))]


def _col_out(s, width, dtype=F32):
    return ((s, width), dtype, (s, LANES), lambda j: (0, j), ())


def _col_cot(g):
    return (g, (g.shape[0], LANES), lambda j: (0, j))


def merge_ops(yp, pm):
    gate = lambda b: (pm, (256, D_MODEL), lambda i, b=b: (i, C_GATE // D_MODEL + b))
    return [_rows(yp[0]), _rows(yp[1]), _rows(yp[2]), gate(0), gate(1), gate(2)]


def ple_ops(gpre, pe, x):
    return [_rows(gpre), _rows(pe), _rows(x)]


def adam_call(name, w, g, m, v):
    shape = w.shape
    last = shape[-1]
    rows = w.size // last
    flat = lambda t: t.reshape(rows, last)
    tm = rows
    for cand in (512, 256, 128, 64, 32, 16, 8):
        if rows % cand == 0 and cand * last * 4 <= 2 * 1024 * 1024:
            tm = cand
            break
    spec = lambda t: (flat(t), (tm, last), lambda i: (i, 0))
    out = ((rows, last), F32, (tm, last), lambda i: (i, 0), ())
    res = tile_fwd(name, _adam_fn, (rows // tm,), [spec(w), spec(g), spec(m), spec(v)], [out, out, out])
    return [r.reshape(shape) for r in res]


def _adam_layers_fn(d, pids, w, m, v, g0, g1):
    g = jnp.where(pids[0] == 0, g0, g1)
    return (g,) + _adam_fn(d, pids, w, g, m, v)


def adam_layers(name, w, m, v, g0, g1):
    _, rows, cols = w.shape
    tm = _row_tile(rows, cols)
    n_t = rows // tm
    lay = lambda t: (t, (1, tm, cols), lambda l, i: (l, i, 0))
    ins = [lay(w), lay(m), lay(v), (g0, (tm, cols), lambda l, i: (i * (1 - l) + (n_t - 1) * l, 0)), (g1, (tm, cols), lambda l, i: (i * l, 0))]
    out = (w.shape, F32, (1, tm, cols), lambda l, i: (l, i, 0), ())
    return tile_fwd(name, _adam_layers_fn, (2, n_t), ins, [out, out, out, out])


def adam_w_in(name, w, m, v, g0, g1):
    rows, n_l, cols = w.shape

    def body(w_ref, m_ref, v_ref, g0_ref, g1_ref, g_out, d_out, m_out, v_out):
        step = 64

        def update(at):
            for l, g_ref in enumerate((g0_ref, g1_ref)):
                g = g_ref[at, :]
                delta, m2, v2 = _adam_fn(False, None, w_ref[at, l, :], g, m_ref[at, l, :], v_ref[at, l, :])
                for ref, val in ((g_out, g), (d_out, delta), (m_out, m2), (v_out, v2)):
                    ref[at, l, :] = val

        def some_rows(i, carry):
            update(pl.ds(pl.multiple_of(i * step, step), step))
            return carry

        lax.fori_loop(0, rows // step, some_rows, 0)
        if rows % step:
            update(pl.ds(rows - rows % step, rows % step))

    both = pl.BlockSpec((rows, n_l, LANES), lambda j: (0, 0, j))
    one = pl.BlockSpec((rows, LANES), lambda j: (0, j))
    return pl.pallas_call(
        body, grid=(cols // LANES,), in_specs=[both, both, both, one, one], out_specs=[both] * 4,
        out_shape=[jax.ShapeDtypeStruct(w.shape, F32)] * 4, name=name, compiler_params=_cparams(1),
    )(w, m, v, g0, g1)


DN_GROUP = 4


def _dn_local_specs(rev_n=None):
    rows = DN_GROUP * DN_CHUNK
    idx = (lambda j: j) if rev_n is None else (lambda j: rev_n - 1 - j)
    return [pl.BlockSpec((rows, 3 * BRANCH), lambda j: (idx(j), 0)), pl.BlockSpec((rows, LANES), lambda j: (idx(j), 0)),
            pl.BlockSpec((DN_GROUP, DN_HEADS, DN_CHUNK), lambda j: (idx(j), 0, 0)), pl.BlockSpec((2, DN_HEADS), lambda j: (0, 0))]


def _dn_group_inputs(qkv, ps, a_rows, c):
    lo = c * DN_CHUNK
    heads = _split_heads(qkv[lo:lo + DN_CHUNK])
    return heads[0:4], heads[4:8], heads[8:12], ps[lo:lo + DN_CHUNK], a_rows[c]


def dn_local_fwd(name, dn_act, ps, a_rows, ad):
    s = dn_act.shape[0]
    n_c, n_g = s // DN_CHUNK, s // (DN_GROUP * DN_CHUNK)
    rows = DN_GROUP * DN_CHUNK

    def body(qkv_ref, ps_ref, ar_ref, ad_ref, u_ref, kc_ref, qd_ref, kd_ref, qk_ref, gl_ref):
        qkv, ps_v, a_rows_v, ad_v = qkv_ref[...], ps_ref[...], ar_ref[...], ad_ref[...]
        args = [[] for _ in range(8)]
        for c in range(DN_GROUP):
            q4, k4, v4, ps_c, ar_c = _dn_group_inputs(qkv, ps_v, a_rows_v, c)
            for lst, vals in zip(args, (q4, k4, v4) + _dn_gates(ps_c, ar_c, ad_v)):
                lst.extend(vals)
        everything = _dn_local(False, *args)
        for c in range(DN_GROUP):
            res = everything[c * DN_HEADS:(c + 1) * DN_HEADS]
            at = pl.ds(c * DN_CHUNK, DN_CHUNK)
            for ref, i in ((u_ref, 0), (kc_ref, 1), (qd_ref, 2), (kd_ref, 3)):
                ref[at, :] = jnp.concatenate([r[i] for r in res], axis=1)
            for h in range(DN_HEADS):
                qk_ref[c, h] = res[h][4]
            gl_ref[c] = _head_rows([r[5] for r in res])

    wide = pl.BlockSpec((rows, BRANCH), lambda j: (j, 0))
    return pl.pallas_call(
        body, grid=(n_g,), in_specs=_dn_local_specs(),
        out_specs=[wide, wide, wide, wide, pl.BlockSpec((DN_GROUP, DN_HEADS, DN_CHUNK, DN_CHUNK), lambda j: (j, 0, 0, 0)),
                   pl.BlockSpec((DN_GROUP, 8, LANES), lambda j: (j, 0, 0))],
        out_shape=[jax.ShapeDtypeStruct((s, BRANCH), F32)] * 4 + [jax.ShapeDtypeStruct((n_c, DN_HEADS, DN_CHUNK, DN_CHUNK), F32),
                                                                 jax.ShapeDtypeStruct((n_c, 8, LANES), F32)],
        name=name, compiler_params=_cparams(1),
    )(dn_act, ps, a_rows, ad)


def dn_local_bwd(name, dn_act, ps, a_rows, ad, cots):
    s = dn_act.shape[0]
    n_c, n_g = s // DN_CHUNK, s // (DN_GROUP * DN_CHUNK)
    rows = DN_GROUP * DN_CHUNK

    def body(qkv_ref, ps_ref, ar_ref, ad_ref, du_ref, dkc_ref, dqd_ref, dkd_ref, dqk_ref, dgl_ref, dqkv_ref, dps_ref, dar_ref, dad_ref):
        first = pl.program_id(0) == 0
        qkv, ps_v, a_rows_v, ad_v = qkv_ref[...], ps_ref[...], ar_ref[...], ad_ref[...]
        d_wide = [r[...] for r in (du_ref, dkc_ref, dqd_ref, dkd_ref)]
        qs, ks, vs, ps_cs, ar_cs, cot = [], [], [], [], [], []
        for c in range(DN_GROUP):
            q4, k4, v4, ps_c, ar_c = _dn_group_inputs(qkv, ps_v, a_rows_v, c)
            qs, ks, vs, ps_cs, ar_cs = qs + q4, ks + k4, vs + v4, ps_cs + [ps_c], ar_cs + [ar_c]
            lo = c * DN_CHUNK
            d_tiles = [_split_heads(t[lo:lo + DN_CHUNK]) for t in d_wide]
            d_gl = dgl_ref[c]
            cot += [(d_tiles[0][h], d_tiles[1][h], d_tiles[2][h], d_tiles[3][h], dqk_ref[c, h], _col(_row(d_gl, h), 0))
                    for h in range(DN_HEADS)]

        def f(qs, ks, vs, ps_cs, ar_cs, ad_v):
            gates = [[] for _ in range(5)]
            for ps_c, ar_c in zip(ps_cs, ar_cs):
                for lst, vals in zip(gates, _dn_gates(ps_c, ar_c, ad_v)):
                    lst.extend(vals)
            return _dn_local(True, qs, ks, vs, *gates)

        _, vjp = jax.vjp(f, qs, ks, vs, ps_cs, ar_cs, ad_v)
        d_q, d_k, d_v, d_ps, d_ar, d_ad = vjp(cot)
        for c in range(DN_GROUP):
            at, hs = pl.ds(c * DN_CHUNK, DN_CHUNK), slice(c * DN_HEADS, (c + 1) * DN_HEADS)
            dqkv_ref[at, :] = jnp.concatenate(d_q[hs] + d_k[hs] + d_v[hs], axis=1).astype(dqkv_ref.dtype)
            dps_ref[at, :] = d_ps[c]
            dar_ref[c] = d_ar[c]
        _store(dad_ref, d_ad, first)

    wide = pl.BlockSpec((rows, BRANCH), lambda j: (j, 0))
    specs = _dn_local_specs()
    return pl.pallas_call(
        body, grid=(n_g,),
        in_specs=specs + [wide, wide, wide, wide, pl.BlockSpec((DN_GROUP, DN_HEADS, DN_CHUNK, DN_CHUNK), lambda j: (j, 0, 0, 0)),
                          pl.BlockSpec((DN_GROUP, 8, LANES), lambda j: (j, 0, 0))],
        out_specs=specs,
        out_shape=[jax.ShapeDtypeStruct((s, 3 * BRANCH), F32), jax.ShapeDtypeStruct((s, LANES), F32),
                   jax.ShapeDtypeStruct((n_c, DN_HEADS, DN_CHUNK), F32), jax.ShapeDtypeStruct((2, DN_HEADS), F32)],
        name=name, compiler_params=_cparams(1),
    )(dn_act, ps, a_rows, ad, *cots)


def _dn_scan_specs(n_c, rev):
    idx = (lambda j: n_c - 1 - j) if rev else (lambda j: j)
    wide = pl.BlockSpec((DN_CHUNK, BRANCH), lambda j: (idx(j), 0))
    return [wide, wide, wide, wide, pl.BlockSpec((1, DN_HEADS, DN_CHUNK, DN_CHUNK), lambda j: (idx(j), 0, 0, 0)),
            pl.BlockSpec((1, 8, LANES), lambda j: (idx(j), 0, 0)), pl.BlockSpec((DN_CHUNK, BRANCH), lambda j: (idx(j), C_DZ // BRANCH)),
            pl.BlockSpec((1, DN_DH), lambda j: (0, 0))]


def _dn_scan_tiles(refs):
    u_ref, kc_ref, qd_ref, kd_ref, qk_ref, gl_ref, z_ref, g_ref = refs
    wide = [_split_heads(r[...]) for r in (u_ref, kc_ref, qd_ref, kd_ref)]
    gl = gl_ref[0]
    return [(wide[0][h], wide[1][h], wide[2][h], wide[3][h], qk_ref[0, h], _col(_row(gl, h), 0)) for h in range(DN_HEADS)], \
        _split_heads(z_ref[...].astype(F32)), g_ref[...]


def dn_scan_fwd(name, local, pm, gain):
    s = pm.shape[0]
    n_c = s // DN_CHUNK

    def body(*refs):
        y_ref, hist_ref, state = refs[8:]

        @pl.when(pl.program_id(0) == 0)
        def _():
            state[...] = jnp.zeros_like(state)

        hist_ref[0] = state[...]
        per_head, z4, gain_v = _dn_scan_tiles(refs[:8])
        ys, s_nexts = _dn_step(False, [state[h] for h in range(DN_HEADS)], per_head, z4, gain_v)
        for h in range(DN_HEADS):
            state[h] = s_nexts[h]
        y_ref[...] = jnp.concatenate(ys, axis=1).astype(y_ref.dtype)

    return pl.pallas_call(
        body, grid=(n_c,), in_specs=_dn_scan_specs(n_c, False),
        out_specs=[pl.BlockSpec((DN_CHUNK, BRANCH), lambda j: (j, 0)),
                   pl.BlockSpec((1, DN_HEADS, DN_DH, DN_DH), lambda j: (j, 0, 0, 0))],
        out_shape=[jax.ShapeDtypeStruct((s, BRANCH), BF16), jax.ShapeDtypeStruct((n_c, DN_HEADS, DN_DH, DN_DH), F32)],
        scratch_shapes=[pltpu.VMEM((DN_HEADS, DN_DH, DN_DH), F32)],
        name=name, compiler_params=_cparams(1),
    )(*local, pm, gain)


def dn_scan_bwd(name, local, pm, gain, hist, dy):
    s = pm.shape[0]
    n_c = s // DN_CHUNK

    def body(*refs):
        hist_ref, dy_ref = refs[8:10]
        du_ref, dkc_ref, dqd_ref, dkd_ref, dqk_ref, dgl_ref, dz_ref, dg_ref, d_state = refs[10:]
        first = pl.program_id(0) == 0

        @pl.when(first)
        def _():
            d_state[...] = jnp.zeros_like(d_state)

        per_head, z4, gain_v = _dn_scan_tiles(refs[:8])
        _, vjp = jax.vjp(functools.partial(_dn_step, True), [hist_ref[0, h] for h in range(DN_HEADS)], per_head, z4, gain_v)
        d_s, grads, d_z, d_gain = vjp((_split_heads(dy_ref[...].astype(F32)), [d_state[h] for h in range(DN_HEADS)]))
        for h in range(DN_HEADS):
            d_state[h] = d_s[h]
        for ref, i in ((du_ref, 0), (dkc_ref, 1), (dqd_ref, 2), (dkd_ref, 3)):
            ref[...] = jnp.concatenate([g[i] for g in grads], axis=1)
        dz_ref[...] = jnp.concatenate(d_z, axis=1).astype(dz_ref.dtype)
        for h in range(DN_HEADS):
            dqk_ref[0, h] = grads[h][4]
        dgl_ref[0] = _head_rows([g[5] for g in grads])
        _store(dg_ref, d_gain, first)

    rev = lambda j: n_c - 1 - j
    specs = _dn_scan_specs(n_c, True)
    return pl.pallas_call(
        body, grid=(n_c,),
        in_specs=specs + [pl.BlockSpec((1, DN_HEADS, DN_DH, DN_DH), lambda j: (rev(j), 0, 0, 0)),
                          pl.BlockSpec((DN_CHUNK, BRANCH), lambda j: (rev(j), 0))],
        out_specs=specs[:6] + [pl.BlockSpec((DN_CHUNK, BRANCH), lambda j: (rev(j), 0)), specs[7]],
        out_shape=[jax.ShapeDtypeStruct((s, BRANCH), F32)] * 4 + [
            jax.ShapeDtypeStruct((n_c, DN_HEADS, DN_CHUNK, DN_CHUNK), F32), jax.ShapeDtypeStruct((n_c, 8, LANES), F32),
            jax.ShapeDtypeStruct((s, BRANCH), BF16), jax.ShapeDtypeStruct((1, DN_DH), F32)],
        scratch_shapes=[pltpu.VMEM((DN_HEADS, DN_DH, DN_DH), F32)],
        name=name, compiler_params=_cparams(1),
    )(*local, pm, gain, hist, dy)


def _seq_layouts(cols, s):
    return cols.T.reshape(cols.shape[1], s // LANES, LANES)


def layer_fwd(li, x, p, w, more_weights=None):
    s = x.shape[0]
    n = lambda t: f"{t}_l{li}"
    h = rms_fwd(n("rms_mix"), x, w["g_mix"])
    pm = mm(n("in_main"), h, w["in_main"], "nt")
    ps = mm(n("in_small"), h, w["in_small"], "nt")
    qn, kn = fox_prep_fwd(n("fox_prep"), pm, w["gq"], w["gk"])
    f_t = _seq_layouts(ps[:, 0:8], s)
    cum = fox_gate_fwd(n("fox_gate"), f_t, w["b_f"])
    cum_c, cum_r = cum.reshape(8, s, 1), cum.reshape(8, 1, s)
    y_fox = fox_attn_fwd(n("fox_attn"), qn, kn, pm, cum_c, cum_r)
    y_sc = tile_fwd(n("sconv"), _sconv_fn, (BRANCH // LANES,), sconv_ops(pm, w["sc_conv_w"]), [_col_out(s, BRANCH, BF16)])[0]
    dn_act = tile_fwd(n("dnconv"), _dnconv_fn, (3 * BRANCH // LANES,), dnconv_ops(pm, w["dn_conv_w"]), [_col_out(s, 3 * BRANCH)])[0]
    a_rows = ps[:, 12:16].reshape(s // DN_CHUNK, DN_CHUNK, DN_HEADS).transpose(0, 2, 1)
    dn_local = dn_local_fwd(n("dn_local"), dn_act, ps, a_rows, w["ad"])
    y_dn, hist = dn_scan_fwd(n("dn_scan"), dn_local, pm, w["dn_gain"])
    ys = (y_fox, y_sc, y_dn)
    if more_weights is not None:
        w = {**w, **more_weights(y_dn)}
    yp = [mm(n(f"branch{b}"), ys[b], w["branch"][b], "nn", blocks=(0, N_CHIPS)) for b in range(3)]
    merged = tile_fwd(n("merge"), _merge_fn, (s // 256,), merge_ops(yp, pm), [((s, D_MODEL), BF16, (256, D_MODEL), lambda i: (i, 0), ())])[0]
    x1 = mm(n("w_o"), merged, w["o"], "nn", add=x)
    h2 = rms_fwd(n("rms_ffn"), x1, w["g_ffn"])
    ug = mm(n("up_g"), h2, w["up"], "nn", blocks=(0, 2))
    uv = mm(n("up_v"), h2, w["up"], "nn", blocks=(2, 2))
    act = tile_fwd(n("ffn_act"), _ffn_act_fn, (D_FF // LANES,), ffn_ops(ug, uv, w["ffn_conv_w"]), [_col_out(s, D_FF, BF16)])[0]
    x2 = mm(n("down"), act, w["down"], "nn", add=x1)
    h3 = rms_fwd(n("rms_ple"), x2, w["g_ple"])
    gpre = mm(n("ple_gate"), h3, w["pg"], "nn")
    pe = mm(n("ple_emb"), p, w["ple"], "nn", blocks=(0, N_CHIPS))
    x3 = tile_fwd(n("ple"), _ple_fn, (s // 256,), ple_ops(gpre, pe, x2), [((s, D_MODEL), F32, (256, D_MODEL), lambda i: (i, 0), ())])[0]
    saved = dict(x=x, h=h, pm=pm, ps=ps, qn=qn, kn=kn, f_t=f_t, cum_c=cum_c, cum_r=cum_r, ys=ys, dn_act=dn_act, dn_local=dn_local,
                 a_rows=a_rows, hist=hist, yp=yp, merged=merged, x1=x1, h2=h2, ug=ug, uv=uv, act=act, x2=x2, h3=h3,
                 gpre=gpre, pe=pe, p=p)
    return x3, saved, w


def hang_on(w, token):
    zero = token[0, 0]
    small = ("g_mix", "g_ffn", "g_ple", "gq", "gk", "b_f", "ad", "dn_gain", "sc_conv_w", "dn_conv_w", "ffn_conv_w")
    return {**w, **{k: w[k] + zero for k in small}}


def layer_bwd(li, dx3, sv, w, hooks=None):
    hooks = hooks or {}

    def stage(key, after, w):
        return hang_on(w, hooks[key](after, g)) if key in hooks else w

    s = dx3.shape[0]
    n = lambda t: f"{t}_l{li}"
    g = {}
    col_own = lambda width: ((s, width), (s, LANES), lambda j: (0, j))
    d_gpre, d_pe = tile_bwd(n("ple_bwd"), _ple_fn, (s // 256,), ple_ops(sv["gpre"], sv["pe"], sv["x2"]), [_rows(dx3)],
                            [(0, (), None, BF16), (1, (), None, BF16)])
    g["w_ple"] = mm(n("d_w_ple"), sv["p"], d_pe, "tn", blocks=(0, N_CHIPS))
    g["w_ple_gate"] = mm(n("d_w_pg"), sv["h3"], d_gpre, "tn").reshape(N_CHIPS, -1, D_MODEL)
    dh3 = mm(n("d_h3"), d_gpre, w["pg"], "nt")
    dx2, d_g_ple = rms_bwd(n("rms_ple_bwd"), sv["x2"], w["g_ple"], dh3, dx3)
    dact = mm(n("d_act"), dx2, w["down"], "nt")
    g["w_down"] = mm(n("d_w_down"), sv["act"], dx2, "tn").reshape(N_CHIPS, -1, D_MODEL)
    taps_own = ((w["ffn_conv_w"].shape[0], D_FF), (w["ffn_conv_w"].shape[0], LANES), lambda j: (0, j))
    d_ug, d_uv, d_fw_g, d_fw_v = tile_bwd(n("ffn_act_bwd"), _ffn_act_fn, (D_FF // LANES,), ffn_ops(sv["ug"], sv["uv"], w["ffn_conv_w"]),
                                          [_col_cot(dact)], [(0, (), None, BF16), (1, (), None, BF16), (2, (), taps_own), (3, (), taps_own)])
    g["ffn_conv_w"] = jnp.concatenate([d_fw_g, d_fw_v], axis=1)
    g["w_up"] = jnp.concatenate([mm(n("d_w_up_g"), sv["h2"], d_ug, "tn", blocks=(0, 2)), mm(n("d_w_up_v"), sv["h2"], d_uv, "tn", blocks=(0, 2))])
    dh2 = mm(n("d_h2_v"), d_uv, w["up"], "nt", blocks=(2, 2), add=mm(n("d_h2_g"), d_ug, w["up"], "nt", blocks=(0, 2)))
    dx1, d_g_ffn = rms_bwd(n("rms_ffn_bwd"), sv["x1"], w["g_ffn"], dh2, dx2)
    w = stage("mid", dx1, w)
    dmerged = mm(n("d_merged"), dx1, w["o"], "nt")
    g["w_o"] = mm(n("d_w_o"), sv["merged"], dx1, "tn").reshape(N_CHIPS, -1, D_MODEL)
    gate_own = ((s, D_MODEL), (256, D_MODEL), lambda i: (i, 0))
    d_yp0, d_yp1, d_yp2, d_g0, d_g1, d_g2 = tile_bwd(
        n("merge_bwd"), _merge_fn, (s // 256,), merge_ops(sv["yp"], sv["pm"]), [_rows(dmerged)],
        [(0, (), None, BF16), (1, (), None, BF16), (2, (), None, BF16), (3, (), gate_own, BF16), (4, (), gate_own, BF16), (5, (), gate_own, BF16)])
    d_yp = (d_yp0, d_yp1, d_yp2)
    g["w_branch"] = jnp.concatenate([mm(n(f"d_w_branch{b}"), sv["ys"][b], d_yp[b], "tn", blocks=(0, N_CHIPS)) for b in range(3)], axis=1)
    d_ys = [mm(n(f"d_y{b}"), d_yp[b], w["branch"][b], "nt", blocks=(0, N_CHIPS)) for b in range(3)]
    w = stage("late", d_ys[2], w)
    *d_local, d_z, d_dngain = dn_scan_bwd(n("dn_scan_bwd"), sv["dn_local"], sv["pm"], w["dn_gain"], sv["hist"], d_ys[2])
    d_dnact, d_ps_dn, d_arows, d_ad = dn_local_bwd(n("dn_local_bwd"), sv["dn_act"], sv["ps"], sv["a_rows"], w["ad"], d_local)
    g["ad"], g["dn_norm_gain"] = d_ad, d_dngain[0]
    d_dnqkv, g["dn_conv_w"] = tile_bwd(n("dnconv_bwd"), _dnconv_fn, (3 * BRANCH // LANES,), dnconv_ops(sv["pm"], w["dn_conv_w"]),
                                       [_col_cot(d_dnact)], [(0, (), col_own(3 * BRANCH), BF16), (1, ())])
    d_sb, d_sc, d_sv, g["sc_conv_w"] = tile_bwd(n("sconv_bwd"), _sconv_fn, (BRANCH // LANES,), sconv_ops(sv["pm"], w["sc_conv_w"]), [_col_cot(d_ys[1])],
                                                [(0, (), col_own(BRANCH), BF16), (1, (), col_own(BRANCH), BF16), (2, (), col_own(BRANCH), BF16), (3, ())])
    w = stage("last", d_dnqkv, w)
    d_qn, d_kn, d_fv, d_cum = fox_attn_bwd(n("fox_attn_bwd"), sv["qn"], sv["kn"], sv["pm"], sv["cum_c"], sv["cum_r"], d_ys[0])
    d_ft, d_bf = fox_gate_bwd(n("fox_gate_bwd"), sv["f_t"], w["b_f"], d_cum.reshape(8, s // LANES, LANES))
    g["b_fox_f"] = d_bf.reshape(8)
    d_fq, d_fk, d_gq, d_gk = fox_prep_bwd(n("fox_prep_bwd"), sv["pm"], w["gq"], w["gk"], d_qn, d_kn)
    g["fox_q_gain"] = d_gq[0, :FOX_DH] + d_gq[0, FOX_DH:]
    g["fox_k_gain"] = d_gk[0, :FOX_DH] + d_gk[0, FOX_DH:]
    d_pm = jnp.concatenate([d_fq, d_fk, d_fv.astype(BF16), d_sb, d_sc, d_sv, d_dnqkv, d_z, d_g0, d_g1, d_g2], axis=1)
    d_a_cols = d_arows.transpose(0, 2, 1).reshape(s, DN_HEADS)
    d_f_cols = d_ft.reshape(8, s).T
    d_ps = d_ps_dn + jnp.concatenate([d_f_cols, jnp.zeros((s, 4), F32), d_a_cols, jnp.zeros((s, LANES - 16), F32)], axis=1)
    g["w_in"] = chip_blocks_w_in(mm(n("d_w_in_main"), d_pm, sv["h"], "tn"), mm(n("d_w_in_small"), d_ps, sv["h"], "tn"))
    dh = mm(n("d_h_small"), d_ps, w["in_small"], "nn", add=mm(n("d_h_main"), d_pm, w["in_main"], "nn"))
    dx, d_g_mix = rms_bwd(n("rms_mix_bwd"), sv["x"], w["g_mix"], dh, dx1)
    g["g_mix"], g["g_ffn"], g["g_ple"] = d_g_mix[0], d_g_ffn[0], d_g_ple[0]
    return dx, g


IN_SHARD = 2052
MAIN_RANGES = ((0, 1536), (1544, 3080), (3080, 4616), (4624, 5136), (5136, 8208))
SMALL_RANGES = ((1536, 1544), (4616, 4620), (4620, 4624))


def _from_chip_blocks(blocks, ranges):
    parts = []
    for lo, hi in ranges:
        for k in range(N_CHIPS):
            a0, a1 = max(lo, k * IN_SHARD), min(hi, (k + 1) * IN_SHARD)
            if a0 < a1:
                parts.append(blocks[k][a0 - k * IN_SHARD:a1 - k * IN_SHARD])
    return parts


def split_w_in(blocks):
    main = jnp.concatenate(_from_chip_blocks(blocks, MAIN_RANGES), axis=0)
    pad = jnp.zeros((LANES - 16, blocks.shape[2]), blocks.dtype)
    return main, jnp.concatenate(_from_chip_blocks(blocks, SMALL_RANGES) + [pad], axis=0)


def chip_blocks_w_in(main, small):
    ranges = sorted([(lo, hi, "m") for lo, hi in MAIN_RANGES] + [(lo, hi, "s") for lo, hi in SMALL_RANGES])
    offs, m_off, s_off = {}, 0, 0
    for lo, hi in MAIN_RANGES:
        offs[lo] = m_off
        m_off += hi - lo
    for lo, hi in SMALL_RANGES:
        offs[lo] = s_off
        s_off += hi - lo
    blocks = []
    for k in range(N_CHIPS):
        parts = []
        for lo, hi, src in ranges:
            a0, a1 = max(lo, k * IN_SHARD), min(hi, (k + 1) * IN_SHARD)
            if a0 < a1:
                arr = main if src == "m" else small
                parts.append(arr[offs[lo] + a0 - lo:offs[lo] + a1 - lo])
        blocks.append(jnp.concatenate(parts, axis=0))
    return jnp.stack(blocks)


def later_weights(got):
    g_branch, g_o, g_up, g_down, g_pg, g_ple = got
    branch = g_branch.reshape(N_CHIPS, 3, BRANCH, -1)
    return dict(branch=[branch[:, b] for b in range(3)], o=g_o.reshape(D_MODEL, D_MODEL), up=g_up,
                down=g_down.reshape(D_FF, D_MODEL), pg=g_pg.reshape(D_MODEL, D_MODEL), ple=g_ple)


def layer_weights(li, got, conv, a):
    main, small = split_w_in(got[0])
    tile2 = lambda v: jnp.concatenate([v, v])[None, :]
    rest = later_weights(got[1:]) if len(got) > 1 else {}
    return dict(
        in_main=main, in_small=small, **rest,
        g_mix=a["g_mix"][li][None, :], g_ffn=a["g_ffn"][li][None, :], g_ple=a["g_ple"][li][None, :],
        gq=tile2(a["fox_q_gain"][li]), gk=tile2(a["fox_k_gain"][li]), b_f=a["b_fox_f"][li].reshape(8, 1, 1),
        ad=jnp.stack([a["dn_a_log"][li], a["dn_dt_bias"][li]]), dn_gain=a["dn_norm_gain"][li][None, :],
        sc_conv_w=conv["sc_conv_w"][li], dn_conv_w=conv["dn_conv_w"][li], ffn_conv_w=conv["ffn_conv_w"][li])


def pack_rows(arrs, dtype):
    flat = jnp.concatenate([t.reshape(-1).astype(dtype) for t in arrs])
    pad = (-flat.shape[0]) % (8 * LANES)
    if pad:
        flat = jnp.concatenate([flat, jnp.zeros((pad,), dtype)])
    return flat.reshape(-1, LANES)


def unpack_rows(buf, shapes):
    flat = buf.reshape(-1)
    out, off = [], 0
    for shp in shapes:
        size = 1
        for dim in shp:
            size *= dim
        out.append(flat[off:off + size].reshape(shp))
        off += size
    return out


def chip_shard(t, axis, k):
    width = t.shape[axis] // N_CHIPS
    return lax.slice_in_dim(t, k * width, (k + 1) * width, axis=axis)


ANY = pl.BlockSpec(memory_space=pl.ANY)


def _position():
    x, y, c = lax.axis_index("x"), lax.axis_index("y"), lax.axis_index("c")
    return x, y, c, [(1 - x, y), (x, 1 - y), (1 - x, 1 - y)]


def gather_small(name, block):
    m_per, n = block.shape

    def body(x_ref, out_ref, token, send_sems, recv_sems, local_sem):
        token[...] = jnp.zeros_like(token)
        x, y, c, chips = _position()
        me, sibling = (x, y, c), (x, y, 1 - c)

        def rows(px, py, pc):
            return out_ref.at[pl.ds((4 * px + 2 * py + pc) * m_per, m_per), :]

        def copy(k, blk, to, src=None):
            return pltpu.make_async_remote_copy(src_ref=rows(*blk) if src is None else src, dst_ref=rows(*blk),
                                                send_sem=send_sems.at[k], recv_sem=recv_sems.at[k], device_id=to, device_id_type=MESH)

        mine = pltpu.make_async_copy(x_ref, rows(*me), local_sem)
        mine.start()
        first = [copy(0, me, sibling, src=x_ref)] + [copy(1 + j, me, (*chip, c), src=x_ref) for j, chip in enumerate(chips)]
        for cp in first:
            cp.start()
        passed = [copy(4 + j, (*chip, c), sibling) for j, chip in enumerate(chips)]
        for j, chip in enumerate(chips):
            copy(1 + j, (*chip, c), me).wait_recv()
            passed[j].start()
        copy(0, sibling, me).wait_recv()
        for j, chip in enumerate(chips):
            copy(4 + j, (*chip, 1 - c), me).wait_recv()
        for cp in first + passed:
            cp.wait_send()
        mine.wait()

    in_vmem = pl.BlockSpec(memory_space=pltpu.VMEM)
    return pl.pallas_call(
        body, out_shape=[jax.ShapeDtypeStruct((8 * m_per, n), block.dtype), jax.ShapeDtypeStruct((8, LANES), F32)],
        in_specs=[in_vmem], out_specs=[in_vmem, in_vmem],
        scratch_shapes=[pltpu.SemaphoreType.DMA((7,)), pltpu.SemaphoreType.DMA((7,)), pltpu.SemaphoreType.DMA],
        name=name, compiler_params=pltpu.CompilerParams(vmem_limit_bytes=VMEM_LIMIT),
    )(block)


def _sems(n):
    return [pltpu.SemaphoreType.DMA((n,)), pltpu.SemaphoreType.DMA((n,))]


def _split_cols(rows):
    return (rows // 2) % 16 != 0


def _half(ref, which, lead=()):
    rows, cols = ref.shape[-2:]
    if _split_cols(rows):
        return ref.at[(*lead, slice(None), pl.ds(which * (cols // 2), cols // 2))]
    return ref.at[(*lead, pl.ds(which * (rows // 2), rows // 2), slice(None))]


def _half_shape(rows, cols):
    return (rows, cols // 2) if _split_cols(rows) else (rows // 2, cols)


def gather_layer(name, shards):
    n_w = len(shards)

    def body(*refs):
        ins, outs = refs[:n_w], refs[n_w:2 * n_w]
        token, send_sems, recv_sems = refs[2 * n_w:]
        token[...] = jnp.zeros_like(token)
        x, y, c, chips = _position()
        sibling = (x, y, 1 - c)

        def part(w, px, py, pc):
            return _half(outs[w], pc, (2 * px + py,))

        def copy(k, w, blk, to, src=None):
            return pltpu.make_async_remote_copy(src_ref=part(w, *blk) if src is None else src, dst_ref=part(w, *blk),
                                                send_sem=send_sems.at[k], recv_sem=recv_sems.at[k], device_id=to, device_id_type=MESH)

        pairs = [(w, j, chip) for w in range(n_w) for j, chip in enumerate(chips)]
        first = [copy(3 * w + j, w, (x, y, c), (*chip, c), src=_half(ins[w], c)) for w, j, chip in pairs]
        for cp in first:
            cp.start()
        passed = [copy(3 * n_w + 3 * w + j, w, (*chip, c), sibling) for w, j, chip in pairs]
        for (w, j, chip), fwd in zip(pairs, passed):
            copy(3 * w + j, w, (*chip, c), (x, y, c)).wait_recv()
            fwd.start()
        for w, j, chip in pairs:
            copy(3 * n_w + 3 * w + j, w, (*chip, 1 - c), (x, y, c)).wait_recv()
        for cp in first + passed:
            cp.wait_send()

    out = pl.pallas_call(
        body, out_shape=[jax.ShapeDtypeStruct((N_CHIPS,) + s.shape, s.dtype) for s in shards] + [jax.ShapeDtypeStruct((8, LANES), F32)],
        in_specs=[ANY] * n_w, out_specs=[ANY] * n_w + [pl.BlockSpec(memory_space=pltpu.VMEM)], scratch_shapes=_sems(6 * n_w), name=name,
    )(*shards)
    return out[:n_w], out[n_w]


def swap_halves(name, grads):
    n_w = len(grads)

    def body(*refs):
        ins, outs = refs[:n_w], refs[n_w:2 * n_w]
        send_sems, recv_sems = refs[2 * n_w:]
        x, y, c, _ = _position()
        cps = [pltpu.make_async_remote_copy(src_ref=_half(ins[w], 1 - c, (slice(None),)), dst_ref=outs[w],
                                            send_sem=send_sems.at[w], recv_sem=recv_sems.at[w], device_id=(x, y, 1 - c),
                                            device_id_type=MESH) for w in range(n_w)]
        for cp in cps:
            cp.start()
        for cp in cps:
            cp.wait()

    return pl.pallas_call(
        body, out_shape=[jax.ShapeDtypeStruct((N_CHIPS,) + _half_shape(*g.shape[1:]), g.dtype) for g in grads],
        in_specs=[ANY] * n_w, out_specs=[ANY] * n_w, scratch_shapes=_sems(n_w), name=name,
    )(*grads)


def scatter_chips(name, partials):
    n_w = len(partials)

    def body(*refs):
        ins, outs = refs[:n_w], refs[n_w:2 * n_w]
        send_sems, recv_sems = refs[2 * n_w:]
        x, y, c, chips = _position()
        cps = [pltpu.make_async_remote_copy(src_ref=ins[w].at[2 * cx + cy], dst_ref=outs[w].at[j], send_sem=send_sems.at[3 * w + j],
                                            recv_sem=recv_sems.at[3 * w + j], device_id=(cx, cy, c), device_id_type=MESH)
               for w in range(n_w) for j, (cx, cy) in enumerate(chips)]
        for cp in cps:
            cp.start()
        for cp in cps:
            cp.wait()

    return pl.pallas_call(
        body, out_shape=[jax.ShapeDtypeStruct((3,) + p.shape[1:], p.dtype) for p in partials],
        in_specs=[ANY] * n_w, out_specs=[ANY] * n_w, scratch_shapes=_sems(3 * n_w), name=name,
    )(*partials)


def share_halves(name, bufs):
    n_w = len(bufs)

    def body(*refs):
        outs = refs[n_w:2 * n_w]
        send_sems, recv_sems = refs[2 * n_w:]
        x, y, c, _ = _position()

        def copy(w, pc):
            half = _half(outs[w], pc)
            return pltpu.make_async_remote_copy(src_ref=half, dst_ref=half, send_sem=send_sems.at[w], recv_sem=recv_sems.at[w],
                                                device_id=(x, y, 1 - c), device_id_type=MESH)

        for w in range(n_w):
            copy(w, c).start()
        for w in range(n_w):
            copy(w, 1 - c).wait_recv()
            copy(w, c).wait_send()

    return pl.pallas_call(
        body, out_shape=[jax.ShapeDtypeStruct(b.shape, b.dtype) for b in bufs], in_specs=[ANY] * n_w, out_specs=[ANY] * n_w,
        input_output_aliases={w: w for w in range(n_w)}, scratch_shapes=_sems(n_w), name=name,
    )(*bufs)


HBM = pl.BlockSpec(memory_space=pltpu.HBM)
SEM = pl.BlockSpec(memory_space=pltpu.SEMAPHORE)
EFFECT = pltpu.SideEffectType.DATAFLOW_SIDE_EFFECTING


def _exchange_copies(kind, srcs, lands):
    x, y, c, chips = _position()
    out = []
    for src, land in zip(srcs, lands):
        if kind == "swap":
            out.append((_half(src, 1 - c, (slice(None),)), land, (x, y, 1 - c)))
            continue
        for j, (cx, cy) in enumerate(chips):
            if kind == "gather":
                out.append((src, land.at[2 * x + y], (cx, cy, c)))
            else:
                out.append((src.at[2 * cx + cy], land.at[j], (cx, cy, c)))
    return out


def _land_shapes(kind, srcs):
    if kind == "gather":
        return [(N_CHIPS,) + s.shape for s in srcs]
    if kind == "swap":
        return [(N_CHIPS,) + _half_shape(*s.shape[1:]) for s in srcs]
    return [(3,) + s.shape[1:] for s in srcs]


def exchange_start(name, kind, srcs):
    n_w = len(srcs)
    shapes = _land_shapes(kind, srcs)
    n_sem = n_w if kind == "swap" else 3 * n_w

    def body(*refs):
        ins, lands = refs[:n_w], refs[n_w:2 * n_w]
        send_sems, recv_sems = refs[2 * n_w:2 * n_w + 2]
        token = refs[-1]
        for i, (src, dst, dev) in enumerate(_exchange_copies(kind, ins, lands)):
            pltpu.make_async_remote_copy(src_ref=src, dst_ref=dst, send_sem=send_sems.at[i], recv_sem=recv_sems.at[i],
                                         device_id=dev, device_id_type=MESH).start()
        token[...] = jnp.zeros_like(token)

    out = pl.pallas_call(
        body, name=name,
        out_shape=(pltpu.SemaphoreType.DMA((n_sem,)), pltpu.SemaphoreType.DMA((n_sem,)),
                   *[pltpu.HBM(s.shape, s.dtype) for s in srcs], *[pltpu.HBM(shp, s.dtype) for shp, s in zip(shapes, srcs)],
                   jax.ShapeDtypeStruct((8, LANES), F32)),
        in_specs=(HBM,) * (2 * n_w), out_specs=(SEM, SEM) + (HBM,) * (2 * n_w) + (pl.BlockSpec(memory_space=pltpu.VMEM),),
        input_output_aliases={i: 2 + i for i in range(2 * n_w)},
        compiler_params=pltpu.CompilerParams(has_side_effects=EFFECT),
    )(*[pltpu.with_memory_space_constraint(s, pltpu.HBM) for s in srcs],
      *[pltpu.with_memory_space_constraint(lax.empty(shp, s.dtype), pltpu.HBM) for shp, s in zip(shapes, srcs)])
    return (kind, n_w, out[:-1]), out[-1]


def exchange_wait(name, handle, after):
    kind, n_w, (send_sems, recv_sems, *thru) = handle
    n_sem = n_w if kind == "swap" else 3 * n_w

    def body(*refs):
        ins, lands = refs[:n_w], refs[n_w:2 * n_w]
        send_sems, recv_sems = refs[2 * n_w:2 * n_w + 2]
        for i, (src, dst, dev) in enumerate(_exchange_copies(kind, ins, lands)):
            cp = pltpu.make_async_remote_copy(src_ref=src, dst_ref=dst, send_sem=send_sems.at[i], recv_sem=recv_sems.at[i],
                                              device_id=dev, device_id_type=MESH)
            cp.wait_send()
            cp.wait_recv()

    out = pl.pallas_call(
        body, name=name, out_shape=tuple(pltpu.HBM(t.shape, t.dtype) for t in thru),
        in_specs=(HBM,) * (2 * n_w) + (SEM, SEM, pl.BlockSpec(memory_space=pl.ANY)), out_specs=(HBM,) * (2 * n_w),
        input_output_aliases={i: i for i in range(2 * n_w)},
        compiler_params=pltpu.CompilerParams(has_side_effects=EFFECT),
    )(*thru, send_sems, recv_sems, after)
    return list(out[n_w:])


def _row_tile(rows, cols):
    best = rows
    if rows * cols * 4 <= 1024 * 1024:
        return rows
    for t in range(16, rows, 16):
        if rows % t == 0 and t * cols * 4 <= 1024 * 1024:
            best = t
    return best


def pair_sum(name, pos, grad, from_sibling):
    _, rows, cols = grad.shape
    h_rows, h_cols = _half_shape(rows, cols)
    tr = _row_tile(h_rows, h_cols)
    n_t = h_rows // tr

    def body(pos_ref, g_ref, s_ref, b_ref, f_ref):
        tot = g_ref[...] + s_ref[...]
        b_ref[...] = tot.astype(BF16)

        @pl.when(pl.program_id(1) == pos_ref[1])
        def _():
            f_ref[...] = tot[0]

    blk = pl.BlockSpec((1, tr, h_cols), lambda i, k, pos: (k, i, 0))
    if _split_cols(rows):
        mine = pl.BlockSpec((1, tr, h_cols), lambda i, k, pos: (k, i, pos[0]))
    else:
        mine = pl.BlockSpec((1, tr, h_cols), lambda i, k, pos: (k, pos[0] * n_t + i, 0))
    return pl.pallas_call(
        body, grid_spec=pltpu.PrefetchScalarGridSpec(
            num_scalar_prefetch=1, grid=(n_t, N_CHIPS), in_specs=[mine, blk],
            out_specs=[blk, pl.BlockSpec((tr, h_cols), lambda i, k, pos: (i, 0))]),
        out_shape=[jax.ShapeDtypeStruct((N_CHIPS, h_rows, h_cols), BF16), jax.ShapeDtypeStruct((h_rows, h_cols), F32)],
        name=name, compiler_params=_cparams(2),
    )(pos, grad, from_sibling)


def chip_sum(name, pos, own, landed, split_cols):
    half, cols = own.shape
    tr = _row_tile(half, cols)
    n_t = half // tr

    def body(pos_ref, p_ref, l_ref, o_ref):
        o_ref[...] = ((p_ref[...] + l_ref[0].astype(F32)) + l_ref[1].astype(F32)) + l_ref[2].astype(F32)

    if split_cols:
        out_spec, out_shape = pl.BlockSpec((tr, cols), lambda i, pos: (i, pos[0])), (half, 2 * cols)
    else:
        out_spec, out_shape = pl.BlockSpec((tr, cols), lambda i, pos: (pos[0] * n_t + i, 0)), (2 * half, cols)
    return pl.pallas_call(
        body, grid_spec=pltpu.PrefetchScalarGridSpec(
            num_scalar_prefetch=1, grid=(n_t,),
            in_specs=[pl.BlockSpec((tr, cols), lambda i, pos: (i, 0)), pl.BlockSpec((3, tr, cols), lambda i, pos: (0, i, 0))],
            out_specs=out_spec),
        out_shape=jax.ShapeDtypeStruct(out_shape, F32), name=name, compiler_params=_cparams(1),
    )(pos, own, landed)


def reduce_scatter_layer(tag, pos, grads):
    n = lambda t: f"{t}_{tag}"
    from_sibling = swap_halves(n("swap_halves"), grads)
    sums = [pair_sum(n(f"pair_sum{w}"), pos, g, s) for w, (g, s) in enumerate(zip(grads, from_sibling))]
    landed = scatter_chips(n("scatter_chips"), [b for b, _ in sums])
    halves = [chip_sum(n(f"chip_sum{w}"), pos, own, l, _split_cols(g.shape[1])) for w, ((_, own), l, g) in enumerate(zip(sums, landed, grads))]
    return share_halves(n("share_halves"), halves)


class OverlappedReduceScatter:
    def __init__(self, tag, pos, grads):
        self.n = lambda t: f"{t}_{tag}"
        self.pos, self.grads = pos, grads
        self.swap, self.token = exchange_start(self.n("swap_start"), "swap", grads)

    def middle(self, after):
        from_sibling = exchange_wait(self.n("swap_wait"), self.swap, after)
        self.sums = [pair_sum(self.n(f"pair_sum{w}"), self.pos, g, s) for w, (g, s) in enumerate(zip(self.grads, from_sibling))]
        self.scatter, self.token = exchange_start(self.n("scatter_start"), "scatter", [b for b, _ in self.sums])

    def finish(self, after):
        landed = exchange_wait(self.n("scatter_wait"), self.scatter, after)
        halves = [chip_sum(self.n(f"chip_sum{w}"), self.pos, own, l, _split_cols(g.shape[1]))
                  for w, ((_, own), l, g) in enumerate(zip(self.sums, landed, self.grads))]
        return share_halves(self.n("share_halves"), halves)


def sum_devices(gathered):
    m_per = gathered.shape[0] // 8

    def body(g_ref, o_ref):
        tot = g_ref[pl.ds(0, m_per), :]
        for dev in range(1, 8):
            tot = tot + g_ref[pl.ds(dev * m_per, m_per), :]
        o_ref[...] = tot

    return pl.pallas_call(
        body, out_shape=jax.ShapeDtypeStruct((m_per, gathered.shape[1]), F32),
        in_specs=[pl.BlockSpec(memory_space=pltpu.VMEM)], out_specs=pl.BlockSpec(memory_space=pltpu.VMEM), name="sum_devices",
    )(gathered)


def kernel(x, p, g_mix, w_in, b_fox_f, fox_q_gain, fox_k_gain, sc_conv_w, dn_conv_w, dn_a_log, dn_dt_bias, dn_norm_gain, w_branch, w_o, g_ffn, w_up, ffn_conv_w, w_down, g_ple, w_ple_gate, w_ple, loss_target, m_g_mix, m_w_in, m_b_fox_f, m_fox_q_gain, m_fox_k_gain, m_sc_conv_w, m_dn_conv_w, m_dn_a_log, m_dn_dt_bias, m_dn_norm_gain, m_w_branch, m_w_o, m_g_ffn, m_w_up, m_ffn_conv_w, m_w_down, m_g_ple, m_w_ple_gate, m_w_ple, v_g_mix, v_w_in, v_b_fox_f, v_fox_q_gain, v_fox_k_gain, v_sc_conv_w, v_dn_conv_w, v_dn_a_log, v_dn_dt_bias, v_dn_norm_gain, v_w_branch, v_w_o, v_g_ffn, v_w_up, v_ffn_conv_w, v_w_down, v_g_ple, v_w_ple_gate, v_w_ple):
    a = dict(g_mix=g_mix, w_in=w_in, b_fox_f=b_fox_f, fox_q_gain=fox_q_gain, fox_k_gain=fox_k_gain, sc_conv_w=sc_conv_w,
             dn_conv_w=dn_conv_w, dn_a_log=dn_a_log, dn_dt_bias=dn_dt_bias, dn_norm_gain=dn_norm_gain, w_branch=w_branch, w_o=w_o,
             g_ffn=g_ffn, w_up=w_up, ffn_conv_w=ffn_conv_w, w_down=w_down, g_ple=g_ple, w_ple_gate=w_ple_gate, w_ple=w_ple)
    mom = dict(g_mix=m_g_mix, w_in=m_w_in, b_fox_f=m_b_fox_f, fox_q_gain=m_fox_q_gain, fox_k_gain=m_fox_k_gain, sc_conv_w=m_sc_conv_w,
               dn_conv_w=m_dn_conv_w, dn_a_log=m_dn_a_log, dn_dt_bias=m_dn_dt_bias, dn_norm_gain=m_dn_norm_gain, w_branch=m_w_branch,
               w_o=m_w_o, g_ffn=m_g_ffn, w_up=m_w_up, ffn_conv_w=m_ffn_conv_w, w_down=m_w_down, g_ple=m_g_ple, w_ple_gate=m_w_ple_gate,
               w_ple=m_w_ple)
    var = dict(g_mix=v_g_mix, w_in=v_w_in, b_fox_f=v_b_fox_f, fox_q_gain=v_fox_q_gain, fox_k_gain=v_fox_k_gain, sc_conv_w=v_sc_conv_w,
               dn_conv_w=v_dn_conv_w, dn_a_log=v_dn_a_log, dn_dt_bias=v_dn_dt_bias, dn_norm_gain=v_dn_norm_gain, w_branch=v_w_branch,
               w_o=v_w_o, g_ffn=v_g_ffn, w_up=v_w_up, ffn_conv_w=v_ffn_conv_w, w_down=v_w_down, g_ple=v_g_ple, w_ple_gate=v_w_ple_gate,
               w_ple=v_w_ple)
    cx, cy, cc = lax.axis_index("x"), lax.axis_index("y"), lax.axis_index("c")
    chip = 2 * cx + cy
    pos = jnp.stack([cc, chip]).astype(jnp.int32)

    def as_blocks(t):
        return t.reshape(2, -1, t.shape[-1])

    def own_block_in(got, shards):
        return [lax.dynamic_update_slice(g, s[None], (chip, 0, 0)) for g, s in zip(got, shards)]

    conv_shapes = [a[nm].shape for nm in CONVS]
    conv_all, conv_token = gather_small("gather_conv_w", pack_rows([a[nm] for nm in CONVS], F32))
    def layer_block(nm, t, li):
        return jnp.transpose(t, (2, 0, 1))[:, li, :] if nm == "w_in" else as_blocks(t)[li]

    shards0 = [(layer_block(nm, a[nm], 0) + conv_token[0, 0]).astype(BF16) for nm in BIG]
    got0, gathered_token = gather_layer("gather_w_in_l0", shards0[:1])
    shards0[1:] = [s + gathered_token[0, 0].astype(BF16) for s in shards0[1:]]
    gather0, gather0_token = exchange_start("gather_start_l0", "gather", shards0[1:])
    shards1 = [(layer_block(nm, a[nm], 1) + gather0_token[0, 0]).astype(BF16) for nm in BIG]
    gather1, gather1_token = exchange_start("gather_start_l1", "gather", shards1)
    conv_rows = conv_all.shape[0] // 8
    conv_chip = [unpack_rows(conv_all[2 * k * conv_rows:(2 * k + 1) * conv_rows], conv_shapes) for k in range(N_CHIPS)]
    conv = {nm: jnp.concatenate([conv_chip[k][i] for k in range(N_CHIPS)], axis=2) for i, nm in enumerate(CONVS)}

    weights, saved = [None, None], [None, None]
    first_weights = hang_on(layer_weights(0, own_block_in(got0, shards0[:1]), conv, a), gather1_token)

    def rest_of_layer0(after):
        return later_weights(own_block_in(exchange_wait("gather_wait_l0", gather0, after), shards0[1:]))

    act, saved[0], weights[0] = layer_fwd(0, x[0], p[0, 0], first_weights, more_weights=rest_of_layer0)
    got1 = exchange_wait("gather_wait_l1", gather1, act)
    act, saved[1], weights[1] = layer_fwd(1, act, p[1, 0], layer_weights(1, own_block_in(got1, shards1), conv, a))
    d_act, loss_part = loss_call(act, loss_target[0])
    loss = lax.psum(loss_part, ("x", "y", "c"))
    layer_grads = [None, None]
    d_act, layer_grads[1] = layer_bwd(1, d_act, saved[1], weights[1])
    rs1 = OverlappedReduceScatter("l1", pos, [layer_grads[1][nm] for nm in BIG])
    rs0 = []

    def stage_mid(after, g):
        rs1.middle(after)
        return rs1.token

    def stage_late(after, g):
        rs0.append(OverlappedReduceScatter("l0", pos, [g[nm] for nm in BIG[1:]]))
        return rs0[0].token

    def stage_last(after, g):
        rs0[0].middle(after)
        return rs0[0].token

    d_act, layer_grads[0] = layer_bwd(0, d_act, saved[0], hang_on(weights[0], rs1.token),
                                      hooks=dict(mid=stage_mid, late=stage_late, last=stage_last))
    reduced = [None, rs1.finish(d_act)]
    reduced[0] = reduce_scatter_layer("w_in_l0", pos, [layer_grads[0]["w_in"]]) + rs0[0].finish(d_act)
    grad_x = d_act[None]

    def both(nm):
        return jnp.stack([layer_grads[0][nm], layer_grads[1][nm]])

    local = {nm: both(nm) for nm in ("g_mix", "b_fox_f", "fox_q_gain", "fox_k_gain", "dn_norm_gain", "g_ffn", "g_ple", "sc_conv_w",
                                      "dn_conv_w", "ffn_conv_w")}
    local["dn_a_log"] = jnp.stack([layer_grads[li]["ad"][0] for li in range(2)])
    local["dn_dt_bias"] = jnp.stack([layer_grads[li]["ad"][1] for li in range(2)])

    small_names = SMALL + CONVS
    small_shapes = [local[nm].shape for nm in small_names]
    small_sum = sum_devices(gather_small("gather_small_grads", pack_rows([local[nm] for nm in small_names], F32))[0])
    small_grads = dict(zip(small_names, unpack_rows(small_sum, small_shapes)))
    for nm in CONVS:
        width = a[nm].shape[2]
        small_grads[nm] = lax.dynamic_slice_in_dim(small_grads[nm], chip * width, width, axis=2)

    grads, deltas, new_m, new_v = dict(small_grads), {}, {}, {}
    for nm in small_names:
        deltas[nm], new_m[nm], new_v[nm] = adam_call(f"adam_{nm}", a[nm], grads[nm], mom[nm], var[nm])
    for i, nm in enumerate(BIG):
        if nm == "w_in":
            stored = lambda t: jnp.transpose(t, (2, 0, 1))
            res = adam_w_in(f"adam_{nm}", stored(a[nm]), stored(mom[nm]), stored(var[nm]), reduced[0][i], reduced[1][i])
            grads[nm], deltas[nm], new_m[nm], new_v[nm] = [jnp.transpose(r, (1, 2, 0)) for r in res]
            continue
        res = adam_layers(f"adam_{nm}", as_blocks(a[nm]), as_blocks(mom[nm]), as_blocks(var[nm]), reduced[0][i], reduced[1][i])
        grads[nm], deltas[nm], new_m[nm], new_v[nm] = [r.reshape(a[nm].shape) for r in res]
    return (loss, grad_x, *[grads[nm] for nm in WEIGHTS], *[deltas[nm] for nm in WEIGHTS], *[new_m[nm] for nm in WEIGHTS],
            *[new_v[nm] for nm in WEIGHTS])
```

```python
import functools

import jax
import jax.numpy as jnp
from jax import lax
from jax.experimental import pallas as pl
from jax.experimental.pallas import tpu as pltpu

F32 = jnp.float32
BF16 = jnp.bfloat16
HI = lax.Precision.HIGHEST
MESH = pl.DeviceIdType.MESH

D_MODEL = 1024
BRANCH = 512
FOX_DH = 64
DN_DH = 128
DN_HEADS = 4
DN_CHUNK = 64
FOX_BLOCK = 128
D_FF = 2816
EPS = 1e-6
N_CHIPS = 4
LANES = 128

ADAM_LR, ADAM_B1, ADAM_B2, ADAM_EPS, ADAM_WD, ADAM_STEP = 0.001, 0.9, 0.999, 1e-08, 0.01, 10

VMEM_LIMIT = 56 * 1024 * 1024

C_FQ, C_FK, C_FV, C_SB, C_SC, C_SV, C_DN, C_DZ, C_GATE = 0, 512, 1024, 1536, 2048, 2560, 3072, 4608, 5120
IN_MAIN = 8192
IN_SIZES = (1536, 8, 1536, 1536, 4, 4, 512, 3072)

BIG = ("w_in", "w_branch", "w_o", "w_up", "w_down", "w_ple_gate", "w_ple")
BIG_AXIS = {"w_in": 2, "w_branch": 3, "w_o": 1, "w_up": 2, "w_down": 1, "w_ple_gate": 1, "w_ple": 2}
CONVS = ("sc_conv_w", "dn_conv_w", "ffn_conv_w")
SMALL = ("g_mix", "b_fox_f", "fox_q_gain", "fox_k_gain", "dn_a_log", "dn_dt_bias", "dn_norm_gain", "g_ffn", "g_ple")
WEIGHTS = ("g_mix", "w_in", "b_fox_f", "fox_q_gain", "fox_k_gain", "sc_conv_w", "dn_conv_w", "dn_a_log", "dn_dt_bias",
           "dn_norm_gain", "w_branch", "w_o", "g_ffn", "w_up", "ffn_conv_w", "w_down", "g_ple", "w_ple_gate", "w_ple")


def _iota(shape, dim):
    return lax.broadcasted_iota(jnp.int32, shape, dim)


def _dg(a, b, mode, prec=None):
    dims = {"nn": ((1,), (0,)), "nt": ((1,), (1,)), "tn": ((0,), (0,))}[mode]
    return lax.dot_general(a, b, (dims, ((), ())), precision=prec, preferred_element_type=F32)


def _bdot_impl(a, b, mode):
    return _dg(a.astype(BF16), b.astype(BF16), mode)


@functools.partial(jax.custom_vjp, nondiff_argnums=(2,))
def _bdot_diff(a, b, mode):
    return _bdot_impl(a, b, mode)


def _bdot_fwd(a, b, mode):
    return _bdot_impl(a, b, mode), (a, b)


def _bdot_bwd(mode, res, g):
    a, b = res
    if mode == "nn":
        da, db = _bdot_impl(g, b, "nt"), _bdot_impl(a, g, "tn")
    elif mode == "nt":
        da, db = _bdot_impl(g, b, "nn"), _bdot_impl(g, a, "tn")
    else:
        da, db = _bdot_impl(b, g, "nt"), _bdot_impl(a, g, "nn")
    return da.astype(a.dtype), db.astype(b.dtype)


_bdot_diff.defvjp(_bdot_fwd, _bdot_bwd)


def _bdot(d):
    return _bdot_diff if d else _bdot_impl


def _shift_impl(x, k):
    return jnp.where(_iota(x.shape, 0) >= k, pltpu.roll(x, k, 0), 0.0)


def _unshift_impl(g, k):
    n = g.shape[0]
    return jnp.where(_iota(g.shape, 0) < n - k, pltpu.roll(g, n - k, 0), 0.0)


@functools.partial(jax.custom_vjp, nondiff_argnums=(1,))
def _shift_diff(x, k):
    return _shift_impl(x, k)


_shift_diff.defvjp(lambda x, k: (_shift_impl(x, k), None), lambda k, _, g: (_unshift_impl(g, k),))


def _row(w, j):
    return jnp.sum(jnp.where(_iota(w.shape, 0) == j, w, 0.0), axis=0, keepdims=True)


def _col(w, j):
    return jnp.sum(jnp.where(_iota(w.shape, 1) == j, w, 0.0), axis=1, keepdims=True)


def _conv(d, x, w):
    shift = _shift_diff if d else _shift_impl
    taps = w.shape[0]
    y = x * _row(w, taps - 1)
    for j in range(taps - 1):
        y = y + shift(x, taps - 1 - j) * _row(w, j)
    return y


def _softplus(x):
    return jnp.maximum(x, 0.0) + jnp.log(1.0 + jnp.exp(-jnp.abs(x)))


def _silu(x):
    return x * jax.nn.sigmoid(x)


def _rms(x, gain):
    return x * lax.rsqrt(jnp.mean(x * x, axis=-1, keepdims=True) + EPS) * gain


def _rms_fn(d, pids, x, gain):
    return (_rms(x, gain),)


def _loss_fn(d, pids, y, t):
    e = y - t
    part = 0.5 / D_MODEL * jnp.sum(e * e, keepdims=True)
    return e * (1.0 / D_MODEL), jnp.broadcast_to(part, (8, LANES))


def _fox_prep_fn(d, pids, q, k, gq, gk):
    first = _iota(q.shape, 1) < FOX_DH

    def norm(x, gain):
        sq = x * x
        ss_a = jnp.sum(jnp.where(first, sq, 0.0), axis=1, keepdims=True)
        ss_b = jnp.sum(jnp.where(first, 0.0, sq), axis=1, keepdims=True)
        rs = jnp.where(first, lax.rsqrt(ss_a / FOX_DH + EPS), lax.rsqrt(ss_b / FOX_DH + EPS))
        return x * rs * gain

    return norm(q, gq) * FOX_DH ** -0.5, norm(k, gk)


def _fox_gate_fn(d, pids, f, bias):
    logf = -_softplus(-(f + bias))
    n_r, n_c = logf.shape
    tri = (_iota((n_c, n_c), 0) <= _iota((n_c, n_c), 1)).astype(F32)
    within = _dg(logf, tri, "nn", HI)
    tot = jnp.broadcast_to(jnp.sum(logf, axis=1, keepdims=True), logf.shape)
    below = (_iota((n_r, n_r), 1) < _iota((n_r, n_r), 0)).astype(F32)
    return (within + _dg(below, tot, "nn", HI),)


def _fox_attn_fn(q_block0, d, pids, q, k, v, cq_a, cq_b, ck_a, ck_b):
    dot = _bdot(d)
    first = _iota(q.shape, 1) < FOX_DH
    n_q, n_k = q.shape[0], k.shape[0]
    causal = ((q_block0 + pids[1]) * n_q + _iota((n_q, n_k), 0)) >= _iota((n_q, n_k), 1)

    qs = [jnp.where(first, q, 0.0), jnp.where(first, 0.0, q)]
    s = _each(lambda qh, cq, ck: jnp.where(causal, dot(qh, k, "nt") + cq - ck, -1e30), qs, [cq_a, cq_b], [ck_a, ck_b])
    e = [jnp.exp(si - lax.stop_gradient(jnp.max(si, axis=1, keepdims=True))) for si in s]
    o_a, o_b = [dot(ei / jnp.sum(ei, axis=1, keepdims=True), v, "nn") for ei in e]
    return (jnp.where(first, o_a, o_b),)


def _sconv_fn(d, pids, sb, sc, sv, w):
    return (sb * _conv(d, sc * sv, w),)


def _dnconv_fn(d, pids, x, w):
    return (_silu(_conv(d, x, w)),)


def _merge_fn(d, pids, y0, y1, y2, g0, g1, g2):
    return (jax.nn.sigmoid(g0) * y0 + jax.nn.sigmoid(g1) * y1 + jax.nn.sigmoid(g2) * y2,)


def _ffn_act_fn(d, pids, ug, uv, wg, wv):
    return (_silu(_conv(d, ug, wg)) * _conv(d, uv, wv),)


def _ple_fn(d, pids, gpre, pe, x):
    return (x + jax.nn.sigmoid(gpre) * pe,)


def _adam_fn(d, pids, w, g, m, v):
    m2 = ADAM_B1 * m + (1.0 - ADAM_B1) * g
    v2 = ADAM_B2 * v + (1.0 - ADAM_B2) * (g * g)
    m_hat = m2 / (1.0 - ADAM_B1 ** ADAM_STEP)
    v_hat = v2 / (1.0 - ADAM_B2 ** ADAM_STEP)
    delta = -ADAM_LR * (m_hat / (jnp.sqrt(v_hat) + ADAM_EPS) + ADAM_WD * w)
    return delta, m2, v2


def _each(fn, *lists):
    return [fn(*args) for args in zip(*lists)]


def _tri_inv_impl(mats):
    n = mats[0].shape[0]
    r, c = _iota((n, n), 0), _iota((n, n), 1)
    diag_blk = (r >> 4) == (c >> 4)
    eye = (r == c).astype(F32)
    mm = lambda us, ws: _each(lambda u, w: _dg(u, w, "nn", HI), us, ws)
    grow = lambda ps, xs: _each(lambda p, px: p + px, ps, mm(ps, xs))
    x = [jnp.where(diag_blk, -a, 0.0) for a in mats]
    p = [eye + xi for xi in x]
    x2 = mm(x, x)
    p = grow(p, x2)
    x4 = mm(x2, x2)
    p = grow(p, x4)
    p = grow(p, mm(x4, x4))
    y = [-yi for yi in mm(p, [jnp.where(diag_blk, 0.0, a) for a in mats])]
    q = grow([eye + yi for yi in y], mm(y, y))
    return mm(q, p)


@jax.custom_vjp
def _tri_inv_diff(mats):
    return _tri_inv_impl(mats)


def _tri_inv_fwd(mats):
    ts = _tri_inv_impl(mats)
    return ts, ts


def _tri_inv_bwd(ts, gs):
    left = _each(lambda t, g: _dg(t, g, "tn", HI), ts, gs)
    return ([-m for m in _each(lambda l, t: _dg(l, t, "nt", HI), left, ts)],)


_tri_inv_diff.defvjp(_tri_inv_fwd, _tri_inv_bwd)


def _dn_local(d, qs, ks, vs, a_cs, a_rs, b_cs, a_logs, dt_bs):
    dot = _bdot(d)
    inv = _tri_inv_diff if d else _tri_inv_impl
    n = qs[0].shape[0]
    r, c = _iota((n, n), 0), _iota((n, n), 1)
    incl, strict, upper = r >= c, r > c, r <= c
    qs = [q * lax.rsqrt(jnp.sum(q * q, axis=1, keepdims=True) + EPS) * DN_DH ** -0.5 for q in qs]
    ks = [k * lax.rsqrt(jnp.sum(k * k, axis=1, keepdims=True) + EPS) for k in ks]
    betas = [jax.nn.sigmoid(b) for b in b_cs]
    rates = [-jnp.exp(a) for a in a_logs]
    g_cs = _each(lambda rate, a, dt: rate * _softplus(a + dt), rates, a_cs, dt_bs)
    g_rs = _each(lambda rate, a, dt: rate * _softplus(a + dt), rates, a_rs, dt_bs)
    gcum_cs = [jnp.sum(jnp.where(incl, g, 0.0), axis=1, keepdims=True) for g in g_rs]
    gcum_rs = [jnp.sum(jnp.where(upper, g, 0.0), axis=0, keepdims=True) for g in g_cs]
    decays = _each(lambda gc, gr: jnp.exp(jnp.where(incl, gc - gr, -1e30)), gcum_cs, gcum_rs)
    kbs = _each(lambda k, b: k * b, ks, betas)
    kk = _each(lambda kb, k: dot(kb, k, "nt"), kbs, ks)
    ts = inv(_each(lambda m, dec: jnp.where(strict, m * dec, 0.0), kk, decays))
    e_gs = [jnp.exp(g) for g in gcum_cs]
    us = _each(lambda t, v, b: _dg(t, v * b, "nn", HI), ts, vs, betas)
    k_cums = _each(lambda t, kb, e: _dg(t, kb * e, "nn", HI), ts, kbs, e_gs)
    qk = _each(lambda q, k: dot(q, k, "nt"), qs, ks)
    qk = _each(lambda m, dec: jnp.where(incl, m * dec, 0.0), qk, decays)
    g_lasts = [jnp.sum(g, axis=0, keepdims=True) for g in g_cs]
    q_decs = _each(lambda q, e: q * e, qs, e_gs)
    k_decs = _each(lambda k, gl, gc: k * jnp.exp(gl - gc), ks, g_lasts, gcum_cs)
    return list(zip(us, k_cums, q_decs, k_decs, qk, g_lasts))


def _dn_step(d, s_prevs, items, zs, gain):
    dot = _bdot(d)
    us, k_cums, q_decs, k_decs, qks, g_lasts = [list(t) for t in zip(*items)]
    v_news = _each(lambda u, kc, s: u - dot(kc, s, "nn"), us, k_cums, s_prevs)
    inter = _each(lambda qd, s: dot(qd, s, "nn"), q_decs, s_prevs)
    outs = _each(lambda o, qk, vn: o + dot(qk, vn, "nn"), inter, qks, v_news)
    s_nexts = _each(lambda s, gl, kd, vn: s * jnp.exp(gl) + dot(kd, vn, "tn"), s_prevs, g_lasts, k_decs, v_news)
    return _each(lambda o, z: _rms(o, gain) * _silu(z), outs, zs), s_nexts


def _split_heads(t):
    return [t[:, h * DN_DH:(h + 1) * DN_DH] for h in range(t.shape[1] // DN_DH)]


def _dn_gates(ps, a_rows, ad):
    hs = range(DN_HEADS)
    return ([_col(ps, 12 + h) for h in hs], [_row(a_rows, h) for h in hs], [_col(ps, 8 + h) for h in hs],
            [_col(_row(ad, 0), h) for h in hs], [_col(_row(ad, 1), h) for h in hs])


def _head_rows(vals):
    row = _iota((8, LANES), 0)
    tile = jnp.zeros((8, LANES), F32)
    for h, val in enumerate(vals):
        tile = tile + jnp.where(row == h, val, 0.0)
    return tile


def _cparams(n_axes):
    return pltpu.CompilerParams(dimension_semantics=("arbitrary",) * n_axes, vmem_limit_bytes=VMEM_LIMIT)


def _first_visit(acc_axes):
    cond = None
    for a in acc_axes:
        here = pl.program_id(a) == 0
        cond = here if cond is None else jnp.logical_and(cond, here)
    return cond


def _tile(ref, widen=False):
    val = ref[...]
    shape = val.shape
    while len(shape) > 2 and shape[0] == 1:
        shape = shape[1:]
    val = val.reshape(shape)
    return val.astype(F32) if widen and val.dtype == BF16 else val


def _store(ref, val, first):
    val = val.astype(ref.dtype).reshape(ref.shape)
    if first is None:
        ref[...] = val
        return

    @pl.when(first)
    def _():
        ref[...] = val

    @pl.when(jnp.logical_not(first))
    def _():
        ref[...] += val


def _specs(ops):
    return [pl.BlockSpec(block, imap) for _, block, imap in ops]


def tile_fwd(name, fn, grid, ins, outs, raw=()):
    n_in = len(ins)

    def body(*refs):
        pids = tuple(pl.program_id(a) for a in range(len(grid)))
        firsts = [_first_visit(o[4]) if o[4] else None for o in outs]
        res = fn(False, pids, *[_tile(r, i not in raw) for i, r in enumerate(refs[:n_in])])
        for ref, val, first in zip(refs[n_in:], res, firsts):
            _store(ref, val, first)

    out = pl.pallas_call(
        body, grid=grid, in_specs=_specs(ins),
        out_specs=[pl.BlockSpec(o[2], o[3]) for o in outs],
        out_shape=[jax.ShapeDtypeStruct(o[0], o[1]) for o in outs],
        name=name, compiler_params=_cparams(len(grid)),
    )(*[a for a, _, _ in ins])
    return out


def tile_bwd(name, fn, grid, ins, cots, diff, adds=None, raw=()):
    adds = adds or {}
    n_in, n_cot = len(ins), len(cots)
    add_pos = sorted(adds)
    diff_idx = [d[0] for d in diff]
    out_desc = [d[2] if len(d) > 2 and d[2] is not None else (ins[d[0]][0].shape, ins[d[0]][1], ins[d[0]][2]) for d in diff]
    out_dtypes = [d[3] if len(d) > 3 else F32 for d in diff]

    def body(*refs):
        pids = tuple(pl.program_id(a) for a in range(len(grid)))
        firsts = [_first_visit(d[1]) if d[1] else None for d in diff]
        vals = [_tile(r, i not in raw) for i, r in enumerate(refs[:n_in])]
        cot_vals = [_tile(r, True) for r in refs[n_in:n_in + n_cot]]
        add_vals = [_tile(r) for r in refs[n_in + n_cot:n_in + n_cot + len(add_pos)]]
        out_refs = refs[n_in + n_cot + len(add_pos):]

        def f(*dv):
            full = list(vals)
            for i, val in zip(diff_idx, dv):
                full[i] = val
            return fn(True, pids, *full)

        prim, vjp = jax.vjp(f, *[vals[i].astype(F32) for i in diff_idx])
        grads = list(vjp(tuple(c.astype(o.dtype) for c, o in zip(cot_vals, prim))))
        for pos, val in zip(add_pos, add_vals):
            grads[pos] = grads[pos] + val.astype(F32)
        for ref, val, first in zip(out_refs, grads, firsts):
            _store(ref, val, first)

    all_ins = list(ins) + list(cots) + [adds[p] for p in add_pos]
    out = pl.pallas_call(
        body, grid=grid, in_specs=_specs(all_ins),
        out_specs=[pl.BlockSpec(o[1], o[2]) for o in out_desc],
        out_shape=[jax.ShapeDtypeStruct(o[0], dt) for o, dt in zip(out_desc, out_dtypes)],
        name=name, compiler_params=_cparams(len(grid)),
    )(*[a for a, _, _ in all_ins])
    return out


def _pick(dim, cands):
    for c in cands:
        if dim % c == 0:
            return c
    return dim


MM_TILES = (1024, 512, 1408, 256, 128)


def mm(name, a, b, mode, add=None, out_dtype=F32, blocks=None):
    wide = None
    if mode == "nn":
        (m, kk), n = a.shape, b.shape[-1]
    elif mode == "nt":
        (m, kk), n = a.shape, b.shape[-2]
    else:
        (kk, m), n = a.shape, b.shape[1]
    if blocks is not None:
        lo, n_blk = blocks
        wide = b.shape[-1] if mode != "tn" else n // n_blk
        if mode == "nn":
            n = wide * n_blk
    tm = _pick(m, MM_TILES)
    if mode == "nt" and blocks is not None:
        tn, tk = _pick(n, MM_TILES), _pick(wide, MM_TILES[:-1])
    elif blocks is not None:
        tn, tk = _pick(wide, MM_TILES[:-1]), _pick(kk, MM_TILES)
    else:
        tn, tk = _pick(n, MM_TILES), _pick(kk, MM_TILES)
    nk = kk // tk
    a_spec = pl.BlockSpec((tk, tm), lambda i, j, k: (k, i)) if mode == "tn" else pl.BlockSpec((tm, tk), lambda i, j, k: (i, k))
    o_spec = pl.BlockSpec((tm, tn), lambda i, j, k: (i, j))
    out_shape = (m, n)
    if blocks is None:
        b_spec = pl.BlockSpec((tn, tk), lambda i, j, k: (j, k)) if mode == "nt" else pl.BlockSpec((tk, tn), lambda i, j, k: (k, j))
    elif mode == "nn":
        per = wide // tn
        b_spec = pl.BlockSpec((1, tk, tn), lambda i, j, k: (lo + j // per, k, j % per))
    elif mode == "nt":
        per = wide // tk
        b_spec = pl.BlockSpec((1, tn, tk), lambda i, j, k: (lo + k // per, j, k % per))
    else:
        per = wide // tn
        b_spec = pl.BlockSpec((tk, tn), lambda i, j, k: (k, j))
        o_spec = pl.BlockSpec((1, tm, tn), lambda i, j, k: (j // per, i, j % per))
        out_shape = (n_blk, m, wide)

    def body(*refs):
        a_ref, b_ref = refs[0], refs[1]
        add_ref = refs[2] if add is not None else None
        o_ref, acc = refs[-2], refs[-1]
        k = pl.program_id(2)
        part = _bdot_impl(_tile(a_ref), _tile(b_ref), mode)

        @pl.when(k == 0)
        def _():
            acc[...] = part

        @pl.when(k > 0)
        def _():
            acc[...] += part

        @pl.when(k == nk - 1)
        def _():
            res = acc[...]
            if add_ref is not None:
                res = res + add_ref[...]
            o_ref[...] = res.astype(o_ref.dtype).reshape(o_ref.shape)

    operands = [a, b] + ([add] if add is not None else [])
    in_specs = [a_spec, b_spec] + ([o_spec] if add is not None else [])
    return pl.pallas_call(
        body, grid=(m // tm, n // tn, nk), in_specs=in_specs, out_specs=o_spec,
        out_shape=jax.ShapeDtypeStruct(out_shape, out_dtype),
        scratch_shapes=[pltpu.VMEM((tm, tn), F32)],
        name=name, compiler_params=_cparams(3),
    )(*operands)


def _rows(x, width=None, off=0, tm=256):
    width = x.shape[1] if width is None else width
    return (x, (tm, width), lambda i, off=off: (i, off))


def _whole(x):
    nd = x.ndim
    return (x, x.shape, lambda *pids, nd=nd: (0,) * nd)


def _rms_ops(x, gain):
    return [_rows(x), _whole(gain)]


def rms_fwd(name, x, gain):
    s, dm = x.shape
    return tile_fwd(name, _rms_fn, (s // 256,), _rms_ops(x, gain), [((s, dm), BF16, (256, dm), lambda i: (i, 0), ())])[0]


def rms_bwd(name, x, gain, dh, dres):
    s = x.shape[0]
    return tile_bwd(name, _rms_fn, (s // 256,), _rms_ops(x, gain), [_rows(dh)], [(0, ()), (1, (0,))], adds={0: _rows(dres)})


def loss_call(y, t):
    s, dm = y.shape
    dy, part = tile_fwd("loss", _loss_fn, (s // 256,), [_rows(y), _rows(t)],
                        [((s, dm), F32, (256, dm), lambda i: (i, 0), ()), ((8, LANES), F32, (8, LANES), lambda i: (0, 0), (0,))])
    return dy, part[0, 0]


def _fox_prep_ops(pm, gq, gk):
    tm = 512
    return [(pm, (tm, LANES), lambda i, j: (i, C_FQ // LANES + j)), (pm, (tm, LANES), lambda i, j: (i, C_FK // LANES + j)),
            _whole(gq), _whole(gk)]


def fox_prep_fwd(name, pm, gq, gk):
    s = pm.shape[0]
    out = ((s, BRANCH), BF16, (512, LANES), lambda i, j: (i, j), ())
    return tile_fwd(name, _fox_prep_fn, (s // 512, 4), _fox_prep_ops(pm, gq, gk), [out, out])


def fox_prep_bwd(name, pm, gq, gk, dqn, dkn):
    s = pm.shape[0]
    cot = lambda g: (g, (512, LANES), lambda i, j: (i, j))
    own = ((s, BRANCH), (512, LANES), lambda i, j: (i, j))
    return tile_bwd(name, _fox_prep_fn, (s // 512, 4), _fox_prep_ops(pm, gq, gk), [cot(dqn), cot(dkn)],
                    [(0, (), own, BF16), (1, (), own, BF16), (2, (0, 1)), (3, (0, 1))])


def _fox_gate_ops(f_t, bias):
    return [(f_t, (1,) + f_t.shape[1:], lambda h: (h, 0, 0)), (bias, (1, 1, 1), lambda h: (h, 0, 0))]


def fox_gate_fwd(name, f_t, bias):
    n_h = f_t.shape[0]
    return tile_fwd(name, _fox_gate_fn, (n_h,), _fox_gate_ops(f_t, bias),
                    [(f_t.shape, F32, (1,) + f_t.shape[1:], lambda h: (h, 0, 0), ())])[0]


def fox_gate_bwd(name, f_t, bias, dcum):
    n_h = f_t.shape[0]
    return tile_bwd(name, _fox_gate_fn, (n_h,), _fox_gate_ops(f_t, bias),
                    [(dcum, (1,) + f_t.shape[1:], lambda h: (h, 0, 0))], [(0, ()), (1, ())])


FOX_GROUPS = 4


def _fox_groups(s):
    per = s // FOX_BLOCK // FOX_GROUPS
    return [(g * per, per, (g + 1) * per * FOX_BLOCK) for g in range(FOX_GROUPS)]


def _fox_attn_ops(qn, kn, pm, cum_c, cum_r, q0, keys):
    nb = FOX_BLOCK
    return [(qn, (nb, LANES), lambda p, i: (q0 + i, p)), (kn, (keys, LANES), lambda p, i: (0, p)),
            (pm, (keys, LANES), lambda p, i: (0, C_FV // LANES + p)),
            (cum_c, (1, nb, 1), lambda p, i: (2 * p, q0 + i, 0)), (cum_c, (1, nb, 1), lambda p, i: (2 * p + 1, q0 + i, 0)),
            (cum_r, (1, 1, keys), lambda p, i: (2 * p, 0, 0)), (cum_r, (1, 1, keys), lambda p, i: (2 * p + 1, 0, 0))]


def fox_attn_fwd(name, qn, kn, pm, cum_c, cum_r):
    s = qn.shape[0]
    parts = []
    for g, (q0, n_q, keys) in enumerate(_fox_groups(s)):
        parts.append(tile_fwd(f"{name}_g{g}", functools.partial(_fox_attn_fn, q0), (4, n_q), _fox_attn_ops(qn, kn, pm, cum_c, cum_r, q0, keys),
                              [((n_q * FOX_BLOCK, BRANCH), BF16, (FOX_BLOCK, LANES), lambda p, i: (i, p), ())], raw=(0, 1, 2))[0])
    return jnp.concatenate(parts, axis=0)


def fox_attn_bwd(name, qn, kn, pm, cum_c, cum_r, dy):
    s = qn.shape[0]
    d_qn, d_kn, d_v, d_cum = [], 0.0, 0.0, 0.0
    for g, (q0, n_q, keys) in enumerate(_fox_groups(s)):
        rows = n_q * FOX_BLOCK
        own_q = ((rows, BRANCH), (FOX_BLOCK, LANES), lambda p, i: (i, p))
        own_k = ((keys, BRANCH), (keys, LANES), lambda p, i: (0, p))
        pair_c = ((4, rows, 1), (1, FOX_BLOCK, 1), lambda p, i: (p, i, 0))
        pair_r = ((4, 1, keys), (1, 1, keys), lambda p, i: (p, 0, 0))
        g_qn, g_kn, g_v, g_cqa, g_cqb, g_cka, g_ckb = tile_bwd(
            f"{name}_g{g}", functools.partial(_fox_attn_fn, q0), (4, n_q), _fox_attn_ops(qn, kn, pm, cum_c, cum_r, q0, keys),
            [(dy, (FOX_BLOCK, LANES), lambda p, i, q0=q0: (q0 + i, p))],
            [(0, (), own_q), (1, (1,), own_k), (2, (1,), own_k), (3, (), pair_c), (4, (), pair_c), (5, (1,), pair_r), (6, (1,), pair_r)])
        d_qn.append(g_qn)
        tail = lambda t, axis: jnp.pad(t, [(0, s - keys) if ax == axis else (0, 0) for ax in range(t.ndim)])
        d_kn, d_v = d_kn + tail(g_kn, 0), d_v + tail(g_v, 0)
        by_q = jnp.stack([g_cqa[:, :, 0], g_cqb[:, :, 0]], axis=1).reshape(8, rows)
        by_k = jnp.stack([g_cka[:, 0, :], g_ckb[:, 0, :]], axis=1).reshape(8, keys)
        d_cum = d_cum + jnp.pad(by_q, [(0, 0), (q0 * FOX_BLOCK, s - q0 * FOX_BLOCK - rows)]) + tail(by_k, 1)
    return jnp.concatenate(d_qn, axis=0), d_kn, d_v, d_cum


def sconv_ops(pm, w):
    s = pm.shape[0]
    blk = lambda c0: (pm, (s, LANES), lambda j, c0=c0: (0, c0 // LANES + j))
    return [blk(C_SB), blk(C_SC), blk(C_SV), (w, (w.shape[0], LANES), lambda j: (0, j))]


def dnconv_ops(pm, w):
    s = pm.shape[0]
    return [(pm, (s, LANES), lambda j: (0, C_DN // LANES + j)), (w, (w.shape[0], LANES), lambda j: (0, j))]


def ffn_ops(ug, uv, w):
    s = ug.shape[0]
    n_t = D_FF // LANES
    return [(ug, (s, LANES), lambda j: (0, j)), (uv, (s, LANES), lambda j: (0, j)),
            (w, (w.shape[0], LANES), lambda j: (0, j)), (w, (w.shape[0], LANES), lambda j: (0, n_t + j))]


def _col_out(s, width, dtype=F32):
    return ((s, width), dtype, (s, LANES), lambda j: (0, j), ())


def _col_cot(g):
    return (g, (g.shape[0], LANES), lambda j: (0, j))


def merge_ops(yp, pm):
    gate = lambda b: (pm, (256, D_MODEL), lambda i, b=b: (i, C_GATE // D_MODEL + b))
    return [_rows(yp[0]), _rows(yp[1]), _rows(yp[2]), gate(0), gate(1), gate(2)]


def ple_ops(gpre, pe, x):
    return [_rows(gpre), _rows(pe), _rows(x)]


def adam_call(name, w, g, m, v):
    shape = w.shape
    last = shape[-1]
    rows = w.size // last
    flat = lambda t: t.reshape(rows, last)
    tm = rows
    for cand in (512, 256, 128, 64, 32, 16, 8):
        if rows % cand == 0 and cand * last * 4 <= 2 * 1024 * 1024:
            tm = cand
            break
    spec = lambda t: (flat(t), (tm, last), lambda i: (i, 0))
    out = ((rows, last), F32, (tm, last), lambda i: (i, 0), ())
    res = tile_fwd(name, _adam_fn, (rows // tm,), [spec(w), spec(g), spec(m), spec(v)], [out, out, out])
    return [r.reshape(shape) for r in res]


def _adam_layers_fn(d, pids, w, m, v, g0, g1):
    g = jnp.where(pids[0] == 0, g0, g1)
    return (g,) + _adam_fn(d, pids, w, g, m, v)


def adam_layers(name, w, m, v, g0, g1):
    _, rows, cols = w.shape
    tm = _row_tile(rows, cols)
    n_t = rows // tm
    lay = lambda t: (t, (1, tm, cols), lambda l, i: (l, i, 0))
    ins = [lay(w), lay(m), lay(v), (g0, (tm, cols), lambda l, i: (i * (1 - l) + (n_t - 1) * l, 0)), (g1, (tm, cols), lambda l, i: (i * l, 0))]
    out = (w.shape, F32, (1, tm, cols), lambda l, i: (l, i, 0), ())
    return tile_fwd(name, _adam_layers_fn, (2, n_t), ins, [out, out, out, out])


def adam_w_in(name, w, m, v, g0, g1):
    rows, n_l, cols = w.shape

    def body(w_ref, m_ref, v_ref, g0_ref, g1_ref, g_out, d_out, m_out, v_out):
        step = 64

        def update(at):
            for l, g_ref in enumerate((g0_ref, g1_ref)):
                g = g_ref[at, :]
                delta, m2, v2 = _adam_fn(False, None, w_ref[at, l, :], g, m_ref[at, l, :], v_ref[at, l, :])
                for ref, val in ((g_out, g), (d_out, delta), (m_out, m2), (v_out, v2)):
                    ref[at, l, :] = val

        def some_rows(i, carry):
            update(pl.ds(pl.multiple_of(i * step, step), step))
            return carry

        lax.fori_loop(0, rows // step, some_rows, 0)
        if rows % step:
            update(pl.ds(rows - rows % step, rows % step))

    both = pl.BlockSpec((rows, n_l, LANES), lambda j: (0, 0, j))
    one = pl.BlockSpec((rows, LANES), lambda j: (0, j))
    return pl.pallas_call(
        body, grid=(cols // LANES,), in_specs=[both, both, both, one, one], out_specs=[both] * 4,
        out_shape=[jax.ShapeDtypeStruct(w.shape, F32)] * 4, name=name, compiler_params=_cparams(1),
    )(w, m, v, g0, g1)


DN_GROUP = 4


def _dn_local_specs(rev_n=None):
    rows = DN_GROUP * DN_CHUNK
    idx = (lambda j: j) if rev_n is None else (lambda j: rev_n - 1 - j)
    return [pl.BlockSpec((rows, 3 * BRANCH), lambda j: (idx(j), 0)), pl.BlockSpec((rows, LANES), lambda j: (idx(j), 0)),
            pl.BlockSpec((DN_GROUP, DN_HEADS, DN_CHUNK), lambda j: (idx(j), 0, 0)), pl.BlockSpec((2, DN_HEADS), lambda j: (0, 0))]


def _dn_group_inputs(qkv, ps, a_rows, c):
    lo = c * DN_CHUNK
    heads = _split_heads(qkv[lo:lo + DN_CHUNK])
    return heads[0:4], heads[4:8], heads[8:12], ps[lo:lo + DN_CHUNK], a_rows[c]


def dn_local_fwd(name, dn_act, ps, a_rows, ad):
    s = dn_act.shape[0]
    n_c, n_g = s // DN_CHUNK, s // (DN_GROUP * DN_CHUNK)
    rows = DN_GROUP * DN_CHUNK

    def body(qkv_ref, ps_ref, ar_ref, ad_ref, u_ref, kc_ref, qd_ref, kd_ref, qk_ref, gl_ref):
        qkv, ps_v, a_rows_v, ad_v = qkv_ref[...], ps_ref[...], ar_ref[...], ad_ref[...]
        args = [[] for _ in range(8)]
        for c in range(DN_GROUP):
            q4, k4, v4, ps_c, ar_c = _dn_group_inputs(qkv, ps_v, a_rows_v, c)
            for lst, vals in zip(args, (q4, k4, v4) + _dn_gates(ps_c, ar_c, ad_v)):
                lst.extend(vals)
        everything = _dn_local(False, *args)
        for c in range(DN_GROUP):
            res = everything[c * DN_HEADS:(c + 1) * DN_HEADS]
            at = pl.ds(c * DN_CHUNK, DN_CHUNK)
            for ref, i in ((u_ref, 0), (kc_ref, 1), (qd_ref, 2), (kd_ref, 3)):
                ref[at, :] = jnp.concatenate([r[i] for r in res], axis=1)
            for h in range(DN_HEADS):
                qk_ref[c, h] = res[h][4]
            gl_ref[c] = _head_rows([r[5] for r in res])

    wide = pl.BlockSpec((rows, BRANCH), lambda j: (j, 0))
    return pl.pallas_call(
        body, grid=(n_g,), in_specs=_dn_local_specs(),
        out_specs=[wide, wide, wide, wide, pl.BlockSpec((DN_GROUP, DN_HEADS, DN_CHUNK, DN_CHUNK), lambda j: (j, 0, 0, 0)),
                   pl.BlockSpec((DN_GROUP, 8, LANES), lambda j: (j, 0, 0))],
        out_shape=[jax.ShapeDtypeStruct((s, BRANCH), F32)] * 4 + [jax.ShapeDtypeStruct((n_c, DN_HEADS, DN_CHUNK, DN_CHUNK), F32),
                                                                 jax.ShapeDtypeStruct((n_c, 8, LANES), F32)],
        name=name, compiler_params=_cparams(1),
    )(dn_act, ps, a_rows, ad)


def dn_local_bwd(name, dn_act, ps, a_rows, ad, cots):
    s = dn_act.shape[0]
    n_c, n_g = s // DN_CHUNK, s // (DN_GROUP * DN_CHUNK)
    rows = DN_GROUP * DN_CHUNK

    def body(qkv_ref, ps_ref, ar_ref, ad_ref, du_ref, dkc_ref, dqd_ref, dkd_ref, dqk_ref, dgl_ref, dqkv_ref, dps_ref, dar_ref, dad_ref):
        first = pl.program_id(0) == 0
        qkv, ps_v, a_rows_v, ad_v = qkv_ref[...], ps_ref[...], ar_ref[...], ad_ref[...]
        d_wide = [r[...] for r in (du_ref, dkc_ref, dqd_ref, dkd_ref)]
        qs, ks, vs, ps_cs, ar_cs, cot = [], [], [], [], [], []
        for c in range(DN_GROUP):
            q4, k4, v4, ps_c, ar_c = _dn_group_inputs(qkv, ps_v, a_rows_v, c)
            qs, ks, vs, ps_cs, ar_cs = qs + q4, ks + k4, vs + v4, ps_cs + [ps_c], ar_cs + [ar_c]
            lo = c * DN_CHUNK
            d_tiles = [_split_heads(t[lo:lo + DN_CHUNK]) for t in d_wide]
            d_gl = dgl_ref[c]
            cot += [(d_tiles[0][h], d_tiles[1][h], d_tiles[2][h], d_tiles[3][h], dqk_ref[c, h], _col(_row(d_gl, h), 0))
                    for h in range(DN_HEADS)]

        def f(qs, ks, vs, ps_cs, ar_cs, ad_v):
            gates = [[] for _ in range(5)]
            for ps_c, ar_c in zip(ps_cs, ar_cs):
                for lst, vals in zip(gates, _dn_gates(ps_c, ar_c, ad_v)):
                    lst.extend(vals)
            return _dn_local(True, qs, ks, vs, *gates)

        _, vjp = jax.vjp(f, qs, ks, vs, ps_cs, ar_cs, ad_v)
        d_q, d_k, d_v, d_ps, d_ar, d_ad = vjp(cot)
        for c in range(DN_GROUP):
            at, hs = pl.ds(c * DN_CHUNK, DN_CHUNK), slice(c * DN_HEADS, (c + 1) * DN_HEADS)
            dqkv_ref[at, :] = jnp.concatenate(d_q[hs] + d_k[hs] + d_v[hs], axis=1).astype(dqkv_ref.dtype)
            dps_ref[at, :] = d_ps[c]
            dar_ref[c] = d_ar[c]
        _store(dad_ref, d_ad, first)

    wide = pl.BlockSpec((rows, BRANCH), lambda j: (j, 0))
    specs = _dn_local_specs()
    return pl.pallas_call(
        body, grid=(n_g,),
        in_specs=specs + [wide, wide, wide, wide, pl.BlockSpec((DN_GROUP, DN_HEADS, DN_CHUNK, DN_CHUNK), lambda j: (j, 0, 0, 0)),
                          pl.BlockSpec((DN_GROUP, 8, LANES), lambda j: (j, 0, 0))],
        out_specs=specs,
        out_shape=[jax.ShapeDtypeStruct((s, 3 * BRANCH), F32), jax.ShapeDtypeStruct((s, LANES), F32),
                   jax.ShapeDtypeStruct((n_c, DN_HEADS, DN_CHUNK), F32), jax.ShapeDtypeStruct((2, DN_HEADS), F32)],
        name=name, compiler_params=_cparams(1),
    )(dn_act, ps, a_rows, ad, *cots)


def _dn_scan_specs(n_c, rev):
    idx = (lambda j: n_c - 1 - j) if rev else (lambda j: j)
    wide = pl.BlockSpec((DN_CHUNK, BRANCH), lambda j: (idx(j), 0))
    return [wide, wide, wide, wide, pl.BlockSpec((1, DN_HEADS, DN_CHUNK, DN_CHUNK), lambda j: (idx(j), 0, 0, 0)),
            pl.BlockSpec((1, 8, LANES), lambda j: (idx(j), 0, 0)), pl.BlockSpec((DN_CHUNK, BRANCH), lambda j: (idx(j), C_DZ // BRANCH)),
            pl.BlockSpec((1, DN_DH), lambda j: (0, 0))]


def _dn_scan_tiles(refs):
    u_ref, kc_ref, qd_ref, kd_ref, qk_ref, gl_ref, z_ref, g_ref = refs
    wide = [_split_heads(r[...]) for r in (u_ref, kc_ref, qd_ref, kd_ref)]
    gl = gl_ref[0]
    return [(wide[0][h], wide[1][h], wide[2][h], wide[3][h], qk_ref[0, h], _col(_row(gl, h), 0)) for h in range(DN_HEADS)], \
        _split_heads(z_ref[...].astype(F32)), g_ref[...]


def dn_scan_fwd(name, local, pm, gain):
    s = pm.shape[0]
    n_c = s // DN_CHUNK

    def body(*refs):
        y_ref, hist_ref, state = refs[8:]

        @pl.when(pl.program_id(0) == 0)
        def _():
            state[...] = jnp.zeros_like(state)

        hist_ref[0] = state[...]
        per_head, z4, gain_v = _dn_scan_tiles(refs[:8])
        ys, s_nexts = _dn_step(False, [state[h] for h in range(DN_HEADS)], per_head, z4, gain_v)
        for h in range(DN_HEADS):
            state[h] = s_nexts[h]
        y_ref[...] = jnp.concatenate(ys, axis=1).astype(y_ref.dtype)

    return pl.pallas_call(
        body, grid=(n_c,), in_specs=_dn_scan_specs(n_c, False),
        out_specs=[pl.BlockSpec((DN_CHUNK, BRANCH), lambda j: (j, 0)),
                   pl.BlockSpec((1, DN_HEADS, DN_DH, DN_DH), lambda j: (j, 0, 0, 0))],
        out_shape=[jax.ShapeDtypeStruct((s, BRANCH), BF16), jax.ShapeDtypeStruct((n_c, DN_HEADS, DN_DH, DN_DH), F32)],
        scratch_shapes=[pltpu.VMEM((DN_HEADS, DN_DH, DN_DH), F32)],
        name=name, compiler_params=_cparams(1),
    )(*local, pm, gain)


def dn_scan_bwd(name, local, pm, gain, hist, dy):
    s = pm.shape[0]
    n_c = s // DN_CHUNK

    def body(*refs):
        hist_ref, dy_ref = refs[8:10]
        du_ref, dkc_ref, dqd_ref, dkd_ref, dqk_ref, dgl_ref, dz_ref, dg_ref, d_state = refs[10:]
        first = pl.program_id(0) == 0

        @pl.when(first)
        def _():
            d_state[...] = jnp.zeros_like(d_state)

        per_head, z4, gain_v = _dn_scan_tiles(refs[:8])
        _, vjp = jax.vjp(functools.partial(_dn_step, True), [hist_ref[0, h] for h in range(DN_HEADS)], per_head, z4, gain_v)
        d_s, grads, d_z, d_gain = vjp((_split_heads(dy_ref[...].astype(F32)), [d_state[h] for h in range(DN_HEADS)]))
        for h in range(DN_HEADS):
            d_state[h] = d_s[h]
        for ref, i in ((du_ref, 0), (dkc_ref, 1), (dqd_ref, 2), (dkd_ref, 3)):
            ref[...] = jnp.concatenate([g[i] for g in grads], axis=1)
        dz_ref[...] = jnp.concatenate(d_z, axis=1).astype(dz_ref.dtype)
        for h in range(DN_HEADS):
            dqk_ref[0, h] = grads[h][4]
        dgl_ref[0] = _head_rows([g[5] for g in grads])
        _store(dg_ref, d_gain, first)

    rev = lambda j: n_c - 1 - j
    specs = _dn_scan_specs(n_c, True)
    return pl.pallas_call(
        body, grid=(n_c,),
        in_specs=specs + [pl.BlockSpec((1, DN_HEADS, DN_DH, DN_DH), lambda j: (rev(j), 0, 0, 0)),
                          pl.BlockSpec((DN_CHUNK, BRANCH), lambda j: (rev(j), 0))],
        out_specs=specs[:6] + [pl.BlockSpec((DN_CHUNK, BRANCH), lambda j: (rev(j), 0)), specs[7]],
        out_shape=[jax.ShapeDtypeStruct((s, BRANCH), F32)] * 4 + [
            jax.ShapeDtypeStruct((n_c, DN_HEADS, DN_CHUNK, DN_CHUNK), F32), jax.ShapeDtypeStruct((n_c, 8, LANES), F32),
            jax.ShapeDtypeStruct((s, BRANCH), BF16), jax.ShapeDtypeStruct((1, DN_DH), F32)],
        scratch_shapes=[pltpu.VMEM((DN_HEADS, DN_DH, DN_DH), F32)],
        name=name, compiler_params=_cparams(1),
    )(*local, pm, gain, hist, dy)


def _seq_layouts(cols, s):
    return cols.T.reshape(cols.shape[1], s // LANES, LANES)


def layer_fwd(li, x, p, w, more_weights=None):
    s = x.shape[0]
    n = lambda t: f"{t}_l{li}"
    h = rms_fwd(n("rms_mix"), x, w["g_mix"])
    pm = mm(n("in_main"), h, w["in_main"], "nt")
    ps = mm(n("in_small"), h, w["in_small"], "nt")
    qn, kn = fox_prep_fwd(n("fox_prep"), pm, w["gq"], w["gk"])
    f_t = _seq_layouts(ps[:, 0:8], s)
    cum = fox_gate_fwd(n("fox_gate"), f_t, w["b_f"])
    cum_c, cum_r = cum.reshape(8, s, 1), cum.reshape(8, 1, s)
    y_fox = fox_attn_fwd(n("fox_attn"), qn, kn, pm, cum_c, cum_r)
    y_sc = tile_fwd(n("sconv"), _sconv_fn, (BRANCH // LANES,), sconv_ops(pm, w["sc_conv_w"]), [_col_out(s, BRANCH, BF16)])[0]
    dn_act = tile_fwd(n("dnconv"), _dnconv_fn, (3 * BRANCH // LANES,), dnconv_ops(pm, w["dn_conv_w"]), [_col_out(s, 3 * BRANCH)])[0]
    a_rows = ps[:, 12:16].reshape(s // DN_CHUNK, DN_CHUNK, DN_HEADS).transpose(0, 2, 1)
    dn_local = dn_local_fwd(n("dn_local"), dn_act, ps, a_rows, w["ad"])
    y_dn, hist = dn_scan_fwd(n("dn_scan"), dn_local, pm, w["dn_gain"])
    ys = (y_fox, y_sc, y_dn)
    if more_weights is not None:
        w = {**w, **more_weights(y_dn)}
    yp = [mm(n(f"branch{b}"), ys[b], w["branch"][b], "nn", blocks=(0, N_CHIPS)) for b in range(3)]
    merged = tile_fwd(n("merge"), _merge_fn, (s // 256,), merge_ops(yp, pm), [((s, D_MODEL), BF16, (256, D_MODEL), lambda i: (i, 0), ())])[0]
    x1 = mm(n("w_o"), merged, w["o"], "nn", add=x)
    h2 = rms_fwd(n("rms_ffn"), x1, w["g_ffn"])
    ug = mm(n("up_g"), h2, w["up"], "nn", blocks=(0, 2))
    uv = mm(n("up_v"), h2, w["up"], "nn", blocks=(2, 2))
    act = tile_fwd(n("ffn_act"), _ffn_act_fn, (D_FF // LANES,), ffn_ops(ug, uv, w["ffn_conv_w"]), [_col_out(s, D_FF, BF16)])[0]
    x2 = mm(n("down"), act, w["down"], "nn", add=x1)
    h3 = rms_fwd(n("rms_ple"), x2, w["g_ple"])
    gpre = mm(n("ple_gate"), h3, w["pg"], "nn")
    pe = mm(n("ple_emb"), p, w["ple"], "nn", blocks=(0, N_CHIPS))
    x3 = tile_fwd(n("ple"), _ple_fn, (s // 256,), ple_ops(gpre, pe, x2), [((s, D_MODEL), F32, (256, D_MODEL), lambda i: (i, 0), ())])[0]
    saved = dict(x=x, h=h, pm=pm, ps=ps, qn=qn, kn=kn, f_t=f_t, cum_c=cum_c, cum_r=cum_r, ys=ys, dn_act=dn_act, dn_local=dn_local,
                 a_rows=a_rows, hist=hist, yp=yp, merged=merged, x1=x1, h2=h2, ug=ug, uv=uv, act=act, x2=x2, h3=h3,
                 gpre=gpre, pe=pe, p=p)
    return x3, saved, w


def hang_on(w, token):
    zero = token[0, 0]
    small = ("g_mix", "g_ffn", "g_ple", "gq", "gk", "b_f", "ad", "dn_gain", "sc_conv_w", "dn_conv_w", "ffn_conv_w")
    return {**w, **{k: w[k] + zero for k in small}}


def layer_bwd(li, dx3, sv, w, hooks=None):
    hooks = hooks or {}

    def stage(key, after, w):
        return hang_on(w, hooks[key](after, g)) if key in hooks else w

    s = dx3.shape[0]
    n = lambda t: f"{t}_l{li}"
    g = {}
    col_own = lambda width: ((s, width), (s, LANES), lambda j: (0, j))
    d_gpre, d_pe = tile_bwd(n("ple_bwd"), _ple_fn, (s // 256,), ple_ops(sv["gpre"], sv["pe"], sv["x2"]), [_rows(dx3)],
                            [(0, (), None, BF16), (1, (), None, BF16)])
    g["w_ple"] = mm(n("d_w_ple"), sv["p"], d_pe, "tn", blocks=(0, N_CHIPS))
    g["w_ple_gate"] = mm(n("d_w_pg"), sv["h3"], d_gpre, "tn").reshape(N_CHIPS, -1, D_MODEL)
    dh3 = mm(n("d_h3"), d_gpre, w["pg"], "nt")
    dx2, d_g_ple = rms_bwd(n("rms_ple_bwd"), sv["x2"], w["g_ple"], dh3, dx3)
    dact = mm(n("d_act"), dx2, w["down"], "nt")
    g["w_down"] = mm(n("d_w_down"), sv["act"], dx2, "tn").reshape(N_CHIPS, -1, D_MODEL)
    taps_own = ((w["ffn_conv_w"].shape[0], D_FF), (w["ffn_conv_w"].shape[0], LANES), lambda j: (0, j))
    d_ug, d_uv, d_fw_g, d_fw_v = tile_bwd(n("ffn_act_bwd"), _ffn_act_fn, (D_FF // LANES,), ffn_ops(sv["ug"], sv["uv"], w["ffn_conv_w"]),
                                          [_col_cot(dact)], [(0, (), None, BF16), (1, (), None, BF16), (2, (), taps_own), (3, (), taps_own)])
    g["ffn_conv_w"] = jnp.concatenate([d_fw_g, d_fw_v], axis=1)
    g["w_up"] = jnp.concatenate([mm(n("d_w_up_g"), sv["h2"], d_ug, "tn", blocks=(0, 2)), mm(n("d_w_up_v"), sv["h2"], d_uv, "tn", blocks=(0, 2))])
    dh2 = mm(n("d_h2_v"), d_uv, w["up"], "nt", blocks=(2, 2), add=mm(n("d_h2_g"), d_ug, w["up"], "nt", blocks=(0, 2)))
    dx1, d_g_ffn = rms_bwd(n("rms_ffn_bwd"), sv["x1"], w["g_ffn"], dh2, dx2)
    w = stage("mid", dx1, w)
    dmerged = mm(n("d_merged"), dx1, w["o"], "nt")
    g["w_o"] = mm(n("d_w_o"), sv["merged"], dx1, "tn").reshape(N_CHIPS, -1, D_MODEL)
    gate_own = ((s, D_MODEL), (256, D_MODEL), lambda i: (i, 0))
    d_yp0, d_yp1, d_yp2, d_g0, d_g1, d_g2 = tile_bwd(
        n("merge_bwd"), _merge_fn, (s // 256,), merge_ops(sv["yp"], sv["pm"]), [_rows(dmerged)],
        [(0, (), None, BF16), (1, (), None, BF16), (2, (), None, BF16), (3, (), gate_own, BF16), (4, (), gate_own, BF16), (5, (), gate_own, BF16)])
    d_yp = (d_yp0, d_yp1, d_yp2)
    g["w_branch"] = jnp.concatenate([mm(n(f"d_w_branch{b}"), sv["ys"][b], d_yp[b], "tn", blocks=(0, N_CHIPS)) for b in range(3)], axis=1)
    d_ys = [mm(n(f"d_y{b}"), d_yp[b], w["branch"][b], "nt", blocks=(0, N_CHIPS)) for b in range(3)]
    w = stage("late", d_ys[2], w)
    *d_local, d_z, d_dngain = dn_scan_bwd(n("dn_scan_bwd"), sv["dn_local"], sv["pm"], w["dn_gain"], sv["hist"], d_ys[2])
    d_dnact, d_ps_dn, d_arows, d_ad = dn_local_bwd(n("dn_local_bwd"), sv["dn_act"], sv["ps"], sv["a_rows"], w["ad"], d_local)
    g["ad"], g["dn_norm_gain"] = d_ad, d_dngain[0]
    d_dnqkv, g["dn_conv_w"] = tile_bwd(n("dnconv_bwd"), _dnconv_fn, (3 * BRANCH // LANES,), dnconv_ops(sv["pm"], w["dn_conv_w"]),
                                       [_col_cot(d_dnact)], [(0, (), col_own(3 * BRANCH), BF16), (1, ())])
    d_sb, d_sc, d_sv, g["sc_conv_w"] = tile_bwd(n("sconv_bwd"), _sconv_fn, (BRANCH // LANES,), sconv_ops(sv["pm"], w["sc_conv_w"]), [_col_cot(d_ys[1])],
                                                [(0, (), col_own(BRANCH), BF16), (1, (), col_own(BRANCH), BF16), (2, (), col_own(BRANCH), BF16), (3, ())])
    w = stage("last", d_dnqkv, w)
    d_qn, d_kn, d_fv, d_cum = fox_attn_bwd(n("fox_attn_bwd"), sv["qn"], sv["kn"], sv["pm"], sv["cum_c"], sv["cum_r"], d_ys[0])
    d_ft, d_bf = fox_gate_bwd(n("fox_gate_bwd"), sv["f_t"], w["b_f"], d_cum.reshape(8, s // LANES, LANES))
    g["b_fox_f"] = d_bf.reshape(8)
    d_fq, d_fk, d_gq, d_gk = fox_prep_bwd(n("fox_prep_bwd"), sv["pm"], w["gq"], w["gk"], d_qn, d_kn)
    g["fox_q_gain"] = d_gq[0, :FOX_DH] + d_gq[0, FOX_DH:]
    g["fox_k_gain"] = d_gk[0, :FOX_DH] + d_gk[0, FOX_DH:]
    d_pm = jnp.concatenate([d_fq, d_fk, d_fv.astype(BF16), d_sb, d_sc, d_sv, d_dnqkv, d_z, d_g0, d_g1, d_g2], axis=1)
    d_a_cols = d_arows.transpose(0, 2, 1).reshape(s, DN_HEADS)
    d_f_cols = d_ft.reshape(8, s).T
    d_ps = d_ps_dn + jnp.concatenate([d_f_cols, jnp.zeros((s, 4), F32), d_a_cols, jnp.zeros((s, LANES - 16), F32)], axis=1)
    g["w_in"] = chip_blocks_w_in(mm(n("d_w_in_main"), d_pm, sv["h"], "tn"), mm(n("d_w_in_small"), d_ps, sv["h"], "tn"))
    w = stage("w_in", g["w_in"], w)
    dh = mm(n("d_h_small"), d_ps, w["in_small"], "nn", add=mm(n("d_h_main"), d_pm, w["in_main"], "nn"))
    dx, d_g_mix = rms_bwd(n("rms_mix_bwd"), sv["x"], w["g_mix"], dh, dx1)
    g["g_mix"], g["g_ffn"], g["g_ple"] = d_g_mix[0], d_g_ffn[0], d_g_ple[0]
    return dx, g


IN_SHARD = 2052
MAIN_RANGES = ((0, 1536), (1544, 3080), (3080, 4616), (4624, 5136), (5136, 8208))
SMALL_RANGES = ((1536, 1544), (4616, 4620), (4620, 4624))


def _from_chip_blocks(blocks, ranges):
    parts = []
    for lo, hi in ranges:
        for k in range(N_CHIPS):
            a0, a1 = max(lo, k * IN_SHARD), min(hi, (k + 1) * IN_SHARD)
            if a0 < a1:
                parts.append(blocks[k][a0 - k * IN_SHARD:a1 - k * IN_SHARD])
    return parts


def split_w_in(blocks):
    main = jnp.concatenate(_from_chip_blocks(blocks, MAIN_RANGES), axis=0)
    pad = jnp.zeros((LANES - 16, blocks.shape[2]), blocks.dtype)
    return main, jnp.concatenate(_from_chip_blocks(blocks, SMALL_RANGES) + [pad], axis=0)


def chip_blocks_w_in(main, small):
    ranges = sorted([(lo, hi, "m") for lo, hi in MAIN_RANGES] + [(lo, hi, "s") for lo, hi in SMALL_RANGES])
    offs, m_off, s_off = {}, 0, 0
    for lo, hi in MAIN_RANGES:
        offs[lo] = m_off
        m_off += hi - lo
    for lo, hi in SMALL_RANGES:
        offs[lo] = s_off
        s_off += hi - lo
    blocks = []
    for k in range(N_CHIPS):
        parts = []
        for lo, hi, src in ranges:
            a0, a1 = max(lo, k * IN_SHARD), min(hi, (k + 1) * IN_SHARD)
            if a0 < a1:
                arr = main if src == "m" else small
                parts.append(arr[offs[lo] + a0 - lo:offs[lo] + a1 - lo])
        blocks.append(jnp.concatenate(parts, axis=0))
    return jnp.stack(blocks)


def later_weights(got):
    g_branch, g_o, g_up, g_down, g_pg, g_ple = got
    branch = g_branch.reshape(N_CHIPS, 3, BRANCH, -1)
    return dict(branch=[branch[:, b] for b in range(3)], o=g_o.reshape(D_MODEL, D_MODEL), up=g_up,
                down=g_down.reshape(D_FF, D_MODEL), pg=g_pg.reshape(D_MODEL, D_MODEL), ple=g_ple)


def layer_weights(li, got, conv, a):
    main, small = split_w_in(got[0])
    tile2 = lambda v: jnp.concatenate([v, v])[None, :]
    rest = later_weights(got[1:]) if len(got) > 1 else {}
    return dict(
        in_main=main, in_small=small, **rest,
        g_mix=a["g_mix"][li][None, :], g_ffn=a["g_ffn"][li][None, :], g_ple=a["g_ple"][li][None, :],
        gq=tile2(a["fox_q_gain"][li]), gk=tile2(a["fox_k_gain"][li]), b_f=a["b_fox_f"][li].reshape(8, 1, 1),
        ad=jnp.stack([a["dn_a_log"][li], a["dn_dt_bias"][li]]), dn_gain=a["dn_norm_gain"][li][None, :],
        sc_conv_w=conv["sc_conv_w"][li], dn_conv_w=conv["dn_conv_w"][li], ffn_conv_w=conv["ffn_conv_w"][li])


def pack_rows(arrs, dtype):
    flat = jnp.concatenate([t.reshape(-1).astype(dtype) for t in arrs])
    pad = (-flat.shape[0]) % (8 * LANES)
    if pad:
        flat = jnp.concatenate([flat, jnp.zeros((pad,), dtype)])
    return flat.reshape(-1, LANES)


def unpack_rows(buf, shapes):
    flat = buf.reshape(-1)
    out, off = [], 0
    for shp in shapes:
        size = 1
        for dim in shp:
            size *= dim
        out.append(flat[off:off + size].reshape(shp))
        off += size
    return out


def chip_shard(t, axis, k):
    width = t.shape[axis] // N_CHIPS
    return lax.slice_in_dim(t, k * width, (k + 1) * width, axis=axis)


ANY = pl.BlockSpec(memory_space=pl.ANY)


def _position():
    x, y, c = lax.axis_index("x"), lax.axis_index("y"), lax.axis_index("c")
    return x, y, c, [(1 - x, y), (x, 1 - y), (1 - x, 1 - y)]


def gather_small(name, block):
    m_per, n = block.shape

    def body(x_ref, out_ref, token, send_sems, recv_sems, local_sem):
        token[...] = jnp.zeros_like(token)
        x, y, c, chips = _position()
        me, sibling = (x, y, c), (x, y, 1 - c)

        def rows(px, py, pc):
            return out_ref.at[pl.ds((4 * px + 2 * py + pc) * m_per, m_per), :]

        def copy(k, blk, to, src=None):
            return pltpu.make_async_remote_copy(src_ref=rows(*blk) if src is None else src, dst_ref=rows(*blk),
                                                send_sem=send_sems.at[k], recv_sem=recv_sems.at[k], device_id=to, device_id_type=MESH)

        mine = pltpu.make_async_copy(x_ref, rows(*me), local_sem)
        mine.start()
        first = [copy(0, me, sibling, src=x_ref)] + [copy(1 + j, me, (*chip, c), src=x_ref) for j, chip in enumerate(chips)]
        for cp in first:
            cp.start()
        passed = [copy(4 + j, (*chip, c), sibling) for j, chip in enumerate(chips)]
        for j, chip in enumerate(chips):
            copy(1 + j, (*chip, c), me).wait_recv()
            passed[j].start()
        copy(0, sibling, me).wait_recv()
        for j, chip in enumerate(chips):
            copy(4 + j, (*chip, 1 - c), me).wait_recv()
        for cp in first + passed:
            cp.wait_send()
        mine.wait()

    in_vmem = pl.BlockSpec(memory_space=pltpu.VMEM)
    return pl.pallas_call(
        body, out_shape=[jax.ShapeDtypeStruct((8 * m_per, n), block.dtype), jax.ShapeDtypeStruct((8, LANES), F32)],
        in_specs=[in_vmem], out_specs=[in_vmem, in_vmem],
        scratch_shapes=[pltpu.SemaphoreType.DMA((7,)), pltpu.SemaphoreType.DMA((7,)), pltpu.SemaphoreType.DMA],
        name=name, compiler_params=pltpu.CompilerParams(vmem_limit_bytes=VMEM_LIMIT),
    )(block)


def _sems(n):
    return [pltpu.SemaphoreType.DMA((n,)), pltpu.SemaphoreType.DMA((n,))]


def _split_cols(rows):
    return (rows // 2) % 16 != 0


def _half(ref, which, lead=()):
    rows, cols = ref.shape[-2:]
    if _split_cols(rows):
        return ref.at[(*lead, slice(None), pl.ds(which * (cols // 2), cols // 2))]
    return ref.at[(*lead, pl.ds(which * (rows // 2), rows // 2), slice(None))]


def _half_shape(rows, cols):
    return (rows, cols // 2) if _split_cols(rows) else (rows // 2, cols)


def gather_layer(name, shards):
    n_w = len(shards)

    def body(*refs):
        ins, outs = refs[:n_w], refs[n_w:2 * n_w]
        token, send_sems, recv_sems = refs[2 * n_w:]
        token[...] = jnp.zeros_like(token)
        x, y, c, chips = _position()
        sibling = (x, y, 1 - c)

        def part(w, px, py, pc):
            return _half(outs[w], pc, (2 * px + py,))

        def copy(k, w, blk, to, src=None):
            return pltpu.make_async_remote_copy(src_ref=part(w, *blk) if src is None else src, dst_ref=part(w, *blk),
                                                send_sem=send_sems.at[k], recv_sem=recv_sems.at[k], device_id=to, device_id_type=MESH)

        pairs = [(w, j, chip) for w in range(n_w) for j, chip in enumerate(chips)]
        first = [copy(3 * w + j, w, (x, y, c), (*chip, c), src=_half(ins[w], c)) for w, j, chip in pairs]
        for cp in first:
            cp.start()
        passed = [copy(3 * n_w + 3 * w + j, w, (*chip, c), sibling) for w, j, chip in pairs]
        for (w, j, chip), fwd in zip(pairs, passed):
            copy(3 * w + j, w, (*chip, c), (x, y, c)).wait_recv()
            fwd.start()
        for w, j, chip in pairs:
            copy(3 * n_w + 3 * w + j, w, (*chip, 1 - c), (x, y, c)).wait_recv()
        for cp in first + passed:
            cp.wait_send()

    out = pl.pallas_call(
        body, out_shape=[jax.ShapeDtypeStruct((N_CHIPS,) + s.shape, s.dtype) for s in shards] + [jax.ShapeDtypeStruct((8, LANES), F32)],
        in_specs=[ANY] * n_w, out_specs=[ANY] * n_w + [pl.BlockSpec(memory_space=pltpu.VMEM)], scratch_shapes=_sems(6 * n_w), name=name,
    )(*shards)
    return out[:n_w], out[n_w]


def swap_halves(name, grads):
    n_w = len(grads)

    def body(*refs):
        ins, outs = refs[:n_w], refs[n_w:2 * n_w]
        send_sems, recv_sems = refs[2 * n_w:]
        x, y, c, _ = _position()
        cps = [pltpu.make_async_remote_copy(src_ref=_half(ins[w], 1 - c, (slice(None),)), dst_ref=outs[w],
                                            send_sem=send_sems.at[w], recv_sem=recv_sems.at[w], device_id=(x, y, 1 - c),
                                            device_id_type=MESH) for w in range(n_w)]
        for cp in cps:
            cp.start()
        for cp in cps:
            cp.wait()

    return pl.pallas_call(
        body, out_shape=[jax.ShapeDtypeStruct((N_CHIPS,) + _half_shape(*g.shape[1:]), g.dtype) for g in grads],
        in_specs=[ANY] * n_w, out_specs=[ANY] * n_w, scratch_shapes=_sems(n_w), name=name,
    )(*grads)


def scatter_chips(name, partials):
    n_w = len(partials)

    def body(*refs):
        ins, outs = refs[:n_w], refs[n_w:2 * n_w]
        send_sems, recv_sems = refs[2 * n_w:]
        x, y, c, chips = _position()
        cps = [pltpu.make_async_remote_copy(src_ref=ins[w].at[2 * cx + cy], dst_ref=outs[w].at[j], send_sem=send_sems.at[3 * w + j],
                                            recv_sem=recv_sems.at[3 * w + j], device_id=(cx, cy, c), device_id_type=MESH)
               for w in range(n_w) for j, (cx, cy) in enumerate(chips)]
        for cp in cps:
            cp.start()
        for cp in cps:
            cp.wait()

    return pl.pallas_call(
        body, out_shape=[jax.ShapeDtypeStruct((3,) + p.shape[1:], p.dtype) for p in partials],
        in_specs=[ANY] * n_w, out_specs=[ANY] * n_w, scratch_shapes=_sems(3 * n_w), name=name,
    )(*partials)


def share_halves(name, bufs):
    n_w = len(bufs)

    def body(*refs):
        outs = refs[n_w:2 * n_w]
        send_sems, recv_sems = refs[2 * n_w:]
        x, y, c, _ = _position()

        def copy(w, pc):
            half = _half(outs[w], pc)
            return pltpu.make_async_remote_copy(src_ref=half, dst_ref=half, send_sem=send_sems.at[w], recv_sem=recv_sems.at[w],
                                                device_id=(x, y, 1 - c), device_id_type=MESH)

        for w in range(n_w):
            copy(w, c).start()
        for w in range(n_w):
            copy(w, 1 - c).wait_recv()
            copy(w, c).wait_send()

    return pl.pallas_call(
        body, out_shape=[jax.ShapeDtypeStruct(b.shape, b.dtype) for b in bufs], in_specs=[ANY] * n_w, out_specs=[ANY] * n_w,
        input_output_aliases={w: w for w in range(n_w)}, scratch_shapes=_sems(n_w), name=name,
    )(*bufs)


HBM = pl.BlockSpec(memory_space=pltpu.HBM)
SEM = pl.BlockSpec(memory_space=pltpu.SEMAPHORE)
EFFECT = pltpu.SideEffectType.DATAFLOW_SIDE_EFFECTING


def _exchange_copies(kind, srcs, lands):
    x, y, c, chips = _position()
    out = []
    for src, land in zip(srcs, lands):
        if kind == "swap":
            out.append((_half(src, 1 - c, (slice(None),)), land, (x, y, 1 - c)))
            continue
        for j, (cx, cy) in enumerate(chips):
            if kind == "gather":
                out.append((src, land.at[2 * x + y], (cx, cy, c)))
            else:
                out.append((src.at[2 * cx + cy], land.at[j], (cx, cy, c)))
    return out


def _land_shapes(kind, srcs):
    if kind == "gather":
        return [(N_CHIPS,) + s.shape for s in srcs]
    if kind == "swap":
        return [(N_CHIPS,) + _half_shape(*s.shape[1:]) for s in srcs]
    return [(3,) + s.shape[1:] for s in srcs]


def exchange_start(name, kind, srcs):
    n_w = len(srcs)
    shapes = _land_shapes(kind, srcs)
    n_sem = n_w if kind == "swap" else 3 * n_w

    def body(*refs):
        ins, lands = refs[:n_w], refs[n_w:2 * n_w]
        send_sems, recv_sems = refs[2 * n_w:2 * n_w + 2]
        token = refs[-1]
        for i, (src, dst, dev) in enumerate(_exchange_copies(kind, ins, lands)):
            pltpu.make_async_remote_copy(src_ref=src, dst_ref=dst, send_sem=send_sems.at[i], recv_sem=recv_sems.at[i],
                                         device_id=dev, device_id_type=MESH).start()
        token[...] = jnp.zeros_like(token)

    out = pl.pallas_call(
        body, name=name,
        out_shape=(pltpu.SemaphoreType.DMA((n_sem,)), pltpu.SemaphoreType.DMA((n_sem,)),
                   *[pltpu.HBM(s.shape, s.dtype) for s in srcs], *[pltpu.HBM(shp, s.dtype) for shp, s in zip(shapes, srcs)],
                   jax.ShapeDtypeStruct((8, LANES), F32)),
        in_specs=(HBM,) * (2 * n_w), out_specs=(SEM, SEM) + (HBM,) * (2 * n_w) + (pl.BlockSpec(memory_space=pltpu.VMEM),),
        input_output_aliases={i: 2 + i for i in range(2 * n_w)},
        compiler_params=pltpu.CompilerParams(has_side_effects=EFFECT),
    )(*[pltpu.with_memory_space_constraint(s, pltpu.HBM) for s in srcs],
      *[pltpu.with_memory_space_constraint(lax.empty(shp, s.dtype), pltpu.HBM) for shp, s in zip(shapes, srcs)])
    return (kind, n_w, out[:-1]), out[-1]


def exchange_wait(name, handle, after):
    kind, n_w, (send_sems, recv_sems, *thru) = handle
    n_sem = n_w if kind == "swap" else 3 * n_w

    def body(*refs):
        ins, lands = refs[:n_w], refs[n_w:2 * n_w]
        send_sems, recv_sems = refs[2 * n_w:2 * n_w + 2]
        for i, (src, dst, dev) in enumerate(_exchange_copies(kind, ins, lands)):
            cp = pltpu.make_async_remote_copy(src_ref=src, dst_ref=dst, send_sem=send_sems.at[i], recv_sem=recv_sems.at[i],
                                              device_id=dev, device_id_type=MESH)
            cp.wait_send()
            cp.wait_recv()

    out = pl.pallas_call(
        body, name=name, out_shape=tuple(pltpu.HBM(t.shape, t.dtype) for t in thru),
        in_specs=(HBM,) * (2 * n_w) + (SEM, SEM, pl.BlockSpec(memory_space=pl.ANY)), out_specs=(HBM,) * (2 * n_w),
        input_output_aliases={i: i for i in range(2 * n_w)},
        compiler_params=pltpu.CompilerParams(has_side_effects=EFFECT),
    )(*thru, send_sems, recv_sems, after)
    return list(out[n_w:])


def _row_tile(rows, cols):
    best = rows
    if rows * cols * 4 <= 1024 * 1024:
        return rows
    for t in range(16, rows, 16):
        if rows % t == 0 and t * cols * 4 <= 1024 * 1024:
            best = t
    return best


def pair_sum(name, pos, grad, from_sibling):
    _, rows, cols = grad.shape
    h_rows, h_cols = _half_shape(rows, cols)
    tr = _row_tile(h_rows, h_cols)
    n_t = h_rows // tr

    def body(pos_ref, g_ref, s_ref, b_ref, f_ref):
        tot = g_ref[...] + s_ref[...]
        b_ref[...] = tot.astype(BF16)

        @pl.when(pl.program_id(1) == pos_ref[1])
        def _():
            f_ref[...] = tot[0]

    blk = pl.BlockSpec((1, tr, h_cols), lambda i, k, pos: (k, i, 0))
    if _split_cols(rows):
        mine = pl.BlockSpec((1, tr, h_cols), lambda i, k, pos: (k, i, pos[0]))
    else:
        mine = pl.BlockSpec((1, tr, h_cols), lambda i, k, pos: (k, pos[0] * n_t + i, 0))
    return pl.pallas_call(
        body, grid_spec=pltpu.PrefetchScalarGridSpec(
            num_scalar_prefetch=1, grid=(n_t, N_CHIPS), in_specs=[mine, blk],
            out_specs=[blk, pl.BlockSpec((tr, h_cols), lambda i, k, pos: (i, 0))]),
        out_shape=[jax.ShapeDtypeStruct((N_CHIPS, h_rows, h_cols), BF16), jax.ShapeDtypeStruct((h_rows, h_cols), F32)],
        name=name, compiler_params=_cparams(2),
    )(pos, grad, from_sibling)


def chip_sum(name, pos, own, landed, split_cols):
    half, cols = own.shape
    tr = _row_tile(half, cols)
    n_t = half // tr

    def body(pos_ref, p_ref, l_ref, o_ref):
        o_ref[...] = ((p_ref[...] + l_ref[0].astype(F32)) + l_ref[1].astype(F32)) + l_ref[2].astype(F32)

    if split_cols:
        out_spec, out_shape = pl.BlockSpec((tr, cols), lambda i, pos: (i, pos[0])), (half, 2 * cols)
    else:
        out_spec, out_shape = pl.BlockSpec((tr, cols), lambda i, pos: (pos[0] * n_t + i, 0)), (2 * half, cols)
    return pl.pallas_call(
        body, grid_spec=pltpu.PrefetchScalarGridSpec(
            num_scalar_prefetch=1, grid=(n_t,),
            in_specs=[pl.BlockSpec((tr, cols), lambda i, pos: (i, 0)), pl.BlockSpec((3, tr, cols), lambda i, pos: (0, i, 0))],
            out_specs=out_spec),
        out_shape=jax.ShapeDtypeStruct(out_shape, F32), name=name, compiler_params=_cparams(1),
    )(pos, own, landed)


def reduce_scatter_layer(tag, pos, grads):
    n = lambda t: f"{t}_{tag}"
    from_sibling = swap_halves(n("swap_halves"), grads)
    sums = [pair_sum(n(f"pair_sum{w}"), pos, g, s) for w, (g, s) in enumerate(zip(grads, from_sibling))]
    landed = scatter_chips(n("scatter_chips"), [b for b, _ in sums])
    halves = [chip_sum(n(f"chip_sum{w}"), pos, own, l, _split_cols(g.shape[1])) for w, ((_, own), l, g) in enumerate(zip(sums, landed, grads))]
    return share_halves(n("share_halves"), halves)


class OverlappedReduceScatter:
    def __init__(self, tag, pos, grads):
        self.n = lambda t: f"{t}_{tag}"
        self.pos, self.grads = pos, grads
        self.swap, self.token = exchange_start(self.n("swap_start"), "swap", grads)

    def middle(self, after):
        from_sibling = exchange_wait(self.n("swap_wait"), self.swap, after)
        self.sums = [pair_sum(self.n(f"pair_sum{w}"), self.pos, g, s) for w, (g, s) in enumerate(zip(self.grads, from_sibling))]
        self.scatter, self.token = exchange_start(self.n("scatter_start"), "scatter", [b for b, _ in self.sums])

    def finish(self, after):
        landed = exchange_wait(self.n("scatter_wait"), self.scatter, after)
        halves = [chip_sum(self.n(f"chip_sum{w}"), self.pos, own, l, _split_cols(g.shape[1]))
                  for w, ((_, own), l, g) in enumerate(zip(self.sums, landed, self.grads))]
        return share_halves(self.n("share_halves"), halves)


def sum_devices(gathered):
    m_per = gathered.shape[0] // 8

    def body(g_ref, o_ref):
        tot = g_ref[pl.ds(0, m_per), :]
        for dev in range(1, 8):
            tot = tot + g_ref[pl.ds(dev * m_per, m_per), :]
        o_ref[...] = tot

    return pl.pallas_call(
        body, out_shape=jax.ShapeDtypeStruct((m_per, gathered.shape[1]), F32),
        in_specs=[pl.BlockSpec(memory_space=pltpu.VMEM)], out_specs=pl.BlockSpec(memory_space=pltpu.VMEM), name="sum_devices",
    )(gathered)


def kernel(x, p, g_mix, w_in, b_fox_f, fox_q_gain, fox_k_gain, sc_conv_w, dn_conv_w, dn_a_log, dn_dt_bias, dn_norm_gain, w_branch, w_o, g_ffn, w_up, ffn_conv_w, w_down, g_ple, w_ple_gate, w_ple, loss_target, m_g_mix, m_w_in, m_b_fox_f, m_fox_q_gain, m_fox_k_gain, m_sc_conv_w, m_dn_conv_w, m_dn_a_log, m_dn_dt_bias, m_dn_norm_gain, m_w_branch, m_w_o, m_g_ffn, m_w_up, m_ffn_conv_w, m_w_down, m_g_ple, m_w_ple_gate, m_w_ple, v_g_mix, v_w_in, v_b_fox_f, v_fox_q_gain, v_fox_k_gain, v_sc_conv_w, v_dn_conv_w, v_dn_a_log, v_dn_dt_bias, v_dn_norm_gain, v_w_branch, v_w_o, v_g_ffn, v_w_up, v_ffn_conv_w, v_w_down, v_g_ple, v_w_ple_gate, v_w_ple):
    a = dict(g_mix=g_mix, w_in=w_in, b_fox_f=b_fox_f, fox_q_gain=fox_q_gain, fox_k_gain=fox_k_gain, sc_conv_w=sc_conv_w,
             dn_conv_w=dn_conv_w, dn_a_log=dn_a_log, dn_dt_bias=dn_dt_bias, dn_norm_gain=dn_norm_gain, w_branch=w_branch, w_o=w_o,
             g_ffn=g_ffn, w_up=w_up, ffn_conv_w=ffn_conv_w, w_down=w_down, g_ple=g_ple, w_ple_gate=w_ple_gate, w_ple=w_ple)
    mom = dict(g_mix=m_g_mix, w_in=m_w_in, b_fox_f=m_b_fox_f, fox_q_gain=m_fox_q_gain, fox_k_gain=m_fox_k_gain, sc_conv_w=m_sc_conv_w,
               dn_conv_w=m_dn_conv_w, dn_a_log=m_dn_a_log, dn_dt_bias=m_dn_dt_bias, dn_norm_gain=m_dn_norm_gain, w_branch=m_w_branch,
               w_o=m_w_o, g_ffn=m_g_ffn, w_up=m_w_up, ffn_conv_w=m_ffn_conv_w, w_down=m_w_down, g_ple=m_g_ple, w_ple_gate=m_w_ple_gate,
               w_ple=m_w_ple)
    var = dict(g_mix=v_g_mix, w_in=v_w_in, b_fox_f=v_b_fox_f, fox_q_gain=v_fox_q_gain, fox_k_gain=v_fox_k_gain, sc_conv_w=v_sc_conv_w,
               dn_conv_w=v_dn_conv_w, dn_a_log=v_dn_a_log, dn_dt_bias=v_dn_dt_bias, dn_norm_gain=v_dn_norm_gain, w_branch=v_w_branch,
               w_o=v_w_o, g_ffn=v_g_ffn, w_up=v_w_up, ffn_conv_w=v_ffn_conv_w, w_down=v_w_down, g_ple=v_g_ple, w_ple_gate=v_w_ple_gate,
               w_ple=v_w_ple)
    cx, cy, cc = lax.axis_index("x"), lax.axis_index("y"), lax.axis_index("c")
    chip = 2 * cx + cy
    pos = jnp.stack([cc, chip]).astype(jnp.int32)

    def as_blocks(t):
        return t.reshape(2, -1, t.shape[-1])

    def own_block_in(got, shards):
        return [lax.dynamic_update_slice(g, s[None], (chip, 0, 0)) for g, s in zip(got, shards)]

    conv_shapes = [a[nm].shape for nm in CONVS]
    conv_all, conv_token = gather_small("gather_conv_w", pack_rows([a[nm] for nm in CONVS], F32))
    def layer_block(nm, t, li):
        return jnp.transpose(t, (2, 0, 1))[:, li, :] if nm == "w_in" else as_blocks(t)[li]

    shards0 = [(layer_block(nm, a[nm], 0) + conv_token[0, 0]).astype(BF16) for nm in BIG]
    got0, gathered_token = gather_layer("gather_w_in_l0", shards0[:1])
    shards0[1:] = [s + gathered_token[0, 0].astype(BF16) for s in shards0[1:]]
    gather0, gather0_token = exchange_start("gather_start_l0", "gather", shards0[1:])
    shards1 = [(layer_block(nm, a[nm], 1) + gather0_token[0, 0]).astype(BF16) for nm in BIG]
    gather1, gather1_token = exchange_start("gather_start_l1", "gather", shards1)
    conv_rows = conv_all.shape[0] // 8
    conv_chip = [unpack_rows(conv_all[2 * k * conv_rows:(2 * k + 1) * conv_rows], conv_shapes) for k in range(N_CHIPS)]
    conv = {nm: jnp.concatenate([conv_chip[k][i] for k in range(N_CHIPS)], axis=2) for i, nm in enumerate(CONVS)}

    weights, saved = [None, None], [None, None]
    first_weights = hang_on(layer_weights(0, own_block_in(got0, shards0[:1]), conv, a), gather1_token)

    def rest_of_layer0(after):
        return later_weights(own_block_in(exchange_wait("gather_wait_l0", gather0, after), shards0[1:]))

    act, saved[0], weights[0] = layer_fwd(0, x[0], p[0, 0], first_weights, more_weights=rest_of_layer0)
    got1 = exchange_wait("gather_wait_l1", gather1, act)
    act, saved[1], weights[1] = layer_fwd(1, act, p[1, 0], layer_weights(1, own_block_in(got1, shards1), conv, a))
    d_act, loss_part = loss_call(act, loss_target[0])
    loss = lax.psum(loss_part, ("x", "y", "c"))
    layer_grads = [None, None]
    d_act, layer_grads[1] = layer_bwd(1, d_act, saved[1], weights[1])
    rs1 = OverlappedReduceScatter("l1", pos, [layer_grads[1][nm] for nm in BIG])
    rs0 = []

    def stage_mid(after, g):
        rs1.middle(after)
        return rs1.token

    def stage_late(after, g):
        rs0.append(OverlappedReduceScatter("l0", pos, [g[nm] for nm in BIG[1:]]))
        return rs0[0].token

    def stage_last(after, g):
        rs0[0].middle(after)
        return rs0[0].token

    def stage_w_in(after, g):
        rs0.append(OverlappedReduceScatter("w_in_l0", pos, [g["w_in"]]))
        return rs0[1].token

    d_act, layer_grads[0] = layer_bwd(0, d_act, saved[0], hang_on(weights[0], rs1.token),
                                      hooks=dict(mid=stage_mid, late=stage_late, last=stage_last, w_in=stage_w_in))
    rs0[1].middle(d_act)
    reduced = [rs0[0].finish(rs0[1].token), rs1.finish(rs0[1].token)]
    grad_x = d_act[None]

    def both(nm):
        return jnp.stack([layer_grads[0][nm], layer_grads[1][nm]])

    local = {nm: both(nm) for nm in ("g_mix", "b_fox_f", "fox_q_gain", "fox_k_gain", "dn_norm_gain", "g_ffn", "g_ple", "sc_conv_w",
                                      "dn_conv_w", "ffn_conv_w")}
    local["dn_a_log"] = jnp.stack([layer_grads[li]["ad"][0] for li in range(2)])
    local["dn_dt_bias"] = jnp.stack([layer_grads[li]["ad"][1] for li in range(2)])

    small_names = SMALL + CONVS
    small_shapes = [local[nm].shape for nm in small_names]
    small_sum = sum_devices(gather_small("gather_small_grads", pack_rows([local[nm] for nm in small_names], F32))[0])
    small_grads = dict(zip(small_names, unpack_rows(small_sum, small_shapes)))
    for nm in CONVS:
        width = a[nm].shape[2]
        small_grads[nm] = lax.dynamic_slice_in_dim(small_grads[nm], chip * width, width, axis=2)

    grads, deltas, new_m, new_v = dict(small_grads), {}, {}, {}
    for nm in small_names:
        deltas[nm], new_m[nm], new_v[nm] = adam_call(f"adam_{nm}", a[nm], grads[nm], mom[nm], var[nm])
    for i, nm in enumerate(BIG[1:]):
        res = adam_layers(f"adam_{nm}", as_blocks(a[nm]), as_blocks(mom[nm]), as_blocks(var[nm]), reduced[0][i], reduced[1][1 + i])
        grads[nm], deltas[nm], new_m[nm], new_v[nm] = [r.reshape(a[nm].shape) for r in res]
    stored = lambda t: jnp.transpose(t, (2, 0, 1))
    res = adam_w_in("adam_w_in", stored(a["w_in"]), stored(mom["w_in"]), stored(var["w_in"]), rs0[1].finish(deltas["w_ple"])[0], reduced[1][0])
    grads["w_in"], deltas["w_in"], new_m["w_in"], new_v["w_in"] = [jnp.transpose(r, (1, 2, 0)) for r in res]
    return (loss, grad_x, *[grads[nm] for nm in WEIGHTS], *[deltas[nm] for nm in WEIGHTS], *[new_m[nm] for nm in WEIGHTS],
            *[new_v[nm] for nm in WEIGHTS])
```

```python
import functools

import jax
import jax.numpy as jnp
from jax import lax
from jax.experimental import pallas as pl
from jax.experimental.pallas import tpu as pltpu

F32 = jnp.float32
BF16 = jnp.bfloat16
HI = lax.Precision.HIGHEST
MESH = pl.DeviceIdType.MESH

D_MODEL = 1024
BRANCH = 512
FOX_DH = 64
DN_DH = 128
DN_HEADS = 4
DN_CHUNK = 64
FOX_BLOCK = 128
D_FF = 2816
EPS = 1e-6
N_CHIPS = 4
LANES = 128

ADAM_LR, ADAM_B1, ADAM_B2, ADAM_EPS, ADAM_WD, ADAM_STEP = 0.001, 0.9, 0.999, 1e-08, 0.01, 10

VMEM_LIMIT = 56 * 1024 * 1024

C_FQ, C_FK, C_FV, C_SB, C_SC, C_SV, C_DN, C_DZ, C_GATE = 0, 512, 1024, 1536, 2048, 2560, 3072, 4608, 5120
IN_MAIN = 8192
IN_SIZES = (1536, 8, 1536, 1536, 4, 4, 512, 3072)

BIG = ("w_in", "w_branch", "w_o", "w_up", "w_down", "w_ple_gate", "w_ple")
BIG_AXIS = {"w_in": 2, "w_branch": 3, "w_o": 1, "w_up": 2, "w_down": 1, "w_ple_gate": 1, "w_ple": 2}
CONVS = ("sc_conv_w", "dn_conv_w", "ffn_conv_w")
SMALL = ("g_mix", "b_fox_f", "fox_q_gain", "fox_k_gain", "dn_a_log", "dn_dt_bias", "dn_norm_gain", "g_ffn", "g_ple")
WEIGHTS = ("g_mix", "w_in", "b_fox_f", "fox_q_gain", "fox_k_gain", "sc_conv_w", "dn_conv_w", "dn_a_log", "dn_dt_bias",
           "dn_norm_gain", "w_branch", "w_o", "g_ffn", "w_up", "ffn_conv_w", "w_down", "g_ple", "w_ple_gate", "w_ple")


def _iota(shape, dim):
    return lax.broadcasted_iota(jnp.int32, shape, dim)


def _dg(a, b, mode, prec=None):
    dims = {"nn": ((1,), (0,)), "nt": ((1,), (1,)), "tn": ((0,), (0,))}[mode]
    return lax.dot_general(a, b, (dims, ((), ())), precision=prec, preferred_element_type=F32)


def _bdot_impl(a, b, mode):
    return _dg(a.astype(BF16), b.astype(BF16), mode)


@functools.partial(jax.custom_vjp, nondiff_argnums=(2,))
def _bdot_diff(a, b, mode):
    return _bdot_impl(a, b, mode)


def _bdot_fwd(a, b, mode):
    return _bdot_impl(a, b, mode), (a, b)


def _bdot_bwd(mode, res, g):
    a, b = res
    if mode == "nn":
        da, db = _bdot_impl(g, b, "nt"), _bdot_impl(a, g, "tn")
    elif mode == "nt":
        da, db = _bdot_impl(g, b, "nn"), _bdot_impl(g, a, "tn")
    else:
        da, db = _bdot_impl(b, g, "nt"), _bdot_impl(a, g, "nn")
    return da.astype(a.dtype), db.astype(b.dtype)


_bdot_diff.defvjp(_bdot_fwd, _bdot_bwd)


def _bdot(d):
    return _bdot_diff if d else _bdot_impl


def _shift_impl(x, k):
    return jnp.where(_iota(x.shape, 0) >= k, pltpu.roll(x, k, 0), 0.0)


def _unshift_impl(g, k):
    n = g.shape[0]
    return jnp.where(_iota(g.shape, 0) < n - k, pltpu.roll(g, n - k, 0), 0.0)


@functools.partial(jax.custom_vjp, nondiff_argnums=(1,))
def _shift_diff(x, k):
    return _shift_impl(x, k)


_shift_diff.defvjp(lambda x, k: (_shift_impl(x, k), None), lambda k, _, g: (_unshift_impl(g, k),))


def _row(w, j):
    return jnp.sum(jnp.where(_iota(w.shape, 0) == j, w, 0.0), axis=0, keepdims=True)


def _col(w, j):
    return jnp.sum(jnp.where(_iota(w.shape, 1) == j, w, 0.0), axis=1, keepdims=True)


def _conv(d, x, w):
    shift = _shift_diff if d else _shift_impl
    taps = w.shape[0]
    y = x * _row(w, taps - 1)
    for j in range(taps - 1):
        y = y + shift(x, taps - 1 - j) * _row(w, j)
    return y


def _softplus(x):
    return jnp.maximum(x, 0.0) + jnp.log(1.0 + jnp.exp(-jnp.abs(x)))


def _silu(x):
    return x * jax.nn.sigmoid(x)


def _rms(x, gain):
    return x * lax.rsqrt(jnp.mean(x * x, axis=-1, keepdims=True) + EPS) * gain


def _rms_fn(d, pids, x, gain):
    return (_rms(x, gain),)


def _loss_fn(d, pids, y, t):
    e = y - t
    part = 0.5 / D_MODEL * jnp.sum(e * e, keepdims=True)
    return e * (1.0 / D_MODEL), jnp.broadcast_to(part, (8, LANES))


def _fox_prep_fn(d, pids, q, k, gq, gk):
    first = _iota(q.shape, 1) < FOX_DH

    def norm(x, gain):
        sq = x * x
        ss_a = jnp.sum(jnp.where(first, sq, 0.0), axis=1, keepdims=True)
        ss_b = jnp.sum(jnp.where(first, 0.0, sq), axis=1, keepdims=True)
        rs = jnp.where(first, lax.rsqrt(ss_a / FOX_DH + EPS), lax.rsqrt(ss_b / FOX_DH + EPS))
        return x * rs * gain

    return norm(q, gq) * FOX_DH ** -0.5, norm(k, gk)


def _fox_gate_fn(d, pids, f, bias):
    logf = -_softplus(-(f + bias))
    n_r, n_c = logf.shape
    tri = (_iota((n_c, n_c), 0) <= _iota((n_c, n_c), 1)).astype(F32)
    within = _dg(logf, tri, "nn", HI)
    tot = jnp.broadcast_to(jnp.sum(logf, axis=1, keepdims=True), logf.shape)
    below = (_iota((n_r, n_r), 1) < _iota((n_r, n_r), 0)).astype(F32)
    return (within + _dg(below, tot, "nn", HI),)


def _fox_attn_fn(q_block0, d, pids, q, k, v, cq_a, cq_b, ck_a, ck_b):
    dot = _bdot(d)
    first = _iota(q.shape, 1) < FOX_DH
    n_q, n_k = q.shape[0], k.shape[0]
    causal = ((q_block0 + pids[1]) * n_q + _iota((n_q, n_k), 0)) >= _iota((n_q, n_k), 1)

    qs = [jnp.where(first, q, 0.0), jnp.where(first, 0.0, q)]
    s = _each(lambda qh, cq, ck: jnp.where(causal, dot(qh, k, "nt") + cq - ck, -1e30), qs, [cq_a, cq_b], [ck_a, ck_b])
    e = [jnp.exp(si - lax.stop_gradient(jnp.max(si, axis=1, keepdims=True))) for si in s]
    o_a, o_b = [dot(ei / jnp.sum(ei, axis=1, keepdims=True), v, "nn") for ei in e]
    return (jnp.where(first, o_a, o_b),)


def _sconv_fn(d, pids, sb, sc, sv, w):
    return (sb * _conv(d, sc * sv, w),)


def _dnconv_fn(d, pids, x, w):
    return (_silu(_conv(d, x, w)),)


def _merge_fn(d, pids, y0, y1, y2, g0, g1, g2):
    return (jax.nn.sigmoid(g0) * y0 + jax.nn.sigmoid(g1) * y1 + jax.nn.sigmoid(g2) * y2,)


def _ffn_act_fn(d, pids, ug, uv, wg, wv):
    return (_silu(_conv(d, ug, wg)) * _conv(d, uv, wv),)


def _ple_fn(d, pids, gpre, pe, x):
    return (x + jax.nn.sigmoid(gpre) * pe,)


def _adam_fn(d, pids, w, g, m, v):
    m2 = ADAM_B1 * m + (1.0 - ADAM_B1) * g
    v2 = ADAM_B2 * v + (1.0 - ADAM_B2) * (g * g)
    m_hat = m2 / (1.0 - ADAM_B1 ** ADAM_STEP)
    v_hat = v2 / (1.0 - ADAM_B2 ** ADAM_STEP)
    delta = -ADAM_LR * (m_hat / (jnp.sqrt(v_hat) + ADAM_EPS) + ADAM_WD * w)
    return delta, m2, v2


def _each(fn, *lists):
    return [fn(*args) for args in zip(*lists)]


def _tri_inv_impl(mats):
    n = mats[0].shape[0]
    r, c = _iota((n, n), 0), _iota((n, n), 1)
    diag_blk = (r >> 4) == (c >> 4)
    eye = (r == c).astype(F32)
    mm = lambda us, ws: _each(lambda u, w: _dg(u, w, "nn", HI), us, ws)
    grow = lambda ps, xs: _each(lambda p, px: p + px, ps, mm(ps, xs))
    x = [jnp.where(diag_blk, -a, 0.0) for a in mats]
    p = [eye + xi for xi in x]
    x2 = mm(x, x)
    p = grow(p, x2)
    x4 = mm(x2, x2)
    p = grow(p, x4)
    p = grow(p, mm(x4, x4))
    y = [-yi for yi in mm(p, [jnp.where(diag_blk, 0.0, a) for a in mats])]
    q = grow([eye + yi for yi in y], mm(y, y))
    return mm(q, p)


@jax.custom_vjp
def _tri_inv_diff(mats):
    return _tri_inv_impl(mats)


def _tri_inv_fwd(mats):
    ts = _tri_inv_impl(mats)
    return ts, ts


def _tri_inv_bwd(ts, gs):
    left = _each(lambda t, g: _dg(t, g, "tn", HI), ts, gs)
    return ([-m for m in _each(lambda l, t: _dg(l, t, "nt", HI), left, ts)],)


_tri_inv_diff.defvjp(_tri_inv_fwd, _tri_inv_bwd)


def _dn_local(d, qs, ks, vs, a_cs, a_rs, b_cs, a_logs, dt_bs):
    dot = _bdot(d)
    inv = _tri_inv_diff if d else _tri_inv_impl
    n = qs[0].shape[0]
    r, c = _iota((n, n), 0), _iota((n, n), 1)
    incl, strict, upper = r >= c, r > c, r <= c
    qs = [q * lax.rsqrt(jnp.sum(q * q, axis=1, keepdims=True) + EPS) * DN_DH ** -0.5 for q in qs]
    ks = [k * lax.rsqrt(jnp.sum(k * k, axis=1, keepdims=True) + EPS) for k in ks]
    betas = [jax.nn.sigmoid(b) for b in b_cs]
    rates = [-jnp.exp(a) for a in a_logs]
    g_cs = _each(lambda rate, a, dt: rate * _softplus(a + dt), rates, a_cs, dt_bs)
    g_rs = _each(lambda rate, a, dt: rate * _softplus(a + dt), rates, a_rs, dt_bs)
    gcum_cs = [jnp.sum(jnp.where(incl, g, 0.0), axis=1, keepdims=True) for g in g_rs]
    gcum_rs = [jnp.sum(jnp.where(upper, g, 0.0), axis=0, keepdims=True) for g in g_cs]
    decays = _each(lambda gc, gr: jnp.exp(jnp.where(incl, gc - gr, -1e30)), gcum_cs, gcum_rs)
    kbs = _each(lambda k, b: k * b, ks, betas)
    kk = _each(lambda kb, k: dot(kb, k, "nt"), kbs, ks)
    ts = inv(_each(lambda m, dec: jnp.where(strict, m * dec, 0.0), kk, decays))
    e_gs = [jnp.exp(g) for g in gcum_cs]
    us = _each(lambda t, v, b: _dg(t, v * b, "nn", HI), ts, vs, betas)
    k_cums = _each(lambda t, kb, e: _dg(t, kb * e, "nn", HI), ts, kbs, e_gs)
    qk = _each(lambda q, k: dot(q, k, "nt"), qs, ks)
    qk = _each(lambda m, dec: jnp.where(incl, m * dec, 0.0), qk, decays)
    g_lasts = [jnp.sum(g, axis=0, keepdims=True) for g in g_cs]
    q_decs = _each(lambda q, e: q * e, qs, e_gs)
    k_decs = _each(lambda k, gl, gc: k * jnp.exp(gl - gc), ks, g_lasts, gcum_cs)
    return list(zip(us, k_cums, q_decs, k_decs, qk, g_lasts))


def _dn_step(d, s_prevs, items, zs, gain):
    dot = _bdot(d)
    us, k_cums, q_decs, k_decs, qks, g_lasts = [list(t) for t in zip(*items)]
    v_news = _each(lambda u, kc, s: u - dot(kc, s, "nn"), us, k_cums, s_prevs)
    inter = _each(lambda qd, s: dot(qd, s, "nn"), q_decs, s_prevs)
    outs = _each(lambda o, qk, vn: o + dot(qk, vn, "nn"), inter, qks, v_news)
    s_nexts = _each(lambda s, gl, kd, vn: s * jnp.exp(gl) + dot(kd, vn, "tn"), s_prevs, g_lasts, k_decs, v_news)
    return _each(lambda o, z: _rms(o, gain) * _silu(z), outs, zs), s_nexts


def _split_heads(t):
    return [t[:, h * DN_DH:(h + 1) * DN_DH] for h in range(t.shape[1] // DN_DH)]


def _dn_gates(ps, a_rows, ad):
    hs = range(DN_HEADS)
    return ([_col(ps, 12 + h) for h in hs], [_row(a_rows, h) for h in hs], [_col(ps, 8 + h) for h in hs],
            [_col(_row(ad, 0), h) for h in hs], [_col(_row(ad, 1), h) for h in hs])


def _head_rows(vals):
    row = _iota((8, LANES), 0)
    tile = jnp.zeros((8, LANES), F32)
    for h, val in enumerate(vals):
        tile = tile + jnp.where(row == h, val, 0.0)
    return tile


def _cparams(n_axes):
    return pltpu.CompilerParams(dimension_semantics=("arbitrary",) * n_axes, vmem_limit_bytes=VMEM_LIMIT)


def _first_visit(acc_axes):
    cond = None
    for a in acc_axes:
        here = pl.program_id(a) == 0
        cond = here if cond is None else jnp.logical_and(cond, here)
    return cond


def _tile(ref, widen=False):
    val = ref[...]
    shape = val.shape
    while len(shape) > 2 and shape[0] == 1:
        shape = shape[1:]
    val = val.reshape(shape)
    return val.astype(F32) if widen and val.dtype == BF16 else val


def _store(ref, val, first):
    val = val.astype(ref.dtype).reshape(ref.shape)
    if first is None:
        ref[...] = val
        return

    @pl.when(first)
    def _():
        ref[...] = val

    @pl.when(jnp.logical_not(first))
    def _():
        ref[...] += val


def _specs(ops):
    return [pl.BlockSpec(block, imap) for _, block, imap in ops]


def tile_fwd(name, fn, grid, ins, outs, raw=()):
    n_in = len(ins)

    def body(*refs):
        pids = tuple(pl.program_id(a) for a in range(len(grid)))
        firsts = [_first_visit(o[4]) if o[4] else None for o in outs]
        res = fn(False, pids, *[_tile(r, i not in raw) for i, r in enumerate(refs[:n_in])])
        for ref, val, first in zip(refs[n_in:], res, firsts):
            _store(ref, val, first)

    out = pl.pallas_call(
        body, grid=grid, in_specs=_specs(ins),
        out_specs=[pl.BlockSpec(o[2], o[3]) for o in outs],
        out_shape=[jax.ShapeDtypeStruct(o[0], o[1]) for o in outs],
        name=name, compiler_params=_cparams(len(grid)),
    )(*[a for a, _, _ in ins])
    return out


def tile_bwd(name, fn, grid, ins, cots, diff, adds=None, raw=()):
    adds = adds or {}
    n_in, n_cot = len(ins), len(cots)
    add_pos = sorted(adds)
    diff_idx = [d[0] for d in diff]
    out_desc = [d[2] if len(d) > 2 and d[2] is not None else (ins[d[0]][0].shape, ins[d[0]][1], ins[d[0]][2]) for d in diff]
    out_dtypes = [d[3] if len(d) > 3 else F32 for d in diff]

    def body(*refs):
        pids = tuple(pl.program_id(a) for a in range(len(grid)))
        firsts = [_first_visit(d[1]) if d[1] else None for d in diff]
        vals = [_tile(r, i not in raw) for i, r in enumerate(refs[:n_in])]
        cot_vals = [_tile(r, True) for r in refs[n_in:n_in + n_cot]]
        add_vals = [_tile(r) for r in refs[n_in + n_cot:n_in + n_cot + len(add_pos)]]
        out_refs = refs[n_in + n_cot + len(add_pos):]

        def f(*dv):
            full = list(vals)
            for i, val in zip(diff_idx, dv):
                full[i] = val
            return fn(True, pids, *full)

        prim, vjp = jax.vjp(f, *[vals[i].astype(F32) for i in diff_idx])
        grads = list(vjp(tuple(c.astype(o.dtype) for c, o in zip(cot_vals, prim))))
        for pos, val in zip(add_pos, add_vals):
            grads[pos] = grads[pos] + val.astype(F32)
        for ref, val, first in zip(out_refs, grads, firsts):
            _store(ref, val, first)

    all_ins = list(ins) + list(cots) + [adds[p] for p in add_pos]
    out = pl.pallas_call(
        body, grid=grid, in_specs=_specs(all_ins),
        out_specs=[pl.BlockSpec(o[1], o[2]) for o in out_desc],
        out_shape=[jax.ShapeDtypeStruct(o[0], dt) for o, dt in zip(out_desc, out_dtypes)],
        name=name, compiler_params=_cparams(len(grid)),
    )(*[a for a, _, _ in all_ins])
    return out


def _pick(dim, cands):
    for c in cands:
        if dim % c == 0:
            return c
    return dim


MM_TILES = (1024, 512, 1408, 256, 128)


def mm(name, a, b, mode, add=None, out_dtype=F32, blocks=None):
    wide = None
    if mode == "nn":
        (m, kk), n = a.shape, b.shape[-1]
    elif mode == "nt":
        (m, kk), n = a.shape, b.shape[-2]
    else:
        (kk, m), n = a.shape, b.shape[1]
    if blocks is not None:
        lo, n_blk = blocks
        wide = b.shape[-1] if mode != "tn" else n // n_blk
        if mode == "nn":
            n = wide * n_blk
    tm = _pick(m, MM_TILES)
    if mode == "nt" and blocks is not None:
        tn, tk = _pick(n, MM_TILES), _pick(wide, MM_TILES[:-1])
    elif blocks is not None:
        tn, tk = _pick(wide, MM_TILES[:-1]), _pick(kk, MM_TILES)
    else:
        tn, tk = _pick(n, MM_TILES), _pick(kk, MM_TILES)
    nk = kk // tk
    a_spec = pl.BlockSpec((tk, tm), lambda i, j, k: (k, i)) if mode == "tn" else pl.BlockSpec((tm, tk), lambda i, j, k: (i, k))
    o_spec = pl.BlockSpec((tm, tn), lambda i, j, k: (i, j))
    out_shape = (m, n)
    if blocks is None:
        b_spec = pl.BlockSpec((tn, tk), lambda i, j, k: (j, k)) if mode == "nt" else pl.BlockSpec((tk, tn), lambda i, j, k: (k, j))
    elif mode == "nn":
        per = wide // tn
        b_spec = pl.BlockSpec((1, tk, tn), lambda i, j, k: (lo + j // per, k, j % per))
    elif mode == "nt":
        per = wide // tk
        b_spec = pl.BlockSpec((1, tn, tk), lambda i, j, k: (lo + k // per, j, k % per))
    else:
        per = wide // tn
        b_spec = pl.BlockSpec((tk, tn), lambda i, j, k: (k, j))
        o_spec = pl.BlockSpec((1, tm, tn), lambda i, j, k: (j // per, i, j % per))
        out_shape = (n_blk, m, wide)

    def body(*refs):
        a_ref, b_ref = refs[0], refs[1]
        add_ref = refs[2] if add is not None else None
        o_ref, acc = refs[-2], refs[-1]
        k = pl.program_id(2)
        part = _bdot_impl(_tile(a_ref), _tile(b_ref), mode)

        @pl.when(k == 0)
        def _():
            acc[...] = part

        @pl.when(k > 0)
        def _():
            acc[...] += part

        @pl.when(k == nk - 1)
        def _():
            res = acc[...]
            if add_ref is not None:
                res = res + add_ref[...]
            o_ref[...] = res.astype(o_ref.dtype).reshape(o_ref.shape)

    operands = [a, b] + ([add] if add is not None else [])
    in_specs = [a_spec, b_spec] + ([o_spec] if add is not None else [])
    return pl.pallas_call(
        body, grid=(m // tm, n // tn, nk), in_specs=in_specs, out_specs=o_spec,
        out_shape=jax.ShapeDtypeStruct(out_shape, out_dtype),
        scratch_shapes=[pltpu.VMEM((tm, tn), F32)],
        name=name, compiler_params=_cparams(3),
    )(*operands)


def _rows(x, width=None, off=0, tm=256):
    width = x.shape[1] if width is None else width
    return (x, (tm, width), lambda i, off=off: (i, off))


def _whole(x):
    nd = x.ndim
    return (x, x.shape, lambda *pids, nd=nd: (0,) * nd)


def _rms_ops(x, gain):
    return [_rows(x), _whole(gain)]


def rms_fwd(name, x, gain):
    s, dm = x.shape
    return tile_fwd(name, _rms_fn, (s // 256,), _rms_ops(x, gain), [((s, dm), BF16, (256, dm), lambda i: (i, 0), ())])[0]


def rms_bwd(name, x, gain, dh, dres):
    s = x.shape[0]
    return tile_bwd(name, _rms_fn, (s // 256,), _rms_ops(x, gain), [_rows(dh)], [(0, ()), (1, (0,))], adds={0: _rows(dres)})


def loss_call(y, t):
    s, dm = y.shape
    dy, part = tile_fwd("loss", _loss_fn, (s // 256,), [_rows(y), _rows(t)],
                        [((s, dm), F32, (256, dm), lambda i: (i, 0), ()), ((8, LANES), F32, (8, LANES), lambda i: (0, 0), (0,))])
    return dy, part[0, 0]


def _fox_prep_ops(pm, gq, gk):
    tm = 512
    return [(pm, (tm, LANES), lambda i, j: (i, C_FQ // LANES + j)), (pm, (tm, LANES), lambda i, j: (i, C_FK // LANES + j)),
            _whole(gq), _whole(gk)]


def fox_prep_fwd(name, pm, gq, gk):
    s = pm.shape[0]
    out = ((s, BRANCH), BF16, (512, LANES), lambda i, j: (i, j), ())
    return tile_fwd(name, _fox_prep_fn, (s // 512, 4), _fox_prep_ops(pm, gq, gk), [out, out])


def fox_prep_bwd(name, pm, gq, gk, dqn, dkn):
    s = pm.shape[0]
    cot = lambda g: (g, (512, LANES), lambda i, j: (i, j))
    own = ((s, BRANCH), (512, LANES), lambda i, j: (i, j))
    return tile_bwd(name, _fox_prep_fn, (s // 512, 4), _fox_prep_ops(pm, gq, gk), [cot(dqn), cot(dkn)],
                    [(0, (), own, BF16), (1, (), own, BF16), (2, (0, 1)), (3, (0, 1))])


def _fox_gate_ops(f_t, bias):
    return [(f_t, (1,) + f_t.shape[1:], lambda h: (h, 0, 0)), (bias, (1, 1, 1), lambda h: (h, 0, 0))]


def fox_gate_fwd(name, f_t, bias):
    n_h = f_t.shape[0]
    return tile_fwd(name, _fox_gate_fn, (n_h,), _fox_gate_ops(f_t, bias),
                    [(f_t.shape, F32, (1,) + f_t.shape[1:], lambda h: (h, 0, 0), ())])[0]


def fox_gate_bwd(name, f_t, bias, dcum):
    n_h = f_t.shape[0]
    return tile_bwd(name, _fox_gate_fn, (n_h,), _fox_gate_ops(f_t, bias),
                    [(dcum, (1,) + f_t.shape[1:], lambda h: (h, 0, 0))], [(0, ()), (1, ())])


FOX_GROUPS = 4


def _fox_groups(s):
    per = s // FOX_BLOCK // FOX_GROUPS
    return [(g * per, per, (g + 1) * per * FOX_BLOCK) for g in range(FOX_GROUPS)]


def _fox_attn_ops(qn, kn, pm, cum_c, cum_r, q0, keys):
    nb = FOX_BLOCK
    return [(qn, (nb, LANES), lambda p, i: (q0 + i, p)), (kn, (keys, LANES), lambda p, i: (0, p)),
            (pm, (keys, LANES), lambda p, i: (0, C_FV // LANES + p)),
            (cum_c, (1, nb, 1), lambda p, i: (2 * p, q0 + i, 0)), (cum_c, (1, nb, 1), lambda p, i: (2 * p + 1, q0 + i, 0)),
            (cum_r, (1, 1, keys), lambda p, i: (2 * p, 0, 0)), (cum_r, (1, 1, keys), lambda p, i: (2 * p + 1, 0, 0))]


def fox_attn_fwd(name, qn, kn, pm, cum_c, cum_r):
    s = qn.shape[0]
    parts = []
    for g, (q0, n_q, keys) in enumerate(_fox_groups(s)):
        parts.append(tile_fwd(f"{name}_g{g}", functools.partial(_fox_attn_fn, q0), (4, n_q), _fox_attn_ops(qn, kn, pm, cum_c, cum_r, q0, keys),
                              [((n_q * FOX_BLOCK, BRANCH), BF16, (FOX_BLOCK, LANES), lambda p, i: (i, p), ())], raw=(0, 1, 2))[0])
    return jnp.concatenate(parts, axis=0)


def fox_attn_bwd(name, qn, kn, pm, cum_c, cum_r, dy):
    s = qn.shape[0]
    d_qn, d_kn, d_v, d_cum = [], 0.0, 0.0, 0.0
    for g, (q0, n_q, keys) in enumerate(_fox_groups(s)):
        rows = n_q * FOX_BLOCK
        own_q = ((rows, BRANCH), (FOX_BLOCK, LANES), lambda p, i: (i, p))
        own_k = ((keys, BRANCH), (keys, LANES), lambda p, i: (0, p))
        pair_c = ((4, rows, 1), (1, FOX_BLOCK, 1), lambda p, i: (p, i, 0))
        pair_r = ((4, 1, keys), (1, 1, keys), lambda p, i: (p, 0, 0))
        g_qn, g_kn, g_v, g_cqa, g_cqb, g_cka, g_ckb = tile_bwd(
            f"{name}_g{g}", functools.partial(_fox_attn_fn, q0), (4, n_q), _fox_attn_ops(qn, kn, pm, cum_c, cum_r, q0, keys),
            [(dy, (FOX_BLOCK, LANES), lambda p, i, q0=q0: (q0 + i, p))],
            [(0, (), own_q), (1, (1,), own_k), (2, (1,), own_k), (3, (), pair_c), (4, (), pair_c), (5, (1,), pair_r), (6, (1,), pair_r)])
        d_qn.append(g_qn)
        tail = lambda t, axis: jnp.pad(t, [(0, s - keys) if ax == axis else (0, 0) for ax in range(t.ndim)])
        d_kn, d_v = d_kn + tail(g_kn, 0), d_v + tail(g_v, 0)
        by_q = jnp.stack([g_cqa[:, :, 0], g_cqb[:, :, 0]], axis=1).reshape(8, rows)
        by_k = jnp.stack([g_cka[:, 0, :], g_ckb[:, 0, :]], axis=1).reshape(8, keys)
        d_cum = d_cum + jnp.pad(by_q, [(0, 0), (q0 * FOX_BLOCK, s - q0 * FOX_BLOCK - rows)]) + tail(by_k, 1)
    return jnp.concatenate(d_qn, axis=0), d_kn, d_v, d_cum


def sconv_ops(pm, w):
    s = pm.shape[0]
    blk = lambda c0: (pm, (s, LANES), lambda j, c0=c0: (0, c0 // LANES + j))
    return [blk(C_SB), blk(C_SC), blk(C_SV), (w, (w.shape[0], LANES), lambda j: (0, j))]


def dnconv_ops(pm, w):
    s = pm.shape[0]
    return [(pm, (s, LANES), lambda j: (0, C_DN // LANES + j)), (w, (w.shape[0], LANES), lambda j: (0, j))]


def ffn_ops(ug, uv, w):
    s = ug.shape[0]
    n_t = D_FF // LANES
    return [(ug, (s, LANES), lambda j: (0, j)), (uv, (s, LANES), lambda j: (0, j)),
            (w, (w.shape[0], LANES), lambda j: (0, j)), (w, (w.shape[0], LANES), lambda j: (0, n_t + j))]


def _col_out(s, width, dtype=F32):
    return ((s, width), dtype, (s, LANES), lambda j: (0, j), ())


def _col_cot(g):
    return (g, (g.shape[0], LANES), lambda j: (0, j))


def merge_ops(yp, pm):
    gate = lambda b: (pm, (256, D_MODEL), lambda i, b=b: (i, C_GATE // D_MODEL + b))
    return [_rows(yp[0]), _rows(yp[1]), _rows(yp[2]), gate(0), gate(1), gate(2)]


def ple_ops(gpre, pe, x):
    return [_rows(gpre), _rows(pe), _rows(x)]


def adam_call(name, w, g, m, v):
    shape = w.shape
    last = shape[-1]
    rows = w.size // last
    flat = lambda t: t.reshape(rows, last)
    tm = rows
    for cand in (512, 256, 128, 64, 32, 16, 8):
        if rows % cand == 0 and cand * last * 4 <= 2 * 1024 * 1024:
            tm = cand
            break
    spec = lambda t: (flat(t), (tm, last), lambda i: (i, 0))
    out = ((rows, last), F32, (tm, last), lambda i: (i, 0), ())
    res = tile_fwd(name, _adam_fn, (rows // tm,), [spec(w), spec(g), spec(m), spec(v)], [out, out, out])
    return [r.reshape(shape) for r in res]


def _adam_layers_fn(d, pids, w, m, v, g0, g1):
    g = jnp.where(pids[0] == 0, g0, g1)
    return (g,) + _adam_fn(d, pids, w, g, m, v)


def adam_layers(name, w, m, v, g0, g1):
    _, rows, cols = w.shape
    tm = _row_tile(rows, cols)
    n_t = rows // tm
    lay = lambda t: (t, (1, tm, cols), lambda l, i: (l, i, 0))
    ins = [lay(w), lay(m), lay(v), (g0, (tm, cols), lambda l, i: (i * (1 - l) + (n_t - 1) * l, 0)), (g1, (tm, cols), lambda l, i: (i * l, 0))]
    out = (w.shape, F32, (1, tm, cols), lambda l, i: (l, i, 0), ())
    return tile_fwd(name, _adam_layers_fn, (2, n_t), ins, [out, out, out, out])


def adam_w_in(name, w, m, v, g0, g1):
    rows, n_l, cols = w.shape

    def body(w_ref, m_ref, v_ref, g0_ref, g1_ref, g_out, d_out, m_out, v_out):
        step = 64

        def update(at):
            for l, g_ref in enumerate((g0_ref, g1_ref)):
                g = g_ref[at, :]
                delta, m2, v2 = _adam_fn(False, None, w_ref[at, l, :], g, m_ref[at, l, :], v_ref[at, l, :])
                for ref, val in ((g_out, g), (d_out, delta), (m_out, m2), (v_out, v2)):
                    ref[at, l, :] = val

        def some_rows(i, carry):
            update(pl.ds(pl.multiple_of(i * step, step), step))
            return carry

        lax.fori_loop(0, rows // step, some_rows, 0)
        if rows % step:
            update(pl.ds(rows - rows % step, rows % step))

    both = pl.BlockSpec((rows, n_l, LANES), lambda j: (0, 0, j))
    one = pl.BlockSpec((rows, LANES), lambda j: (0, j))
    return pl.pallas_call(
        body, grid=(cols // LANES,), in_specs=[both, both, both, one, one], out_specs=[both] * 4,
        out_shape=[jax.ShapeDtypeStruct(w.shape, F32)] * 4, name=name, compiler_params=_cparams(1),
    )(w, m, v, g0, g1)


DN_GROUP = 4


def _dn_local_specs(rev_n=None):
    rows = DN_GROUP * DN_CHUNK
    idx = (lambda j: j) if rev_n is None else (lambda j: rev_n - 1 - j)
    return [pl.BlockSpec((rows, 3 * BRANCH), lambda j: (idx(j), 0)), pl.BlockSpec((rows, LANES), lambda j: (idx(j), 0)),
            pl.BlockSpec((DN_GROUP, DN_HEADS, DN_CHUNK), lambda j: (idx(j), 0, 0)), pl.BlockSpec((2, DN_HEADS), lambda j: (0, 0))]


def _dn_group_inputs(qkv, ps, a_rows, c):
    lo = c * DN_CHUNK
    heads = _split_heads(qkv[lo:lo + DN_CHUNK])
    return heads[0:4], heads[4:8], heads[8:12], ps[lo:lo + DN_CHUNK], a_rows[c]


def dn_local_fwd(name, dn_act, ps, a_rows, ad):
    s = dn_act.shape[0]
    n_c, n_g = s // DN_CHUNK, s // (DN_GROUP * DN_CHUNK)
    rows = DN_GROUP * DN_CHUNK

    def body(qkv_ref, ps_ref, ar_ref, ad_ref, u_ref, kc_ref, qd_ref, kd_ref, qk_ref, gl_ref):
        qkv, ps_v, a_rows_v, ad_v = qkv_ref[...], ps_ref[...], ar_ref[...], ad_ref[...]
        args = [[] for _ in range(8)]
        for c in range(DN_GROUP):
            q4, k4, v4, ps_c, ar_c = _dn_group_inputs(qkv, ps_v, a_rows_v, c)
            for lst, vals in zip(args, (q4, k4, v4) + _dn_gates(ps_c, ar_c, ad_v)):
                lst.extend(vals)
        everything = _dn_local(False, *args)
        for c in range(DN_GROUP):
            res = everything[c * DN_HEADS:(c + 1) * DN_HEADS]
            at = pl.ds(c * DN_CHUNK, DN_CHUNK)
            for ref, i in ((u_ref, 0), (kc_ref, 1), (qd_ref, 2), (kd_ref, 3)):
                ref[at, :] = jnp.concatenate([r[i] for r in res], axis=1)
            for h in range(DN_HEADS):
                qk_ref[c, h] = res[h][4]
            gl_ref[c] = _head_rows([r[5] for r in res])

    wide = pl.BlockSpec((rows, BRANCH), lambda j: (j, 0))
    return pl.pallas_call(
        body, grid=(n_g,), in_specs=_dn_local_specs(),
        out_specs=[wide, wide, wide, wide, pl.BlockSpec((DN_GROUP, DN_HEADS, DN_CHUNK, DN_CHUNK), lambda j: (j, 0, 0, 0)),
                   pl.BlockSpec((DN_GROUP, 8, LANES), lambda j: (j, 0, 0))],
        out_shape=[jax.ShapeDtypeStruct((s, BRANCH), F32)] * 4 + [jax.ShapeDtypeStruct((n_c, DN_HEADS, DN_CHUNK, DN_CHUNK), F32),
                                                                 jax.ShapeDtypeStruct((n_c, 8, LANES), F32)],
        name=name, compiler_params=_cparams(1),
    )(dn_act, ps, a_rows, ad)


def dn_local_bwd(name, dn_act, ps, a_rows, ad, cots):
    s = dn_act.shape[0]
    n_c, n_g = s // DN_CHUNK, s // (DN_GROUP * DN_CHUNK)
    rows = DN_GROUP * DN_CHUNK

    def body(qkv_ref, ps_ref, ar_ref, ad_ref, du_ref, dkc_ref, dqd_ref, dkd_ref, dqk_ref, dgl_ref, dqkv_ref, dps_ref, dar_ref, dad_ref):
        first = pl.program_id(0) == 0
        qkv, ps_v, a_rows_v, ad_v = qkv_ref[...], ps_ref[...], ar_ref[...], ad_ref[...]
        d_wide = [r[...] for r in (du_ref, dkc_ref, dqd_ref, dkd_ref)]
        qs, ks, vs, ps_cs, ar_cs, cot = [], [], [], [], [], []
        for c in range(DN_GROUP):
            q4, k4, v4, ps_c, ar_c = _dn_group_inputs(qkv, ps_v, a_rows_v, c)
            qs, ks, vs, ps_cs, ar_cs = qs + q4, ks + k4, vs + v4, ps_cs + [ps_c], ar_cs + [ar_c]
            lo = c * DN_CHUNK
            d_tiles = [_split_heads(t[lo:lo + DN_CHUNK]) for t in d_wide]
            d_gl = dgl_ref[c]
            cot += [(d_tiles[0][h], d_tiles[1][h], d_tiles[2][h], d_tiles[3][h], dqk_ref[c, h], _col(_row(d_gl, h), 0))
                    for h in range(DN_HEADS)]

        def f(qs, ks, vs, ps_cs, ar_cs, ad_v):
            gates = [[] for _ in range(5)]
            for ps_c, ar_c in zip(ps_cs, ar_cs):
                for lst, vals in zip(gates, _dn_gates(ps_c, ar_c, ad_v)):
                    lst.extend(vals)
            return _dn_local(True, qs, ks, vs, *gates)

        _, vjp = jax.vjp(f, qs, ks, vs, ps_cs, ar_cs, ad_v)
        d_q, d_k, d_v, d_ps, d_ar, d_ad = vjp(cot)
        for c in range(DN_GROUP):
            at, hs = pl.ds(c * DN_CHUNK, DN_CHUNK), slice(c * DN_HEADS, (c + 1) * DN_HEADS)
            dqkv_ref[at, :] = jnp.concatenate(d_q[hs] + d_k[hs] + d_v[hs], axis=1).astype(dqkv_ref.dtype)
            dps_ref[at, :] = d_ps[c]
            dar_ref[c] = d_ar[c]
        _store(dad_ref, d_ad, first)

    wide = pl.BlockSpec((rows, BRANCH), lambda j: (j, 0))
    specs = _dn_local_specs()
    return pl.pallas_call(
        body, grid=(n_g,),
        in_specs=specs + [wide, wide, wide, wide, pl.BlockSpec((DN_GROUP, DN_HEADS, DN_CHUNK, DN_CHUNK), lambda j: (j, 0, 0, 0)),
                          pl.BlockSpec((DN_GROUP, 8, LANES), lambda j: (j, 0, 0))],
        out_specs=specs,
        out_shape=[jax.ShapeDtypeStruct((s, 3 * BRANCH), F32), jax.ShapeDtypeStruct((s, LANES), F32),
                   jax.ShapeDtypeStruct((n_c, DN_HEADS, DN_CHUNK), F32), jax.ShapeDtypeStruct((2, DN_HEADS), F32)],
        name=name, compiler_params=_cparams(1),
    )(dn_act, ps, a_rows, ad, *cots)


def _dn_scan_specs(n_c, rev):
    idx = (lambda j: n_c - 1 - j) if rev else (lambda j: j)
    wide = pl.BlockSpec((DN_CHUNK, BRANCH), lambda j: (idx(j), 0))
    return [wide, wide, wide, wide, pl.BlockSpec((1, DN_HEADS, DN_CHUNK, DN_CHUNK), lambda j: (idx(j), 0, 0, 0)),
            pl.BlockSpec((1, 8, LANES), lambda j: (idx(j), 0, 0)), pl.BlockSpec((DN_CHUNK, BRANCH), lambda j: (idx(j), C_DZ // BRANCH)),
            pl.BlockSpec((1, DN_DH), lambda j: (0, 0))]


def _dn_scan_tiles(refs):
    u_ref, kc_ref, qd_ref, kd_ref, qk_ref, gl_ref, z_ref, g_ref = refs
    wide = [_split_heads(r[...]) for r in (u_ref, kc_ref, qd_ref, kd_ref)]
    gl = gl_ref[0]
    return [(wide[0][h], wide[1][h], wide[2][h], wide[3][h], qk_ref[0, h], _col(_row(gl, h), 0)) for h in range(DN_HEADS)], \
        _split_heads(z_ref[...].astype(F32)), g_ref[...]


def dn_scan_fwd(name, local, pm, gain):
    s = pm.shape[0]
    n_c = s // DN_CHUNK

    def body(*refs):
        y_ref, hist_ref, state = refs[8:]

        @pl.when(pl.program_id(0) == 0)
        def _():
            state[...] = jnp.zeros_like(state)

        hist_ref[0] = state[...]
        per_head, z4, gain_v = _dn_scan_tiles(refs[:8])
        ys, s_nexts = _dn_step(False, [state[h] for h in range(DN_HEADS)], per_head, z4, gain_v)
        for h in range(DN_HEADS):
            state[h] = s_nexts[h]
        y_ref[...] = jnp.concatenate(ys, axis=1).astype(y_ref.dtype)

    return pl.pallas_call(
        body, grid=(n_c,), in_specs=_dn_scan_specs(n_c, False),
        out_specs=[pl.BlockSpec((DN_CHUNK, BRANCH), lambda j: (j, 0)),
                   pl.BlockSpec((1, DN_HEADS, DN_DH, DN_DH), lambda j: (j, 0, 0, 0))],
        out_shape=[jax.ShapeDtypeStruct((s, BRANCH), BF16), jax.ShapeDtypeStruct((n_c, DN_HEADS, DN_DH, DN_DH), F32)],
        scratch_shapes=[pltpu.VMEM((DN_HEADS, DN_DH, DN_DH), F32)],
        name=name, compiler_params=_cparams(1),
    )(*local, pm, gain)


def dn_scan_bwd(name, local, pm, gain, hist, dy):
    s = pm.shape[0]
    n_c = s // DN_CHUNK

    def body(*refs):
        hist_ref, dy_ref = refs[8:10]
        du_ref, dkc_ref, dqd_ref, dkd_ref, dqk_ref, dgl_ref, dz_ref, dg_ref, d_state = refs[10:]
        first = pl.program_id(0) == 0

        @pl.when(first)
        def _():
            d_state[...] = jnp.zeros_like(d_state)

        per_head, z4, gain_v = _dn_scan_tiles(refs[:8])
        _, vjp = jax.vjp(functools.partial(_dn_step, True), [hist_ref[0, h] for h in range(DN_HEADS)], per_head, z4, gain_v)
        d_s, grads, d_z, d_gain = vjp((_split_heads(dy_ref[...].astype(F32)), [d_state[h] for h in range(DN_HEADS)]))
        for h in range(DN_HEADS):
            d_state[h] = d_s[h]
        for ref, i in ((du_ref, 0), (dkc_ref, 1), (dqd_ref, 2), (dkd_ref, 3)):
            ref[...] = jnp.concatenate([g[i] for g in grads], axis=1)
        dz_ref[...] = jnp.concatenate(d_z, axis=1).astype(dz_ref.dtype)
        for h in range(DN_HEADS):
            dqk_ref[0, h] = grads[h][4]
        dgl_ref[0] = _head_rows([g[5] for g in grads])
        _store(dg_ref, d_gain, first)

    rev = lambda j: n_c - 1 - j
    specs = _dn_scan_specs(n_c, True)
    return pl.pallas_call(
        body, grid=(n_c,),
        in_specs=specs + [pl.BlockSpec((1, DN_HEADS, DN_DH, DN_DH), lambda j: (rev(j), 0, 0, 0)),
                          pl.BlockSpec((DN_CHUNK, BRANCH), lambda j: (rev(j), 0))],
        out_specs=specs[:6] + [pl.BlockSpec((DN_CHUNK, BRANCH), lambda j: (rev(j), 0)), specs[7]],
        out_shape=[jax.ShapeDtypeStruct((s, BRANCH), F32)] * 4 + [
            jax.ShapeDtypeStruct((n_c, DN_HEADS, DN_CHUNK, DN_CHUNK), F32), jax.ShapeDtypeStruct((n_c, 8, LANES), F32),
            jax.ShapeDtypeStruct((s, BRANCH), BF16), jax.ShapeDtypeStruct((1, DN_DH), F32)],
        scratch_shapes=[pltpu.VMEM((DN_HEADS, DN_DH, DN_DH), F32)],
        name=name, compiler_params=_cparams(1),
    )(*local, pm, gain, hist, dy)


def _seq_layouts(cols, s):
    return cols.T.reshape(cols.shape[1], s // LANES, LANES)


def layer_fwd(li, x, p, w, more_weights=None):
    s = x.shape[0]
    n = lambda t: f"{t}_l{li}"
    h = rms_fwd(n("rms_mix"), x, w["g_mix"])
    pm = mm(n("in_main"), h, w["in_main"], "nn")
    ps = mm(n("in_small"), h, w["in_small"], "nn")
    qn, kn = fox_prep_fwd(n("fox_prep"), pm, w["gq"], w["gk"])
    f_t = _seq_layouts(ps[:, 0:8], s)
    cum = fox_gate_fwd(n("fox_gate"), f_t, w["b_f"])
    cum_c, cum_r = cum.reshape(8, s, 1), cum.reshape(8, 1, s)
    y_fox = fox_attn_fwd(n("fox_attn"), qn, kn, pm, cum_c, cum_r)
    y_sc = tile_fwd(n("sconv"), _sconv_fn, (BRANCH // LANES,), sconv_ops(pm, w["sc_conv_w"]), [_col_out(s, BRANCH, BF16)])[0]
    dn_act = tile_fwd(n("dnconv"), _dnconv_fn, (3 * BRANCH // LANES,), dnconv_ops(pm, w["dn_conv_w"]), [_col_out(s, 3 * BRANCH)])[0]
    a_rows = ps[:, 12:16].reshape(s // DN_CHUNK, DN_CHUNK, DN_HEADS).transpose(0, 2, 1)
    dn_local = dn_local_fwd(n("dn_local"), dn_act, ps, a_rows, w["ad"])
    y_dn, hist = dn_scan_fwd(n("dn_scan"), dn_local, pm, w["dn_gain"])
    ys = (y_fox, y_sc, y_dn)
    if more_weights is not None:
        w = {**w, **more_weights(y_dn)}
    yp = [mm(n(f"branch{b}"), ys[b], w["branch"][b], "nn", blocks=(0, N_CHIPS)) for b in range(3)]
    merged = tile_fwd(n("merge"), _merge_fn, (s // 256,), merge_ops(yp, pm), [((s, D_MODEL), BF16, (256, D_MODEL), lambda i: (i, 0), ())])[0]
    x1 = mm(n("w_o"), merged, w["o"], "nn", add=x)
    h2 = rms_fwd(n("rms_ffn"), x1, w["g_ffn"])
    ug = mm(n("up_g"), h2, w["up"], "nn", blocks=(0, 2))
    uv = mm(n("up_v"), h2, w["up"], "nn", blocks=(2, 2))
    act = tile_fwd(n("ffn_act"), _ffn_act_fn, (D_FF // LANES,), ffn_ops(ug, uv, w["ffn_conv_w"]), [_col_out(s, D_FF, BF16)])[0]
    x2 = mm(n("down"), act, w["down"], "nn", add=x1)
    h3 = rms_fwd(n("rms_ple"), x2, w["g_ple"])
    gpre = mm(n("ple_gate"), h3, w["pg"], "nn")
    pe = mm(n("ple_emb"), p, w["ple"], "nn", blocks=(0, N_CHIPS))
    x3 = tile_fwd(n("ple"), _ple_fn, (s // 256,), ple_ops(gpre, pe, x2), [((s, D_MODEL), F32, (256, D_MODEL), lambda i: (i, 0), ())])[0]
    saved = dict(x=x, h=h, pm=pm, ps=ps, qn=qn, kn=kn, f_t=f_t, cum_c=cum_c, cum_r=cum_r, ys=ys, dn_act=dn_act, dn_local=dn_local,
                 a_rows=a_rows, hist=hist, yp=yp, merged=merged, x1=x1, h2=h2, ug=ug, uv=uv, act=act, x2=x2, h3=h3,
                 gpre=gpre, pe=pe, p=p)
    return x3, saved, w


def hang_on(w, token):
    zero = token[0, 0]
    small = ("g_mix", "g_ffn", "g_ple", "gq", "gk", "b_f", "ad", "dn_gain", "sc_conv_w", "dn_conv_w", "ffn_conv_w")
    return {**w, **{k: w[k] + zero for k in small}}


def layer_bwd(li, dx3, sv, w, hooks=None):
    hooks = hooks or {}

    def stage(key, after, w):
        return hang_on(w, hooks[key](after, g)) if key in hooks else w

    s = dx3.shape[0]
    n = lambda t: f"{t}_l{li}"
    g = {}
    col_own = lambda width: ((s, width), (s, LANES), lambda j: (0, j))
    d_gpre, d_pe = tile_bwd(n("ple_bwd"), _ple_fn, (s // 256,), ple_ops(sv["gpre"], sv["pe"], sv["x2"]), [_rows(dx3)],
                            [(0, (), None, BF16), (1, (), None, BF16)])
    g["w_ple"] = mm(n("d_w_ple"), sv["p"], d_pe, "tn", blocks=(0, N_CHIPS))
    g["w_ple_gate"] = mm(n("d_w_pg"), sv["h3"], d_gpre, "tn").reshape(N_CHIPS, -1, D_MODEL)
    dh3 = mm(n("d_h3"), d_gpre, w["pg"], "nt")
    dx2, d_g_ple = rms_bwd(n("rms_ple_bwd"), sv["x2"], w["g_ple"], dh3, dx3)
    dact = mm(n("d_act"), dx2, w["down"], "nt")
    g["w_down"] = mm(n("d_w_down"), sv["act"], dx2, "tn").reshape(N_CHIPS, -1, D_MODEL)
    taps_own = ((w["ffn_conv_w"].shape[0], D_FF), (w["ffn_conv_w"].shape[0], LANES), lambda j: (0, j))
    d_ug, d_uv, d_fw_g, d_fw_v = tile_bwd(n("ffn_act_bwd"), _ffn_act_fn, (D_FF // LANES,), ffn_ops(sv["ug"], sv["uv"], w["ffn_conv_w"]),
                                          [_col_cot(dact)], [(0, (), None, BF16), (1, (), None, BF16), (2, (), taps_own), (3, (), taps_own)])
    g["ffn_conv_w"] = jnp.concatenate([d_fw_g, d_fw_v], axis=1)
    g["w_up"] = jnp.concatenate([mm(n("d_w_up_g"), sv["h2"], d_ug, "tn", blocks=(0, 2)), mm(n("d_w_up_v"), sv["h2"], d_uv, "tn", blocks=(0, 2))])
    dh2 = mm(n("d_h2_v"), d_uv, w["up"], "nt", blocks=(2, 2), add=mm(n("d_h2_g"), d_ug, w["up"], "nt", blocks=(0, 2)))
    dx1, d_g_ffn = rms_bwd(n("rms_ffn_bwd"), sv["x1"], w["g_ffn"], dh2, dx2)
    w = stage("mid", dx1, w)
    dmerged = mm(n("d_merged"), dx1, w["o"], "nt")
    g["w_o"] = mm(n("d_w_o"), sv["merged"], dx1, "tn").reshape(N_CHIPS, -1, D_MODEL)
    gate_own = ((s, D_MODEL), (256, D_MODEL), lambda i: (i, 0))
    d_yp0, d_yp1, d_yp2, d_g0, d_g1, d_g2 = tile_bwd(
        n("merge_bwd"), _merge_fn, (s // 256,), merge_ops(sv["yp"], sv["pm"]), [_rows(dmerged)],
        [(0, (), None, BF16), (1, (), None, BF16), (2, (), None, BF16), (3, (), gate_own, BF16), (4, (), gate_own, BF16), (5, (), gate_own, BF16)])
    d_yp = (d_yp0, d_yp1, d_yp2)
    g["w_branch"] = jnp.concatenate([mm(n(f"d_w_branch{b}"), sv["ys"][b], d_yp[b], "tn", blocks=(0, N_CHIPS)) for b in range(3)], axis=1)
    d_ys = [mm(n(f"d_y{b}"), d_yp[b], w["branch"][b], "nt", blocks=(0, N_CHIPS)) for b in range(3)]
    w = stage("late", d_ys[2], w)
    *d_local, d_z, d_dngain = dn_scan_bwd(n("dn_scan_bwd"), sv["dn_local"], sv["pm"], w["dn_gain"], sv["hist"], d_ys[2])
    d_dnact, d_ps_dn, d_arows, d_ad = dn_local_bwd(n("dn_local_bwd"), sv["dn_act"], sv["ps"], sv["a_rows"], w["ad"], d_local)
    g["ad"], g["dn_norm_gain"] = d_ad, d_dngain[0]
    d_dnqkv, g["dn_conv_w"] = tile_bwd(n("dnconv_bwd"), _dnconv_fn, (3 * BRANCH // LANES,), dnconv_ops(sv["pm"], w["dn_conv_w"]),
                                       [_col_cot(d_dnact)], [(0, (), col_own(3 * BRANCH), BF16), (1, ())])
    d_sb, d_sc, d_sv, g["sc_conv_w"] = tile_bwd(n("sconv_bwd"), _sconv_fn, (BRANCH // LANES,), sconv_ops(sv["pm"], w["sc_conv_w"]), [_col_cot(d_ys[1])],
                                                [(0, (), col_own(BRANCH), BF16), (1, (), col_own(BRANCH), BF16), (2, (), col_own(BRANCH), BF16), (3, ())])
    w = stage("last", d_dnqkv, w)
    d_qn, d_kn, d_fv, d_cum = fox_attn_bwd(n("fox_attn_bwd"), sv["qn"], sv["kn"], sv["pm"], sv["cum_c"], sv["cum_r"], d_ys[0])
    d_ft, d_bf = fox_gate_bwd(n("fox_gate_bwd"), sv["f_t"], w["b_f"], d_cum.reshape(8, s // LANES, LANES))
    g["b_fox_f"] = d_bf.reshape(8)
    d_fq, d_fk, d_gq, d_gk = fox_prep_bwd(n("fox_prep_bwd"), sv["pm"], w["gq"], w["gk"], d_qn, d_kn)
    g["fox_q_gain"] = d_gq[0, :FOX_DH] + d_gq[0, FOX_DH:]
    g["fox_k_gain"] = d_gk[0, :FOX_DH] + d_gk[0, FOX_DH:]
    d_pm = jnp.concatenate([d_fq, d_fk, d_fv.astype(BF16), d_sb, d_sc, d_sv, d_dnqkv, d_z, d_g0, d_g1, d_g2], axis=1)
    d_a_cols = d_arows.transpose(0, 2, 1).reshape(s, DN_HEADS)
    d_f_cols = d_ft.reshape(8, s).T
    d_ps = d_ps_dn + jnp.concatenate([d_f_cols, jnp.zeros((s, 4), F32), d_a_cols, jnp.zeros((s, LANES - 16), F32)], axis=1)
    g["w_in"] = chip_blocks_w_in(mm(n("d_w_in_main"), d_pm, sv["h"], "tn"), mm(n("d_w_in_small"), d_ps, sv["h"], "tn"))
    w = stage("w_in", g["w_in"], w)
    dh = mm(n("d_h_small"), d_ps, w["in_small"], "nt", add=mm(n("d_h_main"), d_pm, w["in_main"], "nt"))
    dx, d_g_mix = rms_bwd(n("rms_mix_bwd"), sv["x"], w["g_mix"], dh, dx1)
    g["g_mix"], g["g_ffn"], g["g_ple"] = d_g_mix[0], d_g_ffn[0], d_g_ple[0]
    return dx, g


IN_SHARD = 2052
MAIN_RANGES = ((0, 1536), (1544, 3080), (3080, 4616), (4624, 5136), (5136, 8208))
SMALL_RANGES = ((1536, 1544), (4616, 4620), (4620, 4624))


def _from_chip_blocks(blocks, ranges):
    parts = []
    for lo, hi in ranges:
        for k in range(N_CHIPS):
            a0, a1 = max(lo, k * IN_SHARD), min(hi, (k + 1) * IN_SHARD)
            if a0 < a1:
                parts.append(blocks[k][:, a0 - k * IN_SHARD:a1 - k * IN_SHARD])
    return parts


def split_w_in(blocks):
    main = jnp.concatenate(_from_chip_blocks(blocks, MAIN_RANGES), axis=1)
    pad = jnp.zeros((blocks.shape[1], LANES - 16), blocks.dtype)
    return main, jnp.concatenate(_from_chip_blocks(blocks, SMALL_RANGES) + [pad], axis=1)


def chip_blocks_w_in(main, small):
    ranges = sorted([(lo, hi, "m") for lo, hi in MAIN_RANGES] + [(lo, hi, "s") for lo, hi in SMALL_RANGES])
    offs, m_off, s_off = {}, 0, 0
    for lo, hi in MAIN_RANGES:
        offs[lo] = m_off
        m_off += hi - lo
    for lo, hi in SMALL_RANGES:
        offs[lo] = s_off
        s_off += hi - lo
    blocks = []
    for k in range(N_CHIPS):
        parts = []
        for lo, hi, src in ranges:
            a0, a1 = max(lo, k * IN_SHARD), min(hi, (k + 1) * IN_SHARD)
            if a0 < a1:
                arr = main if src == "m" else small
                parts.append(arr[offs[lo] + a0 - lo:offs[lo] + a1 - lo])
        blocks.append(jnp.concatenate(parts, axis=0))
    return jnp.stack(blocks)


def later_weights(got):
    g_branch, g_o, g_up, g_down, g_pg, g_ple = got
    branch = g_branch.reshape(N_CHIPS, 3, BRANCH, -1)
    return dict(branch=[branch[:, b] for b in range(3)], o=g_o.reshape(D_MODEL, D_MODEL), up=g_up,
                down=g_down.reshape(D_FF, D_MODEL), pg=g_pg.reshape(D_MODEL, D_MODEL), ple=g_ple)


def layer_weights(li, got, conv, a):
    main, small = split_w_in(got[0])
    tile2 = lambda v: jnp.concatenate([v, v])[None, :]
    rest = later_weights(got[1:]) if len(got) > 1 else {}
    return dict(
        in_main=main, in_small=small, **rest,
        g_mix=a["g_mix"][li][None, :], g_ffn=a["g_ffn"][li][None, :], g_ple=a["g_ple"][li][None, :],
        gq=tile2(a["fox_q_gain"][li]), gk=tile2(a["fox_k_gain"][li]), b_f=a["b_fox_f"][li].reshape(8, 1, 1),
        ad=jnp.stack([a["dn_a_log"][li], a["dn_dt_bias"][li]]), dn_gain=a["dn_norm_gain"][li][None, :],
        sc_conv_w=conv["sc_conv_w"][li], dn_conv_w=conv["dn_conv_w"][li], ffn_conv_w=conv["ffn_conv_w"][li])


def pack_rows(arrs, dtype):
    flat = jnp.concatenate([t.reshape(-1).astype(dtype) for t in arrs])
    pad = (-flat.shape[0]) % (8 * LANES)
    if pad:
        flat = jnp.concatenate([flat, jnp.zeros((pad,), dtype)])
    return flat.reshape(-1, LANES)


def unpack_rows(buf, shapes):
    flat = buf.reshape(-1)
    out, off = [], 0
    for shp in shapes:
        size = 1
        for dim in shp:
            size *= dim
        out.append(flat[off:off + size].reshape(shp))
        off += size
    return out


def chip_shard(t, axis, k):
    width = t.shape[axis] // N_CHIPS
    return lax.slice_in_dim(t, k * width, (k + 1) * width, axis=axis)


ANY = pl.BlockSpec(memory_space=pl.ANY)


def _position():
    x, y, c = lax.axis_index("x"), lax.axis_index("y"), lax.axis_index("c")
    return x, y, c, [(1 - x, y), (x, 1 - y), (1 - x, 1 - y)]


def gather_small(name, block):
    m_per, n = block.shape

    def body(x_ref, out_ref, token, send_sems, recv_sems, local_sem):
        token[...] = jnp.zeros_like(token)
        x, y, c, chips = _position()
        me, sibling = (x, y, c), (x, y, 1 - c)

        def rows(px, py, pc):
            return out_ref.at[pl.ds((4 * px + 2 * py + pc) * m_per, m_per), :]

        def copy(k, blk, to, src=None):
            return pltpu.make_async_remote_copy(src_ref=rows(*blk) if src is None else src, dst_ref=rows(*blk),
                                                send_sem=send_sems.at[k], recv_sem=recv_sems.at[k], device_id=to, device_id_type=MESH)

        mine = pltpu.make_async_copy(x_ref, rows(*me), local_sem)
        mine.start()
        first = [copy(0, me, sibling, src=x_ref)] + [copy(1 + j, me, (*chip, c), src=x_ref) for j, chip in enumerate(chips)]
        for cp in first:
            cp.start()
        passed = [copy(4 + j, (*chip, c), sibling) for j, chip in enumerate(chips)]
        for j, chip in enumerate(chips):
            copy(1 + j, (*chip, c), me).wait_recv()
            passed[j].start()
        copy(0, sibling, me).wait_recv()
        for j, chip in enumerate(chips):
            copy(4 + j, (*chip, 1 - c), me).wait_recv()
        for cp in first + passed:
            cp.wait_send()
        mine.wait()

    in_vmem = pl.BlockSpec(memory_space=pltpu.VMEM)
    return pl.pallas_call(
        body, out_shape=[jax.ShapeDtypeStruct((8 * m_per, n), block.dtype), jax.ShapeDtypeStruct((8, LANES), F32)],
        in_specs=[in_vmem], out_specs=[in_vmem, in_vmem],
        scratch_shapes=[pltpu.SemaphoreType.DMA((7,)), pltpu.SemaphoreType.DMA((7,)), pltpu.SemaphoreType.DMA],
        name=name, compiler_params=pltpu.CompilerParams(vmem_limit_bytes=VMEM_LIMIT),
    )(block)


def _sems(n):
    return [pltpu.SemaphoreType.DMA((n,)), pltpu.SemaphoreType.DMA((n,))]


def _split_cols(rows):
    return (rows // 2) % 16 != 0


def _half(ref, which, lead=()):
    rows, cols = ref.shape[-2:]
    if _split_cols(rows):
        return ref.at[(*lead, slice(None), pl.ds(which * (cols // 2), cols // 2))]
    return ref.at[(*lead, pl.ds(which * (rows // 2), rows // 2), slice(None))]


def _half_shape(rows, cols):
    return (rows, cols // 2) if _split_cols(rows) else (rows // 2, cols)


def gather_layer(name, shards):
    n_w = len(shards)

    def body(*refs):
        ins, outs = refs[:n_w], refs[n_w:2 * n_w]
        token, send_sems, recv_sems = refs[2 * n_w:]
        token[...] = jnp.zeros_like(token)
        x, y, c, chips = _position()
        sibling = (x, y, 1 - c)

        def part(w, px, py, pc):
            return _half(outs[w], pc, (2 * px + py,))

        def copy(k, w, blk, to, src=None):
            return pltpu.make_async_remote_copy(src_ref=part(w, *blk) if src is None else src, dst_ref=part(w, *blk),
                                                send_sem=send_sems.at[k], recv_sem=recv_sems.at[k], device_id=to, device_id_type=MESH)

        pairs = [(w, j, chip) for w in range(n_w) for j, chip in enumerate(chips)]
        first = [copy(3 * w + j, w, (x, y, c), (*chip, c), src=_half(ins[w], c)) for w, j, chip in pairs]
        for cp in first:
            cp.start()
        passed = [copy(3 * n_w + 3 * w + j, w, (*chip, c), sibling) for w, j, chip in pairs]
        for (w, j, chip), fwd in zip(pairs, passed):
            copy(3 * w + j, w, (*chip, c), (x, y, c)).wait_recv()
            fwd.start()
        for w, j, chip in pairs:
            copy(3 * n_w + 3 * w + j, w, (*chip, 1 - c), (x, y, c)).wait_recv()
        for cp in first + passed:
            cp.wait_send()

    out = pl.pallas_call(
        body, out_shape=[jax.ShapeDtypeStruct((N_CHIPS,) + s.shape, s.dtype) for s in shards] + [jax.ShapeDtypeStruct((8, LANES), F32)],
        in_specs=[ANY] * n_w, out_specs=[ANY] * n_w + [pl.BlockSpec(memory_space=pltpu.VMEM)], scratch_shapes=_sems(6 * n_w), name=name,
    )(*shards)
    return out[:n_w], out[n_w]


def swap_halves(name, grads):
    n_w = len(grads)

    def body(*refs):
        ins, outs = refs[:n_w], refs[n_w:2 * n_w]
        send_sems, recv_sems = refs[2 * n_w:]
        x, y, c, _ = _position()
        cps = [pltpu.make_async_remote_copy(src_ref=_half(ins[w], 1 - c, (slice(None),)), dst_ref=outs[w],
                                            send_sem=send_sems.at[w], recv_sem=recv_sems.at[w], device_id=(x, y, 1 - c),
                                            device_id_type=MESH) for w in range(n_w)]
        for cp in cps:
            cp.start()
        for cp in cps:
            cp.wait()

    return pl.pallas_call(
        body, out_shape=[jax.ShapeDtypeStruct((N_CHIPS,) + _half_shape(*g.shape[1:]), g.dtype) for g in grads],
        in_specs=[ANY] * n_w, out_specs=[ANY] * n_w, scratch_shapes=_sems(n_w), name=name,
    )(*grads)


def scatter_chips(name, partials):
    n_w = len(partials)

    def body(*refs):
        ins, outs = refs[:n_w], refs[n_w:2 * n_w]
        send_sems, recv_sems = refs[2 * n_w:]
        x, y, c, chips = _position()
        cps = [pltpu.make_async_remote_copy(src_ref=ins[w].at[2 * cx + cy], dst_ref=outs[w].at[j], send_sem=send_sems.at[3 * w + j],
                                            recv_sem=recv_sems.at[3 * w + j], device_id=(cx, cy, c), device_id_type=MESH)
               for w in range(n_w) for j, (cx, cy) in enumerate(chips)]
        for cp in cps:
            cp.start()
        for cp in cps:
            cp.wait()

    return pl.pallas_call(
        body, out_shape=[jax.ShapeDtypeStruct((3,) + p.shape[1:], p.dtype) for p in partials],
        in_specs=[ANY] * n_w, out_specs=[ANY] * n_w, scratch_shapes=_sems(3 * n_w), name=name,
    )(*partials)


def share_halves(name, bufs):
    n_w = len(bufs)

    def body(*refs):
        outs = refs[n_w:2 * n_w]
        send_sems, recv_sems = refs[2 * n_w:]
        x, y, c, _ = _position()

        def copy(w, pc):
            half = _half(outs[w], pc)
            return pltpu.make_async_remote_copy(src_ref=half, dst_ref=half, send_sem=send_sems.at[w], recv_sem=recv_sems.at[w],
                                                device_id=(x, y, 1 - c), device_id_type=MESH)

        for w in range(n_w):
            copy(w, c).start()
        for w in range(n_w):
            copy(w, 1 - c).wait_recv()
            copy(w, c).wait_send()

    return pl.pallas_call(
        body, out_shape=[jax.ShapeDtypeStruct(b.shape, b.dtype) for b in bufs], in_specs=[ANY] * n_w, out_specs=[ANY] * n_w,
        input_output_aliases={w: w for w in range(n_w)}, scratch_shapes=_sems(n_w), name=name,
    )(*bufs)


HBM = pl.BlockSpec(memory_space=pltpu.HBM)
SEM = pl.BlockSpec(memory_space=pltpu.SEMAPHORE)
EFFECT = pltpu.SideEffectType.DATAFLOW_SIDE_EFFECTING


def _exchange_copies(kind, srcs, lands):
    x, y, c, chips = _position()
    out = []
    for src, land in zip(srcs, lands):
        if kind == "swap":
            out.append((_half(src, 1 - c, (slice(None),)), land, (x, y, 1 - c)))
            continue
        for j, (cx, cy) in enumerate(chips):
            if kind == "gather":
                out.append((src, land.at[2 * x + y], (cx, cy, c)))
            else:
                out.append((src.at[2 * cx + cy], land.at[j], (cx, cy, c)))
    return out


def _land_shapes(kind, srcs):
    if kind == "gather":
        return [(N_CHIPS,) + s.shape for s in srcs]
    if kind == "swap":
        return [(N_CHIPS,) + _half_shape(*s.shape[1:]) for s in srcs]
    return [(3,) + s.shape[1:] for s in srcs]


def exchange_start(name, kind, srcs):
    n_w = len(srcs)
    shapes = _land_shapes(kind, srcs)
    n_sem = n_w if kind == "swap" else 3 * n_w

    def body(*refs):
        ins, lands = refs[:n_w], refs[n_w:2 * n_w]
        send_sems, recv_sems = refs[2 * n_w:2 * n_w + 2]
        token = refs[-1]
        for i, (src, dst, dev) in enumerate(_exchange_copies(kind, ins, lands)):
            pltpu.make_async_remote_copy(src_ref=src, dst_ref=dst, send_sem=send_sems.at[i], recv_sem=recv_sems.at[i],
                                         device_id=dev, device_id_type=MESH).start()
        token[...] = jnp.zeros_like(token)

    out = pl.pallas_call(
        body, name=name,
        out_shape=(pltpu.SemaphoreType.DMA((n_sem,)), pltpu.SemaphoreType.DMA((n_sem,)),
                   *[pltpu.HBM(s.shape, s.dtype) for s in srcs], *[pltpu.HBM(shp, s.dtype) for shp, s in zip(shapes, srcs)],
                   jax.ShapeDtypeStruct((8, LANES), F32)),
        in_specs=(HBM,) * (2 * n_w), out_specs=(SEM, SEM) + (HBM,) * (2 * n_w) + (pl.BlockSpec(memory_space=pltpu.VMEM),),
        input_output_aliases={i: 2 + i for i in range(2 * n_w)},
        compiler_params=pltpu.CompilerParams(has_side_effects=EFFECT),
    )(*[pltpu.with_memory_space_constraint(s, pltpu.HBM) for s in srcs],
      *[pltpu.with_memory_space_constraint(lax.empty(shp, s.dtype), pltpu.HBM) for shp, s in zip(shapes, srcs)])
    return (kind, n_w, out[:-1]), out[-1]


def exchange_wait(name, handle, after):
    kind, n_w, (send_sems, recv_sems, *thru) = handle
    n_sem = n_w if kind == "swap" else 3 * n_w

    def body(*refs):
        ins, lands = refs[:n_w], refs[n_w:2 * n_w]
        send_sems, recv_sems = refs[2 * n_w:2 * n_w + 2]
        for i, (src, dst, dev) in enumerate(_exchange_copies(kind, ins, lands)):
            cp = pltpu.make_async_remote_copy(src_ref=src, dst_ref=dst, send_sem=send_sems.at[i], recv_sem=recv_sems.at[i],
                                              device_id=dev, device_id_type=MESH)
            cp.wait_send()
            cp.wait_recv()

    out = pl.pallas_call(
        body, name=name, out_shape=tuple(pltpu.HBM(t.shape, t.dtype) for t in thru),
        in_specs=(HBM,) * (2 * n_w) + (SEM, SEM, pl.BlockSpec(memory_space=pl.ANY)), out_specs=(HBM,) * (2 * n_w),
        input_output_aliases={i: i for i in range(2 * n_w)},
        compiler_params=pltpu.CompilerParams(has_side_effects=EFFECT),
    )(*thru, send_sems, recv_sems, after)
    return list(out[n_w:])


def _row_tile(rows, cols):
    best = rows
    if rows * cols * 4 <= 1024 * 1024:
        return rows
    for t in range(16, rows, 16):
        if rows % t == 0 and t * cols * 4 <= 1024 * 1024:
            best = t
    return best


def pair_sum(name, pos, grad, from_sibling):
    _, rows, cols = grad.shape
    h_rows, h_cols = _half_shape(rows, cols)
    tr = _row_tile(h_rows, h_cols)
    n_t = h_rows // tr

    def body(pos_ref, g_ref, s_ref, b_ref, f_ref):
        tot = g_ref[...] + s_ref[...]
        b_ref[...] = tot.astype(BF16)

        @pl.when(pl.program_id(1) == pos_ref[1])
        def _():
            f_ref[...] = tot[0]

    blk = pl.BlockSpec((1, tr, h_cols), lambda i, k, pos: (k, i, 0))
    if _split_cols(rows):
        mine = pl.BlockSpec((1, tr, h_cols), lambda i, k, pos: (k, i, pos[0]))
    else:
        mine = pl.BlockSpec((1, tr, h_cols), lambda i, k, pos: (k, pos[0] * n_t + i, 0))
    return pl.pallas_call(
        body, grid_spec=pltpu.PrefetchScalarGridSpec(
            num_scalar_prefetch=1, grid=(n_t, N_CHIPS), in_specs=[mine, blk],
            out_specs=[blk, pl.BlockSpec((tr, h_cols), lambda i, k, pos: (i, 0))]),
        out_shape=[jax.ShapeDtypeStruct((N_CHIPS, h_rows, h_cols), BF16), jax.ShapeDtypeStruct((h_rows, h_cols), F32)],
        name=name, compiler_params=_cparams(2),
    )(pos, grad, from_sibling)


def chip_sum(name, pos, own, landed, split_cols):
    half, cols = own.shape
    tr = _row_tile(half, cols)
    n_t = half // tr

    def body(pos_ref, p_ref, l_ref, o_ref):
        o_ref[...] = ((p_ref[...] + l_ref[0].astype(F32)) + l_ref[1].astype(F32)) + l_ref[2].astype(F32)

    if split_cols:
        out_spec, out_shape = pl.BlockSpec((tr, cols), lambda i, pos: (i, pos[0])), (half, 2 * cols)
    else:
        out_spec, out_shape = pl.BlockSpec((tr, cols), lambda i, pos: (pos[0] * n_t + i, 0)), (2 * half, cols)
    return pl.pallas_call(
        body, grid_spec=pltpu.PrefetchScalarGridSpec(
            num_scalar_prefetch=1, grid=(n_t,),
            in_specs=[pl.BlockSpec((tr, cols), lambda i, pos: (i, 0)), pl.BlockSpec((3, tr, cols), lambda i, pos: (0, i, 0))],
            out_specs=out_spec),
        out_shape=jax.ShapeDtypeStruct(out_shape, F32), name=name, compiler_params=_cparams(1),
    )(pos, own, landed)


def reduce_scatter_layer(tag, pos, grads):
    n = lambda t: f"{t}_{tag}"
    from_sibling = swap_halves(n("swap_halves"), grads)
    sums = [pair_sum(n(f"pair_sum{w}"), pos, g, s) for w, (g, s) in enumerate(zip(grads, from_sibling))]
    landed = scatter_chips(n("scatter_chips"), [b for b, _ in sums])
    halves = [chip_sum(n(f"chip_sum{w}"), pos, own, l, _split_cols(g.shape[1])) for w, ((_, own), l, g) in enumerate(zip(sums, landed, grads))]
    return share_halves(n("share_halves"), halves)


class OverlappedReduceScatter:
    def __init__(self, tag, pos, grads):
        self.n = lambda t: f"{t}_{tag}"
        self.pos, self.grads = pos, grads
        self.swap, self.token = exchange_start(self.n("swap_start"), "swap", grads)

    def middle(self, after):
        from_sibling = exchange_wait(self.n("swap_wait"), self.swap, after)
        self.sums = [pair_sum(self.n(f"pair_sum{w}"), self.pos, g, s) for w, (g, s) in enumerate(zip(self.grads, from_sibling))]
        self.scatter, self.token = exchange_start(self.n("scatter_start"), "scatter", [b for b, _ in self.sums])

    def finish(self, after):
        landed = exchange_wait(self.n("scatter_wait"), self.scatter, after)
        halves = [chip_sum(self.n(f"chip_sum{w}"), self.pos, own, l, _split_cols(g.shape[1]))
                  for w, ((_, own), l, g) in enumerate(zip(self.sums, landed, self.grads))]
        return share_halves(self.n("share_halves"), halves)


def sum_devices(gathered):
    m_per = gathered.shape[0] // 8

    def body(g_ref, o_ref):
        tot = g_ref[pl.ds(0, m_per), :]
        for dev in range(1, 8):
            tot = tot + g_ref[pl.ds(dev * m_per, m_per), :]
        o_ref[...] = tot

    return pl.pallas_call(
        body, out_shape=jax.ShapeDtypeStruct((m_per, gathered.shape[1]), F32),
        in_specs=[pl.BlockSpec(memory_space=pltpu.VMEM)], out_specs=pl.BlockSpec(memory_space=pltpu.VMEM), name="sum_devices",
    )(gathered)


def kernel(x, p, g_mix, w_in, b_fox_f, fox_q_gain, fox_k_gain, sc_conv_w, dn_conv_w, dn_a_log, dn_dt_bias, dn_norm_gain, w_branch, w_o, g_ffn, w_up, ffn_conv_w, w_down, g_ple, w_ple_gate, w_ple, loss_target, m_g_mix, m_w_in, m_b_fox_f, m_fox_q_gain, m_fox_k_gain, m_sc_conv_w, m_dn_conv_w, m_dn_a_log, m_dn_dt_bias, m_dn_norm_gain, m_w_branch, m_w_o, m_g_ffn, m_w_up, m_ffn_conv_w, m_w_down, m_g_ple, m_w_ple_gate, m_w_ple, v_g_mix, v_w_in, v_b_fox_f, v_fox_q_gain, v_fox_k_gain, v_sc_conv_w, v_dn_conv_w, v_dn_a_log, v_dn_dt_bias, v_dn_norm_gain, v_w_branch, v_w_o, v_g_ffn, v_w_up, v_ffn_conv_w, v_w_down, v_g_ple, v_w_ple_gate, v_w_ple):
    a = dict(g_mix=g_mix, w_in=w_in, b_fox_f=b_fox_f, fox_q_gain=fox_q_gain, fox_k_gain=fox_k_gain, sc_conv_w=sc_conv_w,
             dn_conv_w=dn_conv_w, dn_a_log=dn_a_log, dn_dt_bias=dn_dt_bias, dn_norm_gain=dn_norm_gain, w_branch=w_branch, w_o=w_o,
             g_ffn=g_ffn, w_up=w_up, ffn_conv_w=ffn_conv_w, w_down=w_down, g_ple=g_ple, w_ple_gate=w_ple_gate, w_ple=w_ple)
    mom = dict(g_mix=m_g_mix, w_in=m_w_in, b_fox_f=m_b_fox_f, fox_q_gain=m_fox_q_gain, fox_k_gain=m_fox_k_gain, sc_conv_w=m_sc_conv_w,
               dn_conv_w=m_dn_conv_w, dn_a_log=m_dn_a_log, dn_dt_bias=m_dn_dt_bias, dn_norm_gain=m_dn_norm_gain, w_branch=m_w_branch,
               w_o=m_w_o, g_ffn=m_g_ffn, w_up=m_w_up, ffn_conv_w=m_ffn_conv_w, w_down=m_w_down, g_ple=m_g_ple, w_ple_gate=m_w_ple_gate,
               w_ple=m_w_ple)
    var = dict(g_mix=v_g_mix, w_in=v_w_in, b_fox_f=v_b_fox_f, fox_q_gain=v_fox_q_gain, fox_k_gain=v_fox_k_gain, sc_conv_w=v_sc_conv_w,
               dn_conv_w=v_dn_conv_w, dn_a_log=v_dn_a_log, dn_dt_bias=v_dn_dt_bias, dn_norm_gain=v_dn_norm_gain, w_branch=v_w_branch,
               w_o=v_w_o, g_ffn=v_g_ffn, w_up=v_w_up, ffn_conv_w=v_ffn_conv_w, w_down=v_w_down, g_ple=v_g_ple, w_ple_gate=v_w_ple_gate,
               w_ple=v_w_ple)
    cx, cy, cc = lax.axis_index("x"), lax.axis_index("y"), lax.axis_index("c")
    chip = 2 * cx + cy
    pos = jnp.stack([cc, chip]).astype(jnp.int32)

    def as_blocks(t):
        return t.reshape(2, -1, t.shape[-1])

    def own_block_in(got, shards):
        return [lax.dynamic_update_slice(g, s[None], (chip, 0, 0)) for g, s in zip(got, shards)]

    conv_shapes = [a[nm].shape for nm in CONVS]
    conv_all, conv_token = gather_small("gather_conv_w", pack_rows([a[nm] for nm in CONVS], F32))
    def layer_block(nm, t, li):
        return as_blocks(t)[li]

    shards0 = [(layer_block(nm, a[nm], 0) + conv_token[0, 0]).astype(BF16) for nm in BIG]
    got0, gathered_token = gather_layer("gather_w_in_l0", shards0[:1])
    shards0[1:] = [s + gathered_token[0, 0].astype(BF16) for s in shards0[1:]]
    gather0, gather0_token = exchange_start("gather_start_l0", "gather", shards0[1:])
    shards1 = [(layer_block(nm, a[nm], 1) + gather0_token[0, 0]).astype(BF16) for nm in BIG]
    gather1, gather1_token = exchange_start("gather_start_l1", "gather", shards1)
    conv_rows = conv_all.shape[0] // 8
    conv_chip = [unpack_rows(conv_all[2 * k * conv_rows:(2 * k + 1) * conv_rows], conv_shapes) for k in range(N_CHIPS)]
    conv = {nm: jnp.concatenate([conv_chip[k][i] for k in range(N_CHIPS)], axis=2) for i, nm in enumerate(CONVS)}

    weights, saved = [None, None], [None, None]
    first_weights = hang_on(layer_weights(0, own_block_in(got0, shards0[:1]), conv, a), gather1_token)

    def rest_of_layer0(after):
        return later_weights(own_block_in(exchange_wait("gather_wait_l0", gather0, after), shards0[1:]))

    act, saved[0], weights[0] = layer_fwd(0, x[0], p[0, 0], first_weights, more_weights=rest_of_layer0)
    got1 = exchange_wait("gather_wait_l1", gather1, act)
    act, saved[1], weights[1] = layer_fwd(1, act, p[1, 0], layer_weights(1, own_block_in(got1, shards1), conv, a))
    d_act, loss_part = loss_call(act, loss_target[0])
    loss = lax.psum(loss_part, ("x", "y", "c"))
    layer_grads = [None, None]
    d_act, layer_grads[1] = layer_bwd(1, d_act, saved[1], weights[1])
    rs1 = OverlappedReduceScatter("l1", pos, [layer_grads[1][nm] for nm in BIG])
    rs0 = []

    def stage_mid(after, g):
        rs1.middle(after)
        return rs1.token

    def stage_late(after, g):
        rs0.append(OverlappedReduceScatter("l0", pos, [g[nm] for nm in BIG[1:]]))
        return rs0[0].token

    def stage_last(after, g):
        rs0[0].middle(after)
        return rs0[0].token

    def stage_w_in(after, g):
        rs0.append(OverlappedReduceScatter("w_in_l0", pos, [g["w_in"]]))
        return rs0[1].token

    d_act, layer_grads[0] = layer_bwd(0, d_act, saved[0], hang_on(weights[0], rs1.token),
                                      hooks=dict(mid=stage_mid, late=stage_late, last=stage_last, w_in=stage_w_in))
    rs0[1].middle(d_act)
    reduced = [rs0[0].finish(rs0[1].token), rs1.finish(rs0[1].token)]
    grad_x = d_act[None]

    def both(nm):
        return jnp.stack([layer_grads[0][nm], layer_grads[1][nm]])

    local = {nm: both(nm) for nm in ("g_mix", "b_fox_f", "fox_q_gain", "fox_k_gain", "dn_norm_gain", "g_ffn", "g_ple", "sc_conv_w",
                                      "dn_conv_w", "ffn_conv_w")}
    local["dn_a_log"] = jnp.stack([layer_grads[li]["ad"][0] for li in range(2)])
    local["dn_dt_bias"] = jnp.stack([layer_grads[li]["ad"][1] for li in range(2)])

    small_names = SMALL + CONVS
    small_shapes = [local[nm].shape for nm in small_names]
    small_sum = sum_devices(gather_small("gather_small_grads", pack_rows([local[nm] for nm in small_names], F32))[0])
    small_grads = dict(zip(small_names, unpack_rows(small_sum, small_shapes)))
    for nm in CONVS:
        width = a[nm].shape[2]
        small_grads[nm] = lax.dynamic_slice_in_dim(small_grads[nm], chip * width, width, axis=2)

    grads, deltas, new_m, new_v = dict(small_grads), {}, {}, {}
    for nm in small_names:
        deltas[nm], new_m[nm], new_v[nm] = adam_call(f"adam_{nm}", a[nm], grads[nm], mom[nm], var[nm])
    for i, nm in enumerate(BIG[1:]):
        res = adam_layers(f"adam_{nm}", as_blocks(a[nm]), as_blocks(mom[nm]), as_blocks(var[nm]), reduced[0][i], reduced[1][1 + i])
        grads[nm], deltas[nm], new_m[nm], new_v[nm] = [r.reshape(a[nm].shape) for r in res]
    stored = lambda t: jnp.transpose(t, (2, 0, 1))
    res = adam_w_in("adam_w_in", stored(a["w_in"]), stored(mom["w_in"]), stored(var["w_in"]), rs0[1].finish(deltas["w_ple"])[0], reduced[1][0])
    grads["w_in"], deltas["w_in"], new_m["w_in"], new_v["w_in"] = [jnp.transpose(r, (1, 2, 0)) for r in res]
    return (loss, grad_x, *[grads[nm] for nm in WEIGHTS], *[deltas[nm] for nm in WEIGHTS], *[new_m[nm] for nm in WEIGHTS],
            *[new_v[nm] for nm in WEIGHTS])
```

```python
import functools

import jax
import jax.numpy as jnp
from jax import lax
from jax.experimental import pallas as pl
from jax.experimental.pallas import tpu as pltpu

F32 = jnp.float32
BF16 = jnp.bfloat16
HI = lax.Precision.HIGHEST
SOLVE = lax.Precision.HIGH
MESH = pl.DeviceIdType.MESH

D_MODEL = 1024
BRANCH = 512
FOX_DH = 64
DN_DH = 128
DN_HEADS = 4
DN_CHUNK = 64
FOX_BLOCK = 128
D_FF = 2816
EPS = 1e-6
N_CHIPS = 4
LANES = 128

ADAM_LR, ADAM_B1, ADAM_B2, ADAM_EPS, ADAM_WD, ADAM_STEP = 0.001, 0.9, 0.999, 1e-08, 0.01, 10

VMEM_LIMIT = 56 * 1024 * 1024

C_FQ, C_FK, C_FV, C_SB, C_SC, C_SV, C_DN, C_DZ, C_GATE = 0, 512, 1024, 1536, 2048, 2560, 3072, 4608, 5120
IN_MAIN = 8192
IN_SIZES = (1536, 8, 1536, 1536, 4, 4, 512, 3072)

BIG = ("w_in", "w_branch", "w_o", "w_up", "w_down", "w_ple_gate", "w_ple")
BIG_AXIS = {"w_in": 2, "w_branch": 3, "w_o": 1, "w_up": 2, "w_down": 1, "w_ple_gate": 1, "w_ple": 2}
CONVS = ("sc_conv_w", "dn_conv_w", "ffn_conv_w")
SMALL = ("g_mix", "b_fox_f", "fox_q_gain", "fox_k_gain", "dn_a_log", "dn_dt_bias", "dn_norm_gain", "g_ffn", "g_ple")
WEIGHTS = ("g_mix", "w_in", "b_fox_f", "fox_q_gain", "fox_k_gain", "sc_conv_w", "dn_conv_w", "dn_a_log", "dn_dt_bias",
           "dn_norm_gain", "w_branch", "w_o", "g_ffn", "w_up", "ffn_conv_w", "w_down", "g_ple", "w_ple_gate", "w_ple")


def _iota(shape, dim):
    return lax.broadcasted_iota(jnp.int32, shape, dim)


def _dg(a, b, mode, prec=None):
    dims = {"nn": ((1,), (0,)), "nt": ((1,), (1,)), "tn": ((0,), (0,))}[mode]
    return lax.dot_general(a, b, (dims, ((), ())), precision=prec, preferred_element_type=F32)


def _bdot_impl(a, b, mode):
    return _dg(a.astype(BF16), b.astype(BF16), mode)


@functools.partial(jax.custom_vjp, nondiff_argnums=(2,))
def _bdot_diff(a, b, mode):
    return _bdot_impl(a, b, mode)


def _bdot_fwd(a, b, mode):
    return _bdot_impl(a, b, mode), (a, b)


def _bdot_bwd(mode, res, g):
    a, b = res
    if mode == "nn":
        da, db = _bdot_impl(g, b, "nt"), _bdot_impl(a, g, "tn")
    elif mode == "nt":
        da, db = _bdot_impl(g, b, "nn"), _bdot_impl(g, a, "tn")
    else:
        da, db = _bdot_impl(b, g, "nt"), _bdot_impl(a, g, "nn")
    return da.astype(a.dtype), db.astype(b.dtype)


_bdot_diff.defvjp(_bdot_fwd, _bdot_bwd)


def _bdot(d):
    return _bdot_diff if d else _bdot_impl


def _shift_impl(x, k):
    return jnp.where(_iota(x.shape, 0) >= k, pltpu.roll(x, k, 0), 0.0)


def _unshift_impl(g, k):
    n = g.shape[0]
    return jnp.where(_iota(g.shape, 0) < n - k, pltpu.roll(g, n - k, 0), 0.0)


@functools.partial(jax.custom_vjp, nondiff_argnums=(1,))
def _shift_diff(x, k):
    return _shift_impl(x, k)


_shift_diff.defvjp(lambda x, k: (_shift_impl(x, k), None), lambda k, _, g: (_unshift_impl(g, k),))


def _row(w, j):
    return jnp.sum(jnp.where(_iota(w.shape, 0) == j, w, 0.0), axis=0, keepdims=True)


def _col(w, j):
    return jnp.sum(jnp.where(_iota(w.shape, 1) == j, w, 0.0), axis=1, keepdims=True)


def _conv(d, x, w):
    shift = _shift_diff if d else _shift_impl
    taps = w.shape[0]
    y = x * _row(w, taps - 1)
    for j in range(taps - 1):
        y = y + shift(x, taps - 1 - j) * _row(w, j)
    return y


def _softplus(x):
    return jnp.maximum(x, 0.0) + jnp.log(1.0 + jnp.exp(-jnp.abs(x)))


def _silu(x):
    return x * jax.nn.sigmoid(x)


def _rms(x, gain):
    return x * lax.rsqrt(jnp.mean(x * x, axis=-1, keepdims=True) + EPS) * gain


def _rms_fn(d, pids, x, gain):
    return (_rms(x, gain),)


def _loss_fn(d, pids, y, t):
    e = y - t
    part = 0.5 / D_MODEL * jnp.sum(e * e, keepdims=True)
    return e * (1.0 / D_MODEL), jnp.broadcast_to(part, (8, LANES))


def _fox_prep_fn(d, pids, q, k, gq, gk):
    first = _iota(q.shape, 1) < FOX_DH

    def norm(x, gain):
        sq = x * x
        ss_a = jnp.sum(jnp.where(first, sq, 0.0), axis=1, keepdims=True)
        ss_b = jnp.sum(jnp.where(first, 0.0, sq), axis=1, keepdims=True)
        rs = jnp.where(first, lax.rsqrt(ss_a / FOX_DH + EPS), lax.rsqrt(ss_b / FOX_DH + EPS))
        return x * rs * gain

    return norm(q, gq) * FOX_DH ** -0.5, norm(k, gk)


def _fox_gate_fn(d, pids, f, bias):
    logf = -_softplus(-(f + bias))
    n_r, n_c = logf.shape
    tri = (_iota((n_c, n_c), 0) <= _iota((n_c, n_c), 1)).astype(F32)
    within = _dg(logf, tri, "nn", HI)
    tot = jnp.broadcast_to(jnp.sum(logf, axis=1, keepdims=True), logf.shape)
    below = (_iota((n_r, n_r), 1) < _iota((n_r, n_r), 0)).astype(F32)
    return (within + _dg(below, tot, "nn", HI),)


def _fox_attn_fn(q_block0, d, pids, q, k, v, cq_a, cq_b, ck_a, ck_b):
    dot = _bdot(d)
    first = _iota(q.shape, 1) < FOX_DH
    n_q, n_k = q.shape[0], k.shape[0]
    causal = ((q_block0 + pids[1]) * n_q + _iota((n_q, n_k), 0)) >= _iota((n_q, n_k), 1)

    qs = [jnp.where(first, q, 0.0), jnp.where(first, 0.0, q)]
    s = _each(lambda qh, cq, ck: jnp.where(causal, dot(qh, k, "nt") + cq - ck, -1e30), qs, [cq_a, cq_b], [ck_a, ck_b])
    e = [jnp.exp(si - lax.stop_gradient(jnp.max(si, axis=1, keepdims=True))) for si in s]
    o_a, o_b = [dot(ei / jnp.sum(ei, axis=1, keepdims=True), v, "nn") for ei in e]
    return (jnp.where(first, o_a, o_b),)


def _sconv_fn(d, pids, sb, sc, sv, w):
    return (sb * _conv(d, sc * sv, w),)


def _dnconv_fn(d, pids, x, w):
    return (_silu(_conv(d, x, w)),)


def _merge_fn(d, pids, y0, y1, y2, g0, g1, g2):
    return (jax.nn.sigmoid(g0) * y0 + jax.nn.sigmoid(g1) * y1 + jax.nn.sigmoid(g2) * y2,)


def _ffn_act_fn(d, pids, ug, uv, wg, wv):
    return (_silu(_conv(d, ug, wg)) * _conv(d, uv, wv),)


def _ple_fn(d, pids, gpre, pe, x):
    return (x + jax.nn.sigmoid(gpre) * pe,)


def _adam_fn(d, pids, w, g, m, v):
    m2 = ADAM_B1 * m + (1.0 - ADAM_B1) * g
    v2 = ADAM_B2 * v + (1.0 - ADAM_B2) * (g * g)
    m_hat = m2 / (1.0 - ADAM_B1 ** ADAM_STEP)
    v_hat = v2 / (1.0 - ADAM_B2 ** ADAM_STEP)
    delta = -ADAM_LR * (m_hat / (jnp.sqrt(v_hat) + ADAM_EPS) + ADAM_WD * w)
    return delta, m2, v2


def _each(fn, *lists):
    return [fn(*args) for args in zip(*lists)]


def _tri_inv_impl(mats):
    n = mats[0].shape[0]
    r, c = _iota((n, n), 0), _iota((n, n), 1)
    diag_blk = (r >> 4) == (c >> 4)
    eye = (r == c).astype(F32)
    mm = lambda us, ws: _each(lambda u, w: _dg(u, w, "nn", SOLVE), us, ws)
    grow = lambda ps, xs: _each(lambda p, px: p + px, ps, mm(ps, xs))
    x = [jnp.where(diag_blk, -a, 0.0) for a in mats]
    p = [eye + xi for xi in x]
    x2 = mm(x, x)
    p = grow(p, x2)
    x4 = mm(x2, x2)
    p = grow(p, x4)
    p = grow(p, mm(x4, x4))
    y = [-yi for yi in mm(p, [jnp.where(diag_blk, 0.0, a) for a in mats])]
    q = grow([eye + yi for yi in y], mm(y, y))
    return mm(q, p)


@jax.custom_vjp
def _tri_inv_diff(mats):
    return _tri_inv_impl(mats)


def _tri_inv_fwd(mats):
    ts = _tri_inv_impl(mats)
    return ts, ts


def _tri_inv_bwd(ts, gs):
    left = _each(lambda t, g: _dg(t, g, "tn", SOLVE), ts, gs)
    return ([-m for m in _each(lambda l, t: _dg(l, t, "nt", SOLVE), left, ts)],)


_tri_inv_diff.defvjp(_tri_inv_fwd, _tri_inv_bwd)


def _dn_local(d, qs, ks, vs, a_cs, a_rs, b_cs, a_logs, dt_bs):
    dot = _bdot(d)
    inv = _tri_inv_diff if d else _tri_inv_impl
    n = qs[0].shape[0]
    r, c = _iota((n, n), 0), _iota((n, n), 1)
    incl, strict, upper = r >= c, r > c, r <= c
    qs = [q * lax.rsqrt(jnp.sum(q * q, axis=1, keepdims=True) + EPS) * DN_DH ** -0.5 for q in qs]
    ks = [k * lax.rsqrt(jnp.sum(k * k, axis=1, keepdims=True) + EPS) for k in ks]
    betas = [jax.nn.sigmoid(b) for b in b_cs]
    rates = [-jnp.exp(a) for a in a_logs]
    g_cs = _each(lambda rate, a, dt: rate * _softplus(a + dt), rates, a_cs, dt_bs)
    g_rs = _each(lambda rate, a, dt: rate * _softplus(a + dt), rates, a_rs, dt_bs)
    gcum_cs = [jnp.sum(jnp.where(incl, g, 0.0), axis=1, keepdims=True) for g in g_rs]
    gcum_rs = [jnp.sum(jnp.where(upper, g, 0.0), axis=0, keepdims=True) for g in g_cs]
    decays = _each(lambda gc, gr: jnp.exp(jnp.where(incl, gc - gr, -1e30)), gcum_cs, gcum_rs)
    kbs = _each(lambda k, b: k * b, ks, betas)
    kk = _each(lambda kb, k: dot(kb, k, "nt"), kbs, ks)
    ts = inv(_each(lambda m, dec: jnp.where(strict, m * dec, 0.0), kk, decays))
    e_gs = [jnp.exp(g) for g in gcum_cs]
    us = _each(lambda t, v, b: _dg(t, v * b, "nn", SOLVE), ts, vs, betas)
    k_cums = _each(lambda t, kb, e: _dg(t, kb * e, "nn", SOLVE), ts, kbs, e_gs)
    qk = _each(lambda q, k: dot(q, k, "nt"), qs, ks)
    qk = _each(lambda m, dec: jnp.where(incl, m * dec, 0.0), qk, decays)
    g_lasts = [jnp.sum(g, axis=0, keepdims=True) for g in g_cs]
    q_decs = _each(lambda q, e: q * e, qs, e_gs)
    k_decs = _each(lambda k, gl, gc: k * jnp.exp(gl - gc), ks, g_lasts, gcum_cs)
    return list(zip(us, k_cums, q_decs, k_decs, qk, g_lasts))


def _dn_step(d, s_prevs, items, zs, gain):
    dot = _bdot(d)
    us, k_cums, q_decs, k_decs, qks, g_lasts = [list(t) for t in zip(*items)]
    v_news = _each(lambda u, kc, s: u - dot(kc, s, "nn"), us, k_cums, s_prevs)
    inter = _each(lambda qd, s: dot(qd, s, "nn"), q_decs, s_prevs)
    outs = _each(lambda o, qk, vn: o + dot(qk, vn, "nn"), inter, qks, v_news)
    s_nexts = _each(lambda s, gl, kd, vn: s * jnp.exp(gl) + dot(kd, vn, "tn"), s_prevs, g_lasts, k_decs, v_news)
    return _each(lambda o, z: _rms(o, gain) * _silu(z), outs, zs), s_nexts


def _split_heads(t):
    return [t[:, h * DN_DH:(h + 1) * DN_DH] for h in range(t.shape[1] // DN_DH)]


def _dn_gates(ps, a_rows, ad):
    hs = range(DN_HEADS)
    return ([_col(ps, 12 + h) for h in hs], [_row(a_rows, h) for h in hs], [_col(ps, 8 + h) for h in hs],
            [_col(_row(ad, 0), h) for h in hs], [_col(_row(ad, 1), h) for h in hs])


def _head_rows(vals):
    row = _iota((8, LANES), 0)
    tile = jnp.zeros((8, LANES), F32)
    for h, val in enumerate(vals):
        tile = tile + jnp.where(row == h, val, 0.0)
    return tile


def _cparams(n_axes):
    return pltpu.CompilerParams(dimension_semantics=("arbitrary",) * n_axes, vmem_limit_bytes=VMEM_LIMIT)


def _first_visit(acc_axes):
    cond = None
    for a in acc_axes:
        here = pl.program_id(a) == 0
        cond = here if cond is None else jnp.logical_and(cond, here)
    return cond


def _tile(ref, widen=False):
    val = ref[...]
    shape = val.shape
    while len(shape) > 2 and shape[0] == 1:
        shape = shape[1:]
    val = val.reshape(shape)
    return val.astype(F32) if widen and val.dtype == BF16 else val


def _store(ref, val, first):
    val = val.astype(ref.dtype).reshape(ref.shape)
    if first is None:
        ref[...] = val
        return

    @pl.when(first)
    def _():
        ref[...] = val

    @pl.when(jnp.logical_not(first))
    def _():
        ref[...] += val


def _specs(ops):
    return [pl.BlockSpec(block, imap) for _, block, imap in ops]


def tile_fwd(name, fn, grid, ins, outs, raw=()):
    n_in = len(ins)

    def body(*refs):
        pids = tuple(pl.program_id(a) for a in range(len(grid)))
        firsts = [_first_visit(o[4]) if o[4] else None for o in outs]
        res = fn(False, pids, *[_tile(r, i not in raw) for i, r in enumerate(refs[:n_in])])
        for ref, val, first in zip(refs[n_in:], res, firsts):
            _store(ref, val, first)

    out = pl.pallas_call(
        body, grid=grid, in_specs=_specs(ins),
        out_specs=[pl.BlockSpec(o[2], o[3]) for o in outs],
        out_shape=[jax.ShapeDtypeStruct(o[0], o[1]) for o in outs],
        name=name, compiler_params=_cparams(len(grid)),
    )(*[a for a, _, _ in ins])
    return out


def tile_bwd(name, fn, grid, ins, cots, diff, adds=None, raw=()):
    adds = adds or {}
    n_in, n_cot = len(ins), len(cots)
    add_pos = sorted(adds)
    diff_idx = [d[0] for d in diff]
    out_desc = [d[2] if len(d) > 2 and d[2] is not None else (ins[d[0]][0].shape, ins[d[0]][1], ins[d[0]][2]) for d in diff]
    out_dtypes = [d[3] if len(d) > 3 else F32 for d in diff]

    def body(*refs):
        pids = tuple(pl.program_id(a) for a in range(len(grid)))
        firsts = [_first_visit(d[1]) if d[1] else None for d in diff]
        vals = [_tile(r, i not in raw) for i, r in enumerate(refs[:n_in])]
        cot_vals = [_tile(r, True) for r in refs[n_in:n_in + n_cot]]
        add_vals = [_tile(r) for r in refs[n_in + n_cot:n_in + n_cot + len(add_pos)]]
        out_refs = refs[n_in + n_cot + len(add_pos):]

        def f(*dv):
            full = list(vals)
            for i, val in zip(diff_idx, dv):
                full[i] = val
            return fn(True, pids, *full)

        prim, vjp = jax.vjp(f, *[vals[i].astype(F32) for i in diff_idx])
        grads = list(vjp(tuple(c.astype(o.dtype) for c, o in zip(cot_vals, prim))))
        for pos, val in zip(add_pos, add_vals):
            grads[pos] = grads[pos] + val.astype(F32)
        for ref, val, first in zip(out_refs, grads, firsts):
            _store(ref, val, first)

    all_ins = list(ins) + list(cots) + [adds[p] for p in add_pos]
    out = pl.pallas_call(
        body, grid=grid, in_specs=_specs(all_ins),
        out_specs=[pl.BlockSpec(o[1], o[2]) for o in out_desc],
        out_shape=[jax.ShapeDtypeStruct(o[0], dt) for o, dt in zip(out_desc, out_dtypes)],
        name=name, compiler_params=_cparams(len(grid)),
    )(*[a for a, _, _ in all_ins])
    return out


def _pick(dim, cands):
    for c in cands:
        if dim % c == 0:
            return c
    return dim


MM_TILES = (1024, 512, 1408, 256, 128)


def mm(name, a, b, mode, add=None, out_dtype=F32, blocks=None):
    wide = None
    if mode == "nn":
        (m, kk), n = a.shape, b.shape[-1]
    elif mode == "nt":
        (m, kk), n = a.shape, b.shape[-2]
    else:
        (kk, m), n = a.shape, b.shape[1]
    if blocks is not None:
        lo, n_blk = blocks
        wide = b.shape[-1] if mode != "tn" else n // n_blk
        if mode == "nn":
            n = wide * n_blk
    tm = _pick(m, MM_TILES)
    if mode == "nt" and blocks is not None:
        tn, tk = _pick(n, MM_TILES), _pick(wide, MM_TILES[:-1])
    elif blocks is not None:
        tn, tk = _pick(wide, MM_TILES[:-1]), _pick(kk, MM_TILES)
    else:
        tn, tk = _pick(n, MM_TILES), _pick(kk, MM_TILES)
    nk = kk // tk
    a_spec = pl.BlockSpec((tk, tm), lambda i, j, k: (k, i)) if mode == "tn" else pl.BlockSpec((tm, tk), lambda i, j, k: (i, k))
    o_spec = pl.BlockSpec((tm, tn), lambda i, j, k: (i, j))
    out_shape = (m, n)
    if blocks is None:
        b_spec = pl.BlockSpec((tn, tk), lambda i, j, k: (j, k)) if mode == "nt" else pl.BlockSpec((tk, tn), lambda i, j, k: (k, j))
    elif mode == "nn":
        per = wide // tn
        b_spec = pl.BlockSpec((1, tk, tn), lambda i, j, k: (lo + j // per, k, j % per))
    elif mode == "nt":
        per = wide // tk
        b_spec = pl.BlockSpec((1, tn, tk), lambda i, j, k: (lo + k // per, j, k % per))
    else:
        per = wide // tn
        b_spec = pl.BlockSpec((tk, tn), lambda i, j, k: (k, j))
        o_spec = pl.BlockSpec((1, tm, tn), lambda i, j, k: (j // per, i, j % per))
        out_shape = (n_blk, m, wide)

    def body(*refs):
        a_ref, b_ref = refs[0], refs[1]
        add_ref = refs[2] if add is not None else None
        o_ref, acc = refs[-2], refs[-1]
        k = pl.program_id(2)
        part = _bdot_impl(_tile(a_ref), _tile(b_ref), mode)

        @pl.when(k == 0)
        def _():
            acc[...] = part

        @pl.when(k > 0)
        def _():
            acc[...] += part

        @pl.when(k == nk - 1)
        def _():
            res = acc[...]
            if add_ref is not None:
                res = res + add_ref[...]
            o_ref[...] = res.astype(o_ref.dtype).reshape(o_ref.shape)

    operands = [a, b] + ([add] if add is not None else [])
    in_specs = [a_spec, b_spec] + ([o_spec] if add is not None else [])
    return pl.pallas_call(
        body, grid=(m // tm, n // tn, nk), in_specs=in_specs, out_specs=o_spec,
        out_shape=jax.ShapeDtypeStruct(out_shape, out_dtype),
        scratch_shapes=[pltpu.VMEM((tm, tn), F32)],
        name=name, compiler_params=_cparams(3),
    )(*operands)


def _rows(x, width=None, off=0, tm=256):
    width = x.shape[1] if width is None else width
    return (x, (tm, width), lambda i, off=off: (i, off))


def _whole(x):
    nd = x.ndim
    return (x, x.shape, lambda *pids, nd=nd: (0,) * nd)


def _rms_ops(x, gain):
    return [_rows(x), _whole(gain)]


def rms_fwd(name, x, gain):
    s, dm = x.shape
    return tile_fwd(name, _rms_fn, (s // 256,), _rms_ops(x, gain), [((s, dm), BF16, (256, dm), lambda i: (i, 0), ())])[0]


def rms_bwd(name, x, gain, dh, dres):
    s = x.shape[0]
    return tile_bwd(name, _rms_fn, (s // 256,), _rms_ops(x, gain), [_rows(dh)], [(0, ()), (1, (0,))], adds={0: _rows(dres)})


def loss_call(y, t):
    s, dm = y.shape
    dy, part = tile_fwd("loss", _loss_fn, (s // 256,), [_rows(y), _rows(t)],
                        [((s, dm), F32, (256, dm), lambda i: (i, 0), ()), ((8, LANES), F32, (8, LANES), lambda i: (0, 0), (0,))])
    return dy, part[0, 0]


def _fox_prep_ops(pm, gq, gk):
    tm = 512
    return [(pm, (tm, LANES), lambda i, j: (i, C_FQ // LANES + j)), (pm, (tm, LANES), lambda i, j: (i, C_FK // LANES + j)),
            _whole(gq), _whole(gk)]


def fox_prep_fwd(name, pm, gq, gk):
    s = pm.shape[0]
    out = ((s, BRANCH), BF16, (512, LANES), lambda i, j: (i, j), ())
    return tile_fwd(name, _fox_prep_fn, (s // 512, 4), _fox_prep_ops(pm, gq, gk), [out, out])


def fox_prep_bwd(name, pm, gq, gk, dqn, dkn):
    s = pm.shape[0]
    cot = lambda g: (g, (512, LANES), lambda i, j: (i, j))
    own = ((s, BRANCH), (512, LANES), lambda i, j: (i, j))
    return tile_bwd(name, _fox_prep_fn, (s // 512, 4), _fox_prep_ops(pm, gq, gk), [cot(dqn), cot(dkn)],
                    [(0, (), own, BF16), (1, (), own, BF16), (2, (0, 1)), (3, (0, 1))])


def _fox_gate_ops(f_t, bias):
    return [(f_t, (1,) + f_t.shape[1:], lambda h: (h, 0, 0)), (bias, (1, 1, 1), lambda h: (h, 0, 0))]


def fox_gate_fwd(name, f_t, bias):
    n_h = f_t.shape[0]
    return tile_fwd(name, _fox_gate_fn, (n_h,), _fox_gate_ops(f_t, bias),
                    [(f_t.shape, F32, (1,) + f_t.shape[1:], lambda h: (h, 0, 0), ())])[0]


def fox_gate_bwd(name, f_t, bias, dcum):
    n_h = f_t.shape[0]
    return tile_bwd(name, _fox_gate_fn, (n_h,), _fox_gate_ops(f_t, bias),
                    [(dcum, (1,) + f_t.shape[1:], lambda h: (h, 0, 0))], [(0, ()), (1, ())])


FOX_GROUPS = 4


def _fox_groups(s):
    per = s // FOX_BLOCK // FOX_GROUPS
    return [(g * per, per, (g + 1) * per * FOX_BLOCK) for g in range(FOX_GROUPS)]


def _fox_attn_ops(qn, kn, pm, cum_c, cum_r, q0, keys):
    nb = FOX_BLOCK
    return [(qn, (nb, LANES), lambda p, i: (q0 + i, p)), (kn, (keys, LANES), lambda p, i: (0, p)),
            (pm, (keys, LANES), lambda p, i: (0, C_FV // LANES + p)),
            (cum_c, (1, nb, 1), lambda p, i: (2 * p, q0 + i, 0)), (cum_c, (1, nb, 1), lambda p, i: (2 * p + 1, q0 + i, 0)),
            (cum_r, (1, 1, keys), lambda p, i: (2 * p, 0, 0)), (cum_r, (1, 1, keys), lambda p, i: (2 * p + 1, 0, 0))]


def fox_attn_fwd(name, qn, kn, pm, cum_c, cum_r):
    s = qn.shape[0]
    parts = []
    for g, (q0, n_q, keys) in enumerate(_fox_groups(s)):
        parts.append(tile_fwd(f"{name}_g{g}", functools.partial(_fox_attn_fn, q0), (4, n_q), _fox_attn_ops(qn, kn, pm, cum_c, cum_r, q0, keys),
                              [((n_q * FOX_BLOCK, BRANCH), BF16, (FOX_BLOCK, LANES), lambda p, i: (i, p), ())], raw=(0, 1, 2))[0])
    return jnp.concatenate(parts, axis=0)


def fox_attn_bwd(name, qn, kn, pm, cum_c, cum_r, dy):
    s = qn.shape[0]
    d_qn, d_kn, d_v, d_cum = [], 0.0, 0.0, 0.0
    for g, (q0, n_q, keys) in enumerate(_fox_groups(s)):
        rows = n_q * FOX_BLOCK
        own_q = ((rows, BRANCH), (FOX_BLOCK, LANES), lambda p, i: (i, p))
        own_k = ((keys, BRANCH), (keys, LANES), lambda p, i: (0, p))
        pair_c = ((4, rows, 1), (1, FOX_BLOCK, 1), lambda p, i: (p, i, 0))
        pair_r = ((4, 1, keys), (1, 1, keys), lambda p, i: (p, 0, 0))
        g_qn, g_kn, g_v, g_cqa, g_cqb, g_cka, g_ckb = tile_bwd(
            f"{name}_g{g}", functools.partial(_fox_attn_fn, q0), (4, n_q), _fox_attn_ops(qn, kn, pm, cum_c, cum_r, q0, keys),
            [(dy, (FOX_BLOCK, LANES), lambda p, i, q0=q0: (q0 + i, p))],
            [(0, (), own_q), (1, (1,), own_k), (2, (1,), own_k), (3, (), pair_c), (4, (), pair_c), (5, (1,), pair_r), (6, (1,), pair_r)])
        d_qn.append(g_qn)
        tail = lambda t, axis: jnp.pad(t, [(0, s - keys) if ax == axis else (0, 0) for ax in range(t.ndim)])
        d_kn, d_v = d_kn + tail(g_kn, 0), d_v + tail(g_v, 0)
        by_q = jnp.stack([g_cqa[:, :, 0], g_cqb[:, :, 0]], axis=1).reshape(8, rows)
        by_k = jnp.stack([g_cka[:, 0, :], g_ckb[:, 0, :]], axis=1).reshape(8, keys)
        d_cum = d_cum + jnp.pad(by_q, [(0, 0), (q0 * FOX_BLOCK, s - q0 * FOX_BLOCK - rows)]) + tail(by_k, 1)
    return jnp.concatenate(d_qn, axis=0), d_kn, d_v, d_cum


def sconv_ops(pm, w):
    s = pm.shape[0]
    blk = lambda c0: (pm, (s, LANES), lambda j, c0=c0: (0, c0 // LANES + j))
    return [blk(C_SB), blk(C_SC), blk(C_SV), (w, (w.shape[0], LANES), lambda j: (0, j))]


def dnconv_ops(pm, w):
    s = pm.shape[0]
    return [(pm, (s, LANES), lambda j: (0, C_DN // LANES + j)), (w, (w.shape[0], LANES), lambda j: (0, j))]


def ffn_ops(ug, uv, w):
    s = ug.shape[0]
    n_t = D_FF // LANES
    return [(ug, (s, LANES), lambda j: (0, j)), (uv, (s, LANES), lambda j: (0, j)),
            (w, (w.shape[0], LANES), lambda j: (0, j)), (w, (w.shape[0], LANES), lambda j: (0, n_t + j))]


def _col_out(s, width, dtype=F32):
    return ((s, width), dtype, (s, LANES), lambda j: (0, j), ())


def _col_cot(g):
    return (g, (g.shape[0], LANES), lambda j: (0, j))


def merge_ops(yp, pm):
    gate = lambda b: (pm, (256, D_MODEL), lambda i, b=b: (i, C_GATE // D_MODEL + b))
    return [_rows(yp[0]), _rows(yp[1]), _rows(yp[2]), gate(0), gate(1), gate(2)]


def ple_ops(gpre, pe, x):
    return [_rows(gpre), _rows(pe), _rows(x)]


def adam_call(name, w, g, m, v):
    shape = w.shape
    last = shape[-1]
    rows = w.size // last
    flat = lambda t: t.reshape(rows, last)
    tm = rows
    for cand in (512, 256, 128, 64, 32, 16, 8):
        if rows % cand == 0 and cand * last * 4 <= 2 * 1024 * 1024:
            tm = cand
            break
    spec = lambda t: (flat(t), (tm, last), lambda i: (i, 0))
    out = ((rows, last), F32, (tm, last), lambda i: (i, 0), ())
    res = tile_fwd(name, _adam_fn, (rows // tm,), [spec(w), spec(g), spec(m), spec(v)], [out, out, out])
    return [r.reshape(shape) for r in res]


def _adam_layers_fn(d, pids, w, m, v, g0, g1):
    g = jnp.where(pids[0] == 0, g0, g1)
    return (g,) + _adam_fn(d, pids, w, g, m, v)


def adam_layers(name, w, m, v, g0, g1):
    _, rows, cols = w.shape
    tm = _row_tile(rows, cols)
    n_t = rows // tm
    lay = lambda t: (t, (1, tm, cols), lambda l, i: (l, i, 0))
    ins = [lay(w), lay(m), lay(v), (g0, (tm, cols), lambda l, i: (i * (1 - l) + (n_t - 1) * l, 0)), (g1, (tm, cols), lambda l, i: (i * l, 0))]
    out = (w.shape, F32, (1, tm, cols), lambda l, i: (l, i, 0), ())
    return tile_fwd(name, _adam_layers_fn, (2, n_t), ins, [out, out, out, out])


def adam_w_in(name, w, m, v, g0, g1):
    rows, n_l, cols = w.shape

    def body(w_ref, m_ref, v_ref, g0_ref, g1_ref, g_out, d_out, m_out, v_out):
        step = 64

        def update(at):
            for l, g_ref in enumerate((g0_ref, g1_ref)):
                g = g_ref[at, :]
                delta, m2, v2 = _adam_fn(False, None, w_ref[at, l, :], g, m_ref[at, l, :], v_ref[at, l, :])
                for ref, val in ((g_out, g), (d_out, delta), (m_out, m2), (v_out, v2)):
                    ref[at, l, :] = val

        def some_rows(i, carry):
            update(pl.ds(pl.multiple_of(i * step, step), step))
            return carry

        lax.fori_loop(0, rows // step, some_rows, 0)
        if rows % step:
            update(pl.ds(rows - rows % step, rows % step))

    both = pl.BlockSpec((rows, n_l, LANES), lambda j: (0, 0, j))
    one = pl.BlockSpec((rows, LANES), lambda j: (0, j))
    return pl.pallas_call(
        body, grid=(cols // LANES,), in_specs=[both, both, both, one, one], out_specs=[both] * 4,
        out_shape=[jax.ShapeDtypeStruct(w.shape, F32)] * 4, name=name, compiler_params=_cparams(1),
    )(w, m, v, g0, g1)


DN_GROUP = 4


def _dn_local_specs(rev_n=None):
    rows = DN_GROUP * DN_CHUNK
    idx = (lambda j: j) if rev_n is None else (lambda j: rev_n - 1 - j)
    return [pl.BlockSpec((rows, 3 * BRANCH), lambda j: (idx(j), 0)), pl.BlockSpec((rows, LANES), lambda j: (idx(j), 0)),
            pl.BlockSpec((DN_GROUP, DN_HEADS, DN_CHUNK), lambda j: (idx(j), 0, 0)), pl.BlockSpec((2, DN_HEADS), lambda j: (0, 0))]


def _dn_group_inputs(qkv, ps, a_rows, c):
    lo = c * DN_CHUNK
    heads = _split_heads(qkv[lo:lo + DN_CHUNK])
    return heads[0:4], heads[4:8], heads[8:12], ps[lo:lo + DN_CHUNK], a_rows[c]


def dn_local_fwd(name, dn_act, ps, a_rows, ad):
    s = dn_act.shape[0]
    n_c, n_g = s // DN_CHUNK, s // (DN_GROUP * DN_CHUNK)
    rows = DN_GROUP * DN_CHUNK

    def body(qkv_ref, ps_ref, ar_ref, ad_ref, u_ref, kc_ref, qd_ref, kd_ref, qk_ref, gl_ref):
        qkv, ps_v, a_rows_v, ad_v = qkv_ref[...], ps_ref[...], ar_ref[...], ad_ref[...]
        args = [[] for _ in range(8)]
        for c in range(DN_GROUP):
            q4, k4, v4, ps_c, ar_c = _dn_group_inputs(qkv, ps_v, a_rows_v, c)
            for lst, vals in zip(args, (q4, k4, v4) + _dn_gates(ps_c, ar_c, ad_v)):
                lst.extend(vals)
        everything = _dn_local(False, *args)
        for c in range(DN_GROUP):
            res = everything[c * DN_HEADS:(c + 1) * DN_HEADS]
            at = pl.ds(c * DN_CHUNK, DN_CHUNK)
            for ref, i in ((u_ref, 0), (kc_ref, 1), (qd_ref, 2), (kd_ref, 3)):
                ref[at, :] = jnp.concatenate([r[i] for r in res], axis=1)
            for h in range(DN_HEADS):
                qk_ref[c, h] = res[h][4]
            gl_ref[c] = _head_rows([r[5] for r in res])

    wide = pl.BlockSpec((rows, BRANCH), lambda j: (j, 0))
    return pl.pallas_call(
        body, grid=(n_g,), in_specs=_dn_local_specs(),
        out_specs=[wide, wide, wide, wide, pl.BlockSpec((DN_GROUP, DN_HEADS, DN_CHUNK, DN_CHUNK), lambda j: (j, 0, 0, 0)),
                   pl.BlockSpec((DN_GROUP, 8, LANES), lambda j: (j, 0, 0))],
        out_shape=[jax.ShapeDtypeStruct((s, BRANCH), F32)] * 4 + [jax.ShapeDtypeStruct((n_c, DN_HEADS, DN_CHUNK, DN_CHUNK), F32),
                                                                 jax.ShapeDtypeStruct((n_c, 8, LANES), F32)],
        name=name, compiler_params=_cparams(1),
    )(dn_act, ps, a_rows, ad)


def dn_local_bwd(name, dn_act, ps, a_rows, ad, cots):
    s = dn_act.shape[0]
    n_c, n_g = s // DN_CHUNK, s // (DN_GROUP * DN_CHUNK)
    rows = DN_GROUP * DN_CHUNK

    def body(qkv_ref, ps_ref, ar_ref, ad_ref, du_ref, dkc_ref, dqd_ref, dkd_ref, dqk_ref, dgl_ref, dqkv_ref, dps_ref, dar_ref, dad_ref):
        first = pl.program_id(0) == 0
        qkv, ps_v, a_rows_v, ad_v = qkv_ref[...], ps_ref[...], ar_ref[...], ad_ref[...]
        d_wide = [r[...] for r in (du_ref, dkc_ref, dqd_ref, dkd_ref)]
        qs, ks, vs, ps_cs, ar_cs, cot = [], [], [], [], [], []
        for c in range(DN_GROUP):
            q4, k4, v4, ps_c, ar_c = _dn_group_inputs(qkv, ps_v, a_rows_v, c)
            qs, ks, vs, ps_cs, ar_cs = qs + q4, ks + k4, vs + v4, ps_cs + [ps_c], ar_cs + [ar_c]
            lo = c * DN_CHUNK
            d_tiles = [_split_heads(t[lo:lo + DN_CHUNK]) for t in d_wide]
            d_gl = dgl_ref[c]
            cot += [(d_tiles[0][h], d_tiles[1][h], d_tiles[2][h], d_tiles[3][h], dqk_ref[c, h], _col(_row(d_gl, h), 0))
                    for h in range(DN_HEADS)]

        def f(qs, ks, vs, ps_cs, ar_cs, ad_v):
            gates = [[] for _ in range(5)]
            for ps_c, ar_c in zip(ps_cs, ar_cs):
                for lst, vals in zip(gates, _dn_gates(ps_c, ar_c, ad_v)):
                    lst.extend(vals)
            return _dn_local(True, qs, ks, vs, *gates)

        _, vjp = jax.vjp(f, qs, ks, vs, ps_cs, ar_cs, ad_v)
        d_q, d_k, d_v, d_ps, d_ar, d_ad = vjp(cot)
        for c in range(DN_GROUP):
            at, hs = pl.ds(c * DN_CHUNK, DN_CHUNK), slice(c * DN_HEADS, (c + 1) * DN_HEADS)
            dqkv_ref[at, :] = jnp.concatenate(d_q[hs] + d_k[hs] + d_v[hs], axis=1).astype(dqkv_ref.dtype)
            dps_ref[at, :] = d_ps[c]
            dar_ref[c] = d_ar[c]
        _store(dad_ref, d_ad, first)

    wide = pl.BlockSpec((rows, BRANCH), lambda j: (j, 0))
    specs = _dn_local_specs()
    return pl.pallas_call(
        body, grid=(n_g,),
        in_specs=specs + [wide, wide, wide, wide, pl.BlockSpec((DN_GROUP, DN_HEADS, DN_CHUNK, DN_CHUNK), lambda j: (j, 0, 0, 0)),
                          pl.BlockSpec((DN_GROUP, 8, LANES), lambda j: (j, 0, 0))],
        out_specs=specs,
        out_shape=[jax.ShapeDtypeStruct((s, 3 * BRANCH), F32), jax.ShapeDtypeStruct((s, LANES), F32),
                   jax.ShapeDtypeStruct((n_c, DN_HEADS, DN_CHUNK), F32), jax.ShapeDtypeStruct((2, DN_HEADS), F32)],
        name=name, compiler_params=_cparams(1),
    )(dn_act, ps, a_rows, ad, *cots)


def _dn_scan_specs(n_c, rev):
    idx = (lambda j: n_c - 1 - j) if rev else (lambda j: j)
    wide = pl.BlockSpec((DN_CHUNK, BRANCH), lambda j: (idx(j), 0))
    return [wide, wide, wide, wide, pl.BlockSpec((1, DN_HEADS, DN_CHUNK, DN_CHUNK), lambda j: (idx(j), 0, 0, 0)),
            pl.BlockSpec((1, 8, LANES), lambda j: (idx(j), 0, 0)), pl.BlockSpec((DN_CHUNK, BRANCH), lambda j: (idx(j), C_DZ // BRANCH)),
            pl.BlockSpec((1, DN_DH), lambda j: (0, 0))]


def _dn_scan_tiles(refs):
    u_ref, kc_ref, qd_ref, kd_ref, qk_ref, gl_ref, z_ref, g_ref = refs
    wide = [_split_heads(r[...]) for r in (u_ref, kc_ref, qd_ref, kd_ref)]
    gl = gl_ref[0]
    return [(wide[0][h], wide[1][h], wide[2][h], wide[3][h], qk_ref[0, h], _col(_row(gl, h), 0)) for h in range(DN_HEADS)], \
        _split_heads(z_ref[...].astype(F32)), g_ref[...]


def dn_scan_fwd(name, local, pm, gain):
    s = pm.shape[0]
    n_c = s // DN_CHUNK

    def body(*refs):
        y_ref, hist_ref, state = refs[8:]

        @pl.when(pl.program_id(0) == 0)
        def _():
            state[...] = jnp.zeros_like(state)

        hist_ref[0] = state[...]
        per_head, z4, gain_v = _dn_scan_tiles(refs[:8])
        ys, s_nexts = _dn_step(False, [state[h] for h in range(DN_HEADS)], per_head, z4, gain_v)
        for h in range(DN_HEADS):
            state[h] = s_nexts[h]
        y_ref[...] = jnp.concatenate(ys, axis=1).astype(y_ref.dtype)

    return pl.pallas_call(
        body, grid=(n_c,), in_specs=_dn_scan_specs(n_c, False),
        out_specs=[pl.BlockSpec((DN_CHUNK, BRANCH), lambda j: (j, 0)),
                   pl.BlockSpec((1, DN_HEADS, DN_DH, DN_DH), lambda j: (j, 0, 0, 0))],
        out_shape=[jax.ShapeDtypeStruct((s, BRANCH), BF16), jax.ShapeDtypeStruct((n_c, DN_HEADS, DN_DH, DN_DH), F32)],
        scratch_shapes=[pltpu.VMEM((DN_HEADS, DN_DH, DN_DH), F32)],
        name=name, compiler_params=_cparams(1),
    )(*local, pm, gain)


def dn_scan_bwd(name, local, pm, gain, hist, dy):
    s = pm.shape[0]
    n_c = s // DN_CHUNK

    def body(*refs):
        hist_ref, dy_ref = refs[8:10]
        du_ref, dkc_ref, dqd_ref, dkd_ref, dqk_ref, dgl_ref, dz_ref, dg_ref, d_state = refs[10:]
        first = pl.program_id(0) == 0

        @pl.when(first)
        def _():
            d_state[...] = jnp.zeros_like(d_state)

        per_head, z4, gain_v = _dn_scan_tiles(refs[:8])
        _, vjp = jax.vjp(functools.partial(_dn_step, True), [hist_ref[0, h] for h in range(DN_HEADS)], per_head, z4, gain_v)
        d_s, grads, d_z, d_gain = vjp((_split_heads(dy_ref[...].astype(F32)), [d_state[h] for h in range(DN_HEADS)]))
        for h in range(DN_HEADS):
            d_state[h] = d_s[h]
        for ref, i in ((du_ref, 0), (dkc_ref, 1), (dqd_ref, 2), (dkd_ref, 3)):
            ref[...] = jnp.concatenate([g[i] for g in grads], axis=1)
        dz_ref[...] = jnp.concatenate(d_z, axis=1).astype(dz_ref.dtype)
        for h in range(DN_HEADS):
            dqk_ref[0, h] = grads[h][4]
        dgl_ref[0] = _head_rows([g[5] for g in grads])
        _store(dg_ref, d_gain, first)

    rev = lambda j: n_c - 1 - j
    specs = _dn_scan_specs(n_c, True)
    return pl.pallas_call(
        body, grid=(n_c,),
        in_specs=specs + [pl.BlockSpec((1, DN_HEADS, DN_DH, DN_DH), lambda j: (rev(j), 0, 0, 0)),
                          pl.BlockSpec((DN_CHUNK, BRANCH), lambda j: (rev(j), 0))],
        out_specs=specs[:6] + [pl.BlockSpec((DN_CHUNK, BRANCH), lambda j: (rev(j), 0)), specs[7]],
        out_shape=[jax.ShapeDtypeStruct((s, BRANCH), F32)] * 4 + [
            jax.ShapeDtypeStruct((n_c, DN_HEADS, DN_CHUNK, DN_CHUNK), F32), jax.ShapeDtypeStruct((n_c, 8, LANES), F32),
            jax.ShapeDtypeStruct((s, BRANCH), BF16), jax.ShapeDtypeStruct((1, DN_DH), F32)],
        scratch_shapes=[pltpu.VMEM((DN_HEADS, DN_DH, DN_DH), F32)],
        name=name, compiler_params=_cparams(1),
    )(*local, pm, gain, hist, dy)


def _seq_layouts(cols, s):
    return cols.T.reshape(cols.shape[1], s // LANES, LANES)


def layer_fwd(li, x, p, w, more_weights=None):
    s = x.shape[0]
    n = lambda t: f"{t}_l{li}"
    h = rms_fwd(n("rms_mix"), x, w["g_mix"])
    pm = mm(n("in_main"), h, w["in_main"], "nn")
    ps = mm(n("in_small"), h, w["in_small"], "nn")
    qn, kn = fox_prep_fwd(n("fox_prep"), pm, w["gq"], w["gk"])
    f_t = _seq_layouts(ps[:, 0:8], s)
    cum = fox_gate_fwd(n("fox_gate"), f_t, w["b_f"])
    cum_c, cum_r = cum.reshape(8, s, 1), cum.reshape(8, 1, s)
    y_fox = fox_attn_fwd(n("fox_attn"), qn, kn, pm, cum_c, cum_r)
    y_sc = tile_fwd(n("sconv"), _sconv_fn, (BRANCH // LANES,), sconv_ops(pm, w["sc_conv_w"]), [_col_out(s, BRANCH, BF16)])[0]
    dn_act = tile_fwd(n("dnconv"), _dnconv_fn, (3 * BRANCH // LANES,), dnconv_ops(pm, w["dn_conv_w"]), [_col_out(s, 3 * BRANCH)])[0]
    a_rows = ps[:, 12:16].reshape(s // DN_CHUNK, DN_CHUNK, DN_HEADS).transpose(0, 2, 1)
    dn_local = dn_local_fwd(n("dn_local"), dn_act, ps, a_rows, w["ad"])
    y_dn, hist = dn_scan_fwd(n("dn_scan"), dn_local, pm, w["dn_gain"])
    ys = (y_fox, y_sc, y_dn)
    if more_weights is not None:
        w = {**w, **more_weights(y_dn)}
    yp = [mm(n(f"branch{b}"), ys[b], w["branch"][b], "nn", blocks=(0, N_CHIPS)) for b in range(3)]
    merged = tile_fwd(n("merge"), _merge_fn, (s // 256,), merge_ops(yp, pm), [((s, D_MODEL), BF16, (256, D_MODEL), lambda i: (i, 0), ())])[0]
    x1 = mm(n("w_o"), merged, w["o"], "nn", add=x)
    h2 = rms_fwd(n("rms_ffn"), x1, w["g_ffn"])
    ug = mm(n("up_g"), h2, w["up"], "nn", blocks=(0, 2))
    uv = mm(n("up_v"), h2, w["up"], "nn", blocks=(2, 2))
    act = tile_fwd(n("ffn_act"), _ffn_act_fn, (D_FF // LANES,), ffn_ops(ug, uv, w["ffn_conv_w"]), [_col_out(s, D_FF, BF16)])[0]
    x2 = mm(n("down"), act, w["down"], "nn", add=x1)
    h3 = rms_fwd(n("rms_ple"), x2, w["g_ple"])
    gpre = mm(n("ple_gate"), h3, w["pg"], "nn")
    pe = mm(n("ple_emb"), p, w["ple"], "nn", blocks=(0, N_CHIPS))
    x3 = tile_fwd(n("ple"), _ple_fn, (s // 256,), ple_ops(gpre, pe, x2), [((s, D_MODEL), F32, (256, D_MODEL), lambda i: (i, 0), ())])[0]
    saved = dict(x=x, h=h, pm=pm, ps=ps, qn=qn, kn=kn, f_t=f_t, cum_c=cum_c, cum_r=cum_r, ys=ys, dn_act=dn_act, dn_local=dn_local,
                 a_rows=a_rows, hist=hist, yp=yp, merged=merged, x1=x1, h2=h2, ug=ug, uv=uv, act=act, x2=x2, h3=h3,
                 gpre=gpre, pe=pe, p=p)
    return x3, saved, w


def hang_on(w, token):
    zero = token[0, 0]
    small = ("g_mix", "g_ffn", "g_ple", "gq", "gk", "b_f", "ad", "dn_gain", "sc_conv_w", "dn_conv_w", "ffn_conv_w")
    return {**w, **{k: w[k] + zero for k in small}}


def layer_bwd(li, dx3, sv, w, hooks=None):
    hooks = hooks or {}

    def stage(key, after, w):
        return hang_on(w, hooks[key](after, g)) if key in hooks else w

    s = dx3.shape[0]
    n = lambda t: f"{t}_l{li}"
    g = {}
    col_own = lambda width: ((s, width), (s, LANES), lambda j: (0, j))
    d_gpre, d_pe = tile_bwd(n("ple_bwd"), _ple_fn, (s // 256,), ple_ops(sv["gpre"], sv["pe"], sv["x2"]), [_rows(dx3)],
                            [(0, (), None, BF16), (1, (), None, BF16)])
    g["w_ple"] = mm(n("d_w_ple"), sv["p"], d_pe, "tn", blocks=(0, N_CHIPS))
    g["w_ple_gate"] = mm(n("d_w_pg"), sv["h3"], d_gpre, "tn").reshape(N_CHIPS, -1, D_MODEL)
    dh3 = mm(n("d_h3"), d_gpre, w["pg"], "nt")
    dx2, d_g_ple = rms_bwd(n("rms_ple_bwd"), sv["x2"], w["g_ple"], dh3, dx3)
    dact = mm(n("d_act"), dx2, w["down"], "nt")
    g["w_down"] = mm(n("d_w_down"), sv["act"], dx2, "tn").reshape(N_CHIPS, -1, D_MODEL)
    taps_own = ((w["ffn_conv_w"].shape[0], D_FF), (w["ffn_conv_w"].shape[0], LANES), lambda j: (0, j))
    d_ug, d_uv, d_fw_g, d_fw_v = tile_bwd(n("ffn_act_bwd"), _ffn_act_fn, (D_FF // LANES,), ffn_ops(sv["ug"], sv["uv"], w["ffn_conv_w"]),
                                          [_col_cot(dact)], [(0, (), None, BF16), (1, (), None, BF16), (2, (), taps_own), (3, (), taps_own)])
    g["ffn_conv_w"] = jnp.concatenate([d_fw_g, d_fw_v], axis=1)
    g["w_up"] = jnp.concatenate([mm(n("d_w_up_g"), sv["h2"], d_ug, "tn", blocks=(0, 2)), mm(n("d_w_up_v"), sv["h2"], d_uv, "tn", blocks=(0, 2))])
    dh2 = mm(n("d_h2_v"), d_uv, w["up"], "nt", blocks=(2, 2), add=mm(n("d_h2_g"), d_ug, w["up"], "nt", blocks=(0, 2)))
    dx1, d_g_ffn = rms_bwd(n("rms_ffn_bwd"), sv["x1"], w["g_ffn"], dh2, dx2)
    w = stage("mid", dx1, w)
    dmerged = mm(n("d_merged"), dx1, w["o"], "nt")
    g["w_o"] = mm(n("d_w_o"), sv["merged"], dx1, "tn").reshape(N_CHIPS, -1, D_MODEL)
    gate_own = ((s, D_MODEL), (256, D_MODEL), lambda i: (i, 0))
    d_yp0, d_yp1, d_yp2, d_g0, d_g1, d_g2 = tile_bwd(
        n("merge_bwd"), _merge_fn, (s // 256,), merge_ops(sv["yp"], sv["pm"]), [_rows(dmerged)],
        [(0, (), None, BF16), (1, (), None, BF16), (2, (), None, BF16), (3, (), gate_own, BF16), (4, (), gate_own, BF16), (5, (), gate_own, BF16)])
    d_yp = (d_yp0, d_yp1, d_yp2)
    g["w_branch"] = jnp.concatenate([mm(n(f"d_w_branch{b}"), sv["ys"][b], d_yp[b], "tn", blocks=(0, N_CHIPS)) for b in range(3)], axis=1)
    d_ys = [mm(n(f"d_y{b}"), d_yp[b], w["branch"][b], "nt", blocks=(0, N_CHIPS)) for b in range(3)]
    w = stage("late", d_ys[2], w)
    *d_local, d_z, d_dngain = dn_scan_bwd(n("dn_scan_bwd"), sv["dn_local"], sv["pm"], w["dn_gain"], sv["hist"], d_ys[2])
    d_dnact, d_ps_dn, d_arows, d_ad = dn_local_bwd(n("dn_local_bwd"), sv["dn_act"], sv["ps"], sv["a_rows"], w["ad"], d_local)
    g["ad"], g["dn_norm_gain"] = d_ad, d_dngain[0]
    d_dnqkv, g["dn_conv_w"] = tile_bwd(n("dnconv_bwd"), _dnconv_fn, (3 * BRANCH // LANES,), dnconv_ops(sv["pm"], w["dn_conv_w"]),
                                       [_col_cot(d_dnact)], [(0, (), col_own(3 * BRANCH), BF16), (1, ())])
    d_sb, d_sc, d_sv, g["sc_conv_w"] = tile_bwd(n("sconv_bwd"), _sconv_fn, (BRANCH // LANES,), sconv_ops(sv["pm"], w["sc_conv_w"]), [_col_cot(d_ys[1])],
                                                [(0, (), col_own(BRANCH), BF16), (1, (), col_own(BRANCH), BF16), (2, (), col_own(BRANCH), BF16), (3, ())])
    w = stage("last", d_dnqkv, w)
    d_qn, d_kn, d_fv, d_cum = fox_attn_bwd(n("fox_attn_bwd"), sv["qn"], sv["kn"], sv["pm"], sv["cum_c"], sv["cum_r"], d_ys[0])
    d_ft, d_bf = fox_gate_bwd(n("fox_gate_bwd"), sv["f_t"], w["b_f"], d_cum.reshape(8, s // LANES, LANES))
    g["b_fox_f"] = d_bf.reshape(8)
    d_fq, d_fk, d_gq, d_gk = fox_prep_bwd(n("fox_prep_bwd"), sv["pm"], w["gq"], w["gk"], d_qn, d_kn)
    g["fox_q_gain"] = d_gq[0, :FOX_DH] + d_gq[0, FOX_DH:]
    g["fox_k_gain"] = d_gk[0, :FOX_DH] + d_gk[0, FOX_DH:]
    d_pm = jnp.concatenate([d_fq, d_fk, d_fv.astype(BF16), d_sb, d_sc, d_sv, d_dnqkv, d_z, d_g0, d_g1, d_g2], axis=1)
    d_a_cols = d_arows.transpose(0, 2, 1).reshape(s, DN_HEADS)
    d_f_cols = d_ft.reshape(8, s).T
    d_ps = d_ps_dn + jnp.concatenate([d_f_cols, jnp.zeros((s, 4), F32), d_a_cols, jnp.zeros((s, LANES - 16), F32)], axis=1)
    g["w_in"] = chip_blocks_w_in(mm(n("d_w_in_main"), d_pm, sv["h"], "tn"), mm(n("d_w_in_small"), d_ps, sv["h"], "tn"))
    w = stage("w_in", g["w_in"], w)
    dh = mm(n("d_h_small"), d_ps, w["in_small"], "nt", add=mm(n("d_h_main"), d_pm, w["in_main"], "nt"))
    dx, d_g_mix = rms_bwd(n("rms_mix_bwd"), sv["x"], w["g_mix"], dh, dx1)
    g["g_mix"], g["g_ffn"], g["g_ple"] = d_g_mix[0], d_g_ffn[0], d_g_ple[0]
    return dx, g


IN_SHARD = 2052
MAIN_RANGES = ((0, 1536), (1544, 3080), (3080, 4616), (4624, 5136), (5136, 8208))
SMALL_RANGES = ((1536, 1544), (4616, 4620), (4620, 4624))


def _from_chip_blocks(blocks, ranges):
    parts = []
    for lo, hi in ranges:
        for k in range(N_CHIPS):
            a0, a1 = max(lo, k * IN_SHARD), min(hi, (k + 1) * IN_SHARD)
            if a0 < a1:
                parts.append(blocks[k][:, a0 - k * IN_SHARD:a1 - k * IN_SHARD])
    return parts


def split_w_in(blocks):
    main = jnp.concatenate(_from_chip_blocks(blocks, MAIN_RANGES), axis=1)
    pad = jnp.zeros((blocks.shape[1], LANES - 16), blocks.dtype)
    return main, jnp.concatenate(_from_chip_blocks(blocks, SMALL_RANGES) + [pad], axis=1)


def chip_blocks_w_in(main, small):
    ranges = sorted([(lo, hi, "m") for lo, hi in MAIN_RANGES] + [(lo, hi, "s") for lo, hi in SMALL_RANGES])
    offs, m_off, s_off = {}, 0, 0
    for lo, hi in MAIN_RANGES:
        offs[lo] = m_off
        m_off += hi - lo
    for lo, hi in SMALL_RANGES:
        offs[lo] = s_off
        s_off += hi - lo
    blocks = []
    for k in range(N_CHIPS):
        parts = []
        for lo, hi, src in ranges:
            a0, a1 = max(lo, k * IN_SHARD), min(hi, (k + 1) * IN_SHARD)
            if a0 < a1:
                arr = main if src == "m" else small
                parts.append(arr[offs[lo] + a0 - lo:offs[lo] + a1 - lo])
        blocks.append(jnp.concatenate(parts, axis=0))
    return jnp.stack(blocks)


def later_weights(got):
    g_branch, g_o, g_up, g_down, g_pg, g_ple = got
    branch = g_branch.reshape(N_CHIPS, 3, BRANCH, -1)
    return dict(branch=[branch[:, b] for b in range(3)], o=g_o.reshape(D_MODEL, D_MODEL), up=g_up,
                down=g_down.reshape(D_FF, D_MODEL), pg=g_pg.reshape(D_MODEL, D_MODEL), ple=g_ple)


def layer_weights(li, got, conv, a):
    main, small = split_w_in(got[0])
    tile2 = lambda v: jnp.concatenate([v, v])[None, :]
    rest = later_weights(got[1:]) if len(got) > 1 else {}
    return dict(
        in_main=main, in_small=small, **rest,
        g_mix=a["g_mix"][li][None, :], g_ffn=a["g_ffn"][li][None, :], g_ple=a["g_ple"][li][None, :],
        gq=tile2(a["fox_q_gain"][li]), gk=tile2(a["fox_k_gain"][li]), b_f=a["b_fox_f"][li].reshape(8, 1, 1),
        ad=jnp.stack([a["dn_a_log"][li], a["dn_dt_bias"][li]]), dn_gain=a["dn_norm_gain"][li][None, :],
        sc_conv_w=conv["sc_conv_w"][li], dn_conv_w=conv["dn_conv_w"][li], ffn_conv_w=conv["ffn_conv_w"][li])


def pack_rows(arrs, dtype):
    flat = jnp.concatenate([t.reshape(-1).astype(dtype) for t in arrs])
    pad = (-flat.shape[0]) % (8 * LANES)
    if pad:
        flat = jnp.concatenate([flat, jnp.zeros((pad,), dtype)])
    return flat.reshape(-1, LANES)


def unpack_rows(buf, shapes):
    flat = buf.reshape(-1)
    out, off = [], 0
    for shp in shapes:
        size = 1
        for dim in shp:
            size *= dim
        out.append(flat[off:off + size].reshape(shp))
        off += size
    return out


def chip_shard(t, axis, k):
    width = t.shape[axis] // N_CHIPS
    return lax.slice_in_dim(t, k * width, (k + 1) * width, axis=axis)


ANY = pl.BlockSpec(memory_space=pl.ANY)


def _position():
    x, y, c = lax.axis_index("x"), lax.axis_index("y"), lax.axis_index("c")
    return x, y, c, [(1 - x, y), (x, 1 - y), (1 - x, 1 - y)]


def gather_small(name, block):
    m_per, n = block.shape

    def body(x_ref, out_ref, token, send_sems, recv_sems, local_sem):
        token[...] = jnp.zeros_like(token)
        x, y, c, chips = _position()
        me, sibling = (x, y, c), (x, y, 1 - c)

        def rows(px, py, pc):
            return out_ref.at[pl.ds((4 * px + 2 * py + pc) * m_per, m_per), :]

        def copy(k, blk, to, src=None):
            return pltpu.make_async_remote_copy(src_ref=rows(*blk) if src is None else src, dst_ref=rows(*blk),
                                                send_sem=send_sems.at[k], recv_sem=recv_sems.at[k], device_id=to, device_id_type=MESH)

        mine = pltpu.make_async_copy(x_ref, rows(*me), local_sem)
        mine.start()
        first = [copy(0, me, sibling, src=x_ref)] + [copy(1 + j, me, (*chip, c), src=x_ref) for j, chip in enumerate(chips)]
        for cp in first:
            cp.start()
        passed = [copy(4 + j, (*chip, c), sibling) for j, chip in enumerate(chips)]
        for j, chip in enumerate(chips):
            copy(1 + j, (*chip, c), me).wait_recv()
            passed[j].start()
        copy(0, sibling, me).wait_recv()
        for j, chip in enumerate(chips):
            copy(4 + j, (*chip, 1 - c), me).wait_recv()
        for cp in first + passed:
            cp.wait_send()
        mine.wait()

    in_vmem = pl.BlockSpec(memory_space=pltpu.VMEM)
    return pl.pallas_call(
        body, out_shape=[jax.ShapeDtypeStruct((8 * m_per, n), block.dtype), jax.ShapeDtypeStruct((8, LANES), F32)],
        in_specs=[in_vmem], out_specs=[in_vmem, in_vmem],
        scratch_shapes=[pltpu.SemaphoreType.DMA((7,)), pltpu.SemaphoreType.DMA((7,)), pltpu.SemaphoreType.DMA],
        name=name, compiler_params=pltpu.CompilerParams(vmem_limit_bytes=VMEM_LIMIT),
    )(block)


def _sems(n):
    return [pltpu.SemaphoreType.DMA((n,)), pltpu.SemaphoreType.DMA((n,))]


def _split_cols(rows):
    return (rows // 2) % 16 != 0


def _half(ref, which, lead=()):
    rows, cols = ref.shape[-2:]
    if _split_cols(rows):
        return ref.at[(*lead, slice(None), pl.ds(which * (cols // 2), cols // 2))]
    return ref.at[(*lead, pl.ds(which * (rows // 2), rows // 2), slice(None))]


def _half_shape(rows, cols):
    return (rows, cols // 2) if _split_cols(rows) else (rows // 2, cols)


def gather_layer(name, shards):
    n_w = len(shards)

    def body(*refs):
        ins, outs = refs[:n_w], refs[n_w:2 * n_w]
        token, send_sems, recv_sems = refs[2 * n_w:]
        token[...] = jnp.zeros_like(token)
        x, y, c, chips = _position()
        sibling = (x, y, 1 - c)

        def part(w, px, py, pc):
            return _half(outs[w], pc, (2 * px + py,))

        def copy(k, w, blk, to, src=None):
            return pltpu.make_async_remote_copy(src_ref=part(w, *blk) if src is None else src, dst_ref=part(w, *blk),
                                                send_sem=send_sems.at[k], recv_sem=recv_sems.at[k], device_id=to, device_id_type=MESH)

        pairs = [(w, j, chip) for w in range(n_w) for j, chip in enumerate(chips)]
        first = [copy(3 * w + j, w, (x, y, c), (*chip, c), src=_half(ins[w], c)) for w, j, chip in pairs]
        for cp in first:
            cp.start()
        passed = [copy(3 * n_w + 3 * w + j, w, (*chip, c), sibling) for w, j, chip in pairs]
        for (w, j, chip), fwd in zip(pairs, passed):
            copy(3 * w + j, w, (*chip, c), (x, y, c)).wait_recv()
            fwd.start()
        for w, j, chip in pairs:
            copy(3 * n_w + 3 * w + j, w, (*chip, 1 - c), (x, y, c)).wait_recv()
        for cp in first + passed:
            cp.wait_send()

    out = pl.pallas_call(
        body, out_shape=[jax.ShapeDtypeStruct((N_CHIPS,) + s.shape, s.dtype) for s in shards] + [jax.ShapeDtypeStruct((8, LANES), F32)],
        in_specs=[ANY] * n_w, out_specs=[ANY] * n_w + [pl.BlockSpec(memory_space=pltpu.VMEM)], scratch_shapes=_sems(6 * n_w), name=name,
    )(*shards)
    return out[:n_w], out[n_w]


def swap_halves(name, grads):
    n_w = len(grads)

    def body(*refs):
        ins, outs = refs[:n_w], refs[n_w:2 * n_w]
        send_sems, recv_sems = refs[2 * n_w:]
        x, y, c, _ = _position()
        cps = [pltpu.make_async_remote_copy(src_ref=_half(ins[w], 1 - c, (slice(None),)), dst_ref=outs[w],
                                            send_sem=send_sems.at[w], recv_sem=recv_sems.at[w], device_id=(x, y, 1 - c),
                                            device_id_type=MESH) for w in range(n_w)]
        for cp in cps:
            cp.start()
        for cp in cps:
            cp.wait()

    return pl.pallas_call(
        body, out_shape=[jax.ShapeDtypeStruct((N_CHIPS,) + _half_shape(*g.shape[1:]), g.dtype) for g in grads],
        in_specs=[ANY] * n_w, out_specs=[ANY] * n_w, scratch_shapes=_sems(n_w), name=name,
    )(*grads)


def scatter_chips(name, partials):
    n_w = len(partials)

    def body(*refs):
        ins, outs = refs[:n_w], refs[n_w:2 * n_w]
        send_sems, recv_sems = refs[2 * n_w:]
        x, y, c, chips = _position()
        cps = [pltpu.make_async_remote_copy(src_ref=ins[w].at[2 * cx + cy], dst_ref=outs[w].at[j], send_sem=send_sems.at[3 * w + j],
                                            recv_sem=recv_sems.at[3 * w + j], device_id=(cx, cy, c), device_id_type=MESH)
               for w in range(n_w) for j, (cx, cy) in enumerate(chips)]
        for cp in cps:
            cp.start()
        for cp in cps:
            cp.wait()

    return pl.pallas_call(
        body, out_shape=[jax.ShapeDtypeStruct((3,) + p.shape[1:], p.dtype) for p in partials],
        in_specs=[ANY] * n_w, out_specs=[ANY] * n_w, scratch_shapes=_sems(3 * n_w), name=name,
    )(*partials)


def share_halves(name, bufs):
    n_w = len(bufs)

    def body(*refs):
        outs = refs[n_w:2 * n_w]
        send_sems, recv_sems = refs[2 * n_w:]
        x, y, c, _ = _position()

        def copy(w, pc):
            half = _half(outs[w], pc)
            return pltpu.make_async_remote_copy(src_ref=half, dst_ref=half, send_sem=send_sems.at[w], recv_sem=recv_sems.at[w],
                                                device_id=(x, y, 1 - c), device_id_type=MESH)

        for w in range(n_w):
            copy(w, c).start()
        for w in range(n_w):
            copy(w, 1 - c).wait_recv()
            copy(w, c).wait_send()

    return pl.pallas_call(
        body, out_shape=[jax.ShapeDtypeStruct(b.shape, b.dtype) for b in bufs], in_specs=[ANY] * n_w, out_specs=[ANY] * n_w,
        input_output_aliases={w: w for w in range(n_w)}, scratch_shapes=_sems(n_w), name=name,
    )(*bufs)


HBM = pl.BlockSpec(memory_space=pltpu.HBM)
SEM = pl.BlockSpec(memory_space=pltpu.SEMAPHORE)
EFFECT = pltpu.SideEffectType.DATAFLOW_SIDE_EFFECTING


def _exchange_copies(kind, srcs, lands):
    x, y, c, chips = _position()
    out = []
    for src, land in zip(srcs, lands):
        if kind == "swap":
            out.append((_half(src, 1 - c, (slice(None),)), land, (x, y, 1 - c)))
            continue
        for j, (cx, cy) in enumerate(chips):
            if kind == "gather":
                out.append((src, land.at[2 * x + y], (cx, cy, c)))
            else:
                out.append((src.at[2 * cx + cy], land.at[j], (cx, cy, c)))
    return out


def _land_shapes(kind, srcs):
    if kind == "gather":
        return [(N_CHIPS,) + s.shape for s in srcs]
    if kind == "swap":
        return [(N_CHIPS,) + _half_shape(*s.shape[1:]) for s in srcs]
    return [(3,) + s.shape[1:] for s in srcs]


def exchange_start(name, kind, srcs):
    n_w = len(srcs)
    shapes = _land_shapes(kind, srcs)
    n_sem = n_w if kind == "swap" else 3 * n_w

    def body(*refs):
        ins, lands = refs[:n_w], refs[n_w:2 * n_w]
        send_sems, recv_sems = refs[2 * n_w:2 * n_w + 2]
        token = refs[-1]
        for i, (src, dst, dev) in enumerate(_exchange_copies(kind, ins, lands)):
            pltpu.make_async_remote_copy(src_ref=src, dst_ref=dst, send_sem=send_sems.at[i], recv_sem=recv_sems.at[i],
                                         device_id=dev, device_id_type=MESH).start()
        token[...] = jnp.zeros_like(token)

    out = pl.pallas_call(
        body, name=name,
        out_shape=(pltpu.SemaphoreType.DMA((n_sem,)), pltpu.SemaphoreType.DMA((n_sem,)),
                   *[pltpu.HBM(s.shape, s.dtype) for s in srcs], *[pltpu.HBM(shp, s.dtype) for shp, s in zip(shapes, srcs)],
                   jax.ShapeDtypeStruct((8, LANES), F32)),
        in_specs=(HBM,) * (2 * n_w), out_specs=(SEM, SEM) + (HBM,) * (2 * n_w) + (pl.BlockSpec(memory_space=pltpu.VMEM),),
        input_output_aliases={i: 2 + i for i in range(2 * n_w)},
        compiler_params=pltpu.CompilerParams(has_side_effects=EFFECT),
    )(*[pltpu.with_memory_space_constraint(s, pltpu.HBM) for s in srcs],
      *[pltpu.with_memory_space_constraint(lax.empty(shp, s.dtype), pltpu.HBM) for shp, s in zip(shapes, srcs)])
    return (kind, n_w, out[:-1]), out[-1]


def exchange_wait(name, handle, after):
    kind, n_w, (send_sems, recv_sems, *thru) = handle
    n_sem = n_w if kind == "swap" else 3 * n_w

    def body(*refs):
        ins, lands = refs[:n_w], refs[n_w:2 * n_w]
        send_sems, recv_sems = refs[2 * n_w:2 * n_w + 2]
        for i, (src, dst, dev) in enumerate(_exchange_copies(kind, ins, lands)):
            cp = pltpu.make_async_remote_copy(src_ref=src, dst_ref=dst, send_sem=send_sems.at[i], recv_sem=recv_sems.at[i],
                                              device_id=dev, device_id_type=MESH)
            cp.wait_send()
            cp.wait_recv()

    out = pl.pallas_call(
        body, name=name, out_shape=tuple(pltpu.HBM(t.shape, t.dtype) for t in thru),
        in_specs=(HBM,) * (2 * n_w) + (SEM, SEM, pl.BlockSpec(memory_space=pl.ANY)), out_specs=(HBM,) * (2 * n_w),
        input_output_aliases={i: i for i in range(2 * n_w)},
        compiler_params=pltpu.CompilerParams(has_side_effects=EFFECT),
    )(*thru, send_sems, recv_sems, after)
    return list(out[n_w:])


def _row_tile(rows, cols):
    best = rows
    if rows * cols * 4 <= 1024 * 1024:
        return rows
    for t in range(16, rows, 16):
        if rows % t == 0 and t * cols * 4 <= 1024 * 1024:
            best = t
    return best


def pair_sum(name, pos, grad, from_sibling):
    _, rows, cols = grad.shape
    h_rows, h_cols = _half_shape(rows, cols)
    tr = _row_tile(h_rows, h_cols)
    n_t = h_rows // tr

    def body(pos_ref, g_ref, s_ref, b_ref, f_ref):
        tot = g_ref[...] + s_ref[...]
        b_ref[...] = tot.astype(BF16)

        @pl.when(pl.program_id(1) == pos_ref[1])
        def _():
            f_ref[...] = tot[0]

    blk = pl.BlockSpec((1, tr, h_cols), lambda i, k, pos: (k, i, 0))
    if _split_cols(rows):
        mine = pl.BlockSpec((1, tr, h_cols), lambda i, k, pos: (k, i, pos[0]))
    else:
        mine = pl.BlockSpec((1, tr, h_cols), lambda i, k, pos: (k, pos[0] * n_t + i, 0))
    return pl.pallas_call(
        body, grid_spec=pltpu.PrefetchScalarGridSpec(
            num_scalar_prefetch=1, grid=(n_t, N_CHIPS), in_specs=[mine, blk],
            out_specs=[blk, pl.BlockSpec((tr, h_cols), lambda i, k, pos: (i, 0))]),
        out_shape=[jax.ShapeDtypeStruct((N_CHIPS, h_rows, h_cols), BF16), jax.ShapeDtypeStruct((h_rows, h_cols), F32)],
        name=name, compiler_params=_cparams(2),
    )(pos, grad, from_sibling)


def chip_sum(name, pos, own, landed, split_cols):
    half, cols = own.shape
    tr = _row_tile(half, cols)
    n_t = half // tr

    def body(pos_ref, p_ref, l_ref, o_ref):
        o_ref[...] = ((p_ref[...] + l_ref[0].astype(F32)) + l_ref[1].astype(F32)) + l_ref[2].astype(F32)

    if split_cols:
        out_spec, out_shape = pl.BlockSpec((tr, cols), lambda i, pos: (i, pos[0])), (half, 2 * cols)
    else:
        out_spec, out_shape = pl.BlockSpec((tr, cols), lambda i, pos: (pos[0] * n_t + i, 0)), (2 * half, cols)
    return pl.pallas_call(
        body, grid_spec=pltpu.PrefetchScalarGridSpec(
            num_scalar_prefetch=1, grid=(n_t,),
            in_specs=[pl.BlockSpec((tr, cols), lambda i, pos: (i, 0)), pl.BlockSpec((3, tr, cols), lambda i, pos: (0, i, 0))],
            out_specs=out_spec),
        out_shape=jax.ShapeDtypeStruct(out_shape, F32), name=name, compiler_params=_cparams(1),
    )(pos, own, landed)


def reduce_scatter_layer(tag, pos, grads):
    n = lambda t: f"{t}_{tag}"
    from_sibling = swap_halves(n("swap_halves"), grads)
    sums = [pair_sum(n(f"pair_sum{w}"), pos, g, s) for w, (g, s) in enumerate(zip(grads, from_sibling))]
    landed = scatter_chips(n("scatter_chips"), [b for b, _ in sums])
    halves = [chip_sum(n(f"chip_sum{w}"), pos, own, l, _split_cols(g.shape[1])) for w, ((_, own), l, g) in enumerate(zip(sums, landed, grads))]
    return share_halves(n("share_halves"), halves)


class OverlappedReduceScatter:
    def __init__(self, tag, pos, grads):
        self.n = lambda t: f"{t}_{tag}"
        self.pos, self.grads = pos, grads
        self.swap, self.token = exchange_start(self.n("swap_start"), "swap", grads)

    def middle(self, after):
        from_sibling = exchange_wait(self.n("swap_wait"), self.swap, after)
        self.sums = [pair_sum(self.n(f"pair_sum{w}"), self.pos, g, s) for w, (g, s) in enumerate(zip(self.grads, from_sibling))]
        self.scatter, self.token = exchange_start(self.n("scatter_start"), "scatter", [b for b, _ in self.sums])

    def finish(self, after):
        landed = exchange_wait(self.n("scatter_wait"), self.scatter, after)
        halves = [chip_sum(self.n(f"chip_sum{w}"), self.pos, own, l, _split_cols(g.shape[1]))
                  for w, ((_, own), l, g) in enumerate(zip(self.sums, landed, self.grads))]
        return share_halves(self.n("share_halves"), halves)


def sum_devices(gathered):
    m_per = gathered.shape[0] // 8

    def body(g_ref, o_ref):
        tot = g_ref[pl.ds(0, m_per), :]
        for dev in range(1, 8):
            tot = tot + g_ref[pl.ds(dev * m_per, m_per), :]
        o_ref[...] = tot

    return pl.pallas_call(
        body, out_shape=jax.ShapeDtypeStruct((m_per, gathered.shape[1]), F32),
        in_specs=[pl.BlockSpec(memory_space=pltpu.VMEM)], out_specs=pl.BlockSpec(memory_space=pltpu.VMEM), name="sum_devices",
    )(gathered)


def kernel(x, p, g_mix, w_in, b_fox_f, fox_q_gain, fox_k_gain, sc_conv_w, dn_conv_w, dn_a_log, dn_dt_bias, dn_norm_gain, w_branch, w_o, g_ffn, w_up, ffn_conv_w, w_down, g_ple, w_ple_gate, w_ple, loss_target, m_g_mix, m_w_in, m_b_fox_f, m_fox_q_gain, m_fox_k_gain, m_sc_conv_w, m_dn_conv_w, m_dn_a_log, m_dn_dt_bias, m_dn_norm_gain, m_w_branch, m_w_o, m_g_ffn, m_w_up, m_ffn_conv_w, m_w_down, m_g_ple, m_w_ple_gate, m_w_ple, v_g_mix, v_w_in, v_b_fox_f, v_fox_q_gain, v_fox_k_gain, v_sc_conv_w, v_dn_conv_w, v_dn_a_log, v_dn_dt_bias, v_dn_norm_gain, v_w_branch, v_w_o, v_g_ffn, v_w_up, v_ffn_conv_w, v_w_down, v_g_ple, v_w_ple_gate, v_w_ple):
    a = dict(g_mix=g_mix, w_in=w_in, b_fox_f=b_fox_f, fox_q_gain=fox_q_gain, fox_k_gain=fox_k_gain, sc_conv_w=sc_conv_w,
             dn_conv_w=dn_conv_w, dn_a_log=dn_a_log, dn_dt_bias=dn_dt_bias, dn_norm_gain=dn_norm_gain, w_branch=w_branch, w_o=w_o,
             g_ffn=g_ffn, w_up=w_up, ffn_conv_w=ffn_conv_w, w_down=w_down, g_ple=g_ple, w_ple_gate=w_ple_gate, w_ple=w_ple)
    mom = dict(g_mix=m_g_mix, w_in=m_w_in, b_fox_f=m_b_fox_f, fox_q_gain=m_fox_q_gain, fox_k_gain=m_fox_k_gain, sc_conv_w=m_sc_conv_w,
               dn_conv_w=m_dn_conv_w, dn_a_log=m_dn_a_log, dn_dt_bias=m_dn_dt_bias, dn_norm_gain=m_dn_norm_gain, w_branch=m_w_branch,
               w_o=m_w_o, g_ffn=m_g_ffn, w_up=m_w_up, ffn_conv_w=m_ffn_conv_w, w_down=m_w_down, g_ple=m_g_ple, w_ple_gate=m_w_ple_gate,
               w_ple=m_w_ple)
    var = dict(g_mix=v_g_mix, w_in=v_w_in, b_fox_f=v_b_fox_f, fox_q_gain=v_fox_q_gain, fox_k_gain=v_fox_k_gain, sc_conv_w=v_sc_conv_w,
               dn_conv_w=v_dn_conv_w, dn_a_log=v_dn_a_log, dn_dt_bias=v_dn_dt_bias, dn_norm_gain=v_dn_norm_gain, w_branch=v_w_branch,
               w_o=v_w_o, g_ffn=v_g_ffn, w_up=v_w_up, ffn_conv_w=v_ffn_conv_w, w_down=v_w_down, g_ple=v_g_ple, w_ple_gate=v_w_ple_gate,
               w_ple=v_w_ple)
    cx, cy, cc = lax.axis_index("x"), lax.axis_index("y"), lax.axis_index("c")
    chip = 2 * cx + cy
    pos = jnp.stack([cc, chip]).astype(jnp.int32)

    def as_blocks(t):
        return t.reshape(2, -1, t.shape[-1])

    def own_block_in(got, shards):
        return [lax.dynamic_update_slice(g, s[None], (chip, 0, 0)) for g, s in zip(got, shards)]

    conv_shapes = [a[nm].shape for nm in CONVS]
    conv_all, conv_token = gather_small("gather_conv_w", pack_rows([a[nm] for nm in CONVS], F32))
    def layer_block(nm, t, li):
        return as_blocks(t)[li]

    shards0 = [(layer_block(nm, a[nm], 0) + conv_token[0, 0]).astype(BF16) for nm in BIG]
    got0, gathered_token = gather_layer("gather_w_in_l0", shards0[:1])
    shards0[1:] = [s + gathered_token[0, 0].astype(BF16) for s in shards0[1:]]
    gather0, gather0_token = exchange_start("gather_start_l0", "gather", shards0[1:])
    shards1 = [(layer_block(nm, a[nm], 1) + gather0_token[0, 0]).astype(BF16) for nm in BIG]
    gather1, gather1_in_token = exchange_start("gather_start_w_in_l1", "gather", shards1[:1])
    shards1[1:] = [s + gather1_in_token[0, 0].astype(BF16) for s in shards1[1:]]
    gather1_rest, gather1_token = exchange_start("gather_start_l1", "gather", shards1[1:])
    conv_rows = conv_all.shape[0] // 8
    conv_chip = [unpack_rows(conv_all[2 * k * conv_rows:(2 * k + 1) * conv_rows], conv_shapes) for k in range(N_CHIPS)]
    conv = {nm: jnp.concatenate([conv_chip[k][i] for k in range(N_CHIPS)], axis=2) for i, nm in enumerate(CONVS)}

    weights, saved = [None, None], [None, None]
    first_weights = hang_on(layer_weights(0, own_block_in(got0, shards0[:1]), conv, a), gather1_token)

    def rest_of_layer0(after):
        return later_weights(own_block_in(exchange_wait("gather_wait_l0", gather0, after), shards0[1:]))

    act, saved[0], weights[0] = layer_fwd(0, x[0], p[0, 0], first_weights, more_weights=rest_of_layer0)
    got1 = exchange_wait("gather_wait_w_in_l1", gather1, act)

    def rest_of_layer1(after):
        return later_weights(own_block_in(exchange_wait("gather_wait_l1", gather1_rest, after), shards1[1:]))

    act, saved[1], weights[1] = layer_fwd(1, act, p[1, 0], layer_weights(1, own_block_in(got1, shards1[:1]), conv, a),
                                          more_weights=rest_of_layer1)
    d_act, loss_part = loss_call(act, loss_target[0])
    loss = lax.psum(loss_part, ("x", "y", "c"))
    layer_grads = [None, None]
    d_act, layer_grads[1] = layer_bwd(1, d_act, saved[1], weights[1])
    rs1 = OverlappedReduceScatter("l1", pos, [layer_grads[1][nm] for nm in BIG])
    rs0 = []

    def stage_mid(after, g):
        rs1.middle(after)
        return rs1.token

    def stage_late(after, g):
        rs0.append(OverlappedReduceScatter("l0", pos, [g[nm] for nm in BIG[1:]]))
        return rs0[0].token

    def stage_last(after, g):
        rs0[0].middle(after)
        return rs0[0].token

    def stage_w_in(after, g):
        rs0.append(OverlappedReduceScatter("w_in_l0", pos, [g["w_in"]]))
        return rs0[1].token

    d_act, layer_grads[0] = layer_bwd(0, d_act, saved[0], hang_on(weights[0], rs1.token),
                                      hooks=dict(mid=stage_mid, late=stage_late, last=stage_last, w_in=stage_w_in))
    rs0[1].middle(d_act)
    reduced = [rs0[0].finish(rs0[1].token), rs1.finish(rs0[1].token)]
    grad_x = d_act[None]

    def both(nm):
        return jnp.stack([layer_grads[0][nm], layer_grads[1][nm]])

    local = {nm: both(nm) for nm in ("g_mix", "b_fox_f", "fox_q_gain", "fox_k_gain", "dn_norm_gain", "g_ffn", "g_ple", "sc_conv_w",
                                      "dn_conv_w", "ffn_conv_w")}
    local["dn_a_log"] = jnp.stack([layer_grads[li]["ad"][0] for li in range(2)])
    local["dn_dt_bias"] = jnp.stack([layer_grads[li]["ad"][1] for li in range(2)])

    small_names = SMALL + CONVS
    small_shapes = [local[nm].shape for nm in small_names]
    small_sum = sum_devices(gather_small("gather_small_grads", pack_rows([local[nm] for nm in small_names], F32))[0])
    small_grads = dict(zip(small_names, unpack_rows(small_sum, small_shapes)))
    for nm in CONVS:
        width = a[nm].shape[2]
        small_grads[nm] = lax.dynamic_slice_in_dim(small_grads[nm], chip * width, width, axis=2)

    grads, deltas, new_m, new_v = dict(small_grads), {}, {}, {}
    for nm in small_names:
        deltas[nm], new_m[nm], new_v[nm] = adam_call(f"adam_{nm}", a[nm], grads[nm], mom[nm], var[nm])
    for i, nm in enumerate(BIG[1:]):
        res = adam_layers(f"adam_{nm}", as_blocks(a[nm]), as_blocks(mom[nm]), as_blocks(var[nm]), reduced[0][i], reduced[1][1 + i])
        grads[nm], deltas[nm], new_m[nm], new_v[nm] = [r.reshape(a[nm].shape) for r in res]
    stored = lambda t: jnp.transpose(t, (2, 0, 1))
    res = adam_w_in("adam_w_in", stored(a["w_in"]), stored(mom["w_in"]), stored(var["w_in"]), rs0[1].finish(deltas["w_ple"])[0], reduced[1][0])
    grads["w_in"], deltas["w_in"], new_m["w_in"], new_v["w_in"] = [jnp.transpose(r, (1, 2, 0)) for r in res]
    return (loss, grad_x, *[grads[nm] for nm in WEIGHTS], *[deltas[nm] for nm in WEIGHTS], *[new_m[nm] for nm in WEIGHTS],
            *[new_v[nm] for nm in WEIGHTS])
```

```python
import functools

import jax
import jax.numpy as jnp
from jax import lax
from jax.experimental import pallas as pl
from jax.experimental.pallas import tpu as pltpu

F32 = jnp.float32
BF16 = jnp.bfloat16
HI = lax.Precision.HIGHEST
SOLVE = lax.Precision.HIGH
MESH = pl.DeviceIdType.MESH

D_MODEL = 1024
BRANCH = 512
FOX_DH = 64
DN_DH = 128
DN_HEADS = 4
DN_CHUNK = 64
FOX_BLOCK = 128
D_FF = 2816
EPS = 1e-6
N_CHIPS = 4
LANES = 128

ADAM_LR, ADAM_B1, ADAM_B2, ADAM_EPS, ADAM_WD, ADAM_STEP = 0.001, 0.9, 0.999, 1e-08, 0.01, 10

VMEM_LIMIT = 56 * 1024 * 1024

C_FQ, C_FK, C_FV, C_SB, C_SC, C_SV, C_DN, C_DZ, C_GATE = 0, 512, 1024, 1536, 2048, 2560, 3072, 4608, 5120
IN_MAIN = 8192
IN_SIZES = (1536, 8, 1536, 1536, 4, 4, 512, 3072)

BIG = ("w_in", "w_branch", "w_o", "w_up", "w_down", "w_ple_gate", "w_ple")
BIG_AXIS = {"w_in": 2, "w_branch": 3, "w_o": 1, "w_up": 2, "w_down": 1, "w_ple_gate": 1, "w_ple": 2}
CONVS = ("sc_conv_w", "dn_conv_w", "ffn_conv_w")
SMALL = ("g_mix", "b_fox_f", "fox_q_gain", "fox_k_gain", "dn_a_log", "dn_dt_bias", "dn_norm_gain", "g_ffn", "g_ple")
WEIGHTS = ("g_mix", "w_in", "b_fox_f", "fox_q_gain", "fox_k_gain", "sc_conv_w", "dn_conv_w", "dn_a_log", "dn_dt_bias",
           "dn_norm_gain", "w_branch", "w_o", "g_ffn", "w_up", "ffn_conv_w", "w_down", "g_ple", "w_ple_gate", "w_ple")


def _iota(shape, dim):
    return lax.broadcasted_iota(jnp.int32, shape, dim)


def _dg(a, b, mode, prec=None):
    dims = {"nn": ((1,), (0,)), "nt": ((1,), (1,)), "tn": ((0,), (0,))}[mode]
    return lax.dot_general(a, b, (dims, ((), ())), precision=prec, preferred_element_type=F32)


def _bdot_impl(a, b, mode):
    return _dg(a.astype(BF16), b.astype(BF16), mode)


@functools.partial(jax.custom_vjp, nondiff_argnums=(2,))
def _bdot_diff(a, b, mode):
    return _bdot_impl(a, b, mode)


def _bdot_fwd(a, b, mode):
    return _bdot_impl(a, b, mode), (a, b)


def _bdot_bwd(mode, res, g):
    a, b = res
    if mode == "nn":
        da, db = _bdot_impl(g, b, "nt"), _bdot_impl(a, g, "tn")
    elif mode == "nt":
        da, db = _bdot_impl(g, b, "nn"), _bdot_impl(g, a, "tn")
    else:
        da, db = _bdot_impl(b, g, "nt"), _bdot_impl(a, g, "nn")
    return da.astype(a.dtype), db.astype(b.dtype)


_bdot_diff.defvjp(_bdot_fwd, _bdot_bwd)


def _bdot(d):
    return _bdot_diff if d else _bdot_impl


def _shift_impl(x, k):
    return jnp.where(_iota(x.shape, 0) >= k, pltpu.roll(x, k, 0), 0.0)


def _unshift_impl(g, k):
    n = g.shape[0]
    return jnp.where(_iota(g.shape, 0) < n - k, pltpu.roll(g, n - k, 0), 0.0)


@functools.partial(jax.custom_vjp, nondiff_argnums=(1,))
def _shift_diff(x, k):
    return _shift_impl(x, k)


_shift_diff.defvjp(lambda x, k: (_shift_impl(x, k), None), lambda k, _, g: (_unshift_impl(g, k),))


def _row(w, j):
    return jnp.sum(jnp.where(_iota(w.shape, 0) == j, w, 0.0), axis=0, keepdims=True)


def _col(w, j):
    return jnp.sum(jnp.where(_iota(w.shape, 1) == j, w, 0.0), axis=1, keepdims=True)


def _conv(d, x, w):
    shift = _shift_diff if d else _shift_impl
    taps = w.shape[0]
    y = x * _row(w, taps - 1)
    for j in range(taps - 1):
        y = y + shift(x, taps - 1 - j) * _row(w, j)
    return y


def _softplus(x):
    return jnp.maximum(x, 0.0) + jnp.log(1.0 + jnp.exp(-jnp.abs(x)))


def _silu(x):
    return x * jax.nn.sigmoid(x)


def _rms(x, gain):
    return x * lax.rsqrt(jnp.mean(x * x, axis=-1, keepdims=True) + EPS) * gain


def _rms_fn(d, pids, x, gain):
    return (_rms(x, gain),)


def _loss_fn(d, pids, y, t):
    e = y - t
    part = 0.5 / D_MODEL * jnp.sum(e * e, keepdims=True)
    return e * (1.0 / D_MODEL), jnp.broadcast_to(part, (8, LANES))


def _fox_prep_fn(d, pids, q, k, gq, gk):
    first = _iota(q.shape, 1) < FOX_DH

    def norm(x, gain):
        sq = x * x
        ss_a = jnp.sum(jnp.where(first, sq, 0.0), axis=1, keepdims=True)
        ss_b = jnp.sum(jnp.where(first, 0.0, sq), axis=1, keepdims=True)
        rs = jnp.where(first, lax.rsqrt(ss_a / FOX_DH + EPS), lax.rsqrt(ss_b / FOX_DH + EPS))
        return x * rs * gain

    return norm(q, gq) * FOX_DH ** -0.5, norm(k, gk)


def _fox_gate_fn(d, pids, f, bias):
    logf = -_softplus(-(f + bias))
    n_r, n_c = logf.shape
    tri = (_iota((n_c, n_c), 0) <= _iota((n_c, n_c), 1)).astype(F32)
    within = _dg(logf, tri, "nn", HI)
    tot = jnp.broadcast_to(jnp.sum(logf, axis=1, keepdims=True), logf.shape)
    below = (_iota((n_r, n_r), 1) < _iota((n_r, n_r), 0)).astype(F32)
    return (within + _dg(below, tot, "nn", HI),)


def _fox_attn_fn(q_block0, d, pids, q, k, v, cq_a, cq_b, ck_a, ck_b):
    dot = _bdot(d)
    first = _iota(q.shape, 1) < FOX_DH
    n_q, n_k = q.shape[0], k.shape[0]
    causal = ((q_block0 + pids[1]) * n_q + _iota((n_q, n_k), 0)) >= _iota((n_q, n_k), 1)

    qs = [jnp.where(first, q, 0.0), jnp.where(first, 0.0, q)]
    s = _each(lambda qh, cq, ck: jnp.where(causal, dot(qh, k, "nt") + cq - ck, -1e30), qs, [cq_a, cq_b], [ck_a, ck_b])
    e = [jnp.exp(si - lax.stop_gradient(jnp.max(si, axis=1, keepdims=True))) for si in s]
    o_a, o_b = [dot(ei / jnp.sum(ei, axis=1, keepdims=True), v, "nn") for ei in e]
    return (jnp.where(first, o_a, o_b),)


def _sconv_fn(d, pids, sb, sc, sv, w):
    return (sb * _conv(d, sc * sv, w),)


def _dnconv_fn(d, pids, x, w):
    return (_silu(_conv(d, x, w)),)


def _merge_fn(d, pids, y0, y1, y2, g0, g1, g2):
    return (jax.nn.sigmoid(g0) * y0 + jax.nn.sigmoid(g1) * y1 + jax.nn.sigmoid(g2) * y2,)


def _ffn_act_fn(d, pids, ug, uv, wg, wv):
    return (_silu(_conv(d, ug, wg)) * _conv(d, uv, wv),)


def _ple_fn(d, pids, gpre, pe, x):
    return (x + jax.nn.sigmoid(gpre) * pe,)


def _adam_fn(d, pids, w, g, m, v):
    m2 = ADAM_B1 * m + (1.0 - ADAM_B1) * g
    v2 = ADAM_B2 * v + (1.0 - ADAM_B2) * (g * g)
    m_hat = m2 * (1.0 / (1.0 - ADAM_B1 ** ADAM_STEP))
    denom = jnp.sqrt(v2 * (1.0 / (1.0 - ADAM_B2 ** ADAM_STEP))) + ADAM_EPS
    inv = pl.reciprocal(denom, approx=True)
    inv = inv * (2.0 - denom * inv)
    delta = -ADAM_LR * (m_hat * inv + ADAM_WD * w)
    return delta, m2, v2


def _each(fn, *lists):
    return [fn(*args) for args in zip(*lists)]


def _tri_inv_impl(mats):
    n = mats[0].shape[0]
    r, c = _iota((n, n), 0), _iota((n, n), 1)
    diag_blk = (r >> 4) == (c >> 4)
    eye = (r == c).astype(F32)
    mm = lambda us, ws: _each(lambda u, w: _dg(u, w, "nn", SOLVE), us, ws)
    grow = lambda ps, xs: _each(lambda p, px: p + px, ps, mm(ps, xs))
    x = [jnp.where(diag_blk, -a, 0.0) for a in mats]
    p = [eye + xi for xi in x]
    x2 = mm(x, x)
    p = grow(p, x2)
    x4 = mm(x2, x2)
    p = grow(p, x4)
    p = grow(p, mm(x4, x4))
    y = [-yi for yi in mm(p, [jnp.where(diag_blk, 0.0, a) for a in mats])]
    q = grow([eye + yi for yi in y], mm(y, y))
    return mm(q, p)


@jax.custom_vjp
def _tri_inv_diff(mats):
    return _tri_inv_impl(mats)


def _tri_inv_fwd(mats):
    ts = _tri_inv_impl(mats)
    return ts, ts


def _tri_inv_bwd(ts, gs):
    left = _each(lambda t, g: _dg(t, g, "tn", SOLVE), ts, gs)
    return ([-m for m in _each(lambda l, t: _dg(l, t, "nt", SOLVE), left, ts)],)


_tri_inv_diff.defvjp(_tri_inv_fwd, _tri_inv_bwd)


def _dn_local(d, qs, ks, vs, a_cs, a_rs, b_cs, a_logs, dt_bs):
    dot = _bdot(d)
    inv = _tri_inv_diff if d else _tri_inv_impl
    n = qs[0].shape[0]
    r, c = _iota((n, n), 0), _iota((n, n), 1)
    incl, strict, upper = r >= c, r > c, r <= c
    qs = [q * lax.rsqrt(jnp.sum(q * q, axis=1, keepdims=True) + EPS) * DN_DH ** -0.5 for q in qs]
    ks = [k * lax.rsqrt(jnp.sum(k * k, axis=1, keepdims=True) + EPS) for k in ks]
    betas = [jax.nn.sigmoid(b) for b in b_cs]
    rates = [-jnp.exp(a) for a in a_logs]
    g_cs = _each(lambda rate, a, dt: rate * _softplus(a + dt), rates, a_cs, dt_bs)
    g_rs = _each(lambda rate, a, dt: rate * _softplus(a + dt), rates, a_rs, dt_bs)
    gcum_cs = [jnp.sum(jnp.where(incl, g, 0.0), axis=1, keepdims=True) for g in g_rs]
    gcum_rs = [jnp.sum(jnp.where(upper, g, 0.0), axis=0, keepdims=True) for g in g_cs]
    decays = _each(lambda gc, gr: jnp.exp(jnp.where(incl, gc - gr, -1e30)), gcum_cs, gcum_rs)
    kbs = _each(lambda k, b: k * b, ks, betas)
    kk = _each(lambda kb, k: dot(kb, k, "nt"), kbs, ks)
    ts = inv(_each(lambda m, dec: jnp.where(strict, m * dec, 0.0), kk, decays))
    e_gs = [jnp.exp(g) for g in gcum_cs]
    us = _each(lambda t, v, b: _dg(t, v * b, "nn", SOLVE), ts, vs, betas)
    k_cums = _each(lambda t, kb, e: _dg(t, kb * e, "nn", SOLVE), ts, kbs, e_gs)
    qk = _each(lambda q, k: dot(q, k, "nt"), qs, ks)
    qk = _each(lambda m, dec: jnp.where(incl, m * dec, 0.0), qk, decays)
    g_lasts = [jnp.sum(g, axis=0, keepdims=True) for g in g_cs]
    q_decs = _each(lambda q, e: q * e, qs, e_gs)
    k_decs = _each(lambda k, gl, gc: k * jnp.exp(gl - gc), ks, g_lasts, gcum_cs)
    return list(zip(us, k_cums, q_decs, k_decs, qk, g_lasts))


def _dn_step(d, s_prevs, items, zs, gain):
    dot = _bdot(d)
    us, k_cums, q_decs, k_decs, qks, g_lasts = [list(t) for t in zip(*items)]
    v_news = _each(lambda u, kc, s: u - dot(kc, s, "nn"), us, k_cums, s_prevs)
    inter = _each(lambda qd, s: dot(qd, s, "nn"), q_decs, s_prevs)
    outs = _each(lambda o, qk, vn: o + dot(qk, vn, "nn"), inter, qks, v_news)
    s_nexts = _each(lambda s, gl, kd, vn: s * jnp.exp(gl) + dot(kd, vn, "tn"), s_prevs, g_lasts, k_decs, v_news)
    return _each(lambda o, z: _rms(o, gain) * _silu(z), outs, zs), s_nexts


def _split_heads(t):
    return [t[:, h * DN_DH:(h + 1) * DN_DH] for h in range(t.shape[1] // DN_DH)]


def _dn_gates(ps, a_rows, ad):
    hs = range(DN_HEADS)
    return ([_col(ps, 12 + h) for h in hs], [_row(a_rows, h) for h in hs], [_col(ps, 8 + h) for h in hs],
            [_col(_row(ad, 0), h) for h in hs], [_col(_row(ad, 1), h) for h in hs])


def _head_rows(vals):
    row = _iota((8, LANES), 0)
    tile = jnp.zeros((8, LANES), F32)
    for h, val in enumerate(vals):
        tile = tile + jnp.where(row == h, val, 0.0)
    return tile


def _cparams(n_axes):
    return pltpu.CompilerParams(dimension_semantics=("arbitrary",) * n_axes, vmem_limit_bytes=VMEM_LIMIT)


def _first_visit(acc_axes):
    cond = None
    for a in acc_axes:
        here = pl.program_id(a) == 0
        cond = here if cond is None else jnp.logical_and(cond, here)
    return cond


def _tile(ref, widen=False):
    val = ref[...]
    shape = val.shape
    while len(shape) > 2 and shape[0] == 1:
        shape = shape[1:]
    val = val.reshape(shape)
    return val.astype(F32) if widen and val.dtype == BF16 else val


def _store(ref, val, first):
    val = val.astype(ref.dtype).reshape(ref.shape)
    if first is None:
        ref[...] = val
        return

    @pl.when(first)
    def _():
        ref[...] = val

    @pl.when(jnp.logical_not(first))
    def _():
        ref[...] += val


def _specs(ops):
    return [pl.BlockSpec(block, imap) for _, block, imap in ops]


def tile_fwd(name, fn, grid, ins, outs, raw=()):
    n_in = len(ins)

    def body(*refs):
        pids = tuple(pl.program_id(a) for a in range(len(grid)))
        firsts = [_first_visit(o[4]) if o[4] else None for o in outs]
        res = fn(False, pids, *[_tile(r, i not in raw) for i, r in enumerate(refs[:n_in])])
        for ref, val, first in zip(refs[n_in:], res, firsts):
            _store(ref, val, first)

    out = pl.pallas_call(
        body, grid=grid, in_specs=_specs(ins),
        out_specs=[pl.BlockSpec(o[2], o[3]) for o in outs],
        out_shape=[jax.ShapeDtypeStruct(o[0], o[1]) for o in outs],
        name=name, compiler_params=_cparams(len(grid)),
    )(*[a for a, _, _ in ins])
    return out


def tile_bwd(name, fn, grid, ins, cots, diff, adds=None, raw=()):
    adds = adds or {}
    n_in, n_cot = len(ins), len(cots)
    add_pos = sorted(adds)
    diff_idx = [d[0] for d in diff]
    out_desc = [d[2] if len(d) > 2 and d[2] is not None else (ins[d[0]][0].shape, ins[d[0]][1], ins[d[0]][2]) for d in diff]
    out_dtypes = [d[3] if len(d) > 3 else F32 for d in diff]

    def body(*refs):
        pids = tuple(pl.program_id(a) for a in range(len(grid)))
        firsts = [_first_visit(d[1]) if d[1] else None for d in diff]
        vals = [_tile(r, i not in raw) for i, r in enumerate(refs[:n_in])]
        cot_vals = [_tile(r, True) for r in refs[n_in:n_in + n_cot]]
        add_vals = [_tile(r) for r in refs[n_in + n_cot:n_in + n_cot + len(add_pos)]]
        out_refs = refs[n_in + n_cot + len(add_pos):]

        def f(*dv):
            full = list(vals)
            for i, val in zip(diff_idx, dv):
                full[i] = val
            return fn(True, pids, *full)

        prim, vjp = jax.vjp(f, *[vals[i].astype(F32) for i in diff_idx])
        grads = list(vjp(tuple(c.astype(o.dtype) for c, o in zip(cot_vals, prim))))
        for pos, val in zip(add_pos, add_vals):
            grads[pos] = grads[pos] + val.astype(F32)
        for ref, val, first in zip(out_refs, grads, firsts):
            _store(ref, val, first)

    all_ins = list(ins) + list(cots) + [adds[p] for p in add_pos]
    out = pl.pallas_call(
        body, grid=grid, in_specs=_specs(all_ins),
        out_specs=[pl.BlockSpec(o[1], o[2]) for o in out_desc],
        out_shape=[jax.ShapeDtypeStruct(o[0], dt) for o, dt in zip(out_desc, out_dtypes)],
        name=name, compiler_params=_cparams(len(grid)),
    )(*[a for a, _, _ in all_ins])
    return out


def _pick(dim, cands):
    for c in cands:
        if dim % c == 0:
            return c
    return dim


MM_TILES = (1024, 512, 1408, 256, 128)


def mm(name, a, b, mode, add=None, out_dtype=F32, blocks=None):
    wide = None
    if mode == "nn":
        (m, kk), n = a.shape, b.shape[-1]
    elif mode == "nt":
        (m, kk), n = a.shape, b.shape[-2]
    else:
        (kk, m), n = a.shape, b.shape[1]
    if blocks is not None:
        lo, n_blk = blocks
        wide = b.shape[-1] if mode != "tn" else n // n_blk
        if mode == "nn":
            n = wide * n_blk
    tm = _pick(m, MM_TILES)
    if mode == "nt" and blocks is not None:
        tn, tk = _pick(n, MM_TILES), _pick(wide, MM_TILES[:-1])
    elif blocks is not None:
        tn, tk = _pick(wide, MM_TILES[:-1]), _pick(kk, MM_TILES)
    else:
        tn, tk = _pick(n, MM_TILES), _pick(kk, MM_TILES)
    nk = kk // tk
    a_spec = pl.BlockSpec((tk, tm), lambda i, j, k: (k, i)) if mode == "tn" else pl.BlockSpec((tm, tk), lambda i, j, k: (i, k))
    o_spec = pl.BlockSpec((tm, tn), lambda i, j, k: (i, j))
    out_shape = (m, n)
    if blocks is None:
        b_spec = pl.BlockSpec((tn, tk), lambda i, j, k: (j, k)) if mode == "nt" else pl.BlockSpec((tk, tn), lambda i, j, k: (k, j))
    elif mode == "nn":
        per = wide // tn
        b_spec = pl.BlockSpec((1, tk, tn), lambda i, j, k: (lo + j // per, k, j % per))
    elif mode == "nt":
        per = wide // tk
        b_spec = pl.BlockSpec((1, tn, tk), lambda i, j, k: (lo + k // per, j, k % per))
    else:
        per = wide // tn
        b_spec = pl.BlockSpec((tk, tn), lambda i, j, k: (k, j))
        o_spec = pl.BlockSpec((1, tm, tn), lambda i, j, k: (j // per, i, j % per))
        out_shape = (n_blk, m, wide)

    def body(*refs):
        a_ref, b_ref = refs[0], refs[1]
        add_ref = refs[2] if add is not None else None
        o_ref, acc = refs[-2], refs[-1]
        k = pl.program_id(2)
        part = _bdot_impl(_tile(a_ref), _tile(b_ref), mode)

        @pl.when(k == 0)
        def _():
            acc[...] = part

        @pl.when(k > 0)
        def _():
            acc[...] += part

        @pl.when(k == nk - 1)
        def _():
            res = acc[...]
            if add_ref is not None:
                res = res + add_ref[...]
            o_ref[...] = res.astype(o_ref.dtype).reshape(o_ref.shape)

    operands = [a, b] + ([add] if add is not None else [])
    in_specs = [a_spec, b_spec] + ([o_spec] if add is not None else [])
    return pl.pallas_call(
        body, grid=(m // tm, n // tn, nk), in_specs=in_specs, out_specs=o_spec,
        out_shape=jax.ShapeDtypeStruct(out_shape, out_dtype),
        scratch_shapes=[pltpu.VMEM((tm, tn), F32)],
        name=name, compiler_params=_cparams(3),
    )(*operands)


def _rows(x, width=None, off=0, tm=256):
    width = x.shape[1] if width is None else width
    return (x, (tm, width), lambda i, off=off: (i, off))


def _whole(x):
    nd = x.ndim
    return (x, x.shape, lambda *pids, nd=nd: (0,) * nd)


def _rms_ops(x, gain):
    return [_rows(x), _whole(gain)]


def rms_fwd(name, x, gain):
    s, dm = x.shape
    return tile_fwd(name, _rms_fn, (s // 256,), _rms_ops(x, gain), [((s, dm), BF16, (256, dm), lambda i: (i, 0), ())])[0]


def rms_bwd(name, x, gain, dh, dres):
    s = x.shape[0]
    return tile_bwd(name, _rms_fn, (s // 256,), _rms_ops(x, gain), [_rows(dh)], [(0, ()), (1, (0,))], adds={0: _rows(dres)})


def loss_call(y, t):
    s, dm = y.shape
    dy, part = tile_fwd("loss", _loss_fn, (s // 256,), [_rows(y), _rows(t)],
                        [((s, dm), F32, (256, dm), lambda i: (i, 0), ()), ((8, LANES), F32, (8, LANES), lambda i: (0, 0), (0,))])
    return dy, part[0, 0]


def _fox_prep_ops(pm, gq, gk):
    tm = 512
    return [(pm, (tm, LANES), lambda i, j: (i, C_FQ // LANES + j)), (pm, (tm, LANES), lambda i, j: (i, C_FK // LANES + j)),
            _whole(gq), _whole(gk)]


def fox_prep_fwd(name, pm, gq, gk):
    s = pm.shape[0]
    out = ((s, BRANCH), BF16, (512, LANES), lambda i, j: (i, j), ())
    return tile_fwd(name, _fox_prep_fn, (s // 512, 4), _fox_prep_ops(pm, gq, gk), [out, out])


def fox_prep_bwd(name, pm, gq, gk, dqn, dkn):
    s = pm.shape[0]
    cot = lambda g: (g, (512, LANES), lambda i, j: (i, j))
    own = ((s, BRANCH), (512, LANES), lambda i, j: (i, j))
    return tile_bwd(name, _fox_prep_fn, (s // 512, 4), _fox_prep_ops(pm, gq, gk), [cot(dqn), cot(dkn)],
                    [(0, (), own, BF16), (1, (), own, BF16), (2, (0, 1)), (3, (0, 1))])


def _fox_gate_ops(f_t, bias):
    return [(f_t, (1,) + f_t.shape[1:], lambda h: (h, 0, 0)), (bias, (1, 1, 1), lambda h: (h, 0, 0))]


def fox_gate_fwd(name, f_t, bias):
    n_h = f_t.shape[0]
    return tile_fwd(name, _fox_gate_fn, (n_h,), _fox_gate_ops(f_t, bias),
                    [(f_t.shape, F32, (1,) + f_t.shape[1:], lambda h: (h, 0, 0), ())])[0]


def fox_gate_bwd(name, f_t, bias, dcum):
    n_h = f_t.shape[0]
    return tile_bwd(name, _fox_gate_fn, (n_h,), _fox_gate_ops(f_t, bias),
                    [(dcum, (1,) + f_t.shape[1:], lambda h: (h, 0, 0))], [(0, ()), (1, ())])


FOX_GROUPS = 4


def _fox_groups(s):
    per = s // FOX_BLOCK // FOX_GROUPS
    return [(g * per, per, (g + 1) * per * FOX_BLOCK) for g in range(FOX_GROUPS)]


def _fox_attn_ops(qn, kn, pm, cum_c, cum_r, q0, keys):
    nb = FOX_BLOCK
    return [(qn, (nb, LANES), lambda p, i: (q0 + i, p)), (kn, (keys, LANES), lambda p, i: (0, p)),
            (pm, (keys, LANES), lambda p, i: (0, C_FV // LANES + p)),
            (cum_c, (1, nb, 1), lambda p, i: (2 * p, q0 + i, 0)), (cum_c, (1, nb, 1), lambda p, i: (2 * p + 1, q0 + i, 0)),
            (cum_r, (1, 1, keys), lambda p, i: (2 * p, 0, 0)), (cum_r, (1, 1, keys), lambda p, i: (2 * p + 1, 0, 0))]


def fox_attn_fwd(name, qn, kn, pm, cum_c, cum_r):
    s = qn.shape[0]
    parts = []
    for g, (q0, n_q, keys) in enumerate(_fox_groups(s)):
        parts.append(tile_fwd(f"{name}_g{g}", functools.partial(_fox_attn_fn, q0), (4, n_q), _fox_attn_ops(qn, kn, pm, cum_c, cum_r, q0, keys),
                              [((n_q * FOX_BLOCK, BRANCH), BF16, (FOX_BLOCK, LANES), lambda p, i: (i, p), ())], raw=(0, 1, 2))[0])
    return jnp.concatenate(parts, axis=0)


def fox_attn_bwd(name, qn, kn, pm, cum_c, cum_r, dy):
    s = qn.shape[0]
    d_qn, d_kn, d_v, d_cum = [], 0.0, 0.0, 0.0
    for g, (q0, n_q, keys) in enumerate(_fox_groups(s)):
        rows = n_q * FOX_BLOCK
        own_q = ((rows, BRANCH), (FOX_BLOCK, LANES), lambda p, i: (i, p))
        own_k = ((keys, BRANCH), (keys, LANES), lambda p, i: (0, p))
        pair_c = ((4, rows, 1), (1, FOX_BLOCK, 1), lambda p, i: (p, i, 0))
        pair_r = ((4, 1, keys), (1, 1, keys), lambda p, i: (p, 0, 0))
        g_qn, g_kn, g_v, g_cqa, g_cqb, g_cka, g_ckb = tile_bwd(
            f"{name}_g{g}", functools.partial(_fox_attn_fn, q0), (4, n_q), _fox_attn_ops(qn, kn, pm, cum_c, cum_r, q0, keys),
            [(dy, (FOX_BLOCK, LANES), lambda p, i, q0=q0: (q0 + i, p))],
            [(0, (), own_q), (1, (1,), own_k), (2, (1,), own_k), (3, (), pair_c), (4, (), pair_c), (5, (1,), pair_r), (6, (1,), pair_r)])
        d_qn.append(g_qn)
        tail = lambda t, axis: jnp.pad(t, [(0, s - keys) if ax == axis else (0, 0) for ax in range(t.ndim)])
        d_kn, d_v = d_kn + tail(g_kn, 0), d_v + tail(g_v, 0)
        by_q = jnp.stack([g_cqa[:, :, 0], g_cqb[:, :, 0]], axis=1).reshape(8, rows)
        by_k = jnp.stack([g_cka[:, 0, :], g_ckb[:, 0, :]], axis=1).reshape(8, keys)
        d_cum = d_cum + jnp.pad(by_q, [(0, 0), (q0 * FOX_BLOCK, s - q0 * FOX_BLOCK - rows)]) + tail(by_k, 1)
    return jnp.concatenate(d_qn, axis=0), d_kn, d_v, d_cum


def sconv_ops(pm, w):
    s = pm.shape[0]
    blk = lambda c0: (pm, (s, LANES), lambda j, c0=c0: (0, c0 // LANES + j))
    return [blk(C_SB), blk(C_SC), blk(C_SV), (w, (w.shape[0], LANES), lambda j: (0, j))]


def dnconv_ops(pm, w):
    s = pm.shape[0]
    return [(pm, (s, LANES), lambda j: (0, C_DN // LANES + j)), (w, (w.shape[0], LANES), lambda j: (0, j))]


def ffn_ops(ug, uv, w):
    s = ug.shape[0]
    n_t = D_FF // LANES
    return [(ug, (s, LANES), lambda j: (0, j)), (uv, (s, LANES), lambda j: (0, j)),
            (w, (w.shape[0], LANES), lambda j: (0, j)), (w, (w.shape[0], LANES), lambda j: (0, n_t + j))]


def _col_out(s, width, dtype=F32):
    return ((s, width), dtype, (s, LANES), lambda j: (0, j), ())


def _col_cot(g):
    return (g, (g.shape[0], LANES), lambda j: (0, j))


def merge_ops(yp, pm):
    gate = lambda b: (pm, (256, D_MODEL), lambda i, b=b: (i, C_GATE // D_MODEL + b))
    return [_rows(yp[0]), _rows(yp[1]), _rows(yp[2]), gate(0), gate(1), gate(2)]


def ple_ops(gpre, pe, x):
    return [_rows(gpre), _rows(pe), _rows(x)]


def adam_call(name, w, g, m, v):
    shape = w.shape
    last = shape[-1]
    rows = w.size // last
    flat = lambda t: t.reshape(rows, last)
    tm = rows
    for cand in (512, 256, 128, 64, 32, 16, 8):
        if rows % cand == 0 and cand * last * 4 <= 2 * 1024 * 1024:
            tm = cand
            break
    spec = lambda t: (flat(t), (tm, last), lambda i: (i, 0))
    out = ((rows, last), F32, (tm, last), lambda i: (i, 0), ())
    res = tile_fwd(name, _adam_fn, (rows // tm,), [spec(w), spec(g), spec(m), spec(v)], [out, out, out])
    return [r.reshape(shape) for r in res]


def _adam_layers_fn(d, pids, w, m, v, g0, g1):
    g = jnp.where(pids[0] == 0, g0, g1)
    return (g,) + _adam_fn(d, pids, w, g, m, v)


def adam_layers(name, w, m, v, g0, g1):
    _, rows, cols = w.shape
    tm = _row_tile(rows, cols)
    n_t = rows // tm
    lay = lambda t: (t, (1, tm, cols), lambda l, i: (l, i, 0))
    ins = [lay(w), lay(m), lay(v), (g0, (tm, cols), lambda l, i: (i * (1 - l) + (n_t - 1) * l, 0)), (g1, (tm, cols), lambda l, i: (i * l, 0))]
    out = (w.shape, F32, (1, tm, cols), lambda l, i: (l, i, 0), ())
    return tile_fwd(name, _adam_layers_fn, (2, n_t), ins, [out, out, out, out])


def adam_w_in(name, w, m, v, g0, g1):
    rows, n_l, cols = w.shape

    def body(w_ref, m_ref, v_ref, g0_ref, g1_ref, g_out, d_out, m_out, v_out):
        step = 64

        def update(at):
            for l, g_ref in enumerate((g0_ref, g1_ref)):
                g = g_ref[at, :]
                delta, m2, v2 = _adam_fn(False, None, w_ref[at, l, :], g, m_ref[at, l, :], v_ref[at, l, :])
                for ref, val in ((g_out, g), (d_out, delta), (m_out, m2), (v_out, v2)):
                    ref[at, l, :] = val

        def some_rows(i, carry):
            update(pl.ds(pl.multiple_of(i * step, step), step))
            return carry

        lax.fori_loop(0, rows // step, some_rows, 0)
        if rows % step:
            update(pl.ds(rows - rows % step, rows % step))

    both = pl.BlockSpec((rows, n_l, LANES), lambda j: (0, 0, j))
    one = pl.BlockSpec((rows, LANES), lambda j: (0, j))
    return pl.pallas_call(
        body, grid=(cols // LANES,), in_specs=[both, both, both, one, one], out_specs=[both] * 4,
        out_shape=[jax.ShapeDtypeStruct(w.shape, F32)] * 4, name=name, compiler_params=_cparams(1),
    )(w, m, v, g0, g1)


DN_GROUP = 4


def _dn_local_specs(rev_n=None):
    rows = DN_GROUP * DN_CHUNK
    idx = (lambda j: j) if rev_n is None else (lambda j: rev_n - 1 - j)
    return [pl.BlockSpec((rows, 3 * BRANCH), lambda j: (idx(j), 0)), pl.BlockSpec((rows, LANES), lambda j: (idx(j), 0)),
            pl.BlockSpec((DN_GROUP, DN_HEADS, DN_CHUNK), lambda j: (idx(j), 0, 0)), pl.BlockSpec((2, DN_HEADS), lambda j: (0, 0))]


def _dn_group_inputs(qkv, ps, a_rows, c):
    lo = c * DN_CHUNK
    heads = _split_heads(qkv[lo:lo + DN_CHUNK])
    return heads[0:4], heads[4:8], heads[8:12], ps[lo:lo + DN_CHUNK], a_rows[c]


def dn_local_fwd(name, dn_act, ps, a_rows, ad):
    s = dn_act.shape[0]
    n_c, n_g = s // DN_CHUNK, s // (DN_GROUP * DN_CHUNK)
    rows = DN_GROUP * DN_CHUNK

    def body(qkv_ref, ps_ref, ar_ref, ad_ref, u_ref, kc_ref, qd_ref, kd_ref, qk_ref, gl_ref):
        qkv, ps_v, a_rows_v, ad_v = qkv_ref[...], ps_ref[...], ar_ref[...], ad_ref[...]
        args = [[] for _ in range(8)]
        for c in range(DN_GROUP):
            q4, k4, v4, ps_c, ar_c = _dn_group_inputs(qkv, ps_v, a_rows_v, c)
            for lst, vals in zip(args, (q4, k4, v4) + _dn_gates(ps_c, ar_c, ad_v)):
                lst.extend(vals)
        everything = _dn_local(False, *args)
        for c in range(DN_GROUP):
            res = everything[c * DN_HEADS:(c + 1) * DN_HEADS]
            at = pl.ds(c * DN_CHUNK, DN_CHUNK)
            for ref, i in ((u_ref, 0), (kc_ref, 1), (qd_ref, 2), (kd_ref, 3)):
                ref[at, :] = jnp.concatenate([r[i] for r in res], axis=1)
            for h in range(DN_HEADS):
                qk_ref[c, h] = res[h][4]
            gl_ref[c] = _head_rows([r[5] for r in res])

    wide = pl.BlockSpec((rows, BRANCH), lambda j: (j, 0))
    return pl.pallas_call(
        body, grid=(n_g,), in_specs=_dn_local_specs(),
        out_specs=[wide, wide, wide, wide, pl.BlockSpec((DN_GROUP, DN_HEADS, DN_CHUNK, DN_CHUNK), lambda j: (j, 0, 0, 0)),
                   pl.BlockSpec((DN_GROUP, 8, LANES), lambda j: (j, 0, 0))],
        out_shape=[jax.ShapeDtypeStruct((s, BRANCH), F32)] * 4 + [jax.ShapeDtypeStruct((n_c, DN_HEADS, DN_CHUNK, DN_CHUNK), F32),
                                                                 jax.ShapeDtypeStruct((n_c, 8, LANES), F32)],
        name=name, compiler_params=_cparams(1),
    )(dn_act, ps, a_rows, ad)


def dn_local_bwd(name, dn_act, ps, a_rows, ad, cots):
    s = dn_act.shape[0]
    n_c, n_g = s // DN_CHUNK, s // (DN_GROUP * DN_CHUNK)
    rows = DN_GROUP * DN_CHUNK

    def body(qkv_ref, ps_ref, ar_ref, ad_ref, du_ref, dkc_ref, dqd_ref, dkd_ref, dqk_ref, dgl_ref, dqkv_ref, dps_ref, dar_ref, dad_ref):
        first = pl.program_id(0) == 0
        qkv, ps_v, a_rows_v, ad_v = qkv_ref[...], ps_ref[...], ar_ref[...], ad_ref[...]
        d_wide = [r[...] for r in (du_ref, dkc_ref, dqd_ref, dkd_ref)]
        qs, ks, vs, ps_cs, ar_cs, cot = [], [], [], [], [], []
        for c in range(DN_GROUP):
            q4, k4, v4, ps_c, ar_c = _dn_group_inputs(qkv, ps_v, a_rows_v, c)
            qs, ks, vs, ps_cs, ar_cs = qs + q4, ks + k4, vs + v4, ps_cs + [ps_c], ar_cs + [ar_c]
            lo = c * DN_CHUNK
            d_tiles = [_split_heads(t[lo:lo + DN_CHUNK]) for t in d_wide]
            d_gl = dgl_ref[c]
            cot += [(d_tiles[0][h], d_tiles[1][h], d_tiles[2][h], d_tiles[3][h], dqk_ref[c, h], _col(_row(d_gl, h), 0))
                    for h in range(DN_HEADS)]

        def f(qs, ks, vs, ps_cs, ar_cs, ad_v):
            gates = [[] for _ in range(5)]
            for ps_c, ar_c in zip(ps_cs, ar_cs):
                for lst, vals in zip(gates, _dn_gates(ps_c, ar_c, ad_v)):
                    lst.extend(vals)
            return _dn_local(True, qs, ks, vs, *gates)

        _, vjp = jax.vjp(f, qs, ks, vs, ps_cs, ar_cs, ad_v)
        d_q, d_k, d_v, d_ps, d_ar, d_ad = vjp(cot)
        for c in range(DN_GROUP):
            at, hs = pl.ds(c * DN_CHUNK, DN_CHUNK), slice(c * DN_HEADS, (c + 1) * DN_HEADS)
            dqkv_ref[at, :] = jnp.concatenate(d_q[hs] + d_k[hs] + d_v[hs], axis=1).astype(dqkv_ref.dtype)
            dps_ref[at, :] = d_ps[c]
            dar_ref[c] = d_ar[c]
        _store(dad_ref, d_ad, first)

    wide = pl.BlockSpec((rows, BRANCH), lambda j: (j, 0))
    specs = _dn_local_specs()
    return pl.pallas_call(
        body, grid=(n_g,),
        in_specs=specs + [wide, wide, wide, wide, pl.BlockSpec((DN_GROUP, DN_HEADS, DN_CHUNK, DN_CHUNK), lambda j: (j, 0, 0, 0)),
                          pl.BlockSpec((DN_GROUP, 8, LANES), lambda j: (j, 0, 0))],
        out_specs=specs,
        out_shape=[jax.ShapeDtypeStruct((s, 3 * BRANCH), F32), jax.ShapeDtypeStruct((s, LANES), F32),
                   jax.ShapeDtypeStruct((n_c, DN_HEADS, DN_CHUNK), F32), jax.ShapeDtypeStruct((2, DN_HEADS), F32)],
        name=name, compiler_params=_cparams(1),
    )(dn_act, ps, a_rows, ad, *cots)


def _dn_scan_specs(n_c, rev):
    idx = (lambda j: n_c - 1 - j) if rev else (lambda j: j)
    wide = pl.BlockSpec((DN_CHUNK, BRANCH), lambda j: (idx(j), 0))
    return [wide, wide, wide, wide, pl.BlockSpec((1, DN_HEADS, DN_CHUNK, DN_CHUNK), lambda j: (idx(j), 0, 0, 0)),
            pl.BlockSpec((1, 8, LANES), lambda j: (idx(j), 0, 0)), pl.BlockSpec((DN_CHUNK, BRANCH), lambda j: (idx(j), C_DZ // BRANCH)),
            pl.BlockSpec((1, DN_DH), lambda j: (0, 0))]


def _dn_scan_tiles(refs):
    u_ref, kc_ref, qd_ref, kd_ref, qk_ref, gl_ref, z_ref, g_ref = refs
    wide = [_split_heads(r[...]) for r in (u_ref, kc_ref, qd_ref, kd_ref)]
    gl = gl_ref[0]
    return [(wide[0][h], wide[1][h], wide[2][h], wide[3][h], qk_ref[0, h], _col(_row(gl, h), 0)) for h in range(DN_HEADS)], \
        _split_heads(z_ref[...].astype(F32)), g_ref[...]


def dn_scan_fwd(name, local, pm, gain):
    s = pm.shape[0]
    n_c = s // DN_CHUNK

    def body(*refs):
        y_ref, hist_ref, state = refs[8:]

        @pl.when(pl.program_id(0) == 0)
        def _():
            state[...] = jnp.zeros_like(state)

        hist_ref[0] = state[...]
        per_head, z4, gain_v = _dn_scan_tiles(refs[:8])
        ys, s_nexts = _dn_step(False, [state[h] for h in range(DN_HEADS)], per_head, z4, gain_v)
        for h in range(DN_HEADS):
            state[h] = s_nexts[h]
        y_ref[...] = jnp.concatenate(ys, axis=1).astype(y_ref.dtype)

    return pl.pallas_call(
        body, grid=(n_c,), in_specs=_dn_scan_specs(n_c, False),
        out_specs=[pl.BlockSpec((DN_CHUNK, BRANCH), lambda j: (j, 0)),
                   pl.BlockSpec((1, DN_HEADS, DN_DH, DN_DH), lambda j: (j, 0, 0, 0))],
        out_shape=[jax.ShapeDtypeStruct((s, BRANCH), BF16), jax.ShapeDtypeStruct((n_c, DN_HEADS, DN_DH, DN_DH), F32)],
        scratch_shapes=[pltpu.VMEM((DN_HEADS, DN_DH, DN_DH), F32)],
        name=name, compiler_params=_cparams(1),
    )(*local, pm, gain)


def dn_scan_bwd(name, local, pm, gain, hist, dy):
    s = pm.shape[0]
    n_c = s // DN_CHUNK

    def body(*refs):
        hist_ref, dy_ref = refs[8:10]
        du_ref, dkc_ref, dqd_ref, dkd_ref, dqk_ref, dgl_ref, dz_ref, dg_ref, d_state = refs[10:]
        first = pl.program_id(0) == 0

        @pl.when(first)
        def _():
            d_state[...] = jnp.zeros_like(d_state)

        per_head, z4, gain_v = _dn_scan_tiles(refs[:8])
        _, vjp = jax.vjp(functools.partial(_dn_step, True), [hist_ref[0, h] for h in range(DN_HEADS)], per_head, z4, gain_v)
        d_s, grads, d_z, d_gain = vjp((_split_heads(dy_ref[...].astype(F32)), [d_state[h] for h in range(DN_HEADS)]))
        for h in range(DN_HEADS):
            d_state[h] = d_s[h]
        for ref, i in ((du_ref, 0), (dkc_ref, 1), (dqd_ref, 2), (dkd_ref, 3)):
            ref[...] = jnp.concatenate([g[i] for g in grads], axis=1)
        dz_ref[...] = jnp.concatenate(d_z, axis=1).astype(dz_ref.dtype)
        for h in range(DN_HEADS):
            dqk_ref[0, h] = grads[h][4]
        dgl_ref[0] = _head_rows([g[5] for g in grads])
        _store(dg_ref, d_gain, first)

    rev = lambda j: n_c - 1 - j
    specs = _dn_scan_specs(n_c, True)
    return pl.pallas_call(
        body, grid=(n_c,),
        in_specs=specs + [pl.BlockSpec((1, DN_HEADS, DN_DH, DN_DH), lambda j: (rev(j), 0, 0, 0)),
                          pl.BlockSpec((DN_CHUNK, BRANCH), lambda j: (rev(j), 0))],
        out_specs=specs[:6] + [pl.BlockSpec((DN_CHUNK, BRANCH), lambda j: (rev(j), 0)), specs[7]],
        out_shape=[jax.ShapeDtypeStruct((s, BRANCH), F32)] * 4 + [
            jax.ShapeDtypeStruct((n_c, DN_HEADS, DN_CHUNK, DN_CHUNK), F32), jax.ShapeDtypeStruct((n_c, 8, LANES), F32),
            jax.ShapeDtypeStruct((s, BRANCH), BF16), jax.ShapeDtypeStruct((1, DN_DH), F32)],
        scratch_shapes=[pltpu.VMEM((DN_HEADS, DN_DH, DN_DH), F32)],
        name=name, compiler_params=_cparams(1),
    )(*local, pm, gain, hist, dy)


def _seq_layouts(cols, s):
    return cols.T.reshape(cols.shape[1], s // LANES, LANES)


def layer_fwd(li, x, p, w, more_weights=None):
    s = x.shape[0]
    n = lambda t: f"{t}_l{li}"
    h = rms_fwd(n("rms_mix"), x, w["g_mix"])
    pm = mm(n("in_main"), h, w["in_main"], "nn")
    ps = mm(n("in_small"), h, w["in_small"], "nn")
    qn, kn = fox_prep_fwd(n("fox_prep"), pm, w["gq"], w["gk"])
    f_t = _seq_layouts(ps[:, 0:8], s)
    cum = fox_gate_fwd(n("fox_gate"), f_t, w["b_f"])
    cum_c, cum_r = cum.reshape(8, s, 1), cum.reshape(8, 1, s)
    y_fox = fox_attn_fwd(n("fox_attn"), qn, kn, pm, cum_c, cum_r)
    y_sc = tile_fwd(n("sconv"), _sconv_fn, (BRANCH // LANES,), sconv_ops(pm, w["sc_conv_w"]), [_col_out(s, BRANCH, BF16)])[0]
    dn_act = tile_fwd(n("dnconv"), _dnconv_fn, (3 * BRANCH // LANES,), dnconv_ops(pm, w["dn_conv_w"]), [_col_out(s, 3 * BRANCH)])[0]
    a_rows = ps[:, 12:16].reshape(s // DN_CHUNK, DN_CHUNK, DN_HEADS).transpose(0, 2, 1)
    dn_local = dn_local_fwd(n("dn_local"), dn_act, ps, a_rows, w["ad"])
    y_dn, hist = dn_scan_fwd(n("dn_scan"), dn_local, pm, w["dn_gain"])
    ys = (y_fox, y_sc, y_dn)
    if more_weights is not None:
        w = {**w, **more_weights(y_dn)}
    yp = [mm(n(f"branch{b}"), ys[b], w["branch"][b], "nn", blocks=(0, N_CHIPS)) for b in range(3)]
    merged = tile_fwd(n("merge"), _merge_fn, (s // 256,), merge_ops(yp, pm), [((s, D_MODEL), BF16, (256, D_MODEL), lambda i: (i, 0), ())])[0]
    x1 = mm(n("w_o"), merged, w["o"], "nn", add=x)
    h2 = rms_fwd(n("rms_ffn"), x1, w["g_ffn"])
    ug = mm(n("up_g"), h2, w["up"], "nn", blocks=(0, 2))
    uv = mm(n("up_v"), h2, w["up"], "nn", blocks=(2, 2))
    act = tile_fwd(n("ffn_act"), _ffn_act_fn, (D_FF // LANES,), ffn_ops(ug, uv, w["ffn_conv_w"]), [_col_out(s, D_FF, BF16)])[0]
    x2 = mm(n("down"), act, w["down"], "nn", add=x1)
    h3 = rms_fwd(n("rms_ple"), x2, w["g_ple"])
    gpre = mm(n("ple_gate"), h3, w["pg"], "nn")
    pe = mm(n("ple_emb"), p, w["ple"], "nn", blocks=(0, N_CHIPS))
    x3 = tile_fwd(n("ple"), _ple_fn, (s // 256,), ple_ops(gpre, pe, x2), [((s, D_MODEL), F32, (256, D_MODEL), lambda i: (i, 0), ())])[0]
    saved = dict(x=x, h=h, pm=pm, ps=ps, qn=qn, kn=kn, f_t=f_t, cum_c=cum_c, cum_r=cum_r, ys=ys, dn_act=dn_act, dn_local=dn_local,
                 a_rows=a_rows, hist=hist, yp=yp, merged=merged, x1=x1, h2=h2, ug=ug, uv=uv, act=act, x2=x2, h3=h3,
                 gpre=gpre, pe=pe, p=p)
    return x3, saved, w


def hang_on(w, token):
    zero = token[0, 0]
    small = ("g_mix", "g_ffn", "g_ple", "gq", "gk", "b_f", "ad", "dn_gain", "sc_conv_w", "dn_conv_w", "ffn_conv_w")
    return {**w, **{k: w[k] + zero for k in small}}


def layer_bwd(li, dx3, sv, w, hooks=None):
    hooks = hooks or {}

    def stage(key, after, w):
        return hang_on(w, hooks[key](after, g)) if key in hooks else w

    s = dx3.shape[0]
    n = lambda t: f"{t}_l{li}"
    g = {}
    col_own = lambda width: ((s, width), (s, LANES), lambda j: (0, j))
    d_gpre, d_pe = tile_bwd(n("ple_bwd"), _ple_fn, (s // 256,), ple_ops(sv["gpre"], sv["pe"], sv["x2"]), [_rows(dx3)],
                            [(0, (), None, BF16), (1, (), None, BF16)])
    g["w_ple"] = mm(n("d_w_ple"), sv["p"], d_pe, "tn", blocks=(0, N_CHIPS))
    g["w_ple_gate"] = mm(n("d_w_pg"), sv["h3"], d_gpre, "tn").reshape(N_CHIPS, -1, D_MODEL)
    dh3 = mm(n("d_h3"), d_gpre, w["pg"], "nt")
    dx2, d_g_ple = rms_bwd(n("rms_ple_bwd"), sv["x2"], w["g_ple"], dh3, dx3)
    dact = mm(n("d_act"), dx2, w["down"], "nt")
    g["w_down"] = mm(n("d_w_down"), sv["act"], dx2, "tn").reshape(N_CHIPS, -1, D_MODEL)
    taps_own = ((w["ffn_conv_w"].shape[0], D_FF), (w["ffn_conv_w"].shape[0], LANES), lambda j: (0, j))
    d_ug, d_uv, d_fw_g, d_fw_v = tile_bwd(n("ffn_act_bwd"), _ffn_act_fn, (D_FF // LANES,), ffn_ops(sv["ug"], sv["uv"], w["ffn_conv_w"]),
                                          [_col_cot(dact)], [(0, (), None, BF16), (1, (), None, BF16), (2, (), taps_own), (3, (), taps_own)])
    g["ffn_conv_w"] = jnp.concatenate([d_fw_g, d_fw_v], axis=1)
    g["w_up"] = jnp.concatenate([mm(n("d_w_up_g"), sv["h2"], d_ug, "tn", blocks=(0, 2)), mm(n("d_w_up_v"), sv["h2"], d_uv, "tn", blocks=(0, 2))])
    dh2 = mm(n("d_h2_v"), d_uv, w["up"], "nt", blocks=(2, 2), add=mm(n("d_h2_g"), d_ug, w["up"], "nt", blocks=(0, 2)))
    dx1, d_g_ffn = rms_bwd(n("rms_ffn_bwd"), sv["x1"], w["g_ffn"], dh2, dx2)
    w = stage("mid", dx1, w)
    dmerged = mm(n("d_merged"), dx1, w["o"], "nt")
    g["w_o"] = mm(n("d_w_o"), sv["merged"], dx1, "tn").reshape(N_CHIPS, -1, D_MODEL)
    gate_own = ((s, D_MODEL), (256, D_MODEL), lambda i: (i, 0))
    d_yp0, d_yp1, d_yp2, d_g0, d_g1, d_g2 = tile_bwd(
        n("merge_bwd"), _merge_fn, (s // 256,), merge_ops(sv["yp"], sv["pm"]), [_rows(dmerged)],
        [(0, (), None, BF16), (1, (), None, BF16), (2, (), None, BF16), (3, (), gate_own, BF16), (4, (), gate_own, BF16), (5, (), gate_own, BF16)])
    d_yp = (d_yp0, d_yp1, d_yp2)
    g["w_branch"] = jnp.concatenate([mm(n(f"d_w_branch{b}"), sv["ys"][b], d_yp[b], "tn", blocks=(0, N_CHIPS)) for b in range(3)], axis=1)
    d_ys = [mm(n(f"d_y{b}"), d_yp[b], w["branch"][b], "nt", blocks=(0, N_CHIPS)) for b in range(3)]
    w = stage("late", d_ys[2], w)
    *d_local, d_z, d_dngain = dn_scan_bwd(n("dn_scan_bwd"), sv["dn_local"], sv["pm"], w["dn_gain"], sv["hist"], d_ys[2])
    d_dnact, d_ps_dn, d_arows, d_ad = dn_local_bwd(n("dn_local_bwd"), sv["dn_act"], sv["ps"], sv["a_rows"], w["ad"], d_local)
    g["ad"], g["dn_norm_gain"] = d_ad, d_dngain[0]
    d_dnqkv, g["dn_conv_w"] = tile_bwd(n("dnconv_bwd"), _dnconv_fn, (3 * BRANCH // LANES,), dnconv_ops(sv["pm"], w["dn_conv_w"]),
                                       [_col_cot(d_dnact)], [(0, (), col_own(3 * BRANCH), BF16), (1, ())])
    d_sb, d_sc, d_sv, g["sc_conv_w"] = tile_bwd(n("sconv_bwd"), _sconv_fn, (BRANCH // LANES,), sconv_ops(sv["pm"], w["sc_conv_w"]), [_col_cot(d_ys[1])],
                                                [(0, (), col_own(BRANCH), BF16), (1, (), col_own(BRANCH), BF16), (2, (), col_own(BRANCH), BF16), (3, ())])
    w = stage("last", d_dnqkv, w)
    d_qn, d_kn, d_fv, d_cum = fox_attn_bwd(n("fox_attn_bwd"), sv["qn"], sv["kn"], sv["pm"], sv["cum_c"], sv["cum_r"], d_ys[0])
    d_ft, d_bf = fox_gate_bwd(n("fox_gate_bwd"), sv["f_t"], w["b_f"], d_cum.reshape(8, s // LANES, LANES))
    g["b_fox_f"] = d_bf.reshape(8)
    d_fq, d_fk, d_gq, d_gk = fox_prep_bwd(n("fox_prep_bwd"), sv["pm"], w["gq"], w["gk"], d_qn, d_kn)
    g["fox_q_gain"] = d_gq[0, :FOX_DH] + d_gq[0, FOX_DH:]
    g["fox_k_gain"] = d_gk[0, :FOX_DH] + d_gk[0, FOX_DH:]
    d_pm = jnp.concatenate([d_fq, d_fk, d_fv.astype(BF16), d_sb, d_sc, d_sv, d_dnqkv, d_z, d_g0, d_g1, d_g2], axis=1)
    d_a_cols = d_arows.transpose(0, 2, 1).reshape(s, DN_HEADS)
    d_f_cols = d_ft.reshape(8, s).T
    d_ps = d_ps_dn + jnp.concatenate([d_f_cols, jnp.zeros((s, 4), F32), d_a_cols, jnp.zeros((s, LANES - 16), F32)], axis=1)
    g["w_in"] = chip_blocks_w_in(mm(n("d_w_in_main"), d_pm, sv["h"], "tn"), mm(n("d_w_in_small"), d_ps, sv["h"], "tn"))
    w = stage("w_in", g["w_in"], w)
    dh = mm(n("d_h_small"), d_ps, w["in_small"], "nt", add=mm(n("d_h_main"), d_pm, w["in_main"], "nt"))
    dx, d_g_mix = rms_bwd(n("rms_mix_bwd"), sv["x"], w["g_mix"], dh, dx1)
    g["g_mix"], g["g_ffn"], g["g_ple"] = d_g_mix[0], d_g_ffn[0], d_g_ple[0]
    return dx, g


IN_SHARD = 2052
MAIN_RANGES = ((0, 1536), (1544, 3080), (3080, 4616), (4624, 5136), (5136, 8208))
SMALL_RANGES = ((1536, 1544), (4616, 4620), (4620, 4624))


def _from_chip_blocks(blocks, ranges):
    parts = []
    for lo, hi in ranges:
        for k in range(N_CHIPS):
            a0, a1 = max(lo, k * IN_SHARD), min(hi, (k + 1) * IN_SHARD)
            if a0 < a1:
                parts.append(blocks[k][:, a0 - k * IN_SHARD:a1 - k * IN_SHARD])
    return parts


def split_w_in(blocks):
    main = jnp.concatenate(_from_chip_blocks(blocks, MAIN_RANGES), axis=1)
    pad = jnp.zeros((blocks.shape[1], LANES - 16), blocks.dtype)
    return main, jnp.concatenate(_from_chip_blocks(blocks, SMALL_RANGES) + [pad], axis=1)


def chip_blocks_w_in(main, small):
    ranges = sorted([(lo, hi, "m") for lo, hi in MAIN_RANGES] + [(lo, hi, "s") for lo, hi in SMALL_RANGES])
    offs, m_off, s_off = {}, 0, 0
    for lo, hi in MAIN_RANGES:
        offs[lo] = m_off
        m_off += hi - lo
    for lo, hi in SMALL_RANGES:
        offs[lo] = s_off
        s_off += hi - lo
    blocks = []
    for k in range(N_CHIPS):
        parts = []
        for lo, hi, src in ranges:
            a0, a1 = max(lo, k * IN_SHARD), min(hi, (k + 1) * IN_SHARD)
            if a0 < a1:
                arr = main if src == "m" else small
                parts.append(arr[offs[lo] + a0 - lo:offs[lo] + a1 - lo])
        blocks.append(jnp.concatenate(parts, axis=0))
    return jnp.stack(blocks)


def later_weights(got):
    g_branch, g_o, g_up, g_down, g_pg, g_ple = got
    branch = g_branch.reshape(N_CHIPS, 3, BRANCH, -1)
    return dict(branch=[branch[:, b] for b in range(3)], o=g_o.reshape(D_MODEL, D_MODEL), up=g_up,
                down=g_down.reshape(D_FF, D_MODEL), pg=g_pg.reshape(D_MODEL, D_MODEL), ple=g_ple)


def layer_weights(li, got, conv, a):
    main, small = split_w_in(got[0])
    tile2 = lambda v: jnp.concatenate([v, v])[None, :]
    rest = later_weights(got[1:]) if len(got) > 1 else {}
    return dict(
        in_main=main, in_small=small, **rest,
        g_mix=a["g_mix"][li][None, :], g_ffn=a["g_ffn"][li][None, :], g_ple=a["g_ple"][li][None, :],
        gq=tile2(a["fox_q_gain"][li]), gk=tile2(a["fox_k_gain"][li]), b_f=a["b_fox_f"][li].reshape(8, 1, 1),
        ad=jnp.stack([a["dn_a_log"][li], a["dn_dt_bias"][li]]), dn_gain=a["dn_norm_gain"][li][None, :],
        sc_conv_w=conv["sc_conv_w"][li], dn_conv_w=conv["dn_conv_w"][li], ffn_conv_w=conv["ffn_conv_w"][li])


def pack_rows(arrs, dtype):
    flat = jnp.concatenate([t.reshape(-1).astype(dtype) for t in arrs])
    pad = (-flat.shape[0]) % (8 * LANES)
    if pad:
        flat = jnp.concatenate([flat, jnp.zeros((pad,), dtype)])
    return flat.reshape(-1, LANES)


def unpack_rows(buf, shapes):
    flat = buf.reshape(-1)
    out, off = [], 0
    for shp in shapes:
        size = 1
        for dim in shp:
            size *= dim
        out.append(flat[off:off + size].reshape(shp))
        off += size
    return out


def chip_shard(t, axis, k):
    width = t.shape[axis] // N_CHIPS
    return lax.slice_in_dim(t, k * width, (k + 1) * width, axis=axis)


ANY = pl.BlockSpec(memory_space=pl.ANY)


def _position():
    x, y, c = lax.axis_index("x"), lax.axis_index("y"), lax.axis_index("c")
    return x, y, c, [(1 - x, y), (x, 1 - y), (1 - x, 1 - y)]


def gather_small(name, block):
    m_per, n = block.shape

    def body(x_ref, out_ref, token, send_sems, recv_sems, local_sem):
        token[...] = jnp.zeros_like(token)
        x, y, c, chips = _position()
        me, sibling = (x, y, c), (x, y, 1 - c)

        def rows(px, py, pc):
            return out_ref.at[pl.ds((4 * px + 2 * py + pc) * m_per, m_per), :]

        def copy(k, blk, to, src=None):
            return pltpu.make_async_remote_copy(src_ref=rows(*blk) if src is None else src, dst_ref=rows(*blk),
                                                send_sem=send_sems.at[k], recv_sem=recv_sems.at[k], device_id=to, device_id_type=MESH)

        mine = pltpu.make_async_copy(x_ref, rows(*me), local_sem)
        mine.start()
        first = [copy(0, me, sibling, src=x_ref)] + [copy(1 + j, me, (*chip, c), src=x_ref) for j, chip in enumerate(chips)]
        for cp in first:
            cp.start()
        passed = [copy(4 + j, (*chip, c), sibling) for j, chip in enumerate(chips)]
        for j, chip in enumerate(chips):
            copy(1 + j, (*chip, c), me).wait_recv()
            passed[j].start()
        copy(0, sibling, me).wait_recv()
        for j, chip in enumerate(chips):
            copy(4 + j, (*chip, 1 - c), me).wait_recv()
        for cp in first + passed:
            cp.wait_send()
        mine.wait()

    in_vmem = pl.BlockSpec(memory_space=pltpu.VMEM)
    return pl.pallas_call(
        body, out_shape=[jax.ShapeDtypeStruct((8 * m_per, n), block.dtype), jax.ShapeDtypeStruct((8, LANES), F32)],
        in_specs=[in_vmem], out_specs=[in_vmem, in_vmem],
        scratch_shapes=[pltpu.SemaphoreType.DMA((7,)), pltpu.SemaphoreType.DMA((7,)), pltpu.SemaphoreType.DMA],
        name=name, compiler_params=pltpu.CompilerParams(vmem_limit_bytes=VMEM_LIMIT),
    )(block)


def _sems(n):
    return [pltpu.SemaphoreType.DMA((n,)), pltpu.SemaphoreType.DMA((n,))]


def _split_cols(rows):
    return (rows // 2) % 16 != 0


def _half(ref, which, lead=()):
    rows, cols = ref.shape[-2:]
    if _split_cols(rows):
        return ref.at[(*lead, slice(None), pl.ds(which * (cols // 2), cols // 2))]
    return ref.at[(*lead, pl.ds(which * (rows // 2), rows // 2), slice(None))]


def _half_shape(rows, cols):
    return (rows, cols // 2) if _split_cols(rows) else (rows // 2, cols)


def gather_layer(name, shards):
    n_w = len(shards)

    def body(*refs):
        ins, outs = refs[:n_w], refs[n_w:2 * n_w]
        token, send_sems, recv_sems = refs[2 * n_w:]
        token[...] = jnp.zeros_like(token)
        x, y, c, chips = _position()
        sibling = (x, y, 1 - c)

        def part(w, px, py, pc):
            return _half(outs[w], pc, (2 * px + py,))

        def copy(k, w, blk, to, src=None):
            return pltpu.make_async_remote_copy(src_ref=part(w, *blk) if src is None else src, dst_ref=part(w, *blk),
                                                send_sem=send_sems.at[k], recv_sem=recv_sems.at[k], device_id=to, device_id_type=MESH)

        pairs = [(w, j, chip) for w in range(n_w) for j, chip in enumerate(chips)]
        first = [copy(3 * w + j, w, (x, y, c), (*chip, c), src=_half(ins[w], c)) for w, j, chip in pairs]
        for cp in first:
            cp.start()
        passed = [copy(3 * n_w + 3 * w + j, w, (*chip, c), sibling) for w, j, chip in pairs]
        for (w, j, chip), fwd in zip(pairs, passed):
            copy(3 * w + j, w, (*chip, c), (x, y, c)).wait_recv()
            fwd.start()
        for w, j, chip in pairs:
            copy(3 * n_w + 3 * w + j, w, (*chip, 1 - c), (x, y, c)).wait_recv()
        for cp in first + passed:
            cp.wait_send()

    out = pl.pallas_call(
        body, out_shape=[jax.ShapeDtypeStruct((N_CHIPS,) + s.shape, s.dtype) for s in shards] + [jax.ShapeDtypeStruct((8, LANES), F32)],
        in_specs=[ANY] * n_w, out_specs=[ANY] * n_w + [pl.BlockSpec(memory_space=pltpu.VMEM)], scratch_shapes=_sems(6 * n_w), name=name,
    )(*shards)
    return out[:n_w], out[n_w]


def swap_halves(name, grads):
    n_w = len(grads)

    def body(*refs):
        ins, outs = refs[:n_w], refs[n_w:2 * n_w]
        send_sems, recv_sems = refs[2 * n_w:]
        x, y, c, _ = _position()
        cps = [pltpu.make_async_remote_copy(src_ref=_half(ins[w], 1 - c, (slice(None),)), dst_ref=outs[w],
                                            send_sem=send_sems.at[w], recv_sem=recv_sems.at[w], device_id=(x, y, 1 - c),
                                            device_id_type=MESH) for w in range(n_w)]
        for cp in cps:
            cp.start()
        for cp in cps:
            cp.wait()

    return pl.pallas_call(
        body, out_shape=[jax.ShapeDtypeStruct((N_CHIPS,) + _half_shape(*g.shape[1:]), g.dtype) for g in grads],
        in_specs=[ANY] * n_w, out_specs=[ANY] * n_w, scratch_shapes=_sems(n_w), name=name,
    )(*grads)


def scatter_chips(name, partials):
    n_w = len(partials)

    def body(*refs):
        ins, outs = refs[:n_w], refs[n_w:2 * n_w]
        send_sems, recv_sems = refs[2 * n_w:]
        x, y, c, chips = _position()
        cps = [pltpu.make_async_remote_copy(src_ref=ins[w].at[2 * cx + cy], dst_ref=outs[w].at[j], send_sem=send_sems.at[3 * w + j],
                                            recv_sem=recv_sems.at[3 * w + j], device_id=(cx, cy, c), device_id_type=MESH)
               for w in range(n_w) for j, (cx, cy) in enumerate(chips)]
        for cp in cps:
            cp.start()
        for cp in cps:
            cp.wait()

    return pl.pallas_call(
        body, out_shape=[jax.ShapeDtypeStruct((3,) + p.shape[1:], p.dtype) for p in partials],
        in_specs=[ANY] * n_w, out_specs=[ANY] * n_w, scratch_shapes=_sems(3 * n_w), name=name,
    )(*partials)


def share_halves(name, bufs):
    n_w = len(bufs)

    def body(*refs):
        outs = refs[n_w:2 * n_w]
        send_sems, recv_sems = refs[2 * n_w:]
        x, y, c, _ = _position()

        def copy(w, pc):
            half = _half(outs[w], pc)
            return pltpu.make_async_remote_copy(src_ref=half, dst_ref=half, send_sem=send_sems.at[w], recv_sem=recv_sems.at[w],
                                                device_id=(x, y, 1 - c), device_id_type=MESH)

        for w in range(n_w):
            copy(w, c).start()
        for w in range(n_w):
            copy(w, 1 - c).wait_recv()
            copy(w, c).wait_send()

    return pl.pallas_call(
        body, out_shape=[jax.ShapeDtypeStruct(b.shape, b.dtype) for b in bufs], in_specs=[ANY] * n_w, out_specs=[ANY] * n_w,
        input_output_aliases={w: w for w in range(n_w)}, scratch_shapes=_sems(n_w), name=name,
    )(*bufs)


HBM = pl.BlockSpec(memory_space=pltpu.HBM)
SEM = pl.BlockSpec(memory_space=pltpu.SEMAPHORE)
EFFECT = pltpu.SideEffectType.DATAFLOW_SIDE_EFFECTING


def _exchange_copies(kind, srcs, lands):
    x, y, c, chips = _position()
    out = []
    for src, land in zip(srcs, lands):
        if kind == "swap":
            out.append((_half(src, 1 - c, (slice(None),)), land, (x, y, 1 - c)))
            continue
        for j, (cx, cy) in enumerate(chips):
            if kind == "gather":
                out.append((src, land.at[2 * x + y], (cx, cy, c)))
            else:
                out.append((src.at[2 * cx + cy], land.at[j], (cx, cy, c)))
    return out


def _land_shapes(kind, srcs):
    if kind == "gather":
        return [(N_CHIPS,) + s.shape for s in srcs]
    if kind == "swap":
        return [(N_CHIPS,) + _half_shape(*s.shape[1:]) for s in srcs]
    return [(3,) + s.shape[1:] for s in srcs]


def exchange_start(name, kind, srcs):
    n_w = len(srcs)
    shapes = _land_shapes(kind, srcs)
    n_sem = n_w if kind == "swap" else 3 * n_w

    def body(*refs):
        ins, lands = refs[:n_w], refs[n_w:2 * n_w]
        send_sems, recv_sems = refs[2 * n_w:2 * n_w + 2]
        token = refs[-1]
        for i, (src, dst, dev) in enumerate(_exchange_copies(kind, ins, lands)):
            pltpu.make_async_remote_copy(src_ref=src, dst_ref=dst, send_sem=send_sems.at[i], recv_sem=recv_sems.at[i],
                                         device_id=dev, device_id_type=MESH).start()
        token[...] = jnp.zeros_like(token)

    out = pl.pallas_call(
        body, name=name,
        out_shape=(pltpu.SemaphoreType.DMA((n_sem,)), pltpu.SemaphoreType.DMA((n_sem,)),
                   *[pltpu.HBM(s.shape, s.dtype) for s in srcs], *[pltpu.HBM(shp, s.dtype) for shp, s in zip(shapes, srcs)],
                   jax.ShapeDtypeStruct((8, LANES), F32)),
        in_specs=(HBM,) * (2 * n_w), out_specs=(SEM, SEM) + (HBM,) * (2 * n_w) + (pl.BlockSpec(memory_space=pltpu.VMEM),),
        input_output_aliases={i: 2 + i for i in range(2 * n_w)},
        compiler_params=pltpu.CompilerParams(has_side_effects=EFFECT),
    )(*[pltpu.with_memory_space_constraint(s, pltpu.HBM) for s in srcs],
      *[pltpu.with_memory_space_constraint(lax.empty(shp, s.dtype), pltpu.HBM) for shp, s in zip(shapes, srcs)])
    return (kind, n_w, out[:-1]), out[-1]


def exchange_wait(name, handle, after):
    kind, n_w, (send_sems, recv_sems, *thru) = handle
    n_sem = n_w if kind == "swap" else 3 * n_w

    def body(*refs):
        ins, lands = refs[:n_w], refs[n_w:2 * n_w]
        send_sems, recv_sems = refs[2 * n_w:2 * n_w + 2]
        for i, (src, dst, dev) in enumerate(_exchange_copies(kind, ins, lands)):
            cp = pltpu.make_async_remote_copy(src_ref=src, dst_ref=dst, send_sem=send_sems.at[i], recv_sem=recv_sems.at[i],
                                              device_id=dev, device_id_type=MESH)
            cp.wait_send()
            cp.wait_recv()

    out = pl.pallas_call(
        body, name=name, out_shape=tuple(pltpu.HBM(t.shape, t.dtype) for t in thru),
        in_specs=(HBM,) * (2 * n_w) + (SEM, SEM, pl.BlockSpec(memory_space=pl.ANY)), out_specs=(HBM,) * (2 * n_w),
        input_output_aliases={i: i for i in range(2 * n_w)},
        compiler_params=pltpu.CompilerParams(has_side_effects=EFFECT),
    )(*thru, send_sems, recv_sems, after)
    return list(out[:n_w]), list(out[n_w:])


def _row_tile(rows, cols):
    best = rows
    if rows * cols * 4 <= 1024 * 1024:
        return rows
    for t in range(16, rows, 16):
        if rows % t == 0 and t * cols * 4 <= 1024 * 1024:
            best = t
    return best


def pair_sum(name, pos, grad, from_sibling):
    _, rows, cols = grad.shape
    h_rows, h_cols = _half_shape(rows, cols)
    tr = _row_tile(h_rows, h_cols)
    n_t = h_rows // tr

    def body(pos_ref, g_ref, s_ref, b_ref, f_ref):
        tot = g_ref[...] + s_ref[...]
        b_ref[...] = tot.astype(BF16)

        @pl.when(pl.program_id(1) == pos_ref[1])
        def _():
            f_ref[...] = tot[0]

    blk = pl.BlockSpec((1, tr, h_cols), lambda i, k, pos: (k, i, 0))
    if _split_cols(rows):
        mine = pl.BlockSpec((1, tr, h_cols), lambda i, k, pos: (k, i, pos[0]))
    else:
        mine = pl.BlockSpec((1, tr, h_cols), lambda i, k, pos: (k, pos[0] * n_t + i, 0))
    return pl.pallas_call(
        body, grid_spec=pltpu.PrefetchScalarGridSpec(
            num_scalar_prefetch=1, grid=(n_t, N_CHIPS), in_specs=[mine, blk],
            out_specs=[blk, pl.BlockSpec((tr, h_cols), lambda i, k, pos: (i, 0))]),
        out_shape=[jax.ShapeDtypeStruct((N_CHIPS, h_rows, h_cols), BF16), jax.ShapeDtypeStruct((h_rows, h_cols), F32)],
        name=name, compiler_params=_cparams(2),
    )(pos, grad, from_sibling)


def chip_sum(name, pos, own, landed, split_cols):
    half, cols = own.shape
    tr = _row_tile(half, cols)
    n_t = half // tr

    def body(pos_ref, p_ref, l_ref, o_ref):
        o_ref[...] = ((p_ref[...] + l_ref[0].astype(F32)) + l_ref[1].astype(F32)) + l_ref[2].astype(F32)

    if split_cols:
        out_spec, out_shape = pl.BlockSpec((tr, cols), lambda i, pos: (i, pos[0])), (half, 2 * cols)
    else:
        out_spec, out_shape = pl.BlockSpec((tr, cols), lambda i, pos: (pos[0] * n_t + i, 0)), (2 * half, cols)
    return pl.pallas_call(
        body, grid_spec=pltpu.PrefetchScalarGridSpec(
            num_scalar_prefetch=1, grid=(n_t,),
            in_specs=[pl.BlockSpec((tr, cols), lambda i, pos: (i, 0)), pl.BlockSpec((3, tr, cols), lambda i, pos: (0, i, 0))],
            out_specs=out_spec),
        out_shape=jax.ShapeDtypeStruct(out_shape, F32), name=name, compiler_params=_cparams(1),
    )(pos, own, landed)


def reduce_scatter_layer(tag, pos, grads):
    n = lambda t: f"{t}_{tag}"
    from_sibling = swap_halves(n("swap_halves"), grads)
    sums = [pair_sum(n(f"pair_sum{w}"), pos, g, s) for w, (g, s) in enumerate(zip(grads, from_sibling))]
    landed = scatter_chips(n("scatter_chips"), [b for b, _ in sums])
    halves = [chip_sum(n(f"chip_sum{w}"), pos, own, l, _split_cols(g.shape[1])) for w, ((_, own), l, g) in enumerate(zip(sums, landed, grads))]
    return share_halves(n("share_halves"), halves)


class OverlappedReduceScatter:
    def __init__(self, tag, pos, grads):
        self.n = lambda t: f"{t}_{tag}"
        self.pos, self.grads = pos, grads
        self.swap, self.token = exchange_start(self.n("swap_start"), "swap", grads)

    def middle(self, after):
        self.grads, from_sibling = exchange_wait(self.n("swap_wait"), self.swap, after)
        self.sums = [pair_sum(self.n(f"pair_sum{w}"), self.pos, g, s) for w, (g, s) in enumerate(zip(self.grads, from_sibling))]
        self.scatter, self.token = exchange_start(self.n("scatter_start"), "scatter", [b for b, _ in self.sums])

    def finish(self, after):
        _, landed = exchange_wait(self.n("scatter_wait"), self.scatter, after)
        halves = [chip_sum(self.n(f"chip_sum{w}"), self.pos, own, l, _split_cols(g.shape[1]))
                  for w, ((_, own), l, g) in enumerate(zip(self.sums, landed, self.grads))]
        return share_halves(self.n("share_halves"), halves)


def sum_devices(gathered):
    m_per = gathered.shape[0] // 8

    def body(g_ref, o_ref):
        tot = g_ref[pl.ds(0, m_per), :]
        for dev in range(1, 8):
            tot = tot + g_ref[pl.ds(dev * m_per, m_per), :]
        o_ref[...] = tot

    return pl.pallas_call(
        body, out_shape=jax.ShapeDtypeStruct((m_per, gathered.shape[1]), F32),
        in_specs=[pl.BlockSpec(memory_space=pltpu.VMEM)], out_specs=pl.BlockSpec(memory_space=pltpu.VMEM), name="sum_devices",
    )(gathered)


def kernel(x, p, g_mix, w_in, b_fox_f, fox_q_gain, fox_k_gain, sc_conv_w, dn_conv_w, dn_a_log, dn_dt_bias, dn_norm_gain, w_branch, w_o, g_ffn, w_up, ffn_conv_w, w_down, g_ple, w_ple_gate, w_ple, loss_target, m_g_mix, m_w_in, m_b_fox_f, m_fox_q_gain, m_fox_k_gain, m_sc_conv_w, m_dn_conv_w, m_dn_a_log, m_dn_dt_bias, m_dn_norm_gain, m_w_branch, m_w_o, m_g_ffn, m_w_up, m_ffn_conv_w, m_w_down, m_g_ple, m_w_ple_gate, m_w_ple, v_g_mix, v_w_in, v_b_fox_f, v_fox_q_gain, v_fox_k_gain, v_sc_conv_w, v_dn_conv_w, v_dn_a_log, v_dn_dt_bias, v_dn_norm_gain, v_w_branch, v_w_o, v_g_ffn, v_w_up, v_ffn_conv_w, v_w_down, v_g_ple, v_w_ple_gate, v_w_ple):
    a = dict(g_mix=g_mix, w_in=w_in, b_fox_f=b_fox_f, fox_q_gain=fox_q_gain, fox_k_gain=fox_k_gain, sc_conv_w=sc_conv_w,
             dn_conv_w=dn_conv_w, dn_a_log=dn_a_log, dn_dt_bias=dn_dt_bias, dn_norm_gain=dn_norm_gain, w_branch=w_branch, w_o=w_o,
             g_ffn=g_ffn, w_up=w_up, ffn_conv_w=ffn_conv_w, w_down=w_down, g_ple=g_ple, w_ple_gate=w_ple_gate, w_ple=w_ple)
    mom = dict(g_mix=m_g_mix, w_in=m_w_in, b_fox_f=m_b_fox_f, fox_q_gain=m_fox_q_gain, fox_k_gain=m_fox_k_gain, sc_conv_w=m_sc_conv_w,
               dn_conv_w=m_dn_conv_w, dn_a_log=m_dn_a_log, dn_dt_bias=m_dn_dt_bias, dn_norm_gain=m_dn_norm_gain, w_branch=m_w_branch,
               w_o=m_w_o, g_ffn=m_g_ffn, w_up=m_w_up, ffn_conv_w=m_ffn_conv_w, w_down=m_w_down, g_ple=m_g_ple, w_ple_gate=m_w_ple_gate,
               w_ple=m_w_ple)
    var = dict(g_mix=v_g_mix, w_in=v_w_in, b_fox_f=v_b_fox_f, fox_q_gain=v_fox_q_gain, fox_k_gain=v_fox_k_gain, sc_conv_w=v_sc_conv_w,
               dn_conv_w=v_dn_conv_w, dn_a_log=v_dn_a_log, dn_dt_bias=v_dn_dt_bias, dn_norm_gain=v_dn_norm_gain, w_branch=v_w_branch,
               w_o=v_w_o, g_ffn=v_g_ffn, w_up=v_w_up, ffn_conv_w=v_ffn_conv_w, w_down=v_w_down, g_ple=v_g_ple, w_ple_gate=v_w_ple_gate,
               w_ple=v_w_ple)
    cx, cy, cc = lax.axis_index("x"), lax.axis_index("y"), lax.axis_index("c")
    chip = 2 * cx + cy
    pos = jnp.stack([cc, chip]).astype(jnp.int32)

    def as_blocks(t):
        return t.reshape(2, -1, t.shape[-1])

    def own_block_in(got, shards):
        return [lax.dynamic_update_slice(g, s[None], (chip, 0, 0)) for g, s in zip(got, shards)]

    conv_shapes = [a[nm].shape for nm in CONVS]
    conv_all, conv_token = gather_small("gather_conv_w", pack_rows([a[nm] for nm in CONVS], F32))
    def layer_block(nm, t, li):
        return as_blocks(t)[li]

    shards0 = [(layer_block(nm, a[nm], 0) + conv_token[0, 0]).astype(BF16) for nm in BIG]
    got0, gathered_token = gather_layer("gather_w_in_l0", shards0[:1])
    shards0[1:] = [s + gathered_token[0, 0].astype(BF16) for s in shards0[1:]]
    gather0, gather0_token = exchange_start("gather_start_l0", "gather", shards0[1:])
    shards1 = [(layer_block(nm, a[nm], 1) + gather0_token[0, 0]).astype(BF16) for nm in BIG]
    gather1, gather1_in_token = exchange_start("gather_start_w_in_l1", "gather", shards1[:1])
    shards1[1:] = [s + gather1_in_token[0, 0].astype(BF16) for s in shards1[1:]]
    gather1_rest, gather1_token = exchange_start("gather_start_l1", "gather", shards1[1:])
    conv_rows = conv_all.shape[0] // 8
    conv_chip = [unpack_rows(conv_all[2 * k * conv_rows:(2 * k + 1) * conv_rows], conv_shapes) for k in range(N_CHIPS)]
    conv = {nm: jnp.concatenate([conv_chip[k][i] for k in range(N_CHIPS)], axis=2) for i, nm in enumerate(CONVS)}

    weights, saved = [None, None], [None, None]
    first_weights = hang_on(layer_weights(0, own_block_in(got0, shards0[:1]), conv, a), gather1_token)

    def rest_of_layer0(after):
        mine, got = exchange_wait("gather_wait_l0", gather0, after)
        return later_weights(own_block_in(got, mine))

    act, saved[0], weights[0] = layer_fwd(0, x[0], p[0, 0], first_weights, more_weights=rest_of_layer0)
    mine1, got1 = exchange_wait("gather_wait_w_in_l1", gather1, act)

    def rest_of_layer1(after):
        mine, got = exchange_wait("gather_wait_l1", gather1_rest, after)
        return later_weights(own_block_in(got, mine))

    act, saved[1], weights[1] = layer_fwd(1, act, p[1, 0], layer_weights(1, own_block_in(got1, mine1), conv, a),
                                          more_weights=rest_of_layer1)
    d_act, loss_part = loss_call(act, loss_target[0])
    loss = lax.psum(loss_part, ("x", "y", "c"))
    layer_grads = [None, None]
    d_act, layer_grads[1] = layer_bwd(1, d_act, saved[1], weights[1])
    rs1 = OverlappedReduceScatter("l1", pos, [layer_grads[1][nm] for nm in BIG])
    rs0 = []

    def stage_mid(after, g):
        rs1.middle(after)
        return rs1.token

    def stage_late(after, g):
        rs0.append(OverlappedReduceScatter("l0", pos, [g[nm] for nm in BIG[1:]]))
        return rs0[0].token

    def stage_last(after, g):
        rs0[0].middle(after)
        return rs0[0].token

    def stage_w_in(after, g):
        rs0.append(OverlappedReduceScatter("w_in_l0", pos, [g["w_in"]]))
        return rs0[1].token

    d_act, layer_grads[0] = layer_bwd(0, d_act, saved[0], hang_on(weights[0], rs1.token),
                                      hooks=dict(mid=stage_mid, late=stage_late, last=stage_last, w_in=stage_w_in))
    rs0[1].middle(d_act)
    reduced = [rs0[0].finish(rs0[1].token), rs1.finish(rs0[1].token)]
    grad_x = d_act[None]

    def both(nm):
        return jnp.stack([layer_grads[0][nm], layer_grads[1][nm]])

    local = {nm: both(nm) for nm in ("g_mix", "b_fox_f", "fox_q_gain", "fox_k_gain", "dn_norm_gain", "g_ffn", "g_ple", "sc_conv_w",
                                      "dn_conv_w", "ffn_conv_w")}
    local["dn_a_log"] = jnp.stack([layer_grads[li]["ad"][0] for li in range(2)])
    local["dn_dt_bias"] = jnp.stack([layer_grads[li]["ad"][1] for li in range(2)])

    small_names = SMALL + CONVS
    small_shapes = [local[nm].shape for nm in small_names]
    small_sum = sum_devices(gather_small("gather_small_grads", pack_rows([local[nm] for nm in small_names], F32))[0])
    small_grads = dict(zip(small_names, unpack_rows(small_sum, small_shapes)))
    for nm in CONVS:
        width = a[nm].shape[2]
        small_grads[nm] = lax.dynamic_slice_in_dim(small_grads[nm], chip * width, width, axis=2)

    grads, deltas, new_m, new_v = dict(small_grads), {}, {}, {}
    for nm in small_names:
        deltas[nm], new_m[nm], new_v[nm] = adam_call(f"adam_{nm}", a[nm], grads[nm], mom[nm], var[nm])
    for i, nm in enumerate(BIG[1:]):
        res = adam_layers(f"adam_{nm}", as_blocks(a[nm]), as_blocks(mom[nm]), as_blocks(var[nm]), reduced[0][i], reduced[1][1 + i])
        grads[nm], deltas[nm], new_m[nm], new_v[nm] = [r.reshape(a[nm].shape) for r in res]
    stored = lambda t: jnp.transpose(t, (2, 0, 1))
    res = adam_w_in("adam_w_in", stored(a["w_in"]), stored(mom["w_in"]), stored(var["w_in"]), rs0[1].finish(deltas["w_ple"])[0], reduced[1][0])
    grads["w_in"], deltas["w_in"], new_m["w_in"], new_v["w_in"] = [jnp.transpose(r, (1, 2, 0)) for r in res]
    return (loss, grad_x, *[grads[nm] for nm in WEIGHTS], *[deltas[nm] for nm in WEIGHTS], *[new_m[nm] for nm in WEIGHTS],
            *[new_v[nm] for nm in WEIGHTS])
```

```python
import functools

import jax
import jax.numpy as jnp
from jax import lax
from jax.experimental import pallas as pl
from jax.experimental.pallas import tpu as pltpu

F32 = jnp.float32
BF16 = jnp.bfloat16
HI = lax.Precision.HIGHEST
SOLVE = lax.Precision.HIGH
MESH = pl.DeviceIdType.MESH

D_MODEL = 1024
BRANCH = 512
FOX_DH = 64
DN_DH = 128
DN_HEADS = 4
DN_CHUNK = 64
FOX_BLOCK = 128
D_FF = 2816
EPS = 1e-6
N_CHIPS = 4
LANES = 128

ADAM_LR, ADAM_B1, ADAM_B2, ADAM_EPS, ADAM_WD, ADAM_STEP = 0.001, 0.9, 0.999, 1e-08, 0.01, 10

VMEM_LIMIT = 56 * 1024 * 1024

C_FQ, C_FK, C_FV, C_SB, C_SC, C_SV, C_DN, C_DZ, C_GATE = 0, 512, 1024, 1536, 2048, 2560, 3072, 4608, 5120
IN_MAIN = 8192
IN_SIZES = (1536, 8, 1536, 1536, 4, 4, 512, 3072)

BIG = ("w_in", "w_branch", "w_o", "w_up", "w_down", "w_ple_gate", "w_ple")
BIG_AXIS = {"w_in": 2, "w_branch": 3, "w_o": 1, "w_up": 2, "w_down": 1, "w_ple_gate": 1, "w_ple": 2}
CONVS = ("sc_conv_w", "dn_conv_w", "ffn_conv_w")
SMALL = ("g_mix", "b_fox_f", "fox_q_gain", "fox_k_gain", "dn_a_log", "dn_dt_bias", "dn_norm_gain", "g_ffn", "g_ple")
WEIGHTS = ("g_mix", "w_in", "b_fox_f", "fox_q_gain", "fox_k_gain", "sc_conv_w", "dn_conv_w", "dn_a_log", "dn_dt_bias",
           "dn_norm_gain", "w_branch", "w_o", "g_ffn", "w_up", "ffn_conv_w", "w_down", "g_ple", "w_ple_gate", "w_ple")


def _iota(shape, dim):
    return lax.broadcasted_iota(jnp.int32, shape, dim)


def _dg(a, b, mode, prec=None):
    dims = {"nn": ((1,), (0,)), "nt": ((1,), (1,)), "tn": ((0,), (0,))}[mode]
    return lax.dot_general(a, b, (dims, ((), ())), precision=prec, preferred_element_type=F32)


def _bdot_impl(a, b, mode):
    return _dg(a.astype(BF16), b.astype(BF16), mode)


@functools.partial(jax.custom_vjp, nondiff_argnums=(2,))
def _bdot_diff(a, b, mode):
    return _bdot_impl(a, b, mode)


def _bdot_fwd(a, b, mode):
    return _bdot_impl(a, b, mode), (a, b)


def _bdot_bwd(mode, res, g):
    a, b = res
    if mode == "nn":
        da, db = _bdot_impl(g, b, "nt"), _bdot_impl(a, g, "tn")
    elif mode == "nt":
        da, db = _bdot_impl(g, b, "nn"), _bdot_impl(g, a, "tn")
    else:
        da, db = _bdot_impl(b, g, "nt"), _bdot_impl(a, g, "nn")
    return da.astype(a.dtype), db.astype(b.dtype)


_bdot_diff.defvjp(_bdot_fwd, _bdot_bwd)


def _bdot(d):
    return _bdot_diff if d else _bdot_impl


def _shift_impl(x, k):
    return jnp.where(_iota(x.shape, 0) >= k, pltpu.roll(x, k, 0), 0.0)


def _unshift_impl(g, k):
    n = g.shape[0]
    return jnp.where(_iota(g.shape, 0) < n - k, pltpu.roll(g, n - k, 0), 0.0)


@functools.partial(jax.custom_vjp, nondiff_argnums=(1,))
def _shift_diff(x, k):
    return _shift_impl(x, k)


_shift_diff.defvjp(lambda x, k: (_shift_impl(x, k), None), lambda k, _, g: (_unshift_impl(g, k),))


def _row(w, j):
    return jnp.sum(jnp.where(_iota(w.shape, 0) == j, w, 0.0), axis=0, keepdims=True)


def _col(w, j):
    return jnp.sum(jnp.where(_iota(w.shape, 1) == j, w, 0.0), axis=1, keepdims=True)


def _conv(d, x, w):
    shift = _shift_diff if d else _shift_impl
    taps = w.shape[0]
    y = x * _row(w, taps - 1)
    for j in range(taps - 1):
        y = y + shift(x, taps - 1 - j) * _row(w, j)
    return y


def _softplus(x):
    return jnp.maximum(x, 0.0) + jnp.log(1.0 + jnp.exp(-jnp.abs(x)))


def _silu(x):
    return x * jax.nn.sigmoid(x)


def _rms(x, gain):
    return x * lax.rsqrt(jnp.mean(x * x, axis=-1, keepdims=True) + EPS) * gain


def _rms_fn(d, pids, x, gain):
    return (_rms(x, gain),)


def _loss_fn(d, pids, y, t):
    e = y - t
    part = 0.5 / D_MODEL * jnp.sum(e * e, keepdims=True)
    return e * (1.0 / D_MODEL), jnp.broadcast_to(part, (8, LANES))


def _fox_prep_fn(d, pids, q, k, gq, gk):
    first = _iota(q.shape, 1) < FOX_DH

    def norm(x, gain):
        sq = x * x
        ss_a = jnp.sum(jnp.where(first, sq, 0.0), axis=1, keepdims=True)
        ss_b = jnp.sum(jnp.where(first, 0.0, sq), axis=1, keepdims=True)
        rs = jnp.where(first, lax.rsqrt(ss_a / FOX_DH + EPS), lax.rsqrt(ss_b / FOX_DH + EPS))
        return x * rs * gain

    return norm(q, gq) * FOX_DH ** -0.5, norm(k, gk)


def _fox_gate_fn(d, pids, f, bias):
    logf = -_softplus(-(f + bias))
    n_r, n_c = logf.shape
    tri = (_iota((n_c, n_c), 0) <= _iota((n_c, n_c), 1)).astype(F32)
    within = _dg(logf, tri, "nn", HI)
    tot = jnp.broadcast_to(jnp.sum(logf, axis=1, keepdims=True), logf.shape)
    below = (_iota((n_r, n_r), 1) < _iota((n_r, n_r), 0)).astype(F32)
    return (within + _dg(below, tot, "nn", HI),)


def _fox_attn_fn(q_block0, d, pids, q, k, v, cq_a, cq_b, ck_a, ck_b):
    dot = _bdot(d)
    first = _iota(q.shape, 1) < FOX_DH
    n_q, n_k = q.shape[0], k.shape[0]
    causal = ((q_block0 + pids[1]) * n_q + _iota((n_q, n_k), 0)) >= _iota((n_q, n_k), 1)

    qs = [jnp.where(first, q, 0.0), jnp.where(first, 0.0, q)]
    s = _each(lambda qh, cq, ck: jnp.where(causal, dot(qh, k, "nt") + cq - ck, -1e30), qs, [cq_a, cq_b], [ck_a, ck_b])
    e = [jnp.exp(si - lax.stop_gradient(jnp.max(si, axis=1, keepdims=True))) for si in s]
    o_a, o_b = [dot(ei / jnp.sum(ei, axis=1, keepdims=True), v, "nn") for ei in e]
    return (jnp.where(first, o_a, o_b),)


def _sconv_fn(d, pids, sb, sc, sv, w):
    return (sb * _conv(d, sc * sv, w),)


def _dnconv_fn(d, pids, x, w):
    return (_silu(_conv(d, x, w)),)


def _merge_fn(d, pids, y0, y1, y2, g0, g1, g2):
    return (jax.nn.sigmoid(g0) * y0 + jax.nn.sigmoid(g1) * y1 + jax.nn.sigmoid(g2) * y2,)


def _ffn_act_fn(d, pids, ug, uv, wg, wv):
    return (_silu(_conv(d, ug, wg)) * _conv(d, uv, wv),)


def _ple_fn(d, pids, gpre, pe, x):
    return (x + jax.nn.sigmoid(gpre) * pe,)


def _adam_fn(d, pids, w, g, m, v):
    m2 = ADAM_B1 * m + (1.0 - ADAM_B1) * g
    v2 = ADAM_B2 * v + (1.0 - ADAM_B2) * (g * g)
    m_hat = m2 / (1.0 - ADAM_B1 ** ADAM_STEP)
    v_hat = v2 / (1.0 - ADAM_B2 ** ADAM_STEP)
    delta = -ADAM_LR * (m_hat / (jnp.sqrt(v_hat) + ADAM_EPS) + ADAM_WD * w)
    return delta, m2, v2


def _each(fn, *lists):
    return [fn(*args) for args in zip(*lists)]


def _tri_inv_impl(mats):
    n = mats[0].shape[0]
    r, c = _iota((n, n), 0), _iota((n, n), 1)
    diag_blk = (r >> 4) == (c >> 4)
    eye = (r == c).astype(F32)
    mm = lambda us, ws: _each(lambda u, w: _dg(u, w, "nn", SOLVE), us, ws)
    grow = lambda ps, xs: _each(lambda p, px: p + px, ps, mm(ps, xs))
    x = [jnp.where(diag_blk, -a, 0.0) for a in mats]
    p = [eye + xi for xi in x]
    x2 = mm(x, x)
    p = grow(p, x2)
    x4 = mm(x2, x2)
    p = grow(p, x4)
    p = grow(p, mm(x4, x4))
    y = [-yi for yi in mm(p, [jnp.where(diag_blk, 0.0, a) for a in mats])]
    q = grow([eye + yi for yi in y], mm(y, y))
    return mm(q, p)


@jax.custom_vjp
def _tri_inv_diff(mats):
    return _tri_inv_impl(mats)


def _tri_inv_fwd(mats):
    ts = _tri_inv_impl(mats)
    return ts, ts


def _tri_inv_bwd(ts, gs):
    left = _each(lambda t, g: _dg(t, g, "tn", SOLVE), ts, gs)
    return ([-m for m in _each(lambda l, t: _dg(l, t, "nt", SOLVE), left, ts)],)


_tri_inv_diff.defvjp(_tri_inv_fwd, _tri_inv_bwd)


def _dn_local(d, qs, ks, vs, a_cs, a_rs, b_cs, a_logs, dt_bs):
    dot = _bdot(d)
    inv = _tri_inv_diff if d else _tri_inv_impl
    n = qs[0].shape[0]
    r, c = _iota((n, n), 0), _iota((n, n), 1)
    incl, strict, upper = r >= c, r > c, r <= c
    qs = [q * lax.rsqrt(jnp.sum(q * q, axis=1, keepdims=True) + EPS) * DN_DH ** -0.5 for q in qs]
    ks = [k * lax.rsqrt(jnp.sum(k * k, axis=1, keepdims=True) + EPS) for k in ks]
    betas = [jax.nn.sigmoid(b) for b in b_cs]
    rates = [-jnp.exp(a) for a in a_logs]
    g_cs = _each(lambda rate, a, dt: rate * _softplus(a + dt), rates, a_cs, dt_bs)
    g_rs = _each(lambda rate, a, dt: rate * _softplus(a + dt), rates, a_rs, dt_bs)
    gcum_cs = [jnp.sum(jnp.where(incl, g, 0.0), axis=1, keepdims=True) for g in g_rs]
    gcum_rs = [jnp.sum(jnp.where(upper, g, 0.0), axis=0, keepdims=True) for g in g_cs]
    decays = _each(lambda gc, gr: jnp.exp(jnp.where(incl, gc - gr, -1e30)), gcum_cs, gcum_rs)
    kbs = _each(lambda k, b: k * b, ks, betas)
    kk = _each(lambda kb, k: dot(kb, k, "nt"), kbs, ks)
    ts = inv(_each(lambda m, dec: jnp.where(strict, m * dec, 0.0), kk, decays))
    e_gs = [jnp.exp(g) for g in gcum_cs]
    us = _each(lambda t, v, b: _dg(t, v * b, "nn", SOLVE), ts, vs, betas)
    k_cums = _each(lambda t, kb, e: _dg(t, kb * e, "nn", SOLVE), ts, kbs, e_gs)
    qk = _each(lambda q, k: dot(q, k, "nt"), qs, ks)
    qk = _each(lambda m, dec: jnp.where(incl, m * dec, 0.0), qk, decays)
    g_lasts = [jnp.sum(g, axis=0, keepdims=True) for g in g_cs]
    q_decs = _each(lambda q, e: q * e, qs, e_gs)
    k_decs = _each(lambda k, gl, gc: k * jnp.exp(gl - gc), ks, g_lasts, gcum_cs)
    return list(zip(us, k_cums, q_decs, k_decs, qk, g_lasts))


def _dn_step(d, s_prevs, items, zs, gain):
    dot = _bdot(d)
    us, k_cums, q_decs, k_decs, qks, g_lasts = [list(t) for t in zip(*items)]
    v_news = _each(lambda u, kc, s: u - dot(kc, s, "nn"), us, k_cums, s_prevs)
    inter = _each(lambda qd, s: dot(qd, s, "nn"), q_decs, s_prevs)
    outs = _each(lambda o, qk, vn: o + dot(qk, vn, "nn"), inter, qks, v_news)
    s_nexts = _each(lambda s, gl, kd, vn: s * jnp.exp(gl) + dot(kd, vn, "tn"), s_prevs, g_lasts, k_decs, v_news)
    return _each(lambda o, z: _rms(o, gain) * _silu(z), outs, zs), s_nexts


def _split_heads(t):
    return [t[:, h * DN_DH:(h + 1) * DN_DH] for h in range(t.shape[1] // DN_DH)]


def _dn_gates(ps, a_rows, ad):
    hs = range(DN_HEADS)
    return ([_col(ps, 12 + h) for h in hs], [_row(a_rows, h) for h in hs], [_col(ps, 8 + h) for h in hs],
            [_col(_row(ad, 0), h) for h in hs], [_col(_row(ad, 1), h) for h in hs])


def _head_rows(vals):
    row = _iota((8, LANES), 0)
    tile = jnp.zeros((8, LANES), F32)
    for h, val in enumerate(vals):
        tile = tile + jnp.where(row == h, val, 0.0)
    return tile


def _cparams(n_axes):
    return pltpu.CompilerParams(dimension_semantics=("arbitrary",) * n_axes, vmem_limit_bytes=VMEM_LIMIT)


def _first_visit(acc_axes):
    cond = None
    for a in acc_axes:
        here = pl.program_id(a) == 0
        cond = here if cond is None else jnp.logical_and(cond, here)
    return cond


def _tile(ref, widen=False):
    val = ref[...]
    shape = val.shape
    while len(shape) > 2 and shape[0] == 1:
        shape = shape[1:]
    val = val.reshape(shape)
    return val.astype(F32) if widen and val.dtype == BF16 else val


def _store(ref, val, first):
    val = val.astype(ref.dtype).reshape(ref.shape)
    if first is None:
        ref[...] = val
        return

    @pl.when(first)
    def _():
        ref[...] = val

    @pl.when(jnp.logical_not(first))
    def _():
        ref[...] += val


def _specs(ops):
    return [pl.BlockSpec(block, imap) for _, block, imap in ops]


def tile_fwd(name, fn, grid, ins, outs, raw=()):
    n_in = len(ins)

    def body(*refs):
        pids = tuple(pl.program_id(a) for a in range(len(grid)))
        firsts = [_first_visit(o[4]) if o[4] else None for o in outs]
        res = fn(False, pids, *[_tile(r, i not in raw) for i, r in enumerate(refs[:n_in])])
        for ref, val, first in zip(refs[n_in:], res, firsts):
            _store(ref, val, first)

    out = pl.pallas_call(
        body, grid=grid, in_specs=_specs(ins),
        out_specs=[pl.BlockSpec(o[2], o[3]) for o in outs],
        out_shape=[jax.ShapeDtypeStruct(o[0], o[1]) for o in outs],
        name=name, compiler_params=_cparams(len(grid)),
    )(*[a for a, _, _ in ins])
    return out


def tile_bwd(name, fn, grid, ins, cots, diff, adds=None, raw=()):
    adds = adds or {}
    n_in, n_cot = len(ins), len(cots)
    add_pos = sorted(adds)
    diff_idx = [d[0] for d in diff]
    out_desc = [d[2] if len(d) > 2 and d[2] is not None else (ins[d[0]][0].shape, ins[d[0]][1], ins[d[0]][2]) for d in diff]
    out_dtypes = [d[3] if len(d) > 3 else F32 for d in diff]

    def body(*refs):
        pids = tuple(pl.program_id(a) for a in range(len(grid)))
        firsts = [_first_visit(d[1]) if d[1] else None for d in diff]
        vals = [_tile(r, i not in raw) for i, r in enumerate(refs[:n_in])]
        cot_vals = [_tile(r, True) for r in refs[n_in:n_in + n_cot]]
        add_vals = [_tile(r) for r in refs[n_in + n_cot:n_in + n_cot + len(add_pos)]]
        out_refs = refs[n_in + n_cot + len(add_pos):]

        def f(*dv):
            full = list(vals)
            for i, val in zip(diff_idx, dv):
                full[i] = val
            return fn(True, pids, *full)

        prim, vjp = jax.vjp(f, *[vals[i].astype(F32) for i in diff_idx])
        grads = list(vjp(tuple(c.astype(o.dtype) for c, o in zip(cot_vals, prim))))
        for pos, val in zip(add_pos, add_vals):
            grads[pos] = grads[pos] + val.astype(F32)
        for ref, val, first in zip(out_refs, grads, firsts):
            _store(ref, val, first)

    all_ins = list(ins) + list(cots) + [adds[p] for p in add_pos]
    out = pl.pallas_call(
        body, grid=grid, in_specs=_specs(all_ins),
        out_specs=[pl.BlockSpec(o[1], o[2]) for o in out_desc],
        out_shape=[jax.ShapeDtypeStruct(o[0], dt) for o, dt in zip(out_desc, out_dtypes)],
        name=name, compiler_params=_cparams(len(grid)),
    )(*[a for a, _, _ in all_ins])
    return out


def _pick(dim, cands):
    for c in cands:
        if dim % c == 0:
            return c
    return dim


MM_TILES = (1024, 512, 1408, 256, 128)


def mm(name, a, b, mode, add=None, out_dtype=F32, blocks=None):
    wide = None
    if mode == "nn":
        (m, kk), n = a.shape, b.shape[-1]
    elif mode == "nt":
        (m, kk), n = a.shape, b.shape[-2]
    else:
        (kk, m), n = a.shape, b.shape[1]
    if blocks is not None:
        lo, n_blk = blocks
        wide = b.shape[-1] if mode != "tn" else n // n_blk
        if mode == "nn":
            n = wide * n_blk
    tm = _pick(m, MM_TILES)
    if mode == "nt" and blocks is not None:
        tn, tk = _pick(n, MM_TILES), _pick(wide, MM_TILES[:-1])
    elif blocks is not None:
        tn, tk = _pick(wide, MM_TILES[:-1]), _pick(kk, MM_TILES)
    else:
        tn, tk = _pick(n, MM_TILES), _pick(kk, MM_TILES)
    if mode == "tn":
        tk = _pick(kk, (2048,) + MM_TILES)
    nk = kk // tk
    a_spec = pl.BlockSpec((tk, tm), lambda i, j, k: (k, i)) if mode == "tn" else pl.BlockSpec((tm, tk), lambda i, j, k: (i, k))
    o_spec = pl.BlockSpec((tm, tn), lambda i, j, k: (i, j))
    out_shape = (m, n)
    if blocks is None:
        b_spec = pl.BlockSpec((tn, tk), lambda i, j, k: (j, k)) if mode == "nt" else pl.BlockSpec((tk, tn), lambda i, j, k: (k, j))
    elif mode == "nn":
        per = wide // tn
        b_spec = pl.BlockSpec((1, tk, tn), lambda i, j, k: (lo + j // per, k, j % per))
    elif mode == "nt":
        per = wide // tk
        b_spec = pl.BlockSpec((1, tn, tk), lambda i, j, k: (lo + k // per, j, k % per))
    else:
        per = wide // tn
        b_spec = pl.BlockSpec((tk, tn), lambda i, j, k: (k, j))
        o_spec = pl.BlockSpec((1, tm, tn), lambda i, j, k: (j // per, i, j % per))
        out_shape = (n_blk, m, wide)

    def body(*refs):
        a_ref, b_ref = refs[0], refs[1]
        add_ref = refs[2] if add is not None else None
        o_ref, acc = refs[-2], refs[-1]
        k = pl.program_id(2)
        part = _bdot_impl(_tile(a_ref), _tile(b_ref), mode)

        @pl.when(k == 0)
        def _():
            acc[...] = part

        @pl.when(k > 0)
        def _():
            acc[...] += part

        @pl.when(k == nk - 1)
        def _():
            res = acc[...]
            if add_ref is not None:
                res = res + add_ref[...]
            o_ref[...] = res.astype(o_ref.dtype).reshape(o_ref.shape)

    operands = [a, b] + ([add] if add is not None else [])
    in_specs = [a_spec, b_spec] + ([o_spec] if add is not None else [])
    return pl.pallas_call(
        body, grid=(m // tm, n // tn, nk), in_specs=in_specs, out_specs=o_spec,
        out_shape=jax.ShapeDtypeStruct(out_shape, out_dtype),
        scratch_shapes=[pltpu.VMEM((tm, tn), F32)],
        name=name, compiler_params=_cparams(3),
    )(*operands)


def _rows(x, width=None, off=0, tm=256):
    width = x.shape[1] if width is None else width
    return (x, (tm, width), lambda i, off=off: (i, off))


def _whole(x):
    nd = x.ndim
    return (x, x.shape, lambda *pids, nd=nd: (0,) * nd)


def _rms_ops(x, gain):
    return [_rows(x), _whole(gain)]


def rms_fwd(name, x, gain):
    s, dm = x.shape
    return tile_fwd(name, _rms_fn, (s // 256,), _rms_ops(x, gain), [((s, dm), BF16, (256, dm), lambda i: (i, 0), ())])[0]


def rms_bwd(name, x, gain, dh, dres):
    s = x.shape[0]
    return tile_bwd(name, _rms_fn, (s // 256,), _rms_ops(x, gain), [_rows(dh)], [(0, ()), (1, (0,))], adds={0: _rows(dres)})


def loss_call(y, t):
    s, dm = y.shape
    dy, part = tile_fwd("loss", _loss_fn, (s // 256,), [_rows(y), _rows(t)],
                        [((s, dm), F32, (256, dm), lambda i: (i, 0), ()), ((8, LANES), F32, (8, LANES), lambda i: (0, 0), (0,))])
    return dy, part[0, 0]


def _fox_prep_ops(pm, gq, gk):
    tm = 512
    return [(pm, (tm, LANES), lambda i, j: (i, C_FQ // LANES + j)), (pm, (tm, LANES), lambda i, j: (i, C_FK // LANES + j)),
            _whole(gq), _whole(gk)]


def fox_prep_fwd(name, pm, gq, gk):
    s = pm.shape[0]
    out = ((s, BRANCH), BF16, (512, LANES), lambda i, j: (i, j), ())
    return tile_fwd(name, _fox_prep_fn, (s // 512, 4), _fox_prep_ops(pm, gq, gk), [out, out])


def fox_prep_bwd(name, pm, gq, gk, dqn, dkn):
    s = pm.shape[0]
    cot = lambda g: (g, (512, LANES), lambda i, j: (i, j))
    own = ((s, BRANCH), (512, LANES), lambda i, j: (i, j))
    return tile_bwd(name, _fox_prep_fn, (s // 512, 4), _fox_prep_ops(pm, gq, gk), [cot(dqn), cot(dkn)],
                    [(0, (), own, BF16), (1, (), own, BF16), (2, (0, 1)), (3, (0, 1))])


def _fox_gate_ops(f_t, bias):
    return [(f_t, (1,) + f_t.shape[1:], lambda h: (h, 0, 0)), (bias, (1, 1, 1), lambda h: (h, 0, 0))]


def fox_gate_fwd(name, f_t, bias):
    n_h = f_t.shape[0]
    return tile_fwd(name, _fox_gate_fn, (n_h,), _fox_gate_ops(f_t, bias),
                    [(f_t.shape, F32, (1,) + f_t.shape[1:], lambda h: (h, 0, 0), ())])[0]


def fox_gate_bwd(name, f_t, bias, dcum):
    n_h = f_t.shape[0]
    return tile_bwd(name, _fox_gate_fn, (n_h,), _fox_gate_ops(f_t, bias),
                    [(dcum, (1,) + f_t.shape[1:], lambda h: (h, 0, 0))], [(0, ()), (1, ())])


FOX_GROUPS = 4


def _fox_groups(s):
    per = s // FOX_BLOCK // FOX_GROUPS
    return [(g * per, per, (g + 1) * per * FOX_BLOCK) for g in range(FOX_GROUPS)]


def _fox_attn_ops(qn, kn, pm, cum_c, cum_r, q0, keys):
    nb = FOX_BLOCK
    return [(qn, (nb, LANES), lambda p, i: (q0 + i, p)), (kn, (keys, LANES), lambda p, i: (0, p)),
            (pm, (keys, LANES), lambda p, i: (0, C_FV // LANES + p)),
            (cum_c, (1, nb, 1), lambda p, i: (2 * p, q0 + i, 0)), (cum_c, (1, nb, 1), lambda p, i: (2 * p + 1, q0 + i, 0)),
            (cum_r, (1, 1, keys), lambda p, i: (2 * p, 0, 0)), (cum_r, (1, 1, keys), lambda p, i: (2 * p + 1, 0, 0))]


def fox_attn_fwd(name, qn, kn, pm, cum_c, cum_r):
    s = qn.shape[0]
    parts = []
    for g, (q0, n_q, keys) in enumerate(_fox_groups(s)):
        parts.append(tile_fwd(f"{name}_g{g}", functools.partial(_fox_attn_fn, q0), (4, n_q), _fox_attn_ops(qn, kn, pm, cum_c, cum_r, q0, keys),
                              [((n_q * FOX_BLOCK, BRANCH), BF16, (FOX_BLOCK, LANES), lambda p, i: (i, p), ())], raw=(0, 1, 2))[0])
    return jnp.concatenate(parts, axis=0)


def fox_attn_bwd(name, qn, kn, pm, cum_c, cum_r, dy):
    s = qn.shape[0]
    d_qn, d_kn, d_v, d_cum = [], 0.0, 0.0, 0.0
    for g, (q0, n_q, keys) in enumerate(_fox_groups(s)):
        rows = n_q * FOX_BLOCK
        own_q = ((rows, BRANCH), (FOX_BLOCK, LANES), lambda p, i: (i, p))
        own_k = ((keys, BRANCH), (keys, LANES), lambda p, i: (0, p))
        pair_c = ((4, rows, 1), (1, FOX_BLOCK, 1), lambda p, i: (p, i, 0))
        pair_r = ((4, 1, keys), (1, 1, keys), lambda p, i: (p, 0, 0))
        g_qn, g_kn, g_v, g_cqa, g_cqb, g_cka, g_ckb = tile_bwd(
            f"{name}_g{g}", functools.partial(_fox_attn_fn, q0), (4, n_q), _fox_attn_ops(qn, kn, pm, cum_c, cum_r, q0, keys),
            [(dy, (FOX_BLOCK, LANES), lambda p, i, q0=q0: (q0 + i, p))],
            [(0, (), own_q), (1, (1,), own_k), (2, (1,), own_k), (3, (), pair_c), (4, (), pair_c), (5, (1,), pair_r), (6, (1,), pair_r)])
        d_qn.append(g_qn)
        tail = lambda t, axis: jnp.pad(t, [(0, s - keys) if ax == axis else (0, 0) for ax in range(t.ndim)])
        d_kn, d_v = d_kn + tail(g_kn, 0), d_v + tail(g_v, 0)
        by_q = jnp.stack([g_cqa[:, :, 0], g_cqb[:, :, 0]], axis=1).reshape(8, rows)
        by_k = jnp.stack([g_cka[:, 0, :], g_ckb[:, 0, :]], axis=1).reshape(8, keys)
        d_cum = d_cum + jnp.pad(by_q, [(0, 0), (q0 * FOX_BLOCK, s - q0 * FOX_BLOCK - rows)]) + tail(by_k, 1)
    return jnp.concatenate(d_qn, axis=0), d_kn, d_v, d_cum


def sconv_ops(pm, w):
    s = pm.shape[0]
    blk = lambda c0: (pm, (s, LANES), lambda j, c0=c0: (0, c0 // LANES + j))
    return [blk(C_SB), blk(C_SC), blk(C_SV), (w, (w.shape[0], LANES), lambda j: (0, j))]


def dnconv_ops(pm, w):
    s = pm.shape[0]
    return [(pm, (s, LANES), lambda j: (0, C_DN // LANES + j)), (w, (w.shape[0], LANES), lambda j: (0, j))]


def ffn_ops(ug, uv, w):
    s = ug.shape[0]
    n_t = D_FF // LANES
    return [(ug, (s, LANES), lambda j: (0, j)), (uv, (s, LANES), lambda j: (0, j)),
            (w, (w.shape[0], LANES), lambda j: (0, j)), (w, (w.shape[0], LANES), lambda j: (0, n_t + j))]


def _col_out(s, width, dtype=F32):
    return ((s, width), dtype, (s, LANES), lambda j: (0, j), ())


def _col_cot(g):
    return (g, (g.shape[0], LANES), lambda j: (0, j))


def merge_ops(yp, pm):
    gate = lambda b: (pm, (256, D_MODEL), lambda i, b=b: (i, C_GATE // D_MODEL + b))
    return [_rows(yp[0]), _rows(yp[1]), _rows(yp[2]), gate(0), gate(1), gate(2)]


def ple_ops(gpre, pe, x):
    return [_rows(gpre), _rows(pe), _rows(x)]


def adam_call(name, w, g, m, v):
    shape = w.shape
    last = shape[-1]
    rows = w.size // last
    flat = lambda t: t.reshape(rows, last)
    tm = rows
    for cand in (512, 256, 128, 64, 32, 16, 8):
        if rows % cand == 0 and cand * last * 4 <= 2 * 1024 * 1024:
            tm = cand
            break
    spec = lambda t: (flat(t), (tm, last), lambda i: (i, 0))
    out = ((rows, last), F32, (tm, last), lambda i: (i, 0), ())
    res = tile_fwd(name, _adam_fn, (rows // tm,), [spec(w), spec(g), spec(m), spec(v)], [out, out, out])
    return [r.reshape(shape) for r in res]


def _adam_layers_fn(d, pids, w, m, v, g0, g1):
    g = jnp.where(pids[0] == 0, g0, g1)
    return (g,) + _adam_fn(d, pids, w, g, m, v)


def adam_layers(name, w, m, v, g0, g1):
    _, rows, cols = w.shape
    tm = _row_tile(rows, cols)
    n_t = rows // tm
    lay = lambda t: (t, (1, tm, cols), lambda l, i: (l, i, 0))
    ins = [lay(w), lay(m), lay(v), (g0, (tm, cols), lambda l, i: (i * (1 - l) + (n_t - 1) * l, 0)), (g1, (tm, cols), lambda l, i: (i * l, 0))]
    out = (w.shape, F32, (1, tm, cols), lambda l, i: (l, i, 0), ())
    return tile_fwd(name, _adam_layers_fn, (2, n_t), ins, [out, out, out, out])


def adam_w_in(name, w, m, v, g0, g1):
    rows, n_l, cols = w.shape

    def body(w_ref, m_ref, v_ref, g0_ref, g1_ref, g_out, d_out, m_out, v_out):
        step = 64

        def update(at):
            g0, g1 = g0_ref[at, :], g1_ref[at, :]
            layer = _iota((g0.shape[0], n_l, LANES), 1)
            g = jnp.where(layer == 0, g0[:, None, :], g1[:, None, :])
            delta, m2, v2 = _adam_fn(False, None, w_ref[at], g, m_ref[at], v_ref[at])
            for ref, val in ((g_out, g), (d_out, delta), (m_out, m2), (v_out, v2)):
                ref[at] = val

        def some_rows(i, carry):
            update(pl.ds(pl.multiple_of(i * step, step), step))
            return carry

        lax.fori_loop(0, rows // step, some_rows, 0)
        if rows % step:
            update(pl.ds(rows - rows % step, rows % step))

    both = pl.BlockSpec((rows, n_l, LANES), lambda j: (0, 0, j))
    one = pl.BlockSpec((rows, LANES), lambda j: (0, j))
    return pl.pallas_call(
        body, grid=(cols // LANES,), in_specs=[both, both, both, one, one], out_specs=[both] * 4,
        out_shape=[jax.ShapeDtypeStruct(w.shape, F32)] * 4, name=name, compiler_params=_cparams(1),
    )(w, m, v, g0, g1)


DN_GROUP = 4


def _dn_local_specs(rev_n=None):
    rows = DN_GROUP * DN_CHUNK
    idx = (lambda j: j) if rev_n is None else (lambda j: rev_n - 1 - j)
    return [pl.BlockSpec((rows, 3 * BRANCH), lambda j: (idx(j), 0)), pl.BlockSpec((rows, LANES), lambda j: (idx(j), 0)),
            pl.BlockSpec((DN_GROUP, DN_HEADS, DN_CHUNK), lambda j: (idx(j), 0, 0)), pl.BlockSpec((2, DN_HEADS), lambda j: (0, 0))]


def _dn_group_inputs(qkv, ps, a_rows, c):
    lo = c * DN_CHUNK
    heads = _split_heads(qkv[lo:lo + DN_CHUNK])
    return heads[0:4], heads[4:8], heads[8:12], ps[lo:lo + DN_CHUNK], a_rows[c]


def dn_local_fwd(name, dn_act, ps, a_rows, ad):
    s = dn_act.shape[0]
    n_c, n_g = s // DN_CHUNK, s // (DN_GROUP * DN_CHUNK)
    rows = DN_GROUP * DN_CHUNK

    def body(qkv_ref, ps_ref, ar_ref, ad_ref, u_ref, kc_ref, qd_ref, kd_ref, qk_ref, gl_ref):
        qkv, ps_v, a_rows_v, ad_v = qkv_ref[...], ps_ref[...], ar_ref[...], ad_ref[...]
        args = [[] for _ in range(8)]
        for c in range(DN_GROUP):
            q4, k4, v4, ps_c, ar_c = _dn_group_inputs(qkv, ps_v, a_rows_v, c)
            for lst, vals in zip(args, (q4, k4, v4) + _dn_gates(ps_c, ar_c, ad_v)):
                lst.extend(vals)
        everything = _dn_local(False, *args)
        for c in range(DN_GROUP):
            res = everything[c * DN_HEADS:(c + 1) * DN_HEADS]
            at = pl.ds(c * DN_CHUNK, DN_CHUNK)
            for ref, i in ((u_ref, 0), (kc_ref, 1), (qd_ref, 2), (kd_ref, 3)):
                ref[at, :] = jnp.concatenate([r[i] for r in res], axis=1)
            for h in range(DN_HEADS):
                qk_ref[c, h] = res[h][4]
            gl_ref[c] = _head_rows([r[5] for r in res])

    wide = pl.BlockSpec((rows, BRANCH), lambda j: (j, 0))
    return pl.pallas_call(
        body, grid=(n_g,), in_specs=_dn_local_specs(),
        out_specs=[wide, wide, wide, wide, pl.BlockSpec((DN_GROUP, DN_HEADS, DN_CHUNK, DN_CHUNK), lambda j: (j, 0, 0, 0)),
                   pl.BlockSpec((DN_GROUP, 8, LANES), lambda j: (j, 0, 0))],
        out_shape=[jax.ShapeDtypeStruct((s, BRANCH), F32)] * 4 + [jax.ShapeDtypeStruct((n_c, DN_HEADS, DN_CHUNK, DN_CHUNK), F32),
                                                                 jax.ShapeDtypeStruct((n_c, 8, LANES), F32)],
        name=name, compiler_params=_cparams(1),
    )(dn_act, ps, a_rows, ad)


def dn_local_bwd(name, dn_act, ps, a_rows, ad, cots):
    s = dn_act.shape[0]
    n_c, n_g = s // DN_CHUNK, s // (DN_GROUP * DN_CHUNK)
    rows = DN_GROUP * DN_CHUNK

    def body(qkv_ref, ps_ref, ar_ref, ad_ref, du_ref, dkc_ref, dqd_ref, dkd_ref, dqk_ref, dgl_ref, dqkv_ref, dps_ref, dar_ref, dad_ref):
        first = pl.program_id(0) == 0
        qkv, ps_v, a_rows_v, ad_v = qkv_ref[...], ps_ref[...], ar_ref[...], ad_ref[...]
        d_wide = [r[...] for r in (du_ref, dkc_ref, dqd_ref, dkd_ref)]
        qs, ks, vs, ps_cs, ar_cs, cot = [], [], [], [], [], []
        for c in range(DN_GROUP):
            q4, k4, v4, ps_c, ar_c = _dn_group_inputs(qkv, ps_v, a_rows_v, c)
            qs, ks, vs, ps_cs, ar_cs = qs + q4, ks + k4, vs + v4, ps_cs + [ps_c], ar_cs + [ar_c]
            lo = c * DN_CHUNK
            d_tiles = [_split_heads(t[lo:lo + DN_CHUNK]) for t in d_wide]
            d_gl = dgl_ref[c]
            cot += [(d_tiles[0][h], d_tiles[1][h], d_tiles[2][h], d_tiles[3][h], dqk_ref[c, h], _col(_row(d_gl, h), 0))
                    for h in range(DN_HEADS)]

        def f(qs, ks, vs, ps_cs, ar_cs, ad_v):
            gates = [[] for _ in range(5)]
            for ps_c, ar_c in zip(ps_cs, ar_cs):
                for lst, vals in zip(gates, _dn_gates(ps_c, ar_c, ad_v)):
                    lst.extend(vals)
            return _dn_local(True, qs, ks, vs, *gates)

        _, vjp = jax.vjp(f, qs, ks, vs, ps_cs, ar_cs, ad_v)
        d_q, d_k, d_v, d_ps, d_ar, d_ad = vjp(cot)
        for c in range(DN_GROUP):
            at, hs = pl.ds(c * DN_CHUNK, DN_CHUNK), slice(c * DN_HEADS, (c + 1) * DN_HEADS)
            dqkv_ref[at, :] = jnp.concatenate(d_q[hs] + d_k[hs] + d_v[hs], axis=1).astype(dqkv_ref.dtype)
            dps_ref[at, :] = d_ps[c]
            dar_ref[c] = d_ar[c]
        _store(dad_ref, d_ad, first)

    wide = pl.BlockSpec((rows, BRANCH), lambda j: (j, 0))
    specs = _dn_local_specs()
    return pl.pallas_call(
        body, grid=(n_g,),
        in_specs=specs + [wide, wide, wide, wide, pl.BlockSpec((DN_GROUP, DN_HEADS, DN_CHUNK, DN_CHUNK), lambda j: (j, 0, 0, 0)),
                          pl.BlockSpec((DN_GROUP, 8, LANES), lambda j: (j, 0, 0))],
        out_specs=specs,
        out_shape=[jax.ShapeDtypeStruct((s, 3 * BRANCH), F32), jax.ShapeDtypeStruct((s, LANES), F32),
                   jax.ShapeDtypeStruct((n_c, DN_HEADS, DN_CHUNK), F32), jax.ShapeDtypeStruct((2, DN_HEADS), F32)],
        name=name, compiler_params=_cparams(1),
    )(dn_act, ps, a_rows, ad, *cots)


def _dn_scan_specs(n_c, rev):
    idx = (lambda j: n_c - 1 - j) if rev else (lambda j: j)
    wide = pl.BlockSpec((DN_CHUNK, BRANCH), lambda j: (idx(j), 0))
    return [wide, wide, wide, wide, pl.BlockSpec((1, DN_HEADS, DN_CHUNK, DN_CHUNK), lambda j: (idx(j), 0, 0, 0)),
            pl.BlockSpec((1, 8, LANES), lambda j: (idx(j), 0, 0)), pl.BlockSpec((DN_CHUNK, BRANCH), lambda j: (idx(j), C_DZ // BRANCH)),
            pl.BlockSpec((1, DN_DH), lambda j: (0, 0))]


def _dn_scan_tiles(refs):
    u_ref, kc_ref, qd_ref, kd_ref, qk_ref, gl_ref, z_ref, g_ref = refs
    wide = [_split_heads(r[...]) for r in (u_ref, kc_ref, qd_ref, kd_ref)]
    gl = gl_ref[0]
    return [(wide[0][h], wide[1][h], wide[2][h], wide[3][h], qk_ref[0, h], _col(_row(gl, h), 0)) for h in range(DN_HEADS)], \
        _split_heads(z_ref[...].astype(F32)), g_ref[...]


def dn_scan_fwd(name, local, pm, gain):
    s = pm.shape[0]
    n_c = s // DN_CHUNK

    def body(*refs):
        y_ref, hist_ref, state = refs[8:]

        @pl.when(pl.program_id(0) == 0)
        def _():
            state[...] = jnp.zeros_like(state)

        hist_ref[0] = state[...]
        per_head, z4, gain_v = _dn_scan_tiles(refs[:8])
        ys, s_nexts = _dn_step(False, [state[h] for h in range(DN_HEADS)], per_head, z4, gain_v)
        for h in range(DN_HEADS):
            state[h] = s_nexts[h]
        y_ref[...] = jnp.concatenate(ys, axis=1).astype(y_ref.dtype)

    return pl.pallas_call(
        body, grid=(n_c,), in_specs=_dn_scan_specs(n_c, False),
        out_specs=[pl.BlockSpec((DN_CHUNK, BRANCH), lambda j: (j, 0)),
                   pl.BlockSpec((1, DN_HEADS, DN_DH, DN_DH), lambda j: (j, 0, 0, 0))],
        out_shape=[jax.ShapeDtypeStruct((s, BRANCH), BF16), jax.ShapeDtypeStruct((n_c, DN_HEADS, DN_DH, DN_DH), F32)],
        scratch_shapes=[pltpu.VMEM((DN_HEADS, DN_DH, DN_DH), F32)],
        name=name, compiler_params=_cparams(1),
    )(*local, pm, gain)


def dn_scan_bwd(name, local, pm, gain, hist, dy):
    s = pm.shape[0]
    n_c = s // DN_CHUNK

    def body(*refs):
        hist_ref, dy_ref = refs[8:10]
        du_ref, dkc_ref, dqd_ref, dkd_ref, dqk_ref, dgl_ref, dz_ref, dg_ref, d_state = refs[10:]
        first = pl.program_id(0) == 0

        @pl.when(first)
        def _():
            d_state[...] = jnp.zeros_like(d_state)

        per_head, z4, gain_v = _dn_scan_tiles(refs[:8])
        _, vjp = jax.vjp(functools.partial(_dn_step, True), [hist_ref[0, h] for h in range(DN_HEADS)], per_head, z4, gain_v)
        d_s, grads, d_z, d_gain = vjp((_split_heads(dy_ref[...].astype(F32)), [d_state[h] for h in range(DN_HEADS)]))
        for h in range(DN_HEADS):
            d_state[h] = d_s[h]
        for ref, i in ((du_ref, 0), (dkc_ref, 1), (dqd_ref, 2), (dkd_ref, 3)):
            ref[...] = jnp.concatenate([g[i] for g in grads], axis=1)
        dz_ref[...] = jnp.concatenate(d_z, axis=1).astype(dz_ref.dtype)
        for h in range(DN_HEADS):
            dqk_ref[0, h] = grads[h][4]
        dgl_ref[0] = _head_rows([g[5] for g in grads])
        _store(dg_ref, d_gain, first)

    rev = lambda j: n_c - 1 - j
    specs = _dn_scan_specs(n_c, True)
    return pl.pallas_call(
        body, grid=(n_c,),
        in_specs=specs + [pl.BlockSpec((1, DN_HEADS, DN_DH, DN_DH), lambda j: (rev(j), 0, 0, 0)),
                          pl.BlockSpec((DN_CHUNK, BRANCH), lambda j: (rev(j), 0))],
        out_specs=specs[:6] + [pl.BlockSpec((DN_CHUNK, BRANCH), lambda j: (rev(j), 0)), specs[7]],
        out_shape=[jax.ShapeDtypeStruct((s, BRANCH), F32)] * 4 + [
            jax.ShapeDtypeStruct((n_c, DN_HEADS, DN_CHUNK, DN_CHUNK), F32), jax.ShapeDtypeStruct((n_c, 8, LANES), F32),
            jax.ShapeDtypeStruct((s, BRANCH), BF16), jax.ShapeDtypeStruct((1, DN_DH), F32)],
        scratch_shapes=[pltpu.VMEM((DN_HEADS, DN_DH, DN_DH), F32)],
        name=name, compiler_params=_cparams(1),
    )(*local, pm, gain, hist, dy)


def _seq_layouts(cols, s):
    return cols.T.reshape(cols.shape[1], s // LANES, LANES)


def layer_fwd(li, x, p, w, more_weights=None):
    s = x.shape[0]
    n = lambda t: f"{t}_l{li}"
    h = rms_fwd(n("rms_mix"), x, w["g_mix"])
    pm = mm(n("in_main"), h, w["in_main"], "nn")
    ps = mm(n("in_small"), h, w["in_small"], "nn")
    qn, kn = fox_prep_fwd(n("fox_prep"), pm, w["gq"], w["gk"])
    f_t = _seq_layouts(ps[:, 0:8], s)
    cum = fox_gate_fwd(n("fox_gate"), f_t, w["b_f"])
    cum_c, cum_r = cum.reshape(8, s, 1), cum.reshape(8, 1, s)
    y_fox = fox_attn_fwd(n("fox_attn"), qn, kn, pm, cum_c, cum_r)
    y_sc = tile_fwd(n("sconv"), _sconv_fn, (BRANCH // LANES,), sconv_ops(pm, w["sc_conv_w"]), [_col_out(s, BRANCH, BF16)])[0]
    dn_act = tile_fwd(n("dnconv"), _dnconv_fn, (3 * BRANCH // LANES,), dnconv_ops(pm, w["dn_conv_w"]), [_col_out(s, 3 * BRANCH)])[0]
    a_rows = ps[:, 12:16].reshape(s // DN_CHUNK, DN_CHUNK, DN_HEADS).transpose(0, 2, 1)
    dn_local = dn_local_fwd(n("dn_local"), dn_act, ps, a_rows, w["ad"])
    y_dn, hist = dn_scan_fwd(n("dn_scan"), dn_local, pm, w["dn_gain"])
    ys = (y_fox, y_sc, y_dn)
    if more_weights is not None:
        w = {**w, **more_weights(y_dn)}
    yp = [mm(n(f"branch{b}"), ys[b], w["branch"][b], "nn", blocks=(0, N_CHIPS)) for b in range(3)]
    merged = tile_fwd(n("merge"), _merge_fn, (s // 256,), merge_ops(yp, pm), [((s, D_MODEL), BF16, (256, D_MODEL), lambda i: (i, 0), ())])[0]
    x1 = mm(n("w_o"), merged, w["o"], "nn", add=x)
    h2 = rms_fwd(n("rms_ffn"), x1, w["g_ffn"])
    ug = mm(n("up_g"), h2, w["up"], "nn", blocks=(0, 2))
    uv = mm(n("up_v"), h2, w["up"], "nn", blocks=(2, 2))
    act = tile_fwd(n("ffn_act"), _ffn_act_fn, (D_FF // LANES,), ffn_ops(ug, uv, w["ffn_conv_w"]), [_col_out(s, D_FF, BF16)])[0]
    x2 = mm(n("down"), act, w["down"], "nn", add=x1)
    h3 = rms_fwd(n("rms_ple"), x2, w["g_ple"])
    gpre = mm(n("ple_gate"), h3, w["pg"], "nn")
    pe = mm(n("ple_emb"), p, w["ple"], "nn", blocks=(0, N_CHIPS))
    x3 = tile_fwd(n("ple"), _ple_fn, (s // 256,), ple_ops(gpre, pe, x2), [((s, D_MODEL), F32, (256, D_MODEL), lambda i: (i, 0), ())])[0]
    saved = dict(x=x, h=h, pm=pm, ps=ps, qn=qn, kn=kn, f_t=f_t, cum_c=cum_c, cum_r=cum_r, ys=ys, dn_act=dn_act, dn_local=dn_local,
                 a_rows=a_rows, hist=hist, yp=yp, merged=merged, x1=x1, h2=h2, ug=ug, uv=uv, act=act, x2=x2, h3=h3,
                 gpre=gpre, pe=pe, p=p)
    return x3, saved, w


def hang_on(w, token):
    zero = token[0, 0]
    small = ("g_mix", "g_ffn", "g_ple", "gq", "gk", "b_f", "ad", "dn_gain", "sc_conv_w", "dn_conv_w", "ffn_conv_w")
    return {**w, **{k: w[k] + zero for k in small}}


def layer_bwd(li, dx3, sv, w, hooks=None):
    hooks = hooks or {}

    def stage(key, after, w):
        return hang_on(w, hooks[key](after, g)) if key in hooks else w

    s = dx3.shape[0]
    n = lambda t: f"{t}_l{li}"
    g = {}
    col_own = lambda width: ((s, width), (s, LANES), lambda j: (0, j))
    d_gpre, d_pe = tile_bwd(n("ple_bwd"), _ple_fn, (s // 256,), ple_ops(sv["gpre"], sv["pe"], sv["x2"]), [_rows(dx3)],
                            [(0, (), None, BF16), (1, (), None, BF16)])
    g["w_ple"] = mm(n("d_w_ple"), sv["p"], d_pe, "tn", blocks=(0, N_CHIPS))
    g["w_ple_gate"] = mm(n("d_w_pg"), sv["h3"], d_gpre, "tn").reshape(N_CHIPS, -1, D_MODEL)
    dh3 = mm(n("d_h3"), d_gpre, w["pg"], "nt")
    dx2, d_g_ple = rms_bwd(n("rms_ple_bwd"), sv["x2"], w["g_ple"], dh3, dx3)
    dact = mm(n("d_act"), dx2, w["down"], "nt")
    g["w_down"] = mm(n("d_w_down"), sv["act"], dx2, "tn").reshape(N_CHIPS, -1, D_MODEL)
    taps_own = ((w["ffn_conv_w"].shape[0], D_FF), (w["ffn_conv_w"].shape[0], LANES), lambda j: (0, j))
    d_ug, d_uv, d_fw_g, d_fw_v = tile_bwd(n("ffn_act_bwd"), _ffn_act_fn, (D_FF // LANES,), ffn_ops(sv["ug"], sv["uv"], w["ffn_conv_w"]),
                                          [_col_cot(dact)], [(0, (), None, BF16), (1, (), None, BF16), (2, (), taps_own), (3, (), taps_own)])
    g["ffn_conv_w"] = jnp.concatenate([d_fw_g, d_fw_v], axis=1)
    g["w_up"] = jnp.concatenate([mm(n("d_w_up_g"), sv["h2"], d_ug, "tn", blocks=(0, 2)), mm(n("d_w_up_v"), sv["h2"], d_uv, "tn", blocks=(0, 2))])
    dh2 = mm(n("d_h2_v"), d_uv, w["up"], "nt", blocks=(2, 2), add=mm(n("d_h2_g"), d_ug, w["up"], "nt", blocks=(0, 2)))
    dx1, d_g_ffn = rms_bwd(n("rms_ffn_bwd"), sv["x1"], w["g_ffn"], dh2, dx2)
    w = stage("mid", dx1, w)
    dmerged = mm(n("d_merged"), dx1, w["o"], "nt")
    g["w_o"] = mm(n("d_w_o"), sv["merged"], dx1, "tn").reshape(N_CHIPS, -1, D_MODEL)
    gate_own = ((s, D_MODEL), (256, D_MODEL), lambda i: (i, 0))
    d_yp0, d_yp1, d_yp2, d_g0, d_g1, d_g2 = tile_bwd(
        n("merge_bwd"), _merge_fn, (s // 256,), merge_ops(sv["yp"], sv["pm"]), [_rows(dmerged)],
        [(0, (), None, BF16), (1, (), None, BF16), (2, (), None, BF16), (3, (), gate_own, BF16), (4, (), gate_own, BF16), (5, (), gate_own, BF16)])
    d_yp = (d_yp0, d_yp1, d_yp2)
    g["w_branch"] = jnp.concatenate([mm(n(f"d_w_branch{b}"), sv["ys"][b], d_yp[b], "tn", blocks=(0, N_CHIPS)) for b in range(3)], axis=1)
    d_ys = [mm(n(f"d_y{b}"), d_yp[b], w["branch"][b], "nt", blocks=(0, N_CHIPS)) for b in range(3)]
    w = stage("late", d_ys[2], w)
    *d_local, d_z, d_dngain = dn_scan_bwd(n("dn_scan_bwd"), sv["dn_local"], sv["pm"], w["dn_gain"], sv["hist"], d_ys[2])
    d_dnact, d_ps_dn, d_arows, d_ad = dn_local_bwd(n("dn_local_bwd"), sv["dn_act"], sv["ps"], sv["a_rows"], w["ad"], d_local)
    g["ad"], g["dn_norm_gain"] = d_ad, d_dngain[0]
    d_dnqkv, g["dn_conv_w"] = tile_bwd(n("dnconv_bwd"), _dnconv_fn, (3 * BRANCH // LANES,), dnconv_ops(sv["pm"], w["dn_conv_w"]),
                                       [_col_cot(d_dnact)], [(0, (), col_own(3 * BRANCH), BF16), (1, ())])
    d_sb, d_sc, d_sv, g["sc_conv_w"] = tile_bwd(n("sconv_bwd"), _sconv_fn, (BRANCH // LANES,), sconv_ops(sv["pm"], w["sc_conv_w"]), [_col_cot(d_ys[1])],
                                                [(0, (), col_own(BRANCH), BF16), (1, (), col_own(BRANCH), BF16), (2, (), col_own(BRANCH), BF16), (3, ())])
    w = stage("last", d_dnqkv, w)
    d_qn, d_kn, d_fv, d_cum = fox_attn_bwd(n("fox_attn_bwd"), sv["qn"], sv["kn"], sv["pm"], sv["cum_c"], sv["cum_r"], d_ys[0])
    d_ft, d_bf = fox_gate_bwd(n("fox_gate_bwd"), sv["f_t"], w["b_f"], d_cum.reshape(8, s // LANES, LANES))
    g["b_fox_f"] = d_bf.reshape(8)
    d_fq, d_fk, d_gq, d_gk = fox_prep_bwd(n("fox_prep_bwd"), sv["pm"], w["gq"], w["gk"], d_qn, d_kn)
    g["fox_q_gain"] = d_gq[0, :FOX_DH] + d_gq[0, FOX_DH:]
    g["fox_k_gain"] = d_gk[0, :FOX_DH] + d_gk[0, FOX_DH:]
    d_pm = jnp.concatenate([d_fq, d_fk, d_fv.astype(BF16), d_sb, d_sc, d_sv, d_dnqkv, d_z, d_g0, d_g1, d_g2], axis=1)
    d_a_cols = d_arows.transpose(0, 2, 1).reshape(s, DN_HEADS)
    d_f_cols = d_ft.reshape(8, s).T
    d_ps = d_ps_dn + jnp.concatenate([d_f_cols, jnp.zeros((s, 4), F32), d_a_cols, jnp.zeros((s, LANES - 16), F32)], axis=1)
    g["w_in"] = chip_blocks_w_in(mm(n("d_w_in_main"), d_pm, sv["h"], "tn"), mm(n("d_w_in_small"), d_ps, sv["h"], "tn"))
    w = stage("w_in", g["w_in"], w)
    dh = mm(n("d_h_small"), d_ps, w["in_small"], "nt", add=mm(n("d_h_main"), d_pm, w["in_main"], "nt"))
    dx, d_g_mix = rms_bwd(n("rms_mix_bwd"), sv["x"], w["g_mix"], dh, dx1)
    g["g_mix"], g["g_ffn"], g["g_ple"] = d_g_mix[0], d_g_ffn[0], d_g_ple[0]
    return dx, g


IN_SHARD = 2052
MAIN_RANGES = ((0, 1536), (1544, 3080), (3080, 4616), (4624, 5136), (5136, 8208))
SMALL_RANGES = ((1536, 1544), (4616, 4620), (4620, 4624))


def _from_chip_blocks(blocks, ranges):
    parts = []
    for lo, hi in ranges:
        for k in range(N_CHIPS):
            a0, a1 = max(lo, k * IN_SHARD), min(hi, (k + 1) * IN_SHARD)
            if a0 < a1:
                parts.append(blocks[k][:, a0 - k * IN_SHARD:a1 - k * IN_SHARD])
    return parts


def split_w_in(blocks):
    main = jnp.concatenate(_from_chip_blocks(blocks, MAIN_RANGES), axis=1)
    pad = jnp.zeros((blocks.shape[1], LANES - 16), blocks.dtype)
    return main, jnp.concatenate(_from_chip_blocks(blocks, SMALL_RANGES) + [pad], axis=1)


def chip_blocks_w_in(main, small):
    ranges = sorted([(lo, hi, "m") for lo, hi in MAIN_RANGES] + [(lo, hi, "s") for lo, hi in SMALL_RANGES])
    offs, m_off, s_off = {}, 0, 0
    for lo, hi in MAIN_RANGES:
        offs[lo] = m_off
        m_off += hi - lo
    for lo, hi in SMALL_RANGES:
        offs[lo] = s_off
        s_off += hi - lo
    blocks = []
    for k in range(N_CHIPS):
        parts = []
        for lo, hi, src in ranges:
            a0, a1 = max(lo, k * IN_SHARD), min(hi, (k + 1) * IN_SHARD)
            if a0 < a1:
                arr = main if src == "m" else small
                parts.append(arr[offs[lo] + a0 - lo:offs[lo] + a1 - lo])
        blocks.append(jnp.concatenate(parts, axis=0))
    return jnp.stack(blocks)


def later_weights(got):
    g_branch, g_o, g_up, g_down, g_pg, g_ple = got
    branch = g_branch.reshape(N_CHIPS, 3, BRANCH, -1)
    return dict(branch=[branch[:, b] for b in range(3)], o=g_o.reshape(D_MODEL, D_MODEL), up=g_up,
                down=g_down.reshape(D_FF, D_MODEL), pg=g_pg.reshape(D_MODEL, D_MODEL), ple=g_ple)


def layer_weights(li, got, conv, a):
    main, small = split_w_in(got[0])
    tile2 = lambda v: jnp.concatenate([v, v])[None, :]
    rest = later_weights(got[1:]) if len(got) > 1 else {}
    return dict(
        in_main=main, in_small=small, **rest,
        g_mix=a["g_mix"][li][None, :], g_ffn=a["g_ffn"][li][None, :], g_ple=a["g_ple"][li][None, :],
        gq=tile2(a["fox_q_gain"][li]), gk=tile2(a["fox_k_gain"][li]), b_f=a["b_fox_f"][li].reshape(8, 1, 1),
        ad=jnp.stack([a["dn_a_log"][li], a["dn_dt_bias"][li]]), dn_gain=a["dn_norm_gain"][li][None, :],
        sc_conv_w=conv["sc_conv_w"][li], dn_conv_w=conv["dn_conv_w"][li], ffn_conv_w=conv["ffn_conv_w"][li])


def pack_rows(arrs, dtype):
    flat = jnp.concatenate([t.reshape(-1).astype(dtype) for t in arrs])
    pad = (-flat.shape[0]) % (8 * LANES)
    if pad:
        flat = jnp.concatenate([flat, jnp.zeros((pad,), dtype)])
    return flat.reshape(-1, LANES)


def unpack_rows(buf, shapes):
    flat = buf.reshape(-1)
    out, off = [], 0
    for shp in shapes:
        size = 1
        for dim in shp:
            size *= dim
        out.append(flat[off:off + size].reshape(shp))
        off += size
    return out


def chip_shard(t, axis, k):
    width = t.shape[axis] // N_CHIPS
    return lax.slice_in_dim(t, k * width, (k + 1) * width, axis=axis)


ANY = pl.BlockSpec(memory_space=pl.ANY)


def _position():
    x, y, c = lax.axis_index("x"), lax.axis_index("y"), lax.axis_index("c")
    return x, y, c, [(1 - x, y), (x, 1 - y), (1 - x, 1 - y)]


def gather_small(name, block):
    m_per, n = block.shape

    def body(x_ref, out_ref, token, send_sems, recv_sems, local_sem):
        token[...] = jnp.zeros_like(token)
        x, y, c, chips = _position()
        me, sibling = (x, y, c), (x, y, 1 - c)

        def rows(px, py, pc):
            return out_ref.at[pl.ds((4 * px + 2 * py + pc) * m_per, m_per), :]

        def copy(k, blk, to, src=None):
            return pltpu.make_async_remote_copy(src_ref=rows(*blk) if src is None else src, dst_ref=rows(*blk),
                                                send_sem=send_sems.at[k], recv_sem=recv_sems.at[k], device_id=to, device_id_type=MESH)

        mine = pltpu.make_async_copy(x_ref, rows(*me), local_sem)
        mine.start()
        first = [copy(0, me, sibling, src=x_ref)] + [copy(1 + j, me, (*chip, c), src=x_ref) for j, chip in enumerate(chips)]
        for cp in first:
            cp.start()
        passed = [copy(4 + j, (*chip, c), sibling) for j, chip in enumerate(chips)]
        for j, chip in enumerate(chips):
            copy(1 + j, (*chip, c), me).wait_recv()
            passed[j].start()
        copy(0, sibling, me).wait_recv()
        for j, chip in enumerate(chips):
            copy(4 + j, (*chip, 1 - c), me).wait_recv()
        for cp in first + passed:
            cp.wait_send()
        mine.wait()

    in_vmem = pl.BlockSpec(memory_space=pltpu.VMEM)
    return pl.pallas_call(
        body, out_shape=[jax.ShapeDtypeStruct((8 * m_per, n), block.dtype), jax.ShapeDtypeStruct((8, LANES), F32)],
        in_specs=[in_vmem], out_specs=[in_vmem, in_vmem],
        scratch_shapes=[pltpu.SemaphoreType.DMA((7,)), pltpu.SemaphoreType.DMA((7,)), pltpu.SemaphoreType.DMA],
        name=name, compiler_params=pltpu.CompilerParams(vmem_limit_bytes=VMEM_LIMIT),
    )(block)


def _sems(n):
    return [pltpu.SemaphoreType.DMA((n,)), pltpu.SemaphoreType.DMA((n,))]


def _split_cols(rows):
    return (rows // 2) % 16 != 0


def _half(ref, which, lead=()):
    rows, cols = ref.shape[-2:]
    if _split_cols(rows):
        return ref.at[(*lead, slice(None), pl.ds(which * (cols // 2), cols // 2))]
    return ref.at[(*lead, pl.ds(which * (rows // 2), rows // 2), slice(None))]


def _half_shape(rows, cols):
    return (rows, cols // 2) if _split_cols(rows) else (rows // 2, cols)


def gather_layer(name, shards):
    n_w = len(shards)

    def body(*refs):
        ins, outs = refs[:n_w], refs[n_w:2 * n_w]
        token, send_sems, recv_sems = refs[2 * n_w:]
        token[...] = jnp.zeros_like(token)
        x, y, c, chips = _position()
        sibling = (x, y, 1 - c)

        def part(w, px, py, pc):
            return _half(outs[w], pc, (2 * px + py,))

        def copy(k, w, blk, to, src=None):
            return pltpu.make_async_remote_copy(src_ref=part(w, *blk) if src is None else src, dst_ref=part(w, *blk),
                                                send_sem=send_sems.at[k], recv_sem=recv_sems.at[k], device_id=to, device_id_type=MESH)

        pairs = [(w, j, chip) for w in range(n_w) for j, chip in enumerate(chips)]
        first = [copy(3 * w + j, w, (x, y, c), (*chip, c), src=_half(ins[w], c)) for w, j, chip in pairs]
        for cp in first:
            cp.start()
        passed = [copy(3 * n_w + 3 * w + j, w, (*chip, c), sibling) for w, j, chip in pairs]
        for (w, j, chip), fwd in zip(pairs, passed):
            copy(3 * w + j, w, (*chip, c), (x, y, c)).wait_recv()
            fwd.start()
        for w, j, chip in pairs:
            copy(3 * n_w + 3 * w + j, w, (*chip, 1 - c), (x, y, c)).wait_recv()
        for cp in first + passed:
            cp.wait_send()

    out = pl.pallas_call(
        body, out_shape=[jax.ShapeDtypeStruct((N_CHIPS,) + s.shape, s.dtype) for s in shards] + [jax.ShapeDtypeStruct((8, LANES), F32)],
        in_specs=[ANY] * n_w, out_specs=[ANY] * n_w + [pl.BlockSpec(memory_space=pltpu.VMEM)], scratch_shapes=_sems(6 * n_w), name=name,
    )(*shards)
    return out[:n_w], out[n_w]


def swap_halves(name, grads):
    n_w = len(grads)

    def body(*refs):
        ins, outs = refs[:n_w], refs[n_w:2 * n_w]
        send_sems, recv_sems = refs[2 * n_w:]
        x, y, c, _ = _position()
        cps = [pltpu.make_async_remote_copy(src_ref=_half(ins[w], 1 - c, (slice(None),)), dst_ref=outs[w],
                                            send_sem=send_sems.at[w], recv_sem=recv_sems.at[w], device_id=(x, y, 1 - c),
                                            device_id_type=MESH) for w in range(n_w)]
        for cp in cps:
            cp.start()
        for cp in cps:
            cp.wait()

    return pl.pallas_call(
        body, out_shape=[jax.ShapeDtypeStruct((N_CHIPS,) + _half_shape(*g.shape[1:]), g.dtype) for g in grads],
        in_specs=[ANY] * n_w, out_specs=[ANY] * n_w, scratch_shapes=_sems(n_w), name=name,
    )(*grads)


def scatter_chips(name, partials):
    n_w = len(partials)

    def body(*refs):
        ins, outs = refs[:n_w], refs[n_w:2 * n_w]
        send_sems, recv_sems = refs[2 * n_w:]
        x, y, c, chips = _position()
        cps = [pltpu.make_async_remote_copy(src_ref=ins[w].at[2 * cx + cy], dst_ref=outs[w].at[j], send_sem=send_sems.at[3 * w + j],
                                            recv_sem=recv_sems.at[3 * w + j], device_id=(cx, cy, c), device_id_type=MESH)
               for w in range(n_w) for j, (cx, cy) in enumerate(chips)]
        for cp in cps:
            cp.start()
        for cp in cps:
            cp.wait()

    return pl.pallas_call(
        body, out_shape=[jax.ShapeDtypeStruct((3,) + p.shape[1:], p.dtype) for p in partials],
        in_specs=[ANY] * n_w, out_specs=[ANY] * n_w, scratch_shapes=_sems(3 * n_w), name=name,
    )(*partials)


def share_halves(name, bufs):
    n_w = len(bufs)

    def body(*refs):
        outs = refs[n_w:2 * n_w]
        send_sems, recv_sems = refs[2 * n_w:]
        x, y, c, _ = _position()

        def copy(w, pc):
            half = _half(outs[w], pc)
            return pltpu.make_async_remote_copy(src_ref=half, dst_ref=half, send_sem=send_sems.at[w], recv_sem=recv_sems.at[w],
                                                device_id=(x, y, 1 - c), device_id_type=MESH)

        for w in range(n_w):
            copy(w, c).start()
        for w in range(n_w):
            copy(w, 1 - c).wait_recv()
            copy(w, c).wait_send()

    return pl.pallas_call(
        body, out_shape=[jax.ShapeDtypeStruct(b.shape, b.dtype) for b in bufs], in_specs=[ANY] * n_w, out_specs=[ANY] * n_w,
        input_output_aliases={w: w for w in range(n_w)}, scratch_shapes=_sems(n_w), name=name,
    )(*bufs)


HBM = pl.BlockSpec(memory_space=pltpu.HBM)
SEM = pl.BlockSpec(memory_space=pltpu.SEMAPHORE)
EFFECT = pltpu.SideEffectType.DATAFLOW_SIDE_EFFECTING


def _exchange_copies(kind, srcs, lands):
    x, y, c, chips = _position()
    out = []
    for src, land in zip(srcs, lands):
        if kind == "swap":
            out.append((_half(src, 1 - c, (slice(None),)), land, (x, y, 1 - c)))
            continue
        for j, (cx, cy) in enumerate(chips):
            if kind == "gather":
                out.append((src, land.at[2 * x + y], (cx, cy, c)))
            else:
                out.append((src.at[2 * cx + cy], land.at[j], (cx, cy, c)))
    return out


def _land_shapes(kind, srcs):
    if kind == "gather":
        return [(N_CHIPS,) + s.shape for s in srcs]
    if kind == "swap":
        return [(N_CHIPS,) + _half_shape(*s.shape[1:]) for s in srcs]
    return [(3,) + s.shape[1:] for s in srcs]


def exchange_start(name, kind, srcs):
    n_w = len(srcs)
    shapes = _land_shapes(kind, srcs)
    n_sem = n_w if kind == "swap" else 3 * n_w

    def body(*refs):
        ins, lands = refs[:n_w], refs[n_w:2 * n_w]
        send_sems, recv_sems = refs[2 * n_w:2 * n_w + 2]
        token = refs[-1]
        for i, (src, dst, dev) in enumerate(_exchange_copies(kind, ins, lands)):
            pltpu.make_async_remote_copy(src_ref=src, dst_ref=dst, send_sem=send_sems.at[i], recv_sem=recv_sems.at[i],
                                         device_id=dev, device_id_type=MESH).start()
        token[...] = jnp.zeros_like(token)

    out = pl.pallas_call(
        body, name=name,
        out_shape=(pltpu.SemaphoreType.DMA((n_sem,)), pltpu.SemaphoreType.DMA((n_sem,)),
                   *[pltpu.HBM(s.shape, s.dtype) for s in srcs], *[pltpu.HBM(shp, s.dtype) for shp, s in zip(shapes, srcs)],
                   jax.ShapeDtypeStruct((8, LANES), F32)),
        in_specs=(HBM,) * (2 * n_w), out_specs=(SEM, SEM) + (HBM,) * (2 * n_w) + (pl.BlockSpec(memory_space=pltpu.VMEM),),
        input_output_aliases={i: 2 + i for i in range(2 * n_w)},
        compiler_params=pltpu.CompilerParams(has_side_effects=EFFECT),
    )(*[pltpu.with_memory_space_constraint(s, pltpu.HBM) for s in srcs],
      *[pltpu.with_memory_space_constraint(lax.empty(shp, s.dtype), pltpu.HBM) for shp, s in zip(shapes, srcs)])
    return (kind, n_w, out[:-1]), out[-1]


def exchange_wait(name, handle, after):
    kind, n_w, (send_sems, recv_sems, *thru) = handle
    n_sem = n_w if kind == "swap" else 3 * n_w

    def body(*refs):
        ins, lands = refs[:n_w], refs[n_w:2 * n_w]
        send_sems, recv_sems = refs[2 * n_w:2 * n_w + 2]
        for i, (src, dst, dev) in enumerate(_exchange_copies(kind, ins, lands)):
            cp = pltpu.make_async_remote_copy(src_ref=src, dst_ref=dst, send_sem=send_sems.at[i], recv_sem=recv_sems.at[i],
                                              device_id=dev, device_id_type=MESH)
            cp.wait_send()
            cp.wait_recv()

    out = pl.pallas_call(
        body, name=name, out_shape=tuple(pltpu.HBM(t.shape, t.dtype) for t in thru),
        in_specs=(HBM,) * (2 * n_w) + (SEM, SEM, pl.BlockSpec(memory_space=pl.ANY)), out_specs=(HBM,) * (2 * n_w),
        input_output_aliases={i: i for i in range(2 * n_w)},
        compiler_params=pltpu.CompilerParams(has_side_effects=EFFECT),
    )(*thru, send_sems, recv_sems, after)
    return list(out[:n_w]), list(out[n_w:])


def _row_tile(rows, cols):
    best = rows
    if rows * cols * 4 <= 1024 * 1024:
        return rows
    for t in range(16, rows, 16):
        if rows % t == 0 and t * cols * 4 <= 1024 * 1024:
            best = t
    return best


def pair_sum(name, pos, grad, from_sibling):
    _, rows, cols = grad.shape
    h_rows, h_cols = _half_shape(rows, cols)
    tr = _row_tile(h_rows, h_cols)
    n_t = h_rows // tr

    def body(pos_ref, g_ref, s_ref, b_ref, f_ref):
        tot = g_ref[...] + s_ref[...]
        b_ref[...] = tot.astype(BF16)

        @pl.when(pl.program_id(1) == pos_ref[1])
        def _():
            f_ref[...] = tot[0]

    blk = pl.BlockSpec((1, tr, h_cols), lambda i, k, pos: (k, i, 0))
    if _split_cols(rows):
        mine = pl.BlockSpec((1, tr, h_cols), lambda i, k, pos: (k, i, pos[0]))
    else:
        mine = pl.BlockSpec((1, tr, h_cols), lambda i, k, pos: (k, pos[0] * n_t + i, 0))
    return pl.pallas_call(
        body, grid_spec=pltpu.PrefetchScalarGridSpec(
            num_scalar_prefetch=1, grid=(n_t, N_CHIPS), in_specs=[mine, blk],
            out_specs=[blk, pl.BlockSpec((tr, h_cols), lambda i, k, pos: (i, 0))]),
        out_shape=[jax.ShapeDtypeStruct((N_CHIPS, h_rows, h_cols), BF16), jax.ShapeDtypeStruct((h_rows, h_cols), F32)],
        name=name, compiler_params=_cparams(2),
    )(pos, grad, from_sibling)


def chip_sum(name, pos, own, landed, split_cols):
    half, cols = own.shape
    tr = _row_tile(half, cols)
    n_t = half // tr

    def body(pos_ref, p_ref, l_ref, o_ref):
        o_ref[...] = ((p_ref[...] + l_ref[0].astype(F32)) + l_ref[1].astype(F32)) + l_ref[2].astype(F32)

    if split_cols:
        out_spec, out_shape = pl.BlockSpec((tr, cols), lambda i, pos: (i, pos[0])), (half, 2 * cols)
    else:
        out_spec, out_shape = pl.BlockSpec((tr, cols), lambda i, pos: (pos[0] * n_t + i, 0)), (2 * half, cols)
    return pl.pallas_call(
        body, grid_spec=pltpu.PrefetchScalarGridSpec(
            num_scalar_prefetch=1, grid=(n_t,),
            in_specs=[pl.BlockSpec((tr, cols), lambda i, pos: (i, 0)), pl.BlockSpec((3, tr, cols), lambda i, pos: (0, i, 0))],
            out_specs=out_spec),
        out_shape=jax.ShapeDtypeStruct(out_shape, F32), name=name, compiler_params=_cparams(1),
    )(pos, own, landed)


def reduce_scatter_layer(tag, pos, grads):
    n = lambda t: f"{t}_{tag}"
    from_sibling = swap_halves(n("swap_halves"), grads)
    sums = [pair_sum(n(f"pair_sum{w}"), pos, g, s) for w, (g, s) in enumerate(zip(grads, from_sibling))]
    landed = scatter_chips(n("scatter_chips"), [b for b, _ in sums])
    halves = [chip_sum(n(f"chip_sum{w}"), pos, own, l, _split_cols(g.shape[1])) for w, ((_, own), l, g) in enumerate(zip(sums, landed, grads))]
    return share_halves(n("share_halves"), halves)


class OverlappedReduceScatter:
    def __init__(self, tag, pos, grads):
        self.n = lambda t: f"{t}_{tag}"
        self.pos, self.grads = pos, grads
        self.swap, self.token = exchange_start(self.n("swap_start"), "swap", grads)

    def middle(self, after):
        self.grads, from_sibling = exchange_wait(self.n("swap_wait"), self.swap, after)
        self.sums = [pair_sum(self.n(f"pair_sum{w}"), self.pos, g, s) for w, (g, s) in enumerate(zip(self.grads, from_sibling))]
        self.scatter, self.token = exchange_start(self.n("scatter_start"), "scatter", [b for b, _ in self.sums])

    def finish(self, after):
        _, landed = exchange_wait(self.n("scatter_wait"), self.scatter, after)
        halves = [chip_sum(self.n(f"chip_sum{w}"), self.pos, own, l, _split_cols(g.shape[1]))
                  for w, ((_, own), l, g) in enumerate(zip(self.sums, landed, self.grads))]
        return share_halves(self.n("share_halves"), halves)


def sum_devices(gathered):
    m_per = gathered.shape[0] // 8

    def body(g_ref, o_ref):
        tot = g_ref[pl.ds(0, m_per), :]
        for dev in range(1, 8):
            tot = tot + g_ref[pl.ds(dev * m_per, m_per), :]
        o_ref[...] = tot

    return pl.pallas_call(
        body, out_shape=jax.ShapeDtypeStruct((m_per, gathered.shape[1]), F32),
        in_specs=[pl.BlockSpec(memory_space=pltpu.VMEM)], out_specs=pl.BlockSpec(memory_space=pltpu.VMEM), name="sum_devices",
    )(gathered)


def kernel(x, p, g_mix, w_in, b_fox_f, fox_q_gain, fox_k_gain, sc_conv_w, dn_conv_w, dn_a_log, dn_dt_bias, dn_norm_gain, w_branch, w_o, g_ffn, w_up, ffn_conv_w, w_down, g_ple, w_ple_gate, w_ple, loss_target, m_g_mix, m_w_in, m_b_fox_f, m_fox_q_gain, m_fox_k_gain, m_sc_conv_w, m_dn_conv_w, m_dn_a_log, m_dn_dt_bias, m_dn_norm_gain, m_w_branch, m_w_o, m_g_ffn, m_w_up, m_ffn_conv_w, m_w_down, m_g_ple, m_w_ple_gate, m_w_ple, v_g_mix, v_w_in, v_b_fox_f, v_fox_q_gain, v_fox_k_gain, v_sc_conv_w, v_dn_conv_w, v_dn_a_log, v_dn_dt_bias, v_dn_norm_gain, v_w_branch, v_w_o, v_g_ffn, v_w_up, v_ffn_conv_w, v_w_down, v_g_ple, v_w_ple_gate, v_w_ple):
    a = dict(g_mix=g_mix, w_in=w_in, b_fox_f=b_fox_f, fox_q_gain=fox_q_gain, fox_k_gain=fox_k_gain, sc_conv_w=sc_conv_w,
             dn_conv_w=dn_conv_w, dn_a_log=dn_a_log, dn_dt_bias=dn_dt_bias, dn_norm_gain=dn_norm_gain, w_branch=w_branch, w_o=w_o,
             g_ffn=g_ffn, w_up=w_up, ffn_conv_w=ffn_conv_w, w_down=w_down, g_ple=g_ple, w_ple_gate=w_ple_gate, w_ple=w_ple)
    mom = dict(g_mix=m_g_mix, w_in=m_w_in, b_fox_f=m_b_fox_f, fox_q_gain=m_fox_q_gain, fox_k_gain=m_fox_k_gain, sc_conv_w=m_sc_conv_w,
               dn_conv_w=m_dn_conv_w, dn_a_log=m_dn_a_log, dn_dt_bias=m_dn_dt_bias, dn_norm_gain=m_dn_norm_gain, w_branch=m_w_branch,
               w_o=m_w_o, g_ffn=m_g_ffn, w_up=m_w_up, ffn_conv_w=m_ffn_conv_w, w_down=m_w_down, g_ple=m_g_ple, w_ple_gate=m_w_ple_gate,
               w_ple=m_w_ple)
    var = dict(g_mix=v_g_mix, w_in=v_w_in, b_fox_f=v_b_fox_f, fox_q_gain=v_fox_q_gain, fox_k_gain=v_fox_k_gain, sc_conv_w=v_sc_conv_w,
               dn_conv_w=v_dn_conv_w, dn_a_log=v_dn_a_log, dn_dt_bias=v_dn_dt_bias, dn_norm_gain=v_dn_norm_gain, w_branch=v_w_branch,
               w_o=v_w_o, g_ffn=v_g_ffn, w_up=v_w_up, ffn_conv_w=v_ffn_conv_w, w_down=v_w_down, g_ple=v_g_ple, w_ple_gate=v_w_ple_gate,
               w_ple=v_w_ple)
    cx, cy, cc = lax.axis_index("x"), lax.axis_index("y"), lax.axis_index("c")
    chip = 2 * cx + cy
    pos = jnp.stack([cc, chip]).astype(jnp.int32)

    def as_blocks(t):
        return t.reshape(2, -1, t.shape[-1])

    def own_block_in(got, shards):
        return [lax.dynamic_update_slice(g, s[None], (chip, 0, 0)) for g, s in zip(got, shards)]

    conv_shapes = [a[nm].shape for nm in CONVS]
    conv_all, conv_token = gather_small("gather_conv_w", pack_rows([a[nm] for nm in CONVS], F32))
    def layer_block(nm, t, li):
        return as_blocks(t)[li]

    shards0 = [(layer_block(nm, a[nm], 0) + conv_token[0, 0]).astype(BF16) for nm in BIG]
    got0, gathered_token = gather_layer("gather_w_in_l0", shards0[:1])
    shards0[1:] = [s + gathered_token[0, 0].astype(BF16) for s in shards0[1:]]
    gather0, gather0_token = exchange_start("gather_start_l0", "gather", shards0[1:])
    shards1 = [(layer_block(nm, a[nm], 1) + gather0_token[0, 0]).astype(BF16) for nm in BIG]
    gather1, gather1_in_token = exchange_start("gather_start_w_in_l1", "gather", shards1[:1])
    shards1[1:] = [s + gather1_in_token[0, 0].astype(BF16) for s in shards1[1:]]
    gather1_rest, gather1_token = exchange_start("gather_start_l1", "gather", shards1[1:])
    conv_rows = conv_all.shape[0] // 8
    conv_chip = [unpack_rows(conv_all[2 * k * conv_rows:(2 * k + 1) * conv_rows], conv_shapes) for k in range(N_CHIPS)]
    conv = {nm: jnp.concatenate([conv_chip[k][i] for k in range(N_CHIPS)], axis=2) for i, nm in enumerate(CONVS)}

    weights, saved = [None, None], [None, None]
    first_weights = hang_on(layer_weights(0, own_block_in(got0, shards0[:1]), conv, a), gather1_token)

    def rest_of_layer0(after):
        mine, got = exchange_wait("gather_wait_l0", gather0, after)
        return later_weights(own_block_in(got, mine))

    act, saved[0], weights[0] = layer_fwd(0, x[0], p[0, 0], first_weights, more_weights=rest_of_layer0)
    mine1, got1 = exchange_wait("gather_wait_w_in_l1", gather1, act)

    def rest_of_layer1(after):
        mine, got = exchange_wait("gather_wait_l1", gather1_rest, after)
        return later_weights(own_block_in(got, mine))

    act, saved[1], weights[1] = layer_fwd(1, act, p[1, 0], layer_weights(1, own_block_in(got1, mine1), conv, a),
                                          more_weights=rest_of_layer1)
    d_act, loss_part = loss_call(act, loss_target[0])
    loss = lax.psum(loss_part, ("x", "y", "c"))
    layer_grads = [None, None]
    d_act, layer_grads[1] = layer_bwd(1, d_act, saved[1], weights[1])
    rs1 = OverlappedReduceScatter("l1", pos, [layer_grads[1][nm] for nm in BIG])
    rs0 = []

    def stage_mid(after, g):
        rs1.middle(after)
        return rs1.token

    def stage_late(after, g):
        rs0.append(OverlappedReduceScatter("l0", pos, [g[nm] for nm in BIG[1:]]))
        return rs0[0].token

    def stage_last(after, g):
        rs0[0].middle(after)
        return rs0[0].token

    def stage_w_in(after, g):
        rs0.append(OverlappedReduceScatter("w_in_l0", pos, [g["w_in"]]))
        return rs0[1].token

    d_act, layer_grads[0] = layer_bwd(0, d_act, saved[0], hang_on(weights[0], rs1.token),
                                      hooks=dict(mid=stage_mid, late=stage_late, last=stage_last, w_in=stage_w_in))
    rs0[1].middle(d_act)
    reduced = [rs0[0].finish(rs0[1].token), rs1.finish(rs0[1].token)]
    grad_x = d_act[None]

    def both(nm):
        return jnp.stack([layer_grads[0][nm], layer_grads[1][nm]])

    local = {nm: both(nm) for nm in ("g_mix", "b_fox_f", "fox_q_gain", "fox_k_gain", "dn_norm_gain", "g_ffn", "g_ple", "sc_conv_w",
                                      "dn_conv_w", "ffn_conv_w")}
    local["dn_a_log"] = jnp.stack([layer_grads[li]["ad"][0] for li in range(2)])
    local["dn_dt_bias"] = jnp.stack([layer_grads[li]["ad"][1] for li in range(2)])

    small_names = SMALL + CONVS
    small_shapes = [local[nm].shape for nm in small_names]
    small_sum = sum_devices(gather_small("gather_small_grads", pack_rows([local[nm] for nm in small_names], F32))[0])
    small_grads = dict(zip(small_names, unpack_rows(small_sum, small_shapes)))
    for nm in CONVS:
        width = a[nm].shape[2]
        small_grads[nm] = lax.dynamic_slice_in_dim(small_grads[nm], chip * width, width, axis=2)

    grads, deltas, new_m, new_v = dict(small_grads), {}, {}, {}
    for nm in small_names:
        deltas[nm], new_m[nm], new_v[nm] = adam_call(f"adam_{nm}", a[nm], grads[nm], mom[nm], var[nm])
    for i, nm in enumerate(BIG[1:]):
        res = adam_layers(f"adam_{nm}", as_blocks(a[nm]), as_blocks(mom[nm]), as_blocks(var[nm]), reduced[0][i], reduced[1][1 + i])
        grads[nm], deltas[nm], new_m[nm], new_v[nm] = [r.reshape(a[nm].shape) for r in res]
    stored = lambda t: jnp.transpose(t, (2, 0, 1))
    res = adam_w_in("adam_w_in", stored(a["w_in"]), stored(mom["w_in"]), stored(var["w_in"]), rs0[1].finish(deltas["w_ple"])[0], reduced[1][0])
    grads["w_in"], deltas["w_in"], new_m["w_in"], new_v["w_in"] = [jnp.transpose(r, (1, 2, 0)) for r in res]
    return (loss, grad_x, *[grads[nm] for nm in WEIGHTS], *[deltas[nm] for nm in WEIGHTS], *[new_m[nm] for nm in WEIGHTS],
            *[new_v[nm] for nm in WEIGHTS])
```

```python
import functools

import jax
import jax.numpy as jnp
from jax import lax
from jax.experimental import pallas as pl
from jax.experimental.pallas import tpu as pltpu

F32 = jnp.float32
BF16 = jnp.bfloat16
HI = lax.Precision.HIGHEST
SOLVE = lax.Precision.HIGH
MESH = pl.DeviceIdType.MESH

D_MODEL = 1024
BRANCH = 512
FOX_DH = 64
DN_DH = 128
DN_HEADS = 4
DN_CHUNK = 64
FOX_BLOCK = 128
D_FF = 2816
EPS = 1e-6
N_CHIPS = 4
LANES = 128

ADAM_LR, ADAM_B1, ADAM_B2, ADAM_EPS, ADAM_WD, ADAM_STEP = 0.001, 0.9, 0.999, 1e-08, 0.01, 10

VMEM_LIMIT = 56 * 1024 * 1024

C_FQ, C_FK, C_FV, C_SB, C_SC, C_SV, C_DN, C_DZ, C_GATE = 0, 512, 1024, 1536, 2048, 2560, 3072, 4608, 5120
IN_MAIN = 8192
IN_SIZES = (1536, 8, 1536, 1536, 4, 4, 512, 3072)

BIG = ("w_in", "w_branch", "w_o", "w_up", "w_down", "w_ple_gate", "w_ple")
BIG_AXIS = {"w_in": 2, "w_branch": 3, "w_o": 1, "w_up": 2, "w_down": 1, "w_ple_gate": 1, "w_ple": 2}
CONVS = ("sc_conv_w", "dn_conv_w", "ffn_conv_w")
SMALL = ("g_mix", "b_fox_f", "fox_q_gain", "fox_k_gain", "dn_a_log", "dn_dt_bias", "dn_norm_gain", "g_ffn", "g_ple")
WEIGHTS = ("g_mix", "w_in", "b_fox_f", "fox_q_gain", "fox_k_gain", "sc_conv_w", "dn_conv_w", "dn_a_log", "dn_dt_bias",
           "dn_norm_gain", "w_branch", "w_o", "g_ffn", "w_up", "ffn_conv_w", "w_down", "g_ple", "w_ple_gate", "w_ple")


def _iota(shape, dim):
    return lax.broadcasted_iota(jnp.int32, shape, dim)


def _dg(a, b, mode, prec=None):
    dims = {"nn": ((1,), (0,)), "nt": ((1,), (1,)), "tn": ((0,), (0,))}[mode]
    return lax.dot_general(a, b, (dims, ((), ())), precision=prec, preferred_element_type=F32)


def _bdot_impl(a, b, mode):
    return _dg(a.astype(BF16), b.astype(BF16), mode)


@functools.partial(jax.custom_vjp, nondiff_argnums=(2,))
def _bdot_diff(a, b, mode):
    return _bdot_impl(a, b, mode)


def _bdot_fwd(a, b, mode):
    return _bdot_impl(a, b, mode), (a, b)


def _bdot_bwd(mode, res, g):
    a, b = res
    if mode == "nn":
        da, db = _bdot_impl(g, b, "nt"), _bdot_impl(a, g, "tn")
    elif mode == "nt":
        da, db = _bdot_impl(g, b, "nn"), _bdot_impl(g, a, "tn")
    else:
        da, db = _bdot_impl(b, g, "nt"), _bdot_impl(a, g, "nn")
    return da.astype(a.dtype), db.astype(b.dtype)


_bdot_diff.defvjp(_bdot_fwd, _bdot_bwd)


def _bdot(d):
    return _bdot_diff if d else _bdot_impl


def _shift_impl(x, k):
    return jnp.where(_iota(x.shape, 0) >= k, pltpu.roll(x, k, 0), 0.0)


def _unshift_impl(g, k):
    n = g.shape[0]
    return jnp.where(_iota(g.shape, 0) < n - k, pltpu.roll(g, n - k, 0), 0.0)


@functools.partial(jax.custom_vjp, nondiff_argnums=(1,))
def _shift_diff(x, k):
    return _shift_impl(x, k)


_shift_diff.defvjp(lambda x, k: (_shift_impl(x, k), None), lambda k, _, g: (_unshift_impl(g, k),))


def _row(w, j):
    return jnp.sum(jnp.where(_iota(w.shape, 0) == j, w, 0.0), axis=0, keepdims=True)


def _col(w, j):
    return jnp.sum(jnp.where(_iota(w.shape, 1) == j, w, 0.0), axis=1, keepdims=True)


def _conv(d, x, w):
    shift = _shift_diff if d else _shift_impl
    taps = w.shape[0]
    y = x * _row(w, taps - 1)
    for j in range(taps - 1):
        y = y + shift(x, taps - 1 - j) * _row(w, j)
    return y


def _softplus(x):
    return jnp.maximum(x, 0.0) + jnp.log(1.0 + jnp.exp(-jnp.abs(x)))


def _silu(x):
    return x * jax.nn.sigmoid(x)


def _rms(x, gain):
    return x * lax.rsqrt(jnp.mean(x * x, axis=-1, keepdims=True) + EPS) * gain


def _rms_fn(d, pids, x, gain):
    return (_rms(x, gain),)


def _loss_fn(d, pids, y, t):
    e = y - t
    part = 0.5 / D_MODEL * jnp.sum(e * e, keepdims=True)
    return e * (1.0 / D_MODEL), jnp.broadcast_to(part, (8, LANES))


def _fox_prep_fn(d, pids, q, k, gq, gk):
    first = _iota(q.shape, 1) < FOX_DH

    def norm(x, gain):
        sq = x * x
        ss_a = jnp.sum(jnp.where(first, sq, 0.0), axis=1, keepdims=True)
        ss_b = jnp.sum(jnp.where(first, 0.0, sq), axis=1, keepdims=True)
        rs = jnp.where(first, lax.rsqrt(ss_a / FOX_DH + EPS), lax.rsqrt(ss_b / FOX_DH + EPS))
        return x * rs * gain

    return norm(q, gq) * FOX_DH ** -0.5, norm(k, gk)


def _fox_gate_fn(d, pids, f, bias):
    logf = -_softplus(-(f + bias))
    n_r, n_c = logf.shape
    tri = (_iota((n_c, n_c), 0) <= _iota((n_c, n_c), 1)).astype(F32)
    within = _dg(logf, tri, "nn", HI)
    tot = jnp.broadcast_to(jnp.sum(logf, axis=1, keepdims=True), logf.shape)
    below = (_iota((n_r, n_r), 1) < _iota((n_r, n_r), 0)).astype(F32)
    return (within + _dg(below, tot, "nn", HI),)


def _fox_attn_fn(q_block0, d, pids, q, k, v, cq_a, cq_b, ck_a, ck_b):
    dot = _bdot(d)
    first = _iota(q.shape, 1) < FOX_DH
    n_q, n_k = q.shape[0], k.shape[0]
    causal = ((q_block0 + pids[1]) * n_q + _iota((n_q, n_k), 0)) >= _iota((n_q, n_k), 1)

    qs = [jnp.where(first, q, 0.0), jnp.where(first, 0.0, q)]
    s = _each(lambda qh, cq, ck: jnp.where(causal, dot(qh, k, "nt") + cq - ck, -1e30), qs, [cq_a, cq_b], [ck_a, ck_b])
    e = [jnp.exp(si - lax.stop_gradient(jnp.max(si, axis=1, keepdims=True))) for si in s]
    o_a, o_b = [dot(ei / jnp.sum(ei, axis=1, keepdims=True), v, "nn") for ei in e]
    return (jnp.where(first, o_a, o_b),)


def _sconv_fn(d, pids, sb, sc, sv, w):
    return (sb * _conv(d, sc * sv, w),)


def _dnconv_fn(d, pids, x, w):
    return (_silu(_conv(d, x, w)),)


def _merge_fn(d, pids, y0, y1, y2, g0, g1, g2):
    return (jax.nn.sigmoid(g0) * y0 + jax.nn.sigmoid(g1) * y1 + jax.nn.sigmoid(g2) * y2,)


def _ffn_act_fn(d, pids, ug, uv, wg, wv):
    return (_silu(_conv(d, ug, wg)) * _conv(d, uv, wv),)


def _ple_fn(d, pids, gpre, pe, x):
    return (x + jax.nn.sigmoid(gpre) * pe,)


def _adam_fn(d, pids, w, g, m, v):
    m2 = ADAM_B1 * m + (1.0 - ADAM_B1) * g
    v2 = ADAM_B2 * v + (1.0 - ADAM_B2) * (g * g)
    m_hat = m2 / (1.0 - ADAM_B1 ** ADAM_STEP)
    v_hat = v2 / (1.0 - ADAM_B2 ** ADAM_STEP)
    delta = -ADAM_LR * (m_hat / (jnp.sqrt(v_hat) + ADAM_EPS) + ADAM_WD * w)
    return delta, m2, v2


def _each(fn, *lists):
    return [fn(*args) for args in zip(*lists)]


def _tri_inv_impl(mats):
    n = mats[0].shape[0]
    r, c = _iota((n, n), 0), _iota((n, n), 1)
    diag_blk = (r >> 4) == (c >> 4)
    eye = (r == c).astype(F32)
    mm = lambda us, ws: _each(lambda u, w: _dg(u, w, "nn", SOLVE), us, ws)
    grow = lambda ps, xs: _each(lambda p, px: p + px, ps, mm(ps, xs))
    x = [jnp.where(diag_blk, -a, 0.0) for a in mats]
    p = [eye + xi for xi in x]
    x2 = mm(x, x)
    p = grow(p, x2)
    x4 = mm(x2, x2)
    p = grow(p, x4)
    p = grow(p, mm(x4, x4))
    y = [-yi for yi in mm(p, [jnp.where(diag_blk, 0.0, a) for a in mats])]
    q = grow([eye + yi for yi in y], mm(y, y))
    return mm(q, p)


@jax.custom_vjp
def _tri_inv_diff(mats):
    return _tri_inv_impl(mats)


def _tri_inv_fwd(mats):
    ts = _tri_inv_impl(mats)
    return ts, ts


def _tri_inv_bwd(ts, gs):
    left = _each(lambda t, g: _dg(t, g, "tn", SOLVE), ts, gs)
    return ([-m for m in _each(lambda l, t: _dg(l, t, "nt", SOLVE), left, ts)],)


_tri_inv_diff.defvjp(_tri_inv_fwd, _tri_inv_bwd)


def _dn_local(d, qs, ks, vs, a_cs, a_rs, b_cs, a_logs, dt_bs):
    dot = _bdot(d)
    inv = _tri_inv_diff if d else _tri_inv_impl
    n = qs[0].shape[0]
    r, c = _iota((n, n), 0), _iota((n, n), 1)
    incl, strict, upper = r >= c, r > c, r <= c
    qs = [q * lax.rsqrt(jnp.sum(q * q, axis=1, keepdims=True) + EPS) * DN_DH ** -0.5 for q in qs]
    ks = [k * lax.rsqrt(jnp.sum(k * k, axis=1, keepdims=True) + EPS) for k in ks]
    betas = [jax.nn.sigmoid(b) for b in b_cs]
    rates = [-jnp.exp(a) for a in a_logs]
    g_cs = _each(lambda rate, a, dt: rate * _softplus(a + dt), rates, a_cs, dt_bs)
    g_rs = _each(lambda rate, a, dt: rate * _softplus(a + dt), rates, a_rs, dt_bs)
    gcum_cs = [jnp.sum(jnp.where(incl, g, 0.0), axis=1, keepdims=True) for g in g_rs]
    gcum_rs = [jnp.sum(jnp.where(upper, g, 0.0), axis=0, keepdims=True) for g in g_cs]
    decays = _each(lambda gc, gr: jnp.exp(jnp.where(incl, gc - gr, -1e30)), gcum_cs, gcum_rs)
    kbs = _each(lambda k, b: k * b, ks, betas)
    kk = _each(lambda kb, k: dot(kb, k, "nt"), kbs, ks)
    ts = inv(_each(lambda m, dec: jnp.where(strict, m * dec, 0.0), kk, decays))
    e_gs = [jnp.exp(g) for g in gcum_cs]
    us = _each(lambda t, v, b: _dg(t, v * b, "nn", SOLVE), ts, vs, betas)
    k_cums = _each(lambda t, kb, e: _dg(t, kb * e, "nn", SOLVE), ts, kbs, e_gs)
    qk = _each(lambda q, k: dot(q, k, "nt"), qs, ks)
    qk = _each(lambda m, dec: jnp.where(incl, m * dec, 0.0), qk, decays)
    g_lasts = [jnp.sum(g, axis=0, keepdims=True) for g in g_cs]
    q_decs = _each(lambda q, e: q * e, qs, e_gs)
    k_decs = _each(lambda k, gl, gc: k * jnp.exp(gl - gc), ks, g_lasts, gcum_cs)
    return list(zip(us, k_cums, q_decs, k_decs, qk, g_lasts))


def _dn_step(d, s_prevs, items, zs, gain):
    dot = _bdot(d)
    us, k_cums, q_decs, k_decs, qks, g_lasts = [list(t) for t in zip(*items)]
    v_news = _each(lambda u, kc, s: u - dot(kc, s, "nn"), us, k_cums, s_prevs)
    inter = _each(lambda qd, s: dot(qd, s, "nn"), q_decs, s_prevs)
    outs = _each(lambda o, qk, vn: o + dot(qk, vn, "nn"), inter, qks, v_news)
    s_nexts = _each(lambda s, gl, kd, vn: s * jnp.exp(gl) + dot(kd, vn, "tn"), s_prevs, g_lasts, k_decs, v_news)
    return _each(lambda o, z: _rms(o, gain) * _silu(z), outs, zs), s_nexts


def _split_heads(t):
    return [t[:, h * DN_DH:(h + 1) * DN_DH] for h in range(t.shape[1] // DN_DH)]


def _dn_gates(ps, a_rows, ad):
    hs = range(DN_HEADS)
    return ([_col(ps, 12 + h) for h in hs], [_row(a_rows, h) for h in hs], [_col(ps, 8 + h) for h in hs],
            [_col(_row(ad, 0), h) for h in hs], [_col(_row(ad, 1), h) for h in hs])


def _head_rows(vals):
    row = _iota((8, LANES), 0)
    tile = jnp.zeros((8, LANES), F32)
    for h, val in enumerate(vals):
        tile = tile + jnp.where(row == h, val, 0.0)
    return tile


def _cparams(n_axes):
    return pltpu.CompilerParams(dimension_semantics=("arbitrary",) * n_axes, vmem_limit_bytes=VMEM_LIMIT)


def _first_visit(acc_axes):
    cond = None
    for a in acc_axes:
        here = pl.program_id(a) == 0
        cond = here if cond is None else jnp.logical_and(cond, here)
    return cond


def _tile(ref, widen=False):
    val = ref[...]
    shape = val.shape
    while len(shape) > 2 and shape[0] == 1:
        shape = shape[1:]
    val = val.reshape(shape)
    return val.astype(F32) if widen and val.dtype == BF16 else val


def _store(ref, val, first):
    val = val.astype(ref.dtype).reshape(ref.shape)
    if first is None:
        ref[...] = val
        return

    @pl.when(first)
    def _():
        ref[...] = val

    @pl.when(jnp.logical_not(first))
    def _():
        ref[...] += val


def _specs(ops):
    return [pl.BlockSpec(block, imap) for _, block, imap in ops]


def tile_fwd(name, fn, grid, ins, outs, raw=()):
    n_in = len(ins)

    def body(*refs):
        pids = tuple(pl.program_id(a) for a in range(len(grid)))
        firsts = [_first_visit(o[4]) if o[4] else None for o in outs]
        res = fn(False, pids, *[_tile(r, i not in raw) for i, r in enumerate(refs[:n_in])])
        for ref, val, first in zip(refs[n_in:], res, firsts):
            _store(ref, val, first)

    out = pl.pallas_call(
        body, grid=grid, in_specs=_specs(ins),
        out_specs=[pl.BlockSpec(o[2], o[3]) for o in outs],
        out_shape=[jax.ShapeDtypeStruct(o[0], o[1]) for o in outs],
        name=name, compiler_params=_cparams(len(grid)),
    )(*[a for a, _, _ in ins])
    return out


def tile_bwd(name, fn, grid, ins, cots, diff, adds=None, raw=()):
    adds = adds or {}
    n_in, n_cot = len(ins), len(cots)
    add_pos = sorted(adds)
    diff_idx = [d[0] for d in diff]
    out_desc = [d[2] if len(d) > 2 and d[2] is not None else (ins[d[0]][0].shape, ins[d[0]][1], ins[d[0]][2]) for d in diff]
    out_dtypes = [d[3] if len(d) > 3 else F32 for d in diff]

    def body(*refs):
        pids = tuple(pl.program_id(a) for a in range(len(grid)))
        firsts = [_first_visit(d[1]) if d[1] else None for d in diff]
        vals = [_tile(r, i not in raw) for i, r in enumerate(refs[:n_in])]
        cot_vals = [_tile(r, True) for r in refs[n_in:n_in + n_cot]]
        add_vals = [_tile(r) for r in refs[n_in + n_cot:n_in + n_cot + len(add_pos)]]
        out_refs = refs[n_in + n_cot + len(add_pos):]

        def f(*dv):
            full = list(vals)
            for i, val in zip(diff_idx, dv):
                full[i] = val
            return fn(True, pids, *full)

        prim, vjp = jax.vjp(f, *[vals[i].astype(F32) for i in diff_idx])
        grads = list(vjp(tuple(c.astype(o.dtype) for c, o in zip(cot_vals, prim))))
        for pos, val in zip(add_pos, add_vals):
            extra = val.astype(F32) if firsts[pos] is None else jnp.where(firsts[pos], val.astype(F32), 0.0)
            grads[pos] = grads[pos] + extra
        for ref, val, first in zip(out_refs, grads, firsts):
            _store(ref, val, first)

    all_ins = list(ins) + list(cots) + [adds[p] for p in add_pos]
    out = pl.pallas_call(
        body, grid=grid, in_specs=_specs(all_ins),
        out_specs=[pl.BlockSpec(o[1], o[2]) for o in out_desc],
        out_shape=[jax.ShapeDtypeStruct(o[0], dt) for o, dt in zip(out_desc, out_dtypes)],
        name=name, compiler_params=_cparams(len(grid)),
    )(*[a for a, _, _ in all_ins])
    return out


def _pick(dim, cands):
    for c in cands:
        if dim % c == 0:
            return c
    return dim


MM_TILES = (1024, 512, 1408, 256, 128)


def mm(name, a, b, mode, add=None, out_dtype=F32, blocks=None):
    wide = None
    if mode == "nn":
        (m, kk), n = a.shape, b.shape[-1]
    elif mode == "nt":
        (m, kk), n = a.shape, b.shape[-2]
    else:
        (kk, m), n = a.shape, b.shape[1]
    if blocks is not None:
        lo, n_blk = blocks
        wide = b.shape[-1] if mode != "tn" else n // n_blk
        if mode == "nn":
            n = wide * n_blk
    tm = _pick(m, MM_TILES)
    if mode == "nt" and blocks is not None:
        tn, tk = _pick(n, MM_TILES), _pick(wide, MM_TILES[:-1])
    elif blocks is not None:
        tn, tk = _pick(wide, MM_TILES[:-1]), _pick(kk, MM_TILES)
    else:
        tn, tk = _pick(n, MM_TILES), _pick(kk, MM_TILES)
    if mode == "tn" or blocks is None:
        tk = _pick(kk, (2048,) + MM_TILES)
    nk = kk // tk
    a_spec = pl.BlockSpec((tk, tm), lambda i, j, k: (k, i)) if mode == "tn" else pl.BlockSpec((tm, tk), lambda i, j, k: (i, k))
    o_spec = pl.BlockSpec((tm, tn), lambda i, j, k: (i, j))
    out_shape = (m, n)
    if blocks is None:
        b_spec = pl.BlockSpec((tn, tk), lambda i, j, k: (j, k)) if mode == "nt" else pl.BlockSpec((tk, tn), lambda i, j, k: (k, j))
    elif mode == "nn":
        per = wide // tn
        b_spec = pl.BlockSpec((1, tk, tn), lambda i, j, k: (lo + j // per, k, j % per))
    elif mode == "nt":
        per = wide // tk
        b_spec = pl.BlockSpec((1, tn, tk), lambda i, j, k: (lo + k // per, j, k % per))
    else:
        per = wide // tn
        b_spec = pl.BlockSpec((tk, tn), lambda i, j, k: (k, j))
        o_spec = pl.BlockSpec((1, tm, tn), lambda i, j, k: (j // per, i, j % per))
        out_shape = (n_blk, m, wide)

    def body(*refs):
        a_ref, b_ref = refs[0], refs[1]
        add_ref = refs[2] if add is not None else None
        o_ref, acc = refs[-2], refs[-1]
        k = pl.program_id(2)
        part = _bdot_impl(_tile(a_ref), _tile(b_ref), mode)

        @pl.when(k == 0)
        def _():
            acc[...] = part

        @pl.when(k > 0)
        def _():
            acc[...] += part

        @pl.when(k == nk - 1)
        def _():
            res = acc[...]
            if add_ref is not None:
                res = res + add_ref[...]
            o_ref[...] = res.astype(o_ref.dtype).reshape(o_ref.shape)

    operands = [a, b] + ([add] if add is not None else [])
    in_specs = [a_spec, b_spec] + ([o_spec] if add is not None else [])
    return pl.pallas_call(
        body, grid=(m // tm, n // tn, nk), in_specs=in_specs, out_specs=o_spec,
        out_shape=jax.ShapeDtypeStruct(out_shape, out_dtype),
        scratch_shapes=[pltpu.VMEM((tm, tn), F32)],
        name=name, compiler_params=_cparams(3),
    )(*operands)


def _rows(x, width=None, off=0, tm=256):
    width = x.shape[1] if width is None else width
    return (x, (tm, width), lambda i, off=off: (i, off))


def _whole(x):
    nd = x.ndim
    return (x, x.shape, lambda *pids, nd=nd: (0,) * nd)


RMS_ROWS = 512


def _rms_ops(x, gain):
    return [_rows(x, tm=RMS_ROWS), _whole(gain)]


def rms_fwd(name, x, gain):
    s, dm = x.shape
    return tile_fwd(name, _rms_fn, (s // RMS_ROWS,), _rms_ops(x, gain), [((s, dm), BF16, (RMS_ROWS, dm), lambda i: (i, 0), ())])[0]


def rms_bwd(name, x, gain, dh, dres):
    s = x.shape[0]
    return tile_bwd(name, _rms_fn, (s // RMS_ROWS,), _rms_ops(x, gain), [_rows(dh, tm=RMS_ROWS)], [(0, ()), (1, (0,))],
                    adds={0: _rows(dres, tm=RMS_ROWS)})


def loss_call(y, t):
    s, dm = y.shape
    dy, part = tile_fwd("loss", _loss_fn, (s // 256,), [_rows(y), _rows(t)],
                        [((s, dm), F32, (256, dm), lambda i: (i, 0), ()), ((8, LANES), F32, (8, LANES), lambda i: (0, 0), (0,))])
    return dy, part[0, 0]


def _fox_prep_ops(pm, gq, gk):
    tm = 512
    return [(pm, (tm, LANES), lambda i, j: (i, C_FQ // LANES + j)), (pm, (tm, LANES), lambda i, j: (i, C_FK // LANES + j)),
            _whole(gq), _whole(gk)]


def fox_prep_fwd(name, pm, gq, gk):
    s = pm.shape[0]
    out = ((s, BRANCH), BF16, (512, LANES), lambda i, j: (i, j), ())
    return tile_fwd(name, _fox_prep_fn, (s // 512, 4), _fox_prep_ops(pm, gq, gk), [out, out])


def fox_prep_bwd(name, pm, gq, gk, dqn, dkn):
    s = pm.shape[0]
    cot = lambda g: (g, (512, LANES), lambda i, j: (i, j))
    own = ((s, BRANCH), (512, LANES), lambda i, j: (i, j))
    return tile_bwd(name, _fox_prep_fn, (s // 512, 4), _fox_prep_ops(pm, gq, gk), [cot(dqn), cot(dkn)],
                    [(0, (), own, BF16), (1, (), own, BF16), (2, (0, 1)), (3, (0, 1))])


def _fox_gate_ops(f_t, bias):
    return [(f_t, (1,) + f_t.shape[1:], lambda h: (h, 0, 0)), (bias, (1, 1, 1), lambda h: (h, 0, 0))]


def fox_gate_fwd(name, f_t, bias):
    n_h = f_t.shape[0]
    return tile_fwd(name, _fox_gate_fn, (n_h,), _fox_gate_ops(f_t, bias),
                    [(f_t.shape, F32, (1,) + f_t.shape[1:], lambda h: (h, 0, 0), ())])[0]


def fox_gate_bwd(name, f_t, bias, dcum):
    n_h = f_t.shape[0]
    return tile_bwd(name, _fox_gate_fn, (n_h,), _fox_gate_ops(f_t, bias),
                    [(dcum, (1,) + f_t.shape[1:], lambda h: (h, 0, 0))], [(0, ()), (1, ())])


FOX_GROUPS = 4


def _fox_groups(s):
    per = s // FOX_BLOCK // FOX_GROUPS
    return [(g * per, per, (g + 1) * per * FOX_BLOCK) for g in range(FOX_GROUPS)]


def _fox_attn_ops(qn, kn, pm, cum_c, cum_r, q0, keys):
    nb = FOX_BLOCK
    return [(qn, (nb, LANES), lambda p, i: (q0 + i, p)), (kn, (keys, LANES), lambda p, i: (0, p)),
            (pm, (keys, LANES), lambda p, i: (0, C_FV // LANES + p)),
            (cum_c, (1, nb, 1), lambda p, i: (2 * p, q0 + i, 0)), (cum_c, (1, nb, 1), lambda p, i: (2 * p + 1, q0 + i, 0)),
            (cum_r, (1, 1, keys), lambda p, i: (2 * p, 0, 0)), (cum_r, (1, 1, keys), lambda p, i: (2 * p + 1, 0, 0))]


def fox_attn_fwd(name, qn, kn, pm, cum_c, cum_r):
    s = qn.shape[0]
    parts = []
    for g, (q0, n_q, keys) in enumerate(_fox_groups(s)):
        parts.append(tile_fwd(f"{name}_g{g}", functools.partial(_fox_attn_fn, q0), (4, n_q), _fox_attn_ops(qn, kn, pm, cum_c, cum_r, q0, keys),
                              [((n_q * FOX_BLOCK, BRANCH), BF16, (FOX_BLOCK, LANES), lambda p, i: (i, p), ())], raw=(0, 1, 2))[0])
    return jnp.concatenate(parts, axis=0)


def fox_attn_bwd(name, qn, kn, pm, cum_c, cum_r, dy):
    s = qn.shape[0]
    groups = _fox_groups(s)
    d_qn, by_q, tails = [None] * len(groups), [None] * len(groups), [None] * len(groups)
    below = None
    for g in reversed(range(len(groups))):
        q0, n_q, keys = groups[g]
        rows = n_q * FOX_BLOCK
        own_q = ((rows, BRANCH), (FOX_BLOCK, LANES), lambda p, i: (i, p))
        own_k = ((keys, BRANCH), (keys, LANES), lambda p, i: (0, p))
        pair_c = ((4, rows, 1), (1, FOX_BLOCK, 1), lambda p, i: (p, i, 0))
        pair_r = ((4, 1, keys), (1, 1, keys), lambda p, i: (p, 0, 0))
        adds = {}
        if below is not None:
            adds = {1: (below[0],) + own_k[1:], 2: (below[1],) + own_k[1:], 5: (below[2],) + pair_r[1:], 6: (below[3],) + pair_r[1:]}
        g_qn, g_kn, g_v, g_cqa, g_cqb, g_cka, g_ckb = tile_bwd(
            f"{name}_g{g}", functools.partial(_fox_attn_fn, q0), (4, n_q), _fox_attn_ops(qn, kn, pm, cum_c, cum_r, q0, keys),
            [(dy, (FOX_BLOCK, LANES), lambda p, i, q0=q0: (q0 + i, p))],
            [(0, (), own_q), (1, (1,), own_k), (2, (1,), own_k), (3, (), pair_c), (4, (), pair_c), (5, (1,), pair_r), (6, (1,), pair_r)],
            adds=adds)
        below = (g_kn, g_v, g_cka, g_ckb)
        lo = groups[g - 1][2] if g else 0
        d_qn[g] = g_qn
        by_q[g] = jnp.stack([g_cqa[:, :, 0], g_cqb[:, :, 0]], axis=1).reshape(8, rows)
        tails[g] = (g_kn[lo:], g_v[lo:], jnp.stack([g_cka[:, 0, lo:], g_ckb[:, 0, lo:]], axis=1).reshape(8, keys - lo))
    d_cum = jnp.concatenate(by_q, axis=1) + jnp.concatenate([t[2] for t in tails], axis=1)
    return jnp.concatenate(d_qn, axis=0), jnp.concatenate([t[0] for t in tails], axis=0), jnp.concatenate([t[1] for t in tails], axis=0), d_cum


def sconv_ops(pm, w):
    s = pm.shape[0]
    blk = lambda c0: (pm, (s, LANES), lambda j, c0=c0: (0, c0 // LANES + j))
    return [blk(C_SB), blk(C_SC), blk(C_SV), (w, (w.shape[0], LANES), lambda j: (0, j))]


def dnconv_ops(pm, w):
    s = pm.shape[0]
    return [(pm, (s, LANES), lambda j: (0, C_DN // LANES + j)), (w, (w.shape[0], LANES), lambda j: (0, j))]


def ffn_ops(ug, uv, w):
    s = ug.shape[0]
    n_t = D_FF // LANES
    return [(ug, (s, LANES), lambda j: (0, j)), (uv, (s, LANES), lambda j: (0, j)),
            (w, (w.shape[0], LANES), lambda j: (0, j)), (w, (w.shape[0], LANES), lambda j: (0, n_t + j))]


def _col_out(s, width, dtype=F32):
    return ((s, width), dtype, (s, LANES), lambda j: (0, j), ())


def _col_cot(g):
    return (g, (g.shape[0], LANES), lambda j: (0, j))


def merge_ops(yp, pm):
    gate = lambda b: (pm, (256, D_MODEL), lambda i, b=b: (i, C_GATE // D_MODEL + b))
    return [_rows(yp[0]), _rows(yp[1]), _rows(yp[2]), gate(0), gate(1), gate(2)]


def ple_ops(gpre, pe, x):
    return [_rows(gpre), _rows(pe), _rows(x)]


def adam_call(name, w, g, m, v):
    shape = w.shape
    last = shape[-1]
    rows = w.size // last
    flat = lambda t: t.reshape(rows, last)
    tm = rows
    for cand in (512, 256, 128, 64, 32, 16, 8):
        if rows % cand == 0 and cand * last * 4 <= 2 * 1024 * 1024:
            tm = cand
            break
    spec = lambda t: (flat(t), (tm, last), lambda i: (i, 0))
    out = ((rows, last), F32, (tm, last), lambda i: (i, 0), ())
    res = tile_fwd(name, _adam_fn, (rows // tm,), [spec(w), spec(g), spec(m), spec(v)], [out, out, out])
    return [r.reshape(shape) for r in res]


def _adam_layers_fn(d, pids, w, m, v, g0, g1):
    g = jnp.where(pids[0] == 0, g0, g1)
    return (g,) + _adam_fn(d, pids, w, g, m, v)


def adam_layers(name, w, m, v, g0, g1):
    _, rows, cols = w.shape
    tm = _row_tile(rows, cols)
    n_t = rows // tm
    lay = lambda t: (t, (1, tm, cols), lambda l, i: (l, i, 0))
    ins = [lay(w), lay(m), lay(v), (g0, (tm, cols), lambda l, i: (i * (1 - l) + (n_t - 1) * l, 0)), (g1, (tm, cols), lambda l, i: (i * l, 0))]
    out = (w.shape, F32, (1, tm, cols), lambda l, i: (l, i, 0), ())
    return tile_fwd(name, _adam_layers_fn, (2, n_t), ins, [out, out, out, out])


def adam_w_in(name, w, m, v, g0, g1):
    rows, n_l, cols = w.shape

    def body(w_ref, m_ref, v_ref, g0_ref, g1_ref, g_out, d_out, m_out, v_out):
        step = 64

        def update(at):
            g0, g1 = g0_ref[at, :], g1_ref[at, :]
            layer = _iota((g0.shape[0], n_l, LANES), 1)
            g = jnp.where(layer == 0, g0[:, None, :], g1[:, None, :])
            delta, m2, v2 = _adam_fn(False, None, w_ref[at], g, m_ref[at], v_ref[at])
            for ref, val in ((g_out, g), (d_out, delta), (m_out, m2), (v_out, v2)):
                ref[at] = val

        def some_rows(i, carry):
            update(pl.ds(pl.multiple_of(i * step, step), step))
            return carry

        lax.fori_loop(0, rows // step, some_rows, 0)
        if rows % step:
            update(pl.ds(rows - rows % step, rows % step))

    both = pl.BlockSpec((rows, n_l, LANES), lambda j: (0, 0, j))
    one = pl.BlockSpec((rows, LANES), lambda j: (0, j))
    return pl.pallas_call(
        body, grid=(cols // LANES,), in_specs=[both, both, both, one, one], out_specs=[both] * 4,
        out_shape=[jax.ShapeDtypeStruct(w.shape, F32)] * 4, name=name, compiler_params=_cparams(1),
    )(w, m, v, g0, g1)


DN_GROUP = 4


def _dn_local_specs(rev_n=None):
    rows = DN_GROUP * DN_CHUNK
    idx = (lambda j: j) if rev_n is None else (lambda j: rev_n - 1 - j)
    return [pl.BlockSpec((rows, 3 * BRANCH), lambda j: (idx(j), 0)), pl.BlockSpec((rows, LANES), lambda j: (idx(j), 0)),
            pl.BlockSpec((DN_GROUP, DN_HEADS, DN_CHUNK), lambda j: (idx(j), 0, 0)), pl.BlockSpec((2, DN_HEADS), lambda j: (0, 0))]


def _dn_group_inputs(qkv, ps, a_rows, c):
    lo = c * DN_CHUNK
    heads = _split_heads(qkv[lo:lo + DN_CHUNK])
    return heads[0:4], heads[4:8], heads[8:12], ps[lo:lo + DN_CHUNK], a_rows[c]


def dn_local_fwd(name, dn_act, ps, a_rows, ad):
    s = dn_act.shape[0]
    n_c, n_g = s // DN_CHUNK, s // (DN_GROUP * DN_CHUNK)
    rows = DN_GROUP * DN_CHUNK

    def body(qkv_ref, ps_ref, ar_ref, ad_ref, u_ref, kc_ref, qd_ref, kd_ref, qk_ref, gl_ref):
        qkv, ps_v, a_rows_v, ad_v = qkv_ref[...], ps_ref[...], ar_ref[...], ad_ref[...]
        args = [[] for _ in range(8)]
        for c in range(DN_GROUP):
            q4, k4, v4, ps_c, ar_c = _dn_group_inputs(qkv, ps_v, a_rows_v, c)
            for lst, vals in zip(args, (q4, k4, v4) + _dn_gates(ps_c, ar_c, ad_v)):
                lst.extend(vals)
        everything = _dn_local(False, *args)
        for c in range(DN_GROUP):
            res = everything[c * DN_HEADS:(c + 1) * DN_HEADS]
            at = pl.ds(c * DN_CHUNK, DN_CHUNK)
            for ref, i in ((u_ref, 0), (kc_ref, 1), (qd_ref, 2), (kd_ref, 3)):
                ref[at, :] = jnp.concatenate([r[i] for r in res], axis=1)
            for h in range(DN_HEADS):
                qk_ref[c, h] = res[h][4]
            gl_ref[c] = _head_rows([r[5] for r in res])

    wide = pl.BlockSpec((rows, BRANCH), lambda j: (j, 0))
    return pl.pallas_call(
        body, grid=(n_g,), in_specs=_dn_local_specs(),
        out_specs=[wide, wide, wide, wide, pl.BlockSpec((DN_GROUP, DN_HEADS, DN_CHUNK, DN_CHUNK), lambda j: (j, 0, 0, 0)),
                   pl.BlockSpec((DN_GROUP, 8, LANES), lambda j: (j, 0, 0))],
        out_shape=[jax.ShapeDtypeStruct((s, BRANCH), F32)] * 4 + [jax.ShapeDtypeStruct((n_c, DN_HEADS, DN_CHUNK, DN_CHUNK), F32),
                                                                 jax.ShapeDtypeStruct((n_c, 8, LANES), F32)],
        name=name, compiler_params=_cparams(1),
    )(dn_act, ps, a_rows, ad)


def dn_local_bwd(name, dn_act, ps, a_rows, ad, cots):
    s = dn_act.shape[0]
    n_c, n_g = s // DN_CHUNK, s // (DN_GROUP * DN_CHUNK)
    rows = DN_GROUP * DN_CHUNK

    def body(qkv_ref, ps_ref, ar_ref, ad_ref, du_ref, dkc_ref, dqd_ref, dkd_ref, dqk_ref, dgl_ref, dqkv_ref, dps_ref, dar_ref, dad_ref):
        first = pl.program_id(0) == 0
        qkv, ps_v, a_rows_v, ad_v = qkv_ref[...], ps_ref[...], ar_ref[...], ad_ref[...]
        d_wide = [r[...] for r in (du_ref, dkc_ref, dqd_ref, dkd_ref)]
        qs, ks, vs, ps_cs, ar_cs, cot = [], [], [], [], [], []
        for c in range(DN_GROUP):
            q4, k4, v4, ps_c, ar_c = _dn_group_inputs(qkv, ps_v, a_rows_v, c)
            qs, ks, vs, ps_cs, ar_cs = qs + q4, ks + k4, vs + v4, ps_cs + [ps_c], ar_cs + [ar_c]
            lo = c * DN_CHUNK
            d_tiles = [_split_heads(t[lo:lo + DN_CHUNK]) for t in d_wide]
            d_gl = dgl_ref[c]
            cot += [(d_tiles[0][h], d_tiles[1][h], d_tiles[2][h], d_tiles[3][h], dqk_ref[c, h], _col(_row(d_gl, h), 0))
                    for h in range(DN_HEADS)]

        def f(qs, ks, vs, ps_cs, ar_cs, ad_v):
            gates = [[] for _ in range(5)]
            for ps_c, ar_c in zip(ps_cs, ar_cs):
                for lst, vals in zip(gates, _dn_gates(ps_c, ar_c, ad_v)):
                    lst.extend(vals)
            return _dn_local(True, qs, ks, vs, *gates)

        _, vjp = jax.vjp(f, qs, ks, vs, ps_cs, ar_cs, ad_v)
        d_q, d_k, d_v, d_ps, d_ar, d_ad = vjp(cot)
        for c in range(DN_GROUP):
            at, hs = pl.ds(c * DN_CHUNK, DN_CHUNK), slice(c * DN_HEADS, (c + 1) * DN_HEADS)
            dqkv_ref[at, :] = jnp.concatenate(d_q[hs] + d_k[hs] + d_v[hs], axis=1).astype(dqkv_ref.dtype)
            dps_ref[at, :] = d_ps[c]
            dar_ref[c] = d_ar[c]
        _store(dad_ref, d_ad, first)

    wide = pl.BlockSpec((rows, BRANCH), lambda j: (j, 0))
    specs = _dn_local_specs()
    return pl.pallas_call(
        body, grid=(n_g,),
        in_specs=specs + [wide, wide, wide, wide, pl.BlockSpec((DN_GROUP, DN_HEADS, DN_CHUNK, DN_CHUNK), lambda j: (j, 0, 0, 0)),
                          pl.BlockSpec((DN_GROUP, 8, LANES), lambda j: (j, 0, 0))],
        out_specs=specs,
        out_shape=[jax.ShapeDtypeStruct((s, 3 * BRANCH), F32), jax.ShapeDtypeStruct((s, LANES), F32),
                   jax.ShapeDtypeStruct((n_c, DN_HEADS, DN_CHUNK), F32), jax.ShapeDtypeStruct((2, DN_HEADS), F32)],
        name=name, compiler_params=_cparams(1),
    )(dn_act, ps, a_rows, ad, *cots)


def _dn_scan_specs(n_c, rev):
    idx = (lambda j: n_c - 1 - j) if rev else (lambda j: j)
    wide = pl.BlockSpec((DN_CHUNK, BRANCH), lambda j: (idx(j), 0))
    return [wide, wide, wide, wide, pl.BlockSpec((1, DN_HEADS, DN_CHUNK, DN_CHUNK), lambda j: (idx(j), 0, 0, 0)),
            pl.BlockSpec((1, 8, LANES), lambda j: (idx(j), 0, 0)), pl.BlockSpec((DN_CHUNK, BRANCH), lambda j: (idx(j), C_DZ // BRANCH)),
            pl.BlockSpec((1, DN_DH), lambda j: (0, 0))]


def _dn_scan_tiles(refs):
    u_ref, kc_ref, qd_ref, kd_ref, qk_ref, gl_ref, z_ref, g_ref = refs
    wide = [_split_heads(r[...]) for r in (u_ref, kc_ref, qd_ref, kd_ref)]
    gl = gl_ref[0]
    return [(wide[0][h], wide[1][h], wide[2][h], wide[3][h], qk_ref[0, h], _col(_row(gl, h), 0)) for h in range(DN_HEADS)], \
        _split_heads(z_ref[...].astype(F32)), g_ref[...]


def dn_scan_fwd(name, local, pm, gain):
    s = pm.shape[0]
    n_c = s // DN_CHUNK

    def body(*refs):
        y_ref, hist_ref, state = refs[8:]

        @pl.when(pl.program_id(0) == 0)
        def _():
            state[...] = jnp.zeros_like(state)

        hist_ref[0] = state[...]
        per_head, z4, gain_v = _dn_scan_tiles(refs[:8])
        ys, s_nexts = _dn_step(False, [state[h] for h in range(DN_HEADS)], per_head, z4, gain_v)
        for h in range(DN_HEADS):
            state[h] = s_nexts[h]
        y_ref[...] = jnp.concatenate(ys, axis=1).astype(y_ref.dtype)

    return pl.pallas_call(
        body, grid=(n_c,), in_specs=_dn_scan_specs(n_c, False),
        out_specs=[pl.BlockSpec((DN_CHUNK, BRANCH), lambda j: (j, 0)),
                   pl.BlockSpec((1, DN_HEADS, DN_DH, DN_DH), lambda j: (j, 0, 0, 0))],
        out_shape=[jax.ShapeDtypeStruct((s, BRANCH), BF16), jax.ShapeDtypeStruct((n_c, DN_HEADS, DN_DH, DN_DH), F32)],
        scratch_shapes=[pltpu.VMEM((DN_HEADS, DN_DH, DN_DH), F32)],
        name=name, compiler_params=_cparams(1),
    )(*local, pm, gain)


def dn_scan_bwd(name, local, pm, gain, hist, dy):
    s = pm.shape[0]
    n_c = s // DN_CHUNK

    def body(*refs):
        hist_ref, dy_ref = refs[8:10]
        du_ref, dkc_ref, dqd_ref, dkd_ref, dqk_ref, dgl_ref, dz_ref, dg_ref, d_state = refs[10:]
        first = pl.program_id(0) == 0

        @pl.when(first)
        def _():
            d_state[...] = jnp.zeros_like(d_state)

        per_head, z4, gain_v = _dn_scan_tiles(refs[:8])
        _, vjp = jax.vjp(functools.partial(_dn_step, True), [hist_ref[0, h] for h in range(DN_HEADS)], per_head, z4, gain_v)
        d_s, grads, d_z, d_gain = vjp((_split_heads(dy_ref[...].astype(F32)), [d_state[h] for h in range(DN_HEADS)]))
        for h in range(DN_HEADS):
            d_state[h] = d_s[h]
        for ref, i in ((du_ref, 0), (dkc_ref, 1), (dqd_ref, 2), (dkd_ref, 3)):
            ref[...] = jnp.concatenate([g[i] for g in grads], axis=1)
        dz_ref[...] = jnp.concatenate(d_z, axis=1).astype(dz_ref.dtype)
        for h in range(DN_HEADS):
            dqk_ref[0, h] = grads[h][4]
        dgl_ref[0] = _head_rows([g[5] for g in grads])
        _store(dg_ref, d_gain, first)

    rev = lambda j: n_c - 1 - j
    specs = _dn_scan_specs(n_c, True)
    return pl.pallas_call(
        body, grid=(n_c,),
        in_specs=specs + [pl.BlockSpec((1, DN_HEADS, DN_DH, DN_DH), lambda j: (rev(j), 0, 0, 0)),
                          pl.BlockSpec((DN_CHUNK, BRANCH), lambda j: (rev(j), 0))],
        out_specs=specs[:6] + [pl.BlockSpec((DN_CHUNK, BRANCH), lambda j: (rev(j), 0)), specs[7]],
        out_shape=[jax.ShapeDtypeStruct((s, BRANCH), F32)] * 4 + [
            jax.ShapeDtypeStruct((n_c, DN_HEADS, DN_CHUNK, DN_CHUNK), F32), jax.ShapeDtypeStruct((n_c, 8, LANES), F32),
            jax.ShapeDtypeStruct((s, BRANCH), BF16), jax.ShapeDtypeStruct((1, DN_DH), F32)],
        scratch_shapes=[pltpu.VMEM((DN_HEADS, DN_DH, DN_DH), F32)],
        name=name, compiler_params=_cparams(1),
    )(*local, pm, gain, hist, dy)


def _seq_layouts(cols, s):
    return cols.T.reshape(cols.shape[1], s // LANES, LANES)


def layer_fwd(li, x, p, w, more_weights=None):
    s = x.shape[0]
    n = lambda t: f"{t}_l{li}"
    h = rms_fwd(n("rms_mix"), x, w["g_mix"])
    pm = mm(n("in_main"), h, w["in_main"], "nn")
    ps = mm(n("in_small"), h, w["in_small"], "nn")
    qn, kn = fox_prep_fwd(n("fox_prep"), pm, w["gq"], w["gk"])
    f_t = _seq_layouts(ps[:, 0:8], s)
    cum = fox_gate_fwd(n("fox_gate"), f_t, w["b_f"])
    cum_c, cum_r = cum.reshape(8, s, 1), cum.reshape(8, 1, s)
    y_fox = fox_attn_fwd(n("fox_attn"), qn, kn, pm, cum_c, cum_r)
    y_sc = tile_fwd(n("sconv"), _sconv_fn, (BRANCH // LANES,), sconv_ops(pm, w["sc_conv_w"]), [_col_out(s, BRANCH, BF16)])[0]
    dn_act = tile_fwd(n("dnconv"), _dnconv_fn, (3 * BRANCH // LANES,), dnconv_ops(pm, w["dn_conv_w"]), [_col_out(s, 3 * BRANCH)])[0]
    a_rows = ps[:, 12:16].reshape(s // DN_CHUNK, DN_CHUNK, DN_HEADS).transpose(0, 2, 1)
    dn_local = dn_local_fwd(n("dn_local"), dn_act, ps, a_rows, w["ad"])
    y_dn, hist = dn_scan_fwd(n("dn_scan"), dn_local, pm, w["dn_gain"])
    ys = (y_fox, y_sc, y_dn)
    if more_weights is not None:
        w = {**w, **more_weights(y_dn)}
    yp = [mm(n(f"branch{b}"), ys[b], w["branch"][b], "nn", blocks=(0, N_CHIPS)) for b in range(3)]
    merged = tile_fwd(n("merge"), _merge_fn, (s // 256,), merge_ops(yp, pm), [((s, D_MODEL), BF16, (256, D_MODEL), lambda i: (i, 0), ())])[0]
    x1 = mm(n("w_o"), merged, w["o"], "nn", add=x)
    h2 = rms_fwd(n("rms_ffn"), x1, w["g_ffn"])
    ug = mm(n("up_g"), h2, w["up"], "nn", blocks=(0, 2))
    uv = mm(n("up_v"), h2, w["up"], "nn", blocks=(2, 2))
    act = tile_fwd(n("ffn_act"), _ffn_act_fn, (D_FF // LANES,), ffn_ops(ug, uv, w["ffn_conv_w"]), [_col_out(s, D_FF, BF16)])[0]
    x2 = mm(n("down"), act, w["down"], "nn", add=x1)
    h3 = rms_fwd(n("rms_ple"), x2, w["g_ple"])
    gpre = mm(n("ple_gate"), h3, w["pg"], "nn")
    pe = mm(n("ple_emb"), p, w["ple"], "nn", blocks=(0, N_CHIPS))
    x3 = tile_fwd(n("ple"), _ple_fn, (s // 256,), ple_ops(gpre, pe, x2), [((s, D_MODEL), F32, (256, D_MODEL), lambda i: (i, 0), ())])[0]
    saved = dict(x=x, h=h, pm=pm, ps=ps, qn=qn, kn=kn, f_t=f_t, cum_c=cum_c, cum_r=cum_r, ys=ys, dn_act=dn_act, dn_local=dn_local,
                 a_rows=a_rows, hist=hist, yp=yp, merged=merged, x1=x1, h2=h2, ug=ug, uv=uv, act=act, x2=x2, h3=h3,
                 gpre=gpre, pe=pe, p=p)
    return x3, saved, w


def hang_on(w, token):
    zero = token[0, 0]
    small = ("g_mix", "g_ffn", "g_ple", "gq", "gk", "b_f", "ad", "dn_gain", "sc_conv_w", "dn_conv_w", "ffn_conv_w")
    return {**w, **{k: w[k] + zero for k in small}}


def layer_bwd(li, dx3, sv, w, hooks=None):
    hooks = hooks or {}

    def stage(key, after, w):
        return hang_on(w, hooks[key](after, g)) if key in hooks else w

    s = dx3.shape[0]
    n = lambda t: f"{t}_l{li}"
    g = {}
    col_own = lambda width: ((s, width), (s, LANES), lambda j: (0, j))
    d_gpre, d_pe = tile_bwd(n("ple_bwd"), _ple_fn, (s // 256,), ple_ops(sv["gpre"], sv["pe"], sv["x2"]), [_rows(dx3)],
                            [(0, (), None, BF16), (1, (), None, BF16)])
    g["w_ple"] = mm(n("d_w_ple"), sv["p"], d_pe, "tn", blocks=(0, N_CHIPS))
    g["w_ple_gate"] = mm(n("d_w_pg"), sv["h3"], d_gpre, "tn").reshape(N_CHIPS, -1, D_MODEL)
    dh3 = mm(n("d_h3"), d_gpre, w["pg"], "nt")
    dx2, d_g_ple = rms_bwd(n("rms_ple_bwd"), sv["x2"], w["g_ple"], dh3, dx3)
    dact = mm(n("d_act"), dx2, w["down"], "nt")
    g["w_down"] = mm(n("d_w_down"), sv["act"], dx2, "tn").reshape(N_CHIPS, -1, D_MODEL)
    taps_own = ((w["ffn_conv_w"].shape[0], D_FF), (w["ffn_conv_w"].shape[0], LANES), lambda j: (0, j))
    d_ug, d_uv, d_fw_g, d_fw_v = tile_bwd(n("ffn_act_bwd"), _ffn_act_fn, (D_FF // LANES,), ffn_ops(sv["ug"], sv["uv"], w["ffn_conv_w"]),
                                          [_col_cot(dact)], [(0, (), None, BF16), (1, (), None, BF16), (2, (), taps_own), (3, (), taps_own)])
    g["ffn_conv_w"] = jnp.concatenate([d_fw_g, d_fw_v], axis=1)
    g["w_up"] = jnp.concatenate([mm(n("d_w_up_g"), sv["h2"], d_ug, "tn", blocks=(0, 2)), mm(n("d_w_up_v"), sv["h2"], d_uv, "tn", blocks=(0, 2))])
    dh2 = mm(n("d_h2_v"), d_uv, w["up"], "nt", blocks=(2, 2), add=mm(n("d_h2_g"), d_ug, w["up"], "nt", blocks=(0, 2)))
    dx1, d_g_ffn = rms_bwd(n("rms_ffn_bwd"), sv["x1"], w["g_ffn"], dh2, dx2)
    w = stage("mid", dx1, w)
    dmerged = mm(n("d_merged"), dx1, w["o"], "nt")
    g["w_o"] = mm(n("d_w_o"), sv["merged"], dx1, "tn").reshape(N_CHIPS, -1, D_MODEL)
    gate_own = ((s, D_MODEL), (256, D_MODEL), lambda i: (i, 0))
    d_yp0, d_yp1, d_yp2, d_g0, d_g1, d_g2 = tile_bwd(
        n("merge_bwd"), _merge_fn, (s // 256,), merge_ops(sv["yp"], sv["pm"]), [_rows(dmerged)],
        [(0, (), None, BF16), (1, (), None, BF16), (2, (), None, BF16), (3, (), gate_own, BF16), (4, (), gate_own, BF16), (5, (), gate_own, BF16)])
    d_yp = (d_yp0, d_yp1, d_yp2)
    g["w_branch"] = jnp.concatenate([mm(n(f"d_w_branch{b}"), sv["ys"][b], d_yp[b], "tn", blocks=(0, N_CHIPS)) for b in range(3)], axis=1)
    d_ys = [mm(n(f"d_y{b}"), d_yp[b], w["branch"][b], "nt", blocks=(0, N_CHIPS)) for b in range(3)]
    w = stage("late", d_ys[2], w)
    *d_local, d_z, d_dngain = dn_scan_bwd(n("dn_scan_bwd"), sv["dn_local"], sv["pm"], w["dn_gain"], sv["hist"], d_ys[2])
    d_dnact, d_ps_dn, d_arows, d_ad = dn_local_bwd(n("dn_local_bwd"), sv["dn_act"], sv["ps"], sv["a_rows"], w["ad"], d_local)
    g["ad"], g["dn_norm_gain"] = d_ad, d_dngain[0]
    d_dnqkv, g["dn_conv_w"] = tile_bwd(n("dnconv_bwd"), _dnconv_fn, (3 * BRANCH // LANES,), dnconv_ops(sv["pm"], w["dn_conv_w"]),
                                       [_col_cot(d_dnact)], [(0, (), col_own(3 * BRANCH), BF16), (1, ())])
    d_sb, d_sc, d_sv, g["sc_conv_w"] = tile_bwd(n("sconv_bwd"), _sconv_fn, (BRANCH // LANES,), sconv_ops(sv["pm"], w["sc_conv_w"]), [_col_cot(d_ys[1])],
                                                [(0, (), col_own(BRANCH), BF16), (1, (), col_own(BRANCH), BF16), (2, (), col_own(BRANCH), BF16), (3, ())])
    w = stage("last", d_dnqkv, w)
    d_qn, d_kn, d_fv, d_cum = fox_attn_bwd(n("fox_attn_bwd"), sv["qn"], sv["kn"], sv["pm"], sv["cum_c"], sv["cum_r"], d_ys[0])
    d_ft, d_bf = fox_gate_bwd(n("fox_gate_bwd"), sv["f_t"], w["b_f"], d_cum.reshape(8, s // LANES, LANES))
    g["b_fox_f"] = d_bf.reshape(8)
    d_fq, d_fk, d_gq, d_gk = fox_prep_bwd(n("fox_prep_bwd"), sv["pm"], w["gq"], w["gk"], d_qn, d_kn)
    g["fox_q_gain"] = d_gq[0, :FOX_DH] + d_gq[0, FOX_DH:]
    g["fox_k_gain"] = d_gk[0, :FOX_DH] + d_gk[0, FOX_DH:]
    d_pm = jnp.concatenate([d_fq, d_fk, d_fv.astype(BF16), d_sb, d_sc, d_sv, d_dnqkv, d_z, d_g0, d_g1, d_g2], axis=1)
    d_a_cols = d_arows.transpose(0, 2, 1).reshape(s, DN_HEADS)
    d_f_cols = d_ft.reshape(8, s).T
    d_ps = d_ps_dn + jnp.concatenate([d_f_cols, jnp.zeros((s, 4), F32), d_a_cols, jnp.zeros((s, LANES - 16), F32)], axis=1)
    g["w_in"] = chip_blocks_w_in(mm(n("d_w_in_main"), d_pm, sv["h"], "tn"), mm(n("d_w_in_small"), d_ps, sv["h"], "tn"))
    w = stage("w_in", g["w_in"], w)
    dh = mm(n("d_h_small"), d_ps, w["in_small"], "nt", add=mm(n("d_h_main"), d_pm, w["in_main"], "nt"))
    dx, d_g_mix = rms_bwd(n("rms_mix_bwd"), sv["x"], w["g_mix"], dh, dx1)
    g["g_mix"], g["g_ffn"], g["g_ple"] = d_g_mix[0], d_g_ffn[0], d_g_ple[0]
    return dx, g


IN_SHARD = 2052
MAIN_RANGES = ((0, 1536), (1544, 3080), (3080, 4616), (4624, 5136), (5136, 8208))
SMALL_RANGES = ((1536, 1544), (4616, 4620), (4620, 4624))


def _from_chip_blocks(blocks, ranges):
    parts = []
    for lo, hi in ranges:
        for k in range(N_CHIPS):
            a0, a1 = max(lo, k * IN_SHARD), min(hi, (k + 1) * IN_SHARD)
            if a0 < a1:
                parts.append(blocks[k][:, a0 - k * IN_SHARD:a1 - k * IN_SHARD])
    return parts


def split_w_in(blocks):
    main = jnp.concatenate(_from_chip_blocks(blocks, MAIN_RANGES), axis=1)
    pad = jnp.zeros((blocks.shape[1], LANES - 16), blocks.dtype)
    return main, jnp.concatenate(_from_chip_blocks(blocks, SMALL_RANGES) + [pad], axis=1)


def chip_blocks_w_in(main, small):
    ranges = sorted([(lo, hi, "m") for lo, hi in MAIN_RANGES] + [(lo, hi, "s") for lo, hi in SMALL_RANGES])
    offs, m_off, s_off = {}, 0, 0
    for lo, hi in MAIN_RANGES:
        offs[lo] = m_off
        m_off += hi - lo
    for lo, hi in SMALL_RANGES:
        offs[lo] = s_off
        s_off += hi - lo
    blocks = []
    for k in range(N_CHIPS):
        parts = []
        for lo, hi, src in ranges:
            a0, a1 = max(lo, k * IN_SHARD), min(hi, (k + 1) * IN_SHARD)
            if a0 < a1:
                arr = main if src == "m" else small
                parts.append(arr[offs[lo] + a0 - lo:offs[lo] + a1 - lo])
        blocks.append(jnp.concatenate(parts, axis=0))
    return jnp.stack(blocks)


def later_weights(got):
    g_branch, g_o, g_up, g_down, g_pg, g_ple = got
    branch = g_branch.reshape(N_CHIPS, 3, BRANCH, -1)
    return dict(branch=[branch[:, b] for b in range(3)], o=g_o.reshape(D_MODEL, D_MODEL), up=g_up,
                down=g_down.reshape(D_FF, D_MODEL), pg=g_pg.reshape(D_MODEL, D_MODEL), ple=g_ple)


def layer_weights(li, got, conv, a):
    main, small = split_w_in(got[0])
    tile2 = lambda v: jnp.concatenate([v, v])[None, :]
    rest = later_weights(got[1:]) if len(got) > 1 else {}
    return dict(
        in_main=main, in_small=small, **rest,
        g_mix=a["g_mix"][li][None, :], g_ffn=a["g_ffn"][li][None, :], g_ple=a["g_ple"][li][None, :],
        gq=tile2(a["fox_q_gain"][li]), gk=tile2(a["fox_k_gain"][li]), b_f=a["b_fox_f"][li].reshape(8, 1, 1),
        ad=jnp.stack([a["dn_a_log"][li], a["dn_dt_bias"][li]]), dn_gain=a["dn_norm_gain"][li][None, :],
        sc_conv_w=conv["sc_conv_w"][li], dn_conv_w=conv["dn_conv_w"][li], ffn_conv_w=conv["ffn_conv_w"][li])


def pack_rows(arrs, dtype):
    flat = jnp.concatenate([t.reshape(-1).astype(dtype) for t in arrs])
    pad = (-flat.shape[0]) % (8 * LANES)
    if pad:
        flat = jnp.concatenate([flat, jnp.zeros((pad,), dtype)])
    return flat.reshape(-1, LANES)


def unpack_rows(buf, shapes):
    flat = buf.reshape(-1)
    out, off = [], 0
    for shp in shapes:
        size = 1
        for dim in shp:
            size *= dim
        out.append(flat[off:off + size].reshape(shp))
        off += size
    return out


def chip_shard(t, axis, k):
    width = t.shape[axis] // N_CHIPS
    return lax.slice_in_dim(t, k * width, (k + 1) * width, axis=axis)


ANY = pl.BlockSpec(memory_space=pl.ANY)


def _position():
    x, y, c = lax.axis_index("x"), lax.axis_index("y"), lax.axis_index("c")
    return x, y, c, [(1 - x, y), (x, 1 - y), (1 - x, 1 - y)]


def gather_small(name, block):
    m_per, n = block.shape

    def body(x_ref, out_ref, token, send_sems, recv_sems, local_sem):
        token[...] = jnp.zeros_like(token)
        x, y, c, chips = _position()
        me, sibling = (x, y, c), (x, y, 1 - c)

        def rows(px, py, pc):
            return out_ref.at[pl.ds((4 * px + 2 * py + pc) * m_per, m_per), :]

        def copy(k, blk, to, src=None):
            return pltpu.make_async_remote_copy(src_ref=rows(*blk) if src is None else src, dst_ref=rows(*blk),
                                                send_sem=send_sems.at[k], recv_sem=recv_sems.at[k], device_id=to, device_id_type=MESH)

        mine = pltpu.make_async_copy(x_ref, rows(*me), local_sem)
        mine.start()
        first = [copy(0, me, sibling, src=x_ref)] + [copy(1 + j, me, (*chip, c), src=x_ref) for j, chip in enumerate(chips)]
        for cp in first:
            cp.start()
        passed = [copy(4 + j, (*chip, c), sibling) for j, chip in enumerate(chips)]
        for j, chip in enumerate(chips):
            copy(1 + j, (*chip, c), me).wait_recv()
            passed[j].start()
        copy(0, sibling, me).wait_recv()
        for j, chip in enumerate(chips):
            copy(4 + j, (*chip, 1 - c), me).wait_recv()
        for cp in first + passed:
            cp.wait_send()
        mine.wait()

    in_vmem = pl.BlockSpec(memory_space=pltpu.VMEM)
    return pl.pallas_call(
        body, out_shape=[jax.ShapeDtypeStruct((8 * m_per, n), block.dtype), jax.ShapeDtypeStruct((8, LANES), F32)],
        in_specs=[in_vmem], out_specs=[in_vmem, in_vmem],
        scratch_shapes=[pltpu.SemaphoreType.DMA((7,)), pltpu.SemaphoreType.DMA((7,)), pltpu.SemaphoreType.DMA],
        name=name, compiler_params=pltpu.CompilerParams(vmem_limit_bytes=VMEM_LIMIT),
    )(block)


def _sems(n):
    return [pltpu.SemaphoreType.DMA((n,)), pltpu.SemaphoreType.DMA((n,))]


def _split_cols(rows):
    return (rows // 2) % 16 != 0


def _half(ref, which, lead=()):
    rows, cols = ref.shape[-2:]
    if _split_cols(rows):
        return ref.at[(*lead, slice(None), pl.ds(which * (cols // 2), cols // 2))]
    return ref.at[(*lead, pl.ds(which * (rows // 2), rows // 2), slice(None))]


def _half_shape(rows, cols):
    return (rows, cols // 2) if _split_cols(rows) else (rows // 2, cols)


def gather_layer(name, shards):
    n_w = len(shards)

    def body(*refs):
        ins, outs = refs[:n_w], refs[n_w:2 * n_w]
        token, send_sems, recv_sems = refs[2 * n_w:]
        token[...] = jnp.zeros_like(token)
        x, y, c, chips = _position()
        sibling = (x, y, 1 - c)

        def part(w, px, py, pc):
            return _half(outs[w], pc, (2 * px + py,))

        def copy(k, w, blk, to, src=None):
            return pltpu.make_async_remote_copy(src_ref=part(w, *blk) if src is None else src, dst_ref=part(w, *blk),
                                                send_sem=send_sems.at[k], recv_sem=recv_sems.at[k], device_id=to, device_id_type=MESH)

        pairs = [(w, j, chip) for w in range(n_w) for j, chip in enumerate(chips)]
        first = [copy(3 * w + j, w, (x, y, c), (*chip, c), src=_half(ins[w], c)) for w, j, chip in pairs]
        for cp in first:
            cp.start()
        passed = [copy(3 * n_w + 3 * w + j, w, (*chip, c), sibling) for w, j, chip in pairs]
        for (w, j, chip), fwd in zip(pairs, passed):
            copy(3 * w + j, w, (*chip, c), (x, y, c)).wait_recv()
            fwd.start()
        for w, j, chip in pairs:
            copy(3 * n_w + 3 * w + j, w, (*chip, 1 - c), (x, y, c)).wait_recv()
        for cp in first + passed:
            cp.wait_send()

    out = pl.pallas_call(
        body, out_shape=[jax.ShapeDtypeStruct((N_CHIPS,) + s.shape, s.dtype) for s in shards] + [jax.ShapeDtypeStruct((8, LANES), F32)],
        in_specs=[ANY] * n_w, out_specs=[ANY] * n_w + [pl.BlockSpec(memory_space=pltpu.VMEM)], scratch_shapes=_sems(6 * n_w), name=name,
    )(*shards)
    return out[:n_w], out[n_w]


def swap_halves(name, grads):
    n_w = len(grads)

    def body(*refs):
        ins, outs = refs[:n_w], refs[n_w:2 * n_w]
        send_sems, recv_sems = refs[2 * n_w:]
        x, y, c, _ = _position()
        cps = [pltpu.make_async_remote_copy(src_ref=_half(ins[w], 1 - c, (slice(None),)), dst_ref=outs[w],
                                            send_sem=send_sems.at[w], recv_sem=recv_sems.at[w], device_id=(x, y, 1 - c),
                                            device_id_type=MESH) for w in range(n_w)]
        for cp in cps:
            cp.start()
        for cp in cps:
            cp.wait()

    return pl.pallas_call(
        body, out_shape=[jax.ShapeDtypeStruct((N_CHIPS,) + _half_shape(*g.shape[1:]), g.dtype) for g in grads],
        in_specs=[ANY] * n_w, out_specs=[ANY] * n_w, scratch_shapes=_sems(n_w), name=name,
    )(*grads)


def scatter_chips(name, partials):
    n_w = len(partials)

    def body(*refs):
        ins, outs = refs[:n_w], refs[n_w:2 * n_w]
        send_sems, recv_sems = refs[2 * n_w:]
        x, y, c, chips = _position()
        cps = [pltpu.make_async_remote_copy(src_ref=ins[w].at[2 * cx + cy], dst_ref=outs[w].at[j], send_sem=send_sems.at[3 * w + j],
                                            recv_sem=recv_sems.at[3 * w + j], device_id=(cx, cy, c), device_id_type=MESH)
               for w in range(n_w) for j, (cx, cy) in enumerate(chips)]
        for cp in cps:
            cp.start()
        for cp in cps:
            cp.wait()

    return pl.pallas_call(
        body, out_shape=[jax.ShapeDtypeStruct((3,) + p.shape[1:], p.dtype) for p in partials],
        in_specs=[ANY] * n_w, out_specs=[ANY] * n_w, scratch_shapes=_sems(3 * n_w), name=name,
    )(*partials)


def share_halves(name, bufs):
    n_w = len(bufs)

    def body(*refs):
        outs = refs[n_w:2 * n_w]
        send_sems, recv_sems = refs[2 * n_w:]
        x, y, c, _ = _position()

        def copy(w, pc):
            half = _half(outs[w], pc)
            return pltpu.make_async_remote_copy(src_ref=half, dst_ref=half, send_sem=send_sems.at[w], recv_sem=recv_sems.at[w],
                                                device_id=(x, y, 1 - c), device_id_type=MESH)

        for w in range(n_w):
            copy(w, c).start()
        for w in range(n_w):
            copy(w, 1 - c).wait_recv()
            copy(w, c).wait_send()

    return pl.pallas_call(
        body, out_shape=[jax.ShapeDtypeStruct(b.shape, b.dtype) for b in bufs], in_specs=[ANY] * n_w, out_specs=[ANY] * n_w,
        input_output_aliases={w: w for w in range(n_w)}, scratch_shapes=_sems(n_w), name=name,
    )(*bufs)


HBM = pl.BlockSpec(memory_space=pltpu.HBM)
SEM = pl.BlockSpec(memory_space=pltpu.SEMAPHORE)
EFFECT = pltpu.SideEffectType.DATAFLOW_SIDE_EFFECTING


def _exchange_copies(kind, srcs, lands):
    x, y, c, chips = _position()
    out = []
    for src, land in zip(srcs, lands):
        if kind == "swap":
            out.append((_half(src, 1 - c, (slice(None),)), land, (x, y, 1 - c)))
            continue
        for j, (cx, cy) in enumerate(chips):
            if kind == "gather":
                out.append((src, land.at[2 * x + y], (cx, cy, c)))
            else:
                out.append((src.at[2 * cx + cy], land.at[j], (cx, cy, c)))
    return out


def _land_shapes(kind, srcs):
    if kind == "gather":
        return [(N_CHIPS,) + s.shape for s in srcs]
    if kind == "swap":
        return [(N_CHIPS,) + _half_shape(*s.shape[1:]) for s in srcs]
    return [(3,) + s.shape[1:] for s in srcs]


def exchange_start(name, kind, srcs):
    n_w = len(srcs)
    shapes = _land_shapes(kind, srcs)
    n_sem = n_w if kind == "swap" else 3 * n_w

    def body(*refs):
        ins, lands = refs[:n_w], refs[n_w:2 * n_w]
        send_sems, recv_sems = refs[2 * n_w:2 * n_w + 2]
        token = refs[-1]
        for i, (src, dst, dev) in enumerate(_exchange_copies(kind, ins, lands)):
            pltpu.make_async_remote_copy(src_ref=src, dst_ref=dst, send_sem=send_sems.at[i], recv_sem=recv_sems.at[i],
                                         device_id=dev, device_id_type=MESH).start()
        token[...] = jnp.zeros_like(token)

    out = pl.pallas_call(
        body, name=name,
        out_shape=(pltpu.SemaphoreType.DMA((n_sem,)), pltpu.SemaphoreType.DMA((n_sem,)),
                   *[pltpu.HBM(s.shape, s.dtype) for s in srcs], *[pltpu.HBM(shp, s.dtype) for shp, s in zip(shapes, srcs)],
                   jax.ShapeDtypeStruct((8, LANES), F32)),
        in_specs=(HBM,) * (2 * n_w), out_specs=(SEM, SEM) + (HBM,) * (2 * n_w) + (pl.BlockSpec(memory_space=pltpu.VMEM),),
        input_output_aliases={i: 2 + i for i in range(2 * n_w)},
        compiler_params=pltpu.CompilerParams(has_side_effects=EFFECT),
    )(*[pltpu.with_memory_space_constraint(s, pltpu.HBM) for s in srcs],
      *[pltpu.with_memory_space_constraint(lax.empty(shp, s.dtype), pltpu.HBM) for shp, s in zip(shapes, srcs)])
    return (kind, n_w, out[:-1]), out[-1]


def exchange_wait(name, handle, after):
    kind, n_w, (send_sems, recv_sems, *thru) = handle
    n_sem = n_w if kind == "swap" else 3 * n_w

    def body(*refs):
        ins, lands = refs[:n_w], refs[n_w:2 * n_w]
        send_sems, recv_sems = refs[2 * n_w:2 * n_w + 2]
        for i, (src, dst, dev) in enumerate(_exchange_copies(kind, ins, lands)):
            cp = pltpu.make_async_remote_copy(src_ref=src, dst_ref=dst, send_sem=send_sems.at[i], recv_sem=recv_sems.at[i],
                                              device_id=dev, device_id_type=MESH)
            cp.wait_send()
            cp.wait_recv()

    out = pl.pallas_call(
        body, name=name, out_shape=tuple(pltpu.HBM(t.shape, t.dtype) for t in thru),
        in_specs=(HBM,) * (2 * n_w) + (SEM, SEM, pl.BlockSpec(memory_space=pl.ANY)), out_specs=(HBM,) * (2 * n_w),
        input_output_aliases={i: i for i in range(2 * n_w)},
        compiler_params=pltpu.CompilerParams(has_side_effects=EFFECT),
    )(*thru, send_sems, recv_sems, after)
    return list(out[:n_w]), list(out[n_w:])


def _row_tile(rows, cols):
    best = rows
    if rows * cols * 4 <= 1024 * 1024:
        return rows
    for t in range(16, rows, 16):
        if rows % t == 0 and t * cols * 4 <= 1024 * 1024:
            best = t
    return best


def pair_sum(name, pos, grad, from_sibling):
    _, rows, cols = grad.shape
    h_rows, h_cols = _half_shape(rows, cols)
    tr = _row_tile(h_rows, h_cols)
    n_t = h_rows // tr

    def body(pos_ref, g_ref, s_ref, b_ref, f_ref):
        tot = g_ref[...] + s_ref[...]
        b_ref[...] = tot.astype(BF16)

        @pl.when(pl.program_id(1) == pos_ref[1])
        def _():
            f_ref[...] = tot[0]

    blk = pl.BlockSpec((1, tr, h_cols), lambda i, k, pos: (k, i, 0))
    if _split_cols(rows):
        mine = pl.BlockSpec((1, tr, h_cols), lambda i, k, pos: (k, i, pos[0]))
    else:
        mine = pl.BlockSpec((1, tr, h_cols), lambda i, k, pos: (k, pos[0] * n_t + i, 0))
    return pl.pallas_call(
        body, grid_spec=pltpu.PrefetchScalarGridSpec(
            num_scalar_prefetch=1, grid=(n_t, N_CHIPS), in_specs=[mine, blk],
            out_specs=[blk, pl.BlockSpec((tr, h_cols), lambda i, k, pos: (i, 0))]),
        out_shape=[jax.ShapeDtypeStruct((N_CHIPS, h_rows, h_cols), BF16), jax.ShapeDtypeStruct((h_rows, h_cols), F32)],
        name=name, compiler_params=_cparams(2),
    )(pos, grad, from_sibling)


def chip_sum(name, pos, own, landed, split_cols):
    half, cols = own.shape
    tr = _row_tile(half, cols)
    n_t = half // tr

    def body(pos_ref, p_ref, l_ref, o_ref):
        o_ref[...] = ((p_ref[...] + l_ref[0].astype(F32)) + l_ref[1].astype(F32)) + l_ref[2].astype(F32)

    if split_cols:
        out_spec, out_shape = pl.BlockSpec((tr, cols), lambda i, pos: (i, pos[0])), (half, 2 * cols)
    else:
        out_spec, out_shape = pl.BlockSpec((tr, cols), lambda i, pos: (pos[0] * n_t + i, 0)), (2 * half, cols)
    return pl.pallas_call(
        body, grid_spec=pltpu.PrefetchScalarGridSpec(
            num_scalar_prefetch=1, grid=(n_t,),
            in_specs=[pl.BlockSpec((tr, cols), lambda i, pos: (i, 0)), pl.BlockSpec((3, tr, cols), lambda i, pos: (0, i, 0))],
            out_specs=out_spec),
        out_shape=jax.ShapeDtypeStruct(out_shape, F32), name=name, compiler_params=_cparams(1),
    )(pos, own, landed)


def reduce_scatter_layer(tag, pos, grads):
    n = lambda t: f"{t}_{tag}"
    from_sibling = swap_halves(n("swap_halves"), grads)
    sums = [pair_sum(n(f"pair_sum{w}"), pos, g, s) for w, (g, s) in enumerate(zip(grads, from_sibling))]
    landed = scatter_chips(n("scatter_chips"), [b for b, _ in sums])
    halves = [chip_sum(n(f"chip_sum{w}"), pos, own, l, _split_cols(g.shape[1])) for w, ((_, own), l, g) in enumerate(zip(sums, landed, grads))]
    return share_halves(n("share_halves"), halves)


class OverlappedReduceScatter:
    def __init__(self, tag, pos, grads):
        self.n = lambda t: f"{t}_{tag}"
        self.pos, self.grads = pos, grads
        self.swap, self.token = exchange_start(self.n("swap_start"), "swap", grads)

    def middle(self, after):
        self.grads, from_sibling = exchange_wait(self.n("swap_wait"), self.swap, after)
        self.sums = [pair_sum(self.n(f"pair_sum{w}"), self.pos, g, s) for w, (g, s) in enumerate(zip(self.grads, from_sibling))]
        self.scatter, self.token = exchange_start(self.n("scatter_start"), "scatter", [b for b, _ in self.sums])

    def finish(self, after):
        _, landed = exchange_wait(self.n("scatter_wait"), self.scatter, after)
        halves = [chip_sum(self.n(f"chip_sum{w}"), self.pos, own, l, _split_cols(g.shape[1]))
                  for w, ((_, own), l, g) in enumerate(zip(self.sums, landed, self.grads))]
        return share_halves(self.n("share_halves"), halves)


def sum_devices(gathered):
    m_per = gathered.shape[0] // 8

    def body(g_ref, o_ref):
        tot = g_ref[pl.ds(0, m_per), :]
        for dev in range(1, 8):
            tot = tot + g_ref[pl.ds(dev * m_per, m_per), :]
        o_ref[...] = tot

    return pl.pallas_call(
        body, out_shape=jax.ShapeDtypeStruct((m_per, gathered.shape[1]), F32),
        in_specs=[pl.BlockSpec(memory_space=pltpu.VMEM)], out_specs=pl.BlockSpec(memory_space=pltpu.VMEM), name="sum_devices",
    )(gathered)


def kernel(x, p, g_mix, w_in, b_fox_f, fox_q_gain, fox_k_gain, sc_conv_w, dn_conv_w, dn_a_log, dn_dt_bias, dn_norm_gain, w_branch, w_o, g_ffn, w_up, ffn_conv_w, w_down, g_ple, w_ple_gate, w_ple, loss_target, m_g_mix, m_w_in, m_b_fox_f, m_fox_q_gain, m_fox_k_gain, m_sc_conv_w, m_dn_conv_w, m_dn_a_log, m_dn_dt_bias, m_dn_norm_gain, m_w_branch, m_w_o, m_g_ffn, m_w_up, m_ffn_conv_w, m_w_down, m_g_ple, m_w_ple_gate, m_w_ple, v_g_mix, v_w_in, v_b_fox_f, v_fox_q_gain, v_fox_k_gain, v_sc_conv_w, v_dn_conv_w, v_dn_a_log, v_dn_dt_bias, v_dn_norm_gain, v_w_branch, v_w_o, v_g_ffn, v_w_up, v_ffn_conv_w, v_w_down, v_g_ple, v_w_ple_gate, v_w_ple):
    a = dict(g_mix=g_mix, w_in=w_in, b_fox_f=b_fox_f, fox_q_gain=fox_q_gain, fox_k_gain=fox_k_gain, sc_conv_w=sc_conv_w,
             dn_conv_w=dn_conv_w, dn_a_log=dn_a_log, dn_dt_bias=dn_dt_bias, dn_norm_gain=dn_norm_gain, w_branch=w_branch, w_o=w_o,
             g_ffn=g_ffn, w_up=w_up, ffn_conv_w=ffn_conv_w, w_down=w_down, g_ple=g_ple, w_ple_gate=w_ple_gate, w_ple=w_ple)
    mom = dict(g_mix=m_g_mix, w_in=m_w_in, b_fox_f=m_b_fox_f, fox_q_gain=m_fox_q_gain, fox_k_gain=m_fox_k_gain, sc_conv_w=m_sc_conv_w,
               dn_conv_w=m_dn_conv_w, dn_a_log=m_dn_a_log, dn_dt_bias=m_dn_dt_bias, dn_norm_gain=m_dn_norm_gain, w_branch=m_w_branch,
               w_o=m_w_o, g_ffn=m_g_ffn, w_up=m_w_up, ffn_conv_w=m_ffn_conv_w, w_down=m_w_down, g_ple=m_g_ple, w_ple_gate=m_w_ple_gate,
               w_ple=m_w_ple)
    var = dict(g_mix=v_g_mix, w_in=v_w_in, b_fox_f=v_b_fox_f, fox_q_gain=v_fox_q_gain, fox_k_gain=v_fox_k_gain, sc_conv_w=v_sc_conv_w,
               dn_conv_w=v_dn_conv_w, dn_a_log=v_dn_a_log, dn_dt_bias=v_dn_dt_bias, dn_norm_gain=v_dn_norm_gain, w_branch=v_w_branch,
               w_o=v_w_o, g_ffn=v_g_ffn, w_up=v_w_up, ffn_conv_w=v_ffn_conv_w, w_down=v_w_down, g_ple=v_g_ple, w_ple_gate=v_w_ple_gate,
               w_ple=v_w_ple)
    cx, cy, cc = lax.axis_index("x"), lax.axis_index("y"), lax.axis_index("c")
    chip = 2 * cx + cy
    pos = jnp.stack([cc, chip]).astype(jnp.int32)

    def as_blocks(t):
        return t.reshape(2, -1, t.shape[-1])

    def own_block_in(got, shards):
        return [lax.dynamic_update_slice(g, s[None], (chip, 0, 0)) for g, s in zip(got, shards)]

    conv_shapes = [a[nm].shape for nm in CONVS]
    conv_all, conv_token = gather_small("gather_conv_w", pack_rows([a[nm] for nm in CONVS], F32))
    def layer_block(nm, t, li):
        return as_blocks(t)[li]

    shards0 = [(layer_block(nm, a[nm], 0) + conv_token[0, 0]).astype(BF16) for nm in BIG]
    got0, gathered_token = gather_layer("gather_w_in_l0", shards0[:1])
    shards0[1:] = [s + gathered_token[0, 0].astype(BF16) for s in shards0[1:]]
    gather0, gather0_token = exchange_start("gather_start_l0", "gather", shards0[1:])
    shards1 = [(layer_block(nm, a[nm], 1) + gather0_token[0, 0]).astype(BF16) for nm in BIG]
    gather1, gather1_in_token = exchange_start("gather_start_w_in_l1", "gather", shards1[:1])
    shards1[1:] = [s + gather1_in_token[0, 0].astype(BF16) for s in shards1[1:]]
    gather1_rest, gather1_token = exchange_start("gather_start_l1", "gather", shards1[1:])
    conv_rows = conv_all.shape[0] // 8
    conv_chip = [unpack_rows(conv_all[2 * k * conv_rows:(2 * k + 1) * conv_rows], conv_shapes) for k in range(N_CHIPS)]
    conv = {nm: jnp.concatenate([conv_chip[k][i] for k in range(N_CHIPS)], axis=2) for i, nm in enumerate(CONVS)}

    weights, saved = [None, None], [None, None]
    first_weights = hang_on(layer_weights(0, own_block_in(got0, shards0[:1]), conv, a), gather1_token)

    def rest_of_layer0(after):
        mine, got = exchange_wait("gather_wait_l0", gather0, after)
        return later_weights(own_block_in(got, mine))

    act, saved[0], weights[0] = layer_fwd(0, x[0], p[0, 0], first_weights, more_weights=rest_of_layer0)
    mine1, got1 = exchange_wait("gather_wait_w_in_l1", gather1, act)

    def rest_of_layer1(after):
        mine, got = exchange_wait("gather_wait_l1", gather1_rest, after)
        return later_weights(own_block_in(got, mine))

    act, saved[1], weights[1] = layer_fwd(1, act, p[1, 0], layer_weights(1, own_block_in(got1, mine1), conv, a),
                                          more_weights=rest_of_layer1)
    d_act, loss_part = loss_call(act, loss_target[0])
    loss = lax.psum(loss_part, ("x", "y", "c"))
    layer_grads = [None, None]
    d_act, layer_grads[1] = layer_bwd(1, d_act, saved[1], weights[1])
    rs1 = OverlappedReduceScatter("l1", pos, [layer_grads[1][nm] for nm in BIG])
    rs0 = []

    def stage_mid(after, g):
        rs1.middle(after)
        return rs1.token

    def stage_late(after, g):
        rs0.append(OverlappedReduceScatter("l0", pos, [g[nm] for nm in BIG[1:]]))
        return rs0[0].token

    def stage_last(after, g):
        rs0[0].middle(after)
        return rs0[0].token

    def stage_w_in(after, g):
        rs0.append(OverlappedReduceScatter("w_in_l0", pos, [g["w_in"]]))
        return rs0[1].token

    d_act, layer_grads[0] = layer_bwd(0, d_act, saved[0], hang_on(weights[0], rs1.token),
                                      hooks=dict(mid=stage_mid, late=stage_late, last=stage_last, w_in=stage_w_in))
    rs0[1].middle(d_act)
    reduced = [rs0[0].finish(rs0[1].token), rs1.finish(rs0[1].token)]
    grad_x = d_act[None]

    def both(nm):
        return jnp.stack([layer_grads[0][nm], layer_grads[1][nm]])

    local = {nm: both(nm) for nm in ("g_mix", "b_fox_f", "fox_q_gain", "fox_k_gain", "dn_norm_gain", "g_ffn", "g_ple", "sc_conv_w",
                                      "dn_conv_w", "ffn_conv_w")}
    local["dn_a_log"] = jnp.stack([layer_grads[li]["ad"][0] for li in range(2)])
    local["dn_dt_bias"] = jnp.stack([layer_grads[li]["ad"][1] for li in range(2)])

    small_names = SMALL + CONVS
    small_shapes = [local[nm].shape for nm in small_names]
    small_sum = sum_devices(gather_small("gather_small_grads", pack_rows([local[nm] for nm in small_names], F32))[0])
    small_grads = dict(zip(small_names, unpack_rows(small_sum, small_shapes)))
    for nm in CONVS:
        width = a[nm].shape[2]
        small_grads[nm] = lax.dynamic_slice_in_dim(small_grads[nm], chip * width, width, axis=2)

    grads, deltas, new_m, new_v = dict(small_grads), {}, {}, {}
    for nm in small_names:
        deltas[nm], new_m[nm], new_v[nm] = adam_call(f"adam_{nm}", a[nm], grads[nm], mom[nm], var[nm])
    for i, nm in enumerate(BIG[1:]):
        res = adam_layers(f"adam_{nm}", as_blocks(a[nm]), as_blocks(mom[nm]), as_blocks(var[nm]), reduced[0][i], reduced[1][1 + i])
        grads[nm], deltas[nm], new_m[nm], new_v[nm] = [r.reshape(a[nm].shape) for r in res]
    stored = lambda t: jnp.transpose(t, (2, 0, 1))
    res = adam_w_in("adam_w_in", stored(a["w_in"]), stored(mom["w_in"]), stored(var["w_in"]), rs0[1].finish(deltas["w_ple"])[0], reduced[1][0])
    grads["w_in"], deltas["w_in"], new_m["w_in"], new_v["w_in"] = [jnp.transpose(r, (1, 2, 0)) for r in res]
    return (loss, grad_x, *[grads[nm] for nm in WEIGHTS], *[deltas[nm] for nm in WEIGHTS], *[new_m[nm] for nm in WEIGHTS],
            *[new_v[nm] for nm in WEIGHTS])
```

```python
import functools

import jax
import jax.numpy as jnp
from jax import lax
from jax.experimental import pallas as pl
from jax.experimental.pallas import tpu as pltpu

F32 = jnp.float32
BF16 = jnp.bfloat16
HI = lax.Precision.HIGHEST
SOLVE = lax.Precision.HIGH
MESH = pl.DeviceIdType.MESH

D_MODEL = 1024
BRANCH = 512
FOX_DH = 64
DN_DH = 128
DN_HEADS = 4
DN_CHUNK = 64
FOX_BLOCK = 128
D_FF = 2816
EPS = 1e-6
N_CHIPS = 4
LANES = 128

ADAM_LR, ADAM_B1, ADAM_B2, ADAM_EPS, ADAM_WD, ADAM_STEP = 0.001, 0.9, 0.999, 1e-08, 0.01, 10

VMEM_LIMIT = 56 * 1024 * 1024

C_FQ, C_FK, C_FV, C_SB, C_SC, C_SV, C_DN, C_DZ, C_GATE = 0, 512, 1024, 1536, 2048, 2560, 3072, 4608, 5120
IN_MAIN = 8192
IN_SIZES = (1536, 8, 1536, 1536, 4, 4, 512, 3072)

BIG = ("w_in", "w_branch", "w_o", "w_up", "w_down", "w_ple_gate", "w_ple")
BIG_AXIS = {"w_in": 2, "w_branch": 3, "w_o": 1, "w_up": 2, "w_down": 1, "w_ple_gate": 1, "w_ple": 2}
CONVS = ("sc_conv_w", "dn_conv_w", "ffn_conv_w")
SMALL = ("g_mix", "b_fox_f", "fox_q_gain", "fox_k_gain", "dn_a_log", "dn_dt_bias", "dn_norm_gain", "g_ffn", "g_ple")
WEIGHTS = ("g_mix", "w_in", "b_fox_f", "fox_q_gain", "fox_k_gain", "sc_conv_w", "dn_conv_w", "dn_a_log", "dn_dt_bias",
           "dn_norm_gain", "w_branch", "w_o", "g_ffn", "w_up", "ffn_conv_w", "w_down", "g_ple", "w_ple_gate", "w_ple")


def _iota(shape, dim):
    return lax.broadcasted_iota(jnp.int32, shape, dim)


def _dg(a, b, mode, prec=None):
    dims = {"nn": ((1,), (0,)), "nt": ((1,), (1,)), "tn": ((0,), (0,))}[mode]
    return lax.dot_general(a, b, (dims, ((), ())), precision=prec, preferred_element_type=F32)


def _bdot_impl(a, b, mode):
    return _dg(a.astype(BF16), b.astype(BF16), mode)


@functools.partial(jax.custom_vjp, nondiff_argnums=(2,))
def _bdot_diff(a, b, mode):
    return _bdot_impl(a, b, mode)


def _bdot_fwd(a, b, mode):
    return _bdot_impl(a, b, mode), (a, b)


def _bdot_bwd(mode, res, g):
    a, b = res
    if mode == "nn":
        da, db = _bdot_impl(g, b, "nt"), _bdot_impl(a, g, "tn")
    elif mode == "nt":
        da, db = _bdot_impl(g, b, "nn"), _bdot_impl(g, a, "tn")
    else:
        da, db = _bdot_impl(b, g, "nt"), _bdot_impl(a, g, "nn")
    return da.astype(a.dtype), db.astype(b.dtype)


_bdot_diff.defvjp(_bdot_fwd, _bdot_bwd)


def _bdot(d):
    return _bdot_diff if d else _bdot_impl


def _shift_impl(x, k):
    return jnp.where(_iota(x.shape, 0) >= k, pltpu.roll(x, k, 0), 0.0)


def _unshift_impl(g, k):
    n = g.shape[0]
    return jnp.where(_iota(g.shape, 0) < n - k, pltpu.roll(g, n - k, 0), 0.0)


@functools.partial(jax.custom_vjp, nondiff_argnums=(1,))
def _shift_diff(x, k):
    return _shift_impl(x, k)


_shift_diff.defvjp(lambda x, k: (_shift_impl(x, k), None), lambda k, _, g: (_unshift_impl(g, k),))


def _row(w, j):
    return jnp.sum(jnp.where(_iota(w.shape, 0) == j, w, 0.0), axis=0, keepdims=True)


def _col(w, j):
    return jnp.sum(jnp.where(_iota(w.shape, 1) == j, w, 0.0), axis=1, keepdims=True)


def _conv(d, x, w):
    shift = _shift_diff if d else _shift_impl
    taps = w.shape[0]
    y = x * _row(w, taps - 1)
    for j in range(taps - 1):
        y = y + shift(x, taps - 1 - j) * _row(w, j)
    return y


def _softplus(x):
    return jnp.maximum(x, 0.0) + jnp.log(1.0 + jnp.exp(-jnp.abs(x)))


def _sigmoid(x):
    return 0.5 * (jnp.tanh(0.5 * x) + 1.0)


def _silu(x):
    return x * _sigmoid(x)


def _rms(x, gain):
    return x * lax.rsqrt(jnp.mean(x * x, axis=-1, keepdims=True) + EPS) * gain


def _rms_fn(d, pids, x, gain):
    return (_rms(x, gain),)


def _loss_fn(d, pids, y, t):
    e = y - t
    part = 0.5 / D_MODEL * jnp.sum(e * e, keepdims=True)
    return e * (1.0 / D_MODEL), jnp.broadcast_to(part, (8, LANES))


def _fox_prep_fn(d, pids, q, k, gq, gk):
    first = _iota(q.shape, 1) < FOX_DH

    def norm(x, gain):
        sq = x * x
        ss_a = jnp.sum(jnp.where(first, sq, 0.0), axis=1, keepdims=True)
        ss_b = jnp.sum(jnp.where(first, 0.0, sq), axis=1, keepdims=True)
        rs = jnp.where(first, lax.rsqrt(ss_a / FOX_DH + EPS), lax.rsqrt(ss_b / FOX_DH + EPS))
        return x * rs * gain

    return norm(q, gq) * FOX_DH ** -0.5, norm(k, gk)


def _fox_gate_fn(d, pids, f, bias):
    logf = -_softplus(-(f + bias))
    n_r, n_c = logf.shape
    tri = (_iota((n_c, n_c), 0) <= _iota((n_c, n_c), 1)).astype(F32)
    within = _dg(logf, tri, "nn", HI)
    tot = jnp.broadcast_to(jnp.sum(logf, axis=1, keepdims=True), logf.shape)
    below = (_iota((n_r, n_r), 1) < _iota((n_r, n_r), 0)).astype(F32)
    return (within + _dg(below, tot, "nn", HI),)


def _fox_attn_fn(q_block0, d, pids, q, k, v, cq_a, cq_b, ck_a, ck_b):
    dot = _bdot(d)
    first = _iota(q.shape, 1) < FOX_DH
    n_q, n_k = q.shape[0], k.shape[0]
    causal = ((q_block0 + pids[1]) * n_q + _iota((n_q, n_k), 0)) >= _iota((n_q, n_k), 1)

    qs = [jnp.where(first, q, 0.0), jnp.where(first, 0.0, q)]
    s = _each(lambda qh, cq, ck: jnp.where(causal, dot(qh, k, "nt") + cq - ck, -1e30), qs, [cq_a, cq_b], [ck_a, ck_b])
    e = [jnp.exp(si - lax.stop_gradient(jnp.max(si, axis=1, keepdims=True))) for si in s]
    o_a, o_b = [dot(ei * (1.0 / jnp.sum(ei, axis=1, keepdims=True)), v, "nn") for ei in e]
    return (jnp.where(first, o_a, o_b),)


def _sconv_fn(d, pids, sb, sc, sv, w):
    return (sb * _conv(d, sc * sv, w),)


def _dnconv_fn(d, pids, x, w):
    return (_silu(_conv(d, x, w)),)


def _merge_fn(d, pids, y0, y1, y2, g0, g1, g2):
    return (_sigmoid(g0) * y0 + _sigmoid(g1) * y1 + _sigmoid(g2) * y2,)


def _ffn_act_fn(d, pids, ug, uv, wg, wv):
    return (_silu(_conv(d, ug, wg)) * _conv(d, uv, wv),)


def _ple_fn(d, pids, gpre, pe, x):
    return (x + _sigmoid(gpre) * pe,)


def _adam_fn(d, pids, w, g, m, v):
    m2 = ADAM_B1 * m + (1.0 - ADAM_B1) * g
    v2 = ADAM_B2 * v + (1.0 - ADAM_B2) * (g * g)
    m_hat = m2 / (1.0 - ADAM_B1 ** ADAM_STEP)
    v_hat = v2 / (1.0 - ADAM_B2 ** ADAM_STEP)
    delta = -ADAM_LR * (m_hat / (jnp.sqrt(v_hat) + ADAM_EPS) + ADAM_WD * w)
    return delta, m2, v2


def _each(fn, *lists):
    return [fn(*args) for args in zip(*lists)]


def _tri_inv_impl(mats):
    n = mats[0].shape[0]
    r, c = _iota((n, n), 0), _iota((n, n), 1)
    diag_blk = (r >> 4) == (c >> 4)
    eye = (r == c).astype(F32)
    mm = lambda us, ws: _each(lambda u, w: _dg(u, w, "nn", SOLVE), us, ws)
    grow = lambda ps, xs: _each(lambda p, px: p + px, ps, mm(ps, xs))
    x = [jnp.where(diag_blk, -a, 0.0) for a in mats]
    p = [eye + xi for xi in x]
    x2 = mm(x, x)
    p = grow(p, x2)
    x4 = mm(x2, x2)
    p = grow(p, x4)
    p = grow(p, mm(x4, x4))
    y = [-yi for yi in mm(p, [jnp.where(diag_blk, 0.0, a) for a in mats])]
    q = grow([eye + yi for yi in y], mm(y, y))
    return mm(q, p)


@jax.custom_vjp
def _tri_inv_diff(mats):
    return _tri_inv_impl(mats)


def _tri_inv_fwd(mats):
    ts = _tri_inv_impl(mats)
    return ts, ts


def _tri_inv_bwd(ts, gs):
    left = _each(lambda t, g: _dg(t, g, "tn", SOLVE), ts, gs)
    return ([-m for m in _each(lambda l, t: _dg(l, t, "nt", SOLVE), left, ts)],)


_tri_inv_diff.defvjp(_tri_inv_fwd, _tri_inv_bwd)


def _dn_local(d, qs, ks, vs, a_cs, a_rs, b_cs, a_logs, dt_bs):
    dot = _bdot(d)
    inv = _tri_inv_diff if d else _tri_inv_impl
    n = qs[0].shape[0]
    r, c = _iota((n, n), 0), _iota((n, n), 1)
    incl, strict, upper = r >= c, r > c, r <= c
    qs = [q * lax.rsqrt(jnp.sum(q * q, axis=1, keepdims=True) + EPS) * DN_DH ** -0.5 for q in qs]
    ks = [k * lax.rsqrt(jnp.sum(k * k, axis=1, keepdims=True) + EPS) for k in ks]
    betas = [_sigmoid(b) for b in b_cs]
    rates = [-jnp.exp(a) for a in a_logs]
    g_cs = _each(lambda rate, a, dt: rate * _softplus(a + dt), rates, a_cs, dt_bs)
    g_rs = _each(lambda rate, a, dt: rate * _softplus(a + dt), rates, a_rs, dt_bs)
    gcum_cs = [jnp.sum(jnp.where(incl, g, 0.0), axis=1, keepdims=True) for g in g_rs]
    gcum_rs = [jnp.sum(jnp.where(upper, g, 0.0), axis=0, keepdims=True) for g in g_cs]
    decays = _each(lambda gc, gr: jnp.exp(jnp.where(incl, gc - gr, -1e30)), gcum_cs, gcum_rs)
    kbs = _each(lambda k, b: k * b, ks, betas)
    kk = _each(lambda kb, k: dot(kb, k, "nt"), kbs, ks)
    ts = inv(_each(lambda m, dec: jnp.where(strict, m * dec, 0.0), kk, decays))
    e_gs = [jnp.exp(g) for g in gcum_cs]
    us = _each(lambda t, v, b: _dg(t, v * b, "nn", SOLVE), ts, vs, betas)
    k_cums = _each(lambda t, kb, e: _dg(t, kb * e, "nn", SOLVE), ts, kbs, e_gs)
    qk = _each(lambda q, k: dot(q, k, "nt"), qs, ks)
    qk = _each(lambda m, dec: jnp.where(incl, m * dec, 0.0), qk, decays)
    g_lasts = [jnp.sum(g, axis=0, keepdims=True) for g in g_cs]
    q_decs = _each(lambda q, e: q * e, qs, e_gs)
    k_decs = _each(lambda k, gl, gc: k * jnp.exp(gl - gc), ks, g_lasts, gcum_cs)
    return list(zip(us, k_cums, q_decs, k_decs, qk, g_lasts))


def _dn_step(d, s_prevs, items, zs, gain):
    dot = _bdot(d)
    us, k_cums, q_decs, k_decs, qks, g_lasts = [list(t) for t in zip(*items)]
    v_news = _each(lambda u, kc, s: u - dot(kc, s, "nn"), us, k_cums, s_prevs)
    inter = _each(lambda qd, s: dot(qd, s, "nn"), q_decs, s_prevs)
    outs = _each(lambda o, qk, vn: o + dot(qk, vn, "nn"), inter, qks, v_news)
    s_nexts = _each(lambda s, gl, kd, vn: s * jnp.exp(gl) + dot(kd, vn, "tn"), s_prevs, g_lasts, k_decs, v_news)
    return _each(lambda o, z: _rms(o, gain) * _silu(z), outs, zs), s_nexts


def _split_heads(t):
    return [t[:, h * DN_DH:(h + 1) * DN_DH] for h in range(t.shape[1] // DN_DH)]


def _dn_gates(ps, a_rows, ad):
    hs = range(DN_HEADS)
    return ([_col(ps, 12 + h) for h in hs], [_row(a_rows, h) for h in hs], [_col(ps, 8 + h) for h in hs],
            [_col(_row(ad, 0), h) for h in hs], [_col(_row(ad, 1), h) for h in hs])


def _head_rows(vals):
    row = _iota((8, LANES), 0)
    tile = jnp.zeros((8, LANES), F32)
    for h, val in enumerate(vals):
        tile = tile + jnp.where(row == h, val, 0.0)
    return tile


def _cparams(n_axes):
    return pltpu.CompilerParams(dimension_semantics=("arbitrary",) * n_axes, vmem_limit_bytes=VMEM_LIMIT)


def _first_visit(acc_axes):
    cond = None
    for a in acc_axes:
        here = pl.program_id(a) == 0
        cond = here if cond is None else jnp.logical_and(cond, here)
    return cond


def _tile(ref, widen=False):
    val = ref[...]
    shape = val.shape
    while len(shape) > 2 and shape[0] == 1:
        shape = shape[1:]
    val = val.reshape(shape)
    return val.astype(F32) if widen and val.dtype == BF16 else val


def _store(ref, val, first):
    val = val.astype(ref.dtype).reshape(ref.shape)
    if first is None:
        ref[...] = val
        return

    @pl.when(first)
    def _():
        ref[...] = val

    @pl.when(jnp.logical_not(first))
    def _():
        ref[...] += val


def _specs(ops):
    return [pl.BlockSpec(block, imap) for _, block, imap in ops]


def tile_fwd(name, fn, grid, ins, outs, raw=()):
    n_in = len(ins)

    def body(*refs):
        pids = tuple(pl.program_id(a) for a in range(len(grid)))
        firsts = [_first_visit(o[4]) if o[4] else None for o in outs]
        res = fn(False, pids, *[_tile(r, i not in raw) for i, r in enumerate(refs[:n_in])])
        for ref, val, first in zip(refs[n_in:], res, firsts):
            _store(ref, val, first)

    out = pl.pallas_call(
        body, grid=grid, in_specs=_specs(ins),
        out_specs=[pl.BlockSpec(o[2], o[3]) for o in outs],
        out_shape=[jax.ShapeDtypeStruct(o[0], o[1]) for o in outs],
        name=name, compiler_params=_cparams(len(grid)),
    )(*[a for a, _, _ in ins])
    return out


def tile_bwd(name, fn, grid, ins, cots, diff, adds=None, raw=()):
    adds = adds or {}
    n_in, n_cot = len(ins), len(cots)
    add_pos = sorted(adds)
    diff_idx = [d[0] for d in diff]
    out_desc = [d[2] if len(d) > 2 and d[2] is not None else (ins[d[0]][0].shape, ins[d[0]][1], ins[d[0]][2]) for d in diff]
    out_dtypes = [d[3] if len(d) > 3 else F32 for d in diff]

    def body(*refs):
        pids = tuple(pl.program_id(a) for a in range(len(grid)))
        firsts = [_first_visit(d[1]) if d[1] else None for d in diff]
        vals = [_tile(r, i not in raw) for i, r in enumerate(refs[:n_in])]
        cot_vals = [_tile(r, True) for r in refs[n_in:n_in + n_cot]]
        add_vals = [_tile(r) for r in refs[n_in + n_cot:n_in + n_cot + len(add_pos)]]
        out_refs = refs[n_in + n_cot + len(add_pos):]

        def f(*dv):
            full = list(vals)
            for i, val in zip(diff_idx, dv):
                full[i] = val
            return fn(True, pids, *full)

        prim, vjp = jax.vjp(f, *[vals[i].astype(F32) for i in diff_idx])
        grads = list(vjp(tuple(c.astype(o.dtype) for c, o in zip(cot_vals, prim))))
        for pos, val in zip(add_pos, add_vals):
            extra = val.astype(F32) if firsts[pos] is None else jnp.where(firsts[pos], val.astype(F32), 0.0)
            grads[pos] = grads[pos] + extra
        for ref, val, first in zip(out_refs, grads, firsts):
            _store(ref, val, first)

    all_ins = list(ins) + list(cots) + [adds[p] for p in add_pos]
    out = pl.pallas_call(
        body, grid=grid, in_specs=_specs(all_ins),
        out_specs=[pl.BlockSpec(o[1], o[2]) for o in out_desc],
        out_shape=[jax.ShapeDtypeStruct(o[0], dt) for o, dt in zip(out_desc, out_dtypes)],
        name=name, compiler_params=_cparams(len(grid)),
    )(*[a for a, _, _ in all_ins])
    return out


def _pick(dim, cands):
    for c in cands:
        if dim % c == 0:
            return c
    return dim


MM_TILES = (1024, 512, 1408, 256, 128)


def mm(name, a, b, mode, add=None, out_dtype=F32, blocks=None):
    wide = None
    if mode == "nn":
        (m, kk), n = a.shape, b.shape[-1]
    elif mode == "nt":
        (m, kk), n = a.shape, b.shape[-2]
    else:
        (kk, m), n = a.shape, b.shape[1]
    if blocks is not None:
        lo, n_blk = blocks
        wide = b.shape[-1] if mode != "tn" else n // n_blk
        if mode == "nn":
            n = wide * n_blk
    tm = _pick(m, MM_TILES)
    if mode == "nt" and blocks is not None:
        tn, tk = _pick(n, MM_TILES), _pick(wide, MM_TILES[:-1])
    elif blocks is not None:
        tn, tk = _pick(wide, MM_TILES[:-1]), _pick(kk, MM_TILES)
    else:
        tn, tk = _pick(n, MM_TILES), _pick(kk, MM_TILES)
    if mode == "tn" or blocks is None:
        tk = _pick(kk, (2048,) + MM_TILES)
    nk = kk // tk
    a_spec = pl.BlockSpec((tk, tm), lambda i, j, k: (k, i)) if mode == "tn" else pl.BlockSpec((tm, tk), lambda i, j, k: (i, k))
    o_spec = pl.BlockSpec((tm, tn), lambda i, j, k: (i, j))
    out_shape = (m, n)
    if blocks is None:
        b_spec = pl.BlockSpec((tn, tk), lambda i, j, k: (j, k)) if mode == "nt" else pl.BlockSpec((tk, tn), lambda i, j, k: (k, j))
    elif mode == "nn":
        per = wide // tn
        b_spec = pl.BlockSpec((1, tk, tn), lambda i, j, k: (lo + j // per, k, j % per))
    elif mode == "nt":
        per = wide // tk
        b_spec = pl.BlockSpec((1, tn, tk), lambda i, j, k: (lo + k // per, j, k % per))
    else:
        per = wide // tn
        b_spec = pl.BlockSpec((tk, tn), lambda i, j, k: (k, j))
        o_spec = pl.BlockSpec((1, tm, tn), lambda i, j, k: (j // per, i, j % per))
        out_shape = (n_blk, m, wide)

    def body(*refs):
        a_ref, b_ref = refs[0], refs[1]
        add_ref = refs[2] if add is not None else None
        o_ref, acc = refs[-2], refs[-1]
        k = pl.program_id(2)
        part = _bdot_impl(_tile(a_ref), _tile(b_ref), mode)

        @pl.when(k == 0)
        def _():
            acc[...] = part

        @pl.when(k > 0)
        def _():
            acc[...] += part

        @pl.when(k == nk - 1)
        def _():
            res = acc[...]
            if add_ref is not None:
                res = res + add_ref[...]
            o_ref[...] = res.astype(o_ref.dtype).reshape(o_ref.shape)

    operands = [a, b] + ([add] if add is not None else [])
    in_specs = [a_spec, b_spec] + ([o_spec] if add is not None else [])
    return pl.pallas_call(
        body, grid=(m // tm, n // tn, nk), in_specs=in_specs, out_specs=o_spec,
        out_shape=jax.ShapeDtypeStruct(out_shape, out_dtype),
        scratch_shapes=[pltpu.VMEM((tm, tn), F32)],
        name=name, compiler_params=_cparams(3),
    )(*operands)


def _rows(x, width=None, off=0, tm=256):
    width = x.shape[1] if width is None else width
    return (x, (tm, width), lambda i, off=off: (i, off))


def _whole(x):
    nd = x.ndim
    return (x, x.shape, lambda *pids, nd=nd: (0,) * nd)


RMS_ROWS = 512


def _rms_ops(x, gain):
    return [_rows(x, tm=RMS_ROWS), _whole(gain)]


def rms_fwd(name, x, gain):
    s, dm = x.shape
    return tile_fwd(name, _rms_fn, (s // RMS_ROWS,), _rms_ops(x, gain), [((s, dm), BF16, (RMS_ROWS, dm), lambda i: (i, 0), ())])[0]


def rms_bwd(name, x, gain, dh, dres):
    s = x.shape[0]
    return tile_bwd(name, _rms_fn, (s // RMS_ROWS,), _rms_ops(x, gain), [_rows(dh, tm=RMS_ROWS)], [(0, ()), (1, (0,))],
                    adds={0: _rows(dres, tm=RMS_ROWS)})


def loss_call(y, t):
    s, dm = y.shape
    dy, part = tile_fwd("loss", _loss_fn, (s // 256,), [_rows(y), _rows(t)],
                        [((s, dm), F32, (256, dm), lambda i: (i, 0), ()), ((8, LANES), F32, (8, LANES), lambda i: (0, 0), (0,))])
    return dy, part[0, 0]


def _fox_prep_ops(pm, gq, gk):
    tm = 512
    return [(pm, (tm, LANES), lambda i, j: (i, C_FQ // LANES + j)), (pm, (tm, LANES), lambda i, j: (i, C_FK // LANES + j)),
            _whole(gq), _whole(gk)]


def fox_prep_fwd(name, pm, gq, gk):
    s = pm.shape[0]
    out = ((s, BRANCH), BF16, (512, LANES), lambda i, j: (i, j), ())
    return tile_fwd(name, _fox_prep_fn, (s // 512, 4), _fox_prep_ops(pm, gq, gk), [out, out])


def fox_prep_bwd(name, pm, gq, gk, dqn, dkn):
    s = pm.shape[0]
    cot = lambda g: (g, (512, LANES), lambda i, j: (i, j))
    own = ((s, BRANCH), (512, LANES), lambda i, j: (i, j))
    return tile_bwd(name, _fox_prep_fn, (s // 512, 4), _fox_prep_ops(pm, gq, gk), [cot(dqn), cot(dkn)],
                    [(0, (), own, BF16), (1, (), own, BF16), (2, (0, 1)), (3, (0, 1))])


def _fox_gate_ops(f_t, bias):
    return [(f_t, (1,) + f_t.shape[1:], lambda h: (h, 0, 0)), (bias, (1, 1, 1), lambda h: (h, 0, 0))]


def fox_gate_fwd(name, f_t, bias):
    n_h = f_t.shape[0]
    return tile_fwd(name, _fox_gate_fn, (n_h,), _fox_gate_ops(f_t, bias),
                    [(f_t.shape, F32, (1,) + f_t.shape[1:], lambda h: (h, 0, 0), ())])[0]


def fox_gate_bwd(name, f_t, bias, dcum):
    n_h = f_t.shape[0]
    return tile_bwd(name, _fox_gate_fn, (n_h,), _fox_gate_ops(f_t, bias),
                    [(dcum, (1,) + f_t.shape[1:], lambda h: (h, 0, 0))], [(0, ()), (1, ())])


FOX_GROUPS = 4


def _fox_groups(s):
    per = s // FOX_BLOCK // FOX_GROUPS
    return [(g * per, per, (g + 1) * per * FOX_BLOCK) for g in range(FOX_GROUPS)]


def _fox_attn_ops(qn, kn, pm, cum_c, cum_r, q0, keys):
    nb = FOX_BLOCK
    return [(qn, (nb, LANES), lambda p, i: (q0 + i, p)), (kn, (keys, LANES), lambda p, i: (0, p)),
            (pm, (keys, LANES), lambda p, i: (0, C_FV // LANES + p)),
            (cum_c, (1, nb, 1), lambda p, i: (2 * p, q0 + i, 0)), (cum_c, (1, nb, 1), lambda p, i: (2 * p + 1, q0 + i, 0)),
            (cum_r, (1, 1, keys), lambda p, i: (2 * p, 0, 0)), (cum_r, (1, 1, keys), lambda p, i: (2 * p + 1, 0, 0))]


def fox_attn_fwd(name, qn, kn, pm, cum_c, cum_r):
    s = qn.shape[0]
    parts = []
    for g, (q0, n_q, keys) in enumerate(_fox_groups(s)):
        parts.append(tile_fwd(f"{name}_g{g}", functools.partial(_fox_attn_fn, q0), (4, n_q), _fox_attn_ops(qn, kn, pm, cum_c, cum_r, q0, keys),
                              [((n_q * FOX_BLOCK, BRANCH), BF16, (FOX_BLOCK, LANES), lambda p, i: (i, p), ())], raw=(0, 1, 2))[0])
    return jnp.concatenate(parts, axis=0)


def fox_attn_bwd(name, qn, kn, pm, cum_c, cum_r, dy):
    s = qn.shape[0]
    groups = _fox_groups(s)
    d_qn, by_q, tails = [None] * len(groups), [None] * len(groups), [None] * len(groups)
    below = None
    for g in reversed(range(len(groups))):
        q0, n_q, keys = groups[g]
        rows = n_q * FOX_BLOCK
        own_q = ((rows, BRANCH), (FOX_BLOCK, LANES), lambda p, i: (i, p))
        own_k = ((keys, BRANCH), (keys, LANES), lambda p, i: (0, p))
        pair_c = ((4, rows, 1), (1, FOX_BLOCK, 1), lambda p, i: (p, i, 0))
        pair_r = ((4, 1, keys), (1, 1, keys), lambda p, i: (p, 0, 0))
        adds = {}
        if below is not None:
            adds = {1: (below[0],) + own_k[1:], 2: (below[1],) + own_k[1:], 5: (below[2],) + pair_r[1:], 6: (below[3],) + pair_r[1:]}
        g_qn, g_kn, g_v, g_cqa, g_cqb, g_cka, g_ckb = tile_bwd(
            f"{name}_g{g}", functools.partial(_fox_attn_fn, q0), (4, n_q), _fox_attn_ops(qn, kn, pm, cum_c, cum_r, q0, keys),
            [(dy, (FOX_BLOCK, LANES), lambda p, i, q0=q0: (q0 + i, p))],
            [(0, (), own_q), (1, (1,), own_k), (2, (1,), own_k), (3, (), pair_c), (4, (), pair_c), (5, (1,), pair_r), (6, (1,), pair_r)],
            adds=adds)
        below = (g_kn, g_v, g_cka, g_ckb)
        lo = groups[g - 1][2] if g else 0
        d_qn[g] = g_qn
        by_q[g] = jnp.stack([g_cqa[:, :, 0], g_cqb[:, :, 0]], axis=1).reshape(8, rows)
        tails[g] = (g_kn[lo:], g_v[lo:], jnp.stack([g_cka[:, 0, lo:], g_ckb[:, 0, lo:]], axis=1).reshape(8, keys - lo))
    d_cum = jnp.concatenate(by_q, axis=1) + jnp.concatenate([t[2] for t in tails], axis=1)
    return jnp.concatenate(d_qn, axis=0), jnp.concatenate([t[0] for t in tails], axis=0), jnp.concatenate([t[1] for t in tails], axis=0), d_cum


def sconv_ops(pm, w):
    s = pm.shape[0]
    blk = lambda c0: (pm, (s, LANES), lambda j, c0=c0: (0, c0 // LANES + j))
    return [blk(C_SB), blk(C_SC), blk(C_SV), (w, (w.shape[0], LANES), lambda j: (0, j))]


def dnconv_ops(pm, w):
    s = pm.shape[0]
    return [(pm, (s, LANES), lambda j: (0, C_DN // LANES + j)), (w, (w.shape[0], LANES), lambda j: (0, j))]


def ffn_ops(ug, uv, w):
    s = ug.shape[0]
    n_t = D_FF // LANES
    return [(ug, (s, LANES), lambda j: (0, j)), (uv, (s, LANES), lambda j: (0, j)),
            (w, (w.shape[0], LANES), lambda j: (0, j)), (w, (w.shape[0], LANES), lambda j: (0, n_t + j))]


def _col_out(s, width, dtype=F32):
    return ((s, width), dtype, (s, LANES), lambda j: (0, j), ())


def _col_cot(g):
    return (g, (g.shape[0], LANES), lambda j: (0, j))


def merge_ops(yp, pm):
    gate = lambda b: (pm, (256, D_MODEL), lambda i, b=b: (i, C_GATE // D_MODEL + b))
    return [_rows(yp[0]), _rows(yp[1]), _rows(yp[2]), gate(0), gate(1), gate(2)]


def ple_ops(gpre, pe, x):
    return [_rows(gpre), _rows(pe), _rows(x)]


def adam_call(name, w, g, m, v):
    shape = w.shape
    last = shape[-1]
    rows = w.size // last
    flat = lambda t: t.reshape(rows, last)
    tm = rows
    for cand in (512, 256, 128, 64, 32, 16, 8):
        if rows % cand == 0 and cand * last * 4 <= 2 * 1024 * 1024:
            tm = cand
            break
    spec = lambda t: (flat(t), (tm, last), lambda i: (i, 0))
    out = ((rows, last), F32, (tm, last), lambda i: (i, 0), ())
    res = tile_fwd(name, _adam_fn, (rows // tm,), [spec(w), spec(g), spec(m), spec(v)], [out, out, out])
    return [r.reshape(shape) for r in res]


def _adam_layers_fn(d, pids, w, m, v, g0, g1):
    g = jnp.where(pids[0] == 0, g0, g1)
    return (g,) + _adam_fn(d, pids, w, g, m, v)


def adam_layers(name, w, m, v, g0, g1):
    _, rows, cols = w.shape
    tm = _row_tile(rows, cols)
    n_t = rows // tm
    lay = lambda t: (t, (1, tm, cols), lambda l, i: (l, i, 0))
    ins = [lay(w), lay(m), lay(v), (g0, (tm, cols), lambda l, i: (i * (1 - l) + (n_t - 1) * l, 0)), (g1, (tm, cols), lambda l, i: (i * l, 0))]
    out = (w.shape, F32, (1, tm, cols), lambda l, i: (l, i, 0), ())
    return tile_fwd(name, _adam_layers_fn, (2, n_t), ins, [out, out, out, out])


def adam_w_in(name, w, m, v, g0, g1):
    rows, n_l, cols = w.shape

    def body(w_ref, m_ref, v_ref, g0_ref, g1_ref, g_out, d_out, m_out, v_out):
        step = 64

        def update(at):
            g0, g1 = g0_ref[at, :], g1_ref[at, :]
            layer = _iota((g0.shape[0], n_l, LANES), 1)
            g = jnp.where(layer == 0, g0[:, None, :], g1[:, None, :])
            delta, m2, v2 = _adam_fn(False, None, w_ref[at], g, m_ref[at], v_ref[at])
            for ref, val in ((g_out, g), (d_out, delta), (m_out, m2), (v_out, v2)):
                ref[at] = val

        def some_rows(i, carry):
            update(pl.ds(pl.multiple_of(i * step, step), step))
            return carry

        lax.fori_loop(0, rows // step, some_rows, 0)
        if rows % step:
            update(pl.ds(rows - rows % step, rows % step))

    both = pl.BlockSpec((rows, n_l, LANES), lambda j: (0, 0, j))
    one = pl.BlockSpec((rows, LANES), lambda j: (0, j))
    return pl.pallas_call(
        body, grid=(cols // LANES,), in_specs=[both, both, both, one, one], out_specs=[both] * 4,
        out_shape=[jax.ShapeDtypeStruct(w.shape, F32)] * 4, name=name, compiler_params=_cparams(1),
    )(w, m, v, g0, g1)


DN_GROUP = 4


def _dn_local_specs(rev_n=None):
    rows = DN_GROUP * DN_CHUNK
    idx = (lambda j: j) if rev_n is None else (lambda j: rev_n - 1 - j)
    return [pl.BlockSpec((rows, 3 * BRANCH), lambda j: (idx(j), 0)), pl.BlockSpec((rows, LANES), lambda j: (idx(j), 0)),
            pl.BlockSpec((DN_GROUP, DN_HEADS, DN_CHUNK), lambda j: (idx(j), 0, 0)), pl.BlockSpec((2, DN_HEADS), lambda j: (0, 0))]


def _dn_group_inputs(qkv, ps, a_rows, c):
    lo = c * DN_CHUNK
    heads = _split_heads(qkv[lo:lo + DN_CHUNK])
    return heads[0:4], heads[4:8], heads[8:12], ps[lo:lo + DN_CHUNK], a_rows[c]


def dn_local_fwd(name, dn_act, ps, a_rows, ad):
    s = dn_act.shape[0]
    n_c, n_g = s // DN_CHUNK, s // (DN_GROUP * DN_CHUNK)
    rows = DN_GROUP * DN_CHUNK

    def body(qkv_ref, ps_ref, ar_ref, ad_ref, u_ref, kc_ref, qd_ref, kd_ref, qk_ref, gl_ref):
        qkv, ps_v, a_rows_v, ad_v = qkv_ref[...], ps_ref[...], ar_ref[...], ad_ref[...]
        args = [[] for _ in range(8)]
        for c in range(DN_GROUP):
            q4, k4, v4, ps_c, ar_c = _dn_group_inputs(qkv, ps_v, a_rows_v, c)
            for lst, vals in zip(args, (q4, k4, v4) + _dn_gates(ps_c, ar_c, ad_v)):
                lst.extend(vals)
        everything = _dn_local(False, *args)
        for c in range(DN_GROUP):
            res = everything[c * DN_HEADS:(c + 1) * DN_HEADS]
            at = pl.ds(c * DN_CHUNK, DN_CHUNK)
            for ref, i in ((u_ref, 0), (kc_ref, 1), (qd_ref, 2), (kd_ref, 3)):
                ref[at, :] = jnp.concatenate([r[i] for r in res], axis=1)
            for h in range(DN_HEADS):
                qk_ref[c, h] = res[h][4]
            gl_ref[c] = _head_rows([r[5] for r in res])

    wide = pl.BlockSpec((rows, BRANCH), lambda j: (j, 0))
    return pl.pallas_call(
        body, grid=(n_g,), in_specs=_dn_local_specs(),
        out_specs=[wide, wide, wide, wide, pl.BlockSpec((DN_GROUP, DN_HEADS, DN_CHUNK, DN_CHUNK), lambda j: (j, 0, 0, 0)),
                   pl.BlockSpec((DN_GROUP, 8, LANES), lambda j: (j, 0, 0))],
        out_shape=[jax.ShapeDtypeStruct((s, BRANCH), F32)] * 4 + [jax.ShapeDtypeStruct((n_c, DN_HEADS, DN_CHUNK, DN_CHUNK), F32),
                                                                 jax.ShapeDtypeStruct((n_c, 8, LANES), F32)],
        name=name, compiler_params=_cparams(1),
    )(dn_act, ps, a_rows, ad)


def dn_local_bwd(name, dn_act, ps, a_rows, ad, cots):
    s = dn_act.shape[0]
    n_c, n_g = s // DN_CHUNK, s // (DN_GROUP * DN_CHUNK)
    rows = DN_GROUP * DN_CHUNK

    def body(qkv_ref, ps_ref, ar_ref, ad_ref, du_ref, dkc_ref, dqd_ref, dkd_ref, dqk_ref, dgl_ref, dqkv_ref, dps_ref, dar_ref, dad_ref):
        first = pl.program_id(0) == 0
        qkv, ps_v, a_rows_v, ad_v = qkv_ref[...], ps_ref[...], ar_ref[...], ad_ref[...]
        d_wide = [r[...] for r in (du_ref, dkc_ref, dqd_ref, dkd_ref)]
        qs, ks, vs, ps_cs, ar_cs, cot = [], [], [], [], [], []
        for c in range(DN_GROUP):
            q4, k4, v4, ps_c, ar_c = _dn_group_inputs(qkv, ps_v, a_rows_v, c)
            qs, ks, vs, ps_cs, ar_cs = qs + q4, ks + k4, vs + v4, ps_cs + [ps_c], ar_cs + [ar_c]
            lo = c * DN_CHUNK
            d_tiles = [_split_heads(t[lo:lo + DN_CHUNK]) for t in d_wide]
            d_gl = dgl_ref[c]
            cot += [(d_tiles[0][h], d_tiles[1][h], d_tiles[2][h], d_tiles[3][h], dqk_ref[c, h], _col(_row(d_gl, h), 0))
                    for h in range(DN_HEADS)]

        def f(qs, ks, vs, ps_cs, ar_cs, ad_v):
            gates = [[] for _ in range(5)]
            for ps_c, ar_c in zip(ps_cs, ar_cs):
                for lst, vals in zip(gates, _dn_gates(ps_c, ar_c, ad_v)):
                    lst.extend(vals)
            return _dn_local(True, qs, ks, vs, *gates)

        _, vjp = jax.vjp(f, qs, ks, vs, ps_cs, ar_cs, ad_v)
        d_q, d_k, d_v, d_ps, d_ar, d_ad = vjp(cot)
        for c in range(DN_GROUP):
            at, hs = pl.ds(c * DN_CHUNK, DN_CHUNK), slice(c * DN_HEADS, (c + 1) * DN_HEADS)
            dqkv_ref[at, :] = jnp.concatenate(d_q[hs] + d_k[hs] + d_v[hs], axis=1).astype(dqkv_ref.dtype)
            dps_ref[at, :] = d_ps[c]
            dar_ref[c] = d_ar[c]
        _store(dad_ref, d_ad, first)

    wide = pl.BlockSpec((rows, BRANCH), lambda j: (j, 0))
    specs = _dn_local_specs()
    return pl.pallas_call(
        body, grid=(n_g,),
        in_specs=specs + [wide, wide, wide, wide, pl.BlockSpec((DN_GROUP, DN_HEADS, DN_CHUNK, DN_CHUNK), lambda j: (j, 0, 0, 0)),
                          pl.BlockSpec((DN_GROUP, 8, LANES), lambda j: (j, 0, 0))],
        out_specs=specs,
        out_shape=[jax.ShapeDtypeStruct((s, 3 * BRANCH), F32), jax.ShapeDtypeStruct((s, LANES), F32),
                   jax.ShapeDtypeStruct((n_c, DN_HEADS, DN_CHUNK), F32), jax.ShapeDtypeStruct((2, DN_HEADS), F32)],
        name=name, compiler_params=_cparams(1),
    )(dn_act, ps, a_rows, ad, *cots)


def _dn_scan_specs(n_c, rev):
    idx = (lambda j: n_c - 1 - j) if rev else (lambda j: j)
    wide = pl.BlockSpec((DN_CHUNK, BRANCH), lambda j: (idx(j), 0))
    return [wide, wide, wide, wide, pl.BlockSpec((1, DN_HEADS, DN_CHUNK, DN_CHUNK), lambda j: (idx(j), 0, 0, 0)),
            pl.BlockSpec((1, 8, LANES), lambda j: (idx(j), 0, 0)), pl.BlockSpec((DN_CHUNK, BRANCH), lambda j: (idx(j), C_DZ // BRANCH)),
            pl.BlockSpec((1, DN_DH), lambda j: (0, 0))]


def _dn_scan_tiles(refs):
    u_ref, kc_ref, qd_ref, kd_ref, qk_ref, gl_ref, z_ref, g_ref = refs
    wide = [_split_heads(r[...]) for r in (u_ref, kc_ref, qd_ref, kd_ref)]
    gl = gl_ref[0]
    return [(wide[0][h], wide[1][h], wide[2][h], wide[3][h], qk_ref[0, h], _col(_row(gl, h), 0)) for h in range(DN_HEADS)], \
        _split_heads(z_ref[...].astype(F32)), g_ref[...]


def dn_scan_fwd(name, local, pm, gain):
    s = pm.shape[0]
    n_c = s // DN_CHUNK

    def body(*refs):
        y_ref, hist_ref, state = refs[8:]

        @pl.when(pl.program_id(0) == 0)
        def _():
            state[...] = jnp.zeros_like(state)

        hist_ref[0] = state[...]
        per_head, z4, gain_v = _dn_scan_tiles(refs[:8])
        ys, s_nexts = _dn_step(False, [state[h] for h in range(DN_HEADS)], per_head, z4, gain_v)
        for h in range(DN_HEADS):
            state[h] = s_nexts[h]
        y_ref[...] = jnp.concatenate(ys, axis=1).astype(y_ref.dtype)

    return pl.pallas_call(
        body, grid=(n_c,), in_specs=_dn_scan_specs(n_c, False),
        out_specs=[pl.BlockSpec((DN_CHUNK, BRANCH), lambda j: (j, 0)),
                   pl.BlockSpec((1, DN_HEADS, DN_DH, DN_DH), lambda j: (j, 0, 0, 0))],
        out_shape=[jax.ShapeDtypeStruct((s, BRANCH), BF16), jax.ShapeDtypeStruct((n_c, DN_HEADS, DN_DH, DN_DH), F32)],
        scratch_shapes=[pltpu.VMEM((DN_HEADS, DN_DH, DN_DH), F32)],
        name=name, compiler_params=_cparams(1),
    )(*local, pm, gain)


def dn_scan_bwd(name, local, pm, gain, hist, dy):
    s = pm.shape[0]
    n_c = s // DN_CHUNK

    def body(*refs):
        hist_ref, dy_ref = refs[8:10]
        du_ref, dkc_ref, dqd_ref, dkd_ref, dqk_ref, dgl_ref, dz_ref, dg_ref, d_state = refs[10:]
        first = pl.program_id(0) == 0

        @pl.when(first)
        def _():
            d_state[...] = jnp.zeros_like(d_state)

        per_head, z4, gain_v = _dn_scan_tiles(refs[:8])
        _, vjp = jax.vjp(functools.partial(_dn_step, True), [hist_ref[0, h] for h in range(DN_HEADS)], per_head, z4, gain_v)
        d_s, grads, d_z, d_gain = vjp((_split_heads(dy_ref[...].astype(F32)), [d_state[h] for h in range(DN_HEADS)]))
        for h in range(DN_HEADS):
            d_state[h] = d_s[h]
        for ref, i in ((du_ref, 0), (dkc_ref, 1), (dqd_ref, 2), (dkd_ref, 3)):
            ref[...] = jnp.concatenate([g[i] for g in grads], axis=1)
        dz_ref[...] = jnp.concatenate(d_z, axis=1).astype(dz_ref.dtype)
        for h in range(DN_HEADS):
            dqk_ref[0, h] = grads[h][4]
        dgl_ref[0] = _head_rows([g[5] for g in grads])
        _store(dg_ref, d_gain, first)

    rev = lambda j: n_c - 1 - j
    specs = _dn_scan_specs(n_c, True)
    return pl.pallas_call(
        body, grid=(n_c,),
        in_specs=specs + [pl.BlockSpec((1, DN_HEADS, DN_DH, DN_DH), lambda j: (rev(j), 0, 0, 0)),
                          pl.BlockSpec((DN_CHUNK, BRANCH), lambda j: (rev(j), 0))],
        out_specs=specs[:6] + [pl.BlockSpec((DN_CHUNK, BRANCH), lambda j: (rev(j), 0)), specs[7]],
        out_shape=[jax.ShapeDtypeStruct((s, BRANCH), F32)] * 4 + [
            jax.ShapeDtypeStruct((n_c, DN_HEADS, DN_CHUNK, DN_CHUNK), F32), jax.ShapeDtypeStruct((n_c, 8, LANES), F32),
            jax.ShapeDtypeStruct((s, BRANCH), BF16), jax.ShapeDtypeStruct((1, DN_DH), F32)],
        scratch_shapes=[pltpu.VMEM((DN_HEADS, DN_DH, DN_DH), F32)],
        name=name, compiler_params=_cparams(1),
    )(*local, pm, gain, hist, dy)


def _seq_layouts(cols, s):
    return cols.T.reshape(cols.shape[1], s // LANES, LANES)


def layer_fwd(li, x, p, w, more_weights=None):
    s = x.shape[0]
    n = lambda t: f"{t}_l{li}"
    h = rms_fwd(n("rms_mix"), x, w["g_mix"])
    pm = mm(n("in_main"), h, w["in_main"], "nn")
    ps = mm(n("in_small"), h, w["in_small"], "nn")
    qn, kn = fox_prep_fwd(n("fox_prep"), pm, w["gq"], w["gk"])
    f_t = _seq_layouts(ps[:, 0:8], s)
    cum = fox_gate_fwd(n("fox_gate"), f_t, w["b_f"])
    cum_c, cum_r = cum.reshape(8, s, 1), cum.reshape(8, 1, s)
    y_fox = fox_attn_fwd(n("fox_attn"), qn, kn, pm, cum_c, cum_r)
    y_sc = tile_fwd(n("sconv"), _sconv_fn, (BRANCH // LANES,), sconv_ops(pm, w["sc_conv_w"]), [_col_out(s, BRANCH, BF16)])[0]
    dn_act = tile_fwd(n("dnconv"), _dnconv_fn, (3 * BRANCH // LANES,), dnconv_ops(pm, w["dn_conv_w"]), [_col_out(s, 3 * BRANCH)])[0]
    a_rows = ps[:, 12:16].reshape(s // DN_CHUNK, DN_CHUNK, DN_HEADS).transpose(0, 2, 1)
    dn_local = dn_local_fwd(n("dn_local"), dn_act, ps, a_rows, w["ad"])
    y_dn, hist = dn_scan_fwd(n("dn_scan"), dn_local, pm, w["dn_gain"])
    ys = (y_fox, y_sc, y_dn)
    if more_weights is not None:
        w = {**w, **more_weights(y_dn)}
    yp = [mm(n(f"branch{b}"), ys[b], w["branch"][b], "nn", blocks=(0, N_CHIPS)) for b in range(3)]
    merged = tile_fwd(n("merge"), _merge_fn, (s // 256,), merge_ops(yp, pm), [((s, D_MODEL), BF16, (256, D_MODEL), lambda i: (i, 0), ())])[0]
    x1 = mm(n("w_o"), merged, w["o"], "nn", add=x)
    h2 = rms_fwd(n("rms_ffn"), x1, w["g_ffn"])
    ug = mm(n("up_g"), h2, w["up"], "nn", blocks=(0, 2))
    uv = mm(n("up_v"), h2, w["up"], "nn", blocks=(2, 2))
    act = tile_fwd(n("ffn_act"), _ffn_act_fn, (D_FF // LANES,), ffn_ops(ug, uv, w["ffn_conv_w"]), [_col_out(s, D_FF, BF16)])[0]
    x2 = mm(n("down"), act, w["down"], "nn", add=x1)
    h3 = rms_fwd(n("rms_ple"), x2, w["g_ple"])
    gpre = mm(n("ple_gate"), h3, w["pg"], "nn")
    pe = mm(n("ple_emb"), p, w["ple"], "nn", blocks=(0, N_CHIPS))
    x3 = tile_fwd(n("ple"), _ple_fn, (s // 256,), ple_ops(gpre, pe, x2), [((s, D_MODEL), F32, (256, D_MODEL), lambda i: (i, 0), ())])[0]
    saved = dict(x=x, h=h, pm=pm, ps=ps, qn=qn, kn=kn, f_t=f_t, cum_c=cum_c, cum_r=cum_r, ys=ys, dn_act=dn_act, dn_local=dn_local,
                 a_rows=a_rows, hist=hist, yp=yp, merged=merged, x1=x1, h2=h2, ug=ug, uv=uv, act=act, x2=x2, h3=h3,
                 gpre=gpre, pe=pe, p=p)
    return x3, saved, w


def hang_on(w, token):
    zero = token[0, 0]
    small = ("g_mix", "g_ffn", "g_ple", "gq", "gk", "b_f", "ad", "dn_gain", "sc_conv_w", "dn_conv_w", "ffn_conv_w")
    return {**w, **{k: w[k] + zero for k in small}}


def layer_bwd(li, dx3, sv, w, hooks=None):
    hooks = hooks or {}

    def stage(key, after, w):
        return hang_on(w, hooks[key](after, g)) if key in hooks else w

    s = dx3.shape[0]
    n = lambda t: f"{t}_l{li}"
    g = {}
    col_own = lambda width: ((s, width), (s, LANES), lambda j: (0, j))
    d_gpre, d_pe = tile_bwd(n("ple_bwd"), _ple_fn, (s // 256,), ple_ops(sv["gpre"], sv["pe"], sv["x2"]), [_rows(dx3)],
                            [(0, (), None, BF16), (1, (), None, BF16)])
    g["w_ple"] = mm(n("d_w_ple"), sv["p"], d_pe, "tn", blocks=(0, N_CHIPS))
    g["w_ple_gate"] = mm(n("d_w_pg"), sv["h3"], d_gpre, "tn").reshape(N_CHIPS, -1, D_MODEL)
    dh3 = mm(n("d_h3"), d_gpre, w["pg"], "nt")
    dx2, d_g_ple = rms_bwd(n("rms_ple_bwd"), sv["x2"], w["g_ple"], dh3, dx3)
    dact = mm(n("d_act"), dx2, w["down"], "nt")
    g["w_down"] = mm(n("d_w_down"), sv["act"], dx2, "tn").reshape(N_CHIPS, -1, D_MODEL)
    taps_own = ((w["ffn_conv_w"].shape[0], D_FF), (w["ffn_conv_w"].shape[0], LANES), lambda j: (0, j))
    d_ug, d_uv, d_fw_g, d_fw_v = tile_bwd(n("ffn_act_bwd"), _ffn_act_fn, (D_FF // LANES,), ffn_ops(sv["ug"], sv["uv"], w["ffn_conv_w"]),
                                          [_col_cot(dact)], [(0, (), None, BF16), (1, (), None, BF16), (2, (), taps_own), (3, (), taps_own)])
    g["ffn_conv_w"] = jnp.concatenate([d_fw_g, d_fw_v], axis=1)
    g["w_up"] = jnp.concatenate([mm(n("d_w_up_g"), sv["h2"], d_ug, "tn", blocks=(0, 2)), mm(n("d_w_up_v"), sv["h2"], d_uv, "tn", blocks=(0, 2))])
    dh2 = mm(n("d_h2_v"), d_uv, w["up"], "nt", blocks=(2, 2), add=mm(n("d_h2_g"), d_ug, w["up"], "nt", blocks=(0, 2)))
    dx1, d_g_ffn = rms_bwd(n("rms_ffn_bwd"), sv["x1"], w["g_ffn"], dh2, dx2)
    w = stage("mid", dx1, w)
    dmerged = mm(n("d_merged"), dx1, w["o"], "nt")
    g["w_o"] = mm(n("d_w_o"), sv["merged"], dx1, "tn").reshape(N_CHIPS, -1, D_MODEL)
    gate_own = ((s, D_MODEL), (256, D_MODEL), lambda i: (i, 0))
    d_yp0, d_yp1, d_yp2, d_g0, d_g1, d_g2 = tile_bwd(
        n("merge_bwd"), _merge_fn, (s // 256,), merge_ops(sv["yp"], sv["pm"]), [_rows(dmerged)],
        [(0, (), None, BF16), (1, (), None, BF16), (2, (), None, BF16), (3, (), gate_own, BF16), (4, (), gate_own, BF16), (5, (), gate_own, BF16)])
    d_yp = (d_yp0, d_yp1, d_yp2)
    g["w_branch"] = jnp.concatenate([mm(n(f"d_w_branch{b}"), sv["ys"][b], d_yp[b], "tn", blocks=(0, N_CHIPS)) for b in range(3)], axis=1)
    d_ys = [mm(n(f"d_y{b}"), d_yp[b], w["branch"][b], "nt", blocks=(0, N_CHIPS)) for b in range(3)]
    w = stage("late", d_ys[2], w)
    *d_local, d_z, d_dngain = dn_scan_bwd(n("dn_scan_bwd"), sv["dn_local"], sv["pm"], w["dn_gain"], sv["hist"], d_ys[2])
    d_dnact, d_ps_dn, d_arows, d_ad = dn_local_bwd(n("dn_local_bwd"), sv["dn_act"], sv["ps"], sv["a_rows"], w["ad"], d_local)
    g["ad"], g["dn_norm_gain"] = d_ad, d_dngain[0]
    d_dnqkv, g["dn_conv_w"] = tile_bwd(n("dnconv_bwd"), _dnconv_fn, (3 * BRANCH // LANES,), dnconv_ops(sv["pm"], w["dn_conv_w"]),
                                       [_col_cot(d_dnact)], [(0, (), col_own(3 * BRANCH), BF16), (1, ())])
    d_sb, d_sc, d_sv, g["sc_conv_w"] = tile_bwd(n("sconv_bwd"), _sconv_fn, (BRANCH // LANES,), sconv_ops(sv["pm"], w["sc_conv_w"]), [_col_cot(d_ys[1])],
                                                [(0, (), col_own(BRANCH), BF16), (1, (), col_own(BRANCH), BF16), (2, (), col_own(BRANCH), BF16), (3, ())])
    w = stage("last", d_dnqkv, w)
    d_qn, d_kn, d_fv, d_cum = fox_attn_bwd(n("fox_attn_bwd"), sv["qn"], sv["kn"], sv["pm"], sv["cum_c"], sv["cum_r"], d_ys[0])
    d_ft, d_bf = fox_gate_bwd(n("fox_gate_bwd"), sv["f_t"], w["b_f"], d_cum.reshape(8, s // LANES, LANES))
    g["b_fox_f"] = d_bf.reshape(8)
    d_fq, d_fk, d_gq, d_gk = fox_prep_bwd(n("fox_prep_bwd"), sv["pm"], w["gq"], w["gk"], d_qn, d_kn)
    g["fox_q_gain"] = d_gq[0, :FOX_DH] + d_gq[0, FOX_DH:]
    g["fox_k_gain"] = d_gk[0, :FOX_DH] + d_gk[0, FOX_DH:]
    d_pm = jnp.concatenate([d_fq, d_fk, d_fv.astype(BF16), d_sb, d_sc, d_sv, d_dnqkv, d_z, d_g0, d_g1, d_g2], axis=1)
    d_a_cols = d_arows.transpose(0, 2, 1).reshape(s, DN_HEADS)
    d_f_cols = d_ft.reshape(8, s).T
    d_ps = d_ps_dn + jnp.concatenate([d_f_cols, jnp.zeros((s, 4), F32), d_a_cols, jnp.zeros((s, LANES - 16), F32)], axis=1)
    g["w_in"] = chip_blocks_w_in(mm(n("d_w_in_main"), d_pm, sv["h"], "tn"), mm(n("d_w_in_small"), d_ps, sv["h"], "tn"))
    w = stage("w_in", g["w_in"], w)
    dh = mm(n("d_h_small"), d_ps, w["in_small"], "nt", add=mm(n("d_h_main"), d_pm, w["in_main"], "nt"))
    dx, d_g_mix = rms_bwd(n("rms_mix_bwd"), sv["x"], w["g_mix"], dh, dx1)
    g["g_mix"], g["g_ffn"], g["g_ple"] = d_g_mix[0], d_g_ffn[0], d_g_ple[0]
    return dx, g


IN_SHARD = 2052
MAIN_RANGES = ((0, 1536), (1544, 3080), (3080, 4616), (4624, 5136), (5136, 8208))
SMALL_RANGES = ((1536, 1544), (4616, 4620), (4620, 4624))


def _from_chip_blocks(blocks, ranges):
    parts = []
    for lo, hi in ranges:
        for k in range(N_CHIPS):
            a0, a1 = max(lo, k * IN_SHARD), min(hi, (k + 1) * IN_SHARD)
            if a0 < a1:
                parts.append(blocks[k][:, a0 - k * IN_SHARD:a1 - k * IN_SHARD])
    return parts


def split_w_in(blocks):
    main = jnp.concatenate(_from_chip_blocks(blocks, MAIN_RANGES), axis=1)
    pad = jnp.zeros((blocks.shape[1], LANES - 16), blocks.dtype)
    return main, jnp.concatenate(_from_chip_blocks(blocks, SMALL_RANGES) + [pad], axis=1)


def chip_blocks_w_in(main, small):
    ranges = sorted([(lo, hi, "m") for lo, hi in MAIN_RANGES] + [(lo, hi, "s") for lo, hi in SMALL_RANGES])
    offs, m_off, s_off = {}, 0, 0
    for lo, hi in MAIN_RANGES:
        offs[lo] = m_off
        m_off += hi - lo
    for lo, hi in SMALL_RANGES:
        offs[lo] = s_off
        s_off += hi - lo
    blocks = []
    for k in range(N_CHIPS):
        parts = []
        for lo, hi, src in ranges:
            a0, a1 = max(lo, k * IN_SHARD), min(hi, (k + 1) * IN_SHARD)
            if a0 < a1:
                arr = main if src == "m" else small
                parts.append(arr[offs[lo] + a0 - lo:offs[lo] + a1 - lo])
        blocks.append(jnp.concatenate(parts, axis=0))
    return jnp.stack(blocks)


def later_weights(got):
    g_branch, g_o, g_up, g_down, g_pg, g_ple = got
    branch = g_branch.reshape(N_CHIPS, 3, BRANCH, -1)
    return dict(branch=[branch[:, b] for b in range(3)], o=g_o.reshape(D_MODEL, D_MODEL), up=g_up,
                down=g_down.reshape(D_FF, D_MODEL), pg=g_pg.reshape(D_MODEL, D_MODEL), ple=g_ple)


def layer_weights(li, got, conv, a):
    main, small = split_w_in(got[0])
    tile2 = lambda v: jnp.concatenate([v, v])[None, :]
    rest = later_weights(got[1:]) if len(got) > 1 else {}
    return dict(
        in_main=main, in_small=small, **rest,
        g_mix=a["g_mix"][li][None, :], g_ffn=a["g_ffn"][li][None, :], g_ple=a["g_ple"][li][None, :],
        gq=tile2(a["fox_q_gain"][li]), gk=tile2(a["fox_k_gain"][li]), b_f=a["b_fox_f"][li].reshape(8, 1, 1),
        ad=jnp.stack([a["dn_a_log"][li], a["dn_dt_bias"][li]]), dn_gain=a["dn_norm_gain"][li][None, :],
        sc_conv_w=conv["sc_conv_w"][li], dn_conv_w=conv["dn_conv_w"][li], ffn_conv_w=conv["ffn_conv_w"][li])


def pack_rows(arrs, dtype):
    flat = jnp.concatenate([t.reshape(-1).astype(dtype) for t in arrs])
    pad = (-flat.shape[0]) % (8 * LANES)
    if pad:
        flat = jnp.concatenate([flat, jnp.zeros((pad,), dtype)])
    return flat.reshape(-1, LANES)


def unpack_rows(buf, shapes):
    flat = buf.reshape(-1)
    out, off = [], 0
    for shp in shapes:
        size = 1
        for dim in shp:
            size *= dim
        out.append(flat[off:off + size].reshape(shp))
        off += size
    return out


def chip_shard(t, axis, k):
    width = t.shape[axis] // N_CHIPS
    return lax.slice_in_dim(t, k * width, (k + 1) * width, axis=axis)


ANY = pl.BlockSpec(memory_space=pl.ANY)


def _position():
    x, y, c = lax.axis_index("x"), lax.axis_index("y"), lax.axis_index("c")
    return x, y, c, [(1 - x, y), (x, 1 - y), (1 - x, 1 - y)]


def gather_small(name, block):
    m_per, n = block.shape

    def body(x_ref, out_ref, token, send_sems, recv_sems, local_sem):
        token[...] = jnp.zeros_like(token)
        x, y, c, chips = _position()
        me, sibling = (x, y, c), (x, y, 1 - c)

        def rows(px, py, pc):
            return out_ref.at[pl.ds((4 * px + 2 * py + pc) * m_per, m_per), :]

        def copy(k, blk, to, src=None):
            return pltpu.make_async_remote_copy(src_ref=rows(*blk) if src is None else src, dst_ref=rows(*blk),
                                                send_sem=send_sems.at[k], recv_sem=recv_sems.at[k], device_id=to, device_id_type=MESH)

        mine = pltpu.make_async_copy(x_ref, rows(*me), local_sem)
        mine.start()
        first = [copy(0, me, sibling, src=x_ref)] + [copy(1 + j, me, (*chip, c), src=x_ref) for j, chip in enumerate(chips)]
        for cp in first:
            cp.start()
        passed = [copy(4 + j, (*chip, c), sibling) for j, chip in enumerate(chips)]
        for j, chip in enumerate(chips):
            copy(1 + j, (*chip, c), me).wait_recv()
            passed[j].start()
        copy(0, sibling, me).wait_recv()
        for j, chip in enumerate(chips):
            copy(4 + j, (*chip, 1 - c), me).wait_recv()
        for cp in first + passed:
            cp.wait_send()
        mine.wait()

    in_vmem = pl.BlockSpec(memory_space=pltpu.VMEM)
    return pl.pallas_call(
        body, out_shape=[jax.ShapeDtypeStruct((8 * m_per, n), block.dtype), jax.ShapeDtypeStruct((8, LANES), F32)],
        in_specs=[in_vmem], out_specs=[in_vmem, in_vmem],
        scratch_shapes=[pltpu.SemaphoreType.DMA((7,)), pltpu.SemaphoreType.DMA((7,)), pltpu.SemaphoreType.DMA],
        name=name, compiler_params=pltpu.CompilerParams(vmem_limit_bytes=VMEM_LIMIT),
    )(block)


def _sems(n):
    return [pltpu.SemaphoreType.DMA((n,)), pltpu.SemaphoreType.DMA((n,))]


def _split_cols(rows):
    return (rows // 2) % 16 != 0


def _half(ref, which, lead=()):
    rows, cols = ref.shape[-2:]
    if _split_cols(rows):
        return ref.at[(*lead, slice(None), pl.ds(which * (cols // 2), cols // 2))]
    return ref.at[(*lead, pl.ds(which * (rows // 2), rows // 2), slice(None))]


def _half_shape(rows, cols):
    return (rows, cols // 2) if _split_cols(rows) else (rows // 2, cols)


def gather_layer(name, shards):
    n_w = len(shards)

    def body(*refs):
        ins, outs = refs[:n_w], refs[n_w:2 * n_w]
        token, send_sems, recv_sems = refs[2 * n_w:]
        token[...] = jnp.zeros_like(token)
        x, y, c, chips = _position()
        sibling = (x, y, 1 - c)

        def part(w, px, py, pc):
            return _half(outs[w], pc, (2 * px + py,))

        def copy(k, w, blk, to, src=None):
            return pltpu.make_async_remote_copy(src_ref=part(w, *blk) if src is None else src, dst_ref=part(w, *blk),
                                                send_sem=send_sems.at[k], recv_sem=recv_sems.at[k], device_id=to, device_id_type=MESH)

        pairs = [(w, j, chip) for w in range(n_w) for j, chip in enumerate(chips)]
        first = [copy(3 * w + j, w, (x, y, c), (*chip, c), src=_half(ins[w], c)) for w, j, chip in pairs]
        for cp in first:
            cp.start()
        passed = [copy(3 * n_w + 3 * w + j, w, (*chip, c), sibling) for w, j, chip in pairs]
        for (w, j, chip), fwd in zip(pairs, passed):
            copy(3 * w + j, w, (*chip, c), (x, y, c)).wait_recv()
            fwd.start()
        for w, j, chip in pairs:
            copy(3 * n_w + 3 * w + j, w, (*chip, 1 - c), (x, y, c)).wait_recv()
        for cp in first + passed:
            cp.wait_send()

    out = pl.pallas_call(
        body, out_shape=[jax.ShapeDtypeStruct((N_CHIPS,) + s.shape, s.dtype) for s in shards] + [jax.ShapeDtypeStruct((8, LANES), F32)],
        in_specs=[ANY] * n_w, out_specs=[ANY] * n_w + [pl.BlockSpec(memory_space=pltpu.VMEM)], scratch_shapes=_sems(6 * n_w), name=name,
    )(*shards)
    return out[:n_w], out[n_w]


def swap_halves(name, grads):
    n_w = len(grads)

    def body(*refs):
        ins, outs = refs[:n_w], refs[n_w:2 * n_w]
        send_sems, recv_sems = refs[2 * n_w:]
        x, y, c, _ = _position()
        cps = [pltpu.make_async_remote_copy(src_ref=_half(ins[w], 1 - c, (slice(None),)), dst_ref=outs[w],
                                            send_sem=send_sems.at[w], recv_sem=recv_sems.at[w], device_id=(x, y, 1 - c),
                                            device_id_type=MESH) for w in range(n_w)]
        for cp in cps:
            cp.start()
        for cp in cps:
            cp.wait()

    return pl.pallas_call(
        body, out_shape=[jax.ShapeDtypeStruct((N_CHIPS,) + _half_shape(*g.shape[1:]), g.dtype) for g in grads],
        in_specs=[ANY] * n_w, out_specs=[ANY] * n_w, scratch_shapes=_sems(n_w), name=name,
    )(*grads)


def scatter_chips(name, partials):
    n_w = len(partials)

    def body(*refs):
        ins, outs = refs[:n_w], refs[n_w:2 * n_w]
        send_sems, recv_sems = refs[2 * n_w:]
        x, y, c, chips = _position()
        cps = [pltpu.make_async_remote_copy(src_ref=ins[w].at[2 * cx + cy], dst_ref=outs[w].at[j], send_sem=send_sems.at[3 * w + j],
                                            recv_sem=recv_sems.at[3 * w + j], device_id=(cx, cy, c), device_id_type=MESH)
               for w in range(n_w) for j, (cx, cy) in enumerate(chips)]
        for cp in cps:
            cp.start()
        for cp in cps:
            cp.wait()

    return pl.pallas_call(
        body, out_shape=[jax.ShapeDtypeStruct((3,) + p.shape[1:], p.dtype) for p in partials],
        in_specs=[ANY] * n_w, out_specs=[ANY] * n_w, scratch_shapes=_sems(3 * n_w), name=name,
    )(*partials)


def share_halves(name, bufs):
    n_w = len(bufs)

    def body(*refs):
        outs = refs[n_w:2 * n_w]
        send_sems, recv_sems = refs[2 * n_w:]
        x, y, c, _ = _position()

        def copy(w, pc):
            half = _half(outs[w], pc)
            return pltpu.make_async_remote_copy(src_ref=half, dst_ref=half, send_sem=send_sems.at[w], recv_sem=recv_sems.at[w],
                                                device_id=(x, y, 1 - c), device_id_type=MESH)

        for w in range(n_w):
            copy(w, c).start()
        for w in range(n_w):
            copy(w, 1 - c).wait_recv()
            copy(w, c).wait_send()

    return pl.pallas_call(
        body, out_shape=[jax.ShapeDtypeStruct(b.shape, b.dtype) for b in bufs], in_specs=[ANY] * n_w, out_specs=[ANY] * n_w,
        input_output_aliases={w: w for w in range(n_w)}, scratch_shapes=_sems(n_w), name=name,
    )(*bufs)


HBM = pl.BlockSpec(memory_space=pltpu.HBM)
SEM = pl.BlockSpec(memory_space=pltpu.SEMAPHORE)
EFFECT = pltpu.SideEffectType.DATAFLOW_SIDE_EFFECTING


def _exchange_copies(kind, srcs, lands):
    x, y, c, chips = _position()
    out = []
    for src, land in zip(srcs, lands):
        if kind == "swap":
            out.append((_half(src, 1 - c, (slice(None),)), land, (x, y, 1 - c)))
            continue
        for j, (cx, cy) in enumerate(chips):
            if kind == "gather":
                out.append((src, land.at[2 * x + y], (cx, cy, c)))
            else:
                out.append((src.at[2 * cx + cy], land.at[j], (cx, cy, c)))
    return out


def _land_shapes(kind, srcs):
    if kind == "gather":
        return [(N_CHIPS,) + s.shape for s in srcs]
    if kind == "swap":
        return [(N_CHIPS,) + _half_shape(*s.shape[1:]) for s in srcs]
    return [(3,) + s.shape[1:] for s in srcs]


def exchange_start(name, kind, srcs):
    n_w = len(srcs)
    shapes = _land_shapes(kind, srcs)
    n_sem = n_w if kind == "swap" else 3 * n_w

    def body(*refs):
        ins, lands = refs[:n_w], refs[n_w:2 * n_w]
        send_sems, recv_sems = refs[2 * n_w:2 * n_w + 2]
        token = refs[-1]
        for i, (src, dst, dev) in enumerate(_exchange_copies(kind, ins, lands)):
            pltpu.make_async_remote_copy(src_ref=src, dst_ref=dst, send_sem=send_sems.at[i], recv_sem=recv_sems.at[i],
                                         device_id=dev, device_id_type=MESH).start()
        token[...] = jnp.zeros_like(token)

    out = pl.pallas_call(
        body, name=name,
        out_shape=(pltpu.SemaphoreType.DMA((n_sem,)), pltpu.SemaphoreType.DMA((n_sem,)),
                   *[pltpu.HBM(s.shape, s.dtype) for s in srcs], *[pltpu.HBM(shp, s.dtype) for shp, s in zip(shapes, srcs)],
                   jax.ShapeDtypeStruct((8, LANES), F32)),
        in_specs=(HBM,) * (2 * n_w), out_specs=(SEM, SEM) + (HBM,) * (2 * n_w) + (pl.BlockSpec(memory_space=pltpu.VMEM),),
        input_output_aliases={i: 2 + i for i in range(2 * n_w)},
        compiler_params=pltpu.CompilerParams(has_side_effects=EFFECT),
    )(*[pltpu.with_memory_space_constraint(s, pltpu.HBM) for s in srcs],
      *[pltpu.with_memory_space_constraint(lax.empty(shp, s.dtype), pltpu.HBM) for shp, s in zip(shapes, srcs)])
    return (kind, n_w, out[:-1]), out[-1]


def exchange_wait(name, handle, after):
    kind, n_w, (send_sems, recv_sems, *thru) = handle
    n_sem = n_w if kind == "swap" else 3 * n_w

    def body(*refs):
        ins, lands = refs[:n_w], refs[n_w:2 * n_w]
        send_sems, recv_sems = refs[2 * n_w:2 * n_w + 2]
        for i, (src, dst, dev) in enumerate(_exchange_copies(kind, ins, lands)):
            cp = pltpu.make_async_remote_copy(src_ref=src, dst_ref=dst, send_sem=send_sems.at[i], recv_sem=recv_sems.at[i],
                                              device_id=dev, device_id_type=MESH)
            cp.wait_send()
            cp.wait_recv()

    out = pl.pallas_call(
        body, name=name, out_shape=tuple(pltpu.HBM(t.shape, t.dtype) for t in thru),
        in_specs=(HBM,) * (2 * n_w) + (SEM, SEM, pl.BlockSpec(memory_space=pl.ANY)), out_specs=(HBM,) * (2 * n_w),
        input_output_aliases={i: i for i in range(2 * n_w)},
        compiler_params=pltpu.CompilerParams(has_side_effects=EFFECT),
    )(*thru, send_sems, recv_sems, after)
    return list(out[:n_w]), list(out[n_w:])


def _row_tile(rows, cols):
    best = rows
    if rows * cols * 4 <= 1024 * 1024:
        return rows
    for t in range(16, rows, 16):
        if rows % t == 0 and t * cols * 4 <= 1024 * 1024:
            best = t
    return best


def pair_sum(name, pos, grad, from_sibling):
    _, rows, cols = grad.shape
    h_rows, h_cols = _half_shape(rows, cols)
    tr = _row_tile(h_rows, h_cols)
    n_t = h_rows // tr

    def body(pos_ref, g_ref, s_ref, b_ref, f_ref):
        tot = g_ref[...] + s_ref[...]
        b_ref[...] = tot.astype(BF16)

        @pl.when(pl.program_id(1) == pos_ref[1])
        def _():
            f_ref[...] = tot[0]

    blk = pl.BlockSpec((1, tr, h_cols), lambda i, k, pos: (k, i, 0))
    if _split_cols(rows):
        mine = pl.BlockSpec((1, tr, h_cols), lambda i, k, pos: (k, i, pos[0]))
    else:
        mine = pl.BlockSpec((1, tr, h_cols), lambda i, k, pos: (k, pos[0] * n_t + i, 0))
    return pl.pallas_call(
        body, grid_spec=pltpu.PrefetchScalarGridSpec(
            num_scalar_prefetch=1, grid=(n_t, N_CHIPS), in_specs=[mine, blk],
            out_specs=[blk, pl.BlockSpec((tr, h_cols), lambda i, k, pos: (i, 0))]),
        out_shape=[jax.ShapeDtypeStruct((N_CHIPS, h_rows, h_cols), BF16), jax.ShapeDtypeStruct((h_rows, h_cols), F32)],
        name=name, compiler_params=_cparams(2),
    )(pos, grad, from_sibling)


def chip_sum(name, pos, own, landed, split_cols):
    half, cols = own.shape
    tr = _row_tile(half, cols)
    n_t = half // tr

    def body(pos_ref, p_ref, l_ref, o_ref):
        o_ref[...] = ((p_ref[...] + l_ref[0].astype(F32)) + l_ref[1].astype(F32)) + l_ref[2].astype(F32)

    if split_cols:
        out_spec, out_shape = pl.BlockSpec((tr, cols), lambda i, pos: (i, pos[0])), (half, 2 * cols)
    else:
        out_spec, out_shape = pl.BlockSpec((tr, cols), lambda i, pos: (pos[0] * n_t + i, 0)), (2 * half, cols)
    return pl.pallas_call(
        body, grid_spec=pltpu.PrefetchScalarGridSpec(
            num_scalar_prefetch=1, grid=(n_t,),
            in_specs=[pl.BlockSpec((tr, cols), lambda i, pos: (i, 0)), pl.BlockSpec((3, tr, cols), lambda i, pos: (0, i, 0))],
            out_specs=out_spec),
        out_shape=jax.ShapeDtypeStruct(out_shape, F32), name=name, compiler_params=_cparams(1),
    )(pos, own, landed)


def reduce_scatter_layer(tag, pos, grads):
    n = lambda t: f"{t}_{tag}"
    from_sibling = swap_halves(n("swap_halves"), grads)
    sums = [pair_sum(n(f"pair_sum{w}"), pos, g, s) for w, (g, s) in enumerate(zip(grads, from_sibling))]
    landed = scatter_chips(n("scatter_chips"), [b for b, _ in sums])
    halves = [chip_sum(n(f"chip_sum{w}"), pos, own, l, _split_cols(g.shape[1])) for w, ((_, own), l, g) in enumerate(zip(sums, landed, grads))]
    return share_halves(n("share_halves"), halves)


class OverlappedReduceScatter:
    def __init__(self, tag, pos, grads):
        self.n = lambda t: f"{t}_{tag}"
        self.pos, self.grads = pos, grads
        self.swap, self.token = exchange_start(self.n("swap_start"), "swap", grads)

    def middle(self, after):
        self.grads, from_sibling = exchange_wait(self.n("swap_wait"), self.swap, after)
        self.sums = [pair_sum(self.n(f"pair_sum{w}"), self.pos, g, s) for w, (g, s) in enumerate(zip(self.grads, from_sibling))]
        self.scatter, self.token = exchange_start(self.n("scatter_start"), "scatter", [b for b, _ in self.sums])

    def finish(self, after):
        _, landed = exchange_wait(self.n("scatter_wait"), self.scatter, after)
        halves = [chip_sum(self.n(f"chip_sum{w}"), self.pos, own, l, _split_cols(g.shape[1]))
                  for w, ((_, own), l, g) in enumerate(zip(self.sums, landed, self.grads))]
        return share_halves(self.n("share_halves"), halves)


def sum_devices(gathered):
    m_per = gathered.shape[0] // 8

    def body(g_ref, o_ref):
        tot = g_ref[pl.ds(0, m_per), :]
        for dev in range(1, 8):
            tot = tot + g_ref[pl.ds(dev * m_per, m_per), :]
        o_ref[...] = tot

    return pl.pallas_call(
        body, out_shape=jax.ShapeDtypeStruct((m_per, gathered.shape[1]), F32),
        in_specs=[pl.BlockSpec(memory_space=pltpu.VMEM)], out_specs=pl.BlockSpec(memory_space=pltpu.VMEM), name="sum_devices",
    )(gathered)


def kernel(x, p, g_mix, w_in, b_fox_f, fox_q_gain, fox_k_gain, sc_conv_w, dn_conv_w, dn_a_log, dn_dt_bias, dn_norm_gain, w_branch, w_o, g_ffn, w_up, ffn_conv_w, w_down, g_ple, w_ple_gate, w_ple, loss_target, m_g_mix, m_w_in, m_b_fox_f, m_fox_q_gain, m_fox_k_gain, m_sc_conv_w, m_dn_conv_w, m_dn_a_log, m_dn_dt_bias, m_dn_norm_gain, m_w_branch, m_w_o, m_g_ffn, m_w_up, m_ffn_conv_w, m_w_down, m_g_ple, m_w_ple_gate, m_w_ple, v_g_mix, v_w_in, v_b_fox_f, v_fox_q_gain, v_fox_k_gain, v_sc_conv_w, v_dn_conv_w, v_dn_a_log, v_dn_dt_bias, v_dn_norm_gain, v_w_branch, v_w_o, v_g_ffn, v_w_up, v_ffn_conv_w, v_w_down, v_g_ple, v_w_ple_gate, v_w_ple):
    a = dict(g_mix=g_mix, w_in=w_in, b_fox_f=b_fox_f, fox_q_gain=fox_q_gain, fox_k_gain=fox_k_gain, sc_conv_w=sc_conv_w,
             dn_conv_w=dn_conv_w, dn_a_log=dn_a_log, dn_dt_bias=dn_dt_bias, dn_norm_gain=dn_norm_gain, w_branch=w_branch, w_o=w_o,
             g_ffn=g_ffn, w_up=w_up, ffn_conv_w=ffn_conv_w, w_down=w_down, g_ple=g_ple, w_ple_gate=w_ple_gate, w_ple=w_ple)
    mom = dict(g_mix=m_g_mix, w_in=m_w_in, b_fox_f=m_b_fox_f, fox_q_gain=m_fox_q_gain, fox_k_gain=m_fox_k_gain, sc_conv_w=m_sc_conv_w,
               dn_conv_w=m_dn_conv_w, dn_a_log=m_dn_a_log, dn_dt_bias=m_dn_dt_bias, dn_norm_gain=m_dn_norm_gain, w_branch=m_w_branch,
               w_o=m_w_o, g_ffn=m_g_ffn, w_up=m_w_up, ffn_conv_w=m_ffn_conv_w, w_down=m_w_down, g_ple=m_g_ple, w_ple_gate=m_w_ple_gate,
               w_ple=m_w_ple)
    var = dict(g_mix=v_g_mix, w_in=v_w_in, b_fox_f=v_b_fox_f, fox_q_gain=v_fox_q_gain, fox_k_gain=v_fox_k_gain, sc_conv_w=v_sc_conv_w,
               dn_conv_w=v_dn_conv_w, dn_a_log=v_dn_a_log, dn_dt_bias=v_dn_dt_bias, dn_norm_gain=v_dn_norm_gain, w_branch=v_w_branch,
               w_o=v_w_o, g_ffn=v_g_ffn, w_up=v_w_up, ffn_conv_w=v_ffn_conv_w, w_down=v_w_down, g_ple=v_g_ple, w_ple_gate=v_w_ple_gate,
               w_ple=v_w_ple)
    cx, cy, cc = lax.axis_index("x"), lax.axis_index("y"), lax.axis_index("c")
    chip = 2 * cx + cy
    pos = jnp.stack([cc, chip]).astype(jnp.int32)

    def as_blocks(t):
        return t.reshape(2, -1, t.shape[-1])

    def own_block_in(got, shards):
        return [lax.dynamic_update_slice(g, s[None], (chip, 0, 0)) for g, s in zip(got, shards)]

    conv_shapes = [a[nm].shape for nm in CONVS]
    conv_all, conv_token = gather_small("gather_conv_w", pack_rows([a[nm] for nm in CONVS], F32))
    def layer_block(nm, t, li):
        return as_blocks(t)[li]

    shards0 = [(layer_block(nm, a[nm], 0) + conv_token[0, 0]).astype(BF16) for nm in BIG]
    got0, gathered_token = gather_layer("gather_w_in_l0", shards0[:1])
    shards0[1:] = [s + gathered_token[0, 0].astype(BF16) for s in shards0[1:]]
    gather0, gather0_token = exchange_start("gather_start_l0", "gather", shards0[1:])
    shards1 = [(layer_block(nm, a[nm], 1) + gather0_token[0, 0]).astype(BF16) for nm in BIG]
    gather1, gather1_in_token = exchange_start("gather_start_w_in_l1", "gather", shards1[:1])
    shards1[1:] = [s + gather1_in_token[0, 0].astype(BF16) for s in shards1[1:]]
    gather1_rest, gather1_token = exchange_start("gather_start_l1", "gather", shards1[1:])
    conv_rows = conv_all.shape[0] // 8
    conv_chip = [unpack_rows(conv_all[2 * k * conv_rows:(2 * k + 1) * conv_rows], conv_shapes) for k in range(N_CHIPS)]
    conv = {nm: jnp.concatenate([conv_chip[k][i] for k in range(N_CHIPS)], axis=2) for i, nm in enumerate(CONVS)}

    weights, saved = [None, None], [None, None]
    first_weights = hang_on(layer_weights(0, own_block_in(got0, shards0[:1]), conv, a), gather1_token)

    def rest_of_layer0(after):
        mine, got = exchange_wait("gather_wait_l0", gather0, after)
        return later_weights(own_block_in(got, mine))

    act, saved[0], weights[0] = layer_fwd(0, x[0], p[0, 0], first_weights, more_weights=rest_of_layer0)
    mine1, got1 = exchange_wait("gather_wait_w_in_l1", gather1, act)

    def rest_of_layer1(after):
        mine, got = exchange_wait("gather_wait_l1", gather1_rest, after)
        return later_weights(own_block_in(got, mine))

    act, saved[1], weights[1] = layer_fwd(1, act, p[1, 0], layer_weights(1, own_block_in(got1, mine1), conv, a),
                                          more_weights=rest_of_layer1)
    d_act, loss_part = loss_call(act, loss_target[0])
    loss = lax.psum(loss_part, ("x", "y", "c"))
    layer_grads = [None, None]
    d_act, layer_grads[1] = layer_bwd(1, d_act, saved[1], weights[1])
    rs1 = OverlappedReduceScatter("l1", pos, [layer_grads[1][nm] for nm in BIG])
    rs0 = []

    def stage_mid(after, g):
        rs1.middle(after)
        return rs1.token

    def stage_late(after, g):
        rs0.append(OverlappedReduceScatter("l0", pos, [g[nm] for nm in BIG[1:]]))
        return rs0[0].token

    def stage_last(after, g):
        rs0[0].middle(after)
        return rs0[0].token

    def stage_w_in(after, g):
        rs0.append(OverlappedReduceScatter("w_in_l0", pos, [g["w_in"]]))
        return rs0[1].token

    d_act, layer_grads[0] = layer_bwd(0, d_act, saved[0], hang_on(weights[0], rs1.token),
                                      hooks=dict(mid=stage_mid, late=stage_late, last=stage_last, w_in=stage_w_in))
    rs0[1].middle(d_act)
    reduced = [rs0[0].finish(rs0[1].token), rs1.finish(rs0[1].token)]
    grad_x = d_act[None]

    def both(nm):
        return jnp.stack([layer_grads[0][nm], layer_grads[1][nm]])

    local = {nm: both(nm) for nm in ("g_mix", "b_fox_f", "fox_q_gain", "fox_k_gain", "dn_norm_gain", "g_ffn", "g_ple", "sc_conv_w",
                                      "dn_conv_w", "ffn_conv_w")}
    local["dn_a_log"] = jnp.stack([layer_grads[li]["ad"][0] for li in range(2)])
    local["dn_dt_bias"] = jnp.stack([layer_grads[li]["ad"][1] for li in range(2)])

    small_names = SMALL + CONVS
    small_shapes = [local[nm].shape for nm in small_names]
    small_sum = sum_devices(gather_small("gather_small_grads", pack_rows([local[nm] for nm in small_names], F32))[0])
    small_grads = dict(zip(small_names, unpack_rows(small_sum, small_shapes)))
    for nm in CONVS:
        width = a[nm].shape[2]
        small_grads[nm] = lax.dynamic_slice_in_dim(small_grads[nm], chip * width, width, axis=2)

    grads, deltas, new_m, new_v = dict(small_grads), {}, {}, {}
    for nm in small_names:
        deltas[nm], new_m[nm], new_v[nm] = adam_call(f"adam_{nm}", a[nm], grads[nm], mom[nm], var[nm])
    for i, nm in enumerate(BIG[1:]):
        res = adam_layers(f"adam_{nm}", as_blocks(a[nm]), as_blocks(mom[nm]), as_blocks(var[nm]), reduced[0][i], reduced[1][1 + i])
        grads[nm], deltas[nm], new_m[nm], new_v[nm] = [r.reshape(a[nm].shape) for r in res]
    stored = lambda t: jnp.transpose(t, (2, 0, 1))
    res = adam_w_in("adam_w_in", stored(a["w_in"]), stored(mom["w_in"]), stored(var["w_in"]), rs0[1].finish(deltas["w_ple"])[0], reduced[1][0])
    grads["w_in"], deltas["w_in"], new_m["w_in"], new_v["w_in"] = [jnp.transpose(r, (1, 2, 0)) for r in res]
    return (loss, grad_x, *[grads[nm] for nm in WEIGHTS], *[deltas[nm] for nm in WEIGHTS], *[new_m[nm] for nm in WEIGHTS],
            *[new_v[nm] for nm in WEIGHTS])
```

```python
import functools

import jax
import jax.numpy as jnp
from jax import lax
from jax.experimental import pallas as pl
from jax.experimental.pallas import tpu as pltpu

F32 = jnp.float32
BF16 = jnp.bfloat16
HI = lax.Precision.HIGHEST
SOLVE = lax.Precision.HIGH
MESH = pl.DeviceIdType.MESH

D_MODEL = 1024
BRANCH = 512
FOX_DH = 64
DN_DH = 128
DN_HEADS = 4
DN_CHUNK = 64
FOX_BLOCK = 128
D_FF = 2816
EPS = 1e-6
N_CHIPS = 4
LANES = 128

ADAM_LR, ADAM_B1, ADAM_B2, ADAM_EPS, ADAM_WD, ADAM_STEP = 0.001, 0.9, 0.999, 1e-08, 0.01, 10

VMEM_LIMIT = 56 * 1024 * 1024

C_FQ, C_FK, C_FV, C_SB, C_SC, C_SV, C_DN, C_DZ, C_GATE = 0, 512, 1024, 1536, 2048, 2560, 3072, 4608, 5120
IN_MAIN = 8192
IN_SIZES = (1536, 8, 1536, 1536, 4, 4, 512, 3072)

BIG = ("w_in", "w_branch", "w_o", "w_up", "w_down", "w_ple_gate", "w_ple")
BIG_AXIS = {"w_in": 2, "w_branch": 3, "w_o": 1, "w_up": 2, "w_down": 1, "w_ple_gate": 1, "w_ple": 2}
CONVS = ("sc_conv_w", "dn_conv_w", "ffn_conv_w")
SMALL = ("g_mix", "b_fox_f", "fox_q_gain", "fox_k_gain", "dn_a_log", "dn_dt_bias", "dn_norm_gain", "g_ffn", "g_ple")
WEIGHTS = ("g_mix", "w_in", "b_fox_f", "fox_q_gain", "fox_k_gain", "sc_conv_w", "dn_conv_w", "dn_a_log", "dn_dt_bias",
           "dn_norm_gain", "w_branch", "w_o", "g_ffn", "w_up", "ffn_conv_w", "w_down", "g_ple", "w_ple_gate", "w_ple")


def _iota(shape, dim):
    return lax.broadcasted_iota(jnp.int32, shape, dim)


def _dg(a, b, mode, prec=None):
    dims = {"nn": ((1,), (0,)), "nt": ((1,), (1,)), "tn": ((0,), (0,))}[mode]
    return lax.dot_general(a, b, (dims, ((), ())), precision=prec, preferred_element_type=F32)


def _bdot_impl(a, b, mode):
    return _dg(a.astype(BF16), b.astype(BF16), mode)


@functools.partial(jax.custom_vjp, nondiff_argnums=(2,))
def _bdot_diff(a, b, mode):
    return _bdot_impl(a, b, mode)


def _bdot_fwd(a, b, mode):
    return _bdot_impl(a, b, mode), (a, b)


def _bdot_bwd(mode, res, g):
    a, b = res
    if mode == "nn":
        da, db = _bdot_impl(g, b, "nt"), _bdot_impl(a, g, "tn")
    elif mode == "nt":
        da, db = _bdot_impl(g, b, "nn"), _bdot_impl(g, a, "tn")
    else:
        da, db = _bdot_impl(b, g, "nt"), _bdot_impl(a, g, "nn")
    return da.astype(a.dtype), db.astype(b.dtype)


_bdot_diff.defvjp(_bdot_fwd, _bdot_bwd)


def _bdot(d):
    return _bdot_diff if d else _bdot_impl


def _shift_impl(x, k):
    return jnp.where(_iota(x.shape, 0) >= k, pltpu.roll(x, k, 0), 0.0)


def _unshift_impl(g, k):
    n = g.shape[0]
    return jnp.where(_iota(g.shape, 0) < n - k, pltpu.roll(g, n - k, 0), 0.0)


@functools.partial(jax.custom_vjp, nondiff_argnums=(1,))
def _shift_diff(x, k):
    return _shift_impl(x, k)


_shift_diff.defvjp(lambda x, k: (_shift_impl(x, k), None), lambda k, _, g: (_unshift_impl(g, k),))


def _row(w, j):
    return jnp.sum(jnp.where(_iota(w.shape, 0) == j, w, 0.0), axis=0, keepdims=True)


def _col(w, j):
    return jnp.sum(jnp.where(_iota(w.shape, 1) == j, w, 0.0), axis=1, keepdims=True)


def _conv(d, x, w):
    shift = _shift_diff if d else _shift_impl
    taps = w.shape[0]
    y = x * _row(w, taps - 1)
    for j in range(taps - 1):
        y = y + shift(x, taps - 1 - j) * _row(w, j)
    return y


def _softplus(x):
    return jnp.maximum(x, 0.0) + jnp.log(1.0 + jnp.exp(-jnp.abs(x)))


def _sigmoid(x):
    return 0.5 * (jnp.tanh(0.5 * x) + 1.0)


def _silu(x):
    return x * _sigmoid(x)


def _rms(x, gain):
    return x * lax.rsqrt(jnp.mean(x * x, axis=-1, keepdims=True) + EPS) * gain


def _rms_fn(d, pids, x, gain):
    return (_rms(x, gain),)


def _loss_fn(d, pids, y, t):
    e = y - t
    part = 0.5 / D_MODEL * jnp.sum(e * e, keepdims=True)
    return e * (1.0 / D_MODEL), jnp.broadcast_to(part, (8, LANES))


def _fox_prep_fn(d, pids, q, k, gq, gk):
    first = _iota(q.shape, 1) < FOX_DH

    def norm(x, gain):
        sq = x * x
        ss_a = jnp.sum(jnp.where(first, sq, 0.0), axis=1, keepdims=True)
        ss_b = jnp.sum(jnp.where(first, 0.0, sq), axis=1, keepdims=True)
        rs = jnp.where(first, lax.rsqrt(ss_a / FOX_DH + EPS), lax.rsqrt(ss_b / FOX_DH + EPS))
        return x * rs * gain

    return norm(q, gq) * FOX_DH ** -0.5, norm(k, gk)


def _fox_gate_fn(d, pids, f, bias):
    logf = -_softplus(-(f + bias))
    n_r, n_c = logf.shape
    tri = (_iota((n_c, n_c), 0) <= _iota((n_c, n_c), 1)).astype(F32)
    within = _dg(logf, tri, "nn", HI)
    tot = jnp.broadcast_to(jnp.sum(logf, axis=1, keepdims=True), logf.shape)
    below = (_iota((n_r, n_r), 1) < _iota((n_r, n_r), 0)).astype(F32)
    return (within + _dg(below, tot, "nn", HI),)


def _fox_attn_fn(q_block0, d, pids, q, k, v, cq_a, cq_b, ck_a, ck_b):
    dot = _bdot(d)
    first = _iota(q.shape, 1) < FOX_DH
    n_q, n_k = q.shape[0], k.shape[0]
    causal = ((q_block0 + pids[1]) * n_q + _iota((n_q, n_k), 0)) >= _iota((n_q, n_k), 1)

    qs = [jnp.where(first, q, 0.0), jnp.where(first, 0.0, q)]
    s = _each(lambda qh, cq, ck: jnp.where(causal, dot(qh, k, "nt") + cq - ck, -1e30), qs, [cq_a, cq_b], [ck_a, ck_b])
    e = [jnp.exp(si - lax.stop_gradient(jnp.max(si, axis=1, keepdims=True))) for si in s]
    o_a, o_b = [dot(ei * (1.0 / jnp.sum(ei, axis=1, keepdims=True)), v, "nn") for ei in e]
    return (jnp.where(first, o_a, o_b),)


def _sconv_fn(d, pids, sb, sc, sv, w):
    return (sb * _conv(d, sc * sv, w),)


def _dnconv_fn(d, pids, x, w):
    return (_silu(_conv(d, x, w)),)


def _merge_fn(d, pids, y0, y1, y2, g0, g1, g2):
    return (_sigmoid(g0) * y0 + _sigmoid(g1) * y1 + _sigmoid(g2) * y2,)


def _ffn_act_fn(d, pids, ug, uv, wg, wv):
    return (_silu(_conv(d, ug, wg)) * _conv(d, uv, wv),)


def _ple_fn(d, pids, gpre, pe, x):
    return (x + _sigmoid(gpre) * pe,)


def _adam_fn(d, pids, w, g, m, v):
    m2 = ADAM_B1 * m + (1.0 - ADAM_B1) * g
    v2 = ADAM_B2 * v + (1.0 - ADAM_B2) * (g * g)
    m_hat = m2 / (1.0 - ADAM_B1 ** ADAM_STEP)
    v_hat = v2 / (1.0 - ADAM_B2 ** ADAM_STEP)
    delta = -ADAM_LR * (m_hat / (jnp.sqrt(v_hat) + ADAM_EPS) + ADAM_WD * w)
    return delta, m2, v2


def _each(fn, *lists):
    return [fn(*args) for args in zip(*lists)]


def _tri_inv_impl(mats):
    n = mats[0].shape[0]
    r, c = _iota((n, n), 0), _iota((n, n), 1)
    diag_blk = (r >> 4) == (c >> 4)
    eye = (r == c).astype(F32)
    mm = lambda us, ws: _each(lambda u, w: _dg(u, w, "nn", SOLVE), us, ws)
    grow = lambda ps, xs: _each(lambda p, px: p + px, ps, mm(ps, xs))
    x = [jnp.where(diag_blk, -a, 0.0) for a in mats]
    p = [eye + xi for xi in x]
    x2 = mm(x, x)
    p = grow(p, x2)
    x4 = mm(x2, x2)
    p = grow(p, x4)
    p = grow(p, mm(x4, x4))
    y = [-yi for yi in mm(p, [jnp.where(diag_blk, 0.0, a) for a in mats])]
    q = grow([eye + yi for yi in y], mm(y, y))
    return mm(q, p)


@jax.custom_vjp
def _tri_inv_diff(mats):
    return _tri_inv_impl(mats)


def _tri_inv_fwd(mats):
    ts = _tri_inv_impl(mats)
    return ts, ts


def _tri_inv_bwd(ts, gs):
    left = _each(lambda t, g: _dg(t, g, "tn", SOLVE), ts, gs)
    return ([-m for m in _each(lambda l, t: _dg(l, t, "nt", SOLVE), left, ts)],)


_tri_inv_diff.defvjp(_tri_inv_fwd, _tri_inv_bwd)


def _dn_local(d, qs, ks, vs, a_cs, a_rs, b_cs, a_logs, dt_bs):
    dot = _bdot(d)
    inv = _tri_inv_diff if d else _tri_inv_impl
    n = qs[0].shape[0]
    r, c = _iota((n, n), 0), _iota((n, n), 1)
    incl, strict, upper = r >= c, r > c, r <= c
    qs = [q * lax.rsqrt(jnp.sum(q * q, axis=1, keepdims=True) + EPS) * DN_DH ** -0.5 for q in qs]
    ks = [k * lax.rsqrt(jnp.sum(k * k, axis=1, keepdims=True) + EPS) for k in ks]
    betas = [_sigmoid(b) for b in b_cs]
    rates = [-jnp.exp(a) for a in a_logs]
    g_cs = _each(lambda rate, a, dt: rate * _softplus(a + dt), rates, a_cs, dt_bs)
    g_rs = _each(lambda rate, a, dt: rate * _softplus(a + dt), rates, a_rs, dt_bs)
    gcum_cs = [jnp.sum(jnp.where(incl, g, 0.0), axis=1, keepdims=True) for g in g_rs]
    gcum_rs = [jnp.sum(jnp.where(upper, g, 0.0), axis=0, keepdims=True) for g in g_cs]
    decays = _each(lambda gc, gr: jnp.exp(jnp.where(incl, gc - gr, -1e30)), gcum_cs, gcum_rs)
    kbs = _each(lambda k, b: k * b, ks, betas)
    kk = _each(lambda kb, k: dot(kb, k, "nt"), kbs, ks)
    ts = inv(_each(lambda m, dec: jnp.where(strict, m * dec, 0.0), kk, decays))
    e_gs = [jnp.exp(g) for g in gcum_cs]
    us = _each(lambda t, v, b: _dg(t, v * b, "nn", SOLVE), ts, vs, betas)
    k_cums = _each(lambda t, kb, e: _dg(t, kb * e, "nn", SOLVE), ts, kbs, e_gs)
    qk = _each(lambda q, k: dot(q, k, "nt"), qs, ks)
    qk = _each(lambda m, dec: jnp.where(incl, m * dec, 0.0), qk, decays)
    g_lasts = [jnp.sum(g, axis=0, keepdims=True) for g in g_cs]
    q_decs = _each(lambda q, e: q * e, qs, e_gs)
    k_decs = _each(lambda k, gl, gc: k * jnp.exp(gl - gc), ks, g_lasts, gcum_cs)
    return list(zip(us, k_cums, q_decs, k_decs, qk, g_lasts))


def _dn_step(d, s_prevs, items, zs, gain):
    dot = _bdot(d)
    us, k_cums, q_decs, k_decs, qks, g_lasts = [list(t) for t in zip(*items)]
    v_news = _each(lambda u, kc, s: u - dot(kc, s, "nn"), us, k_cums, s_prevs)
    inter = _each(lambda qd, s: dot(qd, s, "nn"), q_decs, s_prevs)
    outs = _each(lambda o, qk, vn: o + dot(qk, vn, "nn"), inter, qks, v_news)
    s_nexts = _each(lambda s, gl, kd, vn: s * jnp.exp(gl) + dot(kd, vn, "tn"), s_prevs, g_lasts, k_decs, v_news)
    return _each(lambda o, z: _rms(o, gain) * _silu(z), outs, zs), s_nexts


def _split_heads(t):
    return [t[:, h * DN_DH:(h + 1) * DN_DH] for h in range(t.shape[1] // DN_DH)]


def _dn_gates(ps, a_rows, ad):
    hs = range(DN_HEADS)
    return ([_col(ps, 12 + h) for h in hs], [_row(a_rows, h) for h in hs], [_col(ps, 8 + h) for h in hs],
            [_col(_row(ad, 0), h) for h in hs], [_col(_row(ad, 1), h) for h in hs])


def _head_rows(vals):
    row = _iota((8, LANES), 0)
    tile = jnp.zeros((8, LANES), F32)
    for h, val in enumerate(vals):
        tile = tile + jnp.where(row == h, val, 0.0)
    return tile


def _cparams(n_axes):
    return pltpu.CompilerParams(dimension_semantics=("arbitrary",) * n_axes, vmem_limit_bytes=VMEM_LIMIT)


def _first_visit(acc_axes):
    cond = None
    for a in acc_axes:
        here = pl.program_id(a) == 0
        cond = here if cond is None else jnp.logical_and(cond, here)
    return cond


def _tile(ref, widen=False):
    val = ref[...]
    shape = val.shape
    while len(shape) > 2 and shape[0] == 1:
        shape = shape[1:]
    val = val.reshape(shape)
    return val.astype(F32) if widen and val.dtype == BF16 else val


def _store(ref, val, first):
    val = val.astype(ref.dtype).reshape(ref.shape)
    if first is None:
        ref[...] = val
        return

    @pl.when(first)
    def _():
        ref[...] = val

    @pl.when(jnp.logical_not(first))
    def _():
        ref[...] += val


def _specs(ops):
    return [pl.BlockSpec(block, imap) for _, block, imap in ops]


def tile_fwd(name, fn, grid, ins, outs, raw=()):
    n_in = len(ins)

    def body(*refs):
        pids = tuple(pl.program_id(a) for a in range(len(grid)))
        firsts = [_first_visit(o[4]) if o[4] else None for o in outs]
        res = fn(False, pids, *[_tile(r, i not in raw) for i, r in enumerate(refs[:n_in])])
        for ref, val, first in zip(refs[n_in:], res, firsts):
            _store(ref, val, first)

    out = pl.pallas_call(
        body, grid=grid, in_specs=_specs(ins),
        out_specs=[pl.BlockSpec(o[2], o[3]) for o in outs],
        out_shape=[jax.ShapeDtypeStruct(o[0], o[1]) for o in outs],
        name=name, compiler_params=_cparams(len(grid)),
    )(*[a for a, _, _ in ins])
    return out


def tile_bwd(name, fn, grid, ins, cots, diff, adds=None, raw=()):
    adds = adds or {}
    n_in, n_cot = len(ins), len(cots)
    add_pos = sorted(adds)
    diff_idx = [d[0] for d in diff]
    out_desc = [d[2] if len(d) > 2 and d[2] is not None else (ins[d[0]][0].shape, ins[d[0]][1], ins[d[0]][2]) for d in diff]
    out_dtypes = [d[3] if len(d) > 3 else F32 for d in diff]

    def body(*refs):
        pids = tuple(pl.program_id(a) for a in range(len(grid)))
        firsts = [_first_visit(d[1]) if d[1] else None for d in diff]
        vals = [_tile(r, i not in raw) for i, r in enumerate(refs[:n_in])]
        cot_vals = [_tile(r, True) for r in refs[n_in:n_in + n_cot]]
        add_vals = [_tile(r) for r in refs[n_in + n_cot:n_in + n_cot + len(add_pos)]]
        out_refs = refs[n_in + n_cot + len(add_pos):]

        def f(*dv):
            full = list(vals)
            for i, val in zip(diff_idx, dv):
                full[i] = val
            return fn(True, pids, *full)

        prim, vjp = jax.vjp(f, *[vals[i].astype(F32) for i in diff_idx])
        grads = list(vjp(tuple(c.astype(o.dtype) for c, o in zip(cot_vals, prim))))
        for pos, val in zip(add_pos, add_vals):
            extra = val.astype(F32) if firsts[pos] is None else jnp.where(firsts[pos], val.astype(F32), 0.0)
            grads[pos] = grads[pos] + extra
        for ref, val, first in zip(out_refs, grads, firsts):
            _store(ref, val, first)

    all_ins = list(ins) + list(cots) + [adds[p] for p in add_pos]
    out = pl.pallas_call(
        body, grid=grid, in_specs=_specs(all_ins),
        out_specs=[pl.BlockSpec(o[1], o[2]) for o in out_desc],
        out_shape=[jax.ShapeDtypeStruct(o[0], dt) for o, dt in zip(out_desc, out_dtypes)],
        name=name, compiler_params=_cparams(len(grid)),
    )(*[a for a, _, _ in all_ins])
    return out


def _pick(dim, cands):
    for c in cands:
        if dim % c == 0:
            return c
    return dim


MM_MIN_STEPS = 8
MM_TILES = (1024, 512, 1408, 256, 128)


def mm(name, a, b, mode, add=None, out_dtype=F32, blocks=None):
    wide = None
    if mode == "nn":
        (m, kk), n = a.shape, b.shape[-1]
    elif mode == "nt":
        (m, kk), n = a.shape, b.shape[-2]
    else:
        (kk, m), n = a.shape, b.shape[1]
    if blocks is not None:
        lo, n_blk = blocks
        wide = b.shape[-1] if mode != "tn" else n // n_blk
        if mode == "nn":
            n = wide * n_blk
    tm = _pick(m, MM_TILES)
    if mode == "nt" and blocks is not None:
        tn, tk = _pick(n, MM_TILES), _pick(wide, MM_TILES[:-1])
    elif blocks is not None:
        tn, tk = _pick(wide, MM_TILES[:-1]), _pick(kk, MM_TILES)
    else:
        tn, tk = _pick(n, MM_TILES), _pick(kk, MM_TILES)
    if mode == "tn" or blocks is None:
        tk = _pick(kk, (2048,) + MM_TILES)
    nk = kk // tk
    align = LANES if mode == "tn" else 16
    while (m // tm) * (n // tn) * nk < MM_MIN_STEPS and tm // 2 >= 256 and (tm // 2) % align == 0:
        tm //= 2
    a_spec = pl.BlockSpec((tk, tm), lambda i, j, k: (k, i)) if mode == "tn" else pl.BlockSpec((tm, tk), lambda i, j, k: (i, k))
    o_spec = pl.BlockSpec((tm, tn), lambda i, j, k: (i, j))
    out_shape = (m, n)
    if blocks is None:
        b_spec = pl.BlockSpec((tn, tk), lambda i, j, k: (j, k)) if mode == "nt" else pl.BlockSpec((tk, tn), lambda i, j, k: (k, j))
    elif mode == "nn":
        per = wide // tn
        b_spec = pl.BlockSpec((1, tk, tn), lambda i, j, k: (lo + j // per, k, j % per))
    elif mode == "nt":
        per = wide // tk
        b_spec = pl.BlockSpec((1, tn, tk), lambda i, j, k: (lo + k // per, j, k % per))
    else:
        per = wide // tn
        b_spec = pl.BlockSpec((tk, tn), lambda i, j, k: (k, j))
        o_spec = pl.BlockSpec((1, tm, tn), lambda i, j, k: (j // per, i, j % per))
        out_shape = (n_blk, m, wide)

    def body(*refs):
        a_ref, b_ref = refs[0], refs[1]
        add_ref = refs[2] if add is not None else None
        o_ref, acc = refs[-2], refs[-1]
        k = pl.program_id(2)
        part = _bdot_impl(_tile(a_ref), _tile(b_ref), mode)

        @pl.when(k == 0)
        def _():
            acc[...] = part

        @pl.when(k > 0)
        def _():
            acc[...] += part

        @pl.when(k == nk - 1)
        def _():
            res = acc[...]
            if add_ref is not None:
                res = res + add_ref[...]
            o_ref[...] = res.astype(o_ref.dtype).reshape(o_ref.shape)

    operands = [a, b] + ([add] if add is not None else [])
    in_specs = [a_spec, b_spec] + ([o_spec] if add is not None else [])
    return pl.pallas_call(
        body, grid=(m // tm, n // tn, nk), in_specs=in_specs, out_specs=o_spec,
        out_shape=jax.ShapeDtypeStruct(out_shape, out_dtype),
        scratch_shapes=[pltpu.VMEM((tm, tn), F32)],
        name=name, compiler_params=_cparams(3),
    )(*operands)


def _rows(x, width=None, off=0, tm=256):
    width = x.shape[1] if width is None else width
    return (x, (tm, width), lambda i, off=off: (i, off))


def _whole(x):
    nd = x.ndim
    return (x, x.shape, lambda *pids, nd=nd: (0,) * nd)


RMS_ROWS = 512


def _rms_ops(x, gain):
    return [_rows(x, tm=RMS_ROWS), _whole(gain)]


def rms_fwd(name, x, gain):
    s, dm = x.shape
    return tile_fwd(name, _rms_fn, (s // RMS_ROWS,), _rms_ops(x, gain), [((s, dm), BF16, (RMS_ROWS, dm), lambda i: (i, 0), ())])[0]


def rms_bwd(name, x, gain, dh, dres):
    s = x.shape[0]
    return tile_bwd(name, _rms_fn, (s // RMS_ROWS,), _rms_ops(x, gain), [_rows(dh, tm=RMS_ROWS)], [(0, ()), (1, (0,))],
                    adds={0: _rows(dres, tm=RMS_ROWS)})


def loss_call(y, t):
    s, dm = y.shape
    dy, part = tile_fwd("loss", _loss_fn, (s // 256,), [_rows(y), _rows(t)],
                        [((s, dm), F32, (256, dm), lambda i: (i, 0), ()), ((8, LANES), F32, (8, LANES), lambda i: (0, 0), (0,))])
    return dy, part[0, 0]


def _fox_prep_ops(pm, gq, gk):
    tm = 512
    return [(pm, (tm, LANES), lambda i, j: (i, C_FQ // LANES + j)), (pm, (tm, LANES), lambda i, j: (i, C_FK // LANES + j)),
            _whole(gq), _whole(gk)]


def fox_prep_fwd(name, pm, gq, gk):
    s = pm.shape[0]
    out = ((s, BRANCH), BF16, (512, LANES), lambda i, j: (i, j), ())
    return tile_fwd(name, _fox_prep_fn, (s // 512, 4), _fox_prep_ops(pm, gq, gk), [out, out])


def fox_prep_bwd(name, pm, gq, gk, dqn, dkn):
    s = pm.shape[0]
    cot = lambda g: (g, (512, LANES), lambda i, j: (i, j))
    own = ((s, BRANCH), (512, LANES), lambda i, j: (i, j))
    return tile_bwd(name, _fox_prep_fn, (s // 512, 4), _fox_prep_ops(pm, gq, gk), [cot(dqn), cot(dkn)],
                    [(0, (), own, BF16), (1, (), own, BF16), (2, (0, 1)), (3, (0, 1))])


def _fox_gate_ops(f_t, bias):
    return [(f_t, (1,) + f_t.shape[1:], lambda h: (h, 0, 0)), (bias, (1, 1, 1), lambda h: (h, 0, 0))]


def fox_gate_fwd(name, f_t, bias):
    n_h = f_t.shape[0]
    return tile_fwd(name, _fox_gate_fn, (n_h,), _fox_gate_ops(f_t, bias),
                    [(f_t.shape, F32, (1,) + f_t.shape[1:], lambda h: (h, 0, 0), ())])[0]


def fox_gate_bwd(name, f_t, bias, dcum):
    n_h = f_t.shape[0]
    return tile_bwd(name, _fox_gate_fn, (n_h,), _fox_gate_ops(f_t, bias),
                    [(dcum, (1,) + f_t.shape[1:], lambda h: (h, 0, 0))], [(0, ()), (1, ())])


FOX_GROUPS = 4


def _fox_groups(s):
    per = s // FOX_BLOCK // FOX_GROUPS
    return [(g * per, per, (g + 1) * per * FOX_BLOCK) for g in range(FOX_GROUPS)]


def _fox_attn_ops(qn, kn, pm, cum_c, cum_r, q0, keys):
    nb = FOX_BLOCK
    return [(qn, (nb, LANES), lambda p, i: (q0 + i, p)), (kn, (keys, LANES), lambda p, i: (0, p)),
            (pm, (keys, LANES), lambda p, i: (0, C_FV // LANES + p)),
            (cum_c, (1, nb, 1), lambda p, i: (2 * p, q0 + i, 0)), (cum_c, (1, nb, 1), lambda p, i: (2 * p + 1, q0 + i, 0)),
            (cum_r, (1, 1, keys), lambda p, i: (2 * p, 0, 0)), (cum_r, (1, 1, keys), lambda p, i: (2 * p + 1, 0, 0))]


def fox_attn_fwd(name, qn, kn, pm, cum_c, cum_r):
    s = qn.shape[0]
    parts = []
    for g, (q0, n_q, keys) in enumerate(_fox_groups(s)):
        parts.append(tile_fwd(f"{name}_g{g}", functools.partial(_fox_attn_fn, q0), (4, n_q), _fox_attn_ops(qn, kn, pm, cum_c, cum_r, q0, keys),
                              [((n_q * FOX_BLOCK, BRANCH), BF16, (FOX_BLOCK, LANES), lambda p, i: (i, p), ())], raw=(0, 1, 2))[0])
    return jnp.concatenate(parts, axis=0)


def fox_attn_bwd(name, qn, kn, pm, cum_c, cum_r, dy):
    s = qn.shape[0]
    groups = _fox_groups(s)
    d_qn, by_q, tails = [None] * len(groups), [None] * len(groups), [None] * len(groups)
    below = None
    for g in reversed(range(len(groups))):
        q0, n_q, keys = groups[g]
        rows = n_q * FOX_BLOCK
        own_q = ((rows, BRANCH), (FOX_BLOCK, LANES), lambda p, i: (i, p))
        own_k = ((keys, BRANCH), (keys, LANES), lambda p, i: (0, p))
        pair_c = ((4, rows, 1), (1, FOX_BLOCK, 1), lambda p, i: (p, i, 0))
        pair_r = ((4, 1, keys), (1, 1, keys), lambda p, i: (p, 0, 0))
        adds = {}
        if below is not None:
            adds = {1: (below[0],) + own_k[1:], 2: (below[1],) + own_k[1:], 5: (below[2],) + pair_r[1:], 6: (below[3],) + pair_r[1:]}
        g_qn, g_kn, g_v, g_cqa, g_cqb, g_cka, g_ckb = tile_bwd(
            f"{name}_g{g}", functools.partial(_fox_attn_fn, q0), (4, n_q), _fox_attn_ops(qn, kn, pm, cum_c, cum_r, q0, keys),
            [(dy, (FOX_BLOCK, LANES), lambda p, i, q0=q0: (q0 + i, p))],
            [(0, (), own_q), (1, (1,), own_k), (2, (1,), own_k), (3, (), pair_c), (4, (), pair_c), (5, (1,), pair_r), (6, (1,), pair_r)],
            adds=adds)
        below = (g_kn, g_v, g_cka, g_ckb)
        lo = groups[g - 1][2] if g else 0
        d_qn[g] = g_qn
        by_q[g] = jnp.stack([g_cqa[:, :, 0], g_cqb[:, :, 0]], axis=1).reshape(8, rows)
        tails[g] = (g_kn[lo:], g_v[lo:], jnp.stack([g_cka[:, 0, lo:], g_ckb[:, 0, lo:]], axis=1).reshape(8, keys - lo))
    d_cum = jnp.concatenate(by_q, axis=1) + jnp.concatenate([t[2] for t in tails], axis=1)
    return jnp.concatenate(d_qn, axis=0), jnp.concatenate([t[0] for t in tails], axis=0), jnp.concatenate([t[1] for t in tails], axis=0), d_cum


def sconv_ops(pm, w):
    s = pm.shape[0]
    blk = lambda c0: (pm, (s, LANES), lambda j, c0=c0: (0, c0 // LANES + j))
    return [blk(C_SB), blk(C_SC), blk(C_SV), (w, (w.shape[0], LANES), lambda j: (0, j))]


def dnconv_ops(pm, w):
    s = pm.shape[0]
    return [(pm, (s, LANES), lambda j: (0, C_DN // LANES + j)), (w, (w.shape[0], LANES), lambda j: (0, j))]


def ffn_ops(ug, uv, w):
    s = ug.shape[0]
    n_t = D_FF // LANES
    return [(ug, (s, LANES), lambda j: (0, j)), (uv, (s, LANES), lambda j: (0, j)),
            (w, (w.shape[0], LANES), lambda j: (0, j)), (w, (w.shape[0], LANES), lambda j: (0, n_t + j))]


def _col_out(s, width, dtype=F32):
    return ((s, width), dtype, (s, LANES), lambda j: (0, j), ())


def _col_cot(g):
    return (g, (g.shape[0], LANES), lambda j: (0, j))


def merge_ops(yp, pm):
    gate = lambda b: (pm, (256, D_MODEL), lambda i, b=b: (i, C_GATE // D_MODEL + b))
    return [_rows(yp[0]), _rows(yp[1]), _rows(yp[2]), gate(0), gate(1), gate(2)]


def ple_ops(gpre, pe, x):
    return [_rows(gpre), _rows(pe), _rows(x)]


def adam_call(name, w, g, m, v):
    shape = w.shape
    last = shape[-1]
    rows = w.size // last
    flat = lambda t: t.reshape(rows, last)
    tm = rows
    for cand in (512, 256, 128, 64, 32, 16, 8):
        if rows % cand == 0 and cand * last * 4 <= 2 * 1024 * 1024:
            tm = cand
            break
    spec = lambda t: (flat(t), (tm, last), lambda i: (i, 0))
    out = ((rows, last), F32, (tm, last), lambda i: (i, 0), ())
    res = tile_fwd(name, _adam_fn, (rows // tm,), [spec(w), spec(g), spec(m), spec(v)], [out, out, out])
    return [r.reshape(shape) for r in res]


def _adam_layers_fn(d, pids, w, m, v, g0, g1):
    g = jnp.where(pids[0] == 0, g0, g1)
    return (g,) + _adam_fn(d, pids, w, g, m, v)


def adam_layers(name, w, m, v, g0, g1):
    _, rows, cols = w.shape
    tm = _row_tile(rows, cols)
    n_t = rows // tm
    lay = lambda t: (t, (1, tm, cols), lambda l, i: (l, i, 0))
    ins = [lay(w), lay(m), lay(v), (g0, (tm, cols), lambda l, i: (i * (1 - l) + (n_t - 1) * l, 0)), (g1, (tm, cols), lambda l, i: (i * l, 0))]
    out = (w.shape, F32, (1, tm, cols), lambda l, i: (l, i, 0), ())
    return tile_fwd(name, _adam_layers_fn, (2, n_t), ins, [out, out, out, out])


def adam_w_in(name, w, m, v, g0, g1):
    rows, n_l, cols = w.shape

    def body(w_ref, m_ref, v_ref, g0_ref, g1_ref, g_out, d_out, m_out, v_out):
        step = 64

        def update(at):
            g0, g1 = g0_ref[at, :], g1_ref[at, :]
            layer = _iota((g0.shape[0], n_l, LANES), 1)
            g = jnp.where(layer == 0, g0[:, None, :], g1[:, None, :])
            delta, m2, v2 = _adam_fn(False, None, w_ref[at], g, m_ref[at], v_ref[at])
            for ref, val in ((g_out, g), (d_out, delta), (m_out, m2), (v_out, v2)):
                ref[at] = val

        def some_rows(i, carry):
            update(pl.ds(pl.multiple_of(i * step, step), step))
            return carry

        lax.fori_loop(0, rows // step, some_rows, 0)
        if rows % step:
            update(pl.ds(rows - rows % step, rows % step))

    both = pl.BlockSpec((rows, n_l, LANES), lambda j: (0, 0, j))
    one = pl.BlockSpec((rows, LANES), lambda j: (0, j))
    return pl.pallas_call(
        body, grid=(cols // LANES,), in_specs=[both, both, both, one, one], out_specs=[both] * 4,
        out_shape=[jax.ShapeDtypeStruct(w.shape, F32)] * 4, name=name, compiler_params=_cparams(1),
    )(w, m, v, g0, g1)


DN_GROUP = 4


def _dn_local_specs(rev_n=None):
    rows = DN_GROUP * DN_CHUNK
    idx = (lambda j: j) if rev_n is None else (lambda j: rev_n - 1 - j)
    return [pl.BlockSpec((rows, 3 * BRANCH), lambda j: (idx(j), 0)), pl.BlockSpec((rows, LANES), lambda j: (idx(j), 0)),
            pl.BlockSpec((DN_GROUP, DN_HEADS, DN_CHUNK), lambda j: (idx(j), 0, 0)), pl.BlockSpec((2, DN_HEADS), lambda j: (0, 0))]


def _dn_group_inputs(qkv, ps, a_rows, c):
    lo = c * DN_CHUNK
    heads = _split_heads(qkv[lo:lo + DN_CHUNK])
    return heads[0:4], heads[4:8], heads[8:12], ps[lo:lo + DN_CHUNK], a_rows[c]


def dn_local_fwd(name, dn_act, ps, a_rows, ad):
    s = dn_act.shape[0]
    n_c, n_g = s // DN_CHUNK, s // (DN_GROUP * DN_CHUNK)
    rows = DN_GROUP * DN_CHUNK

    def body(qkv_ref, ps_ref, ar_ref, ad_ref, u_ref, kc_ref, qd_ref, kd_ref, qk_ref, gl_ref):
        qkv, ps_v, a_rows_v, ad_v = qkv_ref[...], ps_ref[...], ar_ref[...], ad_ref[...]
        args = [[] for _ in range(8)]
        for c in range(DN_GROUP):
            q4, k4, v4, ps_c, ar_c = _dn_group_inputs(qkv, ps_v, a_rows_v, c)
            for lst, vals in zip(args, (q4, k4, v4) + _dn_gates(ps_c, ar_c, ad_v)):
                lst.extend(vals)
        everything = _dn_local(False, *args)
        for c in range(DN_GROUP):
            res = everything[c * DN_HEADS:(c + 1) * DN_HEADS]
            at = pl.ds(c * DN_CHUNK, DN_CHUNK)
            for ref, i in ((u_ref, 0), (kc_ref, 1), (qd_ref, 2), (kd_ref, 3)):
                ref[at, :] = jnp.concatenate([r[i] for r in res], axis=1)
            for h in range(DN_HEADS):
                qk_ref[c, h] = res[h][4]
            gl_ref[c] = _head_rows([r[5] for r in res])

    wide = pl.BlockSpec((rows, BRANCH), lambda j: (j, 0))
    return pl.pallas_call(
        body, grid=(n_g,), in_specs=_dn_local_specs(),
        out_specs=[wide, wide, wide, wide, pl.BlockSpec((DN_GROUP, DN_HEADS, DN_CHUNK, DN_CHUNK), lambda j: (j, 0, 0, 0)),
                   pl.BlockSpec((DN_GROUP, 8, LANES), lambda j: (j, 0, 0))],
        out_shape=[jax.ShapeDtypeStruct((s, BRANCH), F32)] * 4 + [jax.ShapeDtypeStruct((n_c, DN_HEADS, DN_CHUNK, DN_CHUNK), F32),
                                                                 jax.ShapeDtypeStruct((n_c, 8, LANES), F32)],
        name=name, compiler_params=_cparams(1),
    )(dn_act, ps, a_rows, ad)


def dn_local_bwd(name, dn_act, ps, a_rows, ad, cots):
    s = dn_act.shape[0]
    n_c, n_g = s // DN_CHUNK, s // (DN_GROUP * DN_CHUNK)
    rows = DN_GROUP * DN_CHUNK

    def body(qkv_ref, ps_ref, ar_ref, ad_ref, du_ref, dkc_ref, dqd_ref, dkd_ref, dqk_ref, dgl_ref, dqkv_ref, dps_ref, dar_ref, dad_ref):
        first = pl.program_id(0) == 0
        qkv, ps_v, a_rows_v, ad_v = qkv_ref[...], ps_ref[...], ar_ref[...], ad_ref[...]
        d_wide = [r[...] for r in (du_ref, dkc_ref, dqd_ref, dkd_ref)]
        qs, ks, vs, ps_cs, ar_cs, cot = [], [], [], [], [], []
        for c in range(DN_GROUP):
            q4, k4, v4, ps_c, ar_c = _dn_group_inputs(qkv, ps_v, a_rows_v, c)
            qs, ks, vs, ps_cs, ar_cs = qs + q4, ks + k4, vs + v4, ps_cs + [ps_c], ar_cs + [ar_c]
            lo = c * DN_CHUNK
            d_tiles = [_split_heads(t[lo:lo + DN_CHUNK]) for t in d_wide]
            d_gl = dgl_ref[c]
            cot += [(d_tiles[0][h], d_tiles[1][h], d_tiles[2][h], d_tiles[3][h], dqk_ref[c, h], _col(_row(d_gl, h), 0))
                    for h in range(DN_HEADS)]

        def f(qs, ks, vs, ps_cs, ar_cs, ad_v):
            gates = [[] for _ in range(5)]
            for ps_c, ar_c in zip(ps_cs, ar_cs):
                for lst, vals in zip(gates, _dn_gates(ps_c, ar_c, ad_v)):
                    lst.extend(vals)
            return _dn_local(True, qs, ks, vs, *gates)

        _, vjp = jax.vjp(f, qs, ks, vs, ps_cs, ar_cs, ad_v)
        d_q, d_k, d_v, d_ps, d_ar, d_ad = vjp(cot)
        for c in range(DN_GROUP):
            at, hs = pl.ds(c * DN_CHUNK, DN_CHUNK), slice(c * DN_HEADS, (c + 1) * DN_HEADS)
            dqkv_ref[at, :] = jnp.concatenate(d_q[hs] + d_k[hs] + d_v[hs], axis=1).astype(dqkv_ref.dtype)
            dps_ref[at, :] = d_ps[c]
            dar_ref[c] = d_ar[c]
        _store(dad_ref, d_ad, first)

    wide = pl.BlockSpec((rows, BRANCH), lambda j: (j, 0))
    specs = _dn_local_specs()
    return pl.pallas_call(
        body, grid=(n_g,),
        in_specs=specs + [wide, wide, wide, wide, pl.BlockSpec((DN_GROUP, DN_HEADS, DN_CHUNK, DN_CHUNK), lambda j: (j, 0, 0, 0)),
                          pl.BlockSpec((DN_GROUP, 8, LANES), lambda j: (j, 0, 0))],
        out_specs=specs,
        out_shape=[jax.ShapeDtypeStruct((s, 3 * BRANCH), F32), jax.ShapeDtypeStruct((s, LANES), F32),
                   jax.ShapeDtypeStruct((n_c, DN_HEADS, DN_CHUNK), F32), jax.ShapeDtypeStruct((2, DN_HEADS), F32)],
        name=name, compiler_params=_cparams(1),
    )(dn_act, ps, a_rows, ad, *cots)


def _dn_scan_specs(n_c, rev):
    idx = (lambda j: n_c - 1 - j) if rev else (lambda j: j)
    wide = pl.BlockSpec((DN_CHUNK, BRANCH), lambda j: (idx(j), 0))
    return [wide, wide, wide, wide, pl.BlockSpec((1, DN_HEADS, DN_CHUNK, DN_CHUNK), lambda j: (idx(j), 0, 0, 0)),
            pl.BlockSpec((1, 8, LANES), lambda j: (idx(j), 0, 0)), pl.BlockSpec((DN_CHUNK, BRANCH), lambda j: (idx(j), C_DZ // BRANCH)),
            pl.BlockSpec((1, DN_DH), lambda j: (0, 0))]


def _dn_scan_tiles(refs):
    u_ref, kc_ref, qd_ref, kd_ref, qk_ref, gl_ref, z_ref, g_ref = refs
    wide = [_split_heads(r[...]) for r in (u_ref, kc_ref, qd_ref, kd_ref)]
    gl = gl_ref[0]
    return [(wide[0][h], wide[1][h], wide[2][h], wide[3][h], qk_ref[0, h], _col(_row(gl, h), 0)) for h in range(DN_HEADS)], \
        _split_heads(z_ref[...].astype(F32)), g_ref[...]


def dn_scan_fwd(name, local, pm, gain):
    s = pm.shape[0]
    n_c = s // DN_CHUNK

    def body(*refs):
        y_ref, hist_ref, state = refs[8:]

        @pl.when(pl.program_id(0) == 0)
        def _():
            state[...] = jnp.zeros_like(state)

        hist_ref[0] = state[...]
        per_head, z4, gain_v = _dn_scan_tiles(refs[:8])
        ys, s_nexts = _dn_step(False, [state[h] for h in range(DN_HEADS)], per_head, z4, gain_v)
        for h in range(DN_HEADS):
            state[h] = s_nexts[h]
        y_ref[...] = jnp.concatenate(ys, axis=1).astype(y_ref.dtype)

    return pl.pallas_call(
        body, grid=(n_c,), in_specs=_dn_scan_specs(n_c, False),
        out_specs=[pl.BlockSpec((DN_CHUNK, BRANCH), lambda j: (j, 0)),
                   pl.BlockSpec((1, DN_HEADS, DN_DH, DN_DH), lambda j: (j, 0, 0, 0))],
        out_shape=[jax.ShapeDtypeStruct((s, BRANCH), BF16), jax.ShapeDtypeStruct((n_c, DN_HEADS, DN_DH, DN_DH), F32)],
        scratch_shapes=[pltpu.VMEM((DN_HEADS, DN_DH, DN_DH), F32)],
        name=name, compiler_params=_cparams(1),
    )(*local, pm, gain)


def dn_scan_bwd(name, local, pm, gain, hist, dy):
    s = pm.shape[0]
    n_c = s // DN_CHUNK

    def body(*refs):
        hist_ref, dy_ref = refs[8:10]
        du_ref, dkc_ref, dqd_ref, dkd_ref, dqk_ref, dgl_ref, dz_ref, dg_ref, d_state = refs[10:]
        first = pl.program_id(0) == 0

        @pl.when(first)
        def _():
            d_state[...] = jnp.zeros_like(d_state)

        per_head, z4, gain_v = _dn_scan_tiles(refs[:8])
        _, vjp = jax.vjp(functools.partial(_dn_step, True), [hist_ref[0, h] for h in range(DN_HEADS)], per_head, z4, gain_v)
        d_s, grads, d_z, d_gain = vjp((_split_heads(dy_ref[...].astype(F32)), [d_state[h] for h in range(DN_HEADS)]))
        for h in range(DN_HEADS):
            d_state[h] = d_s[h]
        for ref, i in ((du_ref, 0), (dkc_ref, 1), (dqd_ref, 2), (dkd_ref, 3)):
            ref[...] = jnp.concatenate([g[i] for g in grads], axis=1)
        dz_ref[...] = jnp.concatenate(d_z, axis=1).astype(dz_ref.dtype)
        for h in range(DN_HEADS):
            dqk_ref[0, h] = grads[h][4]
        dgl_ref[0] = _head_rows([g[5] for g in grads])
        _store(dg_ref, d_gain, first)

    rev = lambda j: n_c - 1 - j
    specs = _dn_scan_specs(n_c, True)
    return pl.pallas_call(
        body, grid=(n_c,),
        in_specs=specs + [pl.BlockSpec((1, DN_HEADS, DN_DH, DN_DH), lambda j: (rev(j), 0, 0, 0)),
                          pl.BlockSpec((DN_CHUNK, BRANCH), lambda j: (rev(j), 0))],
        out_specs=specs[:6] + [pl.BlockSpec((DN_CHUNK, BRANCH), lambda j: (rev(j), 0)), specs[7]],
        out_shape=[jax.ShapeDtypeStruct((s, BRANCH), F32)] * 4 + [
            jax.ShapeDtypeStruct((n_c, DN_HEADS, DN_CHUNK, DN_CHUNK), F32), jax.ShapeDtypeStruct((n_c, 8, LANES), F32),
            jax.ShapeDtypeStruct((s, BRANCH), BF16), jax.ShapeDtypeStruct((1, DN_DH), F32)],
        scratch_shapes=[pltpu.VMEM((DN_HEADS, DN_DH, DN_DH), F32)],
        name=name, compiler_params=_cparams(1),
    )(*local, pm, gain, hist, dy)


def _seq_layouts(cols, s):
    return cols.T.reshape(cols.shape[1], s // LANES, LANES)


def layer_fwd(li, x, p, w, more_weights=None):
    s = x.shape[0]
    n = lambda t: f"{t}_l{li}"
    h = rms_fwd(n("rms_mix"), x, w["g_mix"])
    pm = mm(n("in_main"), h, w["in_main"], "nn")
    ps = mm(n("in_small"), h, w["in_small"], "nn")
    qn, kn = fox_prep_fwd(n("fox_prep"), pm, w["gq"], w["gk"])
    f_t = _seq_layouts(ps[:, 0:8], s)
    cum = fox_gate_fwd(n("fox_gate"), f_t, w["b_f"])
    cum_c, cum_r = cum.reshape(8, s, 1), cum.reshape(8, 1, s)
    y_fox = fox_attn_fwd(n("fox_attn"), qn, kn, pm, cum_c, cum_r)
    y_sc = tile_fwd(n("sconv"), _sconv_fn, (BRANCH // LANES,), sconv_ops(pm, w["sc_conv_w"]), [_col_out(s, BRANCH, BF16)])[0]
    dn_act = tile_fwd(n("dnconv"), _dnconv_fn, (3 * BRANCH // LANES,), dnconv_ops(pm, w["dn_conv_w"]), [_col_out(s, 3 * BRANCH)])[0]
    a_rows = ps[:, 12:16].reshape(s // DN_CHUNK, DN_CHUNK, DN_HEADS).transpose(0, 2, 1)
    dn_local = dn_local_fwd(n("dn_local"), dn_act, ps, a_rows, w["ad"])
    y_dn, hist = dn_scan_fwd(n("dn_scan"), dn_local, pm, w["dn_gain"])
    ys = (y_fox, y_sc, y_dn)
    if more_weights is not None:
        w = {**w, **more_weights(y_dn)}
    yp = [mm(n(f"branch{b}"), ys[b], w["branch"][b], "nn", blocks=(0, N_CHIPS)) for b in range(3)]
    merged = tile_fwd(n("merge"), _merge_fn, (s // 256,), merge_ops(yp, pm), [((s, D_MODEL), BF16, (256, D_MODEL), lambda i: (i, 0), ())])[0]
    x1 = mm(n("w_o"), merged, w["o"], "nn", add=x)
    h2 = rms_fwd(n("rms_ffn"), x1, w["g_ffn"])
    ug = mm(n("up_g"), h2, w["up"], "nn", blocks=(0, 2))
    uv = mm(n("up_v"), h2, w["up"], "nn", blocks=(2, 2))
    act = tile_fwd(n("ffn_act"), _ffn_act_fn, (D_FF // LANES,), ffn_ops(ug, uv, w["ffn_conv_w"]), [_col_out(s, D_FF, BF16)])[0]
    x2 = mm(n("down"), act, w["down"], "nn", add=x1)
    h3 = rms_fwd(n("rms_ple"), x2, w["g_ple"])
    gpre = mm(n("ple_gate"), h3, w["pg"], "nn")
    pe = mm(n("ple_emb"), p, w["ple"], "nn", blocks=(0, N_CHIPS))
    x3 = tile_fwd(n("ple"), _ple_fn, (s // 256,), ple_ops(gpre, pe, x2), [((s, D_MODEL), F32, (256, D_MODEL), lambda i: (i, 0), ())])[0]
    saved = dict(x=x, h=h, pm=pm, ps=ps, qn=qn, kn=kn, f_t=f_t, cum_c=cum_c, cum_r=cum_r, ys=ys, dn_act=dn_act, dn_local=dn_local,
                 a_rows=a_rows, hist=hist, yp=yp, merged=merged, x1=x1, h2=h2, ug=ug, uv=uv, act=act, x2=x2, h3=h3,
                 gpre=gpre, pe=pe, p=p)
    return x3, saved, w


def hang_on(w, token):
    zero = token[0, 0]
    small = ("g_mix", "g_ffn", "g_ple", "gq", "gk", "b_f", "ad", "dn_gain", "sc_conv_w", "dn_conv_w", "ffn_conv_w")
    return {**w, **{k: w[k] + zero for k in small}}


def layer_bwd(li, dx3, sv, w, hooks=None):
    hooks = hooks or {}

    def stage(key, after, w):
        return hang_on(w, hooks[key](after, g)) if key in hooks else w

    s = dx3.shape[0]
    n = lambda t: f"{t}_l{li}"
    g = {}
    col_own = lambda width: ((s, width), (s, LANES), lambda j: (0, j))
    d_gpre, d_pe = tile_bwd(n("ple_bwd"), _ple_fn, (s // 256,), ple_ops(sv["gpre"], sv["pe"], sv["x2"]), [_rows(dx3)],
                            [(0, (), None, BF16), (1, (), None, BF16)])
    g["w_ple"] = mm(n("d_w_ple"), sv["p"], d_pe, "tn", blocks=(0, N_CHIPS))
    g["w_ple_gate"] = mm(n("d_w_pg"), sv["h3"], d_gpre, "tn").reshape(N_CHIPS, -1, D_MODEL)
    dh3 = mm(n("d_h3"), d_gpre, w["pg"], "nt")
    dx2, d_g_ple = rms_bwd(n("rms_ple_bwd"), sv["x2"], w["g_ple"], dh3, dx3)
    dact = mm(n("d_act"), dx2, w["down"], "nt")
    g["w_down"] = mm(n("d_w_down"), sv["act"], dx2, "tn").reshape(N_CHIPS, -1, D_MODEL)
    taps_own = ((w["ffn_conv_w"].shape[0], D_FF), (w["ffn_conv_w"].shape[0], LANES), lambda j: (0, j))
    d_ug, d_uv, d_fw_g, d_fw_v = tile_bwd(n("ffn_act_bwd"), _ffn_act_fn, (D_FF // LANES,), ffn_ops(sv["ug"], sv["uv"], w["ffn_conv_w"]),
                                          [_col_cot(dact)], [(0, (), None, BF16), (1, (), None, BF16), (2, (), taps_own), (3, (), taps_own)])
    g["ffn_conv_w"] = jnp.concatenate([d_fw_g, d_fw_v], axis=1)
    g["w_up"] = jnp.concatenate([mm(n("d_w_up_g"), sv["h2"], d_ug, "tn", blocks=(0, 2)), mm(n("d_w_up_v"), sv["h2"], d_uv, "tn", blocks=(0, 2))])
    dh2 = mm(n("d_h2_v"), d_uv, w["up"], "nt", blocks=(2, 2), add=mm(n("d_h2_g"), d_ug, w["up"], "nt", blocks=(0, 2)))
    dx1, d_g_ffn = rms_bwd(n("rms_ffn_bwd"), sv["x1"], w["g_ffn"], dh2, dx2)
    w = stage("mid", dx1, w)
    dmerged = mm(n("d_merged"), dx1, w["o"], "nt")
    g["w_o"] = mm(n("d_w_o"), sv["merged"], dx1, "tn").reshape(N_CHIPS, -1, D_MODEL)
    gate_own = ((s, D_MODEL), (256, D_MODEL), lambda i: (i, 0))
    d_yp0, d_yp1, d_yp2, d_g0, d_g1, d_g2 = tile_bwd(
        n("merge_bwd"), _merge_fn, (s // 256,), merge_ops(sv["yp"], sv["pm"]), [_rows(dmerged)],
        [(0, (), None, BF16), (1, (), None, BF16), (2, (), None, BF16), (3, (), gate_own, BF16), (4, (), gate_own, BF16), (5, (), gate_own, BF16)])
    d_yp = (d_yp0, d_yp1, d_yp2)
    g["w_branch"] = jnp.concatenate([mm(n(f"d_w_branch{b}"), sv["ys"][b], d_yp[b], "tn", blocks=(0, N_CHIPS)) for b in range(3)], axis=1)
    d_ys = [mm(n(f"d_y{b}"), d_yp[b], w["branch"][b], "nt", blocks=(0, N_CHIPS)) for b in range(3)]
    w = stage("late", d_ys[2], w)
    *d_local, d_z, d_dngain = dn_scan_bwd(n("dn_scan_bwd"), sv["dn_local"], sv["pm"], w["dn_gain"], sv["hist"], d_ys[2])
    d_dnact, d_ps_dn, d_arows, d_ad = dn_local_bwd(n("dn_local_bwd"), sv["dn_act"], sv["ps"], sv["a_rows"], w["ad"], d_local)
    g["ad"], g["dn_norm_gain"] = d_ad, d_dngain[0]
    d_dnqkv, g["dn_conv_w"] = tile_bwd(n("dnconv_bwd"), _dnconv_fn, (3 * BRANCH // LANES,), dnconv_ops(sv["pm"], w["dn_conv_w"]),
                                       [_col_cot(d_dnact)], [(0, (), col_own(3 * BRANCH), BF16), (1, ())])
    d_sb, d_sc, d_sv, g["sc_conv_w"] = tile_bwd(n("sconv_bwd"), _sconv_fn, (BRANCH // LANES,), sconv_ops(sv["pm"], w["sc_conv_w"]), [_col_cot(d_ys[1])],
                                                [(0, (), col_own(BRANCH), BF16), (1, (), col_own(BRANCH), BF16), (2, (), col_own(BRANCH), BF16), (3, ())])
    w = stage("last", d_dnqkv, w)
    d_qn, d_kn, d_fv, d_cum = fox_attn_bwd(n("fox_attn_bwd"), sv["qn"], sv["kn"], sv["pm"], sv["cum_c"], sv["cum_r"], d_ys[0])
    d_ft, d_bf = fox_gate_bwd(n("fox_gate_bwd"), sv["f_t"], w["b_f"], d_cum.reshape(8, s // LANES, LANES))
    g["b_fox_f"] = d_bf.reshape(8)
    d_fq, d_fk, d_gq, d_gk = fox_prep_bwd(n("fox_prep_bwd"), sv["pm"], w["gq"], w["gk"], d_qn, d_kn)
    g["fox_q_gain"] = d_gq[0, :FOX_DH] + d_gq[0, FOX_DH:]
    g["fox_k_gain"] = d_gk[0, :FOX_DH] + d_gk[0, FOX_DH:]
    d_pm = jnp.concatenate([d_fq, d_fk, d_fv.astype(BF16), d_sb, d_sc, d_sv, d_dnqkv, d_z, d_g0, d_g1, d_g2], axis=1)
    d_a_cols = d_arows.transpose(0, 2, 1).reshape(s, DN_HEADS)
    d_f_cols = d_ft.reshape(8, s).T
    d_ps = d_ps_dn + jnp.concatenate([d_f_cols, jnp.zeros((s, 4), F32), d_a_cols, jnp.zeros((s, LANES - 16), F32)], axis=1)
    g["w_in"] = chip_blocks_w_in(mm(n("d_w_in_main"), d_pm, sv["h"], "tn"), mm(n("d_w_in_small"), d_ps, sv["h"], "tn"))
    w = stage("w_in", g["w_in"], w)
    dh = mm(n("d_h_small"), d_ps, w["in_small"], "nt", add=mm(n("d_h_main"), d_pm, w["in_main"], "nt"))
    dx, d_g_mix = rms_bwd(n("rms_mix_bwd"), sv["x"], w["g_mix"], dh, dx1)
    g["g_mix"], g["g_ffn"], g["g_ple"] = d_g_mix[0], d_g_ffn[0], d_g_ple[0]
    return dx, g


IN_SHARD = 2052
MAIN_RANGES = ((0, 1536), (1544, 3080), (3080, 4616), (4624, 5136), (5136, 8208))
SMALL_RANGES = ((1536, 1544), (4616, 4620), (4620, 4624))


def _from_chip_blocks(blocks, ranges):
    parts = []
    for lo, hi in ranges:
        for k in range(N_CHIPS):
            a0, a1 = max(lo, k * IN_SHARD), min(hi, (k + 1) * IN_SHARD)
            if a0 < a1:
                parts.append(blocks[k][:, a0 - k * IN_SHARD:a1 - k * IN_SHARD])
    return parts


def split_w_in(blocks):
    main = jnp.concatenate(_from_chip_blocks(blocks, MAIN_RANGES), axis=1)
    pad = jnp.zeros((blocks.shape[1], LANES - 16), blocks.dtype)
    return main, jnp.concatenate(_from_chip_blocks(blocks, SMALL_RANGES) + [pad], axis=1)


def chip_blocks_w_in(main, small):
    ranges = sorted([(lo, hi, "m") for lo, hi in MAIN_RANGES] + [(lo, hi, "s") for lo, hi in SMALL_RANGES])
    offs, m_off, s_off = {}, 0, 0
    for lo, hi in MAIN_RANGES:
        offs[lo] = m_off
        m_off += hi - lo
    for lo, hi in SMALL_RANGES:
        offs[lo] = s_off
        s_off += hi - lo
    blocks = []
    for k in range(N_CHIPS):
        parts = []
        for lo, hi, src in ranges:
            a0, a1 = max(lo, k * IN_SHARD), min(hi, (k + 1) * IN_SHARD)
            if a0 < a1:
                arr = main if src == "m" else small
                parts.append(arr[offs[lo] + a0 - lo:offs[lo] + a1 - lo])
        blocks.append(jnp.concatenate(parts, axis=0))
    return jnp.stack(blocks)


def later_weights(got):
    g_branch, g_o, g_up, g_down, g_pg, g_ple = got
    branch = g_branch.reshape(N_CHIPS, 3, BRANCH, -1)
    return dict(branch=[branch[:, b] for b in range(3)], o=g_o.reshape(D_MODEL, D_MODEL), up=g_up,
                down=g_down.reshape(D_FF, D_MODEL), pg=g_pg.reshape(D_MODEL, D_MODEL), ple=g_ple)


def layer_weights(li, got, conv, a):
    main, small = split_w_in(got[0])
    tile2 = lambda v: jnp.concatenate([v, v])[None, :]
    rest = later_weights(got[1:]) if len(got) > 1 else {}
    return dict(
        in_main=main, in_small=small, **rest,
        g_mix=a["g_mix"][li][None, :], g_ffn=a["g_ffn"][li][None, :], g_ple=a["g_ple"][li][None, :],
        gq=tile2(a["fox_q_gain"][li]), gk=tile2(a["fox_k_gain"][li]), b_f=a["b_fox_f"][li].reshape(8, 1, 1),
        ad=jnp.stack([a["dn_a_log"][li], a["dn_dt_bias"][li]]), dn_gain=a["dn_norm_gain"][li][None, :],
        sc_conv_w=conv["sc_conv_w"][li], dn_conv_w=conv["dn_conv_w"][li], ffn_conv_w=conv["ffn_conv_w"][li])


def pack_rows(arrs, dtype):
    flat = jnp.concatenate([t.reshape(-1).astype(dtype) for t in arrs])
    pad = (-flat.shape[0]) % (8 * LANES)
    if pad:
        flat = jnp.concatenate([flat, jnp.zeros((pad,), dtype)])
    return flat.reshape(-1, LANES)


def unpack_rows(buf, shapes):
    flat = buf.reshape(-1)
    out, off = [], 0
    for shp in shapes:
        size = 1
        for dim in shp:
            size *= dim
        out.append(flat[off:off + size].reshape(shp))
        off += size
    return out


def chip_shard(t, axis, k):
    width = t.shape[axis] // N_CHIPS
    return lax.slice_in_dim(t, k * width, (k + 1) * width, axis=axis)


ANY = pl.BlockSpec(memory_space=pl.ANY)


def _position():
    x, y, c = lax.axis_index("x"), lax.axis_index("y"), lax.axis_index("c")
    return x, y, c, [(1 - x, y), (x, 1 - y), (1 - x, 1 - y)]


def gather_small(name, block):
    m_per, n = block.shape

    def body(x_ref, out_ref, token, send_sems, recv_sems, local_sem):
        token[...] = jnp.zeros_like(token)
        x, y, c, chips = _position()
        me, sibling = (x, y, c), (x, y, 1 - c)

        def rows(px, py, pc):
            return out_ref.at[pl.ds((4 * px + 2 * py + pc) * m_per, m_per), :]

        def copy(k, blk, to, src=None):
            return pltpu.make_async_remote_copy(src_ref=rows(*blk) if src is None else src, dst_ref=rows(*blk),
                                                send_sem=send_sems.at[k], recv_sem=recv_sems.at[k], device_id=to, device_id_type=MESH)

        mine = pltpu.make_async_copy(x_ref, rows(*me), local_sem)
        mine.start()
        first = [copy(0, me, sibling, src=x_ref)] + [copy(1 + j, me, (*chip, c), src=x_ref) for j, chip in enumerate(chips)]
        for cp in first:
            cp.start()
        passed = [copy(4 + j, (*chip, c), sibling) for j, chip in enumerate(chips)]
        for j, chip in enumerate(chips):
            copy(1 + j, (*chip, c), me).wait_recv()
            passed[j].start()
        copy(0, sibling, me).wait_recv()
        for j, chip in enumerate(chips):
            copy(4 + j, (*chip, 1 - c), me).wait_recv()
        for cp in first + passed:
            cp.wait_send()
        mine.wait()

    in_vmem = pl.BlockSpec(memory_space=pltpu.VMEM)
    return pl.pallas_call(
        body, out_shape=[jax.ShapeDtypeStruct((8 * m_per, n), block.dtype), jax.ShapeDtypeStruct((8, LANES), F32)],
        in_specs=[in_vmem], out_specs=[in_vmem, in_vmem],
        scratch_shapes=[pltpu.SemaphoreType.DMA((7,)), pltpu.SemaphoreType.DMA((7,)), pltpu.SemaphoreType.DMA],
        name=name, compiler_params=pltpu.CompilerParams(vmem_limit_bytes=VMEM_LIMIT),
    )(block)


def _sems(n):
    return [pltpu.SemaphoreType.DMA((n,)), pltpu.SemaphoreType.DMA((n,))]


def _split_cols(rows):
    return (rows // 2) % 16 != 0


def _half(ref, which, lead=()):
    rows, cols = ref.shape[-2:]
    if _split_cols(rows):
        return ref.at[(*lead, slice(None), pl.ds(which * (cols // 2), cols // 2))]
    return ref.at[(*lead, pl.ds(which * (rows // 2), rows // 2), slice(None))]


def _half_shape(rows, cols):
    return (rows, cols // 2) if _split_cols(rows) else (rows // 2, cols)


def gather_layer(name, shards):
    n_w = len(shards)

    def body(*refs):
        ins, outs = refs[:n_w], refs[n_w:2 * n_w]
        token, send_sems, recv_sems = refs[2 * n_w:]
        token[...] = jnp.zeros_like(token)
        x, y, c, chips = _position()
        sibling = (x, y, 1 - c)

        def part(w, px, py, pc):
            return _half(outs[w], pc, (2 * px + py,))

        def copy(k, w, blk, to, src=None):
            return pltpu.make_async_remote_copy(src_ref=part(w, *blk) if src is None else src, dst_ref=part(w, *blk),
                                                send_sem=send_sems.at[k], recv_sem=recv_sems.at[k], device_id=to, device_id_type=MESH)

        pairs = [(w, j, chip) for w in range(n_w) for j, chip in enumerate(chips)]
        first = [copy(3 * w + j, w, (x, y, c), (*chip, c), src=_half(ins[w], c)) for w, j, chip in pairs]
        for cp in first:
            cp.start()
        passed = [copy(3 * n_w + 3 * w + j, w, (*chip, c), sibling) for w, j, chip in pairs]
        for (w, j, chip), fwd in zip(pairs, passed):
            copy(3 * w + j, w, (*chip, c), (x, y, c)).wait_recv()
            fwd.start()
        for w, j, chip in pairs:
            copy(3 * n_w + 3 * w + j, w, (*chip, 1 - c), (x, y, c)).wait_recv()
        for cp in first + passed:
            cp.wait_send()

    out = pl.pallas_call(
        body, out_shape=[jax.ShapeDtypeStruct((N_CHIPS,) + s.shape, s.dtype) for s in shards] + [jax.ShapeDtypeStruct((8, LANES), F32)],
        in_specs=[ANY] * n_w, out_specs=[ANY] * n_w + [pl.BlockSpec(memory_space=pltpu.VMEM)], scratch_shapes=_sems(6 * n_w), name=name,
    )(*shards)
    return out[:n_w], out[n_w]


def swap_halves(name, grads):
    n_w = len(grads)

    def body(*refs):
        ins, outs = refs[:n_w], refs[n_w:2 * n_w]
        send_sems, recv_sems = refs[2 * n_w:]
        x, y, c, _ = _position()
        cps = [pltpu.make_async_remote_copy(src_ref=_half(ins[w], 1 - c, (slice(None),)), dst_ref=outs[w],
                                            send_sem=send_sems.at[w], recv_sem=recv_sems.at[w], device_id=(x, y, 1 - c),
                                            device_id_type=MESH) for w in range(n_w)]
        for cp in cps:
            cp.start()
        for cp in cps:
            cp.wait()

    return pl.pallas_call(
        body, out_shape=[jax.ShapeDtypeStruct((N_CHIPS,) + _half_shape(*g.shape[1:]), g.dtype) for g in grads],
        in_specs=[ANY] * n_w, out_specs=[ANY] * n_w, scratch_shapes=_sems(n_w), name=name,
    )(*grads)


def scatter_chips(name, partials):
    n_w = len(partials)

    def body(*refs):
        ins, outs = refs[:n_w], refs[n_w:2 * n_w]
        send_sems, recv_sems = refs[2 * n_w:]
        x, y, c, chips = _position()
        cps = [pltpu.make_async_remote_copy(src_ref=ins[w].at[2 * cx + cy], dst_ref=outs[w].at[j], send_sem=send_sems.at[3 * w + j],
                                            recv_sem=recv_sems.at[3 * w + j], device_id=(cx, cy, c), device_id_type=MESH)
               for w in range(n_w) for j, (cx, cy) in enumerate(chips)]
        for cp in cps:
            cp.start()
        for cp in cps:
            cp.wait()

    return pl.pallas_call(
        body, out_shape=[jax.ShapeDtypeStruct((3,) + p.shape[1:], p.dtype) for p in partials],
        in_specs=[ANY] * n_w, out_specs=[ANY] * n_w, scratch_shapes=_sems(3 * n_w), name=name,
    )(*partials)


def share_halves(name, bufs):
    n_w = len(bufs)

    def body(*refs):
        outs = refs[n_w:2 * n_w]
        send_sems, recv_sems = refs[2 * n_w:]
        x, y, c, _ = _position()

        def copy(w, pc):
            half = _half(outs[w], pc)
            return pltpu.make_async_remote_copy(src_ref=half, dst_ref=half, send_sem=send_sems.at[w], recv_sem=recv_sems.at[w],
                                                device_id=(x, y, 1 - c), device_id_type=MESH)

        for w in range(n_w):
            copy(w, c).start()
        for w in range(n_w):
            copy(w, 1 - c).wait_recv()
            copy(w, c).wait_send()

    return pl.pallas_call(
        body, out_shape=[jax.ShapeDtypeStruct(b.shape, b.dtype) for b in bufs], in_specs=[ANY] * n_w, out_specs=[ANY] * n_w,
        input_output_aliases={w: w for w in range(n_w)}, scratch_shapes=_sems(n_w), name=name,
    )(*bufs)


HBM = pl.BlockSpec(memory_space=pltpu.HBM)
SEM = pl.BlockSpec(memory_space=pltpu.SEMAPHORE)
EFFECT = pltpu.SideEffectType.DATAFLOW_SIDE_EFFECTING


def _exchange_copies(kind, srcs, lands):
    x, y, c, chips = _position()
    out = []
    for src, land in zip(srcs, lands):
        if kind == "swap":
            out.append((_half(src, 1 - c, (slice(None),)), land, (x, y, 1 - c)))
            continue
        for j, (cx, cy) in enumerate(chips):
            if kind == "gather":
                out.append((src, land.at[2 * x + y], (cx, cy, c)))
            else:
                out.append((src.at[2 * cx + cy], land.at[j], (cx, cy, c)))
    return out


def _land_shapes(kind, srcs):
    if kind == "gather":
        return [(N_CHIPS,) + s.shape for s in srcs]
    if kind == "swap":
        return [(N_CHIPS,) + _half_shape(*s.shape[1:]) for s in srcs]
    return [(3,) + s.shape[1:] for s in srcs]


def exchange_start(name, kind, srcs):
    n_w = len(srcs)
    shapes = _land_shapes(kind, srcs)
    n_sem = n_w if kind == "swap" else 3 * n_w

    def body(*refs):
        ins, lands = refs[:n_w], refs[n_w:2 * n_w]
        send_sems, recv_sems = refs[2 * n_w:2 * n_w + 2]
        token = refs[-1]
        for i, (src, dst, dev) in enumerate(_exchange_copies(kind, ins, lands)):
            pltpu.make_async_remote_copy(src_ref=src, dst_ref=dst, send_sem=send_sems.at[i], recv_sem=recv_sems.at[i],
                                         device_id=dev, device_id_type=MESH).start()
        token[...] = jnp.zeros_like(token)

    out = pl.pallas_call(
        body, name=name,
        out_shape=(pltpu.SemaphoreType.DMA((n_sem,)), pltpu.SemaphoreType.DMA((n_sem,)),
                   *[pltpu.HBM(s.shape, s.dtype) for s in srcs], *[pltpu.HBM(shp, s.dtype) for shp, s in zip(shapes, srcs)],
                   jax.ShapeDtypeStruct((8, LANES), F32)),
        in_specs=(HBM,) * (2 * n_w), out_specs=(SEM, SEM) + (HBM,) * (2 * n_w) + (pl.BlockSpec(memory_space=pltpu.VMEM),),
        input_output_aliases={i: 2 + i for i in range(2 * n_w)},
        compiler_params=pltpu.CompilerParams(has_side_effects=EFFECT),
    )(*[pltpu.with_memory_space_constraint(s, pltpu.HBM) for s in srcs],
      *[pltpu.with_memory_space_constraint(lax.empty(shp, s.dtype), pltpu.HBM) for shp, s in zip(shapes, srcs)])
    return (kind, n_w, out[:-1]), out[-1]


def exchange_wait(name, handle, after):
    kind, n_w, (send_sems, recv_sems, *thru) = handle
    n_sem = n_w if kind == "swap" else 3 * n_w

    def body(*refs):
        ins, lands = refs[:n_w], refs[n_w:2 * n_w]
        send_sems, recv_sems = refs[2 * n_w:2 * n_w + 2]
        for i, (src, dst, dev) in enumerate(_exchange_copies(kind, ins, lands)):
            cp = pltpu.make_async_remote_copy(src_ref=src, dst_ref=dst, send_sem=send_sems.at[i], recv_sem=recv_sems.at[i],
                                              device_id=dev, device_id_type=MESH)
            cp.wait_send()
            cp.wait_recv()

    out = pl.pallas_call(
        body, name=name, out_shape=tuple(pltpu.HBM(t.shape, t.dtype) for t in thru),
        in_specs=(HBM,) * (2 * n_w) + (SEM, SEM, pl.BlockSpec(memory_space=pl.ANY)), out_specs=(HBM,) * (2 * n_w),
        input_output_aliases={i: i for i in range(2 * n_w)},
        compiler_params=pltpu.CompilerParams(has_side_effects=EFFECT),
    )(*thru, send_sems, recv_sems, after)
    return list(out[:n_w]), list(out[n_w:])


def _row_tile(rows, cols):
    best = rows
    if rows * cols * 4 <= 1024 * 1024:
        return rows
    for t in range(16, rows, 16):
        if rows % t == 0 and t * cols * 4 <= 1024 * 1024:
            best = t
    return best


def pair_sum(name, pos, grad, from_sibling):
    _, rows, cols = grad.shape
    h_rows, h_cols = _half_shape(rows, cols)
    tr = _row_tile(h_rows, h_cols)
    n_t = h_rows // tr

    def body(pos_ref, g_ref, s_ref, b_ref, f_ref):
        tot = g_ref[...] + s_ref[...]
        b_ref[...] = tot.astype(BF16)

        @pl.when(pl.program_id(1) == pos_ref[1])
        def _():
            f_ref[...] = tot[0]

    blk = pl.BlockSpec((1, tr, h_cols), lambda i, k, pos: (k, i, 0))
    if _split_cols(rows):
        mine = pl.BlockSpec((1, tr, h_cols), lambda i, k, pos: (k, i, pos[0]))
    else:
        mine = pl.BlockSpec((1, tr, h_cols), lambda i, k, pos: (k, pos[0] * n_t + i, 0))
    return pl.pallas_call(
        body, grid_spec=pltpu.PrefetchScalarGridSpec(
            num_scalar_prefetch=1, grid=(n_t, N_CHIPS), in_specs=[mine, blk],
            out_specs=[blk, pl.BlockSpec((tr, h_cols), lambda i, k, pos: (i, 0))]),
        out_shape=[jax.ShapeDtypeStruct((N_CHIPS, h_rows, h_cols), BF16), jax.ShapeDtypeStruct((h_rows, h_cols), F32)],
        name=name, compiler_params=_cparams(2),
    )(pos, grad, from_sibling)


def chip_sum(name, pos, own, landed, split_cols):
    half, cols = own.shape
    tr = _row_tile(half, cols)
    n_t = half // tr

    def body(pos_ref, p_ref, l_ref, o_ref):
        o_ref[...] = ((p_ref[...] + l_ref[0].astype(F32)) + l_ref[1].astype(F32)) + l_ref[2].astype(F32)

    if split_cols:
        out_spec, out_shape = pl.BlockSpec((tr, cols), lambda i, pos: (i, pos[0])), (half, 2 * cols)
    else:
        out_spec, out_shape = pl.BlockSpec((tr, cols), lambda i, pos: (pos[0] * n_t + i, 0)), (2 * half, cols)
    return pl.pallas_call(
        body, grid_spec=pltpu.PrefetchScalarGridSpec(
            num_scalar_prefetch=1, grid=(n_t,),
            in_specs=[pl.BlockSpec((tr, cols), lambda i, pos: (i, 0)), pl.BlockSpec((3, tr, cols), lambda i, pos: (0, i, 0))],
            out_specs=out_spec),
        out_shape=jax.ShapeDtypeStruct(out_shape, F32), name=name, compiler_params=_cparams(1),
    )(pos, own, landed)


def reduce_scatter_layer(tag, pos, grads):
    n = lambda t: f"{t}_{tag}"
    from_sibling = swap_halves(n("swap_halves"), grads)
    sums = [pair_sum(n(f"pair_sum{w}"), pos, g, s) for w, (g, s) in enumerate(zip(grads, from_sibling))]
    landed = scatter_chips(n("scatter_chips"), [b for b, _ in sums])
    halves = [chip_sum(n(f"chip_sum{w}"), pos, own, l, _split_cols(g.shape[1])) for w, ((_, own), l, g) in enumerate(zip(sums, landed, grads))]
    return share_halves(n("share_halves"), halves)


class OverlappedReduceScatter:
    def __init__(self, tag, pos, grads):
        self.n = lambda t: f"{t}_{tag}"
        self.pos, self.grads = pos, grads
        self.swap, self.token = exchange_start(self.n("swap_start"), "swap", grads)

    def middle(self, after):
        self.grads, from_sibling = exchange_wait(self.n("swap_wait"), self.swap, after)
        self.sums = [pair_sum(self.n(f"pair_sum{w}"), self.pos, g, s) for w, (g, s) in enumerate(zip(self.grads, from_sibling))]
        self.scatter, self.token = exchange_start(self.n("scatter_start"), "scatter", [b for b, _ in self.sums])

    def finish(self, after):
        _, landed = exchange_wait(self.n("scatter_wait"), self.scatter, after)
        halves = [chip_sum(self.n(f"chip_sum{w}"), self.pos, own, l, _split_cols(g.shape[1]))
                  for w, ((_, own), l, g) in enumerate(zip(self.sums, landed, self.grads))]
        return share_halves(self.n("share_halves"), halves)


def sum_devices(gathered):
    m_per = gathered.shape[0] // 8

    def body(g_ref, o_ref):
        tot = g_ref[pl.ds(0, m_per), :]
        for dev in range(1, 8):
            tot = tot + g_ref[pl.ds(dev * m_per, m_per), :]
        o_ref[...] = tot

    return pl.pallas_call(
        body, out_shape=jax.ShapeDtypeStruct((m_per, gathered.shape[1]), F32),
        in_specs=[pl.BlockSpec(memory_space=pltpu.VMEM)], out_specs=pl.BlockSpec(memory_space=pltpu.VMEM), name="sum_devices",
    )(gathered)


def kernel(x, p, g_mix, w_in, b_fox_f, fox_q_gain, fox_k_gain, sc_conv_w, dn_conv_w, dn_a_log, dn_dt_bias, dn_norm_gain, w_branch, w_o, g_ffn, w_up, ffn_conv_w, w_down, g_ple, w_ple_gate, w_ple, loss_target, m_g_mix, m_w_in, m_b_fox_f, m_fox_q_gain, m_fox_k_gain, m_sc_conv_w, m_dn_conv_w, m_dn_a_log, m_dn_dt_bias, m_dn_norm_gain, m_w_branch, m_w_o, m_g_ffn, m_w_up, m_ffn_conv_w, m_w_down, m_g_ple, m_w_ple_gate, m_w_ple, v_g_mix, v_w_in, v_b_fox_f, v_fox_q_gain, v_fox_k_gain, v_sc_conv_w, v_dn_conv_w, v_dn_a_log, v_dn_dt_bias, v_dn_norm_gain, v_w_branch, v_w_o, v_g_ffn, v_w_up, v_ffn_conv_w, v_w_down, v_g_ple, v_w_ple_gate, v_w_ple):
    a = dict(g_mix=g_mix, w_in=w_in, b_fox_f=b_fox_f, fox_q_gain=fox_q_gain, fox_k_gain=fox_k_gain, sc_conv_w=sc_conv_w,
             dn_conv_w=dn_conv_w, dn_a_log=dn_a_log, dn_dt_bias=dn_dt_bias, dn_norm_gain=dn_norm_gain, w_branch=w_branch, w_o=w_o,
             g_ffn=g_ffn, w_up=w_up, ffn_conv_w=ffn_conv_w, w_down=w_down, g_ple=g_ple, w_ple_gate=w_ple_gate, w_ple=w_ple)
    mom = dict(g_mix=m_g_mix, w_in=m_w_in, b_fox_f=m_b_fox_f, fox_q_gain=m_fox_q_gain, fox_k_gain=m_fox_k_gain, sc_conv_w=m_sc_conv_w,
               dn_conv_w=m_dn_conv_w, dn_a_log=m_dn_a_log, dn_dt_bias=m_dn_dt_bias, dn_norm_gain=m_dn_norm_gain, w_branch=m_w_branch,
               w_o=m_w_o, g_ffn=m_g_ffn, w_up=m_w_up, ffn_conv_w=m_ffn_conv_w, w_down=m_w_down, g_ple=m_g_ple, w_ple_gate=m_w_ple_gate,
               w_ple=m_w_ple)
    var = dict(g_mix=v_g_mix, w_in=v_w_in, b_fox_f=v_b_fox_f, fox_q_gain=v_fox_q_gain, fox_k_gain=v_fox_k_gain, sc_conv_w=v_sc_conv_w,
               dn_conv_w=v_dn_conv_w, dn_a_log=v_dn_a_log, dn_dt_bias=v_dn_dt_bias, dn_norm_gain=v_dn_norm_gain, w_branch=v_w_branch,
               w_o=v_w_o, g_ffn=v_g_ffn, w_up=v_w_up, ffn_conv_w=v_ffn_conv_w, w_down=v_w_down, g_ple=v_g_ple, w_ple_gate=v_w_ple_gate,
               w_ple=v_w_ple)
    cx, cy, cc = lax.axis_index("x"), lax.axis_index("y"), lax.axis_index("c")
    chip = 2 * cx + cy
    pos = jnp.stack([cc, chip]).astype(jnp.int32)

    def as_blocks(t):
        return t.reshape(2, -1, t.shape[-1])

    def own_block_in(got, shards):
        return [lax.dynamic_update_slice(g, s[None], (chip, 0, 0)) for g, s in zip(got, shards)]

    conv_shapes = [a[nm].shape for nm in CONVS]
    conv_all, conv_token = gather_small("gather_conv_w", pack_rows([a[nm] for nm in CONVS], F32))
    def layer_block(nm, t, li):
        return as_blocks(t)[li]

    shards0 = [(layer_block(nm, a[nm], 0) + conv_token[0, 0]).astype(BF16) for nm in BIG]
    got0, gathered_token = gather_layer("gather_w_in_l0", shards0[:1])
    shards0[1:] = [s + gathered_token[0, 0].astype(BF16) for s in shards0[1:]]
    gather0, gather0_token = exchange_start("gather_start_l0", "gather", shards0[1:])
    shards1 = [(layer_block(nm, a[nm], 1) + gather0_token[0, 0]).astype(BF16) for nm in BIG]
    gather1, gather1_in_token = exchange_start("gather_start_w_in_l1", "gather", shards1[:1])
    shards1[1:] = [s + gather1_in_token[0, 0].astype(BF16) for s in shards1[1:]]
    gather1_rest, gather1_token = exchange_start("gather_start_l1", "gather", shards1[1:])
    conv_rows = conv_all.shape[0] // 8
    conv_chip = [unpack_rows(conv_all[2 * k * conv_rows:(2 * k + 1) * conv_rows], conv_shapes) for k in range(N_CHIPS)]
    conv = {nm: jnp.concatenate([conv_chip[k][i] for k in range(N_CHIPS)], axis=2) for i, nm in enumerate(CONVS)}

    weights, saved = [None, None], [None, None]
    first_weights = hang_on(layer_weights(0, own_block_in(got0, shards0[:1]), conv, a), gather1_token)

    def rest_of_layer0(after):
        mine, got = exchange_wait("gather_wait_l0", gather0, after)
        return later_weights(own_block_in(got, mine))

    act, saved[0], weights[0] = layer_fwd(0, x[0], p[0, 0], first_weights, more_weights=rest_of_layer0)
    mine1, got1 = exchange_wait("gather_wait_w_in_l1", gather1, act)

    def rest_of_layer1(after):
        mine, got = exchange_wait("gather_wait_l1", gather1_rest, after)
        return later_weights(own_block_in(got, mine))

    act, saved[1], weights[1] = layer_fwd(1, act, p[1, 0], layer_weights(1, own_block_in(got1, mine1), conv, a),
                                          more_weights=rest_of_layer1)
    d_act, loss_part = loss_call(act, loss_target[0])
    loss = lax.psum(loss_part, ("x", "y", "c"))
    layer_grads = [None, None]
    d_act, layer_grads[1] = layer_bwd(1, d_act, saved[1], weights[1])
    rs1 = OverlappedReduceScatter("l1", pos, [layer_grads[1][nm] for nm in BIG])
    rs0 = []

    def stage_mid(after, g):
        rs1.middle(after)
        return rs1.token

    def stage_late(after, g):
        rs0.append(OverlappedReduceScatter("l0", pos, [g[nm] for nm in BIG[1:]]))
        return rs0[0].token

    def stage_last(after, g):
        rs0[0].middle(after)
        return rs0[0].token

    def stage_w_in(after, g):
        rs0.append(OverlappedReduceScatter("w_in_l0", pos, [g["w_in"]]))
        return rs0[1].token

    d_act, layer_grads[0] = layer_bwd(0, d_act, saved[0], hang_on(weights[0], rs1.token),
                                      hooks=dict(mid=stage_mid, late=stage_late, last=stage_last, w_in=stage_w_in))
    rs0[1].middle(d_act)
    reduced = [rs0[0].finish(rs0[1].token), rs1.finish(rs0[1].token)]
    grad_x = d_act[None]

    def both(nm):
        return jnp.stack([layer_grads[0][nm], layer_grads[1][nm]])

    local = {nm: both(nm) for nm in ("g_mix", "b_fox_f", "fox_q_gain", "fox_k_gain", "dn_norm_gain", "g_ffn", "g_ple", "sc_conv_w",
                                      "dn_conv_w", "ffn_conv_w")}
    local["dn_a_log"] = jnp.stack([layer_grads[li]["ad"][0] for li in range(2)])
    local["dn_dt_bias"] = jnp.stack([layer_grads[li]["ad"][1] for li in range(2)])

    small_names = SMALL + CONVS
    small_shapes = [local[nm].shape for nm in small_names]
    small_sum = sum_devices(gather_small("gather_small_grads", pack_rows([local[nm] for nm in small_names], F32))[0])
    small_grads = dict(zip(small_names, unpack_rows(small_sum, small_shapes)))
    for nm in CONVS:
        width = a[nm].shape[2]
        small_grads[nm] = lax.dynamic_slice_in_dim(small_grads[nm], chip * width, width, axis=2)

    grads, deltas, new_m, new_v = dict(small_grads), {}, {}, {}
    for nm in small_names:
        deltas[nm], new_m[nm], new_v[nm] = adam_call(f"adam_{nm}", a[nm], grads[nm], mom[nm], var[nm])
    for i, nm in enumerate(BIG[1:]):
        res = adam_layers(f"adam_{nm}", as_blocks(a[nm]), as_blocks(mom[nm]), as_blocks(var[nm]), reduced[0][i], reduced[1][1 + i])
        grads[nm], deltas[nm], new_m[nm], new_v[nm] = [r.reshape(a[nm].shape) for r in res]
    stored = lambda t: jnp.transpose(t, (2, 0, 1))
    res = adam_w_in("adam_w_in", stored(a["w_in"]), stored(mom["w_in"]), stored(var["w_in"]), rs0[1].finish(deltas["w_ple"])[0], reduced[1][0])
    grads["w_in"], deltas["w_in"], new_m["w_in"], new_v["w_in"] = [jnp.transpose(r, (1, 2, 0)) for r in res]
    return (loss, grad_x, *[grads[nm] for nm in WEIGHTS], *[deltas[nm] for nm in WEIGHTS], *[new_m[nm] for nm in WEIGHTS],
            *[new_v[nm] for nm in WEIGHTS])
```

```python
import functools

import jax
import jax.numpy as jnp
from jax import lax
from jax.experimental import pallas as pl
from jax.experimental.pallas import tpu as pltpu

F32 = jnp.float32
BF16 = jnp.bfloat16
HI = lax.Precision.HIGHEST
SOLVE = lax.Precision.HIGH
MESH = pl.DeviceIdType.MESH

D_MODEL = 1024
BRANCH = 512
FOX_DH = 64
DN_DH = 128
DN_HEADS = 4
DN_CHUNK = 64
FOX_BLOCK = 128
D_FF = 2816
EPS = 1e-6
N_CHIPS = 4
LANES = 128

ADAM_LR, ADAM_B1, ADAM_B2, ADAM_EPS, ADAM_WD, ADAM_STEP = 0.001, 0.9, 0.999, 1e-08, 0.01, 10

VMEM_LIMIT = 56 * 1024 * 1024

C_FQ, C_FK, C_FV, C_SB, C_SC, C_SV, C_DN, C_DZ, C_GATE = 0, 512, 1024, 1536, 2048, 2560, 3072, 4608, 5120
IN_MAIN = 8192

BIG = ("w_in", "w_branch", "w_o", "w_up", "w_down", "w_ple_gate", "w_ple")
CONVS = ("sc_conv_w", "dn_conv_w", "ffn_conv_w")
SMALL = ("g_mix", "b_fox_f", "fox_q_gain", "fox_k_gain", "dn_a_log", "dn_dt_bias", "dn_norm_gain", "g_ffn", "g_ple")
WEIGHTS = ("g_mix", "w_in", "b_fox_f", "fox_q_gain", "fox_k_gain", "sc_conv_w", "dn_conv_w", "dn_a_log", "dn_dt_bias",
           "dn_norm_gain", "w_branch", "w_o", "g_ffn", "w_up", "ffn_conv_w", "w_down", "g_ple", "w_ple_gate", "w_ple")


def _iota(shape, dim):
    return lax.broadcasted_iota(jnp.int32, shape, dim)


def _dg(a, b, mode, prec=None):
    dims = {"nn": ((1,), (0,)), "nt": ((1,), (1,)), "tn": ((0,), (0,))}[mode]
    return lax.dot_general(a, b, (dims, ((), ())), precision=prec, preferred_element_type=F32)


def _bdot_impl(a, b, mode):
    return _dg(a.astype(BF16), b.astype(BF16), mode)


@functools.partial(jax.custom_vjp, nondiff_argnums=(2,))
def _bdot_diff(a, b, mode):
    return _bdot_impl(a, b, mode)


def _bdot_fwd(a, b, mode):
    return _bdot_impl(a, b, mode), (a, b)


def _bdot_bwd(mode, res, g):
    a, b = res
    if mode == "nn":
        da, db = _bdot_impl(g, b, "nt"), _bdot_impl(a, g, "tn")
    elif mode == "nt":
        da, db = _bdot_impl(g, b, "nn"), _bdot_impl(g, a, "tn")
    else:
        da, db = _bdot_impl(b, g, "nt"), _bdot_impl(a, g, "nn")
    return da.astype(a.dtype), db.astype(b.dtype)


_bdot_diff.defvjp(_bdot_fwd, _bdot_bwd)


def _bdot(d):
    return _bdot_diff if d else _bdot_impl


def _shift_impl(x, k):
    return jnp.where(_iota(x.shape, 0) >= k, pltpu.roll(x, k, 0), 0.0)


def _unshift_impl(g, k):
    n = g.shape[0]
    return jnp.where(_iota(g.shape, 0) < n - k, pltpu.roll(g, n - k, 0), 0.0)


@functools.partial(jax.custom_vjp, nondiff_argnums=(1,))
def _shift_diff(x, k):
    return _shift_impl(x, k)


_shift_diff.defvjp(lambda x, k: (_shift_impl(x, k), None), lambda k, _, g: (_unshift_impl(g, k),))


def _row(w, j):
    return jnp.sum(jnp.where(_iota(w.shape, 0) == j, w, 0.0), axis=0, keepdims=True)


def _col(w, j):
    return jnp.sum(jnp.where(_iota(w.shape, 1) == j, w, 0.0), axis=1, keepdims=True)


def _conv(d, x, w):
    shift = _shift_diff if d else _shift_impl
    taps = w.shape[0]
    y = x * _row(w, taps - 1)
    for j in range(taps - 1):
        y = y + shift(x, taps - 1 - j) * _row(w, j)
    return y


def _softplus(x):
    return jnp.maximum(x, 0.0) + jnp.log(1.0 + jnp.exp(-jnp.abs(x)))


def _sigmoid(x):
    return 0.5 * (jnp.tanh(0.5 * x) + 1.0)


def _silu(x):
    return x * _sigmoid(x)


def _rms(x, gain):
    return x * lax.rsqrt(jnp.mean(x * x, axis=-1, keepdims=True) + EPS) * gain


def _rms_fn(d, pids, x, gain):
    return (_rms(x, gain),)


def _loss_fn(d, pids, y, t):
    e = y - t
    part = 0.5 / D_MODEL * jnp.sum(e * e, keepdims=True)
    return e * (1.0 / D_MODEL), jnp.broadcast_to(part, (8, LANES))


def _fox_prep_fn(d, pids, q, k, gq, gk):
    first = _iota(q.shape, 1) < FOX_DH

    def norm(x, gain):
        sq = x * x
        ss_a = jnp.sum(jnp.where(first, sq, 0.0), axis=1, keepdims=True)
        ss_b = jnp.sum(jnp.where(first, 0.0, sq), axis=1, keepdims=True)
        rs = jnp.where(first, lax.rsqrt(ss_a / FOX_DH + EPS), lax.rsqrt(ss_b / FOX_DH + EPS))
        return x * rs * gain

    return norm(q, gq) * FOX_DH ** -0.5, norm(k, gk)


def _fox_gate_fn(d, pids, f, bias):
    logf = -_softplus(-(f + bias))
    n_r, n_c = logf.shape
    tri = (_iota((n_c, n_c), 0) <= _iota((n_c, n_c), 1)).astype(F32)
    within = _dg(logf, tri, "nn", HI)
    tot = jnp.broadcast_to(jnp.sum(logf, axis=1, keepdims=True), logf.shape)
    below = (_iota((n_r, n_r), 1) < _iota((n_r, n_r), 0)).astype(F32)
    return (within + _dg(below, tot, "nn", HI),)


def _fox_attn_fn(q_block0, d, pids, q, k, v, cq_a, cq_b, ck_a, ck_b):
    dot = _bdot(d)
    first = _iota(q.shape, 1) < FOX_DH
    n_q, n_k = q.shape[0], k.shape[0]
    causal = ((q_block0 + pids[1]) * n_q + _iota((n_q, n_k), 0)) >= _iota((n_q, n_k), 1)

    qs = [jnp.where(first, q, 0.0), jnp.where(first, 0.0, q)]
    s = _each(lambda qh, cq, ck: jnp.where(causal, dot(qh, k, "nt") + cq - ck, -1e30), qs, [cq_a, cq_b], [ck_a, ck_b])
    e = [jnp.exp(si - lax.stop_gradient(jnp.max(si, axis=1, keepdims=True))) for si in s]
    o_a, o_b = [dot(ei * (1.0 / jnp.sum(ei, axis=1, keepdims=True)), v, "nn") for ei in e]
    return (jnp.where(first, o_a, o_b),)


def _sconv_fn(d, pids, sb, sc, sv, w):
    return (sb * _conv(d, sc * sv, w),)


def _dnconv_fn(d, pids, x, w):
    return (_silu(_conv(d, x, w)),)


def _merge_fn(d, pids, y0, y1, y2, g0, g1, g2):
    return (_sigmoid(g0) * y0 + _sigmoid(g1) * y1 + _sigmoid(g2) * y2,)


def _ffn_act_fn(d, pids, ug, uv, wg, wv):
    return (_silu(_conv(d, ug, wg)) * _conv(d, uv, wv),)


def _ple_fn(d, pids, gpre, pe, x):
    return (x + _sigmoid(gpre) * pe,)


def _adam_fn(d, pids, w, g, m, v):
    m2 = ADAM_B1 * m + (1.0 - ADAM_B1) * g
    v2 = ADAM_B2 * v + (1.0 - ADAM_B2) * (g * g)
    m_hat = m2 / (1.0 - ADAM_B1 ** ADAM_STEP)
    v_hat = v2 / (1.0 - ADAM_B2 ** ADAM_STEP)
    delta = -ADAM_LR * (m_hat / (jnp.sqrt(v_hat) + ADAM_EPS) + ADAM_WD * w)
    return delta, m2, v2


def _each(fn, *lists):
    return [fn(*args) for args in zip(*lists)]


def _tri_inv_impl(mats):
    n = mats[0].shape[0]
    r, c = _iota((n, n), 0), _iota((n, n), 1)
    diag_blk = (r >> 4) == (c >> 4)
    eye = (r == c).astype(F32)
    mm = lambda us, ws: _each(lambda u, w: _dg(u, w, "nn", SOLVE), us, ws)
    grow = lambda ps, xs: _each(lambda p, px: p + px, ps, mm(ps, xs))
    x = [jnp.where(diag_blk, -a, 0.0) for a in mats]
    p = [eye + xi for xi in x]
    x2 = mm(x, x)
    p = grow(p, x2)
    x4 = mm(x2, x2)
    p = grow(p, x4)
    p = grow(p, mm(x4, x4))
    y = [-yi for yi in mm(p, [jnp.where(diag_blk, 0.0, a) for a in mats])]
    q = grow([eye + yi for yi in y], mm(y, y))
    return mm(q, p)


@jax.custom_vjp
def _tri_inv_diff(mats):
    return _tri_inv_impl(mats)


def _tri_inv_fwd(mats):
    ts = _tri_inv_impl(mats)
    return ts, ts


def _tri_inv_bwd(ts, gs):
    left = _each(lambda t, g: _dg(t, g, "tn", SOLVE), ts, gs)
    return ([-m for m in _each(lambda l, t: _dg(l, t, "nt", SOLVE), left, ts)],)


_tri_inv_diff.defvjp(_tri_inv_fwd, _tri_inv_bwd)


def _dn_local(d, qs, ks, vs, a_cs, a_rs, b_cs, a_logs, dt_bs):
    dot = _bdot(d)
    inv = _tri_inv_diff if d else _tri_inv_impl
    n = qs[0].shape[0]
    r, c = _iota((n, n), 0), _iota((n, n), 1)
    incl, strict, upper = r >= c, r > c, r <= c
    qs = [q * lax.rsqrt(jnp.sum(q * q, axis=1, keepdims=True) + EPS) * DN_DH ** -0.5 for q in qs]
    ks = [k * lax.rsqrt(jnp.sum(k * k, axis=1, keepdims=True) + EPS) for k in ks]
    betas = [_sigmoid(b) for b in b_cs]
    rates = [-jnp.exp(a) for a in a_logs]
    g_cs = _each(lambda rate, a, dt: rate * _softplus(a + dt), rates, a_cs, dt_bs)
    g_rs = _each(lambda rate, a, dt: rate * _softplus(a + dt), rates, a_rs, dt_bs)
    gcum_cs = [jnp.sum(jnp.where(incl, g, 0.0), axis=1, keepdims=True) for g in g_rs]
    gcum_rs = [jnp.sum(jnp.where(upper, g, 0.0), axis=0, keepdims=True) for g in g_cs]
    decays = _each(lambda gc, gr: jnp.exp(jnp.where(incl, gc - gr, -1e30)), gcum_cs, gcum_rs)
    kbs = _each(lambda k, b: k * b, ks, betas)
    kk = _each(lambda kb, k: dot(kb, k, "nt"), kbs, ks)
    ts = inv(_each(lambda m, dec: jnp.where(strict, m * dec, 0.0), kk, decays))
    e_gs = [jnp.exp(g) for g in gcum_cs]
    us = _each(lambda t, v, b: _dg(t, v * b, "nn", SOLVE), ts, vs, betas)
    k_cums = _each(lambda t, kb, e: _dg(t, kb * e, "nn", SOLVE), ts, kbs, e_gs)
    qk = _each(lambda q, k: dot(q, k, "nt"), qs, ks)
    qk = _each(lambda m, dec: jnp.where(incl, m * dec, 0.0), qk, decays)
    g_lasts = [jnp.sum(g, axis=0, keepdims=True) for g in g_cs]
    q_decs = _each(lambda q, e: q * e, qs, e_gs)
    k_decs = _each(lambda k, gl, gc: k * jnp.exp(gl - gc), ks, g_lasts, gcum_cs)
    return list(zip(us, k_cums, q_decs, k_decs, qk, g_lasts))


def _dn_step(d, s_prevs, items, zs, gain):
    dot = _bdot(d)
    us, k_cums, q_decs, k_decs, qks, g_lasts = [list(t) for t in zip(*items)]
    v_news = _each(lambda u, kc, s: u - dot(kc, s, "nn"), us, k_cums, s_prevs)
    inter = _each(lambda qd, s: dot(qd, s, "nn"), q_decs, s_prevs)
    outs = _each(lambda o, qk, vn: o + dot(qk, vn, "nn"), inter, qks, v_news)
    s_nexts = _each(lambda s, gl, kd, vn: s * jnp.exp(gl) + dot(kd, vn, "tn"), s_prevs, g_lasts, k_decs, v_news)
    return _each(lambda o, z: _rms(o, gain) * _silu(z), outs, zs), s_nexts


def _split_heads(t):
    return [t[:, h * DN_DH:(h + 1) * DN_DH] for h in range(t.shape[1] // DN_DH)]


def _dn_gates(ps, a_rows, ad):
    hs = range(DN_HEADS)
    return ([_col(ps, 12 + h) for h in hs], [_row(a_rows, h) for h in hs], [_col(ps, 8 + h) for h in hs],
            [_col(_row(ad, 0), h) for h in hs], [_col(_row(ad, 1), h) for h in hs])


def _head_rows(vals):
    row = _iota((8, LANES), 0)
    tile = jnp.zeros((8, LANES), F32)
    for h, val in enumerate(vals):
        tile = tile + jnp.where(row == h, val, 0.0)
    return tile


def _cparams(n_axes):
    return pltpu.CompilerParams(dimension_semantics=("arbitrary",) * n_axes, vmem_limit_bytes=VMEM_LIMIT)


def _first_visit(acc_axes):
    cond = None
    for a in acc_axes:
        here = pl.program_id(a) == 0
        cond = here if cond is None else jnp.logical_and(cond, here)
    return cond


def _tile(ref, widen=False):
    val = ref[...]
    shape = val.shape
    while len(shape) > 2 and shape[0] == 1:
        shape = shape[1:]
    val = val.reshape(shape)
    return val.astype(F32) if widen and val.dtype == BF16 else val


def _store(ref, val, first):
    val = val.astype(ref.dtype).reshape(ref.shape)
    if first is None:
        ref[...] = val
        return

    @pl.when(first)
    def _():
        ref[...] = val

    @pl.when(jnp.logical_not(first))
    def _():
        ref[...] += val


def _specs(ops):
    return [pl.BlockSpec(block, imap) for _, block, imap in ops]


def tile_fwd(name, fn, grid, ins, outs, raw=()):
    n_in = len(ins)

    def body(*refs):
        pids = tuple(pl.program_id(a) for a in range(len(grid)))
        firsts = [_first_visit(o[4]) if o[4] else None for o in outs]
        res = fn(False, pids, *[_tile(r, i not in raw) for i, r in enumerate(refs[:n_in])])
        for ref, val, first in zip(refs[n_in:], res, firsts):
            _store(ref, val, first)

    out = pl.pallas_call(
        body, grid=grid, in_specs=_specs(ins),
        out_specs=[pl.BlockSpec(o[2], o[3]) for o in outs],
        out_shape=[jax.ShapeDtypeStruct(o[0], o[1]) for o in outs],
        name=name, compiler_params=_cparams(len(grid)),
    )(*[a for a, _, _ in ins])
    return out


def tile_bwd(name, fn, grid, ins, cots, diff, adds=None, raw=()):
    adds = adds or {}
    n_in, n_cot = len(ins), len(cots)
    add_pos = sorted(adds)
    diff_idx = [d[0] for d in diff]
    out_desc = [d[2] if len(d) > 2 and d[2] is not None else (ins[d[0]][0].shape, ins[d[0]][1], ins[d[0]][2]) for d in diff]
    out_dtypes = [d[3] if len(d) > 3 else F32 for d in diff]

    def body(*refs):
        pids = tuple(pl.program_id(a) for a in range(len(grid)))
        firsts = [_first_visit(d[1]) if d[1] else None for d in diff]
        vals = [_tile(r, i not in raw) for i, r in enumerate(refs[:n_in])]
        cot_vals = [_tile(r, True) for r in refs[n_in:n_in + n_cot]]
        add_vals = [_tile(r) for r in refs[n_in + n_cot:n_in + n_cot + len(add_pos)]]
        out_refs = refs[n_in + n_cot + len(add_pos):]

        def f(*dv):
            full = list(vals)
            for i, val in zip(diff_idx, dv):
                full[i] = val
            return fn(True, pids, *full)

        prim, vjp = jax.vjp(f, *[vals[i].astype(F32) for i in diff_idx])
        grads = list(vjp(tuple(c.astype(o.dtype) for c, o in zip(cot_vals, prim))))
        for pos, val in zip(add_pos, add_vals):
            extra = val.astype(F32) if firsts[pos] is None else jnp.where(firsts[pos], val.astype(F32), 0.0)
            grads[pos] = grads[pos] + extra
        for ref, val, first in zip(out_refs, grads, firsts):
            _store(ref, val, first)

    all_ins = list(ins) + list(cots) + [adds[p] for p in add_pos]
    out = pl.pallas_call(
        body, grid=grid, in_specs=_specs(all_ins),
        out_specs=[pl.BlockSpec(o[1], o[2]) for o in out_desc],
        out_shape=[jax.ShapeDtypeStruct(o[0], dt) for o, dt in zip(out_desc, out_dtypes)],
        name=name, compiler_params=_cparams(len(grid)),
    )(*[a for a, _, _ in all_ins])
    return out


def _pick(dim, cands):
    for c in cands:
        if dim % c == 0:
            return c
    return dim


MM_TILES = (1024, 512, 1408, 256, 128)


def mm(name, a, b, mode, add=None, out_dtype=F32, blocks=None):
    wide = None
    if mode == "nn":
        (m, kk), n = a.shape, b.shape[-1]
    elif mode == "nt":
        (m, kk), n = a.shape, b.shape[-2]
    else:
        (kk, m), n = a.shape, b.shape[1]
    if blocks is not None:
        lo, n_blk = blocks
        wide = b.shape[-1] if mode != "tn" else n // n_blk
        if mode == "nn":
            n = wide * n_blk
    tm = _pick(m, MM_TILES)
    if mode == "nt" and blocks is not None:
        tn, tk = _pick(n, MM_TILES), _pick(wide, MM_TILES[:-1])
    elif blocks is not None:
        tn, tk = _pick(wide, MM_TILES[:-1]), _pick(kk, MM_TILES)
    else:
        tn, tk = _pick(n, MM_TILES), _pick(kk, MM_TILES)
    if mode == "tn" or blocks is None:
        tk = _pick(kk, (2048,) + MM_TILES)
    nk = kk // tk
    a_spec = pl.BlockSpec((tk, tm), lambda i, j, k: (k, i)) if mode == "tn" else pl.BlockSpec((tm, tk), lambda i, j, k: (i, k))
    o_spec = pl.BlockSpec((tm, tn), lambda i, j, k: (i, j))
    out_shape = (m, n)
    if blocks is None:
        b_spec = pl.BlockSpec((tn, tk), lambda i, j, k: (j, k)) if mode == "nt" else pl.BlockSpec((tk, tn), lambda i, j, k: (k, j))
    elif mode == "nn":
        per = wide // tn
        b_spec = pl.BlockSpec((1, tk, tn), lambda i, j, k: (lo + j // per, k, j % per))
    elif mode == "nt":
        per = wide // tk
        b_spec = pl.BlockSpec((1, tn, tk), lambda i, j, k: (lo + k // per, j, k % per))
    else:
        per = wide // tn
        b_spec = pl.BlockSpec((tk, tn), lambda i, j, k: (k, j))
        o_spec = pl.BlockSpec((1, tm, tn), lambda i, j, k: (j // per, i, j % per))
        out_shape = (n_blk, m, wide)

    def body(*refs):
        a_ref, b_ref = refs[0], refs[1]
        add_ref = refs[2] if add is not None else None
        o_ref, acc = refs[-2], refs[-1]
        k = pl.program_id(2)
        part = _bdot_impl(_tile(a_ref), _tile(b_ref), mode)

        @pl.when(k == 0)
        def _():
            acc[...] = part

        @pl.when(k > 0)
        def _():
            acc[...] += part

        @pl.when(k == nk - 1)
        def _():
            res = acc[...]
            if add_ref is not None:
                res = res + add_ref[...]
            o_ref[...] = res.astype(o_ref.dtype).reshape(o_ref.shape)

    operands = [a, b] + ([add] if add is not None else [])
    in_specs = [a_spec, b_spec] + ([o_spec] if add is not None else [])
    return pl.pallas_call(
        body, grid=(m // tm, n // tn, nk), in_specs=in_specs, out_specs=o_spec,
        out_shape=jax.ShapeDtypeStruct(out_shape, out_dtype),
        scratch_shapes=[pltpu.VMEM((tm, tn), F32)],
        name=name, compiler_params=_cparams(3),
    )(*operands)


def _rows(x, width=None, off=0, tm=256):
    width = x.shape[1] if width is None else width
    return (x, (tm, width), lambda i, off=off: (i, off))


def _whole(x):
    nd = x.ndim
    return (x, x.shape, lambda *pids, nd=nd: (0,) * nd)


RMS_ROWS = 512


def _rms_ops(x, gain):
    return [_rows(x, tm=RMS_ROWS), _whole(gain)]


def rms_fwd(name, x, gain):
    s, dm = x.shape
    return tile_fwd(name, _rms_fn, (s // RMS_ROWS,), _rms_ops(x, gain), [((s, dm), BF16, (RMS_ROWS, dm), lambda i: (i, 0), ())])[0]


def rms_bwd(name, x, gain, dh, dres):
    s = x.shape[0]
    return tile_bwd(name, _rms_fn, (s // RMS_ROWS,), _rms_ops(x, gain), [_rows(dh, tm=RMS_ROWS)], [(0, ()), (1, (0,))],
                    adds={0: _rows(dres, tm=RMS_ROWS)})


def loss_call(y, t):
    s, dm = y.shape
    dy, part = tile_fwd("loss", _loss_fn, (s // 256,), [_rows(y), _rows(t)],
                        [((s, dm), F32, (256, dm), lambda i: (i, 0), ()), ((8, LANES), F32, (8, LANES), lambda i: (0, 0), (0,))])
    return dy, part[0, 0]


def _fox_prep_ops(pm, gq, gk):
    tm = 512
    return [(pm, (tm, LANES), lambda i, j: (i, C_FQ // LANES + j)), (pm, (tm, LANES), lambda i, j: (i, C_FK // LANES + j)),
            _whole(gq), _whole(gk)]


def fox_prep_fwd(name, pm, gq, gk):
    s = pm.shape[0]
    out = ((s, BRANCH), BF16, (512, LANES), lambda i, j: (i, j), ())
    return tile_fwd(name, _fox_prep_fn, (s // 512, 4), _fox_prep_ops(pm, gq, gk), [out, out])


def fox_prep_bwd(name, pm, gq, gk, dqn, dkn):
    s = pm.shape[0]
    cot = lambda g: (g, (512, LANES), lambda i, j: (i, j))
    own = ((s, BRANCH), (512, LANES), lambda i, j: (i, j))
    return tile_bwd(name, _fox_prep_fn, (s // 512, 4), _fox_prep_ops(pm, gq, gk), [cot(dqn), cot(dkn)],
                    [(0, (), own, BF16), (1, (), own, BF16), (2, (0, 1)), (3, (0, 1))])


def _fox_gate_ops(f_t, bias):
    return [(f_t, (1,) + f_t.shape[1:], lambda h: (h, 0, 0)), (bias, (1, 1, 1), lambda h: (h, 0, 0))]


def fox_gate_fwd(name, f_t, bias):
    n_h = f_t.shape[0]
    return tile_fwd(name, _fox_gate_fn, (n_h,), _fox_gate_ops(f_t, bias),
                    [(f_t.shape, F32, (1,) + f_t.shape[1:], lambda h: (h, 0, 0), ())])[0]


def fox_gate_bwd(name, f_t, bias, dcum):
    n_h = f_t.shape[0]
    return tile_bwd(name, _fox_gate_fn, (n_h,), _fox_gate_ops(f_t, bias),
                    [(dcum, (1,) + f_t.shape[1:], lambda h: (h, 0, 0))], [(0, ()), (1, ())])


FOX_GROUPS = 4


def _fox_groups(s):
    per = s // FOX_BLOCK // FOX_GROUPS
    return [(g * per, per, (g + 1) * per * FOX_BLOCK) for g in range(FOX_GROUPS)]


def _fox_attn_ops(qn, kn, pm, cum_c, cum_r, q0, keys):
    nb = FOX_BLOCK
    return [(qn, (nb, LANES), lambda p, i: (q0 + i, p)), (kn, (keys, LANES), lambda p, i: (0, p)),
            (pm, (keys, LANES), lambda p, i: (0, C_FV // LANES + p)),
            (cum_c, (1, nb, 1), lambda p, i: (2 * p, q0 + i, 0)), (cum_c, (1, nb, 1), lambda p, i: (2 * p + 1, q0 + i, 0)),
            (cum_r, (1, 1, keys), lambda p, i: (2 * p, 0, 0)), (cum_r, (1, 1, keys), lambda p, i: (2 * p + 1, 0, 0))]


def fox_attn_fwd(name, qn, kn, pm, cum_c, cum_r):
    s = qn.shape[0]
    parts = []
    for g, (q0, n_q, keys) in enumerate(_fox_groups(s)):
        parts.append(tile_fwd(f"{name}_g{g}", functools.partial(_fox_attn_fn, q0), (4, n_q), _fox_attn_ops(qn, kn, pm, cum_c, cum_r, q0, keys),
                              [((n_q * FOX_BLOCK, BRANCH), BF16, (FOX_BLOCK, LANES), lambda p, i: (i, p), ())], raw=(0, 1, 2))[0])
    return jnp.concatenate(parts, axis=0)


def fox_attn_bwd(name, qn, kn, pm, cum_c, cum_r, dy):
    s = qn.shape[0]
    groups = _fox_groups(s)
    d_qn, by_q, tails = [None] * len(groups), [None] * len(groups), [None] * len(groups)
    below = None
    for g in reversed(range(len(groups))):
        q0, n_q, keys = groups[g]
        rows = n_q * FOX_BLOCK
        own_q = ((rows, BRANCH), (FOX_BLOCK, LANES), lambda p, i: (i, p))
        own_k = ((keys, BRANCH), (keys, LANES), lambda p, i: (0, p))
        pair_c = ((4, rows, 1), (1, FOX_BLOCK, 1), lambda p, i: (p, i, 0))
        pair_r = ((4, 1, keys), (1, 1, keys), lambda p, i: (p, 0, 0))
        adds = {}
        if below is not None:
            adds = {1: (below[0],) + own_k[1:], 2: (below[1],) + own_k[1:], 5: (below[2],) + pair_r[1:], 6: (below[3],) + pair_r[1:]}
        g_qn, g_kn, g_v, g_cqa, g_cqb, g_cka, g_ckb = tile_bwd(
            f"{name}_g{g}", functools.partial(_fox_attn_fn, q0), (4, n_q), _fox_attn_ops(qn, kn, pm, cum_c, cum_r, q0, keys),
            [(dy, (FOX_BLOCK, LANES), lambda p, i, q0=q0: (q0 + i, p))],
            [(0, (), own_q), (1, (1,), own_k), (2, (1,), own_k), (3, (), pair_c), (4, (), pair_c), (5, (1,), pair_r), (6, (1,), pair_r)],
            adds=adds)
        below = (g_kn, g_v, g_cka, g_ckb)
        lo = groups[g - 1][2] if g else 0
        d_qn[g] = g_qn
        by_q[g] = jnp.stack([g_cqa[:, :, 0], g_cqb[:, :, 0]], axis=1).reshape(8, rows)
        tails[g] = (g_kn[lo:], g_v[lo:], jnp.stack([g_cka[:, 0, lo:], g_ckb[:, 0, lo:]], axis=1).reshape(8, keys - lo))
    d_cum = jnp.concatenate(by_q, axis=1) + jnp.concatenate([t[2] for t in tails], axis=1)
    return jnp.concatenate(d_qn, axis=0), jnp.concatenate([t[0] for t in tails], axis=0), jnp.concatenate([t[1] for t in tails], axis=0), d_cum


def sconv_ops(pm, w):
    s = pm.shape[0]
    blk = lambda c0: (pm, (s, LANES), lambda j, c0=c0: (0, c0 // LANES + j))
    return [blk(C_SB), blk(C_SC), blk(C_SV), (w, (w.shape[0], LANES), lambda j: (0, j))]


def dnconv_ops(pm, w):
    s = pm.shape[0]
    return [(pm, (s, LANES), lambda j: (0, C_DN // LANES + j)), (w, (w.shape[0], LANES), lambda j: (0, j))]


def ffn_ops(ug, uv, w):
    s = ug.shape[0]
    n_t = D_FF // LANES
    return [(ug, (s, LANES), lambda j: (0, j)), (uv, (s, LANES), lambda j: (0, j)),
            (w, (w.shape[0], LANES), lambda j: (0, j)), (w, (w.shape[0], LANES), lambda j: (0, n_t + j))]


def _col_out(s, width, dtype=F32):
    return ((s, width), dtype, (s, LANES), lambda j: (0, j), ())


def _col_cot(g):
    return (g, (g.shape[0], LANES), lambda j: (0, j))


def merge_ops(yp, pm):
    gate = lambda b: (pm, (256, D_MODEL), lambda i, b=b: (i, C_GATE // D_MODEL + b))
    return [_rows(yp[0]), _rows(yp[1]), _rows(yp[2]), gate(0), gate(1), gate(2)]


def ple_ops(gpre, pe, x):
    return [_rows(gpre), _rows(pe), _rows(x)]


def adam_call(name, w, g, m, v):
    shape = w.shape
    last = shape[-1]
    rows = w.size // last
    flat = lambda t: t.reshape(rows, last)
    tm = rows
    for cand in (512, 256, 128, 64, 32, 16, 8):
        if rows % cand == 0 and cand * last * 4 <= 2 * 1024 * 1024:
            tm = cand
            break
    spec = lambda t: (flat(t), (tm, last), lambda i: (i, 0))
    out = ((rows, last), F32, (tm, last), lambda i: (i, 0), ())
    res = tile_fwd(name, _adam_fn, (rows // tm,), [spec(w), spec(g), spec(m), spec(v)], [out, out, out])
    return [r.reshape(shape) for r in res]


def _adam_layers_fn(d, pids, w, m, v, g0, g1):
    g = jnp.where(pids[0] == 0, g0, g1)
    return (g,) + _adam_fn(d, pids, w, g, m, v)


def adam_layers(name, w, m, v, g0, g1):
    _, rows, cols = w.shape
    tm = _row_tile(rows, cols)
    n_t = rows // tm
    lay = lambda t: (t, (1, tm, cols), lambda l, i: (l, i, 0))
    ins = [lay(w), lay(m), lay(v), (g0, (tm, cols), lambda l, i: (i * (1 - l) + (n_t - 1) * l, 0)), (g1, (tm, cols), lambda l, i: (i * l, 0))]
    out = (w.shape, F32, (1, tm, cols), lambda l, i: (l, i, 0), ())
    return tile_fwd(name, _adam_layers_fn, (2, n_t), ins, [out, out, out, out])


def adam_w_in(name, w, m, v, g0, g1):
    rows, n_l, cols = w.shape

    def body(w_ref, m_ref, v_ref, g0_ref, g1_ref, g_out, d_out, m_out, v_out):
        step = 64

        def update(at):
            g0, g1 = g0_ref[at, :], g1_ref[at, :]
            layer = _iota((g0.shape[0], n_l, LANES), 1)
            g = jnp.where(layer == 0, g0[:, None, :], g1[:, None, :])
            delta, m2, v2 = _adam_fn(False, None, w_ref[at], g, m_ref[at], v_ref[at])
            for ref, val in ((g_out, g), (d_out, delta), (m_out, m2), (v_out, v2)):
                ref[at] = val

        def some_rows(i, carry):
            update(pl.ds(pl.multiple_of(i * step, step), step))
            return carry

        lax.fori_loop(0, rows // step, some_rows, 0)
        if rows % step:
            update(pl.ds(rows - rows % step, rows % step))

    both = pl.BlockSpec((rows, n_l, LANES), lambda j: (0, 0, j))
    one = pl.BlockSpec((rows, LANES), lambda j: (0, j))
    return pl.pallas_call(
        body, grid=(cols // LANES,), in_specs=[both, both, both, one, one], out_specs=[both] * 4,
        out_shape=[jax.ShapeDtypeStruct(w.shape, F32)] * 4, name=name, compiler_params=_cparams(1),
    )(w, m, v, g0, g1)


DN_GROUP = 4


def _dn_local_specs():
    rows = DN_GROUP * DN_CHUNK
    return [pl.BlockSpec((rows, 3 * BRANCH), lambda j: (j, 0)), pl.BlockSpec((rows, LANES), lambda j: (j, 0)),
            pl.BlockSpec((DN_GROUP, DN_HEADS, DN_CHUNK), lambda j: (j, 0, 0)), pl.BlockSpec((2, DN_HEADS), lambda j: (0, 0))]


def _dn_group_inputs(qkv, ps, a_rows, c):
    lo = c * DN_CHUNK
    heads = _split_heads(qkv[lo:lo + DN_CHUNK])
    return heads[0:4], heads[4:8], heads[8:12], ps[lo:lo + DN_CHUNK], a_rows[c]


def dn_local_fwd(name, dn_act, ps, a_rows, ad):
    s = dn_act.shape[0]
    n_c, n_g = s // DN_CHUNK, s // (DN_GROUP * DN_CHUNK)
    rows = DN_GROUP * DN_CHUNK

    def body(qkv_ref, ps_ref, ar_ref, ad_ref, u_ref, kc_ref, qd_ref, kd_ref, qk_ref, gl_ref):
        qkv, ps_v, a_rows_v, ad_v = qkv_ref[...], ps_ref[...], ar_ref[...], ad_ref[...]
        args = [[] for _ in range(8)]
        for c in range(DN_GROUP):
            q4, k4, v4, ps_c, ar_c = _dn_group_inputs(qkv, ps_v, a_rows_v, c)
            for lst, vals in zip(args, (q4, k4, v4) + _dn_gates(ps_c, ar_c, ad_v)):
                lst.extend(vals)
        everything = _dn_local(False, *args)
        for c in range(DN_GROUP):
            res = everything[c * DN_HEADS:(c + 1) * DN_HEADS]
            at = pl.ds(c * DN_CHUNK, DN_CHUNK)
            for ref, i in ((u_ref, 0), (kc_ref, 1), (qd_ref, 2), (kd_ref, 3)):
                ref[at, :] = jnp.concatenate([r[i] for r in res], axis=1)
            for h in range(DN_HEADS):
                qk_ref[c, h] = res[h][4]
            gl_ref[c] = _head_rows([r[5] for r in res])

    wide = pl.BlockSpec((rows, BRANCH), lambda j: (j, 0))
    return pl.pallas_call(
        body, grid=(n_g,), in_specs=_dn_local_specs(),
        out_specs=[wide, wide, wide, wide, pl.BlockSpec((DN_GROUP, DN_HEADS, DN_CHUNK, DN_CHUNK), lambda j: (j, 0, 0, 0)),
                   pl.BlockSpec((DN_GROUP, 8, LANES), lambda j: (j, 0, 0))],
        out_shape=[jax.ShapeDtypeStruct((s, BRANCH), F32)] * 4 + [jax.ShapeDtypeStruct((n_c, DN_HEADS, DN_CHUNK, DN_CHUNK), F32),
                                                                 jax.ShapeDtypeStruct((n_c, 8, LANES), F32)],
        name=name, compiler_params=_cparams(1),
    )(dn_act, ps, a_rows, ad)


def dn_local_bwd(name, dn_act, ps, a_rows, ad, cots):
    s = dn_act.shape[0]
    n_c, n_g = s // DN_CHUNK, s // (DN_GROUP * DN_CHUNK)
    rows = DN_GROUP * DN_CHUNK

    def body(qkv_ref, ps_ref, ar_ref, ad_ref, du_ref, dkc_ref, dqd_ref, dkd_ref, dqk_ref, dgl_ref, dqkv_ref, dps_ref, dar_ref, dad_ref):
        first = pl.program_id(0) == 0
        qkv, ps_v, a_rows_v, ad_v = qkv_ref[...], ps_ref[...], ar_ref[...], ad_ref[...]
        d_wide = [r[...] for r in (du_ref, dkc_ref, dqd_ref, dkd_ref)]
        qs, ks, vs, ps_cs, ar_cs, cot = [], [], [], [], [], []
        for c in range(DN_GROUP):
            q4, k4, v4, ps_c, ar_c = _dn_group_inputs(qkv, ps_v, a_rows_v, c)
            qs, ks, vs, ps_cs, ar_cs = qs + q4, ks + k4, vs + v4, ps_cs + [ps_c], ar_cs + [ar_c]
            lo = c * DN_CHUNK
            d_tiles = [_split_heads(t[lo:lo + DN_CHUNK]) for t in d_wide]
            d_gl = dgl_ref[c]
            cot += [(d_tiles[0][h], d_tiles[1][h], d_tiles[2][h], d_tiles[3][h], dqk_ref[c, h], _col(_row(d_gl, h), 0))
                    for h in range(DN_HEADS)]

        def f(qs, ks, vs, ps_cs, ar_cs, ad_v):
            gates = [[] for _ in range(5)]
            for ps_c, ar_c in zip(ps_cs, ar_cs):
                for lst, vals in zip(gates, _dn_gates(ps_c, ar_c, ad_v)):
                    lst.extend(vals)
            return _dn_local(True, qs, ks, vs, *gates)

        _, vjp = jax.vjp(f, qs, ks, vs, ps_cs, ar_cs, ad_v)
        d_q, d_k, d_v, d_ps, d_ar, d_ad = vjp(cot)
        for c in range(DN_GROUP):
            at, hs = pl.ds(c * DN_CHUNK, DN_CHUNK), slice(c * DN_HEADS, (c + 1) * DN_HEADS)
            dqkv_ref[at, :] = jnp.concatenate(d_q[hs] + d_k[hs] + d_v[hs], axis=1).astype(dqkv_ref.dtype)
            dps_ref[at, :] = d_ps[c]
            dar_ref[c] = d_ar[c]
        _store(dad_ref, d_ad, first)

    wide = pl.BlockSpec((rows, BRANCH), lambda j: (j, 0))
    specs = _dn_local_specs()
    return pl.pallas_call(
        body, grid=(n_g,),
        in_specs=specs + [wide, wide, wide, wide, pl.BlockSpec((DN_GROUP, DN_HEADS, DN_CHUNK, DN_CHUNK), lambda j: (j, 0, 0, 0)),
                          pl.BlockSpec((DN_GROUP, 8, LANES), lambda j: (j, 0, 0))],
        out_specs=specs,
        out_shape=[jax.ShapeDtypeStruct((s, 3 * BRANCH), F32), jax.ShapeDtypeStruct((s, LANES), F32),
                   jax.ShapeDtypeStruct((n_c, DN_HEADS, DN_CHUNK), F32), jax.ShapeDtypeStruct((2, DN_HEADS), F32)],
        name=name, compiler_params=_cparams(1),
    )(dn_act, ps, a_rows, ad, *cots)


def _dn_scan_specs(n_c, rev):
    idx = (lambda j: n_c - 1 - j) if rev else (lambda j: j)
    wide = pl.BlockSpec((DN_CHUNK, BRANCH), lambda j: (idx(j), 0))
    return [wide, wide, wide, wide, pl.BlockSpec((1, DN_HEADS, DN_CHUNK, DN_CHUNK), lambda j: (idx(j), 0, 0, 0)),
            pl.BlockSpec((1, 8, LANES), lambda j: (idx(j), 0, 0)), pl.BlockSpec((DN_CHUNK, BRANCH), lambda j: (idx(j), C_DZ // BRANCH)),
            pl.BlockSpec((1, DN_DH), lambda j: (0, 0))]


def _dn_scan_tiles(refs):
    u_ref, kc_ref, qd_ref, kd_ref, qk_ref, gl_ref, z_ref, g_ref = refs
    wide = [_split_heads(r[...]) for r in (u_ref, kc_ref, qd_ref, kd_ref)]
    gl = gl_ref[0]
    return [(wide[0][h], wide[1][h], wide[2][h], wide[3][h], qk_ref[0, h], _col(_row(gl, h), 0)) for h in range(DN_HEADS)], \
        _split_heads(z_ref[...].astype(F32)), g_ref[...]


def dn_scan_fwd(name, local, pm, gain):
    s = pm.shape[0]
    n_c = s // DN_CHUNK

    def body(*refs):
        y_ref, hist_ref, state = refs[8:]

        @pl.when(pl.program_id(0) == 0)
        def _():
            state[...] = jnp.zeros_like(state)

        hist_ref[0] = state[...]
        per_head, z4, gain_v = _dn_scan_tiles(refs[:8])
        ys, s_nexts = _dn_step(False, [state[h] for h in range(DN_HEADS)], per_head, z4, gain_v)
        for h in range(DN_HEADS):
            state[h] = s_nexts[h]
        y_ref[...] = jnp.concatenate(ys, axis=1).astype(y_ref.dtype)

    return pl.pallas_call(
        body, grid=(n_c,), in_specs=_dn_scan_specs(n_c, False),
        out_specs=[pl.BlockSpec((DN_CHUNK, BRANCH), lambda j: (j, 0)),
                   pl.BlockSpec((1, DN_HEADS, DN_DH, DN_DH), lambda j: (j, 0, 0, 0))],
        out_shape=[jax.ShapeDtypeStruct((s, BRANCH), BF16), jax.ShapeDtypeStruct((n_c, DN_HEADS, DN_DH, DN_DH), F32)],
        scratch_shapes=[pltpu.VMEM((DN_HEADS, DN_DH, DN_DH), F32)],
        name=name, compiler_params=_cparams(1),
    )(*local, pm, gain)


def dn_scan_bwd(name, local, pm, gain, hist, dy):
    s = pm.shape[0]
    n_c = s // DN_CHUNK

    def body(*refs):
        hist_ref, dy_ref = refs[8:10]
        du_ref, dkc_ref, dqd_ref, dkd_ref, dqk_ref, dgl_ref, dz_ref, dg_ref, d_state = refs[10:]
        first = pl.program_id(0) == 0

        @pl.when(first)
        def _():
            d_state[...] = jnp.zeros_like(d_state)

        per_head, z4, gain_v = _dn_scan_tiles(refs[:8])
        _, vjp = jax.vjp(functools.partial(_dn_step, True), [hist_ref[0, h] for h in range(DN_HEADS)], per_head, z4, gain_v)
        d_s, grads, d_z, d_gain = vjp((_split_heads(dy_ref[...].astype(F32)), [d_state[h] for h in range(DN_HEADS)]))
        for h in range(DN_HEADS):
            d_state[h] = d_s[h]
        for ref, i in ((du_ref, 0), (dkc_ref, 1), (dqd_ref, 2), (dkd_ref, 3)):
            ref[...] = jnp.concatenate([g[i] for g in grads], axis=1)
        dz_ref[...] = jnp.concatenate(d_z, axis=1).astype(dz_ref.dtype)
        for h in range(DN_HEADS):
            dqk_ref[0, h] = grads[h][4]
        dgl_ref[0] = _head_rows([g[5] for g in grads])
        _store(dg_ref, d_gain, first)

    rev = lambda j: n_c - 1 - j
    specs = _dn_scan_specs(n_c, True)
    return pl.pallas_call(
        body, grid=(n_c,),
        in_specs=specs + [pl.BlockSpec((1, DN_HEADS, DN_DH, DN_DH), lambda j: (rev(j), 0, 0, 0)),
                          pl.BlockSpec((DN_CHUNK, BRANCH), lambda j: (rev(j), 0))],
        out_specs=specs[:6] + [pl.BlockSpec((DN_CHUNK, BRANCH), lambda j: (rev(j), 0)), specs[7]],
        out_shape=[jax.ShapeDtypeStruct((s, BRANCH), F32)] * 4 + [
            jax.ShapeDtypeStruct((n_c, DN_HEADS, DN_CHUNK, DN_CHUNK), F32), jax.ShapeDtypeStruct((n_c, 8, LANES), F32),
            jax.ShapeDtypeStruct((s, BRANCH), BF16), jax.ShapeDtypeStruct((1, DN_DH), F32)],
        scratch_shapes=[pltpu.VMEM((DN_HEADS, DN_DH, DN_DH), F32)],
        name=name, compiler_params=_cparams(1),
    )(*local, pm, gain, hist, dy)


def _seq_layouts(cols, s):
    return cols.T.reshape(cols.shape[1], s // LANES, LANES)


def layer_fwd(li, x, p, w, more_weights=None):
    s = x.shape[0]
    n = lambda t: f"{t}_l{li}"
    h = rms_fwd(n("rms_mix"), x, w["g_mix"])
    pm = mm(n("in_main"), h, w["in_main"], "nn")
    ps = mm(n("in_small"), h, w["in_small"], "nn")
    qn, kn = fox_prep_fwd(n("fox_prep"), pm, w["gq"], w["gk"])
    f_t = _seq_layouts(ps[:, 0:8], s)
    cum = fox_gate_fwd(n("fox_gate"), f_t, w["b_f"])
    cum_c, cum_r = cum.reshape(8, s, 1), cum.reshape(8, 1, s)
    y_fox = fox_attn_fwd(n("fox_attn"), qn, kn, pm, cum_c, cum_r)
    y_sc = tile_fwd(n("sconv"), _sconv_fn, (BRANCH // LANES,), sconv_ops(pm, w["sc_conv_w"]), [_col_out(s, BRANCH, BF16)])[0]
    dn_act = tile_fwd(n("dnconv"), _dnconv_fn, (3 * BRANCH // LANES,), dnconv_ops(pm, w["dn_conv_w"]), [_col_out(s, 3 * BRANCH)])[0]
    a_rows = ps[:, 12:16].reshape(s // DN_CHUNK, DN_CHUNK, DN_HEADS).transpose(0, 2, 1)
    dn_local = dn_local_fwd(n("dn_local"), dn_act, ps, a_rows, w["ad"])
    y_dn, hist = dn_scan_fwd(n("dn_scan"), dn_local, pm, w["dn_gain"])
    ys = (y_fox, y_sc, y_dn)
    if more_weights is not None:
        w = {**w, **more_weights(y_dn)}
    yp = [mm(n(f"branch{b}"), ys[b], w["branch"][b], "nn", blocks=(0, N_CHIPS)) for b in range(3)]
    merged = tile_fwd(n("merge"), _merge_fn, (s // 256,), merge_ops(yp, pm), [((s, D_MODEL), BF16, (256, D_MODEL), lambda i: (i, 0), ())])[0]
    x1 = mm(n("w_o"), merged, w["o"], "nn", add=x)
    h2 = rms_fwd(n("rms_ffn"), x1, w["g_ffn"])
    ug = mm(n("up_g"), h2, w["up"], "nn", blocks=(0, 2))
    uv = mm(n("up_v"), h2, w["up"], "nn", blocks=(2, 2))
    act = tile_fwd(n("ffn_act"), _ffn_act_fn, (D_FF // LANES,), ffn_ops(ug, uv, w["ffn_conv_w"]), [_col_out(s, D_FF, BF16)])[0]
    x2 = mm(n("down"), act, w["down"], "nn", add=x1)
    h3 = rms_fwd(n("rms_ple"), x2, w["g_ple"])
    gpre = mm(n("ple_gate"), h3, w["pg"], "nn")
    pe = mm(n("ple_emb"), p, w["ple"], "nn", blocks=(0, N_CHIPS))
    x3 = tile_fwd(n("ple"), _ple_fn, (s // 256,), ple_ops(gpre, pe, x2), [((s, D_MODEL), F32, (256, D_MODEL), lambda i: (i, 0), ())])[0]
    saved = dict(x=x, h=h, pm=pm, ps=ps, qn=qn, kn=kn, f_t=f_t, cum_c=cum_c, cum_r=cum_r, ys=ys, dn_act=dn_act, dn_local=dn_local,
                 a_rows=a_rows, hist=hist, yp=yp, merged=merged, x1=x1, h2=h2, ug=ug, uv=uv, act=act, x2=x2, h3=h3,
                 gpre=gpre, pe=pe, p=p)
    return x3, saved, w


def hang_on(w, token):
    zero = token[0, 0]
    small = ("g_mix", "g_ffn", "g_ple", "gq", "gk", "b_f", "ad", "dn_gain", "sc_conv_w", "dn_conv_w", "ffn_conv_w")
    return {**w, **{k: w[k] + zero for k in small}}


def layer_bwd(li, dx3, sv, w, hooks=None):
    hooks = hooks or {}

    def stage(key, after, w):
        return hang_on(w, hooks[key](after, g)) if key in hooks else w

    s = dx3.shape[0]
    n = lambda t: f"{t}_l{li}"
    g = {}
    col_own = lambda width: ((s, width), (s, LANES), lambda j: (0, j))
    d_gpre, d_pe = tile_bwd(n("ple_bwd"), _ple_fn, (s // 256,), ple_ops(sv["gpre"], sv["pe"], sv["x2"]), [_rows(dx3)],
                            [(0, (), None, BF16), (1, (), None, BF16)])
    g["w_ple"] = mm(n("d_w_ple"), sv["p"], d_pe, "tn", blocks=(0, N_CHIPS))
    g["w_ple_gate"] = mm(n("d_w_pg"), sv["h3"], d_gpre, "tn").reshape(N_CHIPS, -1, D_MODEL)
    dh3 = mm(n("d_h3"), d_gpre, w["pg"], "nt")
    dx2, d_g_ple = rms_bwd(n("rms_ple_bwd"), sv["x2"], w["g_ple"], dh3, dx3)
    dact = mm(n("d_act"), dx2, w["down"], "nt")
    g["w_down"] = mm(n("d_w_down"), sv["act"], dx2, "tn").reshape(N_CHIPS, -1, D_MODEL)
    taps_own = ((w["ffn_conv_w"].shape[0], D_FF), (w["ffn_conv_w"].shape[0], LANES), lambda j: (0, j))
    d_ug, d_uv, d_fw_g, d_fw_v = tile_bwd(n("ffn_act_bwd"), _ffn_act_fn, (D_FF // LANES,), ffn_ops(sv["ug"], sv["uv"], w["ffn_conv_w"]),
                                          [_col_cot(dact)], [(0, (), None, BF16), (1, (), None, BF16), (2, (), taps_own), (3, (), taps_own)])
    g["ffn_conv_w"] = jnp.concatenate([d_fw_g, d_fw_v], axis=1)
    g["w_up"] = jnp.concatenate([mm(n("d_w_up_g"), sv["h2"], d_ug, "tn", blocks=(0, 2)), mm(n("d_w_up_v"), sv["h2"], d_uv, "tn", blocks=(0, 2))])
    dh2 = mm(n("d_h2_v"), d_uv, w["up"], "nt", blocks=(2, 2), add=mm(n("d_h2_g"), d_ug, w["up"], "nt", blocks=(0, 2)))
    dx1, d_g_ffn = rms_bwd(n("rms_ffn_bwd"), sv["x1"], w["g_ffn"], dh2, dx2)
    w = stage("mid", dx1, w)
    dmerged = mm(n("d_merged"), dx1, w["o"], "nt")
    g["w_o"] = mm(n("d_w_o"), sv["merged"], dx1, "tn").reshape(N_CHIPS, -1, D_MODEL)
    gate_own = ((s, D_MODEL), (256, D_MODEL), lambda i: (i, 0))
    d_yp0, d_yp1, d_yp2, d_g0, d_g1, d_g2 = tile_bwd(
        n("merge_bwd"), _merge_fn, (s // 256,), merge_ops(sv["yp"], sv["pm"]), [_rows(dmerged)],
        [(0, (), None, BF16), (1, (), None, BF16), (2, (), None, BF16), (3, (), gate_own, BF16), (4, (), gate_own, BF16), (5, (), gate_own, BF16)])
    d_yp = (d_yp0, d_yp1, d_yp2)
    g["w_branch"] = jnp.concatenate([mm(n(f"d_w_branch{b}"), sv["ys"][b], d_yp[b], "tn", blocks=(0, N_CHIPS)) for b in range(3)], axis=1)
    d_ys = [mm(n(f"d_y{b}"), d_yp[b], w["branch"][b], "nt", blocks=(0, N_CHIPS)) for b in range(3)]
    w = stage("late", d_ys[2], w)
    *d_local, d_z, d_dngain = dn_scan_bwd(n("dn_scan_bwd"), sv["dn_local"], sv["pm"], w["dn_gain"], sv["hist"], d_ys[2])
    d_dnact, d_ps_dn, d_arows, d_ad = dn_local_bwd(n("dn_local_bwd"), sv["dn_act"], sv["ps"], sv["a_rows"], w["ad"], d_local)
    g["ad"], g["dn_norm_gain"] = d_ad, d_dngain[0]
    d_dnqkv, g["dn_conv_w"] = tile_bwd(n("dnconv_bwd"), _dnconv_fn, (3 * BRANCH // LANES,), dnconv_ops(sv["pm"], w["dn_conv_w"]),
                                       [_col_cot(d_dnact)], [(0, (), col_own(3 * BRANCH), BF16), (1, ())])
    d_sb, d_sc, d_sv, g["sc_conv_w"] = tile_bwd(n("sconv_bwd"), _sconv_fn, (BRANCH // LANES,), sconv_ops(sv["pm"], w["sc_conv_w"]), [_col_cot(d_ys[1])],
                                                [(0, (), col_own(BRANCH), BF16), (1, (), col_own(BRANCH), BF16), (2, (), col_own(BRANCH), BF16), (3, ())])
    w = stage("last", d_dnqkv, w)
    d_qn, d_kn, d_fv, d_cum = fox_attn_bwd(n("fox_attn_bwd"), sv["qn"], sv["kn"], sv["pm"], sv["cum_c"], sv["cum_r"], d_ys[0])
    d_ft, d_bf = fox_gate_bwd(n("fox_gate_bwd"), sv["f_t"], w["b_f"], d_cum.reshape(8, s // LANES, LANES))
    g["b_fox_f"] = d_bf.reshape(8)
    d_fq, d_fk, d_gq, d_gk = fox_prep_bwd(n("fox_prep_bwd"), sv["pm"], w["gq"], w["gk"], d_qn, d_kn)
    g["fox_q_gain"] = d_gq[0, :FOX_DH] + d_gq[0, FOX_DH:]
    g["fox_k_gain"] = d_gk[0, :FOX_DH] + d_gk[0, FOX_DH:]
    d_pm = jnp.concatenate([d_fq, d_fk, d_fv.astype(BF16), d_sb, d_sc, d_sv, d_dnqkv, d_z, d_g0, d_g1, d_g2], axis=1)
    d_a_cols = d_arows.transpose(0, 2, 1).reshape(s, DN_HEADS)
    d_f_cols = d_ft.reshape(8, s).T
    d_ps = d_ps_dn + jnp.concatenate([d_f_cols, jnp.zeros((s, 4), F32), d_a_cols, jnp.zeros((s, LANES - 16), F32)], axis=1)
    g["w_in"] = chip_blocks_w_in(mm(n("d_w_in_main"), d_pm, sv["h"], "tn"), mm(n("d_w_in_small"), d_ps, sv["h"], "tn"))
    w = stage("w_in", g["w_in"], w)
    dh = mm(n("d_h_small"), d_ps, w["in_small"], "nt", add=mm(n("d_h_main"), d_pm, w["in_main"], "nt"))
    dx, d_g_mix = rms_bwd(n("rms_mix_bwd"), sv["x"], w["g_mix"], dh, dx1)
    g["g_mix"], g["g_ffn"], g["g_ple"] = d_g_mix[0], d_g_ffn[0], d_g_ple[0]
    return dx, g


IN_SHARD = 2052
MAIN_RANGES = ((0, 1536), (1544, 3080), (3080, 4616), (4624, 5136), (5136, 8208))
SMALL_RANGES = ((1536, 1544), (4616, 4620), (4620, 4624))


def _from_chip_blocks(blocks, ranges):
    parts = []
    for lo, hi in ranges:
        for k in range(N_CHIPS):
            a0, a1 = max(lo, k * IN_SHARD), min(hi, (k + 1) * IN_SHARD)
            if a0 < a1:
                parts.append(blocks[k][:, a0 - k * IN_SHARD:a1 - k * IN_SHARD])
    return parts


def split_w_in(blocks):
    main = jnp.concatenate(_from_chip_blocks(blocks, MAIN_RANGES), axis=1)
    pad = jnp.zeros((blocks.shape[1], LANES - 16), blocks.dtype)
    return main, jnp.concatenate(_from_chip_blocks(blocks, SMALL_RANGES) + [pad], axis=1)


def chip_blocks_w_in(main, small):
    ranges = sorted([(lo, hi, "m") for lo, hi in MAIN_RANGES] + [(lo, hi, "s") for lo, hi in SMALL_RANGES])
    offs, m_off, s_off = {}, 0, 0
    for lo, hi in MAIN_RANGES:
        offs[lo] = m_off
        m_off += hi - lo
    for lo, hi in SMALL_RANGES:
        offs[lo] = s_off
        s_off += hi - lo
    blocks = []
    for k in range(N_CHIPS):
        parts = []
        for lo, hi, src in ranges:
            a0, a1 = max(lo, k * IN_SHARD), min(hi, (k + 1) * IN_SHARD)
            if a0 < a1:
                arr = main if src == "m" else small
                parts.append(arr[offs[lo] + a0 - lo:offs[lo] + a1 - lo])
        blocks.append(jnp.concatenate(parts, axis=0))
    return jnp.stack(blocks)


def later_weights(got):
    g_branch, g_o, g_up, g_down, g_pg, g_ple = got
    branch = g_branch.reshape(N_CHIPS, 3, BRANCH, -1)
    return dict(branch=[branch[:, b] for b in range(3)], o=g_o.reshape(D_MODEL, D_MODEL), up=g_up,
                down=g_down.reshape(D_FF, D_MODEL), pg=g_pg.reshape(D_MODEL, D_MODEL), ple=g_ple)


def layer_weights(li, got, conv, a):
    main, small = split_w_in(got[0])
    tile2 = lambda v: jnp.concatenate([v, v])[None, :]
    rest = later_weights(got[1:]) if len(got) > 1 else {}
    return dict(
        in_main=main, in_small=small, **rest,
        g_mix=a["g_mix"][li][None, :], g_ffn=a["g_ffn"][li][None, :], g_ple=a["g_ple"][li][None, :],
        gq=tile2(a["fox_q_gain"][li]), gk=tile2(a["fox_k_gain"][li]), b_f=a["b_fox_f"][li].reshape(8, 1, 1),
        ad=jnp.stack([a["dn_a_log"][li], a["dn_dt_bias"][li]]), dn_gain=a["dn_norm_gain"][li][None, :],
        sc_conv_w=conv["sc_conv_w"][li], dn_conv_w=conv["dn_conv_w"][li], ffn_conv_w=conv["ffn_conv_w"][li])


def pack_rows(arrs, dtype):
    flat = jnp.concatenate([t.reshape(-1).astype(dtype) for t in arrs])
    pad = (-flat.shape[0]) % (8 * LANES)
    if pad:
        flat = jnp.concatenate([flat, jnp.zeros((pad,), dtype)])
    return flat.reshape(-1, LANES)


def unpack_rows(buf, shapes):
    flat = buf.reshape(-1)
    out, off = [], 0
    for shp in shapes:
        size = 1
        for dim in shp:
            size *= dim
        out.append(flat[off:off + size].reshape(shp))
        off += size
    return out


ANY = pl.BlockSpec(memory_space=pl.ANY)


def _position():
    x, y, c = lax.axis_index("x"), lax.axis_index("y"), lax.axis_index("c")
    return x, y, c, [(1 - x, y), (x, 1 - y), (1 - x, 1 - y)]


def gather_small(name, block):
    m_per, n = block.shape

    def body(x_ref, out_ref, token, send_sems, recv_sems, local_sem):
        token[...] = jnp.zeros_like(token)
        x, y, c, chips = _position()
        me, sibling = (x, y, c), (x, y, 1 - c)

        def rows(px, py, pc):
            return out_ref.at[pl.ds((4 * px + 2 * py + pc) * m_per, m_per), :]

        def copy(k, blk, to, src=None):
            return pltpu.make_async_remote_copy(src_ref=rows(*blk) if src is None else src, dst_ref=rows(*blk),
                                                send_sem=send_sems.at[k], recv_sem=recv_sems.at[k], device_id=to, device_id_type=MESH)

        mine = pltpu.make_async_copy(x_ref, rows(*me), local_sem)
        mine.start()
        first = [copy(0, me, sibling, src=x_ref)] + [copy(1 + j, me, (*chip, c), src=x_ref) for j, chip in enumerate(chips)]
        for cp in first:
            cp.start()
        passed = [copy(4 + j, (*chip, c), sibling) for j, chip in enumerate(chips)]
        for j, chip in enumerate(chips):
            copy(1 + j, (*chip, c), me).wait_recv()
            passed[j].start()
        copy(0, sibling, me).wait_recv()
        for j, chip in enumerate(chips):
            copy(4 + j, (*chip, 1 - c), me).wait_recv()
        for cp in first + passed:
            cp.wait_send()
        mine.wait()

    in_vmem = pl.BlockSpec(memory_space=pltpu.VMEM)
    return pl.pallas_call(
        body, out_shape=[jax.ShapeDtypeStruct((8 * m_per, n), block.dtype), jax.ShapeDtypeStruct((8, LANES), F32)],
        in_specs=[in_vmem], out_specs=[in_vmem, in_vmem],
        scratch_shapes=[pltpu.SemaphoreType.DMA((7,)), pltpu.SemaphoreType.DMA((7,)), pltpu.SemaphoreType.DMA],
        name=name, compiler_params=pltpu.CompilerParams(vmem_limit_bytes=VMEM_LIMIT),
    )(block)


def _sems(n):
    return [pltpu.SemaphoreType.DMA((n,)), pltpu.SemaphoreType.DMA((n,))]


def _split_cols(rows):
    return (rows // 2) % 16 != 0


def _half(ref, which, lead=()):
    rows, cols = ref.shape[-2:]
    if _split_cols(rows):
        return ref.at[(*lead, slice(None), pl.ds(which * (cols // 2), cols // 2))]
    return ref.at[(*lead, pl.ds(which * (rows // 2), rows // 2), slice(None))]


def _half_shape(rows, cols):
    return (rows, cols // 2) if _split_cols(rows) else (rows // 2, cols)


def forward_halves(name, lands):
    n_w = len(lands)

    def body(*refs):
        outs = refs[n_w:2 * n_w]
        send_sems, recv_sems = refs[2 * n_w:]
        x, y, c, chips = _position()

        def copy(w, j, pc):
            cx, cy = chips[j]
            part = _half(outs[w], pc, (2 * cx + cy,))
            return pltpu.make_async_remote_copy(src_ref=part, dst_ref=part, send_sem=send_sems.at[3 * w + j], recv_sem=recv_sems.at[3 * w + j],
                                                device_id=(x, y, 1 - c), device_id_type=MESH)

        pairs = [(w, j) for w in range(n_w) for j in range(3)]
        for w, j in pairs:
            copy(w, j, c).start()
        for w, j in pairs:
            copy(w, j, 1 - c).wait_recv()
            copy(w, j, c).wait_send()

    return pl.pallas_call(
        body, out_shape=[jax.ShapeDtypeStruct(t.shape, t.dtype) for t in lands], in_specs=[ANY] * n_w, out_specs=[ANY] * n_w,
        input_output_aliases={w: w for w in range(n_w)}, scratch_shapes=_sems(3 * n_w), name=name,
    )(*lands)


def share_halves(name, bufs):
    n_w = len(bufs)

    def body(*refs):
        outs = refs[n_w:2 * n_w]
        send_sems, recv_sems = refs[2 * n_w:]
        x, y, c, _ = _position()

        def copy(w, pc):
            half = _half(outs[w], pc)
            return pltpu.make_async_remote_copy(src_ref=half, dst_ref=half, send_sem=send_sems.at[w], recv_sem=recv_sems.at[w],
                                                device_id=(x, y, 1 - c), device_id_type=MESH)

        for w in range(n_w):
            copy(w, c).start()
        for w in range(n_w):
            copy(w, 1 - c).wait_recv()
            copy(w, c).wait_send()

    return pl.pallas_call(
        body, out_shape=[jax.ShapeDtypeStruct(b.shape, b.dtype) for b in bufs], in_specs=[ANY] * n_w, out_specs=[ANY] * n_w,
        input_output_aliases={w: w for w in range(n_w)}, scratch_shapes=_sems(n_w), name=name,
    )(*bufs)


HBM = pl.BlockSpec(memory_space=pltpu.HBM)
SEM = pl.BlockSpec(memory_space=pltpu.SEMAPHORE)
EFFECT = pltpu.SideEffectType.DATAFLOW_SIDE_EFFECTING


def _exchange_copies(kind, srcs, lands):
    x, y, c, chips = _position()
    out = []
    for src, land in zip(srcs, lands):
        if kind == "swap":
            out.append((_half(src, 1 - c, (slice(None),)), land, (x, y, 1 - c)))
            continue
        for j, (cx, cy) in enumerate(chips):
            if kind == "gather":
                out.append((src, land.at[2 * x + y], (cx, cy, c)))
            elif kind == "gather_half":
                out.append((_half(src, c), _half(land, c, (2 * x + y,)), (cx, cy, c)))
            else:
                out.append((src.at[2 * cx + cy], land.at[j], (cx, cy, c)))
    return out


def _land_shapes(kind, srcs):
    if kind in ("gather", "gather_half"):
        return [(N_CHIPS,) + s.shape for s in srcs]
    if kind == "swap":
        return [(N_CHIPS,) + _half_shape(*s.shape[1:]) for s in srcs]
    return [(3,) + s.shape[1:] for s in srcs]


def exchange_start(name, kind, srcs):
    n_w = len(srcs)
    shapes = _land_shapes(kind, srcs)
    n_sem = n_w if kind == "swap" else 3 * n_w

    def body(*refs):
        ins, lands = refs[:n_w], refs[n_w:2 * n_w]
        send_sems, recv_sems = refs[2 * n_w:2 * n_w + 2]
        token = refs[-1]
        for i, (src, dst, dev) in enumerate(_exchange_copies(kind, ins, lands)):
            pltpu.make_async_remote_copy(src_ref=src, dst_ref=dst, send_sem=send_sems.at[i], recv_sem=recv_sems.at[i],
                                         device_id=dev, device_id_type=MESH).start()
        token[...] = jnp.zeros_like(token)

    out = pl.pallas_call(
        body, name=name,
        out_shape=(pltpu.SemaphoreType.DMA((n_sem,)), pltpu.SemaphoreType.DMA((n_sem,)),
                   *[pltpu.HBM(s.shape, s.dtype) for s in srcs], *[pltpu.HBM(shp, s.dtype) for shp, s in zip(shapes, srcs)],
                   jax.ShapeDtypeStruct((8, LANES), F32)),
        in_specs=(HBM,) * (2 * n_w), out_specs=(SEM, SEM) + (HBM,) * (2 * n_w) + (pl.BlockSpec(memory_space=pltpu.VMEM),),
        input_output_aliases={i: 2 + i for i in range(2 * n_w)},
        compiler_params=pltpu.CompilerParams(has_side_effects=EFFECT),
    )(*[pltpu.with_memory_space_constraint(s, pltpu.HBM) for s in srcs],
      *[pltpu.with_memory_space_constraint(lax.empty(shp, s.dtype), pltpu.HBM) for shp, s in zip(shapes, srcs)])
    return (kind, n_w, out[:-1]), out[-1]


def exchange_wait(name, handle, after):
    kind, n_w, (send_sems, recv_sems, *thru) = handle

    def body(*refs):
        ins, lands = refs[:n_w], refs[n_w:2 * n_w]
        send_sems, recv_sems = refs[2 * n_w:2 * n_w + 2]
        for i, (src, dst, dev) in enumerate(_exchange_copies(kind, ins, lands)):
            cp = pltpu.make_async_remote_copy(src_ref=src, dst_ref=dst, send_sem=send_sems.at[i], recv_sem=recv_sems.at[i],
                                              device_id=dev, device_id_type=MESH)
            cp.wait_send()
            cp.wait_recv()

    out = pl.pallas_call(
        body, name=name, out_shape=tuple(pltpu.HBM(t.shape, t.dtype) for t in thru),
        in_specs=(HBM,) * (2 * n_w) + (SEM, SEM, pl.BlockSpec(memory_space=pl.ANY)), out_specs=(HBM,) * (2 * n_w),
        input_output_aliases={i: i for i in range(2 * n_w)},
        compiler_params=pltpu.CompilerParams(has_side_effects=EFFECT),
    )(*thru, send_sems, recv_sems, after)
    return list(out[:n_w]), list(out[n_w:])


def _row_tile(rows, cols):
    best = rows
    if rows * cols * 4 <= 1024 * 1024:
        return rows
    for t in range(16, rows, 16):
        if rows % t == 0 and t * cols * 4 <= 1024 * 1024:
            best = t
    return best


def pair_sum(name, pos, grad, from_sibling):
    _, rows, cols = grad.shape
    h_rows, h_cols = _half_shape(rows, cols)
    tr = _row_tile(h_rows, h_cols)
    n_t = h_rows // tr

    def body(pos_ref, g_ref, s_ref, b_ref, f_ref):
        tot = g_ref[...] + s_ref[...]
        b_ref[...] = tot.astype(BF16)

        @pl.when(pl.program_id(1) == pos_ref[1])
        def _():
            f_ref[...] = tot[0]

    blk = pl.BlockSpec((1, tr, h_cols), lambda i, k, pos: (k, i, 0))
    if _split_cols(rows):
        mine = pl.BlockSpec((1, tr, h_cols), lambda i, k, pos: (k, i, pos[0]))
    else:
        mine = pl.BlockSpec((1, tr, h_cols), lambda i, k, pos: (k, pos[0] * n_t + i, 0))
    return pl.pallas_call(
        body, grid_spec=pltpu.PrefetchScalarGridSpec(
            num_scalar_prefetch=1, grid=(n_t, N_CHIPS), in_specs=[mine, blk],
            out_specs=[blk, pl.BlockSpec((tr, h_cols), lambda i, k, pos: (i, 0))]),
        out_shape=[jax.ShapeDtypeStruct((N_CHIPS, h_rows, h_cols), BF16), jax.ShapeDtypeStruct((h_rows, h_cols), F32)],
        name=name, compiler_params=_cparams(2),
    )(pos, grad, from_sibling)


def chip_sum(name, pos, own, landed, split_cols):
    half, cols = own.shape
    tr = _row_tile(half, cols)
    n_t = half // tr

    def body(pos_ref, p_ref, l_ref, o_ref):
        o_ref[...] = ((p_ref[...] + l_ref[0].astype(F32)) + l_ref[1].astype(F32)) + l_ref[2].astype(F32)

    if split_cols:
        out_spec, out_shape = pl.BlockSpec((tr, cols), lambda i, pos: (i, pos[0])), (half, 2 * cols)
    else:
        out_spec, out_shape = pl.BlockSpec((tr, cols), lambda i, pos: (pos[0] * n_t + i, 0)), (2 * half, cols)
    return pl.pallas_call(
        body, grid_spec=pltpu.PrefetchScalarGridSpec(
            num_scalar_prefetch=1, grid=(n_t,),
            in_specs=[pl.BlockSpec((tr, cols), lambda i, pos: (i, 0)), pl.BlockSpec((3, tr, cols), lambda i, pos: (0, i, 0))],
            out_specs=out_spec),
        out_shape=jax.ShapeDtypeStruct(out_shape, F32), name=name, compiler_params=_cparams(1),
    )(pos, own, landed)


class OverlappedReduceScatter:
    def __init__(self, tag, pos, grads):
        self.n = lambda t: f"{t}_{tag}"
        self.pos, self.grads = pos, grads
        self.swap, self.token = exchange_start(self.n("swap_start"), "swap", grads)

    def middle(self, after):
        self.grads, from_sibling = exchange_wait(self.n("swap_wait"), self.swap, after)
        self.sums = [pair_sum(self.n(f"pair_sum{w}"), self.pos, g, s) for w, (g, s) in enumerate(zip(self.grads, from_sibling))]
        self.scatter, self.token = exchange_start(self.n("scatter_start"), "scatter", [b for b, _ in self.sums])

    def finish(self, after):
        _, landed = exchange_wait(self.n("scatter_wait"), self.scatter, after)
        halves = [chip_sum(self.n(f"chip_sum{w}"), self.pos, own, l, _split_cols(g.shape[1]))
                  for w, ((_, own), l, g) in enumerate(zip(self.sums, landed, self.grads))]
        return share_halves(self.n("share_halves"), halves)


def sum_devices(gathered):
    m_per = gathered.shape[0] // 8

    def body(g_ref, o_ref):
        tot = g_ref[pl.ds(0, m_per), :]
        for dev in range(1, 8):
            tot = tot + g_ref[pl.ds(dev * m_per, m_per), :]
        o_ref[...] = tot

    return pl.pallas_call(
        body, out_shape=jax.ShapeDtypeStruct((m_per, gathered.shape[1]), F32),
        in_specs=[pl.BlockSpec(memory_space=pltpu.VMEM)], out_specs=pl.BlockSpec(memory_space=pltpu.VMEM), name="sum_devices",
    )(gathered)


def kernel(x, p, g_mix, w_in, b_fox_f, fox_q_gain, fox_k_gain, sc_conv_w, dn_conv_w, dn_a_log, dn_dt_bias, dn_norm_gain, w_branch, w_o, g_ffn, w_up, ffn_conv_w, w_down, g_ple, w_ple_gate, w_ple, loss_target, m_g_mix, m_w_in, m_b_fox_f, m_fox_q_gain, m_fox_k_gain, m_sc_conv_w, m_dn_conv_w, m_dn_a_log, m_dn_dt_bias, m_dn_norm_gain, m_w_branch, m_w_o, m_g_ffn, m_w_up, m_ffn_conv_w, m_w_down, m_g_ple, m_w_ple_gate, m_w_ple, v_g_mix, v_w_in, v_b_fox_f, v_fox_q_gain, v_fox_k_gain, v_sc_conv_w, v_dn_conv_w, v_dn_a_log, v_dn_dt_bias, v_dn_norm_gain, v_w_branch, v_w_o, v_g_ffn, v_w_up, v_ffn_conv_w, v_w_down, v_g_ple, v_w_ple_gate, v_w_ple):
    a = dict(g_mix=g_mix, w_in=w_in, b_fox_f=b_fox_f, fox_q_gain=fox_q_gain, fox_k_gain=fox_k_gain, sc_conv_w=sc_conv_w,
             dn_conv_w=dn_conv_w, dn_a_log=dn_a_log, dn_dt_bias=dn_dt_bias, dn_norm_gain=dn_norm_gain, w_branch=w_branch, w_o=w_o,
             g_ffn=g_ffn, w_up=w_up, ffn_conv_w=ffn_conv_w, w_down=w_down, g_ple=g_ple, w_ple_gate=w_ple_gate, w_ple=w_ple)
    mom = dict(g_mix=m_g_mix, w_in=m_w_in, b_fox_f=m_b_fox_f, fox_q_gain=m_fox_q_gain, fox_k_gain=m_fox_k_gain, sc_conv_w=m_sc_conv_w,
               dn_conv_w=m_dn_conv_w, dn_a_log=m_dn_a_log, dn_dt_bias=m_dn_dt_bias, dn_norm_gain=m_dn_norm_gain, w_branch=m_w_branch,
               w_o=m_w_o, g_ffn=m_g_ffn, w_up=m_w_up, ffn_conv_w=m_ffn_conv_w, w_down=m_w_down, g_ple=m_g_ple, w_ple_gate=m_w_ple_gate,
               w_ple=m_w_ple)
    var = dict(g_mix=v_g_mix, w_in=v_w_in, b_fox_f=v_b_fox_f, fox_q_gain=v_fox_q_gain, fox_k_gain=v_fox_k_gain, sc_conv_w=v_sc_conv_w,
               dn_conv_w=v_dn_conv_w, dn_a_log=v_dn_a_log, dn_dt_bias=v_dn_dt_bias, dn_norm_gain=v_dn_norm_gain, w_branch=v_w_branch,
               w_o=v_w_o, g_ffn=v_g_ffn, w_up=v_w_up, ffn_conv_w=v_ffn_conv_w, w_down=v_w_down, g_ple=v_g_ple, w_ple_gate=v_w_ple_gate,
               w_ple=v_w_ple)
    cx, cy, cc = lax.axis_index("x"), lax.axis_index("y"), lax.axis_index("c")
    chip = 2 * cx + cy
    pos = jnp.stack([cc, chip]).astype(jnp.int32)

    def as_blocks(t):
        return t.reshape(2, -1, t.shape[-1])

    def own_block_in(got, shards):
        return [lax.dynamic_update_slice(g, s[None], (chip, 0, 0)) for g, s in zip(got, shards)]

    conv_shapes = [a[nm].shape for nm in CONVS]
    conv_all, conv_token = gather_small("gather_conv_w", pack_rows([a[nm] for nm in CONVS], F32))
    w_in0 = [(as_blocks(a["w_in"])[0] + conv_token[0, 0]).astype(BF16)]
    gather_in0, gather_in0_token = exchange_start("gather_start_w_in_l0", "gather_half", w_in0)
    shards0 = w_in0 + [(as_blocks(a[nm])[0] + gather_in0_token[0, 0]).astype(BF16) for nm in BIG[1:]]
    gather0, gather0_token = exchange_start("gather_start_l0", "gather", shards0[1:])
    shards1 = [(as_blocks(a[nm])[1] + gather0_token[0, 0]).astype(BF16) for nm in BIG]
    gather1, gather1_in_token = exchange_start("gather_start_w_in_l1", "gather", shards1[:1])
    shards1[1:] = [s + gather1_in_token[0, 0].astype(BF16) for s in shards1[1:]]
    gather1_rest, gather1_token = exchange_start("gather_start_l1", "gather", shards1[1:])
    conv_rows = conv_all.shape[0] // 8
    conv_chip = [unpack_rows(conv_all[2 * k * conv_rows:(2 * k + 1) * conv_rows], conv_shapes) for k in range(N_CHIPS)]
    conv = {nm: jnp.concatenate([conv_chip[k][i] for k in range(N_CHIPS)], axis=2) for i, nm in enumerate(CONVS)}

    weights, saved = [None, None], [None, None]
    mine_in0, got_in0 = exchange_wait("gather_wait_w_in_l0", gather_in0, gather1_token)
    got_in0 = forward_halves("forward_w_in_l0", got_in0)
    first_weights = hang_on(layer_weights(0, own_block_in(got_in0, mine_in0), conv, a), gather1_token)

    def rest_of_layer0(after):
        mine, got = exchange_wait("gather_wait_l0", gather0, after)
        return later_weights(own_block_in(got, mine))

    act, saved[0], weights[0] = layer_fwd(0, x[0], p[0, 0], first_weights, more_weights=rest_of_layer0)
    mine1, got1 = exchange_wait("gather_wait_w_in_l1", gather1, act)

    def rest_of_layer1(after):
        mine, got = exchange_wait("gather_wait_l1", gather1_rest, after)
        return later_weights(own_block_in(got, mine))

    act, saved[1], weights[1] = layer_fwd(1, act, p[1, 0], layer_weights(1, own_block_in(got1, mine1), conv, a),
                                          more_weights=rest_of_layer1)
    d_act, loss_part = loss_call(act, loss_target[0])
    loss = lax.psum(loss_part, ("x", "y", "c"))
    layer_grads = [None, None]
    d_act, layer_grads[1] = layer_bwd(1, d_act, saved[1], weights[1])
    rs1 = OverlappedReduceScatter("l1", pos, [layer_grads[1][nm] for nm in BIG])
    rs0 = []

    def stage_mid(after, g):
        rs1.middle(after)
        return rs1.token

    def stage_late(after, g):
        rs0.append(OverlappedReduceScatter("l0", pos, [g[nm] for nm in BIG[1:]]))
        return rs0[0].token

    def stage_last(after, g):
        rs0[0].middle(after)
        return rs0[0].token

    def stage_w_in(after, g):
        rs0.append(OverlappedReduceScatter("w_in_l0", pos, [g["w_in"]]))
        return rs0[1].token

    d_act, layer_grads[0] = layer_bwd(0, d_act, saved[0], hang_on(weights[0], rs1.token),
                                      hooks=dict(mid=stage_mid, late=stage_late, last=stage_last, w_in=stage_w_in))
    rs0[1].middle(d_act)
    reduced = [rs0[0].finish(rs0[1].token), rs1.finish(rs0[1].token)]
    grad_x = d_act[None]

    def both(nm):
        return jnp.stack([layer_grads[0][nm], layer_grads[1][nm]])

    local = {nm: both(nm) for nm in ("g_mix", "b_fox_f", "fox_q_gain", "fox_k_gain", "dn_norm_gain", "g_ffn", "g_ple", "sc_conv_w",
                                      "dn_conv_w", "ffn_conv_w")}
    local["dn_a_log"] = jnp.stack([layer_grads[li]["ad"][0] for li in range(2)])
    local["dn_dt_bias"] = jnp.stack([layer_grads[li]["ad"][1] for li in range(2)])

    small_names = SMALL + CONVS
    small_shapes = [local[nm].shape for nm in small_names]
    small_sum = sum_devices(gather_small("gather_small_grads", pack_rows([local[nm] for nm in small_names], F32))[0])
    small_grads = dict(zip(small_names, unpack_rows(small_sum, small_shapes)))
    for nm in CONVS:
        width = a[nm].shape[2]
        small_grads[nm] = lax.dynamic_slice_in_dim(small_grads[nm], chip * width, width, axis=2)

    grads, deltas, new_m, new_v = dict(small_grads), {}, {}, {}
    for nm in small_names:
        deltas[nm], new_m[nm], new_v[nm] = adam_call(f"adam_{nm}", a[nm], grads[nm], mom[nm], var[nm])
    for i, nm in enumerate(BIG[1:]):
        res = adam_layers(f"adam_{nm}", as_blocks(a[nm]), as_blocks(mom[nm]), as_blocks(var[nm]), reduced[0][i], reduced[1][1 + i])
        grads[nm], deltas[nm], new_m[nm], new_v[nm] = [r.reshape(a[nm].shape) for r in res]
    stored = lambda t: jnp.transpose(t, (2, 0, 1))
    res = adam_w_in("adam_w_in", stored(a["w_in"]), stored(mom["w_in"]), stored(var["w_in"]), rs0[1].finish(deltas["w_ple"])[0], reduced[1][0])
    grads["w_in"], deltas["w_in"], new_m["w_in"], new_v["w_in"] = [jnp.transpose(r, (1, 2, 0)) for r in res]
    return (loss, grad_x, *[grads[nm] for nm in WEIGHTS], *[deltas[nm] for nm in WEIGHTS], *[new_m[nm] for nm in WEIGHTS],
            *[new_v[nm] for nm in WEIGHTS])
```

```python
import functools

import jax
import jax.numpy as jnp
from jax import lax
from jax.experimental import pallas as pl
from jax.experimental.pallas import tpu as pltpu

F32 = jnp.float32
BF16 = jnp.bfloat16
HI = lax.Precision.HIGHEST
SOLVE = lax.Precision.HIGH
MESH = pl.DeviceIdType.MESH

D_MODEL = 1024
BRANCH = 512
FOX_DH = 64
DN_DH = 128
DN_HEADS = 4
DN_CHUNK = 64
FOX_BLOCK = 128
D_FF = 2816
EPS = 1e-6
N_CHIPS = 4
LANES = 128

ADAM_LR, ADAM_B1, ADAM_B2, ADAM_EPS, ADAM_WD, ADAM_STEP = 0.001, 0.9, 0.999, 1e-08, 0.01, 10

VMEM_LIMIT = 56 * 1024 * 1024

C_FQ, C_FK, C_FV, C_SB, C_SC, C_SV, C_DN, C_DZ, C_GATE = 0, 512, 1024, 1536, 2048, 2560, 3072, 4608, 5120
IN_MAIN = 8192

BIG = ("w_in", "w_branch", "w_o", "w_up", "w_down", "w_ple_gate", "w_ple")
CONVS = ("sc_conv_w", "dn_conv_w", "ffn_conv_w")
SMALL = ("g_mix", "b_fox_f", "fox_q_gain", "fox_k_gain", "dn_a_log", "dn_dt_bias", "dn_norm_gain", "g_ffn", "g_ple")
WEIGHTS = ("g_mix", "w_in", "b_fox_f", "fox_q_gain", "fox_k_gain", "sc_conv_w", "dn_conv_w", "dn_a_log", "dn_dt_bias",
           "dn_norm_gain", "w_branch", "w_o", "g_ffn", "w_up", "ffn_conv_w", "w_down", "g_ple", "w_ple_gate", "w_ple")


def _iota(shape, dim):
    return lax.broadcasted_iota(jnp.int32, shape, dim)


def _dg(a, b, mode, prec=None):
    dims = {"nn": ((1,), (0,)), "nt": ((1,), (1,)), "tn": ((0,), (0,))}[mode]
    return lax.dot_general(a, b, (dims, ((), ())), precision=prec, preferred_element_type=F32)


def _bdot_impl(a, b, mode):
    return _dg(a.astype(BF16), b.astype(BF16), mode)


@functools.partial(jax.custom_vjp, nondiff_argnums=(2,))
def _bdot_diff(a, b, mode):
    return _bdot_impl(a, b, mode)


def _bdot_fwd(a, b, mode):
    return _bdot_impl(a, b, mode), (a, b)


def _bdot_bwd(mode, res, g):
    a, b = res
    if mode == "nn":
        da, db = _bdot_impl(g, b, "nt"), _bdot_impl(a, g, "tn")
    elif mode == "nt":
        da, db = _bdot_impl(g, b, "nn"), _bdot_impl(g, a, "tn")
    else:
        da, db = _bdot_impl(b, g, "nt"), _bdot_impl(a, g, "nn")
    return da.astype(a.dtype), db.astype(b.dtype)


_bdot_diff.defvjp(_bdot_fwd, _bdot_bwd)


def _bdot(d):
    return _bdot_diff if d else _bdot_impl


def _shift_impl(x, k):
    return jnp.where(_iota(x.shape, 0) >= k, pltpu.roll(x, k, 0), 0.0)


def _unshift_impl(g, k):
    n = g.shape[0]
    return jnp.where(_iota(g.shape, 0) < n - k, pltpu.roll(g, n - k, 0), 0.0)


@functools.partial(jax.custom_vjp, nondiff_argnums=(1,))
def _shift_diff(x, k):
    return _shift_impl(x, k)


_shift_diff.defvjp(lambda x, k: (_shift_impl(x, k), None), lambda k, _, g: (_unshift_impl(g, k),))


def _row(w, j):
    return jnp.sum(jnp.where(_iota(w.shape, 0) == j, w, 0.0), axis=0, keepdims=True)


def _col(w, j):
    return jnp.sum(jnp.where(_iota(w.shape, 1) == j, w, 0.0), axis=1, keepdims=True)


def _conv(d, x, w):
    shift = _shift_diff if d else _shift_impl
    taps = w.shape[0]
    y = x * _row(w, taps - 1)
    for j in range(taps - 1):
        y = y + shift(x, taps - 1 - j) * _row(w, j)
    return y


def _softplus(x):
    return jnp.maximum(x, 0.0) + jnp.log(1.0 + jnp.exp(-jnp.abs(x)))


def _sigmoid(x):
    return 0.5 * (jnp.tanh(0.5 * x) + 1.0)


def _silu(x):
    return x * _sigmoid(x)


def _rms(x, gain):
    return x * lax.rsqrt(jnp.mean(x * x, axis=-1, keepdims=True) + EPS) * gain


def _rms_fn(d, pids, x, gain):
    return (_rms(x, gain),)


def _loss_fn(d, pids, y, t):
    e = y - t
    part = 0.5 / D_MODEL * jnp.sum(e * e, keepdims=True)
    return e * (1.0 / D_MODEL), jnp.broadcast_to(part, (8, LANES))


def _fox_prep_fn(d, pids, q, k, gq, gk):
    first = _iota(q.shape, 1) < FOX_DH

    def norm(x, gain):
        sq = x * x
        ss_a = jnp.sum(jnp.where(first, sq, 0.0), axis=1, keepdims=True)
        ss_b = jnp.sum(jnp.where(first, 0.0, sq), axis=1, keepdims=True)
        rs = jnp.where(first, lax.rsqrt(ss_a / FOX_DH + EPS), lax.rsqrt(ss_b / FOX_DH + EPS))
        return x * rs * gain

    return norm(q, gq) * FOX_DH ** -0.5, norm(k, gk)


def _fox_gate_fn(d, pids, f, bias):
    logf = -_softplus(-(f + bias))
    n_r, n_c = logf.shape
    tri = (_iota((n_c, n_c), 0) <= _iota((n_c, n_c), 1)).astype(F32)
    within = _dg(logf, tri, "nn", HI)
    tot = jnp.broadcast_to(jnp.sum(logf, axis=1, keepdims=True), logf.shape)
    below = (_iota((n_r, n_r), 1) < _iota((n_r, n_r), 0)).astype(F32)
    return (within + _dg(below, tot, "nn", HI),)


def _fox_attn_fn(q_block0, d, pids, q, k, v, cq_a, cq_b, ck_a, ck_b):
    dot = _bdot(d)
    first = _iota(q.shape, 1) < FOX_DH
    n_q, n_k = q.shape[0], k.shape[0]
    causal = ((q_block0 + pids[1]) * n_q + _iota((n_q, n_k), 0)) >= _iota((n_q, n_k), 1)

    qs = [jnp.where(first, q, 0.0), jnp.where(first, 0.0, q)]
    s = _each(lambda qh, cq, ck: jnp.where(causal, dot(qh, k, "nt") + cq - ck, -1e30), qs, [cq_a, cq_b], [ck_a, ck_b])
    e = [jnp.exp(si - lax.stop_gradient(jnp.max(si, axis=1, keepdims=True))) for si in s]
    o_a, o_b = [dot(ei * (1.0 / jnp.sum(ei, axis=1, keepdims=True)), v, "nn") for ei in e]
    return (jnp.where(first, o_a, o_b),)


def _sconv_fn(d, pids, sb, sc, sv, w):
    return (sb * _conv(d, sc * sv, w),)


def _dnconv_fn(d, pids, x, w):
    return (_silu(_conv(d, x, w)),)


def _merge_fn(d, pids, y0, y1, y2, g0, g1, g2):
    return (_sigmoid(g0) * y0 + _sigmoid(g1) * y1 + _sigmoid(g2) * y2,)


def _ffn_act_fn(d, pids, ug, uv, wg, wv):
    return (_silu(_conv(d, ug, wg)) * _conv(d, uv, wv),)


def _ple_fn(d, pids, gpre, pe, x):
    return (x + _sigmoid(gpre) * pe,)


def _adam_fn(d, pids, w, g, m, v):
    m2 = ADAM_B1 * m + (1.0 - ADAM_B1) * g
    v2 = ADAM_B2 * v + (1.0 - ADAM_B2) * (g * g)
    m_hat = m2 / (1.0 - ADAM_B1 ** ADAM_STEP)
    v_hat = v2 / (1.0 - ADAM_B2 ** ADAM_STEP)
    delta = -ADAM_LR * (m_hat / (jnp.sqrt(v_hat) + ADAM_EPS) + ADAM_WD * w)
    return delta, m2, v2


def _each(fn, *lists):
    return [fn(*args) for args in zip(*lists)]


def _tri_inv_impl(mats):
    n = mats[0].shape[0]
    r, c = _iota((n, n), 0), _iota((n, n), 1)
    diag_blk = (r >> 4) == (c >> 4)
    eye = (r == c).astype(F32)
    mm = lambda us, ws: _each(lambda u, w: _dg(u, w, "nn", SOLVE), us, ws)
    grow = lambda ps, xs: _each(lambda p, px: p + px, ps, mm(ps, xs))
    x = [jnp.where(diag_blk, -a, 0.0) for a in mats]
    p = [eye + xi for xi in x]
    x2 = mm(x, x)
    p = grow(p, x2)
    x4 = mm(x2, x2)
    p = grow(p, x4)
    p = grow(p, mm(x4, x4))
    y = [-yi for yi in mm(p, [jnp.where(diag_blk, 0.0, a) for a in mats])]
    q = grow([eye + yi for yi in y], mm(y, y))
    return mm(q, p)


@jax.custom_vjp
def _tri_inv_diff(mats):
    return _tri_inv_impl(mats)


def _tri_inv_fwd(mats):
    ts = _tri_inv_impl(mats)
    return ts, ts


def _tri_inv_bwd(ts, gs):
    left = _each(lambda t, g: _dg(t, g, "tn", SOLVE), ts, gs)
    return ([-m for m in _each(lambda l, t: _dg(l, t, "nt", SOLVE), left, ts)],)


_tri_inv_diff.defvjp(_tri_inv_fwd, _tri_inv_bwd)


def _dn_local(d, qs, ks, vs, a_cs, a_rs, b_cs, a_logs, dt_bs):
    dot = _bdot(d)
    inv = _tri_inv_diff if d else _tri_inv_impl
    n = qs[0].shape[0]
    r, c = _iota((n, n), 0), _iota((n, n), 1)
    incl, strict, upper = r >= c, r > c, r <= c
    qs = [q * lax.rsqrt(jnp.sum(q * q, axis=1, keepdims=True) + EPS) * DN_DH ** -0.5 for q in qs]
    ks = [k * lax.rsqrt(jnp.sum(k * k, axis=1, keepdims=True) + EPS) for k in ks]
    betas = [_sigmoid(b) for b in b_cs]
    rates = [-jnp.exp(a) for a in a_logs]
    g_cs = _each(lambda rate, a, dt: rate * _softplus(a + dt), rates, a_cs, dt_bs)
    g_rs = _each(lambda rate, a, dt: rate * _softplus(a + dt), rates, a_rs, dt_bs)
    gcum_cs = [jnp.sum(jnp.where(incl, g, 0.0), axis=1, keepdims=True) for g in g_rs]
    gcum_rs = [jnp.sum(jnp.where(upper, g, 0.0), axis=0, keepdims=True) for g in g_cs]
    decays = _each(lambda gc, gr: jnp.exp(jnp.where(incl, gc - gr, -1e30)), gcum_cs, gcum_rs)
    kbs = _each(lambda k, b: k * b, ks, betas)
    kk = _each(lambda kb, k: dot(kb, k, "nt"), kbs, ks)
    ts = inv(_each(lambda m, dec: jnp.where(strict, m * dec, 0.0), kk, decays))
    e_gs = [jnp.exp(g) for g in gcum_cs]
    us = _each(lambda t, v, b: _dg(t, v * b, "nn", SOLVE), ts, vs, betas)
    k_cums = _each(lambda t, kb, e: _dg(t, kb * e, "nn", SOLVE), ts, kbs, e_gs)
    qk = _each(lambda q, k: dot(q, k, "nt"), qs, ks)
    qk = _each(lambda m, dec: jnp.where(incl, m * dec, 0.0), qk, decays)
    g_lasts = [jnp.sum(g, axis=0, keepdims=True) for g in g_cs]
    q_decs = _each(lambda q, e: q * e, qs, e_gs)
    k_decs = _each(lambda k, gl, gc: k * jnp.exp(gl - gc), ks, g_lasts, gcum_cs)
    return list(zip(us, k_cums, q_decs, k_decs, qk, g_lasts))


def _dn_step(d, s_prevs, items, zs, gain):
    dot = _bdot(d)
    us, k_cums, q_decs, k_decs, qks, g_lasts = [list(t) for t in zip(*items)]
    v_news = _each(lambda u, kc, s: u - dot(kc, s, "nn"), us, k_cums, s_prevs)
    inter = _each(lambda qd, s: dot(qd, s, "nn"), q_decs, s_prevs)
    outs = _each(lambda o, qk, vn: o + dot(qk, vn, "nn"), inter, qks, v_news)
    s_nexts = _each(lambda s, gl, kd, vn: s * jnp.exp(gl) + dot(kd, vn, "tn"), s_prevs, g_lasts, k_decs, v_news)
    return _each(lambda o, z: _rms(o, gain) * _silu(z), outs, zs), s_nexts


def _split_heads(t):
    return [t[:, h * DN_DH:(h + 1) * DN_DH] for h in range(t.shape[1] // DN_DH)]


def _dn_gates(ps, a_rows, ad):
    hs = range(DN_HEADS)
    return ([_col(ps, 12 + h) for h in hs], [_row(a_rows, h) for h in hs], [_col(ps, 8 + h) for h in hs],
            [_col(_row(ad, 0), h) for h in hs], [_col(_row(ad, 1), h) for h in hs])


def _head_rows(vals):
    row = _iota((8, LANES), 0)
    tile = jnp.zeros((8, LANES), F32)
    for h, val in enumerate(vals):
        tile = tile + jnp.where(row == h, val, 0.0)
    return tile


def _cparams(n_axes):
    return pltpu.CompilerParams(dimension_semantics=("arbitrary",) * n_axes, vmem_limit_bytes=VMEM_LIMIT)


def _first_visit(acc_axes):
    cond = None
    for a in acc_axes:
        here = pl.program_id(a) == 0
        cond = here if cond is None else jnp.logical_and(cond, here)
    return cond


def _tile(ref, widen=False):
    val = ref[...]
    shape = val.shape
    while len(shape) > 2 and shape[0] == 1:
        shape = shape[1:]
    val = val.reshape(shape)
    return val.astype(F32) if widen and val.dtype == BF16 else val


def _store(ref, val, first):
    val = val.astype(ref.dtype).reshape(ref.shape)
    if first is None:
        ref[...] = val
        return

    @pl.when(first)
    def _():
        ref[...] = val

    @pl.when(jnp.logical_not(first))
    def _():
        ref[...] += val


def _specs(ops):
    return [pl.BlockSpec(block, imap) for _, block, imap in ops]


def tile_fwd(name, fn, grid, ins, outs, raw=()):
    n_in = len(ins)

    def body(*refs):
        pids = tuple(pl.program_id(a) for a in range(len(grid)))
        firsts = [_first_visit(o[4]) if o[4] else None for o in outs]
        res = fn(False, pids, *[_tile(r, i not in raw) for i, r in enumerate(refs[:n_in])])
        for ref, val, first in zip(refs[n_in:], res, firsts):
            _store(ref, val, first)

    out = pl.pallas_call(
        body, grid=grid, in_specs=_specs(ins),
        out_specs=[pl.BlockSpec(o[2], o[3]) for o in outs],
        out_shape=[jax.ShapeDtypeStruct(o[0], o[1]) for o in outs],
        name=name, compiler_params=_cparams(len(grid)),
    )(*[a for a, _, _ in ins])
    return out


def tile_bwd(name, fn, grid, ins, cots, diff, adds=None, raw=()):
    adds = adds or {}
    n_in, n_cot = len(ins), len(cots)
    add_pos = sorted(adds)
    diff_idx = [d[0] for d in diff]
    out_desc = [d[2] if len(d) > 2 and d[2] is not None else (ins[d[0]][0].shape, ins[d[0]][1], ins[d[0]][2]) for d in diff]
    out_dtypes = [d[3] if len(d) > 3 else F32 for d in diff]

    def body(*refs):
        pids = tuple(pl.program_id(a) for a in range(len(grid)))
        firsts = [_first_visit(d[1]) if d[1] else None for d in diff]
        vals = [_tile(r, i not in raw) for i, r in enumerate(refs[:n_in])]
        cot_vals = [_tile(r, True) for r in refs[n_in:n_in + n_cot]]
        add_vals = [_tile(r) for r in refs[n_in + n_cot:n_in + n_cot + len(add_pos)]]
        out_refs = refs[n_in + n_cot + len(add_pos):]

        def f(*dv):
            full = list(vals)
            for i, val in zip(diff_idx, dv):
                full[i] = val
            return fn(True, pids, *full)

        prim, vjp = jax.vjp(f, *[vals[i].astype(F32) for i in diff_idx])
        grads = list(vjp(tuple(c.astype(o.dtype) for c, o in zip(cot_vals, prim))))
        for pos, val in zip(add_pos, add_vals):
            extra = val.astype(F32) if firsts[pos] is None else jnp.where(firsts[pos], val.astype(F32), 0.0)
            grads[pos] = grads[pos] + extra
        for ref, val, first in zip(out_refs, grads, firsts):
            _store(ref, val, first)

    all_ins = list(ins) + list(cots) + [adds[p] for p in add_pos]
    out = pl.pallas_call(
        body, grid=grid, in_specs=_specs(all_ins),
        out_specs=[pl.BlockSpec(o[1], o[2]) for o in out_desc],
        out_shape=[jax.ShapeDtypeStruct(o[0], dt) for o, dt in zip(out_desc, out_dtypes)],
        name=name, compiler_params=_cparams(len(grid)),
    )(*[a for a, _, _ in all_ins])
    return out


def _pick(dim, cands):
    for c in cands:
        if dim % c == 0:
            return c
    return dim


MM_VMEM_BUDGET = 40 * 1024 * 1024
MM_TILES = (1024, 512, 1408, 256, 128)


def mm(name, a, b, mode, add=None, out_dtype=F32, blocks=None):
    wide = None
    if mode == "nn":
        (m, kk), n = a.shape, b.shape[-1]
    elif mode == "nt":
        (m, kk), n = a.shape, b.shape[-2]
    else:
        (kk, m), n = a.shape, b.shape[1]
    if blocks is not None:
        lo, n_blk = blocks
        wide = b.shape[-1] if mode != "tn" else n // n_blk
        if mode == "nn":
            n = wide * n_blk
    tm = _pick(m, MM_TILES)
    if mode == "nt" and blocks is not None:
        tn, tk = _pick(n, MM_TILES), _pick(wide, MM_TILES[:-1])
    elif blocks is not None:
        tn, tk = _pick(wide, MM_TILES[:-1]), _pick(kk, MM_TILES)
    else:
        tn, tk = _pick(n, MM_TILES), _pick(kk, MM_TILES)
    if mode == "tn" or blocks is None:
        tk = _pick(kk, (2048,) + MM_TILES)
    if mode != "tn" and add is None and m % 2048 == 0:
        windows = 2 * (2048 * tk * a.dtype.itemsize + tk * tn * b.dtype.itemsize + 2048 * tn * jnp.dtype(out_dtype).itemsize)
        if windows + 2048 * tn * 4 <= MM_VMEM_BUDGET:
            tm = 2048
    nk = kk // tk
    a_spec = pl.BlockSpec((tk, tm), lambda i, j, k: (k, i)) if mode == "tn" else pl.BlockSpec((tm, tk), lambda i, j, k: (i, k))
    o_spec = pl.BlockSpec((tm, tn), lambda i, j, k: (i, j))
    out_shape = (m, n)
    if blocks is None:
        b_spec = pl.BlockSpec((tn, tk), lambda i, j, k: (j, k)) if mode == "nt" else pl.BlockSpec((tk, tn), lambda i, j, k: (k, j))
    elif mode == "nn":
        per = wide // tn
        b_spec = pl.BlockSpec((1, tk, tn), lambda i, j, k: (lo + j // per, k, j % per))
    elif mode == "nt":
        per = wide // tk
        b_spec = pl.BlockSpec((1, tn, tk), lambda i, j, k: (lo + k // per, j, k % per))
    else:
        per = wide // tn
        b_spec = pl.BlockSpec((tk, tn), lambda i, j, k: (k, j))
        o_spec = pl.BlockSpec((1, tm, tn), lambda i, j, k: (j // per, i, j % per))
        out_shape = (n_blk, m, wide)

    def body(*refs):
        a_ref, b_ref = refs[0], refs[1]
        add_ref = refs[2] if add is not None else None
        o_ref, acc = refs[-2], refs[-1]
        k = pl.program_id(2)
        part = _bdot_impl(_tile(a_ref), _tile(b_ref), mode)

        @pl.when(k == 0)
        def _():
            acc[...] = part

        @pl.when(k > 0)
        def _():
            acc[...] += part

        @pl.when(k == nk - 1)
        def _():
            res = acc[...]
            if add_ref is not None:
                res = res + add_ref[...]
            o_ref[...] = res.astype(o_ref.dtype).reshape(o_ref.shape)

    operands = [a, b] + ([add] if add is not None else [])
    in_specs = [a_spec, b_spec] + ([o_spec] if add is not None else [])
    return pl.pallas_call(
        body, grid=(m // tm, n // tn, nk), in_specs=in_specs, out_specs=o_spec,
        out_shape=jax.ShapeDtypeStruct(out_shape, out_dtype),
        scratch_shapes=[pltpu.VMEM((tm, tn), F32)],
        name=name, compiler_params=_cparams(3),
    )(*operands)


def _rows(x, width=None, off=0, tm=256):
    width = x.shape[1] if width is None else width
    return (x, (tm, width), lambda i, off=off: (i, off))


def _whole(x):
    nd = x.ndim
    return (x, x.shape, lambda *pids, nd=nd: (0,) * nd)


RMS_ROWS = 512


def _rms_ops(x, gain):
    return [_rows(x, tm=RMS_ROWS), _whole(gain)]


def rms_fwd(name, x, gain):
    s, dm = x.shape
    return tile_fwd(name, _rms_fn, (s // RMS_ROWS,), _rms_ops(x, gain), [((s, dm), BF16, (RMS_ROWS, dm), lambda i: (i, 0), ())])[0]


def rms_bwd(name, x, gain, dh, dres):
    s = x.shape[0]
    return tile_bwd(name, _rms_fn, (s // RMS_ROWS,), _rms_ops(x, gain), [_rows(dh, tm=RMS_ROWS)], [(0, ()), (1, (0,))],
                    adds={0: _rows(dres, tm=RMS_ROWS)})


def loss_call(y, t):
    s, dm = y.shape
    dy, part = tile_fwd("loss", _loss_fn, (s // 256,), [_rows(y), _rows(t)],
                        [((s, dm), F32, (256, dm), lambda i: (i, 0), ()), ((8, LANES), F32, (8, LANES), lambda i: (0, 0), (0,))])
    return dy, part[0, 0]


def _fox_prep_ops(pm, gq, gk):
    tm = 512
    return [(pm, (tm, LANES), lambda i, j: (i, C_FQ // LANES + j)), (pm, (tm, LANES), lambda i, j: (i, C_FK // LANES + j)),
            _whole(gq), _whole(gk)]


def fox_prep_fwd(name, pm, gq, gk):
    s = pm.shape[0]
    out = ((s, BRANCH), BF16, (512, LANES), lambda i, j: (i, j), ())
    return tile_fwd(name, _fox_prep_fn, (s // 512, 4), _fox_prep_ops(pm, gq, gk), [out, out])


def fox_prep_bwd(name, pm, gq, gk, dqn, dkn):
    s = pm.shape[0]
    cot = lambda g: (g, (512, LANES), lambda i, j: (i, j))
    own = ((s, BRANCH), (512, LANES), lambda i, j: (i, j))
    return tile_bwd(name, _fox_prep_fn, (s // 512, 4), _fox_prep_ops(pm, gq, gk), [cot(dqn), cot(dkn)],
                    [(0, (), own, BF16), (1, (), own, BF16), (2, (0, 1)), (3, (0, 1))])


def _fox_gate_ops(f_t, bias):
    return [(f_t, (1,) + f_t.shape[1:], lambda h: (h, 0, 0)), (bias, (1, 1, 1), lambda h: (h, 0, 0))]


def fox_gate_fwd(name, f_t, bias):
    n_h = f_t.shape[0]
    return tile_fwd(name, _fox_gate_fn, (n_h,), _fox_gate_ops(f_t, bias),
                    [(f_t.shape, F32, (1,) + f_t.shape[1:], lambda h: (h, 0, 0), ())])[0]


def fox_gate_bwd(name, f_t, bias, dcum):
    n_h = f_t.shape[0]
    return tile_bwd(name, _fox_gate_fn, (n_h,), _fox_gate_ops(f_t, bias),
                    [(dcum, (1,) + f_t.shape[1:], lambda h: (h, 0, 0))], [(0, ()), (1, ())])


FOX_GROUPS = 4


def _fox_groups(s):
    per = s // FOX_BLOCK // FOX_GROUPS
    return [(g * per, per, (g + 1) * per * FOX_BLOCK) for g in range(FOX_GROUPS)]


def _fox_attn_ops(qn, kn, pm, cum_c, cum_r, q0, keys):
    nb = FOX_BLOCK
    return [(qn, (nb, LANES), lambda p, i: (q0 + i, p)), (kn, (keys, LANES), lambda p, i: (0, p)),
            (pm, (keys, LANES), lambda p, i: (0, C_FV // LANES + p)),
            (cum_c, (1, nb, 1), lambda p, i: (2 * p, q0 + i, 0)), (cum_c, (1, nb, 1), lambda p, i: (2 * p + 1, q0 + i, 0)),
            (cum_r, (1, 1, keys), lambda p, i: (2 * p, 0, 0)), (cum_r, (1, 1, keys), lambda p, i: (2 * p + 1, 0, 0))]


def fox_attn_fwd(name, qn, kn, pm, cum_c, cum_r):
    s = qn.shape[0]
    parts = []
    for g, (q0, n_q, keys) in enumerate(_fox_groups(s)):
        parts.append(tile_fwd(f"{name}_g{g}", functools.partial(_fox_attn_fn, q0), (4, n_q), _fox_attn_ops(qn, kn, pm, cum_c, cum_r, q0, keys),
                              [((n_q * FOX_BLOCK, BRANCH), BF16, (FOX_BLOCK, LANES), lambda p, i: (i, p), ())], raw=(0, 1, 2))[0])
    return jnp.concatenate(parts, axis=0)


def fox_attn_bwd(name, qn, kn, pm, cum_c, cum_r, dy):
    s = qn.shape[0]
    groups = _fox_groups(s)
    d_qn, by_q, tails = [None] * len(groups), [None] * len(groups), [None] * len(groups)
    below = None
    for g in reversed(range(len(groups))):
        q0, n_q, keys = groups[g]
        rows = n_q * FOX_BLOCK
        own_q = ((rows, BRANCH), (FOX_BLOCK, LANES), lambda p, i: (i, p))
        own_k = ((keys, BRANCH), (keys, LANES), lambda p, i: (0, p))
        pair_c = ((4, rows, 1), (1, FOX_BLOCK, 1), lambda p, i: (p, i, 0))
        pair_r = ((4, 1, keys), (1, 1, keys), lambda p, i: (p, 0, 0))
        adds = {}
        if below is not None:
            adds = {1: (below[0],) + own_k[1:], 2: (below[1],) + own_k[1:], 5: (below[2],) + pair_r[1:], 6: (below[3],) + pair_r[1:]}
        g_qn, g_kn, g_v, g_cqa, g_cqb, g_cka, g_ckb = tile_bwd(
            f"{name}_g{g}", functools.partial(_fox_attn_fn, q0), (4, n_q), _fox_attn_ops(qn, kn, pm, cum_c, cum_r, q0, keys),
            [(dy, (FOX_BLOCK, LANES), lambda p, i, q0=q0: (q0 + i, p))],
            [(0, (), own_q), (1, (1,), own_k), (2, (1,), own_k), (3, (), pair_c), (4, (), pair_c), (5, (1,), pair_r), (6, (1,), pair_r)],
            adds=adds)
        below = (g_kn, g_v, g_cka, g_ckb)
        lo = groups[g - 1][2] if g else 0
        d_qn[g] = g_qn
        by_q[g] = jnp.stack([g_cqa[:, :, 0], g_cqb[:, :, 0]], axis=1).reshape(8, rows)
        tails[g] = (g_kn[lo:], g_v[lo:], jnp.stack([g_cka[:, 0, lo:], g_ckb[:, 0, lo:]], axis=1).reshape(8, keys - lo))
    d_cum = jnp.concatenate(by_q, axis=1) + jnp.concatenate([t[2] for t in tails], axis=1)
    return jnp.concatenate(d_qn, axis=0), jnp.concatenate([t[0] for t in tails], axis=0), jnp.concatenate([t[1] for t in tails], axis=0), d_cum


def sconv_ops(pm, w):
    s = pm.shape[0]
    blk = lambda c0: (pm, (s, LANES), lambda j, c0=c0: (0, c0 // LANES + j))
    return [blk(C_SB), blk(C_SC), blk(C_SV), (w, (w.shape[0], LANES), lambda j: (0, j))]


def dnconv_ops(pm, w):
    s = pm.shape[0]
    return [(pm, (s, LANES), lambda j: (0, C_DN // LANES + j)), (w, (w.shape[0], LANES), lambda j: (0, j))]


def ffn_ops(ug, uv, w):
    s = ug.shape[0]
    n_t = D_FF // LANES
    return [(ug, (s, LANES), lambda j: (0, j)), (uv, (s, LANES), lambda j: (0, j)),
            (w, (w.shape[0], LANES), lambda j: (0, j)), (w, (w.shape[0], LANES), lambda j: (0, n_t + j))]


def _col_out(s, width, dtype=F32):
    return ((s, width), dtype, (s, LANES), lambda j: (0, j), ())


def _col_cot(g):
    return (g, (g.shape[0], LANES), lambda j: (0, j))


def merge_ops(yp, pm):
    gate = lambda b: (pm, (256, D_MODEL), lambda i, b=b: (i, C_GATE // D_MODEL + b))
    return [_rows(yp[0]), _rows(yp[1]), _rows(yp[2]), gate(0), gate(1), gate(2)]


def ple_ops(gpre, pe, x):
    return [_rows(gpre), _rows(pe), _rows(x)]


def adam_call(name, w, g, m, v):
    shape = w.shape
    last = shape[-1]
    rows = w.size // last
    flat = lambda t: t.reshape(rows, last)
    tm = rows
    for cand in (512, 256, 128, 64, 32, 16, 8):
        if rows % cand == 0 and cand * last * 4 <= 2 * 1024 * 1024:
            tm = cand
            break
    spec = lambda t: (flat(t), (tm, last), lambda i: (i, 0))
    out = ((rows, last), F32, (tm, last), lambda i: (i, 0), ())
    res = tile_fwd(name, _adam_fn, (rows // tm,), [spec(w), spec(g), spec(m), spec(v)], [out, out, out])
    return [r.reshape(shape) for r in res]


def _adam_layers_fn(d, pids, w, m, v, g0, g1):
    g = jnp.where(pids[0] == 0, g0, g1)
    return (g,) + _adam_fn(d, pids, w, g, m, v)


def adam_layers(name, w, m, v, g0, g1):
    _, rows, cols = w.shape
    tm = _row_tile(rows, cols)
    n_t = rows // tm
    lay = lambda t: (t, (1, tm, cols), lambda l, i: (l, i, 0))
    ins = [lay(w), lay(m), lay(v), (g0, (tm, cols), lambda l, i: (i * (1 - l) + (n_t - 1) * l, 0)), (g1, (tm, cols), lambda l, i: (i * l, 0))]
    out = (w.shape, F32, (1, tm, cols), lambda l, i: (l, i, 0), ())
    return tile_fwd(name, _adam_layers_fn, (2, n_t), ins, [out, out, out, out])


def adam_w_in(name, w, m, v, g0, g1):
    rows, n_l, cols = w.shape

    def body(w_ref, m_ref, v_ref, g0_ref, g1_ref, g_out, d_out, m_out, v_out):
        step = 64

        def update(at):
            g0, g1 = g0_ref[at, :], g1_ref[at, :]
            layer = _iota((g0.shape[0], n_l, LANES), 1)
            g = jnp.where(layer == 0, g0[:, None, :], g1[:, None, :])
            delta, m2, v2 = _adam_fn(False, None, w_ref[at], g, m_ref[at], v_ref[at])
            for ref, val in ((g_out, g), (d_out, delta), (m_out, m2), (v_out, v2)):
                ref[at] = val

        def some_rows(i, carry):
            update(pl.ds(pl.multiple_of(i * step, step), step))
            return carry

        lax.fori_loop(0, rows // step, some_rows, 0)
        if rows % step:
            update(pl.ds(rows - rows % step, rows % step))

    both = pl.BlockSpec((rows, n_l, LANES), lambda j: (0, 0, j))
    one = pl.BlockSpec((rows, LANES), lambda j: (0, j))
    return pl.pallas_call(
        body, grid=(cols // LANES,), in_specs=[both, both, both, one, one], out_specs=[both] * 4,
        out_shape=[jax.ShapeDtypeStruct(w.shape, F32)] * 4, name=name, compiler_params=_cparams(1),
    )(w, m, v, g0, g1)


DN_GROUP = 4


def _dn_local_specs():
    rows = DN_GROUP * DN_CHUNK
    return [pl.BlockSpec((rows, 3 * BRANCH), lambda j: (j, 0)), pl.BlockSpec((rows, LANES), lambda j: (j, 0)),
            pl.BlockSpec((DN_GROUP, DN_HEADS, DN_CHUNK), lambda j: (j, 0, 0)), pl.BlockSpec((2, DN_HEADS), lambda j: (0, 0))]


def _dn_group_inputs(qkv, ps, a_rows, c):
    lo = c * DN_CHUNK
    heads = _split_heads(qkv[lo:lo + DN_CHUNK])
    return heads[0:4], heads[4:8], heads[8:12], ps[lo:lo + DN_CHUNK], a_rows[c]


def dn_local_fwd(name, dn_act, ps, a_rows, ad):
    s = dn_act.shape[0]
    n_c, n_g = s // DN_CHUNK, s // (DN_GROUP * DN_CHUNK)
    rows = DN_GROUP * DN_CHUNK

    def body(qkv_ref, ps_ref, ar_ref, ad_ref, u_ref, kc_ref, qd_ref, kd_ref, qk_ref, gl_ref):
        qkv, ps_v, a_rows_v, ad_v = qkv_ref[...], ps_ref[...], ar_ref[...], ad_ref[...]
        args = [[] for _ in range(8)]
        for c in range(DN_GROUP):
            q4, k4, v4, ps_c, ar_c = _dn_group_inputs(qkv, ps_v, a_rows_v, c)
            for lst, vals in zip(args, (q4, k4, v4) + _dn_gates(ps_c, ar_c, ad_v)):
                lst.extend(vals)
        everything = _dn_local(False, *args)
        for c in range(DN_GROUP):
            res = everything[c * DN_HEADS:(c + 1) * DN_HEADS]
            at = pl.ds(c * DN_CHUNK, DN_CHUNK)
            for ref, i in ((u_ref, 0), (kc_ref, 1), (qd_ref, 2), (kd_ref, 3)):
                ref[at, :] = jnp.concatenate([r[i] for r in res], axis=1)
            for h in range(DN_HEADS):
                qk_ref[c, h] = res[h][4]
            gl_ref[c] = _head_rows([r[5] for r in res])

    wide = pl.BlockSpec((rows, BRANCH), lambda j: (j, 0))
    return pl.pallas_call(
        body, grid=(n_g,), in_specs=_dn_local_specs(),
        out_specs=[wide, wide, wide, wide, pl.BlockSpec((DN_GROUP, DN_HEADS, DN_CHUNK, DN_CHUNK), lambda j: (j, 0, 0, 0)),
                   pl.BlockSpec((DN_GROUP, 8, LANES), lambda j: (j, 0, 0))],
        out_shape=[jax.ShapeDtypeStruct((s, BRANCH), F32)] * 4 + [jax.ShapeDtypeStruct((n_c, DN_HEADS, DN_CHUNK, DN_CHUNK), F32),
                                                                 jax.ShapeDtypeStruct((n_c, 8, LANES), F32)],
        name=name, compiler_params=_cparams(1),
    )(dn_act, ps, a_rows, ad)


def dn_local_bwd(name, dn_act, ps, a_rows, ad, cots):
    s = dn_act.shape[0]
    n_c, n_g = s // DN_CHUNK, s // (DN_GROUP * DN_CHUNK)
    rows = DN_GROUP * DN_CHUNK

    def body(qkv_ref, ps_ref, ar_ref, ad_ref, du_ref, dkc_ref, dqd_ref, dkd_ref, dqk_ref, dgl_ref, dqkv_ref, dps_ref, dar_ref, dad_ref):
        first = pl.program_id(0) == 0
        qkv, ps_v, a_rows_v, ad_v = qkv_ref[...], ps_ref[...], ar_ref[...], ad_ref[...]
        d_wide = [r[...] for r in (du_ref, dkc_ref, dqd_ref, dkd_ref)]
        qs, ks, vs, ps_cs, ar_cs, cot = [], [], [], [], [], []
        for c in range(DN_GROUP):
            q4, k4, v4, ps_c, ar_c = _dn_group_inputs(qkv, ps_v, a_rows_v, c)
            qs, ks, vs, ps_cs, ar_cs = qs + q4, ks + k4, vs + v4, ps_cs + [ps_c], ar_cs + [ar_c]
            lo = c * DN_CHUNK
            d_tiles = [_split_heads(t[lo:lo + DN_CHUNK]) for t in d_wide]
            d_gl = dgl_ref[c]
            cot += [(d_tiles[0][h], d_tiles[1][h], d_tiles[2][h], d_tiles[3][h], dqk_ref[c, h], _col(_row(d_gl, h), 0))
                    for h in range(DN_HEADS)]

        def f(qs, ks, vs, ps_cs, ar_cs, ad_v):
            gates = [[] for _ in range(5)]
            for ps_c, ar_c in zip(ps_cs, ar_cs):
                for lst, vals in zip(gates, _dn_gates(ps_c, ar_c, ad_v)):
                    lst.extend(vals)
            return _dn_local(True, qs, ks, vs, *gates)

        _, vjp = jax.vjp(f, qs, ks, vs, ps_cs, ar_cs, ad_v)
        d_q, d_k, d_v, d_ps, d_ar, d_ad = vjp(cot)
        for c in range(DN_GROUP):
            at, hs = pl.ds(c * DN_CHUNK, DN_CHUNK), slice(c * DN_HEADS, (c + 1) * DN_HEADS)
            dqkv_ref[at, :] = jnp.concatenate(d_q[hs] + d_k[hs] + d_v[hs], axis=1).astype(dqkv_ref.dtype)
            dps_ref[at, :] = d_ps[c]
            dar_ref[c] = d_ar[c]
        _store(dad_ref, d_ad, first)

    wide = pl.BlockSpec((rows, BRANCH), lambda j: (j, 0))
    specs = _dn_local_specs()
    return pl.pallas_call(
        body, grid=(n_g,),
        in_specs=specs + [wide, wide, wide, wide, pl.BlockSpec((DN_GROUP, DN_HEADS, DN_CHUNK, DN_CHUNK), lambda j: (j, 0, 0, 0)),
                          pl.BlockSpec((DN_GROUP, 8, LANES), lambda j: (j, 0, 0))],
        out_specs=specs,
        out_shape=[jax.ShapeDtypeStruct((s, 3 * BRANCH), F32), jax.ShapeDtypeStruct((s, LANES), F32),
                   jax.ShapeDtypeStruct((n_c, DN_HEADS, DN_CHUNK), F32), jax.ShapeDtypeStruct((2, DN_HEADS), F32)],
        name=name, compiler_params=_cparams(1),
    )(dn_act, ps, a_rows, ad, *cots)


def _dn_scan_specs(n_c, rev):
    idx = (lambda j: n_c - 1 - j) if rev else (lambda j: j)
    wide = pl.BlockSpec((DN_CHUNK, BRANCH), lambda j: (idx(j), 0))
    return [wide, wide, wide, wide, pl.BlockSpec((1, DN_HEADS, DN_CHUNK, DN_CHUNK), lambda j: (idx(j), 0, 0, 0)),
            pl.BlockSpec((1, 8, LANES), lambda j: (idx(j), 0, 0)), pl.BlockSpec((DN_CHUNK, BRANCH), lambda j: (idx(j), C_DZ // BRANCH)),
            pl.BlockSpec((1, DN_DH), lambda j: (0, 0))]


def _dn_scan_tiles(refs):
    u_ref, kc_ref, qd_ref, kd_ref, qk_ref, gl_ref, z_ref, g_ref = refs
    wide = [_split_heads(r[...]) for r in (u_ref, kc_ref, qd_ref, kd_ref)]
    gl = gl_ref[0]
    return [(wide[0][h], wide[1][h], wide[2][h], wide[3][h], qk_ref[0, h], _col(_row(gl, h), 0)) for h in range(DN_HEADS)], \
        _split_heads(z_ref[...].astype(F32)), g_ref[...]


def dn_scan_fwd(name, local, pm, gain):
    s = pm.shape[0]
    n_c = s // DN_CHUNK

    def body(*refs):
        y_ref, hist_ref, state = refs[8:]

        @pl.when(pl.program_id(0) == 0)
        def _():
            state[...] = jnp.zeros_like(state)

        hist_ref[0] = state[...]
        per_head, z4, gain_v = _dn_scan_tiles(refs[:8])
        ys, s_nexts = _dn_step(False, [state[h] for h in range(DN_HEADS)], per_head, z4, gain_v)
        for h in range(DN_HEADS):
            state[h] = s_nexts[h]
        y_ref[...] = jnp.concatenate(ys, axis=1).astype(y_ref.dtype)

    return pl.pallas_call(
        body, grid=(n_c,), in_specs=_dn_scan_specs(n_c, False),
        out_specs=[pl.BlockSpec((DN_CHUNK, BRANCH), lambda j: (j, 0)),
                   pl.BlockSpec((1, DN_HEADS, DN_DH, DN_DH), lambda j: (j, 0, 0, 0))],
        out_shape=[jax.ShapeDtypeStruct((s, BRANCH), BF16), jax.ShapeDtypeStruct((n_c, DN_HEADS, DN_DH, DN_DH), F32)],
        scratch_shapes=[pltpu.VMEM((DN_HEADS, DN_DH, DN_DH), F32)],
        name=name, compiler_params=_cparams(1),
    )(*local, pm, gain)


def dn_scan_bwd(name, local, pm, gain, hist, dy):
    s = pm.shape[0]
    n_c = s // DN_CHUNK

    def body(*refs):
        hist_ref, dy_ref = refs[8:10]
        du_ref, dkc_ref, dqd_ref, dkd_ref, dqk_ref, dgl_ref, dz_ref, dg_ref, d_state = refs[10:]
        first = pl.program_id(0) == 0

        @pl.when(first)
        def _():
            d_state[...] = jnp.zeros_like(d_state)

        per_head, z4, gain_v = _dn_scan_tiles(refs[:8])
        _, vjp = jax.vjp(functools.partial(_dn_step, True), [hist_ref[0, h] for h in range(DN_HEADS)], per_head, z4, gain_v)
        d_s, grads, d_z, d_gain = vjp((_split_heads(dy_ref[...].astype(F32)), [d_state[h] for h in range(DN_HEADS)]))
        for h in range(DN_HEADS):
            d_state[h] = d_s[h]
        for ref, i in ((du_ref, 0), (dkc_ref, 1), (dqd_ref, 2), (dkd_ref, 3)):
            ref[...] = jnp.concatenate([g[i] for g in grads], axis=1)
        dz_ref[...] = jnp.concatenate(d_z, axis=1).astype(dz_ref.dtype)
        for h in range(DN_HEADS):
            dqk_ref[0, h] = grads[h][4]
        dgl_ref[0] = _head_rows([g[5] for g in grads])
        _store(dg_ref, d_gain, first)

    rev = lambda j: n_c - 1 - j
    specs = _dn_scan_specs(n_c, True)
    return pl.pallas_call(
        body, grid=(n_c,),
        in_specs=specs + [pl.BlockSpec((1, DN_HEADS, DN_DH, DN_DH), lambda j: (rev(j), 0, 0, 0)),
                          pl.BlockSpec((DN_CHUNK, BRANCH), lambda j: (rev(j), 0))],
        out_specs=specs[:6] + [pl.BlockSpec((DN_CHUNK, BRANCH), lambda j: (rev(j), 0)), specs[7]],
        out_shape=[jax.ShapeDtypeStruct((s, BRANCH), F32)] * 4 + [
            jax.ShapeDtypeStruct((n_c, DN_HEADS, DN_CHUNK, DN_CHUNK), F32), jax.ShapeDtypeStruct((n_c, 8, LANES), F32),
            jax.ShapeDtypeStruct((s, BRANCH), BF16), jax.ShapeDtypeStruct((1, DN_DH), F32)],
        scratch_shapes=[pltpu.VMEM((DN_HEADS, DN_DH, DN_DH), F32)],
        name=name, compiler_params=_cparams(1),
    )(*local, pm, gain, hist, dy)


def _seq_layouts(cols, s):
    return cols.T.reshape(cols.shape[1], s // LANES, LANES)


def layer_fwd(li, x, p, w, more_weights=None):
    s = x.shape[0]
    n = lambda t: f"{t}_l{li}"
    h = rms_fwd(n("rms_mix"), x, w["g_mix"])
    pm = mm(n("in_main"), h, w["in_main"], "nn")
    ps = mm(n("in_small"), h, w["in_small"], "nn")
    qn, kn = fox_prep_fwd(n("fox_prep"), pm, w["gq"], w["gk"])
    f_t = _seq_layouts(ps[:, 0:8], s)
    cum = fox_gate_fwd(n("fox_gate"), f_t, w["b_f"])
    cum_c, cum_r = cum.reshape(8, s, 1), cum.reshape(8, 1, s)
    y_fox = fox_attn_fwd(n("fox_attn"), qn, kn, pm, cum_c, cum_r)
    y_sc = tile_fwd(n("sconv"), _sconv_fn, (BRANCH // LANES,), sconv_ops(pm, w["sc_conv_w"]), [_col_out(s, BRANCH, BF16)])[0]
    dn_act = tile_fwd(n("dnconv"), _dnconv_fn, (3 * BRANCH // LANES,), dnconv_ops(pm, w["dn_conv_w"]), [_col_out(s, 3 * BRANCH)])[0]
    a_rows = ps[:, 12:16].reshape(s // DN_CHUNK, DN_CHUNK, DN_HEADS).transpose(0, 2, 1)
    dn_local = dn_local_fwd(n("dn_local"), dn_act, ps, a_rows, w["ad"])
    y_dn, hist = dn_scan_fwd(n("dn_scan"), dn_local, pm, w["dn_gain"])
    ys = (y_fox, y_sc, y_dn)
    if more_weights is not None:
        w = {**w, **more_weights(y_dn)}
    yp = [mm(n(f"branch{b}"), ys[b], w["branch"][b], "nn", blocks=(0, N_CHIPS)) for b in range(3)]
    merged = tile_fwd(n("merge"), _merge_fn, (s // 256,), merge_ops(yp, pm), [((s, D_MODEL), BF16, (256, D_MODEL), lambda i: (i, 0), ())])[0]
    x1 = mm(n("w_o"), merged, w["o"], "nn", add=x)
    h2 = rms_fwd(n("rms_ffn"), x1, w["g_ffn"])
    ug = mm(n("up_g"), h2, w["up"], "nn", blocks=(0, 2))
    uv = mm(n("up_v"), h2, w["up"], "nn", blocks=(2, 2))
    act = tile_fwd(n("ffn_act"), _ffn_act_fn, (D_FF // LANES,), ffn_ops(ug, uv, w["ffn_conv_w"]), [_col_out(s, D_FF, BF16)])[0]
    x2 = mm(n("down"), act, w["down"], "nn", add=x1)
    h3 = rms_fwd(n("rms_ple"), x2, w["g_ple"])
    gpre = mm(n("ple_gate"), h3, w["pg"], "nn")
    pe = mm(n("ple_emb"), p, w["ple"], "nn", blocks=(0, N_CHIPS))
    x3 = tile_fwd(n("ple"), _ple_fn, (s // 256,), ple_ops(gpre, pe, x2), [((s, D_MODEL), F32, (256, D_MODEL), lambda i: (i, 0), ())])[0]
    saved = dict(x=x, h=h, pm=pm, ps=ps, qn=qn, kn=kn, f_t=f_t, cum_c=cum_c, cum_r=cum_r, ys=ys, dn_act=dn_act, dn_local=dn_local,
                 a_rows=a_rows, hist=hist, yp=yp, merged=merged, x1=x1, h2=h2, ug=ug, uv=uv, act=act, x2=x2, h3=h3,
                 gpre=gpre, pe=pe, p=p)
    return x3, saved, w


def hang_on(w, token):
    zero = token[0, 0]
    small = ("g_mix", "g_ffn", "g_ple", "gq", "gk", "b_f", "ad", "dn_gain", "sc_conv_w", "dn_conv_w", "ffn_conv_w")
    return {**w, **{k: w[k] + zero for k in small}}


def layer_bwd(li, dx3, sv, w, hooks=None):
    hooks = hooks or {}

    def stage(key, after, w):
        return hang_on(w, hooks[key](after, g)) if key in hooks else w

    s = dx3.shape[0]
    n = lambda t: f"{t}_l{li}"
    g = {}
    col_own = lambda width: ((s, width), (s, LANES), lambda j: (0, j))
    d_gpre, d_pe = tile_bwd(n("ple_bwd"), _ple_fn, (s // 256,), ple_ops(sv["gpre"], sv["pe"], sv["x2"]), [_rows(dx3)],
                            [(0, (), None, BF16), (1, (), None, BF16)])
    g["w_ple"] = mm(n("d_w_ple"), sv["p"], d_pe, "tn", blocks=(0, N_CHIPS))
    g["w_ple_gate"] = mm(n("d_w_pg"), sv["h3"], d_gpre, "tn").reshape(N_CHIPS, -1, D_MODEL)
    dh3 = mm(n("d_h3"), d_gpre, w["pg"], "nt")
    dx2, d_g_ple = rms_bwd(n("rms_ple_bwd"), sv["x2"], w["g_ple"], dh3, dx3)
    dact = mm(n("d_act"), dx2, w["down"], "nt")
    g["w_down"] = mm(n("d_w_down"), sv["act"], dx2, "tn").reshape(N_CHIPS, -1, D_MODEL)
    taps_own = ((w["ffn_conv_w"].shape[0], D_FF), (w["ffn_conv_w"].shape[0], LANES), lambda j: (0, j))
    d_ug, d_uv, d_fw_g, d_fw_v = tile_bwd(n("ffn_act_bwd"), _ffn_act_fn, (D_FF // LANES,), ffn_ops(sv["ug"], sv["uv"], w["ffn_conv_w"]),
                                          [_col_cot(dact)], [(0, (), None, BF16), (1, (), None, BF16), (2, (), taps_own), (3, (), taps_own)])
    g["ffn_conv_w"] = jnp.concatenate([d_fw_g, d_fw_v], axis=1)
    g["w_up"] = jnp.concatenate([mm(n("d_w_up_g"), sv["h2"], d_ug, "tn", blocks=(0, 2)), mm(n("d_w_up_v"), sv["h2"], d_uv, "tn", blocks=(0, 2))])
    dh2 = mm(n("d_h2_v"), d_uv, w["up"], "nt", blocks=(2, 2), add=mm(n("d_h2_g"), d_ug, w["up"], "nt", blocks=(0, 2)))
    dx1, d_g_ffn = rms_bwd(n("rms_ffn_bwd"), sv["x1"], w["g_ffn"], dh2, dx2)
    w = stage("mid", dx1, w)
    dmerged = mm(n("d_merged"), dx1, w["o"], "nt")
    g["w_o"] = mm(n("d_w_o"), sv["merged"], dx1, "tn").reshape(N_CHIPS, -1, D_MODEL)
    gate_own = ((s, D_MODEL), (256, D_MODEL), lambda i: (i, 0))
    d_yp0, d_yp1, d_yp2, d_g0, d_g1, d_g2 = tile_bwd(
        n("merge_bwd"), _merge_fn, (s // 256,), merge_ops(sv["yp"], sv["pm"]), [_rows(dmerged)],
        [(0, (), None, BF16), (1, (), None, BF16), (2, (), None, BF16), (3, (), gate_own, BF16), (4, (), gate_own, BF16), (5, (), gate_own, BF16)])
    d_yp = (d_yp0, d_yp1, d_yp2)
    g["w_branch"] = jnp.concatenate([mm(n(f"d_w_branch{b}"), sv["ys"][b], d_yp[b], "tn", blocks=(0, N_CHIPS)) for b in range(3)], axis=1)
    d_ys = [mm(n(f"d_y{b}"), d_yp[b], w["branch"][b], "nt", blocks=(0, N_CHIPS)) for b in range(3)]
    w = stage("late", d_ys[2], w)
    *d_local, d_z, d_dngain = dn_scan_bwd(n("dn_scan_bwd"), sv["dn_local"], sv["pm"], w["dn_gain"], sv["hist"], d_ys[2])
    d_dnact, d_ps_dn, d_arows, d_ad = dn_local_bwd(n("dn_local_bwd"), sv["dn_act"], sv["ps"], sv["a_rows"], w["ad"], d_local)
    g["ad"], g["dn_norm_gain"] = d_ad, d_dngain[0]
    d_dnqkv, g["dn_conv_w"] = tile_bwd(n("dnconv_bwd"), _dnconv_fn, (3 * BRANCH // LANES,), dnconv_ops(sv["pm"], w["dn_conv_w"]),
                                       [_col_cot(d_dnact)], [(0, (), col_own(3 * BRANCH), BF16), (1, ())])
    d_sb, d_sc, d_sv, g["sc_conv_w"] = tile_bwd(n("sconv_bwd"), _sconv_fn, (BRANCH // LANES,), sconv_ops(sv["pm"], w["sc_conv_w"]), [_col_cot(d_ys[1])],
                                                [(0, (), col_own(BRANCH), BF16), (1, (), col_own(BRANCH), BF16), (2, (), col_own(BRANCH), BF16), (3, ())])
    w = stage("last", d_dnqkv, w)
    d_qn, d_kn, d_fv, d_cum = fox_attn_bwd(n("fox_attn_bwd"), sv["qn"], sv["kn"], sv["pm"], sv["cum_c"], sv["cum_r"], d_ys[0])
    d_ft, d_bf = fox_gate_bwd(n("fox_gate_bwd"), sv["f_t"], w["b_f"], d_cum.reshape(8, s // LANES, LANES))
    g["b_fox_f"] = d_bf.reshape(8)
    d_fq, d_fk, d_gq, d_gk = fox_prep_bwd(n("fox_prep_bwd"), sv["pm"], w["gq"], w["gk"], d_qn, d_kn)
    g["fox_q_gain"] = d_gq[0, :FOX_DH] + d_gq[0, FOX_DH:]
    g["fox_k_gain"] = d_gk[0, :FOX_DH] + d_gk[0, FOX_DH:]
    d_pm = jnp.concatenate([d_fq, d_fk, d_fv.astype(BF16), d_sb, d_sc, d_sv, d_dnqkv, d_z, d_g0, d_g1, d_g2], axis=1)
    d_a_cols = d_arows.transpose(0, 2, 1).reshape(s, DN_HEADS)
    d_f_cols = d_ft.reshape(8, s).T
    d_ps = d_ps_dn + jnp.concatenate([d_f_cols, jnp.zeros((s, 4), F32), d_a_cols, jnp.zeros((s, LANES - 16), F32)], axis=1)
    g["w_in"] = chip_blocks_w_in(mm(n("d_w_in_main"), d_pm, sv["h"], "tn"), mm(n("d_w_in_small"), d_ps, sv["h"], "tn"))
    w = stage("w_in", g["w_in"], w)
    dh = mm(n("d_h_small"), d_ps, w["in_small"], "nt", add=mm(n("d_h_main"), d_pm, w["in_main"], "nt"))
    dx, d_g_mix = rms_bwd(n("rms_mix_bwd"), sv["x"], w["g_mix"], dh, dx1)
    g["g_mix"], g["g_ffn"], g["g_ple"] = d_g_mix[0], d_g_ffn[0], d_g_ple[0]
    return dx, g


IN_SHARD = 2052
MAIN_RANGES = ((0, 1536), (1544, 3080), (3080, 4616), (4624, 5136), (5136, 8208))
SMALL_RANGES = ((1536, 1544), (4616, 4620), (4620, 4624))


def _from_chip_blocks(blocks, ranges):
    parts = []
    for lo, hi in ranges:
        for k in range(N_CHIPS):
            a0, a1 = max(lo, k * IN_SHARD), min(hi, (k + 1) * IN_SHARD)
            if a0 < a1:
                parts.append(blocks[k][:, a0 - k * IN_SHARD:a1 - k * IN_SHARD])
    return parts


def split_w_in(blocks):
    main = jnp.concatenate(_from_chip_blocks(blocks, MAIN_RANGES), axis=1)
    pad = jnp.zeros((blocks.shape[1], LANES - 16), blocks.dtype)
    return main, jnp.concatenate(_from_chip_blocks(blocks, SMALL_RANGES) + [pad], axis=1)


def chip_blocks_w_in(main, small):
    ranges = sorted([(lo, hi, "m") for lo, hi in MAIN_RANGES] + [(lo, hi, "s") for lo, hi in SMALL_RANGES])
    offs, m_off, s_off = {}, 0, 0
    for lo, hi in MAIN_RANGES:
        offs[lo] = m_off
        m_off += hi - lo
    for lo, hi in SMALL_RANGES:
        offs[lo] = s_off
        s_off += hi - lo
    blocks = []
    for k in range(N_CHIPS):
        parts = []
        for lo, hi, src in ranges:
            a0, a1 = max(lo, k * IN_SHARD), min(hi, (k + 1) * IN_SHARD)
            if a0 < a1:
                arr = main if src == "m" else small
                parts.append(arr[offs[lo] + a0 - lo:offs[lo] + a1 - lo])
        blocks.append(jnp.concatenate(parts, axis=0))
    return jnp.stack(blocks)


def later_weights(got):
    g_branch, g_o, g_up, g_down, g_pg, g_ple = got
    branch = g_branch.reshape(N_CHIPS, 3, BRANCH, -1)
    return dict(branch=[branch[:, b] for b in range(3)], o=g_o.reshape(D_MODEL, D_MODEL), up=g_up,
                down=g_down.reshape(D_FF, D_MODEL), pg=g_pg.reshape(D_MODEL, D_MODEL), ple=g_ple)


def layer_weights(li, got, conv, a):
    main, small = split_w_in(got[0])
    tile2 = lambda v: jnp.concatenate([v, v])[None, :]
    rest = later_weights(got[1:]) if len(got) > 1 else {}
    return dict(
        in_main=main, in_small=small, **rest,
        g_mix=a["g_mix"][li][None, :], g_ffn=a["g_ffn"][li][None, :], g_ple=a["g_ple"][li][None, :],
        gq=tile2(a["fox_q_gain"][li]), gk=tile2(a["fox_k_gain"][li]), b_f=a["b_fox_f"][li].reshape(8, 1, 1),
        ad=jnp.stack([a["dn_a_log"][li], a["dn_dt_bias"][li]]), dn_gain=a["dn_norm_gain"][li][None, :],
        sc_conv_w=conv["sc_conv_w"][li], dn_conv_w=conv["dn_conv_w"][li], ffn_conv_w=conv["ffn_conv_w"][li])


def pack_rows(arrs, dtype):
    flat = jnp.concatenate([t.reshape(-1).astype(dtype) for t in arrs])
    pad = (-flat.shape[0]) % (8 * LANES)
    if pad:
        flat = jnp.concatenate([flat, jnp.zeros((pad,), dtype)])
    return flat.reshape(-1, LANES)


def unpack_rows(buf, shapes):
    flat = buf.reshape(-1)
    out, off = [], 0
    for shp in shapes:
        size = 1
        for dim in shp:
            size *= dim
        out.append(flat[off:off + size].reshape(shp))
        off += size
    return out


ANY = pl.BlockSpec(memory_space=pl.ANY)


def _position():
    x, y, c = lax.axis_index("x"), lax.axis_index("y"), lax.axis_index("c")
    return x, y, c, [(1 - x, y), (x, 1 - y), (1 - x, 1 - y)]


def gather_small(name, block):
    m_per, n = block.shape

    def body(x_ref, out_ref, token, send_sems, recv_sems, local_sem):
        token[...] = jnp.zeros_like(token)
        x, y, c, chips = _position()
        me, sibling = (x, y, c), (x, y, 1 - c)

        def rows(px, py, pc):
            return out_ref.at[pl.ds((4 * px + 2 * py + pc) * m_per, m_per), :]

        def copy(k, blk, to, src=None):
            return pltpu.make_async_remote_copy(src_ref=rows(*blk) if src is None else src, dst_ref=rows(*blk),
                                                send_sem=send_sems.at[k], recv_sem=recv_sems.at[k], device_id=to, device_id_type=MESH)

        mine = pltpu.make_async_copy(x_ref, rows(*me), local_sem)
        mine.start()
        first = [copy(0, me, sibling, src=x_ref)] + [copy(1 + j, me, (*chip, c), src=x_ref) for j, chip in enumerate(chips)]
        for cp in first:
            cp.start()
        passed = [copy(4 + j, (*chip, c), sibling) for j, chip in enumerate(chips)]
        for j, chip in enumerate(chips):
            copy(1 + j, (*chip, c), me).wait_recv()
            passed[j].start()
        copy(0, sibling, me).wait_recv()
        for j, chip in enumerate(chips):
            copy(4 + j, (*chip, 1 - c), me).wait_recv()
        for cp in first + passed:
            cp.wait_send()
        mine.wait()

    in_vmem = pl.BlockSpec(memory_space=pltpu.VMEM)
    return pl.pallas_call(
        body, out_shape=[jax.ShapeDtypeStruct((8 * m_per, n), block.dtype), jax.ShapeDtypeStruct((8, LANES), F32)],
        in_specs=[in_vmem], out_specs=[in_vmem, in_vmem],
        scratch_shapes=[pltpu.SemaphoreType.DMA((7,)), pltpu.SemaphoreType.DMA((7,)), pltpu.SemaphoreType.DMA],
        name=name, compiler_params=pltpu.CompilerParams(vmem_limit_bytes=VMEM_LIMIT),
    )(block)


def _sems(n):
    return [pltpu.SemaphoreType.DMA((n,)), pltpu.SemaphoreType.DMA((n,))]


def _split_cols(rows):
    return (rows // 2) % 16 != 0


def _half(ref, which, lead=()):
    rows, cols = ref.shape[-2:]
    if _split_cols(rows):
        return ref.at[(*lead, slice(None), pl.ds(which * (cols // 2), cols // 2))]
    return ref.at[(*lead, pl.ds(which * (rows // 2), rows // 2), slice(None))]


def _half_shape(rows, cols):
    return (rows, cols // 2) if _split_cols(rows) else (rows // 2, cols)


def forward_halves(name, lands):
    n_w = len(lands)

    def body(*refs):
        outs = refs[n_w:2 * n_w]
        send_sems, recv_sems = refs[2 * n_w:]
        x, y, c, chips = _position()

        def copy(w, j, pc):
            cx, cy = chips[j]
            part = _half(outs[w], pc, (2 * cx + cy,))
            return pltpu.make_async_remote_copy(src_ref=part, dst_ref=part, send_sem=send_sems.at[3 * w + j], recv_sem=recv_sems.at[3 * w + j],
                                                device_id=(x, y, 1 - c), device_id_type=MESH)

        pairs = [(w, j) for w in range(n_w) for j in range(3)]
        for w, j in pairs:
            copy(w, j, c).start()
        for w, j in pairs:
            copy(w, j, 1 - c).wait_recv()
            copy(w, j, c).wait_send()

    return pl.pallas_call(
        body, out_shape=[jax.ShapeDtypeStruct(t.shape, t.dtype) for t in lands], in_specs=[ANY] * n_w, out_specs=[ANY] * n_w,
        input_output_aliases={w: w for w in range(n_w)}, scratch_shapes=_sems(3 * n_w), name=name,
    )(*lands)


def share_halves(name, bufs):
    n_w = len(bufs)

    def body(*refs):
        outs = refs[n_w:2 * n_w]
        send_sems, recv_sems = refs[2 * n_w:]
        x, y, c, _ = _position()

        def copy(w, pc):
            half = _half(outs[w], pc)
            return pltpu.make_async_remote_copy(src_ref=half, dst_ref=half, send_sem=send_sems.at[w], recv_sem=recv_sems.at[w],
                                                device_id=(x, y, 1 - c), device_id_type=MESH)

        for w in range(n_w):
            copy(w, c).start()
        for w in range(n_w):
            copy(w, 1 - c).wait_recv()
            copy(w, c).wait_send()

    return pl.pallas_call(
        body, out_shape=[jax.ShapeDtypeStruct(b.shape, b.dtype) for b in bufs], in_specs=[ANY] * n_w, out_specs=[ANY] * n_w,
        input_output_aliases={w: w for w in range(n_w)}, scratch_shapes=_sems(n_w), name=name,
    )(*bufs)


HBM = pl.BlockSpec(memory_space=pltpu.HBM)
SEM = pl.BlockSpec(memory_space=pltpu.SEMAPHORE)
EFFECT = pltpu.SideEffectType.DATAFLOW_SIDE_EFFECTING


def _exchange_copies(kind, srcs, lands):
    x, y, c, chips = _position()
    out = []
    for src, land in zip(srcs, lands):
        if kind == "swap":
            out.append((_half(src, 1 - c, (slice(None),)), land, (x, y, 1 - c)))
            continue
        for j, (cx, cy) in enumerate(chips):
            if kind == "gather":
                out.append((src, land.at[2 * x + y], (cx, cy, c)))
            elif kind == "gather_half":
                out.append((_half(src, c), _half(land, c, (2 * x + y,)), (cx, cy, c)))
            else:
                out.append((src.at[2 * cx + cy], land.at[j], (cx, cy, c)))
    return out


def _land_shapes(kind, srcs):
    if kind in ("gather", "gather_half"):
        return [(N_CHIPS,) + s.shape for s in srcs]
    if kind == "swap":
        return [(N_CHIPS,) + _half_shape(*s.shape[1:]) for s in srcs]
    return [(3,) + s.shape[1:] for s in srcs]


def exchange_start(name, kind, srcs):
    n_w = len(srcs)
    shapes = _land_shapes(kind, srcs)
    n_sem = n_w if kind == "swap" else 3 * n_w

    def body(*refs):
        ins, lands = refs[:n_w], refs[n_w:2 * n_w]
        send_sems, recv_sems = refs[2 * n_w:2 * n_w + 2]
        token = refs[-1]
        for i, (src, dst, dev) in enumerate(_exchange_copies(kind, ins, lands)):
            pltpu.make_async_remote_copy(src_ref=src, dst_ref=dst, send_sem=send_sems.at[i], recv_sem=recv_sems.at[i],
                                         device_id=dev, device_id_type=MESH).start()
        token[...] = jnp.zeros_like(token)

    out = pl.pallas_call(
        body, name=name,
        out_shape=(pltpu.SemaphoreType.DMA((n_sem,)), pltpu.SemaphoreType.DMA((n_sem,)),
                   *[pltpu.HBM(s.shape, s.dtype) for s in srcs], *[pltpu.HBM(shp, s.dtype) for shp, s in zip(shapes, srcs)],
                   jax.ShapeDtypeStruct((8, LANES), F32)),
        in_specs=(HBM,) * (2 * n_w), out_specs=(SEM, SEM) + (HBM,) * (2 * n_w) + (pl.BlockSpec(memory_space=pltpu.VMEM),),
        input_output_aliases={i: 2 + i for i in range(2 * n_w)},
        compiler_params=pltpu.CompilerParams(has_side_effects=EFFECT),
    )(*[pltpu.with_memory_space_constraint(s, pltpu.HBM) for s in srcs],
      *[pltpu.with_memory_space_constraint(lax.empty(shp, s.dtype), pltpu.HBM) for shp, s in zip(shapes, srcs)])
    return (kind, n_w, out[:-1]), out[-1]


def exchange_wait(name, handle, after):
    kind, n_w, (send_sems, recv_sems, *thru) = handle

    def body(*refs):
        ins, lands = refs[:n_w], refs[n_w:2 * n_w]
        send_sems, recv_sems = refs[2 * n_w:2 * n_w + 2]
        for i, (src, dst, dev) in enumerate(_exchange_copies(kind, ins, lands)):
            cp = pltpu.make_async_remote_copy(src_ref=src, dst_ref=dst, send_sem=send_sems.at[i], recv_sem=recv_sems.at[i],
                                              device_id=dev, device_id_type=MESH)
            cp.wait_send()
            cp.wait_recv()

    out = pl.pallas_call(
        body, name=name, out_shape=tuple(pltpu.HBM(t.shape, t.dtype) for t in thru),
        in_specs=(HBM,) * (2 * n_w) + (SEM, SEM, pl.BlockSpec(memory_space=pl.ANY)), out_specs=(HBM,) * (2 * n_w),
        input_output_aliases={i: i for i in range(2 * n_w)},
        compiler_params=pltpu.CompilerParams(has_side_effects=EFFECT),
    )(*thru, send_sems, recv_sems, after)
    return list(out[:n_w]), list(out[n_w:])


def _row_tile(rows, cols):
    best = rows
    if rows * cols * 4 <= 2 * 1024 * 1024:
        return rows
    for t in range(16, rows, 16):
        if rows % t == 0 and t * cols * 4 <= 2 * 1024 * 1024:
            best = t
    return best


def pair_sum(name, pos, grad, from_sibling):
    _, rows, cols = grad.shape
    h_rows, h_cols = _half_shape(rows, cols)
    tr = _row_tile(h_rows, h_cols)
    n_t = h_rows // tr

    def body(pos_ref, g_ref, s_ref, b_ref, f_ref):
        tot = g_ref[...] + s_ref[...]
        b_ref[...] = tot.astype(BF16)

        @pl.when(pl.program_id(1) == pos_ref[1])
        def _():
            f_ref[...] = tot[0]

    blk = pl.BlockSpec((1, tr, h_cols), lambda i, k, pos: (k, i, 0))
    if _split_cols(rows):
        mine = pl.BlockSpec((1, tr, h_cols), lambda i, k, pos: (k, i, pos[0]))
    else:
        mine = pl.BlockSpec((1, tr, h_cols), lambda i, k, pos: (k, pos[0] * n_t + i, 0))
    return pl.pallas_call(
        body, grid_spec=pltpu.PrefetchScalarGridSpec(
            num_scalar_prefetch=1, grid=(n_t, N_CHIPS), in_specs=[mine, blk],
            out_specs=[blk, pl.BlockSpec((tr, h_cols), lambda i, k, pos: (i, 0))]),
        out_shape=[jax.ShapeDtypeStruct((N_CHIPS, h_rows, h_cols), BF16), jax.ShapeDtypeStruct((h_rows, h_cols), F32)],
        name=name, compiler_params=_cparams(2),
    )(pos, grad, from_sibling)


def chip_sum(name, pos, own, landed, split_cols):
    half, cols = own.shape
    tr = _row_tile(half, cols)
    n_t = half // tr

    def body(pos_ref, p_ref, l_ref, o_ref):
        o_ref[...] = ((p_ref[...] + l_ref[0].astype(F32)) + l_ref[1].astype(F32)) + l_ref[2].astype(F32)

    if split_cols:
        out_spec, out_shape = pl.BlockSpec((tr, cols), lambda i, pos: (i, pos[0])), (half, 2 * cols)
    else:
        out_spec, out_shape = pl.BlockSpec((tr, cols), lambda i, pos: (pos[0] * n_t + i, 0)), (2 * half, cols)
    return pl.pallas_call(
        body, grid_spec=pltpu.PrefetchScalarGridSpec(
            num_scalar_prefetch=1, grid=(n_t,),
            in_specs=[pl.BlockSpec((tr, cols), lambda i, pos: (i, 0)), pl.BlockSpec((3, tr, cols), lambda i, pos: (0, i, 0))],
            out_specs=out_spec),
        out_shape=jax.ShapeDtypeStruct(out_shape, F32), name=name, compiler_params=_cparams(1),
    )(pos, own, landed)


class OverlappedReduceScatter:
    def __init__(self, tag, pos, grads):
        self.n = lambda t: f"{t}_{tag}"
        self.pos, self.grads = pos, grads
        self.swap, self.token = exchange_start(self.n("swap_start"), "swap", grads)

    def middle(self, after):
        self.grads, from_sibling = exchange_wait(self.n("swap_wait"), self.swap, after)
        self.sums = [pair_sum(self.n(f"pair_sum{w}"), self.pos, g, s) for w, (g, s) in enumerate(zip(self.grads, from_sibling))]
        self.scatter, self.token = exchange_start(self.n("scatter_start"), "scatter", [b for b, _ in self.sums])

    def finish(self, after):
        _, landed = exchange_wait(self.n("scatter_wait"), self.scatter, after)
        halves = [chip_sum(self.n(f"chip_sum{w}"), self.pos, own, l, _split_cols(g.shape[1]))
                  for w, ((_, own), l, g) in enumerate(zip(self.sums, landed, self.grads))]
        return share_halves(self.n("share_halves"), halves)


def sum_devices(gathered):
    m_per = gathered.shape[0] // 8

    def body(g_ref, o_ref):
        tot = g_ref[pl.ds(0, m_per), :]
        for dev in range(1, 8):
            tot = tot + g_ref[pl.ds(dev * m_per, m_per), :]
        o_ref[...] = tot

    return pl.pallas_call(
        body, out_shape=jax.ShapeDtypeStruct((m_per, gathered.shape[1]), F32),
        in_specs=[pl.BlockSpec(memory_space=pltpu.VMEM)], out_specs=pl.BlockSpec(memory_space=pltpu.VMEM), name="sum_devices",
    )(gathered)


def kernel(x, p, g_mix, w_in, b_fox_f, fox_q_gain, fox_k_gain, sc_conv_w, dn_conv_w, dn_a_log, dn_dt_bias, dn_norm_gain, w_branch, w_o, g_ffn, w_up, ffn_conv_w, w_down, g_ple, w_ple_gate, w_ple, loss_target, m_g_mix, m_w_in, m_b_fox_f, m_fox_q_gain, m_fox_k_gain, m_sc_conv_w, m_dn_conv_w, m_dn_a_log, m_dn_dt_bias, m_dn_norm_gain, m_w_branch, m_w_o, m_g_ffn, m_w_up, m_ffn_conv_w, m_w_down, m_g_ple, m_w_ple_gate, m_w_ple, v_g_mix, v_w_in, v_b_fox_f, v_fox_q_gain, v_fox_k_gain, v_sc_conv_w, v_dn_conv_w, v_dn_a_log, v_dn_dt_bias, v_dn_norm_gain, v_w_branch, v_w_o, v_g_ffn, v_w_up, v_ffn_conv_w, v_w_down, v_g_ple, v_w_ple_gate, v_w_ple):
    a = dict(g_mix=g_mix, w_in=w_in, b_fox_f=b_fox_f, fox_q_gain=fox_q_gain, fox_k_gain=fox_k_gain, sc_conv_w=sc_conv_w,
             dn_conv_w=dn_conv_w, dn_a_log=dn_a_log, dn_dt_bias=dn_dt_bias, dn_norm_gain=dn_norm_gain, w_branch=w_branch, w_o=w_o,
             g_ffn=g_ffn, w_up=w_up, ffn_conv_w=ffn_conv_w, w_down=w_down, g_ple=g_ple, w_ple_gate=w_ple_gate, w_ple=w_ple)
    mom = dict(g_mix=m_g_mix, w_in=m_w_in, b_fox_f=m_b_fox_f, fox_q_gain=m_fox_q_gain, fox_k_gain=m_fox_k_gain, sc_conv_w=m_sc_conv_w,
               dn_conv_w=m_dn_conv_w, dn_a_log=m_dn_a_log, dn_dt_bias=m_dn_dt_bias, dn_norm_gain=m_dn_norm_gain, w_branch=m_w_branch,
               w_o=m_w_o, g_ffn=m_g_ffn, w_up=m_w_up, ffn_conv_w=m_ffn_conv_w, w_down=m_w_down, g_ple=m_g_ple, w_ple_gate=m_w_ple_gate,
               w_ple=m_w_ple)
    var = dict(g_mix=v_g_mix, w_in=v_w_in, b_fox_f=v_b_fox_f, fox_q_gain=v_fox_q_gain, fox_k_gain=v_fox_k_gain, sc_conv_w=v_sc_conv_w,
               dn_conv_w=v_dn_conv_w, dn_a_log=v_dn_a_log, dn_dt_bias=v_dn_dt_bias, dn_norm_gain=v_dn_norm_gain, w_branch=v_w_branch,
               w_o=v_w_o, g_ffn=v_g_ffn, w_up=v_w_up, ffn_conv_w=v_ffn_conv_w, w_down=v_w_down, g_ple=v_g_ple, w_ple_gate=v_w_ple_gate,
               w_ple=v_w_ple)
    cx, cy, cc = lax.axis_index("x"), lax.axis_index("y"), lax.axis_index("c")
    chip = 2 * cx + cy
    pos = jnp.stack([cc, chip]).astype(jnp.int32)

    def as_blocks(t):
        return t.reshape(2, -1, t.shape[-1])

    def own_block_in(got, shards):
        return [lax.dynamic_update_slice(g, s[None], (chip, 0, 0)) for g, s in zip(got, shards)]

    conv_shapes = [a[nm].shape for nm in CONVS]
    conv_all, conv_token = gather_small("gather_conv_w", pack_rows([a[nm] for nm in CONVS], F32))
    w_in0 = [(as_blocks(a["w_in"])[0] + conv_token[0, 0]).astype(BF16)]
    gather_in0, gather_in0_token = exchange_start("gather_start_w_in_l0", "gather_half", w_in0)
    shards0 = w_in0 + [(as_blocks(a[nm])[0] + gather_in0_token[0, 0]).astype(BF16) for nm in BIG[1:]]
    gather0, gather0_token = exchange_start("gather_start_l0", "gather", shards0[1:])
    shards1 = [(as_blocks(a[nm])[1] + gather0_token[0, 0]).astype(BF16) for nm in BIG]
    gather1, gather1_in_token = exchange_start("gather_start_w_in_l1", "gather", shards1[:1])
    shards1[1:] = [s + gather1_in_token[0, 0].astype(BF16) for s in shards1[1:]]
    gather1_rest, gather1_token = exchange_start("gather_start_l1", "gather", shards1[1:])
    conv_rows = conv_all.shape[0] // 8
    conv_chip = [unpack_rows(conv_all[2 * k * conv_rows:(2 * k + 1) * conv_rows], conv_shapes) for k in range(N_CHIPS)]
    conv = {nm: jnp.concatenate([conv_chip[k][i] for k in range(N_CHIPS)], axis=2) for i, nm in enumerate(CONVS)}

    weights, saved = [None, None], [None, None]
    mine_in0, got_in0 = exchange_wait("gather_wait_w_in_l0", gather_in0, gather1_token)
    got_in0 = forward_halves("forward_w_in_l0", got_in0)
    first_weights = hang_on(layer_weights(0, own_block_in(got_in0, mine_in0), conv, a), gather1_token)

    def rest_of_layer0(after):
        mine, got = exchange_wait("gather_wait_l0", gather0, after)
        return later_weights(own_block_in(got, mine))

    act, saved[0], weights[0] = layer_fwd(0, x[0], p[0, 0], first_weights, more_weights=rest_of_layer0)
    mine1, got1 = exchange_wait("gather_wait_w_in_l1", gather1, act)

    def rest_of_layer1(after):
        mine, got = exchange_wait("gather_wait_l1", gather1_rest, after)
        return later_weights(own_block_in(got, mine))

    act, saved[1], weights[1] = layer_fwd(1, act, p[1, 0], layer_weights(1, own_block_in(got1, mine1), conv, a),
                                          more_weights=rest_of_layer1)
    d_act, loss_part = loss_call(act, loss_target[0])
    loss = lax.psum(loss_part, ("x", "y", "c"))
    layer_grads = [None, None]
    d_act, layer_grads[1] = layer_bwd(1, d_act, saved[1], weights[1])
    rs1 = OverlappedReduceScatter("l1", pos, [layer_grads[1][nm] for nm in BIG])
    rs0 = []

    def stage_mid(after, g):
        rs1.middle(after)
        return rs1.token

    def stage_late(after, g):
        rs0.append(OverlappedReduceScatter("l0", pos, [g[nm] for nm in BIG[1:]]))
        return rs0[0].token

    def stage_last(after, g):
        rs0[0].middle(after)
        return rs0[0].token

    def stage_w_in(after, g):
        rs0.append(OverlappedReduceScatter("w_in_l0", pos, [g["w_in"]]))
        return rs0[1].token

    d_act, layer_grads[0] = layer_bwd(0, d_act, saved[0], hang_on(weights[0], rs1.token),
                                      hooks=dict(mid=stage_mid, late=stage_late, last=stage_last, w_in=stage_w_in))
    rs0[1].middle(d_act)
    reduced = [rs0[0].finish(rs0[1].token), rs1.finish(rs0[1].token)]
    grad_x = d_act[None]

    def both(nm):
        return jnp.stack([layer_grads[0][nm], layer_grads[1][nm]])

    local = {nm: both(nm) for nm in ("g_mix", "b_fox_f", "fox_q_gain", "fox_k_gain", "dn_norm_gain", "g_ffn", "g_ple", "sc_conv_w",
                                      "dn_conv_w", "ffn_conv_w")}
    local["dn_a_log"] = jnp.stack([layer_grads[li]["ad"][0] for li in range(2)])
    local["dn_dt_bias"] = jnp.stack([layer_grads[li]["ad"][1] for li in range(2)])

    small_names = SMALL + CONVS
    small_shapes = [local[nm].shape for nm in small_names]
    small_sum = sum_devices(gather_small("gather_small_grads", pack_rows([local[nm] for nm in small_names], F32))[0])
    small_grads = dict(zip(small_names, unpack_rows(small_sum, small_shapes)))
    for nm in CONVS:
        width = a[nm].shape[2]
        small_grads[nm] = lax.dynamic_slice_in_dim(small_grads[nm], chip * width, width, axis=2)

    grads, deltas, new_m, new_v = dict(small_grads), {}, {}, {}
    for nm in small_names:
        deltas[nm], new_m[nm], new_v[nm] = adam_call(f"adam_{nm}", a[nm], grads[nm], mom[nm], var[nm])
    for i, nm in enumerate(BIG[1:]):
        res = adam_layers(f"adam_{nm}", as_blocks(a[nm]), as_blocks(mom[nm]), as_blocks(var[nm]), reduced[0][i], reduced[1][1 + i])
        grads[nm], deltas[nm], new_m[nm], new_v[nm] = [r.reshape(a[nm].shape) for r in res]
    stored = lambda t: jnp.transpose(t, (2, 0, 1))
    res = adam_w_in("adam_w_in", stored(a["w_in"]), stored(mom["w_in"]), stored(var["w_in"]), rs0[1].finish(deltas["w_ple"])[0], reduced[1][0])
    grads["w_in"], deltas["w_in"], new_m["w_in"], new_v["w_in"] = [jnp.transpose(r, (1, 2, 0)) for r in res]
    return (loss, grad_x, *[grads[nm] for nm in WEIGHTS], *[deltas[nm] for nm in WEIGHTS], *[new_m[nm] for nm in WEIGHTS],
            *[new_v[nm] for nm in WEIGHTS])
```

```python
import functools

import jax
import jax.numpy as jnp
from jax import lax
from jax.experimental import pallas as pl
from jax.experimental.pallas import tpu as pltpu

F32 = jnp.float32
BF16 = jnp.bfloat16
HI = lax.Precision.HIGHEST
SOLVE = lax.Precision.HIGH
MESH = pl.DeviceIdType.MESH

D_MODEL = 1024
BRANCH = 512
FOX_DH = 64
DN_DH = 128
DN_HEADS = 4
DN_CHUNK = 64
FOX_BLOCK = 128
D_FF = 2816
EPS = 1e-6
N_CHIPS = 4
LANES = 128

ADAM_LR, ADAM_B1, ADAM_B2, ADAM_EPS, ADAM_WD, ADAM_STEP = 0.001, 0.9, 0.999, 1e-08, 0.01, 10

VMEM_LIMIT = 56 * 1024 * 1024

C_FQ, C_FK, C_FV, C_SB, C_SC, C_SV, C_DN, C_DZ, C_GATE = 0, 512, 1024, 1536, 2048, 2560, 3072, 4608, 5120
IN_MAIN = 8192

BIG = ("w_in", "w_branch", "w_o", "w_up", "w_down", "w_ple_gate", "w_ple")
CONVS = ("sc_conv_w", "dn_conv_w", "ffn_conv_w")
SMALL = ("g_mix", "b_fox_f", "fox_q_gain", "fox_k_gain", "dn_a_log", "dn_dt_bias", "dn_norm_gain", "g_ffn", "g_ple")
WEIGHTS = ("g_mix", "w_in", "b_fox_f", "fox_q_gain", "fox_k_gain", "sc_conv_w", "dn_conv_w", "dn_a_log", "dn_dt_bias",
           "dn_norm_gain", "w_branch", "w_o", "g_ffn", "w_up", "ffn_conv_w", "w_down", "g_ple", "w_ple_gate", "w_ple")


def _iota(shape, dim):
    return lax.broadcasted_iota(jnp.int32, shape, dim)


def _dg(a, b, mode, prec=None):
    dims = {"nn": ((1,), (0,)), "nt": ((1,), (1,)), "tn": ((0,), (0,))}[mode]
    return lax.dot_general(a, b, (dims, ((), ())), precision=prec, preferred_element_type=F32)


def _bdot_impl(a, b, mode):
    return _dg(a.astype(BF16), b.astype(BF16), mode)


@functools.partial(jax.custom_vjp, nondiff_argnums=(2,))
def _bdot_diff(a, b, mode):
    return _bdot_impl(a, b, mode)


def _bdot_fwd(a, b, mode):
    return _bdot_impl(a, b, mode), (a, b)


def _bdot_bwd(mode, res, g):
    a, b = res
    if mode == "nn":
        da, db = _bdot_impl(g, b, "nt"), _bdot_impl(a, g, "tn")
    elif mode == "nt":
        da, db = _bdot_impl(g, b, "nn"), _bdot_impl(g, a, "tn")
    else:
        da, db = _bdot_impl(b, g, "nt"), _bdot_impl(a, g, "nn")
    return da.astype(a.dtype), db.astype(b.dtype)


_bdot_diff.defvjp(_bdot_fwd, _bdot_bwd)


def _bdot(d):
    return _bdot_diff if d else _bdot_impl


def _shift_impl(x, k):
    return jnp.where(_iota(x.shape, 0) >= k, pltpu.roll(x, k, 0), 0.0)


def _unshift_impl(g, k):
    n = g.shape[0]
    return jnp.where(_iota(g.shape, 0) < n - k, pltpu.roll(g, n - k, 0), 0.0)


@functools.partial(jax.custom_vjp, nondiff_argnums=(1,))
def _shift_diff(x, k):
    return _shift_impl(x, k)


_shift_diff.defvjp(lambda x, k: (_shift_impl(x, k), None), lambda k, _, g: (_unshift_impl(g, k),))


def _row(w, j):
    return jnp.sum(jnp.where(_iota(w.shape, 0) == j, w, 0.0), axis=0, keepdims=True)


def _col(w, j):
    return jnp.sum(jnp.where(_iota(w.shape, 1) == j, w, 0.0), axis=1, keepdims=True)


def _conv(d, x, w):
    shift = _shift_diff if d else _shift_impl
    taps = w.shape[0]
    y = x * _row(w, taps - 1)
    for j in range(taps - 1):
        y = y + shift(x, taps - 1 - j) * _row(w, j)
    return y


def _softplus(x):
    return jnp.maximum(x, 0.0) + jnp.log(1.0 + jnp.exp(-jnp.abs(x)))


def _sigmoid(x):
    return 0.5 * (jnp.tanh(0.5 * x) + 1.0)


def _silu(x):
    return x * _sigmoid(x)


def _rms(x, gain):
    return x * lax.rsqrt(jnp.mean(x * x, axis=-1, keepdims=True) + EPS) * gain


def _rms_fn(d, pids, x, gain):
    return (_rms(x, gain),)


def _loss_fn(d, pids, y, t):
    e = y - t
    part = 0.5 / D_MODEL * jnp.sum(e * e, keepdims=True)
    return e * (1.0 / D_MODEL), jnp.broadcast_to(part, (8, LANES))


def _fox_prep_fn(d, pids, q, k, gq, gk):
    first = _iota(q.shape, 1) < FOX_DH

    def norm(x, gain):
        sq = x * x
        ss_a = jnp.sum(jnp.where(first, sq, 0.0), axis=1, keepdims=True)
        ss_b = jnp.sum(jnp.where(first, 0.0, sq), axis=1, keepdims=True)
        rs = jnp.where(first, lax.rsqrt(ss_a / FOX_DH + EPS), lax.rsqrt(ss_b / FOX_DH + EPS))
        return x * rs * gain

    return norm(q, gq) * FOX_DH ** -0.5, norm(k, gk)


def _fox_gate_fn(d, pids, f, bias):
    logf = -_softplus(-(f + bias))
    n_r, n_c = logf.shape
    tri = (_iota((n_c, n_c), 0) <= _iota((n_c, n_c), 1)).astype(F32)
    within = _dg(logf, tri, "nn", HI)
    tot = jnp.broadcast_to(jnp.sum(logf, axis=1, keepdims=True), logf.shape)
    below = (_iota((n_r, n_r), 1) < _iota((n_r, n_r), 0)).astype(F32)
    return (within + _dg(below, tot, "nn", HI),)


def _fox_attn_fn(q_block0, d, pids, q, k, v, cq_a, cq_b, ck_a, ck_b):
    dot = _bdot(d)
    first = _iota(q.shape, 1) < FOX_DH
    n_q, n_k = q.shape[0], k.shape[0]
    causal = ((q_block0 + pids[1]) * n_q + _iota((n_q, n_k), 0)) >= _iota((n_q, n_k), 1)

    qs = [jnp.where(first, q, 0.0), jnp.where(first, 0.0, q)]
    s = _each(lambda qh, cq, ck: jnp.where(causal, dot(qh, k, "nt") + cq - ck, -1e30), qs, [cq_a, cq_b], [ck_a, ck_b])
    e = [jnp.exp(si - lax.stop_gradient(jnp.max(si, axis=1, keepdims=True))) for si in s]
    o_a, o_b = [dot(ei * (1.0 / jnp.sum(ei, axis=1, keepdims=True)), v, "nn") for ei in e]
    return (jnp.where(first, o_a, o_b),)


def _sconv_fn(d, pids, sb, sc, sv, w):
    return (sb * _conv(d, sc * sv, w),)


def _dnconv_fn(d, pids, x, w):
    return (_silu(_conv(d, x, w)),)


def _merge_fn(d, pids, y0, y1, y2, g0, g1, g2):
    return (_sigmoid(g0) * y0 + _sigmoid(g1) * y1 + _sigmoid(g2) * y2,)


def _ffn_act_fn(d, pids, ug, uv, wg, wv):
    return (_silu(_conv(d, ug, wg)) * _conv(d, uv, wv),)


def _ple_fn(d, pids, gpre, pe, x):
    return (x + _sigmoid(gpre) * pe,)


def _adam_fn(d, pids, w, g, m, v):
    m2 = ADAM_B1 * m + (1.0 - ADAM_B1) * g
    v2 = ADAM_B2 * v + (1.0 - ADAM_B2) * (g * g)
    m_hat = m2 / (1.0 - ADAM_B1 ** ADAM_STEP)
    v_hat = v2 / (1.0 - ADAM_B2 ** ADAM_STEP)
    delta = -ADAM_LR * (m_hat / (jnp.sqrt(v_hat) + ADAM_EPS) + ADAM_WD * w)
    return delta, m2, v2


def _each(fn, *lists):
    return [fn(*args) for args in zip(*lists)]


def _tri_inv_impl(mats):
    n = mats[0].shape[0]
    r, c = _iota((n, n), 0), _iota((n, n), 1)
    diag_blk = (r >> 4) == (c >> 4)
    eye = (r == c).astype(F32)
    mm = lambda us, ws: _each(lambda u, w: _dg(u, w, "nn", SOLVE), us, ws)
    grow = lambda ps, xs: _each(lambda p, px: p + px, ps, mm(ps, xs))
    x = [jnp.where(diag_blk, -a, 0.0) for a in mats]
    p = [eye + xi for xi in x]
    x2 = mm(x, x)
    p = grow(p, x2)
    x4 = mm(x2, x2)
    p = grow(p, x4)
    p = grow(p, mm(x4, x4))
    y = [-yi for yi in mm(p, [jnp.where(diag_blk, 0.0, a) for a in mats])]
    q = grow([eye + yi for yi in y], mm(y, y))
    return mm(q, p)


@jax.custom_vjp
def _tri_inv_diff(mats):
    return _tri_inv_impl(mats)


def _tri_inv_fwd(mats):
    ts = _tri_inv_impl(mats)
    return ts, ts


def _tri_inv_bwd(ts, gs):
    left = _each(lambda t, g: _dg(t, g, "tn", SOLVE), ts, gs)
    return ([-m for m in _each(lambda l, t: _dg(l, t, "nt", SOLVE), left, ts)],)


_tri_inv_diff.defvjp(_tri_inv_fwd, _tri_inv_bwd)


def _dn_local(d, qs, ks, vs, a_cs, a_rs, b_cs, a_logs, dt_bs):
    dot = _bdot(d)
    inv = _tri_inv_diff if d else _tri_inv_impl
    n = qs[0].shape[0]
    r, c = _iota((n, n), 0), _iota((n, n), 1)
    incl, strict, upper = r >= c, r > c, r <= c
    qs = [q * lax.rsqrt(jnp.sum(q * q, axis=1, keepdims=True) + EPS) * DN_DH ** -0.5 for q in qs]
    ks = [k * lax.rsqrt(jnp.sum(k * k, axis=1, keepdims=True) + EPS) for k in ks]
    betas = [_sigmoid(b) for b in b_cs]
    rates = [-jnp.exp(a) for a in a_logs]
    g_cs = _each(lambda rate, a, dt: rate * _softplus(a + dt), rates, a_cs, dt_bs)
    g_rs = _each(lambda rate, a, dt: rate * _softplus(a + dt), rates, a_rs, dt_bs)
    gcum_cs = [jnp.sum(jnp.where(incl, g, 0.0), axis=1, keepdims=True) for g in g_rs]
    gcum_rs = [jnp.sum(jnp.where(upper, g, 0.0), axis=0, keepdims=True) for g in g_cs]
    decays = _each(lambda gc, gr: jnp.exp(jnp.where(incl, gc - gr, -1e30)), gcum_cs, gcum_rs)
    kbs = _each(lambda k, b: k * b, ks, betas)
    kk = _each(lambda kb, k: dot(kb, k, "nt"), kbs, ks)
    ts = inv(_each(lambda m, dec: jnp.where(strict, m * dec, 0.0), kk, decays))
    e_gs = [jnp.exp(g) for g in gcum_cs]
    us = _each(lambda t, v, b: _dg(t, v * b, "nn", SOLVE), ts, vs, betas)
    k_cums = _each(lambda t, kb, e: _dg(t, kb * e, "nn", SOLVE), ts, kbs, e_gs)
    qk = _each(lambda q, k: dot(q, k, "nt"), qs, ks)
    qk = _each(lambda m, dec: jnp.where(incl, m * dec, 0.0), qk, decays)
    g_lasts = [jnp.sum(g, axis=0, keepdims=True) for g in g_cs]
    q_decs = _each(lambda q, e: q * e, qs, e_gs)
    k_decs = _each(lambda k, gl, gc: k * jnp.exp(gl - gc), ks, g_lasts, gcum_cs)
    return list(zip(us, k_cums, q_decs, k_decs, qk, g_lasts))


def _dn_step(d, s_prevs, items, zs, gain):
    dot = _bdot(d)
    us, k_cums, q_decs, k_decs, qks, g_lasts = [list(t) for t in zip(*items)]
    v_news = _each(lambda u, kc, s: u - dot(kc, s, "nn"), us, k_cums, s_prevs)
    inter = _each(lambda qd, s: dot(qd, s, "nn"), q_decs, s_prevs)
    outs = _each(lambda o, qk, vn: o + dot(qk, vn, "nn"), inter, qks, v_news)
    s_nexts = _each(lambda s, gl, kd, vn: s * jnp.exp(gl) + dot(kd, vn, "tn"), s_prevs, g_lasts, k_decs, v_news)
    return _each(lambda o, z: _rms(o, gain) * _silu(z), outs, zs), s_nexts


def _split_heads(t):
    return [t[:, h * DN_DH:(h + 1) * DN_DH] for h in range(t.shape[1] // DN_DH)]


def _dn_gates(ps, a_rows, ad):
    hs = range(DN_HEADS)
    return ([_col(ps, 12 + h) for h in hs], [_row(a_rows, h) for h in hs], [_col(ps, 8 + h) for h in hs],
            [_col(_row(ad, 0), h) for h in hs], [_col(_row(ad, 1), h) for h in hs])


def _head_rows(vals):
    row = _iota((8, LANES), 0)
    tile = jnp.zeros((8, LANES), F32)
    for h, val in enumerate(vals):
        tile = tile + jnp.where(row == h, val, 0.0)
    return tile


def _cparams(n_axes):
    return pltpu.CompilerParams(dimension_semantics=("arbitrary",) * n_axes, vmem_limit_bytes=VMEM_LIMIT)


def _first_visit(acc_axes):
    cond = None
    for a in acc_axes:
        here = pl.program_id(a) == 0
        cond = here if cond is None else jnp.logical_and(cond, here)
    return cond


def _tile(ref, widen=False):
    val = ref[...]
    shape = val.shape
    while len(shape) > 2 and shape[0] == 1:
        shape = shape[1:]
    val = val.reshape(shape)
    return val.astype(F32) if widen and val.dtype == BF16 else val


def _store(ref, val, first):
    val = val.astype(ref.dtype).reshape(ref.shape)
    if first is None:
        ref[...] = val
        return

    @pl.when(first)
    def _():
        ref[...] = val

    @pl.when(jnp.logical_not(first))
    def _():
        ref[...] += val


def _specs(ops):
    return [pl.BlockSpec(block, imap) for _, block, imap in ops]


def tile_fwd(name, fn, grid, ins, outs, raw=()):
    n_in = len(ins)

    def body(*refs):
        pids = tuple(pl.program_id(a) for a in range(len(grid)))
        firsts = [_first_visit(o[4]) if o[4] else None for o in outs]
        res = fn(False, pids, *[_tile(r, i not in raw) for i, r in enumerate(refs[:n_in])])
        for ref, val, first in zip(refs[n_in:], res, firsts):
            _store(ref, val, first)

    out = pl.pallas_call(
        body, grid=grid, in_specs=_specs(ins),
        out_specs=[pl.BlockSpec(o[2], o[3]) for o in outs],
        out_shape=[jax.ShapeDtypeStruct(o[0], o[1]) for o in outs],
        name=name, compiler_params=_cparams(len(grid)),
    )(*[a for a, _, _ in ins])
    return out


def tile_bwd(name, fn, grid, ins, cots, diff, adds=None, raw=()):
    adds = adds or {}
    n_in, n_cot = len(ins), len(cots)
    add_pos = sorted(adds)
    diff_idx = [d[0] for d in diff]
    out_desc = [d[2] if len(d) > 2 and d[2] is not None else (ins[d[0]][0].shape, ins[d[0]][1], ins[d[0]][2]) for d in diff]
    out_dtypes = [d[3] if len(d) > 3 else F32 for d in diff]

    def body(*refs):
        pids = tuple(pl.program_id(a) for a in range(len(grid)))
        firsts = [_first_visit(d[1]) if d[1] else None for d in diff]
        vals = [_tile(r, i not in raw) for i, r in enumerate(refs[:n_in])]
        cot_vals = [_tile(r, True) for r in refs[n_in:n_in + n_cot]]
        add_vals = [_tile(r) for r in refs[n_in + n_cot:n_in + n_cot + len(add_pos)]]
        out_refs = refs[n_in + n_cot + len(add_pos):]

        def f(*dv):
            full = list(vals)
            for i, val in zip(diff_idx, dv):
                full[i] = val
            return fn(True, pids, *full)

        prim, vjp = jax.vjp(f, *[vals[i].astype(F32) for i in diff_idx])
        grads = list(vjp(tuple(c.astype(o.dtype) for c, o in zip(cot_vals, prim))))
        for pos, val in zip(add_pos, add_vals):
            extra = val.astype(F32) if firsts[pos] is None else jnp.where(firsts[pos], val.astype(F32), 0.0)
            grads[pos] = grads[pos] + extra
        for ref, val, first in zip(out_refs, grads, firsts):
            _store(ref, val, first)

    all_ins = list(ins) + list(cots) + [adds[p] for p in add_pos]
    out = pl.pallas_call(
        body, grid=grid, in_specs=_specs(all_ins),
        out_specs=[pl.BlockSpec(o[1], o[2]) for o in out_desc],
        out_shape=[jax.ShapeDtypeStruct(o[0], dt) for o, dt in zip(out_desc, out_dtypes)],
        name=name, compiler_params=_cparams(len(grid)),
    )(*[a for a, _, _ in all_ins])
    return out


def _pick(dim, cands):
    for c in cands:
        if dim % c == 0:
            return c
    return dim


MM_VMEM_BUDGET = 40 * 1024 * 1024
MM_TILES = (1024, 512, 1408, 256, 128)


def mm(name, a, b, mode, add=None, out_dtype=F32, blocks=None, into=None):
    wide = None
    if mode == "nn":
        (m, kk), n = a.shape, b.shape[-1]
    elif mode == "nt":
        (m, kk), n = a.shape, b.shape[-2]
    else:
        (kk, m), n = a.shape, b.shape[1]
    if blocks is not None:
        lo, n_blk = blocks
        wide = b.shape[-1] if mode != "tn" else n // n_blk
        if mode == "nn":
            n = wide * n_blk
    tm = _pick(m, MM_TILES)
    if mode == "nt" and blocks is not None:
        tn, tk = _pick(n, MM_TILES), _pick(wide, MM_TILES[:-1])
    elif blocks is not None:
        tn, tk = _pick(wide, MM_TILES[:-1]), _pick(kk, MM_TILES)
    else:
        tn, tk = _pick(n, MM_TILES), _pick(kk, MM_TILES)
    if mode == "tn" or blocks is None:
        tk = _pick(kk, (2048,) + MM_TILES)
    if mode != "tn" and add is None and m % 2048 == 0 and (n // tn) * (kk // tk) > 1:
        windows = 2 * (2048 * tk * a.dtype.itemsize + tk * tn * b.dtype.itemsize + 2048 * tn * jnp.dtype(out_dtype).itemsize)
        if windows + 2048 * tn * 4 <= MM_VMEM_BUDGET:
            tm = 2048
    nk = kk // tk
    a_spec = pl.BlockSpec((tk, tm), lambda i, j, k: (k, i)) if mode == "tn" else pl.BlockSpec((tm, tk), lambda i, j, k: (i, k))
    o_spec = pl.BlockSpec((tm, tn), lambda i, j, k: (i, j))
    out_shape = (m, n)
    if blocks is None:
        b_spec = pl.BlockSpec((tn, tk), lambda i, j, k: (j, k)) if mode == "nt" else pl.BlockSpec((tk, tn), lambda i, j, k: (k, j))
    elif mode == "nn":
        per = wide // tn
        b_spec = pl.BlockSpec((1, tk, tn), lambda i, j, k: (lo + j // per, k, j % per))
    elif mode == "nt":
        per = wide // tk
        b_spec = pl.BlockSpec((1, tn, tk), lambda i, j, k: (lo + k // per, j, k % per))
    else:
        per = wide // tn
        total, first = (into[0], into[1]) if into is not None else (n_blk, 0)
        b_spec = pl.BlockSpec((tk, tn), lambda i, j, k: (k, j))
        o_spec = pl.BlockSpec((1, tm, tn), lambda i, j, k: (first + j // per, i, j % per))
        out_shape = (total, m, wide)

    def body(*refs):
        a_ref, b_ref = refs[0], refs[1]
        add_ref = refs[2] if add is not None else None
        o_ref, acc = refs[-2], refs[-1]
        k = pl.program_id(2)
        part = _bdot_impl(_tile(a_ref), _tile(b_ref), mode)

        @pl.when(k == 0)
        def _():
            acc[...] = part

        @pl.when(k > 0)
        def _():
            acc[...] += part

        @pl.when(k == nk - 1)
        def _():
            res = acc[...]
            if add_ref is not None:
                res = res + add_ref[...]
            o_ref[...] = res.astype(o_ref.dtype).reshape(o_ref.shape)

    operands = [a, b] + ([add] if add is not None else [])
    in_specs = [a_spec, b_spec] + ([o_spec] if add is not None else [])
    aliases = {}
    if into is not None and len(into) > 2:
        operands, in_specs, aliases = operands + [into[2]], in_specs + [pl.BlockSpec(memory_space=pl.ANY)], {len(operands): 0}
    return pl.pallas_call(
        body, grid=(m // tm, n // tn, nk), in_specs=in_specs, out_specs=o_spec,
        out_shape=jax.ShapeDtypeStruct(out_shape, out_dtype),
        scratch_shapes=[pltpu.VMEM((tm, tn), F32)], input_output_aliases=aliases,
        name=name, compiler_params=_cparams(3),
    )(*operands)


def _rows(x, width=None, off=0, tm=256):
    width = x.shape[1] if width is None else width
    return (x, (tm, width), lambda i, off=off: (i, off))


def _whole(x):
    nd = x.ndim
    return (x, x.shape, lambda *pids, nd=nd: (0,) * nd)


RMS_ROWS = 512


def _rms_ops(x, gain):
    return [_rows(x, tm=RMS_ROWS), _whole(gain)]


def rms_fwd(name, x, gain):
    s, dm = x.shape
    return tile_fwd(name, _rms_fn, (s // RMS_ROWS,), _rms_ops(x, gain), [((s, dm), BF16, (RMS_ROWS, dm), lambda i: (i, 0), ())])[0]


def rms_bwd(name, x, gain, dh, dres):
    s = x.shape[0]
    return tile_bwd(name, _rms_fn, (s // RMS_ROWS,), _rms_ops(x, gain), [_rows(dh, tm=RMS_ROWS)], [(0, ()), (1, (0,))],
                    adds={0: _rows(dres, tm=RMS_ROWS)})


def loss_call(y, t):
    s, dm = y.shape
    dy, part = tile_fwd("loss", _loss_fn, (s // 256,), [_rows(y), _rows(t)],
                        [((s, dm), F32, (256, dm), lambda i: (i, 0), ()), ((8, LANES), F32, (8, LANES), lambda i: (0, 0), (0,))])
    return dy, part[0, 0]


def _fox_prep_ops(pm, gq, gk):
    tm = 512
    return [(pm, (tm, LANES), lambda i, j: (i, C_FQ // LANES + j)), (pm, (tm, LANES), lambda i, j: (i, C_FK // LANES + j)),
            _whole(gq), _whole(gk)]


def fox_prep_fwd(name, pm, gq, gk):
    s = pm.shape[0]
    out = ((s, BRANCH), BF16, (512, LANES), lambda i, j: (i, j), ())
    return tile_fwd(name, _fox_prep_fn, (s // 512, 4), _fox_prep_ops(pm, gq, gk), [out, out])


def fox_prep_bwd(name, pm, gq, gk, dqn, dkn):
    s = pm.shape[0]
    cot = lambda g: (g, (512, LANES), lambda i, j: (i, j))
    own = ((s, BRANCH), (512, LANES), lambda i, j: (i, j))
    return tile_bwd(name, _fox_prep_fn, (s // 512, 4), _fox_prep_ops(pm, gq, gk), [cot(dqn), cot(dkn)],
                    [(0, (), own, BF16), (1, (), own, BF16), (2, (0, 1)), (3, (0, 1))])


def _fox_gate_ops(f_t, bias):
    return [(f_t, (1,) + f_t.shape[1:], lambda h: (h, 0, 0)), (bias, (1, 1, 1), lambda h: (h, 0, 0))]


def fox_gate_fwd(name, f_t, bias):
    n_h = f_t.shape[0]
    return tile_fwd(name, _fox_gate_fn, (n_h,), _fox_gate_ops(f_t, bias),
                    [(f_t.shape, F32, (1,) + f_t.shape[1:], lambda h: (h, 0, 0), ())])[0]


def fox_gate_bwd(name, f_t, bias, dcum):
    n_h = f_t.shape[0]
    return tile_bwd(name, _fox_gate_fn, (n_h,), _fox_gate_ops(f_t, bias),
                    [(dcum, (1,) + f_t.shape[1:], lambda h: (h, 0, 0))], [(0, ()), (1, ())])


FOX_GROUPS = 4


def _fox_groups(s):
    per = s // FOX_BLOCK // FOX_GROUPS
    return [(g * per, per, (g + 1) * per * FOX_BLOCK) for g in range(FOX_GROUPS)]


def _fox_attn_ops(qn, kn, pm, cum_c, cum_r, q0, keys):
    nb = FOX_BLOCK
    return [(qn, (nb, LANES), lambda p, i: (q0 + i, p)), (kn, (keys, LANES), lambda p, i: (0, p)),
            (pm, (keys, LANES), lambda p, i: (0, C_FV // LANES + p)),
            (cum_c, (1, nb, 1), lambda p, i: (2 * p, q0 + i, 0)), (cum_c, (1, nb, 1), lambda p, i: (2 * p + 1, q0 + i, 0)),
            (cum_r, (1, 1, keys), lambda p, i: (2 * p, 0, 0)), (cum_r, (1, 1, keys), lambda p, i: (2 * p + 1, 0, 0))]


def fox_attn_fwd(name, qn, kn, pm, cum_c, cum_r):
    s = qn.shape[0]
    parts = []
    for g, (q0, n_q, keys) in enumerate(_fox_groups(s)):
        parts.append(tile_fwd(f"{name}_g{g}", functools.partial(_fox_attn_fn, q0), (4, n_q), _fox_attn_ops(qn, kn, pm, cum_c, cum_r, q0, keys),
                              [((n_q * FOX_BLOCK, BRANCH), BF16, (FOX_BLOCK, LANES), lambda p, i: (i, p), ())], raw=(0, 1, 2))[0])
    return jnp.concatenate(parts, axis=0)


def fox_attn_bwd(name, qn, kn, pm, cum_c, cum_r, dy):
    s = qn.shape[0]
    groups = _fox_groups(s)
    d_qn, by_q, tails = [None] * len(groups), [None] * len(groups), [None] * len(groups)
    below = None
    for g in reversed(range(len(groups))):
        q0, n_q, keys = groups[g]
        rows = n_q * FOX_BLOCK
        own_q = ((rows, BRANCH), (FOX_BLOCK, LANES), lambda p, i: (i, p))
        own_k = ((keys, BRANCH), (keys, LANES), lambda p, i: (0, p))
        pair_c = ((4, rows, 1), (1, FOX_BLOCK, 1), lambda p, i: (p, i, 0))
        pair_r = ((4, 1, keys), (1, 1, keys), lambda p, i: (p, 0, 0))
        adds = {}
        if below is not None:
            adds = {1: (below[0],) + own_k[1:], 2: (below[1],) + own_k[1:], 5: (below[2],) + pair_r[1:], 6: (below[3],) + pair_r[1:]}
        g_qn, g_kn, g_v, g_cqa, g_cqb, g_cka, g_ckb = tile_bwd(
            f"{name}_g{g}", functools.partial(_fox_attn_fn, q0), (4, n_q), _fox_attn_ops(qn, kn, pm, cum_c, cum_r, q0, keys),
            [(dy, (FOX_BLOCK, LANES), lambda p, i, q0=q0: (q0 + i, p))],
            [(0, (), own_q), (1, (1,), own_k), (2, (1,), own_k), (3, (), pair_c), (4, (), pair_c), (5, (1,), pair_r), (6, (1,), pair_r)],
            adds=adds)
        below = (g_kn, g_v, g_cka, g_ckb)
        lo = groups[g - 1][2] if g else 0
        d_qn[g] = g_qn
        by_q[g] = jnp.stack([g_cqa[:, :, 0], g_cqb[:, :, 0]], axis=1).reshape(8, rows)
        tails[g] = (g_kn[lo:], g_v[lo:], jnp.stack([g_cka[:, 0, lo:], g_ckb[:, 0, lo:]], axis=1).reshape(8, keys - lo))
    d_cum = jnp.concatenate(by_q, axis=1) + jnp.concatenate([t[2] for t in tails], axis=1)
    return jnp.concatenate(d_qn, axis=0), jnp.concatenate([t[0] for t in tails], axis=0), jnp.concatenate([t[1] for t in tails], axis=0), d_cum


def sconv_ops(pm, w):
    s = pm.shape[0]
    blk = lambda c0: (pm, (s, LANES), lambda j, c0=c0: (0, c0 // LANES + j))
    return [blk(C_SB), blk(C_SC), blk(C_SV), (w, (w.shape[0], LANES), lambda j: (0, j))]


def dnconv_ops(pm, w):
    s = pm.shape[0]
    return [(pm, (s, LANES), lambda j: (0, C_DN // LANES + j)), (w, (w.shape[0], LANES), lambda j: (0, j))]


def ffn_ops(ug, uv, w):
    s = ug.shape[0]
    n_t = D_FF // LANES
    return [(ug, (s, LANES), lambda j: (0, j)), (uv, (s, LANES), lambda j: (0, j)),
            (w, (w.shape[0], LANES), lambda j: (0, j)), (w, (w.shape[0], LANES), lambda j: (0, n_t + j))]


def _col_out(s, width, dtype=F32):
    return ((s, width), dtype, (s, LANES), lambda j: (0, j), ())


def _col_cot(g):
    return (g, (g.shape[0], LANES), lambda j: (0, j))


def merge_ops(yp, pm):
    gate = lambda b: (pm, (256, D_MODEL), lambda i, b=b: (i, C_GATE // D_MODEL + b))
    return [_rows(yp[0]), _rows(yp[1]), _rows(yp[2]), gate(0), gate(1), gate(2)]


def ple_ops(gpre, pe, x):
    return [_rows(gpre), _rows(pe), _rows(x)]


def adam_call(name, w, g, m, v):
    shape = w.shape
    last = shape[-1]
    rows = w.size // last
    flat = lambda t: t.reshape(rows, last)
    tm = rows
    for cand in (512, 256, 128, 64, 32, 16, 8):
        if rows % cand == 0 and cand * last * 4 <= 2 * 1024 * 1024:
            tm = cand
            break
    spec = lambda t: (flat(t), (tm, last), lambda i: (i, 0))
    out = ((rows, last), F32, (tm, last), lambda i: (i, 0), ())
    res = tile_fwd(name, _adam_fn, (rows // tm,), [spec(w), spec(g), spec(m), spec(v)], [out, out, out])
    return [r.reshape(shape) for r in res]


def _adam_layers_fn(d, pids, w, m, v, g0, g1):
    g = jnp.where(pids[0] == 0, g0, g1)
    return (g,) + _adam_fn(d, pids, w, g, m, v)


def adam_layers(name, w, m, v, g0, g1):
    _, rows, cols = w.shape
    tm = _row_tile(rows, cols)
    n_t = rows // tm
    lay = lambda t: (t, (1, tm, cols), lambda l, i: (l, i, 0))
    ins = [lay(w), lay(m), lay(v), (g0, (tm, cols), lambda l, i: (i * (1 - l) + (n_t - 1) * l, 0)), (g1, (tm, cols), lambda l, i: (i * l, 0))]
    out = (w.shape, F32, (1, tm, cols), lambda l, i: (l, i, 0), ())
    return tile_fwd(name, _adam_layers_fn, (2, n_t), ins, [out, out, out, out])


def adam_w_in(name, w, m, v, g0, g1):
    rows, n_l, cols = w.shape

    def body(w_ref, m_ref, v_ref, g0_ref, g1_ref, g_out, d_out, m_out, v_out):
        step = 64

        def update(at):
            g0, g1 = g0_ref[at, :], g1_ref[at, :]
            layer = _iota((g0.shape[0], n_l, LANES), 1)
            g = jnp.where(layer == 0, g0[:, None, :], g1[:, None, :])
            delta, m2, v2 = _adam_fn(False, None, w_ref[at], g, m_ref[at], v_ref[at])
            for ref, val in ((g_out, g), (d_out, delta), (m_out, m2), (v_out, v2)):
                ref[at] = val

        def some_rows(i, carry):
            update(pl.ds(pl.multiple_of(i * step, step), step))
            return carry

        lax.fori_loop(0, rows // step, some_rows, 0)
        if rows % step:
            update(pl.ds(rows - rows % step, rows % step))

    both = pl.BlockSpec((rows, n_l, LANES), lambda j: (0, 0, j))
    one = pl.BlockSpec((rows, LANES), lambda j: (0, j))
    return pl.pallas_call(
        body, grid=(cols // LANES,), in_specs=[both, both, both, one, one], out_specs=[both] * 4,
        out_shape=[jax.ShapeDtypeStruct(w.shape, F32)] * 4, name=name, compiler_params=_cparams(1),
    )(w, m, v, g0, g1)


DN_GROUP = 4


def _dn_local_specs():
    rows = DN_GROUP * DN_CHUNK
    return [pl.BlockSpec((rows, 3 * BRANCH), lambda j: (j, 0)), pl.BlockSpec((rows, LANES), lambda j: (j, 0)),
            pl.BlockSpec((DN_GROUP, DN_HEADS, DN_CHUNK), lambda j: (j, 0, 0)), pl.BlockSpec((2, DN_HEADS), lambda j: (0, 0))]


def _dn_group_inputs(qkv, ps, a_rows, c):
    lo = c * DN_CHUNK
    heads = _split_heads(qkv[lo:lo + DN_CHUNK])
    return heads[0:4], heads[4:8], heads[8:12], ps[lo:lo + DN_CHUNK], a_rows[c]


def dn_local_fwd(name, dn_act, ps, a_rows, ad):
    s = dn_act.shape[0]
    n_c, n_g = s // DN_CHUNK, s // (DN_GROUP * DN_CHUNK)
    rows = DN_GROUP * DN_CHUNK

    def body(qkv_ref, ps_ref, ar_ref, ad_ref, u_ref, kc_ref, qd_ref, kd_ref, qk_ref, gl_ref):
        qkv, ps_v, a_rows_v, ad_v = qkv_ref[...], ps_ref[...], ar_ref[...], ad_ref[...]
        args = [[] for _ in range(8)]
        for c in range(DN_GROUP):
            q4, k4, v4, ps_c, ar_c = _dn_group_inputs(qkv, ps_v, a_rows_v, c)
            for lst, vals in zip(args, (q4, k4, v4) + _dn_gates(ps_c, ar_c, ad_v)):
                lst.extend(vals)
        everything = _dn_local(False, *args)
        for c in range(DN_GROUP):
            res = everything[c * DN_HEADS:(c + 1) * DN_HEADS]
            at = pl.ds(c * DN_CHUNK, DN_CHUNK)
            for ref, i in ((u_ref, 0), (kc_ref, 1), (qd_ref, 2), (kd_ref, 3)):
                ref[at, :] = jnp.concatenate([r[i] for r in res], axis=1)
            for h in range(DN_HEADS):
                qk_ref[c, h] = res[h][4]
            gl_ref[c] = _head_rows([r[5] for r in res])

    wide = pl.BlockSpec((rows, BRANCH), lambda j: (j, 0))
    return pl.pallas_call(
        body, grid=(n_g,), in_specs=_dn_local_specs(),
        out_specs=[wide, wide, wide, wide, pl.BlockSpec((DN_GROUP, DN_HEADS, DN_CHUNK, DN_CHUNK), lambda j: (j, 0, 0, 0)),
                   pl.BlockSpec((DN_GROUP, 8, LANES), lambda j: (j, 0, 0))],
        out_shape=[jax.ShapeDtypeStruct((s, BRANCH), F32)] * 4 + [jax.ShapeDtypeStruct((n_c, DN_HEADS, DN_CHUNK, DN_CHUNK), F32),
                                                                 jax.ShapeDtypeStruct((n_c, 8, LANES), F32)],
        name=name, compiler_params=_cparams(1),
    )(dn_act, ps, a_rows, ad)


def dn_local_bwd(name, dn_act, ps, a_rows, ad, cots):
    s = dn_act.shape[0]
    n_c, n_g = s // DN_CHUNK, s // (DN_GROUP * DN_CHUNK)
    rows = DN_GROUP * DN_CHUNK

    def body(qkv_ref, ps_ref, ar_ref, ad_ref, du_ref, dkc_ref, dqd_ref, dkd_ref, dqk_ref, dgl_ref, dqkv_ref, dps_ref, dar_ref, dad_ref):
        first = pl.program_id(0) == 0
        qkv, ps_v, a_rows_v, ad_v = qkv_ref[...], ps_ref[...], ar_ref[...], ad_ref[...]
        d_wide = [r[...] for r in (du_ref, dkc_ref, dqd_ref, dkd_ref)]
        qs, ks, vs, ps_cs, ar_cs, cot = [], [], [], [], [], []
        for c in range(DN_GROUP):
            q4, k4, v4, ps_c, ar_c = _dn_group_inputs(qkv, ps_v, a_rows_v, c)
            qs, ks, vs, ps_cs, ar_cs = qs + q4, ks + k4, vs + v4, ps_cs + [ps_c], ar_cs + [ar_c]
            lo = c * DN_CHUNK
            d_tiles = [_split_heads(t[lo:lo + DN_CHUNK]) for t in d_wide]
            d_gl = dgl_ref[c]
            cot += [(d_tiles[0][h], d_tiles[1][h], d_tiles[2][h], d_tiles[3][h], dqk_ref[c, h], _col(_row(d_gl, h), 0))
                    for h in range(DN_HEADS)]

        def f(qs, ks, vs, ps_cs, ar_cs, ad_v):
            gates = [[] for _ in range(5)]
            for ps_c, ar_c in zip(ps_cs, ar_cs):
                for lst, vals in zip(gates, _dn_gates(ps_c, ar_c, ad_v)):
                    lst.extend(vals)
            return _dn_local(True, qs, ks, vs, *gates)

        _, vjp = jax.vjp(f, qs, ks, vs, ps_cs, ar_cs, ad_v)
        d_q, d_k, d_v, d_ps, d_ar, d_ad = vjp(cot)
        for c in range(DN_GROUP):
            at, hs = pl.ds(c * DN_CHUNK, DN_CHUNK), slice(c * DN_HEADS, (c + 1) * DN_HEADS)
            dqkv_ref[at, :] = jnp.concatenate(d_q[hs] + d_k[hs] + d_v[hs], axis=1).astype(dqkv_ref.dtype)
            dps_ref[at, :] = d_ps[c]
            dar_ref[c] = d_ar[c]
        _store(dad_ref, d_ad, first)

    wide = pl.BlockSpec((rows, BRANCH), lambda j: (j, 0))
    specs = _dn_local_specs()
    return pl.pallas_call(
        body, grid=(n_g,),
        in_specs=specs + [wide, wide, wide, wide, pl.BlockSpec((DN_GROUP, DN_HEADS, DN_CHUNK, DN_CHUNK), lambda j: (j, 0, 0, 0)),
                          pl.BlockSpec((DN_GROUP, 8, LANES), lambda j: (j, 0, 0))],
        out_specs=specs,
        out_shape=[jax.ShapeDtypeStruct((s, 3 * BRANCH), F32), jax.ShapeDtypeStruct((s, LANES), F32),
                   jax.ShapeDtypeStruct((n_c, DN_HEADS, DN_CHUNK), F32), jax.ShapeDtypeStruct((2, DN_HEADS), F32)],
        name=name, compiler_params=_cparams(1),
    )(dn_act, ps, a_rows, ad, *cots)


def _dn_scan_specs(n_c, rev):
    idx = (lambda j: n_c - 1 - j) if rev else (lambda j: j)
    wide = pl.BlockSpec((DN_CHUNK, BRANCH), lambda j: (idx(j), 0))
    return [wide, wide, wide, wide, pl.BlockSpec((1, DN_HEADS, DN_CHUNK, DN_CHUNK), lambda j: (idx(j), 0, 0, 0)),
            pl.BlockSpec((1, 8, LANES), lambda j: (idx(j), 0, 0)), pl.BlockSpec((DN_CHUNK, BRANCH), lambda j: (idx(j), C_DZ // BRANCH)),
            pl.BlockSpec((1, DN_DH), lambda j: (0, 0))]


def _dn_scan_tiles(refs):
    u_ref, kc_ref, qd_ref, kd_ref, qk_ref, gl_ref, z_ref, g_ref = refs
    wide = [_split_heads(r[...]) for r in (u_ref, kc_ref, qd_ref, kd_ref)]
    gl = gl_ref[0]
    return [(wide[0][h], wide[1][h], wide[2][h], wide[3][h], qk_ref[0, h], _col(_row(gl, h), 0)) for h in range(DN_HEADS)], \
        _split_heads(z_ref[...].astype(F32)), g_ref[...]


def dn_scan_fwd(name, local, pm, gain):
    s = pm.shape[0]
    n_c = s // DN_CHUNK

    def body(*refs):
        y_ref, hist_ref, state = refs[8:]

        @pl.when(pl.program_id(0) == 0)
        def _():
            state[...] = jnp.zeros_like(state)

        hist_ref[0] = state[...]
        per_head, z4, gain_v = _dn_scan_tiles(refs[:8])
        ys, s_nexts = _dn_step(False, [state[h] for h in range(DN_HEADS)], per_head, z4, gain_v)
        for h in range(DN_HEADS):
            state[h] = s_nexts[h]
        y_ref[...] = jnp.concatenate(ys, axis=1).astype(y_ref.dtype)

    return pl.pallas_call(
        body, grid=(n_c,), in_specs=_dn_scan_specs(n_c, False),
        out_specs=[pl.BlockSpec((DN_CHUNK, BRANCH), lambda j: (j, 0)),
                   pl.BlockSpec((1, DN_HEADS, DN_DH, DN_DH), lambda j: (j, 0, 0, 0))],
        out_shape=[jax.ShapeDtypeStruct((s, BRANCH), BF16), jax.ShapeDtypeStruct((n_c, DN_HEADS, DN_DH, DN_DH), F32)],
        scratch_shapes=[pltpu.VMEM((DN_HEADS, DN_DH, DN_DH), F32)],
        name=name, compiler_params=_cparams(1),
    )(*local, pm, gain)


def dn_scan_bwd(name, local, pm, gain, hist, dy):
    s = pm.shape[0]
    n_c = s // DN_CHUNK

    def body(*refs):
        hist_ref, dy_ref = refs[8:10]
        du_ref, dkc_ref, dqd_ref, dkd_ref, dqk_ref, dgl_ref, dz_ref, dg_ref, d_state = refs[10:]
        first = pl.program_id(0) == 0

        @pl.when(first)
        def _():
            d_state[...] = jnp.zeros_like(d_state)

        per_head, z4, gain_v = _dn_scan_tiles(refs[:8])
        _, vjp = jax.vjp(functools.partial(_dn_step, True), [hist_ref[0, h] for h in range(DN_HEADS)], per_head, z4, gain_v)
        d_s, grads, d_z, d_gain = vjp((_split_heads(dy_ref[...].astype(F32)), [d_state[h] for h in range(DN_HEADS)]))
        for h in range(DN_HEADS):
            d_state[h] = d_s[h]
        for ref, i in ((du_ref, 0), (dkc_ref, 1), (dqd_ref, 2), (dkd_ref, 3)):
            ref[...] = jnp.concatenate([g[i] for g in grads], axis=1)
        dz_ref[...] = jnp.concatenate(d_z, axis=1).astype(dz_ref.dtype)
        for h in range(DN_HEADS):
            dqk_ref[0, h] = grads[h][4]
        dgl_ref[0] = _head_rows([g[5] for g in grads])
        _store(dg_ref, d_gain, first)

    rev = lambda j: n_c - 1 - j
    specs = _dn_scan_specs(n_c, True)
    return pl.pallas_call(
        body, grid=(n_c,),
        in_specs=specs + [pl.BlockSpec((1, DN_HEADS, DN_DH, DN_DH), lambda j: (rev(j), 0, 0, 0)),
                          pl.BlockSpec((DN_CHUNK, BRANCH), lambda j: (rev(j), 0))],
        out_specs=specs[:6] + [pl.BlockSpec((DN_CHUNK, BRANCH), lambda j: (rev(j), 0)), specs[7]],
        out_shape=[jax.ShapeDtypeStruct((s, BRANCH), F32)] * 4 + [
            jax.ShapeDtypeStruct((n_c, DN_HEADS, DN_CHUNK, DN_CHUNK), F32), jax.ShapeDtypeStruct((n_c, 8, LANES), F32),
            jax.ShapeDtypeStruct((s, BRANCH), BF16), jax.ShapeDtypeStruct((1, DN_DH), F32)],
        scratch_shapes=[pltpu.VMEM((DN_HEADS, DN_DH, DN_DH), F32)],
        name=name, compiler_params=_cparams(1),
    )(*local, pm, gain, hist, dy)


def _seq_layouts(cols, s):
    return cols.T.reshape(cols.shape[1], s // LANES, LANES)


def layer_fwd(li, x, p, w, more_weights=None):
    s = x.shape[0]
    n = lambda t: f"{t}_l{li}"
    h = rms_fwd(n("rms_mix"), x, w["g_mix"])
    pm = mm(n("in_main"), h, w["in_main"], "nn")
    ps = mm(n("in_small"), h, w["in_small"], "nn")
    qn, kn = fox_prep_fwd(n("fox_prep"), pm, w["gq"], w["gk"])
    f_t = _seq_layouts(ps[:, 0:8], s)
    cum = fox_gate_fwd(n("fox_gate"), f_t, w["b_f"])
    cum_c, cum_r = cum.reshape(8, s, 1), cum.reshape(8, 1, s)
    y_fox = fox_attn_fwd(n("fox_attn"), qn, kn, pm, cum_c, cum_r)
    y_sc = tile_fwd(n("sconv"), _sconv_fn, (BRANCH // LANES,), sconv_ops(pm, w["sc_conv_w"]), [_col_out(s, BRANCH, BF16)])[0]
    dn_act = tile_fwd(n("dnconv"), _dnconv_fn, (3 * BRANCH // LANES,), dnconv_ops(pm, w["dn_conv_w"]), [_col_out(s, 3 * BRANCH)])[0]
    a_rows = ps[:, 12:16].reshape(s // DN_CHUNK, DN_CHUNK, DN_HEADS).transpose(0, 2, 1)
    dn_local = dn_local_fwd(n("dn_local"), dn_act, ps, a_rows, w["ad"])
    y_dn, hist = dn_scan_fwd(n("dn_scan"), dn_local, pm, w["dn_gain"])
    ys = (y_fox, y_sc, y_dn)
    if more_weights is not None:
        w = {**w, **more_weights(y_dn)}
    yp = [mm(n(f"branch{b}"), ys[b], w["branch"][b], "nn", blocks=(0, N_CHIPS)) for b in range(3)]
    merged = tile_fwd(n("merge"), _merge_fn, (s // 256,), merge_ops(yp, pm), [((s, D_MODEL), BF16, (256, D_MODEL), lambda i: (i, 0), ())])[0]
    x1 = mm(n("w_o"), merged, w["o"], "nn", add=x)
    h2 = rms_fwd(n("rms_ffn"), x1, w["g_ffn"])
    ug = mm(n("up_g"), h2, w["up"], "nn", blocks=(0, 2))
    uv = mm(n("up_v"), h2, w["up"], "nn", blocks=(2, 2))
    act = tile_fwd(n("ffn_act"), _ffn_act_fn, (D_FF // LANES,), ffn_ops(ug, uv, w["ffn_conv_w"]), [_col_out(s, D_FF, BF16)])[0]
    x2 = mm(n("down"), act, w["down"], "nn", add=x1)
    h3 = rms_fwd(n("rms_ple"), x2, w["g_ple"])
    gpre = mm(n("ple_gate"), h3, w["pg"], "nn")
    pe = mm(n("ple_emb"), p, w["ple"], "nn", blocks=(0, N_CHIPS))
    x3 = tile_fwd(n("ple"), _ple_fn, (s // 256,), ple_ops(gpre, pe, x2), [((s, D_MODEL), F32, (256, D_MODEL), lambda i: (i, 0), ())])[0]
    saved = dict(x=x, h=h, pm=pm, ps=ps, qn=qn, kn=kn, f_t=f_t, cum_c=cum_c, cum_r=cum_r, ys=ys, dn_act=dn_act, dn_local=dn_local,
                 a_rows=a_rows, hist=hist, yp=yp, merged=merged, x1=x1, h2=h2, ug=ug, uv=uv, act=act, x2=x2, h3=h3,
                 gpre=gpre, pe=pe, p=p)
    return x3, saved, w


def hang_on(w, token):
    zero = token[0, 0]
    small = ("g_mix", "g_ffn", "g_ple", "gq", "gk", "b_f", "ad", "dn_gain", "sc_conv_w", "dn_conv_w", "ffn_conv_w")
    return {**w, **{k: w[k] + zero for k in small}}


def layer_bwd(li, dx3, sv, w, hooks=None):
    hooks = hooks or {}

    def stage(key, after, w):
        return hang_on(w, hooks[key](after, g)) if key in hooks else w

    s = dx3.shape[0]
    n = lambda t: f"{t}_l{li}"
    g = {}
    col_own = lambda width: ((s, width), (s, LANES), lambda j: (0, j))
    d_gpre, d_pe = tile_bwd(n("ple_bwd"), _ple_fn, (s // 256,), ple_ops(sv["gpre"], sv["pe"], sv["x2"]), [_rows(dx3)],
                            [(0, (), None, BF16), (1, (), None, BF16)])
    g["w_ple"] = mm(n("d_w_ple"), sv["p"], d_pe, "tn", blocks=(0, N_CHIPS))
    g["w_ple_gate"] = mm(n("d_w_pg"), sv["h3"], d_gpre, "tn").reshape(N_CHIPS, -1, D_MODEL)
    dh3 = mm(n("d_h3"), d_gpre, w["pg"], "nt")
    dx2, d_g_ple = rms_bwd(n("rms_ple_bwd"), sv["x2"], w["g_ple"], dh3, dx3)
    dact = mm(n("d_act"), dx2, w["down"], "nt")
    g["w_down"] = mm(n("d_w_down"), sv["act"], dx2, "tn").reshape(N_CHIPS, -1, D_MODEL)
    taps_own = ((w["ffn_conv_w"].shape[0], D_FF), (w["ffn_conv_w"].shape[0], LANES), lambda j: (0, j))
    d_ug, d_uv, d_fw_g, d_fw_v = tile_bwd(n("ffn_act_bwd"), _ffn_act_fn, (D_FF // LANES,), ffn_ops(sv["ug"], sv["uv"], w["ffn_conv_w"]),
                                          [_col_cot(dact)], [(0, (), None, BF16), (1, (), None, BF16), (2, (), taps_own), (3, (), taps_own)])
    g["ffn_conv_w"] = jnp.concatenate([d_fw_g, d_fw_v], axis=1)
    gate_half = mm(n("d_w_up_g"), sv["h2"], d_ug, "tn", blocks=(0, 2), into=(N_CHIPS, 0))
    g["w_up"] = mm(n("d_w_up_v"), sv["h2"], d_uv, "tn", blocks=(0, 2), into=(N_CHIPS, 2, gate_half))
    dh2 = mm(n("d_h2_v"), d_uv, w["up"], "nt", blocks=(2, 2), add=mm(n("d_h2_g"), d_ug, w["up"], "nt", blocks=(0, 2)))
    dx1, d_g_ffn = rms_bwd(n("rms_ffn_bwd"), sv["x1"], w["g_ffn"], dh2, dx2)
    w = stage("mid", dx1, w)
    dmerged = mm(n("d_merged"), dx1, w["o"], "nt")
    g["w_o"] = mm(n("d_w_o"), sv["merged"], dx1, "tn").reshape(N_CHIPS, -1, D_MODEL)
    gate_own = ((s, D_MODEL), (256, D_MODEL), lambda i: (i, 0))
    d_yp0, d_yp1, d_yp2, d_g0, d_g1, d_g2 = tile_bwd(
        n("merge_bwd"), _merge_fn, (s // 256,), merge_ops(sv["yp"], sv["pm"]), [_rows(dmerged)],
        [(0, (), None, BF16), (1, (), None, BF16), (2, (), None, BF16), (3, (), gate_own, BF16), (4, (), gate_own, BF16), (5, (), gate_own, BF16)])
    d_yp = (d_yp0, d_yp1, d_yp2)
    g["w_branch"] = jnp.concatenate([mm(n(f"d_w_branch{b}"), sv["ys"][b], d_yp[b], "tn", blocks=(0, N_CHIPS)) for b in range(3)], axis=1)
    d_ys = [mm(n(f"d_y{b}"), d_yp[b], w["branch"][b], "nt", blocks=(0, N_CHIPS)) for b in range(3)]
    w = stage("late", d_ys[2], w)
    *d_local, d_z, d_dngain = dn_scan_bwd(n("dn_scan_bwd"), sv["dn_local"], sv["pm"], w["dn_gain"], sv["hist"], d_ys[2])
    d_dnact, d_ps_dn, d_arows, d_ad = dn_local_bwd(n("dn_local_bwd"), sv["dn_act"], sv["ps"], sv["a_rows"], w["ad"], d_local)
    g["ad"], g["dn_norm_gain"] = d_ad, d_dngain[0]
    d_dnqkv, g["dn_conv_w"] = tile_bwd(n("dnconv_bwd"), _dnconv_fn, (3 * BRANCH // LANES,), dnconv_ops(sv["pm"], w["dn_conv_w"]),
                                       [_col_cot(d_dnact)], [(0, (), col_own(3 * BRANCH), BF16), (1, ())])
    d_sb, d_sc, d_sv, g["sc_conv_w"] = tile_bwd(n("sconv_bwd"), _sconv_fn, (BRANCH // LANES,), sconv_ops(sv["pm"], w["sc_conv_w"]), [_col_cot(d_ys[1])],
                                                [(0, (), col_own(BRANCH), BF16), (1, (), col_own(BRANCH), BF16), (2, (), col_own(BRANCH), BF16), (3, ())])
    w = stage("last", d_dnqkv, w)
    d_qn, d_kn, d_fv, d_cum = fox_attn_bwd(n("fox_attn_bwd"), sv["qn"], sv["kn"], sv["pm"], sv["cum_c"], sv["cum_r"], d_ys[0])
    d_ft, d_bf = fox_gate_bwd(n("fox_gate_bwd"), sv["f_t"], w["b_f"], d_cum.reshape(8, s // LANES, LANES))
    g["b_fox_f"] = d_bf.reshape(8)
    d_fq, d_fk, d_gq, d_gk = fox_prep_bwd(n("fox_prep_bwd"), sv["pm"], w["gq"], w["gk"], d_qn, d_kn)
    g["fox_q_gain"] = d_gq[0, :FOX_DH] + d_gq[0, FOX_DH:]
    g["fox_k_gain"] = d_gk[0, :FOX_DH] + d_gk[0, FOX_DH:]
    d_pm = jnp.concatenate([d_fq, d_fk, d_fv.astype(BF16), d_sb, d_sc, d_sv, d_dnqkv, d_z, d_g0, d_g1, d_g2], axis=1)
    d_a_cols = d_arows.transpose(0, 2, 1).reshape(s, DN_HEADS)
    d_f_cols = d_ft.reshape(8, s).T
    d_ps = d_ps_dn + jnp.concatenate([d_f_cols, jnp.zeros((s, 4), F32), d_a_cols, jnp.zeros((s, LANES - 16), F32)], axis=1)
    g["w_in"] = chip_blocks_w_in(mm(n("d_w_in_main"), d_pm, sv["h"], "tn"), mm(n("d_w_in_small"), d_ps, sv["h"], "tn"))
    w = stage("w_in", g["w_in"], w)
    dh = mm(n("d_h_small"), d_ps, w["in_small"], "nt", add=mm(n("d_h_main"), d_pm, w["in_main"], "nt"))
    dx, d_g_mix = rms_bwd(n("rms_mix_bwd"), sv["x"], w["g_mix"], dh, dx1)
    g["g_mix"], g["g_ffn"], g["g_ple"] = d_g_mix[0], d_g_ffn[0], d_g_ple[0]
    return dx, g


IN_SHARD = 2052
MAIN_RANGES = ((0, 1536), (1544, 3080), (3080, 4616), (4624, 5136), (5136, 8208))
SMALL_RANGES = ((1536, 1544), (4616, 4620), (4620, 4624))


def _from_chip_blocks(blocks, ranges):
    parts = []
    for lo, hi in ranges:
        for k in range(N_CHIPS):
            a0, a1 = max(lo, k * IN_SHARD), min(hi, (k + 1) * IN_SHARD)
            if a0 < a1:
                parts.append(blocks[k][:, a0 - k * IN_SHARD:a1 - k * IN_SHARD])
    return parts


def split_w_in(blocks):
    main = jnp.concatenate(_from_chip_blocks(blocks, MAIN_RANGES), axis=1)
    pad = jnp.zeros((blocks.shape[1], LANES - 16), blocks.dtype)
    return main, jnp.concatenate(_from_chip_blocks(blocks, SMALL_RANGES) + [pad], axis=1)


def chip_blocks_w_in(main, small):
    ranges = sorted([(lo, hi, "m") for lo, hi in MAIN_RANGES] + [(lo, hi, "s") for lo, hi in SMALL_RANGES])
    offs, m_off, s_off = {}, 0, 0
    for lo, hi in MAIN_RANGES:
        offs[lo] = m_off
        m_off += hi - lo
    for lo, hi in SMALL_RANGES:
        offs[lo] = s_off
        s_off += hi - lo
    blocks = []
    for k in range(N_CHIPS):
        parts = []
        for lo, hi, src in ranges:
            a0, a1 = max(lo, k * IN_SHARD), min(hi, (k + 1) * IN_SHARD)
            if a0 < a1:
                arr = main if src == "m" else small
                parts.append(arr[offs[lo] + a0 - lo:offs[lo] + a1 - lo])
        blocks.append(jnp.concatenate(parts, axis=0))
    return jnp.stack(blocks)


def later_weights(got):
    g_branch, g_o, g_up, g_down, g_pg, g_ple = got
    branch = g_branch.reshape(N_CHIPS, 3, BRANCH, -1)
    return dict(branch=[branch[:, b] for b in range(3)], o=g_o.reshape(D_MODEL, D_MODEL), up=g_up,
                down=g_down.reshape(D_FF, D_MODEL), pg=g_pg.reshape(D_MODEL, D_MODEL), ple=g_ple)


def layer_weights(li, got, conv, a):
    main, small = split_w_in(got[0])
    tile2 = lambda v: jnp.concatenate([v, v])[None, :]
    rest = later_weights(got[1:]) if len(got) > 1 else {}
    return dict(
        in_main=main, in_small=small, **rest,
        g_mix=a["g_mix"][li][None, :], g_ffn=a["g_ffn"][li][None, :], g_ple=a["g_ple"][li][None, :],
        gq=tile2(a["fox_q_gain"][li]), gk=tile2(a["fox_k_gain"][li]), b_f=a["b_fox_f"][li].reshape(8, 1, 1),
        ad=jnp.stack([a["dn_a_log"][li], a["dn_dt_bias"][li]]), dn_gain=a["dn_norm_gain"][li][None, :],
        sc_conv_w=conv["sc_conv_w"][li], dn_conv_w=conv["dn_conv_w"][li], ffn_conv_w=conv["ffn_conv_w"][li])


def pack_rows(arrs, dtype):
    flat = jnp.concatenate([t.reshape(-1).astype(dtype) for t in arrs])
    pad = (-flat.shape[0]) % (8 * LANES)
    if pad:
        flat = jnp.concatenate([flat, jnp.zeros((pad,), dtype)])
    return flat.reshape(-1, LANES)


def unpack_rows(buf, shapes):
    flat = buf.reshape(-1)
    out, off = [], 0
    for shp in shapes:
        size = 1
        for dim in shp:
            size *= dim
        out.append(flat[off:off + size].reshape(shp))
        off += size
    return out


ANY = pl.BlockSpec(memory_space=pl.ANY)


def _position():
    x, y, c = lax.axis_index("x"), lax.axis_index("y"), lax.axis_index("c")
    return x, y, c, [(1 - x, y), (x, 1 - y), (1 - x, 1 - y)]


def gather_small(name, block):
    m_per, n = block.shape

    def body(x_ref, out_ref, token, send_sems, recv_sems, local_sem):
        token[...] = jnp.zeros_like(token)
        x, y, c, chips = _position()
        me, sibling = (x, y, c), (x, y, 1 - c)

        def rows(px, py, pc):
            return out_ref.at[pl.ds((4 * px + 2 * py + pc) * m_per, m_per), :]

        def copy(k, blk, to, src=None):
            return pltpu.make_async_remote_copy(src_ref=rows(*blk) if src is None else src, dst_ref=rows(*blk),
                                                send_sem=send_sems.at[k], recv_sem=recv_sems.at[k], device_id=to, device_id_type=MESH)

        mine = pltpu.make_async_copy(x_ref, rows(*me), local_sem)
        mine.start()
        first = [copy(0, me, sibling, src=x_ref)] + [copy(1 + j, me, (*chip, c), src=x_ref) for j, chip in enumerate(chips)]
        for cp in first:
            cp.start()
        passed = [copy(4 + j, (*chip, c), sibling) for j, chip in enumerate(chips)]
        for j, chip in enumerate(chips):
            copy(1 + j, (*chip, c), me).wait_recv()
            passed[j].start()
        copy(0, sibling, me).wait_recv()
        for j, chip in enumerate(chips):
            copy(4 + j, (*chip, 1 - c), me).wait_recv()
        for cp in first + passed:
            cp.wait_send()
        mine.wait()

    in_vmem = pl.BlockSpec(memory_space=pltpu.VMEM)
    return pl.pallas_call(
        body, out_shape=[jax.ShapeDtypeStruct((8 * m_per, n), block.dtype), jax.ShapeDtypeStruct((8, LANES), F32)],
        in_specs=[in_vmem], out_specs=[in_vmem, in_vmem],
        scratch_shapes=[pltpu.SemaphoreType.DMA((7,)), pltpu.SemaphoreType.DMA((7,)), pltpu.SemaphoreType.DMA],
        name=name, compiler_params=pltpu.CompilerParams(vmem_limit_bytes=VMEM_LIMIT),
    )(block)


def _sems(n):
    return [pltpu.SemaphoreType.DMA((n,)), pltpu.SemaphoreType.DMA((n,))]


def _split_cols(rows):
    return (rows // 2) % 16 != 0


def _half(ref, which, lead=()):
    rows, cols = ref.shape[-2:]
    if _split_cols(rows):
        return ref.at[(*lead, slice(None), pl.ds(which * (cols // 2), cols // 2))]
    return ref.at[(*lead, pl.ds(which * (rows // 2), rows // 2), slice(None))]


def _half_shape(rows, cols):
    return (rows, cols // 2) if _split_cols(rows) else (rows // 2, cols)


def forward_halves(name, lands):
    n_w = len(lands)

    def body(*refs):
        outs = refs[n_w:2 * n_w]
        send_sems, recv_sems = refs[2 * n_w:]
        x, y, c, chips = _position()

        def copy(w, j, pc):
            cx, cy = chips[j]
            part = _half(outs[w], pc, (2 * cx + cy,))
            return pltpu.make_async_remote_copy(src_ref=part, dst_ref=part, send_sem=send_sems.at[3 * w + j], recv_sem=recv_sems.at[3 * w + j],
                                                device_id=(x, y, 1 - c), device_id_type=MESH)

        pairs = [(w, j) for w in range(n_w) for j in range(3)]
        for w, j in pairs:
            copy(w, j, c).start()
        for w, j in pairs:
            copy(w, j, 1 - c).wait_recv()
            copy(w, j, c).wait_send()

    return pl.pallas_call(
        body, out_shape=[jax.ShapeDtypeStruct(t.shape, t.dtype) for t in lands], in_specs=[ANY] * n_w, out_specs=[ANY] * n_w,
        input_output_aliases={w: w for w in range(n_w)}, scratch_shapes=_sems(3 * n_w), name=name,
    )(*lands)


def share_halves(name, bufs):
    n_w = len(bufs)

    def body(*refs):
        outs = refs[n_w:2 * n_w]
        send_sems, recv_sems = refs[2 * n_w:]
        x, y, c, _ = _position()

        def copy(w, pc):
            half = _half(outs[w], pc)
            return pltpu.make_async_remote_copy(src_ref=half, dst_ref=half, send_sem=send_sems.at[w], recv_sem=recv_sems.at[w],
                                                device_id=(x, y, 1 - c), device_id_type=MESH)

        for w in range(n_w):
            copy(w, c).start()
        for w in range(n_w):
            copy(w, 1 - c).wait_recv()
            copy(w, c).wait_send()

    return pl.pallas_call(
        body, out_shape=[jax.ShapeDtypeStruct(b.shape, b.dtype) for b in bufs], in_specs=[ANY] * n_w, out_specs=[ANY] * n_w,
        input_output_aliases={w: w for w in range(n_w)}, scratch_shapes=_sems(n_w), name=name,
    )(*bufs)


HBM = pl.BlockSpec(memory_space=pltpu.HBM)
SEM = pl.BlockSpec(memory_space=pltpu.SEMAPHORE)
EFFECT = pltpu.SideEffectType.DATAFLOW_SIDE_EFFECTING


def _exchange_copies(kind, srcs, lands):
    x, y, c, chips = _position()
    out = []
    for src, land in zip(srcs, lands):
        if kind == "swap":
            out.append((_half(src, 1 - c, (slice(None),)), land, (x, y, 1 - c)))
            continue
        for j, (cx, cy) in enumerate(chips):
            if kind == "gather":
                out.append((src, land.at[2 * x + y], (cx, cy, c)))
            elif kind == "gather_half":
                out.append((_half(src, c), _half(land, c, (2 * x + y,)), (cx, cy, c)))
            else:
                out.append((src.at[2 * cx + cy], land.at[j], (cx, cy, c)))
    return out


def _land_shapes(kind, srcs):
    if kind in ("gather", "gather_half"):
        return [(N_CHIPS,) + s.shape for s in srcs]
    if kind == "swap":
        return [(N_CHIPS,) + _half_shape(*s.shape[1:]) for s in srcs]
    return [(3,) + s.shape[1:] for s in srcs]


def exchange_start(name, kind, srcs):
    n_w = len(srcs)
    shapes = _land_shapes(kind, srcs)
    n_sem = n_w if kind == "swap" else 3 * n_w

    def body(*refs):
        ins, lands = refs[:n_w], refs[n_w:2 * n_w]
        send_sems, recv_sems = refs[2 * n_w:2 * n_w + 2]
        token = refs[-1]
        for i, (src, dst, dev) in enumerate(_exchange_copies(kind, ins, lands)):
            pltpu.make_async_remote_copy(src_ref=src, dst_ref=dst, send_sem=send_sems.at[i], recv_sem=recv_sems.at[i],
                                         device_id=dev, device_id_type=MESH).start()
        token[...] = jnp.zeros_like(token)

    out = pl.pallas_call(
        body, name=name,
        out_shape=(pltpu.SemaphoreType.DMA((n_sem,)), pltpu.SemaphoreType.DMA((n_sem,)),
                   *[pltpu.HBM(s.shape, s.dtype) for s in srcs], *[pltpu.HBM(shp, s.dtype) for shp, s in zip(shapes, srcs)],
                   jax.ShapeDtypeStruct((8, LANES), F32)),
        in_specs=(HBM,) * (2 * n_w), out_specs=(SEM, SEM) + (HBM,) * (2 * n_w) + (pl.BlockSpec(memory_space=pltpu.VMEM),),
        input_output_aliases={i: 2 + i for i in range(2 * n_w)},
        compiler_params=pltpu.CompilerParams(has_side_effects=EFFECT),
    )(*[pltpu.with_memory_space_constraint(s, pltpu.HBM) for s in srcs],
      *[pltpu.with_memory_space_constraint(lax.empty(shp, s.dtype), pltpu.HBM) for shp, s in zip(shapes, srcs)])
    return (kind, n_w, out[:-1]), out[-1]


def exchange_wait(name, handle, after):
    kind, n_w, (send_sems, recv_sems, *thru) = handle

    def body(*refs):
        ins, lands = refs[:n_w], refs[n_w:2 * n_w]
        send_sems, recv_sems = refs[2 * n_w:2 * n_w + 2]
        for i, (src, dst, dev) in enumerate(_exchange_copies(kind, ins, lands)):
            cp = pltpu.make_async_remote_copy(src_ref=src, dst_ref=dst, send_sem=send_sems.at[i], recv_sem=recv_sems.at[i],
                                              device_id=dev, device_id_type=MESH)
            cp.wait_send()
            cp.wait_recv()

    out = pl.pallas_call(
        body, name=name, out_shape=tuple(pltpu.HBM(t.shape, t.dtype) for t in thru),
        in_specs=(HBM,) * (2 * n_w) + (SEM, SEM, pl.BlockSpec(memory_space=pl.ANY)), out_specs=(HBM,) * (2 * n_w),
        input_output_aliases={i: i for i in range(2 * n_w)},
        compiler_params=pltpu.CompilerParams(has_side_effects=EFFECT),
    )(*thru, send_sems, recv_sems, after)
    return list(out[:n_w]), list(out[n_w:])


def _row_tile(rows, cols):
    best = rows
    if rows * cols * 4 <= 2 * 1024 * 1024:
        return rows
    for t in range(16, rows, 16):
        if rows % t == 0 and t * cols * 4 <= 2 * 1024 * 1024:
            best = t
    return best


def pair_sum(name, pos, grad, from_sibling):
    _, rows, cols = grad.shape
    h_rows, h_cols = _half_shape(rows, cols)
    tr = _row_tile(h_rows, h_cols)
    n_t = h_rows // tr

    def body(pos_ref, g_ref, s_ref, b_ref, f_ref):
        tot = g_ref[...] + s_ref[...]
        b_ref[...] = tot.astype(BF16)

        @pl.when(pl.program_id(1) == pos_ref[1])
        def _():
            f_ref[...] = tot[0]

    blk = pl.BlockSpec((1, tr, h_cols), lambda i, k, pos: (k, i, 0))
    if _split_cols(rows):
        mine = pl.BlockSpec((1, tr, h_cols), lambda i, k, pos: (k, i, pos[0]))
    else:
        mine = pl.BlockSpec((1, tr, h_cols), lambda i, k, pos: (k, pos[0] * n_t + i, 0))
    return pl.pallas_call(
        body, grid_spec=pltpu.PrefetchScalarGridSpec(
            num_scalar_prefetch=1, grid=(n_t, N_CHIPS), in_specs=[mine, blk],
            out_specs=[blk, pl.BlockSpec((tr, h_cols), lambda i, k, pos: (i, 0))]),
        out_shape=[jax.ShapeDtypeStruct((N_CHIPS, h_rows, h_cols), BF16), jax.ShapeDtypeStruct((h_rows, h_cols), F32)],
        name=name, compiler_params=_cparams(2),
    )(pos, grad, from_sibling)


def chip_sum(name, pos, own, landed, split_cols):
    half, cols = own.shape
    tr = _row_tile(half, cols)
    n_t = half // tr

    def body(pos_ref, p_ref, l_ref, o_ref):
        o_ref[...] = ((p_ref[...] + l_ref[0].astype(F32)) + l_ref[1].astype(F32)) + l_ref[2].astype(F32)

    if split_cols:
        out_spec, out_shape = pl.BlockSpec((tr, cols), lambda i, pos: (i, pos[0])), (half, 2 * cols)
    else:
        out_spec, out_shape = pl.BlockSpec((tr, cols), lambda i, pos: (pos[0] * n_t + i, 0)), (2 * half, cols)
    return pl.pallas_call(
        body, grid_spec=pltpu.PrefetchScalarGridSpec(
            num_scalar_prefetch=1, grid=(n_t,),
            in_specs=[pl.BlockSpec((tr, cols), lambda i, pos: (i, 0)), pl.BlockSpec((3, tr, cols), lambda i, pos: (0, i, 0))],
            out_specs=out_spec),
        out_shape=jax.ShapeDtypeStruct(out_shape, F32), name=name, compiler_params=_cparams(1),
    )(pos, own, landed)


class OverlappedReduceScatter:
    def __init__(self, tag, pos, grads):
        self.n = lambda t: f"{t}_{tag}"
        self.pos, self.grads = pos, grads
        self.swap, self.token = exchange_start(self.n("swap_start"), "swap", grads)

    def middle(self, after):
        self.grads, from_sibling = exchange_wait(self.n("swap_wait"), self.swap, after)
        self.sums = [pair_sum(self.n(f"pair_sum{w}"), self.pos, g, s) for w, (g, s) in enumerate(zip(self.grads, from_sibling))]
        self.scatter, self.token = exchange_start(self.n("scatter_start"), "scatter", [b for b, _ in self.sums])

    def finish(self, after):
        _, landed = exchange_wait(self.n("scatter_wait"), self.scatter, after)
        halves = [chip_sum(self.n(f"chip_sum{w}"), self.pos, own, l, _split_cols(g.shape[1]))
                  for w, ((_, own), l, g) in enumerate(zip(self.sums, landed, self.grads))]
        return share_halves(self.n("share_halves"), halves)


def sum_devices(gathered):
    m_per = gathered.shape[0] // 8

    def body(g_ref, o_ref):
        tot = g_ref[pl.ds(0, m_per), :]
        for dev in range(1, 8):
            tot = tot + g_ref[pl.ds(dev * m_per, m_per), :]
        o_ref[...] = tot

    return pl.pallas_call(
        body, out_shape=jax.ShapeDtypeStruct((m_per, gathered.shape[1]), F32),
        in_specs=[pl.BlockSpec(memory_space=pltpu.VMEM)], out_specs=pl.BlockSpec(memory_space=pltpu.VMEM), name="sum_devices",
    )(gathered)


def kernel(x, p, g_mix, w_in, b_fox_f, fox_q_gain, fox_k_gain, sc_conv_w, dn_conv_w, dn_a_log, dn_dt_bias, dn_norm_gain, w_branch, w_o, g_ffn, w_up, ffn_conv_w, w_down, g_ple, w_ple_gate, w_ple, loss_target, m_g_mix, m_w_in, m_b_fox_f, m_fox_q_gain, m_fox_k_gain, m_sc_conv_w, m_dn_conv_w, m_dn_a_log, m_dn_dt_bias, m_dn_norm_gain, m_w_branch, m_w_o, m_g_ffn, m_w_up, m_ffn_conv_w, m_w_down, m_g_ple, m_w_ple_gate, m_w_ple, v_g_mix, v_w_in, v_b_fox_f, v_fox_q_gain, v_fox_k_gain, v_sc_conv_w, v_dn_conv_w, v_dn_a_log, v_dn_dt_bias, v_dn_norm_gain, v_w_branch, v_w_o, v_g_ffn, v_w_up, v_ffn_conv_w, v_w_down, v_g_ple, v_w_ple_gate, v_w_ple):
    a = dict(g_mix=g_mix, w_in=w_in, b_fox_f=b_fox_f, fox_q_gain=fox_q_gain, fox_k_gain=fox_k_gain, sc_conv_w=sc_conv_w,
             dn_conv_w=dn_conv_w, dn_a_log=dn_a_log, dn_dt_bias=dn_dt_bias, dn_norm_gain=dn_norm_gain, w_branch=w_branch, w_o=w_o,
             g_ffn=g_ffn, w_up=w_up, ffn_conv_w=ffn_conv_w, w_down=w_down, g_ple=g_ple, w_ple_gate=w_ple_gate, w_ple=w_ple)
    mom = dict(g_mix=m_g_mix, w_in=m_w_in, b_fox_f=m_b_fox_f, fox_q_gain=m_fox_q_gain, fox_k_gain=m_fox_k_gain, sc_conv_w=m_sc_conv_w,
               dn_conv_w=m_dn_conv_w, dn_a_log=m_dn_a_log, dn_dt_bias=m_dn_dt_bias, dn_norm_gain=m_dn_norm_gain, w_branch=m_w_branch,
               w_o=m_w_o, g_ffn=m_g_ffn, w_up=m_w_up, ffn_conv_w=m_ffn_conv_w, w_down=m_w_down, g_ple=m_g_ple, w_ple_gate=m_w_ple_gate,
               w_ple=m_w_ple)
    var = dict(g_mix=v_g_mix, w_in=v_w_in, b_fox_f=v_b_fox_f, fox_q_gain=v_fox_q_gain, fox_k_gain=v_fox_k_gain, sc_conv_w=v_sc_conv_w,
               dn_conv_w=v_dn_conv_w, dn_a_log=v_dn_a_log, dn_dt_bias=v_dn_dt_bias, dn_norm_gain=v_dn_norm_gain, w_branch=v_w_branch,
               w_o=v_w_o, g_ffn=v_g_ffn, w_up=v_w_up, ffn_conv_w=v_ffn_conv_w, w_down=v_w_down, g_ple=v_g_ple, w_ple_gate=v_w_ple_gate,
               w_ple=v_w_ple)
    cx, cy, cc = lax.axis_index("x"), lax.axis_index("y"), lax.axis_index("c")
    chip = 2 * cx + cy
    pos = jnp.stack([cc, chip]).astype(jnp.int32)

    def as_blocks(t):
        return t.reshape(2, -1, t.shape[-1])

    def own_block_in(got, shards):
        return [lax.dynamic_update_slice(g, s[None], (chip, 0, 0)) for g, s in zip(got, shards)]

    conv_shapes = [a[nm].shape for nm in CONVS]
    conv_all, conv_token = gather_small("gather_conv_w", pack_rows([a[nm] for nm in CONVS], F32))
    w_in0 = [(as_blocks(a["w_in"])[0] + conv_token[0, 0]).astype(BF16)]
    gather_in0, gather_in0_token = exchange_start("gather_start_w_in_l0", "gather_half", w_in0)
    shards0 = w_in0 + [(as_blocks(a[nm])[0] + gather_in0_token[0, 0]).astype(BF16) for nm in BIG[1:]]
    gather0, gather0_token = exchange_start("gather_start_l0", "gather", shards0[1:])
    shards1 = [(as_blocks(a[nm])[1] + gather0_token[0, 0]).astype(BF16) for nm in BIG]
    gather1, gather1_in_token = exchange_start("gather_start_w_in_l1", "gather", shards1[:1])
    shards1[1:] = [s + gather1_in_token[0, 0].astype(BF16) for s in shards1[1:]]
    gather1_rest, gather1_token = exchange_start("gather_start_l1", "gather", shards1[1:])
    conv_rows = conv_all.shape[0] // 8
    conv_chip = [unpack_rows(conv_all[2 * k * conv_rows:(2 * k + 1) * conv_rows], conv_shapes) for k in range(N_CHIPS)]
    conv = {nm: jnp.concatenate([conv_chip[k][i] for k in range(N_CHIPS)], axis=2) for i, nm in enumerate(CONVS)}

    weights, saved = [None, None], [None, None]
    mine_in0, got_in0 = exchange_wait("gather_wait_w_in_l0", gather_in0, gather1_token)
    got_in0 = forward_halves("forward_w_in_l0", got_in0)
    first_weights = hang_on(layer_weights(0, own_block_in(got_in0, mine_in0), conv, a), gather1_token)

    def rest_of_layer0(after):
        mine, got = exchange_wait("gather_wait_l0", gather0, after)
        return later_weights(own_block_in(got, mine))

    act, saved[0], weights[0] = layer_fwd(0, x[0], p[0, 0], first_weights, more_weights=rest_of_layer0)
    mine1, got1 = exchange_wait("gather_wait_w_in_l1", gather1, act)

    def rest_of_layer1(after):
        mine, got = exchange_wait("gather_wait_l1", gather1_rest, after)
        return later_weights(own_block_in(got, mine))

    act, saved[1], weights[1] = layer_fwd(1, act, p[1, 0], layer_weights(1, own_block_in(got1, mine1), conv, a),
                                          more_weights=rest_of_layer1)
    d_act, loss_part = loss_call(act, loss_target[0])
    loss = lax.psum(loss_part, ("x", "y", "c"))
    layer_grads = [None, None]
    d_act, layer_grads[1] = layer_bwd(1, d_act, saved[1], weights[1])
    rs1 = OverlappedReduceScatter("l1", pos, [layer_grads[1][nm] for nm in BIG])
    rs0 = []

    def stage_mid(after, g):
        rs1.middle(after)
        return rs1.token

    def stage_late(after, g):
        rs0.append(OverlappedReduceScatter("l0", pos, [g[nm] for nm in BIG[1:]]))
        return rs0[0].token

    def stage_last(after, g):
        rs0[0].middle(after)
        return rs0[0].token

    def stage_w_in(after, g):
        rs0.append(OverlappedReduceScatter("w_in_l0", pos, [g["w_in"]]))
        return rs0[1].token

    d_act, layer_grads[0] = layer_bwd(0, d_act, saved[0], hang_on(weights[0], rs1.token),
                                      hooks=dict(mid=stage_mid, late=stage_late, last=stage_last, w_in=stage_w_in))
    rs0[1].middle(d_act)
    reduced = [rs0[0].finish(rs0[1].token), rs1.finish(rs0[1].token)]
    grad_x = d_act[None]

    def both(nm):
        return jnp.stack([layer_grads[0][nm], layer_grads[1][nm]])

    local = {nm: both(nm) for nm in ("g_mix", "b_fox_f", "fox_q_gain", "fox_k_gain", "dn_norm_gain", "g_ffn", "g_ple", "sc_conv_w",
                                      "dn_conv_w", "ffn_conv_w")}
    local["dn_a_log"] = jnp.stack([layer_grads[li]["ad"][0] for li in range(2)])
    local["dn_dt_bias"] = jnp.stack([layer_grads[li]["ad"][1] for li in range(2)])

    small_names = SMALL + CONVS
    small_shapes = [local[nm].shape for nm in small_names]
    small_sum = sum_devices(gather_small("gather_small_grads", pack_rows([local[nm] for nm in small_names], F32))[0])
    small_grads = dict(zip(small_names, unpack_rows(small_sum, small_shapes)))
    for nm in CONVS:
        width = a[nm].shape[2]
        small_grads[nm] = lax.dynamic_slice_in_dim(small_grads[nm], chip * width, width, axis=2)

    grads, deltas, new_m, new_v = dict(small_grads), {}, {}, {}
    for nm in small_names:
        deltas[nm], new_m[nm], new_v[nm] = adam_call(f"adam_{nm}", a[nm], grads[nm], mom[nm], var[nm])
    for i, nm in enumerate(BIG[1:]):
        res = adam_layers(f"adam_{nm}", as_blocks(a[nm]), as_blocks(mom[nm]), as_blocks(var[nm]), reduced[0][i], reduced[1][1 + i])
        grads[nm], deltas[nm], new_m[nm], new_v[nm] = [r.reshape(a[nm].shape) for r in res]
    stored = lambda t: jnp.transpose(t, (2, 0, 1))
    res = adam_w_in("adam_w_in", stored(a["w_in"]), stored(mom["w_in"]), stored(var["w_in"]), rs0[1].finish(deltas["w_ple"])[0], reduced[1][0])
    grads["w_in"], deltas["w_in"], new_m["w_in"], new_v["w_in"] = [jnp.transpose(r, (1, 2, 0)) for r in res]
    return (loss, grad_x, *[grads[nm] for nm in WEIGHTS], *[deltas[nm] for nm in WEIGHTS], *[new_m[nm] for nm in WEIGHTS],
            *[new_v[nm] for nm in WEIGHTS])
```

```python
import functools

import jax
import jax.numpy as jnp
from jax import lax
from jax.experimental import pallas as pl
from jax.experimental.pallas import tpu as pltpu

F32 = jnp.float32
BF16 = jnp.bfloat16
HI = lax.Precision.HIGHEST
SOLVE = lax.Precision.HIGH
MESH = pl.DeviceIdType.MESH

D_MODEL = 1024
BRANCH = 512
FOX_DH = 64
DN_DH = 128
DN_HEADS = 4
DN_CHUNK = 64
FOX_BLOCK = 128
D_FF = 2816
EPS = 1e-6
N_CHIPS = 4
LANES = 128

ADAM_LR, ADAM_B1, ADAM_B2, ADAM_EPS, ADAM_WD, ADAM_STEP = 0.001, 0.9, 0.999, 1e-08, 0.01, 10

VMEM_LIMIT = 56 * 1024 * 1024

C_FQ, C_FK, C_FV, C_SB, C_SC, C_SV, C_DN, C_DZ, C_GATE = 0, 512, 1024, 1536, 2048, 2560, 3072, 4608, 5120
IN_MAIN = 8192

BIG = ("w_in", "w_branch", "w_o", "w_up", "w_down", "w_ple_gate", "w_ple")
CONVS = ("sc_conv_w", "dn_conv_w", "ffn_conv_w")
SMALL = ("g_mix", "b_fox_f", "fox_q_gain", "fox_k_gain", "dn_a_log", "dn_dt_bias", "dn_norm_gain", "g_ffn", "g_ple")
WEIGHTS = ("g_mix", "w_in", "b_fox_f", "fox_q_gain", "fox_k_gain", "sc_conv_w", "dn_conv_w", "dn_a_log", "dn_dt_bias",
           "dn_norm_gain", "w_branch", "w_o", "g_ffn", "w_up", "ffn_conv_w", "w_down", "g_ple", "w_ple_gate", "w_ple")


def _iota(shape, dim):
    return lax.broadcasted_iota(jnp.int32, shape, dim)


def _dg(a, b, mode, prec=None):
    dims = {"nn": ((1,), (0,)), "nt": ((1,), (1,)), "tn": ((0,), (0,))}[mode]
    return lax.dot_general(a, b, (dims, ((), ())), precision=prec, preferred_element_type=F32)


def _bdot_impl(a, b, mode):
    return _dg(a.astype(BF16), b.astype(BF16), mode)


@functools.partial(jax.custom_vjp, nondiff_argnums=(2,))
def _bdot_diff(a, b, mode):
    return _bdot_impl(a, b, mode)


def _bdot_fwd(a, b, mode):
    return _bdot_impl(a, b, mode), (a, b)


def _bdot_bwd(mode, res, g):
    a, b = res
    if mode == "nn":
        da, db = _bdot_impl(g, b, "nt"), _bdot_impl(a, g, "tn")
    elif mode == "nt":
        da, db = _bdot_impl(g, b, "nn"), _bdot_impl(g, a, "tn")
    else:
        da, db = _bdot_impl(b, g, "nt"), _bdot_impl(a, g, "nn")
    return da.astype(a.dtype), db.astype(b.dtype)


_bdot_diff.defvjp(_bdot_fwd, _bdot_bwd)


def _bdot(d):
    return _bdot_diff if d else _bdot_impl


def _shift_impl(x, k):
    return jnp.where(_iota(x.shape, 0) >= k, pltpu.roll(x, k, 0), 0.0)


def _unshift_impl(g, k):
    n = g.shape[0]
    return jnp.where(_iota(g.shape, 0) < n - k, pltpu.roll(g, n - k, 0), 0.0)


@functools.partial(jax.custom_vjp, nondiff_argnums=(1,))
def _shift_diff(x, k):
    return _shift_impl(x, k)


_shift_diff.defvjp(lambda x, k: (_shift_impl(x, k), None), lambda k, _, g: (_unshift_impl(g, k),))


def _row(w, j):
    return jnp.sum(jnp.where(_iota(w.shape, 0) == j, w, 0.0), axis=0, keepdims=True)


def _col(w, j):
    return jnp.sum(jnp.where(_iota(w.shape, 1) == j, w, 0.0), axis=1, keepdims=True)


def _conv(d, x, w):
    shift = _shift_diff if d else _shift_impl
    taps = w.shape[0]
    y = x * _row(w, taps - 1)
    for j in range(taps - 1):
        y = y + shift(x, taps - 1 - j) * _row(w, j)
    return y


def _softplus(x):
    return jnp.maximum(x, 0.0) + jnp.log(1.0 + jnp.exp(-jnp.abs(x)))


def _sigmoid(x):
    return 0.5 * (jnp.tanh(0.5 * x) + 1.0)


def _silu(x):
    return x * _sigmoid(x)


def _rms(x, gain):
    return x * lax.rsqrt(jnp.mean(x * x, axis=-1, keepdims=True) + EPS) * gain


def _rms_fn(d, pids, x, gain):
    return (_rms(x, gain),)


def _loss_fn(d, pids, y, t):
    e = y - t
    part = 0.5 / D_MODEL * jnp.sum(e * e, keepdims=True)
    return e * (1.0 / D_MODEL), jnp.broadcast_to(part, (8, LANES))


def _fox_prep_fn(d, pids, q, k, gq, gk):
    first = _iota(q.shape, 1) < FOX_DH

    def norm(x, gain):
        sq = x * x
        ss_a = jnp.sum(jnp.where(first, sq, 0.0), axis=1, keepdims=True)
        ss_b = jnp.sum(jnp.where(first, 0.0, sq), axis=1, keepdims=True)
        rs = jnp.where(first, lax.rsqrt(ss_a / FOX_DH + EPS), lax.rsqrt(ss_b / FOX_DH + EPS))
        return x * rs * gain

    return norm(q, gq) * FOX_DH ** -0.5, norm(k, gk)


def _fox_gate_fn(d, pids, f, bias):
    logf = -_softplus(-(f + bias))
    n_r, n_c = logf.shape
    tri = (_iota((n_c, n_c), 0) <= _iota((n_c, n_c), 1)).astype(F32)
    within = _dg(logf, tri, "nn", HI)
    tot = jnp.broadcast_to(jnp.sum(logf, axis=1, keepdims=True), logf.shape)
    below = (_iota((n_r, n_r), 1) < _iota((n_r, n_r), 0)).astype(F32)
    return (within + _dg(below, tot, "nn", HI),)


def _fox_attn_fn(q_block0, d, pids, q, k, v, cq_a, cq_b, ck_a, ck_b):
    dot = _bdot(d)
    first = _iota(q.shape, 1) < FOX_DH
    n_q, n_k = q.shape[0], k.shape[0]
    causal = ((q_block0 + pids[1]) * n_q + _iota((n_q, n_k), 0)) >= _iota((n_q, n_k), 1)

    qs = [jnp.where(first, q, 0.0), jnp.where(first, 0.0, q)]
    s = _each(lambda qh, cq, ck: jnp.where(causal, dot(qh, k, "nt") + cq - ck, -1e30), qs, [cq_a, cq_b], [ck_a, ck_b])
    e = [jnp.exp(si - lax.stop_gradient(jnp.max(si, axis=1, keepdims=True))) for si in s]
    o_a, o_b = [dot(ei * (1.0 / jnp.sum(ei, axis=1, keepdims=True)), v, "nn") for ei in e]
    return (jnp.where(first, o_a, o_b),)


def _sconv_fn(d, pids, sb, sc, sv, w):
    return (sb * _conv(d, sc * sv, w),)


def _dnconv_fn(d, pids, x, w):
    return (_silu(_conv(d, x, w)),)


def _merge_fn(d, pids, y0, y1, y2, g0, g1, g2):
    return (_sigmoid(g0) * y0 + _sigmoid(g1) * y1 + _sigmoid(g2) * y2,)


def _ffn_act_fn(d, pids, ug, uv, wg, wv):
    return (_silu(_conv(d, ug, wg)) * _conv(d, uv, wv),)


def _ple_fn(d, pids, gpre, pe, x):
    return (x + _sigmoid(gpre) * pe,)


def _adam_fn(d, pids, w, g, m, v):
    m2 = ADAM_B1 * m + (1.0 - ADAM_B1) * g
    v2 = ADAM_B2 * v + (1.0 - ADAM_B2) * (g * g)
    m_hat = m2 / (1.0 - ADAM_B1 ** ADAM_STEP)
    v_hat = v2 / (1.0 - ADAM_B2 ** ADAM_STEP)
    delta = -ADAM_LR * (m_hat / (jnp.sqrt(v_hat) + ADAM_EPS) + ADAM_WD * w)
    return delta, m2, v2


def _each(fn, *lists):
    return [fn(*args) for args in zip(*lists)]


def _tri_inv_impl(mats):
    n = mats[0].shape[0]
    r, c = _iota((n, n), 0), _iota((n, n), 1)
    diag_blk = (r >> 4) == (c >> 4)
    eye = (r == c).astype(F32)
    mm = lambda us, ws: _each(lambda u, w: _dg(u, w, "nn", SOLVE), us, ws)
    grow = lambda ps, xs: _each(lambda p, px: p + px, ps, mm(ps, xs))
    x = [jnp.where(diag_blk, -a, 0.0) for a in mats]
    p = [eye + xi for xi in x]
    x2 = mm(x, x)
    p = grow(p, x2)
    x4 = mm(x2, x2)
    p = grow(p, x4)
    p = grow(p, mm(x4, x4))
    y = [-yi for yi in mm(p, [jnp.where(diag_blk, 0.0, a) for a in mats])]
    q = grow([eye + yi for yi in y], mm(y, y))
    return mm(q, p)


@jax.custom_vjp
def _tri_inv_diff(mats):
    return _tri_inv_impl(mats)


def _tri_inv_fwd(mats):
    ts = _tri_inv_impl(mats)
    return ts, ts


def _tri_inv_bwd(ts, gs):
    left = _each(lambda t, g: _dg(t, g, "tn", SOLVE), ts, gs)
    return ([-m for m in _each(lambda l, t: _dg(l, t, "nt", SOLVE), left, ts)],)


_tri_inv_diff.defvjp(_tri_inv_fwd, _tri_inv_bwd)


def _dn_local(d, qs, ks, vs, a_cs, a_rs, b_cs, a_logs, dt_bs):
    dot = _bdot(d)
    inv = _tri_inv_diff if d else _tri_inv_impl
    n = qs[0].shape[0]
    r, c = _iota((n, n), 0), _iota((n, n), 1)
    incl, strict, upper = r >= c, r > c, r <= c
    qs = [q * lax.rsqrt(jnp.sum(q * q, axis=1, keepdims=True) + EPS) * DN_DH ** -0.5 for q in qs]
    ks = [k * lax.rsqrt(jnp.sum(k * k, axis=1, keepdims=True) + EPS) for k in ks]
    betas = [_sigmoid(b) for b in b_cs]
    rates = [-jnp.exp(a) for a in a_logs]
    g_cs = _each(lambda rate, a, dt: rate * _softplus(a + dt), rates, a_cs, dt_bs)
    g_rs = _each(lambda rate, a, dt: rate * _softplus(a + dt), rates, a_rs, dt_bs)
    gcum_cs = [jnp.sum(jnp.where(incl, g, 0.0), axis=1, keepdims=True) for g in g_rs]
    gcum_rs = [jnp.sum(jnp.where(upper, g, 0.0), axis=0, keepdims=True) for g in g_cs]
    decays = _each(lambda gc, gr: jnp.exp(jnp.where(incl, gc - gr, -1e30)), gcum_cs, gcum_rs)
    kbs = _each(lambda k, b: k * b, ks, betas)
    kk = _each(lambda kb, k: dot(kb, k, "nt"), kbs, ks)
    ts = inv(_each(lambda m, dec: jnp.where(strict, m * dec, 0.0), kk, decays))
    e_gs = [jnp.exp(g) for g in gcum_cs]
    us = _each(lambda t, v, b: _dg(t, v * b, "nn", SOLVE), ts, vs, betas)
    k_cums = _each(lambda t, kb, e: _dg(t, kb * e, "nn", SOLVE), ts, kbs, e_gs)
    qk = _each(lambda q, k: dot(q, k, "nt"), qs, ks)
    qk = _each(lambda m, dec: jnp.where(incl, m * dec, 0.0), qk, decays)
    g_lasts = [jnp.sum(g, axis=0, keepdims=True) for g in g_cs]
    q_decs = _each(lambda q, e: q * e, qs, e_gs)
    k_decs = _each(lambda k, gl, gc: k * jnp.exp(gl - gc), ks, g_lasts, gcum_cs)
    return list(zip(us, k_cums, q_decs, k_decs, qk, g_lasts))


def _dn_step(d, s_prevs, items, zs, gain):
    dot = _bdot(d)
    us, k_cums, q_decs, k_decs, qks, g_lasts = [list(t) for t in zip(*items)]
    v_news = _each(lambda u, kc, s: u - dot(kc, s, "nn"), us, k_cums, s_prevs)
    inter = _each(lambda qd, s: dot(qd, s, "nn"), q_decs, s_prevs)
    outs = _each(lambda o, qk, vn: o + dot(qk, vn, "nn"), inter, qks, v_news)
    s_nexts = _each(lambda s, gl, kd, vn: s * jnp.exp(gl) + dot(kd, vn, "tn"), s_prevs, g_lasts, k_decs, v_news)
    return _each(lambda o, z: _rms(o, gain) * _silu(z), outs, zs), s_nexts


def _split_heads(t):
    return [t[:, h * DN_DH:(h + 1) * DN_DH] for h in range(t.shape[1] // DN_DH)]


def _dn_gates(ps, a_rows, ad):
    hs = range(DN_HEADS)
    return ([_col(ps, 12 + h) for h in hs], [_row(a_rows, h) for h in hs], [_col(ps, 8 + h) for h in hs],
            [_col(_row(ad, 0), h) for h in hs], [_col(_row(ad, 1), h) for h in hs])


def _head_rows(vals):
    row = _iota((8, LANES), 0)
    tile = jnp.zeros((8, LANES), F32)
    for h, val in enumerate(vals):
        tile = tile + jnp.where(row == h, val, 0.0)
    return tile


def _cparams(n_axes):
    return pltpu.CompilerParams(dimension_semantics=("arbitrary",) * n_axes, vmem_limit_bytes=VMEM_LIMIT)


def _first_visit(acc_axes):
    cond = None
    for a in acc_axes:
        here = pl.program_id(a) == 0
        cond = here if cond is None else jnp.logical_and(cond, here)
    return cond


def _tile(ref, widen=False):
    val = ref[...]
    shape = val.shape
    while len(shape) > 2 and shape[0] == 1:
        shape = shape[1:]
    val = val.reshape(shape)
    return val.astype(F32) if widen and val.dtype == BF16 else val


def _store(ref, val, first):
    val = val.astype(ref.dtype).reshape(ref.shape)
    if first is None:
        ref[...] = val
        return

    @pl.when(first)
    def _():
        ref[...] = val

    @pl.when(jnp.logical_not(first))
    def _():
        ref[...] += val


def _specs(ops):
    return [pl.BlockSpec(block, imap) for _, block, imap in ops]


def tile_fwd(name, fn, grid, ins, outs, raw=()):
    n_in = len(ins)

    def body(*refs):
        pids = tuple(pl.program_id(a) for a in range(len(grid)))
        firsts = [_first_visit(o[4]) if o[4] else None for o in outs]
        res = fn(False, pids, *[_tile(r, i not in raw) for i, r in enumerate(refs[:n_in])])
        for ref, val, first in zip(refs[n_in:], res, firsts):
            _store(ref, val, first)

    out = pl.pallas_call(
        body, grid=grid, in_specs=_specs(ins),
        out_specs=[pl.BlockSpec(o[2], o[3]) for o in outs],
        out_shape=[jax.ShapeDtypeStruct(o[0], o[1]) for o in outs],
        name=name, compiler_params=_cparams(len(grid)),
    )(*[a for a, _, _ in ins])
    return out


def tile_bwd(name, fn, grid, ins, cots, diff, adds=None, raw=()):
    adds = adds or {}
    n_in, n_cot = len(ins), len(cots)
    add_pos = sorted(adds)
    diff_idx = [d[0] for d in diff]
    out_desc = [d[2] if len(d) > 2 and d[2] is not None else (ins[d[0]][0].shape, ins[d[0]][1], ins[d[0]][2]) for d in diff]
    out_dtypes = [d[3] if len(d) > 3 else F32 for d in diff]

    def body(*refs):
        pids = tuple(pl.program_id(a) for a in range(len(grid)))
        firsts = [_first_visit(d[1]) if d[1] else None for d in diff]
        vals = [_tile(r, i not in raw) for i, r in enumerate(refs[:n_in])]
        cot_vals = [_tile(r, True) for r in refs[n_in:n_in + n_cot]]
        add_vals = [_tile(r) for r in refs[n_in + n_cot:n_in + n_cot + len(add_pos)]]
        out_refs = refs[n_in + n_cot + len(add_pos):]

        def f(*dv):
            full = list(vals)
            for i, val in zip(diff_idx, dv):
                full[i] = val
            return fn(True, pids, *full)

        prim, vjp = jax.vjp(f, *[vals[i].astype(F32) for i in diff_idx])
        grads = list(vjp(tuple(c.astype(o.dtype) for c, o in zip(cot_vals, prim))))
        for pos, val in zip(add_pos, add_vals):
            extra = val.astype(F32) if firsts[pos] is None else jnp.where(firsts[pos], val.astype(F32), 0.0)
            grads[pos] = grads[pos] + extra
        for ref, val, first in zip(out_refs, grads, firsts):
            _store(ref, val, first)

    all_ins = list(ins) + list(cots) + [adds[p] for p in add_pos]
    out = pl.pallas_call(
        body, grid=grid, in_specs=_specs(all_ins),
        out_specs=[pl.BlockSpec(o[1], o[2]) for o in out_desc],
        out_shape=[jax.ShapeDtypeStruct(o[0], dt) for o, dt in zip(out_desc, out_dtypes)],
        name=name, compiler_params=_cparams(len(grid)),
    )(*[a for a, _, _ in all_ins])
    return out


def _pick(dim, cands):
    for c in cands:
        if dim % c == 0:
            return c
    return dim


MM_VMEM_BUDGET = 40 * 1024 * 1024
MM_TILES = (1024, 512, 1408, 256, 128)


def mm(name, a, b, mode, add=None, out_dtype=F32, blocks=None, into=None):
    wide = None
    if mode == "nn":
        (m, kk), n = a.shape, b.shape[-1]
    elif mode == "nt":
        (m, kk), n = a.shape, b.shape[-2]
    else:
        (kk, m), n = a.shape, b.shape[1]
    if blocks is not None:
        lo, n_blk = blocks
        wide = b.shape[-1] if mode != "tn" else n // n_blk
        if mode == "nn":
            n = wide * n_blk
    tm = _pick(m, MM_TILES)
    if mode == "nt" and blocks is not None:
        tn, tk = _pick(n, MM_TILES), _pick(wide, MM_TILES[:-1])
    elif blocks is not None:
        tn, tk = _pick(wide, MM_TILES[:-1]), _pick(kk, MM_TILES)
    else:
        tn, tk = _pick(n, MM_TILES), _pick(kk, MM_TILES)
    if mode == "tn" or blocks is None:
        tk = _pick(kk, (2048,) + MM_TILES)
    if mode != "tn" and add is None and m % 2048 == 0 and (n // tn) * (kk // tk) > 1:
        windows = 2 * (2048 * tk * a.dtype.itemsize + tk * tn * b.dtype.itemsize + 2048 * tn * jnp.dtype(out_dtype).itemsize)
        if windows + 2048 * tn * 4 <= MM_VMEM_BUDGET:
            tm = 2048
    nk = kk // tk
    a_spec = pl.BlockSpec((tk, tm), lambda i, j, k: (k, i)) if mode == "tn" else pl.BlockSpec((tm, tk), lambda i, j, k: (i, k))
    o_spec = pl.BlockSpec((tm, tn), lambda i, j, k: (i, j))
    out_shape = (m, n)
    if blocks is None:
        b_spec = pl.BlockSpec((tn, tk), lambda i, j, k: (j, k)) if mode == "nt" else pl.BlockSpec((tk, tn), lambda i, j, k: (k, j))
    elif mode == "nn":
        per = wide // tn
        b_spec = pl.BlockSpec((1, tk, tn), lambda i, j, k: (lo + j // per, k, j % per))
    elif mode == "nt":
        per = wide // tk
        b_spec = pl.BlockSpec((1, tn, tk), lambda i, j, k: (lo + k // per, j, k % per))
    else:
        per = wide // tn
        total, first = (into[0], into[1]) if into is not None else (n_blk, 0)
        b_spec = pl.BlockSpec((tk, tn), lambda i, j, k: (k, j))
        o_spec = pl.BlockSpec((1, tm, tn), lambda i, j, k: (first + j // per, i, j % per))
        out_shape = (total, m, wide)

    def body(*refs):
        a_ref, b_ref = refs[0], refs[1]
        add_ref = refs[2] if add is not None else None
        o_ref, acc = refs[-2], refs[-1]
        k = pl.program_id(2)
        part = _bdot_impl(_tile(a_ref), _tile(b_ref), mode)

        @pl.when(k == 0)
        def _():
            acc[...] = part

        @pl.when(k > 0)
        def _():
            acc[...] += part

        @pl.when(k == nk - 1)
        def _():
            res = acc[...]
            if add_ref is not None:
                res = res + add_ref[...]
            o_ref[...] = res.astype(o_ref.dtype).reshape(o_ref.shape)

    operands = [a, b] + ([add] if add is not None else [])
    in_specs = [a_spec, b_spec] + ([o_spec] if add is not None else [])
    aliases = {}
    if into is not None and len(into) > 2:
        operands, in_specs, aliases = operands + [into[2]], in_specs + [pl.BlockSpec(memory_space=pl.ANY)], {len(operands): 0}
    return pl.pallas_call(
        body, grid=(m // tm, n // tn, nk), in_specs=in_specs, out_specs=o_spec,
        out_shape=jax.ShapeDtypeStruct(out_shape, out_dtype),
        scratch_shapes=[pltpu.VMEM((tm, tn), F32)], input_output_aliases=aliases,
        name=name, compiler_params=_cparams(3),
    )(*operands)


def _rows(x, width=None, off=0, tm=256):
    width = x.shape[1] if width is None else width
    return (x, (tm, width), lambda i, off=off: (i, off))


def _whole(x):
    nd = x.ndim
    return (x, x.shape, lambda *pids, nd=nd: (0,) * nd)


RMS_ROWS = 512


def _rms_ops(x, gain):
    return [_rows(x, tm=RMS_ROWS), _whole(gain)]


def rms_fwd(name, x, gain):
    s, dm = x.shape
    return tile_fwd(name, _rms_fn, (s // RMS_ROWS,), _rms_ops(x, gain), [((s, dm), BF16, (RMS_ROWS, dm), lambda i: (i, 0), ())])[0]


def rms_bwd(name, x, gain, dh, dres):
    s = x.shape[0]
    return tile_bwd(name, _rms_fn, (s // RMS_ROWS,), _rms_ops(x, gain), [_rows(dh, tm=RMS_ROWS)], [(0, ()), (1, (0,))],
                    adds={0: _rows(dres, tm=RMS_ROWS)})


def loss_call(y, t):
    s, dm = y.shape
    dy, part = tile_fwd("loss", _loss_fn, (s // 256,), [_rows(y), _rows(t)],
                        [((s, dm), F32, (256, dm), lambda i: (i, 0), ()), ((8, LANES), F32, (8, LANES), lambda i: (0, 0), (0,))])
    return dy, part[0, 0]


def _fox_prep_ops(pm, gq, gk):
    tm = 512
    return [(pm, (tm, LANES), lambda i, j: (i, C_FQ // LANES + j)), (pm, (tm, LANES), lambda i, j: (i, C_FK // LANES + j)),
            _whole(gq), _whole(gk)]


def fox_prep_fwd(name, pm, gq, gk):
    s = pm.shape[0]
    out = ((s, BRANCH), BF16, (512, LANES), lambda i, j: (i, j), ())
    return tile_fwd(name, _fox_prep_fn, (s // 512, 4), _fox_prep_ops(pm, gq, gk), [out, out])


def fox_prep_bwd(name, pm, gq, gk, dqn, dkn):
    s = pm.shape[0]
    cot = lambda g: (g, (512, LANES), lambda i, j: (i, j))
    own = ((s, BRANCH), (512, LANES), lambda i, j: (i, j))
    return tile_bwd(name, _fox_prep_fn, (s // 512, 4), _fox_prep_ops(pm, gq, gk), [cot(dqn), cot(dkn)],
                    [(0, (), own, BF16), (1, (), own, BF16), (2, (0, 1)), (3, (0, 1))])


def _fox_gate_ops(f_t, bias):
    return [(f_t, (1,) + f_t.shape[1:], lambda h: (h, 0, 0)), (bias, (1, 1, 1), lambda h: (h, 0, 0))]


def fox_gate_fwd(name, f_t, bias):
    n_h = f_t.shape[0]
    return tile_fwd(name, _fox_gate_fn, (n_h,), _fox_gate_ops(f_t, bias),
                    [(f_t.shape, F32, (1,) + f_t.shape[1:], lambda h: (h, 0, 0), ())])[0]


def fox_gate_bwd(name, f_t, bias, dcum):
    n_h = f_t.shape[0]
    return tile_bwd(name, _fox_gate_fn, (n_h,), _fox_gate_ops(f_t, bias),
                    [(dcum, (1,) + f_t.shape[1:], lambda h: (h, 0, 0))], [(0, ()), (1, ())])


FOX_GROUPS = 4


def _fox_groups(s):
    per = s // FOX_BLOCK // FOX_GROUPS
    return [(g * per, per, (g + 1) * per * FOX_BLOCK) for g in range(FOX_GROUPS)]


def _fox_attn_ops(qn, kn, pm, cum_c, cum_r, q0, keys):
    nb = FOX_BLOCK
    return [(qn, (nb, LANES), lambda p, i: (q0 + i, p)), (kn, (keys, LANES), lambda p, i: (0, p)),
            (pm, (keys, LANES), lambda p, i: (0, C_FV // LANES + p)),
            (cum_c, (1, nb, 1), lambda p, i: (2 * p, q0 + i, 0)), (cum_c, (1, nb, 1), lambda p, i: (2 * p + 1, q0 + i, 0)),
            (cum_r, (1, 1, keys), lambda p, i: (2 * p, 0, 0)), (cum_r, (1, 1, keys), lambda p, i: (2 * p + 1, 0, 0))]


def fox_attn_fwd(name, qn, kn, pm, cum_c, cum_r):
    s = qn.shape[0]
    parts = []
    for g, (q0, n_q, keys) in enumerate(_fox_groups(s)):
        parts.append(tile_fwd(f"{name}_g{g}", functools.partial(_fox_attn_fn, q0), (4, n_q), _fox_attn_ops(qn, kn, pm, cum_c, cum_r, q0, keys),
                              [((n_q * FOX_BLOCK, BRANCH), BF16, (FOX_BLOCK, LANES), lambda p, i: (i, p), ())], raw=(0, 1, 2))[0])
    return jnp.concatenate(parts, axis=0)


def fox_attn_bwd(name, qn, kn, pm, cum_c, cum_r, dy):
    s = qn.shape[0]
    groups = _fox_groups(s)
    d_qn, by_q, tails = [None] * len(groups), [None] * len(groups), [None] * len(groups)
    below = None
    for g in reversed(range(len(groups))):
        q0, n_q, keys = groups[g]
        rows = n_q * FOX_BLOCK
        own_q = ((rows, BRANCH), (FOX_BLOCK, LANES), lambda p, i: (i, p))
        own_k = ((keys, BRANCH), (keys, LANES), lambda p, i: (0, p))
        pair_c = ((4, rows, 1), (1, FOX_BLOCK, 1), lambda p, i: (p, i, 0))
        pair_r = ((4, 1, keys), (1, 1, keys), lambda p, i: (p, 0, 0))
        adds = {}
        if below is not None:
            adds = {1: (below[0],) + own_k[1:], 2: (below[1],) + own_k[1:], 5: (below[2],) + pair_r[1:], 6: (below[3],) + pair_r[1:]}
        g_qn, g_kn, g_v, g_cqa, g_cqb, g_cka, g_ckb = tile_bwd(
            f"{name}_g{g}", functools.partial(_fox_attn_fn, q0), (4, n_q), _fox_attn_ops(qn, kn, pm, cum_c, cum_r, q0, keys),
            [(dy, (FOX_BLOCK, LANES), lambda p, i, q0=q0: (q0 + i, p))],
            [(0, (), own_q), (1, (1,), own_k), (2, (1,), own_k), (3, (), pair_c), (4, (), pair_c), (5, (1,), pair_r), (6, (1,), pair_r)],
            adds=adds)
        below = (g_kn, g_v, g_cka, g_ckb)
        lo = groups[g - 1][2] if g else 0
        d_qn[g] = g_qn
        by_q[g] = jnp.stack([g_cqa[:, :, 0], g_cqb[:, :, 0]], axis=1).reshape(8, rows)
        tails[g] = (g_kn[lo:], g_v[lo:], jnp.stack([g_cka[:, 0, lo:], g_ckb[:, 0, lo:]], axis=1).reshape(8, keys - lo))
    d_cum = jnp.concatenate(by_q, axis=1) + jnp.concatenate([t[2] for t in tails], axis=1)
    return jnp.concatenate(d_qn, axis=0), jnp.concatenate([t[0] for t in tails], axis=0), jnp.concatenate([t[1] for t in tails], axis=0), d_cum


def sconv_ops(pm, w):
    s = pm.shape[0]
    blk = lambda c0: (pm, (s, LANES), lambda j, c0=c0: (0, c0 // LANES + j))
    return [blk(C_SB), blk(C_SC), blk(C_SV), (w, (w.shape[0], LANES), lambda j: (0, j))]


def dnconv_ops(pm, w):
    s = pm.shape[0]
    return [(pm, (s, LANES), lambda j: (0, C_DN // LANES + j)), (w, (w.shape[0], LANES), lambda j: (0, j))]


def ffn_ops(ug, uv, w):
    s = ug.shape[0]
    n_t = D_FF // LANES
    return [(ug, (s, LANES), lambda j: (0, j)), (uv, (s, LANES), lambda j: (0, j)),
            (w, (w.shape[0], LANES), lambda j: (0, j)), (w, (w.shape[0], LANES), lambda j: (0, n_t + j))]


def _col_out(s, width, dtype=F32):
    return ((s, width), dtype, (s, LANES), lambda j: (0, j), ())


def _col_cot(g):
    return (g, (g.shape[0], LANES), lambda j: (0, j))


def merge_ops(yp, pm):
    gate = lambda b: (pm, (256, D_MODEL), lambda i, b=b: (i, C_GATE // D_MODEL + b))
    return [_rows(yp[0]), _rows(yp[1]), _rows(yp[2]), gate(0), gate(1), gate(2)]


def ple_ops(gpre, pe, x):
    return [_rows(gpre), _rows(pe), _rows(x)]


def adam_call(name, w, g, m, v):
    shape = w.shape
    last = shape[-1]
    rows = w.size // last
    flat = lambda t: t.reshape(rows, last)
    tm = rows
    for cand in (512, 256, 128, 64, 32, 16, 8):
        if rows % cand == 0 and cand * last * 4 <= 2 * 1024 * 1024:
            tm = cand
            break
    spec = lambda t: (flat(t), (tm, last), lambda i: (i, 0))
    out = ((rows, last), F32, (tm, last), lambda i: (i, 0), ())
    res = tile_fwd(name, _adam_fn, (rows // tm,), [spec(w), spec(g), spec(m), spec(v)], [out, out, out])
    return [r.reshape(shape) for r in res]


def _adam_layers_fn(d, pids, w, m, v, g0, g1):
    g = jnp.where(pids[0] == 0, g0, g1)
    return (g,) + _adam_fn(d, pids, w, g, m, v)


def adam_layers(name, w, m, v, g0, g1):
    _, rows, cols = w.shape
    tm = _row_tile(rows, cols)
    n_t = rows // tm
    lay = lambda t: (t, (1, tm, cols), lambda l, i: (l, i, 0))
    ins = [lay(w), lay(m), lay(v), (g0, (tm, cols), lambda l, i: (i * (1 - l) + (n_t - 1) * l, 0)), (g1, (tm, cols), lambda l, i: (i * l, 0))]
    out = (w.shape, F32, (1, tm, cols), lambda l, i: (l, i, 0), ())
    return tile_fwd(name, _adam_layers_fn, (2, n_t), ins, [out, out, out, out])


def adam_w_in(name, w, m, v, g0, g1):
    rows, n_l, cols = w.shape

    def body(w_ref, m_ref, v_ref, g0_ref, g1_ref, g_out, d_out, m_out, v_out):
        step = 64

        def update(at):
            g0, g1 = g0_ref[at, :], g1_ref[at, :]
            layer = _iota((g0.shape[0], n_l, LANES), 1)
            g = jnp.where(layer == 0, g0[:, None, :], g1[:, None, :])
            delta, m2, v2 = _adam_fn(False, None, w_ref[at], g, m_ref[at], v_ref[at])
            for ref, val in ((g_out, g), (d_out, delta), (m_out, m2), (v_out, v2)):
                ref[at] = val

        def some_rows(i, carry):
            update(pl.ds(pl.multiple_of(i * step, step), step))
            return carry

        lax.fori_loop(0, rows // step, some_rows, 0)
        if rows % step:
            update(pl.ds(rows - rows % step, rows % step))

    both = pl.BlockSpec((rows, n_l, LANES), lambda j: (0, 0, j))
    one = pl.BlockSpec((rows, LANES), lambda j: (0, j))
    return pl.pallas_call(
        body, grid=(cols // LANES,), in_specs=[both, both, both, one, one], out_specs=[both] * 4,
        out_shape=[jax.ShapeDtypeStruct(w.shape, F32)] * 4, name=name, compiler_params=_cparams(1),
    )(w, m, v, g0, g1)


DN_GROUP = 4


def _dn_local_specs():
    rows = DN_GROUP * DN_CHUNK
    return [pl.BlockSpec((rows, 3 * BRANCH), lambda j: (j, 0)), pl.BlockSpec((rows, LANES), lambda j: (j, 0)),
            pl.BlockSpec((DN_GROUP, DN_HEADS, DN_CHUNK), lambda j: (j, 0, 0)), pl.BlockSpec((2, DN_HEADS), lambda j: (0, 0))]


def _dn_group_inputs(qkv, ps, a_rows, c):
    lo = c * DN_CHUNK
    heads = _split_heads(qkv[lo:lo + DN_CHUNK])
    return heads[0:4], heads[4:8], heads[8:12], ps[lo:lo + DN_CHUNK], a_rows[c]


def dn_local_fwd(name, dn_act, ps, a_rows, ad):
    s = dn_act.shape[0]
    n_c, n_g = s // DN_CHUNK, s // (DN_GROUP * DN_CHUNK)
    rows = DN_GROUP * DN_CHUNK

    def body(qkv_ref, ps_ref, ar_ref, ad_ref, u_ref, kc_ref, qd_ref, kd_ref, qk_ref, gl_ref):
        qkv, ps_v, a_rows_v, ad_v = qkv_ref[...], ps_ref[...], ar_ref[...], ad_ref[...]
        args = [[] for _ in range(8)]
        for c in range(DN_GROUP):
            q4, k4, v4, ps_c, ar_c = _dn_group_inputs(qkv, ps_v, a_rows_v, c)
            for lst, vals in zip(args, (q4, k4, v4) + _dn_gates(ps_c, ar_c, ad_v)):
                lst.extend(vals)
        everything = _dn_local(False, *args)
        for c in range(DN_GROUP):
            res = everything[c * DN_HEADS:(c + 1) * DN_HEADS]
            at = pl.ds(c * DN_CHUNK, DN_CHUNK)
            for ref, i in ((u_ref, 0), (kc_ref, 1), (qd_ref, 2), (kd_ref, 3)):
                ref[at, :] = jnp.concatenate([r[i] for r in res], axis=1)
            for h in range(DN_HEADS):
                qk_ref[c, h] = res[h][4]
            gl_ref[c] = _head_rows([r[5] for r in res])

    wide = pl.BlockSpec((rows, BRANCH), lambda j: (j, 0))
    return pl.pallas_call(
        body, grid=(n_g,), in_specs=_dn_local_specs(),
        out_specs=[wide, wide, wide, wide, pl.BlockSpec((DN_GROUP, DN_HEADS, DN_CHUNK, DN_CHUNK), lambda j: (j, 0, 0, 0)),
                   pl.BlockSpec((DN_GROUP, 8, LANES), lambda j: (j, 0, 0))],
        out_shape=[jax.ShapeDtypeStruct((s, BRANCH), F32)] * 4 + [jax.ShapeDtypeStruct((n_c, DN_HEADS, DN_CHUNK, DN_CHUNK), F32),
                                                                 jax.ShapeDtypeStruct((n_c, 8, LANES), F32)],
        name=name, compiler_params=_cparams(1),
    )(dn_act, ps, a_rows, ad)


def dn_local_bwd(name, dn_act, ps, a_rows, ad, cots):
    s = dn_act.shape[0]
    n_c, n_g = s // DN_CHUNK, s // (DN_GROUP * DN_CHUNK)
    rows = DN_GROUP * DN_CHUNK

    def body(qkv_ref, ps_ref, ar_ref, ad_ref, du_ref, dkc_ref, dqd_ref, dkd_ref, dqk_ref, dgl_ref, dqkv_ref, dps_ref, dar_ref, dad_ref):
        first = pl.program_id(0) == 0
        qkv, ps_v, a_rows_v, ad_v = qkv_ref[...], ps_ref[...], ar_ref[...], ad_ref[...]
        d_wide = [r[...] for r in (du_ref, dkc_ref, dqd_ref, dkd_ref)]
        qs, ks, vs, ps_cs, ar_cs, cot = [], [], [], [], [], []
        for c in range(DN_GROUP):
            q4, k4, v4, ps_c, ar_c = _dn_group_inputs(qkv, ps_v, a_rows_v, c)
            qs, ks, vs, ps_cs, ar_cs = qs + q4, ks + k4, vs + v4, ps_cs + [ps_c], ar_cs + [ar_c]
            lo = c * DN_CHUNK
            d_tiles = [_split_heads(t[lo:lo + DN_CHUNK]) for t in d_wide]
            d_gl = dgl_ref[c]
            cot += [(d_tiles[0][h], d_tiles[1][h], d_tiles[2][h], d_tiles[3][h], dqk_ref[c, h], _col(_row(d_gl, h), 0))
                    for h in range(DN_HEADS)]

        def f(qs, ks, vs, ps_cs, ar_cs, ad_v):
            gates = [[] for _ in range(5)]
            for ps_c, ar_c in zip(ps_cs, ar_cs):
                for lst, vals in zip(gates, _dn_gates(ps_c, ar_c, ad_v)):
                    lst.extend(vals)
            return _dn_local(True, qs, ks, vs, *gates)

        _, vjp = jax.vjp(f, qs, ks, vs, ps_cs, ar_cs, ad_v)
        d_q, d_k, d_v, d_ps, d_ar, d_ad = vjp(cot)
        for c in range(DN_GROUP):
            at, hs = pl.ds(c * DN_CHUNK, DN_CHUNK), slice(c * DN_HEADS, (c + 1) * DN_HEADS)
            dqkv_ref[at, :] = jnp.concatenate(d_q[hs] + d_k[hs] + d_v[hs], axis=1).astype(dqkv_ref.dtype)
            dps_ref[at, :] = d_ps[c]
            dar_ref[c] = d_ar[c]
        _store(dad_ref, d_ad, first)

    wide = pl.BlockSpec((rows, BRANCH), lambda j: (j, 0))
    specs = _dn_local_specs()
    return pl.pallas_call(
        body, grid=(n_g,),
        in_specs=specs + [wide, wide, wide, wide, pl.BlockSpec((DN_GROUP, DN_HEADS, DN_CHUNK, DN_CHUNK), lambda j: (j, 0, 0, 0)),
                          pl.BlockSpec((DN_GROUP, 8, LANES), lambda j: (j, 0, 0))],
        out_specs=specs,
        out_shape=[jax.ShapeDtypeStruct((s, 3 * BRANCH), F32), jax.ShapeDtypeStruct((s, LANES), F32),
                   jax.ShapeDtypeStruct((n_c, DN_HEADS, DN_CHUNK), F32), jax.ShapeDtypeStruct((2, DN_HEADS), F32)],
        name=name, compiler_params=_cparams(1),
    )(dn_act, ps, a_rows, ad, *cots)


def _dn_scan_specs(n_c, rev):
    idx = (lambda j: n_c - 1 - j) if rev else (lambda j: j)
    wide = pl.BlockSpec((DN_CHUNK, BRANCH), lambda j: (idx(j), 0))
    return [wide, wide, wide, wide, pl.BlockSpec((1, DN_HEADS, DN_CHUNK, DN_CHUNK), lambda j: (idx(j), 0, 0, 0)),
            pl.BlockSpec((1, 8, LANES), lambda j: (idx(j), 0, 0)), pl.BlockSpec((DN_CHUNK, BRANCH), lambda j: (idx(j), C_DZ // BRANCH)),
            pl.BlockSpec((1, DN_DH), lambda j: (0, 0))]


def _dn_scan_tiles(refs):
    u_ref, kc_ref, qd_ref, kd_ref, qk_ref, gl_ref, z_ref, g_ref = refs
    wide = [_split_heads(r[...]) for r in (u_ref, kc_ref, qd_ref, kd_ref)]
    gl = gl_ref[0]
    return [(wide[0][h], wide[1][h], wide[2][h], wide[3][h], qk_ref[0, h], _col(_row(gl, h), 0)) for h in range(DN_HEADS)], \
        _split_heads(z_ref[...].astype(F32)), g_ref[...]


def dn_scan_fwd(name, local, pm, gain):
    s = pm.shape[0]
    n_c = s // DN_CHUNK

    def body(*refs):
        y_ref, hist_ref, state = refs[8:]

        @pl.when(pl.program_id(0) == 0)
        def _():
            state[...] = jnp.zeros_like(state)

        hist_ref[0] = state[...]
        per_head, z4, gain_v = _dn_scan_tiles(refs[:8])
        ys, s_nexts = _dn_step(False, [state[h] for h in range(DN_HEADS)], per_head, z4, gain_v)
        for h in range(DN_HEADS):
            state[h] = s_nexts[h]
        y_ref[...] = jnp.concatenate(ys, axis=1).astype(y_ref.dtype)

    return pl.pallas_call(
        body, grid=(n_c,), in_specs=_dn_scan_specs(n_c, False),
        out_specs=[pl.BlockSpec((DN_CHUNK, BRANCH), lambda j: (j, 0)),
                   pl.BlockSpec((1, DN_HEADS, DN_DH, DN_DH), lambda j: (j, 0, 0, 0))],
        out_shape=[jax.ShapeDtypeStruct((s, BRANCH), BF16), jax.ShapeDtypeStruct((n_c, DN_HEADS, DN_DH, DN_DH), F32)],
        scratch_shapes=[pltpu.VMEM((DN_HEADS, DN_DH, DN_DH), F32)],
        name=name, compiler_params=_cparams(1),
    )(*local, pm, gain)


def dn_scan_bwd(name, local, pm, gain, hist, dy):
    s = pm.shape[0]
    n_c = s // DN_CHUNK

    def body(*refs):
        hist_ref, dy_ref = refs[8:10]
        du_ref, dkc_ref, dqd_ref, dkd_ref, dqk_ref, dgl_ref, dz_ref, dg_ref, d_state = refs[10:]
        first = pl.program_id(0) == 0

        @pl.when(first)
        def _():
            d_state[...] = jnp.zeros_like(d_state)

        per_head, z4, gain_v = _dn_scan_tiles(refs[:8])
        _, vjp = jax.vjp(functools.partial(_dn_step, True), [hist_ref[0, h] for h in range(DN_HEADS)], per_head, z4, gain_v)
        d_s, grads, d_z, d_gain = vjp((_split_heads(dy_ref[...].astype(F32)), [d_state[h] for h in range(DN_HEADS)]))
        for h in range(DN_HEADS):
            d_state[h] = d_s[h]
        for ref, i in ((du_ref, 0), (dkc_ref, 1), (dqd_ref, 2), (dkd_ref, 3)):
            ref[...] = jnp.concatenate([g[i] for g in grads], axis=1)
        dz_ref[...] = jnp.concatenate(d_z, axis=1).astype(dz_ref.dtype)
        for h in range(DN_HEADS):
            dqk_ref[0, h] = grads[h][4]
        dgl_ref[0] = _head_rows([g[5] for g in grads])
        _store(dg_ref, d_gain, first)

    rev = lambda j: n_c - 1 - j
    specs = _dn_scan_specs(n_c, True)
    return pl.pallas_call(
        body, grid=(n_c,),
        in_specs=specs + [pl.BlockSpec((1, DN_HEADS, DN_DH, DN_DH), lambda j: (rev(j), 0, 0, 0)),
                          pl.BlockSpec((DN_CHUNK, BRANCH), lambda j: (rev(j), 0))],
        out_specs=specs[:6] + [pl.BlockSpec((DN_CHUNK, BRANCH), lambda j: (rev(j), 0)), specs[7]],
        out_shape=[jax.ShapeDtypeStruct((s, BRANCH), F32)] * 4 + [
            jax.ShapeDtypeStruct((n_c, DN_HEADS, DN_CHUNK, DN_CHUNK), F32), jax.ShapeDtypeStruct((n_c, 8, LANES), F32),
            jax.ShapeDtypeStruct((s, BRANCH), BF16), jax.ShapeDtypeStruct((1, DN_DH), F32)],
        scratch_shapes=[pltpu.VMEM((DN_HEADS, DN_DH, DN_DH), F32)],
        name=name, compiler_params=_cparams(1),
    )(*local, pm, gain, hist, dy)


def _seq_layouts(cols, s):
    return cols.T.reshape(cols.shape[1], s // LANES, LANES)


def layer_fwd(li, x, p, w, more_weights=None):
    s = x.shape[0]
    n = lambda t: f"{t}_l{li}"
    h = rms_fwd(n("rms_mix"), x, w["g_mix"])
    pm = mm(n("in_main"), h, w["in_main"], "nn")
    ps = mm(n("in_small"), h, w["in_small"], "nn")
    qn, kn = fox_prep_fwd(n("fox_prep"), pm, w["gq"], w["gk"])
    f_t = _seq_layouts(ps[:, 0:8], s)
    cum = fox_gate_fwd(n("fox_gate"), f_t, w["b_f"])
    cum_c, cum_r = cum.reshape(8, s, 1), cum.reshape(8, 1, s)
    y_fox = fox_attn_fwd(n("fox_attn"), qn, kn, pm, cum_c, cum_r)
    y_sc = tile_fwd(n("sconv"), _sconv_fn, (BRANCH // LANES,), sconv_ops(pm, w["sc_conv_w"]), [_col_out(s, BRANCH, BF16)])[0]
    dn_act = tile_fwd(n("dnconv"), _dnconv_fn, (3 * BRANCH // LANES,), dnconv_ops(pm, w["dn_conv_w"]), [_col_out(s, 3 * BRANCH)])[0]
    a_rows = ps[:, 12:16].reshape(s // DN_CHUNK, DN_CHUNK, DN_HEADS).transpose(0, 2, 1)
    dn_local = dn_local_fwd(n("dn_local"), dn_act, ps, a_rows, w["ad"])
    y_dn, hist = dn_scan_fwd(n("dn_scan"), dn_local, pm, w["dn_gain"])
    ys = (y_fox, y_sc, y_dn)
    if more_weights is not None:
        w = {**w, **more_weights(y_dn)}
    yp = [mm(n(f"branch{b}"), ys[b], w["branch"][b], "nn", blocks=(0, N_CHIPS)) for b in range(3)]
    merged = tile_fwd(n("merge"), _merge_fn, (s // 256,), merge_ops(yp, pm), [((s, D_MODEL), BF16, (256, D_MODEL), lambda i: (i, 0), ())])[0]
    x1 = mm(n("w_o"), merged, w["o"], "nn", add=x)
    h2 = rms_fwd(n("rms_ffn"), x1, w["g_ffn"])
    ug = mm(n("up_g"), h2, w["up"], "nn", blocks=(0, 2))
    uv = mm(n("up_v"), h2, w["up"], "nn", blocks=(2, 2))
    act = tile_fwd(n("ffn_act"), _ffn_act_fn, (D_FF // LANES,), ffn_ops(ug, uv, w["ffn_conv_w"]), [_col_out(s, D_FF, BF16)])[0]
    x2 = mm(n("down"), act, w["down"], "nn", add=x1)
    h3 = rms_fwd(n("rms_ple"), x2, w["g_ple"])
    gpre = mm(n("ple_gate"), h3, w["pg"], "nn")
    pe = mm(n("ple_emb"), p, w["ple"], "nn", blocks=(0, N_CHIPS))
    x3 = tile_fwd(n("ple"), _ple_fn, (s // 256,), ple_ops(gpre, pe, x2), [((s, D_MODEL), F32, (256, D_MODEL), lambda i: (i, 0), ())])[0]
    saved = dict(x=x, h=h, pm=pm, ps=ps, qn=qn, kn=kn, f_t=f_t, cum_c=cum_c, cum_r=cum_r, ys=ys, dn_act=dn_act, dn_local=dn_local,
                 a_rows=a_rows, hist=hist, yp=yp, merged=merged, x1=x1, h2=h2, ug=ug, uv=uv, act=act, x2=x2, h3=h3,
                 gpre=gpre, pe=pe, p=p)
    return x3, saved, w


def hang_on(w, token):
    zero = token[0, 0]
    small = ("g_mix", "g_ffn", "g_ple", "gq", "gk", "b_f", "ad", "dn_gain", "sc_conv_w", "dn_conv_w", "ffn_conv_w")
    return {**w, **{k: w[k] + zero for k in small}}


def layer_bwd(li, dx3, sv, w, hooks=None):
    hooks = hooks or {}

    def stage(key, after, w):
        return hang_on(w, hooks[key](after, g)) if key in hooks else w

    s = dx3.shape[0]
    n = lambda t: f"{t}_l{li}"
    g = {}
    col_own = lambda width: ((s, width), (s, LANES), lambda j: (0, j))
    d_gpre, d_pe = tile_bwd(n("ple_bwd"), _ple_fn, (s // 256,), ple_ops(sv["gpre"], sv["pe"], sv["x2"]), [_rows(dx3)],
                            [(0, (), None, BF16), (1, (), None, BF16)])
    g["w_ple"] = mm(n("d_w_ple"), sv["p"], d_pe, "tn", blocks=(0, N_CHIPS))
    g["w_ple_gate"] = mm(n("d_w_pg"), sv["h3"], d_gpre, "tn").reshape(N_CHIPS, -1, D_MODEL)
    dh3 = mm(n("d_h3"), d_gpre, w["pg"], "nt")
    dx2, d_g_ple = rms_bwd(n("rms_ple_bwd"), sv["x2"], w["g_ple"], dh3, dx3)
    dact = mm(n("d_act"), dx2, w["down"], "nt")
    g["w_down"] = mm(n("d_w_down"), sv["act"], dx2, "tn").reshape(N_CHIPS, -1, D_MODEL)
    taps_own = ((w["ffn_conv_w"].shape[0], D_FF), (w["ffn_conv_w"].shape[0], LANES), lambda j: (0, j))
    d_ug, d_uv, d_fw_g, d_fw_v = tile_bwd(n("ffn_act_bwd"), _ffn_act_fn, (D_FF // LANES,), ffn_ops(sv["ug"], sv["uv"], w["ffn_conv_w"]),
                                          [_col_cot(dact)], [(0, (), None, BF16), (1, (), None, BF16), (2, (), taps_own), (3, (), taps_own)])
    g["ffn_conv_w"] = jnp.concatenate([d_fw_g, d_fw_v], axis=1)
    gate_half = mm(n("d_w_up_g"), sv["h2"], d_ug, "tn", blocks=(0, 2), into=(N_CHIPS, 0))
    g["w_up"] = mm(n("d_w_up_v"), sv["h2"], d_uv, "tn", blocks=(0, 2), into=(N_CHIPS, 2, gate_half))
    dh2 = mm(n("d_h2_v"), d_uv, w["up"], "nt", blocks=(2, 2), add=mm(n("d_h2_g"), d_ug, w["up"], "nt", blocks=(0, 2)))
    dx1, d_g_ffn = rms_bwd(n("rms_ffn_bwd"), sv["x1"], w["g_ffn"], dh2, dx2)
    w = stage("mid", dx1, w)
    dmerged = mm(n("d_merged"), dx1, w["o"], "nt")
    g["w_o"] = mm(n("d_w_o"), sv["merged"], dx1, "tn").reshape(N_CHIPS, -1, D_MODEL)
    gate_own = ((s, D_MODEL), (256, D_MODEL), lambda i: (i, 0))
    d_yp0, d_yp1, d_yp2, d_g0, d_g1, d_g2 = tile_bwd(
        n("merge_bwd"), _merge_fn, (s // 256,), merge_ops(sv["yp"], sv["pm"]), [_rows(dmerged)],
        [(0, (), None, BF16), (1, (), None, BF16), (2, (), None, BF16), (3, (), gate_own, BF16), (4, (), gate_own, BF16), (5, (), gate_own, BF16)])
    d_yp = (d_yp0, d_yp1, d_yp2)
    g["w_branch"] = jnp.concatenate([mm(n(f"d_w_branch{b}"), sv["ys"][b], d_yp[b], "tn", blocks=(0, N_CHIPS)) for b in range(3)], axis=1)
    d_ys = [mm(n(f"d_y{b}"), d_yp[b], w["branch"][b], "nt", blocks=(0, N_CHIPS)) for b in range(3)]
    w = stage("late", d_ys[2], w)
    *d_local, d_z, d_dngain = dn_scan_bwd(n("dn_scan_bwd"), sv["dn_local"], sv["pm"], w["dn_gain"], sv["hist"], d_ys[2])
    d_dnact, d_ps_dn, d_arows, d_ad = dn_local_bwd(n("dn_local_bwd"), sv["dn_act"], sv["ps"], sv["a_rows"], w["ad"], d_local)
    g["ad"], g["dn_norm_gain"] = d_ad, d_dngain[0]
    d_dnqkv, g["dn_conv_w"] = tile_bwd(n("dnconv_bwd"), _dnconv_fn, (3 * BRANCH // LANES,), dnconv_ops(sv["pm"], w["dn_conv_w"]),
                                       [_col_cot(d_dnact)], [(0, (), col_own(3 * BRANCH), BF16), (1, ())])
    d_sb, d_sc, d_sv, g["sc_conv_w"] = tile_bwd(n("sconv_bwd"), _sconv_fn, (BRANCH // LANES,), sconv_ops(sv["pm"], w["sc_conv_w"]), [_col_cot(d_ys[1])],
                                                [(0, (), col_own(BRANCH), BF16), (1, (), col_own(BRANCH), BF16), (2, (), col_own(BRANCH), BF16), (3, ())])
    w = stage("last", d_dnqkv, w)
    d_qn, d_kn, d_fv, d_cum = fox_attn_bwd(n("fox_attn_bwd"), sv["qn"], sv["kn"], sv["pm"], sv["cum_c"], sv["cum_r"], d_ys[0])
    d_ft, d_bf = fox_gate_bwd(n("fox_gate_bwd"), sv["f_t"], w["b_f"], d_cum.reshape(8, s // LANES, LANES))
    g["b_fox_f"] = d_bf.reshape(8)
    d_fq, d_fk, d_gq, d_gk = fox_prep_bwd(n("fox_prep_bwd"), sv["pm"], w["gq"], w["gk"], d_qn, d_kn)
    g["fox_q_gain"] = d_gq[0, :FOX_DH] + d_gq[0, FOX_DH:]
    g["fox_k_gain"] = d_gk[0, :FOX_DH] + d_gk[0, FOX_DH:]
    d_pm = jnp.concatenate([d_fq, d_fk, d_fv.astype(BF16), d_sb, d_sc, d_sv, d_dnqkv, d_z, d_g0, d_g1, d_g2], axis=1)
    d_a_cols = d_arows.transpose(0, 2, 1).reshape(s, DN_HEADS)
    d_f_cols = d_ft.reshape(8, s).T
    d_ps = d_ps_dn + jnp.concatenate([d_f_cols, jnp.zeros((s, 4), F32), d_a_cols, jnp.zeros((s, LANES - 16), F32)], axis=1)
    g["w_in"] = chip_blocks_w_in(mm(n("d_w_in_main"), d_pm, sv["h"], "tn"), mm(n("d_w_in_small"), d_ps, sv["h"], "tn"))
    w = stage("w_in", g["w_in"], w)
    dh = mm(n("d_h_small"), d_ps, w["in_small"], "nt", add=mm(n("d_h_main"), d_pm, w["in_main"], "nt"))
    dx, d_g_mix = rms_bwd(n("rms_mix_bwd"), sv["x"], w["g_mix"], dh, dx1)
    g["g_mix"], g["g_ffn"], g["g_ple"] = d_g_mix[0], d_g_ffn[0], d_g_ple[0]
    return dx, g


IN_SHARD = 2052
MAIN_RANGES = ((0, 1536), (1544, 3080), (3080, 4616), (4624, 5136), (5136, 8208))
SMALL_RANGES = ((1536, 1544), (4616, 4620), (4620, 4624))


def _from_chip_blocks(blocks, ranges):
    parts = []
    for lo, hi in ranges:
        for k in range(N_CHIPS):
            a0, a1 = max(lo, k * IN_SHARD), min(hi, (k + 1) * IN_SHARD)
            if a0 < a1:
                parts.append(blocks[k][:, a0 - k * IN_SHARD:a1 - k * IN_SHARD])
    return parts


def split_w_in(blocks):
    main = jnp.concatenate(_from_chip_blocks(blocks, MAIN_RANGES), axis=1)
    pad = jnp.zeros((blocks.shape[1], LANES - 16), blocks.dtype)
    return main, jnp.concatenate(_from_chip_blocks(blocks, SMALL_RANGES) + [pad], axis=1)


def chip_blocks_w_in(main, small):
    ranges = sorted([(lo, hi, "m") for lo, hi in MAIN_RANGES] + [(lo, hi, "s") for lo, hi in SMALL_RANGES])
    offs, m_off, s_off = {}, 0, 0
    for lo, hi in MAIN_RANGES:
        offs[lo] = m_off
        m_off += hi - lo
    for lo, hi in SMALL_RANGES:
        offs[lo] = s_off
        s_off += hi - lo
    blocks = []
    for k in range(N_CHIPS):
        parts = []
        for lo, hi, src in ranges:
            a0, a1 = max(lo, k * IN_SHARD), min(hi, (k + 1) * IN_SHARD)
            if a0 < a1:
                arr = main if src == "m" else small
                parts.append(arr[offs[lo] + a0 - lo:offs[lo] + a1 - lo])
        blocks.append(jnp.concatenate(parts, axis=0))
    return jnp.stack(blocks)


def later_weights(got):
    g_branch, g_o, g_up, g_down, g_pg, g_ple = got
    branch = g_branch.reshape(N_CHIPS, 3, BRANCH, -1)
    return dict(branch=[branch[:, b] for b in range(3)], o=g_o.reshape(D_MODEL, D_MODEL), up=g_up,
                down=g_down.reshape(D_FF, D_MODEL), pg=g_pg.reshape(D_MODEL, D_MODEL), ple=g_ple)


def layer_weights(li, got, conv, a):
    main, small = split_w_in(got[0])
    tile2 = lambda v: jnp.concatenate([v, v])[None, :]
    rest = later_weights(got[1:]) if len(got) > 1 else {}
    return dict(
        in_main=main, in_small=small, **rest,
        g_mix=a["g_mix"][li][None, :], g_ffn=a["g_ffn"][li][None, :], g_ple=a["g_ple"][li][None, :],
        gq=tile2(a["fox_q_gain"][li]), gk=tile2(a["fox_k_gain"][li]), b_f=a["b_fox_f"][li].reshape(8, 1, 1),
        ad=jnp.stack([a["dn_a_log"][li], a["dn_dt_bias"][li]]), dn_gain=a["dn_norm_gain"][li][None, :],
        sc_conv_w=conv["sc_conv_w"][li], dn_conv_w=conv["dn_conv_w"][li], ffn_conv_w=conv["ffn_conv_w"][li])


def pack_rows(arrs, dtype):
    flat = jnp.concatenate([t.reshape(-1).astype(dtype) for t in arrs])
    pad = (-flat.shape[0]) % (8 * LANES)
    if pad:
        flat = jnp.concatenate([flat, jnp.zeros((pad,), dtype)])
    return flat.reshape(-1, LANES)


def unpack_rows(buf, shapes):
    flat = buf.reshape(-1)
    out, off = [], 0
    for shp in shapes:
        size = 1
        for dim in shp:
            size *= dim
        out.append(flat[off:off + size].reshape(shp))
        off += size
    return out


ANY = pl.BlockSpec(memory_space=pl.ANY)


def _position():
    x, y, c = lax.axis_index("x"), lax.axis_index("y"), lax.axis_index("c")
    return x, y, c, [(1 - x, y), (x, 1 - y), (1 - x, 1 - y)]


def gather_small(name, block):
    m_per, n = block.shape

    def body(x_ref, out_ref, token, send_sems, recv_sems, local_sem):
        token[...] = jnp.zeros_like(token)
        x, y, c, chips = _position()
        me, sibling = (x, y, c), (x, y, 1 - c)

        def rows(px, py, pc):
            return out_ref.at[pl.ds((4 * px + 2 * py + pc) * m_per, m_per), :]

        def copy(k, blk, to, src=None):
            return pltpu.make_async_remote_copy(src_ref=rows(*blk) if src is None else src, dst_ref=rows(*blk),
                                                send_sem=send_sems.at[k], recv_sem=recv_sems.at[k], device_id=to, device_id_type=MESH)

        mine = pltpu.make_async_copy(x_ref, rows(*me), local_sem)
        mine.start()
        first = [copy(0, me, sibling, src=x_ref)] + [copy(1 + j, me, (*chip, c), src=x_ref) for j, chip in enumerate(chips)]
        for cp in first:
            cp.start()
        passed = [copy(4 + j, (*chip, c), sibling) for j, chip in enumerate(chips)]
        for j, chip in enumerate(chips):
            copy(1 + j, (*chip, c), me).wait_recv()
            passed[j].start()
        copy(0, sibling, me).wait_recv()
        for j, chip in enumerate(chips):
            copy(4 + j, (*chip, 1 - c), me).wait_recv()
        for cp in first + passed:
            cp.wait_send()
        mine.wait()

    in_vmem = pl.BlockSpec(memory_space=pltpu.VMEM)
    return pl.pallas_call(
        body, out_shape=[jax.ShapeDtypeStruct((8 * m_per, n), block.dtype), jax.ShapeDtypeStruct((8, LANES), F32)],
        in_specs=[in_vmem], out_specs=[in_vmem, in_vmem],
        scratch_shapes=[pltpu.SemaphoreType.DMA((7,)), pltpu.SemaphoreType.DMA((7,)), pltpu.SemaphoreType.DMA],
        name=name, compiler_params=pltpu.CompilerParams(vmem_limit_bytes=VMEM_LIMIT),
    )(block)


def _sems(n):
    return [pltpu.SemaphoreType.DMA((n,)), pltpu.SemaphoreType.DMA((n,))]


def _split_cols(rows):
    return (rows // 2) % 16 != 0


def _half(ref, which, lead=()):
    rows, cols = ref.shape[-2:]
    if _split_cols(rows):
        return ref.at[(*lead, slice(None), pl.ds(which * (cols // 2), cols // 2))]
    return ref.at[(*lead, pl.ds(which * (rows // 2), rows // 2), slice(None))]


def _half_shape(rows, cols):
    return (rows, cols // 2) if _split_cols(rows) else (rows // 2, cols)


def forward_halves(name, lands):
    n_w = len(lands)

    def body(*refs):
        outs = refs[n_w:2 * n_w]
        send_sems, recv_sems = refs[2 * n_w:]
        x, y, c, chips = _position()

        def copy(w, j, pc):
            cx, cy = chips[j]
            part = _half(outs[w], pc, (2 * cx + cy,))
            return pltpu.make_async_remote_copy(src_ref=part, dst_ref=part, send_sem=send_sems.at[3 * w + j], recv_sem=recv_sems.at[3 * w + j],
                                                device_id=(x, y, 1 - c), device_id_type=MESH)

        pairs = [(w, j) for w in range(n_w) for j in range(3)]
        for w, j in pairs:
            copy(w, j, c).start()
        for w, j in pairs:
            copy(w, j, 1 - c).wait_recv()
            copy(w, j, c).wait_send()

    return pl.pallas_call(
        body, out_shape=[jax.ShapeDtypeStruct(t.shape, t.dtype) for t in lands], in_specs=[ANY] * n_w, out_specs=[ANY] * n_w,
        input_output_aliases={w: w for w in range(n_w)}, scratch_shapes=_sems(3 * n_w), name=name,
    )(*lands)


def share_halves(name, bufs):
    n_w = len(bufs)

    def body(*refs):
        outs = refs[n_w:2 * n_w]
        send_sems, recv_sems = refs[2 * n_w:]
        x, y, c, _ = _position()

        def copy(w, pc):
            half = _half(outs[w], pc)
            return pltpu.make_async_remote_copy(src_ref=half, dst_ref=half, send_sem=send_sems.at[w], recv_sem=recv_sems.at[w],
                                                device_id=(x, y, 1 - c), device_id_type=MESH)

        for w in range(n_w):
            copy(w, c).start()
        for w in range(n_w):
            copy(w, 1 - c).wait_recv()
            copy(w, c).wait_send()

    return pl.pallas_call(
        body, out_shape=[jax.ShapeDtypeStruct(b.shape, b.dtype) for b in bufs], in_specs=[ANY] * n_w, out_specs=[ANY] * n_w,
        input_output_aliases={w: w for w in range(n_w)}, scratch_shapes=_sems(n_w), name=name,
    )(*bufs)


HBM = pl.BlockSpec(memory_space=pltpu.HBM)
SEM = pl.BlockSpec(memory_space=pltpu.SEMAPHORE)
EFFECT = pltpu.SideEffectType.DATAFLOW_SIDE_EFFECTING


def _exchange_copies(kind, srcs, lands):
    x, y, c, chips = _position()
    out = []
    for src, land in zip(srcs, lands):
        if kind == "swap":
            out.append((_half(src, 1 - c, (slice(None),)), land, (x, y, 1 - c)))
            continue
        for j, (cx, cy) in enumerate(chips):
            if kind == "gather":
                out.append((src, land.at[2 * x + y], (cx, cy, c)))
            elif kind == "gather_half":
                out.append((_half(src, c), _half(land, c, (2 * x + y,)), (cx, cy, c)))
            else:
                out.append((src.at[2 * cx + cy], land.at[j], (cx, cy, c)))
    return out


def _land_shapes(kind, srcs):
    if kind in ("gather", "gather_half"):
        return [(N_CHIPS,) + s.shape for s in srcs]
    if kind == "swap":
        return [(N_CHIPS,) + _half_shape(*s.shape[1:]) for s in srcs]
    return [(3,) + s.shape[1:] for s in srcs]


def exchange_start(name, kind, srcs):
    n_w = len(srcs)
    shapes = _land_shapes(kind, srcs)
    n_sem = n_w if kind == "swap" else 3 * n_w

    def body(*refs):
        ins, lands = refs[:n_w], refs[n_w:2 * n_w]
        send_sems, recv_sems = refs[2 * n_w:2 * n_w + 2]
        token = refs[-1]
        for i, (src, dst, dev) in enumerate(_exchange_copies(kind, ins, lands)):
            pltpu.make_async_remote_copy(src_ref=src, dst_ref=dst, send_sem=send_sems.at[i], recv_sem=recv_sems.at[i],
                                         device_id=dev, device_id_type=MESH).start()
        token[...] = jnp.zeros_like(token)

    out = pl.pallas_call(
        body, name=name,
        out_shape=(pltpu.SemaphoreType.DMA((n_sem,)), pltpu.SemaphoreType.DMA((n_sem,)),
                   *[pltpu.HBM(s.shape, s.dtype) for s in srcs], *[pltpu.HBM(shp, s.dtype) for shp, s in zip(shapes, srcs)],
                   jax.ShapeDtypeStruct((8, LANES), F32)),
        in_specs=(HBM,) * (2 * n_w), out_specs=(SEM, SEM) + (HBM,) * (2 * n_w) + (pl.BlockSpec(memory_space=pltpu.VMEM),),
        input_output_aliases={i: 2 + i for i in range(2 * n_w)},
        compiler_params=pltpu.CompilerParams(has_side_effects=EFFECT),
    )(*[pltpu.with_memory_space_constraint(s, pltpu.HBM) for s in srcs],
      *[pltpu.with_memory_space_constraint(lax.empty(shp, s.dtype), pltpu.HBM) for shp, s in zip(shapes, srcs)])
    return (kind, n_w, out[:-1]), out[-1]


def exchange_wait(name, handle, after):
    kind, n_w, (send_sems, recv_sems, *thru) = handle

    def body(*refs):
        ins, lands = refs[:n_w], refs[n_w:2 * n_w]
        send_sems, recv_sems = refs[2 * n_w:2 * n_w + 2]
        for i, (src, dst, dev) in enumerate(_exchange_copies(kind, ins, lands)):
            cp = pltpu.make_async_remote_copy(src_ref=src, dst_ref=dst, send_sem=send_sems.at[i], recv_sem=recv_sems.at[i],
                                              device_id=dev, device_id_type=MESH)
            cp.wait_send()
            cp.wait_recv()

    out = pl.pallas_call(
        body, name=name, out_shape=tuple(pltpu.HBM(t.shape, t.dtype) for t in thru),
        in_specs=(HBM,) * (2 * n_w) + (SEM, SEM, pl.BlockSpec(memory_space=pl.ANY)), out_specs=(HBM,) * (2 * n_w),
        input_output_aliases={i: i for i in range(2 * n_w)},
        compiler_params=pltpu.CompilerParams(has_side_effects=EFFECT),
    )(*thru, send_sems, recv_sems, after)
    return list(out[:n_w]), list(out[n_w:])


def _row_tile(rows, cols):
    best = rows
    if rows * cols * 4 <= 2 * 1024 * 1024:
        return rows
    for t in range(16, rows, 16):
        if rows % t == 0 and t * cols * 4 <= 2 * 1024 * 1024:
            best = t
    return best


def pair_sum(name, pos, grad, from_sibling):
    _, rows, cols = grad.shape
    h_rows, h_cols = _half_shape(rows, cols)
    tr = _row_tile(h_rows, h_cols)
    n_t = h_rows // tr

    def body(pos_ref, g_ref, s_ref, b_ref, f_ref):
        tot = g_ref[...] + s_ref[...]
        b_ref[...] = tot.astype(BF16)

        @pl.when(pl.program_id(1) == pos_ref[1])
        def _():
            f_ref[...] = tot[0]

    blk = pl.BlockSpec((1, tr, h_cols), lambda i, k, pos: (k, i, 0))
    if _split_cols(rows):
        mine = pl.BlockSpec((1, tr, h_cols), lambda i, k, pos: (k, i, pos[0]))
    else:
        mine = pl.BlockSpec((1, tr, h_cols), lambda i, k, pos: (k, pos[0] * n_t + i, 0))
    return pl.pallas_call(
        body, grid_spec=pltpu.PrefetchScalarGridSpec(
            num_scalar_prefetch=1, grid=(n_t, N_CHIPS), in_specs=[mine, blk],
            out_specs=[blk, pl.BlockSpec((tr, h_cols), lambda i, k, pos: (i, 0))]),
        out_shape=[jax.ShapeDtypeStruct((N_CHIPS, h_rows, h_cols), BF16), jax.ShapeDtypeStruct((h_rows, h_cols), F32)],
        name=name, compiler_params=_cparams(2),
    )(pos, grad, from_sibling)


def chip_sum(name, pos, own, landed, split_cols):
    half, cols = own.shape
    tr = _row_tile(half, cols)
    n_t = half // tr

    def body(pos_ref, p_ref, l_ref, o_ref):
        o_ref[...] = ((p_ref[...] + l_ref[0].astype(F32)) + l_ref[1].astype(F32)) + l_ref[2].astype(F32)

    if split_cols:
        out_spec, out_shape = pl.BlockSpec((tr, cols), lambda i, pos: (i, pos[0])), (half, 2 * cols)
    else:
        out_spec, out_shape = pl.BlockSpec((tr, cols), lambda i, pos: (pos[0] * n_t + i, 0)), (2 * half, cols)
    return pl.pallas_call(
        body, grid_spec=pltpu.PrefetchScalarGridSpec(
            num_scalar_prefetch=1, grid=(n_t,),
            in_specs=[pl.BlockSpec((tr, cols), lambda i, pos: (i, 0)), pl.BlockSpec((3, tr, cols), lambda i, pos: (0, i, 0))],
            out_specs=out_spec),
        out_shape=jax.ShapeDtypeStruct(out_shape, F32), name=name, compiler_params=_cparams(1),
    )(pos, own, landed)


class OverlappedReduceScatter:
    def __init__(self, tag, pos, grads):
        self.n = lambda t: f"{t}_{tag}"
        self.pos, self.grads = pos, grads
        self.swap, self.token = exchange_start(self.n("swap_start"), "swap", grads)

    def middle(self, after):
        self.grads, from_sibling = exchange_wait(self.n("swap_wait"), self.swap, after)
        self.sums = [pair_sum(self.n(f"pair_sum{w}"), self.pos, g, s) for w, (g, s) in enumerate(zip(self.grads, from_sibling))]
        self.scatter, self.token = exchange_start(self.n("scatter_start"), "scatter", [b for b, _ in self.sums])

    def finish(self, after):
        _, landed = exchange_wait(self.n("scatter_wait"), self.scatter, after)
        halves = [chip_sum(self.n(f"chip_sum{w}"), self.pos, own, l, _split_cols(g.shape[1]))
                  for w, ((_, own), l, g) in enumerate(zip(self.sums, landed, self.grads))]
        return share_halves(self.n("share_halves"), halves)


def sum_devices(gathered):
    m_per = gathered.shape[0] // 8

    def body(g_ref, o_ref):
        tot = g_ref[pl.ds(0, m_per), :]
        for dev in range(1, 8):
            tot = tot + g_ref[pl.ds(dev * m_per, m_per), :]
        o_ref[...] = tot

    return pl.pallas_call(
        body, out_shape=jax.ShapeDtypeStruct((m_per, gathered.shape[1]), F32),
        in_specs=[pl.BlockSpec(memory_space=pltpu.VMEM)], out_specs=pl.BlockSpec(memory_space=pltpu.VMEM), name="sum_devices",
    )(gathered)


def kernel(x, p, g_mix, w_in, b_fox_f, fox_q_gain, fox_k_gain, sc_conv_w, dn_conv_w, dn_a_log, dn_dt_bias, dn_norm_gain, w_branch, w_o, g_ffn, w_up, ffn_conv_w, w_down, g_ple, w_ple_gate, w_ple, loss_target, m_g_mix, m_w_in, m_b_fox_f, m_fox_q_gain, m_fox_k_gain, m_sc_conv_w, m_dn_conv_w, m_dn_a_log, m_dn_dt_bias, m_dn_norm_gain, m_w_branch, m_w_o, m_g_ffn, m_w_up, m_ffn_conv_w, m_w_down, m_g_ple, m_w_ple_gate, m_w_ple, v_g_mix, v_w_in, v_b_fox_f, v_fox_q_gain, v_fox_k_gain, v_sc_conv_w, v_dn_conv_w, v_dn_a_log, v_dn_dt_bias, v_dn_norm_gain, v_w_branch, v_w_o, v_g_ffn, v_w_up, v_ffn_conv_w, v_w_down, v_g_ple, v_w_ple_gate, v_w_ple):
    a = dict(g_mix=g_mix, w_in=w_in, b_fox_f=b_fox_f, fox_q_gain=fox_q_gain, fox_k_gain=fox_k_gain, sc_conv_w=sc_conv_w,
             dn_conv_w=dn_conv_w, dn_a_log=dn_a_log, dn_dt_bias=dn_dt_bias, dn_norm_gain=dn_norm_gain, w_branch=w_branch, w_o=w_o,
             g_ffn=g_ffn, w_up=w_up, ffn_conv_w=ffn_conv_w, w_down=w_down, g_ple=g_ple, w_ple_gate=w_ple_gate, w_ple=w_ple)
    mom = dict(g_mix=m_g_mix, w_in=m_w_in, b_fox_f=m_b_fox_f, fox_q_gain=m_fox_q_gain, fox_k_gain=m_fox_k_gain, sc_conv_w=m_sc_conv_w,
               dn_conv_w=m_dn_conv_w, dn_a_log=m_dn_a_log, dn_dt_bias=m_dn_dt_bias, dn_norm_gain=m_dn_norm_gain, w_branch=m_w_branch,
               w_o=m_w_o, g_ffn=m_g_ffn, w_up=m_w_up, ffn_conv_w=m_ffn_conv_w, w_down=m_w_down, g_ple=m_g_ple, w_ple_gate=m_w_ple_gate,
               w_ple=m_w_ple)
    var = dict(g_mix=v_g_mix, w_in=v_w_in, b_fox_f=v_b_fox_f, fox_q_gain=v_fox_q_gain, fox_k_gain=v_fox_k_gain, sc_conv_w=v_sc_conv_w,
               dn_conv_w=v_dn_conv_w, dn_a_log=v_dn_a_log, dn_dt_bias=v_dn_dt_bias, dn_norm_gain=v_dn_norm_gain, w_branch=v_w_branch,
               w_o=v_w_o, g_ffn=v_g_ffn, w_up=v_w_up, ffn_conv_w=v_ffn_conv_w, w_down=v_w_down, g_ple=v_g_ple, w_ple_gate=v_w_ple_gate,
               w_ple=v_w_ple)
    cx, cy, cc = lax.axis_index("x"), lax.axis_index("y"), lax.axis_index("c")
    chip = 2 * cx + cy
    pos = jnp.stack([cc, chip]).astype(jnp.int32)

    def as_blocks(t):
        return t.reshape(2, -1, t.shape[-1])

    def own_block_in(got, shards):
        return [lax.dynamic_update_slice(g, s[None], (chip, 0, 0)) for g, s in zip(got, shards)]

    conv_shapes = [a[nm].shape for nm in CONVS]
    conv_all, conv_token = gather_small("gather_conv_w", pack_rows([a[nm] for nm in CONVS], F32))
    def w_in_block(li, token):
        stored = jnp.transpose(a["w_in"], (2, 0, 1))[:, li, :]
        return (stored + token[0, 0]).astype(BF16).T

    w_in0 = [w_in_block(0, conv_token)]
    gather_in0, gather_in0_token = exchange_start("gather_start_w_in_l0", "gather_half", w_in0)
    shards0 = w_in0 + [(as_blocks(a[nm])[0] + gather_in0_token[0, 0]).astype(BF16) for nm in BIG[1:]]
    gather0, gather0_token = exchange_start("gather_start_l0", "gather", shards0[1:])
    shards1 = [w_in_block(1, gather0_token)] + [(as_blocks(a[nm])[1] + gather0_token[0, 0]).astype(BF16) for nm in BIG[1:]]
    gather1, gather1_in_token = exchange_start("gather_start_w_in_l1", "gather", shards1[:1])
    shards1[1:] = [s + gather1_in_token[0, 0].astype(BF16) for s in shards1[1:]]
    gather1_rest, gather1_token = exchange_start("gather_start_l1", "gather", shards1[1:])
    conv_rows = conv_all.shape[0] // 8
    conv_chip = [unpack_rows(conv_all[2 * k * conv_rows:(2 * k + 1) * conv_rows], conv_shapes) for k in range(N_CHIPS)]
    conv = {nm: jnp.concatenate([conv_chip[k][i] for k in range(N_CHIPS)], axis=2) for i, nm in enumerate(CONVS)}

    weights, saved = [None, None], [None, None]
    mine_in0, got_in0 = exchange_wait("gather_wait_w_in_l0", gather_in0, gather1_token)
    got_in0 = forward_halves("forward_w_in_l0", got_in0)
    first_weights = hang_on(layer_weights(0, own_block_in(got_in0, mine_in0), conv, a), gather1_token)

    def rest_of_layer0(after):
        mine, got = exchange_wait("gather_wait_l0", gather0, after)
        return later_weights(own_block_in(got, mine))

    act, saved[0], weights[0] = layer_fwd(0, x[0], p[0, 0], first_weights, more_weights=rest_of_layer0)
    mine1, got1 = exchange_wait("gather_wait_w_in_l1", gather1, act)

    def rest_of_layer1(after):
        mine, got = exchange_wait("gather_wait_l1", gather1_rest, after)
        return later_weights(own_block_in(got, mine))

    act, saved[1], weights[1] = layer_fwd(1, act, p[1, 0], layer_weights(1, own_block_in(got1, mine1), conv, a),
                                          more_weights=rest_of_layer1)
    d_act, loss_part = loss_call(act, loss_target[0])
    loss = lax.psum(loss_part, ("x", "y", "c"))
    layer_grads = [None, None]
    d_act, layer_grads[1] = layer_bwd(1, d_act, saved[1], weights[1])
    rs1 = OverlappedReduceScatter("l1", pos, [layer_grads[1][nm] for nm in BIG])
    rs0 = []

    def stage_mid(after, g):
        rs1.middle(after)
        return rs1.token

    def stage_late(after, g):
        rs0.append(OverlappedReduceScatter("l0", pos, [g[nm] for nm in BIG[1:]]))
        return rs0[0].token

    def stage_last(after, g):
        rs0[0].middle(after)
        return rs0[0].token

    def stage_w_in(after, g):
        rs0.append(OverlappedReduceScatter("w_in_l0", pos, [g["w_in"]]))
        return rs0[1].token

    d_act, layer_grads[0] = layer_bwd(0, d_act, saved[0], hang_on(weights[0], rs1.token),
                                      hooks=dict(mid=stage_mid, late=stage_late, last=stage_last, w_in=stage_w_in))
    rs0[1].middle(d_act)
    reduced = [rs0[0].finish(rs0[1].token), rs1.finish(rs0[1].token)]
    grad_x = d_act[None]

    def both(nm):
        return jnp.stack([layer_grads[0][nm], layer_grads[1][nm]])

    local = {nm: both(nm) for nm in ("g_mix", "b_fox_f", "fox_q_gain", "fox_k_gain", "dn_norm_gain", "g_ffn", "g_ple", "sc_conv_w",
                                      "dn_conv_w", "ffn_conv_w")}
    local["dn_a_log"] = jnp.stack([layer_grads[li]["ad"][0] for li in range(2)])
    local["dn_dt_bias"] = jnp.stack([layer_grads[li]["ad"][1] for li in range(2)])

    small_names = SMALL + CONVS
    small_shapes = [local[nm].shape for nm in small_names]
    small_sum = sum_devices(gather_small("gather_small_grads", pack_rows([local[nm] for nm in small_names], F32))[0])
    small_grads = dict(zip(small_names, unpack_rows(small_sum, small_shapes)))
    for nm in CONVS:
        width = a[nm].shape[2]
        small_grads[nm] = lax.dynamic_slice_in_dim(small_grads[nm], chip * width, width, axis=2)

    grads, deltas, new_m, new_v = dict(small_grads), {}, {}, {}
    for nm in small_names:
        deltas[nm], new_m[nm], new_v[nm] = adam_call(f"adam_{nm}", a[nm], grads[nm], mom[nm], var[nm])
    for i, nm in enumerate(BIG[1:]):
        res = adam_layers(f"adam_{nm}", as_blocks(a[nm]), as_blocks(mom[nm]), as_blocks(var[nm]), reduced[0][i], reduced[1][1 + i])
        grads[nm], deltas[nm], new_m[nm], new_v[nm] = [r.reshape(a[nm].shape) for r in res]
    stored = lambda t: jnp.transpose(t, (2, 0, 1))
    res = adam_w_in("adam_w_in", stored(a["w_in"]), stored(mom["w_in"]), stored(var["w_in"]), rs0[1].finish(deltas["w_ple"])[0], reduced[1][0])
    grads["w_in"], deltas["w_in"], new_m["w_in"], new_v["w_in"] = [jnp.transpose(r, (1, 2, 0)) for r in res]
    return (loss, grad_x, *[grads[nm] for nm in WEIGHTS], *[deltas[nm] for nm in WEIGHTS], *[new_m[nm] for nm in WEIGHTS],
            *[new_v[nm] for nm in WEIGHTS])
```

```python
import functools

import jax
import jax.numpy as jnp
from jax import lax
from jax.experimental import pallas as pl
from jax.experimental.pallas import tpu as pltpu

F32 = jnp.float32
BF16 = jnp.bfloat16
HI = lax.Precision.HIGHEST
SOLVE = lax.Precision.HIGH
MESH = pl.DeviceIdType.MESH

D_MODEL = 1024
BRANCH = 512
FOX_DH = 64
DN_DH = 128
DN_HEADS = 4
DN_CHUNK = 64
FOX_BLOCK = 128
D_FF = 2816
EPS = 1e-6
N_CHIPS = 4
LANES = 128

ADAM_LR, ADAM_B1, ADAM_B2, ADAM_EPS, ADAM_WD, ADAM_STEP = 0.001, 0.9, 0.999, 1e-08, 0.01, 10

VMEM_LIMIT = 56 * 1024 * 1024

C_FQ, C_FK, C_FV, C_SB, C_SC, C_SV, C_DN, C_DZ, C_GATE = 0, 512, 1024, 1536, 2048, 2560, 3072, 4608, 5120
IN_MAIN = 8192

BIG = ("w_in", "w_branch", "w_o", "w_up", "w_down", "w_ple_gate", "w_ple")
CONVS = ("sc_conv_w", "dn_conv_w", "ffn_conv_w")
SMALL = ("g_mix", "b_fox_f", "fox_q_gain", "fox_k_gain", "dn_a_log", "dn_dt_bias", "dn_norm_gain", "g_ffn", "g_ple")
WEIGHTS = ("g_mix", "w_in", "b_fox_f", "fox_q_gain", "fox_k_gain", "sc_conv_w", "dn_conv_w", "dn_a_log", "dn_dt_bias",
           "dn_norm_gain", "w_branch", "w_o", "g_ffn", "w_up", "ffn_conv_w", "w_down", "g_ple", "w_ple_gate", "w_ple")


def _iota(shape, dim):
    return lax.broadcasted_iota(jnp.int32, shape, dim)


def _dg(a, b, mode, prec=None):
    dims = {"nn": ((1,), (0,)), "nt": ((1,), (1,)), "tn": ((0,), (0,))}[mode]
    return lax.dot_general(a, b, (dims, ((), ())), precision=prec, preferred_element_type=F32)


def _bdot_impl(a, b, mode):
    return _dg(a.astype(BF16), b.astype(BF16), mode)


@functools.partial(jax.custom_vjp, nondiff_argnums=(2,))
def _bdot_diff(a, b, mode):
    return _bdot_impl(a, b, mode)


def _bdot_fwd(a, b, mode):
    return _bdot_impl(a, b, mode), (a, b)


def _bdot_bwd(mode, res, g):
    a, b = res
    if mode == "nn":
        da, db = _bdot_impl(g, b, "nt"), _bdot_impl(a, g, "tn")
    elif mode == "nt":
        da, db = _bdot_impl(g, b, "nn"), _bdot_impl(g, a, "tn")
    else:
        da, db = _bdot_impl(b, g, "nt"), _bdot_impl(a, g, "nn")
    return da.astype(a.dtype), db.astype(b.dtype)


_bdot_diff.defvjp(_bdot_fwd, _bdot_bwd)


def _bdot(d):
    return _bdot_diff if d else _bdot_impl


def _shift_impl(x, k):
    return jnp.where(_iota(x.shape, 0) >= k, pltpu.roll(x, k, 0), 0.0)


def _unshift_impl(g, k):
    n = g.shape[0]
    return jnp.where(_iota(g.shape, 0) < n - k, pltpu.roll(g, n - k, 0), 0.0)


@functools.partial(jax.custom_vjp, nondiff_argnums=(1,))
def _shift_diff(x, k):
    return _shift_impl(x, k)


_shift_diff.defvjp(lambda x, k: (_shift_impl(x, k), None), lambda k, _, g: (_unshift_impl(g, k),))


def _row(w, j):
    return jnp.sum(jnp.where(_iota(w.shape, 0) == j, w, 0.0), axis=0, keepdims=True)


def _col(w, j):
    return jnp.sum(jnp.where(_iota(w.shape, 1) == j, w, 0.0), axis=1, keepdims=True)


def _conv(d, x, w):
    shift = _shift_diff if d else _shift_impl
    taps = w.shape[0]
    y = x * _row(w, taps - 1)
    for j in range(taps - 1):
        y = y + shift(x, taps - 1 - j) * _row(w, j)
    return y


def _softplus(x):
    return jnp.maximum(x, 0.0) + jnp.log(1.0 + jnp.exp(-jnp.abs(x)))


def _sigmoid(x):
    return 0.5 * (jnp.tanh(0.5 * x) + 1.0)


def _silu(x):
    return x * _sigmoid(x)


def _rms(x, gain):
    return x * lax.rsqrt(jnp.mean(x * x, axis=-1, keepdims=True) + EPS) * gain


def _rms_fn(d, pids, x, gain):
    return (_rms(x, gain),)


def _loss_fn(d, pids, y, t):
    e = y - t
    part = 0.5 / D_MODEL * jnp.sum(e * e, keepdims=True)
    return e * (1.0 / D_MODEL), jnp.broadcast_to(part, (8, LANES))


def _fox_prep_fn(d, pids, q, k, gq, gk):
    first = _iota(q.shape, 1) < FOX_DH

    def norm(x, gain):
        sq = x * x
        ss_a = jnp.sum(jnp.where(first, sq, 0.0), axis=1, keepdims=True)
        ss_b = jnp.sum(jnp.where(first, 0.0, sq), axis=1, keepdims=True)
        rs = jnp.where(first, lax.rsqrt(ss_a / FOX_DH + EPS), lax.rsqrt(ss_b / FOX_DH + EPS))
        return x * rs * gain

    return norm(q, gq) * FOX_DH ** -0.5, norm(k, gk)


def _fox_gate_fn(d, pids, f, bias):
    logf = -_softplus(-(f + bias))
    n_r, n_c = logf.shape
    tri = (_iota((n_c, n_c), 0) <= _iota((n_c, n_c), 1)).astype(F32)
    within = _dg(logf, tri, "nn", HI)
    tot = jnp.broadcast_to(jnp.sum(logf, axis=1, keepdims=True), logf.shape)
    below = (_iota((n_r, n_r), 1) < _iota((n_r, n_r), 0)).astype(F32)
    return (within + _dg(below, tot, "nn", HI),)


def _fox_attn_fn(q_block0, d, pids, q, k, v, cq_a, cq_b, ck_a, ck_b):
    dot = _bdot(d)
    first = _iota(q.shape, 1) < FOX_DH
    n_q, n_k = q.shape[0], k.shape[0]
    causal = ((q_block0 + pids[1]) * n_q + _iota((n_q, n_k), 0)) >= _iota((n_q, n_k), 1)

    qs = [jnp.where(first, q, 0.0), jnp.where(first, 0.0, q)]
    s = _each(lambda qh, cq, ck: jnp.where(causal, dot(qh, k, "nt") + cq - ck, -1e30), qs, [cq_a, cq_b], [ck_a, ck_b])
    e = [jnp.exp(si - lax.stop_gradient(jnp.max(si, axis=1, keepdims=True))) for si in s]
    o_a, o_b = [dot(ei * (1.0 / jnp.sum(ei, axis=1, keepdims=True)), v, "nn") for ei in e]
    return (jnp.where(first, o_a, o_b),)


def _sconv_fn(d, pids, sb, sc, sv, w):
    return (sb * _conv(d, sc * sv, w),)


def _dnconv_fn(d, pids, x, w):
    return (_silu(_conv(d, x, w)),)


def _merge_fn(d, pids, y0, y1, y2, g0, g1, g2):
    return (_sigmoid(g0) * y0 + _sigmoid(g1) * y1 + _sigmoid(g2) * y2,)


def _ffn_act_fn(d, pids, ug, uv, wg, wv):
    return (_silu(_conv(d, ug, wg)) * _conv(d, uv, wv),)


def _ple_fn(d, pids, gpre, pe, x):
    return (x + _sigmoid(gpre) * pe,)


def _adam_fn(d, pids, w, g, m, v):
    m2 = ADAM_B1 * m + (1.0 - ADAM_B1) * g
    v2 = ADAM_B2 * v + (1.0 - ADAM_B2) * (g * g)
    m_hat = m2 * (1.0 / (1.0 - ADAM_B1 ** ADAM_STEP))
    v_hat = v2 * (1.0 / (1.0 - ADAM_B2 ** ADAM_STEP))
    delta = -ADAM_LR * (m_hat / (jnp.sqrt(v_hat) + ADAM_EPS) + ADAM_WD * w)
    return delta, m2, v2


def _each(fn, *lists):
    return [fn(*args) for args in zip(*lists)]


def _tri_inv_impl(mats):
    n = mats[0].shape[0]
    r, c = _iota((n, n), 0), _iota((n, n), 1)
    diag_blk = (r >> 4) == (c >> 4)
    eye = (r == c).astype(F32)
    mm = lambda us, ws: _each(lambda u, w: _dg(u, w, "nn", SOLVE), us, ws)
    grow = lambda ps, xs: _each(lambda p, px: p + px, ps, mm(ps, xs))
    x = [jnp.where(diag_blk, -a, 0.0) for a in mats]
    p = [eye + xi for xi in x]
    x2 = mm(x, x)
    p = grow(p, x2)
    x4 = mm(x2, x2)
    p = grow(p, x4)
    p = grow(p, mm(x4, x4))
    y = [-yi for yi in mm(p, [jnp.where(diag_blk, 0.0, a) for a in mats])]
    q = grow([eye + yi for yi in y], mm(y, y))
    return mm(q, p)


@jax.custom_vjp
def _tri_inv_diff(mats):
    return _tri_inv_impl(mats)


def _tri_inv_fwd(mats):
    ts = _tri_inv_impl(mats)
    return ts, ts


def _tri_inv_bwd(ts, gs):
    left = _each(lambda t, g: _dg(t, g, "tn", SOLVE), ts, gs)
    return ([-m for m in _each(lambda l, t: _dg(l, t, "nt", SOLVE), left, ts)],)


_tri_inv_diff.defvjp(_tri_inv_fwd, _tri_inv_bwd)


def _dn_local(d, qs, ks, vs, a_cs, a_rs, b_cs, a_logs, dt_bs):
    dot = _bdot(d)
    inv = _tri_inv_diff if d else _tri_inv_impl
    n = qs[0].shape[0]
    r, c = _iota((n, n), 0), _iota((n, n), 1)
    incl, strict, upper = r >= c, r > c, r <= c
    qs = [q * lax.rsqrt(jnp.sum(q * q, axis=1, keepdims=True) + EPS) * DN_DH ** -0.5 for q in qs]
    ks = [k * lax.rsqrt(jnp.sum(k * k, axis=1, keepdims=True) + EPS) for k in ks]
    betas = [_sigmoid(b) for b in b_cs]
    rates = [-jnp.exp(a) for a in a_logs]
    g_cs = _each(lambda rate, a, dt: rate * _softplus(a + dt), rates, a_cs, dt_bs)
    g_rs = _each(lambda rate, a, dt: rate * _softplus(a + dt), rates, a_rs, dt_bs)
    gcum_cs = [jnp.sum(jnp.where(incl, g, 0.0), axis=1, keepdims=True) for g in g_rs]
    gcum_rs = [jnp.sum(jnp.where(upper, g, 0.0), axis=0, keepdims=True) for g in g_cs]
    decays = _each(lambda gc, gr: jnp.exp(jnp.where(incl, gc - gr, -1e30)), gcum_cs, gcum_rs)
    kbs = _each(lambda k, b: k * b, ks, betas)
    kk = _each(lambda kb, k: dot(kb, k, "nt"), kbs, ks)
    ts = inv(_each(lambda m, dec: jnp.where(strict, m * dec, 0.0), kk, decays))
    e_gs = [jnp.exp(g) for g in gcum_cs]
    us = _each(lambda t, v, b: _dg(t, v * b, "nn", SOLVE), ts, vs, betas)
    k_cums = _each(lambda t, kb, e: _dg(t, kb * e, "nn", SOLVE), ts, kbs, e_gs)
    qk = _each(lambda q, k: dot(q, k, "nt"), qs, ks)
    qk = _each(lambda m, dec: jnp.where(incl, m * dec, 0.0), qk, decays)
    g_lasts = [jnp.sum(g, axis=0, keepdims=True) for g in g_cs]
    q_decs = _each(lambda q, e: q * e, qs, e_gs)
    k_decs = _each(lambda k, gl, gc: k * jnp.exp(gl - gc), ks, g_lasts, gcum_cs)
    return list(zip(us, k_cums, q_decs, k_decs, qk, g_lasts))


def _dn_step(d, s_prevs, items, zs, gain):
    dot = _bdot(d)
    us, k_cums, q_decs, k_decs, qks, g_lasts = [list(t) for t in zip(*items)]
    v_news = _each(lambda u, kc, s: u - dot(kc, s, "nn"), us, k_cums, s_prevs)
    inter = _each(lambda qd, s: dot(qd, s, "nn"), q_decs, s_prevs)
    outs = _each(lambda o, qk, vn: o + dot(qk, vn, "nn"), inter, qks, v_news)
    s_nexts = _each(lambda s, gl, kd, vn: s * jnp.exp(gl) + dot(kd, vn, "tn"), s_prevs, g_lasts, k_decs, v_news)
    return _each(lambda o, z: _rms(o, gain) * _silu(z), outs, zs), s_nexts


def _split_heads(t):
    return [t[:, h * DN_DH:(h + 1) * DN_DH] for h in range(t.shape[1] // DN_DH)]


def _dn_gates(ps, a_rows, ad):
    hs = range(DN_HEADS)
    return ([_col(ps, 12 + h) for h in hs], [_row(a_rows, h) for h in hs], [_col(ps, 8 + h) for h in hs],
            [_col(_row(ad, 0), h) for h in hs], [_col(_row(ad, 1), h) for h in hs])


def _head_rows(vals):
    row = _iota((8, LANES), 0)
    tile = jnp.zeros((8, LANES), F32)
    for h, val in enumerate(vals):
        tile = tile + jnp.where(row == h, val, 0.0)
    return tile


def _cparams(n_axes):
    return pltpu.CompilerParams(dimension_semantics=("arbitrary",) * n_axes, vmem_limit_bytes=VMEM_LIMIT)


def _first_visit(acc_axes):
    cond = None
    for a in acc_axes:
        here = pl.program_id(a) == 0
        cond = here if cond is None else jnp.logical_and(cond, here)
    return cond


def _tile(ref, widen=False):
    val = ref[...]
    shape = val.shape
    while len(shape) > 2 and shape[0] == 1:
        shape = shape[1:]
    val = val.reshape(shape)
    return val.astype(F32) if widen and val.dtype == BF16 else val


def _store(ref, val, first):
    val = val.astype(ref.dtype).reshape(ref.shape)
    if first is None:
        ref[...] = val
        return

    @pl.when(first)
    def _():
        ref[...] = val

    @pl.when(jnp.logical_not(first))
    def _():
        ref[...] += val


def _specs(ops):
    return [pl.BlockSpec(block, imap) for _, block, imap in ops]


def tile_fwd(name, fn, grid, ins, outs, raw=()):
    n_in = len(ins)

    def body(*refs):
        pids = tuple(pl.program_id(a) for a in range(len(grid)))
        firsts = [_first_visit(o[4]) if o[4] else None for o in outs]
        res = fn(False, pids, *[_tile(r, i not in raw) for i, r in enumerate(refs[:n_in])])
        for ref, val, first in zip(refs[n_in:], res, firsts):
            _store(ref, val, first)

    out = pl.pallas_call(
        body, grid=grid, in_specs=_specs(ins),
        out_specs=[pl.BlockSpec(o[2], o[3]) for o in outs],
        out_shape=[jax.ShapeDtypeStruct(o[0], o[1]) for o in outs],
        name=name, compiler_params=_cparams(len(grid)),
    )(*[a for a, _, _ in ins])
    return out


def tile_bwd(name, fn, grid, ins, cots, diff, adds=None, raw=()):
    adds = adds or {}
    n_in, n_cot = len(ins), len(cots)
    add_pos = sorted(adds)
    diff_idx = [d[0] for d in diff]
    out_desc = [d[2] if len(d) > 2 and d[2] is not None else (ins[d[0]][0].shape, ins[d[0]][1], ins[d[0]][2]) for d in diff]
    out_dtypes = [d[3] if len(d) > 3 else F32 for d in diff]

    def body(*refs):
        pids = tuple(pl.program_id(a) for a in range(len(grid)))
        firsts = [_first_visit(d[1]) if d[1] else None for d in diff]
        vals = [_tile(r, i not in raw) for i, r in enumerate(refs[:n_in])]
        cot_vals = [_tile(r, True) for r in refs[n_in:n_in + n_cot]]
        add_vals = [_tile(r) for r in refs[n_in + n_cot:n_in + n_cot + len(add_pos)]]
        out_refs = refs[n_in + n_cot + len(add_pos):]

        def f(*dv):
            full = list(vals)
            for i, val in zip(diff_idx, dv):
                full[i] = val
            return fn(True, pids, *full)

        prim, vjp = jax.vjp(f, *[vals[i].astype(F32) for i in diff_idx])
        grads = list(vjp(tuple(c.astype(o.dtype) for c, o in zip(cot_vals, prim))))
        for pos, val in zip(add_pos, add_vals):
            extra = val.astype(F32) if firsts[pos] is None else jnp.where(firsts[pos], val.astype(F32), 0.0)
            grads[pos] = grads[pos] + extra
        for ref, val, first in zip(out_refs, grads, firsts):
            _store(ref, val, first)

    all_ins = list(ins) + list(cots) + [adds[p] for p in add_pos]
    out = pl.pallas_call(
        body, grid=grid, in_specs=_specs(all_ins),
        out_specs=[pl.BlockSpec(o[1], o[2]) for o in out_desc],
        out_shape=[jax.ShapeDtypeStruct(o[0], dt) for o, dt in zip(out_desc, out_dtypes)],
        name=name, compiler_params=_cparams(len(grid)),
    )(*[a for a, _, _ in all_ins])
    return out


def _pick(dim, cands):
    for c in cands:
        if dim % c == 0:
            return c
    return dim


MM_VMEM_BUDGET = 40 * 1024 * 1024
MM_TILES = (1024, 512, 1408, 256, 128)


def mm(name, a, b, mode, add=None, out_dtype=F32, blocks=None, into=None):
    wide = None
    if mode == "nn":
        (m, kk), n = a.shape, b.shape[-1]
    elif mode == "nt":
        (m, kk), n = a.shape, b.shape[-2]
    else:
        (kk, m), n = a.shape, b.shape[1]
    if blocks is not None:
        lo, n_blk = blocks
        wide = b.shape[-1] if mode != "tn" else n // n_blk
        if mode == "nn":
            n = wide * n_blk
    tm = _pick(m, MM_TILES)
    if mode == "nt" and blocks is not None:
        tn, tk = _pick(n, MM_TILES), _pick(wide, MM_TILES[:-1])
    elif blocks is not None:
        tn, tk = _pick(wide, MM_TILES[:-1]), _pick(kk, MM_TILES)
    else:
        tn, tk = _pick(n, MM_TILES), _pick(kk, MM_TILES)
    if mode == "tn" or blocks is None:
        tk = _pick(kk, (2048,) + MM_TILES)
    if mode != "tn" and add is None and m % 2048 == 0 and (n // tn) * (kk // tk) > 1:
        windows = 2 * (2048 * tk * a.dtype.itemsize + tk * tn * b.dtype.itemsize + 2048 * tn * jnp.dtype(out_dtype).itemsize)
        if windows + 2048 * tn * 4 <= MM_VMEM_BUDGET:
            tm = 2048
    nk = kk // tk
    a_spec = pl.BlockSpec((tk, tm), lambda i, j, k: (k, i)) if mode == "tn" else pl.BlockSpec((tm, tk), lambda i, j, k: (i, k))
    o_spec = pl.BlockSpec((tm, tn), lambda i, j, k: (i, j))
    out_shape = (m, n)
    if blocks is None:
        b_spec = pl.BlockSpec((tn, tk), lambda i, j, k: (j, k)) if mode == "nt" else pl.BlockSpec((tk, tn), lambda i, j, k: (k, j))
    elif mode == "nn":
        per = wide // tn
        b_spec = pl.BlockSpec((1, tk, tn), lambda i, j, k: (lo + j // per, k, j % per))
    elif mode == "nt":
        per = wide // tk
        b_spec = pl.BlockSpec((1, tn, tk), lambda i, j, k: (lo + k // per, j, k % per))
    else:
        per = wide // tn
        total, first = (into[0], into[1]) if into is not None else (n_blk, 0)
        b_spec = pl.BlockSpec((tk, tn), lambda i, j, k: (k, j))
        o_spec = pl.BlockSpec((1, tm, tn), lambda i, j, k: (first + j // per, i, j % per))
        out_shape = (total, m, wide)

    def body(*refs):
        a_ref, b_ref = refs[0], refs[1]
        add_ref = refs[2] if add is not None else None
        o_ref, acc = refs[-2], refs[-1]
        k = pl.program_id(2)
        part = _bdot_impl(_tile(a_ref), _tile(b_ref), mode)

        @pl.when(k == 0)
        def _():
            acc[...] = part

        @pl.when(k > 0)
        def _():
            acc[...] += part

        @pl.when(k == nk - 1)
        def _():
            res = acc[...]
            if add_ref is not None:
                res = res + add_ref[...]
            o_ref[...] = res.astype(o_ref.dtype).reshape(o_ref.shape)

    operands = [a, b] + ([add] if add is not None else [])
    in_specs = [a_spec, b_spec] + ([o_spec] if add is not None else [])
    aliases = {}
    if into is not None and len(into) > 2:
        operands, in_specs, aliases = operands + [into[2]], in_specs + [pl.BlockSpec(memory_space=pl.ANY)], {len(operands): 0}
    return pl.pallas_call(
        body, grid=(m // tm, n // tn, nk), in_specs=in_specs, out_specs=o_spec,
        out_shape=jax.ShapeDtypeStruct(out_shape, out_dtype),
        scratch_shapes=[pltpu.VMEM((tm, tn), F32)], input_output_aliases=aliases,
        name=name, compiler_params=_cparams(3),
    )(*operands)


def _rows(x, width=None, off=0, tm=256):
    width = x.shape[1] if width is None else width
    return (x, (tm, width), lambda i, off=off: (i, off))


def _whole(x):
    nd = x.ndim
    return (x, x.shape, lambda *pids, nd=nd: (0,) * nd)


RMS_ROWS = 512


def _rms_ops(x, gain):
    return [_rows(x, tm=RMS_ROWS), _whole(gain)]


def rms_fwd(name, x, gain):
    s, dm = x.shape
    return tile_fwd(name, _rms_fn, (s // RMS_ROWS,), _rms_ops(x, gain), [((s, dm), BF16, (RMS_ROWS, dm), lambda i: (i, 0), ())])[0]


def rms_bwd(name, x, gain, dh, dres):
    s = x.shape[0]
    return tile_bwd(name, _rms_fn, (s // RMS_ROWS,), _rms_ops(x, gain), [_rows(dh, tm=RMS_ROWS)], [(0, ()), (1, (0,))],
                    adds={0: _rows(dres, tm=RMS_ROWS)})


def loss_call(y, t):
    s, dm = y.shape
    dy, part = tile_fwd("loss", _loss_fn, (s // RMS_ROWS,), [_rows(y, tm=RMS_ROWS), _rows(t, tm=RMS_ROWS)],
                        [((s, dm), F32, (RMS_ROWS, dm), lambda i: (i, 0), ()), ((8, LANES), F32, (8, LANES), lambda i: (0, 0), (0,))])
    return dy, part[0, 0]


def _fox_prep_ops(pm, gq, gk):
    tm = 512
    return [(pm, (tm, LANES), lambda i, j: (i, C_FQ // LANES + j)), (pm, (tm, LANES), lambda i, j: (i, C_FK // LANES + j)),
            _whole(gq), _whole(gk)]


def fox_prep_fwd(name, pm, gq, gk):
    s = pm.shape[0]
    out = ((s, BRANCH), BF16, (512, LANES), lambda i, j: (i, j), ())
    return tile_fwd(name, _fox_prep_fn, (s // 512, 4), _fox_prep_ops(pm, gq, gk), [out, out])


def fox_prep_bwd(name, pm, gq, gk, dqn, dkn):
    s = pm.shape[0]
    cot = lambda g: (g, (512, LANES), lambda i, j: (i, j))
    own = ((s, BRANCH), (512, LANES), lambda i, j: (i, j))
    return tile_bwd(name, _fox_prep_fn, (s // 512, 4), _fox_prep_ops(pm, gq, gk), [cot(dqn), cot(dkn)],
                    [(0, (), own, BF16), (1, (), own, BF16), (2, (0, 1)), (3, (0, 1))])


def _fox_gate_ops(f_t, bias):
    return [(f_t, (1,) + f_t.shape[1:], lambda h: (h, 0, 0)), (bias, (1, 1, 1), lambda h: (h, 0, 0))]


def fox_gate_fwd(name, f_t, bias):
    n_h = f_t.shape[0]
    return tile_fwd(name, _fox_gate_fn, (n_h,), _fox_gate_ops(f_t, bias),
                    [(f_t.shape, F32, (1,) + f_t.shape[1:], lambda h: (h, 0, 0), ())])[0]


def fox_gate_bwd(name, f_t, bias, dcum):
    n_h = f_t.shape[0]
    return tile_bwd(name, _fox_gate_fn, (n_h,), _fox_gate_ops(f_t, bias),
                    [(dcum, (1,) + f_t.shape[1:], lambda h: (h, 0, 0))], [(0, ()), (1, ())])


FOX_GROUPS = 4


def _fox_groups(s):
    per = s // FOX_BLOCK // FOX_GROUPS
    return [(g * per, per, (g + 1) * per * FOX_BLOCK) for g in range(FOX_GROUPS)]


def _fox_attn_ops(qn, kn, pm, cum_c, cum_r, q0, keys):
    nb = FOX_BLOCK
    return [(qn, (nb, LANES), lambda p, i: (q0 + i, p)), (kn, (keys, LANES), lambda p, i: (0, p)),
            (pm, (keys, LANES), lambda p, i: (0, C_FV // LANES + p)),
            (cum_c, (1, nb, 1), lambda p, i: (2 * p, q0 + i, 0)), (cum_c, (1, nb, 1), lambda p, i: (2 * p + 1, q0 + i, 0)),
            (cum_r, (1, 1, keys), lambda p, i: (2 * p, 0, 0)), (cum_r, (1, 1, keys), lambda p, i: (2 * p + 1, 0, 0))]


def fox_attn_fwd(name, qn, kn, pm, cum_c, cum_r):
    s = qn.shape[0]
    parts = []
    for g, (q0, n_q, keys) in enumerate(_fox_groups(s)):
        parts.append(tile_fwd(f"{name}_g{g}", functools.partial(_fox_attn_fn, q0), (4, n_q), _fox_attn_ops(qn, kn, pm, cum_c, cum_r, q0, keys),
                              [((n_q * FOX_BLOCK, BRANCH), BF16, (FOX_BLOCK, LANES), lambda p, i: (i, p), ())], raw=(0, 1, 2))[0])
    return jnp.concatenate(parts, axis=0)


def fox_attn_bwd(name, qn, kn, pm, cum_c, cum_r, dy):
    s = qn.shape[0]
    groups = _fox_groups(s)
    d_qn, by_q, tails = [None] * len(groups), [None] * len(groups), [None] * len(groups)
    below = None
    for g in reversed(range(len(groups))):
        q0, n_q, keys = groups[g]
        rows = n_q * FOX_BLOCK
        own_q = ((rows, BRANCH), (FOX_BLOCK, LANES), lambda p, i: (i, p))
        own_k = ((keys, BRANCH), (keys, LANES), lambda p, i: (0, p))
        pair_c = ((4, rows, 1), (1, FOX_BLOCK, 1), lambda p, i: (p, i, 0))
        pair_r = ((4, 1, keys), (1, 1, keys), lambda p, i: (p, 0, 0))
        adds = {}
        if below is not None:
            adds = {1: (below[0],) + own_k[1:], 2: (below[1],) + own_k[1:], 5: (below[2],) + pair_r[1:], 6: (below[3],) + pair_r[1:]}
        g_qn, g_kn, g_v, g_cqa, g_cqb, g_cka, g_ckb = tile_bwd(
            f"{name}_g{g}", functools.partial(_fox_attn_fn, q0), (4, n_q), _fox_attn_ops(qn, kn, pm, cum_c, cum_r, q0, keys),
            [(dy, (FOX_BLOCK, LANES), lambda p, i, q0=q0: (q0 + i, p))],
            [(0, (), own_q), (1, (1,), own_k), (2, (1,), own_k), (3, (), pair_c), (4, (), pair_c), (5, (1,), pair_r), (6, (1,), pair_r)],
            adds=adds)
        below = (g_kn, g_v, g_cka, g_ckb)
        lo = groups[g - 1][2] if g else 0
        d_qn[g] = g_qn
        by_q[g] = jnp.stack([g_cqa[:, :, 0], g_cqb[:, :, 0]], axis=1).reshape(8, rows)
        tails[g] = (g_kn[lo:], g_v[lo:], jnp.stack([g_cka[:, 0, lo:], g_ckb[:, 0, lo:]], axis=1).reshape(8, keys - lo))
    d_cum = jnp.concatenate(by_q, axis=1) + jnp.concatenate([t[2] for t in tails], axis=1)
    return jnp.concatenate(d_qn, axis=0), jnp.concatenate([t[0] for t in tails], axis=0), jnp.concatenate([t[1] for t in tails], axis=0), d_cum


def sconv_ops(pm, w):
    s = pm.shape[0]
    blk = lambda c0: (pm, (s, LANES), lambda j, c0=c0: (0, c0 // LANES + j))
    return [blk(C_SB), blk(C_SC), blk(C_SV), (w, (w.shape[0], LANES), lambda j: (0, j))]


def dnconv_ops(pm, w):
    s = pm.shape[0]
    return [(pm, (s, LANES), lambda j: (0, C_DN // LANES + j)), (w, (w.shape[0], LANES), lambda j: (0, j))]


def ffn_ops(ug, uv, w):
    s = ug.shape[0]
    n_t = D_FF // LANES
    return [(ug, (s, LANES), lambda j: (0, j)), (uv, (s, LANES), lambda j: (0, j)),
            (w, (w.shape[0], LANES), lambda j: (0, j)), (w, (w.shape[0], LANES), lambda j: (0, n_t + j))]


def _col_out(s, width, dtype=F32):
    return ((s, width), dtype, (s, LANES), lambda j: (0, j), ())


def _col_cot(g):
    return (g, (g.shape[0], LANES), lambda j: (0, j))


def merge_ops(yp, pm):
    gate = lambda b: (pm, (256, D_MODEL), lambda i, b=b: (i, C_GATE // D_MODEL + b))
    return [_rows(yp[0]), _rows(yp[1]), _rows(yp[2]), gate(0), gate(1), gate(2)]


def ple_ops(gpre, pe, x):
    return [_rows(gpre, tm=RMS_ROWS), _rows(pe, tm=RMS_ROWS), _rows(x, tm=RMS_ROWS)]


def adam_call(name, w, g, m, v):
    shape = w.shape
    last = shape[-1]
    rows = w.size // last
    flat = lambda t: t.reshape(rows, last)
    tm = rows
    for cand in (512, 256, 128, 64, 32, 16, 8):
        if rows % cand == 0 and cand * last * 4 <= 2 * 1024 * 1024:
            tm = cand
            break
    spec = lambda t: (flat(t), (tm, last), lambda i: (i, 0))
    out = ((rows, last), F32, (tm, last), lambda i: (i, 0), ())
    res = tile_fwd(name, _adam_fn, (rows // tm,), [spec(w), spec(g), spec(m), spec(v)], [out, out, out])
    return [r.reshape(shape) for r in res]


def _adam_layers_fn(d, pids, w, m, v, g0, g1):
    g = jnp.where(pids[0] == 0, g0, g1)
    return (g,) + _adam_fn(d, pids, w, g, m, v)


def adam_layers(name, w, m, v, g0, g1):
    _, rows, cols = w.shape
    tm = _row_tile(rows, cols)
    n_t = rows // tm
    lay = lambda t: (t, (1, tm, cols), lambda l, i: (l, i, 0))
    ins = [lay(w), lay(m), lay(v), (g0, (tm, cols), lambda l, i: (i * (1 - l) + (n_t - 1) * l, 0)), (g1, (tm, cols), lambda l, i: (i * l, 0))]
    out = (w.shape, F32, (1, tm, cols), lambda l, i: (l, i, 0), ())
    return tile_fwd(name, _adam_layers_fn, (2, n_t), ins, [out, out, out, out])


def adam_w_in(name, w, m, v, g0, g1):
    rows, n_l, cols = w.shape

    def body(w_ref, m_ref, v_ref, g0_ref, g1_ref, g_out, d_out, m_out, v_out):
        step = 64

        def update(at):
            g0, g1 = g0_ref[at, :], g1_ref[at, :]
            layer = _iota((g0.shape[0], n_l, LANES), 1)
            g = jnp.where(layer == 0, g0[:, None, :], g1[:, None, :])
            delta, m2, v2 = _adam_fn(False, None, w_ref[at], g, m_ref[at], v_ref[at])
            for ref, val in ((g_out, g), (d_out, delta), (m_out, m2), (v_out, v2)):
                ref[at] = val

        def some_rows(i, carry):
            update(pl.ds(pl.multiple_of(i * step, step), step))
            return carry

        lax.fori_loop(0, rows // step, some_rows, 0)
        if rows % step:
            update(pl.ds(rows - rows % step, rows % step))

    both = pl.BlockSpec((rows, n_l, LANES), lambda j: (0, 0, j))
    one = pl.BlockSpec((rows, LANES), lambda j: (0, j))
    return pl.pallas_call(
        body, grid=(cols // LANES,), in_specs=[both, both, both, one, one], out_specs=[both] * 4,
        out_shape=[jax.ShapeDtypeStruct(w.shape, F32)] * 4, name=name, compiler_params=_cparams(1),
    )(w, m, v, g0, g1)


DN_GROUP = 4


def _dn_local_specs():
    rows = DN_GROUP * DN_CHUNK
    return [pl.BlockSpec((rows, 3 * BRANCH), lambda j: (j, 0)), pl.BlockSpec((rows, LANES), lambda j: (j, 0)),
            pl.BlockSpec((DN_GROUP, DN_HEADS, DN_CHUNK), lambda j: (j, 0, 0)), pl.BlockSpec((2, DN_HEADS), lambda j: (0, 0))]


def _dn_group_inputs(qkv, ps, a_rows, c):
    lo = c * DN_CHUNK
    heads = _split_heads(qkv[lo:lo + DN_CHUNK])
    return heads[0:4], heads[4:8], heads[8:12], ps[lo:lo + DN_CHUNK], a_rows[c]


def dn_local_fwd(name, dn_act, ps, a_rows, ad):
    s = dn_act.shape[0]
    n_c, n_g = s // DN_CHUNK, s // (DN_GROUP * DN_CHUNK)
    rows = DN_GROUP * DN_CHUNK

    def body(qkv_ref, ps_ref, ar_ref, ad_ref, u_ref, kc_ref, qd_ref, kd_ref, qk_ref, gl_ref):
        qkv, ps_v, a_rows_v, ad_v = qkv_ref[...], ps_ref[...], ar_ref[...], ad_ref[...]
        args = [[] for _ in range(8)]
        for c in range(DN_GROUP):
            q4, k4, v4, ps_c, ar_c = _dn_group_inputs(qkv, ps_v, a_rows_v, c)
            for lst, vals in zip(args, (q4, k4, v4) + _dn_gates(ps_c, ar_c, ad_v)):
                lst.extend(vals)
        everything = _dn_local(False, *args)
        for c in range(DN_GROUP):
            res = everything[c * DN_HEADS:(c + 1) * DN_HEADS]
            at = pl.ds(c * DN_CHUNK, DN_CHUNK)
            for ref, i in ((u_ref, 0), (kc_ref, 1), (qd_ref, 2), (kd_ref, 3)):
                ref[at, :] = jnp.concatenate([r[i] for r in res], axis=1)
            for h in range(DN_HEADS):
                qk_ref[c, h] = res[h][4]
            gl_ref[c] = _head_rows([r[5] for r in res])

    wide = pl.BlockSpec((rows, BRANCH), lambda j: (j, 0))
    return pl.pallas_call(
        body, grid=(n_g,), in_specs=_dn_local_specs(),
        out_specs=[wide, wide, wide, wide, pl.BlockSpec((DN_GROUP, DN_HEADS, DN_CHUNK, DN_CHUNK), lambda j: (j, 0, 0, 0)),
                   pl.BlockSpec((DN_GROUP, 8, LANES), lambda j: (j, 0, 0))],
        out_shape=[jax.ShapeDtypeStruct((s, BRANCH), F32)] * 4 + [jax.ShapeDtypeStruct((n_c, DN_HEADS, DN_CHUNK, DN_CHUNK), F32),
                                                                 jax.ShapeDtypeStruct((n_c, 8, LANES), F32)],
        name=name, compiler_params=_cparams(1),
    )(dn_act, ps, a_rows, ad)


def dn_local_bwd(name, dn_act, ps, a_rows, ad, cots):
    s = dn_act.shape[0]
    n_c, n_g = s // DN_CHUNK, s // (DN_GROUP * DN_CHUNK)
    rows = DN_GROUP * DN_CHUNK

    def body(qkv_ref, ps_ref, ar_ref, ad_ref, du_ref, dkc_ref, dqd_ref, dkd_ref, dqk_ref, dgl_ref, dqkv_ref, dps_ref, dar_ref, dad_ref):
        first = pl.program_id(0) == 0
        qkv, ps_v, a_rows_v, ad_v = qkv_ref[...], ps_ref[...], ar_ref[...], ad_ref[...]
        d_wide = [r[...] for r in (du_ref, dkc_ref, dqd_ref, dkd_ref)]
        qs, ks, vs, ps_cs, ar_cs, cot = [], [], [], [], [], []
        for c in range(DN_GROUP):
            q4, k4, v4, ps_c, ar_c = _dn_group_inputs(qkv, ps_v, a_rows_v, c)
            qs, ks, vs, ps_cs, ar_cs = qs + q4, ks + k4, vs + v4, ps_cs + [ps_c], ar_cs + [ar_c]
            lo = c * DN_CHUNK
            d_tiles = [_split_heads(t[lo:lo + DN_CHUNK]) for t in d_wide]
            d_gl = dgl_ref[c]
            cot += [(d_tiles[0][h], d_tiles[1][h], d_tiles[2][h], d_tiles[3][h], dqk_ref[c, h], _col(_row(d_gl, h), 0))
                    for h in range(DN_HEADS)]

        def f(qs, ks, vs, ps_cs, ar_cs, ad_v):
            gates = [[] for _ in range(5)]
            for ps_c, ar_c in zip(ps_cs, ar_cs):
                for lst, vals in zip(gates, _dn_gates(ps_c, ar_c, ad_v)):
                    lst.extend(vals)
            return _dn_local(True, qs, ks, vs, *gates)

        _, vjp = jax.vjp(f, qs, ks, vs, ps_cs, ar_cs, ad_v)
        d_q, d_k, d_v, d_ps, d_ar, d_ad = vjp(cot)
        for c in range(DN_GROUP):
            at, hs = pl.ds(c * DN_CHUNK, DN_CHUNK), slice(c * DN_HEADS, (c + 1) * DN_HEADS)
            dqkv_ref[at, :] = jnp.concatenate(d_q[hs] + d_k[hs] + d_v[hs], axis=1).astype(dqkv_ref.dtype)
            dps_ref[at, :] = d_ps[c]
            dar_ref[c] = d_ar[c]
        _store(dad_ref, d_ad, first)

    wide = pl.BlockSpec((rows, BRANCH), lambda j: (j, 0))
    specs = _dn_local_specs()
    return pl.pallas_call(
        body, grid=(n_g,),
        in_specs=specs + [wide, wide, wide, wide, pl.BlockSpec((DN_GROUP, DN_HEADS, DN_CHUNK, DN_CHUNK), lambda j: (j, 0, 0, 0)),
                          pl.BlockSpec((DN_GROUP, 8, LANES), lambda j: (j, 0, 0))],
        out_specs=specs,
        out_shape=[jax.ShapeDtypeStruct((s, 3 * BRANCH), F32), jax.ShapeDtypeStruct((s, LANES), F32),
                   jax.ShapeDtypeStruct((n_c, DN_HEADS, DN_CHUNK), F32), jax.ShapeDtypeStruct((2, DN_HEADS), F32)],
        name=name, compiler_params=_cparams(1),
    )(dn_act, ps, a_rows, ad, *cots)


def _dn_scan_specs(n_c, rev):
    idx = (lambda j: n_c - 1 - j) if rev else (lambda j: j)
    wide = pl.BlockSpec((DN_CHUNK, BRANCH), lambda j: (idx(j), 0))
    return [wide, wide, wide, wide, pl.BlockSpec((1, DN_HEADS, DN_CHUNK, DN_CHUNK), lambda j: (idx(j), 0, 0, 0)),
            pl.BlockSpec((1, 8, LANES), lambda j: (idx(j), 0, 0)), pl.BlockSpec((DN_CHUNK, BRANCH), lambda j: (idx(j), C_DZ // BRANCH)),
            pl.BlockSpec((1, DN_DH), lambda j: (0, 0))]


def _dn_scan_tiles(refs):
    u_ref, kc_ref, qd_ref, kd_ref, qk_ref, gl_ref, z_ref, g_ref = refs
    wide = [_split_heads(r[...]) for r in (u_ref, kc_ref, qd_ref, kd_ref)]
    gl = gl_ref[0]
    return [(wide[0][h], wide[1][h], wide[2][h], wide[3][h], qk_ref[0, h], _col(_row(gl, h), 0)) for h in range(DN_HEADS)], \
        _split_heads(z_ref[...].astype(F32)), g_ref[...]


def dn_scan_fwd(name, local, pm, gain):
    s = pm.shape[0]
    n_c = s // DN_CHUNK

    def body(*refs):
        y_ref, hist_ref, state = refs[8:]

        @pl.when(pl.program_id(0) == 0)
        def _():
            state[...] = jnp.zeros_like(state)

        hist_ref[0] = state[...]
        per_head, z4, gain_v = _dn_scan_tiles(refs[:8])
        ys, s_nexts = _dn_step(False, [state[h] for h in range(DN_HEADS)], per_head, z4, gain_v)
        for h in range(DN_HEADS):
            state[h] = s_nexts[h]
        y_ref[...] = jnp.concatenate(ys, axis=1).astype(y_ref.dtype)

    return pl.pallas_call(
        body, grid=(n_c,), in_specs=_dn_scan_specs(n_c, False),
        out_specs=[pl.BlockSpec((DN_CHUNK, BRANCH), lambda j: (j, 0)),
                   pl.BlockSpec((1, DN_HEADS, DN_DH, DN_DH), lambda j: (j, 0, 0, 0))],
        out_shape=[jax.ShapeDtypeStruct((s, BRANCH), BF16), jax.ShapeDtypeStruct((n_c, DN_HEADS, DN_DH, DN_DH), F32)],
        scratch_shapes=[pltpu.VMEM((DN_HEADS, DN_DH, DN_DH), F32)],
        name=name, compiler_params=_cparams(1),
    )(*local, pm, gain)


def dn_scan_bwd(name, local, pm, gain, hist, dy):
    s = pm.shape[0]
    n_c = s // DN_CHUNK

    def body(*refs):
        hist_ref, dy_ref = refs[8:10]
        du_ref, dkc_ref, dqd_ref, dkd_ref, dqk_ref, dgl_ref, dz_ref, dg_ref, d_state = refs[10:]
        first = pl.program_id(0) == 0

        @pl.when(first)
        def _():
            d_state[...] = jnp.zeros_like(d_state)

        per_head, z4, gain_v = _dn_scan_tiles(refs[:8])
        _, vjp = jax.vjp(functools.partial(_dn_step, True), [hist_ref[0, h] for h in range(DN_HEADS)], per_head, z4, gain_v)
        d_s, grads, d_z, d_gain = vjp((_split_heads(dy_ref[...].astype(F32)), [d_state[h] for h in range(DN_HEADS)]))
        for h in range(DN_HEADS):
            d_state[h] = d_s[h]
        for ref, i in ((du_ref, 0), (dkc_ref, 1), (dqd_ref, 2), (dkd_ref, 3)):
            ref[...] = jnp.concatenate([g[i] for g in grads], axis=1)
        dz_ref[...] = jnp.concatenate(d_z, axis=1).astype(dz_ref.dtype)
        for h in range(DN_HEADS):
            dqk_ref[0, h] = grads[h][4]
        dgl_ref[0] = _head_rows([g[5] for g in grads])
        _store(dg_ref, d_gain, first)

    rev = lambda j: n_c - 1 - j
    specs = _dn_scan_specs(n_c, True)
    return pl.pallas_call(
        body, grid=(n_c,),
        in_specs=specs + [pl.BlockSpec((1, DN_HEADS, DN_DH, DN_DH), lambda j: (rev(j), 0, 0, 0)),
                          pl.BlockSpec((DN_CHUNK, BRANCH), lambda j: (rev(j), 0))],
        out_specs=specs[:6] + [pl.BlockSpec((DN_CHUNK, BRANCH), lambda j: (rev(j), 0)), specs[7]],
        out_shape=[jax.ShapeDtypeStruct((s, BRANCH), F32)] * 4 + [
            jax.ShapeDtypeStruct((n_c, DN_HEADS, DN_CHUNK, DN_CHUNK), F32), jax.ShapeDtypeStruct((n_c, 8, LANES), F32),
            jax.ShapeDtypeStruct((s, BRANCH), BF16), jax.ShapeDtypeStruct((1, DN_DH), F32)],
        scratch_shapes=[pltpu.VMEM((DN_HEADS, DN_DH, DN_DH), F32)],
        name=name, compiler_params=_cparams(1),
    )(*local, pm, gain, hist, dy)


def _seq_layouts(cols, s):
    return cols.T.reshape(cols.shape[1], s // LANES, LANES)


def layer_fwd(li, x, p, w, more_weights=None):
    s = x.shape[0]
    n = lambda t: f"{t}_l{li}"
    h = rms_fwd(n("rms_mix"), x, w["g_mix"])
    pm = mm(n("in_main"), h, w["in_main"], "nn")
    ps = mm(n("in_small"), h, w["in_small"], "nn")
    qn, kn = fox_prep_fwd(n("fox_prep"), pm, w["gq"], w["gk"])
    f_t = _seq_layouts(ps[:, 0:8], s)
    cum = fox_gate_fwd(n("fox_gate"), f_t, w["b_f"])
    cum_c, cum_r = cum.reshape(8, s, 1), cum.reshape(8, 1, s)
    y_fox = fox_attn_fwd(n("fox_attn"), qn, kn, pm, cum_c, cum_r)
    y_sc = tile_fwd(n("sconv"), _sconv_fn, (BRANCH // LANES,), sconv_ops(pm, w["sc_conv_w"]), [_col_out(s, BRANCH, BF16)])[0]
    dn_act = tile_fwd(n("dnconv"), _dnconv_fn, (3 * BRANCH // LANES,), dnconv_ops(pm, w["dn_conv_w"]), [_col_out(s, 3 * BRANCH)])[0]
    a_rows = ps[:, 12:16].reshape(s // DN_CHUNK, DN_CHUNK, DN_HEADS).transpose(0, 2, 1)
    dn_local = dn_local_fwd(n("dn_local"), dn_act, ps, a_rows, w["ad"])
    y_dn, hist = dn_scan_fwd(n("dn_scan"), dn_local, pm, w["dn_gain"])
    ys = (y_fox, y_sc, y_dn)
    if more_weights is not None:
        w = {**w, **more_weights(y_dn)}
    yp = [mm(n(f"branch{b}"), ys[b], w["branch"][b], "nn", blocks=(0, N_CHIPS)) for b in range(3)]
    merged = tile_fwd(n("merge"), _merge_fn, (s // 256,), merge_ops(yp, pm), [((s, D_MODEL), BF16, (256, D_MODEL), lambda i: (i, 0), ())])[0]
    x1 = mm(n("w_o"), merged, w["o"], "nn", add=x)
    h2 = rms_fwd(n("rms_ffn"), x1, w["g_ffn"])
    ug = mm(n("up_g"), h2, w["up"], "nn", blocks=(0, 2))
    uv = mm(n("up_v"), h2, w["up"], "nn", blocks=(2, 2))
    act = tile_fwd(n("ffn_act"), _ffn_act_fn, (D_FF // LANES,), ffn_ops(ug, uv, w["ffn_conv_w"]), [_col_out(s, D_FF, BF16)])[0]
    x2 = mm(n("down"), act, w["down"], "nn", add=x1)
    h3 = rms_fwd(n("rms_ple"), x2, w["g_ple"])
    gpre = mm(n("ple_gate"), h3, w["pg"], "nn")
    pe = mm(n("ple_emb"), p, w["ple"], "nn", blocks=(0, N_CHIPS))
    x3 = tile_fwd(n("ple"), _ple_fn, (s // RMS_ROWS,), ple_ops(gpre, pe, x2), [((s, D_MODEL), F32, (RMS_ROWS, D_MODEL), lambda i: (i, 0), ())])[0]
    saved = dict(x=x, h=h, pm=pm, ps=ps, qn=qn, kn=kn, f_t=f_t, cum_c=cum_c, cum_r=cum_r, ys=ys, dn_act=dn_act, dn_local=dn_local,
                 a_rows=a_rows, hist=hist, yp=yp, merged=merged, x1=x1, h2=h2, ug=ug, uv=uv, act=act, x2=x2, h3=h3,
                 gpre=gpre, pe=pe, p=p)
    return x3, saved, w


def hang_on(w, token):
    zero = token[0, 0]
    small = ("g_mix", "g_ffn", "g_ple", "gq", "gk", "b_f", "ad", "dn_gain", "sc_conv_w", "dn_conv_w", "ffn_conv_w")
    return {**w, **{k: w[k] + zero for k in small}}


def layer_bwd(li, dx3, sv, w, hooks=None):
    hooks = hooks or {}

    def stage(key, after, w):
        return hang_on(w, hooks[key](after, g)) if key in hooks else w

    s = dx3.shape[0]
    n = lambda t: f"{t}_l{li}"
    g = {}
    col_own = lambda width: ((s, width), (s, LANES), lambda j: (0, j))
    d_gpre, d_pe = tile_bwd(n("ple_bwd"), _ple_fn, (s // RMS_ROWS,), ple_ops(sv["gpre"], sv["pe"], sv["x2"]), [_rows(dx3, tm=RMS_ROWS)],
                            [(0, (), None, BF16), (1, (), None, BF16)])
    g["w_ple"] = mm(n("d_w_ple"), sv["p"], d_pe, "tn", blocks=(0, N_CHIPS))
    g["w_ple_gate"] = mm(n("d_w_pg"), sv["h3"], d_gpre, "tn").reshape(N_CHIPS, -1, D_MODEL)
    dh3 = mm(n("d_h3"), d_gpre, w["pg"], "nt")
    dx2, d_g_ple = rms_bwd(n("rms_ple_bwd"), sv["x2"], w["g_ple"], dh3, dx3)
    dact = mm(n("d_act"), dx2, w["down"], "nt")
    g["w_down"] = mm(n("d_w_down"), sv["act"], dx2, "tn").reshape(N_CHIPS, -1, D_MODEL)
    taps_own = ((w["ffn_conv_w"].shape[0], D_FF), (w["ffn_conv_w"].shape[0], LANES), lambda j: (0, j))
    d_ug, d_uv, d_fw_g, d_fw_v = tile_bwd(n("ffn_act_bwd"), _ffn_act_fn, (D_FF // LANES,), ffn_ops(sv["ug"], sv["uv"], w["ffn_conv_w"]),
                                          [_col_cot(dact)], [(0, (), None, BF16), (1, (), None, BF16), (2, (), taps_own), (3, (), taps_own)])
    g["ffn_conv_w"] = jnp.concatenate([d_fw_g, d_fw_v], axis=1)
    gate_half = mm(n("d_w_up_g"), sv["h2"], d_ug, "tn", blocks=(0, 2), into=(N_CHIPS, 0))
    g["w_up"] = mm(n("d_w_up_v"), sv["h2"], d_uv, "tn", blocks=(0, 2), into=(N_CHIPS, 2, gate_half))
    dh2 = mm(n("d_h2_v"), d_uv, w["up"], "nt", blocks=(2, 2), add=mm(n("d_h2_g"), d_ug, w["up"], "nt", blocks=(0, 2)))
    dx1, d_g_ffn = rms_bwd(n("rms_ffn_bwd"), sv["x1"], w["g_ffn"], dh2, dx2)
    w = stage("mid", dx1, w)
    dmerged = mm(n("d_merged"), dx1, w["o"], "nt")
    g["w_o"] = mm(n("d_w_o"), sv["merged"], dx1, "tn").reshape(N_CHIPS, -1, D_MODEL)
    gate_own = ((s, D_MODEL), (256, D_MODEL), lambda i: (i, 0))
    d_yp0, d_yp1, d_yp2, d_g0, d_g1, d_g2 = tile_bwd(
        n("merge_bwd"), _merge_fn, (s // 256,), merge_ops(sv["yp"], sv["pm"]), [_rows(dmerged)],
        [(0, (), None, BF16), (1, (), None, BF16), (2, (), None, BF16), (3, (), gate_own, BF16), (4, (), gate_own, BF16), (5, (), gate_own, BF16)])
    d_yp = (d_yp0, d_yp1, d_yp2)
    g["w_branch"] = jnp.concatenate([mm(n(f"d_w_branch{b}"), sv["ys"][b], d_yp[b], "tn", blocks=(0, N_CHIPS)) for b in range(3)], axis=1)
    d_ys = [mm(n(f"d_y{b}"), d_yp[b], w["branch"][b], "nt", blocks=(0, N_CHIPS)) for b in range(3)]
    w = stage("late", d_ys[2], w)
    *d_local, d_z, d_dngain = dn_scan_bwd(n("dn_scan_bwd"), sv["dn_local"], sv["pm"], w["dn_gain"], sv["hist"], d_ys[2])
    d_dnact, d_ps_dn, d_arows, d_ad = dn_local_bwd(n("dn_local_bwd"), sv["dn_act"], sv["ps"], sv["a_rows"], w["ad"], d_local)
    g["ad"], g["dn_norm_gain"] = d_ad, d_dngain[0]
    d_dnqkv, g["dn_conv_w"] = tile_bwd(n("dnconv_bwd"), _dnconv_fn, (3 * BRANCH // LANES,), dnconv_ops(sv["pm"], w["dn_conv_w"]),
                                       [_col_cot(d_dnact)], [(0, (), col_own(3 * BRANCH), BF16), (1, ())])
    d_sb, d_sc, d_sv, g["sc_conv_w"] = tile_bwd(n("sconv_bwd"), _sconv_fn, (BRANCH // LANES,), sconv_ops(sv["pm"], w["sc_conv_w"]), [_col_cot(d_ys[1])],
                                                [(0, (), col_own(BRANCH), BF16), (1, (), col_own(BRANCH), BF16), (2, (), col_own(BRANCH), BF16), (3, ())])
    w = stage("last", d_dnqkv, w)
    d_qn, d_kn, d_fv, d_cum = fox_attn_bwd(n("fox_attn_bwd"), sv["qn"], sv["kn"], sv["pm"], sv["cum_c"], sv["cum_r"], d_ys[0])
    d_ft, d_bf = fox_gate_bwd(n("fox_gate_bwd"), sv["f_t"], w["b_f"], d_cum.reshape(8, s // LANES, LANES))
    g["b_fox_f"] = d_bf.reshape(8)
    d_fq, d_fk, d_gq, d_gk = fox_prep_bwd(n("fox_prep_bwd"), sv["pm"], w["gq"], w["gk"], d_qn, d_kn)
    g["fox_q_gain"] = d_gq[0, :FOX_DH] + d_gq[0, FOX_DH:]
    g["fox_k_gain"] = d_gk[0, :FOX_DH] + d_gk[0, FOX_DH:]
    d_pm = jnp.concatenate([d_fq, d_fk, d_fv.astype(BF16), d_sb, d_sc, d_sv, d_dnqkv, d_z, d_g0, d_g1, d_g2], axis=1)
    d_a_cols = d_arows.transpose(0, 2, 1).reshape(s, DN_HEADS)
    d_f_cols = d_ft.reshape(8, s).T
    d_ps = d_ps_dn + jnp.concatenate([d_f_cols, jnp.zeros((s, 4), F32), d_a_cols, jnp.zeros((s, LANES - 16), F32)], axis=1)
    g["w_in"] = chip_blocks_w_in(mm(n("d_w_in_main"), d_pm, sv["h"], "tn"), mm(n("d_w_in_small"), d_ps, sv["h"], "tn"))
    w = stage("w_in", g["w_in"], w)
    dh = mm(n("d_h_small"), d_ps, w["in_small"], "nt", add=mm(n("d_h_main"), d_pm, w["in_main"], "nt"))
    dx, d_g_mix = rms_bwd(n("rms_mix_bwd"), sv["x"], w["g_mix"], dh, dx1)
    g["g_mix"], g["g_ffn"], g["g_ple"] = d_g_mix[0], d_g_ffn[0], d_g_ple[0]
    return dx, g


IN_SHARD = 2052
MAIN_RANGES = ((0, 1536), (1544, 3080), (3080, 4616), (4624, 5136), (5136, 8208))
SMALL_RANGES = ((1536, 1544), (4616, 4620), (4620, 4624))


def _from_chip_blocks(blocks, ranges):
    parts = []
    for lo, hi in ranges:
        for k in range(N_CHIPS):
            a0, a1 = max(lo, k * IN_SHARD), min(hi, (k + 1) * IN_SHARD)
            if a0 < a1:
                parts.append(blocks[k][:, a0 - k * IN_SHARD:a1 - k * IN_SHARD])
    return parts


def split_w_in(blocks):
    main = jnp.concatenate(_from_chip_blocks(blocks, MAIN_RANGES), axis=1)
    pad = jnp.zeros((blocks.shape[1], LANES - 16), blocks.dtype)
    return main, jnp.concatenate(_from_chip_blocks(blocks, SMALL_RANGES) + [pad], axis=1)


def chip_blocks_w_in(main, small):
    ranges = sorted([(lo, hi, "m") for lo, hi in MAIN_RANGES] + [(lo, hi, "s") for lo, hi in SMALL_RANGES])
    offs, m_off, s_off = {}, 0, 0
    for lo, hi in MAIN_RANGES:
        offs[lo] = m_off
        m_off += hi - lo
    for lo, hi in SMALL_RANGES:
        offs[lo] = s_off
        s_off += hi - lo
    blocks = []
    for k in range(N_CHIPS):
        parts = []
        for lo, hi, src in ranges:
            a0, a1 = max(lo, k * IN_SHARD), min(hi, (k + 1) * IN_SHARD)
            if a0 < a1:
                arr = main if src == "m" else small
                parts.append(arr[offs[lo] + a0 - lo:offs[lo] + a1 - lo])
        blocks.append(jnp.concatenate(parts, axis=0))
    return jnp.stack(blocks)


def later_weights(got):
    g_branch, g_o, g_up, g_down, g_pg, g_ple = got
    branch = g_branch.reshape(N_CHIPS, 3, BRANCH, -1)
    return dict(branch=[branch[:, b] for b in range(3)], o=g_o.reshape(D_MODEL, D_MODEL), up=g_up,
                down=g_down.reshape(D_FF, D_MODEL), pg=g_pg.reshape(D_MODEL, D_MODEL), ple=g_ple)


def layer_weights(li, got, conv, a):
    main, small = split_w_in(got[0])
    tile2 = lambda v: jnp.concatenate([v, v])[None, :]
    rest = later_weights(got[1:]) if len(got) > 1 else {}
    return dict(
        in_main=main, in_small=small, **rest,
        g_mix=a["g_mix"][li][None, :], g_ffn=a["g_ffn"][li][None, :], g_ple=a["g_ple"][li][None, :],
        gq=tile2(a["fox_q_gain"][li]), gk=tile2(a["fox_k_gain"][li]), b_f=a["b_fox_f"][li].reshape(8, 1, 1),
        ad=jnp.stack([a["dn_a_log"][li], a["dn_dt_bias"][li]]), dn_gain=a["dn_norm_gain"][li][None, :],
        sc_conv_w=conv["sc_conv_w"][li], dn_conv_w=conv["dn_conv_w"][li], ffn_conv_w=conv["ffn_conv_w"][li])


def pack_rows(arrs, dtype):
    flat = jnp.concatenate([t.reshape(-1).astype(dtype) for t in arrs])
    pad = (-flat.shape[0]) % (8 * LANES)
    if pad:
        flat = jnp.concatenate([flat, jnp.zeros((pad,), dtype)])
    return flat.reshape(-1, LANES)


def unpack_rows(buf, shapes):
    flat = buf.reshape(-1)
    out, off = [], 0
    for shp in shapes:
        size = 1
        for dim in shp:
            size *= dim
        out.append(flat[off:off + size].reshape(shp))
        off += size
    return out


ANY = pl.BlockSpec(memory_space=pl.ANY)


def _position():
    x, y, c = lax.axis_index("x"), lax.axis_index("y"), lax.axis_index("c")
    return x, y, c, [(1 - x, y), (x, 1 - y), (1 - x, 1 - y)]


def gather_small(name, block):
    m_per, n = block.shape

    def body(x_ref, out_ref, token, send_sems, recv_sems, local_sem):
        token[...] = jnp.zeros_like(token)
        x, y, c, chips = _position()
        me, sibling = (x, y, c), (x, y, 1 - c)

        def rows(px, py, pc):
            return out_ref.at[pl.ds((4 * px + 2 * py + pc) * m_per, m_per), :]

        def copy(k, blk, to, src=None):
            return pltpu.make_async_remote_copy(src_ref=rows(*blk) if src is None else src, dst_ref=rows(*blk),
                                                send_sem=send_sems.at[k], recv_sem=recv_sems.at[k], device_id=to, device_id_type=MESH)

        mine = pltpu.make_async_copy(x_ref, rows(*me), local_sem)
        mine.start()
        first = [copy(0, me, sibling, src=x_ref)] + [copy(1 + j, me, (*chip, c), src=x_ref) for j, chip in enumerate(chips)]
        for cp in first:
            cp.start()
        passed = [copy(4 + j, (*chip, c), sibling) for j, chip in enumerate(chips)]
        for j, chip in enumerate(chips):
            copy(1 + j, (*chip, c), me).wait_recv()
            passed[j].start()
        copy(0, sibling, me).wait_recv()
        for j, chip in enumerate(chips):
            copy(4 + j, (*chip, 1 - c), me).wait_recv()
        for cp in first + passed:
            cp.wait_send()
        mine.wait()

    in_vmem = pl.BlockSpec(memory_space=pltpu.VMEM)
    return pl.pallas_call(
        body, out_shape=[jax.ShapeDtypeStruct((8 * m_per, n), block.dtype), jax.ShapeDtypeStruct((8, LANES), F32)],
        in_specs=[in_vmem], out_specs=[in_vmem, in_vmem],
        scratch_shapes=[pltpu.SemaphoreType.DMA((7,)), pltpu.SemaphoreType.DMA((7,)), pltpu.SemaphoreType.DMA],
        name=name, compiler_params=pltpu.CompilerParams(vmem_limit_bytes=VMEM_LIMIT),
    )(block)


def _sems(n):
    return [pltpu.SemaphoreType.DMA((n,)), pltpu.SemaphoreType.DMA((n,))]


def _split_cols(rows):
    return (rows // 2) % 16 != 0


def _half(ref, which, lead=()):
    rows, cols = ref.shape[-2:]
    if _split_cols(rows):
        return ref.at[(*lead, slice(None), pl.ds(which * (cols // 2), cols // 2))]
    return ref.at[(*lead, pl.ds(which * (rows // 2), rows // 2), slice(None))]


def _half_shape(rows, cols):
    return (rows, cols // 2) if _split_cols(rows) else (rows // 2, cols)


def forward_halves(name, lands):
    n_w = len(lands)

    def body(*refs):
        outs = refs[n_w:2 * n_w]
        send_sems, recv_sems = refs[2 * n_w:]
        x, y, c, chips = _position()

        def copy(w, j, pc):
            cx, cy = chips[j]
            part = _half(outs[w], pc, (2 * cx + cy,))
            return pltpu.make_async_remote_copy(src_ref=part, dst_ref=part, send_sem=send_sems.at[3 * w + j], recv_sem=recv_sems.at[3 * w + j],
                                                device_id=(x, y, 1 - c), device_id_type=MESH)

        pairs = [(w, j) for w in range(n_w) for j in range(3)]
        for w, j in pairs:
            copy(w, j, c).start()
        for w, j in pairs:
            copy(w, j, 1 - c).wait_recv()
            copy(w, j, c).wait_send()

    return pl.pallas_call(
        body, out_shape=[jax.ShapeDtypeStruct(t.shape, t.dtype) for t in lands], in_specs=[ANY] * n_w, out_specs=[ANY] * n_w,
        input_output_aliases={w: w for w in range(n_w)}, scratch_shapes=_sems(3 * n_w), name=name,
    )(*lands)


def share_halves(name, bufs):
    n_w = len(bufs)

    def body(*refs):
        outs = refs[n_w:2 * n_w]
        send_sems, recv_sems = refs[2 * n_w:]
        x, y, c, _ = _position()

        def copy(w, pc):
            half = _half(outs[w], pc)
            return pltpu.make_async_remote_copy(src_ref=half, dst_ref=half, send_sem=send_sems.at[w], recv_sem=recv_sems.at[w],
                                                device_id=(x, y, 1 - c), device_id_type=MESH)

        for w in range(n_w):
            copy(w, c).start()
        for w in range(n_w):
            copy(w, 1 - c).wait_recv()
            copy(w, c).wait_send()

    return pl.pallas_call(
        body, out_shape=[jax.ShapeDtypeStruct(b.shape, b.dtype) for b in bufs], in_specs=[ANY] * n_w, out_specs=[ANY] * n_w,
        input_output_aliases={w: w for w in range(n_w)}, scratch_shapes=_sems(n_w), name=name,
    )(*bufs)


HBM = pl.BlockSpec(memory_space=pltpu.HBM)
SEM = pl.BlockSpec(memory_space=pltpu.SEMAPHORE)
EFFECT = pltpu.SideEffectType.DATAFLOW_SIDE_EFFECTING


def _exchange_copies(kind, srcs, lands):
    x, y, c, chips = _position()
    out = []
    for src, land in zip(srcs, lands):
        if kind == "swap":
            out.append((_half(src, 1 - c, (slice(None),)), land, (x, y, 1 - c)))
            continue
        for j, (cx, cy) in enumerate(chips):
            if kind == "gather":
                out.append((src, land.at[2 * x + y], (cx, cy, c)))
            elif kind == "gather_half":
                out.append((_half(src, c), _half(land, c, (2 * x + y,)), (cx, cy, c)))
            else:
                out.append((src.at[2 * cx + cy], land.at[j], (cx, cy, c)))
    return out


def _land_shapes(kind, srcs):
    if kind in ("gather", "gather_half"):
        return [(N_CHIPS,) + s.shape for s in srcs]
    if kind == "swap":
        return [(N_CHIPS,) + _half_shape(*s.shape[1:]) for s in srcs]
    return [(3,) + s.shape[1:] for s in srcs]


def exchange_start(name, kind, srcs):
    n_w = len(srcs)
    shapes = _land_shapes(kind, srcs)
    n_sem = n_w if kind == "swap" else 3 * n_w

    def body(*refs):
        ins, lands = refs[:n_w], refs[n_w:2 * n_w]
        send_sems, recv_sems = refs[2 * n_w:2 * n_w + 2]
        token = refs[-1]
        for i, (src, dst, dev) in enumerate(_exchange_copies(kind, ins, lands)):
            pltpu.make_async_remote_copy(src_ref=src, dst_ref=dst, send_sem=send_sems.at[i], recv_sem=recv_sems.at[i],
                                         device_id=dev, device_id_type=MESH).start()
        token[...] = jnp.zeros_like(token)

    out = pl.pallas_call(
        body, name=name,
        out_shape=(pltpu.SemaphoreType.DMA((n_sem,)), pltpu.SemaphoreType.DMA((n_sem,)),
                   *[pltpu.HBM(s.shape, s.dtype) for s in srcs], *[pltpu.HBM(shp, s.dtype) for shp, s in zip(shapes, srcs)],
                   jax.ShapeDtypeStruct((8, LANES), F32)),
        in_specs=(HBM,) * (2 * n_w), out_specs=(SEM, SEM) + (HBM,) * (2 * n_w) + (pl.BlockSpec(memory_space=pltpu.VMEM),),
        input_output_aliases={i: 2 + i for i in range(2 * n_w)},
        compiler_params=pltpu.CompilerParams(has_side_effects=EFFECT),
    )(*[pltpu.with_memory_space_constraint(s, pltpu.HBM) for s in srcs],
      *[pltpu.with_memory_space_constraint(lax.empty(shp, s.dtype), pltpu.HBM) for shp, s in zip(shapes, srcs)])
    return (kind, n_w, out[:-1]), out[-1]


def exchange_wait(name, handle, after):
    kind, n_w, (send_sems, recv_sems, *thru) = handle

    def body(*refs):
        ins, lands = refs[:n_w], refs[n_w:2 * n_w]
        send_sems, recv_sems = refs[2 * n_w:2 * n_w + 2]
        for i, (src, dst, dev) in enumerate(_exchange_copies(kind, ins, lands)):
            cp = pltpu.make_async_remote_copy(src_ref=src, dst_ref=dst, send_sem=send_sems.at[i], recv_sem=recv_sems.at[i],
                                              device_id=dev, device_id_type=MESH)
            cp.wait_send()
            cp.wait_recv()

    out = pl.pallas_call(
        body, name=name, out_shape=tuple(pltpu.HBM(t.shape, t.dtype) for t in thru),
        in_specs=(HBM,) * (2 * n_w) + (SEM, SEM, pl.BlockSpec(memory_space=pl.ANY)), out_specs=(HBM,) * (2 * n_w),
        input_output_aliases={i: i for i in range(2 * n_w)},
        compiler_params=pltpu.CompilerParams(has_side_effects=EFFECT),
    )(*thru, send_sems, recv_sems, after)
    return list(out[:n_w]), list(out[n_w:])


def _row_tile(rows, cols):
    best = rows
    if rows * cols * 4 <= 2 * 1024 * 1024:
        return rows
    for t in range(16, rows, 16):
        if rows % t == 0 and t * cols * 4 <= 2 * 1024 * 1024:
            best = t
    return best


def pair_sum(name, pos, grad, from_sibling):
    _, rows, cols = grad.shape
    h_rows, h_cols = _half_shape(rows, cols)
    tr = _row_tile(h_rows, h_cols)
    n_t = h_rows // tr

    def body(pos_ref, g_ref, s_ref, b_ref, f_ref):
        tot = g_ref[...] + s_ref[...]
        b_ref[...] = tot.astype(BF16)

        @pl.when(pl.program_id(1) == pos_ref[1])
        def _():
            f_ref[...] = tot[0]

    blk = pl.BlockSpec((1, tr, h_cols), lambda i, k, pos: (k, i, 0))
    if _split_cols(rows):
        mine = pl.BlockSpec((1, tr, h_cols), lambda i, k, pos: (k, i, pos[0]))
    else:
        mine = pl.BlockSpec((1, tr, h_cols), lambda i, k, pos: (k, pos[0] * n_t + i, 0))
    return pl.pallas_call(
        body, grid_spec=pltpu.PrefetchScalarGridSpec(
            num_scalar_prefetch=1, grid=(n_t, N_CHIPS), in_specs=[mine, blk],
            out_specs=[blk, pl.BlockSpec((tr, h_cols), lambda i, k, pos: (i, 0))]),
        out_shape=[jax.ShapeDtypeStruct((N_CHIPS, h_rows, h_cols), BF16), jax.ShapeDtypeStruct((h_rows, h_cols), F32)],
        name=name, compiler_params=_cparams(2),
    )(pos, grad, from_sibling)


def chip_sum(name, pos, own, landed, split_cols):
    half, cols = own.shape
    tr = _row_tile(half, cols)
    n_t = half // tr

    def body(pos_ref, p_ref, l_ref, o_ref):
        o_ref[...] = ((p_ref[...] + l_ref[0].astype(F32)) + l_ref[1].astype(F32)) + l_ref[2].astype(F32)

    if split_cols:
        out_spec, out_shape = pl.BlockSpec((tr, cols), lambda i, pos: (i, pos[0])), (half, 2 * cols)
    else:
        out_spec, out_shape = pl.BlockSpec((tr, cols), lambda i, pos: (pos[0] * n_t + i, 0)), (2 * half, cols)
    return pl.pallas_call(
        body, grid_spec=pltpu.PrefetchScalarGridSpec(
            num_scalar_prefetch=1, grid=(n_t,),
            in_specs=[pl.BlockSpec((tr, cols), lambda i, pos: (i, 0)), pl.BlockSpec((3, tr, cols), lambda i, pos: (0, i, 0))],
            out_specs=out_spec),
        out_shape=jax.ShapeDtypeStruct(out_shape, F32), name=name, compiler_params=_cparams(1),
    )(pos, own, landed)


class OverlappedReduceScatter:
    def __init__(self, tag, pos, grads):
        self.n = lambda t: f"{t}_{tag}"
        self.pos, self.grads = pos, grads
        self.swap, self.token = exchange_start(self.n("swap_start"), "swap", grads)

    def middle(self, after):
        self.grads, from_sibling = exchange_wait(self.n("swap_wait"), self.swap, after)
        self.sums = [pair_sum(self.n(f"pair_sum{w}"), self.pos, g, s) for w, (g, s) in enumerate(zip(self.grads, from_sibling))]
        self.scatter, self.token = exchange_start(self.n("scatter_start"), "scatter", [b for b, _ in self.sums])

    def finish(self, after):
        _, landed = exchange_wait(self.n("scatter_wait"), self.scatter, after)
        halves = [chip_sum(self.n(f"chip_sum{w}"), self.pos, own, l, _split_cols(g.shape[1]))
                  for w, ((_, own), l, g) in enumerate(zip(self.sums, landed, self.grads))]
        return share_halves(self.n("share_halves"), halves)


def sum_devices(gathered):
    m_per = gathered.shape[0] // 8

    def body(g_ref, o_ref):
        tot = g_ref[pl.ds(0, m_per), :]
        for dev in range(1, 8):
            tot = tot + g_ref[pl.ds(dev * m_per, m_per), :]
        o_ref[...] = tot

    return pl.pallas_call(
        body, out_shape=jax.ShapeDtypeStruct((m_per, gathered.shape[1]), F32),
        in_specs=[pl.BlockSpec(memory_space=pltpu.VMEM)], out_specs=pl.BlockSpec(memory_space=pltpu.VMEM), name="sum_devices",
    )(gathered)


def kernel(x, p, g_mix, w_in, b_fox_f, fox_q_gain, fox_k_gain, sc_conv_w, dn_conv_w, dn_a_log, dn_dt_bias, dn_norm_gain, w_branch, w_o, g_ffn, w_up, ffn_conv_w, w_down, g_ple, w_ple_gate, w_ple, loss_target, m_g_mix, m_w_in, m_b_fox_f, m_fox_q_gain, m_fox_k_gain, m_sc_conv_w, m_dn_conv_w, m_dn_a_log, m_dn_dt_bias, m_dn_norm_gain, m_w_branch, m_w_o, m_g_ffn, m_w_up, m_ffn_conv_w, m_w_down, m_g_ple, m_w_ple_gate, m_w_ple, v_g_mix, v_w_in, v_b_fox_f, v_fox_q_gain, v_fox_k_gain, v_sc_conv_w, v_dn_conv_w, v_dn_a_log, v_dn_dt_bias, v_dn_norm_gain, v_w_branch, v_w_o, v_g_ffn, v_w_up, v_ffn_conv_w, v_w_down, v_g_ple, v_w_ple_gate, v_w_ple):
    a = dict(g_mix=g_mix, w_in=w_in, b_fox_f=b_fox_f, fox_q_gain=fox_q_gain, fox_k_gain=fox_k_gain, sc_conv_w=sc_conv_w,
             dn_conv_w=dn_conv_w, dn_a_log=dn_a_log, dn_dt_bias=dn_dt_bias, dn_norm_gain=dn_norm_gain, w_branch=w_branch, w_o=w_o,
             g_ffn=g_ffn, w_up=w_up, ffn_conv_w=ffn_conv_w, w_down=w_down, g_ple=g_ple, w_ple_gate=w_ple_gate, w_ple=w_ple)
    mom = dict(g_mix=m_g_mix, w_in=m_w_in, b_fox_f=m_b_fox_f, fox_q_gain=m_fox_q_gain, fox_k_gain=m_fox_k_gain, sc_conv_w=m_sc_conv_w,
               dn_conv_w=m_dn_conv_w, dn_a_log=m_dn_a_log, dn_dt_bias=m_dn_dt_bias, dn_norm_gain=m_dn_norm_gain, w_branch=m_w_branch,
               w_o=m_w_o, g_ffn=m_g_ffn, w_up=m_w_up, ffn_conv_w=m_ffn_conv_w, w_down=m_w_down, g_ple=m_g_ple, w_ple_gate=m_w_ple_gate,
               w_ple=m_w_ple)
    var = dict(g_mix=v_g_mix, w_in=v_w_in, b_fox_f=v_b_fox_f, fox_q_gain=v_fox_q_gain, fox_k_gain=v_fox_k_gain, sc_conv_w=v_sc_conv_w,
               dn_conv_w=v_dn_conv_w, dn_a_log=v_dn_a_log, dn_dt_bias=v_dn_dt_bias, dn_norm_gain=v_dn_norm_gain, w_branch=v_w_branch,
               w_o=v_w_o, g_ffn=v_g_ffn, w_up=v_w_up, ffn_conv_w=v_ffn_conv_w, w_down=v_w_down, g_ple=v_g_ple, w_ple_gate=v_w_ple_gate,
               w_ple=v_w_ple)
    cx, cy, cc = lax.axis_index("x"), lax.axis_index("y"), lax.axis_index("c")
    chip = 2 * cx + cy
    pos = jnp.stack([cc, chip]).astype(jnp.int32)

    def as_blocks(t):
        return t.reshape(2, -1, t.shape[-1])

    def own_block_in(got, shards):
        return [lax.dynamic_update_slice(g, s[None], (chip, 0, 0)) for g, s in zip(got, shards)]

    conv_shapes = [a[nm].shape for nm in CONVS]
    conv_all, conv_token = gather_small("gather_conv_w", pack_rows([a[nm] for nm in CONVS], F32))
    def w_in_block(li, token):
        stored = jnp.transpose(a["w_in"], (2, 0, 1))[:, li, :]
        return (stored + token[0, 0]).astype(BF16).T

    w_in0 = [w_in_block(0, conv_token)]
    gather_in0, gather_in0_token = exchange_start("gather_start_w_in_l0", "gather_half", w_in0)
    shards0 = w_in0 + [(as_blocks(a[nm])[0] + gather_in0_token[0, 0]).astype(BF16) for nm in BIG[1:]]
    gather0, gather0_token = exchange_start("gather_start_l0", "gather", shards0[1:])
    shards1 = [w_in_block(1, gather0_token)] + [(as_blocks(a[nm])[1] + gather0_token[0, 0]).astype(BF16) for nm in BIG[1:]]
    gather1, gather1_in_token = exchange_start("gather_start_w_in_l1", "gather", shards1[:1])
    shards1[1:] = [s + gather1_in_token[0, 0].astype(BF16) for s in shards1[1:]]
    gather1_rest, gather1_token = exchange_start("gather_start_l1", "gather", shards1[1:])
    conv_rows = conv_all.shape[0] // 8
    conv_chip = [unpack_rows(conv_all[2 * k * conv_rows:(2 * k + 1) * conv_rows], conv_shapes) for k in range(N_CHIPS)]
    conv = {nm: jnp.concatenate([conv_chip[k][i] for k in range(N_CHIPS)], axis=2) for i, nm in enumerate(CONVS)}

    weights, saved = [None, None], [None, None]
    mine_in0, got_in0 = exchange_wait("gather_wait_w_in_l0", gather_in0, gather1_token)
    got_in0 = forward_halves("forward_w_in_l0", got_in0)
    first_weights = hang_on(layer_weights(0, own_block_in(got_in0, mine_in0), conv, a), gather1_token)

    def rest_of_layer0(after):
        mine, got = exchange_wait("gather_wait_l0", gather0, after)
        return later_weights(own_block_in(got, mine))

    act, saved[0], weights[0] = layer_fwd(0, x[0], p[0, 0], first_weights, more_weights=rest_of_layer0)
    mine1, got1 = exchange_wait("gather_wait_w_in_l1", gather1, act)

    def rest_of_layer1(after):
        mine, got = exchange_wait("gather_wait_l1", gather1_rest, after)
        return later_weights(own_block_in(got, mine))

    act, saved[1], weights[1] = layer_fwd(1, act, p[1, 0], layer_weights(1, own_block_in(got1, mine1), conv, a),
                                          more_weights=rest_of_layer1)
    d_act, loss_part = loss_call(act, loss_target[0])
    loss = lax.psum(loss_part, ("x", "y", "c"))
    layer_grads = [None, None]
    d_act, layer_grads[1] = layer_bwd(1, d_act, saved[1], weights[1])
    rs1 = OverlappedReduceScatter("l1", pos, [layer_grads[1][nm] for nm in BIG])
    rs0 = []

    def stage_mid(after, g):
        rs1.middle(after)
        return rs1.token

    def stage_late(after, g):
        rs0.append(OverlappedReduceScatter("l0", pos, [g[nm] for nm in BIG[1:]]))
        return rs0[0].token

    def stage_last(after, g):
        rs0[0].middle(after)
        return rs0[0].token

    def stage_w_in(after, g):
        rs0.append(OverlappedReduceScatter("w_in_l0", pos, [g["w_in"]]))
        return rs0[1].token

    d_act, layer_grads[0] = layer_bwd(0, d_act, saved[0], hang_on(weights[0], rs1.token),
                                      hooks=dict(mid=stage_mid, late=stage_late, last=stage_last, w_in=stage_w_in))
    rs0[1].middle(d_act)
    reduced = [rs0[0].finish(rs0[1].token), rs1.finish(rs0[1].token)]
    grad_x = d_act[None]

    def both(nm):
        return jnp.stack([layer_grads[0][nm], layer_grads[1][nm]])

    local = {nm: both(nm) for nm in ("g_mix", "b_fox_f", "fox_q_gain", "fox_k_gain", "dn_norm_gain", "g_ffn", "g_ple", "sc_conv_w",
                                      "dn_conv_w", "ffn_conv_w")}
    local["dn_a_log"] = jnp.stack([layer_grads[li]["ad"][0] for li in range(2)])
    local["dn_dt_bias"] = jnp.stack([layer_grads[li]["ad"][1] for li in range(2)])

    small_names = SMALL + CONVS
    small_shapes = [local[nm].shape for nm in small_names]
    small_sum = sum_devices(gather_small("gather_small_grads", pack_rows([local[nm] for nm in small_names], F32))[0])
    small_grads = dict(zip(small_names, unpack_rows(small_sum, small_shapes)))
    for nm in CONVS:
        width = a[nm].shape[2]
        small_grads[nm] = lax.dynamic_slice_in_dim(small_grads[nm], chip * width, width, axis=2)

    grads, deltas, new_m, new_v = dict(small_grads), {}, {}, {}
    for nm in small_names:
        deltas[nm], new_m[nm], new_v[nm] = adam_call(f"adam_{nm}", a[nm], grads[nm], mom[nm], var[nm])
    for i, nm in enumerate(BIG[1:]):
        res = adam_layers(f"adam_{nm}", as_blocks(a[nm]), as_blocks(mom[nm]), as_blocks(var[nm]), reduced[0][i], reduced[1][1 + i])
        grads[nm], deltas[nm], new_m[nm], new_v[nm] = [r.reshape(a[nm].shape) for r in res]
    stored = lambda t: jnp.transpose(t, (2, 0, 1))
    res = adam_w_in("adam_w_in", stored(a["w_in"]), stored(mom["w_in"]), stored(var["w_in"]), rs0[1].finish(deltas["w_ple"])[0], reduced[1][0])
    grads["w_in"], deltas["w_in"], new_m["w_in"], new_v["w_in"] = [jnp.transpose(r, (1, 2, 0)) for r in res]
    return (loss, grad_x, *[grads[nm] for nm in WEIGHTS], *[deltas[nm] for nm in WEIGHTS], *[new_m[nm] for nm in WEIGHTS],
            *[new_v[nm] for nm in WEIGHTS])
```

```python
import functools

import jax
import jax.numpy as jnp
from jax import lax
from jax.experimental import pallas as pl
from jax.experimental.pallas import tpu as pltpu

F32 = jnp.float32
BF16 = jnp.bfloat16
HI = lax.Precision.HIGHEST
SOLVE = lax.Precision.HIGH
MESH = pl.DeviceIdType.MESH

D_MODEL = 1024
BRANCH = 512
FOX_DH = 64
DN_DH = 128
DN_HEADS = 4
DN_CHUNK = 64
FOX_BLOCK = 128
D_FF = 2816
EPS = 1e-6
N_CHIPS = 4
LANES = 128

ADAM_LR, ADAM_B1, ADAM_B2, ADAM_EPS, ADAM_WD, ADAM_STEP = 0.001, 0.9, 0.999, 1e-08, 0.01, 10

VMEM_LIMIT = 56 * 1024 * 1024

C_FQ, C_FK, C_FV, C_SB, C_SC, C_SV, C_DN, C_DZ, C_GATE = 0, 512, 1024, 1536, 2048, 2560, 3072, 4608, 5120
IN_MAIN = 8192

BIG = ("w_in", "w_branch", "w_o", "w_up", "w_down", "w_ple_gate", "w_ple")
CONVS = ("sc_conv_w", "dn_conv_w", "ffn_conv_w")
SMALL = ("g_mix", "b_fox_f", "fox_q_gain", "fox_k_gain", "dn_a_log", "dn_dt_bias", "dn_norm_gain", "g_ffn", "g_ple")
WEIGHTS = ("g_mix", "w_in", "b_fox_f", "fox_q_gain", "fox_k_gain", "sc_conv_w", "dn_conv_w", "dn_a_log", "dn_dt_bias",
           "dn_norm_gain", "w_branch", "w_o", "g_ffn", "w_up", "ffn_conv_w", "w_down", "g_ple", "w_ple_gate", "w_ple")


def _iota(shape, dim):
    return lax.broadcasted_iota(jnp.int32, shape, dim)


def _dg(a, b, mode, prec=None):
    dims = {"nn": ((1,), (0,)), "nt": ((1,), (1,)), "tn": ((0,), (0,))}[mode]
    return lax.dot_general(a, b, (dims, ((), ())), precision=prec, preferred_element_type=F32)


def _bdot_impl(a, b, mode):
    return _dg(a.astype(BF16), b.astype(BF16), mode)


@functools.partial(jax.custom_vjp, nondiff_argnums=(2,))
def _bdot_diff(a, b, mode):
    return _bdot_impl(a, b, mode)


def _bdot_fwd(a, b, mode):
    return _bdot_impl(a, b, mode), (a, b)


def _bdot_bwd(mode, res, g):
    a, b = res
    if mode == "nn":
        da, db = _bdot_impl(g, b, "nt"), _bdot_impl(a, g, "tn")
    elif mode == "nt":
        da, db = _bdot_impl(g, b, "nn"), _bdot_impl(g, a, "tn")
    else:
        da, db = _bdot_impl(b, g, "nt"), _bdot_impl(a, g, "nn")
    return da.astype(a.dtype), db.astype(b.dtype)


_bdot_diff.defvjp(_bdot_fwd, _bdot_bwd)


def _bdot(d):
    return _bdot_diff if d else _bdot_impl


def _shift_impl(x, k):
    return jnp.where(_iota(x.shape, 0) >= k, pltpu.roll(x, k, 0), 0.0)


def _unshift_impl(g, k):
    n = g.shape[0]
    return jnp.where(_iota(g.shape, 0) < n - k, pltpu.roll(g, n - k, 0), 0.0)


@functools.partial(jax.custom_vjp, nondiff_argnums=(1,))
def _shift_diff(x, k):
    return _shift_impl(x, k)


_shift_diff.defvjp(lambda x, k: (_shift_impl(x, k), None), lambda k, _, g: (_unshift_impl(g, k),))


def _row(w, j):
    return jnp.sum(jnp.where(_iota(w.shape, 0) == j, w, 0.0), axis=0, keepdims=True)


def _col(w, j):
    return jnp.sum(jnp.where(_iota(w.shape, 1) == j, w, 0.0), axis=1, keepdims=True)


def _conv(d, x, w):
    shift = _shift_diff if d else _shift_impl
    taps = w.shape[0]
    y = x * _row(w, taps - 1)
    for j in range(taps - 1):
        y = y + shift(x, taps - 1 - j) * _row(w, j)
    return y


def _softplus(x):
    return jnp.maximum(x, 0.0) + jnp.log(1.0 + jnp.exp(-jnp.abs(x)))


def _sigmoid(x):
    return 0.5 * (jnp.tanh(0.5 * x) + 1.0)


def _silu(x):
    return x * _sigmoid(x)


def _rms(x, gain):
    return x * lax.rsqrt(jnp.mean(x * x, axis=-1, keepdims=True) + EPS) * gain


def _rms_fn(d, pids, x, gain):
    return (_rms(x, gain),)


def _loss_fn(d, pids, y, t):
    e = y - t
    part = 0.5 / D_MODEL * jnp.sum(e * e, keepdims=True)
    return e * (1.0 / D_MODEL), jnp.broadcast_to(part, (8, LANES))


def _fox_prep_fn(d, pids, q, k, gq, gk):
    first = _iota(q.shape, 1) < FOX_DH

    def norm(x, gain):
        sq = x * x
        ss_a = jnp.sum(jnp.where(first, sq, 0.0), axis=1, keepdims=True)
        ss_b = jnp.sum(jnp.where(first, 0.0, sq), axis=1, keepdims=True)
        rs = jnp.where(first, lax.rsqrt(ss_a / FOX_DH + EPS), lax.rsqrt(ss_b / FOX_DH + EPS))
        return x * rs * gain

    return norm(q, gq) * FOX_DH ** -0.5, norm(k, gk)


def _fox_gate_fn(d, pids, f, bias):
    logf = -_softplus(-(f + bias))
    n_r, n_c = logf.shape
    tri = (_iota((n_c, n_c), 0) <= _iota((n_c, n_c), 1)).astype(F32)
    within = _dg(logf, tri, "nn", HI)
    tot = jnp.broadcast_to(jnp.sum(logf, axis=1, keepdims=True), logf.shape)
    below = (_iota((n_r, n_r), 1) < _iota((n_r, n_r), 0)).astype(F32)
    return (within + _dg(below, tot, "nn", HI),)


def _fox_attn_fn(q_block0, d, pids, q, k, v, cq_a, cq_b, ck_a, ck_b):
    dot = _bdot(d)
    first = _iota(q.shape, 1) < FOX_DH
    n_q, n_k = q.shape[0], k.shape[0]
    causal = ((q_block0 + pids[1]) * n_q + _iota((n_q, n_k), 0)) >= _iota((n_q, n_k), 1)

    qs = [jnp.where(first, q, 0.0), jnp.where(first, 0.0, q)]
    s = _each(lambda qh, cq, ck: jnp.where(causal, dot(qh, k, "nt") + cq - ck, -1e30), qs, [cq_a, cq_b], [ck_a, ck_b])
    e = [jnp.exp(si - lax.stop_gradient(jnp.max(si, axis=1, keepdims=True))) for si in s]
    o_a, o_b = [dot(ei * (1.0 / jnp.sum(ei, axis=1, keepdims=True)), v, "nn") for ei in e]
    return (jnp.where(first, o_a, o_b),)


def _sconv_fn(d, pids, sb, sc, sv, w):
    return (sb * _conv(d, sc * sv, w),)


def _dnconv_fn(d, pids, x, w):
    return (_silu(_conv(d, x, w)),)


def _merge_fn(d, pids, y0, y1, y2, g0, g1, g2):
    return (_sigmoid(g0) * y0 + _sigmoid(g1) * y1 + _sigmoid(g2) * y2,)


def _ffn_act_fn(d, pids, ug, uv, wg, wv):
    return (_silu(_conv(d, ug, wg)) * _conv(d, uv, wv),)


def _ple_fn(d, pids, gpre, pe, x):
    return (x + _sigmoid(gpre) * pe,)


def _adam_fn(d, pids, w, g, m, v):
    m2 = ADAM_B1 * m + (1.0 - ADAM_B1) * g
    v2 = ADAM_B2 * v + (1.0 - ADAM_B2) * (g * g)
    m_hat = m2 / (1.0 - ADAM_B1 ** ADAM_STEP)
    v_hat = v2 / (1.0 - ADAM_B2 ** ADAM_STEP)
    delta = -ADAM_LR * (m_hat / (jnp.sqrt(v_hat) + ADAM_EPS) + ADAM_WD * w)
    return delta, m2, v2


def _each(fn, *lists):
    return [fn(*args) for args in zip(*lists)]


def _tri_inv_impl(mats):
    n = mats[0].shape[0]
    r, c = _iota((n, n), 0), _iota((n, n), 1)
    diag_blk = (r >> 4) == (c >> 4)
    eye = (r == c).astype(F32)
    mm = lambda us, ws: _each(lambda u, w: _dg(u, w, "nn", SOLVE), us, ws)
    grow = lambda ps, xs: _each(lambda p, px: p + px, ps, mm(ps, xs))
    x = [jnp.where(diag_blk, -a, 0.0) for a in mats]
    p = [eye + xi for xi in x]
    x2 = mm(x, x)
    p = grow(p, x2)
    x4 = mm(x2, x2)
    p = grow(p, x4)
    p = grow(p, mm(x4, x4))
    y = [-yi for yi in mm(p, [jnp.where(diag_blk, 0.0, a) for a in mats])]
    q = grow([eye + yi for yi in y], mm(y, y))
    return mm(q, p)


@jax.custom_vjp
def _tri_inv_diff(mats):
    return _tri_inv_impl(mats)


def _tri_inv_fwd(mats):
    ts = _tri_inv_impl(mats)
    return ts, ts


def _tri_inv_bwd(ts, gs):
    left = _each(lambda t, g: _dg(t, g, "tn", SOLVE), ts, gs)
    return ([-m for m in _each(lambda l, t: _dg(l, t, "nt", SOLVE), left, ts)],)


_tri_inv_diff.defvjp(_tri_inv_fwd, _tri_inv_bwd)


def _dn_local(d, qs, ks, vs, a_cs, a_rs, b_cs, a_logs, dt_bs):
    dot = _bdot(d)
    inv = _tri_inv_diff if d else _tri_inv_impl
    n = qs[0].shape[0]
    r, c = _iota((n, n), 0), _iota((n, n), 1)
    incl, strict, upper = r >= c, r > c, r <= c
    qs = [q * lax.rsqrt(jnp.sum(q * q, axis=1, keepdims=True) + EPS) * DN_DH ** -0.5 for q in qs]
    ks = [k * lax.rsqrt(jnp.sum(k * k, axis=1, keepdims=True) + EPS) for k in ks]
    betas = [_sigmoid(b) for b in b_cs]
    rates = [-jnp.exp(a) for a in a_logs]
    g_cs = _each(lambda rate, a, dt: rate * _softplus(a + dt), rates, a_cs, dt_bs)
    g_rs = _each(lambda rate, a, dt: rate * _softplus(a + dt), rates, a_rs, dt_bs)
    gcum_cs = [jnp.sum(jnp.where(incl, g, 0.0), axis=1, keepdims=True) for g in g_rs]
    gcum_rs = [jnp.sum(jnp.where(upper, g, 0.0), axis=0, keepdims=True) for g in g_cs]
    decays = _each(lambda gc, gr: jnp.exp(jnp.where(incl, gc - gr, -1e30)), gcum_cs, gcum_rs)
    kbs = _each(lambda k, b: k * b, ks, betas)
    kk = _each(lambda kb, k: dot(kb, k, "nt"), kbs, ks)
    ts = inv(_each(lambda m, dec: jnp.where(strict, m * dec, 0.0), kk, decays))
    e_gs = [jnp.exp(g) for g in gcum_cs]
    us = _each(lambda t, v, b: _dg(t, v * b, "nn", SOLVE), ts, vs, betas)
    k_cums = _each(lambda t, kb, e: _dg(t, kb * e, "nn", SOLVE), ts, kbs, e_gs)
    qk = _each(lambda q, k: dot(q, k, "nt"), qs, ks)
    qk = _each(lambda m, dec: jnp.where(incl, m * dec, 0.0), qk, decays)
    g_lasts = [jnp.sum(g, axis=0, keepdims=True) for g in g_cs]
    q_decs = _each(lambda q, e: q * e, qs, e_gs)
    k_decs = _each(lambda k, gl, gc: k * jnp.exp(gl - gc), ks, g_lasts, gcum_cs)
    return list(zip(us, k_cums, q_decs, k_decs, qk, g_lasts))


def _dn_step(d, s_prevs, items, zs, gain):
    dot = _bdot(d)
    us, k_cums, q_decs, k_decs, qks, g_lasts = [list(t) for t in zip(*items)]
    v_news = _each(lambda u, kc, s: u - dot(kc, s, "nn"), us, k_cums, s_prevs)
    inter = _each(lambda qd, s: dot(qd, s, "nn"), q_decs, s_prevs)
    outs = _each(lambda o, qk, vn: o + dot(qk, vn, "nn"), inter, qks, v_news)
    s_nexts = _each(lambda s, gl, kd, vn: s * jnp.exp(gl) + dot(kd, vn, "tn"), s_prevs, g_lasts, k_decs, v_news)
    return _each(lambda o, z: _rms(o, gain) * _silu(z), outs, zs), s_nexts


def _split_heads(t):
    return [t[:, h * DN_DH:(h + 1) * DN_DH] for h in range(t.shape[1] // DN_DH)]


def _dn_gates(ps, a_rows, ad):
    hs = range(DN_HEADS)
    return ([_col(ps, 12 + h) for h in hs], [_row(a_rows, h) for h in hs], [_col(ps, 8 + h) for h in hs],
            [_col(_row(ad, 0), h) for h in hs], [_col(_row(ad, 1), h) for h in hs])


def _head_rows(vals):
    row = _iota((8, LANES), 0)
    tile = jnp.zeros((8, LANES), F32)
    for h, val in enumerate(vals):
        tile = tile + jnp.where(row == h, val, 0.0)
    return tile


def _cparams(n_axes):
    return pltpu.CompilerParams(dimension_semantics=("arbitrary",) * n_axes, vmem_limit_bytes=VMEM_LIMIT)


def _first_visit(acc_axes):
    cond = None
    for a in acc_axes:
        here = pl.program_id(a) == 0
        cond = here if cond is None else jnp.logical_and(cond, here)
    return cond


def _tile(ref, widen=False):
    val = ref[...]
    shape = val.shape
    while len(shape) > 2 and shape[0] == 1:
        shape = shape[1:]
    val = val.reshape(shape)
    return val.astype(F32) if widen and val.dtype == BF16 else val


def _store(ref, val, first):
    val = val.astype(ref.dtype).reshape(ref.shape)
    if first is None:
        ref[...] = val
        return

    @pl.when(first)
    def _():
        ref[...] = val

    @pl.when(jnp.logical_not(first))
    def _():
        ref[...] += val


def _specs(ops):
    return [pl.BlockSpec(block, imap) for _, block, imap in ops]


def tile_fwd(name, fn, grid, ins, outs, raw=()):
    n_in = len(ins)

    def body(*refs):
        pids = tuple(pl.program_id(a) for a in range(len(grid)))
        firsts = [_first_visit(o[4]) if o[4] else None for o in outs]
        res = fn(False, pids, *[_tile(r, i not in raw) for i, r in enumerate(refs[:n_in])])
        for ref, val, first in zip(refs[n_in:], res, firsts):
            _store(ref, val, first)

    out = pl.pallas_call(
        body, grid=grid, in_specs=_specs(ins),
        out_specs=[pl.BlockSpec(o[2], o[3]) for o in outs],
        out_shape=[jax.ShapeDtypeStruct(o[0], o[1]) for o in outs],
        name=name, compiler_params=_cparams(len(grid)),
    )(*[a for a, _, _ in ins])
    return out


def tile_bwd(name, fn, grid, ins, cots, diff, adds=None, raw=()):
    adds = adds or {}
    n_in, n_cot = len(ins), len(cots)
    add_pos = sorted(adds)
    diff_idx = [d[0] for d in diff]
    out_desc = [d[2] if len(d) > 2 and d[2] is not None else (ins[d[0]][0].shape, ins[d[0]][1], ins[d[0]][2]) for d in diff]
    out_dtypes = [d[3] if len(d) > 3 else F32 for d in diff]

    def body(*refs):
        pids = tuple(pl.program_id(a) for a in range(len(grid)))
        firsts = [_first_visit(d[1]) if d[1] else None for d in diff]
        vals = [_tile(r, i not in raw) for i, r in enumerate(refs[:n_in])]
        cot_vals = [_tile(r, True) for r in refs[n_in:n_in + n_cot]]
        add_vals = [_tile(r) for r in refs[n_in + n_cot:n_in + n_cot + len(add_pos)]]
        out_refs = refs[n_in + n_cot + len(add_pos):]

        def f(*dv):
            full = list(vals)
            for i, val in zip(diff_idx, dv):
                full[i] = val
            return fn(True, pids, *full)

        prim, vjp = jax.vjp(f, *[vals[i].astype(F32) for i in diff_idx])
        grads = list(vjp(tuple(c.astype(o.dtype) for c, o in zip(cot_vals, prim))))
        for pos, val in zip(add_pos, add_vals):
            extra = val.astype(F32) if firsts[pos] is None else jnp.where(firsts[pos], val.astype(F32), 0.0)
            grads[pos] = grads[pos] + extra
        for ref, val, first in zip(out_refs, grads, firsts):
            _store(ref, val, first)

    all_ins = list(ins) + list(cots) + [adds[p] for p in add_pos]
    out = pl.pallas_call(
        body, grid=grid, in_specs=_specs(all_ins),
        out_specs=[pl.BlockSpec(o[1], o[2]) for o in out_desc],
        out_shape=[jax.ShapeDtypeStruct(o[0], dt) for o, dt in zip(out_desc, out_dtypes)],
        name=name, compiler_params=_cparams(len(grid)),
    )(*[a for a, _, _ in all_ins])
    return out


def _pick(dim, cands):
    for c in cands:
        if dim % c == 0:
            return c
    return dim


MM_VMEM_BUDGET = 40 * 1024 * 1024
MM_TILES = (1024, 512, 1408, 256, 128)


def mm(name, a, b, mode, add=None, out_dtype=F32, blocks=None, into=None):
    wide = None
    if mode == "nn":
        (m, kk), n = a.shape, b.shape[-1]
    elif mode == "nt":
        (m, kk), n = a.shape, b.shape[-2]
    else:
        (kk, m), n = a.shape, b.shape[1]
    if blocks is not None:
        lo, n_blk = blocks
        wide = b.shape[-1] if mode != "tn" else n // n_blk
        if mode == "nn":
            n = wide * n_blk
    tm = _pick(m, MM_TILES)
    if mode == "nt" and blocks is not None:
        tn, tk = _pick(n, MM_TILES), _pick(wide, MM_TILES[:-1])
    elif blocks is not None:
        tn, tk = _pick(wide, MM_TILES[:-1]), _pick(kk, MM_TILES)
    else:
        tn, tk = _pick(n, MM_TILES), _pick(kk, MM_TILES)
    if mode == "tn" or blocks is None:
        tk = _pick(kk, (2048,) + MM_TILES)
    if mode != "tn" and add is None and m % 2048 == 0 and (n // tn) * (kk // tk) > 1:
        windows = 2 * (2048 * tk * a.dtype.itemsize + tk * tn * b.dtype.itemsize + 2048 * tn * jnp.dtype(out_dtype).itemsize)
        if windows + 2048 * tn * 4 <= MM_VMEM_BUDGET:
            tm = 2048
    nk = kk // tk
    a_spec = pl.BlockSpec((tk, tm), lambda i, j, k: (k, i)) if mode == "tn" else pl.BlockSpec((tm, tk), lambda i, j, k: (i, k))
    o_spec = pl.BlockSpec((tm, tn), lambda i, j, k: (i, j))
    out_shape = (m, n)
    if blocks is None:
        b_spec = pl.BlockSpec((tn, tk), lambda i, j, k: (j, k)) if mode == "nt" else pl.BlockSpec((tk, tn), lambda i, j, k: (k, j))
    elif mode == "nn":
        per = wide // tn
        b_spec = pl.BlockSpec((1, tk, tn), lambda i, j, k: (lo + j // per, k, j % per))
    elif mode == "nt":
        per = wide // tk
        b_spec = pl.BlockSpec((1, tn, tk), lambda i, j, k: (lo + k // per, j, k % per))
    else:
        per = wide // tn
        total, first = (into[0], into[1]) if into is not None else (n_blk, 0)
        b_spec = pl.BlockSpec((tk, tn), lambda i, j, k: (k, j))
        o_spec = pl.BlockSpec((1, tm, tn), lambda i, j, k: (first + j // per, i, j % per))
        out_shape = (total, m, wide)

    def body(*refs):
        a_ref, b_ref = refs[0], refs[1]
        add_ref = refs[2] if add is not None else None
        o_ref, acc = refs[-2], refs[-1]
        k = pl.program_id(2)
        part = _bdot_impl(_tile(a_ref), _tile(b_ref), mode)

        @pl.when(k == 0)
        def _():
            acc[...] = part

        @pl.when(k > 0)
        def _():
            acc[...] += part

        @pl.when(k == nk - 1)
        def _():
            res = acc[...]
            if add_ref is not None:
                res = res + add_ref[...]
            o_ref[...] = res.astype(o_ref.dtype).reshape(o_ref.shape)

    operands = [a, b] + ([add] if add is not None else [])
    in_specs = [a_spec, b_spec] + ([o_spec] if add is not None else [])
    aliases = {}
    if into is not None and len(into) > 2:
        operands, in_specs, aliases = operands + [into[2]], in_specs + [pl.BlockSpec(memory_space=pl.ANY)], {len(operands): 0}
    return pl.pallas_call(
        body, grid=(m // tm, n // tn, nk), in_specs=in_specs, out_specs=o_spec,
        out_shape=jax.ShapeDtypeStruct(out_shape, out_dtype),
        scratch_shapes=[pltpu.VMEM((tm, tn), F32)], input_output_aliases=aliases,
        name=name, compiler_params=_cparams(3),
    )(*operands)


def _rows(x, width=None, off=0, tm=256):
    width = x.shape[1] if width is None else width
    return (x, (tm, width), lambda i, off=off: (i, off))


def _whole(x):
    nd = x.ndim
    return (x, x.shape, lambda *pids, nd=nd: (0,) * nd)


RMS_ROWS = 512


def _rms_ops(x, gain):
    return [_rows(x, tm=RMS_ROWS), _whole(gain)]


def rms_fwd(name, x, gain):
    s, dm = x.shape
    return tile_fwd(name, _rms_fn, (s // RMS_ROWS,), _rms_ops(x, gain), [((s, dm), BF16, (RMS_ROWS, dm), lambda i: (i, 0), ())])[0]


def rms_bwd(name, x, gain, dh, dres):
    s = x.shape[0]
    return tile_bwd(name, _rms_fn, (s // RMS_ROWS,), _rms_ops(x, gain), [_rows(dh, tm=RMS_ROWS)], [(0, ()), (1, (0,))],
                    adds={0: _rows(dres, tm=RMS_ROWS)})


def loss_call(y, t):
    s, dm = y.shape
    dy, part = tile_fwd("loss", _loss_fn, (s // RMS_ROWS,), [_rows(y, tm=RMS_ROWS), _rows(t, tm=RMS_ROWS)],
                        [((s, dm), F32, (RMS_ROWS, dm), lambda i: (i, 0), ()), ((8, LANES), F32, (8, LANES), lambda i: (0, 0), (0,))])
    return dy, part[0, 0]


def _fox_prep_ops(pm, gq, gk):
    tm = 512
    return [(pm, (tm, LANES), lambda i, j: (i, C_FQ // LANES + j)), (pm, (tm, LANES), lambda i, j: (i, C_FK // LANES + j)),
            _whole(gq), _whole(gk)]


def fox_prep_fwd(name, pm, gq, gk):
    s = pm.shape[0]
    out = ((s, BRANCH), BF16, (512, LANES), lambda i, j: (i, j), ())
    return tile_fwd(name, _fox_prep_fn, (s // 512, 4), _fox_prep_ops(pm, gq, gk), [out, out])


def fox_prep_bwd(name, pm, gq, gk, dqn, dkn):
    s = pm.shape[0]
    cot = lambda g: (g, (512, LANES), lambda i, j: (i, j))
    own = ((s, BRANCH), (512, LANES), lambda i, j: (i, j))
    return tile_bwd(name, _fox_prep_fn, (s // 512, 4), _fox_prep_ops(pm, gq, gk), [cot(dqn), cot(dkn)],
                    [(0, (), own, BF16), (1, (), own, BF16), (2, (0, 1)), (3, (0, 1))])


def _fox_gate_ops(f_t, bias):
    return [(f_t, (1,) + f_t.shape[1:], lambda h: (h, 0, 0)), (bias, (1, 1, 1), lambda h: (h, 0, 0))]


def fox_gate_fwd(name, f_t, bias):
    n_h = f_t.shape[0]
    return tile_fwd(name, _fox_gate_fn, (n_h,), _fox_gate_ops(f_t, bias),
                    [(f_t.shape, F32, (1,) + f_t.shape[1:], lambda h: (h, 0, 0), ())])[0]


def fox_gate_bwd(name, f_t, bias, dcum):
    n_h = f_t.shape[0]
    return tile_bwd(name, _fox_gate_fn, (n_h,), _fox_gate_ops(f_t, bias),
                    [(dcum, (1,) + f_t.shape[1:], lambda h: (h, 0, 0))], [(0, ()), (1, ())])


FOX_GROUPS = 4


def _fox_groups(s):
    per = s // FOX_BLOCK // FOX_GROUPS
    return [(g * per, per, (g + 1) * per * FOX_BLOCK) for g in range(FOX_GROUPS)]


def _fox_attn_ops(qn, kn, pm, cum_c, cum_r, q0, keys):
    nb = FOX_BLOCK
    return [(qn, (nb, LANES), lambda p, i: (q0 + i, p)), (kn, (keys, LANES), lambda p, i: (0, p)),
            (pm, (keys, LANES), lambda p, i: (0, C_FV // LANES + p)),
            (cum_c, (1, nb, 1), lambda p, i: (2 * p, q0 + i, 0)), (cum_c, (1, nb, 1), lambda p, i: (2 * p + 1, q0 + i, 0)),
            (cum_r, (1, 1, keys), lambda p, i: (2 * p, 0, 0)), (cum_r, (1, 1, keys), lambda p, i: (2 * p + 1, 0, 0))]


def fox_attn_fwd(name, qn, kn, pm, cum_c, cum_r):
    s = qn.shape[0]
    parts = []
    for g, (q0, n_q, keys) in enumerate(_fox_groups(s)):
        parts.append(tile_fwd(f"{name}_g{g}", functools.partial(_fox_attn_fn, q0), (4, n_q), _fox_attn_ops(qn, kn, pm, cum_c, cum_r, q0, keys),
                              [((n_q * FOX_BLOCK, BRANCH), BF16, (FOX_BLOCK, LANES), lambda p, i: (i, p), ())], raw=(0, 1, 2))[0])
    return jnp.concatenate(parts, axis=0)


def fox_attn_bwd(name, qn, kn, pm, cum_c, cum_r, dy):
    s = qn.shape[0]
    groups = _fox_groups(s)
    d_qn, by_q, tails = [None] * len(groups), [None] * len(groups), [None] * len(groups)
    below = None
    for g in reversed(range(len(groups))):
        q0, n_q, keys = groups[g]
        rows = n_q * FOX_BLOCK
        own_q = ((rows, BRANCH), (FOX_BLOCK, LANES), lambda p, i: (i, p))
        own_k = ((keys, BRANCH), (keys, LANES), lambda p, i: (0, p))
        pair_c = ((4, rows, 1), (1, FOX_BLOCK, 1), lambda p, i: (p, i, 0))
        pair_r = ((4, 1, keys), (1, 1, keys), lambda p, i: (p, 0, 0))
        adds = {}
        if below is not None:
            adds = {1: (below[0],) + own_k[1:], 2: (below[1],) + own_k[1:], 5: (below[2],) + pair_r[1:], 6: (below[3],) + pair_r[1:]}
        g_qn, g_kn, g_v, g_cqa, g_cqb, g_cka, g_ckb = tile_bwd(
            f"{name}_g{g}", functools.partial(_fox_attn_fn, q0), (4, n_q), _fox_attn_ops(qn, kn, pm, cum_c, cum_r, q0, keys),
            [(dy, (FOX_BLOCK, LANES), lambda p, i, q0=q0: (q0 + i, p))],
            [(0, (), own_q), (1, (1,), own_k), (2, (1,), own_k), (3, (), pair_c), (4, (), pair_c), (5, (1,), pair_r), (6, (1,), pair_r)],
            adds=adds)
        below = (g_kn, g_v, g_cka, g_ckb)
        lo = groups[g - 1][2] if g else 0
        d_qn[g] = g_qn
        by_q[g] = jnp.stack([g_cqa[:, :, 0], g_cqb[:, :, 0]], axis=1).reshape(8, rows)
        tails[g] = (g_kn[lo:], g_v[lo:], jnp.stack([g_cka[:, 0, lo:], g_ckb[:, 0, lo:]], axis=1).reshape(8, keys - lo))
    d_cum = jnp.concatenate(by_q, axis=1) + jnp.concatenate([t[2] for t in tails], axis=1)
    return jnp.concatenate(d_qn, axis=0), jnp.concatenate([t[0] for t in tails], axis=0), jnp.concatenate([t[1] for t in tails], axis=0), d_cum


def sconv_ops(pm, w):
    s = pm.shape[0]
    blk = lambda c0: (pm, (s, LANES), lambda j, c0=c0: (0, c0 // LANES + j))
    return [blk(C_SB), blk(C_SC), blk(C_SV), (w, (w.shape[0], LANES), lambda j: (0, j))]


def dnconv_ops(pm, w):
    s = pm.shape[0]
    return [(pm, (s, LANES), lambda j: (0, C_DN // LANES + j)), (w, (w.shape[0], LANES), lambda j: (0, j))]


def ffn_ops(ug, uv, w):
    s = ug.shape[0]
    n_t = D_FF // LANES
    return [(ug, (s, LANES), lambda j: (0, j)), (uv, (s, LANES), lambda j: (0, j)),
            (w, (w.shape[0], LANES), lambda j: (0, j)), (w, (w.shape[0], LANES), lambda j: (0, n_t + j))]


def _col_out(s, width, dtype=F32):
    return ((s, width), dtype, (s, LANES), lambda j: (0, j), ())


def _col_cot(g):
    return (g, (g.shape[0], LANES), lambda j: (0, j))


def merge_ops(yp, pm, tm=256):
    gate = lambda b: (pm, (tm, D_MODEL), lambda i, b=b: (i, C_GATE // D_MODEL + b))
    return [_rows(yp[0], tm=tm), _rows(yp[1], tm=tm), _rows(yp[2], tm=tm), gate(0), gate(1), gate(2)]


def ple_ops(gpre, pe, x):
    return [_rows(gpre, tm=RMS_ROWS), _rows(pe, tm=RMS_ROWS), _rows(x, tm=RMS_ROWS)]


def adam_call(name, w, g, m, v):
    shape = w.shape
    last = shape[-1]
    rows = w.size // last
    flat = lambda t: t.reshape(rows, last)
    tm = rows
    for cand in (512, 256, 128, 64, 32, 16, 8):
        if rows % cand == 0 and cand * last * 4 <= 2 * 1024 * 1024:
            tm = cand
            break
    spec = lambda t: (flat(t), (tm, last), lambda i: (i, 0))
    out = ((rows, last), F32, (tm, last), lambda i: (i, 0), ())
    res = tile_fwd(name, _adam_fn, (rows // tm,), [spec(w), spec(g), spec(m), spec(v)], [out, out, out])
    return [r.reshape(shape) for r in res]


def _adam_layers_fn(d, pids, w, m, v, g0, g1):
    g = jnp.where(pids[0] == 0, g0, g1)
    return (g,) + _adam_fn(d, pids, w, g, m, v)


def adam_layers(name, w, m, v, g0, g1):
    _, rows, cols = w.shape
    tm = _row_tile(rows, cols)
    n_t = rows // tm
    lay = lambda t: (t, (1, tm, cols), lambda l, i: (l, i, 0))
    ins = [lay(w), lay(m), lay(v), (g0, (tm, cols), lambda l, i: (i * (1 - l) + (n_t - 1) * l, 0)), (g1, (tm, cols), lambda l, i: (i * l, 0))]
    out = (w.shape, F32, (1, tm, cols), lambda l, i: (l, i, 0), ())
    return tile_fwd(name, _adam_layers_fn, (2, n_t), ins, [out, out, out, out])


def adam_w_in(name, w, m, v, g0, g1):
    rows, n_l, cols = w.shape

    def body(w_ref, m_ref, v_ref, g0_ref, g1_ref, g_out, d_out, m_out, v_out):
        step = 64

        def update(at):
            g0, g1 = g0_ref[at, :], g1_ref[at, :]
            layer = _iota((g0.shape[0], n_l, LANES), 1)
            g = jnp.where(layer == 0, g0[:, None, :], g1[:, None, :])
            delta, m2, v2 = _adam_fn(False, None, w_ref[at], g, m_ref[at], v_ref[at])
            for ref, val in ((g_out, g), (d_out, delta), (m_out, m2), (v_out, v2)):
                ref[at] = val

        def some_rows(i, carry):
            update(pl.ds(pl.multiple_of(i * step, step), step))
            return carry

        lax.fori_loop(0, rows // step, some_rows, 0)
        if rows % step:
            update(pl.ds(rows - rows % step, rows % step))

    both = pl.BlockSpec((rows, n_l, LANES), lambda j: (0, 0, j))
    one = pl.BlockSpec((rows, LANES), lambda j: (0, j))
    return pl.pallas_call(
        body, grid=(cols // LANES,), in_specs=[both, both, both, one, one], out_specs=[both] * 4,
        out_shape=[jax.ShapeDtypeStruct(w.shape, F32)] * 4, name=name, compiler_params=_cparams(1),
    )(w, m, v, g0, g1)


DN_GROUP = 4


def _dn_local_specs():
    rows = DN_GROUP * DN_CHUNK
    return [pl.BlockSpec((rows, 3 * BRANCH), lambda j: (j, 0)), pl.BlockSpec((rows, LANES), lambda j: (j, 0)),
            pl.BlockSpec((DN_GROUP, DN_HEADS, DN_CHUNK), lambda j: (j, 0, 0)), pl.BlockSpec((2, DN_HEADS), lambda j: (0, 0))]


def _dn_group_inputs(qkv, ps, a_rows, c):
    lo = c * DN_CHUNK
    heads = _split_heads(qkv[lo:lo + DN_CHUNK])
    return heads[0:4], heads[4:8], heads[8:12], ps[lo:lo + DN_CHUNK], a_rows[c]


def dn_local_fwd(name, dn_act, ps, a_rows, ad):
    s = dn_act.shape[0]
    n_c, n_g = s // DN_CHUNK, s // (DN_GROUP * DN_CHUNK)
    rows = DN_GROUP * DN_CHUNK

    def body(qkv_ref, ps_ref, ar_ref, ad_ref, u_ref, kc_ref, qd_ref, kd_ref, qk_ref, gl_ref):
        qkv, ps_v, a_rows_v, ad_v = qkv_ref[...], ps_ref[...], ar_ref[...], ad_ref[...]
        args = [[] for _ in range(8)]
        for c in range(DN_GROUP):
            q4, k4, v4, ps_c, ar_c = _dn_group_inputs(qkv, ps_v, a_rows_v, c)
            for lst, vals in zip(args, (q4, k4, v4) + _dn_gates(ps_c, ar_c, ad_v)):
                lst.extend(vals)
        everything = _dn_local(False, *args)
        for c in range(DN_GROUP):
            res = everything[c * DN_HEADS:(c + 1) * DN_HEADS]
            at = pl.ds(c * DN_CHUNK, DN_CHUNK)
            for ref, i in ((u_ref, 0), (kc_ref, 1), (qd_ref, 2), (kd_ref, 3)):
                ref[at, :] = jnp.concatenate([r[i] for r in res], axis=1)
            for h in range(DN_HEADS):
                qk_ref[c, h] = res[h][4]
            gl_ref[c] = _head_rows([r[5] for r in res])

    wide = pl.BlockSpec((rows, BRANCH), lambda j: (j, 0))
    return pl.pallas_call(
        body, grid=(n_g,), in_specs=_dn_local_specs(),
        out_specs=[wide, wide, wide, wide, pl.BlockSpec((DN_GROUP, DN_HEADS, DN_CHUNK, DN_CHUNK), lambda j: (j, 0, 0, 0)),
                   pl.BlockSpec((DN_GROUP, 8, LANES), lambda j: (j, 0, 0))],
        out_shape=[jax.ShapeDtypeStruct((s, BRANCH), F32)] * 4 + [jax.ShapeDtypeStruct((n_c, DN_HEADS, DN_CHUNK, DN_CHUNK), F32),
                                                                 jax.ShapeDtypeStruct((n_c, 8, LANES), F32)],
        name=name, compiler_params=_cparams(1),
    )(dn_act, ps, a_rows, ad)


def dn_local_bwd(name, dn_act, ps, a_rows, ad, cots):
    s = dn_act.shape[0]
    n_c, n_g = s // DN_CHUNK, s // (DN_GROUP * DN_CHUNK)
    rows = DN_GROUP * DN_CHUNK

    def body(qkv_ref, ps_ref, ar_ref, ad_ref, du_ref, dkc_ref, dqd_ref, dkd_ref, dqk_ref, dgl_ref, dqkv_ref, dps_ref, dar_ref, dad_ref):
        first = pl.program_id(0) == 0
        qkv, ps_v, a_rows_v, ad_v = qkv_ref[...], ps_ref[...], ar_ref[...], ad_ref[...]
        d_wide = [r[...] for r in (du_ref, dkc_ref, dqd_ref, dkd_ref)]
        qs, ks, vs, ps_cs, ar_cs, cot = [], [], [], [], [], []
        for c in range(DN_GROUP):
            q4, k4, v4, ps_c, ar_c = _dn_group_inputs(qkv, ps_v, a_rows_v, c)
            qs, ks, vs, ps_cs, ar_cs = qs + q4, ks + k4, vs + v4, ps_cs + [ps_c], ar_cs + [ar_c]
            lo = c * DN_CHUNK
            d_tiles = [_split_heads(t[lo:lo + DN_CHUNK]) for t in d_wide]
            d_gl = dgl_ref[c]
            cot += [(d_tiles[0][h], d_tiles[1][h], d_tiles[2][h], d_tiles[3][h], dqk_ref[c, h], _col(_row(d_gl, h), 0))
                    for h in range(DN_HEADS)]

        def f(qs, ks, vs, ps_cs, ar_cs, ad_v):
            gates = [[] for _ in range(5)]
            for ps_c, ar_c in zip(ps_cs, ar_cs):
                for lst, vals in zip(gates, _dn_gates(ps_c, ar_c, ad_v)):
                    lst.extend(vals)
            return _dn_local(True, qs, ks, vs, *gates)

        _, vjp = jax.vjp(f, qs, ks, vs, ps_cs, ar_cs, ad_v)
        d_q, d_k, d_v, d_ps, d_ar, d_ad = vjp(cot)
        for c in range(DN_GROUP):
            at, hs = pl.ds(c * DN_CHUNK, DN_CHUNK), slice(c * DN_HEADS, (c + 1) * DN_HEADS)
            dqkv_ref[at, :] = jnp.concatenate(d_q[hs] + d_k[hs] + d_v[hs], axis=1).astype(dqkv_ref.dtype)
            dps_ref[at, :] = d_ps[c]
            dar_ref[c] = d_ar[c]
        _store(dad_ref, d_ad, first)

    wide = pl.BlockSpec((rows, BRANCH), lambda j: (j, 0))
    specs = _dn_local_specs()
    return pl.pallas_call(
        body, grid=(n_g,),
        in_specs=specs + [wide, wide, wide, wide, pl.BlockSpec((DN_GROUP, DN_HEADS, DN_CHUNK, DN_CHUNK), lambda j: (j, 0, 0, 0)),
                          pl.BlockSpec((DN_GROUP, 8, LANES), lambda j: (j, 0, 0))],
        out_specs=specs,
        out_shape=[jax.ShapeDtypeStruct((s, 3 * BRANCH), F32), jax.ShapeDtypeStruct((s, LANES), F32),
                   jax.ShapeDtypeStruct((n_c, DN_HEADS, DN_CHUNK), F32), jax.ShapeDtypeStruct((2, DN_HEADS), F32)],
        name=name, compiler_params=_cparams(1),
    )(dn_act, ps, a_rows, ad, *cots)


def _dn_scan_specs(n_c, rev):
    idx = (lambda j: n_c - 1 - j) if rev else (lambda j: j)
    wide = pl.BlockSpec((DN_CHUNK, BRANCH), lambda j: (idx(j), 0))
    return [wide, wide, wide, wide, pl.BlockSpec((1, DN_HEADS, DN_CHUNK, DN_CHUNK), lambda j: (idx(j), 0, 0, 0)),
            pl.BlockSpec((1, 8, LANES), lambda j: (idx(j), 0, 0)), pl.BlockSpec((DN_CHUNK, BRANCH), lambda j: (idx(j), C_DZ // BRANCH)),
            pl.BlockSpec((1, DN_DH), lambda j: (0, 0))]


def _dn_scan_tiles(refs):
    u_ref, kc_ref, qd_ref, kd_ref, qk_ref, gl_ref, z_ref, g_ref = refs
    wide = [_split_heads(r[...]) for r in (u_ref, kc_ref, qd_ref, kd_ref)]
    gl = gl_ref[0]
    return [(wide[0][h], wide[1][h], wide[2][h], wide[3][h], qk_ref[0, h], _col(_row(gl, h), 0)) for h in range(DN_HEADS)], \
        _split_heads(z_ref[...].astype(F32)), g_ref[...]


def dn_scan_fwd(name, local, pm, gain):
    s = pm.shape[0]
    n_c = s // DN_CHUNK

    def body(*refs):
        y_ref, hist_ref, state = refs[8:]

        @pl.when(pl.program_id(0) == 0)
        def _():
            state[...] = jnp.zeros_like(state)

        hist_ref[0] = state[...]
        per_head, z4, gain_v = _dn_scan_tiles(refs[:8])
        ys, s_nexts = _dn_step(False, [state[h] for h in range(DN_HEADS)], per_head, z4, gain_v)
        for h in range(DN_HEADS):
            state[h] = s_nexts[h]
        y_ref[...] = jnp.concatenate(ys, axis=1).astype(y_ref.dtype)

    return pl.pallas_call(
        body, grid=(n_c,), in_specs=_dn_scan_specs(n_c, False),
        out_specs=[pl.BlockSpec((DN_CHUNK, BRANCH), lambda j: (j, 0)),
                   pl.BlockSpec((1, DN_HEADS, DN_DH, DN_DH), lambda j: (j, 0, 0, 0))],
        out_shape=[jax.ShapeDtypeStruct((s, BRANCH), BF16), jax.ShapeDtypeStruct((n_c, DN_HEADS, DN_DH, DN_DH), F32)],
        scratch_shapes=[pltpu.VMEM((DN_HEADS, DN_DH, DN_DH), F32)],
        name=name, compiler_params=_cparams(1),
    )(*local, pm, gain)


def dn_scan_bwd(name, local, pm, gain, hist, dy):
    s = pm.shape[0]
    n_c = s // DN_CHUNK

    def body(*refs):
        hist_ref, dy_ref = refs[8:10]
        du_ref, dkc_ref, dqd_ref, dkd_ref, dqk_ref, dgl_ref, dz_ref, dg_ref, d_state = refs[10:]
        first = pl.program_id(0) == 0

        @pl.when(first)
        def _():
            d_state[...] = jnp.zeros_like(d_state)

        per_head, z4, gain_v = _dn_scan_tiles(refs[:8])
        _, vjp = jax.vjp(functools.partial(_dn_step, True), [hist_ref[0, h] for h in range(DN_HEADS)], per_head, z4, gain_v)
        d_s, grads, d_z, d_gain = vjp((_split_heads(dy_ref[...].astype(F32)), [d_state[h] for h in range(DN_HEADS)]))
        for h in range(DN_HEADS):
            d_state[h] = d_s[h]
        for ref, i in ((du_ref, 0), (dkc_ref, 1), (dqd_ref, 2), (dkd_ref, 3)):
            ref[...] = jnp.concatenate([g[i] for g in grads], axis=1)
        dz_ref[...] = jnp.concatenate(d_z, axis=1).astype(dz_ref.dtype)
        for h in range(DN_HEADS):
            dqk_ref[0, h] = grads[h][4]
        dgl_ref[0] = _head_rows([g[5] for g in grads])
        _store(dg_ref, d_gain, first)

    rev = lambda j: n_c - 1 - j
    specs = _dn_scan_specs(n_c, True)
    return pl.pallas_call(
        body, grid=(n_c,),
        in_specs=specs + [pl.BlockSpec((1, DN_HEADS, DN_DH, DN_DH), lambda j: (rev(j), 0, 0, 0)),
                          pl.BlockSpec((DN_CHUNK, BRANCH), lambda j: (rev(j), 0))],
        out_specs=specs[:6] + [pl.BlockSpec((DN_CHUNK, BRANCH), lambda j: (rev(j), 0)), specs[7]],
        out_shape=[jax.ShapeDtypeStruct((s, BRANCH), F32)] * 4 + [
            jax.ShapeDtypeStruct((n_c, DN_HEADS, DN_CHUNK, DN_CHUNK), F32), jax.ShapeDtypeStruct((n_c, 8, LANES), F32),
            jax.ShapeDtypeStruct((s, BRANCH), BF16), jax.ShapeDtypeStruct((1, DN_DH), F32)],
        scratch_shapes=[pltpu.VMEM((DN_HEADS, DN_DH, DN_DH), F32)],
        name=name, compiler_params=_cparams(1),
    )(*local, pm, gain, hist, dy)


def _seq_layouts(cols, s):
    return cols.T.reshape(cols.shape[1], s // LANES, LANES)


def layer_fwd(li, x, p, w, more_weights=None):
    s = x.shape[0]
    n = lambda t: f"{t}_l{li}"
    h = rms_fwd(n("rms_mix"), x, w["g_mix"])
    pm = mm(n("in_main"), h, w["in_main"], "nn")
    ps = mm(n("in_small"), h, w["in_small"], "nn")
    qn, kn = fox_prep_fwd(n("fox_prep"), pm, w["gq"], w["gk"])
    f_t = _seq_layouts(ps[:, 0:8], s)
    cum = fox_gate_fwd(n("fox_gate"), f_t, w["b_f"])
    cum_c, cum_r = cum.reshape(8, s, 1), cum.reshape(8, 1, s)
    y_fox = fox_attn_fwd(n("fox_attn"), qn, kn, pm, cum_c, cum_r)
    y_sc = tile_fwd(n("sconv"), _sconv_fn, (BRANCH // LANES,), sconv_ops(pm, w["sc_conv_w"]), [_col_out(s, BRANCH, BF16)])[0]
    dn_act = tile_fwd(n("dnconv"), _dnconv_fn, (3 * BRANCH // LANES,), dnconv_ops(pm, w["dn_conv_w"]), [_col_out(s, 3 * BRANCH)])[0]
    a_rows = ps[:, 12:16].reshape(s // DN_CHUNK, DN_CHUNK, DN_HEADS).transpose(0, 2, 1)
    dn_local = dn_local_fwd(n("dn_local"), dn_act, ps, a_rows, w["ad"])
    y_dn, hist = dn_scan_fwd(n("dn_scan"), dn_local, pm, w["dn_gain"])
    ys = (y_fox, y_sc, y_dn)
    if more_weights is not None:
        w = {**w, **more_weights(y_dn)}
    yp = [mm(n(f"branch{b}"), ys[b], w["branch"][b], "nn", blocks=(0, N_CHIPS)) for b in range(3)]
    merged = tile_fwd(n("merge"), _merge_fn, (s // RMS_ROWS,), merge_ops(yp, pm, RMS_ROWS),
                      [((s, D_MODEL), BF16, (RMS_ROWS, D_MODEL), lambda i: (i, 0), ())])[0]
    x1 = mm(n("w_o"), merged, w["o"], "nn", add=x)
    h2 = rms_fwd(n("rms_ffn"), x1, w["g_ffn"])
    ug = mm(n("up_g"), h2, w["up"], "nn", blocks=(0, 2))
    uv = mm(n("up_v"), h2, w["up"], "nn", blocks=(2, 2))
    act = tile_fwd(n("ffn_act"), _ffn_act_fn, (D_FF // LANES,), ffn_ops(ug, uv, w["ffn_conv_w"]), [_col_out(s, D_FF, BF16)])[0]
    x2 = mm(n("down"), act, w["down"], "nn", add=x1)
    h3 = rms_fwd(n("rms_ple"), x2, w["g_ple"])
    gpre = mm(n("ple_gate"), h3, w["pg"], "nn")
    pe = mm(n("ple_emb"), p, w["ple"], "nn", blocks=(0, N_CHIPS))
    x3 = tile_fwd(n("ple"), _ple_fn, (s // RMS_ROWS,), ple_ops(gpre, pe, x2), [((s, D_MODEL), F32, (RMS_ROWS, D_MODEL), lambda i: (i, 0), ())])[0]
    saved = dict(x=x, h=h, pm=pm, ps=ps, qn=qn, kn=kn, f_t=f_t, cum_c=cum_c, cum_r=cum_r, ys=ys, dn_act=dn_act, dn_local=dn_local,
                 a_rows=a_rows, hist=hist, yp=yp, merged=merged, x1=x1, h2=h2, ug=ug, uv=uv, act=act, x2=x2, h3=h3,
                 gpre=gpre, pe=pe, p=p)
    return x3, saved, w


def hang_on(w, token):
    zero = token[0, 0]
    small = ("g_mix", "g_ffn", "g_ple", "gq", "gk", "b_f", "ad", "dn_gain", "sc_conv_w", "dn_conv_w", "ffn_conv_w")
    return {**w, **{k: w[k] + zero for k in small}}


def layer_bwd(li, dx3, sv, w, hooks=None):
    hooks = hooks or {}

    def stage(key, after, w):
        return hang_on(w, hooks[key](after, g)) if key in hooks else w

    s = dx3.shape[0]
    n = lambda t: f"{t}_l{li}"
    g = {}
    col_own = lambda width: ((s, width), (s, LANES), lambda j: (0, j))
    d_gpre, d_pe = tile_bwd(n("ple_bwd"), _ple_fn, (s // RMS_ROWS,), ple_ops(sv["gpre"], sv["pe"], sv["x2"]), [_rows(dx3, tm=RMS_ROWS)],
                            [(0, (), None, BF16), (1, (), None, BF16)])
    g["w_ple"] = mm(n("d_w_ple"), sv["p"], d_pe, "tn", blocks=(0, N_CHIPS))
    g["w_ple_gate"] = mm(n("d_w_pg"), sv["h3"], d_gpre, "tn").reshape(N_CHIPS, -1, D_MODEL)
    dh3 = mm(n("d_h3"), d_gpre, w["pg"], "nt")
    dx2, d_g_ple = rms_bwd(n("rms_ple_bwd"), sv["x2"], w["g_ple"], dh3, dx3)
    dact = mm(n("d_act"), dx2, w["down"], "nt")
    g["w_down"] = mm(n("d_w_down"), sv["act"], dx2, "tn").reshape(N_CHIPS, -1, D_MODEL)
    taps_own = ((w["ffn_conv_w"].shape[0], D_FF), (w["ffn_conv_w"].shape[0], LANES), lambda j: (0, j))
    d_ug, d_uv, d_fw_g, d_fw_v = tile_bwd(n("ffn_act_bwd"), _ffn_act_fn, (D_FF // LANES,), ffn_ops(sv["ug"], sv["uv"], w["ffn_conv_w"]),
                                          [_col_cot(dact)], [(0, (), None, BF16), (1, (), None, BF16), (2, (), taps_own), (3, (), taps_own)])
    g["ffn_conv_w"] = jnp.concatenate([d_fw_g, d_fw_v], axis=1)
    gate_half = mm(n("d_w_up_g"), sv["h2"], d_ug, "tn", blocks=(0, 2), into=(N_CHIPS, 0))
    g["w_up"] = mm(n("d_w_up_v"), sv["h2"], d_uv, "tn", blocks=(0, 2), into=(N_CHIPS, 2, gate_half))
    dh2 = mm(n("d_h2_v"), d_uv, w["up"], "nt", blocks=(2, 2), add=mm(n("d_h2_g"), d_ug, w["up"], "nt", blocks=(0, 2)))
    dx1, d_g_ffn = rms_bwd(n("rms_ffn_bwd"), sv["x1"], w["g_ffn"], dh2, dx2)
    w = stage("mid", dx1, w)
    dmerged = mm(n("d_merged"), dx1, w["o"], "nt")
    g["w_o"] = mm(n("d_w_o"), sv["merged"], dx1, "tn").reshape(N_CHIPS, -1, D_MODEL)
    gate_own = ((s, D_MODEL), (256, D_MODEL), lambda i: (i, 0))
    d_yp0, d_yp1, d_yp2, d_g0, d_g1, d_g2 = tile_bwd(
        n("merge_bwd"), _merge_fn, (s // 256,), merge_ops(sv["yp"], sv["pm"]), [_rows(dmerged)],
        [(0, (), None, BF16), (1, (), None, BF16), (2, (), None, BF16), (3, (), gate_own, BF16), (4, (), gate_own, BF16), (5, (), gate_own, BF16)])
    d_yp = (d_yp0, d_yp1, d_yp2)
    g["w_branch"] = jnp.concatenate([mm(n(f"d_w_branch{b}"), sv["ys"][b], d_yp[b], "tn", blocks=(0, N_CHIPS)) for b in range(3)], axis=1)
    d_ys = [mm(n(f"d_y{b}"), d_yp[b], w["branch"][b], "nt", blocks=(0, N_CHIPS)) for b in range(3)]
    w = stage("late", d_ys[2], w)
    *d_local, d_z, d_dngain = dn_scan_bwd(n("dn_scan_bwd"), sv["dn_local"], sv["pm"], w["dn_gain"], sv["hist"], d_ys[2])
    d_dnact, d_ps_dn, d_arows, d_ad = dn_local_bwd(n("dn_local_bwd"), sv["dn_act"], sv["ps"], sv["a_rows"], w["ad"], d_local)
    g["ad"], g["dn_norm_gain"] = d_ad, d_dngain[0]
    d_dnqkv, g["dn_conv_w"] = tile_bwd(n("dnconv_bwd"), _dnconv_fn, (3 * BRANCH // LANES,), dnconv_ops(sv["pm"], w["dn_conv_w"]),
                                       [_col_cot(d_dnact)], [(0, (), col_own(3 * BRANCH), BF16), (1, ())])
    d_sb, d_sc, d_sv, g["sc_conv_w"] = tile_bwd(n("sconv_bwd"), _sconv_fn, (BRANCH // LANES,), sconv_ops(sv["pm"], w["sc_conv_w"]), [_col_cot(d_ys[1])],
                                                [(0, (), col_own(BRANCH), BF16), (1, (), col_own(BRANCH), BF16), (2, (), col_own(BRANCH), BF16), (3, ())])
    w = stage("last", d_dnqkv, w)
    d_qn, d_kn, d_fv, d_cum = fox_attn_bwd(n("fox_attn_bwd"), sv["qn"], sv["kn"], sv["pm"], sv["cum_c"], sv["cum_r"], d_ys[0])
    d_ft, d_bf = fox_gate_bwd(n("fox_gate_bwd"), sv["f_t"], w["b_f"], d_cum.reshape(8, s // LANES, LANES))
    g["b_fox_f"] = d_bf.reshape(8)
    d_fq, d_fk, d_gq, d_gk = fox_prep_bwd(n("fox_prep_bwd"), sv["pm"], w["gq"], w["gk"], d_qn, d_kn)
    g["fox_q_gain"] = d_gq[0, :FOX_DH] + d_gq[0, FOX_DH:]
    g["fox_k_gain"] = d_gk[0, :FOX_DH] + d_gk[0, FOX_DH:]
    d_pm = jnp.concatenate([d_fq, d_fk, d_fv.astype(BF16), d_sb, d_sc, d_sv, d_dnqkv, d_z, d_g0, d_g1, d_g2], axis=1)
    d_a_cols = d_arows.transpose(0, 2, 1).reshape(s, DN_HEADS)
    d_f_cols = d_ft.reshape(8, s).T
    d_ps = d_ps_dn + jnp.concatenate([d_f_cols, jnp.zeros((s, 4), F32), d_a_cols, jnp.zeros((s, LANES - 16), F32)], axis=1)
    g["w_in"] = chip_blocks_w_in(mm(n("d_w_in_main"), d_pm, sv["h"], "tn"), mm(n("d_w_in_small"), d_ps, sv["h"], "tn"))
    w = stage("w_in", g["w_in"], w)
    dh = mm(n("d_h_small"), d_ps, w["in_small"], "nt", add=mm(n("d_h_main"), d_pm, w["in_main"], "nt"))
    dx, d_g_mix = rms_bwd(n("rms_mix_bwd"), sv["x"], w["g_mix"], dh, dx1)
    g["g_mix"], g["g_ffn"], g["g_ple"] = d_g_mix[0], d_g_ffn[0], d_g_ple[0]
    return dx, g


IN_SHARD = 2052
MAIN_RANGES = ((0, 1536), (1544, 3080), (3080, 4616), (4624, 5136), (5136, 8208))
SMALL_RANGES = ((1536, 1544), (4616, 4620), (4620, 4624))


def _from_chip_blocks(blocks, ranges):
    parts = []
    for lo, hi in ranges:
        for k in range(N_CHIPS):
            a0, a1 = max(lo, k * IN_SHARD), min(hi, (k + 1) * IN_SHARD)
            if a0 < a1:
                parts.append(blocks[k][:, a0 - k * IN_SHARD:a1 - k * IN_SHARD])
    return parts


def split_w_in(blocks):
    main = jnp.concatenate(_from_chip_blocks(blocks, MAIN_RANGES), axis=1)
    pad = jnp.zeros((blocks.shape[1], LANES - 16), blocks.dtype)
    return main, jnp.concatenate(_from_chip_blocks(blocks, SMALL_RANGES) + [pad], axis=1)


def chip_blocks_w_in(main, small):
    ranges = sorted([(lo, hi, "m") for lo, hi in MAIN_RANGES] + [(lo, hi, "s") for lo, hi in SMALL_RANGES])
    offs, m_off, s_off = {}, 0, 0
    for lo, hi in MAIN_RANGES:
        offs[lo] = m_off
        m_off += hi - lo
    for lo, hi in SMALL_RANGES:
        offs[lo] = s_off
        s_off += hi - lo
    blocks = []
    for k in range(N_CHIPS):
        parts = []
        for lo, hi, src in ranges:
            a0, a1 = max(lo, k * IN_SHARD), min(hi, (k + 1) * IN_SHARD)
            if a0 < a1:
                arr = main if src == "m" else small
                parts.append(arr[offs[lo] + a0 - lo:offs[lo] + a1 - lo])
        blocks.append(jnp.concatenate(parts, axis=0))
    return jnp.stack(blocks)


def later_weights(got):
    g_branch, g_o, g_up, g_down, g_pg, g_ple = got
    branch = g_branch.reshape(N_CHIPS, 3, BRANCH, -1)
    return dict(branch=[branch[:, b] for b in range(3)], o=g_o.reshape(D_MODEL, D_MODEL), up=g_up,
                down=g_down.reshape(D_FF, D_MODEL), pg=g_pg.reshape(D_MODEL, D_MODEL), ple=g_ple)


def layer_weights(li, got, conv, a):
    main, small = split_w_in(got[0])
    tile2 = lambda v: jnp.concatenate([v, v])[None, :]
    rest = later_weights(got[1:]) if len(got) > 1 else {}
    return dict(
        in_main=main, in_small=small, **rest,
        g_mix=a["g_mix"][li][None, :], g_ffn=a["g_ffn"][li][None, :], g_ple=a["g_ple"][li][None, :],
        gq=tile2(a["fox_q_gain"][li]), gk=tile2(a["fox_k_gain"][li]), b_f=a["b_fox_f"][li].reshape(8, 1, 1),
        ad=jnp.stack([a["dn_a_log"][li], a["dn_dt_bias"][li]]), dn_gain=a["dn_norm_gain"][li][None, :],
        sc_conv_w=conv["sc_conv_w"][li], dn_conv_w=conv["dn_conv_w"][li], ffn_conv_w=conv["ffn_conv_w"][li])


def pack_rows(arrs, dtype):
    flat = jnp.concatenate([t.reshape(-1).astype(dtype) for t in arrs])
    pad = (-flat.shape[0]) % (8 * LANES)
    if pad:
        flat = jnp.concatenate([flat, jnp.zeros((pad,), dtype)])
    return flat.reshape(-1, LANES)


def unpack_rows(buf, shapes):
    flat = buf.reshape(-1)
    out, off = [], 0
    for shp in shapes:
        size = 1
        for dim in shp:
            size *= dim
        out.append(flat[off:off + size].reshape(shp))
        off += size
    return out


ANY = pl.BlockSpec(memory_space=pl.ANY)


def _position():
    x, y, c = lax.axis_index("x"), lax.axis_index("y"), lax.axis_index("c")
    return x, y, c, [(1 - x, y), (x, 1 - y), (1 - x, 1 - y)]


def gather_small(name, block):
    m_per, n = block.shape

    def body(x_ref, out_ref, token, send_sems, recv_sems, local_sem):
        token[...] = jnp.zeros_like(token)
        x, y, c, chips = _position()
        me, sibling = (x, y, c), (x, y, 1 - c)

        def rows(px, py, pc):
            return out_ref.at[pl.ds((4 * px + 2 * py + pc) * m_per, m_per), :]

        def copy(k, blk, to, src=None):
            return pltpu.make_async_remote_copy(src_ref=rows(*blk) if src is None else src, dst_ref=rows(*blk),
                                                send_sem=send_sems.at[k], recv_sem=recv_sems.at[k], device_id=to, device_id_type=MESH)

        mine = pltpu.make_async_copy(x_ref, rows(*me), local_sem)
        mine.start()
        first = [copy(0, me, sibling, src=x_ref)] + [copy(1 + j, me, (*chip, c), src=x_ref) for j, chip in enumerate(chips)]
        for cp in first:
            cp.start()
        passed = [copy(4 + j, (*chip, c), sibling) for j, chip in enumerate(chips)]
        for j, chip in enumerate(chips):
            copy(1 + j, (*chip, c), me).wait_recv()
            passed[j].start()
        copy(0, sibling, me).wait_recv()
        for j, chip in enumerate(chips):
            copy(4 + j, (*chip, 1 - c), me).wait_recv()
        for cp in first + passed:
            cp.wait_send()
        mine.wait()

    in_vmem = pl.BlockSpec(memory_space=pltpu.VMEM)
    return pl.pallas_call(
        body, out_shape=[jax.ShapeDtypeStruct((8 * m_per, n), block.dtype), jax.ShapeDtypeStruct((8, LANES), F32)],
        in_specs=[in_vmem], out_specs=[in_vmem, in_vmem],
        scratch_shapes=[pltpu.SemaphoreType.DMA((7,)), pltpu.SemaphoreType.DMA((7,)), pltpu.SemaphoreType.DMA],
        name=name, compiler_params=pltpu.CompilerParams(vmem_limit_bytes=VMEM_LIMIT),
    )(block)


def _sems(n):
    return [pltpu.SemaphoreType.DMA((n,)), pltpu.SemaphoreType.DMA((n,))]


def _split_cols(rows):
    return (rows // 2) % 16 != 0


def _half(ref, which, lead=()):
    rows, cols = ref.shape[-2:]
    if _split_cols(rows):
        return ref.at[(*lead, slice(None), pl.ds(which * (cols // 2), cols // 2))]
    return ref.at[(*lead, pl.ds(which * (rows // 2), rows // 2), slice(None))]


def _half_shape(rows, cols):
    return (rows, cols // 2) if _split_cols(rows) else (rows // 2, cols)


def forward_halves(name, lands):
    n_w = len(lands)

    def body(*refs):
        outs = refs[n_w:2 * n_w]
        send_sems, recv_sems = refs[2 * n_w:]
        x, y, c, chips = _position()

        def copy(w, j, pc):
            cx, cy = chips[j]
            part = _half(outs[w], pc, (2 * cx + cy,))
            return pltpu.make_async_remote_copy(src_ref=part, dst_ref=part, send_sem=send_sems.at[3 * w + j], recv_sem=recv_sems.at[3 * w + j],
                                                device_id=(x, y, 1 - c), device_id_type=MESH)

        pairs = [(w, j) for w in range(n_w) for j in range(3)]
        for w, j in pairs:
            copy(w, j, c).start()
        for w, j in pairs:
            copy(w, j, 1 - c).wait_recv()
            copy(w, j, c).wait_send()

    return pl.pallas_call(
        body, out_shape=[jax.ShapeDtypeStruct(t.shape, t.dtype) for t in lands], in_specs=[ANY] * n_w, out_specs=[ANY] * n_w,
        input_output_aliases={w: w for w in range(n_w)}, scratch_shapes=_sems(3 * n_w), name=name,
    )(*lands)


def share_halves(name, bufs):
    n_w = len(bufs)

    def body(*refs):
        outs = refs[n_w:2 * n_w]
        send_sems, recv_sems = refs[2 * n_w:]
        x, y, c, _ = _position()

        def copy(w, pc):
            half = _half(outs[w], pc)
            return pltpu.make_async_remote_copy(src_ref=half, dst_ref=half, send_sem=send_sems.at[w], recv_sem=recv_sems.at[w],
                                                device_id=(x, y, 1 - c), device_id_type=MESH)

        for w in range(n_w):
            copy(w, c).start()
        for w in range(n_w):
            copy(w, 1 - c).wait_recv()
            copy(w, c).wait_send()

    return pl.pallas_call(
        body, out_shape=[jax.ShapeDtypeStruct(b.shape, b.dtype) for b in bufs], in_specs=[ANY] * n_w, out_specs=[ANY] * n_w,
        input_output_aliases={w: w for w in range(n_w)}, scratch_shapes=_sems(n_w), name=name,
    )(*bufs)


HBM = pl.BlockSpec(memory_space=pltpu.HBM)
SEM = pl.BlockSpec(memory_space=pltpu.SEMAPHORE)
EFFECT = pltpu.SideEffectType.DATAFLOW_SIDE_EFFECTING


def _exchange_copies(kind, srcs, lands):
    x, y, c, chips = _position()
    out = []
    for src, land in zip(srcs, lands):
        if kind == "swap":
            out.append((_half(src, 1 - c, (slice(None),)), land, (x, y, 1 - c)))
            continue
        for j, (cx, cy) in enumerate(chips):
            if kind == "gather":
                out.append((src, land.at[2 * x + y], (cx, cy, c)))
            elif kind == "gather_half":
                out.append((_half(src, c), _half(land, c, (2 * x + y,)), (cx, cy, c)))
            else:
                out.append((src.at[2 * cx + cy], land.at[j], (cx, cy, c)))
    return out


def _land_shapes(kind, srcs):
    if kind in ("gather", "gather_half"):
        return [(N_CHIPS,) + s.shape for s in srcs]
    if kind == "swap":
        return [(N_CHIPS,) + _half_shape(*s.shape[1:]) for s in srcs]
    return [(3,) + s.shape[1:] for s in srcs]


def exchange_start(name, kind, srcs):
    n_w = len(srcs)
    shapes = _land_shapes(kind, srcs)
    n_sem = n_w if kind == "swap" else 3 * n_w

    def body(*refs):
        ins, lands = refs[:n_w], refs[n_w:2 * n_w]
        send_sems, recv_sems = refs[2 * n_w:2 * n_w + 2]
        token = refs[-1]
        for i, (src, dst, dev) in enumerate(_exchange_copies(kind, ins, lands)):
            pltpu.make_async_remote_copy(src_ref=src, dst_ref=dst, send_sem=send_sems.at[i], recv_sem=recv_sems.at[i],
                                         device_id=dev, device_id_type=MESH).start()
        token[...] = jnp.zeros_like(token)

    out = pl.pallas_call(
        body, name=name,
        out_shape=(pltpu.SemaphoreType.DMA((n_sem,)), pltpu.SemaphoreType.DMA((n_sem,)),
                   *[pltpu.HBM(s.shape, s.dtype) for s in srcs], *[pltpu.HBM(shp, s.dtype) for shp, s in zip(shapes, srcs)],
                   jax.ShapeDtypeStruct((8, LANES), F32)),
        in_specs=(HBM,) * (2 * n_w), out_specs=(SEM, SEM) + (HBM,) * (2 * n_w) + (pl.BlockSpec(memory_space=pltpu.VMEM),),
        input_output_aliases={i: 2 + i for i in range(2 * n_w)},
        compiler_params=pltpu.CompilerParams(has_side_effects=EFFECT),
    )(*[pltpu.with_memory_space_constraint(s, pltpu.HBM) for s in srcs],
      *[pltpu.with_memory_space_constraint(lax.empty(shp, s.dtype), pltpu.HBM) for shp, s in zip(shapes, srcs)])
    return (kind, n_w, out[:-1]), out[-1]


def exchange_wait(name, handle, after):
    kind, n_w, (send_sems, recv_sems, *thru) = handle

    def body(*refs):
        ins, lands = refs[:n_w], refs[n_w:2 * n_w]
        send_sems, recv_sems = refs[2 * n_w:2 * n_w + 2]
        for i, (src, dst, dev) in enumerate(_exchange_copies(kind, ins, lands)):
            cp = pltpu.make_async_remote_copy(src_ref=src, dst_ref=dst, send_sem=send_sems.at[i], recv_sem=recv_sems.at[i],
                                              device_id=dev, device_id_type=MESH)
            cp.wait_send()
            cp.wait_recv()

    out = pl.pallas_call(
        body, name=name, out_shape=tuple(pltpu.HBM(t.shape, t.dtype) for t in thru),
        in_specs=(HBM,) * (2 * n_w) + (SEM, SEM, pl.BlockSpec(memory_space=pl.ANY)), out_specs=(HBM,) * (2 * n_w),
        input_output_aliases={i: i for i in range(2 * n_w)},
        compiler_params=pltpu.CompilerParams(has_side_effects=EFFECT),
    )(*thru, send_sems, recv_sems, after)
    return list(out[:n_w]), list(out[n_w:])


def _row_tile(rows, cols):
    best = rows
    if rows * cols * 4 <= 2 * 1024 * 1024:
        return rows
    for t in range(16, rows, 16):
        if rows % t == 0 and t * cols * 4 <= 2 * 1024 * 1024:
            best = t
    return best


def pair_sum(name, pos, grad, from_sibling):
    _, rows, cols = grad.shape
    h_rows, h_cols = _half_shape(rows, cols)
    tr = _row_tile(h_rows, h_cols)
    n_t = h_rows // tr

    def body(pos_ref, g_ref, s_ref, b_ref, f_ref):
        tot = g_ref[...] + s_ref[...]
        b_ref[...] = tot.astype(BF16)

        @pl.when(pl.program_id(1) == pos_ref[1])
        def _():
            f_ref[...] = tot[0]

    blk = pl.BlockSpec((1, tr, h_cols), lambda i, k, pos: (k, i, 0))
    if _split_cols(rows):
        mine = pl.BlockSpec((1, tr, h_cols), lambda i, k, pos: (k, i, pos[0]))
    else:
        mine = pl.BlockSpec((1, tr, h_cols), lambda i, k, pos: (k, pos[0] * n_t + i, 0))
    return pl.pallas_call(
        body, grid_spec=pltpu.PrefetchScalarGridSpec(
            num_scalar_prefetch=1, grid=(n_t, N_CHIPS), in_specs=[mine, blk],
            out_specs=[blk, pl.BlockSpec((tr, h_cols), lambda i, k, pos: (i, 0))]),
        out_shape=[jax.ShapeDtypeStruct((N_CHIPS, h_rows, h_cols), BF16), jax.ShapeDtypeStruct((h_rows, h_cols), F32)],
        name=name, compiler_params=_cparams(2),
    )(pos, grad, from_sibling)


def chip_sum(name, pos, own, landed, split_cols):
    half, cols = own.shape
    tr = _row_tile(half, cols)
    n_t = half // tr

    def body(pos_ref, p_ref, l_ref, o_ref):
        o_ref[...] = ((p_ref[...] + l_ref[0].astype(F32)) + l_ref[1].astype(F32)) + l_ref[2].astype(F32)

    if split_cols:
        out_spec, out_shape = pl.BlockSpec((tr, cols), lambda i, pos: (i, pos[0])), (half, 2 * cols)
    else:
        out_spec, out_shape = pl.BlockSpec((tr, cols), lambda i, pos: (pos[0] * n_t + i, 0)), (2 * half, cols)
    return pl.pallas_call(
        body, grid_spec=pltpu.PrefetchScalarGridSpec(
            num_scalar_prefetch=1, grid=(n_t,),
            in_specs=[pl.BlockSpec((tr, cols), lambda i, pos: (i, 0)), pl.BlockSpec((3, tr, cols), lambda i, pos: (0, i, 0))],
            out_specs=out_spec),
        out_shape=jax.ShapeDtypeStruct(out_shape, F32), name=name, compiler_params=_cparams(1),
    )(pos, own, landed)


class OverlappedReduceScatter:
    def __init__(self, tag, pos, grads):
        self.n = lambda t: f"{t}_{tag}"
        self.pos, self.grads = pos, grads
        self.swap, self.token = exchange_start(self.n("swap_start"), "swap", grads)

    def middle(self, after):
        self.grads, from_sibling = exchange_wait(self.n("swap_wait"), self.swap, after)
        self.sums = [pair_sum(self.n(f"pair_sum{w}"), self.pos, g, s) for w, (g, s) in enumerate(zip(self.grads, from_sibling))]
        self.scatter, self.token = exchange_start(self.n("scatter_start"), "scatter", [b for b, _ in self.sums])

    def finish(self, after):
        _, landed = exchange_wait(self.n("scatter_wait"), self.scatter, after)
        halves = [chip_sum(self.n(f"chip_sum{w}"), self.pos, own, l, _split_cols(g.shape[1]))
                  for w, ((_, own), l, g) in enumerate(zip(self.sums, landed, self.grads))]
        return share_halves(self.n("share_halves"), halves)


def sum_devices(gathered):
    m_per = gathered.shape[0] // 8

    def body(g_ref, o_ref):
        tot = g_ref[pl.ds(0, m_per), :]
        for dev in range(1, 8):
            tot = tot + g_ref[pl.ds(dev * m_per, m_per), :]
        o_ref[...] = tot

    return pl.pallas_call(
        body, out_shape=jax.ShapeDtypeStruct((m_per, gathered.shape[1]), F32),
        in_specs=[pl.BlockSpec(memory_space=pltpu.VMEM)], out_specs=pl.BlockSpec(memory_space=pltpu.VMEM), name="sum_devices",
    )(gathered)


def kernel(x, p, g_mix, w_in, b_fox_f, fox_q_gain, fox_k_gain, sc_conv_w, dn_conv_w, dn_a_log, dn_dt_bias, dn_norm_gain, w_branch, w_o, g_ffn, w_up, ffn_conv_w, w_down, g_ple, w_ple_gate, w_ple, loss_target, m_g_mix, m_w_in, m_b_fox_f, m_fox_q_gain, m_fox_k_gain, m_sc_conv_w, m_dn_conv_w, m_dn_a_log, m_dn_dt_bias, m_dn_norm_gain, m_w_branch, m_w_o, m_g_ffn, m_w_up, m_ffn_conv_w, m_w_down, m_g_ple, m_w_ple_gate, m_w_ple, v_g_mix, v_w_in, v_b_fox_f, v_fox_q_gain, v_fox_k_gain, v_sc_conv_w, v_dn_conv_w, v_dn_a_log, v_dn_dt_bias, v_dn_norm_gain, v_w_branch, v_w_o, v_g_ffn, v_w_up, v_ffn_conv_w, v_w_down, v_g_ple, v_w_ple_gate, v_w_ple):
    a = dict(g_mix=g_mix, w_in=w_in, b_fox_f=b_fox_f, fox_q_gain=fox_q_gain, fox_k_gain=fox_k_gain, sc_conv_w=sc_conv_w,
             dn_conv_w=dn_conv_w, dn_a_log=dn_a_log, dn_dt_bias=dn_dt_bias, dn_norm_gain=dn_norm_gain, w_branch=w_branch, w_o=w_o,
             g_ffn=g_ffn, w_up=w_up, ffn_conv_w=ffn_conv_w, w_down=w_down, g_ple=g_ple, w_ple_gate=w_ple_gate, w_ple=w_ple)
    mom = dict(g_mix=m_g_mix, w_in=m_w_in, b_fox_f=m_b_fox_f, fox_q_gain=m_fox_q_gain, fox_k_gain=m_fox_k_gain, sc_conv_w=m_sc_conv_w,
               dn_conv_w=m_dn_conv_w, dn_a_log=m_dn_a_log, dn_dt_bias=m_dn_dt_bias, dn_norm_gain=m_dn_norm_gain, w_branch=m_w_branch,
               w_o=m_w_o, g_ffn=m_g_ffn, w_up=m_w_up, ffn_conv_w=m_ffn_conv_w, w_down=m_w_down, g_ple=m_g_ple, w_ple_gate=m_w_ple_gate,
               w_ple=m_w_ple)
    var = dict(g_mix=v_g_mix, w_in=v_w_in, b_fox_f=v_b_fox_f, fox_q_gain=v_fox_q_gain, fox_k_gain=v_fox_k_gain, sc_conv_w=v_sc_conv_w,
               dn_conv_w=v_dn_conv_w, dn_a_log=v_dn_a_log, dn_dt_bias=v_dn_dt_bias, dn_norm_gain=v_dn_norm_gain, w_branch=v_w_branch,
               w_o=v_w_o, g_ffn=v_g_ffn, w_up=v_w_up, ffn_conv_w=v_ffn_conv_w, w_down=v_w_down, g_ple=v_g_ple, w_ple_gate=v_w_ple_gate,
               w_ple=v_w_ple)
    cx, cy, cc = lax.axis_index("x"), lax.axis_index("y"), lax.axis_index("c")
    chip = 2 * cx + cy
    pos = jnp.stack([cc, chip]).astype(jnp.int32)

    def as_blocks(t):
        return t.reshape(2, -1, t.shape[-1])

    def own_block_in(got, shards):
        return [lax.dynamic_update_slice(g, s[None], (chip, 0, 0)) for g, s in zip(got, shards)]

    conv_shapes = [a[nm].shape for nm in CONVS]
    conv_all, conv_token = gather_small("gather_conv_w", pack_rows([a[nm] for nm in CONVS], F32))
    def w_in_block(li, token):
        stored = jnp.transpose(a["w_in"], (2, 0, 1))[:, li, :]
        return (stored + token[0, 0]).astype(BF16).T

    w_in0 = [w_in_block(0, conv_token)]
    gather_in0, gather_in0_token = exchange_start("gather_start_w_in_l0", "gather_half", w_in0)
    shards0 = w_in0 + [(as_blocks(a[nm])[0] + gather_in0_token[0, 0]).astype(BF16) for nm in BIG[1:]]
    gather0, gather0_token = exchange_start("gather_start_l0", "gather", shards0[1:])
    shards1 = [w_in_block(1, gather0_token)] + [(as_blocks(a[nm])[1] + gather0_token[0, 0]).astype(BF16) for nm in BIG[1:]]
    gather1, gather1_in_token = exchange_start("gather_start_w_in_l1", "gather", shards1[:1])
    shards1[1:] = [s + gather1_in_token[0, 0].astype(BF16) for s in shards1[1:]]
    gather1_rest, gather1_token = exchange_start("gather_start_l1", "gather", shards1[1:])
    conv_rows = conv_all.shape[0] // 8
    conv_chip = [unpack_rows(conv_all[2 * k * conv_rows:(2 * k + 1) * conv_rows], conv_shapes) for k in range(N_CHIPS)]
    conv = {nm: jnp.concatenate([conv_chip[k][i] for k in range(N_CHIPS)], axis=2) for i, nm in enumerate(CONVS)}

    weights, saved = [None, None], [None, None]
    mine_in0, got_in0 = exchange_wait("gather_wait_w_in_l0", gather_in0, gather1_token)
    got_in0 = forward_halves("forward_w_in_l0", got_in0)
    first_weights = hang_on(layer_weights(0, own_block_in(got_in0, mine_in0), conv, a), gather1_token)

    def rest_of_layer0(after):
        mine, got = exchange_wait("gather_wait_l0", gather0, after)
        return later_weights(own_block_in(got, mine))

    act, saved[0], weights[0] = layer_fwd(0, x[0], p[0, 0], first_weights, more_weights=rest_of_layer0)
    mine1, got1 = exchange_wait("gather_wait_w_in_l1", gather1, act)

    def rest_of_layer1(after):
        mine, got = exchange_wait("gather_wait_l1", gather1_rest, after)
        return later_weights(own_block_in(got, mine))

    act, saved[1], weights[1] = layer_fwd(1, act, p[1, 0], layer_weights(1, own_block_in(got1, mine1), conv, a),
                                          more_weights=rest_of_layer1)
    d_act, loss_part = loss_call(act, loss_target[0])
    loss = lax.psum(loss_part, ("x", "y", "c"))
    layer_grads = [None, None]
    d_act, layer_grads[1] = layer_bwd(1, d_act, saved[1], weights[1])
    rs1 = OverlappedReduceScatter("l1", pos, [layer_grads[1][nm] for nm in BIG])
    rs0 = []

    def stage_mid(after, g):
        rs1.middle(after)
        return rs1.token

    def stage_late(after, g):
        rs0.append(OverlappedReduceScatter("l0", pos, [g[nm] for nm in BIG[1:]]))
        return rs0[0].token

    def stage_last(after, g):
        rs0[0].middle(after)
        return rs0[0].token

    def stage_w_in(after, g):
        rs0.append(OverlappedReduceScatter("w_in_l0", pos, [g["w_in"]]))
        return rs0[1].token

    d_act, layer_grads[0] = layer_bwd(0, d_act, saved[0], hang_on(weights[0], rs1.token),
                                      hooks=dict(mid=stage_mid, late=stage_late, last=stage_last, w_in=stage_w_in))
    rs0[1].middle(d_act)
    reduced = [rs0[0].finish(rs0[1].token), rs1.finish(rs0[1].token)]
    grad_x = d_act[None]

    def both(nm):
        return jnp.stack([layer_grads[0][nm], layer_grads[1][nm]])

    local = {nm: both(nm) for nm in ("g_mix", "b_fox_f", "fox_q_gain", "fox_k_gain", "dn_norm_gain", "g_ffn", "g_ple", "sc_conv_w",
                                      "dn_conv_w", "ffn_conv_w")}
    local["dn_a_log"] = jnp.stack([layer_grads[li]["ad"][0] for li in range(2)])
    local["dn_dt_bias"] = jnp.stack([layer_grads[li]["ad"][1] for li in range(2)])

    small_names = SMALL + CONVS
    small_shapes = [local[nm].shape for nm in small_names]
    small_sum = sum_devices(gather_small("gather_small_grads", pack_rows([local[nm] for nm in small_names], F32))[0])
    small_grads = dict(zip(small_names, unpack_rows(small_sum, small_shapes)))
    for nm in CONVS:
        width = a[nm].shape[2]
        small_grads[nm] = lax.dynamic_slice_in_dim(small_grads[nm], chip * width, width, axis=2)

    grads, deltas, new_m, new_v = dict(small_grads), {}, {}, {}
    for nm in small_names:
        deltas[nm], new_m[nm], new_v[nm] = adam_call(f"adam_{nm}", a[nm], grads[nm], mom[nm], var[nm])
    for i, nm in enumerate(BIG[1:]):
        res = adam_layers(f"adam_{nm}", as_blocks(a[nm]), as_blocks(mom[nm]), as_blocks(var[nm]), reduced[0][i], reduced[1][1 + i])
        grads[nm], deltas[nm], new_m[nm], new_v[nm] = [r.reshape(a[nm].shape) for r in res]
    stored = lambda t: jnp.transpose(t, (2, 0, 1))
    res = adam_w_in("adam_w_in", stored(a["w_in"]), stored(mom["w_in"]), stored(var["w_in"]), rs0[1].finish(deltas["w_ple"])[0], reduced[1][0])
    grads["w_in"], deltas["w_in"], new_m["w_in"], new_v["w_in"] = [jnp.transpose(r, (1, 2, 0)) for r in res]
    return (loss, grad_x, *[grads[nm] for nm in WEIGHTS], *[deltas[nm] for nm in WEIGHTS], *[new_m[nm] for nm in WEIGHTS],
            *[new_v[nm] for nm in WEIGHTS])
```

```python
import functools

import jax
import jax.numpy as jnp
from jax import lax
from jax.experimental import pallas as pl
from jax.experimental.pallas import tpu as pltpu

F32 = jnp.float32
BF16 = jnp.bfloat16
HI = lax.Precision.HIGHEST
SOLVE = lax.Precision.HIGH
MESH = pl.DeviceIdType.MESH

D_MODEL = 1024
BRANCH = 512
FOX_DH = 64
DN_DH = 128
DN_HEADS = 4
DN_CHUNK = 64
FOX_BLOCK = 128
D_FF = 2816
EPS = 1e-6
N_CHIPS = 4
LANES = 128

ADAM_LR, ADAM_B1, ADAM_B2, ADAM_EPS, ADAM_WD, ADAM_STEP = 0.001, 0.9, 0.999, 1e-08, 0.01, 10

VMEM_LIMIT = 56 * 1024 * 1024

C_FQ, C_FK, C_FV, C_SB, C_SC, C_SV, C_DN, C_DZ, C_GATE = 0, 512, 1024, 1536, 2048, 2560, 3072, 4608, 5120
IN_MAIN = 8192

BIG = ("w_in", "w_branch", "w_o", "w_up", "w_down", "w_ple_gate", "w_ple")
CONVS = ("sc_conv_w", "dn_conv_w", "ffn_conv_w")
SMALL = ("g_mix", "b_fox_f", "fox_q_gain", "fox_k_gain", "dn_a_log", "dn_dt_bias", "dn_norm_gain", "g_ffn", "g_ple")
WEIGHTS = ("g_mix", "w_in", "b_fox_f", "fox_q_gain", "fox_k_gain", "sc_conv_w", "dn_conv_w", "dn_a_log", "dn_dt_bias",
           "dn_norm_gain", "w_branch", "w_o", "g_ffn", "w_up", "ffn_conv_w", "w_down", "g_ple", "w_ple_gate", "w_ple")


def _iota(shape, dim):
    return lax.broadcasted_iota(jnp.int32, shape, dim)


def _dg(a, b, mode, prec=None):
    dims = {"nn": ((1,), (0,)), "nt": ((1,), (1,)), "tn": ((0,), (0,))}[mode]
    return lax.dot_general(a, b, (dims, ((), ())), precision=prec, preferred_element_type=F32)


def _bdot_impl(a, b, mode):
    return _dg(a.astype(BF16), b.astype(BF16), mode)


@functools.partial(jax.custom_vjp, nondiff_argnums=(2,))
def _bdot_diff(a, b, mode):
    return _bdot_impl(a, b, mode)


def _bdot_fwd(a, b, mode):
    return _bdot_impl(a, b, mode), (a, b)


def _bdot_bwd(mode, res, g):
    a, b = res
    if mode == "nn":
        da, db = _bdot_impl(g, b, "nt"), _bdot_impl(a, g, "tn")
    elif mode == "nt":
        da, db = _bdot_impl(g, b, "nn"), _bdot_impl(g, a, "tn")
    else:
        da, db = _bdot_impl(b, g, "nt"), _bdot_impl(a, g, "nn")
    return da.astype(a.dtype), db.astype(b.dtype)


_bdot_diff.defvjp(_bdot_fwd, _bdot_bwd)


def _bdot(d):
    return _bdot_diff if d else _bdot_impl


def _shift_impl(x, k):
    return jnp.where(_iota(x.shape, 0) >= k, pltpu.roll(x, k, 0), 0.0)


def _unshift_impl(g, k):
    n = g.shape[0]
    return jnp.where(_iota(g.shape, 0) < n - k, pltpu.roll(g, n - k, 0), 0.0)


@functools.partial(jax.custom_vjp, nondiff_argnums=(1,))
def _shift_diff(x, k):
    return _shift_impl(x, k)


_shift_diff.defvjp(lambda x, k: (_shift_impl(x, k), None), lambda k, _, g: (_unshift_impl(g, k),))


def _row(w, j):
    return jnp.sum(jnp.where(_iota(w.shape, 0) == j, w, 0.0), axis=0, keepdims=True)


def _col(w, j):
    return jnp.sum(jnp.where(_iota(w.shape, 1) == j, w, 0.0), axis=1, keepdims=True)


def _conv(d, x, w):
    shift = _shift_diff if d else _shift_impl
    taps = w.shape[0]
    y = x * _row(w, taps - 1)
    for j in range(taps - 1):
        y = y + shift(x, taps - 1 - j) * _row(w, j)
    return y


def _softplus(x):
    return jnp.maximum(x, 0.0) + jnp.log(1.0 + jnp.exp(-jnp.abs(x)))


def _sigmoid(x):
    return 0.5 * (jnp.tanh(0.5 * x) + 1.0)


def _silu(x):
    return x * _sigmoid(x)


def _rms(x, gain):
    return x * lax.rsqrt(jnp.mean(x * x, axis=-1, keepdims=True) + EPS) * gain


def _rms_fn(d, pids, x, gain):
    return (_rms(x, gain),)


def _loss_fn(d, pids, y, t):
    e = y - t
    part = 0.5 / D_MODEL * jnp.sum(e * e, keepdims=True)
    return e * (1.0 / D_MODEL), jnp.broadcast_to(part, (8, LANES))


def _fox_prep_fn(d, pids, q, k, gq, gk):
    first = _iota(q.shape, 1) < FOX_DH

    def norm(x, gain):
        sq = x * x
        ss_a = jnp.sum(jnp.where(first, sq, 0.0), axis=1, keepdims=True)
        ss_b = jnp.sum(jnp.where(first, 0.0, sq), axis=1, keepdims=True)
        rs = jnp.where(first, lax.rsqrt(ss_a / FOX_DH + EPS), lax.rsqrt(ss_b / FOX_DH + EPS))
        return x * rs * gain

    return norm(q, gq) * FOX_DH ** -0.5, norm(k, gk)


def _fox_gate_fn(d, pids, f, bias):
    logf = -_softplus(-(f + bias))
    n_r, n_c = logf.shape
    tri = (_iota((n_c, n_c), 0) <= _iota((n_c, n_c), 1)).astype(F32)
    within = _dg(logf, tri, "nn", HI)
    tot = jnp.broadcast_to(jnp.sum(logf, axis=1, keepdims=True), logf.shape)
    below = (_iota((n_r, n_r), 1) < _iota((n_r, n_r), 0)).astype(F32)
    return (within + _dg(below, tot, "nn", HI),)


def _fox_attn_fn(q_block0, d, pids, q, k, v, cq_a, cq_b, ck_a, ck_b):
    dot = _bdot(d)
    first = _iota(q.shape, 1) < FOX_DH
    n_q, n_k = q.shape[0], k.shape[0]
    causal = ((q_block0 + pids[1]) * n_q + _iota((n_q, n_k), 0)) >= _iota((n_q, n_k), 1)

    qs = [jnp.where(first, q, 0.0), jnp.where(first, 0.0, q)]
    s = _each(lambda qh, cq, ck: jnp.where(causal, dot(qh, k, "nt") + cq - ck, -1e30), qs, [cq_a, cq_b], [ck_a, ck_b])
    e = [jnp.exp(si - lax.stop_gradient(jnp.max(si, axis=1, keepdims=True))) for si in s]
    o_a, o_b = [dot(ei * (1.0 / jnp.sum(ei, axis=1, keepdims=True)), v, "nn") for ei in e]
    return (jnp.where(first, o_a, o_b),)


def _sconv_fn(d, pids, sb, sc, sv, w):
    return (sb * _conv(d, sc * sv, w),)


def _dnconv_fn(d, pids, x, w):
    return (_silu(_conv(d, x, w)),)


def _merge_fn(d, pids, y0, y1, y2, g0, g1, g2):
    return (_sigmoid(g0) * y0 + _sigmoid(g1) * y1 + _sigmoid(g2) * y2,)


def _ffn_act_fn(d, pids, ug, uv, wg, wv):
    return (_silu(_conv(d, ug, wg)) * _conv(d, uv, wv),)


def _ple_fn(d, pids, gpre, pe, x):
    return (x + _sigmoid(gpre) * pe,)


def _adam_fn(d, pids, w, g, m, v):
    m2 = ADAM_B1 * m + (1.0 - ADAM_B1) * g
    v2 = ADAM_B2 * v + (1.0 - ADAM_B2) * (g * g)
    m_hat = m2 / (1.0 - ADAM_B1 ** ADAM_STEP)
    v_hat = v2 / (1.0 - ADAM_B2 ** ADAM_STEP)
    delta = -ADAM_LR * (m_hat / (jnp.sqrt(v_hat) + ADAM_EPS) + ADAM_WD * w)
    return delta, m2, v2


def _each(fn, *lists):
    return [fn(*args) for args in zip(*lists)]


def _tri_inv_impl(mats):
    n = mats[0].shape[0]
    r, c = _iota((n, n), 0), _iota((n, n), 1)
    diag_blk = (r >> 4) == (c >> 4)
    eye = (r == c).astype(F32)
    mm = lambda us, ws: _each(lambda u, w: _dg(u, w, "nn", SOLVE), us, ws)
    grow = lambda ps, xs: _each(lambda p, px: p + px, ps, mm(ps, xs))
    x = [jnp.where(diag_blk, -a, 0.0) for a in mats]
    p = [eye + xi for xi in x]
    x2 = mm(x, x)
    p = grow(p, x2)
    x4 = mm(x2, x2)
    p = grow(p, x4)
    p = grow(p, mm(x4, x4))
    y = [-yi for yi in mm(p, [jnp.where(diag_blk, 0.0, a) for a in mats])]
    q = grow([eye + yi for yi in y], mm(y, y))
    return mm(q, p)


@jax.custom_vjp
def _tri_inv_diff(mats):
    return _tri_inv_impl(mats)


def _tri_inv_fwd(mats):
    ts = _tri_inv_impl(mats)
    return ts, ts


def _tri_inv_bwd(ts, gs):
    left = _each(lambda t, g: _dg(t, g, "tn", SOLVE), ts, gs)
    return ([-m for m in _each(lambda l, t: _dg(l, t, "nt", SOLVE), left, ts)],)


_tri_inv_diff.defvjp(_tri_inv_fwd, _tri_inv_bwd)


def _dn_local(d, qs, ks, vs, a_cs, a_rs, b_cs, a_logs, dt_bs):
    dot = _bdot(d)
    inv = _tri_inv_diff if d else _tri_inv_impl
    n = qs[0].shape[0]
    r, c = _iota((n, n), 0), _iota((n, n), 1)
    incl, strict, upper = r >= c, r > c, r <= c
    qs = [q * lax.rsqrt(jnp.sum(q * q, axis=1, keepdims=True) + EPS) * DN_DH ** -0.5 for q in qs]
    ks = [k * lax.rsqrt(jnp.sum(k * k, axis=1, keepdims=True) + EPS) for k in ks]
    betas = [_sigmoid(b) for b in b_cs]
    rates = [-jnp.exp(a) for a in a_logs]
    g_cs = _each(lambda rate, a, dt: rate * _softplus(a + dt), rates, a_cs, dt_bs)
    g_rs = _each(lambda rate, a, dt: rate * _softplus(a + dt), rates, a_rs, dt_bs)
    gcum_cs = [jnp.sum(jnp.where(incl, g, 0.0), axis=1, keepdims=True) for g in g_rs]
    gcum_rs = [jnp.sum(jnp.where(upper, g, 0.0), axis=0, keepdims=True) for g in g_cs]
    decays = _each(lambda gc, gr: jnp.exp(jnp.where(incl, gc - gr, -1e30)), gcum_cs, gcum_rs)
    kbs = _each(lambda k, b: k * b, ks, betas)
    kk = _each(lambda kb, k: dot(kb, k, "nt"), kbs, ks)
    ts = inv(_each(lambda m, dec: jnp.where(strict, m * dec, 0.0), kk, decays))
    e_gs = [jnp.exp(g) for g in gcum_cs]
    us = _each(lambda t, v, b: _dg(t, v * b, "nn", SOLVE), ts, vs, betas)
    k_cums = _each(lambda t, kb, e: _dg(t, kb * e, "nn", SOLVE), ts, kbs, e_gs)
    qk = _each(lambda q, k: dot(q, k, "nt"), qs, ks)
    qk = _each(lambda m, dec: jnp.where(incl, m * dec, 0.0), qk, decays)
    g_lasts = [jnp.sum(g, axis=0, keepdims=True) for g in g_cs]
    q_decs = _each(lambda q, e: q * e, qs, e_gs)
    k_decs = _each(lambda k, gl, gc: k * jnp.exp(gl - gc), ks, g_lasts, gcum_cs)
    return list(zip(us, k_cums, q_decs, k_decs, qk, g_lasts))


def _dn_step(d, s_prevs, items, zs, gain):
    dot = _bdot(d)
    us, k_cums, q_decs, k_decs, qks, g_lasts = [list(t) for t in zip(*items)]
    v_news = _each(lambda u, kc, s: u - dot(kc, s, "nn"), us, k_cums, s_prevs)
    inter = _each(lambda qd, s: dot(qd, s, "nn"), q_decs, s_prevs)
    outs = _each(lambda o, qk, vn: o + dot(qk, vn, "nn"), inter, qks, v_news)
    s_nexts = _each(lambda s, gl, kd, vn: s * jnp.exp(gl) + dot(kd, vn, "tn"), s_prevs, g_lasts, k_decs, v_news)
    return _each(lambda o, z: _rms(o, gain) * _silu(z), outs, zs), s_nexts


def _split_heads(t):
    return [t[:, h * DN_DH:(h + 1) * DN_DH] for h in range(t.shape[1] // DN_DH)]


def _dn_gates(ps, a_rows, ad):
    hs = range(DN_HEADS)
    return ([_col(ps, 12 + h) for h in hs], [_row(a_rows, h) for h in hs], [_col(ps, 8 + h) for h in hs],
            [_col(_row(ad, 0), h) for h in hs], [_col(_row(ad, 1), h) for h in hs])


def _head_rows(vals):
    row = _iota((8, LANES), 0)
    tile = jnp.zeros((8, LANES), F32)
    for h, val in enumerate(vals):
        tile = tile + jnp.where(row == h, val, 0.0)
    return tile


def _cparams(n_axes):
    return pltpu.CompilerParams(dimension_semantics=("arbitrary",) * n_axes, vmem_limit_bytes=VMEM_LIMIT)


def _first_visit(acc_axes):
    cond = None
    for a in acc_axes:
        here = pl.program_id(a) == 0
        cond = here if cond is None else jnp.logical_and(cond, here)
    return cond


def _tile(ref, widen=False):
    val = ref[...]
    shape = val.shape
    while len(shape) > 2 and shape[0] == 1:
        shape = shape[1:]
    val = val.reshape(shape)
    return val.astype(F32) if widen and val.dtype == BF16 else val


def _store(ref, val, first):
    val = val.astype(ref.dtype).reshape(ref.shape)
    if first is None:
        ref[...] = val
        return

    @pl.when(first)
    def _():
        ref[...] = val

    @pl.when(jnp.logical_not(first))
    def _():
        ref[...] += val


def _specs(ops):
    return [pl.BlockSpec(block, imap) for _, block, imap in ops]


def tile_fwd(name, fn, grid, ins, outs, raw=()):
    n_in = len(ins)

    def body(*refs):
        pids = tuple(pl.program_id(a) for a in range(len(grid)))
        firsts = [_first_visit(o[4]) if o[4] else None for o in outs]
        res = fn(False, pids, *[_tile(r, i not in raw) for i, r in enumerate(refs[:n_in])])
        for ref, val, first in zip(refs[n_in:], res, firsts):
            _store(ref, val, first)

    out = pl.pallas_call(
        body, grid=grid, in_specs=_specs(ins),
        out_specs=[pl.BlockSpec(o[2], o[3]) for o in outs],
        out_shape=[jax.ShapeDtypeStruct(o[0], o[1]) for o in outs],
        name=name, compiler_params=_cparams(len(grid)),
    )(*[a for a, _, _ in ins])
    return out


def tile_bwd(name, fn, grid, ins, cots, diff, adds=None, raw=()):
    adds = adds or {}
    n_in, n_cot = len(ins), len(cots)
    add_pos = sorted(adds)
    diff_idx = [d[0] for d in diff]
    out_desc = [d[2] if len(d) > 2 and d[2] is not None else (ins[d[0]][0].shape, ins[d[0]][1], ins[d[0]][2]) for d in diff]
    out_dtypes = [d[3] if len(d) > 3 else F32 for d in diff]

    def body(*refs):
        pids = tuple(pl.program_id(a) for a in range(len(grid)))
        firsts = [_first_visit(d[1]) if d[1] else None for d in diff]
        vals = [_tile(r, i not in raw) for i, r in enumerate(refs[:n_in])]
        cot_vals = [_tile(r, True) for r in refs[n_in:n_in + n_cot]]
        add_vals = [_tile(r) for r in refs[n_in + n_cot:n_in + n_cot + len(add_pos)]]
        out_refs = refs[n_in + n_cot + len(add_pos):]

        def f(*dv):
            full = list(vals)
            for i, val in zip(diff_idx, dv):
                full[i] = val
            return fn(True, pids, *full)

        prim, vjp = jax.vjp(f, *[vals[i].astype(F32) for i in diff_idx])
        grads = list(vjp(tuple(c.astype(o.dtype) for c, o in zip(cot_vals, prim))))
        for pos, val in zip(add_pos, add_vals):
            extra = val.astype(F32) if firsts[pos] is None else jnp.where(firsts[pos], val.astype(F32), 0.0)
            grads[pos] = grads[pos] + extra
        for ref, val, first in zip(out_refs, grads, firsts):
            _store(ref, val, first)

    all_ins = list(ins) + list(cots) + [adds[p] for p in add_pos]
    out = pl.pallas_call(
        body, grid=grid, in_specs=_specs(all_ins),
        out_specs=[pl.BlockSpec(o[1], o[2]) for o in out_desc],
        out_shape=[jax.ShapeDtypeStruct(o[0], dt) for o, dt in zip(out_desc, out_dtypes)],
        name=name, compiler_params=_cparams(len(grid)),
    )(*[a for a, _, _ in all_ins])
    return out


def _pick(dim, cands):
    for c in cands:
        if dim % c == 0:
            return c
    return dim


MM_VMEM_BUDGET = 40 * 1024 * 1024
MM_TILES = (1024, 512, 1408, 256, 128)


def mm(name, a, b, mode, add=None, out_dtype=F32, blocks=None, into=None):
    wide = None
    if mode == "nn":
        (m, kk), n = a.shape, b.shape[-1]
    elif mode == "nt":
        (m, kk), n = a.shape, b.shape[-2]
    else:
        (kk, m), n = a.shape, b.shape[1]
    if blocks is not None:
        lo, n_blk = blocks
        wide = b.shape[-1] if mode != "tn" else n // n_blk
        if mode == "nn":
            n = wide * n_blk
    tm = _pick(m, MM_TILES)
    if mode == "nt" and blocks is not None:
        tn, tk = _pick(n, MM_TILES), _pick(wide, MM_TILES[:-1])
    elif blocks is not None:
        tn, tk = _pick(wide, MM_TILES[:-1]), _pick(kk, MM_TILES)
    else:
        tn, tk = _pick(n, MM_TILES), _pick(kk, MM_TILES)
    if mode == "tn" or blocks is None:
        tk = _pick(kk, (2048,) + MM_TILES)
    if mode != "tn" and add is None and m % 2048 == 0 and (n // tn) * (kk // tk) > 1:
        windows = 2 * (2048 * tk * a.dtype.itemsize + tk * tn * b.dtype.itemsize + 2048 * tn * jnp.dtype(out_dtype).itemsize)
        if windows + 2048 * tn * 4 <= MM_VMEM_BUDGET:
            tm = 2048
    nk = kk // tk
    a_spec = pl.BlockSpec((tk, tm), lambda i, j, k: (k, i)) if mode == "tn" else pl.BlockSpec((tm, tk), lambda i, j, k: (i, k))
    o_spec = pl.BlockSpec((tm, tn), lambda i, j, k: (i, j))
    out_shape = (m, n)
    if blocks is None:
        b_spec = pl.BlockSpec((tn, tk), lambda i, j, k: (j, k)) if mode == "nt" else pl.BlockSpec((tk, tn), lambda i, j, k: (k, j))
    elif mode == "nn":
        per = wide // tn
        b_spec = pl.BlockSpec((1, tk, tn), lambda i, j, k: (lo + j // per, k, j % per))
    elif mode == "nt":
        per = wide // tk
        b_spec = pl.BlockSpec((1, tn, tk), lambda i, j, k: (lo + k // per, j, k % per))
    else:
        per = wide // tn
        total, first = (into[0], into[1]) if into is not None else (n_blk, 0)
        b_spec = pl.BlockSpec((tk, tn), lambda i, j, k: (k, j))
        o_spec = pl.BlockSpec((1, tm, tn), lambda i, j, k: (first + j // per, i, j % per))
        out_shape = (total, m, wide)

    def body(*refs):
        a_ref, b_ref = refs[0], refs[1]
        add_ref = refs[2] if add is not None else None
        o_ref, acc = refs[-2], refs[-1]
        k = pl.program_id(2)
        part = _bdot_impl(_tile(a_ref), _tile(b_ref), mode)

        @pl.when(k == 0)
        def _():
            acc[...] = part

        @pl.when(k > 0)
        def _():
            acc[...] += part

        @pl.when(k == nk - 1)
        def _():
            res = acc[...]
            if add_ref is not None:
                res = res + add_ref[...]
            o_ref[...] = res.astype(o_ref.dtype).reshape(o_ref.shape)

    operands = [a, b] + ([add] if add is not None else [])
    in_specs = [a_spec, b_spec] + ([o_spec] if add is not None else [])
    aliases = {}
    if into is not None and len(into) > 2:
        operands, in_specs, aliases = operands + [into[2]], in_specs + [pl.BlockSpec(memory_space=pl.ANY)], {len(operands): 0}
    return pl.pallas_call(
        body, grid=(m // tm, n // tn, nk), in_specs=in_specs, out_specs=o_spec,
        out_shape=jax.ShapeDtypeStruct(out_shape, out_dtype),
        scratch_shapes=[pltpu.VMEM((tm, tn), F32)], input_output_aliases=aliases,
        name=name, compiler_params=_cparams(3),
    )(*operands)


def _rows(x, width=None, off=0, tm=256):
    width = x.shape[1] if width is None else width
    return (x, (tm, width), lambda i, off=off: (i, off))


def _whole(x):
    nd = x.ndim
    return (x, x.shape, lambda *pids, nd=nd: (0,) * nd)


RMS_ROWS = 512


def _rms_ops(x, gain):
    return [_rows(x, tm=RMS_ROWS), _whole(gain)]


def rms_fwd(name, x, gain):
    s, dm = x.shape
    return tile_fwd(name, _rms_fn, (s // RMS_ROWS,), _rms_ops(x, gain), [((s, dm), BF16, (RMS_ROWS, dm), lambda i: (i, 0), ())])[0]


def rms_bwd(name, x, gain, dh, dres):
    s = x.shape[0]
    return tile_bwd(name, _rms_fn, (s // RMS_ROWS,), _rms_ops(x, gain), [_rows(dh, tm=RMS_ROWS)], [(0, ()), (1, (0,))],
                    adds={0: _rows(dres, tm=RMS_ROWS)})


def loss_call(y, t):
    s, dm = y.shape
    dy, part = tile_fwd("loss", _loss_fn, (s // RMS_ROWS,), [_rows(y, tm=RMS_ROWS), _rows(t, tm=RMS_ROWS)],
                        [((s, dm), F32, (RMS_ROWS, dm), lambda i: (i, 0), ()), ((8, LANES), F32, (8, LANES), lambda i: (0, 0), (0,))])
    return dy, part[0, 0]


def _fox_prep_ops(pm, gq, gk):
    tm = 512
    return [(pm, (tm, LANES), lambda i, j: (i, C_FQ // LANES + j)), (pm, (tm, LANES), lambda i, j: (i, C_FK // LANES + j)),
            _whole(gq), _whole(gk)]


def fox_prep_fwd(name, pm, gq, gk):
    s = pm.shape[0]
    out = ((s, BRANCH), BF16, (512, LANES), lambda i, j: (i, j), ())
    return tile_fwd(name, _fox_prep_fn, (s // 512, 4), _fox_prep_ops(pm, gq, gk), [out, out])


def fox_prep_bwd(name, pm, gq, gk, dqn, dkn):
    s = pm.shape[0]
    cot = lambda g: (g, (512, LANES), lambda i, j: (i, j))
    own = ((s, BRANCH), (512, LANES), lambda i, j: (i, j))
    return tile_bwd(name, _fox_prep_fn, (s // 512, 4), _fox_prep_ops(pm, gq, gk), [cot(dqn), cot(dkn)],
                    [(0, (), own, BF16), (1, (), own, BF16), (2, (0, 1)), (3, (0, 1))])


def _fox_gate_ops(f_t, bias):
    return [(f_t, (1,) + f_t.shape[1:], lambda h: (h, 0, 0)), (bias, (1, 1, 1), lambda h: (h, 0, 0))]


def fox_gate_fwd(name, f_t, bias):
    n_h = f_t.shape[0]
    return tile_fwd(name, _fox_gate_fn, (n_h,), _fox_gate_ops(f_t, bias),
                    [(f_t.shape, F32, (1,) + f_t.shape[1:], lambda h: (h, 0, 0), ())])[0]


def fox_gate_bwd(name, f_t, bias, dcum):
    n_h = f_t.shape[0]
    return tile_bwd(name, _fox_gate_fn, (n_h,), _fox_gate_ops(f_t, bias),
                    [(dcum, (1,) + f_t.shape[1:], lambda h: (h, 0, 0))], [(0, ()), (1, ())])


FOX_GROUPS = 8


def _fox_groups(s):
    per = s // FOX_BLOCK // FOX_GROUPS
    return [(g * per, per, (g + 1) * per * FOX_BLOCK) for g in range(FOX_GROUPS)]


def _fox_attn_ops(qn, kn, pm, cum_c, cum_r, q0, keys):
    nb = FOX_BLOCK
    return [(qn, (nb, LANES), lambda p, i: (q0 + i, p)), (kn, (keys, LANES), lambda p, i: (0, p)),
            (pm, (keys, LANES), lambda p, i: (0, C_FV // LANES + p)),
            (cum_c, (1, nb, 1), lambda p, i: (2 * p, q0 + i, 0)), (cum_c, (1, nb, 1), lambda p, i: (2 * p + 1, q0 + i, 0)),
            (cum_r, (1, 1, keys), lambda p, i: (2 * p, 0, 0)), (cum_r, (1, 1, keys), lambda p, i: (2 * p + 1, 0, 0))]


def fox_attn_fwd(name, qn, kn, pm, cum_c, cum_r):
    s = qn.shape[0]
    parts = []
    for g, (q0, n_q, keys) in enumerate(_fox_groups(s)):
        parts.append(tile_fwd(f"{name}_g{g}", functools.partial(_fox_attn_fn, q0), (4, n_q), _fox_attn_ops(qn, kn, pm, cum_c, cum_r, q0, keys),
                              [((n_q * FOX_BLOCK, BRANCH), BF16, (FOX_BLOCK, LANES), lambda p, i: (i, p), ())], raw=(0, 1, 2))[0])
    return jnp.concatenate(parts, axis=0)


def fox_attn_bwd(name, qn, kn, pm, cum_c, cum_r, dy):
    s = qn.shape[0]
    groups = _fox_groups(s)
    d_qn, by_q, tails = [None] * len(groups), [None] * len(groups), [None] * len(groups)
    below = None
    for g in reversed(range(len(groups))):
        q0, n_q, keys = groups[g]
        rows = n_q * FOX_BLOCK
        own_q = ((rows, BRANCH), (FOX_BLOCK, LANES), lambda p, i: (i, p))
        own_k = ((keys, BRANCH), (keys, LANES), lambda p, i: (0, p))
        pair_c = ((4, rows, 1), (1, FOX_BLOCK, 1), lambda p, i: (p, i, 0))
        pair_r = ((4, 1, keys), (1, 1, keys), lambda p, i: (p, 0, 0))
        adds = {}
        if below is not None:
            adds = {1: (below[0],) + own_k[1:], 2: (below[1],) + own_k[1:], 5: (below[2],) + pair_r[1:], 6: (below[3],) + pair_r[1:]}
        g_qn, g_kn, g_v, g_cqa, g_cqb, g_cka, g_ckb = tile_bwd(
            f"{name}_g{g}", functools.partial(_fox_attn_fn, q0), (4, n_q), _fox_attn_ops(qn, kn, pm, cum_c, cum_r, q0, keys),
            [(dy, (FOX_BLOCK, LANES), lambda p, i, q0=q0: (q0 + i, p))],
            [(0, (), own_q), (1, (1,), own_k), (2, (1,), own_k), (3, (), pair_c), (4, (), pair_c), (5, (1,), pair_r), (6, (1,), pair_r)],
            adds=adds)
        below = (g_kn, g_v, g_cka, g_ckb)
        lo = groups[g - 1][2] if g else 0
        d_qn[g] = g_qn
        by_q[g] = jnp.stack([g_cqa[:, :, 0], g_cqb[:, :, 0]], axis=1).reshape(8, rows)
        tails[g] = (g_kn[lo:], g_v[lo:], jnp.stack([g_cka[:, 0, lo:], g_ckb[:, 0, lo:]], axis=1).reshape(8, keys - lo))
    d_cum = jnp.concatenate(by_q, axis=1) + jnp.concatenate([t[2] for t in tails], axis=1)
    return jnp.concatenate(d_qn, axis=0), jnp.concatenate([t[0] for t in tails], axis=0), jnp.concatenate([t[1] for t in tails], axis=0), d_cum


def sconv_ops(pm, w):
    s = pm.shape[0]
    blk = lambda c0: (pm, (s, LANES), lambda j, c0=c0: (0, c0 // LANES + j))
    return [blk(C_SB), blk(C_SC), blk(C_SV), (w, (w.shape[0], LANES), lambda j: (0, j))]


def dnconv_ops(pm, w):
    s = pm.shape[0]
    return [(pm, (s, LANES), lambda j: (0, C_DN // LANES + j)), (w, (w.shape[0], LANES), lambda j: (0, j))]


def ffn_ops(ug, uv, w):
    s = ug.shape[0]
    n_t = D_FF // LANES
    return [(ug, (s, LANES), lambda j: (0, j)), (uv, (s, LANES), lambda j: (0, j)),
            (w, (w.shape[0], LANES), lambda j: (0, j)), (w, (w.shape[0], LANES), lambda j: (0, n_t + j))]


def _col_out(s, width, dtype=F32):
    return ((s, width), dtype, (s, LANES), lambda j: (0, j), ())


def _col_cot(g):
    return (g, (g.shape[0], LANES), lambda j: (0, j))


def merge_ops(yp, pm, tm=256):
    gate = lambda b: (pm, (tm, D_MODEL), lambda i, b=b: (i, C_GATE // D_MODEL + b))
    return [_rows(yp[0], tm=tm), _rows(yp[1], tm=tm), _rows(yp[2], tm=tm), gate(0), gate(1), gate(2)]


def ple_ops(gpre, pe, x):
    return [_rows(gpre, tm=RMS_ROWS), _rows(pe, tm=RMS_ROWS), _rows(x, tm=RMS_ROWS)]


def adam_call(name, w, g, m, v):
    shape = w.shape
    last = shape[-1]
    rows = w.size // last
    flat = lambda t: t.reshape(rows, last)
    tm = rows
    for cand in (512, 256, 128, 64, 32, 16, 8):
        if rows % cand == 0 and cand * last * 4 <= 2 * 1024 * 1024:
            tm = cand
            break
    spec = lambda t: (flat(t), (tm, last), lambda i: (i, 0))
    out = ((rows, last), F32, (tm, last), lambda i: (i, 0), ())
    res = tile_fwd(name, _adam_fn, (rows // tm,), [spec(w), spec(g), spec(m), spec(v)], [out, out, out])
    return [r.reshape(shape) for r in res]


def _adam_layers_fn(d, pids, w, m, v, g0, g1):
    g = jnp.where(pids[0] == 0, g0, g1)
    return (g,) + _adam_fn(d, pids, w, g, m, v)


def adam_layers(name, w, m, v, g0, g1):
    _, rows, cols = w.shape
    tm = _row_tile(rows, cols)
    n_t = rows // tm
    lay = lambda t: (t, (1, tm, cols), lambda l, i: (l, i, 0))
    ins = [lay(w), lay(m), lay(v), (g0, (tm, cols), lambda l, i: (i * (1 - l) + (n_t - 1) * l, 0)), (g1, (tm, cols), lambda l, i: (i * l, 0))]
    out = (w.shape, F32, (1, tm, cols), lambda l, i: (l, i, 0), ())
    return tile_fwd(name, _adam_layers_fn, (2, n_t), ins, [out, out, out, out])


def adam_w_in(name, w, m, v, g0, g1):
    rows, n_l, cols = w.shape

    def body(w_ref, m_ref, v_ref, g0_ref, g1_ref, g_out, d_out, m_out, v_out):
        step = 64

        def update(at):
            g0, g1 = g0_ref[at, :], g1_ref[at, :]
            layer = _iota((g0.shape[0], n_l, LANES), 1)
            g = jnp.where(layer == 0, g0[:, None, :], g1[:, None, :])
            delta, m2, v2 = _adam_fn(False, None, w_ref[at], g, m_ref[at], v_ref[at])
            for ref, val in ((g_out, g), (d_out, delta), (m_out, m2), (v_out, v2)):
                ref[at] = val

        def some_rows(i, carry):
            update(pl.ds(pl.multiple_of(i * step, step), step))
            return carry

        lax.fori_loop(0, rows // step, some_rows, 0)
        if rows % step:
            update(pl.ds(rows - rows % step, rows % step))

    both = pl.BlockSpec((rows, n_l, LANES), lambda j: (0, 0, j))
    one = pl.BlockSpec((rows, LANES), lambda j: (0, j))
    return pl.pallas_call(
        body, grid=(cols // LANES,), in_specs=[both, both, both, one, one], out_specs=[both] * 4,
        out_shape=[jax.ShapeDtypeStruct(w.shape, F32)] * 4, name=name, compiler_params=_cparams(1),
    )(w, m, v, g0, g1)


DN_GROUP = 4


def _dn_local_specs():
    rows = DN_GROUP * DN_CHUNK
    return [pl.BlockSpec((rows, 3 * BRANCH), lambda j: (j, 0)), pl.BlockSpec((rows, LANES), lambda j: (j, 0)),
            pl.BlockSpec((DN_GROUP, DN_HEADS, DN_CHUNK), lambda j: (j, 0, 0)), pl.BlockSpec((2, DN_HEADS), lambda j: (0, 0))]


def _dn_group_inputs(qkv, ps, a_rows, c):
    lo = c * DN_CHUNK
    heads = _split_heads(qkv[lo:lo + DN_CHUNK])
    return heads[0:4], heads[4:8], heads[8:12], ps[lo:lo + DN_CHUNK], a_rows[c]


def dn_local_fwd(name, dn_act, ps, a_rows, ad):
    s = dn_act.shape[0]
    n_c, n_g = s // DN_CHUNK, s // (DN_GROUP * DN_CHUNK)
    rows = DN_GROUP * DN_CHUNK

    def body(qkv_ref, ps_ref, ar_ref, ad_ref, u_ref, kc_ref, qd_ref, kd_ref, qk_ref, gl_ref):
        qkv, ps_v, a_rows_v, ad_v = qkv_ref[...], ps_ref[...], ar_ref[...], ad_ref[...]
        args = [[] for _ in range(8)]
        for c in range(DN_GROUP):
            q4, k4, v4, ps_c, ar_c = _dn_group_inputs(qkv, ps_v, a_rows_v, c)
            for lst, vals in zip(args, (q4, k4, v4) + _dn_gates(ps_c, ar_c, ad_v)):
                lst.extend(vals)
        everything = _dn_local(False, *args)
        for c in range(DN_GROUP):
            res = everything[c * DN_HEADS:(c + 1) * DN_HEADS]
            at = pl.ds(c * DN_CHUNK, DN_CHUNK)
            for ref, i in ((u_ref, 0), (kc_ref, 1), (qd_ref, 2), (kd_ref, 3)):
                ref[at, :] = jnp.concatenate([r[i] for r in res], axis=1)
            for h in range(DN_HEADS):
                qk_ref[c, h] = res[h][4]
            gl_ref[c] = _head_rows([r[5] for r in res])

    wide = pl.BlockSpec((rows, BRANCH), lambda j: (j, 0))
    return pl.pallas_call(
        body, grid=(n_g,), in_specs=_dn_local_specs(),
        out_specs=[wide, wide, wide, wide, pl.BlockSpec((DN_GROUP, DN_HEADS, DN_CHUNK, DN_CHUNK), lambda j: (j, 0, 0, 0)),
                   pl.BlockSpec((DN_GROUP, 8, LANES), lambda j: (j, 0, 0))],
        out_shape=[jax.ShapeDtypeStruct((s, BRANCH), F32)] * 4 + [jax.ShapeDtypeStruct((n_c, DN_HEADS, DN_CHUNK, DN_CHUNK), F32),
                                                                 jax.ShapeDtypeStruct((n_c, 8, LANES), F32)],
        name=name, compiler_params=_cparams(1),
    )(dn_act, ps, a_rows, ad)


def dn_local_bwd(name, dn_act, ps, a_rows, ad, cots):
    s = dn_act.shape[0]
    n_c, n_g = s // DN_CHUNK, s // (DN_GROUP * DN_CHUNK)
    rows = DN_GROUP * DN_CHUNK

    def body(qkv_ref, ps_ref, ar_ref, ad_ref, du_ref, dkc_ref, dqd_ref, dkd_ref, dqk_ref, dgl_ref, dqkv_ref, dps_ref, dar_ref, dad_ref):
        first = pl.program_id(0) == 0
        qkv, ps_v, a_rows_v, ad_v = qkv_ref[...], ps_ref[...], ar_ref[...], ad_ref[...]
        d_wide = [r[...] for r in (du_ref, dkc_ref, dqd_ref, dkd_ref)]
        qs, ks, vs, ps_cs, ar_cs, cot = [], [], [], [], [], []
        for c in range(DN_GROUP):
            q4, k4, v4, ps_c, ar_c = _dn_group_inputs(qkv, ps_v, a_rows_v, c)
            qs, ks, vs, ps_cs, ar_cs = qs + q4, ks + k4, vs + v4, ps_cs + [ps_c], ar_cs + [ar_c]
            lo = c * DN_CHUNK
            d_tiles = [_split_heads(t[lo:lo + DN_CHUNK]) for t in d_wide]
            d_gl = dgl_ref[c]
            cot += [(d_tiles[0][h], d_tiles[1][h], d_tiles[2][h], d_tiles[3][h], dqk_ref[c, h], _col(_row(d_gl, h), 0))
                    for h in range(DN_HEADS)]

        def f(qs, ks, vs, ps_cs, ar_cs, ad_v):
            gates = [[] for _ in range(5)]
            for ps_c, ar_c in zip(ps_cs, ar_cs):
                for lst, vals in zip(gates, _dn_gates(ps_c, ar_c, ad_v)):
                    lst.extend(vals)
            return _dn_local(True, qs, ks, vs, *gates)

        _, vjp = jax.vjp(f, qs, ks, vs, ps_cs, ar_cs, ad_v)
        d_q, d_k, d_v, d_ps, d_ar, d_ad = vjp(cot)
        for c in range(DN_GROUP):
            at, hs = pl.ds(c * DN_CHUNK, DN_CHUNK), slice(c * DN_HEADS, (c + 1) * DN_HEADS)
            dqkv_ref[at, :] = jnp.concatenate(d_q[hs] + d_k[hs] + d_v[hs], axis=1).astype(dqkv_ref.dtype)
            dps_ref[at, :] = d_ps[c]
            dar_ref[c] = d_ar[c]
        _store(dad_ref, d_ad, first)

    wide = pl.BlockSpec((rows, BRANCH), lambda j: (j, 0))
    specs = _dn_local_specs()
    return pl.pallas_call(
        body, grid=(n_g,),
        in_specs=specs + [wide, wide, wide, wide, pl.BlockSpec((DN_GROUP, DN_HEADS, DN_CHUNK, DN_CHUNK), lambda j: (j, 0, 0, 0)),
                          pl.BlockSpec((DN_GROUP, 8, LANES), lambda j: (j, 0, 0))],
        out_specs=specs,
        out_shape=[jax.ShapeDtypeStruct((s, 3 * BRANCH), F32), jax.ShapeDtypeStruct((s, LANES), F32),
                   jax.ShapeDtypeStruct((n_c, DN_HEADS, DN_CHUNK), F32), jax.ShapeDtypeStruct((2, DN_HEADS), F32)],
        name=name, compiler_params=_cparams(1),
    )(dn_act, ps, a_rows, ad, *cots)


def _dn_scan_specs(n_c, rev):
    idx = (lambda j: n_c - 1 - j) if rev else (lambda j: j)
    wide = pl.BlockSpec((DN_CHUNK, BRANCH), lambda j: (idx(j), 0))
    return [wide, wide, wide, wide, pl.BlockSpec((1, DN_HEADS, DN_CHUNK, DN_CHUNK), lambda j: (idx(j), 0, 0, 0)),
            pl.BlockSpec((1, 8, LANES), lambda j: (idx(j), 0, 0)), pl.BlockSpec((DN_CHUNK, BRANCH), lambda j: (idx(j), C_DZ // BRANCH)),
            pl.BlockSpec((1, DN_DH), lambda j: (0, 0))]


def _dn_scan_tiles(refs):
    u_ref, kc_ref, qd_ref, kd_ref, qk_ref, gl_ref, z_ref, g_ref = refs
    wide = [_split_heads(r[...]) for r in (u_ref, kc_ref, qd_ref, kd_ref)]
    gl = gl_ref[0]
    return [(wide[0][h], wide[1][h], wide[2][h], wide[3][h], qk_ref[0, h], _col(_row(gl, h), 0)) for h in range(DN_HEADS)], \
        _split_heads(z_ref[...].astype(F32)), g_ref[...]


def dn_scan_fwd(name, local, pm, gain):
    s = pm.shape[0]
    n_c = s // DN_CHUNK

    def body(*refs):
        y_ref, hist_ref, state = refs[8:]

        @pl.when(pl.program_id(0) == 0)
        def _():
            state[...] = jnp.zeros_like(state)

        hist_ref[0] = state[...]
        per_head, z4, gain_v = _dn_scan_tiles(refs[:8])
        ys, s_nexts = _dn_step(False, [state[h] for h in range(DN_HEADS)], per_head, z4, gain_v)
        for h in range(DN_HEADS):
            state[h] = s_nexts[h]
        y_ref[...] = jnp.concatenate(ys, axis=1).astype(y_ref.dtype)

    return pl.pallas_call(
        body, grid=(n_c,), in_specs=_dn_scan_specs(n_c, False),
        out_specs=[pl.BlockSpec((DN_CHUNK, BRANCH), lambda j: (j, 0)),
                   pl.BlockSpec((1, DN_HEADS, DN_DH, DN_DH), lambda j: (j, 0, 0, 0))],
        out_shape=[jax.ShapeDtypeStruct((s, BRANCH), BF16), jax.ShapeDtypeStruct((n_c, DN_HEADS, DN_DH, DN_DH), F32)],
        scratch_shapes=[pltpu.VMEM((DN_HEADS, DN_DH, DN_DH), F32)],
        name=name, compiler_params=_cparams(1),
    )(*local, pm, gain)


def dn_scan_bwd(name, local, pm, gain, hist, dy):
    s = pm.shape[0]
    n_c = s // DN_CHUNK

    def body(*refs):
        hist_ref, dy_ref = refs[8:10]
        du_ref, dkc_ref, dqd_ref, dkd_ref, dqk_ref, dgl_ref, dz_ref, dg_ref, d_state = refs[10:]
        first = pl.program_id(0) == 0

        @pl.when(first)
        def _():
            d_state[...] = jnp.zeros_like(d_state)

        per_head, z4, gain_v = _dn_scan_tiles(refs[:8])
        _, vjp = jax.vjp(functools.partial(_dn_step, True), [hist_ref[0, h] for h in range(DN_HEADS)], per_head, z4, gain_v)
        d_s, grads, d_z, d_gain = vjp((_split_heads(dy_ref[...].astype(F32)), [d_state[h] for h in range(DN_HEADS)]))
        for h in range(DN_HEADS):
            d_state[h] = d_s[h]
        for ref, i in ((du_ref, 0), (dkc_ref, 1), (dqd_ref, 2), (dkd_ref, 3)):
            ref[...] = jnp.concatenate([g[i] for g in grads], axis=1)
        dz_ref[...] = jnp.concatenate(d_z, axis=1).astype(dz_ref.dtype)
        for h in range(DN_HEADS):
            dqk_ref[0, h] = grads[h][4]
        dgl_ref[0] = _head_rows([g[5] for g in grads])
        _store(dg_ref, d_gain, first)

    rev = lambda j: n_c - 1 - j
    specs = _dn_scan_specs(n_c, True)
    return pl.pallas_call(
        body, grid=(n_c,),
        in_specs=specs + [pl.BlockSpec((1, DN_HEADS, DN_DH, DN_DH), lambda j: (rev(j), 0, 0, 0)),
                          pl.BlockSpec((DN_CHUNK, BRANCH), lambda j: (rev(j), 0))],
        out_specs=specs[:6] + [pl.BlockSpec((DN_CHUNK, BRANCH), lambda j: (rev(j), 0)), specs[7]],
        out_shape=[jax.ShapeDtypeStruct((s, BRANCH), F32)] * 4 + [
            jax.ShapeDtypeStruct((n_c, DN_HEADS, DN_CHUNK, DN_CHUNK), F32), jax.ShapeDtypeStruct((n_c, 8, LANES), F32),
            jax.ShapeDtypeStruct((s, BRANCH), BF16), jax.ShapeDtypeStruct((1, DN_DH), F32)],
        scratch_shapes=[pltpu.VMEM((DN_HEADS, DN_DH, DN_DH), F32)],
        name=name, compiler_params=_cparams(1),
    )(*local, pm, gain, hist, dy)


def _seq_layouts(cols, s):
    return cols.T.reshape(cols.shape[1], s // LANES, LANES)


def layer_fwd(li, x, p, w, more_weights=None):
    s = x.shape[0]
    n = lambda t: f"{t}_l{li}"
    h = rms_fwd(n("rms_mix"), x, w["g_mix"])
    pm = mm(n("in_main"), h, w["in_main"], "nn")
    ps = mm(n("in_small"), h, w["in_small"], "nn")
    qn, kn = fox_prep_fwd(n("fox_prep"), pm, w["gq"], w["gk"])
    f_t = _seq_layouts(ps[:, 0:8], s)
    cum = fox_gate_fwd(n("fox_gate"), f_t, w["b_f"])
    cum_c, cum_r = cum.reshape(8, s, 1), cum.reshape(8, 1, s)
    y_fox = fox_attn_fwd(n("fox_attn"), qn, kn, pm, cum_c, cum_r)
    y_sc = tile_fwd(n("sconv"), _sconv_fn, (BRANCH // LANES,), sconv_ops(pm, w["sc_conv_w"]), [_col_out(s, BRANCH, BF16)])[0]
    dn_act = tile_fwd(n("dnconv"), _dnconv_fn, (3 * BRANCH // LANES,), dnconv_ops(pm, w["dn_conv_w"]), [_col_out(s, 3 * BRANCH)])[0]
    a_rows = ps[:, 12:16].reshape(s // DN_CHUNK, DN_CHUNK, DN_HEADS).transpose(0, 2, 1)
    dn_local = dn_local_fwd(n("dn_local"), dn_act, ps, a_rows, w["ad"])
    y_dn, hist = dn_scan_fwd(n("dn_scan"), dn_local, pm, w["dn_gain"])
    ys = (y_fox, y_sc, y_dn)
    if more_weights is not None:
        w = {**w, **more_weights(y_dn)}
    yp = [mm(n(f"branch{b}"), ys[b], w["branch"][b], "nn", blocks=(0, N_CHIPS)) for b in range(3)]
    merged = tile_fwd(n("merge"), _merge_fn, (s // RMS_ROWS,), merge_ops(yp, pm, RMS_ROWS),
                      [((s, D_MODEL), BF16, (RMS_ROWS, D_MODEL), lambda i: (i, 0), ())])[0]
    x1 = mm(n("w_o"), merged, w["o"], "nn", add=x)
    h2 = rms_fwd(n("rms_ffn"), x1, w["g_ffn"])
    ug = mm(n("up_g"), h2, w["up"], "nn", blocks=(0, 2))
    uv = mm(n("up_v"), h2, w["up"], "nn", blocks=(2, 2))
    act = tile_fwd(n("ffn_act"), _ffn_act_fn, (D_FF // LANES,), ffn_ops(ug, uv, w["ffn_conv_w"]), [_col_out(s, D_FF, BF16)])[0]
    x2 = mm(n("down"), act, w["down"], "nn", add=x1)
    h3 = rms_fwd(n("rms_ple"), x2, w["g_ple"])
    gpre = mm(n("ple_gate"), h3, w["pg"], "nn")
    pe = mm(n("ple_emb"), p, w["ple"], "nn", blocks=(0, N_CHIPS))
    x3 = tile_fwd(n("ple"), _ple_fn, (s // RMS_ROWS,), ple_ops(gpre, pe, x2), [((s, D_MODEL), F32, (RMS_ROWS, D_MODEL), lambda i: (i, 0), ())])[0]
    saved = dict(x=x, h=h, pm=pm, ps=ps, qn=qn, kn=kn, f_t=f_t, cum_c=cum_c, cum_r=cum_r, ys=ys, dn_act=dn_act, dn_local=dn_local,
                 a_rows=a_rows, hist=hist, yp=yp, merged=merged, x1=x1, h2=h2, ug=ug, uv=uv, act=act, x2=x2, h3=h3,
                 gpre=gpre, pe=pe, p=p)
    return x3, saved, w


def hang_on(w, token):
    zero = token[0, 0]
    small = ("g_mix", "g_ffn", "g_ple", "gq", "gk", "b_f", "ad", "dn_gain", "sc_conv_w", "dn_conv_w", "ffn_conv_w")
    return {**w, **{k: w[k] + zero for k in small}}


def layer_bwd(li, dx3, sv, w, hooks=None):
    hooks = hooks or {}

    def stage(key, after, w):
        return hang_on(w, hooks[key](after, g)) if key in hooks else w

    s = dx3.shape[0]
    n = lambda t: f"{t}_l{li}"
    g = {}
    col_own = lambda width: ((s, width), (s, LANES), lambda j: (0, j))
    d_gpre, d_pe = tile_bwd(n("ple_bwd"), _ple_fn, (s // RMS_ROWS,), ple_ops(sv["gpre"], sv["pe"], sv["x2"]), [_rows(dx3, tm=RMS_ROWS)],
                            [(0, (), None, BF16), (1, (), None, BF16)])
    g["w_ple"] = mm(n("d_w_ple"), sv["p"], d_pe, "tn", blocks=(0, N_CHIPS))
    g["w_ple_gate"] = mm(n("d_w_pg"), sv["h3"], d_gpre, "tn").reshape(N_CHIPS, -1, D_MODEL)
    dh3 = mm(n("d_h3"), d_gpre, w["pg"], "nt")
    dx2, d_g_ple = rms_bwd(n("rms_ple_bwd"), sv["x2"], w["g_ple"], dh3, dx3)
    dact = mm(n("d_act"), dx2, w["down"], "nt")
    g["w_down"] = mm(n("d_w_down"), sv["act"], dx2, "tn").reshape(N_CHIPS, -1, D_MODEL)
    taps_own = ((w["ffn_conv_w"].shape[0], D_FF), (w["ffn_conv_w"].shape[0], LANES), lambda j: (0, j))
    d_ug, d_uv, d_fw_g, d_fw_v = tile_bwd(n("ffn_act_bwd"), _ffn_act_fn, (D_FF // LANES,), ffn_ops(sv["ug"], sv["uv"], w["ffn_conv_w"]),
                                          [_col_cot(dact)], [(0, (), None, BF16), (1, (), None, BF16), (2, (), taps_own), (3, (), taps_own)])
    g["ffn_conv_w"] = jnp.concatenate([d_fw_g, d_fw_v], axis=1)
    gate_half = mm(n("d_w_up_g"), sv["h2"], d_ug, "tn", blocks=(0, 2), into=(N_CHIPS, 0))
    g["w_up"] = mm(n("d_w_up_v"), sv["h2"], d_uv, "tn", blocks=(0, 2), into=(N_CHIPS, 2, gate_half))
    dh2 = mm(n("d_h2_v"), d_uv, w["up"], "nt", blocks=(2, 2), add=mm(n("d_h2_g"), d_ug, w["up"], "nt", blocks=(0, 2)))
    dx1, d_g_ffn = rms_bwd(n("rms_ffn_bwd"), sv["x1"], w["g_ffn"], dh2, dx2)
    w = stage("mid", dx1, w)
    dmerged = mm(n("d_merged"), dx1, w["o"], "nt")
    g["w_o"] = mm(n("d_w_o"), sv["merged"], dx1, "tn").reshape(N_CHIPS, -1, D_MODEL)
    gate_own = ((s, D_MODEL), (256, D_MODEL), lambda i: (i, 0))
    d_yp0, d_yp1, d_yp2, d_g0, d_g1, d_g2 = tile_bwd(
        n("merge_bwd"), _merge_fn, (s // 256,), merge_ops(sv["yp"], sv["pm"]), [_rows(dmerged)],
        [(0, (), None, BF16), (1, (), None, BF16), (2, (), None, BF16), (3, (), gate_own, BF16), (4, (), gate_own, BF16), (5, (), gate_own, BF16)])
    d_yp = (d_yp0, d_yp1, d_yp2)
    g["w_branch"] = jnp.concatenate([mm(n(f"d_w_branch{b}"), sv["ys"][b], d_yp[b], "tn", blocks=(0, N_CHIPS)) for b in range(3)], axis=1)
    d_ys = [mm(n(f"d_y{b}"), d_yp[b], w["branch"][b], "nt", blocks=(0, N_CHIPS)) for b in range(3)]
    w = stage("late", d_ys[2], w)
    *d_local, d_z, d_dngain = dn_scan_bwd(n("dn_scan_bwd"), sv["dn_local"], sv["pm"], w["dn_gain"], sv["hist"], d_ys[2])
    d_dnact, d_ps_dn, d_arows, d_ad = dn_local_bwd(n("dn_local_bwd"), sv["dn_act"], sv["ps"], sv["a_rows"], w["ad"], d_local)
    g["ad"], g["dn_norm_gain"] = d_ad, d_dngain[0]
    d_dnqkv, g["dn_conv_w"] = tile_bwd(n("dnconv_bwd"), _dnconv_fn, (3 * BRANCH // LANES,), dnconv_ops(sv["pm"], w["dn_conv_w"]),
                                       [_col_cot(d_dnact)], [(0, (), col_own(3 * BRANCH), BF16), (1, ())])
    d_sb, d_sc, d_sv, g["sc_conv_w"] = tile_bwd(n("sconv_bwd"), _sconv_fn, (BRANCH // LANES,), sconv_ops(sv["pm"], w["sc_conv_w"]), [_col_cot(d_ys[1])],
                                                [(0, (), col_own(BRANCH), BF16), (1, (), col_own(BRANCH), BF16), (2, (), col_own(BRANCH), BF16), (3, ())])
    w = stage("last", d_dnqkv, w)
    d_qn, d_kn, d_fv, d_cum = fox_attn_bwd(n("fox_attn_bwd"), sv["qn"], sv["kn"], sv["pm"], sv["cum_c"], sv["cum_r"], d_ys[0])
    d_ft, d_bf = fox_gate_bwd(n("fox_gate_bwd"), sv["f_t"], w["b_f"], d_cum.reshape(8, s // LANES, LANES))
    g["b_fox_f"] = d_bf.reshape(8)
    d_fq, d_fk, d_gq, d_gk = fox_prep_bwd(n("fox_prep_bwd"), sv["pm"], w["gq"], w["gk"], d_qn, d_kn)
    g["fox_q_gain"] = d_gq[0, :FOX_DH] + d_gq[0, FOX_DH:]
    g["fox_k_gain"] = d_gk[0, :FOX_DH] + d_gk[0, FOX_DH:]
    d_pm = jnp.concatenate([d_fq, d_fk, d_fv.astype(BF16), d_sb, d_sc, d_sv, d_dnqkv, d_z, d_g0, d_g1, d_g2], axis=1)
    d_a_cols = d_arows.transpose(0, 2, 1).reshape(s, DN_HEADS)
    d_f_cols = d_ft.reshape(8, s).T
    d_ps = d_ps_dn + jnp.concatenate([d_f_cols, jnp.zeros((s, 4), F32), d_a_cols, jnp.zeros((s, LANES - 16), F32)], axis=1)
    g["w_in"] = chip_blocks_w_in(mm(n("d_w_in_main"), d_pm, sv["h"], "tn"), mm(n("d_w_in_small"), d_ps, sv["h"], "tn"))
    w = stage("w_in", g["w_in"], w)
    dh = mm(n("d_h_small"), d_ps, w["in_small"], "nt", add=mm(n("d_h_main"), d_pm, w["in_main"], "nt"))
    dx, d_g_mix = rms_bwd(n("rms_mix_bwd"), sv["x"], w["g_mix"], dh, dx1)
    g["g_mix"], g["g_ffn"], g["g_ple"] = d_g_mix[0], d_g_ffn[0], d_g_ple[0]
    return dx, g


IN_SHARD = 2052
MAIN_RANGES = ((0, 1536), (1544, 3080), (3080, 4616), (4624, 5136), (5136, 8208))
SMALL_RANGES = ((1536, 1544), (4616, 4620), (4620, 4624))


def _from_chip_blocks(blocks, ranges):
    parts = []
    for lo, hi in ranges:
        for k in range(N_CHIPS):
            a0, a1 = max(lo, k * IN_SHARD), min(hi, (k + 1) * IN_SHARD)
            if a0 < a1:
                parts.append(blocks[k][:, a0 - k * IN_SHARD:a1 - k * IN_SHARD])
    return parts


def split_w_in(blocks):
    main = jnp.concatenate(_from_chip_blocks(blocks, MAIN_RANGES), axis=1)
    pad = jnp.zeros((blocks.shape[1], LANES - 16), blocks.dtype)
    return main, jnp.concatenate(_from_chip_blocks(blocks, SMALL_RANGES) + [pad], axis=1)


def chip_blocks_w_in(main, small):
    ranges = sorted([(lo, hi, "m") for lo, hi in MAIN_RANGES] + [(lo, hi, "s") for lo, hi in SMALL_RANGES])
    offs, m_off, s_off = {}, 0, 0
    for lo, hi in MAIN_RANGES:
        offs[lo] = m_off
        m_off += hi - lo
    for lo, hi in SMALL_RANGES:
        offs[lo] = s_off
        s_off += hi - lo
    blocks = []
    for k in range(N_CHIPS):
        parts = []
        for lo, hi, src in ranges:
            a0, a1 = max(lo, k * IN_SHARD), min(hi, (k + 1) * IN_SHARD)
            if a0 < a1:
                arr = main if src == "m" else small
                parts.append(arr[offs[lo] + a0 - lo:offs[lo] + a1 - lo])
        blocks.append(jnp.concatenate(parts, axis=0))
    return jnp.stack(blocks)


def later_weights(got):
    g_branch, g_o, g_up, g_down, g_pg, g_ple = got
    branch = g_branch.reshape(N_CHIPS, 3, BRANCH, -1)
    return dict(branch=[branch[:, b] for b in range(3)], o=g_o.reshape(D_MODEL, D_MODEL), up=g_up,
                down=g_down.reshape(D_FF, D_MODEL), pg=g_pg.reshape(D_MODEL, D_MODEL), ple=g_ple)


def layer_weights(li, got, conv, a):
    main, small = split_w_in(got[0])
    tile2 = lambda v: jnp.concatenate([v, v])[None, :]
    rest = later_weights(got[1:]) if len(got) > 1 else {}
    return dict(
        in_main=main, in_small=small, **rest,
        g_mix=a["g_mix"][li][None, :], g_ffn=a["g_ffn"][li][None, :], g_ple=a["g_ple"][li][None, :],
        gq=tile2(a["fox_q_gain"][li]), gk=tile2(a["fox_k_gain"][li]), b_f=a["b_fox_f"][li].reshape(8, 1, 1),
        ad=jnp.stack([a["dn_a_log"][li], a["dn_dt_bias"][li]]), dn_gain=a["dn_norm_gain"][li][None, :],
        sc_conv_w=conv["sc_conv_w"][li], dn_conv_w=conv["dn_conv_w"][li], ffn_conv_w=conv["ffn_conv_w"][li])


def pack_rows(arrs, dtype):
    flat = jnp.concatenate([t.reshape(-1).astype(dtype) for t in arrs])
    pad = (-flat.shape[0]) % (8 * LANES)
    if pad:
        flat = jnp.concatenate([flat, jnp.zeros((pad,), dtype)])
    return flat.reshape(-1, LANES)


def unpack_rows(buf, shapes):
    flat = buf.reshape(-1)
    out, off = [], 0
    for shp in shapes:
        size = 1
        for dim in shp:
            size *= dim
        out.append(flat[off:off + size].reshape(shp))
        off += size
    return out


ANY = pl.BlockSpec(memory_space=pl.ANY)


def _position():
    x, y, c = lax.axis_index("x"), lax.axis_index("y"), lax.axis_index("c")
    return x, y, c, [(1 - x, y), (x, 1 - y), (1 - x, 1 - y)]


def gather_small(name, block):
    m_per, n = block.shape

    def body(x_ref, out_ref, token, send_sems, recv_sems, local_sem):
        token[...] = jnp.zeros_like(token)
        x, y, c, chips = _position()
        me, sibling = (x, y, c), (x, y, 1 - c)

        def rows(px, py, pc):
            return out_ref.at[pl.ds((4 * px + 2 * py + pc) * m_per, m_per), :]

        def copy(k, blk, to, src=None):
            return pltpu.make_async_remote_copy(src_ref=rows(*blk) if src is None else src, dst_ref=rows(*blk),
                                                send_sem=send_sems.at[k], recv_sem=recv_sems.at[k], device_id=to, device_id_type=MESH)

        mine = pltpu.make_async_copy(x_ref, rows(*me), local_sem)
        mine.start()
        first = [copy(0, me, sibling, src=x_ref)] + [copy(1 + j, me, (*chip, c), src=x_ref) for j, chip in enumerate(chips)]
        for cp in first:
            cp.start()
        passed = [copy(4 + j, (*chip, c), sibling) for j, chip in enumerate(chips)]
        for j, chip in enumerate(chips):
            copy(1 + j, (*chip, c), me).wait_recv()
            passed[j].start()
        copy(0, sibling, me).wait_recv()
        for j, chip in enumerate(chips):
            copy(4 + j, (*chip, 1 - c), me).wait_recv()
        for cp in first + passed:
            cp.wait_send()
        mine.wait()

    in_vmem = pl.BlockSpec(memory_space=pltpu.VMEM)
    return pl.pallas_call(
        body, out_shape=[jax.ShapeDtypeStruct((8 * m_per, n), block.dtype), jax.ShapeDtypeStruct((8, LANES), F32)],
        in_specs=[in_vmem], out_specs=[in_vmem, in_vmem],
        scratch_shapes=[pltpu.SemaphoreType.DMA((7,)), pltpu.SemaphoreType.DMA((7,)), pltpu.SemaphoreType.DMA],
        name=name, compiler_params=pltpu.CompilerParams(vmem_limit_bytes=VMEM_LIMIT),
    )(block)


def _sems(n):
    return [pltpu.SemaphoreType.DMA((n,)), pltpu.SemaphoreType.DMA((n,))]


def _split_cols(rows):
    return (rows // 2) % 16 != 0


def _half(ref, which, lead=()):
    rows, cols = ref.shape[-2:]
    if _split_cols(rows):
        return ref.at[(*lead, slice(None), pl.ds(which * (cols // 2), cols // 2))]
    return ref.at[(*lead, pl.ds(which * (rows // 2), rows // 2), slice(None))]


def _half_shape(rows, cols):
    return (rows, cols // 2) if _split_cols(rows) else (rows // 2, cols)


def forward_halves(name, lands):
    n_w = len(lands)

    def body(*refs):
        outs = refs[n_w:2 * n_w]
        send_sems, recv_sems = refs[2 * n_w:]
        x, y, c, chips = _position()

        def copy(w, j, pc):
            cx, cy = chips[j]
            part = _half(outs[w], pc, (2 * cx + cy,))
            return pltpu.make_async_remote_copy(src_ref=part, dst_ref=part, send_sem=send_sems.at[3 * w + j], recv_sem=recv_sems.at[3 * w + j],
                                                device_id=(x, y, 1 - c), device_id_type=MESH)

        pairs = [(w, j) for w in range(n_w) for j in range(3)]
        for w, j in pairs:
            copy(w, j, c).start()
        for w, j in pairs:
            copy(w, j, 1 - c).wait_recv()
            copy(w, j, c).wait_send()

    return pl.pallas_call(
        body, out_shape=[jax.ShapeDtypeStruct(t.shape, t.dtype) for t in lands], in_specs=[ANY] * n_w, out_specs=[ANY] * n_w,
        input_output_aliases={w: w for w in range(n_w)}, scratch_shapes=_sems(3 * n_w), name=name,
    )(*lands)


def share_halves(name, bufs):
    n_w = len(bufs)

    def body(*refs):
        outs = refs[n_w:2 * n_w]
        send_sems, recv_sems = refs[2 * n_w:]
        x, y, c, _ = _position()

        def copy(w, pc):
            half = _half(outs[w], pc)
            return pltpu.make_async_remote_copy(src_ref=half, dst_ref=half, send_sem=send_sems.at[w], recv_sem=recv_sems.at[w],
                                                device_id=(x, y, 1 - c), device_id_type=MESH)

        for w in range(n_w):
            copy(w, c).start()
        for w in range(n_w):
            copy(w, 1 - c).wait_recv()
            copy(w, c).wait_send()

    return pl.pallas_call(
        body, out_shape=[jax.ShapeDtypeStruct(b.shape, b.dtype) for b in bufs], in_specs=[ANY] * n_w, out_specs=[ANY] * n_w,
        input_output_aliases={w: w for w in range(n_w)}, scratch_shapes=_sems(n_w), name=name,
    )(*bufs)


HBM = pl.BlockSpec(memory_space=pltpu.HBM)
SEM = pl.BlockSpec(memory_space=pltpu.SEMAPHORE)
EFFECT = pltpu.SideEffectType.DATAFLOW_SIDE_EFFECTING


def _exchange_copies(kind, srcs, lands):
    x, y, c, chips = _position()
    out = []
    for src, land in zip(srcs, lands):
        if kind == "swap":
            out.append((_half(src, 1 - c, (slice(None),)), land, (x, y, 1 - c)))
            continue
        for j, (cx, cy) in enumerate(chips):
            if kind == "gather":
                out.append((src, land.at[2 * x + y], (cx, cy, c)))
            elif kind == "gather_half":
                out.append((_half(src, c), _half(land, c, (2 * x + y,)), (cx, cy, c)))
            else:
                out.append((src.at[2 * cx + cy], land.at[j], (cx, cy, c)))
    return out


def _land_shapes(kind, srcs):
    if kind in ("gather", "gather_half"):
        return [(N_CHIPS,) + s.shape for s in srcs]
    if kind == "swap":
        return [(N_CHIPS,) + _half_shape(*s.shape[1:]) for s in srcs]
    return [(3,) + s.shape[1:] for s in srcs]


def exchange_start(name, kind, srcs):
    n_w = len(srcs)
    shapes = _land_shapes(kind, srcs)
    n_sem = n_w if kind == "swap" else 3 * n_w

    def body(*refs):
        ins, lands = refs[:n_w], refs[n_w:2 * n_w]
        send_sems, recv_sems = refs[2 * n_w:2 * n_w + 2]
        token = refs[-1]
        for i, (src, dst, dev) in enumerate(_exchange_copies(kind, ins, lands)):
            pltpu.make_async_remote_copy(src_ref=src, dst_ref=dst, send_sem=send_sems.at[i], recv_sem=recv_sems.at[i],
                                         device_id=dev, device_id_type=MESH).start()
        token[...] = jnp.zeros_like(token)

    out = pl.pallas_call(
        body, name=name,
        out_shape=(pltpu.SemaphoreType.DMA((n_sem,)), pltpu.SemaphoreType.DMA((n_sem,)),
                   *[pltpu.HBM(s.shape, s.dtype) for s in srcs], *[pltpu.HBM(shp, s.dtype) for shp, s in zip(shapes, srcs)],
                   jax.ShapeDtypeStruct((8, LANES), F32)),
        in_specs=(HBM,) * (2 * n_w), out_specs=(SEM, SEM) + (HBM,) * (2 * n_w) + (pl.BlockSpec(memory_space=pltpu.VMEM),),
        input_output_aliases={i: 2 + i for i in range(2 * n_w)},
        compiler_params=pltpu.CompilerParams(has_side_effects=EFFECT),
    )(*[pltpu.with_memory_space_constraint(s, pltpu.HBM) for s in srcs],
      *[pltpu.with_memory_space_constraint(lax.empty(shp, s.dtype), pltpu.HBM) for shp, s in zip(shapes, srcs)])
    return (kind, n_w, out[:-1]), out[-1]


def exchange_wait(name, handle, after):
    kind, n_w, (send_sems, recv_sems, *thru) = handle

    def body(*refs):
        ins, lands = refs[:n_w], refs[n_w:2 * n_w]
        send_sems, recv_sems = refs[2 * n_w:2 * n_w + 2]
        for i, (src, dst, dev) in enumerate(_exchange_copies(kind, ins, lands)):
            cp = pltpu.make_async_remote_copy(src_ref=src, dst_ref=dst, send_sem=send_sems.at[i], recv_sem=recv_sems.at[i],
                                              device_id=dev, device_id_type=MESH)
            cp.wait_send()
            cp.wait_recv()

    out = pl.pallas_call(
        body, name=name, out_shape=tuple(pltpu.HBM(t.shape, t.dtype) for t in thru),
        in_specs=(HBM,) * (2 * n_w) + (SEM, SEM, pl.BlockSpec(memory_space=pl.ANY)), out_specs=(HBM,) * (2 * n_w),
        input_output_aliases={i: i for i in range(2 * n_w)},
        compiler_params=pltpu.CompilerParams(has_side_effects=EFFECT),
    )(*thru, send_sems, recv_sems, after)
    return list(out[:n_w]), list(out[n_w:])


def _row_tile(rows, cols):
    best = rows
    if rows * cols * 4 <= 2 * 1024 * 1024:
        return rows
    for t in range(16, rows, 16):
        if rows % t == 0 and t * cols * 4 <= 2 * 1024 * 1024:
            best = t
    return best


def pair_sum(name, pos, grad, from_sibling):
    _, rows, cols = grad.shape
    h_rows, h_cols = _half_shape(rows, cols)
    tr = _row_tile(h_rows, h_cols)
    n_t = h_rows // tr

    def body(pos_ref, g_ref, s_ref, b_ref, f_ref):
        tot = g_ref[...] + s_ref[...]
        b_ref[...] = tot.astype(BF16)

        @pl.when(pl.program_id(1) == pos_ref[1])
        def _():
            f_ref[...] = tot[0]

    blk = pl.BlockSpec((1, tr, h_cols), lambda i, k, pos: (k, i, 0))
    if _split_cols(rows):
        mine = pl.BlockSpec((1, tr, h_cols), lambda i, k, pos: (k, i, pos[0]))
    else:
        mine = pl.BlockSpec((1, tr, h_cols), lambda i, k, pos: (k, pos[0] * n_t + i, 0))
    return pl.pallas_call(
        body, grid_spec=pltpu.PrefetchScalarGridSpec(
            num_scalar_prefetch=1, grid=(n_t, N_CHIPS), in_specs=[mine, blk],
            out_specs=[blk, pl.BlockSpec((tr, h_cols), lambda i, k, pos: (i, 0))]),
        out_shape=[jax.ShapeDtypeStruct((N_CHIPS, h_rows, h_cols), BF16), jax.ShapeDtypeStruct((h_rows, h_cols), F32)],
        name=name, compiler_params=_cparams(2),
    )(pos, grad, from_sibling)


def chip_sum(name, pos, own, landed, split_cols):
    half, cols = own.shape
    tr = _row_tile(half, cols)
    n_t = half // tr

    def body(pos_ref, p_ref, l_ref, o_ref):
        o_ref[...] = ((p_ref[...] + l_ref[0].astype(F32)) + l_ref[1].astype(F32)) + l_ref[2].astype(F32)

    if split_cols:
        out_spec, out_shape = pl.BlockSpec((tr, cols), lambda i, pos: (i, pos[0])), (half, 2 * cols)
    else:
        out_spec, out_shape = pl.BlockSpec((tr, cols), lambda i, pos: (pos[0] * n_t + i, 0)), (2 * half, cols)
    return pl.pallas_call(
        body, grid_spec=pltpu.PrefetchScalarGridSpec(
            num_scalar_prefetch=1, grid=(n_t,),
            in_specs=[pl.BlockSpec((tr, cols), lambda i, pos: (i, 0)), pl.BlockSpec((3, tr, cols), lambda i, pos: (0, i, 0))],
            out_specs=out_spec),
        out_shape=jax.ShapeDtypeStruct(out_shape, F32), name=name, compiler_params=_cparams(1),
    )(pos, own, landed)


class OverlappedReduceScatter:
    def __init__(self, tag, pos, grads):
        self.n = lambda t: f"{t}_{tag}"
        self.pos, self.grads = pos, grads
        self.swap, self.token = exchange_start(self.n("swap_start"), "swap", grads)

    def middle(self, after):
        self.grads, from_sibling = exchange_wait(self.n("swap_wait"), self.swap, after)
        self.sums = [pair_sum(self.n(f"pair_sum{w}"), self.pos, g, s) for w, (g, s) in enumerate(zip(self.grads, from_sibling))]
        self.scatter, self.token = exchange_start(self.n("scatter_start"), "scatter", [b for b, _ in self.sums])

    def finish(self, after):
        _, landed = exchange_wait(self.n("scatter_wait"), self.scatter, after)
        halves = [chip_sum(self.n(f"chip_sum{w}"), self.pos, own, l, _split_cols(g.shape[1]))
                  for w, ((_, own), l, g) in enumerate(zip(self.sums, landed, self.grads))]
        return share_halves(self.n("share_halves"), halves)


def sum_devices(gathered):
    m_per = gathered.shape[0] // 8

    def body(g_ref, o_ref):
        tot = g_ref[pl.ds(0, m_per), :]
        for dev in range(1, 8):
            tot = tot + g_ref[pl.ds(dev * m_per, m_per), :]
        o_ref[...] = tot

    return pl.pallas_call(
        body, out_shape=jax.ShapeDtypeStruct((m_per, gathered.shape[1]), F32),
        in_specs=[pl.BlockSpec(memory_space=pltpu.VMEM)], out_specs=pl.BlockSpec(memory_space=pltpu.VMEM), name="sum_devices",
    )(gathered)


def kernel(x, p, g_mix, w_in, b_fox_f, fox_q_gain, fox_k_gain, sc_conv_w, dn_conv_w, dn_a_log, dn_dt_bias, dn_norm_gain, w_branch, w_o, g_ffn, w_up, ffn_conv_w, w_down, g_ple, w_ple_gate, w_ple, loss_target, m_g_mix, m_w_in, m_b_fox_f, m_fox_q_gain, m_fox_k_gain, m_sc_conv_w, m_dn_conv_w, m_dn_a_log, m_dn_dt_bias, m_dn_norm_gain, m_w_branch, m_w_o, m_g_ffn, m_w_up, m_ffn_conv_w, m_w_down, m_g_ple, m_w_ple_gate, m_w_ple, v_g_mix, v_w_in, v_b_fox_f, v_fox_q_gain, v_fox_k_gain, v_sc_conv_w, v_dn_conv_w, v_dn_a_log, v_dn_dt_bias, v_dn_norm_gain, v_w_branch, v_w_o, v_g_ffn, v_w_up, v_ffn_conv_w, v_w_down, v_g_ple, v_w_ple_gate, v_w_ple):
    a = dict(g_mix=g_mix, w_in=w_in, b_fox_f=b_fox_f, fox_q_gain=fox_q_gain, fox_k_gain=fox_k_gain, sc_conv_w=sc_conv_w,
             dn_conv_w=dn_conv_w, dn_a_log=dn_a_log, dn_dt_bias=dn_dt_bias, dn_norm_gain=dn_norm_gain, w_branch=w_branch, w_o=w_o,
             g_ffn=g_ffn, w_up=w_up, ffn_conv_w=ffn_conv_w, w_down=w_down, g_ple=g_ple, w_ple_gate=w_ple_gate, w_ple=w_ple)
    mom = dict(g_mix=m_g_mix, w_in=m_w_in, b_fox_f=m_b_fox_f, fox_q_gain=m_fox_q_gain, fox_k_gain=m_fox_k_gain, sc_conv_w=m_sc_conv_w,
               dn_conv_w=m_dn_conv_w, dn_a_log=m_dn_a_log, dn_dt_bias=m_dn_dt_bias, dn_norm_gain=m_dn_norm_gain, w_branch=m_w_branch,
               w_o=m_w_o, g_ffn=m_g_ffn, w_up=m_w_up, ffn_conv_w=m_ffn_conv_w, w_down=m_w_down, g_ple=m_g_ple, w_ple_gate=m_w_ple_gate,
               w_ple=m_w_ple)
    var = dict(g_mix=v_g_mix, w_in=v_w_in, b_fox_f=v_b_fox_f, fox_q_gain=v_fox_q_gain, fox_k_gain=v_fox_k_gain, sc_conv_w=v_sc_conv_w,
               dn_conv_w=v_dn_conv_w, dn_a_log=v_dn_a_log, dn_dt_bias=v_dn_dt_bias, dn_norm_gain=v_dn_norm_gain, w_branch=v_w_branch,
               w_o=v_w_o, g_ffn=v_g_ffn, w_up=v_w_up, ffn_conv_w=v_ffn_conv_w, w_down=v_w_down, g_ple=v_g_ple, w_ple_gate=v_w_ple_gate,
               w_ple=v_w_ple)
    cx, cy, cc = lax.axis_index("x"), lax.axis_index("y"), lax.axis_index("c")
    chip = 2 * cx + cy
    pos = jnp.stack([cc, chip]).astype(jnp.int32)

    def as_blocks(t):
        return t.reshape(2, -1, t.shape[-1])

    def own_block_in(got, shards):
        return [lax.dynamic_update_slice(g, s[None], (chip, 0, 0)) for g, s in zip(got, shards)]

    conv_shapes = [a[nm].shape for nm in CONVS]
    conv_all, conv_token = gather_small("gather_conv_w", pack_rows([a[nm] for nm in CONVS], F32))
    def w_in_block(li, token):
        stored = jnp.transpose(a["w_in"], (2, 0, 1))[:, li, :]
        return (stored + token[0, 0]).astype(BF16).T

    w_in0 = [w_in_block(0, conv_token)]
    gather_in0, gather_in0_token = exchange_start("gather_start_w_in_l0", "gather_half", w_in0)
    shards0 = w_in0 + [(as_blocks(a[nm])[0] + gather_in0_token[0, 0]).astype(BF16) for nm in BIG[1:]]
    gather0, gather0_token = exchange_start("gather_start_l0", "gather", shards0[1:])
    shards1 = [w_in_block(1, gather0_token)] + [(as_blocks(a[nm])[1] + gather0_token[0, 0]).astype(BF16) for nm in BIG[1:]]
    gather1, gather1_in_token = exchange_start("gather_start_w_in_l1", "gather", shards1[:1])
    shards1[1:] = [s + gather1_in_token[0, 0].astype(BF16) for s in shards1[1:]]
    gather1_rest, gather1_token = exchange_start("gather_start_l1", "gather", shards1[1:])
    conv_rows = conv_all.shape[0] // 8
    conv_chip = [unpack_rows(conv_all[2 * k * conv_rows:(2 * k + 1) * conv_rows], conv_shapes) for k in range(N_CHIPS)]
    conv = {nm: jnp.concatenate([conv_chip[k][i] for k in range(N_CHIPS)], axis=2) for i, nm in enumerate(CONVS)}

    weights, saved = [None, None], [None, None]
    mine_in0, got_in0 = exchange_wait("gather_wait_w_in_l0", gather_in0, gather1_token)
    got_in0 = forward_halves("forward_w_in_l0", got_in0)
    first_weights = hang_on(layer_weights(0, own_block_in(got_in0, mine_in0), conv, a), gather1_token)

    def rest_of_layer0(after):
        mine, got = exchange_wait("gather_wait_l0", gather0, after)
        return later_weights(own_block_in(got, mine))

    act, saved[0], weights[0] = layer_fwd(0, x[0], p[0, 0], first_weights, more_weights=rest_of_layer0)
    mine1, got1 = exchange_wait("gather_wait_w_in_l1", gather1, act)

    def rest_of_layer1(after):
        mine, got = exchange_wait("gather_wait_l1", gather1_rest, after)
        return later_weights(own_block_in(got, mine))

    act, saved[1], weights[1] = layer_fwd(1, act, p[1, 0], layer_weights(1, own_block_in(got1, mine1), conv, a),
                                          more_weights=rest_of_layer1)
    d_act, loss_part = loss_call(act, loss_target[0])
    loss = lax.psum(loss_part, ("x", "y", "c"))
    layer_grads = [None, None]
    d_act, layer_grads[1] = layer_bwd(1, d_act, saved[1], weights[1])
    rs1 = OverlappedReduceScatter("l1", pos, [layer_grads[1][nm] for nm in BIG])
    rs0 = []

    def stage_mid(after, g):
        rs1.middle(after)
        return rs1.token

    def stage_late(after, g):
        rs0.append(OverlappedReduceScatter("l0", pos, [g[nm] for nm in BIG[1:]]))
        return rs0[0].token

    def stage_last(after, g):
        rs0[0].middle(after)
        return rs0[0].token

    def stage_w_in(after, g):
        rs0.append(OverlappedReduceScatter("w_in_l0", pos, [g["w_in"]]))
        return rs0[1].token

    d_act, layer_grads[0] = layer_bwd(0, d_act, saved[0], hang_on(weights[0], rs1.token),
                                      hooks=dict(mid=stage_mid, late=stage_late, last=stage_last, w_in=stage_w_in))
    rs0[1].middle(d_act)
    reduced = [rs0[0].finish(rs0[1].token), rs1.finish(rs0[1].token)]
    grad_x = d_act[None]

    def both(nm):
        return jnp.stack([layer_grads[0][nm], layer_grads[1][nm]])

    local = {nm: both(nm) for nm in ("g_mix", "b_fox_f", "fox_q_gain", "fox_k_gain", "dn_norm_gain", "g_ffn", "g_ple", "sc_conv_w",
                                      "dn_conv_w", "ffn_conv_w")}
    local["dn_a_log"] = jnp.stack([layer_grads[li]["ad"][0] for li in range(2)])
    local["dn_dt_bias"] = jnp.stack([layer_grads[li]["ad"][1] for li in range(2)])

    small_names = SMALL + CONVS
    small_shapes = [local[nm].shape for nm in small_names]
    small_sum = sum_devices(gather_small("gather_small_grads", pack_rows([local[nm] for nm in small_names], F32))[0])
    small_grads = dict(zip(small_names, unpack_rows(small_sum, small_shapes)))
    for nm in CONVS:
        width = a[nm].shape[2]
        small_grads[nm] = lax.dynamic_slice_in_dim(small_grads[nm], chip * width, width, axis=2)

    grads, deltas, new_m, new_v = dict(small_grads), {}, {}, {}
    for nm in small_names:
        deltas[nm], new_m[nm], new_v[nm] = adam_call(f"adam_{nm}", a[nm], grads[nm], mom[nm], var[nm])
    for i, nm in enumerate(BIG[1:]):
        res = adam_layers(f"adam_{nm}", as_blocks(a[nm]), as_blocks(mom[nm]), as_blocks(var[nm]), reduced[0][i], reduced[1][1 + i])
        grads[nm], deltas[nm], new_m[nm], new_v[nm] = [r.reshape(a[nm].shape) for r in res]
    stored = lambda t: jnp.transpose(t, (2, 0, 1))
    res = adam_w_in("adam_w_in", stored(a["w_in"]), stored(mom["w_in"]), stored(var["w_in"]), rs0[1].finish(deltas["w_ple"])[0], reduced[1][0])
    grads["w_in"], deltas["w_in"], new_m["w_in"], new_v["w_in"] = [jnp.transpose(r, (1, 2, 0)) for r in res]
    return (loss, grad_x, *[grads[nm] for nm in WEIGHTS], *[deltas[nm] for nm in WEIGHTS], *[new_m[nm] for nm in WEIGHTS],
            *[new_v[nm] for nm in WEIGHTS])
```

```python
import functools

import jax
import jax.numpy as jnp
from jax import lax
from jax.experimental import pallas as pl
from jax.experimental.pallas import tpu as pltpu

F32 = jnp.float32
BF16 = jnp.bfloat16
HI = lax.Precision.HIGHEST
SOLVE = lax.Precision.HIGH
MESH = pl.DeviceIdType.MESH

D_MODEL = 1024
BRANCH = 512
FOX_DH = 64
DN_DH = 128
DN_HEADS = 4
DN_CHUNK = 64
FOX_BLOCK = 128
D_FF = 2816
EPS = 1e-6
N_CHIPS = 4
LANES = 128

ADAM_LR, ADAM_B1, ADAM_B2, ADAM_EPS, ADAM_WD, ADAM_STEP = 0.001, 0.9, 0.999, 1e-08, 0.01, 10

VMEM_LIMIT = 56 * 1024 * 1024

C_FQ, C_FK, C_FV, C_SB, C_SC, C_SV, C_DN, C_DZ, C_GATE = 0, 512, 1024, 1536, 2048, 2560, 3072, 4608, 5120
IN_MAIN = 8192

BIG = ("w_in", "w_branch", "w_o", "w_up", "w_down", "w_ple_gate", "w_ple")
CONVS = ("sc_conv_w", "dn_conv_w", "ffn_conv_w")
SMALL = ("g_mix", "b_fox_f", "fox_q_gain", "fox_k_gain", "dn_a_log", "dn_dt_bias", "dn_norm_gain", "g_ffn", "g_ple")
WEIGHTS = ("g_mix", "w_in", "b_fox_f", "fox_q_gain", "fox_k_gain", "sc_conv_w", "dn_conv_w", "dn_a_log", "dn_dt_bias",
           "dn_norm_gain", "w_branch", "w_o", "g_ffn", "w_up", "ffn_conv_w", "w_down", "g_ple", "w_ple_gate", "w_ple")


def _iota(shape, dim):
    return lax.broadcasted_iota(jnp.int32, shape, dim)


def _dg(a, b, mode, prec=None):
    dims = {"nn": ((1,), (0,)), "nt": ((1,), (1,)), "tn": ((0,), (0,))}[mode]
    return lax.dot_general(a, b, (dims, ((), ())), precision=prec, preferred_element_type=F32)


def _bdot_impl(a, b, mode):
    return _dg(a.astype(BF16), b.astype(BF16), mode)


@functools.partial(jax.custom_vjp, nondiff_argnums=(2,))
def _bdot_diff(a, b, mode):
    return _bdot_impl(a, b, mode)


def _bdot_fwd(a, b, mode):
    return _bdot_impl(a, b, mode), (a, b)


def _bdot_bwd(mode, res, g):
    a, b = res
    if mode == "nn":
        da, db = _bdot_impl(g, b, "nt"), _bdot_impl(a, g, "tn")
    elif mode == "nt":
        da, db = _bdot_impl(g, b, "nn"), _bdot_impl(g, a, "tn")
    else:
        da, db = _bdot_impl(b, g, "nt"), _bdot_impl(a, g, "nn")
    return da.astype(a.dtype), db.astype(b.dtype)


_bdot_diff.defvjp(_bdot_fwd, _bdot_bwd)


def _bdot(d):
    return _bdot_diff if d else _bdot_impl


def _shift_impl(x, k):
    return jnp.where(_iota(x.shape, 0) >= k, pltpu.roll(x, k, 0), 0.0)


def _unshift_impl(g, k):
    n = g.shape[0]
    return jnp.where(_iota(g.shape, 0) < n - k, pltpu.roll(g, n - k, 0), 0.0)


@functools.partial(jax.custom_vjp, nondiff_argnums=(1,))
def _shift_diff(x, k):
    return _shift_impl(x, k)


_shift_diff.defvjp(lambda x, k: (_shift_impl(x, k), None), lambda k, _, g: (_unshift_impl(g, k),))


def _row(w, j):
    return jnp.sum(jnp.where(_iota(w.shape, 0) == j, w, 0.0), axis=0, keepdims=True)


def _col(w, j):
    return jnp.sum(jnp.where(_iota(w.shape, 1) == j, w, 0.0), axis=1, keepdims=True)


def _conv(d, x, w):
    shift = _shift_diff if d else _shift_impl
    taps = w.shape[0]
    y = x * _row(w, taps - 1)
    for j in range(taps - 1):
        y = y + shift(x, taps - 1 - j) * _row(w, j)
    return y


def _softplus(x):
    return jnp.maximum(x, 0.0) + jnp.log(1.0 + jnp.exp(-jnp.abs(x)))


def _sigmoid(x):
    return 0.5 * (jnp.tanh(0.5 * x) + 1.0)


def _silu(x):
    return x * _sigmoid(x)


def _rms(x, gain):
    return x * lax.rsqrt(jnp.mean(x * x, axis=-1, keepdims=True) + EPS) * gain


def _rms_fn(d, pids, x, gain):
    return (_rms(x, gain),)


def _loss_fn(d, pids, y, t):
    e = y - t
    part = 0.5 / D_MODEL * jnp.sum(e * e, keepdims=True)
    return e * (1.0 / D_MODEL), jnp.broadcast_to(part, (8, LANES))


def _fox_prep_fn(d, pids, q, k, gq, gk):
    first = _iota(q.shape, 1) < FOX_DH

    def norm(x, gain):
        sq = x * x
        ss_a = jnp.sum(jnp.where(first, sq, 0.0), axis=1, keepdims=True)
        ss_b = jnp.sum(jnp.where(first, 0.0, sq), axis=1, keepdims=True)
        rs = jnp.where(first, lax.rsqrt(ss_a / FOX_DH + EPS), lax.rsqrt(ss_b / FOX_DH + EPS))
        return x * rs * gain

    return norm(q, gq) * FOX_DH ** -0.5, norm(k, gk)


def _fox_gate_fn(d, pids, f, bias):
    logf = -_softplus(-(f + bias))
    n_r, n_c = logf.shape
    tri = (_iota((n_c, n_c), 0) <= _iota((n_c, n_c), 1)).astype(F32)
    within = _dg(logf, tri, "nn", HI)
    tot = jnp.broadcast_to(jnp.sum(logf, axis=1, keepdims=True), logf.shape)
    below = (_iota((n_r, n_r), 1) < _iota((n_r, n_r), 0)).astype(F32)
    return (within + _dg(below, tot, "nn", HI),)


def _fox_attn_fn(q_block0, d, pids, q, k, v, cq_a, cq_b, ck_a, ck_b):
    dot = _bdot(d)
    first = _iota(q.shape, 1) < FOX_DH
    n_q, n_k = q.shape[0], k.shape[0]
    causal = ((q_block0 + pids[1]) * n_q + _iota((n_q, n_k), 0)) >= _iota((n_q, n_k), 1)

    qs = [jnp.where(first, q, 0.0), jnp.where(first, 0.0, q)]
    s = _each(lambda qh, cq, ck: jnp.where(causal, dot(qh, k, "nt") + cq - ck, -1e30), qs, [cq_a, cq_b], [ck_a, ck_b])
    e = [jnp.exp(si - lax.stop_gradient(jnp.max(si, axis=1, keepdims=True))) for si in s]
    o_a, o_b = [dot(ei * (1.0 / jnp.sum(ei, axis=1, keepdims=True)), v, "nn") for ei in e]
    return (jnp.where(first, o_a, o_b),)


def _sconv_fn(d, pids, sb, sc, sv, w):
    return (sb * _conv(d, sc * sv, w),)


def _dnconv_fn(d, pids, x, w):
    return (_silu(_conv(d, x, w)),)


def _merge_fn(d, pids, y0, y1, y2, g0, g1, g2):
    return (_sigmoid(g0) * y0 + _sigmoid(g1) * y1 + _sigmoid(g2) * y2,)


def _ffn_act_fn(d, pids, ug, uv, wg, wv):
    return (_silu(_conv(d, ug, wg)) * _conv(d, uv, wv),)


def _ple_fn(d, pids, gpre, pe, x):
    return (x + _sigmoid(gpre) * pe,)


def _adam_fn(d, pids, w, g, m, v):
    m2 = ADAM_B1 * m + (1.0 - ADAM_B1) * g
    v2 = ADAM_B2 * v + (1.0 - ADAM_B2) * (g * g)
    m_hat = m2 / (1.0 - ADAM_B1 ** ADAM_STEP)
    v_hat = v2 / (1.0 - ADAM_B2 ** ADAM_STEP)
    delta = -ADAM_LR * (m_hat / (jnp.sqrt(v_hat) + ADAM_EPS) + ADAM_WD * w)
    return delta, m2, v2


def _each(fn, *lists):
    return [fn(*args) for args in zip(*lists)]


def _tri_inv_impl(mats):
    n = mats[0].shape[0]
    r, c = _iota((n, n), 0), _iota((n, n), 1)
    diag_blk = (r >> 4) == (c >> 4)
    eye = (r == c).astype(F32)
    mm = lambda us, ws: _each(lambda u, w: _dg(u, w, "nn", SOLVE), us, ws)
    grow = lambda ps, xs: _each(lambda p, px: p + px, ps, mm(ps, xs))
    x = [jnp.where(diag_blk, -a, 0.0) for a in mats]
    p = [eye + xi for xi in x]
    x2 = mm(x, x)
    p = grow(p, x2)
    x4 = mm(x2, x2)
    p = grow(p, x4)
    p = grow(p, mm(x4, x4))
    y = [-yi for yi in mm(p, [jnp.where(diag_blk, 0.0, a) for a in mats])]
    q = grow([eye + yi for yi in y], mm(y, y))
    return mm(q, p)


@jax.custom_vjp
def _tri_inv_diff(mats):
    return _tri_inv_impl(mats)


def _tri_inv_fwd(mats):
    ts = _tri_inv_impl(mats)
    return ts, ts


def _tri_inv_bwd(ts, gs):
    left = _each(lambda t, g: _dg(t, g, "tn", SOLVE), ts, gs)
    return ([-m for m in _each(lambda l, t: _dg(l, t, "nt", SOLVE), left, ts)],)


_tri_inv_diff.defvjp(_tri_inv_fwd, _tri_inv_bwd)


def _dn_local(d, qs, ks, vs, a_cs, a_rs, b_cs, a_logs, dt_bs):
    dot = _bdot(d)
    inv = _tri_inv_diff if d else _tri_inv_impl
    n = qs[0].shape[0]
    r, c = _iota((n, n), 0), _iota((n, n), 1)
    incl, strict, upper = r >= c, r > c, r <= c
    qs = [q * lax.rsqrt(jnp.sum(q * q, axis=1, keepdims=True) + EPS) * DN_DH ** -0.5 for q in qs]
    ks = [k * lax.rsqrt(jnp.sum(k * k, axis=1, keepdims=True) + EPS) for k in ks]
    betas = [_sigmoid(b) for b in b_cs]
    rates = [-jnp.exp(a) for a in a_logs]
    g_cs = _each(lambda rate, a, dt: rate * _softplus(a + dt), rates, a_cs, dt_bs)
    g_rs = _each(lambda rate, a, dt: rate * _softplus(a + dt), rates, a_rs, dt_bs)
    gcum_cs = [jnp.sum(jnp.where(incl, g, 0.0), axis=1, keepdims=True) for g in g_rs]
    gcum_rs = [jnp.sum(jnp.where(upper, g, 0.0), axis=0, keepdims=True) for g in g_cs]
    decays = _each(lambda gc, gr: jnp.exp(jnp.where(incl, gc - gr, -1e30)), gcum_cs, gcum_rs)
    kbs = _each(lambda k, b: k * b, ks, betas)
    kk = _each(lambda kb, k: dot(kb, k, "nt"), kbs, ks)
    ts = inv(_each(lambda m, dec: jnp.where(strict, m * dec, 0.0), kk, decays))
    e_gs = [jnp.exp(g) for g in gcum_cs]
    us = _each(lambda t, v, b: _dg(t, v * b, "nn", SOLVE), ts, vs, betas)
    k_cums = _each(lambda t, kb, e: _dg(t, kb * e, "nn", SOLVE), ts, kbs, e_gs)
    qk = _each(lambda q, k: dot(q, k, "nt"), qs, ks)
    qk = _each(lambda m, dec: jnp.where(incl, m * dec, 0.0), qk, decays)
    g_lasts = [jnp.sum(g, axis=0, keepdims=True) for g in g_cs]
    q_decs = _each(lambda q, e: q * e, qs, e_gs)
    k_decs = _each(lambda k, gl, gc: k * jnp.exp(gl - gc), ks, g_lasts, gcum_cs)
    return list(zip(us, k_cums, q_decs, k_decs, qk, g_lasts))


def _dn_step(d, s_prevs, items, zs, gain):
    dot = _bdot(d)
    us, k_cums, q_decs, k_decs, qks, g_lasts = [list(t) for t in zip(*items)]
    v_news = _each(lambda u, kc, s: u - dot(kc, s, "nn"), us, k_cums, s_prevs)
    inter = _each(lambda qd, s: dot(qd, s, "nn"), q_decs, s_prevs)
    outs = _each(lambda o, qk, vn: o + dot(qk, vn, "nn"), inter, qks, v_news)
    s_nexts = _each(lambda s, gl, kd, vn: s * jnp.exp(gl) + dot(kd, vn, "tn"), s_prevs, g_lasts, k_decs, v_news)
    return _each(lambda o, z: _rms(o, gain) * _silu(z), outs, zs), s_nexts


def _split_heads(t):
    return [t[:, h * DN_DH:(h + 1) * DN_DH] for h in range(t.shape[1] // DN_DH)]


def _dn_gates(ps, a_rows, ad):
    hs = range(DN_HEADS)
    return ([_col(ps, 12 + h) for h in hs], [_row(a_rows, h) for h in hs], [_col(ps, 8 + h) for h in hs],
            [_col(_row(ad, 0), h) for h in hs], [_col(_row(ad, 1), h) for h in hs])


def _head_rows(vals):
    row = _iota((8, LANES), 0)
    tile = jnp.zeros((8, LANES), F32)
    for h, val in enumerate(vals):
        tile = tile + jnp.where(row == h, val, 0.0)
    return tile


def _cparams(n_axes):
    return pltpu.CompilerParams(dimension_semantics=("arbitrary",) * n_axes, vmem_limit_bytes=VMEM_LIMIT)


def _first_visit(acc_axes):
    cond = None
    for a in acc_axes:
        here = pl.program_id(a) == 0
        cond = here if cond is None else jnp.logical_and(cond, here)
    return cond


def _tile(ref, widen=False):
    val = ref[...]
    shape = val.shape
    while len(shape) > 2 and shape[0] == 1:
        shape = shape[1:]
    val = val.reshape(shape)
    return val.astype(F32) if widen and val.dtype == BF16 else val


def _store(ref, val, first):
    val = val.astype(ref.dtype).reshape(ref.shape)
    if first is None:
        ref[...] = val
        return

    @pl.when(first)
    def _():
        ref[...] = val

    @pl.when(jnp.logical_not(first))
    def _():
        ref[...] += val


def _specs(ops):
    return [pl.BlockSpec(block, imap) for _, block, imap in ops]


def tile_fwd(name, fn, grid, ins, outs, raw=()):
    n_in = len(ins)

    def body(*refs):
        pids = tuple(pl.program_id(a) for a in range(len(grid)))
        firsts = [_first_visit(o[4]) if o[4] else None for o in outs]
        res = fn(False, pids, *[_tile(r, i not in raw) for i, r in enumerate(refs[:n_in])])
        for ref, val, first in zip(refs[n_in:], res, firsts):
            _store(ref, val, first)

    out = pl.pallas_call(
        body, grid=grid, in_specs=_specs(ins),
        out_specs=[pl.BlockSpec(o[2], o[3]) for o in outs],
        out_shape=[jax.ShapeDtypeStruct(o[0], o[1]) for o in outs],
        name=name, compiler_params=_cparams(len(grid)),
    )(*[a for a, _, _ in ins])
    return out


def tile_bwd(name, fn, grid, ins, cots, diff, adds=None, raw=()):
    adds = adds or {}
    n_in, n_cot = len(ins), len(cots)
    add_pos = sorted(adds)
    diff_idx = [d[0] for d in diff]
    out_desc = [d[2] if len(d) > 2 and d[2] is not None else (ins[d[0]][0].shape, ins[d[0]][1], ins[d[0]][2]) for d in diff]
    out_dtypes = [d[3] if len(d) > 3 else F32 for d in diff]

    def body(*refs):
        pids = tuple(pl.program_id(a) for a in range(len(grid)))
        firsts = [_first_visit(d[1]) if d[1] else None for d in diff]
        vals = [_tile(r, i not in raw) for i, r in enumerate(refs[:n_in])]
        cot_vals = [_tile(r, True) for r in refs[n_in:n_in + n_cot]]
        add_vals = [_tile(r) for r in refs[n_in + n_cot:n_in + n_cot + len(add_pos)]]
        out_refs = refs[n_in + n_cot + len(add_pos):]

        def f(*dv):
            full = list(vals)
            for i, val in zip(diff_idx, dv):
                full[i] = val
            return fn(True, pids, *full)

        prim, vjp = jax.vjp(f, *[vals[i].astype(F32) for i in diff_idx])
        grads = list(vjp(tuple(c.astype(o.dtype) for c, o in zip(cot_vals, prim))))
        for pos, val in zip(add_pos, add_vals):
            extra = val.astype(F32) if firsts[pos] is None else jnp.where(firsts[pos], val.astype(F32), 0.0)
            grads[pos] = grads[pos] + extra
        for ref, val, first in zip(out_refs, grads, firsts):
            _store(ref, val, first)

    all_ins = list(ins) + list(cots) + [adds[p] for p in add_pos]
    out = pl.pallas_call(
        body, grid=grid, in_specs=_specs(all_ins),
        out_specs=[pl.BlockSpec(o[1], o[2]) for o in out_desc],
        out_shape=[jax.ShapeDtypeStruct(o[0], dt) for o, dt in zip(out_desc, out_dtypes)],
        name=name, compiler_params=_cparams(len(grid)),
    )(*[a for a, _, _ in all_ins])
    return out


def _pick(dim, cands):
    for c in cands:
        if dim % c == 0:
            return c
    return dim


MM_VMEM_BUDGET = 40 * 1024 * 1024
MM_TILES = (1024, 512, 1408, 256, 128)


def mm(name, a, b, mode, add=None, out_dtype=F32, blocks=None, into=None):
    wide = None
    if mode == "nn":
        (m, kk), n = a.shape, b.shape[-1]
    elif mode == "nt":
        (m, kk), n = a.shape, b.shape[-2]
    else:
        (kk, m), n = a.shape, b.shape[1]
    if blocks is not None:
        lo, n_blk = blocks
        wide = b.shape[-1] if mode != "tn" else n // n_blk
        if mode == "nn":
            n = wide * n_blk
    tm = _pick(m, MM_TILES)
    if mode == "nt" and blocks is not None:
        tn, tk = _pick(n, MM_TILES), _pick(wide, MM_TILES[:-1])
    elif blocks is not None:
        tn, tk = _pick(wide, MM_TILES[:-1]), _pick(kk, MM_TILES)
    else:
        tn, tk = _pick(n, MM_TILES), _pick(kk, MM_TILES)
    if mode == "tn" or blocks is None:
        tk = _pick(kk, (2048,) + MM_TILES)
    if mode != "tn" and add is None and m % 2048 == 0 and (n // tn) * (kk // tk) > 1:
        windows = 2 * (2048 * tk * a.dtype.itemsize + tk * tn * b.dtype.itemsize + 2048 * tn * jnp.dtype(out_dtype).itemsize)
        if windows + 2048 * tn * 4 <= MM_VMEM_BUDGET:
            tm = 2048
    nk = kk // tk
    a_spec = pl.BlockSpec((tk, tm), lambda i, j, k: (k, i)) if mode == "tn" else pl.BlockSpec((tm, tk), lambda i, j, k: (i, k))
    o_spec = pl.BlockSpec((tm, tn), lambda i, j, k: (i, j))
    out_shape = (m, n)
    if blocks is None:
        b_spec = pl.BlockSpec((tn, tk), lambda i, j, k: (j, k)) if mode == "nt" else pl.BlockSpec((tk, tn), lambda i, j, k: (k, j))
    elif mode == "nn":
        per = wide // tn
        b_spec = pl.BlockSpec((1, tk, tn), lambda i, j, k: (lo + j // per, k, j % per))
    elif mode == "nt":
        per = wide // tk
        b_spec = pl.BlockSpec((1, tn, tk), lambda i, j, k: (lo + k // per, j, k % per))
    else:
        per = wide // tn
        total, first = (into[0], into[1]) if into is not None else (n_blk, 0)
        b_spec = pl.BlockSpec((tk, tn), lambda i, j, k: (k, j))
        o_spec = pl.BlockSpec((1, tm, tn), lambda i, j, k: (first + j // per, i, j % per))
        out_shape = (total, m, wide)

    def body(*refs):
        a_ref, b_ref = refs[0], refs[1]
        add_ref = refs[2] if add is not None else None
        o_ref, acc = refs[-2], refs[-1]
        k = pl.program_id(2)
        part = _bdot_impl(_tile(a_ref), _tile(b_ref), mode)

        @pl.when(k == 0)
        def _():
            acc[...] = part

        @pl.when(k > 0)
        def _():
            acc[...] += part

        @pl.when(k == nk - 1)
        def _():
            res = acc[...]
            if add_ref is not None:
                res = res + add_ref[...]
            o_ref[...] = res.astype(o_ref.dtype).reshape(o_ref.shape)

    operands = [a, b] + ([add] if add is not None else [])
    in_specs = [a_spec, b_spec] + ([o_spec] if add is not None else [])
    aliases = {}
    if into is not None and len(into) > 2:
        operands, in_specs, aliases = operands + [into[2]], in_specs + [pl.BlockSpec(memory_space=pl.ANY)], {len(operands): 0}
    return pl.pallas_call(
        body, grid=(m // tm, n // tn, nk), in_specs=in_specs, out_specs=o_spec,
        out_shape=jax.ShapeDtypeStruct(out_shape, out_dtype),
        scratch_shapes=[pltpu.VMEM((tm, tn), F32)], input_output_aliases=aliases,
        name=name, compiler_params=_cparams(3),
    )(*operands)


def _rows(x, width=None, off=0, tm=256):
    width = x.shape[1] if width is None else width
    return (x, (tm, width), lambda i, off=off: (i, off))


def _whole(x):
    nd = x.ndim
    return (x, x.shape, lambda *pids, nd=nd: (0,) * nd)


RMS_ROWS = 512


def _rms_ops(x, gain):
    return [_rows(x, tm=RMS_ROWS), _whole(gain)]


def rms_fwd(name, x, gain):
    s, dm = x.shape
    return tile_fwd(name, _rms_fn, (s // RMS_ROWS,), _rms_ops(x, gain), [((s, dm), BF16, (RMS_ROWS, dm), lambda i: (i, 0), ())])[0]


def rms_bwd(name, x, gain, dh, dres):
    s = x.shape[0]
    return tile_bwd(name, _rms_fn, (s // RMS_ROWS,), _rms_ops(x, gain), [_rows(dh, tm=RMS_ROWS)], [(0, ()), (1, (0,))],
                    adds={0: _rows(dres, tm=RMS_ROWS)})


def loss_call(y, t):
    s, dm = y.shape
    dy, part = tile_fwd("loss", _loss_fn, (s // RMS_ROWS,), [_rows(y, tm=RMS_ROWS), _rows(t, tm=RMS_ROWS)],
                        [((s, dm), F32, (RMS_ROWS, dm), lambda i: (i, 0), ()), ((8, LANES), F32, (8, LANES), lambda i: (0, 0), (0,))])
    return dy, part[0, 0]


def _fox_prep_ops(pm, gq, gk):
    tm = 512
    return [(pm, (tm, LANES), lambda i, j: (i, C_FQ // LANES + j)), (pm, (tm, LANES), lambda i, j: (i, C_FK // LANES + j)),
            _whole(gq), _whole(gk)]


def fox_prep_fwd(name, pm, gq, gk):
    s = pm.shape[0]
    out = ((s, BRANCH), BF16, (512, LANES), lambda i, j: (i, j), ())
    return tile_fwd(name, _fox_prep_fn, (s // 512, 4), _fox_prep_ops(pm, gq, gk), [out, out])


def fox_prep_bwd(name, pm, gq, gk, dqn, dkn):
    s = pm.shape[0]
    cot = lambda g: (g, (512, LANES), lambda i, j: (i, j))
    own = ((s, BRANCH), (512, LANES), lambda i, j: (i, j))
    return tile_bwd(name, _fox_prep_fn, (s // 512, 4), _fox_prep_ops(pm, gq, gk), [cot(dqn), cot(dkn)],
                    [(0, (), own, BF16), (1, (), own, BF16), (2, (0, 1)), (3, (0, 1))])


def _fox_gate_ops(f_t, bias):
    return [(f_t, (1,) + f_t.shape[1:], lambda h: (h, 0, 0)), (bias, (1, 1, 1), lambda h: (h, 0, 0))]


def fox_gate_fwd(name, f_t, bias):
    n_h = f_t.shape[0]
    return tile_fwd(name, _fox_gate_fn, (n_h,), _fox_gate_ops(f_t, bias),
                    [(f_t.shape, F32, (1,) + f_t.shape[1:], lambda h: (h, 0, 0), ())])[0]


def fox_gate_bwd(name, f_t, bias, dcum):
    n_h = f_t.shape[0]
    return tile_bwd(name, _fox_gate_fn, (n_h,), _fox_gate_ops(f_t, bias),
                    [(dcum, (1,) + f_t.shape[1:], lambda h: (h, 0, 0))], [(0, ()), (1, ())])


FOX_GROUPS = 4


def _fox_groups(s):
    per = s // FOX_BLOCK // FOX_GROUPS
    return [(g * per, per, (g + 1) * per * FOX_BLOCK) for g in range(FOX_GROUPS)]


def _fox_attn_ops(qn, kn, pm, cum_c, cum_r, q0, keys):
    nb = FOX_BLOCK
    return [(qn, (nb, LANES), lambda p, i: (q0 + i, p)), (kn, (keys, LANES), lambda p, i: (0, p)),
            (pm, (keys, LANES), lambda p, i: (0, C_FV // LANES + p)),
            (cum_c, (1, nb, 1), lambda p, i: (2 * p, q0 + i, 0)), (cum_c, (1, nb, 1), lambda p, i: (2 * p + 1, q0 + i, 0)),
            (cum_r, (1, 1, keys), lambda p, i: (2 * p, 0, 0)), (cum_r, (1, 1, keys), lambda p, i: (2 * p + 1, 0, 0))]


def fox_attn_fwd(name, qn, kn, pm, cum_c, cum_r):
    s = qn.shape[0]
    parts = []
    for g, (q0, n_q, keys) in enumerate(_fox_groups(s)):
        parts.append(tile_fwd(f"{name}_g{g}", functools.partial(_fox_attn_fn, q0), (4, n_q), _fox_attn_ops(qn, kn, pm, cum_c, cum_r, q0, keys),
                              [((n_q * FOX_BLOCK, BRANCH), BF16, (FOX_BLOCK, LANES), lambda p, i: (i, p), ())], raw=(0, 1, 2))[0])
    return jnp.concatenate(parts, axis=0)


def fox_attn_bwd(name, qn, kn, pm, cum_c, cum_r, dy):
    s = qn.shape[0]
    groups = _fox_groups(s)
    d_qn, by_q, tails = [None] * len(groups), [None] * len(groups), [None] * len(groups)
    below = None
    for g in reversed(range(len(groups))):
        q0, n_q, keys = groups[g]
        rows = n_q * FOX_BLOCK
        own_q = ((rows, BRANCH), (FOX_BLOCK, LANES), lambda p, i: (i, p))
        own_k = ((keys, BRANCH), (keys, LANES), lambda p, i: (0, p))
        pair_c = ((4, rows, 1), (1, FOX_BLOCK, 1), lambda p, i: (p, i, 0))
        pair_r = ((4, 1, keys), (1, 1, keys), lambda p, i: (p, 0, 0))
        adds = {}
        if below is not None:
            adds = {1: (below[0],) + own_k[1:], 2: (below[1],) + own_k[1:], 5: (below[2],) + pair_r[1:], 6: (below[3],) + pair_r[1:]}
        g_qn, g_kn, g_v, g_cqa, g_cqb, g_cka, g_ckb = tile_bwd(
            f"{name}_g{g}", functools.partial(_fox_attn_fn, q0), (4, n_q), _fox_attn_ops(qn, kn, pm, cum_c, cum_r, q0, keys),
            [(dy, (FOX_BLOCK, LANES), lambda p, i, q0=q0: (q0 + i, p))],
            [(0, (), own_q), (1, (1,), own_k), (2, (1,), own_k), (3, (), pair_c), (4, (), pair_c), (5, (1,), pair_r), (6, (1,), pair_r)],
            adds=adds)
        below = (g_kn, g_v, g_cka, g_ckb)
        lo = groups[g - 1][2] if g else 0
        d_qn[g] = g_qn
        by_q[g] = jnp.stack([g_cqa[:, :, 0], g_cqb[:, :, 0]], axis=1).reshape(8, rows)
        tails[g] = (g_kn[lo:], g_v[lo:], jnp.stack([g_cka[:, 0, lo:], g_ckb[:, 0, lo:]], axis=1).reshape(8, keys - lo))
    d_cum = jnp.concatenate(by_q, axis=1) + jnp.concatenate([t[2] for t in tails], axis=1)
    return jnp.concatenate(d_qn, axis=0), jnp.concatenate([t[0] for t in tails], axis=0), jnp.concatenate([t[1] for t in tails], axis=0), d_cum


def sconv_ops(pm, w):
    s = pm.shape[0]
    blk = lambda c0: (pm, (s, LANES), lambda j, c0=c0: (0, c0 // LANES + j))
    return [blk(C_SB), blk(C_SC), blk(C_SV), (w, (w.shape[0], LANES), lambda j: (0, j))]


def dnconv_ops(pm, w):
    s = pm.shape[0]
    return [(pm, (s, LANES), lambda j: (0, C_DN // LANES + j)), (w, (w.shape[0], LANES), lambda j: (0, j))]


FFN_TILE = 256


def ffn_ops(ug, uv, w):
    s = ug.shape[0]
    n_t = D_FF // FFN_TILE
    return [(ug, (s, FFN_TILE), lambda j: (0, j)), (uv, (s, FFN_TILE), lambda j: (0, j)),
            (w, (w.shape[0], FFN_TILE), lambda j: (0, j)), (w, (w.shape[0], FFN_TILE), lambda j: (0, n_t + j))]


def _col_out(s, width, dtype=F32, tile=LANES):
    return ((s, width), dtype, (s, tile), lambda j: (0, j), ())


def _col_cot(g, tile=LANES):
    return (g, (g.shape[0], tile), lambda j: (0, j))


def merge_ops(yp, pm, tm=256):
    gate = lambda b: (pm, (tm, D_MODEL), lambda i, b=b: (i, C_GATE // D_MODEL + b))
    return [_rows(yp[0], tm=tm), _rows(yp[1], tm=tm), _rows(yp[2], tm=tm), gate(0), gate(1), gate(2)]


def ple_ops(gpre, pe, x):
    return [_rows(gpre, tm=RMS_ROWS), _rows(pe, tm=RMS_ROWS), _rows(x, tm=RMS_ROWS)]


def adam_call(name, w, g, m, v):
    shape = w.shape
    last = shape[-1]
    rows = w.size // last
    flat = lambda t: t.reshape(rows, last)
    tm = rows
    for cand in (512, 256, 128, 64, 32, 16, 8):
        if rows % cand == 0 and cand * last * 4 <= 2 * 1024 * 1024:
            tm = cand
            break
    spec = lambda t: (flat(t), (tm, last), lambda i: (i, 0))
    out = ((rows, last), F32, (tm, last), lambda i: (i, 0), ())
    res = tile_fwd(name, _adam_fn, (rows // tm,), [spec(w), spec(g), spec(m), spec(v)], [out, out, out])
    return [r.reshape(shape) for r in res]


def _adam_layers_fn(d, pids, w, m, v, g0, g1):
    g = jnp.where(pids[0] == 0, g0, g1)
    return (g,) + _adam_fn(d, pids, w, g, m, v)


def adam_layers(name, w, m, v, g0, g1):
    _, rows, cols = w.shape
    tm = _row_tile(rows, cols)
    n_t = rows // tm
    lay = lambda t: (t, (1, tm, cols), lambda l, i: (l, i, 0))
    ins = [lay(w), lay(m), lay(v), (g0, (tm, cols), lambda l, i: (i * (1 - l) + (n_t - 1) * l, 0)), (g1, (tm, cols), lambda l, i: (i * l, 0))]
    out = (w.shape, F32, (1, tm, cols), lambda l, i: (l, i, 0), ())
    return tile_fwd(name, _adam_layers_fn, (2, n_t), ins, [out, out, out, out])


def adam_w_in(name, w, m, v, g0, g1):
    rows, n_l, cols = w.shape

    def body(w_ref, m_ref, v_ref, g0_ref, g1_ref, g_out, d_out, m_out, v_out):
        step = 64

        def update(at):
            g0, g1 = g0_ref[at, :], g1_ref[at, :]
            layer = _iota((g0.shape[0], n_l, LANES), 1)
            g = jnp.where(layer == 0, g0[:, None, :], g1[:, None, :])
            delta, m2, v2 = _adam_fn(False, None, w_ref[at], g, m_ref[at], v_ref[at])
            for ref, val in ((g_out, g), (d_out, delta), (m_out, m2), (v_out, v2)):
                ref[at] = val

        def some_rows(i, carry):
            update(pl.ds(pl.multiple_of(i * step, step), step))
            return carry

        lax.fori_loop(0, rows // step, some_rows, 0)
        if rows % step:
            update(pl.ds(rows - rows % step, rows % step))

    both = pl.BlockSpec((rows, n_l, LANES), lambda j: (0, 0, j))
    one = pl.BlockSpec((rows, LANES), lambda j: (0, j))
    return pl.pallas_call(
        body, grid=(cols // LANES,), in_specs=[both, both, both, one, one], out_specs=[both] * 4,
        out_shape=[jax.ShapeDtypeStruct(w.shape, F32)] * 4, name=name, compiler_params=_cparams(1),
    )(w, m, v, g0, g1)


DN_GROUP = 4


def _dn_local_specs():
    rows = DN_GROUP * DN_CHUNK
    return [pl.BlockSpec((rows, 3 * BRANCH), lambda j: (j, 0)), pl.BlockSpec((rows, LANES), lambda j: (j, 0)),
            pl.BlockSpec((DN_GROUP, DN_HEADS, DN_CHUNK), lambda j: (j, 0, 0)), pl.BlockSpec((2, DN_HEADS), lambda j: (0, 0))]


def _dn_group_inputs(qkv, ps, a_rows, c):
    lo = c * DN_CHUNK
    heads = _split_heads(qkv[lo:lo + DN_CHUNK])
    return heads[0:4], heads[4:8], heads[8:12], ps[lo:lo + DN_CHUNK], a_rows[c]


def dn_local_fwd(name, dn_act, ps, a_rows, ad):
    s = dn_act.shape[0]
    n_c, n_g = s // DN_CHUNK, s // (DN_GROUP * DN_CHUNK)
    rows = DN_GROUP * DN_CHUNK

    def body(qkv_ref, ps_ref, ar_ref, ad_ref, u_ref, kc_ref, qd_ref, kd_ref, qk_ref, gl_ref):
        qkv, ps_v, a_rows_v, ad_v = qkv_ref[...], ps_ref[...], ar_ref[...], ad_ref[...]
        args = [[] for _ in range(8)]
        for c in range(DN_GROUP):
            q4, k4, v4, ps_c, ar_c = _dn_group_inputs(qkv, ps_v, a_rows_v, c)
            for lst, vals in zip(args, (q4, k4, v4) + _dn_gates(ps_c, ar_c, ad_v)):
                lst.extend(vals)
        everything = _dn_local(False, *args)
        for c in range(DN_GROUP):
            res = everything[c * DN_HEADS:(c + 1) * DN_HEADS]
            at = pl.ds(c * DN_CHUNK, DN_CHUNK)
            for ref, i in ((u_ref, 0), (kc_ref, 1), (qd_ref, 2), (kd_ref, 3)):
                ref[at, :] = jnp.concatenate([r[i] for r in res], axis=1)
            for h in range(DN_HEADS):
                qk_ref[c, h] = res[h][4]
            gl_ref[c] = _head_rows([r[5] for r in res])

    wide = pl.BlockSpec((rows, BRANCH), lambda j: (j, 0))
    return pl.pallas_call(
        body, grid=(n_g,), in_specs=_dn_local_specs(),
        out_specs=[wide, wide, wide, wide, pl.BlockSpec((DN_GROUP, DN_HEADS, DN_CHUNK, DN_CHUNK), lambda j: (j, 0, 0, 0)),
                   pl.BlockSpec((DN_GROUP, 8, LANES), lambda j: (j, 0, 0))],
        out_shape=[jax.ShapeDtypeStruct((s, BRANCH), F32)] * 4 + [jax.ShapeDtypeStruct((n_c, DN_HEADS, DN_CHUNK, DN_CHUNK), F32),
                                                                 jax.ShapeDtypeStruct((n_c, 8, LANES), F32)],
        name=name, compiler_params=_cparams(1),
    )(dn_act, ps, a_rows, ad)


def dn_local_bwd(name, dn_act, ps, a_rows, ad, cots):
    s = dn_act.shape[0]
    n_c, n_g = s // DN_CHUNK, s // (DN_GROUP * DN_CHUNK)
    rows = DN_GROUP * DN_CHUNK

    def body(qkv_ref, ps_ref, ar_ref, ad_ref, du_ref, dkc_ref, dqd_ref, dkd_ref, dqk_ref, dgl_ref, dqkv_ref, dps_ref, dar_ref, dad_ref):
        first = pl.program_id(0) == 0
        qkv, ps_v, a_rows_v, ad_v = qkv_ref[...], ps_ref[...], ar_ref[...], ad_ref[...]
        d_wide = [r[...] for r in (du_ref, dkc_ref, dqd_ref, dkd_ref)]
        qs, ks, vs, ps_cs, ar_cs, cot = [], [], [], [], [], []
        for c in range(DN_GROUP):
            q4, k4, v4, ps_c, ar_c = _dn_group_inputs(qkv, ps_v, a_rows_v, c)
            qs, ks, vs, ps_cs, ar_cs = qs + q4, ks + k4, vs + v4, ps_cs + [ps_c], ar_cs + [ar_c]
            lo = c * DN_CHUNK
            d_tiles = [_split_heads(t[lo:lo + DN_CHUNK]) for t in d_wide]
            d_gl = dgl_ref[c]
            cot += [(d_tiles[0][h], d_tiles[1][h], d_tiles[2][h], d_tiles[3][h], dqk_ref[c, h], _col(_row(d_gl, h), 0))
                    for h in range(DN_HEADS)]

        def f(qs, ks, vs, ps_cs, ar_cs, ad_v):
            gates = [[] for _ in range(5)]
            for ps_c, ar_c in zip(ps_cs, ar_cs):
                for lst, vals in zip(gates, _dn_gates(ps_c, ar_c, ad_v)):
                    lst.extend(vals)
            return _dn_local(True, qs, ks, vs, *gates)

        _, vjp = jax.vjp(f, qs, ks, vs, ps_cs, ar_cs, ad_v)
        d_q, d_k, d_v, d_ps, d_ar, d_ad = vjp(cot)
        for c in range(DN_GROUP):
            at, hs = pl.ds(c * DN_CHUNK, DN_CHUNK), slice(c * DN_HEADS, (c + 1) * DN_HEADS)
            dqkv_ref[at, :] = jnp.concatenate(d_q[hs] + d_k[hs] + d_v[hs], axis=1).astype(dqkv_ref.dtype)
            dps_ref[at, :] = d_ps[c]
            dar_ref[c] = d_ar[c]
        _store(dad_ref, d_ad, first)

    wide = pl.BlockSpec((rows, BRANCH), lambda j: (j, 0))
    specs = _dn_local_specs()
    return pl.pallas_call(
        body, grid=(n_g,),
        in_specs=specs + [wide, wide, wide, wide, pl.BlockSpec((DN_GROUP, DN_HEADS, DN_CHUNK, DN_CHUNK), lambda j: (j, 0, 0, 0)),
                          pl.BlockSpec((DN_GROUP, 8, LANES), lambda j: (j, 0, 0))],
        out_specs=specs,
        out_shape=[jax.ShapeDtypeStruct((s, 3 * BRANCH), F32), jax.ShapeDtypeStruct((s, LANES), F32),
                   jax.ShapeDtypeStruct((n_c, DN_HEADS, DN_CHUNK), F32), jax.ShapeDtypeStruct((2, DN_HEADS), F32)],
        name=name, compiler_params=_cparams(1),
    )(dn_act, ps, a_rows, ad, *cots)


def _dn_scan_specs(n_c, rev):
    idx = (lambda j: n_c - 1 - j) if rev else (lambda j: j)
    wide = pl.BlockSpec((DN_CHUNK, BRANCH), lambda j: (idx(j), 0))
    return [wide, wide, wide, wide, pl.BlockSpec((1, DN_HEADS, DN_CHUNK, DN_CHUNK), lambda j: (idx(j), 0, 0, 0)),
            pl.BlockSpec((1, 8, LANES), lambda j: (idx(j), 0, 0)), pl.BlockSpec((DN_CHUNK, BRANCH), lambda j: (idx(j), C_DZ // BRANCH)),
            pl.BlockSpec((1, DN_DH), lambda j: (0, 0))]


def _dn_scan_tiles(refs):
    u_ref, kc_ref, qd_ref, kd_ref, qk_ref, gl_ref, z_ref, g_ref = refs
    wide = [_split_heads(r[...]) for r in (u_ref, kc_ref, qd_ref, kd_ref)]
    gl = gl_ref[0]
    return [(wide[0][h], wide[1][h], wide[2][h], wide[3][h], qk_ref[0, h], _col(_row(gl, h), 0)) for h in range(DN_HEADS)], \
        _split_heads(z_ref[...].astype(F32)), g_ref[...]


def dn_scan_fwd(name, local, pm, gain):
    s = pm.shape[0]
    n_c = s // DN_CHUNK

    def body(*refs):
        y_ref, hist_ref, state = refs[8:]

        @pl.when(pl.program_id(0) == 0)
        def _():
            state[...] = jnp.zeros_like(state)

        hist_ref[0] = state[...]
        per_head, z4, gain_v = _dn_scan_tiles(refs[:8])
        ys, s_nexts = _dn_step(False, [state[h] for h in range(DN_HEADS)], per_head, z4, gain_v)
        for h in range(DN_HEADS):
            state[h] = s_nexts[h]
        y_ref[...] = jnp.concatenate(ys, axis=1).astype(y_ref.dtype)

    return pl.pallas_call(
        body, grid=(n_c,), in_specs=_dn_scan_specs(n_c, False),
        out_specs=[pl.BlockSpec((DN_CHUNK, BRANCH), lambda j: (j, 0)),
                   pl.BlockSpec((1, DN_HEADS, DN_DH, DN_DH), lambda j: (j, 0, 0, 0))],
        out_shape=[jax.ShapeDtypeStruct((s, BRANCH), BF16), jax.ShapeDtypeStruct((n_c, DN_HEADS, DN_DH, DN_DH), F32)],
        scratch_shapes=[pltpu.VMEM((DN_HEADS, DN_DH, DN_DH), F32)],
        name=name, compiler_params=_cparams(1),
    )(*local, pm, gain)


def dn_scan_bwd(name, local, pm, gain, hist, dy):
    s = pm.shape[0]
    n_c = s // DN_CHUNK

    def body(*refs):
        hist_ref, dy_ref = refs[8:10]
        du_ref, dkc_ref, dqd_ref, dkd_ref, dqk_ref, dgl_ref, dz_ref, dg_ref, d_state = refs[10:]
        first = pl.program_id(0) == 0

        @pl.when(first)
        def _():
            d_state[...] = jnp.zeros_like(d_state)

        per_head, z4, gain_v = _dn_scan_tiles(refs[:8])
        _, vjp = jax.vjp(functools.partial(_dn_step, True), [hist_ref[0, h] for h in range(DN_HEADS)], per_head, z4, gain_v)
        d_s, grads, d_z, d_gain = vjp((_split_heads(dy_ref[...].astype(F32)), [d_state[h] for h in range(DN_HEADS)]))
        for h in range(DN_HEADS):
            d_state[h] = d_s[h]
        for ref, i in ((du_ref, 0), (dkc_ref, 1), (dqd_ref, 2), (dkd_ref, 3)):
            ref[...] = jnp.concatenate([g[i] for g in grads], axis=1)
        dz_ref[...] = jnp.concatenate(d_z, axis=1).astype(dz_ref.dtype)
        for h in range(DN_HEADS):
            dqk_ref[0, h] = grads[h][4]
        dgl_ref[0] = _head_rows([g[5] for g in grads])
        _store(dg_ref, d_gain, first)

    rev = lambda j: n_c - 1 - j
    specs = _dn_scan_specs(n_c, True)
    return pl.pallas_call(
        body, grid=(n_c,),
        in_specs=specs + [pl.BlockSpec((1, DN_HEADS, DN_DH, DN_DH), lambda j: (rev(j), 0, 0, 0)),
                          pl.BlockSpec((DN_CHUNK, BRANCH), lambda j: (rev(j), 0))],
        out_specs=specs[:6] + [pl.BlockSpec((DN_CHUNK, BRANCH), lambda j: (rev(j), 0)), specs[7]],
        out_shape=[jax.ShapeDtypeStruct((s, BRANCH), F32)] * 4 + [
            jax.ShapeDtypeStruct((n_c, DN_HEADS, DN_CHUNK, DN_CHUNK), F32), jax.ShapeDtypeStruct((n_c, 8, LANES), F32),
            jax.ShapeDtypeStruct((s, BRANCH), BF16), jax.ShapeDtypeStruct((1, DN_DH), F32)],
        scratch_shapes=[pltpu.VMEM((DN_HEADS, DN_DH, DN_DH), F32)],
        name=name, compiler_params=_cparams(1),
    )(*local, pm, gain, hist, dy)


def _seq_layouts(cols, s):
    return cols.T.reshape(cols.shape[1], s // LANES, LANES)


def layer_fwd(li, x, p, w, more_weights=None):
    s = x.shape[0]
    n = lambda t: f"{t}_l{li}"
    h = rms_fwd(n("rms_mix"), x, w["g_mix"])
    pm = mm(n("in_main"), h, w["in_main"], "nn")
    ps = mm(n("in_small"), h, w["in_small"], "nn")
    qn, kn = fox_prep_fwd(n("fox_prep"), pm, w["gq"], w["gk"])
    f_t = _seq_layouts(ps[:, 0:8], s)
    cum = fox_gate_fwd(n("fox_gate"), f_t, w["b_f"])
    cum_c, cum_r = cum.reshape(8, s, 1), cum.reshape(8, 1, s)
    y_fox = fox_attn_fwd(n("fox_attn"), qn, kn, pm, cum_c, cum_r)
    y_sc = tile_fwd(n("sconv"), _sconv_fn, (BRANCH // LANES,), sconv_ops(pm, w["sc_conv_w"]), [_col_out(s, BRANCH, BF16)])[0]
    dn_act = tile_fwd(n("dnconv"), _dnconv_fn, (3 * BRANCH // LANES,), dnconv_ops(pm, w["dn_conv_w"]), [_col_out(s, 3 * BRANCH)])[0]
    a_rows = ps[:, 12:16].reshape(s // DN_CHUNK, DN_CHUNK, DN_HEADS).transpose(0, 2, 1)
    dn_local = dn_local_fwd(n("dn_local"), dn_act, ps, a_rows, w["ad"])
    y_dn, hist = dn_scan_fwd(n("dn_scan"), dn_local, pm, w["dn_gain"])
    ys = (y_fox, y_sc, y_dn)
    if more_weights is not None:
        w = {**w, **more_weights(y_dn)}
    yp = [mm(n(f"branch{b}"), ys[b], w["branch"][b], "nn", blocks=(0, N_CHIPS)) for b in range(3)]
    merged = tile_fwd(n("merge"), _merge_fn, (s // RMS_ROWS,), merge_ops(yp, pm, RMS_ROWS),
                      [((s, D_MODEL), BF16, (RMS_ROWS, D_MODEL), lambda i: (i, 0), ())])[0]
    x1 = mm(n("w_o"), merged, w["o"], "nn", add=x)
    h2 = rms_fwd(n("rms_ffn"), x1, w["g_ffn"])
    ug = mm(n("up_g"), h2, w["up"], "nn", blocks=(0, 2))
    uv = mm(n("up_v"), h2, w["up"], "nn", blocks=(2, 2))
    act = tile_fwd(n("ffn_act"), _ffn_act_fn, (D_FF // FFN_TILE,), ffn_ops(ug, uv, w["ffn_conv_w"]), [_col_out(s, D_FF, BF16, FFN_TILE)])[0]
    x2 = mm(n("down"), act, w["down"], "nn", add=x1)
    h3 = rms_fwd(n("rms_ple"), x2, w["g_ple"])
    gpre = mm(n("ple_gate"), h3, w["pg"], "nn")
    pe = mm(n("ple_emb"), p, w["ple"], "nn", blocks=(0, N_CHIPS))
    x3 = tile_fwd(n("ple"), _ple_fn, (s // RMS_ROWS,), ple_ops(gpre, pe, x2), [((s, D_MODEL), F32, (RMS_ROWS, D_MODEL), lambda i: (i, 0), ())])[0]
    saved = dict(x=x, h=h, pm=pm, ps=ps, qn=qn, kn=kn, f_t=f_t, cum_c=cum_c, cum_r=cum_r, ys=ys, dn_act=dn_act, dn_local=dn_local,
                 a_rows=a_rows, hist=hist, yp=yp, merged=merged, x1=x1, h2=h2, ug=ug, uv=uv, act=act, x2=x2, h3=h3,
                 gpre=gpre, pe=pe, p=p)
    return x3, saved, w


def hang_on(w, token):
    zero = token[0, 0]
    small = ("g_mix", "g_ffn", "g_ple", "gq", "gk", "b_f", "ad", "dn_gain", "sc_conv_w", "dn_conv_w", "ffn_conv_w")
    return {**w, **{k: w[k] + zero for k in small}}


def layer_bwd(li, dx3, sv, w, hooks=None):
    hooks = hooks or {}

    def stage(key, after, w):
        return hang_on(w, hooks[key](after, g)) if key in hooks else w

    s = dx3.shape[0]
    n = lambda t: f"{t}_l{li}"
    g = {}
    col_own = lambda width: ((s, width), (s, LANES), lambda j: (0, j))
    d_gpre, d_pe = tile_bwd(n("ple_bwd"), _ple_fn, (s // RMS_ROWS,), ple_ops(sv["gpre"], sv["pe"], sv["x2"]), [_rows(dx3, tm=RMS_ROWS)],
                            [(0, (), None, BF16), (1, (), None, BF16)])
    g["w_ple"] = mm(n("d_w_ple"), sv["p"], d_pe, "tn", blocks=(0, N_CHIPS))
    g["w_ple_gate"] = mm(n("d_w_pg"), sv["h3"], d_gpre, "tn").reshape(N_CHIPS, -1, D_MODEL)
    dh3 = mm(n("d_h3"), d_gpre, w["pg"], "nt")
    dx2, d_g_ple = rms_bwd(n("rms_ple_bwd"), sv["x2"], w["g_ple"], dh3, dx3)
    dact = mm(n("d_act"), dx2, w["down"], "nt")
    g["w_down"] = mm(n("d_w_down"), sv["act"], dx2, "tn").reshape(N_CHIPS, -1, D_MODEL)
    taps_own = ((w["ffn_conv_w"].shape[0], D_FF), (w["ffn_conv_w"].shape[0], FFN_TILE), lambda j: (0, j))
    d_ug, d_uv, d_fw_g, d_fw_v = tile_bwd(n("ffn_act_bwd"), _ffn_act_fn, (D_FF // FFN_TILE,), ffn_ops(sv["ug"], sv["uv"], w["ffn_conv_w"]),
                                          [_col_cot(dact, FFN_TILE)], [(0, (), None, BF16), (1, (), None, BF16), (2, (), taps_own), (3, (), taps_own)])
    g["ffn_conv_w"] = jnp.concatenate([d_fw_g, d_fw_v], axis=1)
    gate_half = mm(n("d_w_up_g"), sv["h2"], d_ug, "tn", blocks=(0, 2), into=(N_CHIPS, 0))
    g["w_up"] = mm(n("d_w_up_v"), sv["h2"], d_uv, "tn", blocks=(0, 2), into=(N_CHIPS, 2, gate_half))
    dh2 = mm(n("d_h2_v"), d_uv, w["up"], "nt", blocks=(2, 2), add=mm(n("d_h2_g"), d_ug, w["up"], "nt", blocks=(0, 2)))
    dx1, d_g_ffn = rms_bwd(n("rms_ffn_bwd"), sv["x1"], w["g_ffn"], dh2, dx2)
    w = stage("mid", dx1, w)
    dmerged = mm(n("d_merged"), dx1, w["o"], "nt")
    g["w_o"] = mm(n("d_w_o"), sv["merged"], dx1, "tn").reshape(N_CHIPS, -1, D_MODEL)
    gate_own = ((s, D_MODEL), (256, D_MODEL), lambda i: (i, 0))
    d_yp0, d_yp1, d_yp2, d_g0, d_g1, d_g2 = tile_bwd(
        n("merge_bwd"), _merge_fn, (s // 256,), merge_ops(sv["yp"], sv["pm"]), [_rows(dmerged)],
        [(0, (), None, BF16), (1, (), None, BF16), (2, (), None, BF16), (3, (), gate_own, BF16), (4, (), gate_own, BF16), (5, (), gate_own, BF16)])
    d_yp = (d_yp0, d_yp1, d_yp2)
    g["w_branch"] = jnp.concatenate([mm(n(f"d_w_branch{b}"), sv["ys"][b], d_yp[b], "tn", blocks=(0, N_CHIPS)) for b in range(3)], axis=1)
    d_ys = [mm(n(f"d_y{b}"), d_yp[b], w["branch"][b], "nt", blocks=(0, N_CHIPS)) for b in range(3)]
    w = stage("late", d_ys[2], w)
    *d_local, d_z, d_dngain = dn_scan_bwd(n("dn_scan_bwd"), sv["dn_local"], sv["pm"], w["dn_gain"], sv["hist"], d_ys[2])
    d_dnact, d_ps_dn, d_arows, d_ad = dn_local_bwd(n("dn_local_bwd"), sv["dn_act"], sv["ps"], sv["a_rows"], w["ad"], d_local)
    g["ad"], g["dn_norm_gain"] = d_ad, d_dngain[0]
    d_dnqkv, g["dn_conv_w"] = tile_bwd(n("dnconv_bwd"), _dnconv_fn, (3 * BRANCH // LANES,), dnconv_ops(sv["pm"], w["dn_conv_w"]),
                                       [_col_cot(d_dnact)], [(0, (), col_own(3 * BRANCH), BF16), (1, ())])
    d_sb, d_sc, d_sv, g["sc_conv_w"] = tile_bwd(n("sconv_bwd"), _sconv_fn, (BRANCH // LANES,), sconv_ops(sv["pm"], w["sc_conv_w"]), [_col_cot(d_ys[1])],
                                                [(0, (), col_own(BRANCH), BF16), (1, (), col_own(BRANCH), BF16), (2, (), col_own(BRANCH), BF16), (3, ())])
    w = stage("last", d_dnqkv, w)
    d_qn, d_kn, d_fv, d_cum = fox_attn_bwd(n("fox_attn_bwd"), sv["qn"], sv["kn"], sv["pm"], sv["cum_c"], sv["cum_r"], d_ys[0])
    d_ft, d_bf = fox_gate_bwd(n("fox_gate_bwd"), sv["f_t"], w["b_f"], d_cum.reshape(8, s // LANES, LANES))
    g["b_fox_f"] = d_bf.reshape(8)
    d_fq, d_fk, d_gq, d_gk = fox_prep_bwd(n("fox_prep_bwd"), sv["pm"], w["gq"], w["gk"], d_qn, d_kn)
    g["fox_q_gain"] = d_gq[0, :FOX_DH] + d_gq[0, FOX_DH:]
    g["fox_k_gain"] = d_gk[0, :FOX_DH] + d_gk[0, FOX_DH:]
    d_pm = jnp.concatenate([d_fq, d_fk, d_fv.astype(BF16), d_sb, d_sc, d_sv, d_dnqkv, d_z, d_g0, d_g1, d_g2], axis=1)
    d_a_cols = d_arows.transpose(0, 2, 1).reshape(s, DN_HEADS)
    d_f_cols = d_ft.reshape(8, s).T
    d_ps = d_ps_dn + jnp.concatenate([d_f_cols, jnp.zeros((s, 4), F32), d_a_cols, jnp.zeros((s, LANES - 16), F32)], axis=1)
    g["w_in"] = chip_blocks_w_in(mm(n("d_w_in_main"), d_pm, sv["h"], "tn"), mm(n("d_w_in_small"), d_ps, sv["h"], "tn"))
    w = stage("w_in", g["w_in"], w)
    dh = mm(n("d_h_small"), d_ps, w["in_small"], "nt", add=mm(n("d_h_main"), d_pm, w["in_main"], "nt"))
    dx, d_g_mix = rms_bwd(n("rms_mix_bwd"), sv["x"], w["g_mix"], dh, dx1)
    g["g_mix"], g["g_ffn"], g["g_ple"] = d_g_mix[0], d_g_ffn[0], d_g_ple[0]
    return dx, g


IN_SHARD = 2052
MAIN_RANGES = ((0, 1536), (1544, 3080), (3080, 4616), (4624, 5136), (5136, 8208))
SMALL_RANGES = ((1536, 1544), (4616, 4620), (4620, 4624))


def _from_chip_blocks(blocks, ranges):
    parts = []
    for lo, hi in ranges:
        for k in range(N_CHIPS):
            a0, a1 = max(lo, k * IN_SHARD), min(hi, (k + 1) * IN_SHARD)
            if a0 < a1:
                parts.append(blocks[k][:, a0 - k * IN_SHARD:a1 - k * IN_SHARD])
    return parts


def split_w_in(blocks):
    main = jnp.concatenate(_from_chip_blocks(blocks, MAIN_RANGES), axis=1)
    pad = jnp.zeros((blocks.shape[1], LANES - 16), blocks.dtype)
    return main, jnp.concatenate(_from_chip_blocks(blocks, SMALL_RANGES) + [pad], axis=1)


def chip_blocks_w_in(main, small):
    ranges = sorted([(lo, hi, "m") for lo, hi in MAIN_RANGES] + [(lo, hi, "s") for lo, hi in SMALL_RANGES])
    offs, m_off, s_off = {}, 0, 0
    for lo, hi in MAIN_RANGES:
        offs[lo] = m_off
        m_off += hi - lo
    for lo, hi in SMALL_RANGES:
        offs[lo] = s_off
        s_off += hi - lo
    blocks = []
    for k in range(N_CHIPS):
        parts = []
        for lo, hi, src in ranges:
            a0, a1 = max(lo, k * IN_SHARD), min(hi, (k + 1) * IN_SHARD)
            if a0 < a1:
                arr = main if src == "m" else small
                parts.append(arr[offs[lo] + a0 - lo:offs[lo] + a1 - lo])
        blocks.append(jnp.concatenate(parts, axis=0))
    return jnp.stack(blocks)


def later_weights(got):
    g_branch, g_o, g_up, g_down, g_pg, g_ple = got
    branch = g_branch.reshape(N_CHIPS, 3, BRANCH, -1)
    return dict(branch=[branch[:, b] for b in range(3)], o=g_o.reshape(D_MODEL, D_MODEL), up=g_up,
                down=g_down.reshape(D_FF, D_MODEL), pg=g_pg.reshape(D_MODEL, D_MODEL), ple=g_ple)


def layer_weights(li, got, conv, a):
    main, small = split_w_in(got[0])
    tile2 = lambda v: jnp.concatenate([v, v])[None, :]
    rest = later_weights(got[1:]) if len(got) > 1 else {}
    return dict(
        in_main=main, in_small=small, **rest,
        g_mix=a["g_mix"][li][None, :], g_ffn=a["g_ffn"][li][None, :], g_ple=a["g_ple"][li][None, :],
        gq=tile2(a["fox_q_gain"][li]), gk=tile2(a["fox_k_gain"][li]), b_f=a["b_fox_f"][li].reshape(8, 1, 1),
        ad=jnp.stack([a["dn_a_log"][li], a["dn_dt_bias"][li]]), dn_gain=a["dn_norm_gain"][li][None, :],
        sc_conv_w=conv["sc_conv_w"][li], dn_conv_w=conv["dn_conv_w"][li], ffn_conv_w=conv["ffn_conv_w"][li])


def pack_rows(arrs, dtype):
    flat = jnp.concatenate([t.reshape(-1).astype(dtype) for t in arrs])
    pad = (-flat.shape[0]) % (8 * LANES)
    if pad:
        flat = jnp.concatenate([flat, jnp.zeros((pad,), dtype)])
    return flat.reshape(-1, LANES)


def unpack_rows(buf, shapes):
    flat = buf.reshape(-1)
    out, off = [], 0
    for shp in shapes:
        size = 1
        for dim in shp:
            size *= dim
        out.append(flat[off:off + size].reshape(shp))
        off += size
    return out


ANY = pl.BlockSpec(memory_space=pl.ANY)


def _position():
    x, y, c = lax.axis_index("x"), lax.axis_index("y"), lax.axis_index("c")
    return x, y, c, [(1 - x, y), (x, 1 - y), (1 - x, 1 - y)]


def gather_small(name, block):
    m_per, n = block.shape

    def body(x_ref, out_ref, token, send_sems, recv_sems, local_sem):
        token[...] = jnp.zeros_like(token)
        x, y, c, chips = _position()
        me, sibling = (x, y, c), (x, y, 1 - c)

        def rows(px, py, pc):
            return out_ref.at[pl.ds((4 * px + 2 * py + pc) * m_per, m_per), :]

        def copy(k, blk, to, src=None):
            return pltpu.make_async_remote_copy(src_ref=rows(*blk) if src is None else src, dst_ref=rows(*blk),
                                                send_sem=send_sems.at[k], recv_sem=recv_sems.at[k], device_id=to, device_id_type=MESH)

        mine = pltpu.make_async_copy(x_ref, rows(*me), local_sem)
        mine.start()
        first = [copy(0, me, sibling, src=x_ref)] + [copy(1 + j, me, (*chip, c), src=x_ref) for j, chip in enumerate(chips)]
        for cp in first:
            cp.start()
        passed = [copy(4 + j, (*chip, c), sibling) for j, chip in enumerate(chips)]
        for j, chip in enumerate(chips):
            copy(1 + j, (*chip, c), me).wait_recv()
            passed[j].start()
        copy(0, sibling, me).wait_recv()
        for j, chip in enumerate(chips):
            copy(4 + j, (*chip, 1 - c), me).wait_recv()
        for cp in first + passed:
            cp.wait_send()
        mine.wait()

    in_vmem = pl.BlockSpec(memory_space=pltpu.VMEM)
    return pl.pallas_call(
        body, out_shape=[jax.ShapeDtypeStruct((8 * m_per, n), block.dtype), jax.ShapeDtypeStruct((8, LANES), F32)],
        in_specs=[in_vmem], out_specs=[in_vmem, in_vmem],
        scratch_shapes=[pltpu.SemaphoreType.DMA((7,)), pltpu.SemaphoreType.DMA((7,)), pltpu.SemaphoreType.DMA],
        name=name, compiler_params=pltpu.CompilerParams(vmem_limit_bytes=VMEM_LIMIT),
    )(block)


def _sems(n):
    return [pltpu.SemaphoreType.DMA((n,)), pltpu.SemaphoreType.DMA((n,))]


def _split_cols(rows):
    return (rows // 2) % 16 != 0


def _half(ref, which, lead=()):
    rows, cols = ref.shape[-2:]
    if _split_cols(rows):
        return ref.at[(*lead, slice(None), pl.ds(which * (cols // 2), cols // 2))]
    return ref.at[(*lead, pl.ds(which * (rows // 2), rows // 2), slice(None))]


def _half_shape(rows, cols):
    return (rows, cols // 2) if _split_cols(rows) else (rows // 2, cols)


def forward_halves(name, lands):
    n_w = len(lands)

    def body(*refs):
        outs = refs[n_w:2 * n_w]
        send_sems, recv_sems = refs[2 * n_w:]
        x, y, c, chips = _position()

        def copy(w, j, pc):
            cx, cy = chips[j]
            part = _half(outs[w], pc, (2 * cx + cy,))
            return pltpu.make_async_remote_copy(src_ref=part, dst_ref=part, send_sem=send_sems.at[3 * w + j], recv_sem=recv_sems.at[3 * w + j],
                                                device_id=(x, y, 1 - c), device_id_type=MESH)

        pairs = [(w, j) for w in range(n_w) for j in range(3)]
        for w, j in pairs:
            copy(w, j, c).start()
        for w, j in pairs:
            copy(w, j, 1 - c).wait_recv()
            copy(w, j, c).wait_send()

    return pl.pallas_call(
        body, out_shape=[jax.ShapeDtypeStruct(t.shape, t.dtype) for t in lands], in_specs=[ANY] * n_w, out_specs=[ANY] * n_w,
        input_output_aliases={w: w for w in range(n_w)}, scratch_shapes=_sems(3 * n_w), name=name,
    )(*lands)


def share_halves(name, bufs):
    n_w = len(bufs)

    def body(*refs):
        outs = refs[n_w:2 * n_w]
        send_sems, recv_sems = refs[2 * n_w:]
        x, y, c, _ = _position()

        def copy(w, pc):
            half = _half(outs[w], pc)
            return pltpu.make_async_remote_copy(src_ref=half, dst_ref=half, send_sem=send_sems.at[w], recv_sem=recv_sems.at[w],
                                                device_id=(x, y, 1 - c), device_id_type=MESH)

        for w in range(n_w):
            copy(w, c).start()
        for w in range(n_w):
            copy(w, 1 - c).wait_recv()
            copy(w, c).wait_send()

    return pl.pallas_call(
        body, out_shape=[jax.ShapeDtypeStruct(b.shape, b.dtype) for b in bufs], in_specs=[ANY] * n_w, out_specs=[ANY] * n_w,
        input_output_aliases={w: w for w in range(n_w)}, scratch_shapes=_sems(n_w), name=name,
    )(*bufs)


HBM = pl.BlockSpec(memory_space=pltpu.HBM)
SEM = pl.BlockSpec(memory_space=pltpu.SEMAPHORE)
EFFECT = pltpu.SideEffectType.DATAFLOW_SIDE_EFFECTING


def _exchange_copies(kind, srcs, lands):
    x, y, c, chips = _position()
    out = []
    for src, land in zip(srcs, lands):
        if kind == "swap":
            out.append((_half(src, 1 - c, (slice(None),)), land, (x, y, 1 - c)))
            continue
        for j, (cx, cy) in enumerate(chips):
            if kind == "gather":
                out.append((src, land.at[2 * x + y], (cx, cy, c)))
            elif kind == "gather_half":
                out.append((_half(src, c), _half(land, c, (2 * x + y,)), (cx, cy, c)))
            else:
                out.append((src.at[2 * cx + cy], land.at[j], (cx, cy, c)))
    return out


def _land_shapes(kind, srcs):
    if kind in ("gather", "gather_half"):
        return [(N_CHIPS,) + s.shape for s in srcs]
    if kind == "swap":
        return [(N_CHIPS,) + _half_shape(*s.shape[1:]) for s in srcs]
    return [(3,) + s.shape[1:] for s in srcs]


def exchange_start(name, kind, srcs):
    n_w = len(srcs)
    shapes = _land_shapes(kind, srcs)
    n_sem = n_w if kind == "swap" else 3 * n_w

    def body(*refs):
        ins, lands = refs[:n_w], refs[n_w:2 * n_w]
        send_sems, recv_sems = refs[2 * n_w:2 * n_w + 2]
        token = refs[-1]
        for i, (src, dst, dev) in enumerate(_exchange_copies(kind, ins, lands)):
            pltpu.make_async_remote_copy(src_ref=src, dst_ref=dst, send_sem=send_sems.at[i], recv_sem=recv_sems.at[i],
                                         device_id=dev, device_id_type=MESH).start()
        token[...] = jnp.zeros_like(token)

    out = pl.pallas_call(
        body, name=name,
        out_shape=(pltpu.SemaphoreType.DMA((n_sem,)), pltpu.SemaphoreType.DMA((n_sem,)),
                   *[pltpu.HBM(s.shape, s.dtype) for s in srcs], *[pltpu.HBM(shp, s.dtype) for shp, s in zip(shapes, srcs)],
                   jax.ShapeDtypeStruct((8, LANES), F32)),
        in_specs=(HBM,) * (2 * n_w), out_specs=(SEM, SEM) + (HBM,) * (2 * n_w) + (pl.BlockSpec(memory_space=pltpu.VMEM),),
        input_output_aliases={i: 2 + i for i in range(2 * n_w)},
        compiler_params=pltpu.CompilerParams(has_side_effects=EFFECT),
    )(*[pltpu.with_memory_space_constraint(s, pltpu.HBM) for s in srcs],
      *[pltpu.with_memory_space_constraint(lax.empty(shp, s.dtype), pltpu.HBM) for shp, s in zip(shapes, srcs)])
    return (kind, n_w, out[:-1]), out[-1]


def exchange_wait(name, handle, after):
    kind, n_w, (send_sems, recv_sems, *thru) = handle

    def body(*refs):
        ins, lands = refs[:n_w], refs[n_w:2 * n_w]
        send_sems, recv_sems = refs[2 * n_w:2 * n_w + 2]
        for i, (src, dst, dev) in enumerate(_exchange_copies(kind, ins, lands)):
            cp = pltpu.make_async_remote_copy(src_ref=src, dst_ref=dst, send_sem=send_sems.at[i], recv_sem=recv_sems.at[i],
                                              device_id=dev, device_id_type=MESH)
            cp.wait_send()
            cp.wait_recv()

    out = pl.pallas_call(
        body, name=name, out_shape=tuple(pltpu.HBM(t.shape, t.dtype) for t in thru),
        in_specs=(HBM,) * (2 * n_w) + (SEM, SEM, pl.BlockSpec(memory_space=pl.ANY)), out_specs=(HBM,) * (2 * n_w),
        input_output_aliases={i: i for i in range(2 * n_w)},
        compiler_params=pltpu.CompilerParams(has_side_effects=EFFECT),
    )(*thru, send_sems, recv_sems, after)
    return list(out[:n_w]), list(out[n_w:])


def _row_tile(rows, cols):
    best = rows
    if rows * cols * 4 <= 2 * 1024 * 1024:
        return rows
    for t in range(16, rows, 16):
        if rows % t == 0 and t * cols * 4 <= 2 * 1024 * 1024:
            best = t
    return best


def pair_sum(name, pos, grad, from_sibling):
    _, rows, cols = grad.shape
    h_rows, h_cols = _half_shape(rows, cols)
    tr = _row_tile(h_rows, h_cols)
    n_t = h_rows // tr

    def body(pos_ref, g_ref, s_ref, b_ref, f_ref):
        tot = g_ref[...] + s_ref[...]
        b_ref[...] = tot.astype(BF16)

        @pl.when(pl.program_id(1) == pos_ref[1])
        def _():
            f_ref[...] = tot[0]

    blk = pl.BlockSpec((1, tr, h_cols), lambda i, k, pos: (k, i, 0))
    if _split_cols(rows):
        mine = pl.BlockSpec((1, tr, h_cols), lambda i, k, pos: (k, i, pos[0]))
    else:
        mine = pl.BlockSpec((1, tr, h_cols), lambda i, k, pos: (k, pos[0] * n_t + i, 0))
    return pl.pallas_call(
        body, grid_spec=pltpu.PrefetchScalarGridSpec(
            num_scalar_prefetch=1, grid=(n_t, N_CHIPS), in_specs=[mine, blk],
            out_specs=[blk, pl.BlockSpec((tr, h_cols), lambda i, k, pos: (i, 0))]),
        out_shape=[jax.ShapeDtypeStruct((N_CHIPS, h_rows, h_cols), BF16), jax.ShapeDtypeStruct((h_rows, h_cols), F32)],
        name=name, compiler_params=_cparams(2),
    )(pos, grad, from_sibling)


def chip_sum(name, pos, own, landed, split_cols):
    half, cols = own.shape
    tr = _row_tile(half, cols)
    n_t = half // tr

    def body(pos_ref, p_ref, l_ref, o_ref):
        o_ref[...] = ((p_ref[...] + l_ref[0].astype(F32)) + l_ref[1].astype(F32)) + l_ref[2].astype(F32)

    if split_cols:
        out_spec, out_shape = pl.BlockSpec((tr, cols), lambda i, pos: (i, pos[0])), (half, 2 * cols)
    else:
        out_spec, out_shape = pl.BlockSpec((tr, cols), lambda i, pos: (pos[0] * n_t + i, 0)), (2 * half, cols)
    return pl.pallas_call(
        body, grid_spec=pltpu.PrefetchScalarGridSpec(
            num_scalar_prefetch=1, grid=(n_t,),
            in_specs=[pl.BlockSpec((tr, cols), lambda i, pos: (i, 0)), pl.BlockSpec((3, tr, cols), lambda i, pos: (0, i, 0))],
            out_specs=out_spec),
        out_shape=jax.ShapeDtypeStruct(out_shape, F32), name=name, compiler_params=_cparams(1),
    )(pos, own, landed)


class OverlappedReduceScatter:
    def __init__(self, tag, pos, grads):
        self.n = lambda t: f"{t}_{tag}"
        self.pos, self.grads = pos, grads
        self.swap, self.token = exchange_start(self.n("swap_start"), "swap", grads)

    def middle(self, after):
        self.grads, from_sibling = exchange_wait(self.n("swap_wait"), self.swap, after)
        self.sums = [pair_sum(self.n(f"pair_sum{w}"), self.pos, g, s) for w, (g, s) in enumerate(zip(self.grads, from_sibling))]
        self.scatter, self.token = exchange_start(self.n("scatter_start"), "scatter", [b for b, _ in self.sums])

    def finish(self, after):
        _, landed = exchange_wait(self.n("scatter_wait"), self.scatter, after)
        halves = [chip_sum(self.n(f"chip_sum{w}"), self.pos, own, l, _split_cols(g.shape[1]))
                  for w, ((_, own), l, g) in enumerate(zip(self.sums, landed, self.grads))]
        return share_halves(self.n("share_halves"), halves)


def sum_devices(gathered):
    m_per = gathered.shape[0] // 8

    def body(g_ref, o_ref):
        tot = g_ref[pl.ds(0, m_per), :]
        for dev in range(1, 8):
            tot = tot + g_ref[pl.ds(dev * m_per, m_per), :]
        o_ref[...] = tot

    return pl.pallas_call(
        body, out_shape=jax.ShapeDtypeStruct((m_per, gathered.shape[1]), F32),
        in_specs=[pl.BlockSpec(memory_space=pltpu.VMEM)], out_specs=pl.BlockSpec(memory_space=pltpu.VMEM), name="sum_devices",
    )(gathered)


def kernel(x, p, g_mix, w_in, b_fox_f, fox_q_gain, fox_k_gain, sc_conv_w, dn_conv_w, dn_a_log, dn_dt_bias, dn_norm_gain, w_branch, w_o, g_ffn, w_up, ffn_conv_w, w_down, g_ple, w_ple_gate, w_ple, loss_target, m_g_mix, m_w_in, m_b_fox_f, m_fox_q_gain, m_fox_k_gain, m_sc_conv_w, m_dn_conv_w, m_dn_a_log, m_dn_dt_bias, m_dn_norm_gain, m_w_branch, m_w_o, m_g_ffn, m_w_up, m_ffn_conv_w, m_w_down, m_g_ple, m_w_ple_gate, m_w_ple, v_g_mix, v_w_in, v_b_fox_f, v_fox_q_gain, v_fox_k_gain, v_sc_conv_w, v_dn_conv_w, v_dn_a_log, v_dn_dt_bias, v_dn_norm_gain, v_w_branch, v_w_o, v_g_ffn, v_w_up, v_ffn_conv_w, v_w_down, v_g_ple, v_w_ple_gate, v_w_ple):
    a = dict(g_mix=g_mix, w_in=w_in, b_fox_f=b_fox_f, fox_q_gain=fox_q_gain, fox_k_gain=fox_k_gain, sc_conv_w=sc_conv_w,
             dn_conv_w=dn_conv_w, dn_a_log=dn_a_log, dn_dt_bias=dn_dt_bias, dn_norm_gain=dn_norm_gain, w_branch=w_branch, w_o=w_o,
             g_ffn=g_ffn, w_up=w_up, ffn_conv_w=ffn_conv_w, w_down=w_down, g_ple=g_ple, w_ple_gate=w_ple_gate, w_ple=w_ple)
    mom = dict(g_mix=m_g_mix, w_in=m_w_in, b_fox_f=m_b_fox_f, fox_q_gain=m_fox_q_gain, fox_k_gain=m_fox_k_gain, sc_conv_w=m_sc_conv_w,
               dn_conv_w=m_dn_conv_w, dn_a_log=m_dn_a_log, dn_dt_bias=m_dn_dt_bias, dn_norm_gain=m_dn_norm_gain, w_branch=m_w_branch,
               w_o=m_w_o, g_ffn=m_g_ffn, w_up=m_w_up, ffn_conv_w=m_ffn_conv_w, w_down=m_w_down, g_ple=m_g_ple, w_ple_gate=m_w_ple_gate,
               w_ple=m_w_ple)
    var = dict(g_mix=v_g_mix, w_in=v_w_in, b_fox_f=v_b_fox_f, fox_q_gain=v_fox_q_gain, fox_k_gain=v_fox_k_gain, sc_conv_w=v_sc_conv_w,
               dn_conv_w=v_dn_conv_w, dn_a_log=v_dn_a_log, dn_dt_bias=v_dn_dt_bias, dn_norm_gain=v_dn_norm_gain, w_branch=v_w_branch,
               w_o=v_w_o, g_ffn=v_g_ffn, w_up=v_w_up, ffn_conv_w=v_ffn_conv_w, w_down=v_w_down, g_ple=v_g_ple, w_ple_gate=v_w_ple_gate,
               w_ple=v_w_ple)
    cx, cy, cc = lax.axis_index("x"), lax.axis_index("y"), lax.axis_index("c")
    chip = 2 * cx + cy
    pos = jnp.stack([cc, chip]).astype(jnp.int32)

    def as_blocks(t):
        return t.reshape(2, -1, t.shape[-1])

    def own_block_in(got, shards):
        return [lax.dynamic_update_slice(g, s[None], (chip, 0, 0)) for g, s in zip(got, shards)]

    conv_shapes = [a[nm].shape for nm in CONVS]
    conv_all, conv_token = gather_small("gather_conv_w", pack_rows([a[nm] for nm in CONVS], F32))
    def w_in_block(li, token):
        stored = jnp.transpose(a["w_in"], (2, 0, 1))[:, li, :]
        return (stored + token[0, 0]).astype(BF16).T

    w_in0 = [w_in_block(0, conv_token)]
    gather_in0, gather_in0_token = exchange_start("gather_start_w_in_l0", "gather_half", w_in0)
    shards0 = w_in0 + [(as_blocks(a[nm])[0] + gather_in0_token[0, 0]).astype(BF16) for nm in BIG[1:]]
    gather0, gather0_token = exchange_start("gather_start_l0", "gather", shards0[1:])
    shards1 = [w_in_block(1, gather0_token)] + [(as_blocks(a[nm])[1] + gather0_token[0, 0]).astype(BF16) for nm in BIG[1:]]
    gather1, gather1_in_token = exchange_start("gather_start_w_in_l1", "gather", shards1[:1])
    shards1[1:] = [s + gather1_in_token[0, 0].astype(BF16) for s in shards1[1:]]
    gather1_rest, gather1_token = exchange_start("gather_start_l1", "gather", shards1[1:])
    conv_rows = conv_all.shape[0] // 8
    conv_chip = [unpack_rows(conv_all[2 * k * conv_rows:(2 * k + 1) * conv_rows], conv_shapes) for k in range(N_CHIPS)]
    conv = {nm: jnp.concatenate([conv_chip[k][i] for k in range(N_CHIPS)], axis=2) for i, nm in enumerate(CONVS)}

    weights, saved = [None, None], [None, None]
    mine_in0, got_in0 = exchange_wait("gather_wait_w_in_l0", gather_in0, gather1_token)
    got_in0 = forward_halves("forward_w_in_l0", got_in0)
    first_weights = hang_on(layer_weights(0, own_block_in(got_in0, mine_in0), conv, a), gather1_token)

    def rest_of_layer0(after):
        mine, got = exchange_wait("gather_wait_l0", gather0, after)
        return later_weights(own_block_in(got, mine))

    act, saved[0], weights[0] = layer_fwd(0, x[0], p[0, 0], first_weights, more_weights=rest_of_layer0)
    mine1, got1 = exchange_wait("gather_wait_w_in_l1", gather1, act)

    def rest_of_layer1(after):
        mine, got = exchange_wait("gather_wait_l1", gather1_rest, after)
        return later_weights(own_block_in(got, mine))

    act, saved[1], weights[1] = layer_fwd(1, act, p[1, 0], layer_weights(1, own_block_in(got1, mine1), conv, a),
                                          more_weights=rest_of_layer1)
    d_act, loss_part = loss_call(act, loss_target[0])
    loss = lax.psum(loss_part, ("x", "y", "c"))
    layer_grads = [None, None]
    d_act, layer_grads[1] = layer_bwd(1, d_act, saved[1], weights[1])
    rs1 = OverlappedReduceScatter("l1", pos, [layer_grads[1][nm] for nm in BIG])
    rs0 = []

    def stage_mid(after, g):
        rs1.middle(after)
        return rs1.token

    def stage_late(after, g):
        rs0.append(OverlappedReduceScatter("l0", pos, [g[nm] for nm in BIG[1:]]))
        return rs0[0].token

    def stage_last(after, g):
        rs0[0].middle(after)
        return rs0[0].token

    def stage_w_in(after, g):
        rs0.append(OverlappedReduceScatter("w_in_l0", pos, [g["w_in"]]))
        return rs0[1].token

    d_act, layer_grads[0] = layer_bwd(0, d_act, saved[0], hang_on(weights[0], rs1.token),
                                      hooks=dict(mid=stage_mid, late=stage_late, last=stage_last, w_in=stage_w_in))
    rs0[1].middle(d_act)
    reduced = [rs0[0].finish(rs0[1].token), rs1.finish(rs0[1].token)]
    grad_x = d_act[None]

    def both(nm):
        return jnp.stack([layer_grads[0][nm], layer_grads[1][nm]])

    local = {nm: both(nm) for nm in ("g_mix", "b_fox_f", "fox_q_gain", "fox_k_gain", "dn_norm_gain", "g_ffn", "g_ple", "sc_conv_w",
                                      "dn_conv_w", "ffn_conv_w")}
    local["dn_a_log"] = jnp.stack([layer_grads[li]["ad"][0] for li in range(2)])
    local["dn_dt_bias"] = jnp.stack([layer_grads[li]["ad"][1] for li in range(2)])

    small_names = SMALL + CONVS
    small_shapes = [local[nm].shape for nm in small_names]
    small_sum = sum_devices(gather_small("gather_small_grads", pack_rows([local[nm] for nm in small_names], F32))[0])
    small_grads = dict(zip(small_names, unpack_rows(small_sum, small_shapes)))
    for nm in CONVS:
        width = a[nm].shape[2]
        small_grads[nm] = lax.dynamic_slice_in_dim(small_grads[nm], chip * width, width, axis=2)

    grads, deltas, new_m, new_v = dict(small_grads), {}, {}, {}
    for nm in small_names:
        deltas[nm], new_m[nm], new_v[nm] = adam_call(f"adam_{nm}", a[nm], grads[nm], mom[nm], var[nm])
    for i, nm in enumerate(BIG[1:]):
        res = adam_layers(f"adam_{nm}", as_blocks(a[nm]), as_blocks(mom[nm]), as_blocks(var[nm]), reduced[0][i], reduced[1][1 + i])
        grads[nm], deltas[nm], new_m[nm], new_v[nm] = [r.reshape(a[nm].shape) for r in res]
    stored = lambda t: jnp.transpose(t, (2, 0, 1))
    res = adam_w_in("adam_w_in", stored(a["w_in"]), stored(mom["w_in"]), stored(var["w_in"]), rs0[1].finish(deltas["w_ple"])[0], reduced[1][0])
    grads["w_in"], deltas["w_in"], new_m["w_in"], new_v["w_in"] = [jnp.transpose(r, (1, 2, 0)) for r in res]
    return (loss, grad_x, *[grads[nm] for nm in WEIGHTS], *[deltas[nm] for nm in WEIGHTS], *[new_m[nm] for nm in WEIGHTS],
            *[new_v[nm] for nm in WEIGHTS])
```

```python
import functools

import jax
import jax.numpy as jnp
from jax import lax
from jax.experimental import pallas as pl
from jax.experimental.pallas import tpu as pltpu

F32 = jnp.float32
BF16 = jnp.bfloat16
HI = lax.Precision.HIGHEST
SOLVE = lax.Precision.HIGH
MESH = pl.DeviceIdType.MESH

D_MODEL = 1024
BRANCH = 512
FOX_DH = 64
DN_DH = 128
DN_HEADS = 4
DN_CHUNK = 64
FOX_BLOCK = 128
D_FF = 2816
EPS = 1e-6
N_CHIPS = 4
LANES = 128

ADAM_LR, ADAM_B1, ADAM_B2, ADAM_EPS, ADAM_WD, ADAM_STEP = 0.001, 0.9, 0.999, 1e-08, 0.01, 10

VMEM_LIMIT = 56 * 1024 * 1024

C_FQ, C_FK, C_FV, C_SB, C_SC, C_SV, C_DN, C_DZ, C_GATE = 0, 512, 1024, 1536, 2048, 2560, 3072, 4608, 5120
IN_MAIN = 8192

BIG = ("w_in", "w_branch", "w_o", "w_up", "w_down", "w_ple_gate", "w_ple")
CONVS = ("sc_conv_w", "dn_conv_w", "ffn_conv_w")
SMALL = ("g_mix", "b_fox_f", "fox_q_gain", "fox_k_gain", "dn_a_log", "dn_dt_bias", "dn_norm_gain", "g_ffn", "g_ple")
WEIGHTS = ("g_mix", "w_in", "b_fox_f", "fox_q_gain", "fox_k_gain", "sc_conv_w", "dn_conv_w", "dn_a_log", "dn_dt_bias",
           "dn_norm_gain", "w_branch", "w_o", "g_ffn", "w_up", "ffn_conv_w", "w_down", "g_ple", "w_ple_gate", "w_ple")


def _iota(shape, dim):
    return lax.broadcasted_iota(jnp.int32, shape, dim)


def _dg(a, b, mode, prec=None):
    dims = {"nn": ((1,), (0,)), "nt": ((1,), (1,)), "tn": ((0,), (0,))}[mode]
    return lax.dot_general(a, b, (dims, ((), ())), precision=prec, preferred_element_type=F32)


def _bdot_impl(a, b, mode):
    return _dg(a.astype(BF16), b.astype(BF16), mode)


@functools.partial(jax.custom_vjp, nondiff_argnums=(2,))
def _bdot_diff(a, b, mode):
    return _bdot_impl(a, b, mode)


def _bdot_fwd(a, b, mode):
    return _bdot_impl(a, b, mode), (a, b)


def _bdot_bwd(mode, res, g):
    a, b = res
    if mode == "nn":
        da, db = _bdot_impl(g, b, "nt"), _bdot_impl(a, g, "tn")
    elif mode == "nt":
        da, db = _bdot_impl(g, b, "nn"), _bdot_impl(g, a, "tn")
    else:
        da, db = _bdot_impl(b, g, "nt"), _bdot_impl(a, g, "nn")
    return da.astype(a.dtype), db.astype(b.dtype)


_bdot_diff.defvjp(_bdot_fwd, _bdot_bwd)


def _bdot(d):
    return _bdot_diff if d else _bdot_impl


def _shift_impl(x, k):
    return jnp.where(_iota(x.shape, 0) >= k, pltpu.roll(x, k, 0), 0.0)


def _unshift_impl(g, k):
    n = g.shape[0]
    return jnp.where(_iota(g.shape, 0) < n - k, pltpu.roll(g, n - k, 0), 0.0)


@functools.partial(jax.custom_vjp, nondiff_argnums=(1,))
def _shift_diff(x, k):
    return _shift_impl(x, k)


_shift_diff.defvjp(lambda x, k: (_shift_impl(x, k), None), lambda k, _, g: (_unshift_impl(g, k),))


def _row(w, j):
    return jnp.sum(jnp.where(_iota(w.shape, 0) == j, w, 0.0), axis=0, keepdims=True)


def _col(w, j):
    return jnp.sum(jnp.where(_iota(w.shape, 1) == j, w, 0.0), axis=1, keepdims=True)


def _conv(d, x, w):
    shift = _shift_diff if d else _shift_impl
    taps = w.shape[0]
    y = x * _row(w, taps - 1)
    for j in range(taps - 1):
        y = y + shift(x, taps - 1 - j) * _row(w, j)
    return y


def _softplus(x):
    return jnp.maximum(x, 0.0) + jnp.log(1.0 + jnp.exp(-jnp.abs(x)))


def _sigmoid(x):
    return 0.5 * (jnp.tanh(0.5 * x) + 1.0)


def _silu(x):
    return x * _sigmoid(x)


def _rms(x, gain):
    return x * lax.rsqrt(jnp.mean(x * x, axis=-1, keepdims=True) + EPS) * gain


def _rms_fn(d, pids, x, gain):
    return (_rms(x, gain),)


def _loss_fn(d, pids, y, t):
    e = y - t
    part = 0.5 / D_MODEL * jnp.sum(e * e, keepdims=True)
    return e * (1.0 / D_MODEL), jnp.broadcast_to(part, (8, LANES))


def _fox_prep_fn(d, pids, q, k, gq, gk):
    first = _iota(q.shape, 1) < FOX_DH

    def norm(x, gain):
        sq = x * x
        ss_a = jnp.sum(jnp.where(first, sq, 0.0), axis=1, keepdims=True)
        ss_b = jnp.sum(jnp.where(first, 0.0, sq), axis=1, keepdims=True)
        rs = jnp.where(first, lax.rsqrt(ss_a / FOX_DH + EPS), lax.rsqrt(ss_b / FOX_DH + EPS))
        return x * rs * gain

    return norm(q, gq) * FOX_DH ** -0.5, norm(k, gk)


def _fox_gate_fn(d, pids, f, bias):
    logf = -_softplus(-(f + bias))
    n_r, n_c = logf.shape
    tri = (_iota((n_c, n_c), 0) <= _iota((n_c, n_c), 1)).astype(F32)
    within = _dg(logf, tri, "nn", HI)
    tot = jnp.broadcast_to(jnp.sum(logf, axis=1, keepdims=True), logf.shape)
    below = (_iota((n_r, n_r), 1) < _iota((n_r, n_r), 0)).astype(F32)
    return (within + _dg(below, tot, "nn", HI),)


def _fox_attn_fn(q_block0, d, pids, q, k, v, cq_a, cq_b, ck_a, ck_b):
    dot = _bdot(d)
    first = _iota(q.shape, 1) < FOX_DH
    n_q, n_k = q.shape[0], k.shape[0]
    causal = ((q_block0 + pids[1]) * n_q + _iota((n_q, n_k), 0)) >= _iota((n_q, n_k), 1)

    qs = [jnp.where(first, q, 0.0), jnp.where(first, 0.0, q)]
    s = _each(lambda qh, cq, ck: jnp.where(causal, dot(qh, k, "nt") + cq - ck, -1e30), qs, [cq_a, cq_b], [ck_a, ck_b])
    e = [jnp.exp(si - lax.stop_gradient(jnp.max(si, axis=1, keepdims=True))) for si in s]
    o_a, o_b = [dot(ei * (1.0 / jnp.sum(ei, axis=1, keepdims=True)), v, "nn") for ei in e]
    return (jnp.where(first, o_a, o_b),)


def _sconv_fn(d, pids, sb, sc, sv, w):
    return (sb * _conv(d, sc * sv, w),)


def _dnconv_fn(d, pids, x, w):
    return (_silu(_conv(d, x, w)),)


def _merge_fn(d, pids, y0, y1, y2, g0, g1, g2):
    return (_sigmoid(g0) * y0 + _sigmoid(g1) * y1 + _sigmoid(g2) * y2,)


def _ffn_act_fn(d, pids, ug, uv, wg, wv):
    return (_silu(_conv(d, ug, wg)) * _conv(d, uv, wv),)


def _ple_fn(d, pids, gpre, pe, x):
    return (x + _sigmoid(gpre) * pe,)


def _adam_fn(d, pids, w, g, m, v):
    m2 = ADAM_B1 * m + (1.0 - ADAM_B1) * g
    v2 = ADAM_B2 * v + (1.0 - ADAM_B2) * (g * g)
    m_hat = m2 / (1.0 - ADAM_B1 ** ADAM_STEP)
    v_hat = v2 / (1.0 - ADAM_B2 ** ADAM_STEP)
    delta = -ADAM_LR * (m_hat / (jnp.sqrt(v_hat) + ADAM_EPS) + ADAM_WD * w)
    return delta, m2, v2


def _each(fn, *lists):
    return [fn(*args) for args in zip(*lists)]


def _tri_inv_impl(mats):
    n = mats[0].shape[0]
    r, c = _iota((n, n), 0), _iota((n, n), 1)
    diag_blk = (r >> 4) == (c >> 4)
    eye = (r == c).astype(F32)
    mm = lambda us, ws: _each(lambda u, w: _dg(u, w, "nn", SOLVE), us, ws)
    grow = lambda ps, xs: _each(lambda p, px: p + px, ps, mm(ps, xs))
    x = [jnp.where(diag_blk, -a, 0.0) for a in mats]
    p = [eye + xi for xi in x]
    x2 = mm(x, x)
    p = grow(p, x2)
    x4 = mm(x2, x2)
    p = grow(p, x4)
    p = grow(p, mm(x4, x4))
    y = [-yi for yi in mm(p, [jnp.where(diag_blk, 0.0, a) for a in mats])]
    q = grow([eye + yi for yi in y], mm(y, y))
    return mm(q, p)


@jax.custom_vjp
def _tri_inv_diff(mats):
    return _tri_inv_impl(mats)


def _tri_inv_fwd(mats):
    ts = _tri_inv_impl(mats)
    return ts, ts


def _tri_inv_bwd(ts, gs):
    left = _each(lambda t, g: _dg(t, g, "tn", SOLVE), ts, gs)
    return ([-m for m in _each(lambda l, t: _dg(l, t, "nt", SOLVE), left, ts)],)


_tri_inv_diff.defvjp(_tri_inv_fwd, _tri_inv_bwd)


def _dn_local(d, qs, ks, vs, a_cs, a_rs, b_cs, a_logs, dt_bs):
    dot = _bdot(d)
    inv = _tri_inv_diff if d else _tri_inv_impl
    n = qs[0].shape[0]
    r, c = _iota((n, n), 0), _iota((n, n), 1)
    incl, strict, upper = r >= c, r > c, r <= c
    qs = [q * lax.rsqrt(jnp.sum(q * q, axis=1, keepdims=True) + EPS) * DN_DH ** -0.5 for q in qs]
    ks = [k * lax.rsqrt(jnp.sum(k * k, axis=1, keepdims=True) + EPS) for k in ks]
    betas = [_sigmoid(b) for b in b_cs]
    rates = [-jnp.exp(a) for a in a_logs]
    g_cs = _each(lambda rate, a, dt: rate * _softplus(a + dt), rates, a_cs, dt_bs)
    g_rs = _each(lambda rate, a, dt: rate * _softplus(a + dt), rates, a_rs, dt_bs)
    gcum_cs = [jnp.sum(jnp.where(incl, g, 0.0), axis=1, keepdims=True) for g in g_rs]
    gcum_rs = [jnp.sum(jnp.where(upper, g, 0.0), axis=0, keepdims=True) for g in g_cs]
    decays = _each(lambda gc, gr: jnp.exp(jnp.where(incl, gc - gr, -1e30)), gcum_cs, gcum_rs)
    kbs = _each(lambda k, b: k * b, ks, betas)
    kk = _each(lambda kb, k: dot(kb, k, "nt"), kbs, ks)
    ts = inv(_each(lambda m, dec: jnp.where(strict, m * dec, 0.0), kk, decays))
    e_gs = [jnp.exp(g) for g in gcum_cs]
    us = _each(lambda t, v, b: _dg(t, v * b, "nn", SOLVE), ts, vs, betas)
    k_cums = _each(lambda t, kb, e: _dg(t, kb * e, "nn", SOLVE), ts, kbs, e_gs)
    qk = _each(lambda q, k: dot(q, k, "nt"), qs, ks)
    qk = _each(lambda m, dec: jnp.where(incl, m * dec, 0.0), qk, decays)
    g_lasts = [jnp.sum(g, axis=0, keepdims=True) for g in g_cs]
    q_decs = _each(lambda q, e: q * e, qs, e_gs)
    k_decs = _each(lambda k, gl, gc: k * jnp.exp(gl - gc), ks, g_lasts, gcum_cs)
    return list(zip(us, k_cums, q_decs, k_decs, qk, g_lasts))


def _dn_step(d, s_prevs, items, zs, gain):
    dot = _bdot(d)
    us, k_cums, q_decs, k_decs, qks, g_lasts = [list(t) for t in zip(*items)]
    v_news = _each(lambda u, kc, s: u - dot(kc, s, "nn"), us, k_cums, s_prevs)
    inter = _each(lambda qd, s: dot(qd, s, "nn"), q_decs, s_prevs)
    outs = _each(lambda o, qk, vn: o + dot(qk, vn, "nn"), inter, qks, v_news)
    s_nexts = _each(lambda s, gl, kd, vn: s * jnp.exp(gl) + dot(kd, vn, "tn"), s_prevs, g_lasts, k_decs, v_news)
    return _each(lambda o, z: _rms(o, gain) * _silu(z), outs, zs), s_nexts


def _split_heads(t):
    return [t[:, h * DN_DH:(h + 1) * DN_DH] for h in range(t.shape[1] // DN_DH)]


def _dn_gates(ps, a_rows, ad):
    hs = range(DN_HEADS)
    return ([_col(ps, 12 + h) for h in hs], [_row(a_rows, h) for h in hs], [_col(ps, 8 + h) for h in hs],
            [_col(_row(ad, 0), h) for h in hs], [_col(_row(ad, 1), h) for h in hs])


def _head_rows(vals):
    row = _iota((8, LANES), 0)
    tile = jnp.zeros((8, LANES), F32)
    for h, val in enumerate(vals):
        tile = tile + jnp.where(row == h, val, 0.0)
    return tile


def _cparams(n_axes):
    return pltpu.CompilerParams(dimension_semantics=("arbitrary",) * n_axes, vmem_limit_bytes=VMEM_LIMIT)


def _first_visit(acc_axes):
    cond = None
    for a in acc_axes:
        here = pl.program_id(a) == 0
        cond = here if cond is None else jnp.logical_and(cond, here)
    return cond


def _tile(ref, widen=False):
    val = ref[...]
    shape = val.shape
    while len(shape) > 2 and shape[0] == 1:
        shape = shape[1:]
    val = val.reshape(shape)
    return val.astype(F32) if widen and val.dtype == BF16 else val


def _store(ref, val, first):
    val = val.astype(ref.dtype).reshape(ref.shape)
    if first is None:
        ref[...] = val
        return

    @pl.when(first)
    def _():
        ref[...] = val

    @pl.when(jnp.logical_not(first))
    def _():
        ref[...] += val


def _specs(ops):
    return [pl.BlockSpec(block, imap) for _, block, imap in ops]


def tile_fwd(name, fn, grid, ins, outs, raw=()):
    n_in = len(ins)

    def body(*refs):
        pids = tuple(pl.program_id(a) for a in range(len(grid)))
        firsts = [_first_visit(o[4]) if o[4] else None for o in outs]
        res = fn(False, pids, *[_tile(r, i not in raw) for i, r in enumerate(refs[:n_in])])
        for ref, val, first in zip(refs[n_in:], res, firsts):
            _store(ref, val, first)

    out = pl.pallas_call(
        body, grid=grid, in_specs=_specs(ins),
        out_specs=[pl.BlockSpec(o[2], o[3]) for o in outs],
        out_shape=[jax.ShapeDtypeStruct(o[0], o[1]) for o in outs],
        name=name, compiler_params=_cparams(len(grid)),
    )(*[a for a, _, _ in ins])
    return out


def tile_bwd(name, fn, grid, ins, cots, diff, adds=None, raw=()):
    adds = adds or {}
    n_in, n_cot = len(ins), len(cots)
    add_pos = sorted(adds)
    diff_idx = [d[0] for d in diff]
    out_desc = [d[2] if len(d) > 2 and d[2] is not None else (ins[d[0]][0].shape, ins[d[0]][1], ins[d[0]][2]) for d in diff]
    out_dtypes = [d[3] if len(d) > 3 else F32 for d in diff]

    def body(*refs):
        pids = tuple(pl.program_id(a) for a in range(len(grid)))
        firsts = [_first_visit(d[1]) if d[1] else None for d in diff]
        vals = [_tile(r, i not in raw) for i, r in enumerate(refs[:n_in])]
        cot_vals = [_tile(r, True) for r in refs[n_in:n_in + n_cot]]
        add_vals = [_tile(r) for r in refs[n_in + n_cot:n_in + n_cot + len(add_pos)]]
        out_refs = refs[n_in + n_cot + len(add_pos):]

        def f(*dv):
            full = list(vals)
            for i, val in zip(diff_idx, dv):
                full[i] = val
            return fn(True, pids, *full)

        prim, vjp = jax.vjp(f, *[vals[i].astype(F32) for i in diff_idx])
        grads = list(vjp(tuple(c.astype(o.dtype) for c, o in zip(cot_vals, prim))))
        for pos, val in zip(add_pos, add_vals):
            extra = val.astype(F32) if firsts[pos] is None else jnp.where(firsts[pos], val.astype(F32), 0.0)
            grads[pos] = grads[pos] + extra
        for ref, val, first in zip(out_refs, grads, firsts):
            _store(ref, val, first)

    all_ins = list(ins) + list(cots) + [adds[p] for p in add_pos]
    out = pl.pallas_call(
        body, grid=grid, in_specs=_specs(all_ins),
        out_specs=[pl.BlockSpec(o[1], o[2]) for o in out_desc],
        out_shape=[jax.ShapeDtypeStruct(o[0], dt) for o, dt in zip(out_desc, out_dtypes)],
        name=name, compiler_params=_cparams(len(grid)),
    )(*[a for a, _, _ in all_ins])
    return out


def _pick(dim, cands):
    for c in cands:
        if dim % c == 0:
            return c
    return dim


MM_VMEM_BUDGET = 40 * 1024 * 1024
MM_TILES = (1024, 512, 1408, 256, 128)


def mm(name, a, b, mode, add=None, out_dtype=F32, blocks=None, into=None):
    wide = None
    if mode == "nn":
        (m, kk), n = a.shape, b.shape[-1]
    elif mode == "nt":
        (m, kk), n = a.shape, b.shape[-2]
    else:
        (kk, m), n = a.shape, b.shape[1]
    if blocks is not None:
        lo, n_blk = blocks
        wide = b.shape[-1] if mode != "tn" else n // n_blk
        if mode == "nn":
            n = wide * n_blk
    tm = _pick(m, MM_TILES)
    if mode == "nt" and blocks is not None:
        tn, tk = _pick(n, MM_TILES), _pick(wide, MM_TILES[:-1])
    elif blocks is not None:
        tn, tk = _pick(wide, MM_TILES[:-1]), _pick(kk, MM_TILES)
    else:
        tn, tk = _pick(n, MM_TILES), _pick(kk, MM_TILES)
    if mode == "tn" or blocks is None:
        tk = _pick(kk, (2048,) + MM_TILES)
    if mode != "tn" and add is None and m % 2048 == 0 and (n // tn) * (kk // tk) > 1:
        windows = 2 * (2048 * tk * a.dtype.itemsize + tk * tn * b.dtype.itemsize + 2048 * tn * jnp.dtype(out_dtype).itemsize)
        if windows + 2048 * tn * 4 <= MM_VMEM_BUDGET:
            tm = 2048
    nk = kk // tk
    a_spec = pl.BlockSpec((tk, tm), lambda i, j, k: (k, i)) if mode == "tn" else pl.BlockSpec((tm, tk), lambda i, j, k: (i, k))
    o_spec = pl.BlockSpec((tm, tn), lambda i, j, k: (i, j))
    out_shape = (m, n)
    if blocks is None:
        b_spec = pl.BlockSpec((tn, tk), lambda i, j, k: (j, k)) if mode == "nt" else pl.BlockSpec((tk, tn), lambda i, j, k: (k, j))
    elif mode == "nn":
        per = wide // tn
        b_spec = pl.BlockSpec((1, tk, tn), lambda i, j, k: (lo + j // per, k, j % per))
    elif mode == "nt":
        per = wide // tk
        b_spec = pl.BlockSpec((1, tn, tk), lambda i, j, k: (lo + k // per, j, k % per))
    else:
        per = wide // tn
        total, first = (into[0], into[1]) if into is not None else (n_blk, 0)
        b_spec = pl.BlockSpec((tk, tn), lambda i, j, k: (k, j))
        o_spec = pl.BlockSpec((1, tm, tn), lambda i, j, k: (first + j // per, i, j % per))
        out_shape = (total, m, wide)

    def body(*refs):
        a_ref, b_ref = refs[0], refs[1]
        add_ref = refs[2] if add is not None else None
        o_ref, acc = refs[-2], refs[-1]
        k = pl.program_id(2)
        part = _bdot_impl(_tile(a_ref), _tile(b_ref), mode)

        @pl.when(k == 0)
        def _():
            acc[...] = part

        @pl.when(k > 0)
        def _():
            acc[...] += part

        @pl.when(k == nk - 1)
        def _():
            res = acc[...]
            if add_ref is not None:
                res = res + add_ref[...]
            o_ref[...] = res.astype(o_ref.dtype).reshape(o_ref.shape)

    operands = [a, b] + ([add] if add is not None else [])
    in_specs = [a_spec, b_spec] + ([o_spec] if add is not None else [])
    aliases = {}
    if into is not None and len(into) > 2:
        operands, in_specs, aliases = operands + [into[2]], in_specs + [pl.BlockSpec(memory_space=pl.ANY)], {len(operands): 0}
    return pl.pallas_call(
        body, grid=(m // tm, n // tn, nk), in_specs=in_specs, out_specs=o_spec,
        out_shape=jax.ShapeDtypeStruct(out_shape, out_dtype),
        scratch_shapes=[pltpu.VMEM((tm, tn), F32)], input_output_aliases=aliases,
        name=name, compiler_params=_cparams(3),
    )(*operands)


def _rows(x, width=None, off=0, tm=256):
    width = x.shape[1] if width is None else width
    return (x, (tm, width), lambda i, off=off: (i, off))


def _whole(x):
    nd = x.ndim
    return (x, x.shape, lambda *pids, nd=nd: (0,) * nd)


RMS_ROWS = 512


def _rms_ops(x, gain):
    return [_rows(x, tm=RMS_ROWS), _whole(gain)]


def rms_fwd(name, x, gain):
    s, dm = x.shape
    return tile_fwd(name, _rms_fn, (s // RMS_ROWS,), _rms_ops(x, gain), [((s, dm), BF16, (RMS_ROWS, dm), lambda i: (i, 0), ())])[0]


def rms_bwd(name, x, gain, dh, dres):
    s = x.shape[0]
    return tile_bwd(name, _rms_fn, (s // RMS_ROWS,), _rms_ops(x, gain), [_rows(dh, tm=RMS_ROWS)], [(0, ()), (1, (0,))],
                    adds={0: _rows(dres, tm=RMS_ROWS)})


def loss_call(y, t):
    s, dm = y.shape
    dy, part = tile_fwd("loss", _loss_fn, (s // RMS_ROWS,), [_rows(y, tm=RMS_ROWS), _rows(t, tm=RMS_ROWS)],
                        [((s, dm), F32, (RMS_ROWS, dm), lambda i: (i, 0), ()), ((8, LANES), F32, (8, LANES), lambda i: (0, 0), (0,))])
    return dy, part[0, 0]


def _fox_prep_ops(pm, gq, gk):
    tm = 512
    return [(pm, (tm, LANES), lambda i, j: (i, C_FQ // LANES + j)), (pm, (tm, LANES), lambda i, j: (i, C_FK // LANES + j)),
            _whole(gq), _whole(gk)]


def fox_prep_fwd(name, pm, gq, gk):
    s = pm.shape[0]
    out = ((s, BRANCH), BF16, (512, LANES), lambda i, j: (i, j), ())
    return tile_fwd(name, _fox_prep_fn, (s // 512, 4), _fox_prep_ops(pm, gq, gk), [out, out])


def fox_prep_bwd(name, pm, gq, gk, dqn, dkn):
    s = pm.shape[0]
    cot = lambda g: (g, (512, LANES), lambda i, j: (i, j))
    own = ((s, BRANCH), (512, LANES), lambda i, j: (i, j))
    return tile_bwd(name, _fox_prep_fn, (s // 512, 4), _fox_prep_ops(pm, gq, gk), [cot(dqn), cot(dkn)],
                    [(0, (), own, BF16), (1, (), own, BF16), (2, (0, 1)), (3, (0, 1))])


def _fox_gate_ops(f_t, bias):
    return [(f_t, (1,) + f_t.shape[1:], lambda h: (h, 0, 0)), (bias, (1, 1, 1), lambda h: (h, 0, 0))]


def fox_gate_fwd(name, f_t, bias):
    n_h = f_t.shape[0]
    return tile_fwd(name, _fox_gate_fn, (n_h,), _fox_gate_ops(f_t, bias),
                    [(f_t.shape, F32, (1,) + f_t.shape[1:], lambda h: (h, 0, 0), ())])[0]


def fox_gate_bwd(name, f_t, bias, dcum):
    n_h = f_t.shape[0]
    return tile_bwd(name, _fox_gate_fn, (n_h,), _fox_gate_ops(f_t, bias),
                    [(dcum, (1,) + f_t.shape[1:], lambda h: (h, 0, 0))], [(0, ()), (1, ())])


FOX_GROUPS = 4


def _fox_groups(s):
    per = s // FOX_BLOCK // FOX_GROUPS
    return [(g * per, per, (g + 1) * per * FOX_BLOCK) for g in range(FOX_GROUPS)]


def _fox_attn_ops(qn, kn, pm, cum_c, cum_r, q0, keys):
    nb = FOX_BLOCK
    return [(qn, (nb, LANES), lambda p, i: (q0 + i, p)), (kn, (keys, LANES), lambda p, i: (0, p)),
            (pm, (keys, LANES), lambda p, i: (0, C_FV // LANES + p)),
            (cum_c, (1, nb, 1), lambda p, i: (2 * p, q0 + i, 0)), (cum_c, (1, nb, 1), lambda p, i: (2 * p + 1, q0 + i, 0)),
            (cum_r, (1, 1, keys), lambda p, i: (2 * p, 0, 0)), (cum_r, (1, 1, keys), lambda p, i: (2 * p + 1, 0, 0))]


def fox_attn_fwd(name, qn, kn, pm, cum_c, cum_r):
    s = qn.shape[0]
    parts = []
    for g, (q0, n_q, keys) in enumerate(_fox_groups(s)):
        parts.append(tile_fwd(f"{name}_g{g}", functools.partial(_fox_attn_fn, q0), (4, n_q), _fox_attn_ops(qn, kn, pm, cum_c, cum_r, q0, keys),
                              [((n_q * FOX_BLOCK, BRANCH), BF16, (FOX_BLOCK, LANES), lambda p, i: (i, p), ())], raw=(0, 1, 2))[0])
    return jnp.concatenate(parts, axis=0)


def fox_attn_bwd(name, qn, kn, pm, cum_c, cum_r, dy):
    s = qn.shape[0]
    groups = _fox_groups(s)
    d_qn, by_q, tails = [None] * len(groups), [None] * len(groups), [None] * len(groups)
    below = None
    for g in reversed(range(len(groups))):
        q0, n_q, keys = groups[g]
        rows = n_q * FOX_BLOCK
        own_q = ((rows, BRANCH), (FOX_BLOCK, LANES), lambda p, i: (i, p))
        own_k = ((keys, BRANCH), (keys, LANES), lambda p, i: (0, p))
        pair_c = ((4, rows, 1), (1, FOX_BLOCK, 1), lambda p, i: (p, i, 0))
        pair_r = ((4, 1, keys), (1, 1, keys), lambda p, i: (p, 0, 0))
        adds = {}
        if below is not None:
            adds = {1: (below[0],) + own_k[1:], 2: (below[1],) + own_k[1:], 5: (below[2],) + pair_r[1:], 6: (below[3],) + pair_r[1:]}
        g_qn, g_kn, g_v, g_cqa, g_cqb, g_cka, g_ckb = tile_bwd(
            f"{name}_g{g}", functools.partial(_fox_attn_fn, q0), (4, n_q), _fox_attn_ops(qn, kn, pm, cum_c, cum_r, q0, keys),
            [(dy, (FOX_BLOCK, LANES), lambda p, i, q0=q0: (q0 + i, p))],
            [(0, (), own_q), (1, (1,), own_k), (2, (1,), own_k), (3, (), pair_c), (4, (), pair_c), (5, (1,), pair_r), (6, (1,), pair_r)],
            adds=adds)
        below = (g_kn, g_v, g_cka, g_ckb)
        lo = groups[g - 1][2] if g else 0
        d_qn[g] = g_qn
        by_q[g] = jnp.stack([g_cqa[:, :, 0], g_cqb[:, :, 0]], axis=1).reshape(8, rows)
        tails[g] = (g_kn[lo:], g_v[lo:], jnp.stack([g_cka[:, 0, lo:], g_ckb[:, 0, lo:]], axis=1).reshape(8, keys - lo))
    d_cum = jnp.concatenate(by_q, axis=1) + jnp.concatenate([t[2] for t in tails], axis=1)
    return jnp.concatenate(d_qn, axis=0), jnp.concatenate([t[0] for t in tails], axis=0), jnp.concatenate([t[1] for t in tails], axis=0), d_cum


def sconv_ops(pm, w):
    s = pm.shape[0]
    blk = lambda c0: (pm, (s, LANES), lambda j, c0=c0: (0, c0 // LANES + j))
    return [blk(C_SB), blk(C_SC), blk(C_SV), (w, (w.shape[0], LANES), lambda j: (0, j))]


def dnconv_ops(pm, w):
    s = pm.shape[0]
    return [(pm, (s, FFN_TILE), lambda j: (0, C_DN // FFN_TILE + j)), (w, (w.shape[0], FFN_TILE), lambda j: (0, j))]


FFN_TILE = 256


def ffn_ops(ug, uv, w):
    s = ug.shape[0]
    n_t = D_FF // FFN_TILE
    return [(ug, (s, FFN_TILE), lambda j: (0, j)), (uv, (s, FFN_TILE), lambda j: (0, j)),
            (w, (w.shape[0], FFN_TILE), lambda j: (0, j)), (w, (w.shape[0], FFN_TILE), lambda j: (0, n_t + j))]


def _col_out(s, width, dtype=F32, tile=LANES):
    return ((s, width), dtype, (s, tile), lambda j: (0, j), ())


def _col_cot(g, tile=LANES):
    return (g, (g.shape[0], tile), lambda j: (0, j))


def merge_ops(yp, pm, tm=256):
    gate = lambda b: (pm, (tm, D_MODEL), lambda i, b=b: (i, C_GATE // D_MODEL + b))
    return [_rows(yp[0], tm=tm), _rows(yp[1], tm=tm), _rows(yp[2], tm=tm), gate(0), gate(1), gate(2)]


def ple_ops(gpre, pe, x):
    return [_rows(gpre, tm=RMS_ROWS), _rows(pe, tm=RMS_ROWS), _rows(x, tm=RMS_ROWS)]


def adam_call(name, w, g, m, v):
    shape = w.shape
    last = shape[-1]
    rows = w.size // last
    flat = lambda t: t.reshape(rows, last)
    tm = rows
    for cand in (512, 256, 128, 64, 32, 16, 8):
        if rows % cand == 0 and cand * last * 4 <= 2 * 1024 * 1024:
            tm = cand
            break
    spec = lambda t: (flat(t), (tm, last), lambda i: (i, 0))
    out = ((rows, last), F32, (tm, last), lambda i: (i, 0), ())
    res = tile_fwd(name, _adam_fn, (rows // tm,), [spec(w), spec(g), spec(m), spec(v)], [out, out, out])
    return [r.reshape(shape) for r in res]


def _adam_layers_fn(d, pids, w, m, v, g0, g1):
    g = jnp.where(pids[0] == 0, g0, g1)
    return (g,) + _adam_fn(d, pids, w, g, m, v)


def adam_layers(name, w, m, v, g0, g1):
    _, rows, cols = w.shape
    tm = _row_tile(rows, cols)
    n_t = rows // tm
    lay = lambda t: (t, (1, tm, cols), lambda l, i: (l, i, 0))
    ins = [lay(w), lay(m), lay(v), (g0, (tm, cols), lambda l, i: (i * (1 - l) + (n_t - 1) * l, 0)), (g1, (tm, cols), lambda l, i: (i * l, 0))]
    out = (w.shape, F32, (1, tm, cols), lambda l, i: (l, i, 0), ())
    return tile_fwd(name, _adam_layers_fn, (2, n_t), ins, [out, out, out, out])


def adam_w_in(name, w, m, v, g0, g1):
    rows, n_l, cols = w.shape

    def body(w_ref, m_ref, v_ref, g0_ref, g1_ref, g_out, d_out, m_out, v_out):
        step = 64

        def update(at):
            g0, g1 = g0_ref[at, :], g1_ref[at, :]
            layer = _iota((g0.shape[0], n_l, LANES), 1)
            g = jnp.where(layer == 0, g0[:, None, :], g1[:, None, :])
            delta, m2, v2 = _adam_fn(False, None, w_ref[at], g, m_ref[at], v_ref[at])
            for ref, val in ((g_out, g), (d_out, delta), (m_out, m2), (v_out, v2)):
                ref[at] = val

        def some_rows(i, carry):
            update(pl.ds(pl.multiple_of(i * step, step), step))
            return carry

        lax.fori_loop(0, rows // step, some_rows, 0)
        if rows % step:
            update(pl.ds(rows - rows % step, rows % step))

    both = pl.BlockSpec((rows, n_l, LANES), lambda j: (0, 0, j))
    one = pl.BlockSpec((rows, LANES), lambda j: (0, j))
    return pl.pallas_call(
        body, grid=(cols // LANES,), in_specs=[both, both, both, one, one], out_specs=[both] * 4,
        out_shape=[jax.ShapeDtypeStruct(w.shape, F32)] * 4, name=name, compiler_params=_cparams(1),
    )(w, m, v, g0, g1)


DN_GROUP = 4


def _dn_local_specs():
    rows = DN_GROUP * DN_CHUNK
    return [pl.BlockSpec((rows, 3 * BRANCH), lambda j: (j, 0)), pl.BlockSpec((rows, LANES), lambda j: (j, 0)),
            pl.BlockSpec((DN_GROUP, DN_HEADS, DN_CHUNK), lambda j: (j, 0, 0)), pl.BlockSpec((2, DN_HEADS), lambda j: (0, 0))]


def _dn_group_inputs(qkv, ps, a_rows, c):
    lo = c * DN_CHUNK
    heads = _split_heads(qkv[lo:lo + DN_CHUNK])
    return heads[0:4], heads[4:8], heads[8:12], ps[lo:lo + DN_CHUNK], a_rows[c]


def dn_local_fwd(name, dn_act, ps, a_rows, ad):
    s = dn_act.shape[0]
    n_c, n_g = s // DN_CHUNK, s // (DN_GROUP * DN_CHUNK)
    rows = DN_GROUP * DN_CHUNK

    def body(qkv_ref, ps_ref, ar_ref, ad_ref, u_ref, kc_ref, qd_ref, kd_ref, qk_ref, gl_ref):
        qkv, ps_v, a_rows_v, ad_v = qkv_ref[...], ps_ref[...], ar_ref[...], ad_ref[...]
        args = [[] for _ in range(8)]
        for c in range(DN_GROUP):
            q4, k4, v4, ps_c, ar_c = _dn_group_inputs(qkv, ps_v, a_rows_v, c)
            for lst, vals in zip(args, (q4, k4, v4) + _dn_gates(ps_c, ar_c, ad_v)):
                lst.extend(vals)
        everything = _dn_local(False, *args)
        for c in range(DN_GROUP):
            res = everything[c * DN_HEADS:(c + 1) * DN_HEADS]
            at = pl.ds(c * DN_CHUNK, DN_CHUNK)
            for ref, i in ((u_ref, 0), (kc_ref, 1), (qd_ref, 2), (kd_ref, 3)):
                ref[at, :] = jnp.concatenate([r[i] for r in res], axis=1)
            for h in range(DN_HEADS):
                qk_ref[c, h] = res[h][4]
            gl_ref[c] = _head_rows([r[5] for r in res])

    wide = pl.BlockSpec((rows, BRANCH), lambda j: (j, 0))
    return pl.pallas_call(
        body, grid=(n_g,), in_specs=_dn_local_specs(),
        out_specs=[wide, wide, wide, wide, pl.BlockSpec((DN_GROUP, DN_HEADS, DN_CHUNK, DN_CHUNK), lambda j: (j, 0, 0, 0)),
                   pl.BlockSpec((DN_GROUP, 8, LANES), lambda j: (j, 0, 0))],
        out_shape=[jax.ShapeDtypeStruct((s, BRANCH), F32)] * 4 + [jax.ShapeDtypeStruct((n_c, DN_HEADS, DN_CHUNK, DN_CHUNK), F32),
                                                                 jax.ShapeDtypeStruct((n_c, 8, LANES), F32)],
        name=name, compiler_params=_cparams(1),
    )(dn_act, ps, a_rows, ad)


def dn_local_bwd(name, dn_act, ps, a_rows, ad, cots):
    s = dn_act.shape[0]
    n_c, n_g = s // DN_CHUNK, s // (DN_GROUP * DN_CHUNK)
    rows = DN_GROUP * DN_CHUNK

    def body(qkv_ref, ps_ref, ar_ref, ad_ref, du_ref, dkc_ref, dqd_ref, dkd_ref, dqk_ref, dgl_ref, dqkv_ref, dps_ref, dar_ref, dad_ref):
        first = pl.program_id(0) == 0
        qkv, ps_v, a_rows_v, ad_v = qkv_ref[...], ps_ref[...], ar_ref[...], ad_ref[...]
        d_wide = [r[...] for r in (du_ref, dkc_ref, dqd_ref, dkd_ref)]
        qs, ks, vs, ps_cs, ar_cs, cot = [], [], [], [], [], []
        for c in range(DN_GROUP):
            q4, k4, v4, ps_c, ar_c = _dn_group_inputs(qkv, ps_v, a_rows_v, c)
            qs, ks, vs, ps_cs, ar_cs = qs + q4, ks + k4, vs + v4, ps_cs + [ps_c], ar_cs + [ar_c]
            lo = c * DN_CHUNK
            d_tiles = [_split_heads(t[lo:lo + DN_CHUNK]) for t in d_wide]
            d_gl = dgl_ref[c]
            cot += [(d_tiles[0][h], d_tiles[1][h], d_tiles[2][h], d_tiles[3][h], dqk_ref[c, h], _col(_row(d_gl, h), 0))
                    for h in range(DN_HEADS)]

        def f(qs, ks, vs, ps_cs, ar_cs, ad_v):
            gates = [[] for _ in range(5)]
            for ps_c, ar_c in zip(ps_cs, ar_cs):
                for lst, vals in zip(gates, _dn_gates(ps_c, ar_c, ad_v)):
                    lst.extend(vals)
            return _dn_local(True, qs, ks, vs, *gates)

        _, vjp = jax.vjp(f, qs, ks, vs, ps_cs, ar_cs, ad_v)
        d_q, d_k, d_v, d_ps, d_ar, d_ad = vjp(cot)
        for c in range(DN_GROUP):
            at, hs = pl.ds(c * DN_CHUNK, DN_CHUNK), slice(c * DN_HEADS, (c + 1) * DN_HEADS)
            dqkv_ref[at, :] = jnp.concatenate(d_q[hs] + d_k[hs] + d_v[hs], axis=1).astype(dqkv_ref.dtype)
            dps_ref[at, :] = d_ps[c]
            dar_ref[c] = d_ar[c]
        _store(dad_ref, d_ad, first)

    wide = pl.BlockSpec((rows, BRANCH), lambda j: (j, 0))
    specs = _dn_local_specs()
    return pl.pallas_call(
        body, grid=(n_g,),
        in_specs=specs + [wide, wide, wide, wide, pl.BlockSpec((DN_GROUP, DN_HEADS, DN_CHUNK, DN_CHUNK), lambda j: (j, 0, 0, 0)),
                          pl.BlockSpec((DN_GROUP, 8, LANES), lambda j: (j, 0, 0))],
        out_specs=specs,
        out_shape=[jax.ShapeDtypeStruct((s, 3 * BRANCH), F32), jax.ShapeDtypeStruct((s, LANES), F32),
                   jax.ShapeDtypeStruct((n_c, DN_HEADS, DN_CHUNK), F32), jax.ShapeDtypeStruct((2, DN_HEADS), F32)],
        name=name, compiler_params=_cparams(1),
    )(dn_act, ps, a_rows, ad, *cots)


def _dn_scan_specs(n_c, rev):
    idx = (lambda j: n_c - 1 - j) if rev else (lambda j: j)
    wide = pl.BlockSpec((DN_CHUNK, BRANCH), lambda j: (idx(j), 0))
    return [wide, wide, wide, wide, pl.BlockSpec((1, DN_HEADS, DN_CHUNK, DN_CHUNK), lambda j: (idx(j), 0, 0, 0)),
            pl.BlockSpec((1, 8, LANES), lambda j: (idx(j), 0, 0)), pl.BlockSpec((DN_CHUNK, BRANCH), lambda j: (idx(j), C_DZ // BRANCH)),
            pl.BlockSpec((1, DN_DH), lambda j: (0, 0))]


def _dn_scan_tiles(refs):
    u_ref, kc_ref, qd_ref, kd_ref, qk_ref, gl_ref, z_ref, g_ref = refs
    wide = [_split_heads(r[...]) for r in (u_ref, kc_ref, qd_ref, kd_ref)]
    gl = gl_ref[0]
    return [(wide[0][h], wide[1][h], wide[2][h], wide[3][h], qk_ref[0, h], _col(_row(gl, h), 0)) for h in range(DN_HEADS)], \
        _split_heads(z_ref[...].astype(F32)), g_ref[...]


def dn_scan_fwd(name, local, pm, gain):
    s = pm.shape[0]
    n_c = s // DN_CHUNK

    def body(*refs):
        y_ref, hist_ref, state = refs[8:]

        @pl.when(pl.program_id(0) == 0)
        def _():
            state[...] = jnp.zeros_like(state)

        hist_ref[0] = state[...]
        per_head, z4, gain_v = _dn_scan_tiles(refs[:8])
        ys, s_nexts = _dn_step(False, [state[h] for h in range(DN_HEADS)], per_head, z4, gain_v)
        for h in range(DN_HEADS):
            state[h] = s_nexts[h]
        y_ref[...] = jnp.concatenate(ys, axis=1).astype(y_ref.dtype)

    return pl.pallas_call(
        body, grid=(n_c,), in_specs=_dn_scan_specs(n_c, False),
        out_specs=[pl.BlockSpec((DN_CHUNK, BRANCH), lambda j: (j, 0)),
                   pl.BlockSpec((1, DN_HEADS, DN_DH, DN_DH), lambda j: (j, 0, 0, 0))],
        out_shape=[jax.ShapeDtypeStruct((s, BRANCH), BF16), jax.ShapeDtypeStruct((n_c, DN_HEADS, DN_DH, DN_DH), F32)],
        scratch_shapes=[pltpu.VMEM((DN_HEADS, DN_DH, DN_DH), F32)],
        name=name, compiler_params=_cparams(1),
    )(*local, pm, gain)


def dn_scan_bwd(name, local, pm, gain, hist, dy):
    s = pm.shape[0]
    n_c = s // DN_CHUNK

    def body(*refs):
        hist_ref, dy_ref = refs[8:10]
        du_ref, dkc_ref, dqd_ref, dkd_ref, dqk_ref, dgl_ref, dz_ref, dg_ref, d_state = refs[10:]
        first = pl.program_id(0) == 0

        @pl.when(first)
        def _():
            d_state[...] = jnp.zeros_like(d_state)

        per_head, z4, gain_v = _dn_scan_tiles(refs[:8])
        _, vjp = jax.vjp(functools.partial(_dn_step, True), [hist_ref[0, h] for h in range(DN_HEADS)], per_head, z4, gain_v)
        d_s, grads, d_z, d_gain = vjp((_split_heads(dy_ref[...].astype(F32)), [d_state[h] for h in range(DN_HEADS)]))
        for h in range(DN_HEADS):
            d_state[h] = d_s[h]
        for ref, i in ((du_ref, 0), (dkc_ref, 1), (dqd_ref, 2), (dkd_ref, 3)):
            ref[...] = jnp.concatenate([g[i] for g in grads], axis=1)
        dz_ref[...] = jnp.concatenate(d_z, axis=1).astype(dz_ref.dtype)
        for h in range(DN_HEADS):
            dqk_ref[0, h] = grads[h][4]
        dgl_ref[0] = _head_rows([g[5] for g in grads])
        _store(dg_ref, d_gain, first)

    rev = lambda j: n_c - 1 - j
    specs = _dn_scan_specs(n_c, True)
    return pl.pallas_call(
        body, grid=(n_c,),
        in_specs=specs + [pl.BlockSpec((1, DN_HEADS, DN_DH, DN_DH), lambda j: (rev(j), 0, 0, 0)),
                          pl.BlockSpec((DN_CHUNK, BRANCH), lambda j: (rev(j), 0))],
        out_specs=specs[:6] + [pl.BlockSpec((DN_CHUNK, BRANCH), lambda j: (rev(j), 0)), specs[7]],
        out_shape=[jax.ShapeDtypeStruct((s, BRANCH), F32)] * 4 + [
            jax.ShapeDtypeStruct((n_c, DN_HEADS, DN_CHUNK, DN_CHUNK), F32), jax.ShapeDtypeStruct((n_c, 8, LANES), F32),
            jax.ShapeDtypeStruct((s, BRANCH), BF16), jax.ShapeDtypeStruct((1, DN_DH), F32)],
        scratch_shapes=[pltpu.VMEM((DN_HEADS, DN_DH, DN_DH), F32)],
        name=name, compiler_params=_cparams(1),
    )(*local, pm, gain, hist, dy)


def _seq_layouts(cols, s):
    return cols.T.reshape(cols.shape[1], s // LANES, LANES)


def layer_fwd(li, x, p, w, more_weights=None):
    s = x.shape[0]
    n = lambda t: f"{t}_l{li}"
    h = rms_fwd(n("rms_mix"), x, w["g_mix"])
    pm = mm(n("in_main"), h, w["in_main"], "nn")
    ps = mm(n("in_small"), h, w["in_small"], "nn")
    qn, kn = fox_prep_fwd(n("fox_prep"), pm, w["gq"], w["gk"])
    f_t = _seq_layouts(ps[:, 0:8], s)
    cum = fox_gate_fwd(n("fox_gate"), f_t, w["b_f"])
    cum_c, cum_r = cum.reshape(8, s, 1), cum.reshape(8, 1, s)
    y_fox = fox_attn_fwd(n("fox_attn"), qn, kn, pm, cum_c, cum_r)
    y_sc = tile_fwd(n("sconv"), _sconv_fn, (BRANCH // LANES,), sconv_ops(pm, w["sc_conv_w"]), [_col_out(s, BRANCH, BF16)])[0]
    dn_act = tile_fwd(n("dnconv"), _dnconv_fn, (3 * BRANCH // FFN_TILE,), dnconv_ops(pm, w["dn_conv_w"]), [_col_out(s, 3 * BRANCH, F32, FFN_TILE)])[0]
    a_rows = ps[:, 12:16].reshape(s // DN_CHUNK, DN_CHUNK, DN_HEADS).transpose(0, 2, 1)
    dn_local = dn_local_fwd(n("dn_local"), dn_act, ps, a_rows, w["ad"])
    y_dn, hist = dn_scan_fwd(n("dn_scan"), dn_local, pm, w["dn_gain"])
    ys = (y_fox, y_sc, y_dn)
    if more_weights is not None:
        w = {**w, **more_weights(y_dn)}
    yp = [mm(n(f"branch{b}"), ys[b], w["branch"][b], "nn", blocks=(0, N_CHIPS)) for b in range(3)]
    merged = tile_fwd(n("merge"), _merge_fn, (s // RMS_ROWS,), merge_ops(yp, pm, RMS_ROWS),
                      [((s, D_MODEL), BF16, (RMS_ROWS, D_MODEL), lambda i: (i, 0), ())])[0]
    x1 = mm(n("w_o"), merged, w["o"], "nn", add=x)
    h2 = rms_fwd(n("rms_ffn"), x1, w["g_ffn"])
    ug = mm(n("up_g"), h2, w["up"], "nn", blocks=(0, 2))
    uv = mm(n("up_v"), h2, w["up"], "nn", blocks=(2, 2))
    act = tile_fwd(n("ffn_act"), _ffn_act_fn, (D_FF // FFN_TILE,), ffn_ops(ug, uv, w["ffn_conv_w"]), [_col_out(s, D_FF, BF16, FFN_TILE)])[0]
    x2 = mm(n("down"), act, w["down"], "nn", add=x1)
    h3 = rms_fwd(n("rms_ple"), x2, w["g_ple"])
    gpre = mm(n("ple_gate"), h3, w["pg"], "nn")
    pe = mm(n("ple_emb"), p, w["ple"], "nn", blocks=(0, N_CHIPS))
    x3 = tile_fwd(n("ple"), _ple_fn, (s // RMS_ROWS,), ple_ops(gpre, pe, x2), [((s, D_MODEL), F32, (RMS_ROWS, D_MODEL), lambda i: (i, 0), ())])[0]
    saved = dict(x=x, h=h, pm=pm, ps=ps, qn=qn, kn=kn, f_t=f_t, cum_c=cum_c, cum_r=cum_r, ys=ys, dn_act=dn_act, dn_local=dn_local,
                 a_rows=a_rows, hist=hist, yp=yp, merged=merged, x1=x1, h2=h2, ug=ug, uv=uv, act=act, x2=x2, h3=h3,
                 gpre=gpre, pe=pe, p=p)
    return x3, saved, w


def hang_on(w, token):
    zero = token[0, 0]
    small = ("g_mix", "g_ffn", "g_ple", "gq", "gk", "b_f", "ad", "dn_gain", "sc_conv_w", "dn_conv_w", "ffn_conv_w")
    return {**w, **{k: w[k] + zero for k in small}}


def layer_bwd(li, dx3, sv, w, hooks=None):
    hooks = hooks or {}

    def stage(key, after, w):
        return hang_on(w, hooks[key](after, g)) if key in hooks else w

    s = dx3.shape[0]
    n = lambda t: f"{t}_l{li}"
    g = {}
    col_own = lambda width: ((s, width), (s, LANES), lambda j: (0, j))
    d_gpre, d_pe = tile_bwd(n("ple_bwd"), _ple_fn, (s // RMS_ROWS,), ple_ops(sv["gpre"], sv["pe"], sv["x2"]), [_rows(dx3, tm=RMS_ROWS)],
                            [(0, (), None, BF16), (1, (), None, BF16)])
    g["w_ple"] = mm(n("d_w_ple"), sv["p"], d_pe, "tn", blocks=(0, N_CHIPS))
    g["w_ple_gate"] = mm(n("d_w_pg"), sv["h3"], d_gpre, "tn").reshape(N_CHIPS, -1, D_MODEL)
    dh3 = mm(n("d_h3"), d_gpre, w["pg"], "nt")
    dx2, d_g_ple = rms_bwd(n("rms_ple_bwd"), sv["x2"], w["g_ple"], dh3, dx3)
    dact = mm(n("d_act"), dx2, w["down"], "nt")
    g["w_down"] = mm(n("d_w_down"), sv["act"], dx2, "tn").reshape(N_CHIPS, -1, D_MODEL)
    taps_own = ((w["ffn_conv_w"].shape[0], D_FF), (w["ffn_conv_w"].shape[0], FFN_TILE), lambda j: (0, j))
    d_ug, d_uv, d_fw_g, d_fw_v = tile_bwd(n("ffn_act_bwd"), _ffn_act_fn, (D_FF // FFN_TILE,), ffn_ops(sv["ug"], sv["uv"], w["ffn_conv_w"]),
                                          [_col_cot(dact, FFN_TILE)], [(0, (), None, BF16), (1, (), None, BF16), (2, (), taps_own), (3, (), taps_own)])
    g["ffn_conv_w"] = jnp.concatenate([d_fw_g, d_fw_v], axis=1)
    gate_half = mm(n("d_w_up_g"), sv["h2"], d_ug, "tn", blocks=(0, 2), into=(N_CHIPS, 0))
    g["w_up"] = mm(n("d_w_up_v"), sv["h2"], d_uv, "tn", blocks=(0, 2), into=(N_CHIPS, 2, gate_half))
    dh2 = mm(n("d_h2_v"), d_uv, w["up"], "nt", blocks=(2, 2), add=mm(n("d_h2_g"), d_ug, w["up"], "nt", blocks=(0, 2)))
    dx1, d_g_ffn = rms_bwd(n("rms_ffn_bwd"), sv["x1"], w["g_ffn"], dh2, dx2)
    w = stage("mid", dx1, w)
    dmerged = mm(n("d_merged"), dx1, w["o"], "nt")
    g["w_o"] = mm(n("d_w_o"), sv["merged"], dx1, "tn").reshape(N_CHIPS, -1, D_MODEL)
    gate_own = ((s, D_MODEL), (256, D_MODEL), lambda i: (i, 0))
    d_yp0, d_yp1, d_yp2, d_g0, d_g1, d_g2 = tile_bwd(
        n("merge_bwd"), _merge_fn, (s // 256,), merge_ops(sv["yp"], sv["pm"]), [_rows(dmerged)],
        [(0, (), None, BF16), (1, (), None, BF16), (2, (), None, BF16), (3, (), gate_own, BF16), (4, (), gate_own, BF16), (5, (), gate_own, BF16)])
    d_yp = (d_yp0, d_yp1, d_yp2)
    g["w_branch"] = jnp.concatenate([mm(n(f"d_w_branch{b}"), sv["ys"][b], d_yp[b], "tn", blocks=(0, N_CHIPS)) for b in range(3)], axis=1)
    d_ys = [mm(n(f"d_y{b}"), d_yp[b], w["branch"][b], "nt", blocks=(0, N_CHIPS)) for b in range(3)]
    w = stage("late", d_ys[2], w)
    *d_local, d_z, d_dngain = dn_scan_bwd(n("dn_scan_bwd"), sv["dn_local"], sv["pm"], w["dn_gain"], sv["hist"], d_ys[2])
    d_dnact, d_ps_dn, d_arows, d_ad = dn_local_bwd(n("dn_local_bwd"), sv["dn_act"], sv["ps"], sv["a_rows"], w["ad"], d_local)
    g["ad"], g["dn_norm_gain"] = d_ad, d_dngain[0]
    wide_own = ((s, 3 * BRANCH), (s, FFN_TILE), lambda j: (0, j))
    d_dnqkv, g["dn_conv_w"] = tile_bwd(n("dnconv_bwd"), _dnconv_fn, (3 * BRANCH // FFN_TILE,), dnconv_ops(sv["pm"], w["dn_conv_w"]),
                                       [_col_cot(d_dnact, FFN_TILE)], [(0, (), wide_own, BF16), (1, ())])
    d_sb, d_sc, d_sv, g["sc_conv_w"] = tile_bwd(n("sconv_bwd"), _sconv_fn, (BRANCH // LANES,), sconv_ops(sv["pm"], w["sc_conv_w"]), [_col_cot(d_ys[1])],
                                                [(0, (), col_own(BRANCH), BF16), (1, (), col_own(BRANCH), BF16), (2, (), col_own(BRANCH), BF16), (3, ())])
    w = stage("last", d_dnqkv, w)
    d_qn, d_kn, d_fv, d_cum = fox_attn_bwd(n("fox_attn_bwd"), sv["qn"], sv["kn"], sv["pm"], sv["cum_c"], sv["cum_r"], d_ys[0])
    d_ft, d_bf = fox_gate_bwd(n("fox_gate_bwd"), sv["f_t"], w["b_f"], d_cum.reshape(8, s // LANES, LANES))
    g["b_fox_f"] = d_bf.reshape(8)
    d_fq, d_fk, d_gq, d_gk = fox_prep_bwd(n("fox_prep_bwd"), sv["pm"], w["gq"], w["gk"], d_qn, d_kn)
    g["fox_q_gain"] = d_gq[0, :FOX_DH] + d_gq[0, FOX_DH:]
    g["fox_k_gain"] = d_gk[0, :FOX_DH] + d_gk[0, FOX_DH:]
    d_pm = jnp.concatenate([d_fq, d_fk, d_fv.astype(BF16), d_sb, d_sc, d_sv, d_dnqkv, d_z, d_g0, d_g1, d_g2], axis=1)
    d_a_cols = d_arows.transpose(0, 2, 1).reshape(s, DN_HEADS)
    d_f_cols = d_ft.reshape(8, s).T
    d_ps = d_ps_dn + jnp.concatenate([d_f_cols, jnp.zeros((s, 4), F32), d_a_cols, jnp.zeros((s, LANES - 16), F32)], axis=1)
    g["w_in"] = chip_blocks_w_in(mm(n("d_w_in_main"), d_pm, sv["h"], "tn"), mm(n("d_w_in_small"), d_ps, sv["h"], "tn"))
    w = stage("w_in", g["w_in"], w)
    dh = mm(n("d_h_small"), d_ps, w["in_small"], "nt", add=mm(n("d_h_main"), d_pm, w["in_main"], "nt"))
    dx, d_g_mix = rms_bwd(n("rms_mix_bwd"), sv["x"], w["g_mix"], dh, dx1)
    g["g_mix"], g["g_ffn"], g["g_ple"] = d_g_mix[0], d_g_ffn[0], d_g_ple[0]
    return dx, g


IN_SHARD = 2052
MAIN_RANGES = ((0, 1536), (1544, 3080), (3080, 4616), (4624, 5136), (5136, 8208))
SMALL_RANGES = ((1536, 1544), (4616, 4620), (4620, 4624))


def _from_chip_blocks(blocks, ranges):
    parts = []
    for lo, hi in ranges:
        for k in range(N_CHIPS):
            a0, a1 = max(lo, k * IN_SHARD), min(hi, (k + 1) * IN_SHARD)
            if a0 < a1:
                parts.append(blocks[k][:, a0 - k * IN_SHARD:a1 - k * IN_SHARD])
    return parts


def split_w_in(blocks):
    main = jnp.concatenate(_from_chip_blocks(blocks, MAIN_RANGES), axis=1)
    pad = jnp.zeros((blocks.shape[1], LANES - 16), blocks.dtype)
    return main, jnp.concatenate(_from_chip_blocks(blocks, SMALL_RANGES) + [pad], axis=1)


def chip_blocks_w_in(main, small):
    ranges = sorted([(lo, hi, "m") for lo, hi in MAIN_RANGES] + [(lo, hi, "s") for lo, hi in SMALL_RANGES])
    offs, m_off, s_off = {}, 0, 0
    for lo, hi in MAIN_RANGES:
        offs[lo] = m_off
        m_off += hi - lo
    for lo, hi in SMALL_RANGES:
        offs[lo] = s_off
        s_off += hi - lo
    blocks = []
    for k in range(N_CHIPS):
        parts = []
        for lo, hi, src in ranges:
            a0, a1 = max(lo, k * IN_SHARD), min(hi, (k + 1) * IN_SHARD)
            if a0 < a1:
                arr = main if src == "m" else small
                parts.append(arr[offs[lo] + a0 - lo:offs[lo] + a1 - lo])
        blocks.append(jnp.concatenate(parts, axis=0))
    return jnp.stack(blocks)


def later_weights(got):
    g_branch, g_o, g_up, g_down, g_pg, g_ple = got
    branch = g_branch.reshape(N_CHIPS, 3, BRANCH, -1)
    return dict(branch=[branch[:, b] for b in range(3)], o=g_o.reshape(D_MODEL, D_MODEL), up=g_up,
                down=g_down.reshape(D_FF, D_MODEL), pg=g_pg.reshape(D_MODEL, D_MODEL), ple=g_ple)


def layer_weights(li, got, conv, a):
    main, small = split_w_in(got[0])
    tile2 = lambda v: jnp.concatenate([v, v])[None, :]
    rest = later_weights(got[1:]) if len(got) > 1 else {}
    return dict(
        in_main=main, in_small=small, **rest,
        g_mix=a["g_mix"][li][None, :], g_ffn=a["g_ffn"][li][None, :], g_ple=a["g_ple"][li][None, :],
        gq=tile2(a["fox_q_gain"][li]), gk=tile2(a["fox_k_gain"][li]), b_f=a["b_fox_f"][li].reshape(8, 1, 1),
        ad=jnp.stack([a["dn_a_log"][li], a["dn_dt_bias"][li]]), dn_gain=a["dn_norm_gain"][li][None, :],
        sc_conv_w=conv["sc_conv_w"][li], dn_conv_w=conv["dn_conv_w"][li], ffn_conv_w=conv["ffn_conv_w"][li])


def pack_rows(arrs, dtype):
    flat = jnp.concatenate([t.reshape(-1).astype(dtype) for t in arrs])
    pad = (-flat.shape[0]) % (8 * LANES)
    if pad:
        flat = jnp.concatenate([flat, jnp.zeros((pad,), dtype)])
    return flat.reshape(-1, LANES)


def unpack_rows(buf, shapes):
    flat = buf.reshape(-1)
    out, off = [], 0
    for shp in shapes:
        size = 1
        for dim in shp:
            size *= dim
        out.append(flat[off:off + size].reshape(shp))
        off += size
    return out


ANY = pl.BlockSpec(memory_space=pl.ANY)


def _position():
    x, y, c = lax.axis_index("x"), lax.axis_index("y"), lax.axis_index("c")
    return x, y, c, [(1 - x, y), (x, 1 - y), (1 - x, 1 - y)]


def gather_small(name, block):
    m_per, n = block.shape

    def body(x_ref, out_ref, token, send_sems, recv_sems, local_sem):
        token[...] = jnp.zeros_like(token)
        x, y, c, chips = _position()
        me, sibling = (x, y, c), (x, y, 1 - c)

        def rows(px, py, pc):
            return out_ref.at[pl.ds((4 * px + 2 * py + pc) * m_per, m_per), :]

        def copy(k, blk, to, src=None):
            return pltpu.make_async_remote_copy(src_ref=rows(*blk) if src is None else src, dst_ref=rows(*blk),
                                                send_sem=send_sems.at[k], recv_sem=recv_sems.at[k], device_id=to, device_id_type=MESH)

        mine = pltpu.make_async_copy(x_ref, rows(*me), local_sem)
        mine.start()
        first = [copy(0, me, sibling, src=x_ref)] + [copy(1 + j, me, (*chip, c), src=x_ref) for j, chip in enumerate(chips)]
        for cp in first:
            cp.start()
        passed = [copy(4 + j, (*chip, c), sibling) for j, chip in enumerate(chips)]
        for j, chip in enumerate(chips):
            copy(1 + j, (*chip, c), me).wait_recv()
            passed[j].start()
        copy(0, sibling, me).wait_recv()
        for j, chip in enumerate(chips):
            copy(4 + j, (*chip, 1 - c), me).wait_recv()
        for cp in first + passed:
            cp.wait_send()
        mine.wait()

    in_vmem = pl.BlockSpec(memory_space=pltpu.VMEM)
    return pl.pallas_call(
        body, out_shape=[jax.ShapeDtypeStruct((8 * m_per, n), block.dtype), jax.ShapeDtypeStruct((8, LANES), F32)],
        in_specs=[in_vmem], out_specs=[in_vmem, in_vmem],
        scratch_shapes=[pltpu.SemaphoreType.DMA((7,)), pltpu.SemaphoreType.DMA((7,)), pltpu.SemaphoreType.DMA],
        name=name, compiler_params=pltpu.CompilerParams(vmem_limit_bytes=VMEM_LIMIT),
    )(block)


def _sems(n):
    return [pltpu.SemaphoreType.DMA((n,)), pltpu.SemaphoreType.DMA((n,))]


def _split_cols(rows):
    return (rows // 2) % 16 != 0


def _half(ref, which, lead=()):
    rows, cols = ref.shape[-2:]
    if _split_cols(rows):
        return ref.at[(*lead, slice(None), pl.ds(which * (cols // 2), cols // 2))]
    return ref.at[(*lead, pl.ds(which * (rows // 2), rows // 2), slice(None))]


def _half_shape(rows, cols):
    return (rows, cols // 2) if _split_cols(rows) else (rows // 2, cols)


def forward_halves(name, lands):
    n_w = len(lands)

    def body(*refs):
        outs = refs[n_w:2 * n_w]
        send_sems, recv_sems = refs[2 * n_w:]
        x, y, c, chips = _position()

        def copy(w, j, pc):
            cx, cy = chips[j]
            part = _half(outs[w], pc, (2 * cx + cy,))
            return pltpu.make_async_remote_copy(src_ref=part, dst_ref=part, send_sem=send_sems.at[3 * w + j], recv_sem=recv_sems.at[3 * w + j],
                                                device_id=(x, y, 1 - c), device_id_type=MESH)

        pairs = [(w, j) for w in range(n_w) for j in range(3)]
        for w, j in pairs:
            copy(w, j, c).start()
        for w, j in pairs:
            copy(w, j, 1 - c).wait_recv()
            copy(w, j, c).wait_send()

    return pl.pallas_call(
        body, out_shape=[jax.ShapeDtypeStruct(t.shape, t.dtype) for t in lands], in_specs=[ANY] * n_w, out_specs=[ANY] * n_w,
        input_output_aliases={w: w for w in range(n_w)}, scratch_shapes=_sems(3 * n_w), name=name,
    )(*lands)


def share_halves(name, bufs):
    n_w = len(bufs)

    def body(*refs):
        outs = refs[n_w:2 * n_w]
        send_sems, recv_sems = refs[2 * n_w:]
        x, y, c, _ = _position()

        def copy(w, pc):
            half = _half(outs[w], pc)
            return pltpu.make_async_remote_copy(src_ref=half, dst_ref=half, send_sem=send_sems.at[w], recv_sem=recv_sems.at[w],
                                                device_id=(x, y, 1 - c), device_id_type=MESH)

        for w in range(n_w):
            copy(w, c).start()
        for w in range(n_w):
            copy(w, 1 - c).wait_recv()
            copy(w, c).wait_send()

    return pl.pallas_call(
        body, out_shape=[jax.ShapeDtypeStruct(b.shape, b.dtype) for b in bufs], in_specs=[ANY] * n_w, out_specs=[ANY] * n_w,
        input_output_aliases={w: w for w in range(n_w)}, scratch_shapes=_sems(n_w), name=name,
    )(*bufs)


HBM = pl.BlockSpec(memory_space=pltpu.HBM)
SEM = pl.BlockSpec(memory_space=pltpu.SEMAPHORE)
EFFECT = pltpu.SideEffectType.DATAFLOW_SIDE_EFFECTING


def _exchange_copies(kind, srcs, lands):
    x, y, c, chips = _position()
    out = []
    for src, land in zip(srcs, lands):
        if kind == "swap":
            out.append((_half(src, 1 - c, (slice(None),)), land, (x, y, 1 - c)))
            continue
        for j, (cx, cy) in enumerate(chips):
            if kind == "gather":
                out.append((src, land.at[2 * x + y], (cx, cy, c)))
            elif kind == "gather_half":
                out.append((_half(src, c), _half(land, c, (2 * x + y,)), (cx, cy, c)))
            else:
                out.append((src.at[2 * cx + cy], land.at[j], (cx, cy, c)))
    return out


def _land_shapes(kind, srcs):
    if kind in ("gather", "gather_half"):
        return [(N_CHIPS,) + s.shape for s in srcs]
    if kind == "swap":
        return [(N_CHIPS,) + _half_shape(*s.shape[1:]) for s in srcs]
    return [(3,) + s.shape[1:] for s in srcs]


def exchange_start(name, kind, srcs):
    n_w = len(srcs)
    shapes = _land_shapes(kind, srcs)
    n_sem = n_w if kind == "swap" else 3 * n_w

    def body(*refs):
        ins, lands = refs[:n_w], refs[n_w:2 * n_w]
        send_sems, recv_sems = refs[2 * n_w:2 * n_w + 2]
        token = refs[-1]
        for i, (src, dst, dev) in enumerate(_exchange_copies(kind, ins, lands)):
            pltpu.make_async_remote_copy(src_ref=src, dst_ref=dst, send_sem=send_sems.at[i], recv_sem=recv_sems.at[i],
                                         device_id=dev, device_id_type=MESH).start()
        token[...] = jnp.zeros_like(token)

    out = pl.pallas_call(
        body, name=name,
        out_shape=(pltpu.SemaphoreType.DMA((n_sem,)), pltpu.SemaphoreType.DMA((n_sem,)),
                   *[pltpu.HBM(s.shape, s.dtype) for s in srcs], *[pltpu.HBM(shp, s.dtype) for shp, s in zip(shapes, srcs)],
                   jax.ShapeDtypeStruct((8, LANES), F32)),
        in_specs=(HBM,) * (2 * n_w), out_specs=(SEM, SEM) + (HBM,) * (2 * n_w) + (pl.BlockSpec(memory_space=pltpu.VMEM),),
        input_output_aliases={i: 2 + i for i in range(2 * n_w)},
        compiler_params=pltpu.CompilerParams(has_side_effects=EFFECT),
    )(*[pltpu.with_memory_space_constraint(s, pltpu.HBM) for s in srcs],
      *[pltpu.with_memory_space_constraint(lax.empty(shp, s.dtype), pltpu.HBM) for shp, s in zip(shapes, srcs)])
    return (kind, n_w, out[:-1]), out[-1]


def exchange_wait(name, handle, after):
    kind, n_w, (send_sems, recv_sems, *thru) = handle

    def body(*refs):
        ins, lands = refs[:n_w], refs[n_w:2 * n_w]
        send_sems, recv_sems = refs[2 * n_w:2 * n_w + 2]
        for i, (src, dst, dev) in enumerate(_exchange_copies(kind, ins, lands)):
            cp = pltpu.make_async_remote_copy(src_ref=src, dst_ref=dst, send_sem=send_sems.at[i], recv_sem=recv_sems.at[i],
                                              device_id=dev, device_id_type=MESH)
            cp.wait_send()
            cp.wait_recv()

    out = pl.pallas_call(
        body, name=name, out_shape=tuple(pltpu.HBM(t.shape, t.dtype) for t in thru),
        in_specs=(HBM,) * (2 * n_w) + (SEM, SEM, pl.BlockSpec(memory_space=pl.ANY)), out_specs=(HBM,) * (2 * n_w),
        input_output_aliases={i: i for i in range(2 * n_w)},
        compiler_params=pltpu.CompilerParams(has_side_effects=EFFECT),
    )(*thru, send_sems, recv_sems, after)
    return list(out[:n_w]), list(out[n_w:])


def _row_tile(rows, cols):
    best = rows
    if rows * cols * 4 <= 2 * 1024 * 1024:
        return rows
    for t in range(16, rows, 16):
        if rows % t == 0 and t * cols * 4 <= 2 * 1024 * 1024:
            best = t
    return best


def pair_sum(name, pos, grad, from_sibling):
    _, rows, cols = grad.shape
    h_rows, h_cols = _half_shape(rows, cols)
    tr = _row_tile(h_rows, h_cols)
    n_t = h_rows // tr

    def body(pos_ref, g_ref, s_ref, b_ref, f_ref):
        tot = g_ref[...] + s_ref[...]
        b_ref[...] = tot.astype(BF16)

        @pl.when(pl.program_id(1) == pos_ref[1])
        def _():
            f_ref[...] = tot[0]

    blk = pl.BlockSpec((1, tr, h_cols), lambda i, k, pos: (k, i, 0))
    if _split_cols(rows):
        mine = pl.BlockSpec((1, tr, h_cols), lambda i, k, pos: (k, i, pos[0]))
    else:
        mine = pl.BlockSpec((1, tr, h_cols), lambda i, k, pos: (k, pos[0] * n_t + i, 0))
    return pl.pallas_call(
        body, grid_spec=pltpu.PrefetchScalarGridSpec(
            num_scalar_prefetch=1, grid=(n_t, N_CHIPS), in_specs=[mine, blk],
            out_specs=[blk, pl.BlockSpec((tr, h_cols), lambda i, k, pos: (i, 0))]),
        out_shape=[jax.ShapeDtypeStruct((N_CHIPS, h_rows, h_cols), BF16), jax.ShapeDtypeStruct((h_rows, h_cols), F32)],
        name=name, compiler_params=_cparams(2),
    )(pos, grad, from_sibling)


def chip_sum(name, pos, own, landed, split_cols):
    half, cols = own.shape
    tr = _row_tile(half, cols)
    n_t = half // tr

    def body(pos_ref, p_ref, l_ref, o_ref):
        o_ref[...] = ((p_ref[...] + l_ref[0].astype(F32)) + l_ref[1].astype(F32)) + l_ref[2].astype(F32)

    if split_cols:
        out_spec, out_shape = pl.BlockSpec((tr, cols), lambda i, pos: (i, pos[0])), (half, 2 * cols)
    else:
        out_spec, out_shape = pl.BlockSpec((tr, cols), lambda i, pos: (pos[0] * n_t + i, 0)), (2 * half, cols)
    return pl.pallas_call(
        body, grid_spec=pltpu.PrefetchScalarGridSpec(
            num_scalar_prefetch=1, grid=(n_t,),
            in_specs=[pl.BlockSpec((tr, cols), lambda i, pos: (i, 0)), pl.BlockSpec((3, tr, cols), lambda i, pos: (0, i, 0))],
            out_specs=out_spec),
        out_shape=jax.ShapeDtypeStruct(out_shape, F32), name=name, compiler_params=_cparams(1),
    )(pos, own, landed)


class OverlappedReduceScatter:
    def __init__(self, tag, pos, grads):
        self.n = lambda t: f"{t}_{tag}"
        self.pos, self.grads = pos, grads
        self.swap, self.token = exchange_start(self.n("swap_start"), "swap", grads)

    def middle(self, after):
        self.grads, from_sibling = exchange_wait(self.n("swap_wait"), self.swap, after)
        self.sums = [pair_sum(self.n(f"pair_sum{w}"), self.pos, g, s) for w, (g, s) in enumerate(zip(self.grads, from_sibling))]
        self.scatter, self.token = exchange_start(self.n("scatter_start"), "scatter", [b for b, _ in self.sums])

    def finish(self, after):
        _, landed = exchange_wait(self.n("scatter_wait"), self.scatter, after)
        halves = [chip_sum(self.n(f"chip_sum{w}"), self.pos, own, l, _split_cols(g.shape[1]))
                  for w, ((_, own), l, g) in enumerate(zip(self.sums, landed, self.grads))]
        return share_halves(self.n("share_halves"), halves)


def sum_devices(gathered):
    m_per = gathered.shape[0] // 8

    def body(g_ref, o_ref):
        tot = g_ref[pl.ds(0, m_per), :]
        for dev in range(1, 8):
            tot = tot + g_ref[pl.ds(dev * m_per, m_per), :]
        o_ref[...] = tot

    return pl.pallas_call(
        body, out_shape=jax.ShapeDtypeStruct((m_per, gathered.shape[1]), F32),
        in_specs=[pl.BlockSpec(memory_space=pltpu.VMEM)], out_specs=pl.BlockSpec(memory_space=pltpu.VMEM), name="sum_devices",
    )(gathered)


def kernel(x, p, g_mix, w_in, b_fox_f, fox_q_gain, fox_k_gain, sc_conv_w, dn_conv_w, dn_a_log, dn_dt_bias, dn_norm_gain, w_branch, w_o, g_ffn, w_up, ffn_conv_w, w_down, g_ple, w_ple_gate, w_ple, loss_target, m_g_mix, m_w_in, m_b_fox_f, m_fox_q_gain, m_fox_k_gain, m_sc_conv_w, m_dn_conv_w, m_dn_a_log, m_dn_dt_bias, m_dn_norm_gain, m_w_branch, m_w_o, m_g_ffn, m_w_up, m_ffn_conv_w, m_w_down, m_g_ple, m_w_ple_gate, m_w_ple, v_g_mix, v_w_in, v_b_fox_f, v_fox_q_gain, v_fox_k_gain, v_sc_conv_w, v_dn_conv_w, v_dn_a_log, v_dn_dt_bias, v_dn_norm_gain, v_w_branch, v_w_o, v_g_ffn, v_w_up, v_ffn_conv_w, v_w_down, v_g_ple, v_w_ple_gate, v_w_ple):
    a = dict(g_mix=g_mix, w_in=w_in, b_fox_f=b_fox_f, fox_q_gain=fox_q_gain, fox_k_gain=fox_k_gain, sc_conv_w=sc_conv_w,
             dn_conv_w=dn_conv_w, dn_a_log=dn_a_log, dn_dt_bias=dn_dt_bias, dn_norm_gain=dn_norm_gain, w_branch=w_branch, w_o=w_o,
             g_ffn=g_ffn, w_up=w_up, ffn_conv_w=ffn_conv_w, w_down=w_down, g_ple=g_ple, w_ple_gate=w_ple_gate, w_ple=w_ple)
    mom = dict(g_mix=m_g_mix, w_in=m_w_in, b_fox_f=m_b_fox_f, fox_q_gain=m_fox_q_gain, fox_k_gain=m_fox_k_gain, sc_conv_w=m_sc_conv_w,
               dn_conv_w=m_dn_conv_w, dn_a_log=m_dn_a_log, dn_dt_bias=m_dn_dt_bias, dn_norm_gain=m_dn_norm_gain, w_branch=m_w_branch,
               w_o=m_w_o, g_ffn=m_g_ffn, w_up=m_w_up, ffn_conv_w=m_ffn_conv_w, w_down=m_w_down, g_ple=m_g_ple, w_ple_gate=m_w_ple_gate,
               w_ple=m_w_ple)
    var = dict(g_mix=v_g_mix, w_in=v_w_in, b_fox_f=v_b_fox_f, fox_q_gain=v_fox_q_gain, fox_k_gain=v_fox_k_gain, sc_conv_w=v_sc_conv_w,
               dn_conv_w=v_dn_conv_w, dn_a_log=v_dn_a_log, dn_dt_bias=v_dn_dt_bias, dn_norm_gain=v_dn_norm_gain, w_branch=v_w_branch,
               w_o=v_w_o, g_ffn=v_g_ffn, w_up=v_w_up, ffn_conv_w=v_ffn_conv_w, w_down=v_w_down, g_ple=v_g_ple, w_ple_gate=v_w_ple_gate,
               w_ple=v_w_ple)
    cx, cy, cc = lax.axis_index("x"), lax.axis_index("y"), lax.axis_index("c")
    chip = 2 * cx + cy
    pos = jnp.stack([cc, chip]).astype(jnp.int32)

    def as_blocks(t):
        return t.reshape(2, -1, t.shape[-1])

    def own_block_in(got, shards):
        return [lax.dynamic_update_slice(g, s[None], (chip, 0, 0)) for g, s in zip(got, shards)]

    conv_shapes = [a[nm].shape for nm in CONVS]
    conv_all, conv_token = gather_small("gather_conv_w", pack_rows([a[nm] for nm in CONVS], F32))
    def w_in_block(li, token):
        stored = jnp.transpose(a["w_in"], (2, 0, 1))[:, li, :]
        return (stored + token[0, 0]).astype(BF16).T

    w_in0 = [w_in_block(0, conv_token)]
    gather_in0, gather_in0_token = exchange_start("gather_start_w_in_l0", "gather_half", w_in0)
    shards0 = w_in0 + [(as_blocks(a[nm])[0] + gather_in0_token[0, 0]).astype(BF16) for nm in BIG[1:]]
    gather0, gather0_token = exchange_start("gather_start_l0", "gather", shards0[1:])
    shards1 = [w_in_block(1, gather0_token)] + [(as_blocks(a[nm])[1] + gather0_token[0, 0]).astype(BF16) for nm in BIG[1:]]
    gather1, gather1_in_token = exchange_start("gather_start_w_in_l1", "gather", shards1[:1])
    shards1[1:] = [s + gather1_in_token[0, 0].astype(BF16) for s in shards1[1:]]
    gather1_rest, gather1_token = exchange_start("gather_start_l1", "gather", shards1[1:])
    conv_rows = conv_all.shape[0] // 8
    conv_chip = [unpack_rows(conv_all[2 * k * conv_rows:(2 * k + 1) * conv_rows], conv_shapes) for k in range(N_CHIPS)]
    conv = {nm: jnp.concatenate([conv_chip[k][i] for k in range(N_CHIPS)], axis=2) for i, nm in enumerate(CONVS)}

    weights, saved = [None, None], [None, None]
    mine_in0, got_in0 = exchange_wait("gather_wait_w_in_l0", gather_in0, gather1_token)
    got_in0 = forward_halves("forward_w_in_l0", got_in0)
    first_weights = hang_on(layer_weights(0, own_block_in(got_in0, mine_in0), conv, a), gather1_token)

    def rest_of_layer0(after):
        mine, got = exchange_wait("gather_wait_l0", gather0, after)
        return later_weights(own_block_in(got, mine))

    act, saved[0], weights[0] = layer_fwd(0, x[0], p[0, 0], first_weights, more_weights=rest_of_layer0)
    mine1, got1 = exchange_wait("gather_wait_w_in_l1", gather1, act)

    def rest_of_layer1(after):
        mine, got = exchange_wait("gather_wait_l1", gather1_rest, after)
        return later_weights(own_block_in(got, mine))

    act, saved[1], weights[1] = layer_fwd(1, act, p[1, 0], layer_weights(1, own_block_in(got1, mine1), conv, a),
                                          more_weights=rest_of_layer1)
    d_act, loss_part = loss_call(act, loss_target[0])
    loss = lax.psum(loss_part, ("x", "y", "c"))
    layer_grads = [None, None]
    d_act, layer_grads[1] = layer_bwd(1, d_act, saved[1], weights[1])
    rs1 = OverlappedReduceScatter("l1", pos, [layer_grads[1][nm] for nm in BIG])
    rs0 = []

    def stage_mid(after, g):
        rs1.middle(after)
        return rs1.token

    def stage_late(after, g):
        rs0.append(OverlappedReduceScatter("l0", pos, [g[nm] for nm in BIG[1:]]))
        return rs0[0].token

    def stage_last(after, g):
        rs0[0].middle(after)
        return rs0[0].token

    def stage_w_in(after, g):
        rs0.append(OverlappedReduceScatter("w_in_l0", pos, [g["w_in"]]))
        return rs0[1].token

    d_act, layer_grads[0] = layer_bwd(0, d_act, saved[0], hang_on(weights[0], rs1.token),
                                      hooks=dict(mid=stage_mid, late=stage_late, last=stage_last, w_in=stage_w_in))
    rs0[1].middle(d_act)
    reduced = [rs0[0].finish(rs0[1].token), rs1.finish(rs0[1].token)]
    grad_x = d_act[None]

    def both(nm):
        return jnp.stack([layer_grads[0][nm], layer_grads[1][nm]])

    local = {nm: both(nm) for nm in ("g_mix", "b_fox_f", "fox_q_gain", "fox_k_gain", "dn_norm_gain", "g_ffn", "g_ple", "sc_conv_w",
                                      "dn_conv_w", "ffn_conv_w")}
    local["dn_a_log"] = jnp.stack([layer_grads[li]["ad"][0] for li in range(2)])
    local["dn_dt_bias"] = jnp.stack([layer_grads[li]["ad"][1] for li in range(2)])

    small_names = SMALL + CONVS
    small_shapes = [local[nm].shape for nm in small_names]
    small_sum = sum_devices(gather_small("gather_small_grads", pack_rows([local[nm] for nm in small_names], F32))[0])
    small_grads = dict(zip(small_names, unpack_rows(small_sum, small_shapes)))
    for nm in CONVS:
        width = a[nm].shape[2]
        small_grads[nm] = lax.dynamic_slice_in_dim(small_grads[nm], chip * width, width, axis=2)

    grads, deltas, new_m, new_v = dict(small_grads), {}, {}, {}
    for nm in small_names:
        deltas[nm], new_m[nm], new_v[nm] = adam_call(f"adam_{nm}", a[nm], grads[nm], mom[nm], var[nm])
    for i, nm in enumerate(BIG[1:]):
        res = adam_layers(f"adam_{nm}", as_blocks(a[nm]), as_blocks(mom[nm]), as_blocks(var[nm]), reduced[0][i], reduced[1][1 + i])
        grads[nm], deltas[nm], new_m[nm], new_v[nm] = [r.reshape(a[nm].shape) for r in res]
    stored = lambda t: jnp.transpose(t, (2, 0, 1))
    res = adam_w_in("adam_w_in", stored(a["w_in"]), stored(mom["w_in"]), stored(var["w_in"]), rs0[1].finish(deltas["w_ple"])[0], reduced[1][0])
    grads["w_in"], deltas["w_in"], new_m["w_in"], new_v["w_in"] = [jnp.transpose(r, (1, 2, 0)) for r in res]
    return (loss, grad_x, *[grads[nm] for nm in WEIGHTS], *[deltas[nm] for nm in WEIGHTS], *[new_m[nm] for nm in WEIGHTS],
            *[new_v[nm] for nm in WEIGHTS])
```
